```python
import jax, jax.numpy as jnp
from jax import lax
import numpy as np

D_MODEL = 1024
BATCH = 8
SEQ = 4096
DEPTH = 2

N_A_LAYERS = DEPTH // 2
N_B_LAYERS = DEPTH - N_A_LAYERS
CHUNK = 128
A_EXPAND = 2
A_WIDTH = A_EXPAND * D_MODEL
A_GROUPS = 16
A_GROUP_DIM = A_WIDTH // A_GROUPS
HEAD_DIM = 64
N_Q_HEADS = D_MODEL // HEAD_DIM
N_KV_HEADS = max(1, N_Q_HEADS // 8)
Q_PER_KV = N_Q_HEADS // N_KV_HEADS
B_WIDTH = N_Q_HEADS * HEAD_DIM
KV_WIDTH = N_KV_HEADS * HEAD_DIM
WINDOW = 128
ROPE_THETA = 10000.0
EPS = 1e-5

kernel_name = "yoco_gmlp_swa_sink_hybrid"


def rms_norm(x, g):
    xf = x.astype(jnp.float32)
    y = xf * lax.rsqrt(jnp.mean(xf * xf, axis=-1, keepdims=True) + EPS) * g.astype(jnp.float32)
    return y.astype(x.dtype)


def layer_norm(x, g, b):
    xf = x.astype(jnp.float32)
    mu = jnp.mean(xf, axis=-1, keepdims=True)
    xc = xf - mu
    var = jnp.mean(xc * xc, axis=-1, keepdims=True)
    y = xc * lax.rsqrt(var + EPS) * g.astype(jnp.float32) + b.astype(jnp.float32)
    return y.astype(x.dtype)


def rotary(x, pos):
    dh = x.shape[-1]
    inv_freq = ROPE_THETA ** (-jnp.arange(0, dh, 2, dtype=jnp.float32) / dh)
    ang = pos[:, None] * inv_freq[None, :]
    cos = jnp.cos(ang)[None, :, None, :].astype(x.dtype)
    sin = jnp.sin(ang)[None, :, None, :].astype(x.dtype)
    x1, x2 = jnp.split(x, 2, axis=-1)
    return jnp.concatenate([x1 * cos - x2 * sin, x2 * cos + x1 * sin], axis=-1)


def band(t):
    b, s, h, d = t.shape
    blk = t.reshape(b, s // CHUNK, CHUNK, h, d)
    prev = jnp.concatenate([jnp.zeros_like(blk[:, :1]), blk[:, :-1]], axis=1)
    return jnp.concatenate([prev, blk], axis=2)


def gmlp_mixer(h, w_in, ln_g, ln_b, ws, bs, w_out):
    b, s, _ = h.shape
    nc = s // CHUNK
    z = h @ w_in
    u, v, g = jnp.split(z, 3, axis=-1)
    v = layer_norm(v, ln_g, ln_b)
    v = v.reshape(b, nc, CHUNK, A_GROUPS, A_GROUP_DIM)
    causal = jnp.tril(jnp.ones((CHUNK, CHUNK), dtype=bool))
    wsm = jnp.where(causal[None], ws, jnp.zeros_like(ws)).astype(v.dtype)
    sv = jnp.einsum('gts,bcsgd->bctgd', wsm, v) + bs.T[:, :, None].astype(v.dtype)
    sv = sv.reshape(b, s, A_WIDTH)
    y = u * sv * jax.nn.silu(g)
    return y @ w_out


def swa_mixer(h, k_band, v_band, pos, w_in, b_q, sinks, w_out):
    b, s, _ = h.shape
    nb = s // CHUNK
    z = h @ w_in
    q, g = jnp.split(z, 2, axis=-1)
    q = (q + b_q).reshape(b, s, N_Q_HEADS, HEAD_DIM)
    q = rotary(q, pos).reshape(b, nb, CHUNK, N_KV_HEADS, Q_PER_KV, HEAD_DIM)
    scores = jnp.einsum('bnqhrd,bnkhd->bnhrqk', q, k_band).astype(jnp.float32) * (HEAD_DIM ** -0.5)
    qi = jnp.arange(CHUNK)[:, None]
    kj = jnp.arange(2 * CHUNK)[None, :]
    rel = kj - CHUNK - qi
    in_window = (rel <= 0) & (rel > -WINDOW)
    key_valid = (jnp.arange(nb)[:, None] * CHUNK + jnp.arange(2 * CHUNK)[None, :] - CHUNK) >= 0
    mask = in_window[None] & key_valid[:, None, :]
    scores = jnp.where(mask[None, :, None, None], scores, -jnp.inf)
    sink = sinks.astype(jnp.float32).reshape(N_KV_HEADS, Q_PER_KV)[None, None, :, :, None, None]
    m = jnp.maximum(jnp.max(scores, axis=-1, keepdims=True), sink)
    p = jnp.exp(scores - m)
    denom = jnp.sum(p, axis=-1, keepdims=True) + jnp.exp(sink - m)
    p = (p / denom).astype(v_band.dtype)
    o = jnp.einsum('bnhrqk,bnkhd->bnqhrd', p, v_band).reshape(b, s, B_WIDTH)
    y = o * jax.nn.silu(g)
    return y @ w_out


def _fwd_setup_inputs(seed: int = 0) -> dict:
    key = jax.random.key(seed)
    ks = jax.random.split(key, 20)
    f32 = jnp.float32
    nrm = lambda k, shp, sc: jax.random.normal(k, shp, f32) * sc
    return {
        "x": nrm(ks[0], (BATCH, SEQ, D_MODEL), 1.0),
        "a_norm_g": 1.0 + nrm(ks[1], (N_A_LAYERS, D_MODEL), 0.02),
        "a_w_in": nrm(ks[2], (N_A_LAYERS, D_MODEL, 3 * A_WIDTH), D_MODEL ** -0.5),
        "a_ln_g": 1.0 + nrm(ks[3], (N_A_LAYERS, A_WIDTH), 0.02),
        "a_ln_b": nrm(ks[4], (N_A_LAYERS, A_WIDTH), 0.02),
        "a_ws": nrm(ks[5], (N_A_LAYERS, A_GROUPS, CHUNK, CHUNK), 0.5 * CHUNK ** -0.5),
        "a_bs": 1.0 + nrm(ks[6], (N_A_LAYERS, A_GROUPS, CHUNK), 0.02),
        "a_w_out": nrm(ks[7], (N_A_LAYERS, A_WIDTH, D_MODEL), 0.5 * A_WIDTH ** -0.5),
        "kv_norm_g": 1.0 + nrm(ks[8], (D_MODEL,), 0.02),
        "w_kv": nrm(ks[9], (D_MODEL, 2 * KV_WIDTH), D_MODEL ** -0.5),
        "b_kv": nrm(ks[10], (2 * KV_WIDTH,), 0.02),
        "b_norm_g": 1.0 + nrm(ks[11], (N_B_LAYERS, D_MODEL), 0.02),
        "b_w_in": nrm(ks[12], (N_B_LAYERS, D_MODEL, 2 * B_WIDTH), D_MODEL ** -0.5),
        "b_bq": nrm(ks[13], (N_B_LAYERS, B_WIDTH), 0.02),
        "b_sinks": nrm(ks[14], (N_B_LAYERS, N_Q_HEADS), 1.0),
        "b_w_out": nrm(ks[15], (N_B_LAYERS, B_WIDTH, D_MODEL), B_WIDTH ** -0.5),
        "final_norm_g": 1.0 + nrm(ks[16], (D_MODEL,), 0.02),
    }


def _fwd_reference(x, a_norm_g, a_w_in, a_ln_g, a_ln_b, a_ws, a_bs, a_w_out, kv_norm_g, w_kv, b_kv,
              b_norm_g, b_w_in, b_bq, b_sinks, b_w_out, final_norm_g):
    b, s, _ = x.shape
    pos = jnp.arange(s, dtype=jnp.float32)
    h = x
    k_band = None
    v_band = None
    for l in range(DEPTH):
        if l < N_A_LAYERS:
            i = l
            h = h + gmlp_mixer(rms_norm(h, a_norm_g[i]), a_w_in[i], a_ln_g[i], a_ln_b[i],
                               a_ws[i], a_bs[i], a_w_out[i])
        else:
            if l == N_A_LAYERS:
                kv = rms_norm(h, kv_norm_g) @ w_kv + b_kv
                k, v = jnp.split(kv, 2, axis=-1)
                k = rotary(k.reshape(b, s, N_KV_HEADS, HEAD_DIM), pos)
                v = v.reshape(b, s, N_KV_HEADS, HEAD_DIM)
                k_band = band(k)
                v_band = band(v)
            i = l - N_A_LAYERS
            h = h + swa_mixer(rms_norm(h, b_norm_g[i]), k_band, v_band, pos,
                              b_w_in[i], b_bq[i], b_sinks[i], b_w_out[i])
    return rms_norm(h, final_norm_g)


import jax as _jax
import jax.numpy as _jnp

TWIN_FORMAT = 'train_step'
FWD_PARAMS = ['x', 'a_norm_g', 'a_w_in', 'a_ln_g', 'a_ln_b', 'a_ws', 'a_bs', 'a_w_out', 'kv_norm_g', 'w_kv', 'b_kv', 'b_norm_g', 'b_w_in', 'b_bq', 'b_sinks', 'b_w_out', 'final_norm_g']
TWIN_WEIGHTS = ['a_norm_g', 'a_w_in', 'a_ln_g', 'a_ln_b', 'a_ws', 'a_bs', 'a_w_out', 'kv_norm_g', 'w_kv', 'b_kv', 'b_norm_g', 'b_w_in', 'b_bq', 'b_sinks', 'b_w_out', 'final_norm_g']
TWIN_DIFF_INPUT = 'x'
TWIN_INPUTS = ['x', 'a_norm_g', 'a_w_in', 'a_ln_g', 'a_ln_b', 'a_ws', 'a_bs', 'a_w_out', 'kv_norm_g', 'w_kv', 'b_kv', 'b_norm_g', 'b_w_in', 'b_bq', 'b_sinks', 'b_w_out', 'final_norm_g', 'loss_target', 'm_a_norm_g', 'm_a_w_in', 'm_a_ln_g', 'm_a_ln_b', 'm_a_ws', 'm_a_bs', 'm_a_w_out', 'm_kv_norm_g', 'm_w_kv', 'm_b_kv', 'm_b_norm_g', 'm_b_w_in', 'm_b_bq', 'm_b_sinks', 'm_b_w_out', 'm_final_norm_g', 'v_a_norm_g', 'v_a_w_in', 'v_a_ln_g', 'v_a_ln_b', 'v_a_ws', 'v_a_bs', 'v_a_w_out', 'v_kv_norm_g', 'v_w_kv', 'v_b_kv', 'v_b_norm_g', 'v_b_w_in', 'v_b_bq', 'v_b_sinks', 'v_b_w_out', 'v_final_norm_g']
TWIN_OUTPUTS = ['loss', 'grad_x', 'grad_a_norm_g', 'grad_a_w_in', 'grad_a_ln_g', 'grad_a_ln_b', 'grad_a_ws', 'grad_a_bs', 'grad_a_w_out', 'grad_kv_norm_g', 'grad_w_kv', 'grad_b_kv', 'grad_b_norm_g', 'grad_b_w_in', 'grad_b_bq', 'grad_b_sinks', 'grad_b_w_out', 'grad_final_norm_g', 'delta_a_norm_g', 'delta_a_w_in', 'delta_a_ln_g', 'delta_a_ln_b', 'delta_a_ws', 'delta_a_bs', 'delta_a_w_out', 'delta_kv_norm_g', 'delta_w_kv', 'delta_b_kv', 'delta_b_norm_g', 'delta_b_w_in', 'delta_b_bq', 'delta_b_sinks', 'delta_b_w_out', 'delta_final_norm_g', 'new_m_a_norm_g', 'new_m_a_w_in', 'new_m_a_ln_g', 'new_m_a_ln_b', 'new_m_a_ws', 'new_m_a_bs', 'new_m_a_w_out', 'new_m_kv_norm_g', 'new_m_w_kv', 'new_m_b_kv', 'new_m_b_norm_g', 'new_m_b_w_in', 'new_m_b_bq', 'new_m_b_sinks', 'new_m_b_w_out', 'new_m_final_norm_g', 'new_v_a_norm_g', 'new_v_a_w_in', 'new_v_a_ln_g', 'new_v_a_ln_b', 'new_v_a_ws', 'new_v_a_bs', 'new_v_a_w_out', 'new_v_kv_norm_g', 'new_v_w_kv', 'new_v_b_kv', 'new_v_b_norm_g', 'new_v_b_w_in', 'new_v_b_bq', 'new_v_b_sinks', 'new_v_b_w_out', 'new_v_final_norm_g']
TWIN_LEAF_KINDS = {'loss': 'loss', 'grad_x': 'grad_x', 'grad_a_norm_g': 'grad_w', 'grad_a_w_in': 'grad_w', 'grad_a_ln_g': 'grad_w', 'grad_a_ln_b': 'grad_w', 'grad_a_ws': 'grad_w', 'grad_a_bs': 'grad_w', 'grad_a_w_out': 'grad_w', 'grad_kv_norm_g': 'grad_w', 'grad_w_kv': 'grad_w', 'grad_b_kv': 'grad_w', 'grad_b_norm_g': 'grad_w', 'grad_b_w_in': 'grad_w', 'grad_b_bq': 'grad_w', 'grad_b_sinks': 'grad_w', 'grad_b_w_out': 'grad_w', 'grad_final_norm_g': 'grad_w', 'delta_a_norm_g': 'delta_w', 'delta_a_w_in': 'delta_w', 'delta_a_ln_g': 'delta_w', 'delta_a_ln_b': 'delta_w', 'delta_a_ws': 'delta_w', 'delta_a_bs': 'delta_w', 'delta_a_w_out': 'delta_w', 'delta_kv_norm_g': 'delta_w', 'delta_w_kv': 'delta_w', 'delta_b_kv': 'delta_w', 'delta_b_norm_g': 'delta_w', 'delta_b_w_in': 'delta_w', 'delta_b_bq': 'delta_w', 'delta_b_sinks': 'delta_w', 'delta_b_w_out': 'delta_w', 'delta_final_norm_g': 'delta_w', 'new_m_a_norm_g': 'new_m', 'new_m_a_w_in': 'new_m', 'new_m_a_ln_g': 'new_m', 'new_m_a_ln_b': 'new_m', 'new_m_a_ws': 'new_m', 'new_m_a_bs': 'new_m', 'new_m_a_w_out': 'new_m', 'new_m_kv_norm_g': 'new_m', 'new_m_w_kv': 'new_m', 'new_m_b_kv': 'new_m', 'new_m_b_norm_g': 'new_m', 'new_m_b_w_in': 'new_m', 'new_m_b_bq': 'new_m', 'new_m_b_sinks': 'new_m', 'new_m_b_w_out': 'new_m', 'new_m_final_norm_g': 'new_m', 'new_v_a_norm_g': 'new_v', 'new_v_a_w_in': 'new_v', 'new_v_a_ln_g': 'new_v', 'new_v_a_ln_b': 'new_v', 'new_v_a_ws': 'new_v', 'new_v_a_bs': 'new_v', 'new_v_a_w_out': 'new_v', 'new_v_kv_norm_g': 'new_v', 'new_v_w_kv': 'new_v', 'new_v_b_kv': 'new_v', 'new_v_b_norm_g': 'new_v', 'new_v_b_w_in': 'new_v', 'new_v_b_bq': 'new_v', 'new_v_b_sinks': 'new_v', 'new_v_b_w_out': 'new_v', 'new_v_final_norm_g': 'new_v'}


def _forward(args):
    return _fwd_reference(*[args[k] for k in FWD_PARAMS])


def _output_shape():
    def fwd():
        inp = _fwd_setup_inputs(0)
        return _fwd_reference(*[inp[k] for k in FWD_PARAMS])
    out = _jax.eval_shape(fwd)
    return out.shape, out.dtype

N_MICROBATCH = 1
ADAM_LR = 0.001
ADAM_B1 = 0.9
ADAM_B2 = 0.999
ADAM_EPS = 1e-08
ADAM_WD = 0.01
ADAM_STEP = 10
PER_EXAMPLE_BATCH_AXIS = {'x': 0, 'loss_target': 0}
SHARED_INPUTS = []
_WEIGHT_DTYPES = {'a_norm_g': _jnp.float32, 'a_w_in': _jnp.float32, 'a_ln_g': _jnp.float32, 'a_ln_b': _jnp.float32, 'a_ws': _jnp.float32, 'a_bs': _jnp.float32, 'a_w_out': _jnp.float32, 'kv_norm_g': _jnp.float32, 'w_kv': _jnp.float32, 'b_kv': _jnp.float32, 'b_norm_g': _jnp.float32, 'b_w_in': _jnp.float32, 'b_bq': _jnp.float32, 'b_sinks': _jnp.float32, 'b_w_out': _jnp.float32, 'final_norm_g': _jnp.float32}
MOMENT_SCALE = {'a_norm_g': 8.229656e-02, 'a_w_in': 3.265692e-02, 'a_ln_g': 1.270194e-02, 'a_ln_b': 1.230281e-02, 'a_ws': 2.542161e-02, 'a_bs': 3.617930e-02, 'a_w_out': 1.083165e-01, 'kv_norm_g': 2.851345e-02, 'w_kv': 5.409363e-02, 'b_kv': 1.971836e-01, 'b_norm_g': 2.830734e-02, 'b_w_in': 1.947107e-02, 'b_bq': 1.777066e-02, 'b_sinks': 1.651759e-02, 'b_w_out': 2.050347e-02, 'final_norm_g': 3.201081e+01}


def _to_microbatches(a, axis):
    t = _jnp.moveaxis(a, axis, 0)
    t = t.reshape((N_MICROBATCH, t.shape[0] // N_MICROBATCH) + t.shape[1:])
    return _jnp.moveaxis(t, 1, axis + 1)


def setup_inputs(seed: int = 0) -> dict:
    inp = _fwd_setup_inputs(seed)
    key = _jax.random.fold_in(_jax.random.key(seed), 7919)
    shape, _ = _output_shape()
    out = dict(inp)
    out["loss_target"] = _jax.random.normal(_jax.random.fold_in(key, 0), shape, _jnp.float32)
    for i, name in enumerate(TWIN_WEIGHTS):
        w = inp[name].astype(_jnp.float32)
        if MOMENT_SCALE is None:
            s = _jnp.sqrt(_jnp.mean(_jnp.square(w)) + 1e-30)
        else:
            s = MOMENT_SCALE[name]
        km, kv = _jax.random.split(_jax.random.fold_in(key, i + 1))
        out[name] = w
        out["m_" + name] = s * _jax.random.normal(km, w.shape, _jnp.float32)
        out["v_" + name] = (s * s) * _jax.random.uniform(kv, w.shape, _jnp.float32, 0.5, 1.5)
    if N_MICROBATCH > 1:
        for name, axis in PER_EXAMPLE_BATCH_AXIS.items():
            out[name] = _to_microbatches(out[name], axis)
    return {'x': out['x'], 'a_norm_g': out['a_norm_g'], 'a_w_in': out['a_w_in'], 'a_ln_g': out['a_ln_g'], 'a_ln_b': out['a_ln_b'], 'a_ws': out['a_ws'], 'a_bs': out['a_bs'], 'a_w_out': out['a_w_out'], 'kv_norm_g': out['kv_norm_g'], 'w_kv': out['w_kv'], 'b_kv': out['b_kv'], 'b_norm_g': out['b_norm_g'], 'b_w_in': out['b_w_in'], 'b_bq': out['b_bq'], 'b_sinks': out['b_sinks'], 'b_w_out': out['b_w_out'], 'final_norm_g': out['final_norm_g'], 'loss_target': out['loss_target'], 'm_a_norm_g': out['m_a_norm_g'], 'm_a_w_in': out['m_a_w_in'], 'm_a_ln_g': out['m_a_ln_g'], 'm_a_ln_b': out['m_a_ln_b'], 'm_a_ws': out['m_a_ws'], 'm_a_bs': out['m_a_bs'], 'm_a_w_out': out['m_a_w_out'], 'm_kv_norm_g': out['m_kv_norm_g'], 'm_w_kv': out['m_w_kv'], 'm_b_kv': out['m_b_kv'], 'm_b_norm_g': out['m_b_norm_g'], 'm_b_w_in': out['m_b_w_in'], 'm_b_bq': out['m_b_bq'], 'm_b_sinks': out['m_b_sinks'], 'm_b_w_out': out['m_b_w_out'], 'm_final_norm_g': out['m_final_norm_g'], 'v_a_norm_g': out['v_a_norm_g'], 'v_a_w_in': out['v_a_w_in'], 'v_a_ln_g': out['v_a_ln_g'], 'v_a_ln_b': out['v_a_ln_b'], 'v_a_ws': out['v_a_ws'], 'v_a_bs': out['v_a_bs'], 'v_a_w_out': out['v_a_w_out'], 'v_kv_norm_g': out['v_kv_norm_g'], 'v_w_kv': out['v_w_kv'], 'v_b_kv': out['v_b_kv'], 'v_b_norm_g': out['v_b_norm_g'], 'v_b_w_in': out['v_b_w_in'], 'v_b_bq': out['v_b_bq'], 'v_b_sinks': out['v_b_sinks'], 'v_b_w_out': out['v_b_w_out'], 'v_final_norm_g': out['v_final_norm_g']}


def _loss(weights, diff, rest, loss_target):
    with _jax.named_scope("forward"):
        args = {**rest, TWIN_DIFF_INPUT: diff, **{k: w.astype(_WEIGHT_DTYPES[k]) for k, w in weights.items()}}
        y = _forward(args)
    with _jax.named_scope("loss_head"):
        err = _jnp.square(y.astype(_jnp.float32) - loss_target)
        return 0.5 * _jnp.sum(_jnp.mean(err, axis=-1)) if err.ndim else 0.5 * err


def _adamw(w, g, m, v):
    m = ADAM_B1 * m + (1.0 - ADAM_B1) * g
    v = ADAM_B2 * v + (1.0 - ADAM_B2) * _jnp.square(g)
    m_hat = m / (1.0 - ADAM_B1 ** ADAM_STEP)
    v_hat = v / (1.0 - ADAM_B2 ** ADAM_STEP)
    delta = -ADAM_LR * (m_hat / (_jnp.sqrt(v_hat) + ADAM_EPS) + ADAM_WD * w)
    return delta, m, v


def reference(x, a_norm_g, a_w_in, a_ln_g, a_ln_b, a_ws, a_bs, a_w_out, kv_norm_g, w_kv, b_kv, b_norm_g, b_w_in, b_bq, b_sinks, b_w_out, final_norm_g, loss_target, m_a_norm_g, m_a_w_in, m_a_ln_g, m_a_ln_b, m_a_ws, m_a_bs, m_a_w_out, m_kv_norm_g, m_w_kv, m_b_kv, m_b_norm_g, m_b_w_in, m_b_bq, m_b_sinks, m_b_w_out, m_final_norm_g, v_a_norm_g, v_a_w_in, v_a_ln_g, v_a_ln_b, v_a_ws, v_a_bs, v_a_w_out, v_kv_norm_g, v_w_kv, v_b_kv, v_b_norm_g, v_b_w_in, v_b_bq, v_b_sinks, v_b_w_out, v_final_norm_g):
    given = dict(x=x, a_norm_g=a_norm_g, a_w_in=a_w_in, a_ln_g=a_ln_g, a_ln_b=a_ln_b, a_ws=a_ws, a_bs=a_bs, a_w_out=a_w_out, kv_norm_g=kv_norm_g, w_kv=w_kv, b_kv=b_kv, b_norm_g=b_norm_g, b_w_in=b_w_in, b_bq=b_bq, b_sinks=b_sinks, b_w_out=b_w_out, final_norm_g=final_norm_g, loss_target=loss_target, m_a_norm_g=m_a_norm_g, m_a_w_in=m_a_w_in, m_a_ln_g=m_a_ln_g, m_a_ln_b=m_a_ln_b, m_a_ws=m_a_ws, m_a_bs=m_a_bs, m_a_w_out=m_a_w_out, m_kv_norm_g=m_kv_norm_g, m_w_kv=m_w_kv, m_b_kv=m_b_kv, m_b_norm_g=m_b_norm_g, m_b_w_in=m_b_w_in, m_b_bq=m_b_bq, m_b_sinks=m_b_sinks, m_b_w_out=m_b_w_out, m_final_norm_g=m_final_norm_g, v_a_norm_g=v_a_norm_g, v_a_w_in=v_a_w_in, v_a_ln_g=v_a_ln_g, v_a_ln_b=v_a_ln_b, v_a_ws=v_a_ws, v_a_bs=v_a_bs, v_a_w_out=v_a_w_out, v_kv_norm_g=v_kv_norm_g, v_w_kv=v_w_kv, v_b_kv=v_b_kv, v_b_norm_g=v_b_norm_g, v_b_w_in=v_b_w_in, v_b_bq=v_b_bq, v_b_sinks=v_b_sinks, v_b_w_out=v_b_w_out, v_final_norm_g=v_final_norm_g)
    weights = {n: given[n] for n in TWIN_WEIGHTS}
    shared = {n: given[n] for n in SHARED_INPUTS}
    per_example = {n: given[n] for n in ['x']}
    grad_fn = _jax.value_and_grad(_loss, argnums=(0, 1))

    def one_microbatch(ex, loss_target):
        ex = dict(ex)
        diff = ex.pop(TWIN_DIFF_INPUT)
        return grad_fn(weights, diff, {**shared, **ex}, loss_target)

    if N_MICROBATCH == 1:
        loss, (grad_w, grad_x) = one_microbatch(per_example, given["loss_target"])
    else:
        def body(carry, xs):
            loss_sum, grad_sum = carry
            l_k, (gw_k, gx_k) = one_microbatch(xs[0], xs[1])
            with _jax.named_scope("update"):
                return (loss_sum + l_k, _jax.tree.map(_jnp.add, grad_sum, gw_k)), gx_k

        init = (_jnp.zeros((), _jnp.float32), _jax.tree.map(_jnp.zeros_like, weights))
        (loss, grad_w), grad_x = _jax.lax.scan(body, init, (per_example, given["loss_target"]))
    with _jax.named_scope("update"):
        delta_w, new_m, new_v = {}, {}, {}
        for n in TWIN_WEIGHTS:
            delta_w[n], new_m[n], new_v[n] = _adamw(weights[n], grad_w[n], given["m_" + n], given["v_" + n])
    return (loss, grad_x, *[grad_w[n] for n in TWIN_WEIGHTS], *[delta_w[n] for n in TWIN_WEIGHTS],
            *[new_m[n] for n in TWIN_WEIGHTS], *[new_v[n] for n in TWIN_WEIGHTS])
```

```python
import functools
import math

import jax
import jax.numpy as jnp
from jax import lax
from jax.experimental import pallas as pl
from jax.experimental.pallas import tpu as pltpu

F32 = jnp.float32
BF16 = jnp.bfloat16

D_MODEL = 1024
CHUNK = 128
A_WIDTH = 2048
A_GROUPS = 16
HEAD_DIM = 64
N_Q_HEADS = 16
N_KV_HEADS = 2
Q_PER_KV = 8
B_WIDTH = 1024
KV_WIDTH = 128
ROPE_THETA = 10000.0
EPS = 1e-5
N_CHIPS = 4

ADAM_LR = 0.001
ADAM_B1 = 0.9
ADAM_B2 = 0.999
ADAM_EPS = 1e-08
ADAM_WD = 0.01
ADAM_STEP = 10

V7X_VMEM_BYTES = 64 * 1024 * 1024
VMEM_LIMIT = 48 * 1024 * 1024
MESH = pl.DeviceIdType.MESH
NEG_BIG = -1e30


def _cparams(**kw):
    return pltpu.CompilerParams(vmem_limit_bytes=VMEM_LIMIT, **kw)


def _matmul(a, b, *, dims, grid, a_spec, b_spec, o_spec, out_shape, name, acc_axis=None,
            residual=None, r_spec=None):
    has_res = residual is not None

    def body(*refs):
        if has_res:
            a_ref, b_ref, r_ref, o_ref = refs
        else:
            a_ref, b_ref, o_ref = refs
        part = lax.dot_general(a_ref[...], b_ref[...], dims, preferred_element_type=F32)
        if acc_axis is None:
            if has_res:
                part = part + r_ref[...]
            o_ref[...] = part.astype(o_ref.dtype)
        else:
            k = pl.program_id(acc_axis)

            @pl.when(k == 0)
            def _():
                o_ref[...] = part

            @pl.when(k > 0)
            def _():
                o_ref[...] += part

    in_specs = [a_spec, b_spec] + ([r_spec] if has_res else [])
    args = (a, b) + ((residual,) if has_res else ())
    return pl.pallas_call(
        body, grid=grid, in_specs=in_specs, out_specs=o_spec, out_shape=out_shape, name=name,
        compiler_params=_cparams(),
    )(*args)


NN = (((1,), (0,)), ((), ()))
NT = (((1,), (1,)), ((), ()))
TN = (((0,), (0,)), ((), ()))


def _row_tile(s, want):
    return min(s, want)


def _mm_nn(a, b, *, name, tn, out_dtype=F32, residual=None, tm=512):
    s, k = a.shape
    tm = _row_tile(s, tm)
    if b.ndim == 3:
        nsh, _, nc = b.shape
        npb = nc // tn
        n = nsh * nc
        b_spec = pl.BlockSpec((None, k, tn), lambda i, j: (j // npb, 0, j % npb))
    else:
        n = b.shape[1]
        b_spec = pl.BlockSpec((k, tn), lambda i, j: (0, j))
    return _matmul(
        a, b, dims=NN, grid=(s // tm, n // tn),
        a_spec=pl.BlockSpec((tm, k), lambda i, j: (i, 0)), b_spec=b_spec,
        o_spec=pl.BlockSpec((tm, tn), lambda i, j: (i, j)),
        out_shape=jax.ShapeDtypeStruct((s, n), out_dtype), name=name,
        residual=residual, r_spec=pl.BlockSpec((tm, tn), lambda i, j: (i, j)) if residual is not None else None)


def _mm_nt(a, b, *, name, tk, tn=None, tm=512):
    s, k = a.shape
    tm = _row_tile(s, tm)
    if b.ndim == 3:
        nsh, n, kc = b.shape
        npb = kc // tk
        return _matmul(
            a, b, dims=NT, grid=(s // tm, k // tk), acc_axis=1,
            a_spec=pl.BlockSpec((tm, tk), lambda i, kk: (i, kk)),
            b_spec=pl.BlockSpec((None, n, tk), lambda i, kk: (kk // npb, 0, kk % npb)),
            o_spec=pl.BlockSpec((tm, n), lambda i, kk: (i, 0)),
            out_shape=jax.ShapeDtypeStruct((s, n), F32), name=name)
    n = b.shape[0]
    tn = n if tn is None else tn
    assert tk == k
    return _matmul(
        a, b, dims=NT, grid=(s // tm, n // tn),
        a_spec=pl.BlockSpec((tm, k), lambda i, j: (i, 0)),
        b_spec=pl.BlockSpec((tn, k), lambda i, j: (j, 0)),
        o_spec=pl.BlockSpec((tm, tn), lambda i, j: (i, j)),
        out_shape=jax.ShapeDtypeStruct((s, n), F32), name=name)


def _mm_tn(a, b, *, name, tm, tn, tk=512, shards=None):
    s, m = a.shape
    n = b.shape[1]
    tk = _row_tile(s, tk)
    if shards is None:
        o_spec = pl.BlockSpec((tm, tn), lambda i, j, kk: (i, j))
        out_shape = jax.ShapeDtypeStruct((m, n), F32)
    else:
        assert tm == m
        nc = n // shards
        npb = nc // tn
        o_spec = pl.BlockSpec((None, m, tn), lambda i, j, kk: (j // npb, 0, j % npb))
        out_shape = jax.ShapeDtypeStruct((shards, m, nc), F32)
    return _matmul(
        a, b, dims=TN, grid=(m // tm, n // tn, s // tk), acc_axis=2,
        a_spec=pl.BlockSpec((tk, tm), lambda i, j, kk: (kk, i)),
        b_spec=pl.BlockSpec((tk, tn), lambda i, j, kk: (kk, j)),
        o_spec=o_spec, out_shape=out_shape, name=name)


def _rstd(x):
    return lax.rsqrt(jnp.mean(x * x, axis=-1, keepdims=True) + EPS)


def _rms_fwd(x, gains, *, name, tr=256):
    s, d = x.shape
    tr = _row_tile(s, tr)
    ng = len(gains)

    def body(*refs):
        x_ref = refs[0]
        xv = x_ref[...]
        xh = xv * _rstd(xv)
        for t in range(ng):
            refs[1 + ng + t][...] = (xh * refs[1 + t][...]).astype(BF16)

    row = pl.BlockSpec((tr, d), lambda i: (i, 0))
    vec = pl.BlockSpec((1, d), lambda i: (0, 0))
    return pl.pallas_call(
        body, grid=(s // tr,), in_specs=[row] + [vec] * ng, out_specs=[row] * ng,
        out_shape=[jax.ShapeDtypeStruct((s, d), BF16)] * ng, name=name, compiler_params=_cparams(),
    )(x, *gains)


def _rms_bwd(x, dns, gains, dres, *, name, tr=256):
    s, d = x.shape
    tr = _row_tile(s, tr)
    ng = len(gains)

    def body(*refs):
        x_ref = refs[0]
        dn_refs = refs[1:1 + ng]
        g_refs = refs[1 + ng:1 + 2 * ng]
        dres_ref = refs[1 + 2 * ng]
        dx_ref, dxb_ref = refs[2 + 2 * ng], refs[3 + 2 * ng]
        dg_refs = refs[4 + 2 * ng:]
        i = pl.program_id(0)
        xv = x_ref[...]
        r = _rstd(xv)
        xh = xv * r
        acc = jnp.zeros_like(xv)
        for t in range(ng):
            dn = dn_refs[t][...]
            acc = acc + dn * g_refs[t][...]
            dgt = jnp.sum(dn * xh, axis=0, keepdims=True)

            @pl.when(i == 0)
            def _(t=t, dgt=dgt):
                dg_refs[t][...] = dgt

            @pl.when(i > 0)
            def _(t=t, dgt=dgt):
                dg_refs[t][...] += dgt

        dx = dres_ref[...] + r * (acc - xh * jnp.mean(acc * xh, axis=-1, keepdims=True))
        dx_ref[...] = dx
        dxb_ref[...] = dx.astype(BF16)

    row = pl.BlockSpec((tr, d), lambda i: (i, 0))
    vec = pl.BlockSpec((1, d), lambda i: (0, 0))
    outs = pl.pallas_call(
        body, grid=(s // tr,), in_specs=[row] + [row] * ng + [vec] * ng + [row],
        out_specs=[row, row] + [vec] * ng,
        out_shape=[jax.ShapeDtypeStruct((s, d), F32), jax.ShapeDtypeStruct((s, d), BF16)]
        + [jax.ShapeDtypeStruct((1, d), F32)] * ng,
        name=name, compiler_params=_cparams(),
    )(x, *dns, *gains, dres)
    return outs[0], outs[1], outs[2:]


def _loss_head(h, tgt, gain, *, tr=256):
    s, d = h.shape
    tr = _row_tile(s, tr)

    def body(h_ref, t_ref, g_ref, loss_ref, dh_ref, dhb_ref, dg_ref):
        i = pl.program_id(0)
        hv = h_ref[...]
        g = g_ref[...]
        r = _rstd(hv)
        xh = hv * r
        diff = xh * g - t_ref[...]
        part = 0.5 / d * jnp.sum(jnp.sum(diff * diff, axis=-1, keepdims=True), axis=0, keepdims=True)
        dout = diff * (1.0 / d)
        a = dout * g
        dh = r * (a - xh * jnp.mean(a * xh, axis=-1, keepdims=True))
        dh_ref[...] = dh
        dhb_ref[...] = dh.astype(BF16)
        dgt = jnp.sum(dout * xh, axis=0, keepdims=True)
        lpart = jnp.broadcast_to(part, (8, 128))

        @pl.when(i == 0)
        def _():
            dg_ref[...] = dgt
            loss_ref[...] = lpart

        @pl.when(i > 0)
        def _():
            dg_ref[...] += dgt
            loss_ref[...] += lpart

    row = pl.BlockSpec((tr, d), lambda i: (i, 0))
    vec = pl.BlockSpec((1, d), lambda i: (0, 0))
    return pl.pallas_call(
        body, grid=(s // tr,), in_specs=[row, row, vec],
        out_specs=[pl.BlockSpec((8, 128), lambda i: (0, 0)), row, row, vec],
        out_shape=[jax.ShapeDtypeStruct((8, 128), F32), jax.ShapeDtypeStruct((s, d), F32),
                   jax.ShapeDtypeStruct((s, d), BF16), jax.ShapeDtypeStruct((1, d), F32)],
        name="loss_head", compiler_params=_cparams(),
    )(h, tgt, gain)


def _causal_mask(transposed=False):
    row = lax.broadcasted_iota(jnp.int32, (CHUNK, CHUNK), 0)
    col = lax.broadcasted_iota(jnp.int32, (CHUNK, CHUNK), 1)
    return col >= row if transposed else col <= row


def _silu_parts(g):
    sg = jax.nn.sigmoid(g)
    return g * sg, sg * (1.0 + g * (1.0 - sg))


def _gate_fwd(z, ln_g, ln_b, ws, bs_t, *, tr=256):
    s = z.shape[0]
    tr = _row_tile(s, tr)
    w = A_WIDTH

    def body(u_ref, v_ref, g_ref, lg_ref, lb_ref, ws_ref, bst_ref, y_ref):
        v = v_ref[...]
        mu = jnp.mean(v, axis=-1, keepdims=True)
        xc = v - mu
        rs = lax.rsqrt(jnp.mean(xc * xc, axis=-1, keepdims=True) + EPS)
        vln = (xc * rs * lg_ref[...] + lb_ref[...]).astype(BF16)
        mask = _causal_mask()
        for grp in range(A_GROUPS):
            cols = slice(grp * CHUNK, (grp + 1) * CHUNK)
            wsm = jnp.where(mask, ws_ref[grp], 0.0).astype(BF16)
            bcol = bst_ref[:, grp:grp + 1]
            for ci in range(tr // CHUNK):
                rows = slice(ci * CHUNK, (ci + 1) * CHUNK)
                sv = jnp.dot(wsm, vln[rows, cols], preferred_element_type=F32) + bcol
                gv = g_ref[rows, cols]
                y_ref[rows, cols] = (u_ref[rows, cols] * sv * (gv * jax.nn.sigmoid(gv))).astype(BF16)

    vec = pl.BlockSpec((1, w), lambda i: (0, 0))
    return pl.pallas_call(
        body, grid=(s // tr,),
        in_specs=[pl.BlockSpec((tr, w), lambda i: (i, 0)), pl.BlockSpec((tr, w), lambda i: (i, 1)),
                  pl.BlockSpec((tr, w), lambda i: (i, 2)), vec, vec,
                  pl.BlockSpec((A_GROUPS, CHUNK, CHUNK), lambda i: (0, 0, 0)),
                  pl.BlockSpec((CHUNK, A_GROUPS), lambda i: (0, 0))],
        out_specs=pl.BlockSpec((tr, w), lambda i: (i, 0)),
        out_shape=jax.ShapeDtypeStruct((s, w), BF16), name="gate_fwd", compiler_params=_cparams(),
    )(z, z, z, ln_g, ln_b, ws, bs_t)


def _gate_bwd(z, dy, ln_g, ln_b, ws, ws_t, bs_t, *, tr=256):
    s = z.shape[0]
    tr = _row_tile(s, tr)
    w = A_WIDTH
    nsteps = s // tr

    def body(u_ref, v_ref, g_ref, dy_ref, lg_ref, lb_ref, ws_ref, wst_ref, bst_ref,
             dz_ref, dlg_ref, dlb_ref, dws_ref, dbst_ref, dvln_sc, dsv_sc):
        i = pl.program_id(0)

        @pl.when(i == 0)
        def _():
            dws_ref[...] = jnp.zeros_like(dws_ref)
            dsv_sc[...] = jnp.zeros_like(dsv_sc)

        v = v_ref[...]
        mu = jnp.mean(v, axis=-1, keepdims=True)
        xc = v - mu
        rs = lax.rsqrt(jnp.mean(xc * xc, axis=-1, keepdims=True) + EPS)
        xh = xc * rs
        lg = lg_ref[...]
        vln = (xh * lg + lb_ref[...]).astype(BF16)
        mask = _causal_mask()
        mask_t = _causal_mask(transposed=True)
        for grp in range(A_GROUPS):
            cols = slice(grp * CHUNK, (grp + 1) * CHUNK)
            wsm = jnp.where(mask, ws_ref[grp], 0.0).astype(BF16)
            wsm_t = jnp.where(mask_t, wst_ref[grp], 0.0).astype(BF16)
            bcol = bst_ref[:, grp:grp + 1]
            for ci in range(tr // CHUNK):
                rows = slice(ci * CHUNK, (ci + 1) * CHUNK)
                vb = vln[rows, cols]
                sv = jnp.dot(wsm, vb, preferred_element_type=F32) + bcol
                uv = u_ref[rows, cols]
                silu, dsilu = _silu_parts(g_ref[rows, cols])
                dyv = dy_ref[rows, cols]
                dyu = dyv * uv
                dz_ref[rows, cols] = (dyv * sv * silu).astype(BF16)
                dz_ref[rows, 2 * w + grp * CHUNK:2 * w + (grp + 1) * CHUNK] = (dyu * sv * dsilu).astype(BF16)
                dsv = dyu * silu
                dsvb = dsv.astype(BF16)
                dvln_sc[rows, cols] = jnp.dot(wsm_t, dsvb, preferred_element_type=F32)
                dws_ref[grp] += lax.dot_general(dsvb, vb, NT, preferred_element_type=F32)
                dsv_sc[grp] += dsv
        dvln = dvln_sc[...]
        dlg_t = jnp.sum(dvln * xh, axis=0, keepdims=True)
        dlb_t = jnp.sum(dvln, axis=0, keepdims=True)
        a = dvln * lg
        dv = rs * (a - jnp.mean(a, axis=-1, keepdims=True) - xh * jnp.mean(a * xh, axis=-1, keepdims=True))
        dz_ref[:, w:2 * w] = dv.astype(BF16)

        @pl.when(i == 0)
        def _():
            dlg_ref[...] = dlg_t
            dlb_ref[...] = dlb_t

        @pl.when(i > 0)
        def _():
            dlg_ref[...] += dlg_t
            dlb_ref[...] += dlb_t

        @pl.when(i == nsteps - 1)
        def _():
            for grp in range(A_GROUPS):
                dws_ref[grp] = jnp.where(mask, dws_ref[grp], 0.0)
                dbst_ref[:, grp:grp + 1] = jnp.sum(dsv_sc[grp], axis=-1, keepdims=True)

    vec = pl.BlockSpec((1, w), lambda i: (0, 0))
    wsspec = pl.BlockSpec((A_GROUPS, CHUNK, CHUNK), lambda i: (0, 0, 0))
    bsspec = pl.BlockSpec((CHUNK, A_GROUPS), lambda i: (0, 0))
    return pl.pallas_call(
        body, grid=(nsteps,),
        in_specs=[pl.BlockSpec((tr, w), lambda i: (i, 0)), pl.BlockSpec((tr, w), lambda i: (i, 1)),
                  pl.BlockSpec((tr, w), lambda i: (i, 2)), pl.BlockSpec((tr, w), lambda i: (i, 0)),
                  vec, vec, wsspec, wsspec, bsspec],
        out_specs=[pl.BlockSpec((tr, 3 * w), lambda i: (i, 0)), vec, vec, wsspec, bsspec],
        out_shape=[jax.ShapeDtypeStruct((s, 3 * w), BF16), jax.ShapeDtypeStruct((1, w), F32),
                   jax.ShapeDtypeStruct((1, w), F32), jax.ShapeDtypeStruct((A_GROUPS, CHUNK, CHUNK), F32),
                   jax.ShapeDtypeStruct((CHUNK, A_GROUPS), F32)],
        scratch_shapes=[pltpu.VMEM((tr, w), F32), pltpu.VMEM((A_GROUPS, CHUNK, CHUNK), F32)],
        name="gate_bwd", compiler_params=_cparams(),
    )(z, z, z, dy, ln_g, ln_b, ws, ws_t, bs_t)


def _rope_tables(s):
    inv_freq = ROPE_THETA ** (-jnp.arange(0, HEAD_DIM, 2, dtype=F32) / HEAD_DIM)
    ang = jnp.arange(s, dtype=F32)[:, None] * inv_freq[None, :]
    cos, sin = jnp.cos(ang), jnp.sin(ang)
    cos2 = jnp.concatenate([cos, cos], axis=-1)
    sin2 = jnp.concatenate([-sin, sin], axis=-1)
    return jnp.tile(cos2, (1, 2)), jnp.tile(sin2, (1, 2))


def _swap_halves(x):
    n = x.shape[-1]
    lane = lax.broadcasted_iota(jnp.int32, x.shape, x.ndim - 1)
    first = (lane % HEAD_DIM) < (HEAD_DIM // 2)
    return jnp.where(first, pltpu.roll(x, n - HEAD_DIM // 2, x.ndim - 1), pltpu.roll(x, HEAD_DIM // 2, x.ndim - 1))


def _tile_lanes(t, width):
    return jnp.tile(t, (1, width // t.shape[-1]))


def _kv_rope(kv, b_kv, cos, sin, *, tr=512):
    s = kv.shape[0]
    tr = _row_tile(s, tr)

    def body(kv_ref, b_ref, c_ref, s_ref, k_ref, v_ref):
        x = kv_ref[...] + b_ref[...]
        k = x[:, :KV_WIDTH]
        k_ref[...] = (k * c_ref[...] + _swap_halves(k) * s_ref[...]).astype(BF16)
        v_ref[...] = x[:, KV_WIDTH:].astype(BF16)

    tab = pl.BlockSpec((tr, KV_WIDTH), lambda i: (i, 0))
    return pl.pallas_call(
        body, grid=(s // tr,),
        in_specs=[pl.BlockSpec((tr, 2 * KV_WIDTH), lambda i: (i, 0)),
                  pl.BlockSpec((1, 2 * KV_WIDTH), lambda i: (0, 0)), tab, tab],
        out_specs=[tab, tab], out_shape=[jax.ShapeDtypeStruct((s, KV_WIDTH), BF16)] * 2,
        name="kv_rope", compiler_params=_cparams(),
    )(kv, b_kv, cos, sin)


def _kv_rope_bwd(dk_rot, dv, cos, sin, *, tr=512):
    s = dk_rot.shape[0]
    tr = _row_tile(s, tr)

    def body(dk_ref, dv_ref, c_ref, s_ref, dkv_ref, db_ref):
        i = pl.program_id(0)
        d = dk_ref[...]
        dk = d * c_ref[...] + _swap_halves(d * s_ref[...])
        dvv = dv_ref[...]
        dkv_ref[:, :KV_WIDTH] = dk.astype(BF16)
        dkv_ref[:, KV_WIDTH:] = dvv.astype(BF16)
        sk = jnp.sum(dk, axis=0, keepdims=True)
        sv = jnp.sum(dvv, axis=0, keepdims=True)

        @pl.when(i == 0)
        def _():
            db_ref[:, :KV_WIDTH] = sk
            db_ref[:, KV_WIDTH:] = sv

        @pl.when(i > 0)
        def _():
            db_ref[:, :KV_WIDTH] += sk
            db_ref[:, KV_WIDTH:] += sv

    tab = pl.BlockSpec((tr, KV_WIDTH), lambda i: (i, 0))
    return pl.pallas_call(
        body, grid=(s // tr,), in_specs=[tab, tab, tab, tab],
        out_specs=[pl.BlockSpec((tr, 2 * KV_WIDTH), lambda i: (i, 0)),
                   pl.BlockSpec((1, 2 * KV_WIDTH), lambda i: (0, 0))],
        out_shape=[jax.ShapeDtypeStruct((s, 2 * KV_WIDTH), BF16), jax.ShapeDtypeStruct((1, 2 * KV_WIDTH), F32)],
        name="kv_rope_bwd", compiler_params=_cparams(),
    )(dk_rot, dv, cos, sin)


def _window_mask(i):
    q = lax.broadcasted_iota(jnp.int32, (CHUNK, 2 * CHUNK), 0)
    k = lax.broadcasted_iota(jnp.int32, (CHUNK, 2 * CHUNK), 1)
    first_valid = jnp.where(i > 0, 0, CHUNK)
    prev = (k < CHUNK) & (k > q) & (k >= first_valid)
    cur = (k >= CHUNK) & (k - CHUNK <= q)
    return prev | cur


def _attn_specs():
    qspec = pl.BlockSpec((CHUNK, B_WIDTH), lambda i: (i, 0))
    gspec = pl.BlockSpec((CHUNK, B_WIDTH), lambda i: (i, 1))
    prev = pl.BlockSpec((CHUNK, KV_WIDTH), lambda i: (jnp.maximum(i - 1, 0), 0))
    cur = pl.BlockSpec((CHUNK, KV_WIDTH), lambda i: (i, 0))
    bq = pl.BlockSpec((1, B_WIDTH), lambda i: (0, 0))
    sinks = pl.BlockSpec(memory_space=pltpu.SMEM)
    return qspec, gspec, prev, cur, bq, sinks


def _rot_q(zq_ref, bq_ref, c_ref, s_ref):
    q = zq_ref[...] + bq_ref[...]
    cos = _tile_lanes(c_ref[...], B_WIDTH)
    sin = _tile_lanes(s_ref[...], B_WIDTH)
    return (q * cos + _swap_halves(q) * sin).astype(BF16), cos, sin


def _head_probs(qh, kh, mask, sink):
    sc = lax.dot_general(qh, kh, NT, preferred_element_type=F32) * (HEAD_DIM ** -0.5)
    sc = jnp.where(mask, sc, NEG_BIG)
    m = jnp.maximum(jnp.max(sc, axis=-1, keepdims=True), sink)
    p = jnp.exp(sc - m)
    esink = jnp.exp(sink - m)
    inv = 1.0 / (jnp.sum(p, axis=-1, keepdims=True) + esink)
    return p * inv, esink * inv


def _attn_fwd(zb, kr, vv, cos, sin, b_bq, sinks):
    s = zb.shape[0]

    def body(zq_ref, zg_ref, kp_ref, kc_ref, vp_ref, vc_ref, c_ref, s_ref, bq_ref, sk_ref, y_ref, o_sc):
        i = pl.program_id(0)
        qr, _, _ = _rot_q(zq_ref, bq_ref, c_ref, s_ref)
        kcat = jnp.concatenate([kp_ref[...], kc_ref[...]], axis=0)
        vcat = jnp.concatenate([vp_ref[...], vc_ref[...]], axis=0)
        mask = _window_mask(i)
        for h in range(N_Q_HEADS):
            kvh = h // Q_PER_KV
            hc = slice(h * HEAD_DIM, (h + 1) * HEAD_DIM)
            kc = slice(kvh * HEAD_DIM, (kvh + 1) * HEAD_DIM)
            p, _ = _head_probs(qr[:, hc], kcat[:, kc], mask, sk_ref[0, h])
            o_sc[:, hc] = jnp.dot(p.astype(BF16), vcat[:, kc], preferred_element_type=F32)
        gv = zg_ref[...]
        y_ref[...] = (o_sc[...] * (gv * jax.nn.sigmoid(gv))).astype(BF16)

    qspec, gspec, prev, cur, bq, sk = _attn_specs()
    return pl.pallas_call(
        body, grid=(s // CHUNK,),
        in_specs=[qspec, gspec, prev, cur, prev, cur, cur, cur, bq, sk],
        out_specs=qspec, out_shape=jax.ShapeDtypeStruct((s, B_WIDTH), BF16),
        scratch_shapes=[pltpu.VMEM((CHUNK, B_WIDTH), F32)],
        name="attn_fwd", compiler_params=_cparams(),
    )(zb, zb, kr, kr, vv, vv, cos, sin, b_bq, sinks)


def _attn_bwd(zb, dyb, kr, vv, cos, sin, b_bq, sinks):
    s = zb.shape[0]

    def body(zq_ref, zg_ref, dy_ref, kp_ref, kc_ref, vp_ref, vc_ref, c_ref, s_ref, bq_ref, sk_ref,
             dz_ref, dk_ref, dv_ref, dbq_ref, dsk_ref, o_sc, dq_sc):
        i = pl.program_id(0)

        @pl.when(i == 0)
        def _():
            dk_ref[...] = jnp.zeros_like(dk_ref)
            dv_ref[...] = jnp.zeros_like(dv_ref)
            dbq_ref[...] = jnp.zeros_like(dbq_ref)
            dsk_ref[...] = jnp.zeros_like(dsk_ref)

        qr, cos, sin = _rot_q(zq_ref, bq_ref, c_ref, s_ref)
        kcat = jnp.concatenate([kp_ref[...], kc_ref[...]], axis=0)
        vcat = jnp.concatenate([vp_ref[...], vc_ref[...]], axis=0)
        mask = _window_mask(i)
        gv = zg_ref[...]
        silu, dsilu = _silu_parts(gv)
        dyv = dy_ref[...]
        do_all = (dyv * silu).astype(BF16)
        lane = lax.broadcasted_iota(jnp.int32, (1, 128), 1)
        dsk_row = jnp.zeros((1, 128), F32)
        for kvh in range(N_KV_HEADS):
            kc = slice(kvh * HEAD_DIM, (kvh + 1) * HEAD_DIM)
            kh, vh = kcat[:, kc], vcat[:, kc]
            dk_acc = jnp.zeros((2 * CHUNK, HEAD_DIM), F32)
            dv_acc = jnp.zeros((2 * CHUNK, HEAD_DIM), F32)
            for r in range(Q_PER_KV):
                h = kvh * Q_PER_KV + r
                hc = slice(h * HEAD_DIM, (h + 1) * HEAD_DIM)
                qh = qr[:, hc]
                p, psink = _head_probs(qh, kh, mask, sk_ref[0, h])
                pb = p.astype(BF16)
                o_sc[:, hc] = jnp.dot(pb, vh, preferred_element_type=F32)
                doh = do_all[:, hc]
                dp = lax.dot_general(doh, vh, NT, preferred_element_type=F32)
                delta = jnp.sum(p * dp, axis=-1, keepdims=True)
                ds = (p * (dp - delta) * (HEAD_DIM ** -0.5)).astype(BF16)
                dq_sc[:, hc] = jnp.dot(ds, kh, preferred_element_type=F32)
                dk_acc = dk_acc + lax.dot_general(ds, qh, TN, preferred_element_type=F32)
                dv_acc = dv_acc + lax.dot_general(pb, doh, TN, preferred_element_type=F32)
                dsink = -jnp.sum(psink * delta, axis=0, keepdims=True)
                dsk_row = dsk_row + jnp.where(lane == h, dsink, 0.0)
            cur_rows = pl.ds(pl.multiple_of(i * CHUNK, CHUNK), CHUNK)
            dk_ref[cur_rows, kc] += dk_acc[CHUNK:]
            dv_ref[cur_rows, kc] += dv_acc[CHUNK:]

            @pl.when(i > 0)
            def _(kc=kc, dk_acc=dk_acc, dv_acc=dv_acc):
                prev_rows = pl.ds(pl.multiple_of((i - 1) * CHUNK, CHUNK), CHUNK)
                dk_ref[prev_rows, kc] += dk_acc[:CHUNK]
                dv_ref[prev_rows, kc] += dv_acc[:CHUNK]

        dsk_ref[0:1, :] += dsk_row
        dqr = dq_sc[...]
        dq = dqr * cos + _swap_halves(dqr * sin)
        dbq_ref[...] += jnp.sum(dq, axis=0, keepdims=True)
        dz_ref[:, :B_WIDTH] = dq.astype(BF16)
        dz_ref[:, B_WIDTH:] = (dyv * o_sc[...] * dsilu).astype(BF16)

    qspec, gspec, prev, cur, bq, sk = _attn_specs()
    full = pl.BlockSpec((s, KV_WIDTH), lambda i: (0, 0))
    return pl.pallas_call(
        body, grid=(s // CHUNK,),
        in_specs=[qspec, gspec, qspec, prev, cur, prev, cur, cur, cur, bq, sk],
        out_specs=[pl.BlockSpec((CHUNK, 2 * B_WIDTH), lambda i: (i, 0)), full, full, bq,
                   pl.BlockSpec((8, 128), lambda i: (0, 0))],
        out_shape=[jax.ShapeDtypeStruct((s, 2 * B_WIDTH), BF16), jax.ShapeDtypeStruct((s, KV_WIDTH), F32),
                   jax.ShapeDtypeStruct((s, KV_WIDTH), F32), jax.ShapeDtypeStruct((1, B_WIDTH), F32),
                   jax.ShapeDtypeStruct((8, 128), F32)],
        scratch_shapes=[pltpu.VMEM((CHUNK, B_WIDTH), F32), pltpu.VMEM((CHUNK, B_WIDTH), F32)],
        name="attn_bwd", compiler_params=_cparams(),
    )(zb, zb, dyb, kr, kr, vv, vv, cos, sin, b_bq, sinks)


def _local_step(x, tgt, w):
    s = x.shape[0]
    cos, sin = _rope_tables(s)
    ws = w["a_ws"]
    ws_t = jnp.swapaxes(ws, 1, 2)
    bs_t = w["a_bs"].T

    (n_a,) = _rms_fwd(x, [w["a_norm_g"]], name="rms_a")
    z = _mm_nn(n_a, w["a_w_in"], name="mm_a_in", tn=768)
    y = _gate_fwd(z, w["a_ln_g"], w["a_ln_b"], ws, bs_t)
    h1 = _mm_nn(y, w["a_w_out"], name="mm_a_out", tn=D_MODEL, residual=x)
    n_kv, n_b = _rms_fwd(h1, [w["kv_norm_g"], w["b_norm_g"]], name="rms_b")
    kv = _mm_nn(n_kv, w["w_kv"], name="mm_kv", tn=2 * KV_WIDTH)
    kr, vv = _kv_rope(kv, w["b_kv"], cos, sin)
    zb = _mm_nn(n_b, w["b_w_in"], name="mm_b_in", tn=512)
    yb = _attn_fwd(zb, kr, vv, cos, sin, w["b_bq"], w["b_sinks"])
    h2 = _mm_nn(yb, w["b_w_out"], name="mm_b_out", tn=D_MODEL, residual=h1)
    loss_blk, dh2, dh2b, d_final_g = _loss_head(h2, tgt, w["final_norm_g"])

    d_b_w_out = _mm_tn(yb, dh2b, name="mm_d_b_w_out", tm=B_WIDTH, tn=D_MODEL)
    dyb = _mm_nt(dh2b, w["b_w_out"], name="mm_dyb", tk=D_MODEL)
    dzb, dk_rot, dv, d_bq, d_sinks = _attn_bwd(zb, dyb, kr, vv, cos, sin, w["b_bq"], w["b_sinks"])
    dkv, d_b_kv = _kv_rope_bwd(dk_rot, dv, cos, sin)
    d_b_w_in = _mm_tn(n_b, dzb, name="mm_d_b_w_in", tm=D_MODEL, tn=512, shards=N_CHIPS)
    dn_b = _mm_nt(dzb, w["b_w_in"], name="mm_dn_b", tk=512)
    d_w_kv = _mm_tn(n_kv, dkv, name="mm_d_w_kv", tm=D_MODEL, tn=2 * KV_WIDTH)
    dn_kv = _mm_nt(dkv, w["w_kv"], name="mm_dn_kv", tk=2 * KV_WIDTH)
    dh1, dh1b, (d_kv_g, d_b_g) = _rms_bwd(h1, [dn_kv, dn_b], [w["kv_norm_g"], w["b_norm_g"]], dh2, name="rms_b_bwd")

    d_a_w_out = _mm_tn(y, dh1b, name="mm_d_a_w_out", tm=1024, tn=D_MODEL)
    dy = _mm_nt(dh1b, w["a_w_out"], name="mm_dy", tk=D_MODEL, tn=1024)
    dz, d_ln_g, d_ln_b, d_ws, d_bs_t = _gate_bwd(z, dy, w["a_ln_g"], w["a_ln_b"], ws, ws_t, bs_t)
    d_a_w_in = _mm_tn(n_a, dz, name="mm_d_a_w_in", tm=D_MODEL, tn=768, shards=N_CHIPS)
    dn_a = _mm_nt(dz, w["a_w_in"], name="mm_dn_a", tk=768)
    dx, _, (d_a_g,) = _rms_bwd(x, [dn_a], [w["a_norm_g"]], dh1, name="rms_a_bwd")

    big = {
        "a_w_in": d_a_w_in,
        "a_w_out": d_a_w_out.reshape(N_CHIPS, A_WIDTH // N_CHIPS, D_MODEL),
        "w_kv": d_w_kv.reshape(N_CHIPS, D_MODEL // N_CHIPS, 2 * KV_WIDTH),
        "b_w_in": d_b_w_in,
        "b_w_out": d_b_w_out.reshape(N_CHIPS, B_WIDTH // N_CHIPS, D_MODEL),
    }
    small = {
        "a_ws": d_ws, "a_bs": d_bs_t.T, "a_norm_g": d_a_g, "a_ln_g": d_ln_g, "a_ln_b": d_ln_b,
        "kv_norm_g": d_kv_g, "b_kv": d_b_kv, "b_norm_g": d_b_g, "b_bq": d_bq,
        "b_sinks": d_sinks[0:1, :N_Q_HEADS], "final_norm_g": d_final_g,
    }
    return loss_blk, dx, big, small


def _place():
    x, y, c = lax.axis_index("x"), lax.axis_index("y"), lax.axis_index("c")
    return x, y, c, [(1 - x, y), (x, 1 - y), (1 - x, 1 - y)]


HBM = pl.BlockSpec(memory_space=pl.ANY)


def _gather_shards(arrs):
    n = len(arrs)

    def body(*refs):
        ins, outs = refs[:n], refs[n:2 * n]
        send_ici, recv_ici, send_d2d, recv_d2d, local_sem = refs[2 * n:]
        x, y, c, chips = _place()
        me = 2 * x + y
        sibling = (x, y, 1 - c)
        local = [pltpu.make_async_copy(ins[a], outs[a].at[me], local_sem.at[a]) for a in range(n)]
        for cp in local:
            cp.start()

        def rows(a, half):
            hr = arrs[a].shape[0] // 2
            return pl.ds(half * hr, hr)

        def ici(a, j, src_chip, to):
            return pltpu.make_async_remote_copy(
                src_ref=ins[a].at[rows(a, c)], dst_ref=outs[a].at[src_chip, rows(a, c)],
                send_sem=send_ici.at[a, j], recv_sem=recv_ici.at[a, j], device_id=to, device_id_type=MESH)

        def d2d(a, j, chip, half):
            blk = outs[a].at[chip, rows(a, half)]
            return pltpu.make_async_remote_copy(
                src_ref=blk, dst_ref=blk, send_sem=send_d2d.at[a, j], recv_sem=recv_d2d.at[a, j],
                device_id=sibling, device_id_type=MESH)

        sends = [ici(a, j, me, (*chip, c)) for a in range(n) for j, chip in enumerate(chips)]
        for cp in sends:
            cp.start()
        passes = []
        for a in range(n):
            for j, (px, py) in enumerate(chips):
                ici(a, j, 2 * px + py, (px, py, c)).wait_recv()
                cp = d2d(a, j, 2 * px + py, c)
                cp.start()
                passes.append(cp)
        for a in range(n):
            for j, (px, py) in enumerate(chips):
                d2d(a, j, 2 * px + py, 1 - c).wait_recv()
        for cp in sends + passes:
            cp.wait_send()
        for cp in local:
            cp.wait()

    return pl.pallas_call(
        body, in_specs=[HBM] * n, out_specs=[HBM] * n,
        out_shape=[jax.ShapeDtypeStruct((N_CHIPS,) + a.shape, a.dtype) for a in arrs],
        scratch_shapes=[pltpu.SemaphoreType.DMA((n, 3)), pltpu.SemaphoreType.DMA((n, 3)),
                        pltpu.SemaphoreType.DMA((n, 3)), pltpu.SemaphoreType.DMA((n, 3)),
                        pltpu.SemaphoreType.DMA((n,))],
        name="gather_weights",
    )(*arrs)


def _exchange_halves(grads):
    n = len(grads)

    def body(*refs):
        ins, outs = refs[:n], refs[n:2 * n]
        send_sem, recv_sem = refs[2 * n:]
        x, y, c, _ = _place()
        cps = []
        for a in range(n):
            hr = grads[a].shape[1] // 2
            cp = pltpu.make_async_remote_copy(
                src_ref=ins[a].at[:, pl.ds((1 - c) * hr, hr), :], dst_ref=outs[a],
                send_sem=send_sem.at[a], recv_sem=recv_sem.at[a], device_id=(x, y, 1 - c), device_id_type=MESH)
            cp.start()
            cps.append(cp)
        for cp in cps:
            cp.wait()

    return pl.pallas_call(
        body, in_specs=[HBM] * n, out_specs=[HBM] * n,
        out_shape=[jax.ShapeDtypeStruct((g.shape[0], g.shape[1] // 2, g.shape[2]), g.dtype) for g in grads],
        scratch_shapes=[pltpu.SemaphoreType.DMA((n,)), pltpu.SemaphoreType.DMA((n,))],
        name="exchange_halves",
    )(*grads)


def _relations():
    return [(r >> 2 & 1, r >> 1 & 1, r & 1) for r in range(1, 8)]


def _scatter_partials(chip_sums, small):
    n = len(chip_sums)

    def body(*refs):
        ins, small_in = refs[:n], refs[n]
        outs, small_out = refs[n + 1:2 * n + 1], refs[2 * n + 1]
        send_sem, recv_sem, ssend_sem, srecv_sem = refs[2 * n + 2:]
        x, y, c, chips = _place()
        cps = []
        for a in range(n):
            for j, (px, py) in enumerate(chips):
                cp = pltpu.make_async_remote_copy(
                    src_ref=ins[a].at[2 * px + py], dst_ref=outs[a].at[j],
                    send_sem=send_sem.at[a, j], recv_sem=recv_sem.at[a, j], device_id=(px, py, c), device_id_type=MESH)
                cp.start()
                cps.append(cp)
        for r, (fx, fy, fc) in enumerate(_relations(), start=1):
            px, py, pc = x ^ fx, y ^ fy, c ^ fc
            cp = pltpu.make_async_remote_copy(
                src_ref=small_in.at[4 * px + 2 * py + pc], dst_ref=small_out.at[r],
                send_sem=ssend_sem.at[r - 1], recv_sem=srecv_sem.at[r - 1], device_id=(px, py, pc), device_id_type=MESH)
            cp.start()
            cps.append(cp)
        for cp in cps:
            cp.wait()

    return pl.pallas_call(
        body, in_specs=[HBM] * (n + 1), out_specs=[HBM] * (n + 1),
        out_shape=[jax.ShapeDtypeStruct((3,) + t.shape[1:], t.dtype) for t in chip_sums]
        + [jax.ShapeDtypeStruct(small.shape, small.dtype)],
        scratch_shapes=[pltpu.SemaphoreType.DMA((n, 3)), pltpu.SemaphoreType.DMA((n, 3)),
                        pltpu.SemaphoreType.DMA((7,)), pltpu.SemaphoreType.DMA((7,))],
        name="scatter_partials",
    )(*chip_sums, small)


def _share_reduced(halves, small):
    n = len(halves)

    def body(*refs):
        ins, small_in = refs[:n], refs[n]
        outs, small_out = refs[n + 1:2 * n + 1], refs[2 * n + 1]
        send_sem, recv_sem, ssend_sem, srecv_sem = refs[2 * n + 2:]
        x, y, c, _ = _place()
        me = 4 * x + 2 * y + c
        cps = []
        for a in range(n):
            hr = halves[a].shape[0] // 2
            cp = pltpu.make_async_remote_copy(
                src_ref=ins[a].at[pl.ds(c * hr, hr)], dst_ref=outs[a].at[pl.ds(c * hr, hr)],
                send_sem=send_sem.at[a], recv_sem=recv_sem.at[a], device_id=(x, y, 1 - c), device_id_type=MESH)
            cp.start()
            cps.append(cp)
        for r, (fx, fy, fc) in enumerate(_relations(), start=1):
            cp = pltpu.make_async_remote_copy(
                src_ref=small_in.at[me], dst_ref=small_out.at[me],
                send_sem=ssend_sem.at[r - 1], recv_sem=srecv_sem.at[r - 1],
                device_id=(x ^ fx, y ^ fy, c ^ fc), device_id_type=MESH)
            cp.start()
            cps.append(cp)
        for a in range(n):
            hr = halves[a].shape[0] // 2
            other = outs[a].at[pl.ds((1 - c) * hr, hr)]
            pltpu.make_async_remote_copy(
                src_ref=other, dst_ref=other, send_sem=send_sem.at[a], recv_sem=recv_sem.at[a],
                device_id=(x, y, 1 - c), device_id_type=MESH).wait_recv()
        for r, (fx, fy, fc) in enumerate(_relations(), start=1):
            theirs = small_out.at[4 * (x ^ fx) + 2 * (y ^ fy) + (c ^ fc)]
            pltpu.make_async_remote_copy(
                src_ref=theirs, dst_ref=theirs, send_sem=ssend_sem.at[r - 1], recv_sem=srecv_sem.at[r - 1],
                device_id=(x ^ fx, y ^ fy, c ^ fc), device_id_type=MESH).wait_recv()
        for cp in cps:
            cp.wait_send()

    return pl.pallas_call(
        body, in_specs=[HBM] * (n + 1), out_specs=[HBM] * (n + 1),
        out_shape=[jax.ShapeDtypeStruct(h.shape, h.dtype) for h in halves]
        + [jax.ShapeDtypeStruct(small.shape, small.dtype)],
        input_output_aliases={i: i for i in range(n + 1)},
        scratch_shapes=[pltpu.SemaphoreType.DMA((n,)), pltpu.SemaphoreType.DMA((n,)),
                        pltpu.SemaphoreType.DMA((7,)), pltpu.SemaphoreType.DMA((7,))],
        name="share_reduced",
    )(*halves, small)


def _col_tile(cols):
    return 512 if cols % 512 == 0 else cols


def _add_sibling(grad, recv, core, *, name):
    k, r, c = grad.shape
    hr = r // 2
    tr = min(hr, 256)
    tc = _col_tile(c)
    nrb = hr // tr

    def body(core_ref, g_ref, r_ref, o_ref):
        o_ref[...] = g_ref[...] + r_ref[...]

    return pl.pallas_call(
        body,
        grid_spec=pltpu.PrefetchScalarGridSpec(
            num_scalar_prefetch=1, grid=(k, nrb, c // tc),
            in_specs=[pl.BlockSpec((None, tr, tc), lambda kk, i, j, core: (kk, core[0] * nrb + i, j)),
                      pl.BlockSpec((None, tr, tc), lambda kk, i, j, core: (kk, i, j))],
            out_specs=pl.BlockSpec((None, tr, tc), lambda kk, i, j, core: (kk, i, j))),
        out_shape=jax.ShapeDtypeStruct((k, hr, c), F32), name=name, compiler_params=_cparams(),
    )(core, grad, recv)


def _sum_chips(chip_sums, recv, place, *, name):
    _, hr, c = chip_sums.shape
    tr = min(hr, 256)
    tc = _col_tile(c)
    nrb = hr // tr

    def body(place_ref, own_ref, r0_ref, r1_ref, r2_ref, o_ref):
        o_ref[...] = ((own_ref[...] + r0_ref[...]) + r1_ref[...]) + r2_ref[...]

    def rspec(j):
        return pl.BlockSpec((None, tr, tc), lambda i, jj, place: (j, i, jj))

    return pl.pallas_call(
        body,
        grid_spec=pltpu.PrefetchScalarGridSpec(
            num_scalar_prefetch=1, grid=(nrb, c // tc),
            in_specs=[pl.BlockSpec((None, tr, tc), lambda i, jj, place: (place[0], i, jj)),
                      rspec(0), rspec(1), rspec(2)],
            out_specs=pl.BlockSpec((tr, tc), lambda i, jj, place: (place[1] * nrb + i, jj))),
        out_shape=jax.ShapeDtypeStruct((2 * hr, c), F32), name=name, compiler_params=_cparams(),
    )(place, chip_sums, recv, recv, recv)


def _sum_small(small, recv, place):
    _, sr, _ = small.shape

    def body(place_ref, own_ref, r_ref, o_ref):
        acc = own_ref[...]
        for r in range(1, 8):
            acc = acc + r_ref[r]
        o_ref[...] = acc

    return pl.pallas_call(
        body,
        grid_spec=pltpu.PrefetchScalarGridSpec(
            num_scalar_prefetch=1, grid=(1,),
            in_specs=[pl.BlockSpec((None, sr, 128), lambda i, place: (place[2], 0, 0)),
                      pl.BlockSpec((8, sr, 128), lambda i, place: (0, 0, 0))],
            out_specs=pl.BlockSpec((None, sr, 128), lambda i, place: (place[2], 0, 0))),
        out_shape=jax.ShapeDtypeStruct(small.shape, F32), name="sum_small", compiler_params=_cparams(),
    )(place, small, recv)


def _adamw(w, g, m, v, *, name):
    r, c = w.shape
    tr = 256 if r % 256 == 0 else r
    tc = _col_tile(c)
    bc1 = 1.0 - ADAM_B1 ** ADAM_STEP
    bc2 = 1.0 - ADAM_B2 ** ADAM_STEP

    def body(w_ref, g_ref, m_ref, v_ref, d_ref, nm_ref, nv_ref):
        gv = g_ref[...]
        nm = ADAM_B1 * m_ref[...] + (1.0 - ADAM_B1) * gv
        nv = ADAM_B2 * v_ref[...] + (1.0 - ADAM_B2) * (gv * gv)
        d_ref[...] = -ADAM_LR * ((nm / bc1) / (jnp.sqrt(nv / bc2) + ADAM_EPS) + ADAM_WD * w_ref[...])
        nm_ref[...] = nm
        nv_ref[...] = nv

    spec = pl.BlockSpec((tr, tc), lambda i, j: (i, j))
    return pl.pallas_call(
        body, grid=(r // tr, c // tc), in_specs=[spec] * 4, out_specs=[spec] * 3,
        out_shape=[jax.ShapeDtypeStruct((r, c), F32)] * 3, name=name, compiler_params=_cparams(),
    )(w, g, m, v)


SMALL_ORDER = ["a_ws", "a_bs", "a_norm_g", "a_ln_g", "a_ln_b", "kv_norm_g", "b_kv", "b_norm_g", "b_bq",
               "b_sinks", "final_norm_g"]
SHARDED_SMALL = {"a_norm_g", "a_ln_g", "a_ln_b"}


PACK_TILE = 8 * 128


def _rows128(a):
    flat = a.reshape(-1)
    return jnp.pad(flat, (0, (-flat.shape[0]) % PACK_TILE)).reshape(-1, 128)


def _pack_rows(parts, multiple):
    rows = [_rows128(p) for p in parts]
    total = sum(r.shape[0] for r in rows)
    pad = (-total) % multiple
    if pad:
        rows.append(jnp.zeros((pad, 128), rows[0].dtype))
    return jnp.concatenate(rows, axis=0)


def _unpack_rows(packed, shapes):
    out, row = [], 0
    for shp in shapes:
        size = math.prod(shp)
        nrow = -(-size // PACK_TILE) * 8
        out.append(packed[row:row + nrow].reshape(-1)[:size].reshape(shp))
        row += nrow
    return out


WEIGHTS = ["a_norm_g", "a_w_in", "a_ln_g", "a_ln_b", "a_ws", "a_bs", "a_w_out", "kv_norm_g", "w_kv", "b_kv",
           "b_norm_g", "b_w_in", "b_bq", "b_sinks", "b_w_out", "final_norm_g"]
BIG = ["a_w_in", "a_w_out", "w_kv", "b_w_in", "b_w_out"]


def _step(x, loss_target, p, m, v):
    xi, yi, ci = lax.axis_index("x"), lax.axis_index("y"), lax.axis_index("c")
    chip = 2 * xi + yi
    device = 4 * xi + 2 * yi + ci
    core = jnp.reshape(ci, (1,)).astype(jnp.int32)
    place = jnp.stack([chip, ci, device]).astype(jnp.int32)

    shard2d = {n: p[n].reshape(p[n].shape[-2:]) for n in BIG}
    vec_shapes = [p[n].shape for n in ("a_norm_g", "a_ln_g", "a_ln_b")]
    vec_pack = _pack_rows([p["a_norm_g"], p["a_ln_g"], p["a_ln_b"]], 16)
    gathered = _gather_shards([shard2d[n].astype(BF16) for n in BIG] + [vec_pack])
    full = dict(zip(BIG, gathered[:len(BIG)]))
    vecs = [_unpack_rows(gathered[-1][k], vec_shapes) for k in range(N_CHIPS)]
    w = {
        "a_w_in": full["a_w_in"], "b_w_in": full["b_w_in"],
        "a_w_out": full["a_w_out"].reshape(A_WIDTH, D_MODEL),
        "w_kv": full["w_kv"].reshape(D_MODEL, 2 * KV_WIDTH),
        "b_w_out": full["b_w_out"].reshape(B_WIDTH, D_MODEL),
        "a_norm_g": jnp.concatenate([vk[0] for vk in vecs], axis=-1),
        "a_ln_g": jnp.concatenate([vk[1] for vk in vecs], axis=-1),
        "a_ln_b": jnp.concatenate([vk[2] for vk in vecs], axis=-1),
        "a_ws": p["a_ws"][0], "a_bs": p["a_bs"][0],
        "kv_norm_g": p["kv_norm_g"].reshape(1, -1), "b_kv": p["b_kv"].reshape(1, -1),
        "b_norm_g": p["b_norm_g"], "b_bq": p["b_bq"], "b_sinks": p["b_sinks"],
        "final_norm_g": p["final_norm_g"].reshape(1, -1),
    }

    loss_blk, dx, big, small = _local_step(x[0], loss_target[0], w)

    small_shapes = [small[n].shape for n in SMALL_ORDER]
    small_pack = _pack_rows([small[n] for n in SMALL_ORDER], 64)
    seg = small_pack.shape[0] // 8
    small_pack = small_pack.reshape(8, seg, 128)
    partial = [big[n] for n in BIG]
    from_sibling = _exchange_halves(partial)
    chip_sums = [_add_sibling(g, r, core, name="add_sibling_" + n) for g, r, n in zip(partial, from_sibling, BIG)]
    arrived = _scatter_partials(chip_sums, small_pack)
    halves = [_sum_chips(t, r, place, name="sum_chips_" + n) for t, r, n in zip(chip_sums, arrived[:-1], BIG)]
    small_mine = _sum_small(small_pack, arrived[-1], place)
    shared = _share_reduced(halves, small_mine)
    grad_big = dict(zip(BIG, shared[:-1]))
    small_full = _unpack_rows(shared[-1].reshape(8 * seg, 128), small_shapes)
    grads = {}
    for n, gfull in zip(SMALL_ORDER, small_full):
        if n in SHARDED_SMALL:
            width = p[n].shape[-1]
            gfull = lax.dynamic_slice_in_dim(gfull, chip * width, width, axis=-1)
        grads[n] = gfull.reshape(p[n].shape)
    for n in BIG:
        grads[n] = grad_big[n].reshape(p[n].shape)

    delta, new_m, new_v = {}, {}, {}
    for n in BIG:
        d, nm, nv = _adamw(shard2d[n], grad_big[n], m[n].reshape(shard2d[n].shape), v[n].reshape(shard2d[n].shape),
                           name="adamw_" + n)
        delta[n], new_m[n], new_v[n] = d.reshape(p[n].shape), nm.reshape(p[n].shape), nv.reshape(p[n].shape)
    shapes = [p[n].shape for n in SMALL_ORDER]
    packs = [_pack_rows([src[n] for n in SMALL_ORDER], 8) for src in (p, grads, m, v)]
    outs = _adamw(*packs, name="adamw_small")
    for res, packed in zip((delta, new_m, new_v), outs):
        for n, val in zip(SMALL_ORDER, _unpack_rows(packed, shapes)):
            res[n] = val

    loss = lax.psum(loss_blk[0, 0], ("x", "y", "c"))
    return (loss, dx[None], *[grads[n] for n in WEIGHTS], *[delta[n] for n in WEIGHTS],
            *[new_m[n] for n in WEIGHTS], *[new_v[n] for n in WEIGHTS])


def kernel(x, a_norm_g, a_w_in, a_ln_g, a_ln_b, a_ws, a_bs, a_w_out, kv_norm_g, w_kv, b_kv, b_norm_g, b_w_in, b_bq, b_sinks, b_w_out, final_norm_g, loss_target, m_a_norm_g, m_a_w_in, m_a_ln_g, m_a_ln_b, m_a_ws, m_a_bs, m_a_w_out, m_kv_norm_g, m_w_kv, m_b_kv, m_b_norm_g, m_b_w_in, m_b_bq, m_b_sinks, m_b_w_out, m_final_norm_g, v_a_norm_g, v_a_w_in, v_a_ln_g, v_a_ln_b, v_a_ws, v_a_bs, v_a_w_out, v_kv_norm_g, v_w_kv, v_b_kv, v_b_norm_g, v_b_w_in, v_b_bq, v_b_sinks, v_b_w_out, v_final_norm_g):
    p = dict(a_norm_g=a_norm_g, a_w_in=a_w_in, a_ln_g=a_ln_g, a_ln_b=a_ln_b, a_ws=a_ws, a_bs=a_bs, a_w_out=a_w_out,
             kv_norm_g=kv_norm_g, w_kv=w_kv, b_kv=b_kv, b_norm_g=b_norm_g, b_w_in=b_w_in, b_bq=b_bq, b_sinks=b_sinks,
             b_w_out=b_w_out, final_norm_g=final_norm_g)
    m = dict(a_norm_g=m_a_norm_g, a_w_in=m_a_w_in, a_ln_g=m_a_ln_g, a_ln_b=m_a_ln_b, a_ws=m_a_ws, a_bs=m_a_bs,
             a_w_out=m_a_w_out, kv_norm_g=m_kv_norm_g, w_kv=m_w_kv, b_kv=m_b_kv, b_norm_g=m_b_norm_g, b_w_in=m_b_w_in,
             b_bq=m_b_bq, b_sinks=m_b_sinks, b_w_out=m_b_w_out, final_norm_g=m_final_norm_g)
    v = dict(a_norm_g=v_a_norm_g, a_w_in=v_a_w_in, a_ln_g=v_a_ln_g, a_ln_b=v_a_ln_b, a_ws=v_a_ws, a_bs=v_a_bs,
             a_w_out=v_a_w_out, kv_norm_g=v_kv_norm_g, w_kv=v_w_kv, b_kv=v_b_kv, b_norm_g=v_b_norm_g, b_w_in=v_b_w_in,
             b_bq=v_b_bq, b_sinks=v_b_sinks, b_w_out=v_b_w_out, final_norm_g=v_final_norm_g)
    return _step(x, loss_target, p, m, v)
```

```python
import functools
import math

import jax
import jax.numpy as jnp
from jax import lax
from jax.experimental import pallas as pl
from jax.experimental.pallas import tpu as pltpu

F32 = jnp.float32
BF16 = jnp.bfloat16

D_MODEL = 1024
CHUNK = 128
A_WIDTH = 2048
A_GROUPS = 16
HEAD_DIM = 64
N_Q_HEADS = 16
N_KV_HEADS = 2
Q_PER_KV = 8
B_WIDTH = 1024
KV_WIDTH = 128
ROPE_THETA = 10000.0
EPS = 1e-5
N_CHIPS = 4

ADAM_LR = 0.001
ADAM_B1 = 0.9
ADAM_B2 = 0.999
ADAM_EPS = 1e-08
ADAM_WD = 0.01
ADAM_STEP = 10

V7X_VMEM_BYTES = 64 * 1024 * 1024
VMEM_LIMIT = 48 * 1024 * 1024
MESH = pl.DeviceIdType.MESH
NEG_BIG = -1e30


def _cparams(**kw):
    return pltpu.CompilerParams(vmem_limit_bytes=VMEM_LIMIT, **kw)


def _matmul(a, b, *, dims, grid, a_spec, b_spec, o_spec, out_shape, name, acc_axis=None,
            residual=None, r_spec=None):
    has_res = residual is not None

    def body(*refs):
        if has_res:
            a_ref, b_ref, r_ref, o_ref = refs
        else:
            a_ref, b_ref, o_ref = refs
        part = lax.dot_general(a_ref[...], b_ref[...], dims, preferred_element_type=F32)
        if acc_axis is None:
            if has_res:
                part = part + r_ref[...]
            o_ref[...] = part.astype(o_ref.dtype)
        else:
            k = pl.program_id(acc_axis)

            @pl.when(k == 0)
            def _():
                o_ref[...] = part

            @pl.when(k > 0)
            def _():
                o_ref[...] += part

    in_specs = [a_spec, b_spec] + ([r_spec] if has_res else [])
    args = (a, b) + ((residual,) if has_res else ())
    return pl.pallas_call(
        body, grid=grid, in_specs=in_specs, out_specs=o_spec, out_shape=out_shape, name=name,
        compiler_params=_cparams(),
    )(*args)


NN = (((1,), (0,)), ((), ()))
NT = (((1,), (1,)), ((), ()))
TN = (((0,), (0,)), ((), ()))


def _row_tile(s, want):
    return min(s, want)


def _mm_nn(a, b, *, name, tn, out_dtype=F32, residual=None, tm=512):
    s, k = a.shape
    tm = _row_tile(s, tm)
    if b.ndim == 3:
        nsh, _, nc = b.shape
        npb = nc // tn
        n = nsh * nc
        b_spec = pl.BlockSpec((None, k, tn), lambda i, j: (j // npb, 0, j % npb))
    else:
        n = b.shape[1]
        b_spec = pl.BlockSpec((k, tn), lambda i, j: (0, j))
    return _matmul(
        a, b, dims=NN, grid=(s // tm, n // tn),
        a_spec=pl.BlockSpec((tm, k), lambda i, j: (i, 0)), b_spec=b_spec,
        o_spec=pl.BlockSpec((tm, tn), lambda i, j: (i, j)),
        out_shape=jax.ShapeDtypeStruct((s, n), out_dtype), name=name,
        residual=residual, r_spec=pl.BlockSpec((tm, tn), lambda i, j: (i, j)) if residual is not None else None)


def _mm_nt(a, b, *, name, tk, tn=None, tm=512, out_dtype=F32):
    s, k = a.shape
    tm = _row_tile(s, tm)
    if b.ndim == 3:
        nsh, n, kc = b.shape
        npb = kc // tk
        return _matmul(
            a, b, dims=NT, grid=(s // tm, k // tk), acc_axis=1,
            a_spec=pl.BlockSpec((tm, tk), lambda i, kk: (i, kk)),
            b_spec=pl.BlockSpec((None, n, tk), lambda i, kk: (kk // npb, 0, kk % npb)),
            o_spec=pl.BlockSpec((tm, n), lambda i, kk: (i, 0)),
            out_shape=jax.ShapeDtypeStruct((s, n), F32), name=name)
    n = b.shape[0]
    tn = n if tn is None else tn
    assert tk == k
    return _matmul(
        a, b, dims=NT, grid=(s // tm, n // tn),
        a_spec=pl.BlockSpec((tm, k), lambda i, j: (i, 0)),
        b_spec=pl.BlockSpec((tn, k), lambda i, j: (j, 0)),
        o_spec=pl.BlockSpec((tm, tn), lambda i, j: (i, j)),
        out_shape=jax.ShapeDtypeStruct((s, n), out_dtype), name=name)


def _mm_tn(a, b, *, name, tm, tn, tk=512, shards=None):
    s, m = a.shape
    n = b.shape[1]
    tk = _row_tile(s, tk)
    if shards is None:
        o_spec = pl.BlockSpec((tm, tn), lambda i, j, kk: (i, j))
        out_shape = jax.ShapeDtypeStruct((m, n), F32)
    else:
        assert tm == m
        nc = n // shards
        npb = nc // tn
        o_spec = pl.BlockSpec((None, m, tn), lambda i, j, kk: (j // npb, 0, j % npb))
        out_shape = jax.ShapeDtypeStruct((shards, m, nc), F32)
    return _matmul(
        a, b, dims=TN, grid=(m // tm, n // tn, s // tk), acc_axis=2,
        a_spec=pl.BlockSpec((tk, tm), lambda i, j, kk: (kk, i)),
        b_spec=pl.BlockSpec((tk, tn), lambda i, j, kk: (kk, j)),
        o_spec=o_spec, out_shape=out_shape, name=name)


def _rstd(x):
    return lax.rsqrt(jnp.mean(x * x, axis=-1, keepdims=True) + EPS)


def _rms_fwd(x, gains, *, name, tr=256):
    s, d = x.shape
    tr = _row_tile(s, tr)
    ng = len(gains)

    def body(*refs):
        x_ref = refs[0]
        xv = x_ref[...]
        xh = xv * _rstd(xv)
        for t in range(ng):
            refs[1 + ng + t][...] = (xh * refs[1 + t][...]).astype(BF16)

    row = pl.BlockSpec((tr, d), lambda i: (i, 0))
    vec = pl.BlockSpec((1, d), lambda i: (0, 0))
    return pl.pallas_call(
        body, grid=(s // tr,), in_specs=[row] + [vec] * ng, out_specs=[row] * ng,
        out_shape=[jax.ShapeDtypeStruct((s, d), BF16)] * ng, name=name, compiler_params=_cparams(),
    )(x, *gains)


def _rms_bwd(x, dns, gains, dres, *, name, tr=256):
    s, d = x.shape
    tr = _row_tile(s, tr)
    ng = len(gains)

    def body(*refs):
        x_ref = refs[0]
        dn_refs = refs[1:1 + ng]
        g_refs = refs[1 + ng:1 + 2 * ng]
        dres_ref = refs[1 + 2 * ng]
        dx_ref, dxb_ref = refs[2 + 2 * ng], refs[3 + 2 * ng]
        dg_refs = refs[4 + 2 * ng:]
        i = pl.program_id(0)
        xv = x_ref[...]
        r = _rstd(xv)
        xh = xv * r
        acc = jnp.zeros_like(xv)
        for t in range(ng):
            dn = dn_refs[t][...]
            acc = acc + dn * g_refs[t][...]
            dgt = jnp.sum(dn * xh, axis=0, keepdims=True)

            @pl.when(i == 0)
            def _(t=t, dgt=dgt):
                dg_refs[t][...] = dgt

            @pl.when(i > 0)
            def _(t=t, dgt=dgt):
                dg_refs[t][...] += dgt

        dx = dres_ref[...] + r * (acc - xh * jnp.mean(acc * xh, axis=-1, keepdims=True))
        dx_ref[...] = dx
        dxb_ref[...] = dx.astype(BF16)

    row = pl.BlockSpec((tr, d), lambda i: (i, 0))
    vec = pl.BlockSpec((1, d), lambda i: (0, 0))
    outs = pl.pallas_call(
        body, grid=(s // tr,), in_specs=[row] + [row] * ng + [vec] * ng + [row],
        out_specs=[row, row] + [vec] * ng,
        out_shape=[jax.ShapeDtypeStruct((s, d), F32), jax.ShapeDtypeStruct((s, d), BF16)]
        + [jax.ShapeDtypeStruct((1, d), F32)] * ng,
        name=name, compiler_params=_cparams(),
    )(x, *dns, *gains, dres)
    return outs[0], outs[1], outs[2:]


def _loss_head(h, tgt, gain, *, tr=256):
    s, d = h.shape
    tr = _row_tile(s, tr)

    def body(h_ref, t_ref, g_ref, loss_ref, dh_ref, dhb_ref, dg_ref):
        i = pl.program_id(0)
        hv = h_ref[...]
        g = g_ref[...]
        r = _rstd(hv)
        xh = hv * r
        diff = xh * g - t_ref[...]
        part = 0.5 / d * jnp.sum(jnp.sum(diff * diff, axis=-1, keepdims=True), axis=0, keepdims=True)
        dout = diff * (1.0 / d)
        a = dout * g
        dh = r * (a - xh * jnp.mean(a * xh, axis=-1, keepdims=True))
        dh_ref[...] = dh
        dhb_ref[...] = dh.astype(BF16)
        dgt = jnp.sum(dout * xh, axis=0, keepdims=True)
        lpart = jnp.broadcast_to(part, (8, 128))

        @pl.when(i == 0)
        def _():
            dg_ref[...] = dgt
            loss_ref[...] = lpart

        @pl.when(i > 0)
        def _():
            dg_ref[...] += dgt
            loss_ref[...] += lpart

    row = pl.BlockSpec((tr, d), lambda i: (i, 0))
    vec = pl.BlockSpec((1, d), lambda i: (0, 0))
    return pl.pallas_call(
        body, grid=(s // tr,), in_specs=[row, row, vec],
        out_specs=[pl.BlockSpec((8, 128), lambda i: (0, 0)), row, row, vec],
        out_shape=[jax.ShapeDtypeStruct((8, 128), F32), jax.ShapeDtypeStruct((s, d), F32),
                   jax.ShapeDtypeStruct((s, d), BF16), jax.ShapeDtypeStruct((1, d), F32)],
        name="loss_head", compiler_params=_cparams(),
    )(h, tgt, gain)


def _causal_mask(transposed=False):
    row = lax.broadcasted_iota(jnp.int32, (CHUNK, CHUNK), 0)
    col = lax.broadcasted_iota(jnp.int32, (CHUNK, CHUNK), 1)
    return col >= row if transposed else col <= row


def _silu_parts(g):
    sg = jax.nn.sigmoid(g)
    return g * sg, sg * (1.0 + g * (1.0 - sg))


def _gate_fwd(z, ln_g, ln_b, ws, bs_t, *, tr=256):
    s = z.shape[0]
    tr = _row_tile(s, tr)
    w = A_WIDTH

    def body(u_ref, v_ref, g_ref, lg_ref, lb_ref, ws_ref, bst_ref, y_ref):
        v = v_ref[...].astype(F32)
        mu = jnp.mean(v, axis=-1, keepdims=True)
        xc = v - mu
        rs = lax.rsqrt(jnp.mean(xc * xc, axis=-1, keepdims=True) + EPS)
        vln = (xc * rs * lg_ref[...] + lb_ref[...]).astype(BF16)
        mask = _causal_mask()
        for grp in range(A_GROUPS):
            cols = slice(grp * CHUNK, (grp + 1) * CHUNK)
            wsm = jnp.where(mask, ws_ref[grp], 0.0).astype(BF16)
            bcol = bst_ref[:, grp:grp + 1]
            for ci in range(tr // CHUNK):
                rows = slice(ci * CHUNK, (ci + 1) * CHUNK)
                sv = jnp.dot(wsm, vln[rows, cols], preferred_element_type=F32) + bcol
                gv = g_ref[rows, cols].astype(F32)
                y_ref[rows, cols] = (u_ref[rows, cols].astype(F32) * sv * (gv * jax.nn.sigmoid(gv))).astype(BF16)

    vec = pl.BlockSpec((1, w), lambda i: (0, 0))
    return pl.pallas_call(
        body, grid=(s // tr,),
        in_specs=[pl.BlockSpec((tr, w), lambda i: (i, 0)), pl.BlockSpec((tr, w), lambda i: (i, 1)),
                  pl.BlockSpec((tr, w), lambda i: (i, 2)), vec, vec,
                  pl.BlockSpec((A_GROUPS, CHUNK, CHUNK), lambda i: (0, 0, 0)),
                  pl.BlockSpec((CHUNK, A_GROUPS), lambda i: (0, 0))],
        out_specs=pl.BlockSpec((tr, w), lambda i: (i, 0)),
        out_shape=jax.ShapeDtypeStruct((s, w), BF16), name="gate_fwd", compiler_params=_cparams(),
    )(z, z, z, ln_g, ln_b, ws, bs_t)


def _gate_bwd(z, dy, ln_g, ln_b, ws, ws_t, bs_t, *, tr=256):
    s = z.shape[0]
    tr = _row_tile(s, tr)
    w = A_WIDTH
    nsteps = s // tr

    def body(u_ref, v_ref, g_ref, dy_ref, lg_ref, lb_ref, ws_ref, wst_ref, bst_ref,
             dz_ref, dlg_ref, dlb_ref, dws_ref, dbst_ref, dvln_sc, dsv_sc):
        i = pl.program_id(0)

        @pl.when(i == 0)
        def _():
            dws_ref[...] = jnp.zeros_like(dws_ref)
            dsv_sc[...] = jnp.zeros_like(dsv_sc)

        v = v_ref[...].astype(F32)
        mu = jnp.mean(v, axis=-1, keepdims=True)
        xc = v - mu
        rs = lax.rsqrt(jnp.mean(xc * xc, axis=-1, keepdims=True) + EPS)
        xh = xc * rs
        lg = lg_ref[...]
        vln = (xh * lg + lb_ref[...]).astype(BF16)
        mask = _causal_mask()
        mask_t = _causal_mask(transposed=True)
        for grp in range(A_GROUPS):
            cols = slice(grp * CHUNK, (grp + 1) * CHUNK)
            wsm = jnp.where(mask, ws_ref[grp], 0.0).astype(BF16)
            wsm_t = jnp.where(mask_t, wst_ref[grp], 0.0).astype(BF16)
            bcol = bst_ref[:, grp:grp + 1]
            for ci in range(tr // CHUNK):
                rows = slice(ci * CHUNK, (ci + 1) * CHUNK)
                vb = vln[rows, cols]
                sv = jnp.dot(wsm, vb, preferred_element_type=F32) + bcol
                uv = u_ref[rows, cols].astype(F32)
                silu, dsilu = _silu_parts(g_ref[rows, cols].astype(F32))
                dyv = dy_ref[rows, cols].astype(F32)
                dyu = dyv * uv
                dz_ref[rows, cols] = (dyv * sv * silu).astype(BF16)
                dz_ref[rows, 2 * w + grp * CHUNK:2 * w + (grp + 1) * CHUNK] = (dyu * sv * dsilu).astype(BF16)
                dsv = dyu * silu
                dsvb = dsv.astype(BF16)
                dvln_sc[rows, cols] = jnp.dot(wsm_t, dsvb, preferred_element_type=F32)
                dws_ref[grp] += lax.dot_general(dsvb, vb, NT, preferred_element_type=F32)
                dsv_sc[grp] += dsv
        dvln = dvln_sc[...]
        dlg_t = jnp.sum(dvln * xh, axis=0, keepdims=True)
        dlb_t = jnp.sum(dvln, axis=0, keepdims=True)
        a = dvln * lg
        dv = rs * (a - jnp.mean(a, axis=-1, keepdims=True) - xh * jnp.mean(a * xh, axis=-1, keepdims=True))
        dz_ref[:, w:2 * w] = dv.astype(BF16)

        @pl.when(i == 0)
        def _():
            dlg_ref[...] = dlg_t
            dlb_ref[...] = dlb_t

        @pl.when(i > 0)
        def _():
            dlg_ref[...] += dlg_t
            dlb_ref[...] += dlb_t

        @pl.when(i == nsteps - 1)
        def _():
            for grp in range(A_GROUPS):
                dws_ref[grp] = jnp.where(mask, dws_ref[grp], 0.0)
                dbst_ref[:, grp:grp + 1] = jnp.sum(dsv_sc[grp], axis=-1, keepdims=True)

    vec = pl.BlockSpec((1, w), lambda i: (0, 0))
    wsspec = pl.BlockSpec((A_GROUPS, CHUNK, CHUNK), lambda i: (0, 0, 0))
    bsspec = pl.BlockSpec((CHUNK, A_GROUPS), lambda i: (0, 0))
    return pl.pallas_call(
        body, grid=(nsteps,),
        in_specs=[pl.BlockSpec((tr, w), lambda i: (i, 0)), pl.BlockSpec((tr, w), lambda i: (i, 1)),
                  pl.BlockSpec((tr, w), lambda i: (i, 2)), pl.BlockSpec((tr, w), lambda i: (i, 0)),
                  vec, vec, wsspec, wsspec, bsspec],
        out_specs=[pl.BlockSpec((tr, 3 * w), lambda i: (i, 0)), vec, vec, wsspec, bsspec],
        out_shape=[jax.ShapeDtypeStruct((s, 3 * w), BF16), jax.ShapeDtypeStruct((1, w), F32),
                   jax.ShapeDtypeStruct((1, w), F32), jax.ShapeDtypeStruct((A_GROUPS, CHUNK, CHUNK), F32),
                   jax.ShapeDtypeStruct((CHUNK, A_GROUPS), F32)],
        scratch_shapes=[pltpu.VMEM((tr, w), F32), pltpu.VMEM((A_GROUPS, CHUNK, CHUNK), F32)],
        name="gate_bwd", compiler_params=_cparams(),
    )(z, z, z, dy, ln_g, ln_b, ws, ws_t, bs_t)


def _rope_tables(s):
    inv_freq = ROPE_THETA ** (-jnp.arange(0, HEAD_DIM, 2, dtype=F32) / HEAD_DIM)
    ang = jnp.arange(s, dtype=F32)[:, None] * inv_freq[None, :]
    cos, sin = jnp.cos(ang), jnp.sin(ang)
    cos2 = jnp.concatenate([cos, cos], axis=-1)
    sin2 = jnp.concatenate([-sin, sin], axis=-1)
    return jnp.tile(cos2, (1, 2)), jnp.tile(sin2, (1, 2))


def _swap_halves(x):
    n = x.shape[-1]
    lane = lax.broadcasted_iota(jnp.int32, x.shape, x.ndim - 1)
    first = (lane % HEAD_DIM) < (HEAD_DIM // 2)
    return jnp.where(first, pltpu.roll(x, n - HEAD_DIM // 2, x.ndim - 1), pltpu.roll(x, HEAD_DIM // 2, x.ndim - 1))


def _tile_lanes(t, width):
    return jnp.tile(t, (1, width // t.shape[-1]))


def _kv_rope(kv, b_kv, cos, sin, *, tr=512):
    s = kv.shape[0]
    tr = _row_tile(s, tr)

    def body(kv_ref, b_ref, c_ref, s_ref, k_ref, v_ref):
        x = kv_ref[...] + b_ref[...]
        k = x[:, :KV_WIDTH]
        k_ref[...] = (k * c_ref[...] + _swap_halves(k) * s_ref[...]).astype(BF16)
        v_ref[...] = x[:, KV_WIDTH:].astype(BF16)

    tab = pl.BlockSpec((tr, KV_WIDTH), lambda i: (i, 0))
    return pl.pallas_call(
        body, grid=(s // tr,),
        in_specs=[pl.BlockSpec((tr, 2 * KV_WIDTH), lambda i: (i, 0)),
                  pl.BlockSpec((1, 2 * KV_WIDTH), lambda i: (0, 0)), tab, tab],
        out_specs=[tab, tab], out_shape=[jax.ShapeDtypeStruct((s, KV_WIDTH), BF16)] * 2,
        name="kv_rope", compiler_params=_cparams(),
    )(kv, b_kv, cos, sin)


def _kv_rope_bwd(dk_rot, dv, cos, sin, *, tr=512):
    s = dk_rot.shape[0]
    tr = _row_tile(s, tr)

    def body(dk_ref, dv_ref, c_ref, s_ref, dkv_ref, db_ref):
        i = pl.program_id(0)
        d = dk_ref[...]
        dk = d * c_ref[...] + _swap_halves(d * s_ref[...])
        dvv = dv_ref[...]
        dkv_ref[:, :KV_WIDTH] = dk.astype(BF16)
        dkv_ref[:, KV_WIDTH:] = dvv.astype(BF16)
        sk = jnp.sum(dk, axis=0, keepdims=True)
        sv = jnp.sum(dvv, axis=0, keepdims=True)

        @pl.when(i == 0)
        def _():
            db_ref[:, :KV_WIDTH] = sk
            db_ref[:, KV_WIDTH:] = sv

        @pl.when(i > 0)
        def _():
            db_ref[:, :KV_WIDTH] += sk
            db_ref[:, KV_WIDTH:] += sv

    tab = pl.BlockSpec((tr, KV_WIDTH), lambda i: (i, 0))
    return pl.pallas_call(
        body, grid=(s // tr,), in_specs=[tab, tab, tab, tab],
        out_specs=[pl.BlockSpec((tr, 2 * KV_WIDTH), lambda i: (i, 0)),
                   pl.BlockSpec((1, 2 * KV_WIDTH), lambda i: (0, 0))],
        out_shape=[jax.ShapeDtypeStruct((s, 2 * KV_WIDTH), BF16), jax.ShapeDtypeStruct((1, 2 * KV_WIDTH), F32)],
        name="kv_rope_bwd", compiler_params=_cparams(),
    )(dk_rot, dv, cos, sin)


def _window_mask(i):
    q = lax.broadcasted_iota(jnp.int32, (CHUNK, 2 * CHUNK), 0)
    k = lax.broadcasted_iota(jnp.int32, (CHUNK, 2 * CHUNK), 1)
    first_valid = jnp.where(i > 0, 0, CHUNK)
    prev = (k < CHUNK) & (k > q) & (k >= first_valid)
    cur = (k >= CHUNK) & (k - CHUNK <= q)
    return prev | cur


def _attn_specs():
    qspec = pl.BlockSpec((CHUNK, B_WIDTH), lambda i: (i, 0))
    gspec = pl.BlockSpec((CHUNK, B_WIDTH), lambda i: (i, 1))
    prev = pl.BlockSpec((CHUNK, KV_WIDTH), lambda i: (jnp.maximum(i - 1, 0), 0))
    cur = pl.BlockSpec((CHUNK, KV_WIDTH), lambda i: (i, 0))
    bq = pl.BlockSpec((1, B_WIDTH), lambda i: (0, 0))
    sinks = pl.BlockSpec(memory_space=pltpu.SMEM)
    return qspec, gspec, prev, cur, bq, sinks


def _rot_q(zq_ref, bq_ref, c_ref, s_ref):
    q = zq_ref[...].astype(F32) + bq_ref[...]
    cos = _tile_lanes(c_ref[...], B_WIDTH)
    sin = _tile_lanes(s_ref[...], B_WIDTH)
    return (q * cos + _swap_halves(q) * sin).astype(BF16), cos, sin


def _head_probs(qh, kh, mask, sink):
    sc = lax.dot_general(qh, kh, NT, preferred_element_type=F32) * (HEAD_DIM ** -0.5)
    sc = jnp.where(mask, sc, NEG_BIG)
    m = jnp.maximum(jnp.max(sc, axis=-1, keepdims=True), sink)
    p = jnp.exp(sc - m)
    esink = jnp.exp(sink - m)
    inv = 1.0 / (jnp.sum(p, axis=-1, keepdims=True) + esink)
    return p * inv, esink * inv


def _attn_fwd(zb, kr, vv, cos, sin, b_bq, sinks):
    s = zb.shape[0]

    def body(zq_ref, zg_ref, kp_ref, kc_ref, vp_ref, vc_ref, c_ref, s_ref, bq_ref, sk_ref, y_ref, o_sc):
        i = pl.program_id(0)
        qr, _, _ = _rot_q(zq_ref, bq_ref, c_ref, s_ref)
        kcat = jnp.concatenate([kp_ref[...], kc_ref[...]], axis=0)
        vcat = jnp.concatenate([vp_ref[...], vc_ref[...]], axis=0)
        mask = _window_mask(i)
        for h in range(N_Q_HEADS):
            kvh = h // Q_PER_KV
            hc = slice(h * HEAD_DIM, (h + 1) * HEAD_DIM)
            kc = slice(kvh * HEAD_DIM, (kvh + 1) * HEAD_DIM)
            p, _ = _head_probs(qr[:, hc], kcat[:, kc], mask, sk_ref[0, h])
            o_sc[:, hc] = jnp.dot(p.astype(BF16), vcat[:, kc], preferred_element_type=F32)
        gv = zg_ref[...].astype(F32)
        y_ref[...] = (o_sc[...] * (gv * jax.nn.sigmoid(gv))).astype(BF16)

    qspec, gspec, prev, cur, bq, sk = _attn_specs()
    return pl.pallas_call(
        body, grid=(s // CHUNK,),
        in_specs=[qspec, gspec, prev, cur, prev, cur, cur, cur, bq, sk],
        out_specs=qspec, out_shape=jax.ShapeDtypeStruct((s, B_WIDTH), BF16),
        scratch_shapes=[pltpu.VMEM((CHUNK, B_WIDTH), F32)],
        name="attn_fwd", compiler_params=_cparams(),
    )(zb, zb, kr, kr, vv, vv, cos, sin, b_bq, sinks)


def _attn_bwd(zb, dyb, kr, vv, cos, sin, b_bq, sinks):
    s = zb.shape[0]

    def body(zq_ref, zg_ref, dy_ref, kp_ref, kc_ref, vp_ref, vc_ref, c_ref, s_ref, bq_ref, sk_ref,
             dz_ref, dk_ref, dv_ref, dbq_ref, dsk_ref, o_sc, dq_sc):
        i = pl.program_id(0)

        @pl.when(i == 0)
        def _():
            dk_ref[...] = jnp.zeros_like(dk_ref)
            dv_ref[...] = jnp.zeros_like(dv_ref)
            dbq_ref[...] = jnp.zeros_like(dbq_ref)
            dsk_ref[...] = jnp.zeros_like(dsk_ref)

        qr, cos, sin = _rot_q(zq_ref, bq_ref, c_ref, s_ref)
        kcat = jnp.concatenate([kp_ref[...], kc_ref[...]], axis=0)
        vcat = jnp.concatenate([vp_ref[...], vc_ref[...]], axis=0)
        mask = _window_mask(i)
        gv = zg_ref[...].astype(F32)
        silu, dsilu = _silu_parts(gv)
        dyv = dy_ref[...].astype(F32)
        do_all = (dyv * silu).astype(BF16)
        lane = lax.broadcasted_iota(jnp.int32, (1, 128), 1)
        dsk_row = jnp.zeros((1, 128), F32)
        for kvh in range(N_KV_HEADS):
            kc = slice(kvh * HEAD_DIM, (kvh + 1) * HEAD_DIM)
            kh, vh = kcat[:, kc], vcat[:, kc]
            dk_acc = jnp.zeros((2 * CHUNK, HEAD_DIM), F32)
            dv_acc = jnp.zeros((2 * CHUNK, HEAD_DIM), F32)
            for r in range(Q_PER_KV):
                h = kvh * Q_PER_KV + r
                hc = slice(h * HEAD_DIM, (h + 1) * HEAD_DIM)
                qh = qr[:, hc]
                p, psink = _head_probs(qh, kh, mask, sk_ref[0, h])
                pb = p.astype(BF16)
                o_sc[:, hc] = jnp.dot(pb, vh, preferred_element_type=F32)
                doh = do_all[:, hc]
                dp = lax.dot_general(doh, vh, NT, preferred_element_type=F32)
                delta = jnp.sum(p * dp, axis=-1, keepdims=True)
                ds = (p * (dp - delta) * (HEAD_DIM ** -0.5)).astype(BF16)
                dq_sc[:, hc] = jnp.dot(ds, kh, preferred_element_type=F32)
                dk_acc = dk_acc + lax.dot_general(ds, qh, TN, preferred_element_type=F32)
                dv_acc = dv_acc + lax.dot_general(pb, doh, TN, preferred_element_type=F32)
                dsink = -jnp.sum(psink * delta, axis=0, keepdims=True)
                dsk_row = dsk_row + jnp.where(lane == h, dsink, 0.0)
            cur_rows = pl.ds(pl.multiple_of(i * CHUNK, CHUNK), CHUNK)
            dk_ref[cur_rows, kc] += dk_acc[CHUNK:]
            dv_ref[cur_rows, kc] += dv_acc[CHUNK:]

            @pl.when(i > 0)
            def _(kc=kc, dk_acc=dk_acc, dv_acc=dv_acc):
                prev_rows = pl.ds(pl.multiple_of((i - 1) * CHUNK, CHUNK), CHUNK)
                dk_ref[prev_rows, kc] += dk_acc[:CHUNK]
                dv_ref[prev_rows, kc] += dv_acc[:CHUNK]

        dsk_ref[0:1, :] += dsk_row
        dqr = dq_sc[...]
        dq = dqr * cos + _swap_halves(dqr * sin)
        dbq_ref[...] += jnp.sum(dq, axis=0, keepdims=True)
        dz_ref[:, :B_WIDTH] = dq.astype(BF16)
        dz_ref[:, B_WIDTH:] = (dyv * o_sc[...] * dsilu).astype(BF16)

    qspec, gspec, prev, cur, bq, sk = _attn_specs()
    full = pl.BlockSpec((s, KV_WIDTH), lambda i: (0, 0))
    return pl.pallas_call(
        body, grid=(s // CHUNK,),
        in_specs=[qspec, gspec, qspec, prev, cur, prev, cur, cur, cur, bq, sk],
        out_specs=[pl.BlockSpec((CHUNK, 2 * B_WIDTH), lambda i: (i, 0)), full, full, bq,
                   pl.BlockSpec((8, 128), lambda i: (0, 0))],
        out_shape=[jax.ShapeDtypeStruct((s, 2 * B_WIDTH), BF16), jax.ShapeDtypeStruct((s, KV_WIDTH), F32),
                   jax.ShapeDtypeStruct((s, KV_WIDTH), F32), jax.ShapeDtypeStruct((1, B_WIDTH), F32),
                   jax.ShapeDtypeStruct((8, 128), F32)],
        scratch_shapes=[pltpu.VMEM((CHUNK, B_WIDTH), F32), pltpu.VMEM((CHUNK, B_WIDTH), F32)],
        name="attn_bwd", compiler_params=_cparams(),
    )(zb, zb, dyb, kr, kr, vv, vv, cos, sin, b_bq, sinks)


def _local_step(x, tgt, w):
    s = x.shape[0]
    cos, sin = _rope_tables(s)
    ws = w["a_ws"]
    ws_t = jnp.swapaxes(ws, 1, 2)
    bs_t = w["a_bs"].T

    (n_a,) = _rms_fwd(x, [w["a_norm_g"]], name="rms_a")
    z = _mm_nn(n_a, w["a_w_in"], name="mm_a_in", tn=768, tm=1024, out_dtype=BF16)
    y = _gate_fwd(z, w["a_ln_g"], w["a_ln_b"], ws, bs_t)
    h1 = _mm_nn(y, w["a_w_out"], name="mm_a_out", tn=D_MODEL, residual=x)
    n_kv, n_b = _rms_fwd(h1, [w["kv_norm_g"], w["b_norm_g"]], name="rms_b")
    kv = _mm_nn(n_kv, w["w_kv"], name="mm_kv", tn=2 * KV_WIDTH)
    kr, vv = _kv_rope(kv, w["b_kv"], cos, sin)
    zb = _mm_nn(n_b, w["b_w_in"], name="mm_b_in", tn=512, tm=1024, out_dtype=BF16)
    yb = _attn_fwd(zb, kr, vv, cos, sin, w["b_bq"], w["b_sinks"])
    h2 = _mm_nn(yb, w["b_w_out"], name="mm_b_out", tn=D_MODEL, residual=h1)
    loss_blk, dh2, dh2b, d_final_g = _loss_head(h2, tgt, w["final_norm_g"])

    d_b_w_out = _mm_tn(yb, dh2b, name="mm_d_b_w_out", tm=B_WIDTH, tn=D_MODEL)
    dyb = _mm_nt(dh2b, w["b_w_out"], name="mm_dyb", tk=D_MODEL, out_dtype=BF16)
    dzb, dk_rot, dv, d_bq, d_sinks = _attn_bwd(zb, dyb, kr, vv, cos, sin, w["b_bq"], w["b_sinks"])
    dkv, d_b_kv = _kv_rope_bwd(dk_rot, dv, cos, sin)
    d_b_w_in = _mm_tn(n_b, dzb, name="mm_d_b_w_in", tm=D_MODEL, tn=512, shards=N_CHIPS)
    dn_b = _mm_nt(dzb, w["b_w_in"], name="mm_dn_b", tk=512)
    d_w_kv = _mm_tn(n_kv, dkv, name="mm_d_w_kv", tm=D_MODEL, tn=2 * KV_WIDTH)
    dn_kv = _mm_nt(dkv, w["w_kv"], name="mm_dn_kv", tk=2 * KV_WIDTH)
    dh1, dh1b, (d_kv_g, d_b_g) = _rms_bwd(h1, [dn_kv, dn_b], [w["kv_norm_g"], w["b_norm_g"]], dh2, name="rms_b_bwd")

    d_a_w_out = _mm_tn(y, dh1b, name="mm_d_a_w_out", tm=1024, tn=D_MODEL)
    dy = _mm_nt(dh1b, w["a_w_out"], name="mm_dy", tk=D_MODEL, tn=1024, out_dtype=BF16)
    dz, d_ln_g, d_ln_b, d_ws, d_bs_t = _gate_bwd(z, dy, w["a_ln_g"], w["a_ln_b"], ws, ws_t, bs_t)
    d_a_w_in = _mm_tn(n_a, dz, name="mm_d_a_w_in", tm=D_MODEL, tn=1536, shards=N_CHIPS)
    dn_a = _mm_nt(dz, w["a_w_in"], name="mm_dn_a", tk=768, tm=1024)
    dx, _, (d_a_g,) = _rms_bwd(x, [dn_a], [w["a_norm_g"]], dh1, name="rms_a_bwd")

    big = {
        "a_w_in": d_a_w_in,
        "a_w_out": d_a_w_out.reshape(N_CHIPS, A_WIDTH // N_CHIPS, D_MODEL),
        "w_kv": d_w_kv.reshape(N_CHIPS, D_MODEL // N_CHIPS, 2 * KV_WIDTH),
        "b_w_in": d_b_w_in,
        "b_w_out": d_b_w_out.reshape(N_CHIPS, B_WIDTH // N_CHIPS, D_MODEL),
    }
    small = {
        "a_ws": d_ws, "a_bs": d_bs_t.T, "a_norm_g": d_a_g, "a_ln_g": d_ln_g, "a_ln_b": d_ln_b,
        "kv_norm_g": d_kv_g, "b_kv": d_b_kv, "b_norm_g": d_b_g, "b_bq": d_bq,
        "b_sinks": d_sinks[0:1, :N_Q_HEADS], "final_norm_g": d_final_g,
    }
    return loss_blk, dx, big, small


def _place():
    x, y, c = lax.axis_index("x"), lax.axis_index("y"), lax.axis_index("c")
    return x, y, c, [(1 - x, y), (x, 1 - y), (1 - x, 1 - y)]


HBM = pl.BlockSpec(memory_space=pl.ANY)


def _gather_shards(arrs):
    n = len(arrs)

    def body(*refs):
        ins, outs = refs[:n], refs[n:2 * n]
        send_ici, recv_ici, send_d2d, recv_d2d, local_sem = refs[2 * n:]
        x, y, c, chips = _place()
        me = 2 * x + y
        sibling = (x, y, 1 - c)
        local = [pltpu.make_async_copy(ins[a], outs[a].at[me], local_sem.at[a]) for a in range(n)]
        for cp in local:
            cp.start()

        def rows(a, half):
            hr = arrs[a].shape[0] // 2
            return pl.ds(half * hr, hr)

        def ici(a, j, src_chip, to):
            return pltpu.make_async_remote_copy(
                src_ref=ins[a].at[rows(a, c)], dst_ref=outs[a].at[src_chip, rows(a, c)],
                send_sem=send_ici.at[a, j], recv_sem=recv_ici.at[a, j], device_id=to, device_id_type=MESH)

        def d2d(a, j, chip, half):
            blk = outs[a].at[chip, rows(a, half)]
            return pltpu.make_async_remote_copy(
                src_ref=blk, dst_ref=blk, send_sem=send_d2d.at[a, j], recv_sem=recv_d2d.at[a, j],
                device_id=sibling, device_id_type=MESH)

        sends = [ici(a, j, me, (*chip, c)) for a in range(n) for j, chip in enumerate(chips)]
        for cp in sends:
            cp.start()
        passes = []
        for a in range(n):
            for j, (px, py) in enumerate(chips):
                ici(a, j, 2 * px + py, (px, py, c)).wait_recv()
                cp = d2d(a, j, 2 * px + py, c)
                cp.start()
                passes.append(cp)
        for a in range(n):
            for j, (px, py) in enumerate(chips):
                d2d(a, j, 2 * px + py, 1 - c).wait_recv()
        for cp in sends + passes:
            cp.wait_send()
        for cp in local:
            cp.wait()

    return pl.pallas_call(
        body, in_specs=[HBM] * n, out_specs=[HBM] * n,
        out_shape=[jax.ShapeDtypeStruct((N_CHIPS,) + a.shape, a.dtype) for a in arrs],
        scratch_shapes=[pltpu.SemaphoreType.DMA((n, 3)), pltpu.SemaphoreType.DMA((n, 3)),
                        pltpu.SemaphoreType.DMA((n, 3)), pltpu.SemaphoreType.DMA((n, 3)),
                        pltpu.SemaphoreType.DMA((n,))],
        name="gather_weights",
    )(*arrs)


def _exchange_halves(grads):
    n = len(grads)

    def body(*refs):
        ins, outs = refs[:n], refs[n:2 * n]
        send_sem, recv_sem = refs[2 * n:]
        x, y, c, _ = _place()
        cps = []
        for a in range(n):
            hr = grads[a].shape[1] // 2
            cp = pltpu.make_async_remote_copy(
                src_ref=ins[a].at[:, pl.ds((1 - c) * hr, hr), :], dst_ref=outs[a],
                send_sem=send_sem.at[a], recv_sem=recv_sem.at[a], device_id=(x, y, 1 - c), device_id_type=MESH)
            cp.start()
            cps.append(cp)
        for cp in cps:
            cp.wait()

    return pl.pallas_call(
        body, in_specs=[HBM] * n, out_specs=[HBM] * n,
        out_shape=[jax.ShapeDtypeStruct((g.shape[0], g.shape[1] // 2, g.shape[2]), g.dtype) for g in grads],
        scratch_shapes=[pltpu.SemaphoreType.DMA((n,)), pltpu.SemaphoreType.DMA((n,))],
        name="exchange_halves",
    )(*grads)


def _relations():
    return [(r >> 2 & 1, r >> 1 & 1, r & 1) for r in range(1, 8)]


def _scatter_partials(chip_sums, small):
    n = len(chip_sums)

    def body(*refs):
        ins, small_in = refs[:n], refs[n]
        outs, small_out = refs[n + 1:2 * n + 1], refs[2 * n + 1]
        send_sem, recv_sem, ssend_sem, srecv_sem = refs[2 * n + 2:]
        x, y, c, chips = _place()
        cps = []
        for a in range(n):
            for j, (px, py) in enumerate(chips):
                cp = pltpu.make_async_remote_copy(
                    src_ref=ins[a].at[2 * px + py], dst_ref=outs[a].at[j],
                    send_sem=send_sem.at[a, j], recv_sem=recv_sem.at[a, j], device_id=(px, py, c), device_id_type=MESH)
                cp.start()
                cps.append(cp)
        for r, (fx, fy, fc) in enumerate(_relations(), start=1):
            px, py, pc = x ^ fx, y ^ fy, c ^ fc
            cp = pltpu.make_async_remote_copy(
                src_ref=small_in.at[4 * px + 2 * py + pc], dst_ref=small_out.at[r],
                send_sem=ssend_sem.at[r - 1], recv_sem=srecv_sem.at[r - 1], device_id=(px, py, pc), device_id_type=MESH)
            cp.start()
            cps.append(cp)
        for cp in cps:
            cp.wait()

    return pl.pallas_call(
        body, in_specs=[HBM] * (n + 1), out_specs=[HBM] * (n + 1),
        out_shape=[jax.ShapeDtypeStruct((3,) + t.shape[1:], t.dtype) for t in chip_sums]
        + [jax.ShapeDtypeStruct(small.shape, small.dtype)],
        scratch_shapes=[pltpu.SemaphoreType.DMA((n, 3)), pltpu.SemaphoreType.DMA((n, 3)),
                        pltpu.SemaphoreType.DMA((7,)), pltpu.SemaphoreType.DMA((7,))],
        name="scatter_partials",
    )(*chip_sums, small)


def _share_reduced(halves, small):
    n = len(halves)

    def body(*refs):
        ins, small_in = refs[:n], refs[n]
        outs, small_out = refs[n + 1:2 * n + 1], refs[2 * n + 1]
        send_sem, recv_sem, ssend_sem, srecv_sem = refs[2 * n + 2:]
        x, y, c, _ = _place()
        me = 4 * x + 2 * y + c
        cps = []
        for a in range(n):
            hr = halves[a].shape[0] // 2
            cp = pltpu.make_async_remote_copy(
                src_ref=ins[a].at[pl.ds(c * hr, hr)], dst_ref=outs[a].at[pl.ds(c * hr, hr)],
                send_sem=send_sem.at[a], recv_sem=recv_sem.at[a], device_id=(x, y, 1 - c), device_id_type=MESH)
            cp.start()
            cps.append(cp)
        for r, (fx, fy, fc) in enumerate(_relations(), start=1):
            cp = pltpu.make_async_remote_copy(
                src_ref=small_in.at[me], dst_ref=small_out.at[me],
                send_sem=ssend_sem.at[r - 1], recv_sem=srecv_sem.at[r - 1],
                device_id=(x ^ fx, y ^ fy, c ^ fc), device_id_type=MESH)
            cp.start()
            cps.append(cp)
        for a in range(n):
            hr = halves[a].shape[0] // 2
            other = outs[a].at[pl.ds((1 - c) * hr, hr)]
            pltpu.make_async_remote_copy(
                src_ref=other, dst_ref=other, send_sem=send_sem.at[a], recv_sem=recv_sem.at[a],
                device_id=(x, y, 1 - c), device_id_type=MESH).wait_recv()
        for r, (fx, fy, fc) in enumerate(_relations(), start=1):
            theirs = small_out.at[4 * (x ^ fx) + 2 * (y ^ fy) + (c ^ fc)]
            pltpu.make_async_remote_copy(
                src_ref=theirs, dst_ref=theirs, send_sem=ssend_sem.at[r - 1], recv_sem=srecv_sem.at[r - 1],
                device_id=(x ^ fx, y ^ fy, c ^ fc), device_id_type=MESH).wait_recv()
        for cp in cps:
            cp.wait_send()

    return pl.pallas_call(
        body, in_specs=[HBM] * (n + 1), out_specs=[HBM] * (n + 1),
        out_shape=[jax.ShapeDtypeStruct(h.shape, h.dtype) for h in halves]
        + [jax.ShapeDtypeStruct(small.shape, small.dtype)],
        input_output_aliases={i: i for i in range(n + 1)},
        scratch_shapes=[pltpu.SemaphoreType.DMA((n,)), pltpu.SemaphoreType.DMA((n,)),
                        pltpu.SemaphoreType.DMA((7,)), pltpu.SemaphoreType.DMA((7,))],
        name="share_reduced",
    )(*halves, small)


def _col_tile(cols):
    return cols if cols <= 2048 else 512


def _add_sibling(grad, recv, core, *, name):
    k, r, c = grad.shape
    hr = r // 2
    tr = min(hr, 256)
    tc = _col_tile(c)
    nrb = hr // tr

    def body(core_ref, g_ref, r_ref, o_ref):
        o_ref[...] = (g_ref[...] + r_ref[...]).astype(BF16)

    return pl.pallas_call(
        body,
        grid_spec=pltpu.PrefetchScalarGridSpec(
            num_scalar_prefetch=1, grid=(k, nrb, c // tc),
            in_specs=[pl.BlockSpec((None, tr, tc), lambda kk, i, j, core: (kk, core[0] * nrb + i, j)),
                      pl.BlockSpec((None, tr, tc), lambda kk, i, j, core: (kk, i, j))],
            out_specs=pl.BlockSpec((None, tr, tc), lambda kk, i, j, core: (kk, i, j))),
        out_shape=jax.ShapeDtypeStruct((k, hr, c), BF16), name=name, compiler_params=_cparams(),
    )(core, grad, recv)


def _sum_chips(grad, from_sibling, recv, place, *, name):
    _, hr, c = from_sibling.shape
    tr = min(hr, 256)
    tc = _col_tile(c)
    nrb = hr // tr

    def body(place_ref, g_ref, s_ref, r0_ref, r1_ref, r2_ref, o_ref):
        own = g_ref[...] + s_ref[...]
        o_ref[...] = ((own + r0_ref[...].astype(F32)) + r1_ref[...].astype(F32)) + r2_ref[...].astype(F32)

    def rspec(j):
        return pl.BlockSpec((None, tr, tc), lambda i, jj, place: (j, i, jj))

    return pl.pallas_call(
        body,
        grid_spec=pltpu.PrefetchScalarGridSpec(
            num_scalar_prefetch=1, grid=(nrb, c // tc),
            in_specs=[pl.BlockSpec((None, tr, tc), lambda i, jj, place: (place[0], place[1] * nrb + i, jj)),
                      pl.BlockSpec((None, tr, tc), lambda i, jj, place: (place[0], i, jj)),
                      rspec(0), rspec(1), rspec(2)],
            out_specs=pl.BlockSpec((tr, tc), lambda i, jj, place: (place[1] * nrb + i, jj))),
        out_shape=jax.ShapeDtypeStruct((2 * hr, c), F32), name=name, compiler_params=_cparams(),
    )(place, grad, from_sibling, recv, recv, recv)


def _sum_small(small, recv, place):
    _, sr, _ = small.shape

    def body(place_ref, own_ref, r_ref, o_ref):
        acc = own_ref[...]
        for r in range(1, 8):
            acc = acc + r_ref[r]
        o_ref[...] = acc

    return pl.pallas_call(
        body,
        grid_spec=pltpu.PrefetchScalarGridSpec(
            num_scalar_prefetch=1, grid=(1,),
            in_specs=[pl.BlockSpec((None, sr, 128), lambda i, place: (place[2], 0, 0)),
                      pl.BlockSpec((8, sr, 128), lambda i, place: (0, 0, 0))],
            out_specs=pl.BlockSpec((None, sr, 128), lambda i, place: (place[2], 0, 0))),
        out_shape=jax.ShapeDtypeStruct(small.shape, F32), name="sum_small", compiler_params=_cparams(),
    )(place, small, recv)


def _adamw(w, g, m, v, *, name):
    r, c = w.shape
    tr = 256 if r % 256 == 0 else r
    tc = _col_tile(c)
    bc1 = 1.0 - ADAM_B1 ** ADAM_STEP
    bc2 = 1.0 - ADAM_B2 ** ADAM_STEP

    def body(w_ref, g_ref, m_ref, v_ref, d_ref, nm_ref, nv_ref):
        gv = g_ref[...]
        nm = ADAM_B1 * m_ref[...] + (1.0 - ADAM_B1) * gv
        nv = ADAM_B2 * v_ref[...] + (1.0 - ADAM_B2) * (gv * gv)
        d_ref[...] = -ADAM_LR * ((nm / bc1) / (jnp.sqrt(nv / bc2) + ADAM_EPS) + ADAM_WD * w_ref[...])
        nm_ref[...] = nm
        nv_ref[...] = nv

    spec = pl.BlockSpec((tr, tc), lambda i, j: (i, j))
    return pl.pallas_call(
        body, grid=(r // tr, c // tc), in_specs=[spec] * 4, out_specs=[spec] * 3,
        out_shape=[jax.ShapeDtypeStruct((r, c), F32)] * 3, name=name, compiler_params=_cparams(),
    )(w, g, m, v)


SMALL_ORDER = ["a_ws", "a_bs", "a_norm_g", "a_ln_g", "a_ln_b", "kv_norm_g", "b_kv", "b_norm_g", "b_bq",
               "b_sinks", "final_norm_g"]
SHARDED_SMALL = {"a_norm_g", "a_ln_g", "a_ln_b"}


PACK_TILE = 8 * 128


def _rows128(a):
    flat = a.reshape(-1)
    return jnp.pad(flat, (0, (-flat.shape[0]) % PACK_TILE)).reshape(-1, 128)


def _pack_rows(parts, multiple):
    rows = [_rows128(p) for p in parts]
    total = sum(r.shape[0] for r in rows)
    pad = (-total) % multiple
    if pad:
        rows.append(jnp.zeros((pad, 128), rows[0].dtype))
    return jnp.concatenate(rows, axis=0)


def _unpack_rows(packed, shapes):
    out, row = [], 0
    for shp in shapes:
        size = math.prod(shp)
        nrow = -(-size // PACK_TILE) * 8
        out.append(packed[row:row + nrow].reshape(-1)[:size].reshape(shp))
        row += nrow
    return out


WEIGHTS = ["a_norm_g", "a_w_in", "a_ln_g", "a_ln_b", "a_ws", "a_bs", "a_w_out", "kv_norm_g", "w_kv", "b_kv",
           "b_norm_g", "b_w_in", "b_bq", "b_sinks", "b_w_out", "final_norm_g"]
BIG = ["a_w_in", "a_w_out", "w_kv", "b_w_in", "b_w_out"]


def _step(x, loss_target, p, m, v):
    xi, yi, ci = lax.axis_index("x"), lax.axis_index("y"), lax.axis_index("c")
    chip = 2 * xi + yi
    device = 4 * xi + 2 * yi + ci
    core = jnp.reshape(ci, (1,)).astype(jnp.int32)
    place = jnp.stack([chip, ci, device]).astype(jnp.int32)

    shard2d = {n: p[n].reshape(p[n].shape[-2:]) for n in BIG}
    vec_shapes = [p[n].shape for n in ("a_norm_g", "a_ln_g", "a_ln_b")]
    vec_pack = _pack_rows([p["a_norm_g"], p["a_ln_g"], p["a_ln_b"]], 16)
    gathered = _gather_shards([shard2d[n].astype(BF16) for n in BIG] + [vec_pack])
    full = dict(zip(BIG, gathered[:len(BIG)]))
    vecs = [_unpack_rows(gathered[-1][k], vec_shapes) for k in range(N_CHIPS)]
    w = {
        "a_w_in": full["a_w_in"], "b_w_in": full["b_w_in"],
        "a_w_out": full["a_w_out"].reshape(A_WIDTH, D_MODEL),
        "w_kv": full["w_kv"].reshape(D_MODEL, 2 * KV_WIDTH),
        "b_w_out": full["b_w_out"].reshape(B_WIDTH, D_MODEL),
        "a_norm_g": jnp.concatenate([vk[0] for vk in vecs], axis=-1),
        "a_ln_g": jnp.concatenate([vk[1] for vk in vecs], axis=-1),
        "a_ln_b": jnp.concatenate([vk[2] for vk in vecs], axis=-1),
        "a_ws": p["a_ws"][0], "a_bs": p["a_bs"][0],
        "kv_norm_g": p["kv_norm_g"].reshape(1, -1), "b_kv": p["b_kv"].reshape(1, -1),
        "b_norm_g": p["b_norm_g"], "b_bq": p["b_bq"], "b_sinks": p["b_sinks"],
        "final_norm_g": p["final_norm_g"].reshape(1, -1),
    }

    loss_blk, dx, big, small = _local_step(x[0], loss_target[0], w)

    small_shapes = [small[n].shape for n in SMALL_ORDER] + [(1, 1)]
    small_pack = _pack_rows([small[n] for n in SMALL_ORDER] + [loss_blk[0:1, 0:1]], 64)
    seg = small_pack.shape[0] // 8
    small_pack = small_pack.reshape(8, seg, 128)
    partial = [big[n] for n in BIG]
    from_sibling = _exchange_halves(partial)
    chip_sums = [_add_sibling(g, r, core, name="add_sibling_" + n) for g, r, n in zip(partial, from_sibling, BIG)]
    arrived = _scatter_partials(chip_sums, small_pack)
    halves = [_sum_chips(g, fs, r, place, name="sum_chips_" + n)
              for g, fs, r, n in zip(partial, from_sibling, arrived[:-1], BIG)]
    small_mine = _sum_small(small_pack, arrived[-1], place)
    shared = _share_reduced(halves, small_mine)
    grad_big = dict(zip(BIG, shared[:-1]))
    small_full = _unpack_rows(shared[-1].reshape(8 * seg, 128), small_shapes)
    loss = small_full[-1].reshape(())
    grads = {}
    for n, gfull in zip(SMALL_ORDER, small_full):
        if n in SHARDED_SMALL:
            width = p[n].shape[-1]
            gfull = lax.dynamic_slice_in_dim(gfull, chip * width, width, axis=-1)
        grads[n] = gfull.reshape(p[n].shape)
    for n in BIG:
        grads[n] = grad_big[n].reshape(p[n].shape)

    delta, new_m, new_v = {}, {}, {}
    for n in BIG:
        d, nm, nv = _adamw(shard2d[n], grad_big[n], m[n].reshape(shard2d[n].shape), v[n].reshape(shard2d[n].shape),
                           name="adamw_" + n)
        delta[n], new_m[n], new_v[n] = d.reshape(p[n].shape), nm.reshape(p[n].shape), nv.reshape(p[n].shape)
    shapes = [p[n].shape for n in SMALL_ORDER]
    packs = [_pack_rows([src[n] for n in SMALL_ORDER], 8) for src in (p, grads, m, v)]
    outs = _adamw(*packs, name="adamw_small")
    for res, packed in zip((delta, new_m, new_v), outs):
        for n, val in zip(SMALL_ORDER, _unpack_rows(packed, shapes)):
            res[n] = val

    return (loss, dx[None], *[grads[n] for n in WEIGHTS], *[delta[n] for n in WEIGHTS],
            *[new_m[n] for n in WEIGHTS], *[new_v[n] for n in WEIGHTS])


def kernel(x, a_norm_g, a_w_in, a_ln_g, a_ln_b, a_ws, a_bs, a_w_out, kv_norm_g, w_kv, b_kv, b_norm_g, b_w_in, b_bq, b_sinks, b_w_out, final_norm_g, loss_target, m_a_norm_g, m_a_w_in, m_a_ln_g, m_a_ln_b, m_a_ws, m_a_bs, m_a_w_out, m_kv_norm_g, m_w_kv, m_b_kv, m_b_norm_g, m_b_w_in, m_b_bq, m_b_sinks, m_b_w_out, m_final_norm_g, v_a_norm_g, v_a_w_in, v_a_ln_g, v_a_ln_b, v_a_ws, v_a_bs, v_a_w_out, v_kv_norm_g, v_w_kv, v_b_kv, v_b_norm_g, v_b_w_in, v_b_bq, v_b_sinks, v_b_w_out, v_final_norm_g):
    p = dict(a_norm_g=a_norm_g, a_w_in=a_w_in, a_ln_g=a_ln_g, a_ln_b=a_ln_b, a_ws=a_ws, a_bs=a_bs, a_w_out=a_w_out,
             kv_norm_g=kv_norm_g, w_kv=w_kv, b_kv=b_kv, b_norm_g=b_norm_g, b_w_in=b_w_in, b_bq=b_bq, b_sinks=b_sinks,
             b_w_out=b_w_out, final_norm_g=final_norm_g)
    m = dict(a_norm_g=m_a_norm_g, a_w_in=m_a_w_in, a_ln_g=m_a_ln_g, a_ln_b=m_a_ln_b, a_ws=m_a_ws, a_bs=m_a_bs,
             a_w_out=m_a_w_out, kv_norm_g=m_kv_norm_g, w_kv=m_w_kv, b_kv=m_b_kv, b_norm_g=m_b_norm_g, b_w_in=m_b_w_in,
             b_bq=m_b_bq, b_sinks=m_b_sinks, b_w_out=m_b_w_out, final_norm_g=m_final_norm_g)
    v = dict(a_norm_g=v_a_norm_g, a_w_in=v_a_w_in, a_ln_g=v_a_ln_g, a_ln_b=v_a_ln_b, a_ws=v_a_ws, a_bs=v_a_bs,
             a_w_out=v_a_w_out, kv_norm_g=v_kv_norm_g, w_kv=v_w_kv, b_kv=v_b_kv, b_norm_g=v_b_norm_g, b_w_in=v_b_w_in,
             b_bq=v_b_bq, b_sinks=v_b_sinks, b_w_out=v_b_w_out, final_norm_g=v_final_norm_g)
    return _step(x, loss_target, p, m, v)
```

```python
import functools
import math

import jax
import jax.numpy as jnp
from jax import lax
from jax.experimental import pallas as pl
from jax.experimental.pallas import tpu as pltpu

F32 = jnp.float32
BF16 = jnp.bfloat16

D_MODEL = 1024
CHUNK = 128
A_WIDTH = 2048
A_GROUPS = 16
HEAD_DIM = 64
N_Q_HEADS = 16
N_KV_HEADS = 2
Q_PER_KV = 8
B_WIDTH = 1024
KV_WIDTH = 128
ROPE_THETA = 10000.0
EPS = 1e-5
N_CHIPS = 4

ADAM_LR = 0.001
ADAM_B1 = 0.9
ADAM_B2 = 0.999
ADAM_EPS = 1e-08
ADAM_WD = 0.01
ADAM_STEP = 10

V7X_VMEM_BYTES = 64 * 1024 * 1024
VMEM_LIMIT = 48 * 1024 * 1024
MESH = pl.DeviceIdType.MESH
NEG_BIG = -1e30


def _cparams(**kw):
    return pltpu.CompilerParams(vmem_limit_bytes=VMEM_LIMIT, **kw)


def _matmul(a, b, *, dims, grid, a_spec, b_spec, o_spec, out_shape, name, acc_axis=None,
            residual=None, r_spec=None):
    has_res = residual is not None

    def body(*refs):
        if has_res:
            a_ref, b_ref, r_ref, o_ref = refs
        else:
            a_ref, b_ref, o_ref = refs
        part = lax.dot_general(a_ref[...], b_ref[...], dims, preferred_element_type=F32)
        if acc_axis is None:
            if has_res:
                part = part + r_ref[...]
            o_ref[...] = part.astype(o_ref.dtype)
        else:
            k = pl.program_id(acc_axis)

            @pl.when(k == 0)
            def _():
                o_ref[...] = part

            @pl.when(k > 0)
            def _():
                o_ref[...] += part

    in_specs = [a_spec, b_spec] + ([r_spec] if has_res else [])
    args = (a, b) + ((residual,) if has_res else ())
    return pl.pallas_call(
        body, grid=grid, in_specs=in_specs, out_specs=o_spec, out_shape=out_shape, name=name,
        compiler_params=_cparams(),
    )(*args)


NN = (((1,), (0,)), ((), ()))
NT = (((1,), (1,)), ((), ()))
TN = (((0,), (0,)), ((), ()))


def _row_tile(s, want):
    return min(s, want)


def _mm_nn(a, b, *, name, tn, out_dtype=F32, residual=None, tm=512):
    s, k = a.shape
    tm = _row_tile(s, tm)
    if b.ndim == 3:
        nsh, _, nc = b.shape
        npb = nc // tn
        n = nsh * nc
        b_spec = pl.BlockSpec((None, k, tn), lambda i, j: (j // npb, 0, j % npb))
    else:
        n = b.shape[1]
        b_spec = pl.BlockSpec((k, tn), lambda i, j: (0, j))
    return _matmul(
        a, b, dims=NN, grid=(s // tm, n // tn),
        a_spec=pl.BlockSpec((tm, k), lambda i, j: (i, 0)), b_spec=b_spec,
        o_spec=pl.BlockSpec((tm, tn), lambda i, j: (i, j)),
        out_shape=jax.ShapeDtypeStruct((s, n), out_dtype), name=name,
        residual=residual, r_spec=pl.BlockSpec((tm, tn), lambda i, j: (i, j)) if residual is not None else None)


def _mm_nt(a, b, *, name, tk, tn=None, tm=512, out_dtype=F32):
    s, k = a.shape
    tm = _row_tile(s, tm)
    if b.ndim == 3:
        nsh, n, kc = b.shape
        npb = kc // tk
        return _matmul(
            a, b, dims=NT, grid=(s // tm, k // tk), acc_axis=1,
            a_spec=pl.BlockSpec((tm, tk), lambda i, kk: (i, kk)),
            b_spec=pl.BlockSpec((None, n, tk), lambda i, kk: (kk // npb, 0, kk % npb)),
            o_spec=pl.BlockSpec((tm, n), lambda i, kk: (i, 0)),
            out_shape=jax.ShapeDtypeStruct((s, n), F32), name=name)
    n = b.shape[0]
    tn = n if tn is None else tn
    assert tk == k
    return _matmul(
        a, b, dims=NT, grid=(s // tm, n // tn),
        a_spec=pl.BlockSpec((tm, k), lambda i, j: (i, 0)),
        b_spec=pl.BlockSpec((tn, k), lambda i, j: (j, 0)),
        o_spec=pl.BlockSpec((tm, tn), lambda i, j: (i, j)),
        out_shape=jax.ShapeDtypeStruct((s, n), out_dtype), name=name)


def _mm_tn(a, b, *, name, tm, tn, tk=512, shards=None):
    s, m = a.shape
    n = b.shape[1]
    tk = _row_tile(s, tk)
    if shards is None:
        o_spec = pl.BlockSpec((tm, tn), lambda i, j, kk: (i, j))
        out_shape = jax.ShapeDtypeStruct((m, n), F32)
    else:
        assert tm == m
        nc = n // shards
        npb = nc // tn
        o_spec = pl.BlockSpec((None, m, tn), lambda i, j, kk: (j // npb, 0, j % npb))
        out_shape = jax.ShapeDtypeStruct((shards, m, nc), F32)
    return _matmul(
        a, b, dims=TN, grid=(m // tm, n // tn, s // tk), acc_axis=2,
        a_spec=pl.BlockSpec((tk, tm), lambda i, j, kk: (kk, i)),
        b_spec=pl.BlockSpec((tk, tn), lambda i, j, kk: (kk, j)),
        o_spec=o_spec, out_shape=out_shape, name=name)


def _rstd(x):
    return lax.rsqrt(jnp.mean(x * x, axis=-1, keepdims=True) + EPS)


def _rms_fwd(x, gains, *, name, tr=256):
    s, d = x.shape
    tr = _row_tile(s, tr)
    ng = len(gains)

    def body(*refs):
        x_ref = refs[0]
        xv = x_ref[...]
        xh = xv * _rstd(xv)
        for t in range(ng):
            refs[1 + ng + t][...] = (xh * refs[1 + t][...]).astype(BF16)

    row = pl.BlockSpec((tr, d), lambda i: (i, 0))
    vec = pl.BlockSpec((1, d), lambda i: (0, 0))
    return pl.pallas_call(
        body, grid=(s // tr,), in_specs=[row] + [vec] * ng, out_specs=[row] * ng,
        out_shape=[jax.ShapeDtypeStruct((s, d), BF16)] * ng, name=name, compiler_params=_cparams(),
    )(x, *gains)


def _rms_bwd(x, dns, gains, dres, *, name, tr=256):
    s, d = x.shape
    tr = _row_tile(s, tr)
    ng = len(gains)

    def body(*refs):
        x_ref = refs[0]
        dn_refs = refs[1:1 + ng]
        g_refs = refs[1 + ng:1 + 2 * ng]
        dres_ref = refs[1 + 2 * ng]
        dx_ref, dxb_ref = refs[2 + 2 * ng], refs[3 + 2 * ng]
        dg_refs = refs[4 + 2 * ng:]
        i = pl.program_id(0)
        xv = x_ref[...]
        r = _rstd(xv)
        xh = xv * r
        acc = jnp.zeros_like(xv)
        for t in range(ng):
            dn = dn_refs[t][...]
            acc = acc + dn * g_refs[t][...]
            dgt = jnp.sum(dn * xh, axis=0, keepdims=True)

            @pl.when(i == 0)
            def _(t=t, dgt=dgt):
                dg_refs[t][...] = dgt

            @pl.when(i > 0)
            def _(t=t, dgt=dgt):
                dg_refs[t][...] += dgt

        dx = dres_ref[...] + r * (acc - xh * jnp.mean(acc * xh, axis=-1, keepdims=True))
        dx_ref[...] = dx
        dxb_ref[...] = dx.astype(BF16)

    row = pl.BlockSpec((tr, d), lambda i: (i, 0))
    vec = pl.BlockSpec((1, d), lambda i: (0, 0))
    outs = pl.pallas_call(
        body, grid=(s // tr,), in_specs=[row] + [row] * ng + [vec] * ng + [row],
        out_specs=[row, row] + [vec] * ng,
        out_shape=[jax.ShapeDtypeStruct((s, d), F32), jax.ShapeDtypeStruct((s, d), BF16)]
        + [jax.ShapeDtypeStruct((1, d), F32)] * ng,
        name=name, compiler_params=_cparams(),
    )(x, *dns, *gains, dres)
    return outs[0], outs[1], outs[2:]


def _loss_head(h, tgt, gain, *, tr=256):
    s, d = h.shape
    tr = _row_tile(s, tr)

    def body(h_ref, t_ref, g_ref, loss_ref, dh_ref, dhb_ref, dg_ref):
        i = pl.program_id(0)
        hv = h_ref[...]
        g = g_ref[...]
        r = _rstd(hv)
        xh = hv * r
        diff = xh * g - t_ref[...]
        part = 0.5 / d * jnp.sum(jnp.sum(diff * diff, axis=-1, keepdims=True), axis=0, keepdims=True)
        dout = diff * (1.0 / d)
        a = dout * g
        dh = r * (a - xh * jnp.mean(a * xh, axis=-1, keepdims=True))
        dh_ref[...] = dh
        dhb_ref[...] = dh.astype(BF16)
        dgt = jnp.sum(dout * xh, axis=0, keepdims=True)
        lpart = jnp.broadcast_to(part, (8, 128))

        @pl.when(i == 0)
        def _():
            dg_ref[...] = dgt
            loss_ref[...] = lpart

        @pl.when(i > 0)
        def _():
            dg_ref[...] += dgt
            loss_ref[...] += lpart

    row = pl.BlockSpec((tr, d), lambda i: (i, 0))
    vec = pl.BlockSpec((1, d), lambda i: (0, 0))
    return pl.pallas_call(
        body, grid=(s // tr,), in_specs=[row, row, vec],
        out_specs=[pl.BlockSpec((8, 128), lambda i: (0, 0)), row, row, vec],
        out_shape=[jax.ShapeDtypeStruct((8, 128), F32), jax.ShapeDtypeStruct((s, d), F32),
                   jax.ShapeDtypeStruct((s, d), BF16), jax.ShapeDtypeStruct((1, d), F32)],
        name="loss_head", compiler_params=_cparams(),
    )(h, tgt, gain)


def _causal_mask(transposed=False):
    row = lax.broadcasted_iota(jnp.int32, (CHUNK, CHUNK), 0)
    col = lax.broadcasted_iota(jnp.int32, (CHUNK, CHUNK), 1)
    return col >= row if transposed else col <= row


def _silu_parts(g):
    sg = jax.nn.sigmoid(g)
    return g * sg, sg * (1.0 + g * (1.0 - sg))


def _gate_fwd(z, ln_g, ln_b, ws, bs_t, *, tr=256):
    s = z.shape[0]
    tr = _row_tile(s, tr)
    w = A_WIDTH

    def body(u_ref, v_ref, g_ref, lg_ref, lb_ref, ws_ref, bst_ref, y_ref):
        v = v_ref[...].astype(F32)
        mu = jnp.mean(v, axis=-1, keepdims=True)
        xc = v - mu
        rs = lax.rsqrt(jnp.mean(xc * xc, axis=-1, keepdims=True) + EPS)
        vln = (xc * rs * lg_ref[...] + lb_ref[...]).astype(BF16)
        mask = _causal_mask()
        for grp in range(A_GROUPS):
            cols = slice(grp * CHUNK, (grp + 1) * CHUNK)
            wsm = jnp.where(mask, ws_ref[grp], 0.0).astype(BF16)
            bcol = bst_ref[:, grp:grp + 1]
            for ci in range(tr // CHUNK):
                rows = slice(ci * CHUNK, (ci + 1) * CHUNK)
                sv = jnp.dot(wsm, vln[rows, cols], preferred_element_type=F32) + bcol
                gv = g_ref[rows, cols].astype(F32)
                y_ref[rows, cols] = (u_ref[rows, cols].astype(F32) * sv * (gv * jax.nn.sigmoid(gv))).astype(BF16)

    vec = pl.BlockSpec((1, w), lambda i: (0, 0))
    return pl.pallas_call(
        body, grid=(s // tr,),
        in_specs=[pl.BlockSpec((tr, w), lambda i: (i, 0)), pl.BlockSpec((tr, w), lambda i: (i, 1)),
                  pl.BlockSpec((tr, w), lambda i: (i, 2)), vec, vec,
                  pl.BlockSpec((A_GROUPS, CHUNK, CHUNK), lambda i: (0, 0, 0)),
                  pl.BlockSpec((CHUNK, A_GROUPS), lambda i: (0, 0))],
        out_specs=pl.BlockSpec((tr, w), lambda i: (i, 0)),
        out_shape=jax.ShapeDtypeStruct((s, w), BF16), name="gate_fwd", compiler_params=_cparams(),
    )(z, z, z, ln_g, ln_b, ws, bs_t)


def _gate_bwd(z, dy, ln_g, ln_b, ws, ws_t, bs_t, *, tr=256):
    s = z.shape[0]
    tr = _row_tile(s, tr)
    w = A_WIDTH
    nsteps = s // tr

    def body(u_ref, v_ref, g_ref, dy_ref, lg_ref, lb_ref, ws_ref, wst_ref, bst_ref,
             dz_ref, dlg_ref, dlb_ref, dws_ref, dbst_ref, dvln_sc, dsv_sc):
        i = pl.program_id(0)

        @pl.when(i == 0)
        def _():
            dws_ref[...] = jnp.zeros_like(dws_ref)
            dsv_sc[...] = jnp.zeros_like(dsv_sc)

        v = v_ref[...].astype(F32)
        mu = jnp.mean(v, axis=-1, keepdims=True)
        xc = v - mu
        rs = lax.rsqrt(jnp.mean(xc * xc, axis=-1, keepdims=True) + EPS)
        xh = xc * rs
        lg = lg_ref[...]
        vln = (xh * lg + lb_ref[...]).astype(BF16)
        mask = _causal_mask()
        mask_t = _causal_mask(transposed=True)
        for grp in range(A_GROUPS):
            cols = slice(grp * CHUNK, (grp + 1) * CHUNK)
            wsm = jnp.where(mask, ws_ref[grp], 0.0).astype(BF16)
            wsm_t = jnp.where(mask_t, wst_ref[grp], 0.0).astype(BF16)
            bcol = bst_ref[:, grp:grp + 1]
            for ci in range(tr // CHUNK):
                rows = slice(ci * CHUNK, (ci + 1) * CHUNK)
                vb = vln[rows, cols]
                sv = jnp.dot(wsm, vb, preferred_element_type=F32) + bcol
                uv = u_ref[rows, cols].astype(F32)
                silu, dsilu = _silu_parts(g_ref[rows, cols].astype(F32))
                dyv = dy_ref[rows, cols].astype(F32)
                dyu = dyv * uv
                dz_ref[rows, cols] = (dyv * sv * silu).astype(BF16)
                dz_ref[rows, 2 * w + grp * CHUNK:2 * w + (grp + 1) * CHUNK] = (dyu * sv * dsilu).astype(BF16)
                dsv = dyu * silu
                dsvb = dsv.astype(BF16)
                dvln_sc[rows, cols] = jnp.dot(wsm_t, dsvb, preferred_element_type=F32)
                dws_ref[grp] += lax.dot_general(dsvb, vb, NT, preferred_element_type=F32)
                dsv_sc[grp] += dsv
        dvln = dvln_sc[...]
        dlg_t = jnp.sum(dvln * xh, axis=0, keepdims=True)
        dlb_t = jnp.sum(dvln, axis=0, keepdims=True)
        a = dvln * lg
        dv = rs * (a - jnp.mean(a, axis=-1, keepdims=True) - xh * jnp.mean(a * xh, axis=-1, keepdims=True))
        dz_ref[:, w:2 * w] = dv.astype(BF16)

        @pl.when(i == 0)
        def _():
            dlg_ref[...] = dlg_t
            dlb_ref[...] = dlb_t

        @pl.when(i > 0)
        def _():
            dlg_ref[...] += dlg_t
            dlb_ref[...] += dlb_t

        @pl.when(i == nsteps - 1)
        def _():
            for grp in range(A_GROUPS):
                dws_ref[grp] = jnp.where(mask, dws_ref[grp], 0.0)
                dbst_ref[:, grp:grp + 1] = jnp.sum(dsv_sc[grp], axis=-1, keepdims=True)

    vec = pl.BlockSpec((1, w), lambda i: (0, 0))
    wsspec = pl.BlockSpec((A_GROUPS, CHUNK, CHUNK), lambda i: (0, 0, 0))
    bsspec = pl.BlockSpec((CHUNK, A_GROUPS), lambda i: (0, 0))
    return pl.pallas_call(
        body, grid=(nsteps,),
        in_specs=[pl.BlockSpec((tr, w), lambda i: (i, 0)), pl.BlockSpec((tr, w), lambda i: (i, 1)),
                  pl.BlockSpec((tr, w), lambda i: (i, 2)), pl.BlockSpec((tr, w), lambda i: (i, 0)),
                  vec, vec, wsspec, wsspec, bsspec],
        out_specs=[pl.BlockSpec((tr, 3 * w), lambda i: (i, 0)), vec, vec, wsspec, bsspec],
        out_shape=[jax.ShapeDtypeStruct((s, 3 * w), BF16), jax.ShapeDtypeStruct((1, w), F32),
                   jax.ShapeDtypeStruct((1, w), F32), jax.ShapeDtypeStruct((A_GROUPS, CHUNK, CHUNK), F32),
                   jax.ShapeDtypeStruct((CHUNK, A_GROUPS), F32)],
        scratch_shapes=[pltpu.VMEM((tr, w), F32), pltpu.VMEM((A_GROUPS, CHUNK, CHUNK), F32)],
        name="gate_bwd", compiler_params=_cparams(),
    )(z, z, z, dy, ln_g, ln_b, ws, ws_t, bs_t)


def _rope_tables(s):
    inv_freq = ROPE_THETA ** (-jnp.arange(0, HEAD_DIM, 2, dtype=F32) / HEAD_DIM)
    ang = jnp.arange(s, dtype=F32)[:, None] * inv_freq[None, :]
    cos, sin = jnp.cos(ang), jnp.sin(ang)
    cos2 = jnp.concatenate([cos, cos], axis=-1)
    sin2 = jnp.concatenate([-sin, sin], axis=-1)
    return jnp.tile(cos2, (1, 2)), jnp.tile(sin2, (1, 2))


def _swap_halves(x):
    n = x.shape[-1]
    lane = lax.broadcasted_iota(jnp.int32, x.shape, x.ndim - 1)
    first = (lane % HEAD_DIM) < (HEAD_DIM // 2)
    return jnp.where(first, pltpu.roll(x, n - HEAD_DIM // 2, x.ndim - 1), pltpu.roll(x, HEAD_DIM // 2, x.ndim - 1))


def _tile_lanes(t, width):
    return jnp.tile(t, (1, width // t.shape[-1]))


def _kv_rope(kv, b_kv, cos, sin, *, tr=512):
    s = kv.shape[0]
    tr = _row_tile(s, tr)

    def body(kv_ref, b_ref, c_ref, s_ref, k_ref, v_ref):
        x = kv_ref[...] + b_ref[...]
        k = x[:, :KV_WIDTH]
        k_ref[...] = (k * c_ref[...] + _swap_halves(k) * s_ref[...]).astype(BF16)
        v_ref[...] = x[:, KV_WIDTH:].astype(BF16)

    tab = pl.BlockSpec((tr, KV_WIDTH), lambda i: (i, 0))
    return pl.pallas_call(
        body, grid=(s // tr,),
        in_specs=[pl.BlockSpec((tr, 2 * KV_WIDTH), lambda i: (i, 0)),
                  pl.BlockSpec((1, 2 * KV_WIDTH), lambda i: (0, 0)), tab, tab],
        out_specs=[tab, tab], out_shape=[jax.ShapeDtypeStruct((s, KV_WIDTH), BF16)] * 2,
        name="kv_rope", compiler_params=_cparams(),
    )(kv, b_kv, cos, sin)


def _kv_rope_bwd(dk_rot, dv, cos, sin, *, tr=512):
    s = dk_rot.shape[0]
    tr = _row_tile(s, tr)

    def body(dk_ref, dv_ref, c_ref, s_ref, dkv_ref, db_ref):
        i = pl.program_id(0)
        d = dk_ref[...]
        dk = d * c_ref[...] + _swap_halves(d * s_ref[...])
        dvv = dv_ref[...]
        dkv_ref[:, :KV_WIDTH] = dk.astype(BF16)
        dkv_ref[:, KV_WIDTH:] = dvv.astype(BF16)
        sk = jnp.sum(dk, axis=0, keepdims=True)
        sv = jnp.sum(dvv, axis=0, keepdims=True)

        @pl.when(i == 0)
        def _():
            db_ref[:, :KV_WIDTH] = sk
            db_ref[:, KV_WIDTH:] = sv

        @pl.when(i > 0)
        def _():
            db_ref[:, :KV_WIDTH] += sk
            db_ref[:, KV_WIDTH:] += sv

    tab = pl.BlockSpec((tr, KV_WIDTH), lambda i: (i, 0))
    return pl.pallas_call(
        body, grid=(s // tr,), in_specs=[tab, tab, tab, tab],
        out_specs=[pl.BlockSpec((tr, 2 * KV_WIDTH), lambda i: (i, 0)),
                   pl.BlockSpec((1, 2 * KV_WIDTH), lambda i: (0, 0))],
        out_shape=[jax.ShapeDtypeStruct((s, 2 * KV_WIDTH), BF16), jax.ShapeDtypeStruct((1, 2 * KV_WIDTH), F32)],
        name="kv_rope_bwd", compiler_params=_cparams(),
    )(dk_rot, dv, cos, sin)


def _window_mask(i):
    q = lax.broadcasted_iota(jnp.int32, (CHUNK, 2 * CHUNK), 0)
    k = lax.broadcasted_iota(jnp.int32, (CHUNK, 2 * CHUNK), 1)
    first_valid = jnp.where(i > 0, 0, CHUNK)
    prev = (k < CHUNK) & (k > q) & (k >= first_valid)
    cur = (k >= CHUNK) & (k - CHUNK <= q)
    return prev | cur


def _attn_specs():
    qspec = pl.BlockSpec((CHUNK, B_WIDTH), lambda i: (i, 0))
    gspec = pl.BlockSpec((CHUNK, B_WIDTH), lambda i: (i, 1))
    prev = pl.BlockSpec((CHUNK, KV_WIDTH), lambda i: (jnp.maximum(i - 1, 0), 0))
    cur = pl.BlockSpec((CHUNK, KV_WIDTH), lambda i: (i, 0))
    bq = pl.BlockSpec((1, B_WIDTH), lambda i: (0, 0))
    sinks = pl.BlockSpec(memory_space=pltpu.SMEM)
    return qspec, gspec, prev, cur, bq, sinks


def _rot_q(zq_ref, bq_ref, c_ref, s_ref):
    q = zq_ref[...].astype(F32) + bq_ref[...]
    cos = _tile_lanes(c_ref[...], B_WIDTH)
    sin = _tile_lanes(s_ref[...], B_WIDTH)
    return (q * cos + _swap_halves(q) * sin).astype(BF16), cos, sin


def _head_probs(qh, kh, mask, sink):
    sc = lax.dot_general(qh, kh, NT, preferred_element_type=F32) * (HEAD_DIM ** -0.5)
    sc = jnp.where(mask, sc, NEG_BIG)
    m = jnp.maximum(jnp.max(sc, axis=-1, keepdims=True), sink)
    p = jnp.exp(sc - m)
    esink = jnp.exp(sink - m)
    inv = 1.0 / (jnp.sum(p, axis=-1, keepdims=True) + esink)
    return p * inv, esink * inv


def _attn_fwd(zb, kr, vv, cos, sin, b_bq, sinks):
    s = zb.shape[0]

    def body(zq_ref, zg_ref, kp_ref, kc_ref, vp_ref, vc_ref, c_ref, s_ref, bq_ref, sk_ref, y_ref, o_sc):
        i = pl.program_id(0)
        qr, _, _ = _rot_q(zq_ref, bq_ref, c_ref, s_ref)
        kcat = jnp.concatenate([kp_ref[...], kc_ref[...]], axis=0)
        vcat = jnp.concatenate([vp_ref[...], vc_ref[...]], axis=0)
        mask = _window_mask(i)
        for h in range(N_Q_HEADS):
            kvh = h // Q_PER_KV
            hc = slice(h * HEAD_DIM, (h + 1) * HEAD_DIM)
            kc = slice(kvh * HEAD_DIM, (kvh + 1) * HEAD_DIM)
            p, _ = _head_probs(qr[:, hc], kcat[:, kc], mask, sk_ref[0, h])
            o_sc[:, hc] = jnp.dot(p.astype(BF16), vcat[:, kc], preferred_element_type=F32)
        gv = zg_ref[...].astype(F32)
        y_ref[...] = (o_sc[...] * (gv * jax.nn.sigmoid(gv))).astype(BF16)

    qspec, gspec, prev, cur, bq, sk = _attn_specs()
    return pl.pallas_call(
        body, grid=(s // CHUNK,),
        in_specs=[qspec, gspec, prev, cur, prev, cur, cur, cur, bq, sk],
        out_specs=qspec, out_shape=jax.ShapeDtypeStruct((s, B_WIDTH), BF16),
        scratch_shapes=[pltpu.VMEM((CHUNK, B_WIDTH), F32)],
        name="attn_fwd", compiler_params=_cparams(),
    )(zb, zb, kr, kr, vv, vv, cos, sin, b_bq, sinks)


def _attn_bwd(zb, dyb, kr, vv, cos, sin, b_bq, sinks):
    s = zb.shape[0]

    def body(zq_ref, zg_ref, dy_ref, kp_ref, kc_ref, vp_ref, vc_ref, c_ref, s_ref, bq_ref, sk_ref,
             dz_ref, dk_ref, dv_ref, dbq_ref, dsk_ref, o_sc, dq_sc):
        i = pl.program_id(0)

        @pl.when(i == 0)
        def _():
            dk_ref[...] = jnp.zeros_like(dk_ref)
            dv_ref[...] = jnp.zeros_like(dv_ref)
            dbq_ref[...] = jnp.zeros_like(dbq_ref)
            dsk_ref[...] = jnp.zeros_like(dsk_ref)

        qr, cos, sin = _rot_q(zq_ref, bq_ref, c_ref, s_ref)
        kcat = jnp.concatenate([kp_ref[...], kc_ref[...]], axis=0)
        vcat = jnp.concatenate([vp_ref[...], vc_ref[...]], axis=0)
        mask = _window_mask(i)
        gv = zg_ref[...].astype(F32)
        silu, dsilu = _silu_parts(gv)
        dyv = dy_ref[...].astype(F32)
        do_all = (dyv * silu).astype(BF16)
        lane = lax.broadcasted_iota(jnp.int32, (1, 128), 1)
        dsk_row = jnp.zeros((1, 128), F32)
        for kvh in range(N_KV_HEADS):
            kc = slice(kvh * HEAD_DIM, (kvh + 1) * HEAD_DIM)
            kh, vh = kcat[:, kc], vcat[:, kc]
            dk_acc = jnp.zeros((2 * CHUNK, HEAD_DIM), F32)
            dv_acc = jnp.zeros((2 * CHUNK, HEAD_DIM), F32)
            for r in range(Q_PER_KV):
                h = kvh * Q_PER_KV + r
                hc = slice(h * HEAD_DIM, (h + 1) * HEAD_DIM)
                qh = qr[:, hc]
                p, psink = _head_probs(qh, kh, mask, sk_ref[0, h])
                pb = p.astype(BF16)
                o_sc[:, hc] = jnp.dot(pb, vh, preferred_element_type=F32)
                doh = do_all[:, hc]
                dp = lax.dot_general(doh, vh, NT, preferred_element_type=F32)
                delta = jnp.sum(p * dp, axis=-1, keepdims=True)
                ds = (p * (dp - delta) * (HEAD_DIM ** -0.5)).astype(BF16)
                dq_sc[:, hc] = jnp.dot(ds, kh, preferred_element_type=F32)
                dk_acc = dk_acc + lax.dot_general(ds, qh, TN, preferred_element_type=F32)
                dv_acc = dv_acc + lax.dot_general(pb, doh, TN, preferred_element_type=F32)
                dsink = -jnp.sum(psink * delta, axis=0, keepdims=True)
                dsk_row = dsk_row + jnp.where(lane == h, dsink, 0.0)
            cur_rows = pl.ds(pl.multiple_of(i * CHUNK, CHUNK), CHUNK)
            dk_ref[cur_rows, kc] += dk_acc[CHUNK:]
            dv_ref[cur_rows, kc] += dv_acc[CHUNK:]

            @pl.when(i > 0)
            def _(kc=kc, dk_acc=dk_acc, dv_acc=dv_acc):
                prev_rows = pl.ds(pl.multiple_of((i - 1) * CHUNK, CHUNK), CHUNK)
                dk_ref[prev_rows, kc] += dk_acc[:CHUNK]
                dv_ref[prev_rows, kc] += dv_acc[:CHUNK]

        dsk_ref[0:1, :] += dsk_row
        dqr = dq_sc[...]
        dq = dqr * cos + _swap_halves(dqr * sin)
        dbq_ref[...] += jnp.sum(dq, axis=0, keepdims=True)
        dz_ref[:, :B_WIDTH] = dq.astype(BF16)
        dz_ref[:, B_WIDTH:] = (dyv * o_sc[...] * dsilu).astype(BF16)

    qspec, gspec, prev, cur, bq, sk = _attn_specs()
    full = pl.BlockSpec((s, KV_WIDTH), lambda i: (0, 0))
    return pl.pallas_call(
        body, grid=(s // CHUNK,),
        in_specs=[qspec, gspec, qspec, prev, cur, prev, cur, cur, cur, bq, sk],
        out_specs=[pl.BlockSpec((CHUNK, 2 * B_WIDTH), lambda i: (i, 0)), full, full, bq,
                   pl.BlockSpec((8, 128), lambda i: (0, 0))],
        out_shape=[jax.ShapeDtypeStruct((s, 2 * B_WIDTH), BF16), jax.ShapeDtypeStruct((s, KV_WIDTH), F32),
                   jax.ShapeDtypeStruct((s, KV_WIDTH), F32), jax.ShapeDtypeStruct((1, B_WIDTH), F32),
                   jax.ShapeDtypeStruct((8, 128), F32)],
        scratch_shapes=[pltpu.VMEM((CHUNK, B_WIDTH), F32), pltpu.VMEM((CHUNK, B_WIDTH), F32)],
        name="attn_bwd", compiler_params=_cparams(),
    )(zb, zb, dyb, kr, kr, vv, vv, cos, sin, b_bq, sinks)


HEADS_PER_BLOCK = 128 // HEAD_DIM
BLOCKS_PER_KV = Q_PER_KV // HEADS_PER_BLOCK
SCALE = HEAD_DIM ** -0.5


def _left_half(rows):
    return lax.broadcasted_iota(jnp.int32, (rows, 128), 1) < HEAD_DIM


def _dup_heads(x):
    left = _left_half(x.shape[0])
    swapped = pltpu.roll(x, HEAD_DIM, 1)
    return jnp.concatenate([jnp.where(left, x, swapped), jnp.where(left, swapped, x)], axis=-1)


def _fold_heads(a):
    b0, b1 = a[:, :128], a[:, 128:]
    f0 = b0 + pltpu.roll(b0, HEAD_DIM, 1)
    f1 = b1 + pltpu.roll(b1, HEAD_DIM, 1)
    return jnp.where(_left_half(a.shape[0]), f0, f1)


def _kv_rope2(kv, b_kv, cos, sin, *, tr=512):
    s = kv.shape[0]
    tr = _row_tile(s, tr)

    def body(kv_ref, b_ref, c_ref, s_ref, k_ref, v_ref):
        x = kv_ref[...] + b_ref[...]
        k = x[:, :KV_WIDTH]
        k_ref[...] = _dup_heads(k * c_ref[...] + _swap_halves(k) * s_ref[...]).astype(BF16)
        v_ref[...] = _dup_heads(x[:, KV_WIDTH:]).astype(BF16)

    tab = pl.BlockSpec((tr, KV_WIDTH), lambda i: (i, 0))
    wide = pl.BlockSpec((tr, 2 * KV_WIDTH), lambda i: (i, 0))
    return pl.pallas_call(
        body, grid=(s // tr,),
        in_specs=[wide, pl.BlockSpec((1, 2 * KV_WIDTH), lambda i: (0, 0)), tab, tab],
        out_specs=[wide, wide], out_shape=[jax.ShapeDtypeStruct((s, 2 * KV_WIDTH), BF16)] * 2,
        name="kv_rope", compiler_params=_cparams(),
    )(kv, b_kv, cos, sin)


def _kv_rope_bwd2(dk2, dv2, cos, sin, *, tr=512):
    s = dk2.shape[0]
    tr = _row_tile(s, tr)

    def body(dk_ref, dv_ref, c_ref, s_ref, dkv_ref, db_ref):
        i = pl.program_id(0)
        d = _fold_heads(dk_ref[...])
        dk = d * c_ref[...] + _swap_halves(d * s_ref[...])
        dvv = _fold_heads(dv_ref[...])
        dkv_ref[:, :KV_WIDTH] = dk.astype(BF16)
        dkv_ref[:, KV_WIDTH:] = dvv.astype(BF16)
        sk = jnp.sum(dk, axis=0, keepdims=True)
        sv = jnp.sum(dvv, axis=0, keepdims=True)

        @pl.when(i == 0)
        def _():
            db_ref[:, :KV_WIDTH] = sk
            db_ref[:, KV_WIDTH:] = sv

        @pl.when(i > 0)
        def _():
            db_ref[:, :KV_WIDTH] += sk
            db_ref[:, KV_WIDTH:] += sv

    tab = pl.BlockSpec((tr, KV_WIDTH), lambda i: (i, 0))
    wide = pl.BlockSpec((tr, 2 * KV_WIDTH), lambda i: (i, 0))
    return pl.pallas_call(
        body, grid=(s // tr,), in_specs=[wide, wide, tab, tab],
        out_specs=[wide, pl.BlockSpec((1, 2 * KV_WIDTH), lambda i: (0, 0))],
        out_shape=[jax.ShapeDtypeStruct((s, 2 * KV_WIDTH), BF16), jax.ShapeDtypeStruct((1, 2 * KV_WIDTH), F32)],
        name="kv_rope_bwd", compiler_params=_cparams(),
    )(dk2, dv2, cos, sin)


def _stacked_mask(i):
    cols = Q_PER_KV * CHUNK
    k = lax.broadcasted_iota(jnp.int32, (2 * CHUNK, cols), 0)
    q = lax.broadcasted_iota(jnp.int32, (2 * CHUNK, cols), 1) & (CHUNK - 1)
    first_valid = jnp.where(i > 0, 0, CHUNK)
    prev = (k < CHUNK) & (k > q) & (k >= first_valid)
    cur = (k >= CHUNK) & (k - CHUNK <= q)
    return prev | cur


def _stack_heads(blocks, left):
    parts = []
    for b in blocks:
        parts.append(jnp.where(left, b, jnp.zeros_like(b)))
        parts.append(jnp.where(left, jnp.zeros_like(b), b))
    return jnp.concatenate(parts, axis=0)


def _unstack_heads(xt):
    top = lax.broadcasted_iota(jnp.int32, (128, CHUNK), 0) < HEAD_DIM
    return [jnp.where(top, xt[:, (2 * b) * CHUNK:(2 * b + 1) * CHUNK], xt[:, (2 * b + 1) * CHUNK:(2 * b + 2) * CHUNK]).T
            for b in range(BLOCKS_PER_KV)]


def _sink_row(sk_ref, kvh):
    return jnp.concatenate([jnp.full((1, CHUNK), sk_ref[0, kvh * Q_PER_KV + r], F32) for r in range(Q_PER_KV)], axis=1)


def _stacked_probs(qs, kd, mask, sink):
    sc = lax.dot_general(kd, qs, NT, preferred_element_type=F32) * SCALE
    sc = jnp.where(mask, sc, NEG_BIG)
    m = jnp.maximum(jnp.max(sc, axis=0, keepdims=True), sink)
    p = jnp.exp(sc - m)
    esink = jnp.exp(sink - m)
    inv = 1.0 / (jnp.sum(p, axis=0, keepdims=True) + esink)
    return p * inv, esink * inv


def _lane_block(b):
    return slice(b * 128, (b + 1) * 128)


def _rope_blocks(zq_ref, bq_ref, cos, sin, kvh):
    out = []
    for b in range(BLOCKS_PER_KV):
        cols = _lane_block(kvh * BLOCKS_PER_KV + b)
        q = zq_ref[:, cols].astype(F32) + bq_ref[:, cols]
        out.append((q * cos + _swap_halves(q) * sin).astype(BF16))
    return out


def _attn_specs2():
    qspec = pl.BlockSpec((CHUNK, B_WIDTH), lambda i: (i, 0))
    gspec = pl.BlockSpec((CHUNK, B_WIDTH), lambda i: (i, 1))
    prev = pl.BlockSpec((CHUNK, 2 * KV_WIDTH), lambda i: (jnp.maximum(i - 1, 0), 0))
    cur = pl.BlockSpec((CHUNK, 2 * KV_WIDTH), lambda i: (i, 0))
    tab = pl.BlockSpec((CHUNK, KV_WIDTH), lambda i: (i, 0))
    bq = pl.BlockSpec((1, B_WIDTH), lambda i: (0, 0))
    sinks = pl.BlockSpec(memory_space=pltpu.SMEM)
    return qspec, gspec, prev, cur, tab, bq, sinks


def _attn_fwd2(zb, k2, v2, cos, sin, b_bq, sinks):
    s = zb.shape[0]

    def body(zq_ref, zg_ref, kp_ref, kc_ref, vp_ref, vc_ref, c_ref, s_ref, bq_ref, sk_ref, y_ref):
        i = pl.program_id(0)
        cos, sin = c_ref[...], s_ref[...]
        kcat = jnp.concatenate([kp_ref[...], kc_ref[...]], axis=0)
        vcat = jnp.concatenate([vp_ref[...], vc_ref[...]], axis=0)
        mask = _stacked_mask(i)
        left = _left_half(CHUNK)
        for kvh in range(N_KV_HEADS):
            qs = _stack_heads(_rope_blocks(zq_ref, bq_ref, cos, sin, kvh), left)
            p, _ = _stacked_probs(qs, kcat[:, _lane_block(kvh)], mask, _sink_row(sk_ref, kvh))
            ot = lax.dot_general(vcat[:, _lane_block(kvh)], p.astype(BF16), TN, preferred_element_type=F32)
            for b, ob in enumerate(_unstack_heads(ot)):
                cols = _lane_block(kvh * BLOCKS_PER_KV + b)
                gv = zg_ref[:, cols].astype(F32)
                y_ref[:, cols] = (ob * (gv * jax.nn.sigmoid(gv))).astype(BF16)

    qspec, gspec, prev, cur, tab, bq, sk = _attn_specs2()
    return pl.pallas_call(
        body, grid=(s // CHUNK,),
        in_specs=[qspec, gspec, prev, cur, prev, cur, tab, tab, bq, sk],
        out_specs=qspec, out_shape=jax.ShapeDtypeStruct((s, B_WIDTH), BF16),
        name="attn_fwd", compiler_params=_cparams(),
    )(zb, zb, k2, k2, v2, v2, cos, sin, b_bq, sinks)


def _attn_bwd2(zb, dyb, k2, v2, cos, sin, b_bq, sinks):
    s = zb.shape[0]

    def body(zq_ref, zg_ref, dy_ref, kp_ref, kc_ref, vp_ref, vc_ref, c_ref, s_ref, bq_ref, sk_ref,
             dz_ref, dk_ref, dv_ref, dbq_ref, dsk_ref):
        i = pl.program_id(0)

        @pl.when(i == 0)
        def _():
            dk_ref[...] = jnp.zeros_like(dk_ref)
            dv_ref[...] = jnp.zeros_like(dv_ref)
            dbq_ref[...] = jnp.zeros_like(dbq_ref)
            dsk_ref[...] = jnp.zeros_like(dsk_ref)

        cos, sin = c_ref[...], s_ref[...]
        kcat = jnp.concatenate([kp_ref[...], kc_ref[...]], axis=0)
        vcat = jnp.concatenate([vp_ref[...], vc_ref[...]], axis=0)
        mask = _stacked_mask(i)
        left = _left_half(CHUNK)
        lane = lax.broadcasted_iota(jnp.int32, (1, 128), 1)
        dsk_row = jnp.zeros((1, 128), F32)
        cur_rows = pl.ds(pl.multiple_of(i * CHUNK, CHUNK), CHUNK)
        for kvh in range(N_KV_HEADS):
            kd, vd = kcat[:, _lane_block(kvh)], vcat[:, _lane_block(kvh)]
            qs = _stack_heads(_rope_blocks(zq_ref, bq_ref, cos, sin, kvh), left)
            p, psink = _stacked_probs(qs, kd, mask, _sink_row(sk_ref, kvh))
            pb = p.astype(BF16)
            ot = lax.dot_general(vd, pb, TN, preferred_element_type=F32)
            gates, dys = [], []
            for b in range(BLOCKS_PER_KV):
                cols = _lane_block(kvh * BLOCKS_PER_KV + b)
                gates.append(_silu_parts(zg_ref[:, cols].astype(F32)))
                dys.append(dy_ref[:, cols].astype(F32))
            dos = _stack_heads([(dyv * silu).astype(BF16) for dyv, (silu, _) in zip(dys, gates)], left)
            dp = lax.dot_general(vd, dos, NT, preferred_element_type=F32)
            delta = jnp.sum(p * dp, axis=0, keepdims=True)
            ds = (p * (dp - delta) * SCALE).astype(BF16)
            dqt = lax.dot_general(kd, ds, TN, preferred_element_type=F32)
            dk_part = jnp.dot(ds, qs, preferred_element_type=F32)
            dv_part = jnp.dot(pb, dos, preferred_element_type=F32)
            dk_ref[cur_rows, _lane_block(kvh)] += dk_part[CHUNK:]
            dv_ref[cur_rows, _lane_block(kvh)] += dv_part[CHUNK:]

            @pl.when(i > 0)
            def _(kvh=kvh, dk_part=dk_part, dv_part=dv_part):
                prev_rows = pl.ds(pl.multiple_of((i - 1) * CHUNK, CHUNK), CHUNK)
                dk_ref[prev_rows, _lane_block(kvh)] += dk_part[:CHUNK]
                dv_ref[prev_rows, _lane_block(kvh)] += dv_part[:CHUNK]

            sink_grad = psink * delta
            for r in range(Q_PER_KV):
                dsink = -jnp.sum(sink_grad[:, r * CHUNK:(r + 1) * CHUNK], axis=1, keepdims=True)
                dsk_row = dsk_row + jnp.where(lane == kvh * Q_PER_KV + r, dsink, 0.0)
            blocks = zip(_unstack_heads(ot), _unstack_heads(dqt), dys, gates)
            for b, (ob, dqr, dyv, (_, dsilu)) in enumerate(blocks):
                blk = kvh * BLOCKS_PER_KV + b
                dq = dqr * cos + _swap_halves(dqr * sin)
                dbq_ref[:, _lane_block(blk)] += jnp.sum(dq, axis=0, keepdims=True)
                dz_ref[:, _lane_block(blk)] = dq.astype(BF16)
                dz_ref[:, _lane_block(B_WIDTH // 128 + blk)] = (dyv * ob * dsilu).astype(BF16)
        dsk_ref[0:1, :] += dsk_row

    qspec, gspec, prev, cur, tab, bq, sk = _attn_specs2()
    full = pl.BlockSpec((s, 2 * KV_WIDTH), lambda i: (0, 0))
    return pl.pallas_call(
        body, grid=(s // CHUNK,),
        in_specs=[qspec, gspec, qspec, prev, cur, prev, cur, tab, tab, bq, sk],
        out_specs=[pl.BlockSpec((CHUNK, 2 * B_WIDTH), lambda i: (i, 0)), full, full, bq,
                   pl.BlockSpec((8, 128), lambda i: (0, 0))],
        out_shape=[jax.ShapeDtypeStruct((s, 2 * B_WIDTH), BF16), jax.ShapeDtypeStruct((s, 2 * KV_WIDTH), F32),
                   jax.ShapeDtypeStruct((s, 2 * KV_WIDTH), F32), jax.ShapeDtypeStruct((1, B_WIDTH), F32),
                   jax.ShapeDtypeStruct((8, 128), F32)],
        name="attn_bwd", compiler_params=_cparams(),
    )(zb, zb, dyb, k2, k2, v2, v2, cos, sin, b_bq, sinks)


def _local_step(x, tgt, w):
    s = x.shape[0]
    cos, sin = _rope_tables(s)
    ws = w["a_ws"]
    ws_t = jnp.swapaxes(ws, 1, 2)
    bs_t = w["a_bs"].T

    (n_a,) = _rms_fwd(x, [w["a_norm_g"]], name="rms_a")
    z = _mm_nn(n_a, w["a_w_in"], name="mm_a_in", tn=768, tm=1024, out_dtype=BF16)
    y = _gate_fwd(z, w["a_ln_g"], w["a_ln_b"], ws, bs_t)
    h1 = _mm_nn(y, w["a_w_out"], name="mm_a_out", tn=D_MODEL, residual=x)
    n_kv, n_b = _rms_fwd(h1, [w["kv_norm_g"], w["b_norm_g"]], name="rms_b")
    kv = _mm_nn(n_kv, w["w_kv"], name="mm_kv", tn=2 * KV_WIDTH)
    kr, vv = _kv_rope2(kv, w["b_kv"], cos, sin)
    zb = _mm_nn(n_b, w["b_w_in"], name="mm_b_in", tn=512, tm=1024, out_dtype=BF16)
    yb = _attn_fwd2(zb, kr, vv, cos, sin, w["b_bq"], w["b_sinks"])
    h2 = _mm_nn(yb, w["b_w_out"], name="mm_b_out", tn=D_MODEL, residual=h1)
    loss_blk, dh2, dh2b, d_final_g = _loss_head(h2, tgt, w["final_norm_g"])

    d_b_w_out = _mm_tn(yb, dh2b, name="mm_d_b_w_out", tm=B_WIDTH, tn=D_MODEL)
    dyb = _mm_nt(dh2b, w["b_w_out"], name="mm_dyb", tk=D_MODEL, out_dtype=BF16)
    dzb, dk_rot, dv, d_bq, d_sinks = _attn_bwd2(zb, dyb, kr, vv, cos, sin, w["b_bq"], w["b_sinks"])
    dkv, d_b_kv = _kv_rope_bwd2(dk_rot, dv, cos, sin)
    d_b_w_in = _mm_tn(n_b, dzb, name="mm_d_b_w_in", tm=D_MODEL, tn=512, shards=N_CHIPS)
    dn_b = _mm_nt(dzb, w["b_w_in"], name="mm_dn_b", tk=512)
    d_w_kv = _mm_tn(n_kv, dkv, name="mm_d_w_kv", tm=D_MODEL, tn=2 * KV_WIDTH)
    dn_kv = _mm_nt(dkv, w["w_kv"], name="mm_dn_kv", tk=2 * KV_WIDTH)
    dh1, dh1b, (d_kv_g, d_b_g) = _rms_bwd(h1, [dn_kv, dn_b], [w["kv_norm_g"], w["b_norm_g"]], dh2, name="rms_b_bwd")

    d_a_w_out = _mm_tn(y, dh1b, name="mm_d_a_w_out", tm=1024, tn=D_MODEL)
    dy = _mm_nt(dh1b, w["a_w_out"], name="mm_dy", tk=D_MODEL, tn=1024, out_dtype=BF16)
    dz, d_ln_g, d_ln_b, d_ws, d_bs_t = _gate_bwd(z, dy, w["a_ln_g"], w["a_ln_b"], ws, ws_t, bs_t)
    d_a_w_in = _mm_tn(n_a, dz, name="mm_d_a_w_in", tm=D_MODEL, tn=1536, shards=N_CHIPS)
    dn_a = _mm_nt(dz, w["a_w_in"], name="mm_dn_a", tk=768, tm=1024)
    dx, _, (d_a_g,) = _rms_bwd(x, [dn_a], [w["a_norm_g"]], dh1, name="rms_a_bwd")

    big = {
        "a_w_in": d_a_w_in,
        "a_w_out": d_a_w_out.reshape(N_CHIPS, A_WIDTH // N_CHIPS, D_MODEL),
        "w_kv": d_w_kv.reshape(N_CHIPS, D_MODEL // N_CHIPS, 2 * KV_WIDTH),
        "b_w_in": d_b_w_in,
        "b_w_out": d_b_w_out.reshape(N_CHIPS, B_WIDTH // N_CHIPS, D_MODEL),
    }
    small = {
        "a_ws": d_ws, "a_bs": d_bs_t.T, "a_norm_g": d_a_g, "a_ln_g": d_ln_g, "a_ln_b": d_ln_b,
        "kv_norm_g": d_kv_g, "b_kv": d_b_kv, "b_norm_g": d_b_g, "b_bq": d_bq,
        "b_sinks": d_sinks[0:1, :N_Q_HEADS], "final_norm_g": d_final_g,
    }
    return loss_blk, dx, big, small


def _place():
    x, y, c = lax.axis_index("x"), lax.axis_index("y"), lax.axis_index("c")
    return x, y, c, [(1 - x, y), (x, 1 - y), (1 - x, 1 - y)]


HBM = pl.BlockSpec(memory_space=pl.ANY)


def _gather_shards(arrs):
    n = len(arrs)

    def body(*refs):
        ins, outs = refs[:n], refs[n:2 * n]
        send_ici, recv_ici, send_d2d, recv_d2d, local_sem = refs[2 * n:]
        x, y, c, chips = _place()
        me = 2 * x + y
        sibling = (x, y, 1 - c)
        local = [pltpu.make_async_copy(ins[a], outs[a].at[me], local_sem.at[a]) for a in range(n)]
        for cp in local:
            cp.start()

        def rows(a, half):
            hr = arrs[a].shape[0] // 2
            return pl.ds(half * hr, hr)

        def ici(a, j, src_chip, to):
            return pltpu.make_async_remote_copy(
                src_ref=ins[a].at[rows(a, c)], dst_ref=outs[a].at[src_chip, rows(a, c)],
                send_sem=send_ici.at[a, j], recv_sem=recv_ici.at[a, j], device_id=to, device_id_type=MESH)

        def d2d(a, j, chip, half):
            blk = outs[a].at[chip, rows(a, half)]
            return pltpu.make_async_remote_copy(
                src_ref=blk, dst_ref=blk, send_sem=send_d2d.at[a, j], recv_sem=recv_d2d.at[a, j],
                device_id=sibling, device_id_type=MESH)

        sends = [ici(a, j, me, (*chip, c)) for a in range(n) for j, chip in enumerate(chips)]
        for cp in sends:
            cp.start()
        passes = []
        for a in range(n):
            for j, (px, py) in enumerate(chips):
                ici(a, j, 2 * px + py, (px, py, c)).wait_recv()
                cp = d2d(a, j, 2 * px + py, c)
                cp.start()
                passes.append(cp)
        for a in range(n):
            for j, (px, py) in enumerate(chips):
                d2d(a, j, 2 * px + py, 1 - c).wait_recv()
        for cp in sends + passes:
            cp.wait_send()
        for cp in local:
            cp.wait()

    return pl.pallas_call(
        body, in_specs=[HBM] * n, out_specs=[HBM] * n,
        out_shape=[jax.ShapeDtypeStruct((N_CHIPS,) + a.shape, a.dtype) for a in arrs],
        scratch_shapes=[pltpu.SemaphoreType.DMA((n, 3)), pltpu.SemaphoreType.DMA((n, 3)),
                        pltpu.SemaphoreType.DMA((n, 3)), pltpu.SemaphoreType.DMA((n, 3)),
                        pltpu.SemaphoreType.DMA((n,))],
        name="gather_weights",
    )(*arrs)


def _exchange_halves(grads):
    n = len(grads)

    def body(*refs):
        ins, outs = refs[:n], refs[n:2 * n]
        send_sem, recv_sem = refs[2 * n:]
        x, y, c, _ = _place()
        cps = []
        for a in range(n):
            hr = grads[a].shape[1] // 2
            cp = pltpu.make_async_remote_copy(
                src_ref=ins[a].at[:, pl.ds((1 - c) * hr, hr), :], dst_ref=outs[a],
                send_sem=send_sem.at[a], recv_sem=recv_sem.at[a], device_id=(x, y, 1 - c), device_id_type=MESH)
            cp.start()
            cps.append(cp)
        for cp in cps:
            cp.wait()

    return pl.pallas_call(
        body, in_specs=[HBM] * n, out_specs=[HBM] * n,
        out_shape=[jax.ShapeDtypeStruct((g.shape[0], g.shape[1] // 2, g.shape[2]), g.dtype) for g in grads],
        scratch_shapes=[pltpu.SemaphoreType.DMA((n,)), pltpu.SemaphoreType.DMA((n,))],
        name="exchange_halves",
    )(*grads)


def _relations():
    return [(r >> 2 & 1, r >> 1 & 1, r & 1) for r in range(1, 8)]


def _scatter_partials(chip_sums, small):
    n = len(chip_sums)

    def body(*refs):
        ins, small_in = refs[:n], refs[n]
        outs, small_out = refs[n + 1:2 * n + 1], refs[2 * n + 1]
        send_sem, recv_sem, ssend_sem, srecv_sem = refs[2 * n + 2:]
        x, y, c, chips = _place()
        cps = []
        for a in range(n):
            for j, (px, py) in enumerate(chips):
                cp = pltpu.make_async_remote_copy(
                    src_ref=ins[a].at[2 * px + py], dst_ref=outs[a].at[j],
                    send_sem=send_sem.at[a, j], recv_sem=recv_sem.at[a, j], device_id=(px, py, c), device_id_type=MESH)
                cp.start()
                cps.append(cp)
        for r, (fx, fy, fc) in enumerate(_relations(), start=1):
            px, py, pc = x ^ fx, y ^ fy, c ^ fc
            cp = pltpu.make_async_remote_copy(
                src_ref=small_in.at[4 * px + 2 * py + pc], dst_ref=small_out.at[r],
                send_sem=ssend_sem.at[r - 1], recv_sem=srecv_sem.at[r - 1], device_id=(px, py, pc), device_id_type=MESH)
            cp.start()
            cps.append(cp)
        for cp in cps:
            cp.wait()

    return pl.pallas_call(
        body, in_specs=[HBM] * (n + 1), out_specs=[HBM] * (n + 1),
        out_shape=[jax.ShapeDtypeStruct((3,) + t.shape[1:], t.dtype) for t in chip_sums]
        + [jax.ShapeDtypeStruct(small.shape, small.dtype)],
        scratch_shapes=[pltpu.SemaphoreType.DMA((n, 3)), pltpu.SemaphoreType.DMA((n, 3)),
                        pltpu.SemaphoreType.DMA((7,)), pltpu.SemaphoreType.DMA((7,))],
        name="scatter_partials",
    )(*chip_sums, small)


def _share_reduced(halves, small):
    n = len(halves)

    def body(*refs):
        ins, small_in = refs[:n], refs[n]
        outs, small_out = refs[n + 1:2 * n + 1], refs[2 * n + 1]
        send_sem, recv_sem, ssend_sem, srecv_sem = refs[2 * n + 2:]
        x, y, c, _ = _place()
        me = 4 * x + 2 * y + c
        cps = []
        for a in range(n):
            hr = halves[a].shape[0] // 2
            cp = pltpu.make_async_remote_copy(
                src_ref=ins[a].at[pl.ds(c * hr, hr)], dst_ref=outs[a].at[pl.ds(c * hr, hr)],
                send_sem=send_sem.at[a], recv_sem=recv_sem.at[a], device_id=(x, y, 1 - c), device_id_type=MESH)
            cp.start()
            cps.append(cp)
        for r, (fx, fy, fc) in enumerate(_relations(), start=1):
            cp = pltpu.make_async_remote_copy(
                src_ref=small_in.at[me], dst_ref=small_out.at[me],
                send_sem=ssend_sem.at[r - 1], recv_sem=srecv_sem.at[r - 1],
                device_id=(x ^ fx, y ^ fy, c ^ fc), device_id_type=MESH)
            cp.start()
            cps.append(cp)
        for a in range(n):
            hr = halves[a].shape[0] // 2
            other = outs[a].at[pl.ds((1 - c) * hr, hr)]
            pltpu.make_async_remote_copy(
                src_ref=other, dst_ref=other, send_sem=send_sem.at[a], recv_sem=recv_sem.at[a],
                device_id=(x, y, 1 - c), device_id_type=MESH).wait_recv()
        for r, (fx, fy, fc) in enumerate(_relations(), start=1):
            theirs = small_out.at[4 * (x ^ fx) + 2 * (y ^ fy) + (c ^ fc)]
            pltpu.make_async_remote_copy(
                src_ref=theirs, dst_ref=theirs, send_sem=ssend_sem.at[r - 1], recv_sem=srecv_sem.at[r - 1],
                device_id=(x ^ fx, y ^ fy, c ^ fc), device_id_type=MESH).wait_recv()
        for cp in cps:
            cp.wait_send()

    return pl.pallas_call(
        body, in_specs=[HBM] * (n + 1), out_specs=[HBM] * (n + 1),
        out_shape=[jax.ShapeDtypeStruct(h.shape, h.dtype) for h in halves]
        + [jax.ShapeDtypeStruct(small.shape, small.dtype)],
        input_output_aliases={i: i for i in range(n + 1)},
        scratch_shapes=[pltpu.SemaphoreType.DMA((n,)), pltpu.SemaphoreType.DMA((n,)),
                        pltpu.SemaphoreType.DMA((7,)), pltpu.SemaphoreType.DMA((7,))],
        name="share_reduced",
    )(*halves, small)


def _col_tile(cols):
    return cols if cols <= 2048 else 512


def _add_sibling(grad, recv, core, *, name):
    k, r, c = grad.shape
    hr = r // 2
    tr = min(hr, 256)
    tc = _col_tile(c)
    nrb = hr // tr

    def body(core_ref, g_ref, r_ref, o_ref):
        o_ref[...] = (g_ref[...] + r_ref[...]).astype(BF16)

    return pl.pallas_call(
        body,
        grid_spec=pltpu.PrefetchScalarGridSpec(
            num_scalar_prefetch=1, grid=(k, nrb, c // tc),
            in_specs=[pl.BlockSpec((None, tr, tc), lambda kk, i, j, core: (kk, core[0] * nrb + i, j)),
                      pl.BlockSpec((None, tr, tc), lambda kk, i, j, core: (kk, i, j))],
            out_specs=pl.BlockSpec((None, tr, tc), lambda kk, i, j, core: (kk, i, j))),
        out_shape=jax.ShapeDtypeStruct((k, hr, c), BF16), name=name, compiler_params=_cparams(),
    )(core, grad, recv)


def _sum_chips(grad, from_sibling, recv, place, *, name):
    _, hr, c = from_sibling.shape
    tr = min(hr, 256)
    tc = _col_tile(c)
    nrb = hr // tr

    def body(place_ref, g_ref, s_ref, r0_ref, r1_ref, r2_ref, o_ref):
        own = g_ref[...] + s_ref[...]
        o_ref[...] = ((own + r0_ref[...].astype(F32)) + r1_ref[...].astype(F32)) + r2_ref[...].astype(F32)

    def rspec(j):
        return pl.BlockSpec((None, tr, tc), lambda i, jj, place: (j, i, jj))

    return pl.pallas_call(
        body,
        grid_spec=pltpu.PrefetchScalarGridSpec(
            num_scalar_prefetch=1, grid=(nrb, c // tc),
            in_specs=[pl.BlockSpec((None, tr, tc), lambda i, jj, place: (place[0], place[1] * nrb + i, jj)),
                      pl.BlockSpec((None, tr, tc), lambda i, jj, place: (place[0], i, jj)),
                      rspec(0), rspec(1), rspec(2)],
            out_specs=pl.BlockSpec((tr, tc), lambda i, jj, place: (place[1] * nrb + i, jj))),
        out_shape=jax.ShapeDtypeStruct((2 * hr, c), F32), name=name, compiler_params=_cparams(),
    )(place, grad, from_sibling, recv, recv, recv)


def _sum_small(small, recv, place):
    _, sr, _ = small.shape

    def body(place_ref, own_ref, r_ref, o_ref):
        acc = own_ref[...]
        for r in range(1, 8):
            acc = acc + r_ref[r]
        o_ref[...] = acc

    return pl.pallas_call(
        body,
        grid_spec=pltpu.PrefetchScalarGridSpec(
            num_scalar_prefetch=1, grid=(1,),
            in_specs=[pl.BlockSpec((None, sr, 128), lambda i, place: (place[2], 0, 0)),
                      pl.BlockSpec((8, sr, 128), lambda i, place: (0, 0, 0))],
            out_specs=pl.BlockSpec((None, sr, 128), lambda i, place: (place[2], 0, 0))),
        out_shape=jax.ShapeDtypeStruct(small.shape, F32), name="sum_small", compiler_params=_cparams(),
    )(place, small, recv)


def _adamw(w, g, m, v, *, name):
    r, c = w.shape
    tr = 256 if r % 256 == 0 else r
    tc = _col_tile(c)
    bc1 = 1.0 - ADAM_B1 ** ADAM_STEP
    bc2 = 1.0 - ADAM_B2 ** ADAM_STEP

    def body(w_ref, g_ref, m_ref, v_ref, d_ref, nm_ref, nv_ref):
        gv = g_ref[...]
        nm = ADAM_B1 * m_ref[...] + (1.0 - ADAM_B1) * gv
        nv = ADAM_B2 * v_ref[...] + (1.0 - ADAM_B2) * (gv * gv)
        d_ref[...] = -ADAM_LR * ((nm / bc1) / (jnp.sqrt(nv / bc2) + ADAM_EPS) + ADAM_WD * w_ref[...])
        nm_ref[...] = nm
        nv_ref[...] = nv

    spec = pl.BlockSpec((tr, tc), lambda i, j: (i, j))
    return pl.pallas_call(
        body, grid=(r // tr, c // tc), in_specs=[spec] * 4, out_specs=[spec] * 3,
        out_shape=[jax.ShapeDtypeStruct((r, c), F32)] * 3, name=name, compiler_params=_cparams(),
    )(w, g, m, v)


SMALL_ORDER = ["a_ws", "a_bs", "a_norm_g", "a_ln_g", "a_ln_b", "kv_norm_g", "b_kv", "b_norm_g", "b_bq",
               "b_sinks", "final_norm_g"]
SHARDED_SMALL = {"a_norm_g", "a_ln_g", "a_ln_b"}


PACK_TILE = 8 * 128


def _rows128(a):
    flat = a.reshape(-1)
    return jnp.pad(flat, (0, (-flat.shape[0]) % PACK_TILE)).reshape(-1, 128)


def _pack_rows(parts, multiple):
    rows = [_rows128(p) for p in parts]
    total = sum(r.shape[0] for r in rows)
    pad = (-total) % multiple
    if pad:
        rows.append(jnp.zeros((pad, 128), rows[0].dtype))
    return jnp.concatenate(rows, axis=0)


def _unpack_rows(packed, shapes):
    out, row = [], 0
    for shp in shapes:
        size = math.prod(shp)
        nrow = -(-size // PACK_TILE) * 8
        out.append(packed[row:row + nrow].reshape(-1)[:size].reshape(shp))
        row += nrow
    return out


WEIGHTS = ["a_norm_g", "a_w_in", "a_ln_g", "a_ln_b", "a_ws", "a_bs", "a_w_out", "kv_norm_g", "w_kv", "b_kv",
           "b_norm_g", "b_w_in", "b_bq", "b_sinks", "b_w_out", "final_norm_g"]
BIG = ["a_w_in", "a_w_out", "w_kv", "b_w_in", "b_w_out"]


def _step(x, loss_target, p, m, v):
    xi, yi, ci = lax.axis_index("x"), lax.axis_index("y"), lax.axis_index("c")
    chip = 2 * xi + yi
    device = 4 * xi + 2 * yi + ci
    core = jnp.reshape(ci, (1,)).astype(jnp.int32)
    place = jnp.stack([chip, ci, device]).astype(jnp.int32)

    shard2d = {n: p[n].reshape(p[n].shape[-2:]) for n in BIG}
    vec_shapes = [p[n].shape for n in ("a_norm_g", "a_ln_g", "a_ln_b")]
    vec_pack = _pack_rows([p["a_norm_g"], p["a_ln_g"], p["a_ln_b"]], 16)
    gathered = _gather_shards([shard2d[n].astype(BF16) for n in BIG] + [vec_pack])
    full = dict(zip(BIG, gathered[:len(BIG)]))
    vecs = [_unpack_rows(gathered[-1][k], vec_shapes) for k in range(N_CHIPS)]
    w = {
        "a_w_in": full["a_w_in"], "b_w_in": full["b_w_in"],
        "a_w_out": full["a_w_out"].reshape(A_WIDTH, D_MODEL),
        "w_kv": full["w_kv"].reshape(D_MODEL, 2 * KV_WIDTH),
        "b_w_out": full["b_w_out"].reshape(B_WIDTH, D_MODEL),
        "a_norm_g": jnp.concatenate([vk[0] for vk in vecs], axis=-1),
        "a_ln_g": jnp.concatenate([vk[1] for vk in vecs], axis=-1),
        "a_ln_b": jnp.concatenate([vk[2] for vk in vecs], axis=-1),
        "a_ws": p["a_ws"][0], "a_bs": p["a_bs"][0],
        "kv_norm_g": p["kv_norm_g"].reshape(1, -1), "b_kv": p["b_kv"].reshape(1, -1),
        "b_norm_g": p["b_norm_g"], "b_bq": p["b_bq"], "b_sinks": p["b_sinks"],
        "final_norm_g": p["final_norm_g"].reshape(1, -1),
    }

    loss_blk, dx, big, small = _local_step(x[0], loss_target[0], w)

    small_shapes = [small[n].shape for n in SMALL_ORDER] + [(1, 1)]
    small_pack = _pack_rows([small[n] for n in SMALL_ORDER] + [loss_blk[0:1, 0:1]], 64)
    seg = small_pack.shape[0] // 8
    small_pack = small_pack.reshape(8, seg, 128)
    partial = [big[n] for n in BIG]
    from_sibling = _exchange_halves(partial)
    chip_sums = [_add_sibling(g, r, core, name="add_sibling_" + n) for g, r, n in zip(partial, from_sibling, BIG)]
    arrived = _scatter_partials(chip_sums, small_pack)
    halves = [_sum_chips(g, fs, r, place, name="sum_chips_" + n)
              for g, fs, r, n in zip(partial, from_sibling, arrived[:-1], BIG)]
    small_mine = _sum_small(small_pack, arrived[-1], place)
    shared = _share_reduced(halves, small_mine)
    grad_big = dict(zip(BIG, shared[:-1]))
    small_full = _unpack_rows(shared[-1].reshape(8 * seg, 128), small_shapes)
    loss = small_full[-1].reshape(())
    grads = {}
    for n, gfull in zip(SMALL_ORDER, small_full):
        if n in SHARDED_SMALL:
            width = p[n].shape[-1]
            gfull = lax.dynamic_slice_in_dim(gfull, chip * width, width, axis=-1)
        grads[n] = gfull.reshape(p[n].shape)
    for n in BIG:
        grads[n] = grad_big[n].reshape(p[n].shape)

    delta, new_m, new_v = {}, {}, {}
    for n in BIG:
        d, nm, nv = _adamw(shard2d[n], grad_big[n], m[n].reshape(shard2d[n].shape), v[n].reshape(shard2d[n].shape),
                           name="adamw_" + n)
        delta[n], new_m[n], new_v[n] = d.reshape(p[n].shape), nm.reshape(p[n].shape), nv.reshape(p[n].shape)
    shapes = [p[n].shape for n in SMALL_ORDER]
    packs = [_pack_rows([src[n] for n in SMALL_ORDER], 8) for src in (p, grads, m, v)]
    outs = _adamw(*packs, name="adamw_small")
    for res, packed in zip((delta, new_m, new_v), outs):
        for n, val in zip(SMALL_ORDER, _unpack_rows(packed, shapes)):
            res[n] = val

    return (loss, dx[None], *[grads[n] for n in WEIGHTS], *[delta[n] for n in WEIGHTS],
            *[new_m[n] for n in WEIGHTS], *[new_v[n] for n in WEIGHTS])


def kernel(x, a_norm_g, a_w_in, a_ln_g, a_ln_b, a_ws, a_bs, a_w_out, kv_norm_g, w_kv, b_kv, b_norm_g, b_w_in, b_bq, b_sinks, b_w_out, final_norm_g, loss_target, m_a_norm_g, m_a_w_in, m_a_ln_g, m_a_ln_b, m_a_ws, m_a_bs, m_a_w_out, m_kv_norm_g, m_w_kv, m_b_kv, m_b_norm_g, m_b_w_in, m_b_bq, m_b_sinks, m_b_w_out, m_final_norm_g, v_a_norm_g, v_a_w_in, v_a_ln_g, v_a_ln_b, v_a_ws, v_a_bs, v_a_w_out, v_kv_norm_g, v_w_kv, v_b_kv, v_b_norm_g, v_b_w_in, v_b_bq, v_b_sinks, v_b_w_out, v_final_norm_g):
    p = dict(a_norm_g=a_norm_g, a_w_in=a_w_in, a_ln_g=a_ln_g, a_ln_b=a_ln_b, a_ws=a_ws, a_bs=a_bs, a_w_out=a_w_out,
             kv_norm_g=kv_norm_g, w_kv=w_kv, b_kv=b_kv, b_norm_g=b_norm_g, b_w_in=b_w_in, b_bq=b_bq, b_sinks=b_sinks,
             b_w_out=b_w_out, final_norm_g=final_norm_g)
    m = dict(a_norm_g=m_a_norm_g, a_w_in=m_a_w_in, a_ln_g=m_a_ln_g, a_ln_b=m_a_ln_b, a_ws=m_a_ws, a_bs=m_a_bs,
             a_w_out=m_a_w_out, kv_norm_g=m_kv_norm_g, w_kv=m_w_kv, b_kv=m_b_kv, b_norm_g=m_b_norm_g, b_w_in=m_b_w_in,
             b_bq=m_b_bq, b_sinks=m_b_sinks, b_w_out=m_b_w_out, final_norm_g=m_final_norm_g)
    v = dict(a_norm_g=v_a_norm_g, a_w_in=v_a_w_in, a_ln_g=v_a_ln_g, a_ln_b=v_a_ln_b, a_ws=v_a_ws, a_bs=v_a_bs,
             a_w_out=v_a_w_out, kv_norm_g=v_kv_norm_g, w_kv=v_w_kv, b_kv=v_b_kv, b_norm_g=v_b_norm_g, b_w_in=v_b_w_in,
             b_bq=v_b_bq, b_sinks=v_b_sinks, b_w_out=v_b_w_out, final_norm_g=v_final_norm_g)
    return _step(x, loss_target, p, m, v)
```

```python
import functools
import math

import jax
import jax.numpy as jnp
from jax import lax
from jax.experimental import pallas as pl
from jax.experimental.pallas import tpu as pltpu

F32 = jnp.float32
BF16 = jnp.bfloat16

D_MODEL = 1024
CHUNK = 128
A_WIDTH = 2048
A_GROUPS = 16
HEAD_DIM = 64
N_Q_HEADS = 16
N_KV_HEADS = 2
Q_PER_KV = 8
B_WIDTH = 1024
KV_WIDTH = 128
ROPE_THETA = 10000.0
EPS = 1e-5
N_CHIPS = 4

ADAM_LR = 0.001
ADAM_B1 = 0.9
ADAM_B2 = 0.999
ADAM_EPS = 1e-08
ADAM_WD = 0.01
ADAM_STEP = 10

VMEM_LIMIT = 48 * 1024 * 1024
MESH = pl.DeviceIdType.MESH
NEG_BIG = -1e30
HBM = pl.BlockSpec(memory_space=pl.ANY)

NN = (((1,), (0,)), ((), ()))
NT = (((1,), (1,)), ((), ()))
TN = (((0,), (0,)), ((), ()))


def _cparams(**kw):
    return pltpu.CompilerParams(vmem_limit_bytes=VMEM_LIMIT, **kw)


class _Side:
    def __init__(self, ins, out_shapes, sems, start, finish, aliases=None):
        self.ins, self.out_shapes, self.sems = list(ins), list(out_shapes), list(sems)
        self.start, self.finish = start, finish
        self.aliases = dict(aliases or {})


def _join(sides):
    sides = [s for s in sides if s is not None]
    if not sides:
        return None
    offs, i, o, m = [], 0, 0, 0
    for s in sides:
        offs.append((i, o, m))
        i, o, m = i + len(s.ins), o + len(s.out_shapes), m + len(s.sems)

    def run(which):
        def go(ins, outs, sems):
            for s, (a, b, c) in zip(sides, offs):
                getattr(s, which)(ins[a:a + len(s.ins)], outs[b:b + len(s.out_shapes)], sems[c:c + len(s.sems)])
        return go

    aliases = {}
    for s, (a, b, _) in zip(sides, offs):
        aliases.update({a + k: b + v for k, v in s.aliases.items()})
    return _Side([x for s in sides for x in s.ins], [x for s in sides for x in s.out_shapes],
                 [x for s in sides for x in s.sems], run("start"), run("finish"), aliases)


def _split(side_outs, sides):
    out, pos = [], 0
    for s in sides:
        out.append(list(side_outs[pos:pos + len(s.out_shapes)]))
        pos += len(s.out_shapes)
    return out


def _call(body, *, grid, in_specs, out_specs, out_shape, args, name, scratch=(), side=None):
    in_specs, out_specs, out_shape, scratch = list(in_specs), list(out_specs), list(out_shape), list(scratch)
    if side is None:
        res = pl.pallas_call(body, grid=grid, in_specs=in_specs, out_specs=out_specs, out_shape=out_shape,
                             scratch_shapes=scratch, name=name, compiler_params=_cparams())(*args)
        return list(res), []
    n_in, n_out, n_sc = len(in_specs), len(out_specs), len(scratch)
    s_in, s_out = len(side.ins), len(side.out_shapes)

    def wrapped(*refs):
        ins, refs = refs[:n_in], refs[n_in:]
        side_ins, refs = refs[:s_in], refs[s_in:]
        outs, refs = refs[:n_out], refs[n_out:]
        side_outs, refs = refs[:s_out], refs[s_out:]
        scr, side_sems = refs[:n_sc], refs[n_sc:]
        ids = [pl.program_id(a) for a in range(len(grid))]
        first = functools.reduce(jnp.logical_and, [i == 0 for i in ids])
        last = functools.reduce(jnp.logical_and, [i == g - 1 for i, g in zip(ids, grid)])

        @pl.when(first)
        def _():
            side.start(side_ins, side_outs, side_sems)

        body(*ins, *outs, *scr)

        @pl.when(last)
        def _():
            side.finish(side_ins, side_outs, side_sems)

    res = pl.pallas_call(
        wrapped, grid=grid, in_specs=in_specs + [HBM] * s_in, out_specs=out_specs + [HBM] * s_out,
        out_shape=out_shape + side.out_shapes, scratch_shapes=scratch + side.sems,
        input_output_aliases={n_in + k: n_out + v for k, v in side.aliases.items()},
        name=name, compiler_params=_cparams(),
    )(*args, *side.ins)
    return list(res[:n_out]), list(res[n_out:])


def _comm_call(side, name):
    s_in, s_out = len(side.ins), len(side.out_shapes)

    def body(*refs):
        ins, outs, sems = refs[:s_in], refs[s_in:s_in + s_out], refs[s_in + s_out:]
        side.start(ins, outs, sems)
        side.finish(ins, outs, sems)

    return list(pl.pallas_call(
        body, in_specs=[HBM] * s_in, out_specs=[HBM] * s_out, out_shape=side.out_shapes, scratch_shapes=side.sems,
        input_output_aliases=side.aliases, name=name,
    )(*side.ins))


def _matmul(a, b, *, dims, grid, a_spec, b_spec, o_spec, out_shape, name, acc_axis=None,
            residual=None, r_spec=None, side=None):
    has_res = residual is not None

    def body(*refs):
        if has_res:
            a_ref, b_ref, r_ref, o_ref = refs
        else:
            a_ref, b_ref, o_ref = refs
        part = lax.dot_general(a_ref[...], b_ref[...], dims, preferred_element_type=F32)
        if acc_axis is None:
            if has_res:
                part = part + r_ref[...]
            o_ref[...] = part.astype(o_ref.dtype)
        else:
            k = pl.program_id(acc_axis)

            @pl.when(k == 0)
            def _():
                o_ref[...] = part

            @pl.when(k > 0)
            def _():
                o_ref[...] += part

    in_specs = [a_spec, b_spec] + ([r_spec] if has_res else [])
    args = (a, b) + ((residual,) if has_res else ())
    (out,), side_outs = _call(body, grid=grid, in_specs=in_specs, out_specs=[o_spec], out_shape=[out_shape],
                              args=args, name=name, side=side)
    return (out, side_outs) if side is not None else out


def _row_tile(s, want):
    return min(s, want)


def _mm_nn(a, b, *, name, tn, out_dtype=F32, residual=None, tm=512, side=None):
    s, k = a.shape
    tm = _row_tile(s, tm)
    if b.ndim == 3:
        nsh, _, nc = b.shape
        npb = nc // tn
        n = nsh * nc
        b_spec = pl.BlockSpec((None, k, tn), lambda i, j: (j // npb, 0, j % npb))
    else:
        n = b.shape[1]
        b_spec = pl.BlockSpec((k, tn), lambda i, j: (0, j))
    return _matmul(
        a, b, dims=NN, grid=(s // tm, n // tn),
        a_spec=pl.BlockSpec((tm, k), lambda i, j: (i, 0)), b_spec=b_spec,
        o_spec=pl.BlockSpec((tm, tn), lambda i, j: (i, j)),
        out_shape=jax.ShapeDtypeStruct((s, n), out_dtype), name=name, side=side,
        residual=residual, r_spec=pl.BlockSpec((tm, tn), lambda i, j: (i, j)) if residual is not None else None)


def _mm_nt(a, b, *, name, tk, tn=None, tm=512, out_dtype=F32, side=None):
    s, k = a.shape
    tm = _row_tile(s, tm)
    if b.ndim == 3:
        nsh, n, kc = b.shape
        npb = kc // tk
        return _matmul(
            a, b, dims=NT, grid=(s // tm, k // tk), acc_axis=1,
            a_spec=pl.BlockSpec((tm, tk), lambda i, kk: (i, kk)),
            b_spec=pl.BlockSpec((None, n, tk), lambda i, kk: (kk // npb, 0, kk % npb)),
            o_spec=pl.BlockSpec((tm, n), lambda i, kk: (i, 0)),
            out_shape=jax.ShapeDtypeStruct((s, n), F32), name=name, side=side)
    n = b.shape[0]
    tn = n if tn is None else tn
    assert tk == k
    return _matmul(
        a, b, dims=NT, grid=(s // tm, n // tn),
        a_spec=pl.BlockSpec((tm, k), lambda i, j: (i, 0)),
        b_spec=pl.BlockSpec((tn, k), lambda i, j: (j, 0)),
        o_spec=pl.BlockSpec((tm, tn), lambda i, j: (i, j)),
        out_shape=jax.ShapeDtypeStruct((s, n), out_dtype), name=name, side=side)


def _mm_tn(a, b, *, name, tm, tn, tk=512, shards=None, side=None):
    s, m = a.shape
    n = b.shape[1]
    tk = _row_tile(s, tk)
    if shards is None:
        o_spec = pl.BlockSpec((tm, tn), lambda i, j, kk: (i, j))
        out_shape = jax.ShapeDtypeStruct((m, n), F32)
    else:
        assert tm == m
        nc = n // shards
        npb = nc // tn
        o_spec = pl.BlockSpec((None, m, tn), lambda i, j, kk: (j // npb, 0, j % npb))
        out_shape = jax.ShapeDtypeStruct((shards, m, nc), F32)
    return _matmul(
        a, b, dims=TN, grid=(m // tm, n // tn, s // tk), acc_axis=2,
        a_spec=pl.BlockSpec((tk, tm), lambda i, j, kk: (kk, i)),
        b_spec=pl.BlockSpec((tk, tn), lambda i, j, kk: (kk, j)),
        o_spec=o_spec, out_shape=out_shape, name=name, side=side)


def _rstd(x):
    return lax.rsqrt(jnp.mean(x * x, axis=-1, keepdims=True) + EPS)


def _rms_fwd(x, gains, *, name, tr=256):
    s, d = x.shape
    tr = _row_tile(s, tr)
    ng = len(gains)

    def body(*refs):
        xv = refs[0][...]
        xh = xv * _rstd(xv)
        for t in range(ng):
            refs[1 + ng + t][...] = (xh * refs[1 + t][...]).astype(BF16)

    row = pl.BlockSpec((tr, d), lambda i: (i, 0))
    vec = pl.BlockSpec((1, d), lambda i: (0, 0))
    outs, _ = _call(body, grid=(s // tr,), in_specs=[row] + [vec] * ng, out_specs=[row] * ng,
                    out_shape=[jax.ShapeDtypeStruct((s, d), BF16)] * ng, args=(x, *gains), name=name)
    return outs


def _rms_bwd(x, dns, gains, dres, *, name, tr=256):
    s, d = x.shape
    tr = _row_tile(s, tr)
    ng = len(gains)

    def body(*refs):
        x_ref = refs[0]
        dn_refs = refs[1:1 + ng]
        g_refs = refs[1 + ng:1 + 2 * ng]
        dres_ref = refs[1 + 2 * ng]
        dx_ref, dxb_ref = refs[2 + 2 * ng], refs[3 + 2 * ng]
        dg_refs = refs[4 + 2 * ng:]
        i = pl.program_id(0)
        xv = x_ref[...]
        r = _rstd(xv)
        xh = xv * r
        acc = jnp.zeros_like(xv)
        for t in range(ng):
            dn = dn_refs[t][...]
            acc = acc + dn * g_refs[t][...]
            dgt = jnp.sum(dn * xh, axis=0, keepdims=True)

            @pl.when(i == 0)
            def _(t=t, dgt=dgt):
                dg_refs[t][...] = dgt

            @pl.when(i > 0)
            def _(t=t, dgt=dgt):
                dg_refs[t][...] += dgt

        dx = dres_ref[...] + r * (acc - xh * jnp.mean(acc * xh, axis=-1, keepdims=True))
        dx_ref[...] = dx
        dxb_ref[...] = dx.astype(BF16)

    row = pl.BlockSpec((tr, d), lambda i: (i, 0))
    vec = pl.BlockSpec((1, d), lambda i: (0, 0))
    outs, _ = _call(
        body, grid=(s // tr,), in_specs=[row] + [row] * ng + [vec] * ng + [row],
        out_specs=[row, row] + [vec] * ng,
        out_shape=[jax.ShapeDtypeStruct((s, d), F32), jax.ShapeDtypeStruct((s, d), BF16)]
        + [jax.ShapeDtypeStruct((1, d), F32)] * ng,
        args=(x, *dns, *gains, dres), name=name)
    return outs[0], outs[1], outs[2:]


def _loss_head(h, tgt, gain, *, tr=256):
    s, d = h.shape
    tr = _row_tile(s, tr)

    def body(h_ref, t_ref, g_ref, loss_ref, dh_ref, dhb_ref, dg_ref):
        i = pl.program_id(0)
        hv = h_ref[...]
        g = g_ref[...]
        r = _rstd(hv)
        xh = hv * r
        diff = xh * g - t_ref[...]
        part = 0.5 / d * jnp.sum(jnp.sum(diff * diff, axis=-1, keepdims=True), axis=0, keepdims=True)
        dout = diff * (1.0 / d)
        a = dout * g
        dh = r * (a - xh * jnp.mean(a * xh, axis=-1, keepdims=True))
        dh_ref[...] = dh
        dhb_ref[...] = dh.astype(BF16)
        dgt = jnp.sum(dout * xh, axis=0, keepdims=True)
        lpart = jnp.broadcast_to(part, (8, 128))

        @pl.when(i == 0)
        def _():
            dg_ref[...] = dgt
            loss_ref[...] = lpart

        @pl.when(i > 0)
        def _():
            dg_ref[...] += dgt
            loss_ref[...] += lpart

    row = pl.BlockSpec((tr, d), lambda i: (i, 0))
    vec = pl.BlockSpec((1, d), lambda i: (0, 0))
    outs, _ = _call(
        body, grid=(s // tr,), in_specs=[row, row, vec],
        out_specs=[pl.BlockSpec((8, 128), lambda i: (0, 0)), row, row, vec],
        out_shape=[jax.ShapeDtypeStruct((8, 128), F32), jax.ShapeDtypeStruct((s, d), F32),
                   jax.ShapeDtypeStruct((s, d), BF16), jax.ShapeDtypeStruct((1, d), F32)],
        args=(h, tgt, gain), name="loss_head")
    return outs


def _causal_mask(transposed=False):
    row = lax.broadcasted_iota(jnp.int32, (CHUNK, CHUNK), 0)
    col = lax.broadcasted_iota(jnp.int32, (CHUNK, CHUNK), 1)
    return col >= row if transposed else col <= row


def _silu_parts(g):
    sg = jax.nn.sigmoid(g)
    return g * sg, sg * (1.0 + g * (1.0 - sg))


def _gate_fwd(z, ln_g, ln_b, ws, bs_t, *, tr=256):
    s = z.shape[0]
    tr = _row_tile(s, tr)
    w = A_WIDTH

    def body(u_ref, v_ref, g_ref, lg_ref, lb_ref, ws_ref, bst_ref, y_ref):
        v = v_ref[...].astype(F32)
        mu = jnp.mean(v, axis=-1, keepdims=True)
        xc = v - mu
        rs = lax.rsqrt(jnp.mean(xc * xc, axis=-1, keepdims=True) + EPS)
        vln = (xc * rs * lg_ref[...] + lb_ref[...]).astype(BF16)
        mask = _causal_mask()
        for grp in range(A_GROUPS):
            cols = slice(grp * CHUNK, (grp + 1) * CHUNK)
            wsm = jnp.where(mask, ws_ref[grp], 0.0).astype(BF16)
            bcol = bst_ref[:, grp:grp + 1]
            for ci in range(tr // CHUNK):
                rows = slice(ci * CHUNK, (ci + 1) * CHUNK)
                sv = jnp.dot(wsm, vln[rows, cols], preferred_element_type=F32) + bcol
                gv = g_ref[rows, cols].astype(F32)
                y_ref[rows, cols] = (u_ref[rows, cols].astype(F32) * sv * (gv * jax.nn.sigmoid(gv))).astype(BF16)

    vec = pl.BlockSpec((1, w), lambda i: (0, 0))
    (y,), _ = _call(
        body, grid=(s // tr,),
        in_specs=[pl.BlockSpec((tr, w), lambda i: (i, 0)), pl.BlockSpec((tr, w), lambda i: (i, 1)),
                  pl.BlockSpec((tr, w), lambda i: (i, 2)), vec, vec,
                  pl.BlockSpec((A_GROUPS, CHUNK, CHUNK), lambda i: (0, 0, 0)),
                  pl.BlockSpec((CHUNK, A_GROUPS), lambda i: (0, 0))],
        out_specs=[pl.BlockSpec((tr, w), lambda i: (i, 0))],
        out_shape=[jax.ShapeDtypeStruct((s, w), BF16)], args=(z, z, z, ln_g, ln_b, ws, bs_t), name="gate_fwd")
    return y


def _gate_bwd(z, dy, ln_g, ln_b, ws, ws_t, bs_t, *, tr=256, side=None):
    s = z.shape[0]
    tr = _row_tile(s, tr)
    w = A_WIDTH
    nsteps = s // tr

    def body(u_ref, v_ref, g_ref, dy_ref, lg_ref, lb_ref, ws_ref, wst_ref, bst_ref,
             dz_ref, dlg_ref, dlb_ref, dws_ref, dbst_ref, dvln_sc, dsv_sc):
        i = pl.program_id(0)

        @pl.when(i == 0)
        def _():
            dws_ref[...] = jnp.zeros_like(dws_ref)
            dsv_sc[...] = jnp.zeros_like(dsv_sc)

        v = v_ref[...].astype(F32)
        mu = jnp.mean(v, axis=-1, keepdims=True)
        xc = v - mu
        rs = lax.rsqrt(jnp.mean(xc * xc, axis=-1, keepdims=True) + EPS)
        xh = xc * rs
        lg = lg_ref[...]
        vln = (xh * lg + lb_ref[...]).astype(BF16)
        mask = _causal_mask()
        mask_t = _causal_mask(transposed=True)
        for grp in range(A_GROUPS):
            cols = slice(grp * CHUNK, (grp + 1) * CHUNK)
            wsm = jnp.where(mask, ws_ref[grp], 0.0).astype(BF16)
            wsm_t = jnp.where(mask_t, wst_ref[grp], 0.0).astype(BF16)
            bcol = bst_ref[:, grp:grp + 1]
            for ci in range(tr // CHUNK):
                rows = slice(ci * CHUNK, (ci + 1) * CHUNK)
                vb = vln[rows, cols]
                sv = jnp.dot(wsm, vb, preferred_element_type=F32) + bcol
                uv = u_ref[rows, cols].astype(F32)
                silu, dsilu = _silu_parts(g_ref[rows, cols].astype(F32))
                dyv = dy_ref[rows, cols].astype(F32)
                dyu = dyv * uv
                dz_ref[rows, cols] = (dyv * sv * silu).astype(BF16)
                dz_ref[rows, 2 * w + grp * CHUNK:2 * w + (grp + 1) * CHUNK] = (dyu * sv * dsilu).astype(BF16)
                dsv = dyu * silu
                dsvb = dsv.astype(BF16)
                dvln_sc[rows, cols] = jnp.dot(wsm_t, dsvb, preferred_element_type=F32)
                dws_ref[grp] += lax.dot_general(dsvb, vb, NT, preferred_element_type=F32)
                dsv_sc[grp] += dsv
        dvln = dvln_sc[...]
        dlg_t = jnp.sum(dvln * xh, axis=0, keepdims=True)
        dlb_t = jnp.sum(dvln, axis=0, keepdims=True)
        a = dvln * lg
        dv = rs * (a - jnp.mean(a, axis=-1, keepdims=True) - xh * jnp.mean(a * xh, axis=-1, keepdims=True))
        dz_ref[:, w:2 * w] = dv.astype(BF16)

        @pl.when(i == 0)
        def _():
            dlg_ref[...] = dlg_t
            dlb_ref[...] = dlb_t

        @pl.when(i > 0)
        def _():
            dlg_ref[...] += dlg_t
            dlb_ref[...] += dlb_t

        @pl.when(i == nsteps - 1)
        def _():
            for grp in range(A_GROUPS):
                dws_ref[grp] = jnp.where(mask, dws_ref[grp], 0.0)
                dbst_ref[:, grp:grp + 1] = jnp.sum(dsv_sc[grp], axis=-1, keepdims=True)

    vec = pl.BlockSpec((1, w), lambda i: (0, 0))
    wsspec = pl.BlockSpec((A_GROUPS, CHUNK, CHUNK), lambda i: (0, 0, 0))
    bsspec = pl.BlockSpec((CHUNK, A_GROUPS), lambda i: (0, 0))
    return _call(
        body, grid=(nsteps,),
        in_specs=[pl.BlockSpec((tr, w), lambda i: (i, 0)), pl.BlockSpec((tr, w), lambda i: (i, 1)),
                  pl.BlockSpec((tr, w), lambda i: (i, 2)), pl.BlockSpec((tr, w), lambda i: (i, 0)),
                  vec, vec, wsspec, wsspec, bsspec],
        out_specs=[pl.BlockSpec((tr, 3 * w), lambda i: (i, 0)), vec, vec, wsspec, bsspec],
        out_shape=[jax.ShapeDtypeStruct((s, 3 * w), BF16), jax.ShapeDtypeStruct((1, w), F32),
                   jax.ShapeDtypeStruct((1, w), F32), jax.ShapeDtypeStruct((A_GROUPS, CHUNK, CHUNK), F32),
                   jax.ShapeDtypeStruct((CHUNK, A_GROUPS), F32)],
        scratch=[pltpu.VMEM((tr, w), F32), pltpu.VMEM((A_GROUPS, CHUNK, CHUNK), F32)],
        args=(z, z, z, dy, ln_g, ln_b, ws, ws_t, bs_t), name="gate_bwd", side=side)


HEADS_PER_BLOCK = 128 // HEAD_DIM
BLOCKS_PER_KV = Q_PER_KV // HEADS_PER_BLOCK
SCALE = HEAD_DIM ** -0.5


def _rope_tables(s):
    inv_freq = ROPE_THETA ** (-jnp.arange(0, HEAD_DIM, 2, dtype=F32) / HEAD_DIM)
    ang = jnp.arange(s, dtype=F32)[:, None] * inv_freq[None, :]
    cos, sin = jnp.cos(ang), jnp.sin(ang)
    cos2 = jnp.concatenate([cos, cos], axis=-1)
    sin2 = jnp.concatenate([-sin, sin], axis=-1)
    return jnp.tile(cos2, (1, 2)), jnp.tile(sin2, (1, 2))


def _swap_halves(x):
    n = x.shape[-1]
    lane = lax.broadcasted_iota(jnp.int32, x.shape, x.ndim - 1)
    first = (lane % HEAD_DIM) < (HEAD_DIM // 2)
    return jnp.where(first, pltpu.roll(x, n - HEAD_DIM // 2, x.ndim - 1), pltpu.roll(x, HEAD_DIM // 2, x.ndim - 1))


def _left_half(rows):
    return lax.broadcasted_iota(jnp.int32, (rows, 128), 1) < HEAD_DIM


def _dup_heads(x):
    left = _left_half(x.shape[0])
    swapped = pltpu.roll(x, HEAD_DIM, 1)
    return jnp.concatenate([jnp.where(left, x, swapped), jnp.where(left, swapped, x)], axis=-1)


def _fold_heads(a):
    b0, b1 = a[:, :128], a[:, 128:]
    f0 = b0 + pltpu.roll(b0, HEAD_DIM, 1)
    f1 = b1 + pltpu.roll(b1, HEAD_DIM, 1)
    return jnp.where(_left_half(a.shape[0]), f0, f1)


def _kv_rope(kv, b_kv, cos, sin, *, tr=512):
    s = kv.shape[0]
    tr = _row_tile(s, tr)

    def body(kv_ref, b_ref, c_ref, s_ref, k_ref, v_ref):
        x = kv_ref[...] + b_ref[...]
        k = x[:, :KV_WIDTH]
        k_ref[...] = _dup_heads(k * c_ref[...] + _swap_halves(k) * s_ref[...]).astype(BF16)
        v_ref[...] = _dup_heads(x[:, KV_WIDTH:]).astype(BF16)

    tab = pl.BlockSpec((tr, KV_WIDTH), lambda i: (i, 0))
    wide = pl.BlockSpec((tr, 2 * KV_WIDTH), lambda i: (i, 0))
    outs, _ = _call(body, grid=(s // tr,),
                    in_specs=[wide, pl.BlockSpec((1, 2 * KV_WIDTH), lambda i: (0, 0)), tab, tab],
                    out_specs=[wide, wide], out_shape=[jax.ShapeDtypeStruct((s, 2 * KV_WIDTH), BF16)] * 2,
                    args=(kv, b_kv, cos, sin), name="kv_rope")
    return outs


def _kv_rope_bwd(dk2, dv2, cos, sin, *, tr=512):
    s = dk2.shape[0]
    tr = _row_tile(s, tr)

    def body(dk_ref, dv_ref, c_ref, s_ref, dkv_ref, db_ref):
        i = pl.program_id(0)
        d = _fold_heads(dk_ref[...])
        dk = d * c_ref[...] + _swap_halves(d * s_ref[...])
        dvv = _fold_heads(dv_ref[...])
        dkv_ref[:, :KV_WIDTH] = dk.astype(BF16)
        dkv_ref[:, KV_WIDTH:] = dvv.astype(BF16)
        sk = jnp.sum(dk, axis=0, keepdims=True)
        sv = jnp.sum(dvv, axis=0, keepdims=True)

        @pl.when(i == 0)
        def _():
            db_ref[:, :KV_WIDTH] = sk
            db_ref[:, KV_WIDTH:] = sv

        @pl.when(i > 0)
        def _():
            db_ref[:, :KV_WIDTH] += sk
            db_ref[:, KV_WIDTH:] += sv

    tab = pl.BlockSpec((tr, KV_WIDTH), lambda i: (i, 0))
    wide = pl.BlockSpec((tr, 2 * KV_WIDTH), lambda i: (i, 0))
    outs, _ = _call(body, grid=(s // tr,), in_specs=[wide, wide, tab, tab],
                    out_specs=[wide, pl.BlockSpec((1, 2 * KV_WIDTH), lambda i: (0, 0))],
                    out_shape=[jax.ShapeDtypeStruct((s, 2 * KV_WIDTH), BF16),
                               jax.ShapeDtypeStruct((1, 2 * KV_WIDTH), F32)],
                    args=(dk2, dv2, cos, sin), name="kv_rope_bwd")
    return outs


def _stacked_mask(i):
    cols = Q_PER_KV * CHUNK
    k = lax.broadcasted_iota(jnp.int32, (2 * CHUNK, cols), 0)
    q = lax.broadcasted_iota(jnp.int32, (2 * CHUNK, cols), 1) & (CHUNK - 1)
    first_valid = jnp.where(i > 0, 0, CHUNK)
    prev = (k < CHUNK) & (k > q) & (k >= first_valid)
    cur = (k >= CHUNK) & (k - CHUNK <= q)
    return prev | cur


def _stack_heads(blocks, left):
    parts = []
    for b in blocks:
        parts.append(jnp.where(left, b, jnp.zeros_like(b)))
        parts.append(jnp.where(left, jnp.zeros_like(b), b))
    return jnp.concatenate(parts, axis=0)


def _unstack_heads(xt):
    top = lax.broadcasted_iota(jnp.int32, (128, CHUNK), 0) < HEAD_DIM
    return [jnp.where(top, xt[:, (2 * b) * CHUNK:(2 * b + 1) * CHUNK], xt[:, (2 * b + 1) * CHUNK:(2 * b + 2) * CHUNK]).T
            for b in range(BLOCKS_PER_KV)]


def _sink_row(sk_ref, kvh):
    return jnp.concatenate([jnp.full((1, CHUNK), sk_ref[0, kvh * Q_PER_KV + r], F32) for r in range(Q_PER_KV)], axis=1)


def _stacked_probs(qs, kd, mask, sink):
    sc = lax.dot_general(kd, qs, NT, preferred_element_type=F32) * SCALE
    sc = jnp.where(mask, sc, NEG_BIG)
    m = jnp.maximum(jnp.max(sc, axis=0, keepdims=True), sink)
    p = jnp.exp(sc - m)
    esink = jnp.exp(sink - m)
    inv = 1.0 / (jnp.sum(p, axis=0, keepdims=True) + esink)
    return p * inv, esink * inv


def _lane_block(b):
    return slice(b * 128, (b + 1) * 128)


def _rope_blocks(zq_ref, bq_ref, cos, sin, kvh):
    out = []
    for b in range(BLOCKS_PER_KV):
        cols = _lane_block(kvh * BLOCKS_PER_KV + b)
        q = zq_ref[:, cols].astype(F32) + bq_ref[:, cols]
        out.append((q * cos + _swap_halves(q) * sin).astype(BF16))
    return out


def _attn_specs():
    qspec = pl.BlockSpec((CHUNK, B_WIDTH), lambda i: (i, 0))
    gspec = pl.BlockSpec((CHUNK, B_WIDTH), lambda i: (i, 1))
    prev = pl.BlockSpec((CHUNK, 2 * KV_WIDTH), lambda i: (jnp.maximum(i - 1, 0), 0))
    cur = pl.BlockSpec((CHUNK, 2 * KV_WIDTH), lambda i: (i, 0))
    tab = pl.BlockSpec((CHUNK, KV_WIDTH), lambda i: (i, 0))
    bq = pl.BlockSpec((1, B_WIDTH), lambda i: (0, 0))
    sinks = pl.BlockSpec(memory_space=pltpu.SMEM)
    return qspec, gspec, prev, cur, tab, bq, sinks


def _attn_fwd(zb, k2, v2, cos, sin, b_bq, sinks):
    s = zb.shape[0]

    def body(zq_ref, zg_ref, kp_ref, kc_ref, vp_ref, vc_ref, c_ref, s_ref, bq_ref, sk_ref, y_ref):
        i = pl.program_id(0)
        cos, sin = c_ref[...], s_ref[...]
        kcat = jnp.concatenate([kp_ref[...], kc_ref[...]], axis=0)
        vcat = jnp.concatenate([vp_ref[...], vc_ref[...]], axis=0)
        mask = _stacked_mask(i)
        left = _left_half(CHUNK)
        for kvh in range(N_KV_HEADS):
            qs = _stack_heads(_rope_blocks(zq_ref, bq_ref, cos, sin, kvh), left)
            p, _ = _stacked_probs(qs, kcat[:, _lane_block(kvh)], mask, _sink_row(sk_ref, kvh))
            ot = lax.dot_general(vcat[:, _lane_block(kvh)], p.astype(BF16), TN, preferred_element_type=F32)
            for b, ob in enumerate(_unstack_heads(ot)):
                cols = _lane_block(kvh * BLOCKS_PER_KV + b)
                gv = zg_ref[:, cols].astype(F32)
                y_ref[:, cols] = (ob * (gv * jax.nn.sigmoid(gv))).astype(BF16)

    qspec, gspec, prev, cur, tab, bq, sk = _attn_specs()
    (y,), _ = _call(body, grid=(s // CHUNK,), in_specs=[qspec, gspec, prev, cur, prev, cur, tab, tab, bq, sk],
                    out_specs=[qspec], out_shape=[jax.ShapeDtypeStruct((s, B_WIDTH), BF16)],
                    args=(zb, zb, k2, k2, v2, v2, cos, sin, b_bq, sinks), name="attn_fwd")
    return y


def _attn_bwd(zb, dyb, k2, v2, cos, sin, b_bq, sinks):
    s = zb.shape[0]

    def body(zq_ref, zg_ref, dy_ref, kp_ref, kc_ref, vp_ref, vc_ref, c_ref, s_ref, bq_ref, sk_ref,
             dz_ref, dk_ref, dv_ref, dbq_ref, dsk_ref):
        i = pl.program_id(0)

        @pl.when(i == 0)
        def _():
            dk_ref[...] = jnp.zeros_like(dk_ref)
            dv_ref[...] = jnp.zeros_like(dv_ref)
            dbq_ref[...] = jnp.zeros_like(dbq_ref)
            dsk_ref[...] = jnp.zeros_like(dsk_ref)

        cos, sin = c_ref[...], s_ref[...]
        kcat = jnp.concatenate([kp_ref[...], kc_ref[...]], axis=0)
        vcat = jnp.concatenate([vp_ref[...], vc_ref[...]], axis=0)
        mask = _stacked_mask(i)
        left = _left_half(CHUNK)
        lane = lax.broadcasted_iota(jnp.int32, (1, 128), 1)
        dsk_row = jnp.zeros((1, 128), F32)
        cur_rows = pl.ds(pl.multiple_of(i * CHUNK, CHUNK), CHUNK)
        for kvh in range(N_KV_HEADS):
            kd, vd = kcat[:, _lane_block(kvh)], vcat[:, _lane_block(kvh)]
            qs = _stack_heads(_rope_blocks(zq_ref, bq_ref, cos, sin, kvh), left)
            p, psink = _stacked_probs(qs, kd, mask, _sink_row(sk_ref, kvh))
            pb = p.astype(BF16)
            ot = lax.dot_general(vd, pb, TN, preferred_element_type=F32)
            gates, dys = [], []
            for b in range(BLOCKS_PER_KV):
                cols = _lane_block(kvh * BLOCKS_PER_KV + b)
                gates.append(_silu_parts(zg_ref[:, cols].astype(F32)))
                dys.append(dy_ref[:, cols].astype(F32))
            dos = _stack_heads([(dyv * silu).astype(BF16) for dyv, (silu, _) in zip(dys, gates)], left)
            dp = lax.dot_general(vd, dos, NT, preferred_element_type=F32)
            delta = jnp.sum(p * dp, axis=0, keepdims=True)
            ds = (p * (dp - delta) * SCALE).astype(BF16)
            dqt = lax.dot_general(kd, ds, TN, preferred_element_type=F32)
            dk_part = jnp.dot(ds, qs, preferred_element_type=F32)
            dv_part = jnp.dot(pb, dos, preferred_element_type=F32)
            dk_ref[cur_rows, _lane_block(kvh)] += dk_part[CHUNK:]
            dv_ref[cur_rows, _lane_block(kvh)] += dv_part[CHUNK:]

            @pl.when(i > 0)
            def _(kvh=kvh, dk_part=dk_part, dv_part=dv_part):
                prev_rows = pl.ds(pl.multiple_of((i - 1) * CHUNK, CHUNK), CHUNK)
                dk_ref[prev_rows, _lane_block(kvh)] += dk_part[:CHUNK]
                dv_ref[prev_rows, _lane_block(kvh)] += dv_part[:CHUNK]

            sink_grad = psink * delta
            for r in range(Q_PER_KV):
                dsink = -jnp.sum(sink_grad[:, r * CHUNK:(r + 1) * CHUNK], axis=1, keepdims=True)
                dsk_row = dsk_row + jnp.where(lane == kvh * Q_PER_KV + r, dsink, 0.0)
            blocks = zip(_unstack_heads(ot), _unstack_heads(dqt), dys, gates)
            for b, (ob, dqr, dyv, (_, dsilu)) in enumerate(blocks):
                blk = kvh * BLOCKS_PER_KV + b
                dq = dqr * cos + _swap_halves(dqr * sin)
                dbq_ref[:, _lane_block(blk)] += jnp.sum(dq, axis=0, keepdims=True)
                dz_ref[:, _lane_block(blk)] = dq.astype(BF16)
                dz_ref[:, _lane_block(B_WIDTH // 128 + blk)] = (dyv * ob * dsilu).astype(BF16)
        dsk_ref[0:1, :] += dsk_row

    qspec, gspec, prev, cur, tab, bq, sk = _attn_specs()
    full = pl.BlockSpec((s, 2 * KV_WIDTH), lambda i: (0, 0))
    outs, _ = _call(
        body, grid=(s // CHUNK,),
        in_specs=[qspec, gspec, qspec, prev, cur, prev, cur, tab, tab, bq, sk],
        out_specs=[pl.BlockSpec((CHUNK, 2 * B_WIDTH), lambda i: (i, 0)), full, full, bq,
                   pl.BlockSpec((8, 128), lambda i: (0, 0))],
        out_shape=[jax.ShapeDtypeStruct((s, 2 * B_WIDTH), BF16), jax.ShapeDtypeStruct((s, 2 * KV_WIDTH), F32),
                   jax.ShapeDtypeStruct((s, 2 * KV_WIDTH), F32), jax.ShapeDtypeStruct((1, B_WIDTH), F32),
                   jax.ShapeDtypeStruct((8, 128), F32)],
        args=(zb, zb, dyb, k2, k2, v2, v2, cos, sin, b_bq, sinks), name="attn_bwd")
    return outs


def _place():
    x, y, c = lax.axis_index("x"), lax.axis_index("y"), lax.axis_index("c")
    return x, y, c, [(1 - x, y), (x, 1 - y), (1 - x, 1 - y)]


def _relations():
    return [(r >> 2 & 1, r >> 1 & 1, r & 1) for r in range(1, 8)]


def _gather_side(arrs):
    n = len(arrs)

    def copies(ins, outs, sems):
        send_ici, recv_ici, send_d2d, recv_d2d, local_sem = sems
        x, y, c, chips = _place()
        me = 2 * x + y

        def rows(a, half):
            hr = arrs[a].shape[0] // 2
            return pl.ds(half * hr, hr)

        def ici(a, j, src_chip, to):
            return pltpu.make_async_remote_copy(
                src_ref=ins[a].at[rows(a, c)], dst_ref=outs[a].at[src_chip, rows(a, c)],
                send_sem=send_ici.at[a, j], recv_sem=recv_ici.at[a, j], device_id=to, device_id_type=MESH)

        def d2d(a, j, chip, half):
            blk = outs[a].at[chip, rows(a, half)]
            return pltpu.make_async_remote_copy(
                src_ref=blk, dst_ref=blk, send_sem=send_d2d.at[a, j], recv_sem=recv_d2d.at[a, j],
                device_id=(x, y, 1 - c), device_id_type=MESH)

        local = [pltpu.make_async_copy(ins[a], outs[a].at[me], local_sem.at[a]) for a in range(n)]
        pairs = [(a, j, chip) for a in range(n) for j, chip in enumerate(chips)]
        return c, me, local, ici, d2d, pairs

    def start(ins, outs, sems):
        c, me, local, ici, _, pairs = copies(ins, outs, sems)
        for cp in local:
            cp.start()
        for a, j, chip in pairs:
            ici(a, j, me, (*chip, c)).start()

    def finish(ins, outs, sems):
        c, me, local, ici, d2d, pairs = copies(ins, outs, sems)
        for a, j, (px, py) in pairs:
            ici(a, j, 2 * px + py, (px, py, c)).wait_recv()
            d2d(a, j, 2 * px + py, c).start()
        for a, j, (px, py) in pairs:
            d2d(a, j, 2 * px + py, 1 - c).wait_recv()
        for a, j, (px, py) in pairs:
            ici(a, j, me, (px, py, c)).wait_send()
            d2d(a, j, 2 * px + py, c).wait_send()
        for cp in local:
            cp.wait()

    return _Side(arrs, [jax.ShapeDtypeStruct((N_CHIPS,) + a.shape, a.dtype) for a in arrs],
                 [pltpu.SemaphoreType.DMA((n, 3))] * 4 + [pltpu.SemaphoreType.DMA((n,))], start, finish)


def _exchange_side(grads):
    n = len(grads)

    def copies(ins, outs, sems):
        send_sem, recv_sem = sems
        x, y, c, _ = _place()
        cps = []
        for a in range(n):
            hr = grads[a].shape[1] // 2
            cps.append(pltpu.make_async_remote_copy(
                src_ref=ins[a].at[:, pl.ds((1 - c) * hr, hr), :], dst_ref=outs[a],
                send_sem=send_sem.at[a], recv_sem=recv_sem.at[a], device_id=(x, y, 1 - c), device_id_type=MESH))
        return cps

    def start(ins, outs, sems):
        for cp in copies(ins, outs, sems):
            cp.start()

    def finish(ins, outs, sems):
        for cp in copies(ins, outs, sems):
            cp.wait()

    return _Side(grads, [jax.ShapeDtypeStruct((g.shape[0], g.shape[1] // 2, g.shape[2]), g.dtype) for g in grads],
                 [pltpu.SemaphoreType.DMA((n,))] * 2, start, finish)


def _scatter_side(chip_sums, small=None):
    n = len(chip_sums)
    arrs = list(chip_sums) + ([small] if small is not None else [])

    def copies(ins, outs, sems):
        x, y, c, chips = _place()
        cps = []
        for a in range(n):
            for j, (px, py) in enumerate(chips):
                cps.append(pltpu.make_async_remote_copy(
                    src_ref=ins[a].at[2 * px + py], dst_ref=outs[a].at[j],
                    send_sem=sems[0].at[a, j], recv_sem=sems[1].at[a, j], device_id=(px, py, c), device_id_type=MESH))
        if small is not None:
            for r, (fx, fy, fc) in enumerate(_relations(), start=1):
                px, py, pc = x ^ fx, y ^ fy, c ^ fc
                cps.append(pltpu.make_async_remote_copy(
                    src_ref=ins[n].at[4 * px + 2 * py + pc], dst_ref=outs[n].at[r],
                    send_sem=sems[2].at[r - 1], recv_sem=sems[3].at[r - 1], device_id=(px, py, pc),
                    device_id_type=MESH))
        return cps

    def start(ins, outs, sems):
        for cp in copies(ins, outs, sems):
            cp.start()

    def finish(ins, outs, sems):
        for cp in copies(ins, outs, sems):
            cp.wait()

    shapes = [jax.ShapeDtypeStruct((3,) + t.shape[1:], t.dtype) for t in chip_sums]
    sems = [pltpu.SemaphoreType.DMA((n, 3))] * 2
    if small is not None:
        shapes.append(jax.ShapeDtypeStruct(small.shape, small.dtype))
        sems += [pltpu.SemaphoreType.DMA((7,))] * 2
    return _Side(arrs, shapes, sems, start, finish)


def _share_side(halves, small=None):
    n = len(halves)
    arrs = list(halves) + ([small] if small is not None else [])

    def copies(ins, outs, sems):
        x, y, c, _ = _place()
        me = 4 * x + 2 * y + c
        sends, recvs = [], []
        for a in range(n):
            hr = halves[a].shape[0] // 2
            sends.append(pltpu.make_async_remote_copy(
                src_ref=ins[a].at[pl.ds(c * hr, hr)], dst_ref=outs[a].at[pl.ds(c * hr, hr)],
                send_sem=sems[0].at[a], recv_sem=sems[1].at[a], device_id=(x, y, 1 - c), device_id_type=MESH))
            other = outs[a].at[pl.ds((1 - c) * hr, hr)]
            recvs.append(pltpu.make_async_remote_copy(
                src_ref=other, dst_ref=other, send_sem=sems[0].at[a], recv_sem=sems[1].at[a],
                device_id=(x, y, 1 - c), device_id_type=MESH))
        if small is not None:
            for r, (fx, fy, fc) in enumerate(_relations(), start=1):
                px, py, pc = x ^ fx, y ^ fy, c ^ fc
                sends.append(pltpu.make_async_remote_copy(
                    src_ref=ins[n].at[me], dst_ref=outs[n].at[me],
                    send_sem=sems[2].at[r - 1], recv_sem=sems[3].at[r - 1], device_id=(px, py, pc),
                    device_id_type=MESH))
                theirs = outs[n].at[4 * px + 2 * py + pc]
                recvs.append(pltpu.make_async_remote_copy(
                    src_ref=theirs, dst_ref=theirs, send_sem=sems[2].at[r - 1], recv_sem=sems[3].at[r - 1],
                    device_id=(px, py, pc), device_id_type=MESH))
        return sends, recvs

    def start(ins, outs, sems):
        for cp in copies(ins, outs, sems)[0]:
            cp.start()

    def finish(ins, outs, sems):
        sends, recvs = copies(ins, outs, sems)
        for cp in recvs:
            cp.wait_recv()
        for cp in sends:
            cp.wait_send()

    sems = [pltpu.SemaphoreType.DMA((n,))] * 2 + ([pltpu.SemaphoreType.DMA((7,))] * 2 if small is not None else [])
    return _Side(arrs, [jax.ShapeDtypeStruct(h.shape, h.dtype) for h in arrs], sems, start, finish,
                 aliases={i: i for i in range(len(arrs))})


def _col_tile(cols):
    return cols if cols <= 2048 else 512


def _add_sibling(grad, recv, core, *, name):
    k, r, c = grad.shape
    hr = r // 2
    tr = min(hr, 256)
    tc = _col_tile(c)
    nrb = hr // tr

    def body(core_ref, g_ref, r_ref, o_ref):
        o_ref[...] = (g_ref[...] + r_ref[...]).astype(BF16)

    return pl.pallas_call(
        body,
        grid_spec=pltpu.PrefetchScalarGridSpec(
            num_scalar_prefetch=1, grid=(k, nrb, c // tc),
            in_specs=[pl.BlockSpec((None, tr, tc), lambda kk, i, j, core: (kk, core[0] * nrb + i, j)),
                      pl.BlockSpec((None, tr, tc), lambda kk, i, j, core: (kk, i, j))],
            out_specs=pl.BlockSpec((None, tr, tc), lambda kk, i, j, core: (kk, i, j))),
        out_shape=jax.ShapeDtypeStruct((k, hr, c), BF16), name=name, compiler_params=_cparams(),
    )(core, grad, recv)


def _sum_chips(grad, from_sibling, recv, place, *, name):
    _, hr, c = from_sibling.shape
    tr = min(hr, 256)
    tc = _col_tile(c)
    nrb = hr // tr

    def body(place_ref, g_ref, s_ref, r0_ref, r1_ref, r2_ref, o_ref):
        own = g_ref[...] + s_ref[...]
        o_ref[...] = ((own + r0_ref[...].astype(F32)) + r1_ref[...].astype(F32)) + r2_ref[...].astype(F32)

    def rspec(j):
        return pl.BlockSpec((None, tr, tc), lambda i, jj, place: (j, i, jj))

    return pl.pallas_call(
        body,
        grid_spec=pltpu.PrefetchScalarGridSpec(
            num_scalar_prefetch=1, grid=(nrb, c // tc),
            in_specs=[pl.BlockSpec((None, tr, tc), lambda i, jj, place: (place[0], place[1] * nrb + i, jj)),
                      pl.BlockSpec((None, tr, tc), lambda i, jj, place: (place[0], i, jj)),
                      rspec(0), rspec(1), rspec(2)],
            out_specs=pl.BlockSpec((tr, tc), lambda i, jj, place: (place[1] * nrb + i, jj))),
        out_shape=jax.ShapeDtypeStruct((2 * hr, c), F32), name=name, compiler_params=_cparams(),
    )(place, grad, from_sibling, recv, recv, recv)


def _sum_small(small, recv, place):
    _, sr, _ = small.shape

    def body(place_ref, own_ref, r_ref, o_ref):
        acc = own_ref[...]
        for r in range(1, 8):
            acc = acc + r_ref[r]
        o_ref[...] = acc

    return pl.pallas_call(
        body,
        grid_spec=pltpu.PrefetchScalarGridSpec(
            num_scalar_prefetch=1, grid=(1,),
            in_specs=[pl.BlockSpec((None, sr, 128), lambda i, place: (place[2], 0, 0)),
                      pl.BlockSpec((8, sr, 128), lambda i, place: (0, 0, 0))],
            out_specs=pl.BlockSpec((None, sr, 128), lambda i, place: (place[2], 0, 0))),
        out_shape=jax.ShapeDtypeStruct(small.shape, F32), name="sum_small", compiler_params=_cparams(),
    )(place, small, recv)


def _adamw(w, g, m, v, *, name):
    r, c = w.shape
    tr = 256 if r % 256 == 0 else r
    tc = _col_tile(c)
    bc1 = 1.0 - ADAM_B1 ** ADAM_STEP
    bc2 = 1.0 - ADAM_B2 ** ADAM_STEP

    def body(w_ref, g_ref, m_ref, v_ref, d_ref, nm_ref, nv_ref):
        gv = g_ref[...]
        nm = ADAM_B1 * m_ref[...] + (1.0 - ADAM_B1) * gv
        nv = ADAM_B2 * v_ref[...] + (1.0 - ADAM_B2) * (gv * gv)
        d_ref[...] = -ADAM_LR * ((nm / bc1) / (jnp.sqrt(nv / bc2) + ADAM_EPS) + ADAM_WD * w_ref[...])
        nm_ref[...] = nm
        nv_ref[...] = nv

    spec = pl.BlockSpec((tr, tc), lambda i, j: (i, j))
    outs, _ = _call(body, grid=(r // tr, c // tc), in_specs=[spec] * 4, out_specs=[spec] * 3,
                    out_shape=[jax.ShapeDtypeStruct((r, c), F32)] * 3, args=(w, g, m, v), name=name)
    return outs


SMALL_ORDER = ["a_ws", "a_bs", "a_norm_g", "a_ln_g", "a_ln_b", "kv_norm_g", "b_kv", "b_norm_g", "b_bq",
               "b_sinks", "final_norm_g"]
SHARDED_SMALL = {"a_norm_g", "a_ln_g", "a_ln_b"}
PACK_TILE = 8 * 128


def _rows128(a):
    flat = a.reshape(-1)
    return jnp.pad(flat, (0, (-flat.shape[0]) % PACK_TILE)).reshape(-1, 128)


def _pack_rows(parts, multiple):
    rows = [_rows128(p) for p in parts]
    total = sum(r.shape[0] for r in rows)
    pad = (-total) % multiple
    if pad:
        rows.append(jnp.zeros((pad, 128), rows[0].dtype))
    return jnp.concatenate(rows, axis=0)


def _unpack_rows(packed, shapes):
    out, row = [], 0
    for shp in shapes:
        size = math.prod(shp)
        nrow = -(-size // PACK_TILE) * 8
        out.append(packed[row:row + nrow].reshape(-1)[:size].reshape(shp))
        row += nrow
    return out


WEIGHTS = ["a_norm_g", "a_w_in", "a_ln_g", "a_ln_b", "a_ws", "a_bs", "a_w_out", "kv_norm_g", "w_kv", "b_kv",
           "b_norm_g", "b_w_in", "b_bq", "b_sinks", "b_w_out", "final_norm_g"]
BIG = ["a_w_in", "a_w_out", "w_kv", "b_w_in", "b_w_out"]


class _Reduction:
    def __init__(self, names, partials, core, place, small=None):
        self.names, self.partials, self.core, self.place, self.small = names, partials, core, place, small

    def exchange_side(self):
        return _exchange_side(self.partials)

    def took_exchange(self, from_sibling):
        self.from_sibling = from_sibling
        self.chip_sums = [_add_sibling(g, r, self.core, name="add_sibling_" + n)
                          for g, r, n in zip(self.partials, from_sibling, self.names)]

    def scatter_side(self):
        return _scatter_side(self.chip_sums, self.small)

    def took_scatter(self, arrived):
        big = arrived[:len(self.names)]
        self.halves = [_sum_chips(g, fs, r, self.place, name="sum_chips_" + n)
                       for g, fs, r, n in zip(self.partials, self.from_sibling, big, self.names)]
        self.small_mine = _sum_small(self.small, arrived[-1], self.place) if self.small is not None else None

    def share_side(self):
        return _share_side(self.halves, self.small_mine)

    def took_share(self, shared):
        self.grads = dict(zip(self.names, shared[:len(self.names)]))
        self.small_full = shared[-1] if self.small is not None else None


def _step(x, loss_target, p, m, v):
    xi, yi, ci = lax.axis_index("x"), lax.axis_index("y"), lax.axis_index("c")
    chip = 2 * xi + yi
    device = 4 * xi + 2 * yi + ci
    core = jnp.reshape(ci, (1,)).astype(jnp.int32)
    place = jnp.stack([chip, ci, device]).astype(jnp.int32)
    x, tgt = x[0], loss_target[0]
    s = x.shape[0]
    cos, sin = _rope_tables(s)

    shard2d = {n: p[n].reshape(p[n].shape[-2:]) for n in BIG}
    shard_bf = {n: shard2d[n].astype(BF16) for n in BIG}
    ws = p["a_ws"][0]
    ws_t = jnp.swapaxes(ws, 1, 2)
    bs_t = p["a_bs"][0].T
    kv_norm_g, b_kv = p["kv_norm_g"].reshape(1, -1), p["b_kv"].reshape(1, -1)
    final_norm_g = p["final_norm_g"].reshape(1, -1)

    vec_shapes = [p[n].shape for n in ("a_norm_g", "a_ln_g", "a_ln_b")]
    vec_pack = _pack_rows([p["a_norm_g"], p["a_ln_g"], p["a_ln_b"]], 16)
    a_w_in, vec_all = _comm_call(_gather_side([shard_bf["a_w_in"], vec_pack]), "gather_first")
    vecs = [_unpack_rows(vec_all[k], vec_shapes) for k in range(N_CHIPS)]
    a_norm_g, a_ln_g, a_ln_b = (jnp.concatenate([vk[t] for vk in vecs], axis=-1) for t in range(3))

    (n_a,) = _rms_fwd(x, [a_norm_g], name="rms_a")
    z, (a_w_out, w_kv, b_w_in) = _mm_nn(
        n_a, a_w_in, name="mm_a_in", tn=768, tm=1024, out_dtype=BF16,
        side=_gather_side([shard_bf["a_w_out"], shard_bf["w_kv"], shard_bf["b_w_in"]]))
    a_w_out, w_kv = a_w_out.reshape(A_WIDTH, D_MODEL), w_kv.reshape(D_MODEL, 2 * KV_WIDTH)
    y = _gate_fwd(z, a_ln_g, a_ln_b, ws, bs_t)
    h1, (b_w_out,) = _mm_nn(y, a_w_out, name="mm_a_out", tn=D_MODEL, residual=x,
                            side=_gather_side([shard_bf["b_w_out"]]))
    b_w_out = b_w_out.reshape(B_WIDTH, D_MODEL)
    n_kv, n_b = _rms_fwd(h1, [kv_norm_g, p["b_norm_g"]], name="rms_b")
    kv = _mm_nn(n_kv, w_kv, name="mm_kv", tn=2 * KV_WIDTH)
    kr, vv = _kv_rope(kv, b_kv, cos, sin)
    zb = _mm_nn(n_b, b_w_in, name="mm_b_in", tn=512, tm=1024, out_dtype=BF16)
    yb = _attn_fwd(zb, kr, vv, cos, sin, p["b_bq"], p["b_sinks"])
    h2 = _mm_nn(yb, b_w_out, name="mm_b_out", tn=D_MODEL, residual=h1)
    loss_blk, dh2, dh2b, d_final_g = _loss_head(h2, tgt, final_norm_g)

    d_b_w_out = _mm_tn(yb, dh2b, name="mm_d_b_w_out", tm=B_WIDTH, tn=D_MODEL)
    red_bo = _Reduction(["b_w_out"], [d_b_w_out.reshape(N_CHIPS, B_WIDTH // N_CHIPS, D_MODEL)], core, place)
    dyb, got = _mm_nt(dh2b, b_w_out, name="mm_dyb", tk=D_MODEL, out_dtype=BF16, side=red_bo.exchange_side())
    red_bo.took_exchange(got)
    dzb, dk_rot, dv, d_bq, d_sinks = _attn_bwd(zb, dyb, kr, vv, cos, sin, p["b_bq"], p["b_sinks"])
    dkv, d_b_kv = _kv_rope_bwd(dk_rot, dv, cos, sin)
    d_b_w_in, got = _mm_tn(n_b, dzb, name="mm_d_b_w_in", tm=D_MODEL, tn=512, shards=N_CHIPS,
                           side=red_bo.scatter_side())
    red_bo.took_scatter(got)
    dn_b, got = _mm_nt(dzb, b_w_in, name="mm_dn_b", tk=512, side=red_bo.share_side())
    red_bo.took_share(got)
    d_w_kv = _mm_tn(n_kv, dkv, name="mm_d_w_kv", tm=D_MODEL, tn=2 * KV_WIDTH)
    red_bi = _Reduction(["b_w_in", "w_kv"], [d_b_w_in, d_w_kv.reshape(N_CHIPS, D_MODEL // N_CHIPS, 2 * KV_WIDTH)],
                        core, place)
    dn_kv, got = _mm_nt(dkv, w_kv, name="mm_dn_kv", tk=2 * KV_WIDTH, side=red_bi.exchange_side())
    red_bi.took_exchange(got)
    dh1, dh1b, (d_kv_g, d_b_g) = _rms_bwd(h1, [dn_kv, dn_b], [kv_norm_g, p["b_norm_g"]], dh2, name="rms_b_bwd")

    d_a_w_out, got = _mm_tn(y, dh1b, name="mm_d_a_w_out", tm=1024, tn=D_MODEL, side=red_bi.scatter_side())
    red_bi.took_scatter(got)
    red_ao = _Reduction(["a_w_out"], [d_a_w_out.reshape(N_CHIPS, A_WIDTH // N_CHIPS, D_MODEL)], core, place)
    sides = [red_ao.exchange_side(), red_bi.share_side()]
    dy, got = _mm_nt(dh1b, a_w_out, name="mm_dy", tk=D_MODEL, tn=1024, out_dtype=BF16, side=_join(sides))
    got = _split(got, sides)
    red_ao.took_exchange(got[0])
    red_bi.took_share(got[1])
    (dz, d_ln_g, d_ln_b, d_ws, d_bs_t), got = _gate_bwd(z, dy, a_ln_g, a_ln_b, ws, ws_t, bs_t,
                                                        side=red_ao.scatter_side())
    red_ao.took_scatter(got)
    d_a_w_in, got = _mm_tn(n_a, dz, name="mm_d_a_w_in", tm=D_MODEL, tn=1536, shards=N_CHIPS,
                           side=red_ao.share_side())
    red_ao.took_share(got)

    small = {
        "a_ws": d_ws, "a_bs": d_bs_t.T, "a_ln_g": d_ln_g, "a_ln_b": d_ln_b,
        "kv_norm_g": d_kv_g, "b_kv": d_b_kv, "b_norm_g": d_b_g, "b_bq": d_bq,
        "b_sinks": d_sinks[0:1, :N_Q_HEADS], "final_norm_g": d_final_g,
    }
    red_ai = _Reduction(["a_w_in"], [d_a_w_in], core, place)
    dn_a, got = _mm_nt(dz, a_w_in, name="mm_dn_a", tk=768, tm=1024, side=red_ai.exchange_side())
    red_ai.took_exchange(got)
    dx, _, (d_a_g,) = _rms_bwd(x, [dn_a], [a_norm_g], dh1, name="rms_a_bwd")
    small["a_norm_g"] = d_a_g

    small_shapes = [small[n].shape for n in SMALL_ORDER] + [(1, 1)]
    small_pack = _pack_rows([small[n] for n in SMALL_ORDER] + [loss_blk[0:1, 0:1]], 64)
    seg = small_pack.shape[0] // 8
    red_ai.small = small_pack.reshape(8, seg, 128)
    red_ai.took_scatter(_comm_call(red_ai.scatter_side(), "scatter_last"))
    red_ai.took_share(_comm_call(red_ai.share_side(), "share_last"))
    small_full = _unpack_rows(red_ai.small_full.reshape(8 * seg, 128), small_shapes)
    loss = small_full[-1].reshape(())

    grad_big = {**red_bo.grads, **red_bi.grads, **red_ao.grads, **red_ai.grads}
    grads = {}
    for n, gfull in zip(SMALL_ORDER, small_full):
        if n in SHARDED_SMALL:
            width = p[n].shape[-1]
            gfull = lax.dynamic_slice_in_dim(gfull, chip * width, width, axis=-1)
        grads[n] = gfull.reshape(p[n].shape)
    for n in BIG:
        grads[n] = grad_big[n].reshape(p[n].shape)

    delta, new_m, new_v = {}, {}, {}
    for n in BIG:
        d, nm, nv = _adamw(shard2d[n], grad_big[n], m[n].reshape(shard2d[n].shape), v[n].reshape(shard2d[n].shape),
                           name="adamw_" + n)
        delta[n], new_m[n], new_v[n] = d.reshape(p[n].shape), nm.reshape(p[n].shape), nv.reshape(p[n].shape)
    shapes = [p[n].shape for n in SMALL_ORDER]
    packs = [_pack_rows([src[n] for n in SMALL_ORDER], 8) for src in (p, grads, m, v)]
    outs = _adamw(*packs, name="adamw_small")
    for res, packed in zip((delta, new_m, new_v), outs):
        for n, val in zip(SMALL_ORDER, _unpack_rows(packed, shapes)):
            res[n] = val

    return (loss, dx[None], *[grads[n] for n in WEIGHTS], *[delta[n] for n in WEIGHTS],
            *[new_m[n] for n in WEIGHTS], *[new_v[n] for n in WEIGHTS])


def kernel(x, a_norm_g, a_w_in, a_ln_g, a_ln_b, a_ws, a_bs, a_w_out, kv_norm_g, w_kv, b_kv, b_norm_g, b_w_in, b_bq, b_sinks, b_w_out, final_norm_g, loss_target, m_a_norm_g, m_a_w_in, m_a_ln_g, m_a_ln_b, m_a_ws, m_a_bs, m_a_w_out, m_kv_norm_g, m_w_kv, m_b_kv, m_b_norm_g, m_b_w_in, m_b_bq, m_b_sinks, m_b_w_out, m_final_norm_g, v_a_norm_g, v_a_w_in, v_a_ln_g, v_a_ln_b, v_a_ws, v_a_bs, v_a_w_out, v_kv_norm_g, v_w_kv, v_b_kv, v_b_norm_g, v_b_w_in, v_b_bq, v_b_sinks, v_b_w_out, v_final_norm_g):
    p = dict(a_norm_g=a_norm_g, a_w_in=a_w_in, a_ln_g=a_ln_g, a_ln_b=a_ln_b, a_ws=a_ws, a_bs=a_bs, a_w_out=a_w_out,
             kv_norm_g=kv_norm_g, w_kv=w_kv, b_kv=b_kv, b_norm_g=b_norm_g, b_w_in=b_w_in, b_bq=b_bq, b_sinks=b_sinks,
             b_w_out=b_w_out, final_norm_g=final_norm_g)
    m = dict(a_norm_g=m_a_norm_g, a_w_in=m_a_w_in, a_ln_g=m_a_ln_g, a_ln_b=m_a_ln_b, a_ws=m_a_ws, a_bs=m_a_bs,
             a_w_out=m_a_w_out, kv_norm_g=m_kv_norm_g, w_kv=m_w_kv, b_kv=m_b_kv, b_norm_g=m_b_norm_g, b_w_in=m_b_w_in,
             b_bq=m_b_bq, b_sinks=m_b_sinks, b_w_out=m_b_w_out, final_norm_g=m_final_norm_g)
    v = dict(a_norm_g=v_a_norm_g, a_w_in=v_a_w_in, a_ln_g=v_a_ln_g, a_ln_b=v_a_ln_b, a_ws=v_a_ws, a_bs=v_a_bs,
             a_w_out=v_a_w_out, kv_norm_g=v_kv_norm_g, w_kv=v_w_kv, b_kv=v_b_kv, b_norm_g=v_b_norm_g, b_w_in=v_b_w_in,
             b_bq=v_b_bq, b_sinks=v_b_sinks, b_w_out=v_b_w_out, final_norm_g=v_final_norm_g)
    return _step(x, loss_target, p, m, v)
```

```python
import functools
import math

import jax
import jax.numpy as jnp
from jax import lax
from jax.experimental import pallas as pl
from jax.experimental.pallas import tpu as pltpu

F32 = jnp.float32
BF16 = jnp.bfloat16

D_MODEL = 1024
CHUNK = 128
A_WIDTH = 2048
A_GROUPS = 16
HEAD_DIM = 64
N_Q_HEADS = 16
N_KV_HEADS = 2
Q_PER_KV = 8
B_WIDTH = 1024
KV_WIDTH = 128
ROPE_THETA = 10000.0
EPS = 1e-5
N_CHIPS = 4

ADAM_LR = 0.001
ADAM_B1 = 0.9
ADAM_B2 = 0.999
ADAM_EPS = 1e-08
ADAM_WD = 0.01
ADAM_STEP = 10

VMEM_LIMIT = 48 * 1024 * 1024
MESH = pl.DeviceIdType.MESH
NEG_BIG = -1e30
HBM = pl.BlockSpec(memory_space=pl.ANY)

NN = (((1,), (0,)), ((), ()))
NT = (((1,), (1,)), ((), ()))
TN = (((0,), (0,)), ((), ()))


def _cparams(**kw):
    return pltpu.CompilerParams(vmem_limit_bytes=VMEM_LIMIT, **kw)


class _Side:
    def __init__(self, ins, out_shapes, sems, start, finish, aliases=None):
        self.ins, self.out_shapes, self.sems = list(ins), list(out_shapes), list(sems)
        self.start, self.finish = start, finish
        self.aliases = dict(aliases or {})


def _join(sides):
    sides = [s for s in sides if s is not None]
    if not sides:
        return None
    offs, i, o, m = [], 0, 0, 0
    for s in sides:
        offs.append((i, o, m))
        i, o, m = i + len(s.ins), o + len(s.out_shapes), m + len(s.sems)

    def run(which):
        def go(ins, outs, sems):
            for s, (a, b, c) in zip(sides, offs):
                getattr(s, which)(ins[a:a + len(s.ins)], outs[b:b + len(s.out_shapes)], sems[c:c + len(s.sems)])
        return go

    aliases = {}
    for s, (a, b, _) in zip(sides, offs):
        aliases.update({a + k: b + v for k, v in s.aliases.items()})
    return _Side([x for s in sides for x in s.ins], [x for s in sides for x in s.out_shapes],
                 [x for s in sides for x in s.sems], run("start"), run("finish"), aliases)


def _split(side_outs, sides):
    out, pos = [], 0
    for s in sides:
        out.append(list(side_outs[pos:pos + len(s.out_shapes)]))
        pos += len(s.out_shapes)
    return out


def _call(body, *, grid, in_specs, out_specs, out_shape, args, name, scratch=(), side=None):
    in_specs, out_specs, out_shape, scratch = list(in_specs), list(out_specs), list(out_shape), list(scratch)
    if side is None:
        res = pl.pallas_call(body, grid=grid, in_specs=in_specs, out_specs=out_specs, out_shape=out_shape,
                             scratch_shapes=scratch, name=name, compiler_params=_cparams())(*args)
        return list(res), []
    n_in, n_out, n_sc = len(in_specs), len(out_specs), len(scratch)
    s_in, s_out = len(side.ins), len(side.out_shapes)

    def wrapped(*refs):
        ins, refs = refs[:n_in], refs[n_in:]
        side_ins, refs = refs[:s_in], refs[s_in:]
        outs, refs = refs[:n_out], refs[n_out:]
        side_outs, refs = refs[:s_out], refs[s_out:]
        scr, side_sems = refs[:n_sc], refs[n_sc:]
        ids = [pl.program_id(a) for a in range(len(grid))]
        first = functools.reduce(jnp.logical_and, [i == 0 for i in ids])
        last = functools.reduce(jnp.logical_and, [i == g - 1 for i, g in zip(ids, grid)])

        @pl.when(first)
        def _():
            side.start(side_ins, side_outs, side_sems)

        body(*ins, *outs, *scr)

        @pl.when(last)
        def _():
            side.finish(side_ins, side_outs, side_sems)

    res = pl.pallas_call(
        wrapped, grid=grid, in_specs=in_specs + [HBM] * s_in, out_specs=out_specs + [HBM] * s_out,
        out_shape=out_shape + side.out_shapes, scratch_shapes=scratch + side.sems,
        input_output_aliases={n_in + k: n_out + v for k, v in side.aliases.items()},
        name=name, compiler_params=_cparams(),
    )(*args, *side.ins)
    return list(res[:n_out]), list(res[n_out:])


def _comm_call(side, name):
    s_in, s_out = len(side.ins), len(side.out_shapes)

    def body(*refs):
        ins, outs, sems = refs[:s_in], refs[s_in:s_in + s_out], refs[s_in + s_out:]
        side.start(ins, outs, sems)
        side.finish(ins, outs, sems)

    return list(pl.pallas_call(
        body, in_specs=[HBM] * s_in, out_specs=[HBM] * s_out, out_shape=side.out_shapes, scratch_shapes=side.sems,
        input_output_aliases=side.aliases, name=name,
    )(*side.ins))


def _matmul(a, b, *, dims, grid, a_spec, b_spec, o_spec, out_shape, name, acc_axis=None,
            residual=None, r_spec=None, side=None):
    has_res = residual is not None

    def body(*refs):
        if has_res:
            a_ref, b_ref, r_ref, o_ref = refs
        else:
            a_ref, b_ref, o_ref = refs
        part = lax.dot_general(a_ref[...], b_ref[...], dims, preferred_element_type=F32)
        if acc_axis is None:
            if has_res:
                part = part + r_ref[...]
            o_ref[...] = part.astype(o_ref.dtype)
        else:
            k = pl.program_id(acc_axis)

            @pl.when(k == 0)
            def _():
                o_ref[...] = part

            @pl.when(k > 0)
            def _():
                o_ref[...] += part

    in_specs = [a_spec, b_spec] + ([r_spec] if has_res else [])
    args = (a, b) + ((residual,) if has_res else ())
    (out,), side_outs = _call(body, grid=grid, in_specs=in_specs, out_specs=[o_spec], out_shape=[out_shape],
                              args=args, name=name, side=side)
    return (out, side_outs) if side is not None else out


def _row_tile(s, want):
    return min(s, want)


def _mm_nn(a, b, *, name, tn, out_dtype=F32, residual=None, tm=512, side=None):
    s, k = a.shape
    tm = _row_tile(s, tm)
    if b.ndim == 3:
        nsh, _, nc = b.shape
        npb = nc // tn
        n = nsh * nc
        b_spec = pl.BlockSpec((None, k, tn), lambda i, j: (j // npb, 0, j % npb))
    else:
        n = b.shape[1]
        b_spec = pl.BlockSpec((k, tn), lambda i, j: (0, j))
    return _matmul(
        a, b, dims=NN, grid=(s // tm, n // tn),
        a_spec=pl.BlockSpec((tm, k), lambda i, j: (i, 0)), b_spec=b_spec,
        o_spec=pl.BlockSpec((tm, tn), lambda i, j: (i, j)),
        out_shape=jax.ShapeDtypeStruct((s, n), out_dtype), name=name, side=side,
        residual=residual, r_spec=pl.BlockSpec((tm, tn), lambda i, j: (i, j)) if residual is not None else None)


def _mm_nt(a, b, *, name, tk, tn=None, tm=512, out_dtype=F32, side=None):
    s, k = a.shape
    tm = _row_tile(s, tm)
    if b.ndim == 3:
        nsh, n, kc = b.shape
        npb = kc // tk
        return _matmul(
            a, b, dims=NT, grid=(s // tm, k // tk), acc_axis=1,
            a_spec=pl.BlockSpec((tm, tk), lambda i, kk: (i, kk)),
            b_spec=pl.BlockSpec((None, n, tk), lambda i, kk: (kk // npb, 0, kk % npb)),
            o_spec=pl.BlockSpec((tm, n), lambda i, kk: (i, 0)),
            out_shape=jax.ShapeDtypeStruct((s, n), F32), name=name, side=side)
    n = b.shape[0]
    tn = n if tn is None else tn
    assert tk == k
    return _matmul(
        a, b, dims=NT, grid=(s // tm, n // tn),
        a_spec=pl.BlockSpec((tm, k), lambda i, j: (i, 0)),
        b_spec=pl.BlockSpec((tn, k), lambda i, j: (j, 0)),
        o_spec=pl.BlockSpec((tm, tn), lambda i, j: (i, j)),
        out_shape=jax.ShapeDtypeStruct((s, n), out_dtype), name=name, side=side)


def _mm_tn(a, b, *, name, tm, tn, tk=512, shards=None, side=None):
    s, m = a.shape
    n = b.shape[1]
    tk = _row_tile(s, tk)
    if shards is None:
        o_spec = pl.BlockSpec((tm, tn), lambda i, j, kk: (i, j))
        out_shape = jax.ShapeDtypeStruct((m, n), F32)
    else:
        assert tm == m
        nc = n // shards
        npb = nc // tn
        o_spec = pl.BlockSpec((None, m, tn), lambda i, j, kk: (j // npb, 0, j % npb))
        out_shape = jax.ShapeDtypeStruct((shards, m, nc), F32)
    return _matmul(
        a, b, dims=TN, grid=(m // tm, n // tn, s // tk), acc_axis=2,
        a_spec=pl.BlockSpec((tk, tm), lambda i, j, kk: (kk, i)),
        b_spec=pl.BlockSpec((tk, tn), lambda i, j, kk: (kk, j)),
        o_spec=o_spec, out_shape=out_shape, name=name, side=side)


def _rstd(x):
    return lax.rsqrt(jnp.mean(x * x, axis=-1, keepdims=True) + EPS)


def _rms_fwd(x, gains, *, name, tr=256):
    s, d = x.shape
    tr = _row_tile(s, tr)
    ng = len(gains)

    def body(*refs):
        xv = refs[0][...]
        xh = xv * _rstd(xv)
        for t in range(ng):
            refs[1 + ng + t][...] = (xh * refs[1 + t][...]).astype(BF16)

    row = pl.BlockSpec((tr, d), lambda i: (i, 0))
    vec = pl.BlockSpec((1, d), lambda i: (0, 0))
    outs, _ = _call(body, grid=(s // tr,), in_specs=[row] + [vec] * ng, out_specs=[row] * ng,
                    out_shape=[jax.ShapeDtypeStruct((s, d), BF16)] * ng, args=(x, *gains), name=name)
    return outs


def _rms_bwd(x, dns, gains, dres, *, name, tr=256):
    s, d = x.shape
    tr = _row_tile(s, tr)
    ng = len(gains)

    def body(*refs):
        x_ref = refs[0]
        dn_refs = refs[1:1 + ng]
        g_refs = refs[1 + ng:1 + 2 * ng]
        dres_ref = refs[1 + 2 * ng]
        dx_ref, dxb_ref = refs[2 + 2 * ng], refs[3 + 2 * ng]
        dg_refs = refs[4 + 2 * ng:]
        i = pl.program_id(0)
        xv = x_ref[...]
        r = _rstd(xv)
        xh = xv * r
        acc = jnp.zeros_like(xv)
        for t in range(ng):
            dn = dn_refs[t][...]
            acc = acc + dn * g_refs[t][...]
            dgt = jnp.sum(dn * xh, axis=0, keepdims=True)

            @pl.when(i == 0)
            def _(t=t, dgt=dgt):
                dg_refs[t][...] = dgt

            @pl.when(i > 0)
            def _(t=t, dgt=dgt):
                dg_refs[t][...] += dgt

        dx = dres_ref[...] + r * (acc - xh * jnp.mean(acc * xh, axis=-1, keepdims=True))
        dx_ref[...] = dx
        dxb_ref[...] = dx.astype(BF16)

    row = pl.BlockSpec((tr, d), lambda i: (i, 0))
    vec = pl.BlockSpec((1, d), lambda i: (0, 0))
    outs, _ = _call(
        body, grid=(s // tr,), in_specs=[row] + [row] * ng + [vec] * ng + [row],
        out_specs=[row, row] + [vec] * ng,
        out_shape=[jax.ShapeDtypeStruct((s, d), F32), jax.ShapeDtypeStruct((s, d), BF16)]
        + [jax.ShapeDtypeStruct((1, d), F32)] * ng,
        args=(x, *dns, *gains, dres), name=name)
    return outs[0], outs[1], outs[2:]


def _loss_head(h, tgt, gain, *, tr=256):
    s, d = h.shape
    tr = _row_tile(s, tr)

    def body(h_ref, t_ref, g_ref, loss_ref, dh_ref, dhb_ref, dg_ref):
        i = pl.program_id(0)
        hv = h_ref[...]
        g = g_ref[...]
        r = _rstd(hv)
        xh = hv * r
        diff = xh * g - t_ref[...]
        part = 0.5 / d * jnp.sum(jnp.sum(diff * diff, axis=-1, keepdims=True), axis=0, keepdims=True)
        dout = diff * (1.0 / d)
        a = dout * g
        dh = r * (a - xh * jnp.mean(a * xh, axis=-1, keepdims=True))
        dh_ref[...] = dh
        dhb_ref[...] = dh.astype(BF16)
        dgt = jnp.sum(dout * xh, axis=0, keepdims=True)
        lpart = jnp.broadcast_to(part, (8, 128))

        @pl.when(i == 0)
        def _():
            dg_ref[...] = dgt
            loss_ref[...] = lpart

        @pl.when(i > 0)
        def _():
            dg_ref[...] += dgt
            loss_ref[...] += lpart

    row = pl.BlockSpec((tr, d), lambda i: (i, 0))
    vec = pl.BlockSpec((1, d), lambda i: (0, 0))
    outs, _ = _call(
        body, grid=(s // tr,), in_specs=[row, row, vec],
        out_specs=[pl.BlockSpec((8, 128), lambda i: (0, 0)), row, row, vec],
        out_shape=[jax.ShapeDtypeStruct((8, 128), F32), jax.ShapeDtypeStruct((s, d), F32),
                   jax.ShapeDtypeStruct((s, d), BF16), jax.ShapeDtypeStruct((1, d), F32)],
        args=(h, tgt, gain), name="loss_head")
    return outs


def _causal_mask(transposed=False):
    row = lax.broadcasted_iota(jnp.int32, (CHUNK, CHUNK), 0)
    col = lax.broadcasted_iota(jnp.int32, (CHUNK, CHUNK), 1)
    return col >= row if transposed else col <= row


def _silu_parts(g):
    sg = jax.nn.sigmoid(g)
    return g * sg, sg * (1.0 + g * (1.0 - sg))


def _gate_fwd(z, ln_g, ln_b, ws, bs_t, *, tr=256, side=None):
    s = z.shape[0]
    tr = _row_tile(s, tr)
    w = A_WIDTH

    def body(u_ref, v_ref, g_ref, lg_ref, lb_ref, ws_ref, bst_ref, y_ref):
        v = v_ref[...].astype(F32)
        mu = jnp.mean(v, axis=-1, keepdims=True)
        xc = v - mu
        rs = lax.rsqrt(jnp.mean(xc * xc, axis=-1, keepdims=True) + EPS)
        vln = (xc * rs * lg_ref[...] + lb_ref[...]).astype(BF16)
        mask = _causal_mask()
        for grp in range(A_GROUPS):
            cols = slice(grp * CHUNK, (grp + 1) * CHUNK)
            wsm = jnp.where(mask, ws_ref[grp], 0.0).astype(BF16)
            bcol = bst_ref[:, grp:grp + 1]
            for ci in range(tr // CHUNK):
                rows = slice(ci * CHUNK, (ci + 1) * CHUNK)
                sv = jnp.dot(wsm, vln[rows, cols], preferred_element_type=F32) + bcol
                gv = g_ref[rows, cols].astype(F32)
                y_ref[rows, cols] = (u_ref[rows, cols].astype(F32) * sv * (gv * jax.nn.sigmoid(gv))).astype(BF16)

    vec = pl.BlockSpec((1, w), lambda i: (0, 0))
    (y,), side_outs = _call(
        body, grid=(s // tr,),
        in_specs=[pl.BlockSpec((tr, w), lambda i: (i, 0)), pl.BlockSpec((tr, w), lambda i: (i, 1)),
                  pl.BlockSpec((tr, w), lambda i: (i, 2)), vec, vec,
                  pl.BlockSpec((A_GROUPS, CHUNK, CHUNK), lambda i: (0, 0, 0)),
                  pl.BlockSpec((CHUNK, A_GROUPS), lambda i: (0, 0))],
        out_specs=[pl.BlockSpec((tr, w), lambda i: (i, 0))],
        out_shape=[jax.ShapeDtypeStruct((s, w), BF16)], args=(z, z, z, ln_g, ln_b, ws, bs_t), name="gate_fwd",
        side=side)
    return y, side_outs


def _gate_bwd(z, dy, ln_g, ln_b, ws, ws_t, bs_t, *, tr=256, side=None):
    s = z.shape[0]
    tr = _row_tile(s, tr)
    w = A_WIDTH
    nsteps = s // tr

    def body(u_ref, v_ref, g_ref, dy_ref, lg_ref, lb_ref, ws_ref, wst_ref, bst_ref,
             dz_ref, dlg_ref, dlb_ref, dws_ref, dbst_ref, dvln_sc, dsv_sc):
        i = pl.program_id(0)

        @pl.when(i == 0)
        def _():
            dws_ref[...] = jnp.zeros_like(dws_ref)
            dsv_sc[...] = jnp.zeros_like(dsv_sc)

        v = v_ref[...].astype(F32)
        mu = jnp.mean(v, axis=-1, keepdims=True)
        xc = v - mu
        rs = lax.rsqrt(jnp.mean(xc * xc, axis=-1, keepdims=True) + EPS)
        xh = xc * rs
        lg = lg_ref[...]
        vln = (xh * lg + lb_ref[...]).astype(BF16)
        mask = _causal_mask()
        mask_t = _causal_mask(transposed=True)
        for grp in range(A_GROUPS):
            cols = slice(grp * CHUNK, (grp + 1) * CHUNK)
            wsm = jnp.where(mask, ws_ref[grp], 0.0).astype(BF16)
            wsm_t = jnp.where(mask_t, wst_ref[grp], 0.0).astype(BF16)
            bcol = bst_ref[:, grp:grp + 1]
            for ci in range(tr // CHUNK):
                rows = slice(ci * CHUNK, (ci + 1) * CHUNK)
                vb = vln[rows, cols]
                sv = jnp.dot(wsm, vb, preferred_element_type=F32) + bcol
                uv = u_ref[rows, cols].astype(F32)
                silu, dsilu = _silu_parts(g_ref[rows, cols].astype(F32))
                dyv = dy_ref[rows, cols].astype(F32)
                dyu = dyv * uv
                dz_ref[rows, cols] = (dyv * sv * silu).astype(BF16)
                dz_ref[rows, 2 * w + grp * CHUNK:2 * w + (grp + 1) * CHUNK] = (dyu * sv * dsilu).astype(BF16)
                dsv = dyu * silu
                dsvb = dsv.astype(BF16)
                dvln_sc[rows, cols] = jnp.dot(wsm_t, dsvb, preferred_element_type=F32)
                dws_ref[grp] += lax.dot_general(dsvb, vb, NT, preferred_element_type=F32)
                dsv_sc[grp] += dsv
        dvln = dvln_sc[...]
        dlg_t = jnp.sum(dvln * xh, axis=0, keepdims=True)
        dlb_t = jnp.sum(dvln, axis=0, keepdims=True)
        a = dvln * lg
        dv = rs * (a - jnp.mean(a, axis=-1, keepdims=True) - xh * jnp.mean(a * xh, axis=-1, keepdims=True))
        dz_ref[:, w:2 * w] = dv.astype(BF16)

        @pl.when(i == 0)
        def _():
            dlg_ref[...] = dlg_t
            dlb_ref[...] = dlb_t

        @pl.when(i > 0)
        def _():
            dlg_ref[...] += dlg_t
            dlb_ref[...] += dlb_t

        @pl.when(i == nsteps - 1)
        def _():
            for grp in range(A_GROUPS):
                dws_ref[grp] = jnp.where(mask, dws_ref[grp], 0.0)
                dbst_ref[:, grp:grp + 1] = jnp.sum(dsv_sc[grp], axis=-1, keepdims=True)

    vec = pl.BlockSpec((1, w), lambda i: (0, 0))
    wsspec = pl.BlockSpec((A_GROUPS, CHUNK, CHUNK), lambda i: (0, 0, 0))
    bsspec = pl.BlockSpec((CHUNK, A_GROUPS), lambda i: (0, 0))
    return _call(
        body, grid=(nsteps,),
        in_specs=[pl.BlockSpec((tr, w), lambda i: (i, 0)), pl.BlockSpec((tr, w), lambda i: (i, 1)),
                  pl.BlockSpec((tr, w), lambda i: (i, 2)), pl.BlockSpec((tr, w), lambda i: (i, 0)),
                  vec, vec, wsspec, wsspec, bsspec],
        out_specs=[pl.BlockSpec((tr, 3 * w), lambda i: (i, 0)), vec, vec, wsspec, bsspec],
        out_shape=[jax.ShapeDtypeStruct((s, 3 * w), BF16), jax.ShapeDtypeStruct((1, w), F32),
                   jax.ShapeDtypeStruct((1, w), F32), jax.ShapeDtypeStruct((A_GROUPS, CHUNK, CHUNK), F32),
                   jax.ShapeDtypeStruct((CHUNK, A_GROUPS), F32)],
        scratch=[pltpu.VMEM((tr, w), F32), pltpu.VMEM((A_GROUPS, CHUNK, CHUNK), F32)],
        args=(z, z, z, dy, ln_g, ln_b, ws, ws_t, bs_t), name="gate_bwd", side=side)


HEADS_PER_BLOCK = 128 // HEAD_DIM
BLOCKS_PER_KV = Q_PER_KV // HEADS_PER_BLOCK
SCALE = HEAD_DIM ** -0.5


def _rope_tables(s):
    inv_freq = ROPE_THETA ** (-jnp.arange(0, HEAD_DIM, 2, dtype=F32) / HEAD_DIM)
    ang = jnp.arange(s, dtype=F32)[:, None] * inv_freq[None, :]
    cos, sin = jnp.cos(ang), jnp.sin(ang)
    cos2 = jnp.concatenate([cos, cos], axis=-1)
    sin2 = jnp.concatenate([-sin, sin], axis=-1)
    return jnp.tile(cos2, (1, 2)), jnp.tile(sin2, (1, 2))


def _swap_halves(x):
    n = x.shape[-1]
    lane = lax.broadcasted_iota(jnp.int32, x.shape, x.ndim - 1)
    first = (lane % HEAD_DIM) < (HEAD_DIM // 2)
    return jnp.where(first, pltpu.roll(x, n - HEAD_DIM // 2, x.ndim - 1), pltpu.roll(x, HEAD_DIM // 2, x.ndim - 1))


def _left_half(rows):
    return lax.broadcasted_iota(jnp.int32, (rows, 128), 1) < HEAD_DIM


def _dup_heads(x):
    left = _left_half(x.shape[0])
    swapped = pltpu.roll(x, HEAD_DIM, 1)
    return jnp.concatenate([jnp.where(left, x, swapped), jnp.where(left, swapped, x)], axis=-1)


def _fold_heads(a):
    b0, b1 = a[:, :128], a[:, 128:]
    f0 = b0 + pltpu.roll(b0, HEAD_DIM, 1)
    f1 = b1 + pltpu.roll(b1, HEAD_DIM, 1)
    return jnp.where(_left_half(a.shape[0]), f0, f1)


def _kv_rope(kv, b_kv, cos, sin, *, tr=512):
    s = kv.shape[0]
    tr = _row_tile(s, tr)

    def body(kv_ref, b_ref, c_ref, s_ref, k_ref, v_ref):
        x = kv_ref[...] + b_ref[...]
        k = x[:, :KV_WIDTH]
        k_ref[...] = _dup_heads(k * c_ref[...] + _swap_halves(k) * s_ref[...]).astype(BF16)
        v_ref[...] = _dup_heads(x[:, KV_WIDTH:]).astype(BF16)

    tab = pl.BlockSpec((tr, KV_WIDTH), lambda i: (i, 0))
    wide = pl.BlockSpec((tr, 2 * KV_WIDTH), lambda i: (i, 0))
    outs, _ = _call(body, grid=(s // tr,),
                    in_specs=[wide, pl.BlockSpec((1, 2 * KV_WIDTH), lambda i: (0, 0)), tab, tab],
                    out_specs=[wide, wide], out_shape=[jax.ShapeDtypeStruct((s, 2 * KV_WIDTH), BF16)] * 2,
                    args=(kv, b_kv, cos, sin), name="kv_rope")
    return outs


def _kv_rope_bwd(dk2, dv2, cos, sin, *, tr=512):
    s = dk2.shape[0]
    tr = _row_tile(s, tr)

    def body(dk_ref, dv_ref, c_ref, s_ref, dkv_ref, db_ref):
        i = pl.program_id(0)
        d = _fold_heads(dk_ref[...])
        dk = d * c_ref[...] + _swap_halves(d * s_ref[...])
        dvv = _fold_heads(dv_ref[...])
        dkv_ref[:, :KV_WIDTH] = dk.astype(BF16)
        dkv_ref[:, KV_WIDTH:] = dvv.astype(BF16)
        sk = jnp.sum(dk, axis=0, keepdims=True)
        sv = jnp.sum(dvv, axis=0, keepdims=True)

        @pl.when(i == 0)
        def _():
            db_ref[:, :KV_WIDTH] = sk
            db_ref[:, KV_WIDTH:] = sv

        @pl.when(i > 0)
        def _():
            db_ref[:, :KV_WIDTH] += sk
            db_ref[:, KV_WIDTH:] += sv

    tab = pl.BlockSpec((tr, KV_WIDTH), lambda i: (i, 0))
    wide = pl.BlockSpec((tr, 2 * KV_WIDTH), lambda i: (i, 0))
    outs, _ = _call(body, grid=(s // tr,), in_specs=[wide, wide, tab, tab],
                    out_specs=[wide, pl.BlockSpec((1, 2 * KV_WIDTH), lambda i: (0, 0))],
                    out_shape=[jax.ShapeDtypeStruct((s, 2 * KV_WIDTH), BF16),
                               jax.ShapeDtypeStruct((1, 2 * KV_WIDTH), F32)],
                    args=(dk2, dv2, cos, sin), name="kv_rope_bwd")
    return outs


def _stacked_mask(i):
    cols = Q_PER_KV * CHUNK
    k = lax.broadcasted_iota(jnp.int32, (2 * CHUNK, cols), 0)
    q = lax.broadcasted_iota(jnp.int32, (2 * CHUNK, cols), 1) & (CHUNK - 1)
    first_valid = jnp.where(i > 0, 0, CHUNK)
    prev = (k < CHUNK) & (k > q) & (k >= first_valid)
    cur = (k >= CHUNK) & (k - CHUNK <= q)
    return prev | cur


def _stack_heads(blocks, left):
    parts = []
    for b in blocks:
        parts.append(jnp.where(left, b, jnp.zeros_like(b)))
        parts.append(jnp.where(left, jnp.zeros_like(b), b))
    return jnp.concatenate(parts, axis=0)


def _unstack_heads(xt):
    top = lax.broadcasted_iota(jnp.int32, (128, CHUNK), 0) < HEAD_DIM
    return [jnp.where(top, xt[:, (2 * b) * CHUNK:(2 * b + 1) * CHUNK], xt[:, (2 * b + 1) * CHUNK:(2 * b + 2) * CHUNK]).T
            for b in range(BLOCKS_PER_KV)]


def _sink_row(sk_ref, kvh):
    return jnp.concatenate([jnp.full((1, CHUNK), sk_ref[0, kvh * Q_PER_KV + r], F32) for r in range(Q_PER_KV)], axis=1)


def _stacked_probs(qs, kd, mask, sink):
    sc = lax.dot_general(kd, qs, NT, preferred_element_type=F32) * SCALE
    sc = jnp.where(mask, sc, NEG_BIG)
    m = jnp.maximum(jnp.max(sc, axis=0, keepdims=True), sink)
    p = jnp.exp(sc - m)
    esink = jnp.exp(sink - m)
    inv = 1.0 / (jnp.sum(p, axis=0, keepdims=True) + esink)
    return p * inv, esink * inv


def _lane_block(b):
    return slice(b * 128, (b + 1) * 128)


def _rope_blocks(zq_ref, bq_ref, cos, sin, kvh):
    out = []
    for b in range(BLOCKS_PER_KV):
        cols = _lane_block(kvh * BLOCKS_PER_KV + b)
        q = zq_ref[:, cols].astype(F32) + bq_ref[:, cols]
        out.append((q * cos + _swap_halves(q) * sin).astype(BF16))
    return out


def _attn_specs():
    qspec = pl.BlockSpec((CHUNK, B_WIDTH), lambda i: (i, 0))
    gspec = pl.BlockSpec((CHUNK, B_WIDTH), lambda i: (i, 1))
    prev = pl.BlockSpec((CHUNK, 2 * KV_WIDTH), lambda i: (jnp.maximum(i - 1, 0), 0))
    cur = pl.BlockSpec((CHUNK, 2 * KV_WIDTH), lambda i: (i, 0))
    tab = pl.BlockSpec((CHUNK, KV_WIDTH), lambda i: (i, 0))
    bq = pl.BlockSpec((1, B_WIDTH), lambda i: (0, 0))
    sinks = pl.BlockSpec(memory_space=pltpu.SMEM)
    return qspec, gspec, prev, cur, tab, bq, sinks


def _attn_fwd(zb, k2, v2, cos, sin, b_bq, sinks):
    s = zb.shape[0]

    def body(zq_ref, zg_ref, kp_ref, kc_ref, vp_ref, vc_ref, c_ref, s_ref, bq_ref, sk_ref, y_ref):
        i = pl.program_id(0)
        cos, sin = c_ref[...], s_ref[...]
        kcat = jnp.concatenate([kp_ref[...], kc_ref[...]], axis=0)
        vcat = jnp.concatenate([vp_ref[...], vc_ref[...]], axis=0)
        mask = _stacked_mask(i)
        left = _left_half(CHUNK)
        for kvh in range(N_KV_HEADS):
            qs = _stack_heads(_rope_blocks(zq_ref, bq_ref, cos, sin, kvh), left)
            p, _ = _stacked_probs(qs, kcat[:, _lane_block(kvh)], mask, _sink_row(sk_ref, kvh))
            ot = lax.dot_general(vcat[:, _lane_block(kvh)], p.astype(BF16), TN, preferred_element_type=F32)
            for b, ob in enumerate(_unstack_heads(ot)):
                cols = _lane_block(kvh * BLOCKS_PER_KV + b)
                gv = zg_ref[:, cols].astype(F32)
                y_ref[:, cols] = (ob * (gv * jax.nn.sigmoid(gv))).astype(BF16)

    qspec, gspec, prev, cur, tab, bq, sk = _attn_specs()
    (y,), _ = _call(body, grid=(s // CHUNK,), in_specs=[qspec, gspec, prev, cur, prev, cur, tab, tab, bq, sk],
                    out_specs=[qspec], out_shape=[jax.ShapeDtypeStruct((s, B_WIDTH), BF16)],
                    args=(zb, zb, k2, k2, v2, v2, cos, sin, b_bq, sinks), name="attn_fwd")
    return y


def _attn_bwd(zb, dyb, k2, v2, cos, sin, b_bq, sinks):
    s = zb.shape[0]

    def body(zq_ref, zg_ref, dy_ref, kp_ref, kc_ref, vp_ref, vc_ref, c_ref, s_ref, bq_ref, sk_ref,
             dz_ref, dk_ref, dv_ref, dbq_ref, dsk_ref):
        i = pl.program_id(0)

        @pl.when(i == 0)
        def _():
            dk_ref[...] = jnp.zeros_like(dk_ref)
            dv_ref[...] = jnp.zeros_like(dv_ref)
            dbq_ref[...] = jnp.zeros_like(dbq_ref)
            dsk_ref[...] = jnp.zeros_like(dsk_ref)

        cos, sin = c_ref[...], s_ref[...]
        kcat = jnp.concatenate([kp_ref[...], kc_ref[...]], axis=0)
        vcat = jnp.concatenate([vp_ref[...], vc_ref[...]], axis=0)
        mask = _stacked_mask(i)
        left = _left_half(CHUNK)
        lane = lax.broadcasted_iota(jnp.int32, (1, 128), 1)
        dsk_row = jnp.zeros((1, 128), F32)
        cur_rows = pl.ds(pl.multiple_of(i * CHUNK, CHUNK), CHUNK)
        for kvh in range(N_KV_HEADS):
            kd, vd = kcat[:, _lane_block(kvh)], vcat[:, _lane_block(kvh)]
            qs = _stack_heads(_rope_blocks(zq_ref, bq_ref, cos, sin, kvh), left)
            p, psink = _stacked_probs(qs, kd, mask, _sink_row(sk_ref, kvh))
            pb = p.astype(BF16)
            ot = lax.dot_general(vd, pb, TN, preferred_element_type=F32)
            gates, dys = [], []
            for b in range(BLOCKS_PER_KV):
                cols = _lane_block(kvh * BLOCKS_PER_KV + b)
                gates.append(_silu_parts(zg_ref[:, cols].astype(F32)))
                dys.append(dy_ref[:, cols].astype(F32))
            dos = _stack_heads([(dyv * silu).astype(BF16) for dyv, (silu, _) in zip(dys, gates)], left)
            dp = lax.dot_general(vd, dos, NT, preferred_element_type=F32)
            delta = jnp.sum(p * dp, axis=0, keepdims=True)
            ds = (p * (dp - delta) * SCALE).astype(BF16)
            dqt = lax.dot_general(kd, ds, TN, preferred_element_type=F32)
            dk_part = jnp.dot(ds, qs, preferred_element_type=F32)
            dv_part = jnp.dot(pb, dos, preferred_element_type=F32)
            dk_ref[cur_rows, _lane_block(kvh)] += dk_part[CHUNK:]
            dv_ref[cur_rows, _lane_block(kvh)] += dv_part[CHUNK:]

            @pl.when(i > 0)
            def _(kvh=kvh, dk_part=dk_part, dv_part=dv_part):
                prev_rows = pl.ds(pl.multiple_of((i - 1) * CHUNK, CHUNK), CHUNK)
                dk_ref[prev_rows, _lane_block(kvh)] += dk_part[:CHUNK]
                dv_ref[prev_rows, _lane_block(kvh)] += dv_part[:CHUNK]

            sink_grad = psink * delta
            for r in range(Q_PER_KV):
                dsink = -jnp.sum(sink_grad[:, r * CHUNK:(r + 1) * CHUNK], axis=1, keepdims=True)
                dsk_row = dsk_row + jnp.where(lane == kvh * Q_PER_KV + r, dsink, 0.0)
            blocks = zip(_unstack_heads(ot), _unstack_heads(dqt), dys, gates)
            for b, (ob, dqr, dyv, (_, dsilu)) in enumerate(blocks):
                blk = kvh * BLOCKS_PER_KV + b
                dq = dqr * cos + _swap_halves(dqr * sin)
                dbq_ref[:, _lane_block(blk)] += jnp.sum(dq, axis=0, keepdims=True)
                dz_ref[:, _lane_block(blk)] = dq.astype(BF16)
                dz_ref[:, _lane_block(B_WIDTH // 128 + blk)] = (dyv * ob * dsilu).astype(BF16)
        dsk_ref[0:1, :] += dsk_row

    qspec, gspec, prev, cur, tab, bq, sk = _attn_specs()
    full = pl.BlockSpec((s, 2 * KV_WIDTH), lambda i: (0, 0))
    outs, _ = _call(
        body, grid=(s // CHUNK,),
        in_specs=[qspec, gspec, qspec, prev, cur, prev, cur, tab, tab, bq, sk],
        out_specs=[pl.BlockSpec((CHUNK, 2 * B_WIDTH), lambda i: (i, 0)), full, full, bq,
                   pl.BlockSpec((8, 128), lambda i: (0, 0))],
        out_shape=[jax.ShapeDtypeStruct((s, 2 * B_WIDTH), BF16), jax.ShapeDtypeStruct((s, 2 * KV_WIDTH), F32),
                   jax.ShapeDtypeStruct((s, 2 * KV_WIDTH), F32), jax.ShapeDtypeStruct((1, B_WIDTH), F32),
                   jax.ShapeDtypeStruct((8, 128), F32)],
        args=(zb, zb, dyb, k2, k2, v2, v2, cos, sin, b_bq, sinks), name="attn_bwd")
    return outs


def _place():
    x, y, c = lax.axis_index("x"), lax.axis_index("y"), lax.axis_index("c")
    return x, y, c, [(1 - x, y), (x, 1 - y), (1 - x, 1 - y)]


def _relations():
    return [(r >> 2 & 1, r >> 1 & 1, r & 1) for r in range(1, 8)]


def _gather_side(arrs):
    n = len(arrs)

    def copies(ins, outs, sems):
        send_ici, recv_ici, send_d2d, recv_d2d, local_sem = sems
        x, y, c, chips = _place()
        me = 2 * x + y

        def rows(a, half):
            hr = arrs[a].shape[0] // 2
            return pl.ds(half * hr, hr)

        def ici(a, j, src_chip, to):
            return pltpu.make_async_remote_copy(
                src_ref=ins[a].at[rows(a, c)], dst_ref=outs[a].at[src_chip, rows(a, c)],
                send_sem=send_ici.at[a, j], recv_sem=recv_ici.at[a, j], device_id=to, device_id_type=MESH)

        def d2d(a, j, chip, half):
            blk = outs[a].at[chip, rows(a, half)]
            return pltpu.make_async_remote_copy(
                src_ref=blk, dst_ref=blk, send_sem=send_d2d.at[a, j], recv_sem=recv_d2d.at[a, j],
                device_id=(x, y, 1 - c), device_id_type=MESH)

        local = [pltpu.make_async_copy(ins[a], outs[a].at[me], local_sem.at[a]) for a in range(n)]
        pairs = [(a, j, chip) for a in range(n) for j, chip in enumerate(chips)]
        return c, me, local, ici, d2d, pairs

    def start(ins, outs, sems):
        c, me, local, ici, _, pairs = copies(ins, outs, sems)
        for cp in local:
            cp.start()
        for a, j, chip in pairs:
            ici(a, j, me, (*chip, c)).start()

    def finish(ins, outs, sems):
        c, me, local, ici, d2d, pairs = copies(ins, outs, sems)
        for a, j, (px, py) in pairs:
            ici(a, j, 2 * px + py, (px, py, c)).wait_recv()
            d2d(a, j, 2 * px + py, c).start()
        for a, j, (px, py) in pairs:
            d2d(a, j, 2 * px + py, 1 - c).wait_recv()
        for a, j, (px, py) in pairs:
            ici(a, j, me, (px, py, c)).wait_send()
            d2d(a, j, 2 * px + py, c).wait_send()
        for cp in local:
            cp.wait()

    return _Side(arrs, [jax.ShapeDtypeStruct((N_CHIPS,) + a.shape, a.dtype) for a in arrs],
                 [pltpu.SemaphoreType.DMA((n, 3))] * 4 + [pltpu.SemaphoreType.DMA((n,))], start, finish)


def _exchange_side(grads):
    n = len(grads)

    def copies(ins, outs, sems):
        send_sem, recv_sem = sems
        x, y, c, _ = _place()
        cps = []
        for a in range(n):
            hr = grads[a].shape[1] // 2
            cps.append(pltpu.make_async_remote_copy(
                src_ref=ins[a].at[:, pl.ds((1 - c) * hr, hr), :], dst_ref=outs[a],
                send_sem=send_sem.at[a], recv_sem=recv_sem.at[a], device_id=(x, y, 1 - c), device_id_type=MESH))
        return cps

    def start(ins, outs, sems):
        for cp in copies(ins, outs, sems):
            cp.start()

    def finish(ins, outs, sems):
        for cp in copies(ins, outs, sems):
            cp.wait()

    return _Side(grads, [jax.ShapeDtypeStruct((g.shape[0], g.shape[1] // 2, g.shape[2]), g.dtype) for g in grads],
                 [pltpu.SemaphoreType.DMA((n,))] * 2, start, finish)


def _scatter_side(chip_sums, small=None):
    n = len(chip_sums)
    arrs = list(chip_sums) + ([small] if small is not None else [])

    def copies(ins, outs, sems):
        x, y, c, chips = _place()
        cps = []
        for a in range(n):
            for j, (px, py) in enumerate(chips):
                cps.append(pltpu.make_async_remote_copy(
                    src_ref=ins[a].at[2 * px + py], dst_ref=outs[a].at[j],
                    send_sem=sems[0].at[a, j], recv_sem=sems[1].at[a, j], device_id=(px, py, c), device_id_type=MESH))
        if small is not None:
            for r, (fx, fy, fc) in enumerate(_relations(), start=1):
                px, py, pc = x ^ fx, y ^ fy, c ^ fc
                cps.append(pltpu.make_async_remote_copy(
                    src_ref=ins[n].at[4 * px + 2 * py + pc], dst_ref=outs[n].at[r],
                    send_sem=sems[2].at[r - 1], recv_sem=sems[3].at[r - 1], device_id=(px, py, pc),
                    device_id_type=MESH))
        return cps

    def start(ins, outs, sems):
        for cp in copies(ins, outs, sems):
            cp.start()

    def finish(ins, outs, sems):
        for cp in copies(ins, outs, sems):
            cp.wait()

    shapes = [jax.ShapeDtypeStruct((3,) + t.shape[1:], t.dtype) for t in chip_sums]
    sems = [pltpu.SemaphoreType.DMA((n, 3))] * 2
    if small is not None:
        shapes.append(jax.ShapeDtypeStruct(small.shape, small.dtype))
        sems += [pltpu.SemaphoreType.DMA((7,))] * 2
    return _Side(arrs, shapes, sems, start, finish)


def _share_side(halves, small=None):
    n = len(halves)
    arrs = list(halves) + ([small] if small is not None else [])

    def copies(ins, outs, sems):
        x, y, c, _ = _place()
        me = 4 * x + 2 * y + c
        sends, recvs = [], []
        for a in range(n):
            hr = halves[a].shape[0] // 2
            sends.append(pltpu.make_async_remote_copy(
                src_ref=ins[a].at[pl.ds(c * hr, hr)], dst_ref=outs[a].at[pl.ds(c * hr, hr)],
                send_sem=sems[0].at[a], recv_sem=sems[1].at[a], device_id=(x, y, 1 - c), device_id_type=MESH))
            other = outs[a].at[pl.ds((1 - c) * hr, hr)]
            recvs.append(pltpu.make_async_remote_copy(
                src_ref=other, dst_ref=other, send_sem=sems[0].at[a], recv_sem=sems[1].at[a],
                device_id=(x, y, 1 - c), device_id_type=MESH))
        if small is not None:
            for r, (fx, fy, fc) in enumerate(_relations(), start=1):
                px, py, pc = x ^ fx, y ^ fy, c ^ fc
                sends.append(pltpu.make_async_remote_copy(
                    src_ref=ins[n].at[me], dst_ref=outs[n].at[me],
                    send_sem=sems[2].at[r - 1], recv_sem=sems[3].at[r - 1], device_id=(px, py, pc),
                    device_id_type=MESH))
                theirs = outs[n].at[4 * px + 2 * py + pc]
                recvs.append(pltpu.make_async_remote_copy(
                    src_ref=theirs, dst_ref=theirs, send_sem=sems[2].at[r - 1], recv_sem=sems[3].at[r - 1],
                    device_id=(px, py, pc), device_id_type=MESH))
        return sends, recvs

    def start(ins, outs, sems):
        for cp in copies(ins, outs, sems)[0]:
            cp.start()

    def finish(ins, outs, sems):
        sends, recvs = copies(ins, outs, sems)
        for cp in recvs:
            cp.wait_recv()
        for cp in sends:
            cp.wait_send()

    sems = [pltpu.SemaphoreType.DMA((n,))] * 2 + ([pltpu.SemaphoreType.DMA((7,))] * 2 if small is not None else [])
    return _Side(arrs, [jax.ShapeDtypeStruct(h.shape, h.dtype) for h in arrs], sems, start, finish,
                 aliases={i: i for i in range(len(arrs))})


def _mm_gathering(a, shard, order, *, name, tm=1024):
    s, k = a.shape
    nc = shard.shape[1]
    tm = _row_tile(s, tm)
    tn = nc // 2
    hr = k // 2

    def body(order_ref, a_ref, shard_ref, z_ref, full_ref, wbuf, send_ici, recv_ici, send_d2d, recv_d2d, local_sem, load_sem):
        t, jj, i = pl.program_id(0), pl.program_id(1), pl.program_id(2)
        x, y, c, chips = _place()
        me = 2 * x + y

        def rows(half):
            return pl.ds(half * hr, hr)

        def ici(j, src_chip, to):
            return pltpu.make_async_remote_copy(
                src_ref=shard_ref.at[rows(c)], dst_ref=full_ref.at[src_chip, rows(c)],
                send_sem=send_ici.at[j], recv_sem=recv_ici.at[j], device_id=to, device_id_type=MESH)

        def d2d(j, chip, half):
            blk = full_ref.at[chip, rows(half)]
            return pltpu.make_async_remote_copy(
                src_ref=blk, dst_ref=blk, send_sem=send_d2d.at[j], recv_sem=recv_d2d.at[j],
                device_id=(x, y, 1 - c), device_id_type=MESH)

        def load(src):
            for h in range(2):
                cp = pltpu.make_async_copy(src.at[:, pl.ds(h * tn, tn)], wbuf.at[h], load_sem.at[h])
                cp.start()
            for h in range(2):
                pltpu.make_async_copy(src.at[:, pl.ds(h * tn, tn)], wbuf.at[h], load_sem.at[h]).wait()

        local = pltpu.make_async_copy(shard_ref, full_ref.at[me], local_sem)
        new_shard = jnp.logical_and(jj == 0, i == 0)

        @pl.when(jnp.logical_and(new_shard, t == 0))
        def _():
            local.start()
            for j, chip in enumerate(chips):
                ici(j, me, (*chip, c)).start()
            load(shard_ref)

        for j, (px, py) in enumerate(chips):
            @pl.when(jnp.logical_and(new_shard, t == j + 1))
            def _(j=j, px=px, py=py):
                chip = 2 * px + py
                ici(j, chip, (px, py, c)).wait_recv()
                d2d(j, chip, c).start()
                d2d(j, chip, 1 - c).wait_recv()
                load(full_ref.at[chip])

        z_ref[...] = jnp.dot(a_ref[...], wbuf[jj], preferred_element_type=F32).astype(z_ref.dtype)

        last = functools.reduce(jnp.logical_and, [t == N_CHIPS - 1, jj == 1, i == s // tm - 1])

        @pl.when(last)
        def _():
            for j, (px, py) in enumerate(chips):
                ici(j, me, (px, py, c)).wait_send()
                d2d(j, 2 * px + py, c).wait_send()
            local.wait()

    return pl.pallas_call(
        body,
        grid_spec=pltpu.PrefetchScalarGridSpec(
            num_scalar_prefetch=1, grid=(N_CHIPS, 2, s // tm),
            in_specs=[pl.BlockSpec((tm, k), lambda t, jj, i, order: (i, 0)), HBM],
            out_specs=[pl.BlockSpec((tm, tn), lambda t, jj, i, order: (i, order[t] * 2 + jj)), HBM],
            scratch_shapes=[pltpu.VMEM((2, k, tn), BF16)] + [pltpu.SemaphoreType.DMA((3,))] * 4
            + [pltpu.SemaphoreType.DMA, pltpu.SemaphoreType.DMA((2,))]),
        out_shape=[jax.ShapeDtypeStruct((s, N_CHIPS * nc), BF16), jax.ShapeDtypeStruct((N_CHIPS, k, nc), BF16)],
        name=name, compiler_params=_cparams(),
    )(order, a, shard)


def _col_tile(cols):
    return cols if cols <= 2048 else 512


def _add_sibling(grad, recv, core, *, name):
    k, r, c = grad.shape
    hr = r // 2
    tr = min(hr, 256)
    tc = _col_tile(c)
    nrb = hr // tr

    def body(core_ref, g_ref, r_ref, o_ref):
        o_ref[...] = (g_ref[...] + r_ref[...]).astype(BF16)

    return pl.pallas_call(
        body,
        grid_spec=pltpu.PrefetchScalarGridSpec(
            num_scalar_prefetch=1, grid=(k, nrb, c // tc),
            in_specs=[pl.BlockSpec((None, tr, tc), lambda kk, i, j, core: (kk, core[0] * nrb + i, j)),
                      pl.BlockSpec((None, tr, tc), lambda kk, i, j, core: (kk, i, j))],
            out_specs=pl.BlockSpec((None, tr, tc), lambda kk, i, j, core: (kk, i, j))),
        out_shape=jax.ShapeDtypeStruct((k, hr, c), BF16), name=name, compiler_params=_cparams(),
    )(core, grad, recv)


def _sum_chips(grad, from_sibling, recv, place, *, name):
    _, hr, c = from_sibling.shape
    tr = min(hr, 256)
    tc = _col_tile(c)
    nrb = hr // tr

    def body(place_ref, g_ref, s_ref, r0_ref, r1_ref, r2_ref, o_ref):
        own = g_ref[...] + s_ref[...]
        o_ref[...] = ((own + r0_ref[...].astype(F32)) + r1_ref[...].astype(F32)) + r2_ref[...].astype(F32)

    def rspec(j):
        return pl.BlockSpec((None, tr, tc), lambda i, jj, place: (j, i, jj))

    return pl.pallas_call(
        body,
        grid_spec=pltpu.PrefetchScalarGridSpec(
            num_scalar_prefetch=1, grid=(nrb, c // tc),
            in_specs=[pl.BlockSpec((None, tr, tc), lambda i, jj, place: (place[0], place[1] * nrb + i, jj)),
                      pl.BlockSpec((None, tr, tc), lambda i, jj, place: (place[0], i, jj)),
                      rspec(0), rspec(1), rspec(2)],
            out_specs=pl.BlockSpec((tr, tc), lambda i, jj, place: (place[1] * nrb + i, jj))),
        out_shape=jax.ShapeDtypeStruct((2 * hr, c), F32), name=name, compiler_params=_cparams(),
    )(place, grad, from_sibling, recv, recv, recv)


def _sum_small(small, recv, place):
    _, sr, _ = small.shape

    def body(place_ref, own_ref, r_ref, o_ref):
        acc = own_ref[...]
        for r in range(1, 8):
            acc = acc + r_ref[r]
        o_ref[...] = acc

    return pl.pallas_call(
        body,
        grid_spec=pltpu.PrefetchScalarGridSpec(
            num_scalar_prefetch=1, grid=(1,),
            in_specs=[pl.BlockSpec((None, sr, 128), lambda i, place: (place[2], 0, 0)),
                      pl.BlockSpec((8, sr, 128), lambda i, place: (0, 0, 0))],
            out_specs=pl.BlockSpec((None, sr, 128), lambda i, place: (place[2], 0, 0))),
        out_shape=jax.ShapeDtypeStruct(small.shape, F32), name="sum_small", compiler_params=_cparams(),
    )(place, small, recv)


def _spread_side(vec):
    def copies(ins, outs, sems):
        x, y, c, _ = _place()
        return [pltpu.make_async_remote_copy(
            src_ref=ins[0], dst_ref=outs[0].at[r], send_sem=sems[0].at[r - 1], recv_sem=sems[1].at[r - 1],
            device_id=(x ^ fx, y ^ fy, c ^ fc), device_id_type=MESH)
            for r, (fx, fy, fc) in enumerate(_relations(), start=1)]

    def start(ins, outs, sems):
        for cp in copies(ins, outs, sems):
            cp.start()

    def finish(ins, outs, sems):
        for cp in copies(ins, outs, sems):
            cp.wait()

    return _Side([vec], [jax.ShapeDtypeStruct((8,) + vec.shape, vec.dtype)], [pltpu.SemaphoreType.DMA((7,))] * 2,
                 start, finish)


def _sum_in_device_order(own, spread, place):
    def body(place_ref, own_ref, r_ref, o_ref):
        me = place_ref[2]
        acc = jnp.zeros_like(own_ref[...])
        for d in range(8):
            slot = jnp.where(me == d, 1, me ^ d)
            acc = acc + jnp.where(me == d, own_ref[...], r_ref[slot])
        o_ref[...] = acc

    return pl.pallas_call(
        body,
        grid_spec=pltpu.PrefetchScalarGridSpec(
            num_scalar_prefetch=1, grid=(1,),
            in_specs=[pl.BlockSpec(own.shape, lambda i, place: (0, 0)),
                      pl.BlockSpec(spread.shape, lambda i, place: (0, 0, 0))],
            out_specs=pl.BlockSpec(own.shape, lambda i, place: (0, 0))),
        out_shape=jax.ShapeDtypeStruct(own.shape, F32), name="sum_in_device_order", compiler_params=_cparams(),
    )(place, own, spread)


def _adamw(w, g, m, v, *, name):
    r, c = w.shape
    tr = 256 if r % 256 == 0 else r
    tc = _col_tile(c)
    bc1 = 1.0 - ADAM_B1 ** ADAM_STEP
    bc2 = 1.0 - ADAM_B2 ** ADAM_STEP

    def body(w_ref, g_ref, m_ref, v_ref, d_ref, nm_ref, nv_ref):
        gv = g_ref[...]
        nm = ADAM_B1 * m_ref[...] + (1.0 - ADAM_B1) * gv
        nv = ADAM_B2 * v_ref[...] + (1.0 - ADAM_B2) * (gv * gv)
        d_ref[...] = -ADAM_LR * ((nm / bc1) / (jnp.sqrt(nv / bc2) + ADAM_EPS) + ADAM_WD * w_ref[...])
        nm_ref[...] = nm
        nv_ref[...] = nv

    spec = pl.BlockSpec((tr, tc), lambda i, j: (i, j))
    outs, _ = _call(body, grid=(r // tr, c // tc), in_specs=[spec] * 4, out_specs=[spec] * 3,
                    out_shape=[jax.ShapeDtypeStruct((r, c), F32)] * 3, args=(w, g, m, v), name=name)
    return outs


SMALL_ORDER = ["a_ws", "a_bs", "a_norm_g", "a_ln_g", "a_ln_b", "kv_norm_g", "b_kv", "b_norm_g", "b_bq",
               "b_sinks", "final_norm_g"]
SHARDED_SMALL = {"a_norm_g", "a_ln_g", "a_ln_b"}
PACK_TILE = 8 * 128


def _rows128(a):
    flat = a.reshape(-1)
    return jnp.pad(flat, (0, (-flat.shape[0]) % PACK_TILE)).reshape(-1, 128)


def _pack_rows(parts, multiple):
    rows = [_rows128(p) for p in parts]
    total = sum(r.shape[0] for r in rows)
    pad = (-total) % multiple
    if pad:
        rows.append(jnp.zeros((pad, 128), rows[0].dtype))
    return jnp.concatenate(rows, axis=0)


def _unpack_rows(packed, shapes):
    out, row = [], 0
    for shp in shapes:
        size = math.prod(shp)
        nrow = -(-size // PACK_TILE) * 8
        out.append(packed[row:row + nrow].reshape(-1)[:size].reshape(shp))
        row += nrow
    return out


WEIGHTS = ["a_norm_g", "a_w_in", "a_ln_g", "a_ln_b", "a_ws", "a_bs", "a_w_out", "kv_norm_g", "w_kv", "b_kv",
           "b_norm_g", "b_w_in", "b_bq", "b_sinks", "b_w_out", "final_norm_g"]
BIG = ["a_w_in", "a_w_out", "w_kv", "b_w_in", "b_w_out"]


class _Reduction:
    def __init__(self, names, partials, core, place, small=None):
        self.names, self.partials, self.core, self.place, self.small = names, partials, core, place, small

    def exchange_side(self):
        return _exchange_side(self.partials)

    def took_exchange(self, from_sibling):
        self.from_sibling = from_sibling
        self.chip_sums = [_add_sibling(g, r, self.core, name="add_sibling_" + n)
                          for g, r, n in zip(self.partials, from_sibling, self.names)]

    def scatter_side(self):
        return _scatter_side(self.chip_sums, self.small)

    def took_scatter(self, arrived):
        big = arrived[:len(self.names)]
        self.halves = [_sum_chips(g, fs, r, self.place, name="sum_chips_" + n)
                       for g, fs, r, n in zip(self.partials, self.from_sibling, big, self.names)]
        self.small_mine = _sum_small(self.small, arrived[-1], self.place) if self.small is not None else None

    def share_side(self):
        return _share_side(self.halves, self.small_mine)

    def took_share(self, shared):
        self.grads = dict(zip(self.names, shared[:len(self.names)]))
        self.small_full = shared[-1] if self.small is not None else None


def _step(x, loss_target, p, m, v):
    xi, yi, ci = lax.axis_index("x"), lax.axis_index("y"), lax.axis_index("c")
    chip = 2 * xi + yi
    device = 4 * xi + 2 * yi + ci
    core = jnp.reshape(ci, (1,)).astype(jnp.int32)
    place = jnp.stack([chip, ci, device]).astype(jnp.int32)
    x, tgt = x[0], loss_target[0]
    s = x.shape[0]
    cos, sin = _rope_tables(s)

    shard2d = {n: p[n].reshape(p[n].shape[-2:]) for n in BIG}
    shard_bf = {n: shard2d[n].astype(BF16) for n in BIG}
    ws = p["a_ws"][0]
    ws_t = jnp.swapaxes(ws, 1, 2)
    bs_t = p["a_bs"][0].T
    kv_norm_g, b_kv = p["kv_norm_g"].reshape(1, -1), p["b_kv"].reshape(1, -1)
    final_norm_g = p["final_norm_g"].reshape(1, -1)

    vec_shapes = [p[n].shape for n in ("a_norm_g", "a_ln_g", "a_ln_b")]
    vec_pack = _pack_rows([p["a_norm_g"], p["a_ln_g"], p["a_ln_b"]], 16)
    (vec_all,) = _comm_call(_gather_side([vec_pack]), "gather_vectors")
    vecs = [_unpack_rows(vec_all[k], vec_shapes) for k in range(N_CHIPS)]
    a_norm_g, a_ln_g, a_ln_b = (jnp.concatenate([vk[t] for vk in vecs], axis=-1) for t in range(3))

    (n_a,) = _rms_fwd(x, [a_norm_g], name="rms_a")
    order = jnp.stack([chip, 2 * (1 - xi) + yi, 2 * xi + (1 - yi), 2 * (1 - xi) + (1 - yi)]).astype(jnp.int32)
    z, a_w_in = _mm_gathering(n_a, shard_bf["a_w_in"], order, name="mm_a_in")
    y, (a_w_out,) = _gate_fwd(z, a_ln_g, a_ln_b, ws, bs_t, side=_gather_side([shard_bf["a_w_out"]]))
    a_w_out = a_w_out.reshape(A_WIDTH, D_MODEL)
    h1, (w_kv, b_w_in) = _mm_nn(y, a_w_out, name="mm_a_out", tn=D_MODEL, residual=x,
                                side=_gather_side([shard_bf["w_kv"], shard_bf["b_w_in"]]))
    w_kv = w_kv.reshape(D_MODEL, 2 * KV_WIDTH)
    n_kv, n_b = _rms_fwd(h1, [kv_norm_g, p["b_norm_g"]], name="rms_b")
    kv = _mm_nn(n_kv, w_kv, name="mm_kv", tn=2 * KV_WIDTH)
    kr, vv = _kv_rope(kv, b_kv, cos, sin)
    zb, (b_w_out,) = _mm_nn(n_b, b_w_in, name="mm_b_in", tn=512, tm=1024, out_dtype=BF16,
                            side=_gather_side([shard_bf["b_w_out"]]))
    b_w_out = b_w_out.reshape(B_WIDTH, D_MODEL)
    yb = _attn_fwd(zb, kr, vv, cos, sin, p["b_bq"], p["b_sinks"])
    h2 = _mm_nn(yb, b_w_out, name="mm_b_out", tn=D_MODEL, residual=h1)
    loss_blk, dh2, dh2b, d_final_g = _loss_head(h2, tgt, final_norm_g)

    d_b_w_out = _mm_tn(yb, dh2b, name="mm_d_b_w_out", tm=B_WIDTH, tn=D_MODEL)
    red_bo = _Reduction(["b_w_out"], [d_b_w_out.reshape(N_CHIPS, B_WIDTH // N_CHIPS, D_MODEL)], core, place)
    dyb, got = _mm_nt(dh2b, b_w_out, name="mm_dyb", tk=D_MODEL, out_dtype=BF16, side=red_bo.exchange_side())
    red_bo.took_exchange(got)
    dzb, dk_rot, dv, d_bq, d_sinks = _attn_bwd(zb, dyb, kr, vv, cos, sin, p["b_bq"], p["b_sinks"])
    dkv, d_b_kv = _kv_rope_bwd(dk_rot, dv, cos, sin)
    d_b_w_in, got = _mm_tn(n_b, dzb, name="mm_d_b_w_in", tm=D_MODEL, tn=512, shards=N_CHIPS,
                           side=red_bo.scatter_side())
    red_bo.took_scatter(got)
    dn_b, got = _mm_nt(dzb, b_w_in, name="mm_dn_b", tk=512, side=red_bo.share_side())
    red_bo.took_share(got)
    d_w_kv = _mm_tn(n_kv, dkv, name="mm_d_w_kv", tm=D_MODEL, tn=2 * KV_WIDTH)
    red_bi = _Reduction(["b_w_in", "w_kv"], [d_b_w_in, d_w_kv.reshape(N_CHIPS, D_MODEL // N_CHIPS, 2 * KV_WIDTH)],
                        core, place)
    dn_kv, got = _mm_nt(dkv, w_kv, name="mm_dn_kv", tk=2 * KV_WIDTH, side=red_bi.exchange_side())
    red_bi.took_exchange(got)
    dh1, dh1b, (d_kv_g, d_b_g) = _rms_bwd(h1, [dn_kv, dn_b], [kv_norm_g, p["b_norm_g"]], dh2, name="rms_b_bwd")

    d_a_w_out, got = _mm_tn(y, dh1b, name="mm_d_a_w_out", tm=1024, tn=D_MODEL, side=red_bi.scatter_side())
    red_bi.took_scatter(got)
    red_ao = _Reduction(["a_w_out"], [d_a_w_out.reshape(N_CHIPS, A_WIDTH // N_CHIPS, D_MODEL)], core, place)
    sides = [red_ao.exchange_side(), red_bi.share_side()]
    dy, got = _mm_nt(dh1b, a_w_out, name="mm_dy", tk=D_MODEL, tn=1024, out_dtype=BF16, side=_join(sides))
    got = _split(got, sides)
    red_ao.took_exchange(got[0])
    red_bi.took_share(got[1])
    (dz, d_ln_g, d_ln_b, d_ws, d_bs_t), got = _gate_bwd(z, dy, a_ln_g, a_ln_b, ws, ws_t, bs_t,
                                                        side=red_ao.scatter_side())
    red_ao.took_scatter(got)
    d_a_w_in, got = _mm_tn(n_a, dz, name="mm_d_a_w_in", tm=D_MODEL, tn=1536, shards=N_CHIPS,
                           side=red_ao.share_side())
    red_ao.took_share(got)

    small = {
        "a_ws": d_ws, "a_bs": d_bs_t.T, "a_ln_g": d_ln_g, "a_ln_b": d_ln_b,
        "kv_norm_g": d_kv_g, "b_kv": d_b_kv, "b_norm_g": d_b_g, "b_bq": d_bq,
        "b_sinks": d_sinks[0:1, :N_Q_HEADS], "final_norm_g": d_final_g,
    }
    packed = [n for n in SMALL_ORDER if n != "a_norm_g"]
    small_shapes = [small[n].shape for n in packed] + [(1, 1)]
    small_pack = _pack_rows([small[n] for n in packed] + [loss_blk[0:1, 0:1]], 64)
    seg = small_pack.shape[0] // 8
    red_ai = _Reduction(["a_w_in"], [d_a_w_in], core, place, small=small_pack.reshape(8, seg, 128))
    red_ai.took_exchange(_comm_call(red_ai.exchange_side(), "exchange_last"))
    dn_a, got = _mm_nt(dz, a_w_in, name="mm_dn_a", tk=768, tm=1024, side=red_ai.scatter_side())
    red_ai.took_scatter(got)
    dx, _, (d_a_g,) = _rms_bwd(x, [dn_a], [a_norm_g], dh1, name="rms_a_bwd")
    d_a_g = _rows128(d_a_g)
    sides = [red_ai.share_side(), _spread_side(d_a_g)]
    got = _split(_comm_call(_join(sides), "share_last"), sides)
    red_ai.took_share(got[0])
    small_full = dict(zip(packed + ["loss"], _unpack_rows(red_ai.small_full.reshape(8 * seg, 128), small_shapes)))
    small_full["a_norm_g"] = _sum_in_device_order(d_a_g, got[1][0], place).reshape(1, -1)
    loss = small_full["loss"].reshape(())

    grad_big = {**red_bo.grads, **red_bi.grads, **red_ao.grads, **red_ai.grads}
    grads = {}
    for n in SMALL_ORDER:
        gfull = small_full[n]
        if n in SHARDED_SMALL:
            width = p[n].shape[-1]
            gfull = lax.dynamic_slice_in_dim(gfull, chip * width, width, axis=-1)
        grads[n] = gfull.reshape(p[n].shape)
    for n in BIG:
        grads[n] = grad_big[n].reshape(p[n].shape)

    delta, new_m, new_v = {}, {}, {}
    for n in BIG:
        d, nm, nv = _adamw(shard2d[n], grad_big[n], m[n].reshape(shard2d[n].shape), v[n].reshape(shard2d[n].shape),
                           name="adamw_" + n)
        delta[n], new_m[n], new_v[n] = d.reshape(p[n].shape), nm.reshape(p[n].shape), nv.reshape(p[n].shape)
    shapes = [p[n].shape for n in SMALL_ORDER]
    packs = [_pack_rows([src[n] for n in SMALL_ORDER], 8) for src in (p, grads, m, v)]
    outs = _adamw(*packs, name="adamw_small")
    for res, packed in zip((delta, new_m, new_v), outs):
        for n, val in zip(SMALL_ORDER, _unpack_rows(packed, shapes)):
            res[n] = val

    return (loss, dx[None], *[grads[n] for n in WEIGHTS], *[delta[n] for n in WEIGHTS],
            *[new_m[n] for n in WEIGHTS], *[new_v[n] for n in WEIGHTS])


def kernel(x, a_norm_g, a_w_in, a_ln_g, a_ln_b, a_ws, a_bs, a_w_out, kv_norm_g, w_kv, b_kv, b_norm_g, b_w_in, b_bq, b_sinks, b_w_out, final_norm_g, loss_target, m_a_norm_g, m_a_w_in, m_a_ln_g, m_a_ln_b, m_a_ws, m_a_bs, m_a_w_out, m_kv_norm_g, m_w_kv, m_b_kv, m_b_norm_g, m_b_w_in, m_b_bq, m_b_sinks, m_b_w_out, m_final_norm_g, v_a_norm_g, v_a_w_in, v_a_ln_g, v_a_ln_b, v_a_ws, v_a_bs, v_a_w_out, v_kv_norm_g, v_w_kv, v_b_kv, v_b_norm_g, v_b_w_in, v_b_bq, v_b_sinks, v_b_w_out, v_final_norm_g):
    p = dict(a_norm_g=a_norm_g, a_w_in=a_w_in, a_ln_g=a_ln_g, a_ln_b=a_ln_b, a_ws=a_ws, a_bs=a_bs, a_w_out=a_w_out,
             kv_norm_g=kv_norm_g, w_kv=w_kv, b_kv=b_kv, b_norm_g=b_norm_g, b_w_in=b_w_in, b_bq=b_bq, b_sinks=b_sinks,
             b_w_out=b_w_out, final_norm_g=final_norm_g)
    m = dict(a_norm_g=m_a_norm_g, a_w_in=m_a_w_in, a_ln_g=m_a_ln_g, a_ln_b=m_a_ln_b, a_ws=m_a_ws, a_bs=m_a_bs,
             a_w_out=m_a_w_out, kv_norm_g=m_kv_norm_g, w_kv=m_w_kv, b_kv=m_b_kv, b_norm_g=m_b_norm_g, b_w_in=m_b_w_in,
             b_bq=m_b_bq, b_sinks=m_b_sinks, b_w_out=m_b_w_out, final_norm_g=m_final_norm_g)
    v = dict(a_norm_g=v_a_norm_g, a_w_in=v_a_w_in, a_ln_g=v_a_ln_g, a_ln_b=v_a_ln_b, a_ws=v_a_ws, a_bs=v_a_bs,
             a_w_out=v_a_w_out, kv_norm_g=v_kv_norm_g, w_kv=v_w_kv, b_kv=v_b_kv, b_norm_g=v_b_norm_g, b_w_in=v_b_w_in,
             b_bq=v_b_bq, b_sinks=v_b_sinks, b_w_out=v_b_w_out, final_norm_g=v_final_norm_g)
    return _step(x, loss_target, p, m, v)
```

```python
import functools
import math

import jax
import jax.numpy as jnp
from jax import lax
from jax.experimental import pallas as pl
from jax.experimental.pallas import tpu as pltpu

F32 = jnp.float32
BF16 = jnp.bfloat16

D_MODEL = 1024
CHUNK = 128
A_WIDTH = 2048
A_GROUPS = 16
HEAD_DIM = 64
N_Q_HEADS = 16
N_KV_HEADS = 2
Q_PER_KV = 8
B_WIDTH = 1024
KV_WIDTH = 128
ROPE_THETA = 10000.0
EPS = 1e-5
N_CHIPS = 4

ADAM_LR = 0.001
ADAM_B1 = 0.9
ADAM_B2 = 0.999
ADAM_EPS = 1e-08
ADAM_WD = 0.01
ADAM_STEP = 10

VMEM_LIMIT = 48 * 1024 * 1024
MESH = pl.DeviceIdType.MESH
NEG_BIG = -1e30
HBM = pl.BlockSpec(memory_space=pl.ANY)

NN = (((1,), (0,)), ((), ()))
NT = (((1,), (1,)), ((), ()))
TN = (((0,), (0,)), ((), ()))


def _cparams(**kw):
    return pltpu.CompilerParams(vmem_limit_bytes=VMEM_LIMIT, **kw)


class _Side:
    def __init__(self, ins, out_shapes, sems, start, finish, aliases=None):
        self.ins, self.out_shapes, self.sems = list(ins), list(out_shapes), list(sems)
        self.start, self.finish = start, finish
        self.aliases = dict(aliases or {})


def _join(sides):
    sides = [s for s in sides if s is not None]
    if not sides:
        return None
    offs, i, o, m = [], 0, 0, 0
    for s in sides:
        offs.append((i, o, m))
        i, o, m = i + len(s.ins), o + len(s.out_shapes), m + len(s.sems)

    def run(which):
        def go(ins, outs, sems):
            for s, (a, b, c) in zip(sides, offs):
                getattr(s, which)(ins[a:a + len(s.ins)], outs[b:b + len(s.out_shapes)], sems[c:c + len(s.sems)])
        return go

    aliases = {}
    for s, (a, b, _) in zip(sides, offs):
        aliases.update({a + k: b + v for k, v in s.aliases.items()})
    return _Side([x for s in sides for x in s.ins], [x for s in sides for x in s.out_shapes],
                 [x for s in sides for x in s.sems], run("start"), run("finish"), aliases)


def _split(side_outs, sides):
    out, pos = [], 0
    for s in sides:
        out.append(list(side_outs[pos:pos + len(s.out_shapes)]))
        pos += len(s.out_shapes)
    return out


def _call(body, *, grid, in_specs, out_specs, out_shape, args, name, scratch=(), side=None):
    in_specs, out_specs, out_shape, scratch = list(in_specs), list(out_specs), list(out_shape), list(scratch)
    if side is None:
        res = pl.pallas_call(body, grid=grid, in_specs=in_specs, out_specs=out_specs, out_shape=out_shape,
                             scratch_shapes=scratch, name=name, compiler_params=_cparams())(*args)
        return list(res), []
    n_in, n_out, n_sc = len(in_specs), len(out_specs), len(scratch)
    s_in, s_out = len(side.ins), len(side.out_shapes)

    def wrapped(*refs):
        ins, refs = refs[:n_in], refs[n_in:]
        side_ins, refs = refs[:s_in], refs[s_in:]
        outs, refs = refs[:n_out], refs[n_out:]
        side_outs, refs = refs[:s_out], refs[s_out:]
        scr, side_sems = refs[:n_sc], refs[n_sc:]
        ids = [pl.program_id(a) for a in range(len(grid))]
        first = functools.reduce(jnp.logical_and, [i == 0 for i in ids])
        last = functools.reduce(jnp.logical_and, [i == g - 1 for i, g in zip(ids, grid)])

        @pl.when(first)
        def _():
            side.start(side_ins, side_outs, side_sems)

        body(*ins, *outs, *scr)

        @pl.when(last)
        def _():
            side.finish(side_ins, side_outs, side_sems)

    res = pl.pallas_call(
        wrapped, grid=grid, in_specs=in_specs + [HBM] * s_in, out_specs=out_specs + [HBM] * s_out,
        out_shape=out_shape + side.out_shapes, scratch_shapes=scratch + side.sems,
        input_output_aliases={n_in + k: n_out + v for k, v in side.aliases.items()},
        name=name, compiler_params=_cparams(),
    )(*args, *side.ins)
    return list(res[:n_out]), list(res[n_out:])


def _comm_call(side, name):
    s_in, s_out = len(side.ins), len(side.out_shapes)

    def body(*refs):
        ins, outs, sems = refs[:s_in], refs[s_in:s_in + s_out], refs[s_in + s_out:]
        side.start(ins, outs, sems)
        side.finish(ins, outs, sems)

    return list(pl.pallas_call(
        body, in_specs=[HBM] * s_in, out_specs=[HBM] * s_out, out_shape=side.out_shapes, scratch_shapes=side.sems,
        input_output_aliases=side.aliases, name=name,
    )(*side.ins))


def _matmul(a, b, *, dims, grid, a_spec, b_spec, o_spec, out_shape, name, acc_axis=None,
            residual=None, r_spec=None, side=None):
    has_res = residual is not None

    def body(*refs):
        if has_res:
            a_ref, b_ref, r_ref, o_ref = refs
        else:
            a_ref, b_ref, o_ref = refs
        part = lax.dot_general(a_ref[...], b_ref[...], dims, preferred_element_type=F32)
        if acc_axis is None:
            if has_res:
                part = part + r_ref[...]
            o_ref[...] = part.astype(o_ref.dtype)
        else:
            k = pl.program_id(acc_axis)

            @pl.when(k == 0)
            def _():
                o_ref[...] = part

            @pl.when(k > 0)
            def _():
                o_ref[...] += part

    in_specs = [a_spec, b_spec] + ([r_spec] if has_res else [])
    args = (a, b) + ((residual,) if has_res else ())
    (out,), side_outs = _call(body, grid=grid, in_specs=in_specs, out_specs=[o_spec], out_shape=[out_shape],
                              args=args, name=name, side=side)
    return (out, side_outs) if side is not None else out


def _row_tile(s, want):
    return min(s, want)


def _mm_nn(a, b, *, name, tn, out_dtype=F32, residual=None, tm=512, side=None):
    s, k = a.shape
    tm = _row_tile(s, tm)
    if b.ndim == 3:
        nsh, _, nc = b.shape
        npb = nc // tn
        n = nsh * nc
        b_spec = pl.BlockSpec((None, k, tn), lambda i, j: (j // npb, 0, j % npb))
    else:
        n = b.shape[1]
        b_spec = pl.BlockSpec((k, tn), lambda i, j: (0, j))
    return _matmul(
        a, b, dims=NN, grid=(s // tm, n // tn),
        a_spec=pl.BlockSpec((tm, k), lambda i, j: (i, 0)), b_spec=b_spec,
        o_spec=pl.BlockSpec((tm, tn), lambda i, j: (i, j)),
        out_shape=jax.ShapeDtypeStruct((s, n), out_dtype), name=name, side=side,
        residual=residual, r_spec=pl.BlockSpec((tm, tn), lambda i, j: (i, j)) if residual is not None else None)


def _mm_nt(a, b, *, name, tn=None, tm=512, out_dtype=F32, side=None):
    s, k = a.shape
    tm = _row_tile(s, tm)
    if b.ndim == 3:
        nsh, n, kc = b.shape

        def body(a_ref, b_ref, o_ref):
            acc = None
            for sh in range(nsh):
                part = lax.dot_general(a_ref[:, sh * kc:(sh + 1) * kc], b_ref[sh], NT, preferred_element_type=F32)
                acc = part if acc is None else acc + part
            o_ref[...] = acc

        (out,), side_outs = _call(
            body, grid=(s // tm,),
            in_specs=[pl.BlockSpec((tm, k), lambda i: (i, 0)), pl.BlockSpec((nsh, n, kc), lambda i: (0, 0, 0))],
            out_specs=[pl.BlockSpec((tm, n), lambda i: (i, 0))], out_shape=[jax.ShapeDtypeStruct((s, n), F32)],
            args=(a, b), name=name, side=side)
        return (out, side_outs) if side is not None else out
    n = b.shape[0]
    tn = n if tn is None else tn
    return _matmul(
        a, b, dims=NT, grid=(s // tm, n // tn),
        a_spec=pl.BlockSpec((tm, k), lambda i, j: (i, 0)),
        b_spec=pl.BlockSpec((tn, k), lambda i, j: (j, 0)),
        o_spec=pl.BlockSpec((tm, tn), lambda i, j: (i, j)),
        out_shape=jax.ShapeDtypeStruct((s, n), out_dtype), name=name, side=side)


def _mm_tn(a, b, *, name, tm, tn, tk=2048, shards=None, side=None):
    s, m = a.shape
    n = b.shape[1]
    tk = _row_tile(s, tk)
    if shards is None:
        o_spec = pl.BlockSpec((tm, tn), lambda i, j, kk: (i, j))
        out_shape = jax.ShapeDtypeStruct((m, n), F32)
    else:
        assert tm == m
        nc = n // shards
        npb = nc // tn
        o_spec = pl.BlockSpec((None, m, tn), lambda i, j, kk: (j // npb, 0, j % npb))
        out_shape = jax.ShapeDtypeStruct((shards, m, nc), F32)
    return _matmul(
        a, b, dims=TN, grid=(m // tm, n // tn, s // tk), acc_axis=2,
        a_spec=pl.BlockSpec((tk, tm), lambda i, j, kk: (kk, i)),
        b_spec=pl.BlockSpec((tk, tn), lambda i, j, kk: (kk, j)),
        o_spec=o_spec, out_shape=out_shape, name=name, side=side)


def _rstd(x):
    return lax.rsqrt(jnp.mean(x * x, axis=-1, keepdims=True) + EPS)


def _rms_fwd(x, gains, *, name, tr=256):
    s, d = x.shape
    tr = _row_tile(s, tr)
    ng = len(gains)

    def body(*refs):
        xv = refs[0][...]
        xh = xv * _rstd(xv)
        for t in range(ng):
            refs[1 + ng + t][...] = (xh * refs[1 + t][...]).astype(BF16)

    row = pl.BlockSpec((tr, d), lambda i: (i, 0))
    vec = pl.BlockSpec((1, d), lambda i: (0, 0))
    outs, _ = _call(body, grid=(s // tr,), in_specs=[row] + [vec] * ng, out_specs=[row] * ng,
                    out_shape=[jax.ShapeDtypeStruct((s, d), BF16)] * ng, args=(x, *gains), name=name)
    return outs


def _rms_bwd(x, dns, gains, dres, *, name, tr=256):
    s, d = x.shape
    tr = _row_tile(s, tr)
    ng = len(gains)

    def body(*refs):
        x_ref = refs[0]
        dn_refs = refs[1:1 + ng]
        g_refs = refs[1 + ng:1 + 2 * ng]
        dres_ref = refs[1 + 2 * ng]
        dx_ref, dxb_ref = refs[2 + 2 * ng], refs[3 + 2 * ng]
        dg_refs = refs[4 + 2 * ng:]
        i = pl.program_id(0)
        xv = x_ref[...]
        r = _rstd(xv)
        xh = xv * r
        acc = jnp.zeros_like(xv)
        for t in range(ng):
            dn = dn_refs[t][...]
            acc = acc + dn * g_refs[t][...]
            dgt = jnp.sum(dn * xh, axis=0, keepdims=True)

            @pl.when(i == 0)
            def _(t=t, dgt=dgt):
                dg_refs[t][...] = dgt

            @pl.when(i > 0)
            def _(t=t, dgt=dgt):
                dg_refs[t][...] += dgt

        dx = dres_ref[...] + r * (acc - xh * jnp.mean(acc * xh, axis=-1, keepdims=True))
        dx_ref[...] = dx
        dxb_ref[...] = dx.astype(BF16)

    row = pl.BlockSpec((tr, d), lambda i: (i, 0))
    vec = pl.BlockSpec((1, d), lambda i: (0, 0))
    outs, _ = _call(
        body, grid=(s // tr,), in_specs=[row] + [row] * ng + [vec] * ng + [row],
        out_specs=[row, row] + [vec] * ng,
        out_shape=[jax.ShapeDtypeStruct((s, d), F32), jax.ShapeDtypeStruct((s, d), BF16)]
        + [jax.ShapeDtypeStruct((1, d), F32)] * ng,
        args=(x, *dns, *gains, dres), name=name)
    return outs[0], outs[1], outs[2:]


def _loss_head(h, tgt, gain, *, tr=256):
    s, d = h.shape
    tr = _row_tile(s, tr)

    def body(h_ref, t_ref, g_ref, loss_ref, dh_ref, dhb_ref, dg_ref):
        i = pl.program_id(0)
        hv = h_ref[...]
        g = g_ref[...]
        r = _rstd(hv)
        xh = hv * r
        diff = xh * g - t_ref[...]
        part = 0.5 / d * jnp.sum(jnp.sum(diff * diff, axis=-1, keepdims=True), axis=0, keepdims=True)
        dout = diff * (1.0 / d)
        a = dout * g
        dh = r * (a - xh * jnp.mean(a * xh, axis=-1, keepdims=True))
        dh_ref[...] = dh
        dhb_ref[...] = dh.astype(BF16)
        dgt = jnp.sum(dout * xh, axis=0, keepdims=True)
        lpart = jnp.broadcast_to(part, (8, 128))

        @pl.when(i == 0)
        def _():
            dg_ref[...] = dgt
            loss_ref[...] = lpart

        @pl.when(i > 0)
        def _():
            dg_ref[...] += dgt
            loss_ref[...] += lpart

    row = pl.BlockSpec((tr, d), lambda i: (i, 0))
    vec = pl.BlockSpec((1, d), lambda i: (0, 0))
    outs, _ = _call(
        body, grid=(s // tr,), in_specs=[row, row, vec],
        out_specs=[pl.BlockSpec((8, 128), lambda i: (0, 0)), row, row, vec],
        out_shape=[jax.ShapeDtypeStruct((8, 128), F32), jax.ShapeDtypeStruct((s, d), F32),
                   jax.ShapeDtypeStruct((s, d), BF16), jax.ShapeDtypeStruct((1, d), F32)],
        args=(h, tgt, gain), name="loss_head")
    return outs


def _causal_mask(transposed=False):
    row = lax.broadcasted_iota(jnp.int32, (CHUNK, CHUNK), 0)
    col = lax.broadcasted_iota(jnp.int32, (CHUNK, CHUNK), 1)
    return col >= row if transposed else col <= row


def _silu_parts(g):
    sg = jax.nn.sigmoid(g)
    return g * sg, sg * (1.0 + g * (1.0 - sg))


def _gate_fwd(z, ln_g, ln_b, ws, bs_t, *, tr=256, side=None):
    s = z.shape[0]
    tr = _row_tile(s, tr)
    w = A_WIDTH

    def body(u_ref, v_ref, g_ref, lg_ref, lb_ref, ws_ref, bst_ref, y_ref):
        v = v_ref[...].astype(F32)
        mu = jnp.mean(v, axis=-1, keepdims=True)
        xc = v - mu
        rs = lax.rsqrt(jnp.mean(xc * xc, axis=-1, keepdims=True) + EPS)
        vln = (xc * rs * lg_ref[...] + lb_ref[...]).astype(BF16)
        mask = _causal_mask()
        for grp in range(A_GROUPS):
            cols = slice(grp * CHUNK, (grp + 1) * CHUNK)
            wsm = jnp.where(mask, ws_ref[grp], 0.0).astype(BF16)
            bcol = bst_ref[:, grp:grp + 1]
            for ci in range(tr // CHUNK):
                rows = slice(ci * CHUNK, (ci + 1) * CHUNK)
                sv = jnp.dot(wsm, vln[rows, cols], preferred_element_type=F32) + bcol
                gv = g_ref[rows, cols].astype(F32)
                y_ref[rows, cols] = (u_ref[rows, cols].astype(F32) * sv * (gv * jax.nn.sigmoid(gv))).astype(BF16)

    vec = pl.BlockSpec((1, w), lambda i: (0, 0))
    (y,), side_outs = _call(
        body, grid=(s // tr,),
        in_specs=[pl.BlockSpec((tr, w), lambda i: (i, 0)), pl.BlockSpec((tr, w), lambda i: (i, 1)),
                  pl.BlockSpec((tr, w), lambda i: (i, 2)), vec, vec,
                  pl.BlockSpec((A_GROUPS, CHUNK, CHUNK), lambda i: (0, 0, 0)),
                  pl.BlockSpec((CHUNK, A_GROUPS), lambda i: (0, 0))],
        out_specs=[pl.BlockSpec((tr, w), lambda i: (i, 0))],
        out_shape=[jax.ShapeDtypeStruct((s, w), BF16)], args=(z, z, z, ln_g, ln_b, ws, bs_t), name="gate_fwd",
        side=side)
    return y, side_outs


def _gate_bwd(z, dy, ln_g, ln_b, ws, ws_t, bs_t, *, tr=256, side=None):
    s = z.shape[0]
    tr = _row_tile(s, tr)
    w = A_WIDTH
    nsteps = s // tr

    def body(u_ref, v_ref, g_ref, dy_ref, lg_ref, lb_ref, ws_ref, wst_ref, bst_ref,
             dz_ref, dlg_ref, dlb_ref, dws_ref, dbst_ref, dvln_sc, dsv_sc):
        i = pl.program_id(0)

        @pl.when(i == 0)
        def _():
            dws_ref[...] = jnp.zeros_like(dws_ref)
            dsv_sc[...] = jnp.zeros_like(dsv_sc)

        v = v_ref[...].astype(F32)
        mu = jnp.mean(v, axis=-1, keepdims=True)
        xc = v - mu
        rs = lax.rsqrt(jnp.mean(xc * xc, axis=-1, keepdims=True) + EPS)
        xh = xc * rs
        lg = lg_ref[...]
        vln = (xh * lg + lb_ref[...]).astype(BF16)
        mask = _causal_mask()
        mask_t = _causal_mask(transposed=True)
        for grp in range(A_GROUPS):
            cols = slice(grp * CHUNK, (grp + 1) * CHUNK)
            wsm = jnp.where(mask, ws_ref[grp], 0.0).astype(BF16)
            wsm_t = jnp.where(mask_t, wst_ref[grp], 0.0).astype(BF16)
            bcol = bst_ref[:, grp:grp + 1]
            for ci in range(tr // CHUNK):
                rows = slice(ci * CHUNK, (ci + 1) * CHUNK)
                vb = vln[rows, cols]
                sv = jnp.dot(wsm, vb, preferred_element_type=F32) + bcol
                uv = u_ref[rows, cols].astype(F32)
                silu, dsilu = _silu_parts(g_ref[rows, cols].astype(F32))
                dyv = dy_ref[rows, cols].astype(F32)
                dyu = dyv * uv
                dz_ref[rows, cols] = (dyv * sv * silu).astype(BF16)
                dz_ref[rows, 2 * w + grp * CHUNK:2 * w + (grp + 1) * CHUNK] = (dyu * sv * dsilu).astype(BF16)
                dsv = dyu * silu
                dsvb = dsv.astype(BF16)
                dvln_sc[rows, cols] = jnp.dot(wsm_t, dsvb, preferred_element_type=F32)
                dws_ref[grp] += lax.dot_general(dsvb, vb, NT, preferred_element_type=F32)
                dsv_sc[grp] += dsv
        dvln = dvln_sc[...]
        dlg_t = jnp.sum(dvln * xh, axis=0, keepdims=True)
        dlb_t = jnp.sum(dvln, axis=0, keepdims=True)
        a = dvln * lg
        dv = rs * (a - jnp.mean(a, axis=-1, keepdims=True) - xh * jnp.mean(a * xh, axis=-1, keepdims=True))
        dz_ref[:, w:2 * w] = dv.astype(BF16)

        @pl.when(i == 0)
        def _():
            dlg_ref[...] = dlg_t
            dlb_ref[...] = dlb_t

        @pl.when(i > 0)
        def _():
            dlg_ref[...] += dlg_t
            dlb_ref[...] += dlb_t

        @pl.when(i == nsteps - 1)
        def _():
            for grp in range(A_GROUPS):
                dws_ref[grp] = jnp.where(mask, dws_ref[grp], 0.0)
                dbst_ref[:, grp:grp + 1] = jnp.sum(dsv_sc[grp], axis=-1, keepdims=True)

    vec = pl.BlockSpec((1, w), lambda i: (0, 0))
    wsspec = pl.BlockSpec((A_GROUPS, CHUNK, CHUNK), lambda i: (0, 0, 0))
    bsspec = pl.BlockSpec((CHUNK, A_GROUPS), lambda i: (0, 0))
    return _call(
        body, grid=(nsteps,),
        in_specs=[pl.BlockSpec((tr, w), lambda i: (i, 0)), pl.BlockSpec((tr, w), lambda i: (i, 1)),
                  pl.BlockSpec((tr, w), lambda i: (i, 2)), pl.BlockSpec((tr, w), lambda i: (i, 0)),
                  vec, vec, wsspec, wsspec, bsspec],
        out_specs=[pl.BlockSpec((tr, 3 * w), lambda i: (i, 0)), vec, vec, wsspec, bsspec],
        out_shape=[jax.ShapeDtypeStruct((s, 3 * w), BF16), jax.ShapeDtypeStruct((1, w), F32),
                   jax.ShapeDtypeStruct((1, w), F32), jax.ShapeDtypeStruct((A_GROUPS, CHUNK, CHUNK), F32),
                   jax.ShapeDtypeStruct((CHUNK, A_GROUPS), F32)],
        scratch=[pltpu.VMEM((tr, w), F32), pltpu.VMEM((A_GROUPS, CHUNK, CHUNK), F32)],
        args=(z, z, z, dy, ln_g, ln_b, ws, ws_t, bs_t), name="gate_bwd", side=side)


HEADS_PER_BLOCK = 128 // HEAD_DIM
BLOCKS_PER_KV = Q_PER_KV // HEADS_PER_BLOCK
SCALE = HEAD_DIM ** -0.5


def _rope_tables(s):
    inv_freq = ROPE_THETA ** (-jnp.arange(0, HEAD_DIM, 2, dtype=F32) / HEAD_DIM)
    ang = jnp.arange(s, dtype=F32)[:, None] * inv_freq[None, :]
    cos, sin = jnp.cos(ang), jnp.sin(ang)
    cos2 = jnp.concatenate([cos, cos], axis=-1)
    sin2 = jnp.concatenate([-sin, sin], axis=-1)
    return jnp.tile(cos2, (1, 2)), jnp.tile(sin2, (1, 2))


def _swap_halves(x):
    n = x.shape[-1]
    lane = lax.broadcasted_iota(jnp.int32, x.shape, x.ndim - 1)
    first = (lane % HEAD_DIM) < (HEAD_DIM // 2)
    return jnp.where(first, pltpu.roll(x, n - HEAD_DIM // 2, x.ndim - 1), pltpu.roll(x, HEAD_DIM // 2, x.ndim - 1))


def _left_half(rows):
    return lax.broadcasted_iota(jnp.int32, (rows, 128), 1) < HEAD_DIM


def _dup_heads(x):
    left = _left_half(x.shape[0])
    swapped = pltpu.roll(x, HEAD_DIM, 1)
    return jnp.concatenate([jnp.where(left, x, swapped), jnp.where(left, swapped, x)], axis=-1)


def _fold_heads(a):
    b0, b1 = a[:, :128], a[:, 128:]
    f0 = b0 + pltpu.roll(b0, HEAD_DIM, 1)
    f1 = b1 + pltpu.roll(b1, HEAD_DIM, 1)
    return jnp.where(_left_half(a.shape[0]), f0, f1)


def _kv_rope(kv, b_kv, cos, sin, *, tr=512):
    s = kv.shape[0]
    tr = _row_tile(s, tr)

    def body(kv_ref, b_ref, c_ref, s_ref, k_ref, v_ref):
        x = kv_ref[...] + b_ref[...]
        k = x[:, :KV_WIDTH]
        k_ref[...] = _dup_heads(k * c_ref[...] + _swap_halves(k) * s_ref[...]).astype(BF16)
        v_ref[...] = _dup_heads(x[:, KV_WIDTH:]).astype(BF16)

    tab = pl.BlockSpec((tr, KV_WIDTH), lambda i: (i, 0))
    wide = pl.BlockSpec((tr, 2 * KV_WIDTH), lambda i: (i, 0))
    outs, _ = _call(body, grid=(s // tr,),
                    in_specs=[wide, pl.BlockSpec((1, 2 * KV_WIDTH), lambda i: (0, 0)), tab, tab],
                    out_specs=[wide, wide], out_shape=[jax.ShapeDtypeStruct((s, 2 * KV_WIDTH), BF16)] * 2,
                    args=(kv, b_kv, cos, sin), name="kv_rope")
    return outs


def _kv_rope_bwd(dk2, dv2, cos, sin, *, tr=512):
    s = dk2.shape[0]
    tr = _row_tile(s, tr)

    def body(dk_ref, dv_ref, c_ref, s_ref, dkv_ref, db_ref):
        i = pl.program_id(0)
        d = _fold_heads(dk_ref[...])
        dk = d * c_ref[...] + _swap_halves(d * s_ref[...])
        dvv = _fold_heads(dv_ref[...])
        dkv_ref[:, :KV_WIDTH] = dk.astype(BF16)
        dkv_ref[:, KV_WIDTH:] = dvv.astype(BF16)
        sk = jnp.sum(dk, axis=0, keepdims=True)
        sv = jnp.sum(dvv, axis=0, keepdims=True)

        @pl.when(i == 0)
        def _():
            db_ref[:, :KV_WIDTH] = sk
            db_ref[:, KV_WIDTH:] = sv

        @pl.when(i > 0)
        def _():
            db_ref[:, :KV_WIDTH] += sk
            db_ref[:, KV_WIDTH:] += sv

    tab = pl.BlockSpec((tr, KV_WIDTH), lambda i: (i, 0))
    wide = pl.BlockSpec((tr, 2 * KV_WIDTH), lambda i: (i, 0))
    outs, _ = _call(body, grid=(s // tr,), in_specs=[wide, wide, tab, tab],
                    out_specs=[wide, pl.BlockSpec((1, 2 * KV_WIDTH), lambda i: (0, 0))],
                    out_shape=[jax.ShapeDtypeStruct((s, 2 * KV_WIDTH), BF16),
                               jax.ShapeDtypeStruct((1, 2 * KV_WIDTH), F32)],
                    args=(dk2, dv2, cos, sin), name="kv_rope_bwd")
    return outs


def _stacked_mask(i):
    cols = Q_PER_KV * CHUNK
    k = lax.broadcasted_iota(jnp.int32, (2 * CHUNK, cols), 0)
    q = lax.broadcasted_iota(jnp.int32, (2 * CHUNK, cols), 1) & (CHUNK - 1)
    first_valid = jnp.where(i > 0, 0, CHUNK)
    prev = (k < CHUNK) & (k > q) & (k >= first_valid)
    cur = (k >= CHUNK) & (k - CHUNK <= q)
    return prev | cur


def _stack_heads(blocks, left):
    parts = []
    for b in blocks:
        parts.append(jnp.where(left, b, jnp.zeros_like(b)))
        parts.append(jnp.where(left, jnp.zeros_like(b), b))
    return jnp.concatenate(parts, axis=0)


def _unstack_heads(xt):
    top = lax.broadcasted_iota(jnp.int32, (128, CHUNK), 0) < HEAD_DIM
    return [jnp.where(top, xt[:, (2 * b) * CHUNK:(2 * b + 1) * CHUNK], xt[:, (2 * b + 1) * CHUNK:(2 * b + 2) * CHUNK]).T
            for b in range(BLOCKS_PER_KV)]


def _sink_row(sk_ref, kvh):
    return jnp.concatenate([jnp.full((1, CHUNK), sk_ref[0, kvh * Q_PER_KV + r], F32) for r in range(Q_PER_KV)], axis=1)


def _stacked_probs(qs, kd, mask, sink):
    sc = lax.dot_general(kd, qs, NT, preferred_element_type=F32) * SCALE
    sc = jnp.where(mask, sc, NEG_BIG)
    m = jnp.maximum(jnp.max(sc, axis=0, keepdims=True), sink)
    p = jnp.exp(sc - m)
    esink = jnp.exp(sink - m)
    inv = 1.0 / (jnp.sum(p, axis=0, keepdims=True) + esink)
    return p * inv, esink * inv


def _lane_block(b):
    return slice(b * 128, (b + 1) * 128)


def _rope_blocks(zq_ref, bq_ref, cos, sin, kvh):
    out = []
    for b in range(BLOCKS_PER_KV):
        cols = _lane_block(kvh * BLOCKS_PER_KV + b)
        q = zq_ref[:, cols].astype(F32) + bq_ref[:, cols]
        out.append((q * cos + _swap_halves(q) * sin).astype(BF16))
    return out


def _attn_specs():
    qspec = pl.BlockSpec((CHUNK, B_WIDTH), lambda i: (i, 0))
    gspec = pl.BlockSpec((CHUNK, B_WIDTH), lambda i: (i, 1))
    prev = pl.BlockSpec((CHUNK, 2 * KV_WIDTH), lambda i: (jnp.maximum(i - 1, 0), 0))
    cur = pl.BlockSpec((CHUNK, 2 * KV_WIDTH), lambda i: (i, 0))
    tab = pl.BlockSpec((CHUNK, KV_WIDTH), lambda i: (i, 0))
    bq = pl.BlockSpec((1, B_WIDTH), lambda i: (0, 0))
    sinks = pl.BlockSpec(memory_space=pltpu.SMEM)
    return qspec, gspec, prev, cur, tab, bq, sinks


def _attn_fwd(zb, k2, v2, cos, sin, b_bq, sinks):
    s = zb.shape[0]

    def body(zq_ref, zg_ref, kp_ref, kc_ref, vp_ref, vc_ref, c_ref, s_ref, bq_ref, sk_ref, y_ref):
        i = pl.program_id(0)
        cos, sin = c_ref[...], s_ref[...]
        kcat = jnp.concatenate([kp_ref[...], kc_ref[...]], axis=0)
        vcat = jnp.concatenate([vp_ref[...], vc_ref[...]], axis=0)
        mask = _stacked_mask(i)
        left = _left_half(CHUNK)
        for kvh in range(N_KV_HEADS):
            qs = _stack_heads(_rope_blocks(zq_ref, bq_ref, cos, sin, kvh), left)
            p, _ = _stacked_probs(qs, kcat[:, _lane_block(kvh)], mask, _sink_row(sk_ref, kvh))
            ot = lax.dot_general(vcat[:, _lane_block(kvh)], p.astype(BF16), TN, preferred_element_type=F32)
            for b, ob in enumerate(_unstack_heads(ot)):
                cols = _lane_block(kvh * BLOCKS_PER_KV + b)
                gv = zg_ref[:, cols].astype(F32)
                y_ref[:, cols] = (ob * (gv * jax.nn.sigmoid(gv))).astype(BF16)

    qspec, gspec, prev, cur, tab, bq, sk = _attn_specs()
    (y,), _ = _call(body, grid=(s // CHUNK,), in_specs=[qspec, gspec, prev, cur, prev, cur, tab, tab, bq, sk],
                    out_specs=[qspec], out_shape=[jax.ShapeDtypeStruct((s, B_WIDTH), BF16)],
                    args=(zb, zb, k2, k2, v2, v2, cos, sin, b_bq, sinks), name="attn_fwd")
    return y


def _attn_bwd(zb, dyb, k2, v2, cos, sin, b_bq, sinks):
    s = zb.shape[0]

    def body(zq_ref, zg_ref, dy_ref, kp_ref, kc_ref, vp_ref, vc_ref, c_ref, s_ref, bq_ref, sk_ref,
             dz_ref, dk_ref, dv_ref, dbq_ref, dsk_ref):
        i = pl.program_id(0)

        @pl.when(i == 0)
        def _():
            dk_ref[...] = jnp.zeros_like(dk_ref)
            dv_ref[...] = jnp.zeros_like(dv_ref)
            dbq_ref[...] = jnp.zeros_like(dbq_ref)
            dsk_ref[...] = jnp.zeros_like(dsk_ref)

        cos, sin = c_ref[...], s_ref[...]
        kcat = jnp.concatenate([kp_ref[...], kc_ref[...]], axis=0)
        vcat = jnp.concatenate([vp_ref[...], vc_ref[...]], axis=0)
        mask = _stacked_mask(i)
        left = _left_half(CHUNK)
        lane = lax.broadcasted_iota(jnp.int32, (1, 128), 1)
        dsk_row = jnp.zeros((1, 128), F32)
        cur_rows = pl.ds(pl.multiple_of(i * CHUNK, CHUNK), CHUNK)
        for kvh in range(N_KV_HEADS):
            kd, vd = kcat[:, _lane_block(kvh)], vcat[:, _lane_block(kvh)]
            qs = _stack_heads(_rope_blocks(zq_ref, bq_ref, cos, sin, kvh), left)
            p, psink = _stacked_probs(qs, kd, mask, _sink_row(sk_ref, kvh))
            pb = p.astype(BF16)
            ot = lax.dot_general(vd, pb, TN, preferred_element_type=F32)
            gates, dys = [], []
            for b in range(BLOCKS_PER_KV):
                cols = _lane_block(kvh * BLOCKS_PER_KV + b)
                gates.append(_silu_parts(zg_ref[:, cols].astype(F32)))
                dys.append(dy_ref[:, cols].astype(F32))
            dos = _stack_heads([(dyv * silu).astype(BF16) for dyv, (silu, _) in zip(dys, gates)], left)
            dp = lax.dot_general(vd, dos, NT, preferred_element_type=F32)
            delta = jnp.sum(p * dp, axis=0, keepdims=True)
            ds = (p * (dp - delta) * SCALE).astype(BF16)
            dqt = lax.dot_general(kd, ds, TN, preferred_element_type=F32)
            dk_part = jnp.dot(ds, qs, preferred_element_type=F32)
            dv_part = jnp.dot(pb, dos, preferred_element_type=F32)
            dk_ref[cur_rows, _lane_block(kvh)] += dk_part[CHUNK:]
            dv_ref[cur_rows, _lane_block(kvh)] += dv_part[CHUNK:]

            @pl.when(i > 0)
            def _(kvh=kvh, dk_part=dk_part, dv_part=dv_part):
                prev_rows = pl.ds(pl.multiple_of((i - 1) * CHUNK, CHUNK), CHUNK)
                dk_ref[prev_rows, _lane_block(kvh)] += dk_part[:CHUNK]
                dv_ref[prev_rows, _lane_block(kvh)] += dv_part[:CHUNK]

            sink_grad = psink * delta
            for r in range(Q_PER_KV):
                dsink = -jnp.sum(sink_grad[:, r * CHUNK:(r + 1) * CHUNK], axis=1, keepdims=True)
                dsk_row = dsk_row + jnp.where(lane == kvh * Q_PER_KV + r, dsink, 0.0)
            blocks = zip(_unstack_heads(ot), _unstack_heads(dqt), dys, gates)
            for b, (ob, dqr, dyv, (_, dsilu)) in enumerate(blocks):
                blk = kvh * BLOCKS_PER_KV + b
                dq = dqr * cos + _swap_halves(dqr * sin)
                dbq_ref[:, _lane_block(blk)] += jnp.sum(dq, axis=0, keepdims=True)
                dz_ref[:, _lane_block(blk)] = dq.astype(BF16)
                dz_ref[:, _lane_block(B_WIDTH // 128 + blk)] = (dyv * ob * dsilu).astype(BF16)
        dsk_ref[0:1, :] += dsk_row

    qspec, gspec, prev, cur, tab, bq, sk = _attn_specs()
    full = pl.BlockSpec((s, 2 * KV_WIDTH), lambda i: (0, 0))
    outs, _ = _call(
        body, grid=(s // CHUNK,),
        in_specs=[qspec, gspec, qspec, prev, cur, prev, cur, tab, tab, bq, sk],
        out_specs=[pl.BlockSpec((CHUNK, 2 * B_WIDTH), lambda i: (i, 0)), full, full, bq,
                   pl.BlockSpec((8, 128), lambda i: (0, 0))],
        out_shape=[jax.ShapeDtypeStruct((s, 2 * B_WIDTH), BF16), jax.ShapeDtypeStruct((s, 2 * KV_WIDTH), F32),
                   jax.ShapeDtypeStruct((s, 2 * KV_WIDTH), F32), jax.ShapeDtypeStruct((1, B_WIDTH), F32),
                   jax.ShapeDtypeStruct((8, 128), F32)],
        args=(zb, zb, dyb, k2, k2, v2, v2, cos, sin, b_bq, sinks), name="attn_bwd")
    return outs


def _place():
    x, y, c = lax.axis_index("x"), lax.axis_index("y"), lax.axis_index("c")
    return x, y, c, [(1 - x, y), (x, 1 - y), (1 - x, 1 - y)]


def _relations():
    return [(r >> 2 & 1, r >> 1 & 1, r & 1) for r in range(1, 8)]


def _gather_side(arrs):
    n = len(arrs)

    def copies(ins, outs, sems):
        send_ici, recv_ici, send_d2d, recv_d2d, local_sem = sems
        x, y, c, chips = _place()
        me = 2 * x + y

        def rows(a, half):
            hr = arrs[a].shape[0] // 2
            return pl.ds(half * hr, hr)

        def ici(a, j, src_chip, to):
            return pltpu.make_async_remote_copy(
                src_ref=ins[a].at[rows(a, c)], dst_ref=outs[a].at[src_chip, rows(a, c)],
                send_sem=send_ici.at[a, j], recv_sem=recv_ici.at[a, j], device_id=to, device_id_type=MESH)

        def d2d(a, j, chip, half):
            blk = outs[a].at[chip, rows(a, half)]
            return pltpu.make_async_remote_copy(
                src_ref=blk, dst_ref=blk, send_sem=send_d2d.at[a, j], recv_sem=recv_d2d.at[a, j],
                device_id=(x, y, 1 - c), device_id_type=MESH)

        local = [pltpu.make_async_copy(ins[a], outs[a].at[me], local_sem.at[a]) for a in range(n)]
        pairs = [(a, j, chip) for a in range(n) for j, chip in enumerate(chips)]
        return c, me, local, ici, d2d, pairs

    def start(ins, outs, sems):
        c, me, local, ici, _, pairs = copies(ins, outs, sems)
        for cp in local:
            cp.start()
        for a, j, chip in pairs:
            ici(a, j, me, (*chip, c)).start()

    def finish(ins, outs, sems):
        c, me, local, ici, d2d, pairs = copies(ins, outs, sems)
        for a, j, (px, py) in pairs:
            ici(a, j, 2 * px + py, (px, py, c)).wait_recv()
            d2d(a, j, 2 * px + py, c).start()
        for a, j, (px, py) in pairs:
            d2d(a, j, 2 * px + py, 1 - c).wait_recv()
        for a, j, (px, py) in pairs:
            ici(a, j, me, (px, py, c)).wait_send()
            d2d(a, j, 2 * px + py, c).wait_send()
        for cp in local:
            cp.wait()

    return _Side(arrs, [jax.ShapeDtypeStruct((N_CHIPS,) + a.shape, a.dtype) for a in arrs],
                 [pltpu.SemaphoreType.DMA((n, 3))] * 4 + [pltpu.SemaphoreType.DMA((n,))], start, finish)


def _exchange_side(grads):
    n = len(grads)

    def copies(ins, outs, sems):
        send_sem, recv_sem = sems
        x, y, c, _ = _place()
        cps = []
        for a in range(n):
            hr = grads[a].shape[1] // 2
            cps.append(pltpu.make_async_remote_copy(
                src_ref=ins[a].at[:, pl.ds((1 - c) * hr, hr), :], dst_ref=outs[a],
                send_sem=send_sem.at[a], recv_sem=recv_sem.at[a], device_id=(x, y, 1 - c), device_id_type=MESH))
        return cps

    def start(ins, outs, sems):
        for cp in copies(ins, outs, sems):
            cp.start()

    def finish(ins, outs, sems):
        for cp in copies(ins, outs, sems):
            cp.wait()

    return _Side(grads, [jax.ShapeDtypeStruct((g.shape[0], g.shape[1] // 2, g.shape[2]), g.dtype) for g in grads],
                 [pltpu.SemaphoreType.DMA((n,))] * 2, start, finish)


def _scatter_side(chip_sums, small=None):
    n = len(chip_sums)
    arrs = list(chip_sums) + ([small] if small is not None else [])

    def copies(ins, outs, sems):
        x, y, c, chips = _place()
        cps = []
        for a in range(n):
            for j, (px, py) in enumerate(chips):
                cps.append(pltpu.make_async_remote_copy(
                    src_ref=ins[a].at[2 * px + py], dst_ref=outs[a].at[j],
                    send_sem=sems[0].at[a, j], recv_sem=sems[1].at[a, j], device_id=(px, py, c), device_id_type=MESH))
        if small is not None:
            for r, (fx, fy, fc) in enumerate(_relations(), start=1):
                px, py, pc = x ^ fx, y ^ fy, c ^ fc
                cps.append(pltpu.make_async_remote_copy(
                    src_ref=ins[n].at[4 * px + 2 * py + pc], dst_ref=outs[n].at[r],
                    send_sem=sems[2].at[r - 1], recv_sem=sems[3].at[r - 1], device_id=(px, py, pc),
                    device_id_type=MESH))
        return cps

    def start(ins, outs, sems):
        for cp in copies(ins, outs, sems):
            cp.start()

    def finish(ins, outs, sems):
        for cp in copies(ins, outs, sems):
            cp.wait()

    shapes = [jax.ShapeDtypeStruct((3,) + t.shape[1:], t.dtype) for t in chip_sums]
    sems = [pltpu.SemaphoreType.DMA((n, 3))] * 2
    if small is not None:
        shapes.append(jax.ShapeDtypeStruct(small.shape, small.dtype))
        sems += [pltpu.SemaphoreType.DMA((7,))] * 2
    return _Side(arrs, shapes, sems, start, finish)


def _share_side(halves, small=None):
    n = len(halves)
    arrs = list(halves) + ([small] if small is not None else [])

    def copies(ins, outs, sems):
        x, y, c, _ = _place()
        me = 4 * x + 2 * y + c
        sends, recvs = [], []
        for a in range(n):
            hr = halves[a].shape[0] // 2
            sends.append(pltpu.make_async_remote_copy(
                src_ref=ins[a].at[pl.ds(c * hr, hr)], dst_ref=outs[a].at[pl.ds(c * hr, hr)],
                send_sem=sems[0].at[a], recv_sem=sems[1].at[a], device_id=(x, y, 1 - c), device_id_type=MESH))
            other = outs[a].at[pl.ds((1 - c) * hr, hr)]
            recvs.append(pltpu.make_async_remote_copy(
                src_ref=other, dst_ref=other, send_sem=sems[0].at[a], recv_sem=sems[1].at[a],
                device_id=(x, y, 1 - c), device_id_type=MESH))
        if small is not None:
            for r, (fx, fy, fc) in enumerate(_relations(), start=1):
                px, py, pc = x ^ fx, y ^ fy, c ^ fc
                sends.append(pltpu.make_async_remote_copy(
                    src_ref=ins[n].at[me], dst_ref=outs[n].at[me],
                    send_sem=sems[2].at[r - 1], recv_sem=sems[3].at[r - 1], device_id=(px, py, pc),
                    device_id_type=MESH))
                theirs = outs[n].at[4 * px + 2 * py + pc]
                recvs.append(pltpu.make_async_remote_copy(
                    src_ref=theirs, dst_ref=theirs, send_sem=sems[2].at[r - 1], recv_sem=sems[3].at[r - 1],
                    device_id=(px, py, pc), device_id_type=MESH))
        return sends, recvs

    def start(ins, outs, sems):
        for cp in copies(ins, outs, sems)[0]:
            cp.start()

    def finish(ins, outs, sems):
        sends, recvs = copies(ins, outs, sems)
        for cp in recvs:
            cp.wait_recv()
        for cp in sends:
            cp.wait_send()

    sems = [pltpu.SemaphoreType.DMA((n,))] * 2 + ([pltpu.SemaphoreType.DMA((7,))] * 2 if small is not None else [])
    return _Side(arrs, [jax.ShapeDtypeStruct(h.shape, h.dtype) for h in arrs], sems, start, finish,
                 aliases={i: i for i in range(len(arrs))})


def _mm_gathering(a, shard, order, *, name, tm=1024):
    s, k = a.shape
    nc = shard.shape[1]
    tm = _row_tile(s, tm)
    tn = nc // 2
    hr = k // 2

    def body(order_ref, a_ref, shard_ref, z_ref, full_ref, wbuf, send_ici, recv_ici, send_d2d, recv_d2d, local_sem, load_sem):
        t, jj, i = pl.program_id(0), pl.program_id(1), pl.program_id(2)
        x, y, c, chips = _place()
        me = 2 * x + y

        def rows(half):
            return pl.ds(half * hr, hr)

        def ici(j, src_chip, to):
            return pltpu.make_async_remote_copy(
                src_ref=shard_ref.at[rows(c)], dst_ref=full_ref.at[src_chip, rows(c)],
                send_sem=send_ici.at[j], recv_sem=recv_ici.at[j], device_id=to, device_id_type=MESH)

        def d2d(j, chip, half):
            blk = full_ref.at[chip, rows(half)]
            return pltpu.make_async_remote_copy(
                src_ref=blk, dst_ref=blk, send_sem=send_d2d.at[j], recv_sem=recv_d2d.at[j],
                device_id=(x, y, 1 - c), device_id_type=MESH)

        def load(src):
            for h in range(2):
                cp = pltpu.make_async_copy(src.at[:, pl.ds(h * tn, tn)], wbuf.at[h], load_sem.at[h])
                cp.start()
            for h in range(2):
                pltpu.make_async_copy(src.at[:, pl.ds(h * tn, tn)], wbuf.at[h], load_sem.at[h]).wait()

        local = pltpu.make_async_copy(shard_ref, full_ref.at[me], local_sem)
        new_shard = jnp.logical_and(jj == 0, i == 0)

        @pl.when(jnp.logical_and(new_shard, t == 0))
        def _():
            local.start()
            for j, chip in enumerate(chips):
                ici(j, me, (*chip, c)).start()
            load(shard_ref)

        for j, (px, py) in enumerate(chips):
            @pl.when(jnp.logical_and(new_shard, t == j + 1))
            def _(j=j, px=px, py=py):
                chip = 2 * px + py
                ici(j, chip, (px, py, c)).wait_recv()
                d2d(j, chip, c).start()
                d2d(j, chip, 1 - c).wait_recv()
                load(full_ref.at[chip])

        z_ref[...] = jnp.dot(a_ref[...], wbuf[jj], preferred_element_type=F32).astype(z_ref.dtype)

        last = functools.reduce(jnp.logical_and, [t == N_CHIPS - 1, jj == 1, i == s // tm - 1])

        @pl.when(last)
        def _():
            for j, (px, py) in enumerate(chips):
                ici(j, me, (px, py, c)).wait_send()
                d2d(j, 2 * px + py, c).wait_send()
            local.wait()

    return pl.pallas_call(
        body,
        grid_spec=pltpu.PrefetchScalarGridSpec(
            num_scalar_prefetch=1, grid=(N_CHIPS, 2, s // tm),
            in_specs=[pl.BlockSpec((tm, k), lambda t, jj, i, order: (i, 0)), HBM],
            out_specs=[pl.BlockSpec((tm, tn), lambda t, jj, i, order: (i, order[t] * 2 + jj)), HBM],
            scratch_shapes=[pltpu.VMEM((2, k, tn), BF16)] + [pltpu.SemaphoreType.DMA((3,))] * 4
            + [pltpu.SemaphoreType.DMA, pltpu.SemaphoreType.DMA((2,))]),
        out_shape=[jax.ShapeDtypeStruct((s, N_CHIPS * nc), BF16), jax.ShapeDtypeStruct((N_CHIPS, k, nc), BF16)],
        name=name, compiler_params=_cparams(),
    )(order, a, shard)


def _col_tile(cols):
    return cols if cols <= 2048 else 512


def _add_sibling(grad, recv, core, *, name):
    k, r, c = grad.shape
    hr = r // 2
    tr = min(hr, 256)
    tc = _col_tile(c)
    nrb = hr // tr

    def body(core_ref, g_ref, r_ref, o_ref):
        o_ref[...] = (g_ref[...] + r_ref[...]).astype(BF16)

    return pl.pallas_call(
        body,
        grid_spec=pltpu.PrefetchScalarGridSpec(
            num_scalar_prefetch=1, grid=(k, nrb, c // tc),
            in_specs=[pl.BlockSpec((None, tr, tc), lambda kk, i, j, core: (kk, core[0] * nrb + i, j)),
                      pl.BlockSpec((None, tr, tc), lambda kk, i, j, core: (kk, i, j))],
            out_specs=pl.BlockSpec((None, tr, tc), lambda kk, i, j, core: (kk, i, j))),
        out_shape=jax.ShapeDtypeStruct((k, hr, c), BF16), name=name, compiler_params=_cparams(),
    )(core, grad, recv)


def _sum_chips(grad, from_sibling, recv, place, *, name):
    _, hr, c = from_sibling.shape
    tr = min(hr, 256)
    tc = _col_tile(c)
    nrb = hr // tr

    def body(place_ref, g_ref, s_ref, r0_ref, r1_ref, r2_ref, o_ref):
        own = g_ref[...] + s_ref[...]
        o_ref[...] = ((own + r0_ref[...].astype(F32)) + r1_ref[...].astype(F32)) + r2_ref[...].astype(F32)

    def rspec(j):
        return pl.BlockSpec((None, tr, tc), lambda i, jj, place: (j, i, jj))

    return pl.pallas_call(
        body,
        grid_spec=pltpu.PrefetchScalarGridSpec(
            num_scalar_prefetch=1, grid=(nrb, c // tc),
            in_specs=[pl.BlockSpec((None, tr, tc), lambda i, jj, place: (place[0], place[1] * nrb + i, jj)),
                      pl.BlockSpec((None, tr, tc), lambda i, jj, place: (place[0], i, jj)),
                      rspec(0), rspec(1), rspec(2)],
            out_specs=pl.BlockSpec((tr, tc), lambda i, jj, place: (place[1] * nrb + i, jj))),
        out_shape=jax.ShapeDtypeStruct((2 * hr, c), F32), name=name, compiler_params=_cparams(),
    )(place, grad, from_sibling, recv, recv, recv)


def _sum_small(small, recv, place):
    _, sr, _ = small.shape

    def body(place_ref, own_ref, r_ref, o_ref):
        acc = own_ref[...]
        for r in range(1, 8):
            acc = acc + r_ref[r]
        o_ref[...] = acc

    return pl.pallas_call(
        body,
        grid_spec=pltpu.PrefetchScalarGridSpec(
            num_scalar_prefetch=1, grid=(1,),
            in_specs=[pl.BlockSpec((None, sr, 128), lambda i, place: (place[2], 0, 0)),
                      pl.BlockSpec((8, sr, 128), lambda i, place: (0, 0, 0))],
            out_specs=pl.BlockSpec((None, sr, 128), lambda i, place: (place[2], 0, 0))),
        out_shape=jax.ShapeDtypeStruct(small.shape, F32), name="sum_small", compiler_params=_cparams(),
    )(place, small, recv)


def _spread_side(vec):
    def copies(ins, outs, sems):
        x, y, c, _ = _place()
        return [pltpu.make_async_remote_copy(
            src_ref=ins[0], dst_ref=outs[0].at[r], send_sem=sems[0].at[r - 1], recv_sem=sems[1].at[r - 1],
            device_id=(x ^ fx, y ^ fy, c ^ fc), device_id_type=MESH)
            for r, (fx, fy, fc) in enumerate(_relations(), start=1)]

    def start(ins, outs, sems):
        for cp in copies(ins, outs, sems):
            cp.start()

    def finish(ins, outs, sems):
        for cp in copies(ins, outs, sems):
            cp.wait()

    return _Side([vec], [jax.ShapeDtypeStruct((8,) + vec.shape, vec.dtype)], [pltpu.SemaphoreType.DMA((7,))] * 2,
                 start, finish)


def _sum_in_device_order(own, spread, place):
    def body(place_ref, own_ref, r_ref, o_ref):
        me = place_ref[2]
        acc = jnp.zeros_like(own_ref[...])
        for d in range(8):
            slot = jnp.where(me == d, 1, me ^ d)
            acc = acc + jnp.where(me == d, own_ref[...], r_ref[slot])
        o_ref[...] = acc

    return pl.pallas_call(
        body,
        grid_spec=pltpu.PrefetchScalarGridSpec(
            num_scalar_prefetch=1, grid=(1,),
            in_specs=[pl.BlockSpec(own.shape, lambda i, place: (0, 0)),
                      pl.BlockSpec(spread.shape, lambda i, place: (0, 0, 0))],
            out_specs=pl.BlockSpec(own.shape, lambda i, place: (0, 0))),
        out_shape=jax.ShapeDtypeStruct(own.shape, F32), name="sum_in_device_order", compiler_params=_cparams(),
    )(place, own, spread)


def _adamw(w, g, m, v, *, name):
    r, c = w.shape
    tr = 256 if r % 256 == 0 else r
    tc = _col_tile(c)
    bc1 = 1.0 - ADAM_B1 ** ADAM_STEP
    bc2 = 1.0 - ADAM_B2 ** ADAM_STEP

    def body(w_ref, g_ref, m_ref, v_ref, d_ref, nm_ref, nv_ref):
        gv = g_ref[...]
        nm = ADAM_B1 * m_ref[...] + (1.0 - ADAM_B1) * gv
        nv = ADAM_B2 * v_ref[...] + (1.0 - ADAM_B2) * (gv * gv)
        d_ref[...] = -ADAM_LR * ((nm / bc1) / (jnp.sqrt(nv / bc2) + ADAM_EPS) + ADAM_WD * w_ref[...])
        nm_ref[...] = nm
        nv_ref[...] = nv

    spec = pl.BlockSpec((tr, tc), lambda i, j: (i, j))
    outs, _ = _call(body, grid=(r // tr, c // tc), in_specs=[spec] * 4, out_specs=[spec] * 3,
                    out_shape=[jax.ShapeDtypeStruct((r, c), F32)] * 3, args=(w, g, m, v), name=name)
    return outs


SMALL_ORDER = ["a_ws", "a_bs", "a_norm_g", "a_ln_g", "a_ln_b", "kv_norm_g", "b_kv", "b_norm_g", "b_bq",
               "b_sinks", "final_norm_g"]
SHARDED_SMALL = {"a_norm_g", "a_ln_g", "a_ln_b"}
PACK_TILE = 8 * 128


def _rows128(a):
    flat = a.reshape(-1)
    return jnp.pad(flat, (0, (-flat.shape[0]) % PACK_TILE)).reshape(-1, 128)


def _pack_rows(parts, multiple):
    rows = [_rows128(p) for p in parts]
    total = sum(r.shape[0] for r in rows)
    pad = (-total) % multiple
    if pad:
        rows.append(jnp.zeros((pad, 128), rows[0].dtype))
    return jnp.concatenate(rows, axis=0)


def _unpack_rows(packed, shapes):
    out, row = [], 0
    for shp in shapes:
        size = math.prod(shp)
        nrow = -(-size // PACK_TILE) * 8
        out.append(packed[row:row + nrow].reshape(-1)[:size].reshape(shp))
        row += nrow
    return out


WEIGHTS = ["a_norm_g", "a_w_in", "a_ln_g", "a_ln_b", "a_ws", "a_bs", "a_w_out", "kv_norm_g", "w_kv", "b_kv",
           "b_norm_g", "b_w_in", "b_bq", "b_sinks", "b_w_out", "final_norm_g"]
BIG = ["a_w_in", "a_w_out", "w_kv", "b_w_in", "b_w_out"]


class _Reduction:
    def __init__(self, names, partials, core, place, small=None):
        self.names, self.partials, self.core, self.place, self.small = names, partials, core, place, small

    def exchange_side(self):
        return _exchange_side(self.partials)

    def took_exchange(self, from_sibling):
        self.from_sibling = from_sibling
        self.chip_sums = [_add_sibling(g, r, self.core, name="add_sibling_" + n)
                          for g, r, n in zip(self.partials, from_sibling, self.names)]

    def scatter_side(self):
        return _scatter_side(self.chip_sums, self.small)

    def took_scatter(self, arrived):
        big = arrived[:len(self.names)]
        self.halves = [_sum_chips(g, fs, r, self.place, name="sum_chips_" + n)
                       for g, fs, r, n in zip(self.partials, self.from_sibling, big, self.names)]
        self.small_mine = _sum_small(self.small, arrived[-1], self.place) if self.small is not None else None

    def share_side(self):
        return _share_side(self.halves, self.small_mine)

    def took_share(self, shared):
        self.grads = dict(zip(self.names, shared[:len(self.names)]))
        self.small_full = shared[-1] if self.small is not None else None


def _step(x, loss_target, p, m, v):
    xi, yi, ci = lax.axis_index("x"), lax.axis_index("y"), lax.axis_index("c")
    chip = 2 * xi + yi
    device = 4 * xi + 2 * yi + ci
    core = jnp.reshape(ci, (1,)).astype(jnp.int32)
    place = jnp.stack([chip, ci, device]).astype(jnp.int32)
    x, tgt = x[0], loss_target[0]
    s = x.shape[0]
    cos, sin = _rope_tables(s)

    shard2d = {n: p[n].reshape(p[n].shape[-2:]) for n in BIG}
    shard_bf = {n: shard2d[n].astype(BF16) for n in BIG}
    ws = p["a_ws"][0]
    ws_t = jnp.swapaxes(ws, 1, 2)
    bs_t = p["a_bs"][0].T
    kv_norm_g, b_kv = p["kv_norm_g"].reshape(1, -1), p["b_kv"].reshape(1, -1)
    final_norm_g = p["final_norm_g"].reshape(1, -1)

    vec_shapes = [p[n].shape for n in ("a_norm_g", "a_ln_g", "a_ln_b")]
    vec_pack = _pack_rows([p["a_norm_g"], p["a_ln_g"], p["a_ln_b"]], 16)
    (vec_all,) = _comm_call(_gather_side([vec_pack]), "gather_vectors")
    vecs = [_unpack_rows(vec_all[k], vec_shapes) for k in range(N_CHIPS)]
    a_norm_g, a_ln_g, a_ln_b = (jnp.concatenate([vk[t] for vk in vecs], axis=-1) for t in range(3))

    (n_a,) = _rms_fwd(x, [a_norm_g], name="rms_a")
    order = jnp.stack([chip, 2 * (1 - xi) + yi, 2 * xi + (1 - yi), 2 * (1 - xi) + (1 - yi)]).astype(jnp.int32)
    z, a_w_in = _mm_gathering(n_a, shard_bf["a_w_in"], order, name="mm_a_in")
    y, (a_w_out,) = _gate_fwd(z, a_ln_g, a_ln_b, ws, bs_t, side=_gather_side([shard_bf["a_w_out"]]))
    a_w_out = a_w_out.reshape(A_WIDTH, D_MODEL)
    h1, (w_kv, b_w_in) = _mm_nn(y, a_w_out, name="mm_a_out", tn=D_MODEL, residual=x,
                                side=_gather_side([shard_bf["w_kv"], shard_bf["b_w_in"]]))
    w_kv = w_kv.reshape(D_MODEL, 2 * KV_WIDTH)
    n_kv, n_b = _rms_fwd(h1, [kv_norm_g, p["b_norm_g"]], name="rms_b")
    kv = _mm_nn(n_kv, w_kv, name="mm_kv", tn=2 * KV_WIDTH)
    kr, vv = _kv_rope(kv, b_kv, cos, sin)
    zb, (b_w_out,) = _mm_nn(n_b, b_w_in, name="mm_b_in", tn=512, tm=1024, out_dtype=BF16,
                            side=_gather_side([shard_bf["b_w_out"]]))
    b_w_out = b_w_out.reshape(B_WIDTH, D_MODEL)
    yb = _attn_fwd(zb, kr, vv, cos, sin, p["b_bq"], p["b_sinks"])
    h2 = _mm_nn(yb, b_w_out, name="mm_b_out", tn=D_MODEL, residual=h1)
    loss_blk, dh2, dh2b, d_final_g = _loss_head(h2, tgt, final_norm_g)

    d_b_w_out = _mm_tn(yb, dh2b, name="mm_d_b_w_out", tm=B_WIDTH, tn=D_MODEL)
    red_bo = _Reduction(["b_w_out"], [d_b_w_out.reshape(N_CHIPS, B_WIDTH // N_CHIPS, D_MODEL)], core, place)
    dyb, got = _mm_nt(dh2b, b_w_out, name="mm_dyb", out_dtype=BF16, side=red_bo.exchange_side())
    red_bo.took_exchange(got)
    dzb, dk_rot, dv, d_bq, d_sinks = _attn_bwd(zb, dyb, kr, vv, cos, sin, p["b_bq"], p["b_sinks"])
    dkv, d_b_kv = _kv_rope_bwd(dk_rot, dv, cos, sin)
    d_b_w_in, got = _mm_tn(n_b, dzb, name="mm_d_b_w_in", tm=D_MODEL, tn=512, shards=N_CHIPS,
                           side=red_bo.scatter_side())
    red_bo.took_scatter(got)
    dn_b, got = _mm_nt(dzb, b_w_in, name="mm_dn_b", side=red_bo.share_side())
    red_bo.took_share(got)
    d_w_kv = _mm_tn(n_kv, dkv, name="mm_d_w_kv", tm=D_MODEL, tn=2 * KV_WIDTH)
    red_bi = _Reduction(["b_w_in", "w_kv"], [d_b_w_in, d_w_kv.reshape(N_CHIPS, D_MODEL // N_CHIPS, 2 * KV_WIDTH)],
                        core, place)
    dn_kv, got = _mm_nt(dkv, w_kv, name="mm_dn_kv", side=red_bi.exchange_side())
    red_bi.took_exchange(got)
    dh1, dh1b, (d_kv_g, d_b_g) = _rms_bwd(h1, [dn_kv, dn_b], [kv_norm_g, p["b_norm_g"]], dh2, name="rms_b_bwd")

    d_a_w_out, got = _mm_tn(y, dh1b, name="mm_d_a_w_out", tm=1024, tn=D_MODEL, side=red_bi.scatter_side())
    red_bi.took_scatter(got)
    red_ao = _Reduction(["a_w_out"], [d_a_w_out.reshape(N_CHIPS, A_WIDTH // N_CHIPS, D_MODEL)], core, place)
    sides = [red_ao.exchange_side(), red_bi.share_side()]
    dy, got = _mm_nt(dh1b, a_w_out, name="mm_dy", tn=1024, out_dtype=BF16, side=_join(sides))
    got = _split(got, sides)
    red_ao.took_exchange(got[0])
    red_bi.took_share(got[1])
    (dz, d_ln_g, d_ln_b, d_ws, d_bs_t), got = _gate_bwd(z, dy, a_ln_g, a_ln_b, ws, ws_t, bs_t,
                                                        side=red_ao.scatter_side())
    red_ao.took_scatter(got)
    d_a_w_in, got = _mm_tn(n_a, dz, name="mm_d_a_w_in", tm=D_MODEL, tn=1536, shards=N_CHIPS,
                           side=red_ao.share_side())
    red_ao.took_share(got)

    small = {
        "a_ws": d_ws, "a_bs": d_bs_t.T, "a_ln_g": d_ln_g, "a_ln_b": d_ln_b,
        "kv_norm_g": d_kv_g, "b_kv": d_b_kv, "b_norm_g": d_b_g, "b_bq": d_bq,
        "b_sinks": d_sinks[0:1, :N_Q_HEADS], "final_norm_g": d_final_g,
    }
    packed = [n for n in SMALL_ORDER if n != "a_norm_g"]
    small_shapes = [small[n].shape for n in packed] + [(1, 1)]
    small_pack = _pack_rows([small[n] for n in packed] + [loss_blk[0:1, 0:1]], 64)
    seg = small_pack.shape[0] // 8
    red_ai = _Reduction(["a_w_in"], [d_a_w_in], core, place, small=small_pack.reshape(8, seg, 128))
    red_ai.took_exchange(_comm_call(red_ai.exchange_side(), "exchange_last"))
    dn_a, got = _mm_nt(dz, a_w_in, name="mm_dn_a", tm=256, side=red_ai.scatter_side())
    red_ai.took_scatter(got)
    dx, _, (d_a_g,) = _rms_bwd(x, [dn_a], [a_norm_g], dh1, name="rms_a_bwd")
    d_a_g = _rows128(d_a_g)
    sides = [red_ai.share_side(), _spread_side(d_a_g)]
    got = _split(_comm_call(_join(sides), "share_last"), sides)
    red_ai.took_share(got[0])
    small_full = dict(zip(packed + ["loss"], _unpack_rows(red_ai.small_full.reshape(8 * seg, 128), small_shapes)))
    small_full["a_norm_g"] = _sum_in_device_order(d_a_g, got[1][0], place).reshape(1, -1)
    loss = small_full["loss"].reshape(())

    grad_big = {**red_bo.grads, **red_bi.grads, **red_ao.grads, **red_ai.grads}
    grads = {}
    for n in SMALL_ORDER:
        gfull = small_full[n]
        if n in SHARDED_SMALL:
            width = p[n].shape[-1]
            gfull = lax.dynamic_slice_in_dim(gfull, chip * width, width, axis=-1)
        grads[n] = gfull.reshape(p[n].shape)
    for n in BIG:
        grads[n] = grad_big[n].reshape(p[n].shape)

    delta, new_m, new_v = {}, {}, {}
    for n in BIG:
        d, nm, nv = _adamw(shard2d[n], grad_big[n], m[n].reshape(shard2d[n].shape), v[n].reshape(shard2d[n].shape),
                           name="adamw_" + n)
        delta[n], new_m[n], new_v[n] = d.reshape(p[n].shape), nm.reshape(p[n].shape), nv.reshape(p[n].shape)
    shapes = [p[n].shape for n in SMALL_ORDER]
    packs = [_pack_rows([src[n] for n in SMALL_ORDER], 8) for src in (p, grads, m, v)]
    outs = _adamw(*packs, name="adamw_small")
    for res, packed in zip((delta, new_m, new_v), outs):
        for n, val in zip(SMALL_ORDER, _unpack_rows(packed, shapes)):
            res[n] = val

    return (loss, dx[None], *[grads[n] for n in WEIGHTS], *[delta[n] for n in WEIGHTS],
            *[new_m[n] for n in WEIGHTS], *[new_v[n] for n in WEIGHTS])


def kernel(x, a_norm_g, a_w_in, a_ln_g, a_ln_b, a_ws, a_bs, a_w_out, kv_norm_g, w_kv, b_kv, b_norm_g, b_w_in, b_bq, b_sinks, b_w_out, final_norm_g, loss_target, m_a_norm_g, m_a_w_in, m_a_ln_g, m_a_ln_b, m_a_ws, m_a_bs, m_a_w_out, m_kv_norm_g, m_w_kv, m_b_kv, m_b_norm_g, m_b_w_in, m_b_bq, m_b_sinks, m_b_w_out, m_final_norm_g, v_a_norm_g, v_a_w_in, v_a_ln_g, v_a_ln_b, v_a_ws, v_a_bs, v_a_w_out, v_kv_norm_g, v_w_kv, v_b_kv, v_b_norm_g, v_b_w_in, v_b_bq, v_b_sinks, v_b_w_out, v_final_norm_g):
    p = dict(a_norm_g=a_norm_g, a_w_in=a_w_in, a_ln_g=a_ln_g, a_ln_b=a_ln_b, a_ws=a_ws, a_bs=a_bs, a_w_out=a_w_out,
             kv_norm_g=kv_norm_g, w_kv=w_kv, b_kv=b_kv, b_norm_g=b_norm_g, b_w_in=b_w_in, b_bq=b_bq, b_sinks=b_sinks,
             b_w_out=b_w_out, final_norm_g=final_norm_g)
    m = dict(a_norm_g=m_a_norm_g, a_w_in=m_a_w_in, a_ln_g=m_a_ln_g, a_ln_b=m_a_ln_b, a_ws=m_a_ws, a_bs=m_a_bs,
             a_w_out=m_a_w_out, kv_norm_g=m_kv_norm_g, w_kv=m_w_kv, b_kv=m_b_kv, b_norm_g=m_b_norm_g, b_w_in=m_b_w_in,
             b_bq=m_b_bq, b_sinks=m_b_sinks, b_w_out=m_b_w_out, final_norm_g=m_final_norm_g)
    v = dict(a_norm_g=v_a_norm_g, a_w_in=v_a_w_in, a_ln_g=v_a_ln_g, a_ln_b=v_a_ln_b, a_ws=v_a_ws, a_bs=v_a_bs,
             a_w_out=v_a_w_out, kv_norm_g=v_kv_norm_g, w_kv=v_w_kv, b_kv=v_b_kv, b_norm_g=v_b_norm_g, b_w_in=v_b_w_in,
             b_bq=v_b_bq, b_sinks=v_b_sinks, b_w_out=v_b_w_out, final_norm_g=v_final_norm_g)
    return _step(x, loss_target, p, m, v)
```

```python
import functools
import math

import jax
import jax.numpy as jnp
from jax import lax
from jax.experimental import pallas as pl
from jax.experimental.pallas import tpu as pltpu

F32 = jnp.float32
BF16 = jnp.bfloat16

D_MODEL = 1024
CHUNK = 128
A_WIDTH = 2048
A_GROUPS = 16
HEAD_DIM = 64
N_Q_HEADS = 16
N_KV_HEADS = 2
Q_PER_KV = 8
B_WIDTH = 1024
KV_WIDTH = 128
ROPE_THETA = 10000.0
EPS = 1e-5
N_CHIPS = 4

ADAM_LR = 0.001
ADAM_B1 = 0.9
ADAM_B2 = 0.999
ADAM_EPS = 1e-08
ADAM_WD = 0.01
ADAM_STEP = 10

VMEM_LIMIT = 48 * 1024 * 1024
MESH = pl.DeviceIdType.MESH
NEG_BIG = -1e30
HBM = pl.BlockSpec(memory_space=pl.ANY)

NN = (((1,), (0,)), ((), ()))
NT = (((1,), (1,)), ((), ()))
TN = (((0,), (0,)), ((), ()))


def _cparams(**kw):
    return pltpu.CompilerParams(vmem_limit_bytes=VMEM_LIMIT, **kw)


class _Side:
    def __init__(self, ins, out_shapes, sems, start, finish, aliases=None):
        self.ins, self.out_shapes, self.sems = list(ins), list(out_shapes), list(sems)
        self.start, self.finish = start, finish
        self.aliases = dict(aliases or {})


def _join(sides):
    sides = [s for s in sides if s is not None]
    if not sides:
        return None
    offs, i, o, m = [], 0, 0, 0
    for s in sides:
        offs.append((i, o, m))
        i, o, m = i + len(s.ins), o + len(s.out_shapes), m + len(s.sems)

    def run(which):
        def go(ins, outs, sems):
            for s, (a, b, c) in zip(sides, offs):
                getattr(s, which)(ins[a:a + len(s.ins)], outs[b:b + len(s.out_shapes)], sems[c:c + len(s.sems)])
        return go

    aliases = {}
    for s, (a, b, _) in zip(sides, offs):
        aliases.update({a + k: b + v for k, v in s.aliases.items()})
    return _Side([x for s in sides for x in s.ins], [x for s in sides for x in s.out_shapes],
                 [x for s in sides for x in s.sems], run("start"), run("finish"), aliases)


def _split(side_outs, sides):
    out, pos = [], 0
    for s in sides:
        out.append(list(side_outs[pos:pos + len(s.out_shapes)]))
        pos += len(s.out_shapes)
    return out


def _call(body, *, grid, in_specs, out_specs, out_shape, args, name, scratch=(), side=None):
    in_specs, out_specs, out_shape, scratch = list(in_specs), list(out_specs), list(out_shape), list(scratch)
    if side is None:
        res = pl.pallas_call(body, grid=grid, in_specs=in_specs, out_specs=out_specs, out_shape=out_shape,
                             scratch_shapes=scratch, name=name, compiler_params=_cparams())(*args)
        return list(res), []
    n_in, n_out, n_sc = len(in_specs), len(out_specs), len(scratch)
    s_in, s_out = len(side.ins), len(side.out_shapes)

    def wrapped(*refs):
        ins, refs = refs[:n_in], refs[n_in:]
        side_ins, refs = refs[:s_in], refs[s_in:]
        outs, refs = refs[:n_out], refs[n_out:]
        side_outs, refs = refs[:s_out], refs[s_out:]
        scr, side_sems = refs[:n_sc], refs[n_sc:]
        ids = [pl.program_id(a) for a in range(len(grid))]
        first = functools.reduce(jnp.logical_and, [i == 0 for i in ids])
        last = functools.reduce(jnp.logical_and, [i == g - 1 for i, g in zip(ids, grid)])

        @pl.when(first)
        def _():
            side.start(side_ins, side_outs, side_sems)

        body(*ins, *outs, *scr)

        @pl.when(last)
        def _():
            side.finish(side_ins, side_outs, side_sems)

    res = pl.pallas_call(
        wrapped, grid=grid, in_specs=in_specs + [HBM] * s_in, out_specs=out_specs + [HBM] * s_out,
        out_shape=out_shape + side.out_shapes, scratch_shapes=scratch + side.sems,
        input_output_aliases={n_in + k: n_out + v for k, v in side.aliases.items()},
        name=name, compiler_params=_cparams(),
    )(*args, *side.ins)
    return list(res[:n_out]), list(res[n_out:])


def _comm_call(side, name):
    s_in, s_out = len(side.ins), len(side.out_shapes)

    def body(*refs):
        ins, outs, sems = refs[:s_in], refs[s_in:s_in + s_out], refs[s_in + s_out:]
        side.start(ins, outs, sems)
        side.finish(ins, outs, sems)

    return list(pl.pallas_call(
        body, in_specs=[HBM] * s_in, out_specs=[HBM] * s_out, out_shape=side.out_shapes, scratch_shapes=side.sems,
        input_output_aliases=side.aliases, name=name,
    )(*side.ins))


def _matmul(a, b, *, dims, grid, a_spec, b_spec, o_spec, out_shape, name, acc_axis=None,
            residual=None, r_spec=None, side=None):
    has_res = residual is not None

    def body(*refs):
        if has_res:
            a_ref, b_ref, r_ref, o_ref = refs
        else:
            a_ref, b_ref, o_ref = refs
        part = lax.dot_general(a_ref[...], b_ref[...], dims, preferred_element_type=F32)
        if acc_axis is None:
            if has_res:
                part = part + r_ref[...]
            o_ref[...] = part.astype(o_ref.dtype)
        else:
            k = pl.program_id(acc_axis)

            @pl.when(k == 0)
            def _():
                o_ref[...] = part

            @pl.when(k > 0)
            def _():
                o_ref[...] += part

    in_specs = [a_spec, b_spec] + ([r_spec] if has_res else [])
    args = (a, b) + ((residual,) if has_res else ())
    (out,), side_outs = _call(body, grid=grid, in_specs=in_specs, out_specs=[o_spec], out_shape=[out_shape],
                              args=args, name=name, side=side)
    return (out, side_outs) if side is not None else out


def _row_tile(s, want):
    return min(s, want)


def _mm_nn(a, b, *, name, tn, out_dtype=F32, residual=None, tm=512, side=None):
    s, k = a.shape
    tm = _row_tile(s, tm)
    if b.ndim == 3:
        nsh, _, nc = b.shape
        npb = nc // tn
        n = nsh * nc
        b_spec = pl.BlockSpec((None, k, tn), lambda i, j: (j // npb, 0, j % npb))
    else:
        n = b.shape[1]
        b_spec = pl.BlockSpec((k, tn), lambda i, j: (0, j))
    return _matmul(
        a, b, dims=NN, grid=(s // tm, n // tn),
        a_spec=pl.BlockSpec((tm, k), lambda i, j: (i, 0)), b_spec=b_spec,
        o_spec=pl.BlockSpec((tm, tn), lambda i, j: (i, j)),
        out_shape=jax.ShapeDtypeStruct((s, n), out_dtype), name=name, side=side,
        residual=residual, r_spec=pl.BlockSpec((tm, tn), lambda i, j: (i, j)) if residual is not None else None)


def _mm_nt(a, b, *, name, tn=None, tm=512, out_dtype=F32, side=None):
    s, k = a.shape
    tm = _row_tile(s, tm)
    if b.ndim == 3:
        nsh, n, kc = b.shape

        def body(a_ref, b_ref, o_ref):
            acc = None
            for sh in range(nsh):
                part = lax.dot_general(a_ref[:, sh * kc:(sh + 1) * kc], b_ref[sh], NT, preferred_element_type=F32)
                acc = part if acc is None else acc + part
            o_ref[...] = acc

        (out,), side_outs = _call(
            body, grid=(s // tm,),
            in_specs=[pl.BlockSpec((tm, k), lambda i: (i, 0)), pl.BlockSpec((nsh, n, kc), lambda i: (0, 0, 0))],
            out_specs=[pl.BlockSpec((tm, n), lambda i: (i, 0))], out_shape=[jax.ShapeDtypeStruct((s, n), F32)],
            args=(a, b), name=name, side=side)
        return (out, side_outs) if side is not None else out
    n = b.shape[0]
    tn = n if tn is None else tn
    return _matmul(
        a, b, dims=NT, grid=(s // tm, n // tn),
        a_spec=pl.BlockSpec((tm, k), lambda i, j: (i, 0)),
        b_spec=pl.BlockSpec((tn, k), lambda i, j: (j, 0)),
        o_spec=pl.BlockSpec((tm, tn), lambda i, j: (i, j)),
        out_shape=jax.ShapeDtypeStruct((s, n), out_dtype), name=name, side=side)


def _mm_tn(a, b, *, name, tm, tn, tk=2048, shards=None, side=None):
    s, m = a.shape
    n = b.shape[1]
    tk = _row_tile(s, tk)
    if shards is None:
        o_spec = pl.BlockSpec((tm, tn), lambda i, j, kk: (i, j))
        out_shape = jax.ShapeDtypeStruct((m, n), F32)
    else:
        assert tm == m
        nc = n // shards
        npb = nc // tn
        o_spec = pl.BlockSpec((None, m, tn), lambda i, j, kk: (j // npb, 0, j % npb))
        out_shape = jax.ShapeDtypeStruct((shards, m, nc), F32)
    return _matmul(
        a, b, dims=TN, grid=(m // tm, n // tn, s // tk), acc_axis=2,
        a_spec=pl.BlockSpec((tk, tm), lambda i, j, kk: (kk, i)),
        b_spec=pl.BlockSpec((tk, tn), lambda i, j, kk: (kk, j)),
        o_spec=o_spec, out_shape=out_shape, name=name, side=side)


def _rstd(x):
    return lax.rsqrt(jnp.mean(x * x, axis=-1, keepdims=True) + EPS)


def _rms_fwd(x, gains, *, name, tr=256):
    s, d = x.shape
    tr = _row_tile(s, tr)
    ng = len(gains)

    def body(*refs):
        xv = refs[0][...]
        xh = xv * _rstd(xv)
        for t in range(ng):
            refs[1 + ng + t][...] = (xh * refs[1 + t][...]).astype(BF16)

    row = pl.BlockSpec((tr, d), lambda i: (i, 0))
    vec = pl.BlockSpec((1, d), lambda i: (0, 0))
    outs, _ = _call(body, grid=(s // tr,), in_specs=[row] + [vec] * ng, out_specs=[row] * ng,
                    out_shape=[jax.ShapeDtypeStruct((s, d), BF16)] * ng, args=(x, *gains), name=name)
    return outs


def _rms_bwd(x, dns, gains, dres, *, name, tr=256):
    s, d = x.shape
    tr = _row_tile(s, tr)
    ng = len(gains)

    def body(*refs):
        x_ref = refs[0]
        dn_refs = refs[1:1 + ng]
        g_refs = refs[1 + ng:1 + 2 * ng]
        dres_ref = refs[1 + 2 * ng]
        dx_ref, dxb_ref = refs[2 + 2 * ng], refs[3 + 2 * ng]
        dg_refs = refs[4 + 2 * ng:]
        i = pl.program_id(0)
        xv = x_ref[...]
        r = _rstd(xv)
        xh = xv * r
        acc = jnp.zeros_like(xv)
        for t in range(ng):
            dn = dn_refs[t][...]
            acc = acc + dn * g_refs[t][...]
            dgt = jnp.sum(dn * xh, axis=0, keepdims=True)

            @pl.when(i == 0)
            def _(t=t, dgt=dgt):
                dg_refs[t][...] = dgt

            @pl.when(i > 0)
            def _(t=t, dgt=dgt):
                dg_refs[t][...] += dgt

        dx = dres_ref[...] + r * (acc - xh * jnp.mean(acc * xh, axis=-1, keepdims=True))
        dx_ref[...] = dx
        dxb_ref[...] = dx.astype(BF16)

    row = pl.BlockSpec((tr, d), lambda i: (i, 0))
    vec = pl.BlockSpec((1, d), lambda i: (0, 0))
    outs, _ = _call(
        body, grid=(s // tr,), in_specs=[row] + [row] * ng + [vec] * ng + [row],
        out_specs=[row, row] + [vec] * ng,
        out_shape=[jax.ShapeDtypeStruct((s, d), F32), jax.ShapeDtypeStruct((s, d), BF16)]
        + [jax.ShapeDtypeStruct((1, d), F32)] * ng,
        args=(x, *dns, *gains, dres), name=name)
    return outs[0], outs[1], outs[2:]


def _loss_head(h, tgt, gain, *, tr=256):
    s, d = h.shape
    tr = _row_tile(s, tr)

    def body(h_ref, t_ref, g_ref, loss_ref, dh_ref, dhb_ref, dg_ref):
        i = pl.program_id(0)
        hv = h_ref[...]
        g = g_ref[...]
        r = _rstd(hv)
        xh = hv * r
        diff = xh * g - t_ref[...]
        part = 0.5 / d * jnp.sum(jnp.sum(diff * diff, axis=-1, keepdims=True), axis=0, keepdims=True)
        dout = diff * (1.0 / d)
        a = dout * g
        dh = r * (a - xh * jnp.mean(a * xh, axis=-1, keepdims=True))
        dh_ref[...] = dh
        dhb_ref[...] = dh.astype(BF16)
        dgt = jnp.sum(dout * xh, axis=0, keepdims=True)
        lpart = jnp.broadcast_to(part, (8, 128))

        @pl.when(i == 0)
        def _():
            dg_ref[...] = dgt
            loss_ref[...] = lpart

        @pl.when(i > 0)
        def _():
            dg_ref[...] += dgt
            loss_ref[...] += lpart

    row = pl.BlockSpec((tr, d), lambda i: (i, 0))
    vec = pl.BlockSpec((1, d), lambda i: (0, 0))
    outs, _ = _call(
        body, grid=(s // tr,), in_specs=[row, row, vec],
        out_specs=[pl.BlockSpec((8, 128), lambda i: (0, 0)), row, row, vec],
        out_shape=[jax.ShapeDtypeStruct((8, 128), F32), jax.ShapeDtypeStruct((s, d), F32),
                   jax.ShapeDtypeStruct((s, d), BF16), jax.ShapeDtypeStruct((1, d), F32)],
        args=(h, tgt, gain), name="loss_head")
    return outs


def _accumulate(i, ref, value):
    @pl.when(i == 0)
    def _():
        ref[...] = value

    @pl.when(i > 0)
    def _():
        ref[...] += value


def _mm_residual_norms(y, w, res, gains, *, name, tm=512, side=None):
    s, k = y.shape
    d = w.shape[1]
    tm = _row_tile(s, tm)
    ng = len(gains)

    def body(y_ref, w_ref, r_ref, *rest):
        g_refs, h_ref, n_refs = rest[:ng], rest[ng], rest[ng + 1:]
        h = r_ref[...] + jnp.dot(y_ref[...], w_ref[...], preferred_element_type=F32)
        h_ref[...] = h
        xh = h * _rstd(h)
        for t in range(ng):
            n_refs[t][...] = (xh * g_refs[t][...]).astype(BF16)

    row = pl.BlockSpec((tm, d), lambda i: (i, 0))
    vec = pl.BlockSpec((1, d), lambda i: (0, 0))
    return _call(
        body, grid=(s // tm,),
        in_specs=[pl.BlockSpec((tm, k), lambda i: (i, 0)), pl.BlockSpec((k, d), lambda i: (0, 0)), row] + [vec] * ng,
        out_specs=[row] * (1 + ng),
        out_shape=[jax.ShapeDtypeStruct((s, d), F32)] + [jax.ShapeDtypeStruct((s, d), BF16)] * ng,
        args=(y, w, res, *gains), name=name, side=side)


def _mm_residual_loss(y, w, res, tgt, gain, *, name, tm=512):
    s, k = y.shape
    d = w.shape[1]
    tm = _row_tile(s, tm)

    def body(y_ref, w_ref, r_ref, t_ref, g_ref, loss_ref, dh_ref, dhb_ref, dg_ref):
        i = pl.program_id(0)
        hv = r_ref[...] + jnp.dot(y_ref[...], w_ref[...], preferred_element_type=F32)
        g = g_ref[...]
        r = _rstd(hv)
        xh = hv * r
        diff = xh * g - t_ref[...]
        part = 0.5 / d * jnp.sum(jnp.sum(diff * diff, axis=-1, keepdims=True), axis=0, keepdims=True)
        dout = diff * (1.0 / d)
        a = dout * g
        dh = r * (a - xh * jnp.mean(a * xh, axis=-1, keepdims=True))
        dh_ref[...] = dh
        dhb_ref[...] = dh.astype(BF16)
        _accumulate(i, dg_ref, jnp.sum(dout * xh, axis=0, keepdims=True))
        _accumulate(i, loss_ref, jnp.broadcast_to(part, (8, 128)))

    row = pl.BlockSpec((tm, d), lambda i: (i, 0))
    vec = pl.BlockSpec((1, d), lambda i: (0, 0))
    outs, _ = _call(
        body, grid=(s // tm,),
        in_specs=[pl.BlockSpec((tm, k), lambda i: (i, 0)), pl.BlockSpec((k, d), lambda i: (0, 0)), row, row, vec],
        out_specs=[pl.BlockSpec((8, 128), lambda i: (0, 0)), row, row, vec],
        out_shape=[jax.ShapeDtypeStruct((8, 128), F32), jax.ShapeDtypeStruct((s, d), F32),
                   jax.ShapeDtypeStruct((s, d), BF16), jax.ShapeDtypeStruct((1, d), F32)],
        args=(y, w, res, tgt, gain), name=name)
    return outs


def _mm_nt_rms_bwd(terms, x, dres, *, name, tm, side=None):
    s, d = x.shape
    tm = _row_tile(s, tm)
    nt = len(terms)

    def body(*refs):
        a_refs, b_refs, g_refs = refs[0:3 * nt:3], refs[1:3 * nt:3], refs[2:3 * nt:3]
        x_ref, dres_ref = refs[3 * nt], refs[3 * nt + 1]
        dx_ref, dxb_ref = refs[3 * nt + 2], refs[3 * nt + 3]
        dg_refs = refs[3 * nt + 4:]
        i = pl.program_id(0)
        xv = x_ref[...]
        r = _rstd(xv)
        xh = xv * r
        acc = jnp.zeros_like(xv)
        for t in range(nt):
            b_ref = b_refs[t]
            if len(b_ref.shape) == 3:
                kc = b_ref.shape[2]
                dn = None
                for sh in range(b_ref.shape[0]):
                    part = lax.dot_general(a_refs[t][:, sh * kc:(sh + 1) * kc], b_ref[sh], NT, preferred_element_type=F32)
                    dn = part if dn is None else dn + part
            else:
                dn = lax.dot_general(a_refs[t][...], b_ref[...], NT, preferred_element_type=F32)
            acc = acc + dn * g_refs[t][...]
            _accumulate(i, dg_refs[t], jnp.sum(dn * xh, axis=0, keepdims=True))
        dx = dres_ref[...] + r * (acc - xh * jnp.mean(acc * xh, axis=-1, keepdims=True))
        dx_ref[...] = dx
        dxb_ref[...] = dx.astype(BF16)

    row = pl.BlockSpec((tm, d), lambda i: (i, 0))
    vec = pl.BlockSpec((1, d), lambda i: (0, 0))
    in_specs, args = [], []
    for a, b, g in terms:
        in_specs += [pl.BlockSpec((tm, a.shape[1]), lambda i: (i, 0)),
                     pl.BlockSpec(b.shape, (lambda i: (0, 0, 0)) if b.ndim == 3 else (lambda i: (0, 0))), vec]
        args += [a, b, g]
    return _call(
        body, grid=(s // tm,), in_specs=in_specs + [row, row], out_specs=[row, row] + [vec] * nt,
        out_shape=[jax.ShapeDtypeStruct((s, d), F32), jax.ShapeDtypeStruct((s, d), BF16)]
        + [jax.ShapeDtypeStruct((1, d), F32)] * nt,
        args=(*args, x, dres), name=name, side=side)


def _causal_mask(transposed=False):
    row = lax.broadcasted_iota(jnp.int32, (CHUNK, CHUNK), 0)
    col = lax.broadcasted_iota(jnp.int32, (CHUNK, CHUNK), 1)
    return col >= row if transposed else col <= row


def _silu_parts(g):
    sg = jax.nn.sigmoid(g)
    return g * sg, sg * (1.0 + g * (1.0 - sg))


def _gate_fwd(z, ln_g, ln_b, ws, bs_t, *, tr=256, side=None):
    s = z.shape[0]
    tr = _row_tile(s, tr)
    w = A_WIDTH

    def body(u_ref, v_ref, g_ref, lg_ref, lb_ref, ws_ref, bst_ref, y_ref):
        v = v_ref[...].astype(F32)
        mu = jnp.mean(v, axis=-1, keepdims=True)
        xc = v - mu
        rs = lax.rsqrt(jnp.mean(xc * xc, axis=-1, keepdims=True) + EPS)
        vln = (xc * rs * lg_ref[...] + lb_ref[...]).astype(BF16)
        mask = _causal_mask()
        for grp in range(A_GROUPS):
            cols = slice(grp * CHUNK, (grp + 1) * CHUNK)
            wsm = jnp.where(mask, ws_ref[grp], 0.0).astype(BF16)
            bcol = bst_ref[:, grp:grp + 1]
            for ci in range(tr // CHUNK):
                rows = slice(ci * CHUNK, (ci + 1) * CHUNK)
                sv = jnp.dot(wsm, vln[rows, cols], preferred_element_type=F32) + bcol
                gv = g_ref[rows, cols].astype(F32)
                y_ref[rows, cols] = (u_ref[rows, cols].astype(F32) * sv * (gv * jax.nn.sigmoid(gv))).astype(BF16)

    vec = pl.BlockSpec((1, w), lambda i: (0, 0))
    (y,), side_outs = _call(
        body, grid=(s // tr,),
        in_specs=[pl.BlockSpec((tr, w), lambda i: (i, 0)), pl.BlockSpec((tr, w), lambda i: (i, 1)),
                  pl.BlockSpec((tr, w), lambda i: (i, 2)), vec, vec,
                  pl.BlockSpec((A_GROUPS, CHUNK, CHUNK), lambda i: (0, 0, 0)),
                  pl.BlockSpec((CHUNK, A_GROUPS), lambda i: (0, 0))],
        out_specs=[pl.BlockSpec((tr, w), lambda i: (i, 0))],
        out_shape=[jax.ShapeDtypeStruct((s, w), BF16)], args=(z, z, z, ln_g, ln_b, ws, bs_t), name="gate_fwd",
        side=side)
    return y, side_outs


def _gate_bwd(z, dy, ln_g, ln_b, ws, ws_t, bs_t, *, tr=256, side=None):
    s = z.shape[0]
    tr = _row_tile(s, tr)
    w = A_WIDTH
    nsteps = s // tr

    def body(u_ref, v_ref, g_ref, dy_ref, lg_ref, lb_ref, ws_ref, wst_ref, bst_ref,
             dz_ref, dlg_ref, dlb_ref, dws_ref, dbst_ref, dvln_sc, dsv_sc):
        i = pl.program_id(0)

        @pl.when(i == 0)
        def _():
            dws_ref[...] = jnp.zeros_like(dws_ref)
            dsv_sc[...] = jnp.zeros_like(dsv_sc)

        v = v_ref[...].astype(F32)
        mu = jnp.mean(v, axis=-1, keepdims=True)
        xc = v - mu
        rs = lax.rsqrt(jnp.mean(xc * xc, axis=-1, keepdims=True) + EPS)
        xh = xc * rs
        lg = lg_ref[...]
        vln = (xh * lg + lb_ref[...]).astype(BF16)
        mask = _causal_mask()
        mask_t = _causal_mask(transposed=True)
        for grp in range(A_GROUPS):
            cols = slice(grp * CHUNK, (grp + 1) * CHUNK)
            wsm = jnp.where(mask, ws_ref[grp], 0.0).astype(BF16)
            wsm_t = jnp.where(mask_t, wst_ref[grp], 0.0).astype(BF16)
            bcol = bst_ref[:, grp:grp + 1]
            for ci in range(tr // CHUNK):
                rows = slice(ci * CHUNK, (ci + 1) * CHUNK)
                vb = vln[rows, cols]
                sv = jnp.dot(wsm, vb, preferred_element_type=F32) + bcol
                uv = u_ref[rows, cols].astype(F32)
                silu, dsilu = _silu_parts(g_ref[rows, cols].astype(F32))
                dyv = dy_ref[rows, cols].astype(F32)
                dyu = dyv * uv
                dz_ref[rows, cols] = (dyv * sv * silu).astype(BF16)
                dz_ref[rows, 2 * w + grp * CHUNK:2 * w + (grp + 1) * CHUNK] = (dyu * sv * dsilu).astype(BF16)
                dsv = dyu * silu
                dsvb = dsv.astype(BF16)
                dvln_sc[rows, cols] = jnp.dot(wsm_t, dsvb, preferred_element_type=F32)
                dws_ref[grp] += lax.dot_general(dsvb, vb, NT, preferred_element_type=F32)
                dsv_sc[grp] += dsv
        dvln = dvln_sc[...]
        dlg_t = jnp.sum(dvln * xh, axis=0, keepdims=True)
        dlb_t = jnp.sum(dvln, axis=0, keepdims=True)
        a = dvln * lg
        dv = rs * (a - jnp.mean(a, axis=-1, keepdims=True) - xh * jnp.mean(a * xh, axis=-1, keepdims=True))
        dz_ref[:, w:2 * w] = dv.astype(BF16)

        @pl.when(i == 0)
        def _():
            dlg_ref[...] = dlg_t
            dlb_ref[...] = dlb_t

        @pl.when(i > 0)
        def _():
            dlg_ref[...] += dlg_t
            dlb_ref[...] += dlb_t

        @pl.when(i == nsteps - 1)
        def _():
            for grp in range(A_GROUPS):
                dws_ref[grp] = jnp.where(mask, dws_ref[grp], 0.0)
                dbst_ref[:, grp:grp + 1] = jnp.sum(dsv_sc[grp], axis=-1, keepdims=True)

    vec = pl.BlockSpec((1, w), lambda i: (0, 0))
    wsspec = pl.BlockSpec((A_GROUPS, CHUNK, CHUNK), lambda i: (0, 0, 0))
    bsspec = pl.BlockSpec((CHUNK, A_GROUPS), lambda i: (0, 0))
    return _call(
        body, grid=(nsteps,),
        in_specs=[pl.BlockSpec((tr, w), lambda i: (i, 0)), pl.BlockSpec((tr, w), lambda i: (i, 1)),
                  pl.BlockSpec((tr, w), lambda i: (i, 2)), pl.BlockSpec((tr, w), lambda i: (i, 0)),
                  vec, vec, wsspec, wsspec, bsspec],
        out_specs=[pl.BlockSpec((tr, 3 * w), lambda i: (i, 0)), vec, vec, wsspec, bsspec],
        out_shape=[jax.ShapeDtypeStruct((s, 3 * w), BF16), jax.ShapeDtypeStruct((1, w), F32),
                   jax.ShapeDtypeStruct((1, w), F32), jax.ShapeDtypeStruct((A_GROUPS, CHUNK, CHUNK), F32),
                   jax.ShapeDtypeStruct((CHUNK, A_GROUPS), F32)],
        scratch=[pltpu.VMEM((tr, w), F32), pltpu.VMEM((A_GROUPS, CHUNK, CHUNK), F32)],
        args=(z, z, z, dy, ln_g, ln_b, ws, ws_t, bs_t), name="gate_bwd", side=side)


HEADS_PER_BLOCK = 128 // HEAD_DIM
BLOCKS_PER_KV = Q_PER_KV // HEADS_PER_BLOCK
SCALE = HEAD_DIM ** -0.5


def _rope_tables(s):
    inv_freq = ROPE_THETA ** (-jnp.arange(0, HEAD_DIM, 2, dtype=F32) / HEAD_DIM)
    ang = jnp.arange(s, dtype=F32)[:, None] * inv_freq[None, :]
    cos, sin = jnp.cos(ang), jnp.sin(ang)
    cos2 = jnp.concatenate([cos, cos], axis=-1)
    sin2 = jnp.concatenate([-sin, sin], axis=-1)
    return jnp.tile(cos2, (1, 2)), jnp.tile(sin2, (1, 2))


def _swap_halves(x):
    n = x.shape[-1]
    lane = lax.broadcasted_iota(jnp.int32, x.shape, x.ndim - 1)
    first = (lane % HEAD_DIM) < (HEAD_DIM // 2)
    return jnp.where(first, pltpu.roll(x, n - HEAD_DIM // 2, x.ndim - 1), pltpu.roll(x, HEAD_DIM // 2, x.ndim - 1))


def _left_half(rows):
    return lax.broadcasted_iota(jnp.int32, (rows, 128), 1) < HEAD_DIM


def _dup_heads(x):
    left = _left_half(x.shape[0])
    swapped = pltpu.roll(x, HEAD_DIM, 1)
    return jnp.concatenate([jnp.where(left, x, swapped), jnp.where(left, swapped, x)], axis=-1)


def _fold_heads(a):
    b0, b1 = a[:, :128], a[:, 128:]
    f0 = b0 + pltpu.roll(b0, HEAD_DIM, 1)
    f1 = b1 + pltpu.roll(b1, HEAD_DIM, 1)
    return jnp.where(_left_half(a.shape[0]), f0, f1)


def _kv_rope(kv, b_kv, cos, sin, *, tr=512):
    s = kv.shape[0]
    tr = _row_tile(s, tr)

    def body(kv_ref, b_ref, c_ref, s_ref, k_ref, v_ref):
        x = kv_ref[...] + b_ref[...]
        k = x[:, :KV_WIDTH]
        k_ref[...] = _dup_heads(k * c_ref[...] + _swap_halves(k) * s_ref[...]).astype(BF16)
        v_ref[...] = _dup_heads(x[:, KV_WIDTH:]).astype(BF16)

    tab = pl.BlockSpec((tr, KV_WIDTH), lambda i: (i, 0))
    wide = pl.BlockSpec((tr, 2 * KV_WIDTH), lambda i: (i, 0))
    outs, _ = _call(body, grid=(s // tr,),
                    in_specs=[wide, pl.BlockSpec((1, 2 * KV_WIDTH), lambda i: (0, 0)), tab, tab],
                    out_specs=[wide, wide], out_shape=[jax.ShapeDtypeStruct((s, 2 * KV_WIDTH), BF16)] * 2,
                    args=(kv, b_kv, cos, sin), name="kv_rope")
    return outs


def _kv_rope_bwd(dk2, dv2, cos, sin, *, tr=512):
    s = dk2.shape[0]
    tr = _row_tile(s, tr)

    def body(dk_ref, dv_ref, c_ref, s_ref, dkv_ref, db_ref):
        i = pl.program_id(0)
        d = _fold_heads(dk_ref[...])
        dk = d * c_ref[...] + _swap_halves(d * s_ref[...])
        dvv = _fold_heads(dv_ref[...])
        dkv_ref[:, :KV_WIDTH] = dk.astype(BF16)
        dkv_ref[:, KV_WIDTH:] = dvv.astype(BF16)
        sk = jnp.sum(dk, axis=0, keepdims=True)
        sv = jnp.sum(dvv, axis=0, keepdims=True)

        @pl.when(i == 0)
        def _():
            db_ref[:, :KV_WIDTH] = sk
            db_ref[:, KV_WIDTH:] = sv

        @pl.when(i > 0)
        def _():
            db_ref[:, :KV_WIDTH] += sk
            db_ref[:, KV_WIDTH:] += sv

    tab = pl.BlockSpec((tr, KV_WIDTH), lambda i: (i, 0))
    wide = pl.BlockSpec((tr, 2 * KV_WIDTH), lambda i: (i, 0))
    outs, _ = _call(body, grid=(s // tr,), in_specs=[wide, wide, tab, tab],
                    out_specs=[wide, pl.BlockSpec((1, 2 * KV_WIDTH), lambda i: (0, 0))],
                    out_shape=[jax.ShapeDtypeStruct((s, 2 * KV_WIDTH), BF16),
                               jax.ShapeDtypeStruct((1, 2 * KV_WIDTH), F32)],
                    args=(dk2, dv2, cos, sin), name="kv_rope_bwd")
    return outs


def _stacked_mask(i):
    cols = Q_PER_KV * CHUNK
    k = lax.broadcasted_iota(jnp.int32, (2 * CHUNK, cols), 0)
    q = lax.broadcasted_iota(jnp.int32, (2 * CHUNK, cols), 1) & (CHUNK - 1)
    first_valid = jnp.where(i > 0, 0, CHUNK)
    prev = (k < CHUNK) & (k > q) & (k >= first_valid)
    cur = (k >= CHUNK) & (k - CHUNK <= q)
    return prev | cur


def _stack_heads(blocks, left):
    parts = []
    for b in blocks:
        parts.append(jnp.where(left, b, jnp.zeros_like(b)))
        parts.append(jnp.where(left, jnp.zeros_like(b), b))
    return jnp.concatenate(parts, axis=0)


def _unstack_heads(xt):
    top = lax.broadcasted_iota(jnp.int32, (128, CHUNK), 0) < HEAD_DIM
    return [jnp.where(top, xt[:, (2 * b) * CHUNK:(2 * b + 1) * CHUNK], xt[:, (2 * b + 1) * CHUNK:(2 * b + 2) * CHUNK]).T
            for b in range(BLOCKS_PER_KV)]


def _sink_row(sk_ref, kvh):
    return jnp.concatenate([jnp.full((1, CHUNK), sk_ref[0, kvh * Q_PER_KV + r], F32) for r in range(Q_PER_KV)], axis=1)


def _stacked_probs(qs, kd, mask, sink):
    sc = lax.dot_general(kd, qs, NT, preferred_element_type=F32) * SCALE
    sc = jnp.where(mask, sc, NEG_BIG)
    m = jnp.maximum(jnp.max(sc, axis=0, keepdims=True), sink)
    p = jnp.exp(sc - m)
    esink = jnp.exp(sink - m)
    inv = 1.0 / (jnp.sum(p, axis=0, keepdims=True) + esink)
    return p * inv, esink * inv


def _lane_block(b):
    return slice(b * 128, (b + 1) * 128)


def _rope_blocks(zq_ref, bq_ref, cos, sin, kvh):
    out = []
    for b in range(BLOCKS_PER_KV):
        cols = _lane_block(kvh * BLOCKS_PER_KV + b)
        q = zq_ref[:, cols].astype(F32) + bq_ref[:, cols]
        out.append((q * cos + _swap_halves(q) * sin).astype(BF16))
    return out


def _attn_specs():
    qspec = pl.BlockSpec((CHUNK, B_WIDTH), lambda i: (i, 0))
    gspec = pl.BlockSpec((CHUNK, B_WIDTH), lambda i: (i, 1))
    prev = pl.BlockSpec((CHUNK, 2 * KV_WIDTH), lambda i: (jnp.maximum(i - 1, 0), 0))
    cur = pl.BlockSpec((CHUNK, 2 * KV_WIDTH), lambda i: (i, 0))
    tab = pl.BlockSpec((CHUNK, KV_WIDTH), lambda i: (i, 0))
    bq = pl.BlockSpec((1, B_WIDTH), lambda i: (0, 0))
    sinks = pl.BlockSpec(memory_space=pltpu.SMEM)
    return qspec, gspec, prev, cur, tab, bq, sinks


def _attn_fwd(zb, k2, v2, cos, sin, b_bq, sinks):
    s = zb.shape[0]

    def body(zq_ref, zg_ref, kp_ref, kc_ref, vp_ref, vc_ref, c_ref, s_ref, bq_ref, sk_ref, y_ref):
        i = pl.program_id(0)
        cos, sin = c_ref[...], s_ref[...]
        kcat = jnp.concatenate([kp_ref[...], kc_ref[...]], axis=0)
        vcat = jnp.concatenate([vp_ref[...], vc_ref[...]], axis=0)
        mask = _stacked_mask(i)
        left = _left_half(CHUNK)
        for kvh in range(N_KV_HEADS):
            qs = _stack_heads(_rope_blocks(zq_ref, bq_ref, cos, sin, kvh), left)
            p, _ = _stacked_probs(qs, kcat[:, _lane_block(kvh)], mask, _sink_row(sk_ref, kvh))
            ot = lax.dot_general(vcat[:, _lane_block(kvh)], p.astype(BF16), TN, preferred_element_type=F32)
            for b, ob in enumerate(_unstack_heads(ot)):
                cols = _lane_block(kvh * BLOCKS_PER_KV + b)
                gv = zg_ref[:, cols].astype(F32)
                y_ref[:, cols] = (ob * (gv * jax.nn.sigmoid(gv))).astype(BF16)

    qspec, gspec, prev, cur, tab, bq, sk = _attn_specs()
    (y,), _ = _call(body, grid=(s // CHUNK,), in_specs=[qspec, gspec, prev, cur, prev, cur, tab, tab, bq, sk],
                    out_specs=[qspec], out_shape=[jax.ShapeDtypeStruct((s, B_WIDTH), BF16)],
                    args=(zb, zb, k2, k2, v2, v2, cos, sin, b_bq, sinks), name="attn_fwd")
    return y


def _attn_bwd(zb, dyb, k2, v2, cos, sin, b_bq, sinks):
    s = zb.shape[0]

    def body(zq_ref, zg_ref, dy_ref, kp_ref, kc_ref, vp_ref, vc_ref, c_ref, s_ref, bq_ref, sk_ref,
             dz_ref, dk_ref, dv_ref, dbq_ref, dsk_ref):
        i = pl.program_id(0)

        @pl.when(i == 0)
        def _():
            dk_ref[...] = jnp.zeros_like(dk_ref)
            dv_ref[...] = jnp.zeros_like(dv_ref)
            dbq_ref[...] = jnp.zeros_like(dbq_ref)
            dsk_ref[...] = jnp.zeros_like(dsk_ref)

        cos, sin = c_ref[...], s_ref[...]
        kcat = jnp.concatenate([kp_ref[...], kc_ref[...]], axis=0)
        vcat = jnp.concatenate([vp_ref[...], vc_ref[...]], axis=0)
        mask = _stacked_mask(i)
        left = _left_half(CHUNK)
        lane = lax.broadcasted_iota(jnp.int32, (1, 128), 1)
        dsk_row = jnp.zeros((1, 128), F32)
        cur_rows = pl.ds(pl.multiple_of(i * CHUNK, CHUNK), CHUNK)
        for kvh in range(N_KV_HEADS):
            kd, vd = kcat[:, _lane_block(kvh)], vcat[:, _lane_block(kvh)]
            qs = _stack_heads(_rope_blocks(zq_ref, bq_ref, cos, sin, kvh), left)
            p, psink = _stacked_probs(qs, kd, mask, _sink_row(sk_ref, kvh))
            pb = p.astype(BF16)
            ot = lax.dot_general(vd, pb, TN, preferred_element_type=F32)
            gates, dys = [], []
            for b in range(BLOCKS_PER_KV):
                cols = _lane_block(kvh * BLOCKS_PER_KV + b)
                gates.append(_silu_parts(zg_ref[:, cols].astype(F32)))
                dys.append(dy_ref[:, cols].astype(F32))
            dos = _stack_heads([(dyv * silu).astype(BF16) for dyv, (silu, _) in zip(dys, gates)], left)
            dp = lax.dot_general(vd, dos, NT, preferred_element_type=F32)
            delta = jnp.sum(p * dp, axis=0, keepdims=True)
            ds = (p * (dp - delta) * SCALE).astype(BF16)
            dqt = lax.dot_general(kd, ds, TN, preferred_element_type=F32)
            dk_part = jnp.dot(ds, qs, preferred_element_type=F32)
            dv_part = jnp.dot(pb, dos, preferred_element_type=F32)
            dk_ref[cur_rows, _lane_block(kvh)] += dk_part[CHUNK:]
            dv_ref[cur_rows, _lane_block(kvh)] += dv_part[CHUNK:]

            @pl.when(i > 0)
            def _(kvh=kvh, dk_part=dk_part, dv_part=dv_part):
                prev_rows = pl.ds(pl.multiple_of((i - 1) * CHUNK, CHUNK), CHUNK)
                dk_ref[prev_rows, _lane_block(kvh)] += dk_part[:CHUNK]
                dv_ref[prev_rows, _lane_block(kvh)] += dv_part[:CHUNK]

            sink_grad = psink * delta
            for r in range(Q_PER_KV):
                dsink = -jnp.sum(sink_grad[:, r * CHUNK:(r + 1) * CHUNK], axis=1, keepdims=True)
                dsk_row = dsk_row + jnp.where(lane == kvh * Q_PER_KV + r, dsink, 0.0)
            blocks = zip(_unstack_heads(ot), _unstack_heads(dqt), dys, gates)
            for b, (ob, dqr, dyv, (_, dsilu)) in enumerate(blocks):
                blk = kvh * BLOCKS_PER_KV + b
                dq = dqr * cos + _swap_halves(dqr * sin)
                dbq_ref[:, _lane_block(blk)] += jnp.sum(dq, axis=0, keepdims=True)
                dz_ref[:, _lane_block(blk)] = dq.astype(BF16)
                dz_ref[:, _lane_block(B_WIDTH // 128 + blk)] = (dyv * ob * dsilu).astype(BF16)
        dsk_ref[0:1, :] += dsk_row

    qspec, gspec, prev, cur, tab, bq, sk = _attn_specs()
    full = pl.BlockSpec((s, 2 * KV_WIDTH), lambda i: (0, 0))
    outs, _ = _call(
        body, grid=(s // CHUNK,),
        in_specs=[qspec, gspec, qspec, prev, cur, prev, cur, tab, tab, bq, sk],
        out_specs=[pl.BlockSpec((CHUNK, 2 * B_WIDTH), lambda i: (i, 0)), full, full, bq,
                   pl.BlockSpec((8, 128), lambda i: (0, 0))],
        out_shape=[jax.ShapeDtypeStruct((s, 2 * B_WIDTH), BF16), jax.ShapeDtypeStruct((s, 2 * KV_WIDTH), F32),
                   jax.ShapeDtypeStruct((s, 2 * KV_WIDTH), F32), jax.ShapeDtypeStruct((1, B_WIDTH), F32),
                   jax.ShapeDtypeStruct((8, 128), F32)],
        args=(zb, zb, dyb, k2, k2, v2, v2, cos, sin, b_bq, sinks), name="attn_bwd")
    return outs


def _place():
    x, y, c = lax.axis_index("x"), lax.axis_index("y"), lax.axis_index("c")
    return x, y, c, [(1 - x, y), (x, 1 - y), (1 - x, 1 - y)]


def _relations():
    return [(r >> 2 & 1, r >> 1 & 1, r & 1) for r in range(1, 8)]


def _gather_side(arrs):
    n = len(arrs)

    def copies(ins, outs, sems):
        send_ici, recv_ici, send_d2d, recv_d2d, local_sem = sems
        x, y, c, chips = _place()
        me = 2 * x + y

        def rows(a, half):
            hr = arrs[a].shape[0] // 2
            return pl.ds(half * hr, hr)

        def ici(a, j, src_chip, to):
            return pltpu.make_async_remote_copy(
                src_ref=ins[a].at[rows(a, c)], dst_ref=outs[a].at[src_chip, rows(a, c)],
                send_sem=send_ici.at[a, j], recv_sem=recv_ici.at[a, j], device_id=to, device_id_type=MESH)

        def d2d(a, j, chip, half):
            blk = outs[a].at[chip, rows(a, half)]
            return pltpu.make_async_remote_copy(
                src_ref=blk, dst_ref=blk, send_sem=send_d2d.at[a, j], recv_sem=recv_d2d.at[a, j],
                device_id=(x, y, 1 - c), device_id_type=MESH)

        local = [pltpu.make_async_copy(ins[a], outs[a].at[me], local_sem.at[a]) for a in range(n)]
        pairs = [(a, j, chip) for a in range(n) for j, chip in enumerate(chips)]
        return c, me, local, ici, d2d, pairs

    def start(ins, outs, sems):
        c, me, local, ici, _, pairs = copies(ins, outs, sems)
        for cp in local:
            cp.start()
        for a, j, chip in pairs:
            ici(a, j, me, (*chip, c)).start()

    def finish(ins, outs, sems):
        c, me, local, ici, d2d, pairs = copies(ins, outs, sems)
        for a, j, (px, py) in pairs:
            ici(a, j, 2 * px + py, (px, py, c)).wait_recv()
            d2d(a, j, 2 * px + py, c).start()
        for a, j, (px, py) in pairs:
            d2d(a, j, 2 * px + py, 1 - c).wait_recv()
        for a, j, (px, py) in pairs:
            ici(a, j, me, (px, py, c)).wait_send()
            d2d(a, j, 2 * px + py, c).wait_send()
        for cp in local:
            cp.wait()

    return _Side(arrs, [jax.ShapeDtypeStruct((N_CHIPS,) + a.shape, a.dtype) for a in arrs],
                 [pltpu.SemaphoreType.DMA((n, 3))] * 4 + [pltpu.SemaphoreType.DMA((n,))], start, finish)


def _exchange_side(grads):
    n = len(grads)

    def copies(ins, outs, sems):
        send_sem, recv_sem = sems
        x, y, c, _ = _place()
        cps = []
        for a in range(n):
            hr = grads[a].shape[1] // 2
            cps.append(pltpu.make_async_remote_copy(
                src_ref=ins[a].at[:, pl.ds((1 - c) * hr, hr), :], dst_ref=outs[a],
                send_sem=send_sem.at[a], recv_sem=recv_sem.at[a], device_id=(x, y, 1 - c), device_id_type=MESH))
        return cps

    def start(ins, outs, sems):
        for cp in copies(ins, outs, sems):
            cp.start()

    def finish(ins, outs, sems):
        for cp in copies(ins, outs, sems):
            cp.wait()

    return _Side(grads, [jax.ShapeDtypeStruct((g.shape[0], g.shape[1] // 2, g.shape[2]), g.dtype) for g in grads],
                 [pltpu.SemaphoreType.DMA((n,))] * 2, start, finish)


def _scatter_side(chip_sums, small=None):
    n = len(chip_sums)
    arrs = list(chip_sums) + ([small] if small is not None else [])

    def copies(ins, outs, sems):
        x, y, c, chips = _place()
        cps = []
        for a in range(n):
            for j, (px, py) in enumerate(chips):
                cps.append(pltpu.make_async_remote_copy(
                    src_ref=ins[a].at[2 * px + py], dst_ref=outs[a].at[j],
                    send_sem=sems[0].at[a, j], recv_sem=sems[1].at[a, j], device_id=(px, py, c), device_id_type=MESH))
        if small is not None:
            for r, (fx, fy, fc) in enumerate(_relations(), start=1):
                px, py, pc = x ^ fx, y ^ fy, c ^ fc
                cps.append(pltpu.make_async_remote_copy(
                    src_ref=ins[n].at[4 * px + 2 * py + pc], dst_ref=outs[n].at[r],
                    send_sem=sems[2].at[r - 1], recv_sem=sems[3].at[r - 1], device_id=(px, py, pc),
                    device_id_type=MESH))
        return cps

    def start(ins, outs, sems):
        for cp in copies(ins, outs, sems):
            cp.start()

    def finish(ins, outs, sems):
        for cp in copies(ins, outs, sems):
            cp.wait()

    shapes = [jax.ShapeDtypeStruct((3,) + t.shape[1:], t.dtype) for t in chip_sums]
    sems = [pltpu.SemaphoreType.DMA((n, 3))] * 2
    if small is not None:
        shapes.append(jax.ShapeDtypeStruct(small.shape, small.dtype))
        sems += [pltpu.SemaphoreType.DMA((7,))] * 2
    return _Side(arrs, shapes, sems, start, finish)


def _share_side(halves, small=None):
    n = len(halves)
    arrs = list(halves) + ([small] if small is not None else [])

    def copies(ins, outs, sems):
        x, y, c, _ = _place()
        me = 4 * x + 2 * y + c
        sends, recvs = [], []
        for a in range(n):
            hr = halves[a].shape[0] // 2
            sends.append(pltpu.make_async_remote_copy(
                src_ref=ins[a].at[pl.ds(c * hr, hr)], dst_ref=outs[a].at[pl.ds(c * hr, hr)],
                send_sem=sems[0].at[a], recv_sem=sems[1].at[a], device_id=(x, y, 1 - c), device_id_type=MESH))
            other = outs[a].at[pl.ds((1 - c) * hr, hr)]
            recvs.append(pltpu.make_async_remote_copy(
                src_ref=other, dst_ref=other, send_sem=sems[0].at[a], recv_sem=sems[1].at[a],
                device_id=(x, y, 1 - c), device_id_type=MESH))
        if small is not None:
            for r, (fx, fy, fc) in enumerate(_relations(), start=1):
                px, py, pc = x ^ fx, y ^ fy, c ^ fc
                sends.append(pltpu.make_async_remote_copy(
                    src_ref=ins[n].at[me], dst_ref=outs[n].at[me],
                    send_sem=sems[2].at[r - 1], recv_sem=sems[3].at[r - 1], device_id=(px, py, pc),
                    device_id_type=MESH))
                theirs = outs[n].at[4 * px + 2 * py + pc]
                recvs.append(pltpu.make_async_remote_copy(
                    src_ref=theirs, dst_ref=theirs, send_sem=sems[2].at[r - 1], recv_sem=sems[3].at[r - 1],
                    device_id=(px, py, pc), device_id_type=MESH))
        return sends, recvs

    def start(ins, outs, sems):
        for cp in copies(ins, outs, sems)[0]:
            cp.start()

    def finish(ins, outs, sems):
        sends, recvs = copies(ins, outs, sems)
        for cp in recvs:
            cp.wait_recv()
        for cp in sends:
            cp.wait_send()

    sems = [pltpu.SemaphoreType.DMA((n,))] * 2 + ([pltpu.SemaphoreType.DMA((7,))] * 2 if small is not None else [])
    return _Side(arrs, [jax.ShapeDtypeStruct(h.shape, h.dtype) for h in arrs], sems, start, finish,
                 aliases={i: i for i in range(len(arrs))})


def _mm_gathering(a, shard, order, *, name, tm=1024):
    s, k = a.shape
    nc = shard.shape[1]
    tm = _row_tile(s, tm)
    tn = nc // 2
    hr = k // 2

    def body(order_ref, a_ref, shard_ref, z_ref, full_ref, wbuf, send_ici, recv_ici, send_d2d, recv_d2d, local_sem, load_sem):
        t, jj, i = pl.program_id(0), pl.program_id(1), pl.program_id(2)
        x, y, c, chips = _place()
        me = 2 * x + y

        def rows(half):
            return pl.ds(half * hr, hr)

        def ici(j, src_chip, to):
            return pltpu.make_async_remote_copy(
                src_ref=shard_ref.at[rows(c)], dst_ref=full_ref.at[src_chip, rows(c)],
                send_sem=send_ici.at[j], recv_sem=recv_ici.at[j], device_id=to, device_id_type=MESH)

        def d2d(j, chip, half):
            blk = full_ref.at[chip, rows(half)]
            return pltpu.make_async_remote_copy(
                src_ref=blk, dst_ref=blk, send_sem=send_d2d.at[j], recv_sem=recv_d2d.at[j],
                device_id=(x, y, 1 - c), device_id_type=MESH)

        def load(src):
            for h in range(2):
                cp = pltpu.make_async_copy(src.at[:, pl.ds(h * tn, tn)], wbuf.at[h], load_sem.at[h])
                cp.start()
            for h in range(2):
                pltpu.make_async_copy(src.at[:, pl.ds(h * tn, tn)], wbuf.at[h], load_sem.at[h]).wait()

        local = pltpu.make_async_copy(shard_ref, full_ref.at[me], local_sem)
        new_shard = jnp.logical_and(jj == 0, i == 0)

        @pl.when(jnp.logical_and(new_shard, t == 0))
        def _():
            local.start()
            for j, chip in enumerate(chips):
                ici(j, me, (*chip, c)).start()
            load(shard_ref)

        for j, (px, py) in enumerate(chips):
            @pl.when(jnp.logical_and(new_shard, t == j + 1))
            def _(j=j, px=px, py=py):
                chip = 2 * px + py
                ici(j, chip, (px, py, c)).wait_recv()
                d2d(j, chip, c).start()
                d2d(j, chip, 1 - c).wait_recv()
                load(full_ref.at[chip])

        z_ref[...] = jnp.dot(a_ref[...], wbuf[jj], preferred_element_type=F32).astype(z_ref.dtype)

        last = functools.reduce(jnp.logical_and, [t == N_CHIPS - 1, jj == 1, i == s // tm - 1])

        @pl.when(last)
        def _():
            for j, (px, py) in enumerate(chips):
                ici(j, me, (px, py, c)).wait_send()
                d2d(j, 2 * px + py, c).wait_send()
            local.wait()

    return pl.pallas_call(
        body,
        grid_spec=pltpu.PrefetchScalarGridSpec(
            num_scalar_prefetch=1, grid=(N_CHIPS, 2, s // tm),
            in_specs=[pl.BlockSpec((tm, k), lambda t, jj, i, order: (i, 0)), HBM],
            out_specs=[pl.BlockSpec((tm, tn), lambda t, jj, i, order: (i, order[t] * 2 + jj)), HBM],
            scratch_shapes=[pltpu.VMEM((2, k, tn), BF16)] + [pltpu.SemaphoreType.DMA((3,))] * 4
            + [pltpu.SemaphoreType.DMA, pltpu.SemaphoreType.DMA((2,))]),
        out_shape=[jax.ShapeDtypeStruct((s, N_CHIPS * nc), BF16), jax.ShapeDtypeStruct((N_CHIPS, k, nc), BF16)],
        name=name, compiler_params=_cparams(),
    )(order, a, shard)


def _col_tile(cols):
    return cols if cols <= 2048 else 512


def _add_sibling(grad, recv, core, *, name):
    k, r, c = grad.shape
    hr = r // 2
    tr = min(hr, 256)
    tc = _col_tile(c)
    nrb = hr // tr

    def body(core_ref, g_ref, r_ref, o_ref):
        o_ref[...] = (g_ref[...] + r_ref[...]).astype(BF16)

    return pl.pallas_call(
        body,
        grid_spec=pltpu.PrefetchScalarGridSpec(
            num_scalar_prefetch=1, grid=(k, nrb, c // tc),
            in_specs=[pl.BlockSpec((None, tr, tc), lambda kk, i, j, core: (kk, core[0] * nrb + i, j)),
                      pl.BlockSpec((None, tr, tc), lambda kk, i, j, core: (kk, i, j))],
            out_specs=pl.BlockSpec((None, tr, tc), lambda kk, i, j, core: (kk, i, j))),
        out_shape=jax.ShapeDtypeStruct((k, hr, c), BF16), name=name, compiler_params=_cparams(),
    )(core, grad, recv)


def _sum_chips(grad, from_sibling, recv, place, *, name):
    _, hr, c = from_sibling.shape
    tr = min(hr, 256)
    tc = _col_tile(c)
    nrb = hr // tr

    def body(place_ref, g_ref, s_ref, r0_ref, r1_ref, r2_ref, o_ref):
        own = g_ref[...] + s_ref[...]
        o_ref[...] = ((own + r0_ref[...].astype(F32)) + r1_ref[...].astype(F32)) + r2_ref[...].astype(F32)

    def rspec(j):
        return pl.BlockSpec((None, tr, tc), lambda i, jj, place: (j, i, jj))

    return pl.pallas_call(
        body,
        grid_spec=pltpu.PrefetchScalarGridSpec(
            num_scalar_prefetch=1, grid=(nrb, c // tc),
            in_specs=[pl.BlockSpec((None, tr, tc), lambda i, jj, place: (place[0], place[1] * nrb + i, jj)),
                      pl.BlockSpec((None, tr, tc), lambda i, jj, place: (place[0], i, jj)),
                      rspec(0), rspec(1), rspec(2)],
            out_specs=pl.BlockSpec((tr, tc), lambda i, jj, place: (place[1] * nrb + i, jj))),
        out_shape=jax.ShapeDtypeStruct((2 * hr, c), F32), name=name, compiler_params=_cparams(),
    )(place, grad, from_sibling, recv, recv, recv)


def _sum_small(small, recv, place):
    _, sr, _ = small.shape

    def body(place_ref, own_ref, r_ref, o_ref):
        acc = own_ref[...]
        for r in range(1, 8):
            acc = acc + r_ref[r]
        o_ref[...] = acc

    return pl.pallas_call(
        body,
        grid_spec=pltpu.PrefetchScalarGridSpec(
            num_scalar_prefetch=1, grid=(1,),
            in_specs=[pl.BlockSpec((None, sr, 128), lambda i, place: (place[2], 0, 0)),
                      pl.BlockSpec((8, sr, 128), lambda i, place: (0, 0, 0))],
            out_specs=pl.BlockSpec((None, sr, 128), lambda i, place: (place[2], 0, 0))),
        out_shape=jax.ShapeDtypeStruct(small.shape, F32), name="sum_small", compiler_params=_cparams(),
    )(place, small, recv)


def _spread_side(vec):
    def copies(ins, outs, sems):
        x, y, c, _ = _place()
        return [pltpu.make_async_remote_copy(
            src_ref=ins[0], dst_ref=outs[0].at[r], send_sem=sems[0].at[r - 1], recv_sem=sems[1].at[r - 1],
            device_id=(x ^ fx, y ^ fy, c ^ fc), device_id_type=MESH)
            for r, (fx, fy, fc) in enumerate(_relations(), start=1)]

    def start(ins, outs, sems):
        for cp in copies(ins, outs, sems):
            cp.start()

    def finish(ins, outs, sems):
        for cp in copies(ins, outs, sems):
            cp.wait()

    return _Side([vec], [jax.ShapeDtypeStruct((8,) + vec.shape, vec.dtype)], [pltpu.SemaphoreType.DMA((7,))] * 2,
                 start, finish)


def _sum_in_device_order(own, spread, place):
    def body(place_ref, own_ref, r_ref, o_ref):
        me = place_ref[2]
        acc = jnp.zeros_like(own_ref[...])
        for d in range(8):
            slot = jnp.where(me == d, 1, me ^ d)
            acc = acc + jnp.where(me == d, own_ref[...], r_ref[slot])
        o_ref[...] = acc

    return pl.pallas_call(
        body,
        grid_spec=pltpu.PrefetchScalarGridSpec(
            num_scalar_prefetch=1, grid=(1,),
            in_specs=[pl.BlockSpec(own.shape, lambda i, place: (0, 0)),
                      pl.BlockSpec(spread.shape, lambda i, place: (0, 0, 0))],
            out_specs=pl.BlockSpec(own.shape, lambda i, place: (0, 0))),
        out_shape=jax.ShapeDtypeStruct(own.shape, F32), name="sum_in_device_order", compiler_params=_cparams(),
    )(place, own, spread)


def _adamw(w, g, m, v, *, name):
    r, c = w.shape
    tr = 256 if r % 256 == 0 else r
    tc = _col_tile(c)
    bc1 = 1.0 - ADAM_B1 ** ADAM_STEP
    bc2 = 1.0 - ADAM_B2 ** ADAM_STEP

    def body(w_ref, g_ref, m_ref, v_ref, d_ref, nm_ref, nv_ref):
        gv = g_ref[...]
        nm = ADAM_B1 * m_ref[...] + (1.0 - ADAM_B1) * gv
        nv = ADAM_B2 * v_ref[...] + (1.0 - ADAM_B2) * (gv * gv)
        d_ref[...] = -ADAM_LR * ((nm / bc1) / (jnp.sqrt(nv / bc2) + ADAM_EPS) + ADAM_WD * w_ref[...])
        nm_ref[...] = nm
        nv_ref[...] = nv

    spec = pl.BlockSpec((tr, tc), lambda i, j: (i, j))
    outs, _ = _call(body, grid=(r // tr, c // tc), in_specs=[spec] * 4, out_specs=[spec] * 3,
                    out_shape=[jax.ShapeDtypeStruct((r, c), F32)] * 3, args=(w, g, m, v), name=name)
    return outs


SMALL_ORDER = ["a_ws", "a_bs", "a_norm_g", "a_ln_g", "a_ln_b", "kv_norm_g", "b_kv", "b_norm_g", "b_bq",
               "b_sinks", "final_norm_g"]
SHARDED_SMALL = {"a_norm_g", "a_ln_g", "a_ln_b"}
PACK_TILE = 8 * 128


def _rows128(a):
    flat = a.reshape(-1)
    return jnp.pad(flat, (0, (-flat.shape[0]) % PACK_TILE)).reshape(-1, 128)


def _pack_rows(parts, multiple):
    rows = [_rows128(p) for p in parts]
    total = sum(r.shape[0] for r in rows)
    pad = (-total) % multiple
    if pad:
        rows.append(jnp.zeros((pad, 128), rows[0].dtype))
    return jnp.concatenate(rows, axis=0)


def _unpack_rows(packed, shapes):
    out, row = [], 0
    for shp in shapes:
        size = math.prod(shp)
        nrow = -(-size // PACK_TILE) * 8
        out.append(packed[row:row + nrow].reshape(-1)[:size].reshape(shp))
        row += nrow
    return out


WEIGHTS = ["a_norm_g", "a_w_in", "a_ln_g", "a_ln_b", "a_ws", "a_bs", "a_w_out", "kv_norm_g", "w_kv", "b_kv",
           "b_norm_g", "b_w_in", "b_bq", "b_sinks", "b_w_out", "final_norm_g"]
BIG = ["a_w_in", "a_w_out", "w_kv", "b_w_in", "b_w_out"]


class _Reduction:
    def __init__(self, names, partials, core, place, small=None):
        self.names, self.partials, self.core, self.place, self.small = names, partials, core, place, small

    def exchange_side(self):
        return _exchange_side(self.partials)

    def took_exchange(self, from_sibling):
        self.from_sibling = from_sibling
        self.chip_sums = [_add_sibling(g, r, self.core, name="add_sibling_" + n)
                          for g, r, n in zip(self.partials, from_sibling, self.names)]

    def scatter_side(self):
        return _scatter_side(self.chip_sums, self.small)

    def took_scatter(self, arrived):
        big = arrived[:len(self.names)]
        self.halves = [_sum_chips(g, fs, r, self.place, name="sum_chips_" + n)
                       for g, fs, r, n in zip(self.partials, self.from_sibling, big, self.names)]
        self.small_mine = _sum_small(self.small, arrived[-1], self.place) if self.small is not None else None

    def share_side(self):
        return _share_side(self.halves, self.small_mine)

    def took_share(self, shared):
        self.grads = dict(zip(self.names, shared[:len(self.names)]))
        self.small_full = shared[-1] if self.small is not None else None


def _step(x, loss_target, p, m, v):
    xi, yi, ci = lax.axis_index("x"), lax.axis_index("y"), lax.axis_index("c")
    chip = 2 * xi + yi
    device = 4 * xi + 2 * yi + ci
    core = jnp.reshape(ci, (1,)).astype(jnp.int32)
    place = jnp.stack([chip, ci, device]).astype(jnp.int32)
    x, tgt = x[0], loss_target[0]
    s = x.shape[0]
    cos, sin = _rope_tables(s)

    shard2d = {n: p[n].reshape(p[n].shape[-2:]) for n in BIG}
    shard_bf = {n: shard2d[n].astype(BF16) for n in BIG}
    ws = p["a_ws"][0]
    ws_t = jnp.swapaxes(ws, 1, 2)
    bs_t = p["a_bs"][0].T
    kv_norm_g, b_kv = p["kv_norm_g"].reshape(1, -1), p["b_kv"].reshape(1, -1)
    final_norm_g = p["final_norm_g"].reshape(1, -1)

    vec_shapes = [p[n].shape for n in ("a_norm_g", "a_ln_g", "a_ln_b")]
    vec_pack = _pack_rows([p["a_norm_g"], p["a_ln_g"], p["a_ln_b"]], 16)
    (vec_all,) = _comm_call(_gather_side([vec_pack]), "gather_vectors")
    vecs = [_unpack_rows(vec_all[k], vec_shapes) for k in range(N_CHIPS)]
    a_norm_g, a_ln_g, a_ln_b = (jnp.concatenate([vk[t] for vk in vecs], axis=-1) for t in range(3))

    (n_a,) = _rms_fwd(x, [a_norm_g], name="rms_a")
    order = jnp.stack([chip, 2 * (1 - xi) + yi, 2 * xi + (1 - yi), 2 * (1 - xi) + (1 - yi)]).astype(jnp.int32)
    z, a_w_in = _mm_gathering(n_a, shard_bf["a_w_in"], order, name="mm_a_in")
    y, (a_w_out,) = _gate_fwd(z, a_ln_g, a_ln_b, ws, bs_t, side=_gather_side([shard_bf["a_w_out"]]))
    a_w_out = a_w_out.reshape(A_WIDTH, D_MODEL)
    (h1, n_kv, n_b), (w_kv, b_w_in) = _mm_residual_norms(
        y, a_w_out, x, [kv_norm_g, p["b_norm_g"]], name="mm_a_out",
        side=_gather_side([shard_bf["w_kv"], shard_bf["b_w_in"]]))
    w_kv = w_kv.reshape(D_MODEL, 2 * KV_WIDTH)
    kv = _mm_nn(n_kv, w_kv, name="mm_kv", tn=2 * KV_WIDTH)
    kr, vv = _kv_rope(kv, b_kv, cos, sin)
    zb, (b_w_out,) = _mm_nn(n_b, b_w_in, name="mm_b_in", tn=512, tm=1024, out_dtype=BF16,
                            side=_gather_side([shard_bf["b_w_out"]]))
    b_w_out = b_w_out.reshape(B_WIDTH, D_MODEL)
    yb = _attn_fwd(zb, kr, vv, cos, sin, p["b_bq"], p["b_sinks"])
    loss_blk, dh2, dh2b, d_final_g = _mm_residual_loss(yb, b_w_out, h1, tgt, final_norm_g, name="mm_b_out")

    d_b_w_out = _mm_tn(yb, dh2b, name="mm_d_b_w_out", tm=B_WIDTH, tn=D_MODEL)
    red_bo = _Reduction(["b_w_out"], [d_b_w_out.reshape(N_CHIPS, B_WIDTH // N_CHIPS, D_MODEL)], core, place)
    dyb, got = _mm_nt(dh2b, b_w_out, name="mm_dyb", out_dtype=BF16, side=red_bo.exchange_side())
    red_bo.took_exchange(got)
    dzb, dk_rot, dv, d_bq, d_sinks = _attn_bwd(zb, dyb, kr, vv, cos, sin, p["b_bq"], p["b_sinks"])
    dkv, d_b_kv = _kv_rope_bwd(dk_rot, dv, cos, sin)
    d_b_w_in, got = _mm_tn(n_b, dzb, name="mm_d_b_w_in", tm=D_MODEL, tn=512, shards=N_CHIPS,
                           side=red_bo.scatter_side())
    red_bo.took_scatter(got)
    d_w_kv, got = _mm_tn(n_kv, dkv, name="mm_d_w_kv", tm=D_MODEL, tn=2 * KV_WIDTH, side=red_bo.share_side())
    red_bo.took_share(got)
    red_bi = _Reduction(["b_w_in", "w_kv"], [d_b_w_in, d_w_kv.reshape(N_CHIPS, D_MODEL // N_CHIPS, 2 * KV_WIDTH)],
                        core, place)
    (dh1, dh1b, d_kv_g, d_b_g), got = _mm_nt_rms_bwd(
        [(dkv, w_kv, kv_norm_g), (dzb, b_w_in, p["b_norm_g"])], h1, dh2, name="mm_dn_b", tm=512,
        side=red_bi.exchange_side())
    red_bi.took_exchange(got)

    d_a_w_out, got = _mm_tn(y, dh1b, name="mm_d_a_w_out", tm=1024, tn=D_MODEL, side=red_bi.scatter_side())
    red_bi.took_scatter(got)
    red_ao = _Reduction(["a_w_out"], [d_a_w_out.reshape(N_CHIPS, A_WIDTH // N_CHIPS, D_MODEL)], core, place)
    sides = [red_ao.exchange_side(), red_bi.share_side()]
    dy, got = _mm_nt(dh1b, a_w_out, name="mm_dy", tn=1024, out_dtype=BF16, side=_join(sides))
    got = _split(got, sides)
    red_ao.took_exchange(got[0])
    red_bi.took_share(got[1])
    (dz, d_ln_g, d_ln_b, d_ws, d_bs_t), got = _gate_bwd(z, dy, a_ln_g, a_ln_b, ws, ws_t, bs_t,
                                                        side=red_ao.scatter_side())
    red_ao.took_scatter(got)
    d_a_w_in, got = _mm_tn(n_a, dz, name="mm_d_a_w_in", tm=D_MODEL, tn=1536, shards=N_CHIPS,
                           side=red_ao.share_side())
    red_ao.took_share(got)

    small = {
        "a_ws": d_ws, "a_bs": d_bs_t.T, "a_ln_g": d_ln_g, "a_ln_b": d_ln_b,
        "kv_norm_g": d_kv_g, "b_kv": d_b_kv, "b_norm_g": d_b_g, "b_bq": d_bq,
        "b_sinks": d_sinks[0:1, :N_Q_HEADS], "final_norm_g": d_final_g,
    }
    packed = [n for n in SMALL_ORDER if n != "a_norm_g"]
    small_shapes = [small[n].shape for n in packed] + [(1, 1)]
    small_pack = _pack_rows([small[n] for n in packed] + [loss_blk[0:1, 0:1]], 64)
    seg = small_pack.shape[0] // 8
    red_ai = _Reduction(["a_w_in"], [d_a_w_in], core, place, small=small_pack.reshape(8, seg, 128))
    red_ai.took_exchange(_comm_call(red_ai.exchange_side(), "exchange_last"))
    (dx, _, d_a_g), got = _mm_nt_rms_bwd([(dz, a_w_in, a_norm_g)], x, dh1, name="mm_dn_a", tm=256,
                                         side=red_ai.scatter_side())
    red_ai.took_scatter(got)
    d_a_g = _rows128(d_a_g)
    sides = [red_ai.share_side(), _spread_side(d_a_g)]
    got = _split(_comm_call(_join(sides), "share_last"), sides)
    red_ai.took_share(got[0])
    small_full = dict(zip(packed + ["loss"], _unpack_rows(red_ai.small_full.reshape(8 * seg, 128), small_shapes)))
    small_full["a_norm_g"] = _sum_in_device_order(d_a_g, got[1][0], place).reshape(1, -1)
    loss = small_full["loss"].reshape(())

    grad_big = {**red_bo.grads, **red_bi.grads, **red_ao.grads, **red_ai.grads}
    grads = {}
    for n in SMALL_ORDER:
        gfull = small_full[n]
        if n in SHARDED_SMALL:
            width = p[n].shape[-1]
            gfull = lax.dynamic_slice_in_dim(gfull, chip * width, width, axis=-1)
        grads[n] = gfull.reshape(p[n].shape)
    for n in BIG:
        grads[n] = grad_big[n].reshape(p[n].shape)

    delta, new_m, new_v = {}, {}, {}
    for n in BIG:
        d, nm, nv = _adamw(shard2d[n], grad_big[n], m[n].reshape(shard2d[n].shape), v[n].reshape(shard2d[n].shape),
                           name="adamw_" + n)
        delta[n], new_m[n], new_v[n] = d.reshape(p[n].shape), nm.reshape(p[n].shape), nv.reshape(p[n].shape)
    shapes = [p[n].shape for n in SMALL_ORDER]
    packs = [_pack_rows([src[n] for n in SMALL_ORDER], 8) for src in (p, grads, m, v)]
    outs = _adamw(*packs, name="adamw_small")
    for res, packed in zip((delta, new_m, new_v), outs):
        for n, val in zip(SMALL_ORDER, _unpack_rows(packed, shapes)):
            res[n] = val

    return (loss, dx[None], *[grads[n] for n in WEIGHTS], *[delta[n] for n in WEIGHTS],
            *[new_m[n] for n in WEIGHTS], *[new_v[n] for n in WEIGHTS])


def kernel(x, a_norm_g, a_w_in, a_ln_g, a_ln_b, a_ws, a_bs, a_w_out, kv_norm_g, w_kv, b_kv, b_norm_g, b_w_in, b_bq, b_sinks, b_w_out, final_norm_g, loss_target, m_a_norm_g, m_a_w_in, m_a_ln_g, m_a_ln_b, m_a_ws, m_a_bs, m_a_w_out, m_kv_norm_g, m_w_kv, m_b_kv, m_b_norm_g, m_b_w_in, m_b_bq, m_b_sinks, m_b_w_out, m_final_norm_g, v_a_norm_g, v_a_w_in, v_a_ln_g, v_a_ln_b, v_a_ws, v_a_bs, v_a_w_out, v_kv_norm_g, v_w_kv, v_b_kv, v_b_norm_g, v_b_w_in, v_b_bq, v_b_sinks, v_b_w_out, v_final_norm_g):
    p = dict(a_norm_g=a_norm_g, a_w_in=a_w_in, a_ln_g=a_ln_g, a_ln_b=a_ln_b, a_ws=a_ws, a_bs=a_bs, a_w_out=a_w_out,
             kv_norm_g=kv_norm_g, w_kv=w_kv, b_kv=b_kv, b_norm_g=b_norm_g, b_w_in=b_w_in, b_bq=b_bq, b_sinks=b_sinks,
             b_w_out=b_w_out, final_norm_g=final_norm_g)
    m = dict(a_norm_g=m_a_norm_g, a_w_in=m_a_w_in, a_ln_g=m_a_ln_g, a_ln_b=m_a_ln_b, a_ws=m_a_ws, a_bs=m_a_bs,
             a_w_out=m_a_w_out, kv_norm_g=m_kv_norm_g, w_kv=m_w_kv, b_kv=m_b_kv, b_norm_g=m_b_norm_g, b_w_in=m_b_w_in,
             b_bq=m_b_bq, b_sinks=m_b_sinks, b_w_out=m_b_w_out, final_norm_g=m_final_norm_g)
    v = dict(a_norm_g=v_a_norm_g, a_w_in=v_a_w_in, a_ln_g=v_a_ln_g, a_ln_b=v_a_ln_b, a_ws=v_a_ws, a_bs=v_a_bs,
             a_w_out=v_a_w_out, kv_norm_g=v_kv_norm_g, w_kv=v_w_kv, b_kv=v_b_kv, b_norm_g=v_b_norm_g, b_w_in=v_b_w_in,
             b_bq=v_b_bq, b_sinks=v_b_sinks, b_w_out=v_b_w_out, final_norm_g=v_final_norm_g)
    return _step(x, loss_target, p, m, v)
```

```python
import functools
import math

import jax
import jax.numpy as jnp
from jax import lax
from jax.experimental import pallas as pl
from jax.experimental.pallas import tpu as pltpu

F32 = jnp.float32
BF16 = jnp.bfloat16

D_MODEL = 1024
CHUNK = 128
A_WIDTH = 2048
A_GROUPS = 16
HEAD_DIM = 64
N_Q_HEADS = 16
N_KV_HEADS = 2
Q_PER_KV = 8
B_WIDTH = 1024
KV_WIDTH = 128
ROPE_THETA = 10000.0
EPS = 1e-5
N_CHIPS = 4

ADAM_LR = 0.001
ADAM_B1 = 0.9
ADAM_B2 = 0.999
ADAM_EPS = 1e-08
ADAM_WD = 0.01
ADAM_STEP = 10

VMEM_LIMIT = 48 * 1024 * 1024
MESH = pl.DeviceIdType.MESH
NEG_BIG = -1e30
HBM = pl.BlockSpec(memory_space=pl.ANY)

NN = (((1,), (0,)), ((), ()))
NT = (((1,), (1,)), ((), ()))
TN = (((0,), (0,)), ((), ()))


def _cparams(**kw):
    return pltpu.CompilerParams(vmem_limit_bytes=VMEM_LIMIT, **kw)


class _Side:
    def __init__(self, ins, out_shapes, sems, start, finish, aliases=None):
        self.ins, self.out_shapes, self.sems = list(ins), list(out_shapes), list(sems)
        self.start, self.finish = start, finish
        self.aliases = dict(aliases or {})


def _join(sides):
    sides = [s for s in sides if s is not None]
    if not sides:
        return None
    offs, i, o, m = [], 0, 0, 0
    for s in sides:
        offs.append((i, o, m))
        i, o, m = i + len(s.ins), o + len(s.out_shapes), m + len(s.sems)

    def run(which):
        def go(ins, outs, sems):
            for s, (a, b, c) in zip(sides, offs):
                getattr(s, which)(ins[a:a + len(s.ins)], outs[b:b + len(s.out_shapes)], sems[c:c + len(s.sems)])
        return go

    aliases = {}
    for s, (a, b, _) in zip(sides, offs):
        aliases.update({a + k: b + v for k, v in s.aliases.items()})
    return _Side([x for s in sides for x in s.ins], [x for s in sides for x in s.out_shapes],
                 [x for s in sides for x in s.sems], run("start"), run("finish"), aliases)


def _split(side_outs, sides):
    out, pos = [], 0
    for s in sides:
        out.append(list(side_outs[pos:pos + len(s.out_shapes)]))
        pos += len(s.out_shapes)
    return out


def _call(body, *, grid, in_specs, out_specs, out_shape, args, name, scratch=(), side=None):
    in_specs, out_specs, out_shape, scratch = list(in_specs), list(out_specs), list(out_shape), list(scratch)
    if side is None:
        res = pl.pallas_call(body, grid=grid, in_specs=in_specs, out_specs=out_specs, out_shape=out_shape,
                             scratch_shapes=scratch, name=name, compiler_params=_cparams())(*args)
        return list(res), []
    n_in, n_out, n_sc = len(in_specs), len(out_specs), len(scratch)
    s_in, s_out = len(side.ins), len(side.out_shapes)

    def wrapped(*refs):
        ins, refs = refs[:n_in], refs[n_in:]
        side_ins, refs = refs[:s_in], refs[s_in:]
        outs, refs = refs[:n_out], refs[n_out:]
        side_outs, refs = refs[:s_out], refs[s_out:]
        scr, side_sems = refs[:n_sc], refs[n_sc:]
        ids = [pl.program_id(a) for a in range(len(grid))]
        first = functools.reduce(jnp.logical_and, [i == 0 for i in ids])
        last = functools.reduce(jnp.logical_and, [i == g - 1 for i, g in zip(ids, grid)])

        @pl.when(first)
        def _():
            side.start(side_ins, side_outs, side_sems)

        body(*ins, *outs, *scr)

        @pl.when(last)
        def _():
            side.finish(side_ins, side_outs, side_sems)

    res = pl.pallas_call(
        wrapped, grid=grid, in_specs=in_specs + [HBM] * s_in, out_specs=out_specs + [HBM] * s_out,
        out_shape=out_shape + side.out_shapes, scratch_shapes=scratch + side.sems,
        input_output_aliases={n_in + k: n_out + v for k, v in side.aliases.items()},
        name=name, compiler_params=_cparams(),
    )(*args, *side.ins)
    return list(res[:n_out]), list(res[n_out:])


def _comm_call(side, name):
    s_in, s_out = len(side.ins), len(side.out_shapes)

    def body(*refs):
        ins, outs, sems = refs[:s_in], refs[s_in:s_in + s_out], refs[s_in + s_out:]
        side.start(ins, outs, sems)
        side.finish(ins, outs, sems)

    return list(pl.pallas_call(
        body, in_specs=[HBM] * s_in, out_specs=[HBM] * s_out, out_shape=side.out_shapes, scratch_shapes=side.sems,
        input_output_aliases=side.aliases, name=name,
    )(*side.ins))


def _matmul(a, b, *, dims, grid, a_spec, b_spec, o_spec, out_shape, name, acc_axis=None,
            residual=None, r_spec=None, side=None):
    has_res = residual is not None

    def body(*refs):
        if has_res:
            a_ref, b_ref, r_ref, o_ref = refs
        else:
            a_ref, b_ref, o_ref = refs
        part = lax.dot_general(a_ref[...], b_ref[...], dims, preferred_element_type=F32)
        if acc_axis is None:
            if has_res:
                part = part + r_ref[...]
            o_ref[...] = part.astype(o_ref.dtype)
        else:
            k = pl.program_id(acc_axis)

            @pl.when(k == 0)
            def _():
                o_ref[...] = part

            @pl.when(k > 0)
            def _():
                o_ref[...] += part

    in_specs = [a_spec, b_spec] + ([r_spec] if has_res else [])
    args = (a, b) + ((residual,) if has_res else ())
    (out,), side_outs = _call(body, grid=grid, in_specs=in_specs, out_specs=[o_spec], out_shape=[out_shape],
                              args=args, name=name, side=side)
    return (out, side_outs) if side is not None else out


def _row_tile(s, want):
    return min(s, want)


def _mm_nn(a, b, *, name, tn, out_dtype=F32, residual=None, tm=512, side=None):
    s, k = a.shape
    tm = _row_tile(s, tm)
    if b.ndim == 3:
        nsh, _, nc = b.shape
        npb = nc // tn
        n = nsh * nc
        b_spec = pl.BlockSpec((None, k, tn), lambda i, j: (j // npb, 0, j % npb))
    else:
        n = b.shape[1]
        b_spec = pl.BlockSpec((k, tn), lambda i, j: (0, j))
    return _matmul(
        a, b, dims=NN, grid=(s // tm, n // tn),
        a_spec=pl.BlockSpec((tm, k), lambda i, j: (i, 0)), b_spec=b_spec,
        o_spec=pl.BlockSpec((tm, tn), lambda i, j: (i, j)),
        out_shape=jax.ShapeDtypeStruct((s, n), out_dtype), name=name, side=side,
        residual=residual, r_spec=pl.BlockSpec((tm, tn), lambda i, j: (i, j)) if residual is not None else None)


def _mm_nt(a, b, *, name, tn=None, tm=512, out_dtype=F32, side=None):
    s, k = a.shape
    tm = _row_tile(s, tm)
    if b.ndim == 3:
        nsh, n, kc = b.shape

        def body(a_ref, b_ref, o_ref):
            acc = None
            for sh in range(nsh):
                part = lax.dot_general(a_ref[:, sh * kc:(sh + 1) * kc], b_ref[sh], NT, preferred_element_type=F32)
                acc = part if acc is None else acc + part
            o_ref[...] = acc

        (out,), side_outs = _call(
            body, grid=(s // tm,),
            in_specs=[pl.BlockSpec((tm, k), lambda i: (i, 0)), pl.BlockSpec((nsh, n, kc), lambda i: (0, 0, 0))],
            out_specs=[pl.BlockSpec((tm, n), lambda i: (i, 0))], out_shape=[jax.ShapeDtypeStruct((s, n), F32)],
            args=(a, b), name=name, side=side)
        return (out, side_outs) if side is not None else out
    n = b.shape[0]
    tn = n if tn is None else tn
    return _matmul(
        a, b, dims=NT, grid=(s // tm, n // tn),
        a_spec=pl.BlockSpec((tm, k), lambda i, j: (i, 0)),
        b_spec=pl.BlockSpec((tn, k), lambda i, j: (j, 0)),
        o_spec=pl.BlockSpec((tm, tn), lambda i, j: (i, j)),
        out_shape=jax.ShapeDtypeStruct((s, n), out_dtype), name=name, side=side)


def _mm_tn(a, b, *, name, tm, tn, tk=2048, shards=None, side=None):
    s, m = a.shape
    n = b.shape[1]
    tk = _row_tile(s, tk)
    if shards is None:
        o_spec = pl.BlockSpec((tm, tn), lambda i, j, kk: (i, j))
        out_shape = jax.ShapeDtypeStruct((m, n), F32)
    else:
        assert tm == m
        nc = n // shards
        npb = nc // tn
        o_spec = pl.BlockSpec((None, m, tn), lambda i, j, kk: (j // npb, 0, j % npb))
        out_shape = jax.ShapeDtypeStruct((shards, m, nc), F32)
    return _matmul(
        a, b, dims=TN, grid=(m // tm, n // tn, s // tk), acc_axis=2,
        a_spec=pl.BlockSpec((tk, tm), lambda i, j, kk: (kk, i)),
        b_spec=pl.BlockSpec((tk, tn), lambda i, j, kk: (kk, j)),
        o_spec=o_spec, out_shape=out_shape, name=name, side=side)


def _rstd(x):
    return lax.rsqrt(jnp.mean(x * x, axis=-1, keepdims=True) + EPS)


def _rms_fwd(x, gains, *, name, tr=256):
    s, d = x.shape
    tr = _row_tile(s, tr)
    ng = len(gains)

    def body(*refs):
        xv = refs[0][...]
        xh = xv * _rstd(xv)
        for t in range(ng):
            refs[1 + ng + t][...] = (xh * refs[1 + t][...]).astype(BF16)

    row = pl.BlockSpec((tr, d), lambda i: (i, 0))
    vec = pl.BlockSpec((1, d), lambda i: (0, 0))
    outs, _ = _call(body, grid=(s // tr,), in_specs=[row] + [vec] * ng, out_specs=[row] * ng,
                    out_shape=[jax.ShapeDtypeStruct((s, d), BF16)] * ng, args=(x, *gains), name=name)
    return outs


def _rms_bwd(x, dns, gains, dres, *, name, tr=256):
    s, d = x.shape
    tr = _row_tile(s, tr)
    ng = len(gains)

    def body(*refs):
        x_ref = refs[0]
        dn_refs = refs[1:1 + ng]
        g_refs = refs[1 + ng:1 + 2 * ng]
        dres_ref = refs[1 + 2 * ng]
        dx_ref, dxb_ref = refs[2 + 2 * ng], refs[3 + 2 * ng]
        dg_refs = refs[4 + 2 * ng:]
        i = pl.program_id(0)
        xv = x_ref[...]
        r = _rstd(xv)
        xh = xv * r
        acc = jnp.zeros_like(xv)
        for t in range(ng):
            dn = dn_refs[t][...]
            acc = acc + dn * g_refs[t][...]
            dgt = jnp.sum(dn * xh, axis=0, keepdims=True)

            @pl.when(i == 0)
            def _(t=t, dgt=dgt):
                dg_refs[t][...] = dgt

            @pl.when(i > 0)
            def _(t=t, dgt=dgt):
                dg_refs[t][...] += dgt

        dx = dres_ref[...] + r * (acc - xh * jnp.mean(acc * xh, axis=-1, keepdims=True))
        dx_ref[...] = dx
        dxb_ref[...] = dx.astype(BF16)

    row = pl.BlockSpec((tr, d), lambda i: (i, 0))
    vec = pl.BlockSpec((1, d), lambda i: (0, 0))
    outs, _ = _call(
        body, grid=(s // tr,), in_specs=[row] + [row] * ng + [vec] * ng + [row],
        out_specs=[row, row] + [vec] * ng,
        out_shape=[jax.ShapeDtypeStruct((s, d), F32), jax.ShapeDtypeStruct((s, d), BF16)]
        + [jax.ShapeDtypeStruct((1, d), F32)] * ng,
        args=(x, *dns, *gains, dres), name=name)
    return outs[0], outs[1], outs[2:]


def _loss_head(h, tgt, gain, *, tr=256):
    s, d = h.shape
    tr = _row_tile(s, tr)

    def body(h_ref, t_ref, g_ref, loss_ref, dh_ref, dhb_ref, dg_ref):
        i = pl.program_id(0)
        hv = h_ref[...]
        g = g_ref[...]
        r = _rstd(hv)
        xh = hv * r
        diff = xh * g - t_ref[...]
        part = 0.5 / d * jnp.sum(jnp.sum(diff * diff, axis=-1, keepdims=True), axis=0, keepdims=True)
        dout = diff * (1.0 / d)
        a = dout * g
        dh = r * (a - xh * jnp.mean(a * xh, axis=-1, keepdims=True))
        dh_ref[...] = dh
        dhb_ref[...] = dh.astype(BF16)
        dgt = jnp.sum(dout * xh, axis=0, keepdims=True)
        lpart = jnp.broadcast_to(part, (8, 128))

        @pl.when(i == 0)
        def _():
            dg_ref[...] = dgt
            loss_ref[...] = lpart

        @pl.when(i > 0)
        def _():
            dg_ref[...] += dgt
            loss_ref[...] += lpart

    row = pl.BlockSpec((tr, d), lambda i: (i, 0))
    vec = pl.BlockSpec((1, d), lambda i: (0, 0))
    outs, _ = _call(
        body, grid=(s // tr,), in_specs=[row, row, vec],
        out_specs=[pl.BlockSpec((8, 128), lambda i: (0, 0)), row, row, vec],
        out_shape=[jax.ShapeDtypeStruct((8, 128), F32), jax.ShapeDtypeStruct((s, d), F32),
                   jax.ShapeDtypeStruct((s, d), BF16), jax.ShapeDtypeStruct((1, d), F32)],
        args=(h, tgt, gain), name="loss_head")
    return outs


def _accumulate(i, ref, value):
    @pl.when(i == 0)
    def _():
        ref[...] = value

    @pl.when(i > 0)
    def _():
        ref[...] += value


def _mm_residual_norms(y, w, res, gains, *, name, tm=512, side=None):
    s, k = y.shape
    d = w.shape[1]
    tm = _row_tile(s, tm)
    ng = len(gains)

    def body(y_ref, w_ref, r_ref, *rest):
        g_refs, h_ref, n_refs = rest[:ng], rest[ng], rest[ng + 1:]
        h = r_ref[...] + jnp.dot(y_ref[...], w_ref[...], preferred_element_type=F32)
        h_ref[...] = h
        xh = h * _rstd(h)
        for t in range(ng):
            n_refs[t][...] = (xh * g_refs[t][...]).astype(BF16)

    row = pl.BlockSpec((tm, d), lambda i: (i, 0))
    vec = pl.BlockSpec((1, d), lambda i: (0, 0))
    return _call(
        body, grid=(s // tm,),
        in_specs=[pl.BlockSpec((tm, k), lambda i: (i, 0)), pl.BlockSpec((k, d), lambda i: (0, 0)), row] + [vec] * ng,
        out_specs=[row] * (1 + ng),
        out_shape=[jax.ShapeDtypeStruct((s, d), F32)] + [jax.ShapeDtypeStruct((s, d), BF16)] * ng,
        args=(y, w, res, *gains), name=name, side=side)


def _mm_residual_loss(y, w, res, tgt, gain, *, name, tm=512):
    s, k = y.shape
    d = w.shape[1]
    tm = _row_tile(s, tm)

    def body(y_ref, w_ref, r_ref, t_ref, g_ref, loss_ref, dh_ref, dhb_ref, dg_ref):
        i = pl.program_id(0)
        hv = r_ref[...] + jnp.dot(y_ref[...], w_ref[...], preferred_element_type=F32)
        g = g_ref[...]
        r = _rstd(hv)
        xh = hv * r
        diff = xh * g - t_ref[...]
        part = 0.5 / d * jnp.sum(jnp.sum(diff * diff, axis=-1, keepdims=True), axis=0, keepdims=True)
        dout = diff * (1.0 / d)
        a = dout * g
        dh = r * (a - xh * jnp.mean(a * xh, axis=-1, keepdims=True))
        dh_ref[...] = dh
        dhb_ref[...] = dh.astype(BF16)
        _accumulate(i, dg_ref, jnp.sum(dout * xh, axis=0, keepdims=True))
        _accumulate(i, loss_ref, jnp.broadcast_to(part, (8, 128)))

    row = pl.BlockSpec((tm, d), lambda i: (i, 0))
    vec = pl.BlockSpec((1, d), lambda i: (0, 0))
    outs, _ = _call(
        body, grid=(s // tm,),
        in_specs=[pl.BlockSpec((tm, k), lambda i: (i, 0)), pl.BlockSpec((k, d), lambda i: (0, 0)), row, row, vec],
        out_specs=[pl.BlockSpec((8, 128), lambda i: (0, 0)), row, row, vec],
        out_shape=[jax.ShapeDtypeStruct((8, 128), F32), jax.ShapeDtypeStruct((s, d), F32),
                   jax.ShapeDtypeStruct((s, d), BF16), jax.ShapeDtypeStruct((1, d), F32)],
        args=(y, w, res, tgt, gain), name=name)
    return outs


def _mm_nt_rms_bwd(terms, x, dres, *, name, tm, side=None):
    s, d = x.shape
    tm = _row_tile(s, tm)
    nt = len(terms)

    def body(*refs):
        a_refs, b_refs, g_refs = refs[0:3 * nt:3], refs[1:3 * nt:3], refs[2:3 * nt:3]
        x_ref, dres_ref = refs[3 * nt], refs[3 * nt + 1]
        dx_ref, dxb_ref = refs[3 * nt + 2], refs[3 * nt + 3]
        dg_refs = refs[3 * nt + 4:]
        i = pl.program_id(0)
        xv = x_ref[...]
        r = _rstd(xv)
        xh = xv * r
        acc = jnp.zeros_like(xv)
        for t in range(nt):
            b_ref = b_refs[t]
            if len(b_ref.shape) == 3:
                kc = b_ref.shape[2]
                dn = None
                for sh in range(b_ref.shape[0]):
                    part = lax.dot_general(a_refs[t][:, sh * kc:(sh + 1) * kc], b_ref[sh], NT, preferred_element_type=F32)
                    dn = part if dn is None else dn + part
            else:
                dn = lax.dot_general(a_refs[t][...], b_ref[...], NT, preferred_element_type=F32)
            acc = acc + dn * g_refs[t][...]
            _accumulate(i, dg_refs[t], jnp.sum(dn * xh, axis=0, keepdims=True))
        dx = dres_ref[...] + r * (acc - xh * jnp.mean(acc * xh, axis=-1, keepdims=True))
        dx_ref[...] = dx
        dxb_ref[...] = dx.astype(BF16)

    row = pl.BlockSpec((tm, d), lambda i: (i, 0))
    vec = pl.BlockSpec((1, d), lambda i: (0, 0))
    in_specs, args = [], []
    for a, b, g in terms:
        in_specs += [pl.BlockSpec((tm, a.shape[1]), lambda i: (i, 0)),
                     pl.BlockSpec(b.shape, (lambda i: (0, 0, 0)) if b.ndim == 3 else (lambda i: (0, 0))), vec]
        args += [a, b, g]
    return _call(
        body, grid=(s // tm,), in_specs=in_specs + [row, row], out_specs=[row, row] + [vec] * nt,
        out_shape=[jax.ShapeDtypeStruct((s, d), F32), jax.ShapeDtypeStruct((s, d), BF16)]
        + [jax.ShapeDtypeStruct((1, d), F32)] * nt,
        args=(*args, x, dres), name=name, side=side)


def _causal_mask(transposed=False):
    row = lax.broadcasted_iota(jnp.int32, (CHUNK, CHUNK), 0)
    col = lax.broadcasted_iota(jnp.int32, (CHUNK, CHUNK), 1)
    return col >= row if transposed else col <= row


def _silu_parts(g):
    sg = jax.nn.sigmoid(g)
    return g * sg, sg * (1.0 + g * (1.0 - sg))


def _gate_fwd(z, ln_g, ln_b, ws, bs_t, *, tr=256, side=None):
    s = z.shape[0]
    tr = _row_tile(s, tr)
    w = A_WIDTH

    def body(u_ref, v_ref, g_ref, lg_ref, lb_ref, ws_ref, bst_ref, y_ref):
        v = v_ref[...].astype(F32)
        mu = jnp.mean(v, axis=-1, keepdims=True)
        xc = v - mu
        rs = lax.rsqrt(jnp.mean(xc * xc, axis=-1, keepdims=True) + EPS)
        vln = (xc * rs * lg_ref[...] + lb_ref[...]).astype(BF16)
        mask = _causal_mask()
        for grp in range(A_GROUPS):
            cols = slice(grp * CHUNK, (grp + 1) * CHUNK)
            wsm = jnp.where(mask, ws_ref[grp], 0.0).astype(BF16)
            bcol = bst_ref[:, grp:grp + 1]
            for ci in range(tr // CHUNK):
                rows = slice(ci * CHUNK, (ci + 1) * CHUNK)
                sv = jnp.dot(wsm, vln[rows, cols], preferred_element_type=F32) + bcol
                gv = g_ref[rows, cols].astype(F32)
                y_ref[rows, cols] = (u_ref[rows, cols].astype(F32) * sv * (gv * jax.nn.sigmoid(gv))).astype(BF16)

    vec = pl.BlockSpec((1, w), lambda i: (0, 0))
    (y,), side_outs = _call(
        body, grid=(s // tr,),
        in_specs=[pl.BlockSpec((tr, w), lambda i: (i, 0)), pl.BlockSpec((tr, w), lambda i: (i, 1)),
                  pl.BlockSpec((tr, w), lambda i: (i, 2)), vec, vec,
                  pl.BlockSpec((A_GROUPS, CHUNK, CHUNK), lambda i: (0, 0, 0)),
                  pl.BlockSpec((CHUNK, A_GROUPS), lambda i: (0, 0))],
        out_specs=[pl.BlockSpec((tr, w), lambda i: (i, 0))],
        out_shape=[jax.ShapeDtypeStruct((s, w), BF16)], args=(z, z, z, ln_g, ln_b, ws, bs_t), name="gate_fwd",
        side=side)
    return y, side_outs


def _gate_bwd(z, dy, ln_g, ln_b, ws, ws_t, bs_t, *, tr=256, side=None):
    s = z.shape[0]
    tr = _row_tile(s, tr)
    w = A_WIDTH
    nsteps = s // tr

    def body(u_ref, v_ref, g_ref, dy_ref, lg_ref, lb_ref, ws_ref, wst_ref, bst_ref,
             dz_ref, dlg_ref, dlb_ref, dws_ref, dbst_ref, dvln_sc, dsv_sc):
        i = pl.program_id(0)

        @pl.when(i == 0)
        def _():
            dws_ref[...] = jnp.zeros_like(dws_ref)
            dsv_sc[...] = jnp.zeros_like(dsv_sc)

        v = v_ref[...].astype(F32)
        mu = jnp.mean(v, axis=-1, keepdims=True)
        xc = v - mu
        rs = lax.rsqrt(jnp.mean(xc * xc, axis=-1, keepdims=True) + EPS)
        xh = xc * rs
        lg = lg_ref[...]
        vln = (xh * lg + lb_ref[...]).astype(BF16)
        mask = _causal_mask()
        mask_t = _causal_mask(transposed=True)
        for grp in range(A_GROUPS):
            cols = slice(grp * CHUNK, (grp + 1) * CHUNK)
            wsm = jnp.where(mask, ws_ref[grp], 0.0).astype(BF16)
            wsm_t = jnp.where(mask_t, wst_ref[grp], 0.0).astype(BF16)
            bcol = bst_ref[:, grp:grp + 1]
            for ci in range(tr // CHUNK):
                rows = slice(ci * CHUNK, (ci + 1) * CHUNK)
                vb = vln[rows, cols]
                sv = jnp.dot(wsm, vb, preferred_element_type=F32) + bcol
                uv = u_ref[rows, cols].astype(F32)
                silu, dsilu = _silu_parts(g_ref[rows, cols].astype(F32))
                dyv = dy_ref[rows, cols].astype(F32)
                dyu = dyv * uv
                dz_ref[rows, cols] = (dyv * sv * silu).astype(BF16)
                dz_ref[rows, 2 * w + grp * CHUNK:2 * w + (grp + 1) * CHUNK] = (dyu * sv * dsilu).astype(BF16)
                dsv = dyu * silu
                dsvb = dsv.astype(BF16)
                dvln_sc[rows, cols] = jnp.dot(wsm_t, dsvb, preferred_element_type=F32)
                dws_ref[grp] += lax.dot_general(dsvb, vb, NT, preferred_element_type=F32)
                dsv_sc[grp] += dsv
        dvln = dvln_sc[...]
        dlg_t = jnp.sum(dvln * xh, axis=0, keepdims=True)
        dlb_t = jnp.sum(dvln, axis=0, keepdims=True)
        a = dvln * lg
        dv = rs * (a - jnp.mean(a, axis=-1, keepdims=True) - xh * jnp.mean(a * xh, axis=-1, keepdims=True))
        dz_ref[:, w:2 * w] = dv.astype(BF16)

        @pl.when(i == 0)
        def _():
            dlg_ref[...] = dlg_t
            dlb_ref[...] = dlb_t

        @pl.when(i > 0)
        def _():
            dlg_ref[...] += dlg_t
            dlb_ref[...] += dlb_t

        @pl.when(i == nsteps - 1)
        def _():
            for grp in range(A_GROUPS):
                dws_ref[grp] = jnp.where(mask, dws_ref[grp], 0.0)
                dbst_ref[:, grp:grp + 1] = jnp.sum(dsv_sc[grp], axis=-1, keepdims=True)

    vec = pl.BlockSpec((1, w), lambda i: (0, 0))
    wsspec = pl.BlockSpec((A_GROUPS, CHUNK, CHUNK), lambda i: (0, 0, 0))
    bsspec = pl.BlockSpec((CHUNK, A_GROUPS), lambda i: (0, 0))
    return _call(
        body, grid=(nsteps,),
        in_specs=[pl.BlockSpec((tr, w), lambda i: (i, 0)), pl.BlockSpec((tr, w), lambda i: (i, 1)),
                  pl.BlockSpec((tr, w), lambda i: (i, 2)), pl.BlockSpec((tr, w), lambda i: (i, 0)),
                  vec, vec, wsspec, wsspec, bsspec],
        out_specs=[pl.BlockSpec((tr, 3 * w), lambda i: (i, 0)), vec, vec, wsspec, bsspec],
        out_shape=[jax.ShapeDtypeStruct((s, 3 * w), BF16), jax.ShapeDtypeStruct((1, w), F32),
                   jax.ShapeDtypeStruct((1, w), F32), jax.ShapeDtypeStruct((A_GROUPS, CHUNK, CHUNK), F32),
                   jax.ShapeDtypeStruct((CHUNK, A_GROUPS), F32)],
        scratch=[pltpu.VMEM((tr, w), F32), pltpu.VMEM((A_GROUPS, CHUNK, CHUNK), F32)],
        args=(z, z, z, dy, ln_g, ln_b, ws, ws_t, bs_t), name="gate_bwd", side=side)


HEADS_PER_BLOCK = 128 // HEAD_DIM
BLOCKS_PER_KV = Q_PER_KV // HEADS_PER_BLOCK
SCALE = HEAD_DIM ** -0.5
LOG2_E = math.log2(math.e)


def _rope_tables(s):
    inv_freq = ROPE_THETA ** (-jnp.arange(0, HEAD_DIM, 2, dtype=F32) / HEAD_DIM)
    ang = jnp.arange(s, dtype=F32)[:, None] * inv_freq[None, :]
    cos, sin = jnp.cos(ang), jnp.sin(ang)
    cos2 = jnp.concatenate([cos, cos], axis=-1)
    sin2 = jnp.concatenate([-sin, sin], axis=-1)
    return jnp.tile(cos2, (1, 2)), jnp.tile(sin2, (1, 2))


def _swap_halves(x):
    n = x.shape[-1]
    lane = lax.broadcasted_iota(jnp.int32, x.shape, x.ndim - 1)
    first = (lane % HEAD_DIM) < (HEAD_DIM // 2)
    return jnp.where(first, pltpu.roll(x, n - HEAD_DIM // 2, x.ndim - 1), pltpu.roll(x, HEAD_DIM // 2, x.ndim - 1))


def _left_half(rows):
    return lax.broadcasted_iota(jnp.int32, (rows, 128), 1) < HEAD_DIM


def _dup_heads(x):
    left = _left_half(x.shape[0])
    swapped = pltpu.roll(x, HEAD_DIM, 1)
    return jnp.concatenate([jnp.where(left, x, swapped), jnp.where(left, swapped, x)], axis=-1)


def _fold_heads(a):
    b0, b1 = a[:, :128], a[:, 128:]
    f0 = b0 + pltpu.roll(b0, HEAD_DIM, 1)
    f1 = b1 + pltpu.roll(b1, HEAD_DIM, 1)
    return jnp.where(_left_half(a.shape[0]), f0, f1)


def _kv_rope(kv, b_kv, cos, sin, *, tr=512):
    s = kv.shape[0]
    tr = _row_tile(s, tr)

    def body(kv_ref, b_ref, c_ref, s_ref, k_ref, v_ref):
        x = kv_ref[...] + b_ref[...]
        k = x[:, :KV_WIDTH]
        k_ref[...] = _dup_heads(k * c_ref[...] + _swap_halves(k) * s_ref[...]).astype(BF16)
        v_ref[...] = _dup_heads(x[:, KV_WIDTH:]).astype(BF16)

    tab = pl.BlockSpec((tr, KV_WIDTH), lambda i: (i, 0))
    wide = pl.BlockSpec((tr, 2 * KV_WIDTH), lambda i: (i, 0))
    outs, _ = _call(body, grid=(s // tr,),
                    in_specs=[wide, pl.BlockSpec((1, 2 * KV_WIDTH), lambda i: (0, 0)), tab, tab],
                    out_specs=[wide, wide], out_shape=[jax.ShapeDtypeStruct((s, 2 * KV_WIDTH), BF16)] * 2,
                    args=(kv, b_kv, cos, sin), name="kv_rope")
    return outs


def _kv_rope_bwd(dk2, dv2, cos, sin, *, tr=512):
    s = dk2.shape[0]
    tr = _row_tile(s, tr)

    def body(dk_ref, dv_ref, c_ref, s_ref, dkv_ref, db_ref):
        i = pl.program_id(0)
        d = _fold_heads(dk_ref[...])
        dk = d * c_ref[...] + _swap_halves(d * s_ref[...])
        dvv = _fold_heads(dv_ref[...])
        dkv_ref[:, :KV_WIDTH] = dk.astype(BF16)
        dkv_ref[:, KV_WIDTH:] = dvv.astype(BF16)
        sk = jnp.sum(dk, axis=0, keepdims=True)
        sv = jnp.sum(dvv, axis=0, keepdims=True)

        @pl.when(i == 0)
        def _():
            db_ref[:, :KV_WIDTH] = sk
            db_ref[:, KV_WIDTH:] = sv

        @pl.when(i > 0)
        def _():
            db_ref[:, :KV_WIDTH] += sk
            db_ref[:, KV_WIDTH:] += sv

    tab = pl.BlockSpec((tr, KV_WIDTH), lambda i: (i, 0))
    wide = pl.BlockSpec((tr, 2 * KV_WIDTH), lambda i: (i, 0))
    outs, _ = _call(body, grid=(s // tr,), in_specs=[wide, wide, tab, tab],
                    out_specs=[wide, pl.BlockSpec((1, 2 * KV_WIDTH), lambda i: (0, 0))],
                    out_shape=[jax.ShapeDtypeStruct((s, 2 * KV_WIDTH), BF16),
                               jax.ShapeDtypeStruct((1, 2 * KV_WIDTH), F32)],
                    args=(dk2, dv2, cos, sin), name="kv_rope_bwd")
    return outs


def _from_previous():
    cols = Q_PER_KV * CHUNK
    k = lax.broadcasted_iota(jnp.int32, (CHUNK, cols), 0)
    q = lax.broadcasted_iota(jnp.int32, (CHUNK, cols), 1) & (CHUNK - 1)
    return k > q


def _fold(x2, prev):
    return jnp.where(prev, x2[:CHUNK], x2[CHUNK:])


def _unfold(x, prev):
    zero = jnp.zeros_like(x)
    return jnp.concatenate([jnp.where(prev, x, zero), jnp.where(prev, zero, x)], axis=0)


def _stack_heads(blocks, left):
    parts = []
    for b in blocks:
        parts.append(jnp.where(left, b, jnp.zeros_like(b)))
        parts.append(jnp.where(left, jnp.zeros_like(b), b))
    return jnp.concatenate(parts, axis=0)


def _unstack_heads(xt):
    top = lax.broadcasted_iota(jnp.int32, (128, CHUNK), 0) < HEAD_DIM
    return [jnp.where(top, xt[:, (2 * b) * CHUNK:(2 * b + 1) * CHUNK], xt[:, (2 * b + 1) * CHUNK:(2 * b + 2) * CHUNK]).T
            for b in range(BLOCKS_PER_KV)]


def _sink_row(sk_ref, kvh):
    return jnp.concatenate([jnp.full((1, CHUNK), sk_ref[0, kvh * Q_PER_KV + r], F32) for r in range(Q_PER_KV)], axis=1)


def _stacked_probs(qs, kd, prev, sink, i):
    sc2 = lax.dot_general(kd, qs, NT, preferred_element_type=F32)
    no_previous = jnp.where(i > 0, 0.0, NEG_BIG)
    sc = jnp.where(prev, sc2[:CHUNK] + no_previous, sc2[CHUNK:])
    sink = sink * (1.0 / SCALE)
    m = jnp.maximum(jnp.max(sc, axis=0, keepdims=True), sink)
    p = jnp.exp2((sc - m) * (SCALE * LOG2_E))
    esink = jnp.exp2((sink - m) * (SCALE * LOG2_E))
    inv = 1.0 / (jnp.sum(p, axis=0, keepdims=True) + esink)
    return p * inv, esink * inv


def _lane_block(b):
    return slice(b * 128, (b + 1) * 128)


def _rope_blocks(zq_ref, bq_ref, cos, sin, kvh):
    out = []
    for b in range(BLOCKS_PER_KV):
        cols = _lane_block(kvh * BLOCKS_PER_KV + b)
        q = zq_ref[:, cols].astype(F32) + bq_ref[:, cols]
        out.append((q * cos + _swap_halves(q) * sin).astype(BF16))
    return out


def _attn_specs():
    qspec = pl.BlockSpec((CHUNK, B_WIDTH), lambda i: (i, 0))
    gspec = pl.BlockSpec((CHUNK, B_WIDTH), lambda i: (i, 1))
    prev = pl.BlockSpec((CHUNK, 2 * KV_WIDTH), lambda i: (jnp.maximum(i - 1, 0), 0))
    cur = pl.BlockSpec((CHUNK, 2 * KV_WIDTH), lambda i: (i, 0))
    tab = pl.BlockSpec((CHUNK, KV_WIDTH), lambda i: (i, 0))
    bq = pl.BlockSpec((1, B_WIDTH), lambda i: (0, 0))
    sinks = pl.BlockSpec(memory_space=pltpu.SMEM)
    return qspec, gspec, prev, cur, tab, bq, sinks


def _attn_fwd(zb, k2, v2, cos, sin, b_bq, sinks, *, side=None):
    s = zb.shape[0]

    def body(zq_ref, zg_ref, kp_ref, kc_ref, vp_ref, vc_ref, c_ref, s_ref, bq_ref, sk_ref, y_ref):
        i = pl.program_id(0)
        cos, sin = c_ref[...], s_ref[...]
        kcat = jnp.concatenate([kp_ref[...], kc_ref[...]], axis=0)
        vcat = jnp.concatenate([vp_ref[...], vc_ref[...]], axis=0)
        prev = _from_previous()
        left = _left_half(CHUNK)
        for kvh in range(N_KV_HEADS):
            qs = _stack_heads(_rope_blocks(zq_ref, bq_ref, cos, sin, kvh), left)
            p, _ = _stacked_probs(qs, kcat[:, _lane_block(kvh)], prev, _sink_row(sk_ref, kvh), i)
            ot = lax.dot_general(vcat[:, _lane_block(kvh)], _unfold(p, prev).astype(BF16), TN,
                                 preferred_element_type=F32)
            for b, ob in enumerate(_unstack_heads(ot)):
                cols = _lane_block(kvh * BLOCKS_PER_KV + b)
                gv = zg_ref[:, cols].astype(F32)
                y_ref[:, cols] = (ob * (gv * jax.nn.sigmoid(gv))).astype(BF16)

    qspec, gspec, prev, cur, tab, bq, sk = _attn_specs()
    (y,), side_outs = _call(body, grid=(s // CHUNK,), in_specs=[qspec, gspec, prev, cur, prev, cur, tab, tab, bq, sk],
                            out_specs=[qspec], out_shape=[jax.ShapeDtypeStruct((s, B_WIDTH), BF16)],
                            args=(zb, zb, k2, k2, v2, v2, cos, sin, b_bq, sinks), name="attn_fwd", side=side)
    return y, side_outs


def _attn_bwd(zb, dyb, k2, v2, cos, sin, b_bq, sinks):
    s = zb.shape[0]

    def body(zq_ref, zg_ref, dy_ref, kp_ref, kc_ref, vp_ref, vc_ref, c_ref, s_ref, bq_ref, sk_ref,
             dz_ref, dk_ref, dv_ref, dbq_ref, dsk_ref):
        i = pl.program_id(0)

        @pl.when(i == 0)
        def _():
            dk_ref[...] = jnp.zeros_like(dk_ref)
            dv_ref[...] = jnp.zeros_like(dv_ref)
            dbq_ref[...] = jnp.zeros_like(dbq_ref)
            dsk_ref[...] = jnp.zeros_like(dsk_ref)

        cos, sin = c_ref[...], s_ref[...]
        kcat = jnp.concatenate([kp_ref[...], kc_ref[...]], axis=0)
        vcat = jnp.concatenate([vp_ref[...], vc_ref[...]], axis=0)
        prev = _from_previous()
        left = _left_half(CHUNK)
        lane = lax.broadcasted_iota(jnp.int32, (1, 128), 1)
        dsk_row = jnp.zeros((1, 128), F32)
        cur_rows = pl.ds(pl.multiple_of(i * CHUNK, CHUNK), CHUNK)
        for kvh in range(N_KV_HEADS):
            kd, vd = kcat[:, _lane_block(kvh)], vcat[:, _lane_block(kvh)]
            qs = _stack_heads(_rope_blocks(zq_ref, bq_ref, cos, sin, kvh), left)
            p, psink = _stacked_probs(qs, kd, prev, _sink_row(sk_ref, kvh), i)
            pb = _unfold(p, prev).astype(BF16)
            ot = lax.dot_general(vd, pb, TN, preferred_element_type=F32)
            gates, dys = [], []
            for b in range(BLOCKS_PER_KV):
                cols = _lane_block(kvh * BLOCKS_PER_KV + b)
                gates.append(_silu_parts(zg_ref[:, cols].astype(F32)))
                dys.append(dy_ref[:, cols].astype(F32))
            dos = _stack_heads([(dyv * silu).astype(BF16) for dyv, (silu, _) in zip(dys, gates)], left)
            dp = _fold(lax.dot_general(vd, dos, NT, preferred_element_type=F32), prev)
            delta = jnp.sum(p * dp, axis=0, keepdims=True)
            ds = _unfold(p * (dp - delta) * SCALE, prev).astype(BF16)
            dqt = lax.dot_general(kd, ds, TN, preferred_element_type=F32)
            dk_part = jnp.dot(ds, qs, preferred_element_type=F32)
            dv_part = jnp.dot(pb, dos, preferred_element_type=F32)
            dk_ref[cur_rows, _lane_block(kvh)] += dk_part[CHUNK:]
            dv_ref[cur_rows, _lane_block(kvh)] += dv_part[CHUNK:]

            @pl.when(i > 0)
            def _(kvh=kvh, dk_part=dk_part, dv_part=dv_part):
                prev_rows = pl.ds(pl.multiple_of((i - 1) * CHUNK, CHUNK), CHUNK)
                dk_ref[prev_rows, _lane_block(kvh)] += dk_part[:CHUNK]
                dv_ref[prev_rows, _lane_block(kvh)] += dv_part[:CHUNK]

            sink_grad = psink * delta
            for r in range(Q_PER_KV):
                dsink = -jnp.sum(sink_grad[:, r * CHUNK:(r + 1) * CHUNK], axis=1, keepdims=True)
                dsk_row = dsk_row + jnp.where(lane == kvh * Q_PER_KV + r, dsink, 0.0)
            blocks = zip(_unstack_heads(ot), _unstack_heads(dqt), dys, gates)
            for b, (ob, dqr, dyv, (_, dsilu)) in enumerate(blocks):
                blk = kvh * BLOCKS_PER_KV + b
                dq = dqr * cos + _swap_halves(dqr * sin)
                dbq_ref[:, _lane_block(blk)] += jnp.sum(dq, axis=0, keepdims=True)
                dz_ref[:, _lane_block(blk)] = dq.astype(BF16)
                dz_ref[:, _lane_block(B_WIDTH // 128 + blk)] = (dyv * ob * dsilu).astype(BF16)
        dsk_ref[0:1, :] += dsk_row

    qspec, gspec, prev, cur, tab, bq, sk = _attn_specs()
    full = pl.BlockSpec((s, 2 * KV_WIDTH), lambda i: (0, 0))
    outs, _ = _call(
        body, grid=(s // CHUNK,),
        in_specs=[qspec, gspec, qspec, prev, cur, prev, cur, tab, tab, bq, sk],
        out_specs=[pl.BlockSpec((CHUNK, 2 * B_WIDTH), lambda i: (i, 0)), full, full, bq,
                   pl.BlockSpec((8, 128), lambda i: (0, 0))],
        out_shape=[jax.ShapeDtypeStruct((s, 2 * B_WIDTH), BF16), jax.ShapeDtypeStruct((s, 2 * KV_WIDTH), F32),
                   jax.ShapeDtypeStruct((s, 2 * KV_WIDTH), F32), jax.ShapeDtypeStruct((1, B_WIDTH), F32),
                   jax.ShapeDtypeStruct((8, 128), F32)],
        args=(zb, zb, dyb, k2, k2, v2, v2, cos, sin, b_bq, sinks), name="attn_bwd")
    return outs


def _place():
    x, y, c = lax.axis_index("x"), lax.axis_index("y"), lax.axis_index("c")
    return x, y, c, [(1 - x, y), (x, 1 - y), (1 - x, 1 - y)]


def _relations():
    return [(r >> 2 & 1, r >> 1 & 1, r & 1) for r in range(1, 8)]


def _gather_side(arrs):
    n = len(arrs)

    def copies(ins, outs, sems):
        send_ici, recv_ici, send_d2d, recv_d2d, local_sem = sems
        x, y, c, chips = _place()
        me = 2 * x + y

        def rows(a, half):
            hr = arrs[a].shape[0] // 2
            return pl.ds(half * hr, hr)

        def ici(a, j, src_chip, to):
            return pltpu.make_async_remote_copy(
                src_ref=ins[a].at[rows(a, c)], dst_ref=outs[a].at[src_chip, rows(a, c)],
                send_sem=send_ici.at[a, j], recv_sem=recv_ici.at[a, j], device_id=to, device_id_type=MESH)

        def d2d(a, j, chip, half):
            blk = outs[a].at[chip, rows(a, half)]
            return pltpu.make_async_remote_copy(
                src_ref=blk, dst_ref=blk, send_sem=send_d2d.at[a, j], recv_sem=recv_d2d.at[a, j],
                device_id=(x, y, 1 - c), device_id_type=MESH)

        local = [pltpu.make_async_copy(ins[a], outs[a].at[me], local_sem.at[a]) for a in range(n)]
        pairs = [(a, j, chip) for a in range(n) for j, chip in enumerate(chips)]
        return c, me, local, ici, d2d, pairs

    def start(ins, outs, sems):
        c, me, local, ici, _, pairs = copies(ins, outs, sems)
        for cp in local:
            cp.start()
        for a, j, chip in pairs:
            ici(a, j, me, (*chip, c)).start()

    def finish(ins, outs, sems):
        c, me, local, ici, d2d, pairs = copies(ins, outs, sems)
        for a, j, (px, py) in pairs:
            ici(a, j, 2 * px + py, (px, py, c)).wait_recv()
            d2d(a, j, 2 * px + py, c).start()
        for a, j, (px, py) in pairs:
            d2d(a, j, 2 * px + py, 1 - c).wait_recv()
        for a, j, (px, py) in pairs:
            ici(a, j, me, (px, py, c)).wait_send()
            d2d(a, j, 2 * px + py, c).wait_send()
        for cp in local:
            cp.wait()

    return _Side(arrs, [jax.ShapeDtypeStruct((N_CHIPS,) + a.shape, a.dtype) for a in arrs],
                 [pltpu.SemaphoreType.DMA((n, 3))] * 4 + [pltpu.SemaphoreType.DMA((n,))], start, finish)


def _exchange_side(grads):
    n = len(grads)

    def copies(ins, outs, sems):
        send_sem, recv_sem = sems
        x, y, c, _ = _place()
        cps = []
        for a in range(n):
            hr = grads[a].shape[1] // 2
            cps.append(pltpu.make_async_remote_copy(
                src_ref=ins[a].at[:, pl.ds((1 - c) * hr, hr), :], dst_ref=outs[a],
                send_sem=send_sem.at[a], recv_sem=recv_sem.at[a], device_id=(x, y, 1 - c), device_id_type=MESH))
        return cps

    def start(ins, outs, sems):
        for cp in copies(ins, outs, sems):
            cp.start()

    def finish(ins, outs, sems):
        for cp in copies(ins, outs, sems):
            cp.wait()

    return _Side(grads, [jax.ShapeDtypeStruct((g.shape[0], g.shape[1] // 2, g.shape[2]), g.dtype) for g in grads],
                 [pltpu.SemaphoreType.DMA((n,))] * 2, start, finish)


def _scatter_side(chip_sums, small=None):
    n = len(chip_sums)
    arrs = list(chip_sums) + ([small] if small is not None else [])

    def copies(ins, outs, sems):
        x, y, c, chips = _place()
        cps = []
        for a in range(n):
            for j, (px, py) in enumerate(chips):
                cps.append(pltpu.make_async_remote_copy(
                    src_ref=ins[a].at[2 * px + py], dst_ref=outs[a].at[j],
                    send_sem=sems[0].at[a, j], recv_sem=sems[1].at[a, j], device_id=(px, py, c), device_id_type=MESH))
        if small is not None:
            for r, (fx, fy, fc) in enumerate(_relations(), start=1):
                px, py, pc = x ^ fx, y ^ fy, c ^ fc
                cps.append(pltpu.make_async_remote_copy(
                    src_ref=ins[n].at[4 * px + 2 * py + pc], dst_ref=outs[n].at[r],
                    send_sem=sems[2].at[r - 1], recv_sem=sems[3].at[r - 1], device_id=(px, py, pc),
                    device_id_type=MESH))
        return cps

    def start(ins, outs, sems):
        for cp in copies(ins, outs, sems):
            cp.start()

    def finish(ins, outs, sems):
        for cp in copies(ins, outs, sems):
            cp.wait()

    shapes = [jax.ShapeDtypeStruct((3,) + t.shape[1:], t.dtype) for t in chip_sums]
    sems = [pltpu.SemaphoreType.DMA((n, 3))] * 2
    if small is not None:
        shapes.append(jax.ShapeDtypeStruct(small.shape, small.dtype))
        sems += [pltpu.SemaphoreType.DMA((7,))] * 2
    return _Side(arrs, shapes, sems, start, finish)


def _share_side(halves, small=None):
    n = len(halves)
    arrs = list(halves) + ([small] if small is not None else [])

    def copies(ins, outs, sems):
        x, y, c, _ = _place()
        me = 4 * x + 2 * y + c
        sends, recvs = [], []
        for a in range(n):
            hr = halves[a].shape[0] // 2
            sends.append(pltpu.make_async_remote_copy(
                src_ref=ins[a].at[pl.ds(c * hr, hr)], dst_ref=outs[a].at[pl.ds(c * hr, hr)],
                send_sem=sems[0].at[a], recv_sem=sems[1].at[a], device_id=(x, y, 1 - c), device_id_type=MESH))
            other = outs[a].at[pl.ds((1 - c) * hr, hr)]
            recvs.append(pltpu.make_async_remote_copy(
                src_ref=other, dst_ref=other, send_sem=sems[0].at[a], recv_sem=sems[1].at[a],
                device_id=(x, y, 1 - c), device_id_type=MESH))
        if small is not None:
            for r, (fx, fy, fc) in enumerate(_relations(), start=1):
                px, py, pc = x ^ fx, y ^ fy, c ^ fc
                sends.append(pltpu.make_async_remote_copy(
                    src_ref=ins[n].at[me], dst_ref=outs[n].at[me],
                    send_sem=sems[2].at[r - 1], recv_sem=sems[3].at[r - 1], device_id=(px, py, pc),
                    device_id_type=MESH))
                theirs = outs[n].at[4 * px + 2 * py + pc]
                recvs.append(pltpu.make_async_remote_copy(
                    src_ref=theirs, dst_ref=theirs, send_sem=sems[2].at[r - 1], recv_sem=sems[3].at[r - 1],
                    device_id=(px, py, pc), device_id_type=MESH))
        return sends, recvs

    def start(ins, outs, sems):
        for cp in copies(ins, outs, sems)[0]:
            cp.start()

    def finish(ins, outs, sems):
        sends, recvs = copies(ins, outs, sems)
        for cp in recvs:
            cp.wait_recv()
        for cp in sends:
            cp.wait_send()

    sems = [pltpu.SemaphoreType.DMA((n,))] * 2 + ([pltpu.SemaphoreType.DMA((7,))] * 2 if small is not None else [])
    return _Side(arrs, [jax.ShapeDtypeStruct(h.shape, h.dtype) for h in arrs], sems, start, finish,
                 aliases={i: i for i in range(len(arrs))})


def _mm_gathering(a, shard, order, *, name, tm=1024):
    s, k = a.shape
    nc = shard.shape[1]
    tm = _row_tile(s, tm)
    tn = nc // 2
    hr = k // 2

    def body(order_ref, a_ref, shard_ref, z_ref, full_ref, wbuf, send_ici, recv_ici, send_d2d, recv_d2d, local_sem, load_sem):
        t, jj, i = pl.program_id(0), pl.program_id(1), pl.program_id(2)
        x, y, c, chips = _place()
        me = 2 * x + y

        def rows(half):
            return pl.ds(half * hr, hr)

        def ici(j, src_chip, to):
            return pltpu.make_async_remote_copy(
                src_ref=shard_ref.at[rows(c)], dst_ref=full_ref.at[src_chip, rows(c)],
                send_sem=send_ici.at[j], recv_sem=recv_ici.at[j], device_id=to, device_id_type=MESH)

        def d2d(j, chip, half):
            blk = full_ref.at[chip, rows(half)]
            return pltpu.make_async_remote_copy(
                src_ref=blk, dst_ref=blk, send_sem=send_d2d.at[j], recv_sem=recv_d2d.at[j],
                device_id=(x, y, 1 - c), device_id_type=MESH)

        def load(src):
            for h in range(2):
                cp = pltpu.make_async_copy(src.at[:, pl.ds(h * tn, tn)], wbuf.at[h], load_sem.at[h])
                cp.start()
            for h in range(2):
                pltpu.make_async_copy(src.at[:, pl.ds(h * tn, tn)], wbuf.at[h], load_sem.at[h]).wait()

        local = pltpu.make_async_copy(shard_ref, full_ref.at[me], local_sem)
        new_shard = jnp.logical_and(jj == 0, i == 0)

        @pl.when(jnp.logical_and(new_shard, t == 0))
        def _():
            local.start()
            for j, chip in enumerate(chips):
                ici(j, me, (*chip, c)).start()
            load(shard_ref)

        for j, (px, py) in enumerate(chips):
            @pl.when(jnp.logical_and(new_shard, t == j + 1))
            def _(j=j, px=px, py=py):
                chip = 2 * px + py
                ici(j, chip, (px, py, c)).wait_recv()
                d2d(j, chip, c).start()
                d2d(j, chip, 1 - c).wait_recv()
                load(full_ref.at[chip])

        z_ref[...] = jnp.dot(a_ref[...], wbuf[jj], preferred_element_type=F32).astype(z_ref.dtype)

        last = functools.reduce(jnp.logical_and, [t == N_CHIPS - 1, jj == 1, i == s // tm - 1])

        @pl.when(last)
        def _():
            for j, (px, py) in enumerate(chips):
                ici(j, me, (px, py, c)).wait_send()
                d2d(j, 2 * px + py, c).wait_send()
            local.wait()

    return pl.pallas_call(
        body,
        grid_spec=pltpu.PrefetchScalarGridSpec(
            num_scalar_prefetch=1, grid=(N_CHIPS, 2, s // tm),
            in_specs=[pl.BlockSpec((tm, k), lambda t, jj, i, order: (i, 0)), HBM],
            out_specs=[pl.BlockSpec((tm, tn), lambda t, jj, i, order: (i, order[t] * 2 + jj)), HBM],
            scratch_shapes=[pltpu.VMEM((2, k, tn), BF16)] + [pltpu.SemaphoreType.DMA((3,))] * 4
            + [pltpu.SemaphoreType.DMA, pltpu.SemaphoreType.DMA((2,))]),
        out_shape=[jax.ShapeDtypeStruct((s, N_CHIPS * nc), BF16), jax.ShapeDtypeStruct((N_CHIPS, k, nc), BF16)],
        name=name, compiler_params=_cparams(),
    )(order, a, shard)


def _col_tile(cols):
    return cols if cols <= 2048 else 512


def _add_sibling(grad, recv, core, *, name):
    k, r, c = grad.shape
    hr = r // 2
    tr = min(hr, 256)
    tc = _col_tile(c)
    nrb = hr // tr

    def body(core_ref, g_ref, r_ref, o_ref):
        o_ref[...] = (g_ref[...] + r_ref[...]).astype(BF16)

    return pl.pallas_call(
        body,
        grid_spec=pltpu.PrefetchScalarGridSpec(
            num_scalar_prefetch=1, grid=(k, nrb, c // tc),
            in_specs=[pl.BlockSpec((None, tr, tc), lambda kk, i, j, core: (kk, core[0] * nrb + i, j)),
                      pl.BlockSpec((None, tr, tc), lambda kk, i, j, core: (kk, i, j))],
            out_specs=pl.BlockSpec((None, tr, tc), lambda kk, i, j, core: (kk, i, j))),
        out_shape=jax.ShapeDtypeStruct((k, hr, c), BF16), name=name, compiler_params=_cparams(),
    )(core, grad, recv)


def _sum_chips(grad, from_sibling, recv, place, *, name):
    _, hr, c = from_sibling.shape
    tr = min(hr, 256)
    tc = _col_tile(c)
    nrb = hr // tr

    def body(place_ref, g_ref, s_ref, r0_ref, r1_ref, r2_ref, o_ref):
        own = g_ref[...] + s_ref[...]
        o_ref[...] = ((own + r0_ref[...].astype(F32)) + r1_ref[...].astype(F32)) + r2_ref[...].astype(F32)

    def rspec(j):
        return pl.BlockSpec((None, tr, tc), lambda i, jj, place: (j, i, jj))

    return pl.pallas_call(
        body,
        grid_spec=pltpu.PrefetchScalarGridSpec(
            num_scalar_prefetch=1, grid=(nrb, c // tc),
            in_specs=[pl.BlockSpec((None, tr, tc), lambda i, jj, place: (place[0], place[1] * nrb + i, jj)),
                      pl.BlockSpec((None, tr, tc), lambda i, jj, place: (place[0], i, jj)),
                      rspec(0), rspec(1), rspec(2)],
            out_specs=pl.BlockSpec((tr, tc), lambda i, jj, place: (place[1] * nrb + i, jj))),
        out_shape=jax.ShapeDtypeStruct((2 * hr, c), F32), name=name, compiler_params=_cparams(),
    )(place, grad, from_sibling, recv, recv, recv)


def _sum_small(small, recv, place):
    _, sr, _ = small.shape

    def body(place_ref, own_ref, r_ref, o_ref):
        acc = own_ref[...]
        for r in range(1, 8):
            acc = acc + r_ref[r]
        o_ref[...] = acc

    return pl.pallas_call(
        body,
        grid_spec=pltpu.PrefetchScalarGridSpec(
            num_scalar_prefetch=1, grid=(1,),
            in_specs=[pl.BlockSpec((None, sr, 128), lambda i, place: (place[2], 0, 0)),
                      pl.BlockSpec((8, sr, 128), lambda i, place: (0, 0, 0))],
            out_specs=pl.BlockSpec((None, sr, 128), lambda i, place: (place[2], 0, 0))),
        out_shape=jax.ShapeDtypeStruct(small.shape, F32), name="sum_small", compiler_params=_cparams(),
    )(place, small, recv)


def _spread_side(vec):
    def copies(ins, outs, sems):
        x, y, c, _ = _place()
        return [pltpu.make_async_remote_copy(
            src_ref=ins[0], dst_ref=outs[0].at[r], send_sem=sems[0].at[r - 1], recv_sem=sems[1].at[r - 1],
            device_id=(x ^ fx, y ^ fy, c ^ fc), device_id_type=MESH)
            for r, (fx, fy, fc) in enumerate(_relations(), start=1)]

    def start(ins, outs, sems):
        for cp in copies(ins, outs, sems):
            cp.start()

    def finish(ins, outs, sems):
        for cp in copies(ins, outs, sems):
            cp.wait()

    return _Side([vec], [jax.ShapeDtypeStruct((8,) + vec.shape, vec.dtype)], [pltpu.SemaphoreType.DMA((7,))] * 2,
                 start, finish)


def _sum_in_device_order(own, spread, place):
    def body(place_ref, own_ref, r_ref, o_ref):
        me = place_ref[2]
        acc = jnp.zeros_like(own_ref[...])
        for d in range(8):
            slot = jnp.where(me == d, 1, me ^ d)
            acc = acc + jnp.where(me == d, own_ref[...], r_ref[slot])
        o_ref[...] = acc

    return pl.pallas_call(
        body,
        grid_spec=pltpu.PrefetchScalarGridSpec(
            num_scalar_prefetch=1, grid=(1,),
            in_specs=[pl.BlockSpec(own.shape, lambda i, place: (0, 0)),
                      pl.BlockSpec(spread.shape, lambda i, place: (0, 0, 0))],
            out_specs=pl.BlockSpec(own.shape, lambda i, place: (0, 0))),
        out_shape=jax.ShapeDtypeStruct(own.shape, F32), name="sum_in_device_order", compiler_params=_cparams(),
    )(place, own, spread)


def _adamw(w, g, m, v, *, name):
    r, c = w.shape
    tr = 256 if r % 256 == 0 else r
    tc = _col_tile(c)
    bc1 = 1.0 - ADAM_B1 ** ADAM_STEP
    bc2 = 1.0 - ADAM_B2 ** ADAM_STEP

    def body(w_ref, g_ref, m_ref, v_ref, d_ref, nm_ref, nv_ref):
        gv = g_ref[...]
        nm = ADAM_B1 * m_ref[...] + (1.0 - ADAM_B1) * gv
        nv = ADAM_B2 * v_ref[...] + (1.0 - ADAM_B2) * (gv * gv)
        d_ref[...] = -ADAM_LR * ((nm / bc1) / (jnp.sqrt(nv / bc2) + ADAM_EPS) + ADAM_WD * w_ref[...])
        nm_ref[...] = nm
        nv_ref[...] = nv

    spec = pl.BlockSpec((tr, tc), lambda i, j: (i, j))
    outs, _ = _call(body, grid=(r // tr, c // tc), in_specs=[spec] * 4, out_specs=[spec] * 3,
                    out_shape=[jax.ShapeDtypeStruct((r, c), F32)] * 3, args=(w, g, m, v), name=name)
    return outs


SMALL_ORDER = ["a_ws", "a_bs", "a_norm_g", "a_ln_g", "a_ln_b", "kv_norm_g", "b_kv", "b_norm_g", "b_bq",
               "b_sinks", "final_norm_g"]
SHARDED_SMALL = {"a_norm_g", "a_ln_g", "a_ln_b"}
PACK_TILE = 8 * 128


def _rows128(a):
    flat = a.reshape(-1)
    return jnp.pad(flat, (0, (-flat.shape[0]) % PACK_TILE)).reshape(-1, 128)


def _pack_rows(parts, multiple):
    rows = [_rows128(p) for p in parts]
    total = sum(r.shape[0] for r in rows)
    pad = (-total) % multiple
    if pad:
        rows.append(jnp.zeros((pad, 128), rows[0].dtype))
    return jnp.concatenate(rows, axis=0)


def _unpack_rows(packed, shapes):
    out, row = [], 0
    for shp in shapes:
        size = math.prod(shp)
        nrow = -(-size // PACK_TILE) * 8
        out.append(packed[row:row + nrow].reshape(-1)[:size].reshape(shp))
        row += nrow
    return out


WEIGHTS = ["a_norm_g", "a_w_in", "a_ln_g", "a_ln_b", "a_ws", "a_bs", "a_w_out", "kv_norm_g", "w_kv", "b_kv",
           "b_norm_g", "b_w_in", "b_bq", "b_sinks", "b_w_out", "final_norm_g"]
BIG = ["a_w_in", "a_w_out", "w_kv", "b_w_in", "b_w_out"]


class _Reduction:
    def __init__(self, names, partials, core, place, small=None):
        self.names, self.partials, self.core, self.place, self.small = names, partials, core, place, small

    def exchange_side(self):
        return _exchange_side(self.partials)

    def took_exchange(self, from_sibling):
        self.from_sibling = from_sibling
        self.chip_sums = [_add_sibling(g, r, self.core, name="add_sibling_" + n)
                          for g, r, n in zip(self.partials, from_sibling, self.names)]

    def scatter_side(self):
        return _scatter_side(self.chip_sums, self.small)

    def took_scatter(self, arrived):
        big = arrived[:len(self.names)]
        self.halves = [_sum_chips(g, fs, r, self.place, name="sum_chips_" + n)
                       for g, fs, r, n in zip(self.partials, self.from_sibling, big, self.names)]
        self.small_mine = _sum_small(self.small, arrived[-1], self.place) if self.small is not None else None

    def share_side(self):
        return _share_side(self.halves, self.small_mine)

    def took_share(self, shared):
        self.grads = dict(zip(self.names, shared[:len(self.names)]))
        self.small_full = shared[-1] if self.small is not None else None


def _step(x, loss_target, p, m, v):
    xi, yi, ci = lax.axis_index("x"), lax.axis_index("y"), lax.axis_index("c")
    chip = 2 * xi + yi
    device = 4 * xi + 2 * yi + ci
    core = jnp.reshape(ci, (1,)).astype(jnp.int32)
    place = jnp.stack([chip, ci, device]).astype(jnp.int32)
    x, tgt = x[0], loss_target[0]
    s = x.shape[0]
    cos, sin = _rope_tables(s)

    shard2d = {n: p[n].reshape(p[n].shape[-2:]) for n in BIG}
    shard_bf = {n: shard2d[n].astype(BF16) for n in BIG}
    ws = p["a_ws"][0]
    ws_t = jnp.swapaxes(ws, 1, 2)
    bs_t = p["a_bs"][0].T
    kv_norm_g, b_kv = p["kv_norm_g"].reshape(1, -1), p["b_kv"].reshape(1, -1)
    final_norm_g = p["final_norm_g"].reshape(1, -1)

    vec_shapes = [p[n].shape for n in ("a_norm_g", "a_ln_g", "a_ln_b")]
    vec_pack = _pack_rows([p["a_norm_g"], p["a_ln_g"], p["a_ln_b"]], 16)
    (vec_all,) = _comm_call(_gather_side([vec_pack]), "gather_vectors")
    vecs = [_unpack_rows(vec_all[k], vec_shapes) for k in range(N_CHIPS)]
    a_norm_g, a_ln_g, a_ln_b = (jnp.concatenate([vk[t] for vk in vecs], axis=-1) for t in range(3))

    (n_a,) = _rms_fwd(x, [a_norm_g], name="rms_a")
    order = jnp.stack([chip, 2 * (1 - xi) + yi, 2 * xi + (1 - yi), 2 * (1 - xi) + (1 - yi)]).astype(jnp.int32)
    z, a_w_in = _mm_gathering(n_a, shard_bf["a_w_in"], order, name="mm_a_in")
    y, (a_w_out,) = _gate_fwd(z, a_ln_g, a_ln_b, ws, bs_t, side=_gather_side([shard_bf["a_w_out"]]))
    a_w_out = a_w_out.reshape(A_WIDTH, D_MODEL)
    (h1, n_kv, n_b), (w_kv, b_w_in) = _mm_residual_norms(
        y, a_w_out, x, [kv_norm_g, p["b_norm_g"]], name="mm_a_out",
        side=_gather_side([shard_bf["w_kv"], shard_bf["b_w_in"]]))
    w_kv = w_kv.reshape(D_MODEL, 2 * KV_WIDTH)
    kv = _mm_nn(n_kv, w_kv, name="mm_kv", tn=2 * KV_WIDTH)
    kr, vv = _kv_rope(kv, b_kv, cos, sin)
    zb = _mm_nn(n_b, b_w_in, name="mm_b_in", tn=512, tm=1024, out_dtype=BF16)
    yb, (b_w_out,) = _attn_fwd(zb, kr, vv, cos, sin, p["b_bq"], p["b_sinks"], side=_gather_side([shard_bf["b_w_out"]]))
    b_w_out = b_w_out.reshape(B_WIDTH, D_MODEL)
    loss_blk, dh2, dh2b, d_final_g = _mm_residual_loss(yb, b_w_out, h1, tgt, final_norm_g, name="mm_b_out")

    d_b_w_out = _mm_tn(yb, dh2b, name="mm_d_b_w_out", tm=B_WIDTH, tn=D_MODEL)
    red_bo = _Reduction(["b_w_out"], [d_b_w_out.reshape(N_CHIPS, B_WIDTH // N_CHIPS, D_MODEL)], core, place)
    dyb, got = _mm_nt(dh2b, b_w_out, name="mm_dyb", out_dtype=BF16, side=red_bo.exchange_side())
    red_bo.took_exchange(got)
    dzb, dk_rot, dv, d_bq, d_sinks = _attn_bwd(zb, dyb, kr, vv, cos, sin, p["b_bq"], p["b_sinks"])
    dkv, d_b_kv = _kv_rope_bwd(dk_rot, dv, cos, sin)
    d_b_w_in, got = _mm_tn(n_b, dzb, name="mm_d_b_w_in", tm=D_MODEL, tn=512, shards=N_CHIPS,
                           side=red_bo.scatter_side())
    red_bo.took_scatter(got)
    d_w_kv, got = _mm_tn(n_kv, dkv, name="mm_d_w_kv", tm=D_MODEL, tn=2 * KV_WIDTH, side=red_bo.share_side())
    red_bo.took_share(got)
    red_bi = _Reduction(["b_w_in", "w_kv"], [d_b_w_in, d_w_kv.reshape(N_CHIPS, D_MODEL // N_CHIPS, 2 * KV_WIDTH)],
                        core, place)
    (dh1, dh1b, d_kv_g, d_b_g), got = _mm_nt_rms_bwd(
        [(dkv, w_kv, kv_norm_g), (dzb, b_w_in, p["b_norm_g"])], h1, dh2, name="mm_dn_b", tm=512,
        side=red_bi.exchange_side())
    red_bi.took_exchange(got)

    d_a_w_out, got = _mm_tn(y, dh1b, name="mm_d_a_w_out", tm=1024, tn=D_MODEL, side=red_bi.scatter_side())
    red_bi.took_scatter(got)
    red_ao = _Reduction(["a_w_out"], [d_a_w_out.reshape(N_CHIPS, A_WIDTH // N_CHIPS, D_MODEL)], core, place)
    sides = [red_ao.exchange_side(), red_bi.share_side()]
    dy, got = _mm_nt(dh1b, a_w_out, name="mm_dy", tn=1024, out_dtype=BF16, side=_join(sides))
    got = _split(got, sides)
    red_ao.took_exchange(got[0])
    red_bi.took_share(got[1])
    (dz, d_ln_g, d_ln_b, d_ws, d_bs_t), got = _gate_bwd(z, dy, a_ln_g, a_ln_b, ws, ws_t, bs_t,
                                                        side=red_ao.scatter_side())
    red_ao.took_scatter(got)
    d_a_w_in, got = _mm_tn(n_a, dz, name="mm_d_a_w_in", tm=D_MODEL, tn=1536, shards=N_CHIPS,
                           side=red_ao.share_side())
    red_ao.took_share(got)

    small = {
        "a_ws": d_ws, "a_bs": d_bs_t.T, "a_ln_g": d_ln_g, "a_ln_b": d_ln_b,
        "kv_norm_g": d_kv_g, "b_kv": d_b_kv, "b_norm_g": d_b_g, "b_bq": d_bq,
        "b_sinks": d_sinks[0:1, :N_Q_HEADS], "final_norm_g": d_final_g,
    }
    packed = [n for n in SMALL_ORDER if n != "a_norm_g"]
    small_shapes = [small[n].shape for n in packed] + [(1, 1)]
    small_pack = _pack_rows([small[n] for n in packed] + [loss_blk[0:1, 0:1]], 64)
    seg = small_pack.shape[0] // 8
    red_ai = _Reduction(["a_w_in"], [d_a_w_in], core, place, small=small_pack.reshape(8, seg, 128))
    red_ai.took_exchange(_comm_call(red_ai.exchange_side(), "exchange_last"))
    (dx, _, d_a_g), got = _mm_nt_rms_bwd([(dz, a_w_in, a_norm_g)], x, dh1, name="mm_dn_a", tm=256,
                                         side=red_ai.scatter_side())
    red_ai.took_scatter(got)
    d_a_g = _rows128(d_a_g)
    sides = [red_ai.share_side(), _spread_side(d_a_g)]
    got = _split(_comm_call(_join(sides), "share_last"), sides)
    red_ai.took_share(got[0])
    small_full = dict(zip(packed + ["loss"], _unpack_rows(red_ai.small_full.reshape(8 * seg, 128), small_shapes)))
    small_full["a_norm_g"] = _sum_in_device_order(d_a_g, got[1][0], place).reshape(1, -1)
    loss = small_full["loss"].reshape(())

    grad_big = {**red_bo.grads, **red_bi.grads, **red_ao.grads, **red_ai.grads}
    grads = {}
    for n in SMALL_ORDER:
        gfull = small_full[n]
        if n in SHARDED_SMALL:
            width = p[n].shape[-1]
            gfull = lax.dynamic_slice_in_dim(gfull, chip * width, width, axis=-1)
        grads[n] = gfull.reshape(p[n].shape)
    for n in BIG:
        grads[n] = grad_big[n].reshape(p[n].shape)

    delta, new_m, new_v = {}, {}, {}
    for n in BIG:
        d, nm, nv = _adamw(shard2d[n], grad_big[n], m[n].reshape(shard2d[n].shape), v[n].reshape(shard2d[n].shape),
                           name="adamw_" + n)
        delta[n], new_m[n], new_v[n] = d.reshape(p[n].shape), nm.reshape(p[n].shape), nv.reshape(p[n].shape)
    shapes = [p[n].shape for n in SMALL_ORDER]
    packs = [_pack_rows([src[n] for n in SMALL_ORDER], 8) for src in (p, grads, m, v)]
    outs = _adamw(*packs, name="adamw_small")
    for res, packed in zip((delta, new_m, new_v), outs):
        for n, val in zip(SMALL_ORDER, _unpack_rows(packed, shapes)):
            res[n] = val

    return (loss, dx[None], *[grads[n] for n in WEIGHTS], *[delta[n] for n in WEIGHTS],
            *[new_m[n] for n in WEIGHTS], *[new_v[n] for n in WEIGHTS])


def kernel(x, a_norm_g, a_w_in, a_ln_g, a_ln_b, a_ws, a_bs, a_w_out, kv_norm_g, w_kv, b_kv, b_norm_g, b_w_in, b_bq, b_sinks, b_w_out, final_norm_g, loss_target, m_a_norm_g, m_a_w_in, m_a_ln_g, m_a_ln_b, m_a_ws, m_a_bs, m_a_w_out, m_kv_norm_g, m_w_kv, m_b_kv, m_b_norm_g, m_b_w_in, m_b_bq, m_b_sinks, m_b_w_out, m_final_norm_g, v_a_norm_g, v_a_w_in, v_a_ln_g, v_a_ln_b, v_a_ws, v_a_bs, v_a_w_out, v_kv_norm_g, v_w_kv, v_b_kv, v_b_norm_g, v_b_w_in, v_b_bq, v_b_sinks, v_b_w_out, v_final_norm_g):
    p = dict(a_norm_g=a_norm_g, a_w_in=a_w_in, a_ln_g=a_ln_g, a_ln_b=a_ln_b, a_ws=a_ws, a_bs=a_bs, a_w_out=a_w_out,
             kv_norm_g=kv_norm_g, w_kv=w_kv, b_kv=b_kv, b_norm_g=b_norm_g, b_w_in=b_w_in, b_bq=b_bq, b_sinks=b_sinks,
             b_w_out=b_w_out, final_norm_g=final_norm_g)
    m = dict(a_norm_g=m_a_norm_g, a_w_in=m_a_w_in, a_ln_g=m_a_ln_g, a_ln_b=m_a_ln_b, a_ws=m_a_ws, a_bs=m_a_bs,
             a_w_out=m_a_w_out, kv_norm_g=m_kv_norm_g, w_kv=m_w_kv, b_kv=m_b_kv, b_norm_g=m_b_norm_g, b_w_in=m_b_w_in,
             b_bq=m_b_bq, b_sinks=m_b_sinks, b_w_out=m_b_w_out, final_norm_g=m_final_norm_g)
    v = dict(a_norm_g=v_a_norm_g, a_w_in=v_a_w_in, a_ln_g=v_a_ln_g, a_ln_b=v_a_ln_b, a_ws=v_a_ws, a_bs=v_a_bs,
             a_w_out=v_a_w_out, kv_norm_g=v_kv_norm_g, w_kv=v_w_kv, b_kv=v_b_kv, b_norm_g=v_b_norm_g, b_w_in=v_b_w_in,
             b_bq=v_b_bq, b_sinks=v_b_sinks, b_w_out=v_b_w_out, final_norm_g=v_final_norm_g)
    return _step(x, loss_target, p, m, v)
```

```python
import functools
import math

import jax
import jax.numpy as jnp
from jax import lax
from jax.experimental import pallas as pl
from jax.experimental.pallas import tpu as pltpu

F32 = jnp.float32
BF16 = jnp.bfloat16

D_MODEL = 1024
CHUNK = 128
A_WIDTH = 2048
A_GROUPS = 16
HEAD_DIM = 64
N_Q_HEADS = 16
N_KV_HEADS = 2
Q_PER_KV = 8
B_WIDTH = 1024
KV_WIDTH = 128
ROPE_THETA = 10000.0
EPS = 1e-5
N_CHIPS = 4

ADAM_LR = 0.001
ADAM_B1 = 0.9
ADAM_B2 = 0.999
ADAM_EPS = 1e-08
ADAM_WD = 0.01
ADAM_STEP = 10

VMEM_LIMIT = 48 * 1024 * 1024
MESH = pl.DeviceIdType.MESH
NEG_BIG = -1e30
HBM = pl.BlockSpec(memory_space=pl.ANY)

NN = (((1,), (0,)), ((), ()))
NT = (((1,), (1,)), ((), ()))
TN = (((0,), (0,)), ((), ()))


def _cparams(**kw):
    return pltpu.CompilerParams(vmem_limit_bytes=VMEM_LIMIT, **kw)


class _Side:
    def __init__(self, ins, out_shapes, sems, start, finish, aliases=None):
        self.ins, self.out_shapes, self.sems = list(ins), list(out_shapes), list(sems)
        self.start, self.finish = start, finish
        self.aliases = dict(aliases or {})


def _join(sides):
    sides = [s for s in sides if s is not None]
    if not sides:
        return None
    offs, i, o, m = [], 0, 0, 0
    for s in sides:
        offs.append((i, o, m))
        i, o, m = i + len(s.ins), o + len(s.out_shapes), m + len(s.sems)

    def run(which):
        def go(ins, outs, sems):
            for s, (a, b, c) in zip(sides, offs):
                getattr(s, which)(ins[a:a + len(s.ins)], outs[b:b + len(s.out_shapes)], sems[c:c + len(s.sems)])
        return go

    aliases = {}
    for s, (a, b, _) in zip(sides, offs):
        aliases.update({a + k: b + v for k, v in s.aliases.items()})
    return _Side([x for s in sides for x in s.ins], [x for s in sides for x in s.out_shapes],
                 [x for s in sides for x in s.sems], run("start"), run("finish"), aliases)


def _split(side_outs, sides):
    out, pos = [], 0
    for s in sides:
        out.append(list(side_outs[pos:pos + len(s.out_shapes)]))
        pos += len(s.out_shapes)
    return out


def _call(body, *, grid, in_specs, out_specs, out_shape, args, name, scratch=(), side=None):
    in_specs, out_specs, out_shape, scratch = list(in_specs), list(out_specs), list(out_shape), list(scratch)
    if side is None:
        res = pl.pallas_call(body, grid=grid, in_specs=in_specs, out_specs=out_specs, out_shape=out_shape,
                             scratch_shapes=scratch, name=name, compiler_params=_cparams())(*args)
        return list(res), []
    n_in, n_out, n_sc = len(in_specs), len(out_specs), len(scratch)
    s_in, s_out = len(side.ins), len(side.out_shapes)

    def wrapped(*refs):
        ins, refs = refs[:n_in], refs[n_in:]
        side_ins, refs = refs[:s_in], refs[s_in:]
        outs, refs = refs[:n_out], refs[n_out:]
        side_outs, refs = refs[:s_out], refs[s_out:]
        scr, side_sems = refs[:n_sc], refs[n_sc:]
        ids = [pl.program_id(a) for a in range(len(grid))]
        first = functools.reduce(jnp.logical_and, [i == 0 for i in ids])
        last = functools.reduce(jnp.logical_and, [i == g - 1 for i, g in zip(ids, grid)])

        @pl.when(first)
        def _():
            side.start(side_ins, side_outs, side_sems)

        body(*ins, *outs, *scr)

        @pl.when(last)
        def _():
            side.finish(side_ins, side_outs, side_sems)

    res = pl.pallas_call(
        wrapped, grid=grid, in_specs=in_specs + [HBM] * s_in, out_specs=out_specs + [HBM] * s_out,
        out_shape=out_shape + side.out_shapes, scratch_shapes=scratch + side.sems,
        input_output_aliases={n_in + k: n_out + v for k, v in side.aliases.items()},
        name=name, compiler_params=_cparams(),
    )(*args, *side.ins)
    return list(res[:n_out]), list(res[n_out:])


def _comm_call(side, name):
    s_in, s_out = len(side.ins), len(side.out_shapes)

    def body(*refs):
        ins, outs, sems = refs[:s_in], refs[s_in:s_in + s_out], refs[s_in + s_out:]
        side.start(ins, outs, sems)
        side.finish(ins, outs, sems)

    return list(pl.pallas_call(
        body, in_specs=[HBM] * s_in, out_specs=[HBM] * s_out, out_shape=side.out_shapes, scratch_shapes=side.sems,
        input_output_aliases=side.aliases, name=name,
    )(*side.ins))


def _matmul(a, b, *, dims, grid, a_spec, b_spec, o_spec, out_shape, name, acc_axis=None,
            residual=None, r_spec=None, side=None):
    has_res = residual is not None

    def body(*refs):
        if has_res:
            a_ref, b_ref, r_ref, o_ref = refs
        else:
            a_ref, b_ref, o_ref = refs
        part = lax.dot_general(a_ref[...], b_ref[...], dims, preferred_element_type=F32)
        if acc_axis is None:
            if has_res:
                part = part + r_ref[...]
            o_ref[...] = part.astype(o_ref.dtype)
        else:
            k = pl.program_id(acc_axis)

            @pl.when(k == 0)
            def _():
                o_ref[...] = part

            @pl.when(k > 0)
            def _():
                o_ref[...] += part

    in_specs = [a_spec, b_spec] + ([r_spec] if has_res else [])
    args = (a, b) + ((residual,) if has_res else ())
    (out,), side_outs = _call(body, grid=grid, in_specs=in_specs, out_specs=[o_spec], out_shape=[out_shape],
                              args=args, name=name, side=side)
    return (out, side_outs) if side is not None else out


def _row_tile(s, want):
    return min(s, want)


def _mm_nn(a, b, *, name, tn, out_dtype=F32, residual=None, tm=512, side=None):
    s, k = a.shape
    tm = _row_tile(s, tm)
    if b.ndim == 3:
        nsh, _, nc = b.shape
        npb = nc // tn
        n = nsh * nc
        b_spec = pl.BlockSpec((None, k, tn), lambda i, j: (j // npb, 0, j % npb))
    else:
        n = b.shape[1]
        b_spec = pl.BlockSpec((k, tn), lambda i, j: (0, j))
    return _matmul(
        a, b, dims=NN, grid=(s // tm, n // tn),
        a_spec=pl.BlockSpec((tm, k), lambda i, j: (i, 0)), b_spec=b_spec,
        o_spec=pl.BlockSpec((tm, tn), lambda i, j: (i, j)),
        out_shape=jax.ShapeDtypeStruct((s, n), out_dtype), name=name, side=side,
        residual=residual, r_spec=pl.BlockSpec((tm, tn), lambda i, j: (i, j)) if residual is not None else None)


def _mm_nt(a, b, *, name, tn=None, tm=512, out_dtype=F32, side=None):
    s, k = a.shape
    tm = _row_tile(s, tm)
    if b.ndim == 3:
        nsh, n, kc = b.shape

        def body(a_ref, b_ref, o_ref):
            acc = None
            for sh in range(nsh):
                part = lax.dot_general(a_ref[:, sh * kc:(sh + 1) * kc], b_ref[sh], NT, preferred_element_type=F32)
                acc = part if acc is None else acc + part
            o_ref[...] = acc

        (out,), side_outs = _call(
            body, grid=(s // tm,),
            in_specs=[pl.BlockSpec((tm, k), lambda i: (i, 0)), pl.BlockSpec((nsh, n, kc), lambda i: (0, 0, 0))],
            out_specs=[pl.BlockSpec((tm, n), lambda i: (i, 0))], out_shape=[jax.ShapeDtypeStruct((s, n), F32)],
            args=(a, b), name=name, side=side)
        return (out, side_outs) if side is not None else out
    n = b.shape[0]
    tn = n if tn is None else tn
    return _matmul(
        a, b, dims=NT, grid=(s // tm, n // tn),
        a_spec=pl.BlockSpec((tm, k), lambda i, j: (i, 0)),
        b_spec=pl.BlockSpec((tn, k), lambda i, j: (j, 0)),
        o_spec=pl.BlockSpec((tm, tn), lambda i, j: (i, j)),
        out_shape=jax.ShapeDtypeStruct((s, n), out_dtype), name=name, side=side)


def _mm_tn(a, b, *, name, tm, tn, tk=2048, shards=None, side=None):
    s, m = a.shape
    n = b.shape[1]
    tk = _row_tile(s, tk)
    if shards is None:
        o_spec = pl.BlockSpec((tm, tn), lambda i, j, kk: (i, j))
        out_shape = jax.ShapeDtypeStruct((m, n), F32)
    else:
        assert tm == m
        nc = n // shards
        npb = nc // tn
        o_spec = pl.BlockSpec((None, m, tn), lambda i, j, kk: (j // npb, 0, j % npb))
        out_shape = jax.ShapeDtypeStruct((shards, m, nc), F32)
    return _matmul(
        a, b, dims=TN, grid=(m // tm, n // tn, s // tk), acc_axis=2,
        a_spec=pl.BlockSpec((tk, tm), lambda i, j, kk: (kk, i)),
        b_spec=pl.BlockSpec((tk, tn), lambda i, j, kk: (kk, j)),
        o_spec=o_spec, out_shape=out_shape, name=name, side=side)


def _rstd(x):
    return lax.rsqrt(jnp.mean(x * x, axis=-1, keepdims=True) + EPS)


def _rms_fwd(x, gains, *, name, tr=256):
    s, d = x.shape
    tr = _row_tile(s, tr)
    ng = len(gains)

    def body(*refs):
        xv = refs[0][...]
        xh = xv * _rstd(xv)
        for t in range(ng):
            refs[1 + ng + t][...] = (xh * refs[1 + t][...]).astype(BF16)

    row = pl.BlockSpec((tr, d), lambda i: (i, 0))
    vec = pl.BlockSpec((1, d), lambda i: (0, 0))
    outs, _ = _call(body, grid=(s // tr,), in_specs=[row] + [vec] * ng, out_specs=[row] * ng,
                    out_shape=[jax.ShapeDtypeStruct((s, d), BF16)] * ng, args=(x, *gains), name=name)
    return outs


def _rms_bwd(x, dns, gains, dres, *, name, tr=256):
    s, d = x.shape
    tr = _row_tile(s, tr)
    ng = len(gains)

    def body(*refs):
        x_ref = refs[0]
        dn_refs = refs[1:1 + ng]
        g_refs = refs[1 + ng:1 + 2 * ng]
        dres_ref = refs[1 + 2 * ng]
        dx_ref, dxb_ref = refs[2 + 2 * ng], refs[3 + 2 * ng]
        dg_refs = refs[4 + 2 * ng:]
        i = pl.program_id(0)
        xv = x_ref[...]
        r = _rstd(xv)
        xh = xv * r
        acc = jnp.zeros_like(xv)
        for t in range(ng):
            dn = dn_refs[t][...]
            acc = acc + dn * g_refs[t][...]
            dgt = jnp.sum(dn * xh, axis=0, keepdims=True)

            @pl.when(i == 0)
            def _(t=t, dgt=dgt):
                dg_refs[t][...] = dgt

            @pl.when(i > 0)
            def _(t=t, dgt=dgt):
                dg_refs[t][...] += dgt

        dx = dres_ref[...] + r * (acc - xh * jnp.mean(acc * xh, axis=-1, keepdims=True))
        dx_ref[...] = dx
        dxb_ref[...] = dx.astype(BF16)

    row = pl.BlockSpec((tr, d), lambda i: (i, 0))
    vec = pl.BlockSpec((1, d), lambda i: (0, 0))
    outs, _ = _call(
        body, grid=(s // tr,), in_specs=[row] + [row] * ng + [vec] * ng + [row],
        out_specs=[row, row] + [vec] * ng,
        out_shape=[jax.ShapeDtypeStruct((s, d), F32), jax.ShapeDtypeStruct((s, d), BF16)]
        + [jax.ShapeDtypeStruct((1, d), F32)] * ng,
        args=(x, *dns, *gains, dres), name=name)
    return outs[0], outs[1], outs[2:]


def _loss_head(h, tgt, gain, *, tr=256):
    s, d = h.shape
    tr = _row_tile(s, tr)

    def body(h_ref, t_ref, g_ref, loss_ref, dh_ref, dhb_ref, dg_ref):
        i = pl.program_id(0)
        hv = h_ref[...]
        g = g_ref[...]
        r = _rstd(hv)
        xh = hv * r
        diff = xh * g - t_ref[...]
        part = 0.5 / d * jnp.sum(jnp.sum(diff * diff, axis=-1, keepdims=True), axis=0, keepdims=True)
        dout = diff * (1.0 / d)
        a = dout * g
        dh = r * (a - xh * jnp.mean(a * xh, axis=-1, keepdims=True))
        dh_ref[...] = dh
        dhb_ref[...] = dh.astype(BF16)
        dgt = jnp.sum(dout * xh, axis=0, keepdims=True)
        lpart = jnp.broadcast_to(part, (8, 128))

        @pl.when(i == 0)
        def _():
            dg_ref[...] = dgt
            loss_ref[...] = lpart

        @pl.when(i > 0)
        def _():
            dg_ref[...] += dgt
            loss_ref[...] += lpart

    row = pl.BlockSpec((tr, d), lambda i: (i, 0))
    vec = pl.BlockSpec((1, d), lambda i: (0, 0))
    outs, _ = _call(
        body, grid=(s // tr,), in_specs=[row, row, vec],
        out_specs=[pl.BlockSpec((8, 128), lambda i: (0, 0)), row, row, vec],
        out_shape=[jax.ShapeDtypeStruct((8, 128), F32), jax.ShapeDtypeStruct((s, d), F32),
                   jax.ShapeDtypeStruct((s, d), BF16), jax.ShapeDtypeStruct((1, d), F32)],
        args=(h, tgt, gain), name="loss_head")
    return outs


def _accumulate(i, ref, value):
    @pl.when(i == 0)
    def _():
        ref[...] = value

    @pl.when(i > 0)
    def _():
        ref[...] += value


def _mm_residual_norms(y, w, res, gains, *, name, tm=512, side=None):
    s, k = y.shape
    d = w.shape[1]
    tm = _row_tile(s, tm)
    ng = len(gains)

    def body(y_ref, w_ref, r_ref, *rest):
        g_refs, h_ref, n_refs = rest[:ng], rest[ng], rest[ng + 1:]
        h = r_ref[...] + jnp.dot(y_ref[...], w_ref[...], preferred_element_type=F32)
        h_ref[...] = h
        xh = h * _rstd(h)
        for t in range(ng):
            n_refs[t][...] = (xh * g_refs[t][...]).astype(BF16)

    row = pl.BlockSpec((tm, d), lambda i: (i, 0))
    vec = pl.BlockSpec((1, d), lambda i: (0, 0))
    return _call(
        body, grid=(s // tm,),
        in_specs=[pl.BlockSpec((tm, k), lambda i: (i, 0)), pl.BlockSpec((k, d), lambda i: (0, 0)), row] + [vec] * ng,
        out_specs=[row] * (1 + ng),
        out_shape=[jax.ShapeDtypeStruct((s, d), F32)] + [jax.ShapeDtypeStruct((s, d), BF16)] * ng,
        args=(y, w, res, *gains), name=name, side=side)


def _mm_residual_loss(y, w, res, tgt, gain, *, name, tm=512):
    s, k = y.shape
    d = w.shape[1]
    tm = _row_tile(s, tm)

    def body(y_ref, w_ref, r_ref, t_ref, g_ref, loss_ref, dh_ref, dhb_ref, dg_ref):
        i = pl.program_id(0)
        hv = r_ref[...] + jnp.dot(y_ref[...], w_ref[...], preferred_element_type=F32)
        g = g_ref[...]
        r = _rstd(hv)
        xh = hv * r
        diff = xh * g - t_ref[...]
        part = 0.5 / d * jnp.sum(jnp.sum(diff * diff, axis=-1, keepdims=True), axis=0, keepdims=True)
        dout = diff * (1.0 / d)
        a = dout * g
        dh = r * (a - xh * jnp.mean(a * xh, axis=-1, keepdims=True))
        dh_ref[...] = dh
        dhb_ref[...] = dh.astype(BF16)
        _accumulate(i, dg_ref, jnp.sum(dout * xh, axis=0, keepdims=True))
        _accumulate(i, loss_ref, jnp.broadcast_to(part, (8, 128)))

    row = pl.BlockSpec((tm, d), lambda i: (i, 0))
    vec = pl.BlockSpec((1, d), lambda i: (0, 0))
    outs, _ = _call(
        body, grid=(s // tm,),
        in_specs=[pl.BlockSpec((tm, k), lambda i: (i, 0)), pl.BlockSpec((k, d), lambda i: (0, 0)), row, row, vec],
        out_specs=[pl.BlockSpec((8, 128), lambda i: (0, 0)), row, row, vec],
        out_shape=[jax.ShapeDtypeStruct((8, 128), F32), jax.ShapeDtypeStruct((s, d), F32),
                   jax.ShapeDtypeStruct((s, d), BF16), jax.ShapeDtypeStruct((1, d), F32)],
        args=(y, w, res, tgt, gain), name=name)
    return outs


def _mm_nt_rms_bwd(terms, x, dres, *, name, tm, side=None):
    s, d = x.shape
    tm = _row_tile(s, tm)
    nt = len(terms)

    def body(*refs):
        a_refs, b_refs, g_refs = refs[0:3 * nt:3], refs[1:3 * nt:3], refs[2:3 * nt:3]
        x_ref, dres_ref = refs[3 * nt], refs[3 * nt + 1]
        dx_ref, dxb_ref = refs[3 * nt + 2], refs[3 * nt + 3]
        dg_refs = refs[3 * nt + 4:]
        i = pl.program_id(0)
        xv = x_ref[...]
        r = _rstd(xv)
        xh = xv * r
        acc = jnp.zeros_like(xv)
        for t in range(nt):
            b_ref = b_refs[t]
            if len(b_ref.shape) == 3:
                kc = b_ref.shape[2]
                dn = None
                for sh in range(b_ref.shape[0]):
                    part = lax.dot_general(a_refs[t][:, sh * kc:(sh + 1) * kc], b_ref[sh], NT, preferred_element_type=F32)
                    dn = part if dn is None else dn + part
            else:
                dn = lax.dot_general(a_refs[t][...], b_ref[...], NT, preferred_element_type=F32)
            acc = acc + dn * g_refs[t][...]
            _accumulate(i, dg_refs[t], jnp.sum(dn * xh, axis=0, keepdims=True))
        dx = dres_ref[...] + r * (acc - xh * jnp.mean(acc * xh, axis=-1, keepdims=True))
        dx_ref[...] = dx
        dxb_ref[...] = dx.astype(BF16)

    row = pl.BlockSpec((tm, d), lambda i: (i, 0))
    vec = pl.BlockSpec((1, d), lambda i: (0, 0))
    in_specs, args = [], []
    for a, b, g in terms:
        in_specs += [pl.BlockSpec((tm, a.shape[1]), lambda i: (i, 0)),
                     pl.BlockSpec(b.shape, (lambda i: (0, 0, 0)) if b.ndim == 3 else (lambda i: (0, 0))), vec]
        args += [a, b, g]
    return _call(
        body, grid=(s // tm,), in_specs=in_specs + [row, row], out_specs=[row, row] + [vec] * nt,
        out_shape=[jax.ShapeDtypeStruct((s, d), F32), jax.ShapeDtypeStruct((s, d), BF16)]
        + [jax.ShapeDtypeStruct((1, d), F32)] * nt,
        args=(*args, x, dres), name=name, side=side)


def _causal_mask(transposed=False):
    row = lax.broadcasted_iota(jnp.int32, (CHUNK, CHUNK), 0)
    col = lax.broadcasted_iota(jnp.int32, (CHUNK, CHUNK), 1)
    return col >= row if transposed else col <= row


def _silu_parts(g):
    sg = jax.nn.sigmoid(g)
    return g * sg, sg * (1.0 + g * (1.0 - sg))


def _gate_fwd(z, ln_g, ln_b, ws, bs_t, *, tr=256, side=None):
    s = z.shape[0]
    tr = _row_tile(s, tr)
    w = A_WIDTH

    def body(u_ref, v_ref, g_ref, lg_ref, lb_ref, ws_ref, bst_ref, y_ref):
        v = v_ref[...].astype(F32)
        mu = jnp.mean(v, axis=-1, keepdims=True)
        xc = v - mu
        rs = lax.rsqrt(jnp.mean(xc * xc, axis=-1, keepdims=True) + EPS)
        vln = (xc * rs * lg_ref[...] + lb_ref[...]).astype(BF16)
        mask = _causal_mask()
        for grp in range(A_GROUPS):
            cols = slice(grp * CHUNK, (grp + 1) * CHUNK)
            wsm = jnp.where(mask, ws_ref[grp], 0.0).astype(BF16)
            bcol = bst_ref[:, grp:grp + 1]
            for ci in range(tr // CHUNK):
                rows = slice(ci * CHUNK, (ci + 1) * CHUNK)
                sv = jnp.dot(wsm, vln[rows, cols], preferred_element_type=F32) + bcol
                gv = g_ref[rows, cols].astype(F32)
                y_ref[rows, cols] = (u_ref[rows, cols].astype(F32) * sv * (gv * jax.nn.sigmoid(gv))).astype(BF16)

    vec = pl.BlockSpec((1, w), lambda i: (0, 0))
    (y,), side_outs = _call(
        body, grid=(s // tr,),
        in_specs=[pl.BlockSpec((tr, w), lambda i: (i, 0)), pl.BlockSpec((tr, w), lambda i: (i, 1)),
                  pl.BlockSpec((tr, w), lambda i: (i, 2)), vec, vec,
                  pl.BlockSpec((A_GROUPS, CHUNK, CHUNK), lambda i: (0, 0, 0)),
                  pl.BlockSpec((CHUNK, A_GROUPS), lambda i: (0, 0))],
        out_specs=[pl.BlockSpec((tr, w), lambda i: (i, 0))],
        out_shape=[jax.ShapeDtypeStruct((s, w), BF16)], args=(z, z, z, ln_g, ln_b, ws, bs_t), name="gate_fwd",
        side=side)
    return y, side_outs


def _gate_bwd(z, dy, ln_g, ln_b, ws, ws_t, bs_t, *, tr=256, side=None):
    s = z.shape[0]
    tr = _row_tile(s, tr)
    w = A_WIDTH
    nsteps = s // tr

    def body(u_ref, v_ref, g_ref, dy_ref, lg_ref, lb_ref, ws_ref, wst_ref, bst_ref,
             dz_ref, dlg_ref, dlb_ref, dws_ref, dbst_ref, dvln_sc, dsv_sc):
        i = pl.program_id(0)

        @pl.when(i == 0)
        def _():
            dws_ref[...] = jnp.zeros_like(dws_ref)
            dsv_sc[...] = jnp.zeros_like(dsv_sc)

        v = v_ref[...].astype(F32)
        mu = jnp.mean(v, axis=-1, keepdims=True)
        xc = v - mu
        rs = lax.rsqrt(jnp.mean(xc * xc, axis=-1, keepdims=True) + EPS)
        xh = xc * rs
        lg = lg_ref[...]
        vln = (xh * lg + lb_ref[...]).astype(BF16)
        mask = _causal_mask()
        mask_t = _causal_mask(transposed=True)
        for grp in range(A_GROUPS):
            cols = slice(grp * CHUNK, (grp + 1) * CHUNK)
            wsm = jnp.where(mask, ws_ref[grp], 0.0).astype(BF16)
            wsm_t = jnp.where(mask_t, wst_ref[grp], 0.0).astype(BF16)
            bcol = bst_ref[:, grp:grp + 1]
            for ci in range(tr // CHUNK):
                rows = slice(ci * CHUNK, (ci + 1) * CHUNK)
                vb = vln[rows, cols]
                sv = jnp.dot(wsm, vb, preferred_element_type=F32) + bcol
                uv = u_ref[rows, cols].astype(F32)
                silu, dsilu = _silu_parts(g_ref[rows, cols].astype(F32))
                dyv = dy_ref[rows, cols].astype(F32)
                dyu = dyv * uv
                dz_ref[rows, cols] = (dyv * sv * silu).astype(BF16)
                dz_ref[rows, 2 * w + grp * CHUNK:2 * w + (grp + 1) * CHUNK] = (dyu * sv * dsilu).astype(BF16)
                dsv = dyu * silu
                dsvb = dsv.astype(BF16)
                dvln_sc[rows, cols] = jnp.dot(wsm_t, dsvb, preferred_element_type=F32)
                dws_ref[grp] += lax.dot_general(dsvb, vb, NT, preferred_element_type=F32)
                dsv_sc[grp] += dsv
        dvln = dvln_sc[...]
        dlg_t = jnp.sum(dvln * xh, axis=0, keepdims=True)
        dlb_t = jnp.sum(dvln, axis=0, keepdims=True)
        a = dvln * lg
        dv = rs * (a - jnp.mean(a, axis=-1, keepdims=True) - xh * jnp.mean(a * xh, axis=-1, keepdims=True))
        dz_ref[:, w:2 * w] = dv.astype(BF16)

        @pl.when(i == 0)
        def _():
            dlg_ref[...] = dlg_t
            dlb_ref[...] = dlb_t

        @pl.when(i > 0)
        def _():
            dlg_ref[...] += dlg_t
            dlb_ref[...] += dlb_t

        @pl.when(i == nsteps - 1)
        def _():
            for grp in range(A_GROUPS):
                dws_ref[grp] = jnp.where(mask, dws_ref[grp], 0.0)
                dbst_ref[:, grp:grp + 1] = jnp.sum(dsv_sc[grp], axis=-1, keepdims=True)

    vec = pl.BlockSpec((1, w), lambda i: (0, 0))
    wsspec = pl.BlockSpec((A_GROUPS, CHUNK, CHUNK), lambda i: (0, 0, 0))
    bsspec = pl.BlockSpec((CHUNK, A_GROUPS), lambda i: (0, 0))
    return _call(
        body, grid=(nsteps,),
        in_specs=[pl.BlockSpec((tr, w), lambda i: (i, 0)), pl.BlockSpec((tr, w), lambda i: (i, 1)),
                  pl.BlockSpec((tr, w), lambda i: (i, 2)), pl.BlockSpec((tr, w), lambda i: (i, 0)),
                  vec, vec, wsspec, wsspec, bsspec],
        out_specs=[pl.BlockSpec((tr, 3 * w), lambda i: (i, 0)), vec, vec, wsspec, bsspec],
        out_shape=[jax.ShapeDtypeStruct((s, 3 * w), BF16), jax.ShapeDtypeStruct((1, w), F32),
                   jax.ShapeDtypeStruct((1, w), F32), jax.ShapeDtypeStruct((A_GROUPS, CHUNK, CHUNK), F32),
                   jax.ShapeDtypeStruct((CHUNK, A_GROUPS), F32)],
        scratch=[pltpu.VMEM((tr, w), F32), pltpu.VMEM((A_GROUPS, CHUNK, CHUNK), F32)],
        args=(z, z, z, dy, ln_g, ln_b, ws, ws_t, bs_t), name="gate_bwd", side=side)


HEADS_PER_BLOCK = 128 // HEAD_DIM
BLOCKS_PER_KV = Q_PER_KV // HEADS_PER_BLOCK
SCALE = HEAD_DIM ** -0.5
LOG2_E = math.log2(math.e)


def _rope_tables(s):
    inv_freq = ROPE_THETA ** (-jnp.arange(0, HEAD_DIM, 2, dtype=F32) / HEAD_DIM)
    ang = jnp.arange(s, dtype=F32)[:, None] * inv_freq[None, :]
    cos, sin = jnp.cos(ang), jnp.sin(ang)
    cos2 = jnp.concatenate([cos, cos], axis=-1)
    sin2 = jnp.concatenate([-sin, sin], axis=-1)
    return jnp.tile(cos2, (1, 2)), jnp.tile(sin2, (1, 2))


def _swap_halves(x):
    n = x.shape[-1]
    lane = lax.broadcasted_iota(jnp.int32, x.shape, x.ndim - 1)
    first = (lane % HEAD_DIM) < (HEAD_DIM // 2)
    return jnp.where(first, pltpu.roll(x, n - HEAD_DIM // 2, x.ndim - 1), pltpu.roll(x, HEAD_DIM // 2, x.ndim - 1))


def _left_half(rows):
    return lax.broadcasted_iota(jnp.int32, (rows, 128), 1) < HEAD_DIM


def _dup_heads(x):
    left = _left_half(x.shape[0])
    swapped = pltpu.roll(x, HEAD_DIM, 1)
    return jnp.concatenate([jnp.where(left, x, swapped), jnp.where(left, swapped, x)], axis=-1)


def _fold_heads(a):
    b0, b1 = a[:, :128], a[:, 128:]
    f0 = b0 + pltpu.roll(b0, HEAD_DIM, 1)
    f1 = b1 + pltpu.roll(b1, HEAD_DIM, 1)
    return jnp.where(_left_half(a.shape[0]), f0, f1)


def _kv_rope(kv, b_kv, cos, sin, *, tr=512):
    s = kv.shape[0]
    tr = _row_tile(s, tr)

    def body(kv_ref, b_ref, c_ref, s_ref, k_ref, v_ref):
        x = kv_ref[...] + b_ref[...]
        k = x[:, :KV_WIDTH]
        k_ref[...] = _dup_heads(k * c_ref[...] + _swap_halves(k) * s_ref[...]).astype(BF16)
        v_ref[...] = _dup_heads(x[:, KV_WIDTH:]).astype(BF16)

    tab = pl.BlockSpec((tr, KV_WIDTH), lambda i: (i, 0))
    wide = pl.BlockSpec((tr, 2 * KV_WIDTH), lambda i: (i, 0))
    outs, _ = _call(body, grid=(s // tr,),
                    in_specs=[wide, pl.BlockSpec((1, 2 * KV_WIDTH), lambda i: (0, 0)), tab, tab],
                    out_specs=[wide, wide], out_shape=[jax.ShapeDtypeStruct((s, 2 * KV_WIDTH), BF16)] * 2,
                    args=(kv, b_kv, cos, sin), name="kv_rope")
    return outs


def _kv_rope_bwd(dk2, dv2, cos, sin, *, tr=512):
    s = dk2.shape[0]
    tr = _row_tile(s, tr)

    def body(dk_ref, dv_ref, c_ref, s_ref, dkv_ref, db_ref):
        i = pl.program_id(0)
        d = _fold_heads(dk_ref[...])
        dk = d * c_ref[...] + _swap_halves(d * s_ref[...])
        dvv = _fold_heads(dv_ref[...])
        dkv_ref[:, :KV_WIDTH] = dk.astype(BF16)
        dkv_ref[:, KV_WIDTH:] = dvv.astype(BF16)
        sk = jnp.sum(dk, axis=0, keepdims=True)
        sv = jnp.sum(dvv, axis=0, keepdims=True)

        @pl.when(i == 0)
        def _():
            db_ref[:, :KV_WIDTH] = sk
            db_ref[:, KV_WIDTH:] = sv

        @pl.when(i > 0)
        def _():
            db_ref[:, :KV_WIDTH] += sk
            db_ref[:, KV_WIDTH:] += sv

    tab = pl.BlockSpec((tr, KV_WIDTH), lambda i: (i, 0))
    wide = pl.BlockSpec((tr, 2 * KV_WIDTH), lambda i: (i, 0))
    outs, _ = _call(body, grid=(s // tr,), in_specs=[wide, wide, tab, tab],
                    out_specs=[wide, pl.BlockSpec((1, 2 * KV_WIDTH), lambda i: (0, 0))],
                    out_shape=[jax.ShapeDtypeStruct((s, 2 * KV_WIDTH), BF16),
                               jax.ShapeDtypeStruct((1, 2 * KV_WIDTH), F32)],
                    args=(dk2, dv2, cos, sin), name="kv_rope_bwd")
    return outs


def _from_previous():
    cols = Q_PER_KV * CHUNK
    k = lax.broadcasted_iota(jnp.int32, (CHUNK, cols), 0)
    q = lax.broadcasted_iota(jnp.int32, (CHUNK, cols), 1) & (CHUNK - 1)
    return k > q


def _fold(x2, prev):
    return jnp.where(prev, x2[:CHUNK], x2[CHUNK:])


def _unfold(x, prev):
    zero = jnp.zeros_like(x)
    return jnp.concatenate([jnp.where(prev, x, zero), jnp.where(prev, zero, x)], axis=0)


def _stack_heads(blocks, left):
    parts = []
    for b in blocks:
        parts.append(jnp.where(left, b, jnp.zeros_like(b)))
        parts.append(jnp.where(left, jnp.zeros_like(b), b))
    return jnp.concatenate(parts, axis=0)


def _unstack_heads(xt):
    top = lax.broadcasted_iota(jnp.int32, (128, CHUNK), 0) < HEAD_DIM
    return [jnp.where(top, xt[:, (2 * b) * CHUNK:(2 * b + 1) * CHUNK], xt[:, (2 * b + 1) * CHUNK:(2 * b + 2) * CHUNK]).T
            for b in range(BLOCKS_PER_KV)]


def _sink_row(sk_ref, kvh):
    return jnp.concatenate([jnp.full((1, CHUNK), sk_ref[0, kvh * Q_PER_KV + r], F32) for r in range(Q_PER_KV)], axis=1)


def _stacked_probs(qs, kd, prev, sink, i):
    sc2 = lax.dot_general(kd, qs, NT, preferred_element_type=F32)
    no_previous = jnp.where(i > 0, 0.0, NEG_BIG)
    sc = jnp.where(prev, sc2[:CHUNK] + no_previous, sc2[CHUNK:])
    sink = sink * (1.0 / SCALE)
    m = jnp.maximum(jnp.max(sc, axis=0, keepdims=True), sink)
    p = jnp.exp2((sc - m) * (SCALE * LOG2_E))
    esink = jnp.exp2((sink - m) * (SCALE * LOG2_E))
    inv = 1.0 / (jnp.sum(p, axis=0, keepdims=True) + esink)
    return p * inv, esink * inv


def _lane_block(b):
    return slice(b * 128, (b + 1) * 128)


def _rope_blocks(zq_ref, bq_ref, cos, sin, kvh):
    out = []
    for b in range(BLOCKS_PER_KV):
        cols = _lane_block(kvh * BLOCKS_PER_KV + b)
        q = zq_ref[:, cols].astype(F32) + bq_ref[:, cols]
        out.append((q * cos + _swap_halves(q) * sin).astype(BF16))
    return out


def _attn_specs():
    qspec = pl.BlockSpec((CHUNK, B_WIDTH), lambda i: (i, 0))
    gspec = pl.BlockSpec((CHUNK, B_WIDTH), lambda i: (i, 1))
    prev = pl.BlockSpec((CHUNK, 2 * KV_WIDTH), lambda i: (jnp.maximum(i - 1, 0), 0))
    cur = pl.BlockSpec((CHUNK, 2 * KV_WIDTH), lambda i: (i, 0))
    tab = pl.BlockSpec((CHUNK, KV_WIDTH), lambda i: (i, 0))
    bq = pl.BlockSpec((1, B_WIDTH), lambda i: (0, 0))
    sinks = pl.BlockSpec(memory_space=pltpu.SMEM)
    return qspec, gspec, prev, cur, tab, bq, sinks


def _attn_fwd(zb, k2, v2, cos, sin, b_bq, sinks, *, side=None):
    s = zb.shape[0]

    def body(zq_ref, zg_ref, kp_ref, kc_ref, vp_ref, vc_ref, c_ref, s_ref, bq_ref, sk_ref, y_ref):
        i = pl.program_id(0)
        cos, sin = c_ref[...], s_ref[...]
        kcat = jnp.concatenate([kp_ref[...], kc_ref[...]], axis=0)
        vcat = jnp.concatenate([vp_ref[...], vc_ref[...]], axis=0)
        prev = _from_previous()
        left = _left_half(CHUNK)
        for kvh in range(N_KV_HEADS):
            qs = _stack_heads(_rope_blocks(zq_ref, bq_ref, cos, sin, kvh), left)
            p, _ = _stacked_probs(qs, kcat[:, _lane_block(kvh)], prev, _sink_row(sk_ref, kvh), i)
            ot = lax.dot_general(vcat[:, _lane_block(kvh)], _unfold(p, prev).astype(BF16), TN,
                                 preferred_element_type=F32)
            for b, ob in enumerate(_unstack_heads(ot)):
                cols = _lane_block(kvh * BLOCKS_PER_KV + b)
                gv = zg_ref[:, cols].astype(F32)
                y_ref[:, cols] = (ob * (gv * jax.nn.sigmoid(gv))).astype(BF16)

    qspec, gspec, prev, cur, tab, bq, sk = _attn_specs()
    (y,), side_outs = _call(body, grid=(s // CHUNK,), in_specs=[qspec, gspec, prev, cur, prev, cur, tab, tab, bq, sk],
                            out_specs=[qspec], out_shape=[jax.ShapeDtypeStruct((s, B_WIDTH), BF16)],
                            args=(zb, zb, k2, k2, v2, v2, cos, sin, b_bq, sinks), name="attn_fwd", side=side)
    return y, side_outs


def _attn_bwd(zb, dyb, k2, v2, cos, sin, b_bq, sinks):
    s = zb.shape[0]

    def body(zq_ref, zg_ref, dy_ref, kp_ref, kc_ref, vp_ref, vc_ref, c_ref, s_ref, bq_ref, sk_ref,
             dz_ref, dk_ref, dv_ref, dbq_ref, dsk_ref):
        i = pl.program_id(0)

        @pl.when(i == 0)
        def _():
            dk_ref[...] = jnp.zeros_like(dk_ref)
            dv_ref[...] = jnp.zeros_like(dv_ref)
            dbq_ref[...] = jnp.zeros_like(dbq_ref)
            dsk_ref[...] = jnp.zeros_like(dsk_ref)

        cos, sin = c_ref[...], s_ref[...]
        kcat = jnp.concatenate([kp_ref[...], kc_ref[...]], axis=0)
        vcat = jnp.concatenate([vp_ref[...], vc_ref[...]], axis=0)
        prev = _from_previous()
        left = _left_half(CHUNK)
        lane = lax.broadcasted_iota(jnp.int32, (1, 128), 1)
        dsk_row = jnp.zeros((1, 128), F32)
        cur_rows = pl.ds(pl.multiple_of(i * CHUNK, CHUNK), CHUNK)
        for kvh in range(N_KV_HEADS):
            kd, vd = kcat[:, _lane_block(kvh)], vcat[:, _lane_block(kvh)]
            qs = _stack_heads(_rope_blocks(zq_ref, bq_ref, cos, sin, kvh), left)
            p, psink = _stacked_probs(qs, kd, prev, _sink_row(sk_ref, kvh), i)
            pb = _unfold(p, prev).astype(BF16)
            ot = lax.dot_general(vd, pb, TN, preferred_element_type=F32)
            gates, dys = [], []
            for b in range(BLOCKS_PER_KV):
                cols = _lane_block(kvh * BLOCKS_PER_KV + b)
                gates.append(_silu_parts(zg_ref[:, cols].astype(F32)))
                dys.append(dy_ref[:, cols].astype(F32))
            dos = _stack_heads([(dyv * silu).astype(BF16) for dyv, (silu, _) in zip(dys, gates)], left)
            dp = _fold(lax.dot_general(vd, dos, NT, preferred_element_type=F32), prev)
            delta = jnp.sum(p * dp, axis=0, keepdims=True)
            ds = _unfold(p * (dp - delta) * SCALE, prev).astype(BF16)
            dqt = lax.dot_general(kd, ds, TN, preferred_element_type=F32)
            dk_part = jnp.dot(ds, qs, preferred_element_type=F32)
            dv_part = jnp.dot(pb, dos, preferred_element_type=F32)
            dk_ref[cur_rows, _lane_block(kvh)] += dk_part[CHUNK:]
            dv_ref[cur_rows, _lane_block(kvh)] += dv_part[CHUNK:]

            @pl.when(i > 0)
            def _(kvh=kvh, dk_part=dk_part, dv_part=dv_part):
                prev_rows = pl.ds(pl.multiple_of((i - 1) * CHUNK, CHUNK), CHUNK)
                dk_ref[prev_rows, _lane_block(kvh)] += dk_part[:CHUNK]
                dv_ref[prev_rows, _lane_block(kvh)] += dv_part[:CHUNK]

            sink_grad = psink * delta
            for r in range(Q_PER_KV):
                dsink = -jnp.sum(sink_grad[:, r * CHUNK:(r + 1) * CHUNK], axis=1, keepdims=True)
                dsk_row = dsk_row + jnp.where(lane == kvh * Q_PER_KV + r, dsink, 0.0)
            blocks = zip(_unstack_heads(ot), _unstack_heads(dqt), dys, gates)
            for b, (ob, dqr, dyv, (_, dsilu)) in enumerate(blocks):
                blk = kvh * BLOCKS_PER_KV + b
                dq = dqr * cos + _swap_halves(dqr * sin)
                dbq_ref[:, _lane_block(blk)] += jnp.sum(dq, axis=0, keepdims=True)
                dz_ref[:, _lane_block(blk)] = dq.astype(BF16)
                dz_ref[:, _lane_block(B_WIDTH // 128 + blk)] = (dyv * ob * dsilu).astype(BF16)
        dsk_ref[0:1, :] += dsk_row

    qspec, gspec, prev, cur, tab, bq, sk = _attn_specs()
    full = pl.BlockSpec((s, 2 * KV_WIDTH), lambda i: (0, 0))
    outs, _ = _call(
        body, grid=(s // CHUNK,),
        in_specs=[qspec, gspec, qspec, prev, cur, prev, cur, tab, tab, bq, sk],
        out_specs=[pl.BlockSpec((CHUNK, 2 * B_WIDTH), lambda i: (i, 0)), full, full, bq,
                   pl.BlockSpec((8, 128), lambda i: (0, 0))],
        out_shape=[jax.ShapeDtypeStruct((s, 2 * B_WIDTH), BF16), jax.ShapeDtypeStruct((s, 2 * KV_WIDTH), F32),
                   jax.ShapeDtypeStruct((s, 2 * KV_WIDTH), F32), jax.ShapeDtypeStruct((1, B_WIDTH), F32),
                   jax.ShapeDtypeStruct((8, 128), F32)],
        args=(zb, zb, dyb, k2, k2, v2, v2, cos, sin, b_bq, sinks), name="attn_bwd")
    return outs


def _place():
    x, y, c = lax.axis_index("x"), lax.axis_index("y"), lax.axis_index("c")
    return x, y, c, [(1 - x, y), (x, 1 - y), (1 - x, 1 - y)]


def _relations():
    return [(r >> 2 & 1, r >> 1 & 1, r & 1) for r in range(1, 8)]


def _gather_side(arrs):
    n = len(arrs)

    def copies(ins, outs, sems):
        send_ici, recv_ici, send_d2d, recv_d2d, local_sem = sems
        x, y, c, chips = _place()
        me = 2 * x + y

        def rows(a, half):
            hr = arrs[a].shape[0] // 2
            return pl.ds(half * hr, hr)

        def ici(a, j, src_chip, to):
            return pltpu.make_async_remote_copy(
                src_ref=ins[a].at[rows(a, c)], dst_ref=outs[a].at[src_chip, rows(a, c)],
                send_sem=send_ici.at[a, j], recv_sem=recv_ici.at[a, j], device_id=to, device_id_type=MESH)

        def d2d(a, j, chip, half):
            blk = outs[a].at[chip, rows(a, half)]
            return pltpu.make_async_remote_copy(
                src_ref=blk, dst_ref=blk, send_sem=send_d2d.at[a, j], recv_sem=recv_d2d.at[a, j],
                device_id=(x, y, 1 - c), device_id_type=MESH)

        local = [pltpu.make_async_copy(ins[a], outs[a].at[me], local_sem.at[a]) for a in range(n)]
        pairs = [(a, j, chip) for a in range(n) for j, chip in enumerate(chips)]
        return c, me, local, ici, d2d, pairs

    def start(ins, outs, sems):
        c, me, local, ici, _, pairs = copies(ins, outs, sems)
        for cp in local:
            cp.start()
        for a, j, chip in pairs:
            ici(a, j, me, (*chip, c)).start()

    def finish(ins, outs, sems):
        c, me, local, ici, d2d, pairs = copies(ins, outs, sems)
        for a, j, (px, py) in pairs:
            ici(a, j, 2 * px + py, (px, py, c)).wait_recv()
            d2d(a, j, 2 * px + py, c).start()
        for a, j, (px, py) in pairs:
            d2d(a, j, 2 * px + py, 1 - c).wait_recv()
        for a, j, (px, py) in pairs:
            ici(a, j, me, (px, py, c)).wait_send()
            d2d(a, j, 2 * px + py, c).wait_send()
        for cp in local:
            cp.wait()

    return _Side(arrs, [jax.ShapeDtypeStruct((N_CHIPS,) + a.shape, a.dtype) for a in arrs],
                 [pltpu.SemaphoreType.DMA((n, 3))] * 4 + [pltpu.SemaphoreType.DMA((n,))], start, finish)


def _exchange_side(grads):
    n = len(grads)

    def copies(ins, outs, sems):
        send_sem, recv_sem = sems
        x, y, c, _ = _place()
        cps = []
        for a in range(n):
            hr = grads[a].shape[1] // 2
            cps.append(pltpu.make_async_remote_copy(
                src_ref=ins[a].at[:, pl.ds((1 - c) * hr, hr), :], dst_ref=outs[a],
                send_sem=send_sem.at[a], recv_sem=recv_sem.at[a], device_id=(x, y, 1 - c), device_id_type=MESH))
        return cps

    def start(ins, outs, sems):
        for cp in copies(ins, outs, sems):
            cp.start()

    def finish(ins, outs, sems):
        for cp in copies(ins, outs, sems):
            cp.wait()

    return _Side(grads, [jax.ShapeDtypeStruct((g.shape[0], g.shape[1] // 2, g.shape[2]), g.dtype) for g in grads],
                 [pltpu.SemaphoreType.DMA((n,))] * 2, start, finish)


def _scatter_side(chip_sums, small=None):
    n = len(chip_sums)
    arrs = list(chip_sums) + ([small] if small is not None else [])

    def copies(ins, outs, sems):
        x, y, c, chips = _place()
        cps = []
        for a in range(n):
            for j, (px, py) in enumerate(chips):
                cps.append(pltpu.make_async_remote_copy(
                    src_ref=ins[a].at[2 * px + py], dst_ref=outs[a].at[j],
                    send_sem=sems[0].at[a, j], recv_sem=sems[1].at[a, j], device_id=(px, py, c), device_id_type=MESH))
        if small is not None:
            for r, (fx, fy, fc) in enumerate(_relations(), start=1):
                px, py, pc = x ^ fx, y ^ fy, c ^ fc
                cps.append(pltpu.make_async_remote_copy(
                    src_ref=ins[n].at[4 * px + 2 * py + pc], dst_ref=outs[n].at[r],
                    send_sem=sems[2].at[r - 1], recv_sem=sems[3].at[r - 1], device_id=(px, py, pc),
                    device_id_type=MESH))
        return cps

    def start(ins, outs, sems):
        for cp in copies(ins, outs, sems):
            cp.start()

    def finish(ins, outs, sems):
        for cp in copies(ins, outs, sems):
            cp.wait()

    shapes = [jax.ShapeDtypeStruct((3,) + t.shape[1:], t.dtype) for t in chip_sums]
    sems = [pltpu.SemaphoreType.DMA((n, 3))] * 2
    if small is not None:
        shapes.append(jax.ShapeDtypeStruct(small.shape, small.dtype))
        sems += [pltpu.SemaphoreType.DMA((7,))] * 2
    return _Side(arrs, shapes, sems, start, finish)


def _small_scatter_side(small):
    def copies(ins, outs, sems):
        x, y, c, _ = _place()
        cps = []
        for r, (fx, fy, fc) in enumerate(_relations(), start=1):
            px, py, pc = x ^ fx, y ^ fy, c ^ fc
            cps.append(pltpu.make_async_remote_copy(
                src_ref=ins[0].at[4 * px + 2 * py + pc], dst_ref=outs[0].at[r],
                send_sem=sems[0].at[r - 1], recv_sem=sems[1].at[r - 1], device_id=(px, py, pc), device_id_type=MESH))
        return cps

    def start(ins, outs, sems):
        for cp in copies(ins, outs, sems):
            cp.start()

    def finish(ins, outs, sems):
        for cp in copies(ins, outs, sems):
            cp.wait()

    return _Side([small], [jax.ShapeDtypeStruct(small.shape, small.dtype)], [pltpu.SemaphoreType.DMA((7,))] * 2,
                 start, finish)


def _share_side(halves, small=None):
    n = len(halves)
    arrs = list(halves) + ([small] if small is not None else [])

    def copies(ins, outs, sems):
        x, y, c, _ = _place()
        me = 4 * x + 2 * y + c
        sends, recvs = [], []
        for a in range(n):
            hr = halves[a].shape[0] // 2
            sends.append(pltpu.make_async_remote_copy(
                src_ref=ins[a].at[pl.ds(c * hr, hr)], dst_ref=outs[a].at[pl.ds(c * hr, hr)],
                send_sem=sems[0].at[a], recv_sem=sems[1].at[a], device_id=(x, y, 1 - c), device_id_type=MESH))
            other = outs[a].at[pl.ds((1 - c) * hr, hr)]
            recvs.append(pltpu.make_async_remote_copy(
                src_ref=other, dst_ref=other, send_sem=sems[0].at[a], recv_sem=sems[1].at[a],
                device_id=(x, y, 1 - c), device_id_type=MESH))
        if small is not None:
            for r, (fx, fy, fc) in enumerate(_relations(), start=1):
                px, py, pc = x ^ fx, y ^ fy, c ^ fc
                sends.append(pltpu.make_async_remote_copy(
                    src_ref=ins[n].at[me], dst_ref=outs[n].at[me],
                    send_sem=sems[2].at[r - 1], recv_sem=sems[3].at[r - 1], device_id=(px, py, pc),
                    device_id_type=MESH))
                theirs = outs[n].at[4 * px + 2 * py + pc]
                recvs.append(pltpu.make_async_remote_copy(
                    src_ref=theirs, dst_ref=theirs, send_sem=sems[2].at[r - 1], recv_sem=sems[3].at[r - 1],
                    device_id=(px, py, pc), device_id_type=MESH))
        return sends, recvs

    def start(ins, outs, sems):
        for cp in copies(ins, outs, sems)[0]:
            cp.start()

    def finish(ins, outs, sems):
        sends, recvs = copies(ins, outs, sems)
        for cp in recvs:
            cp.wait_recv()
        for cp in sends:
            cp.wait_send()

    sems = [pltpu.SemaphoreType.DMA((n,))] * 2 + ([pltpu.SemaphoreType.DMA((7,))] * 2 if small is not None else [])
    return _Side(arrs, [jax.ShapeDtypeStruct(h.shape, h.dtype) for h in arrs], sems, start, finish,
                 aliases={i: i for i in range(len(arrs))})


def _mm_gathering(a, shard, order, *, name, tm=1024):
    s, k = a.shape
    nc = shard.shape[1]
    tm = _row_tile(s, tm)
    tn = nc // 2
    hr = k // 2

    qr = hr // 2

    def body(order_ref, a_ref, shard_ref, z_ref, full_ref, wbuf, send_ici, recv_ici, send_relay, recv_relay,
             send_d2d, recv_d2d, local_sem, load_sem):
        t, jj, i = pl.program_id(0), pl.program_id(1), pl.program_id(2)
        x, y, c, chips = _place()
        me = 2 * x + y
        nbrs = chips[:2]
        chip_of = [2 * px + py for px, py in chips]

        def quarter(q):
            return pl.ds(c * hr + q * qr, qr)

        def whole(half):
            return pl.ds(half * hr, hr)

        def direct(j, src_chip):
            return pltpu.make_async_remote_copy(
                src_ref=shard_ref.at[whole(c)], dst_ref=full_ref.at[src_chip, whole(c)],
                send_sem=send_ici.at[j], recv_sem=recv_ici.at[j], device_id=(*nbrs[j], c), device_id_type=MESH)

        def relay(j, src_chip):
            blk = full_ref.at[src_chip, quarter(j)]
            return pltpu.make_async_remote_copy(
                src_ref=blk, dst_ref=blk, send_sem=send_relay.at[j], recv_sem=recv_relay.at[j],
                device_id=(*nbrs[1 - j], c), device_id_type=MESH)

        def d2d(j, chip, rows):
            blk = full_ref.at[chip, rows]
            return pltpu.make_async_remote_copy(
                src_ref=blk, dst_ref=blk, send_sem=send_d2d.at[j], recv_sem=recv_d2d.at[j],
                device_id=(x, y, 1 - c), device_id_type=MESH)

        def load(src):
            for h in range(2):
                pltpu.make_async_copy(src.at[:, pl.ds(h * tn, tn)], wbuf.at[h], load_sem.at[h]).start()
            for h in range(2):
                pltpu.make_async_copy(src.at[:, pl.ds(h * tn, tn)], wbuf.at[h], load_sem.at[h]).wait()

        def sibling_quarter(q):
            return pl.ds((1 - c) * hr + q * qr, qr)

        local = pltpu.make_async_copy(shard_ref, full_ref.at[me], local_sem)
        new_shard = jnp.logical_and(jj == 0, i == 0)

        @pl.when(jnp.logical_and(new_shard, t == 0))
        def _():
            local.start()
            for j in range(2):
                direct(j, me).start()
            load(shard_ref)

        for j in range(2):
            @pl.when(jnp.logical_and(new_shard, t == j + 1))
            def _(j=j):
                direct(j, chip_of[j]).wait_recv()
                relay(j, chip_of[j]).start()
                d2d(j, chip_of[j], whole(c)).start()
                d2d(j, chip_of[j], whole(1 - c)).wait_recv()
                load(full_ref.at[chip_of[j]])

        @pl.when(jnp.logical_and(new_shard, t == 3))
        def _():
            for j in range(2):
                relay(1 - j, chip_of[2]).wait_recv()
                d2d(2 + j, chip_of[2], quarter(1 - j)).start()
            for j in range(2):
                d2d(2 + j, chip_of[2], sibling_quarter(1 - j)).wait_recv()
            load(full_ref.at[chip_of[2]])

        z_ref[...] = jnp.dot(a_ref[...], wbuf[jj], preferred_element_type=F32).astype(z_ref.dtype)

        last = functools.reduce(jnp.logical_and, [t == N_CHIPS - 1, jj == 1, i == s // tm - 1])

        @pl.when(last)
        def _():
            for j in range(2):
                direct(j, me).wait_send()
                relay(j, chip_of[j]).wait_send()
                d2d(j, chip_of[j], whole(c)).wait_send()
                d2d(2 + j, chip_of[2], quarter(1 - j)).wait_send()
            local.wait()

    return pl.pallas_call(
        body,
        grid_spec=pltpu.PrefetchScalarGridSpec(
            num_scalar_prefetch=1, grid=(N_CHIPS, 2, s // tm),
            in_specs=[pl.BlockSpec((tm, k), lambda t, jj, i, order: (i, 0)), HBM],
            out_specs=[pl.BlockSpec((tm, tn), lambda t, jj, i, order: (i, order[t] * 2 + jj)), HBM],
            scratch_shapes=[pltpu.VMEM((2, k, tn), BF16)] + [pltpu.SemaphoreType.DMA((2,))] * 4
            + [pltpu.SemaphoreType.DMA((4,))] * 2 + [pltpu.SemaphoreType.DMA, pltpu.SemaphoreType.DMA((2,))]),
        out_shape=[jax.ShapeDtypeStruct((s, N_CHIPS * nc), BF16), jax.ShapeDtypeStruct((N_CHIPS, k, nc), BF16)],
        name=name, compiler_params=_cparams(),
    )(order, a, shard)


def _col_tile(cols):
    return cols if cols <= 2048 else 512


def _add_sibling(grad, recv, core, *, name):
    k, r, c = grad.shape
    hr = r // 2
    tr = min(hr, 256)
    tc = _col_tile(c)
    nrb = hr // tr

    def body(core_ref, g_ref, r_ref, o_ref):
        o_ref[...] = (g_ref[...] + r_ref[...]).astype(BF16)

    return pl.pallas_call(
        body,
        grid_spec=pltpu.PrefetchScalarGridSpec(
            num_scalar_prefetch=1, grid=(k, nrb, c // tc),
            in_specs=[pl.BlockSpec((None, tr, tc), lambda kk, i, j, core: (kk, core[0] * nrb + i, j)),
                      pl.BlockSpec((None, tr, tc), lambda kk, i, j, core: (kk, i, j))],
            out_specs=pl.BlockSpec((None, tr, tc), lambda kk, i, j, core: (kk, i, j))),
        out_shape=jax.ShapeDtypeStruct((k, hr, c), BF16), name=name, compiler_params=_cparams(),
    )(core, grad, recv)


def _sum_chips(grad, from_sibling, recv, place, *, name):
    _, hr, c = from_sibling.shape
    tr = min(hr, 256)
    tc = _col_tile(c)
    nrb = hr // tr

    def body(place_ref, g_ref, s_ref, r0_ref, r1_ref, r2_ref, o_ref):
        own = g_ref[...] + s_ref[...]
        o_ref[...] = ((own + r0_ref[...].astype(F32)) + r1_ref[...].astype(F32)) + r2_ref[...].astype(F32)

    def rspec(j):
        return pl.BlockSpec((None, tr, tc), lambda i, jj, place: (j, i, jj))

    return pl.pallas_call(
        body,
        grid_spec=pltpu.PrefetchScalarGridSpec(
            num_scalar_prefetch=1, grid=(nrb, c // tc),
            in_specs=[pl.BlockSpec((None, tr, tc), lambda i, jj, place: (place[0], place[1] * nrb + i, jj)),
                      pl.BlockSpec((None, tr, tc), lambda i, jj, place: (place[0], i, jj)),
                      rspec(0), rspec(1), rspec(2)],
            out_specs=pl.BlockSpec((tr, tc), lambda i, jj, place: (place[1] * nrb + i, jj))),
        out_shape=jax.ShapeDtypeStruct((2 * hr, c), F32), name=name, compiler_params=_cparams(),
    )(place, grad, from_sibling, recv, recv, recv)


def _sum_small(small, recv, place):
    _, sr, _ = small.shape

    def body(place_ref, own_ref, r_ref, o_ref):
        acc = own_ref[...]
        for r in range(1, 8):
            acc = acc + r_ref[r]
        o_ref[...] = acc

    return pl.pallas_call(
        body,
        grid_spec=pltpu.PrefetchScalarGridSpec(
            num_scalar_prefetch=1, grid=(1,),
            in_specs=[pl.BlockSpec((None, sr, 128), lambda i, place: (place[2], 0, 0)),
                      pl.BlockSpec((8, sr, 128), lambda i, place: (0, 0, 0))],
            out_specs=pl.BlockSpec((None, sr, 128), lambda i, place: (place[2], 0, 0))),
        out_shape=jax.ShapeDtypeStruct(small.shape, F32), name="sum_small", compiler_params=_cparams(),
    )(place, small, recv)


def _spread_side(vec):
    def copies(ins, outs, sems):
        x, y, c, _ = _place()
        return [pltpu.make_async_remote_copy(
            src_ref=ins[0], dst_ref=outs[0].at[r], send_sem=sems[0].at[r - 1], recv_sem=sems[1].at[r - 1],
            device_id=(x ^ fx, y ^ fy, c ^ fc), device_id_type=MESH)
            for r, (fx, fy, fc) in enumerate(_relations(), start=1)]

    def start(ins, outs, sems):
        for cp in copies(ins, outs, sems):
            cp.start()

    def finish(ins, outs, sems):
        for cp in copies(ins, outs, sems):
            cp.wait()

    return _Side([vec], [jax.ShapeDtypeStruct((8,) + vec.shape, vec.dtype)], [pltpu.SemaphoreType.DMA((7,))] * 2,
                 start, finish)


def _sum_in_device_order(own, spread, place):
    def body(place_ref, own_ref, r_ref, o_ref):
        me = place_ref[2]
        acc = jnp.zeros_like(own_ref[...])
        for d in range(8):
            slot = jnp.where(me == d, 1, me ^ d)
            acc = acc + jnp.where(me == d, own_ref[...], r_ref[slot])
        o_ref[...] = acc

    return pl.pallas_call(
        body,
        grid_spec=pltpu.PrefetchScalarGridSpec(
            num_scalar_prefetch=1, grid=(1,),
            in_specs=[pl.BlockSpec(own.shape, lambda i, place: (0, 0)),
                      pl.BlockSpec(spread.shape, lambda i, place: (0, 0, 0))],
            out_specs=pl.BlockSpec(own.shape, lambda i, place: (0, 0))),
        out_shape=jax.ShapeDtypeStruct(own.shape, F32), name="sum_in_device_order", compiler_params=_cparams(),
    )(place, own, spread)


def _adamw(w, g, m, v, *, name):
    r, c = w.shape
    tr = 256 if r % 256 == 0 else r
    tc = _col_tile(c)
    bc1 = 1.0 - ADAM_B1 ** ADAM_STEP
    bc2 = 1.0 - ADAM_B2 ** ADAM_STEP

    def body(w_ref, g_ref, m_ref, v_ref, d_ref, nm_ref, nv_ref):
        gv = g_ref[...]
        nm = ADAM_B1 * m_ref[...] + (1.0 - ADAM_B1) * gv
        nv = ADAM_B2 * v_ref[...] + (1.0 - ADAM_B2) * (gv * gv)
        d_ref[...] = -ADAM_LR * ((nm / bc1) / (jnp.sqrt(nv / bc2) + ADAM_EPS) + ADAM_WD * w_ref[...])
        nm_ref[...] = nm
        nv_ref[...] = nv

    spec = pl.BlockSpec((tr, tc), lambda i, j: (i, j))
    outs, _ = _call(body, grid=(r // tr, c // tc), in_specs=[spec] * 4, out_specs=[spec] * 3,
                    out_shape=[jax.ShapeDtypeStruct((r, c), F32)] * 3, args=(w, g, m, v), name=name)
    return outs


SMALL_ORDER = ["a_ws", "a_bs", "a_norm_g", "a_ln_g", "a_ln_b", "kv_norm_g", "b_kv", "b_norm_g", "b_bq",
               "b_sinks", "final_norm_g"]
SHARDED_SMALL = {"a_norm_g", "a_ln_g", "a_ln_b"}
PACK_TILE = 8 * 128


def _rows128(a):
    flat = a.reshape(-1)
    return jnp.pad(flat, (0, (-flat.shape[0]) % PACK_TILE)).reshape(-1, 128)


def _pack_rows(parts, multiple):
    rows = [_rows128(p) for p in parts]
    total = sum(r.shape[0] for r in rows)
    pad = (-total) % multiple
    if pad:
        rows.append(jnp.zeros((pad, 128), rows[0].dtype))
    return jnp.concatenate(rows, axis=0)


def _unpack_rows(packed, shapes):
    out, row = [], 0
    for shp in shapes:
        size = math.prod(shp)
        nrow = -(-size // PACK_TILE) * 8
        out.append(packed[row:row + nrow].reshape(-1)[:size].reshape(shp))
        row += nrow
    return out


WEIGHTS = ["a_norm_g", "a_w_in", "a_ln_g", "a_ln_b", "a_ws", "a_bs", "a_w_out", "kv_norm_g", "w_kv", "b_kv",
           "b_norm_g", "b_w_in", "b_bq", "b_sinks", "b_w_out", "final_norm_g"]
BIG = ["a_w_in", "a_w_out", "w_kv", "b_w_in", "b_w_out"]


class _Reduction:
    def __init__(self, names, partials, core, place, small=None):
        self.names, self.partials, self.core, self.place, self.small = names, partials, core, place, small

    def exchange_side(self):
        return _exchange_side(self.partials)

    def took_exchange(self, from_sibling):
        self.from_sibling = from_sibling
        self.chip_sums = [_add_sibling(g, r, self.core, name="add_sibling_" + n)
                          for g, r, n in zip(self.partials, from_sibling, self.names)]

    def scatter_side(self):
        return _scatter_side(self.chip_sums, self.small)

    def took_scatter(self, arrived):
        big = arrived[:len(self.names)]
        self.halves = [_sum_chips(g, fs, r, self.place, name="sum_chips_" + n)
                       for g, fs, r, n in zip(self.partials, self.from_sibling, big, self.names)]
        self.small_mine = _sum_small(self.small, arrived[-1], self.place) if self.small is not None else None

    def share_side(self):
        return _share_side(self.halves, self.small_mine)

    def took_share(self, shared):
        self.grads = dict(zip(self.names, shared[:len(self.names)]))
        self.small_full = shared[-1] if self.small is not None else None


def _step(x, loss_target, p, m, v):
    xi, yi, ci = lax.axis_index("x"), lax.axis_index("y"), lax.axis_index("c")
    chip = 2 * xi + yi
    device = 4 * xi + 2 * yi + ci
    core = jnp.reshape(ci, (1,)).astype(jnp.int32)
    place = jnp.stack([chip, ci, device]).astype(jnp.int32)
    x, tgt = x[0], loss_target[0]
    s = x.shape[0]
    cos, sin = _rope_tables(s)

    shard2d = {n: p[n].reshape(p[n].shape[-2:]) for n in BIG}
    shard_bf = {n: shard2d[n].astype(BF16) for n in BIG}
    ws = p["a_ws"][0]
    ws_t = jnp.swapaxes(ws, 1, 2)
    bs_t = p["a_bs"][0].T
    kv_norm_g, b_kv = p["kv_norm_g"].reshape(1, -1), p["b_kv"].reshape(1, -1)
    final_norm_g = p["final_norm_g"].reshape(1, -1)

    vec_shapes = [p[n].shape for n in ("a_norm_g", "a_ln_g", "a_ln_b")]
    vec_pack = _pack_rows([p["a_norm_g"], p["a_ln_g"], p["a_ln_b"]], 16)
    (vec_all,) = _comm_call(_gather_side([vec_pack]), "gather_vectors")
    vecs = [_unpack_rows(vec_all[k], vec_shapes) for k in range(N_CHIPS)]
    a_norm_g, a_ln_g, a_ln_b = (jnp.concatenate([vk[t] for vk in vecs], axis=-1) for t in range(3))

    (n_a,) = _rms_fwd(x, [a_norm_g], name="rms_a")
    order = jnp.stack([chip, 2 * (1 - xi) + yi, 2 * xi + (1 - yi), 2 * (1 - xi) + (1 - yi)]).astype(jnp.int32)
    z, a_w_in = _mm_gathering(n_a, shard_bf["a_w_in"], order, name="mm_a_in")
    y, (a_w_out,) = _gate_fwd(z, a_ln_g, a_ln_b, ws, bs_t, side=_gather_side([shard_bf["a_w_out"]]))
    a_w_out = a_w_out.reshape(A_WIDTH, D_MODEL)
    (h1, n_kv, n_b), (w_kv, b_w_in) = _mm_residual_norms(
        y, a_w_out, x, [kv_norm_g, p["b_norm_g"]], name="mm_a_out",
        side=_gather_side([shard_bf["w_kv"], shard_bf["b_w_in"]]))
    w_kv = w_kv.reshape(D_MODEL, 2 * KV_WIDTH)
    kv = _mm_nn(n_kv, w_kv, name="mm_kv", tn=2 * KV_WIDTH)
    kr, vv = _kv_rope(kv, b_kv, cos, sin)
    zb = _mm_nn(n_b, b_w_in, name="mm_b_in", tn=512, tm=1024, out_dtype=BF16)
    yb, (b_w_out,) = _attn_fwd(zb, kr, vv, cos, sin, p["b_bq"], p["b_sinks"], side=_gather_side([shard_bf["b_w_out"]]))
    b_w_out = b_w_out.reshape(B_WIDTH, D_MODEL)
    loss_blk, dh2, dh2b, d_final_g = _mm_residual_loss(yb, b_w_out, h1, tgt, final_norm_g, name="mm_b_out")

    d_b_w_out = _mm_tn(yb, dh2b, name="mm_d_b_w_out", tm=B_WIDTH, tn=D_MODEL)
    red_bo = _Reduction(["b_w_out"], [d_b_w_out.reshape(N_CHIPS, B_WIDTH // N_CHIPS, D_MODEL)], core, place)
    dyb, got = _mm_nt(dh2b, b_w_out, name="mm_dyb", out_dtype=BF16, side=red_bo.exchange_side())
    red_bo.took_exchange(got)
    dzb, dk_rot, dv, d_bq, d_sinks = _attn_bwd(zb, dyb, kr, vv, cos, sin, p["b_bq"], p["b_sinks"])
    dkv, d_b_kv = _kv_rope_bwd(dk_rot, dv, cos, sin)
    d_b_w_in, got = _mm_tn(n_b, dzb, name="mm_d_b_w_in", tm=D_MODEL, tn=512, shards=N_CHIPS,
                           side=red_bo.scatter_side())
    red_bo.took_scatter(got)
    d_w_kv, got = _mm_tn(n_kv, dkv, name="mm_d_w_kv", tm=D_MODEL, tn=2 * KV_WIDTH, side=red_bo.share_side())
    red_bo.took_share(got)
    red_bi = _Reduction(["b_w_in", "w_kv"], [d_b_w_in, d_w_kv.reshape(N_CHIPS, D_MODEL // N_CHIPS, 2 * KV_WIDTH)],
                        core, place)
    (dh1, dh1b, d_kv_g, d_b_g), got = _mm_nt_rms_bwd(
        [(dkv, w_kv, kv_norm_g), (dzb, b_w_in, p["b_norm_g"])], h1, dh2, name="mm_dn_b", tm=512,
        side=red_bi.exchange_side())
    red_bi.took_exchange(got)

    d_a_w_out, got = _mm_tn(y, dh1b, name="mm_d_a_w_out", tm=1024, tn=D_MODEL, side=red_bi.scatter_side())
    red_bi.took_scatter(got)
    red_ao = _Reduction(["a_w_out"], [d_a_w_out.reshape(N_CHIPS, A_WIDTH // N_CHIPS, D_MODEL)], core, place)
    sides = [red_ao.exchange_side(), red_bi.share_side()]
    dy, got = _mm_nt(dh1b, a_w_out, name="mm_dy", tn=1024, out_dtype=BF16, side=_join(sides))
    got = _split(got, sides)
    red_ao.took_exchange(got[0])
    red_bi.took_share(got[1])
    (dz, d_ln_g, d_ln_b, d_ws, d_bs_t), got = _gate_bwd(z, dy, a_ln_g, a_ln_b, ws, ws_t, bs_t,
                                                        side=red_ao.scatter_side())
    red_ao.took_scatter(got)
    small = {
        "a_ws": d_ws, "a_bs": d_bs_t.T, "a_ln_g": d_ln_g, "a_ln_b": d_ln_b,
        "kv_norm_g": d_kv_g, "b_kv": d_b_kv, "b_norm_g": d_b_g, "b_bq": d_bq,
        "b_sinks": d_sinks[0:1, :N_Q_HEADS], "final_norm_g": d_final_g,
    }
    packed = [n for n in SMALL_ORDER if n != "a_norm_g"]
    small_shapes = [small[n].shape for n in packed] + [(1, 1)]
    small_pack = _pack_rows([small[n] for n in packed] + [loss_blk[0:1, 0:1]], 64)
    seg = small_pack.shape[0] // 8
    small_pack = small_pack.reshape(8, seg, 128)
    sides = [red_ao.share_side(), _small_scatter_side(small_pack)]
    d_a_w_in, got = _mm_tn(n_a, dz, name="mm_d_a_w_in", tm=D_MODEL, tn=1536, shards=N_CHIPS, side=_join(sides))
    got = _split(got, sides)
    red_ao.took_share(got[0])
    small_mine = _sum_small(small_pack, got[1][0], place)

    red_ai = _Reduction(["a_w_in"], [d_a_w_in], core, place)
    red_ai.took_exchange(_comm_call(red_ai.exchange_side(), "exchange_last"))
    (dx, _, d_a_g), got = _mm_nt_rms_bwd([(dz, a_w_in, a_norm_g)], x, dh1, name="mm_dn_a", tm=256,
                                         side=red_ai.scatter_side())
    red_ai.took_scatter(got)
    red_ai.small, red_ai.small_mine = small_pack, small_mine
    d_a_g = _rows128(d_a_g)
    sides = [red_ai.share_side(), _spread_side(d_a_g)]
    got = _split(_comm_call(_join(sides), "share_last"), sides)
    red_ai.took_share(got[0])
    small_full = dict(zip(packed + ["loss"], _unpack_rows(red_ai.small_full.reshape(8 * seg, 128), small_shapes)))
    small_full["a_norm_g"] = _sum_in_device_order(d_a_g, got[1][0], place).reshape(1, -1)
    loss = small_full["loss"].reshape(())

    grad_big = {**red_bo.grads, **red_bi.grads, **red_ao.grads, **red_ai.grads}
    grads = {}
    for n in SMALL_ORDER:
        gfull = small_full[n]
        if n in SHARDED_SMALL:
            width = p[n].shape[-1]
            gfull = lax.dynamic_slice_in_dim(gfull, chip * width, width, axis=-1)
        grads[n] = gfull.reshape(p[n].shape)
    for n in BIG:
        grads[n] = grad_big[n].reshape(p[n].shape)

    delta, new_m, new_v = {}, {}, {}
    for n in BIG:
        d, nm, nv = _adamw(shard2d[n], grad_big[n], m[n].reshape(shard2d[n].shape), v[n].reshape(shard2d[n].shape),
                           name="adamw_" + n)
        delta[n], new_m[n], new_v[n] = d.reshape(p[n].shape), nm.reshape(p[n].shape), nv.reshape(p[n].shape)
    shapes = [p[n].shape for n in SMALL_ORDER]
    packs = [_pack_rows([src[n] for n in SMALL_ORDER], 8) for src in (p, grads, m, v)]
    outs = _adamw(*packs, name="adamw_small")
    for res, packed in zip((delta, new_m, new_v), outs):
        for n, val in zip(SMALL_ORDER, _unpack_rows(packed, shapes)):
            res[n] = val

    return (loss, dx[None], *[grads[n] for n in WEIGHTS], *[delta[n] for n in WEIGHTS],
            *[new_m[n] for n in WEIGHTS], *[new_v[n] for n in WEIGHTS])


def kernel(x, a_norm_g, a_w_in, a_ln_g, a_ln_b, a_ws, a_bs, a_w_out, kv_norm_g, w_kv, b_kv, b_norm_g, b_w_in, b_bq, b_sinks, b_w_out, final_norm_g, loss_target, m_a_norm_g, m_a_w_in, m_a_ln_g, m_a_ln_b, m_a_ws, m_a_bs, m_a_w_out, m_kv_norm_g, m_w_kv, m_b_kv, m_b_norm_g, m_b_w_in, m_b_bq, m_b_sinks, m_b_w_out, m_final_norm_g, v_a_norm_g, v_a_w_in, v_a_ln_g, v_a_ln_b, v_a_ws, v_a_bs, v_a_w_out, v_kv_norm_g, v_w_kv, v_b_kv, v_b_norm_g, v_b_w_in, v_b_bq, v_b_sinks, v_b_w_out, v_final_norm_g):
    p = dict(a_norm_g=a_norm_g, a_w_in=a_w_in, a_ln_g=a_ln_g, a_ln_b=a_ln_b, a_ws=a_ws, a_bs=a_bs, a_w_out=a_w_out,
             kv_norm_g=kv_norm_g, w_kv=w_kv, b_kv=b_kv, b_norm_g=b_norm_g, b_w_in=b_w_in, b_bq=b_bq, b_sinks=b_sinks,
             b_w_out=b_w_out, final_norm_g=final_norm_g)
    m = dict(a_norm_g=m_a_norm_g, a_w_in=m_a_w_in, a_ln_g=m_a_ln_g, a_ln_b=m_a_ln_b, a_ws=m_a_ws, a_bs=m_a_bs,
             a_w_out=m_a_w_out, kv_norm_g=m_kv_norm_g, w_kv=m_w_kv, b_kv=m_b_kv, b_norm_g=m_b_norm_g, b_w_in=m_b_w_in,
             b_bq=m_b_bq, b_sinks=m_b_sinks, b_w_out=m_b_w_out, final_norm_g=m_final_norm_g)
    v = dict(a_norm_g=v_a_norm_g, a_w_in=v_a_w_in, a_ln_g=v_a_ln_g, a_ln_b=v_a_ln_b, a_ws=v_a_ws, a_bs=v_a_bs,
             a_w_out=v_a_w_out, kv_norm_g=v_kv_norm_g, w_kv=v_w_kv, b_kv=v_b_kv, b_norm_g=v_b_norm_g, b_w_in=v_b_w_in,
             b_bq=v_b_bq, b_sinks=v_b_sinks, b_w_out=v_b_w_out, final_norm_g=v_final_norm_g)
    return _step(x, loss_target, p, m, v)
```

```python
import functools
import math

import jax
import jax.numpy as jnp
from jax import lax
from jax.experimental import pallas as pl
from jax.experimental.pallas import tpu as pltpu

F32 = jnp.float32
BF16 = jnp.bfloat16

D_MODEL = 1024
CHUNK = 128
A_WIDTH = 2048
A_GROUPS = 16
HEAD_DIM = 64
N_Q_HEADS = 16
N_KV_HEADS = 2
Q_PER_KV = 8
B_WIDTH = 1024
KV_WIDTH = 128
ROPE_THETA = 10000.0
EPS = 1e-5
N_CHIPS = 4

ADAM_LR = 0.001
ADAM_B1 = 0.9
ADAM_B2 = 0.999
ADAM_EPS = 1e-08
ADAM_WD = 0.01
ADAM_STEP = 10

VMEM_LIMIT = 48 * 1024 * 1024
MESH = pl.DeviceIdType.MESH
NEG_BIG = -1e30
HBM = pl.BlockSpec(memory_space=pl.ANY)

NN = (((1,), (0,)), ((), ()))
NT = (((1,), (1,)), ((), ()))
TN = (((0,), (0,)), ((), ()))


def _cparams(**kw):
    return pltpu.CompilerParams(vmem_limit_bytes=VMEM_LIMIT, **kw)


class _Side:
    def __init__(self, ins, out_shapes, sems, start, finish, aliases=None):
        self.ins, self.out_shapes, self.sems = list(ins), list(out_shapes), list(sems)
        self.start, self.finish = start, finish
        self.aliases = dict(aliases or {})


def _join(sides):
    sides = [s for s in sides if s is not None]
    if not sides:
        return None
    offs, i, o, m = [], 0, 0, 0
    for s in sides:
        offs.append((i, o, m))
        i, o, m = i + len(s.ins), o + len(s.out_shapes), m + len(s.sems)

    def run(which):
        def go(ins, outs, sems):
            for s, (a, b, c) in zip(sides, offs):
                getattr(s, which)(ins[a:a + len(s.ins)], outs[b:b + len(s.out_shapes)], sems[c:c + len(s.sems)])
        return go

    aliases = {}
    for s, (a, b, _) in zip(sides, offs):
        aliases.update({a + k: b + v for k, v in s.aliases.items()})
    return _Side([x for s in sides for x in s.ins], [x for s in sides for x in s.out_shapes],
                 [x for s in sides for x in s.sems], run("start"), run("finish"), aliases)


def _split(side_outs, sides):
    out, pos = [], 0
    for s in sides:
        out.append(list(side_outs[pos:pos + len(s.out_shapes)]))
        pos += len(s.out_shapes)
    return out


def _call(body, *, grid, in_specs, out_specs, out_shape, args, name, scratch=(), side=None):
    in_specs, out_specs, out_shape, scratch = list(in_specs), list(out_specs), list(out_shape), list(scratch)
    if side is None:
        res = pl.pallas_call(body, grid=grid, in_specs=in_specs, out_specs=out_specs, out_shape=out_shape,
                             scratch_shapes=scratch, name=name, compiler_params=_cparams())(*args)
        return list(res), []
    n_in, n_out, n_sc = len(in_specs), len(out_specs), len(scratch)
    s_in, s_out = len(side.ins), len(side.out_shapes)

    def wrapped(*refs):
        ins, refs = refs[:n_in], refs[n_in:]
        side_ins, refs = refs[:s_in], refs[s_in:]
        outs, refs = refs[:n_out], refs[n_out:]
        side_outs, refs = refs[:s_out], refs[s_out:]
        scr, side_sems = refs[:n_sc], refs[n_sc:]
        ids = [pl.program_id(a) for a in range(len(grid))]
        first = functools.reduce(jnp.logical_and, [i == 0 for i in ids])
        last = functools.reduce(jnp.logical_and, [i == g - 1 for i, g in zip(ids, grid)])

        @pl.when(first)
        def _():
            side.start(side_ins, side_outs, side_sems)

        body(*ins, *outs, *scr)

        @pl.when(last)
        def _():
            side.finish(side_ins, side_outs, side_sems)

    res = pl.pallas_call(
        wrapped, grid=grid, in_specs=in_specs + [HBM] * s_in, out_specs=out_specs + [HBM] * s_out,
        out_shape=out_shape + side.out_shapes, scratch_shapes=scratch + side.sems,
        input_output_aliases={n_in + k: n_out + v for k, v in side.aliases.items()},
        name=name, compiler_params=_cparams(),
    )(*args, *side.ins)
    return list(res[:n_out]), list(res[n_out:])


def _comm_call(side, name):
    s_in, s_out = len(side.ins), len(side.out_shapes)

    def body(*refs):
        ins, outs, sems = refs[:s_in], refs[s_in:s_in + s_out], refs[s_in + s_out:]
        side.start(ins, outs, sems)
        side.finish(ins, outs, sems)

    return list(pl.pallas_call(
        body, in_specs=[HBM] * s_in, out_specs=[HBM] * s_out, out_shape=side.out_shapes, scratch_shapes=side.sems,
        input_output_aliases=side.aliases, name=name,
    )(*side.ins))


def _matmul(a, b, *, dims, grid, a_spec, b_spec, o_spec, out_shape, name, acc_axis=None,
            residual=None, r_spec=None, side=None):
    has_res = residual is not None

    def body(*refs):
        if has_res:
            a_ref, b_ref, r_ref, o_ref = refs
        else:
            a_ref, b_ref, o_ref = refs
        part = lax.dot_general(a_ref[...], b_ref[...], dims, preferred_element_type=F32)
        if acc_axis is None:
            if has_res:
                part = part + r_ref[...]
            o_ref[...] = part.astype(o_ref.dtype)
        else:
            k = pl.program_id(acc_axis)

            @pl.when(k == 0)
            def _():
                o_ref[...] = part

            @pl.when(k > 0)
            def _():
                o_ref[...] += part

    in_specs = [a_spec, b_spec] + ([r_spec] if has_res else [])
    args = (a, b) + ((residual,) if has_res else ())
    (out,), side_outs = _call(body, grid=grid, in_specs=in_specs, out_specs=[o_spec], out_shape=[out_shape],
                              args=args, name=name, side=side)
    return (out, side_outs) if side is not None else out


def _row_tile(s, want):
    return min(s, want)


def _mm_nn(a, b, *, name, tn, out_dtype=F32, residual=None, tm=512, side=None):
    s, k = a.shape
    tm = _row_tile(s, tm)
    if b.ndim == 3:
        nsh, _, nc = b.shape
        npb = nc // tn
        n = nsh * nc
        b_spec = pl.BlockSpec((None, k, tn), lambda i, j: (j // npb, 0, j % npb))
    else:
        n = b.shape[1]
        b_spec = pl.BlockSpec((k, tn), lambda i, j: (0, j))
    return _matmul(
        a, b, dims=NN, grid=(s // tm, n // tn),
        a_spec=pl.BlockSpec((tm, k), lambda i, j: (i, 0)), b_spec=b_spec,
        o_spec=pl.BlockSpec((tm, tn), lambda i, j: (i, j)),
        out_shape=jax.ShapeDtypeStruct((s, n), out_dtype), name=name, side=side,
        residual=residual, r_spec=pl.BlockSpec((tm, tn), lambda i, j: (i, j)) if residual is not None else None)


def _mm_nt(a, b, *, name, tn=None, tm=512, out_dtype=F32, side=None):
    s, k = a.shape
    tm = _row_tile(s, tm)
    n = b.shape[0]
    tn = n if tn is None else tn
    return _matmul(
        a, b, dims=NT, grid=(s // tm, n // tn),
        a_spec=pl.BlockSpec((tm, k), lambda i, j: (i, 0)),
        b_spec=pl.BlockSpec((tn, k), lambda i, j: (j, 0)),
        o_spec=pl.BlockSpec((tm, tn), lambda i, j: (i, j)),
        out_shape=jax.ShapeDtypeStruct((s, n), out_dtype), name=name, side=side)


def _mm_tn(a, b, *, name, tm, tn, tk=2048, shards=None, side=None):
    s, m = a.shape
    n = b.shape[1]
    tk = _row_tile(s, tk)
    if shards is None:
        o_spec = pl.BlockSpec((tm, tn), lambda i, j, kk: (i, j))
        out_shape = jax.ShapeDtypeStruct((m, n), F32)
    else:
        assert tm == m
        nc = n // shards
        npb = nc // tn
        o_spec = pl.BlockSpec((None, m, tn), lambda i, j, kk: (j // npb, 0, j % npb))
        out_shape = jax.ShapeDtypeStruct((shards, m, nc), F32)
    return _matmul(
        a, b, dims=TN, grid=(m // tm, n // tn, s // tk), acc_axis=2,
        a_spec=pl.BlockSpec((tk, tm), lambda i, j, kk: (kk, i)),
        b_spec=pl.BlockSpec((tk, tn), lambda i, j, kk: (kk, j)),
        o_spec=o_spec, out_shape=out_shape, name=name, side=side)


def _rstd(x):
    return lax.rsqrt(jnp.mean(x * x, axis=-1, keepdims=True) + EPS)


def _rms_fwd(x, gains, *, name, tr=256):
    s, d = x.shape
    tr = _row_tile(s, tr)
    ng = len(gains)

    def body(*refs):
        xv = refs[0][...]
        xh = xv * _rstd(xv)
        for t in range(ng):
            refs[1 + ng + t][...] = (xh * refs[1 + t][...]).astype(BF16)

    row = pl.BlockSpec((tr, d), lambda i: (i, 0))
    vec = pl.BlockSpec((1, d), lambda i: (0, 0))
    outs, _ = _call(body, grid=(s // tr,), in_specs=[row] + [vec] * ng, out_specs=[row] * ng,
                    out_shape=[jax.ShapeDtypeStruct((s, d), BF16)] * ng, args=(x, *gains), name=name)
    return outs


def _accumulate(i, ref, value):
    @pl.when(i == 0)
    def _():
        ref[...] = value

    @pl.when(i > 0)
    def _():
        ref[...] += value


def _mm_residual_norms(y, w, res, gains, *, name, tm=512, side=None):
    s, k = y.shape
    d = w.shape[1]
    tm = _row_tile(s, tm)
    ng = len(gains)

    def body(y_ref, w_ref, r_ref, *rest):
        g_refs, h_ref, n_refs = rest[:ng], rest[ng], rest[ng + 1:]
        h = r_ref[...] + jnp.dot(y_ref[...], w_ref[...], preferred_element_type=F32)
        h_ref[...] = h
        xh = h * _rstd(h)
        for t in range(ng):
            n_refs[t][...] = (xh * g_refs[t][...]).astype(BF16)

    row = pl.BlockSpec((tm, d), lambda i: (i, 0))
    vec = pl.BlockSpec((1, d), lambda i: (0, 0))
    return _call(
        body, grid=(s // tm,),
        in_specs=[pl.BlockSpec((tm, k), lambda i: (i, 0)), pl.BlockSpec((k, d), lambda i: (0, 0)), row] + [vec] * ng,
        out_specs=[row] * (1 + ng),
        out_shape=[jax.ShapeDtypeStruct((s, d), F32)] + [jax.ShapeDtypeStruct((s, d), BF16)] * ng,
        args=(y, w, res, *gains), name=name, side=side)


def _mm_residual_loss(y, w, res, tgt, gain, *, name, tm=512):
    s, k = y.shape
    d = w.shape[1]
    tm = _row_tile(s, tm)

    def body(y_ref, w_ref, r_ref, t_ref, g_ref, loss_ref, dh_ref, dhb_ref, dg_ref):
        i = pl.program_id(0)
        hv = r_ref[...] + jnp.dot(y_ref[...], w_ref[...], preferred_element_type=F32)
        g = g_ref[...]
        r = _rstd(hv)
        xh = hv * r
        diff = xh * g - t_ref[...]
        part = 0.5 / d * jnp.sum(jnp.sum(diff * diff, axis=-1, keepdims=True), axis=0, keepdims=True)
        dout = diff * (1.0 / d)
        a = dout * g
        dh = r * (a - xh * jnp.mean(a * xh, axis=-1, keepdims=True))
        dh_ref[...] = dh
        dhb_ref[...] = dh.astype(BF16)
        _accumulate(i, dg_ref, jnp.sum(dout * xh, axis=0, keepdims=True))
        _accumulate(i, loss_ref, jnp.broadcast_to(part, (8, 128)))

    row = pl.BlockSpec((tm, d), lambda i: (i, 0))
    vec = pl.BlockSpec((1, d), lambda i: (0, 0))
    outs, _ = _call(
        body, grid=(s // tm,),
        in_specs=[pl.BlockSpec((tm, k), lambda i: (i, 0)), pl.BlockSpec((k, d), lambda i: (0, 0)), row, row, vec],
        out_specs=[pl.BlockSpec((8, 128), lambda i: (0, 0)), row, row, vec],
        out_shape=[jax.ShapeDtypeStruct((8, 128), F32), jax.ShapeDtypeStruct((s, d), F32),
                   jax.ShapeDtypeStruct((s, d), BF16), jax.ShapeDtypeStruct((1, d), F32)],
        args=(y, w, res, tgt, gain), name=name)
    return outs


def _mm_nt_rms_bwd(terms, x, dres, *, name, tm, side=None):
    s, d = x.shape
    tm = _row_tile(s, tm)
    nt = len(terms)

    def body(*refs):
        a_refs, b_refs, g_refs = refs[0:3 * nt:3], refs[1:3 * nt:3], refs[2:3 * nt:3]
        x_ref, dres_ref = refs[3 * nt], refs[3 * nt + 1]
        dx_ref, dxb_ref = refs[3 * nt + 2], refs[3 * nt + 3]
        dg_refs = refs[3 * nt + 4:]
        i = pl.program_id(0)
        xv = x_ref[...]
        r = _rstd(xv)
        xh = xv * r
        acc = jnp.zeros_like(xv)
        for t in range(nt):
            b_ref = b_refs[t]
            if len(b_ref.shape) == 3:
                kc = b_ref.shape[2]
                dn = None
                for sh in range(b_ref.shape[0]):
                    part = lax.dot_general(a_refs[t][:, sh * kc:(sh + 1) * kc], b_ref[sh], NT, preferred_element_type=F32)
                    dn = part if dn is None else dn + part
            else:
                dn = lax.dot_general(a_refs[t][...], b_ref[...], NT, preferred_element_type=F32)
            acc = acc + dn * g_refs[t][...]
            _accumulate(i, dg_refs[t], jnp.sum(dn * xh, axis=0, keepdims=True))
        dx = dres_ref[...] + r * (acc - xh * jnp.mean(acc * xh, axis=-1, keepdims=True))
        dx_ref[...] = dx
        dxb_ref[...] = dx.astype(BF16)

    row = pl.BlockSpec((tm, d), lambda i: (i, 0))
    vec = pl.BlockSpec((1, d), lambda i: (0, 0))
    in_specs, args = [], []
    for a, b, g in terms:
        in_specs += [pl.BlockSpec((tm, a.shape[1]), lambda i: (i, 0)),
                     pl.BlockSpec(b.shape, (lambda i: (0, 0, 0)) if b.ndim == 3 else (lambda i: (0, 0))), vec]
        args += [a, b, g]
    return _call(
        body, grid=(s // tm,), in_specs=in_specs + [row, row], out_specs=[row, row] + [vec] * nt,
        out_shape=[jax.ShapeDtypeStruct((s, d), F32), jax.ShapeDtypeStruct((s, d), BF16)]
        + [jax.ShapeDtypeStruct((1, d), F32)] * nt,
        args=(*args, x, dres), name=name, side=side)


def _causal_mask(transposed=False):
    row = lax.broadcasted_iota(jnp.int32, (CHUNK, CHUNK), 0)
    col = lax.broadcasted_iota(jnp.int32, (CHUNK, CHUNK), 1)
    return col >= row if transposed else col <= row


def _silu_parts(g):
    sg = jax.nn.sigmoid(g)
    return g * sg, sg * (1.0 + g * (1.0 - sg))


def _gate_fwd(z, ln_g, ln_b, ws, bs_t, *, tr=256, side=None):
    s = z.shape[0]
    tr = _row_tile(s, tr)
    w = A_WIDTH

    def body(u_ref, v_ref, g_ref, lg_ref, lb_ref, ws_ref, bst_ref, y_ref):
        v = v_ref[...].astype(F32)
        mu = jnp.mean(v, axis=-1, keepdims=True)
        xc = v - mu
        rs = lax.rsqrt(jnp.mean(xc * xc, axis=-1, keepdims=True) + EPS)
        vln = (xc * rs * lg_ref[...] + lb_ref[...]).astype(BF16)
        mask = _causal_mask()
        for grp in range(A_GROUPS):
            cols = slice(grp * CHUNK, (grp + 1) * CHUNK)
            wsm = jnp.where(mask, ws_ref[grp], 0.0).astype(BF16)
            bcol = bst_ref[:, grp:grp + 1]
            for ci in range(tr // CHUNK):
                rows = slice(ci * CHUNK, (ci + 1) * CHUNK)
                sv = jnp.dot(wsm, vln[rows, cols], preferred_element_type=F32) + bcol
                gv = g_ref[rows, cols].astype(F32)
                y_ref[rows, cols] = (u_ref[rows, cols].astype(F32) * sv * (gv * jax.nn.sigmoid(gv))).astype(BF16)

    vec = pl.BlockSpec((1, w), lambda i: (0, 0))
    (y,), side_outs = _call(
        body, grid=(s // tr,),
        in_specs=[pl.BlockSpec((tr, w), lambda i: (i, 0)), pl.BlockSpec((tr, w), lambda i: (i, 1)),
                  pl.BlockSpec((tr, w), lambda i: (i, 2)), vec, vec,
                  pl.BlockSpec((A_GROUPS, CHUNK, CHUNK), lambda i: (0, 0, 0)),
                  pl.BlockSpec((CHUNK, A_GROUPS), lambda i: (0, 0))],
        out_specs=[pl.BlockSpec((tr, w), lambda i: (i, 0))],
        out_shape=[jax.ShapeDtypeStruct((s, w), BF16)], args=(z, z, z, ln_g, ln_b, ws, bs_t), name="gate_fwd",
        side=side)
    return y, side_outs


def _gate_bwd(z, dy, ln_g, ln_b, ws, ws_t, bs_t, *, tr=256, side=None):
    s = z.shape[0]
    tr = _row_tile(s, tr)
    w = A_WIDTH
    nsteps = s // tr

    def body(u_ref, v_ref, g_ref, dy_ref, lg_ref, lb_ref, ws_ref, wst_ref, bst_ref,
             dz_ref, dlg_ref, dlb_ref, dws_ref, dbst_ref, dvln_sc, dsv_sc):
        i = pl.program_id(0)

        @pl.when(i == 0)
        def _():
            dws_ref[...] = jnp.zeros_like(dws_ref)
            dsv_sc[...] = jnp.zeros_like(dsv_sc)

        v = v_ref[...].astype(F32)
        mu = jnp.mean(v, axis=-1, keepdims=True)
        xc = v - mu
        rs = lax.rsqrt(jnp.mean(xc * xc, axis=-1, keepdims=True) + EPS)
        xh = xc * rs
        lg = lg_ref[...]
        vln = (xh * lg + lb_ref[...]).astype(BF16)
        mask = _causal_mask()
        mask_t = _causal_mask(transposed=True)
        for grp in range(A_GROUPS):
            cols = slice(grp * CHUNK, (grp + 1) * CHUNK)
            wsm = jnp.where(mask, ws_ref[grp], 0.0).astype(BF16)
            wsm_t = jnp.where(mask_t, wst_ref[grp], 0.0).astype(BF16)
            bcol = bst_ref[:, grp:grp + 1]
            for ci in range(tr // CHUNK):
                rows = slice(ci * CHUNK, (ci + 1) * CHUNK)
                vb = vln[rows, cols]
                sv = jnp.dot(wsm, vb, preferred_element_type=F32) + bcol
                uv = u_ref[rows, cols].astype(F32)
                silu, dsilu = _silu_parts(g_ref[rows, cols].astype(F32))
                dyv = dy_ref[rows, cols].astype(F32)
                dyu = dyv * uv
                dz_ref[rows, cols] = (dyv * sv * silu).astype(BF16)
                dz_ref[rows, 2 * w + grp * CHUNK:2 * w + (grp + 1) * CHUNK] = (dyu * sv * dsilu).astype(BF16)
                dsv = dyu * silu
                dsvb = dsv.astype(BF16)
                dvln_sc[rows, cols] = jnp.dot(wsm_t, dsvb, preferred_element_type=F32)
                dws_ref[grp] += lax.dot_general(dsvb, vb, NT, preferred_element_type=F32)
                dsv_sc[grp] += dsv
        dvln = dvln_sc[...]
        dlg_t = jnp.sum(dvln * xh, axis=0, keepdims=True)
        dlb_t = jnp.sum(dvln, axis=0, keepdims=True)
        a = dvln * lg
        dv = rs * (a - jnp.mean(a, axis=-1, keepdims=True) - xh * jnp.mean(a * xh, axis=-1, keepdims=True))
        dz_ref[:, w:2 * w] = dv.astype(BF16)

        @pl.when(i == 0)
        def _():
            dlg_ref[...] = dlg_t
            dlb_ref[...] = dlb_t

        @pl.when(i > 0)
        def _():
            dlg_ref[...] += dlg_t
            dlb_ref[...] += dlb_t

        @pl.when(i == nsteps - 1)
        def _():
            for grp in range(A_GROUPS):
                dws_ref[grp] = jnp.where(mask, dws_ref[grp], 0.0)
                dbst_ref[:, grp:grp + 1] = jnp.sum(dsv_sc[grp], axis=-1, keepdims=True)

    vec = pl.BlockSpec((1, w), lambda i: (0, 0))
    wsspec = pl.BlockSpec((A_GROUPS, CHUNK, CHUNK), lambda i: (0, 0, 0))
    bsspec = pl.BlockSpec((CHUNK, A_GROUPS), lambda i: (0, 0))
    return _call(
        body, grid=(nsteps,),
        in_specs=[pl.BlockSpec((tr, w), lambda i: (i, 0)), pl.BlockSpec((tr, w), lambda i: (i, 1)),
                  pl.BlockSpec((tr, w), lambda i: (i, 2)), pl.BlockSpec((tr, w), lambda i: (i, 0)),
                  vec, vec, wsspec, wsspec, bsspec],
        out_specs=[pl.BlockSpec((tr, 3 * w), lambda i: (i, 0)), vec, vec, wsspec, bsspec],
        out_shape=[jax.ShapeDtypeStruct((s, 3 * w), BF16), jax.ShapeDtypeStruct((1, w), F32),
                   jax.ShapeDtypeStruct((1, w), F32), jax.ShapeDtypeStruct((A_GROUPS, CHUNK, CHUNK), F32),
                   jax.ShapeDtypeStruct((CHUNK, A_GROUPS), F32)],
        scratch=[pltpu.VMEM((tr, w), F32), pltpu.VMEM((A_GROUPS, CHUNK, CHUNK), F32)],
        args=(z, z, z, dy, ln_g, ln_b, ws, ws_t, bs_t), name="gate_bwd", side=side)


HEADS_PER_BLOCK = 128 // HEAD_DIM
BLOCKS_PER_KV = Q_PER_KV // HEADS_PER_BLOCK
SCALE = HEAD_DIM ** -0.5
LOG2_E = math.log2(math.e)


def _rope_tables(s):
    inv_freq = ROPE_THETA ** (-jnp.arange(0, HEAD_DIM, 2, dtype=F32) / HEAD_DIM)
    ang = jnp.arange(s, dtype=F32)[:, None] * inv_freq[None, :]
    cos, sin = jnp.cos(ang), jnp.sin(ang)
    cos2 = jnp.concatenate([cos, cos], axis=-1)
    sin2 = jnp.concatenate([-sin, sin], axis=-1)
    return jnp.tile(cos2, (1, 2)), jnp.tile(sin2, (1, 2))


def _swap_halves(x):
    n = x.shape[-1]
    lane = lax.broadcasted_iota(jnp.int32, x.shape, x.ndim - 1)
    first = (lane % HEAD_DIM) < (HEAD_DIM // 2)
    return jnp.where(first, pltpu.roll(x, n - HEAD_DIM // 2, x.ndim - 1), pltpu.roll(x, HEAD_DIM // 2, x.ndim - 1))


def _left_half(rows):
    return lax.broadcasted_iota(jnp.int32, (rows, 128), 1) < HEAD_DIM


def _dup_heads(x):
    left = _left_half(x.shape[0])
    swapped = pltpu.roll(x, HEAD_DIM, 1)
    return jnp.concatenate([jnp.where(left, x, swapped), jnp.where(left, swapped, x)], axis=-1)


def _fold_heads(a):
    b0, b1 = a[:, :128], a[:, 128:]
    f0 = b0 + pltpu.roll(b0, HEAD_DIM, 1)
    f1 = b1 + pltpu.roll(b1, HEAD_DIM, 1)
    return jnp.where(_left_half(a.shape[0]), f0, f1)


def _kv_rope(kv, b_kv, cos, sin, *, tr=512):
    s = kv.shape[0]
    tr = _row_tile(s, tr)

    def body(kv_ref, b_ref, c_ref, s_ref, k_ref, v_ref):
        x = kv_ref[...] + b_ref[...]
        k = x[:, :KV_WIDTH]
        k_ref[...] = _dup_heads(k * c_ref[...] + _swap_halves(k) * s_ref[...]).astype(BF16)
        v_ref[...] = _dup_heads(x[:, KV_WIDTH:]).astype(BF16)

    tab = pl.BlockSpec((tr, KV_WIDTH), lambda i: (i, 0))
    wide = pl.BlockSpec((tr, 2 * KV_WIDTH), lambda i: (i, 0))
    outs, _ = _call(body, grid=(s // tr,),
                    in_specs=[wide, pl.BlockSpec((1, 2 * KV_WIDTH), lambda i: (0, 0)), tab, tab],
                    out_specs=[wide, wide], out_shape=[jax.ShapeDtypeStruct((s, 2 * KV_WIDTH), BF16)] * 2,
                    args=(kv, b_kv, cos, sin), name="kv_rope")
    return outs


def _kv_rope_bwd(dk2, dv2, cos, sin, *, tr=512):
    s = dk2.shape[0]
    tr = _row_tile(s, tr)

    def body(dk_ref, dv_ref, c_ref, s_ref, dkv_ref, db_ref):
        i = pl.program_id(0)
        d = _fold_heads(dk_ref[...])
        dk = d * c_ref[...] + _swap_halves(d * s_ref[...])
        dvv = _fold_heads(dv_ref[...])
        dkv_ref[:, :KV_WIDTH] = dk.astype(BF16)
        dkv_ref[:, KV_WIDTH:] = dvv.astype(BF16)
        sk = jnp.sum(dk, axis=0, keepdims=True)
        sv = jnp.sum(dvv, axis=0, keepdims=True)

        @pl.when(i == 0)
        def _():
            db_ref[:, :KV_WIDTH] = sk
            db_ref[:, KV_WIDTH:] = sv

        @pl.when(i > 0)
        def _():
            db_ref[:, :KV_WIDTH] += sk
            db_ref[:, KV_WIDTH:] += sv

    tab = pl.BlockSpec((tr, KV_WIDTH), lambda i: (i, 0))
    wide = pl.BlockSpec((tr, 2 * KV_WIDTH), lambda i: (i, 0))
    outs, _ = _call(body, grid=(s // tr,), in_specs=[wide, wide, tab, tab],
                    out_specs=[wide, pl.BlockSpec((1, 2 * KV_WIDTH), lambda i: (0, 0))],
                    out_shape=[jax.ShapeDtypeStruct((s, 2 * KV_WIDTH), BF16),
                               jax.ShapeDtypeStruct((1, 2 * KV_WIDTH), F32)],
                    args=(dk2, dv2, cos, sin), name="kv_rope_bwd")
    return outs


def _from_previous():
    cols = Q_PER_KV * CHUNK
    k = lax.broadcasted_iota(jnp.int32, (CHUNK, cols), 0)
    q = lax.broadcasted_iota(jnp.int32, (CHUNK, cols), 1) & (CHUNK - 1)
    return k > q


def _fold(x2, prev):
    return jnp.where(prev, x2[:CHUNK], x2[CHUNK:])


def _unfold(x, prev):
    zero = jnp.zeros_like(x)
    return jnp.concatenate([jnp.where(prev, x, zero), jnp.where(prev, zero, x)], axis=0)


def _stack_heads(blocks, left):
    parts = []
    for b in blocks:
        parts.append(jnp.where(left, b, jnp.zeros_like(b)))
        parts.append(jnp.where(left, jnp.zeros_like(b), b))
    return jnp.concatenate(parts, axis=0)


def _unstack_heads(xt):
    top = lax.broadcasted_iota(jnp.int32, (128, CHUNK), 0) < HEAD_DIM
    return [jnp.where(top, xt[:, (2 * b) * CHUNK:(2 * b + 1) * CHUNK], xt[:, (2 * b + 1) * CHUNK:(2 * b + 2) * CHUNK]).T
            for b in range(BLOCKS_PER_KV)]


def _sink_row(sk_ref, kvh):
    return jnp.concatenate([jnp.full((1, CHUNK), sk_ref[0, kvh * Q_PER_KV + r], F32) for r in range(Q_PER_KV)], axis=1)


def _stacked_probs(qs, kd, prev, sink, i):
    sc2 = lax.dot_general(kd, qs, NT, preferred_element_type=F32)
    no_previous = jnp.where(i > 0, 0.0, NEG_BIG)
    sc = jnp.where(prev, sc2[:CHUNK] + no_previous, sc2[CHUNK:])
    sink = sink * (1.0 / SCALE)
    m = jnp.maximum(jnp.max(sc, axis=0, keepdims=True), sink)
    p = jnp.exp2((sc - m) * (SCALE * LOG2_E))
    esink = jnp.exp2((sink - m) * (SCALE * LOG2_E))
    inv = 1.0 / (jnp.sum(p, axis=0, keepdims=True) + esink)
    return p * inv, esink * inv


def _lane_block(b):
    return slice(b * 128, (b + 1) * 128)


def _rope_blocks(zq_ref, bq_ref, cos, sin, kvh):
    out = []
    for b in range(BLOCKS_PER_KV):
        cols = _lane_block(kvh * BLOCKS_PER_KV + b)
        q = zq_ref[:, cols].astype(F32) + bq_ref[:, cols]
        out.append((q * cos + _swap_halves(q) * sin).astype(BF16))
    return out


def _attn_specs():
    qspec = pl.BlockSpec((CHUNK, B_WIDTH), lambda i: (i, 0))
    gspec = pl.BlockSpec((CHUNK, B_WIDTH), lambda i: (i, 1))
    prev = pl.BlockSpec((CHUNK, 2 * KV_WIDTH), lambda i: (jnp.maximum(i - 1, 0), 0))
    cur = pl.BlockSpec((CHUNK, 2 * KV_WIDTH), lambda i: (i, 0))
    tab = pl.BlockSpec((CHUNK, KV_WIDTH), lambda i: (i, 0))
    bq = pl.BlockSpec((1, B_WIDTH), lambda i: (0, 0))
    sinks = pl.BlockSpec(memory_space=pltpu.SMEM)
    return qspec, gspec, prev, cur, tab, bq, sinks


def _attn_fwd(zb, k2, v2, cos, sin, b_bq, sinks, *, side=None):
    s = zb.shape[0]

    def body(zq_ref, zg_ref, kp_ref, kc_ref, vp_ref, vc_ref, c_ref, s_ref, bq_ref, sk_ref, y_ref):
        i = pl.program_id(0)
        cos, sin = c_ref[...], s_ref[...]
        kcat = jnp.concatenate([kp_ref[...], kc_ref[...]], axis=0)
        vcat = jnp.concatenate([vp_ref[...], vc_ref[...]], axis=0)
        prev = _from_previous()
        left = _left_half(CHUNK)
        for kvh in range(N_KV_HEADS):
            qs = _stack_heads(_rope_blocks(zq_ref, bq_ref, cos, sin, kvh), left)
            p, _ = _stacked_probs(qs, kcat[:, _lane_block(kvh)], prev, _sink_row(sk_ref, kvh), i)
            ot = lax.dot_general(vcat[:, _lane_block(kvh)], _unfold(p, prev).astype(BF16), TN,
                                 preferred_element_type=F32)
            for b, ob in enumerate(_unstack_heads(ot)):
                cols = _lane_block(kvh * BLOCKS_PER_KV + b)
                gv = zg_ref[:, cols].astype(F32)
                y_ref[:, cols] = (ob * (gv * jax.nn.sigmoid(gv))).astype(BF16)

    qspec, gspec, prev, cur, tab, bq, sk = _attn_specs()
    (y,), side_outs = _call(body, grid=(s // CHUNK,), in_specs=[qspec, gspec, prev, cur, prev, cur, tab, tab, bq, sk],
                            out_specs=[qspec], out_shape=[jax.ShapeDtypeStruct((s, B_WIDTH), BF16)],
                            args=(zb, zb, k2, k2, v2, v2, cos, sin, b_bq, sinks), name="attn_fwd", side=side)
    return y, side_outs


def _attn_bwd(zb, dyb, k2, v2, cos, sin, b_bq, sinks):
    s = zb.shape[0]

    def body(zq_ref, zg_ref, dy_ref, kp_ref, kc_ref, vp_ref, vc_ref, c_ref, s_ref, bq_ref, sk_ref,
             dz_ref, dk_ref, dv_ref, dbq_ref, dsk_ref):
        i = pl.program_id(0)

        @pl.when(i == 0)
        def _():
            dk_ref[...] = jnp.zeros_like(dk_ref)
            dv_ref[...] = jnp.zeros_like(dv_ref)
            dbq_ref[...] = jnp.zeros_like(dbq_ref)
            dsk_ref[...] = jnp.zeros_like(dsk_ref)

        cos, sin = c_ref[...], s_ref[...]
        kcat = jnp.concatenate([kp_ref[...], kc_ref[...]], axis=0)
        vcat = jnp.concatenate([vp_ref[...], vc_ref[...]], axis=0)
        prev = _from_previous()
        left = _left_half(CHUNK)
        lane = lax.broadcasted_iota(jnp.int32, (1, 128), 1)
        dsk_row = jnp.zeros((1, 128), F32)
        cur_rows = pl.ds(pl.multiple_of(i * CHUNK, CHUNK), CHUNK)
        for kvh in range(N_KV_HEADS):
            kd, vd = kcat[:, _lane_block(kvh)], vcat[:, _lane_block(kvh)]
            qs = _stack_heads(_rope_blocks(zq_ref, bq_ref, cos, sin, kvh), left)
            p, psink = _stacked_probs(qs, kd, prev, _sink_row(sk_ref, kvh), i)
            pb = _unfold(p, prev).astype(BF16)
            ot = lax.dot_general(vd, pb, TN, preferred_element_type=F32)
            gates, dys = [], []
            for b in range(BLOCKS_PER_KV):
                cols = _lane_block(kvh * BLOCKS_PER_KV + b)
                gates.append(_silu_parts(zg_ref[:, cols].astype(F32)))
                dys.append(dy_ref[:, cols].astype(F32))
            dos = _stack_heads([(dyv * silu).astype(BF16) for dyv, (silu, _) in zip(dys, gates)], left)
            dp = _fold(lax.dot_general(vd, dos, NT, preferred_element_type=F32), prev)
            delta = jnp.sum(p * dp, axis=0, keepdims=True)
            ds = _unfold(p * (dp - delta) * SCALE, prev).astype(BF16)
            dqt = lax.dot_general(kd, ds, TN, preferred_element_type=F32)
            dk_part = jnp.dot(ds, qs, preferred_element_type=F32)
            dv_part = jnp.dot(pb, dos, preferred_element_type=F32)
            dk_ref[cur_rows, _lane_block(kvh)] += dk_part[CHUNK:]
            dv_ref[cur_rows, _lane_block(kvh)] += dv_part[CHUNK:]

            @pl.when(i > 0)
            def _(kvh=kvh, dk_part=dk_part, dv_part=dv_part):
                prev_rows = pl.ds(pl.multiple_of((i - 1) * CHUNK, CHUNK), CHUNK)
                dk_ref[prev_rows, _lane_block(kvh)] += dk_part[:CHUNK]
                dv_ref[prev_rows, _lane_block(kvh)] += dv_part[:CHUNK]

            sink_grad = psink * delta
            for r in range(Q_PER_KV):
                dsink = -jnp.sum(sink_grad[:, r * CHUNK:(r + 1) * CHUNK], axis=1, keepdims=True)
                dsk_row = dsk_row + jnp.where(lane == kvh * Q_PER_KV + r, dsink, 0.0)
            blocks = zip(_unstack_heads(ot), _unstack_heads(dqt), dys, gates)
            for b, (ob, dqr, dyv, (_, dsilu)) in enumerate(blocks):
                blk = kvh * BLOCKS_PER_KV + b
                dq = dqr * cos + _swap_halves(dqr * sin)
                dbq_ref[:, _lane_block(blk)] += jnp.sum(dq, axis=0, keepdims=True)
                dz_ref[:, _lane_block(blk)] = dq.astype(BF16)
                dz_ref[:, _lane_block(B_WIDTH // 128 + blk)] = (dyv * ob * dsilu).astype(BF16)
        dsk_ref[0:1, :] += dsk_row

    qspec, gspec, prev, cur, tab, bq, sk = _attn_specs()
    full = pl.BlockSpec((s, 2 * KV_WIDTH), lambda i: (0, 0))
    outs, _ = _call(
        body, grid=(s // CHUNK,),
        in_specs=[qspec, gspec, qspec, prev, cur, prev, cur, tab, tab, bq, sk],
        out_specs=[pl.BlockSpec((CHUNK, 2 * B_WIDTH), lambda i: (i, 0)), full, full, bq,
                   pl.BlockSpec((8, 128), lambda i: (0, 0))],
        out_shape=[jax.ShapeDtypeStruct((s, 2 * B_WIDTH), BF16), jax.ShapeDtypeStruct((s, 2 * KV_WIDTH), F32),
                   jax.ShapeDtypeStruct((s, 2 * KV_WIDTH), F32), jax.ShapeDtypeStruct((1, B_WIDTH), F32),
                   jax.ShapeDtypeStruct((8, 128), F32)],
        args=(zb, zb, dyb, k2, k2, v2, v2, cos, sin, b_bq, sinks), name="attn_bwd")
    return outs


def _place():
    x, y, c = lax.axis_index("x"), lax.axis_index("y"), lax.axis_index("c")
    return x, y, c, [(1 - x, y), (x, 1 - y), (1 - x, 1 - y)]


def _relations():
    return [(r >> 2 & 1, r >> 1 & 1, r & 1) for r in range(1, 8)]


def _gather_side(arrs):
    n = len(arrs)

    def copies(ins, outs, sems):
        send_ici, recv_ici, send_d2d, recv_d2d, local_sem = sems
        x, y, c, chips = _place()
        me = 2 * x + y

        def rows(a, half):
            hr = arrs[a].shape[0] // 2
            return pl.ds(half * hr, hr)

        def ici(a, j, src_chip, to):
            return pltpu.make_async_remote_copy(
                src_ref=ins[a].at[rows(a, c)], dst_ref=outs[a].at[src_chip, rows(a, c)],
                send_sem=send_ici.at[a, j], recv_sem=recv_ici.at[a, j], device_id=to, device_id_type=MESH)

        def d2d(a, j, chip, half):
            blk = outs[a].at[chip, rows(a, half)]
            return pltpu.make_async_remote_copy(
                src_ref=blk, dst_ref=blk, send_sem=send_d2d.at[a, j], recv_sem=recv_d2d.at[a, j],
                device_id=(x, y, 1 - c), device_id_type=MESH)

        local = [pltpu.make_async_copy(ins[a], outs[a].at[me], local_sem.at[a]) for a in range(n)]
        pairs = [(a, j, chip) for a in range(n) for j, chip in enumerate(chips)]
        return c, me, local, ici, d2d, pairs

    def start(ins, outs, sems):
        c, me, local, ici, _, pairs = copies(ins, outs, sems)
        for cp in local:
            cp.start()
        for a, j, chip in pairs:
            ici(a, j, me, (*chip, c)).start()

    def finish(ins, outs, sems):
        c, me, local, ici, d2d, pairs = copies(ins, outs, sems)
        for a, j, (px, py) in pairs:
            ici(a, j, 2 * px + py, (px, py, c)).wait_recv()
            d2d(a, j, 2 * px + py, c).start()
        for a, j, (px, py) in pairs:
            d2d(a, j, 2 * px + py, 1 - c).wait_recv()
        for a, j, (px, py) in pairs:
            ici(a, j, me, (px, py, c)).wait_send()
            d2d(a, j, 2 * px + py, c).wait_send()
        for cp in local:
            cp.wait()

    return _Side(arrs, [jax.ShapeDtypeStruct((N_CHIPS,) + a.shape, a.dtype) for a in arrs],
                 [pltpu.SemaphoreType.DMA((n, 3))] * 4 + [pltpu.SemaphoreType.DMA((n,))], start, finish)


def _exchange_side(grads):
    n = len(grads)

    def copies(ins, outs, sems):
        send_sem, recv_sem = sems
        x, y, c, _ = _place()
        cps = []
        for a in range(n):
            hr = grads[a].shape[1] // 2
            cps.append(pltpu.make_async_remote_copy(
                src_ref=ins[a].at[:, pl.ds((1 - c) * hr, hr), :], dst_ref=outs[a],
                send_sem=send_sem.at[a], recv_sem=recv_sem.at[a], device_id=(x, y, 1 - c), device_id_type=MESH))
        return cps

    def start(ins, outs, sems):
        for cp in copies(ins, outs, sems):
            cp.start()

    def finish(ins, outs, sems):
        for cp in copies(ins, outs, sems):
            cp.wait()

    return _Side(grads, [jax.ShapeDtypeStruct((g.shape[0], g.shape[1] // 2, g.shape[2]), g.dtype) for g in grads],
                 [pltpu.SemaphoreType.DMA((n,))] * 2, start, finish)


def _scatter_side(chip_sums, small=None):
    n = len(chip_sums)
    arrs = list(chip_sums) + ([small] if small is not None else [])

    def copies(ins, outs, sems):
        x, y, c, chips = _place()
        cps = []
        for a in range(n):
            for j, (px, py) in enumerate(chips):
                cps.append(pltpu.make_async_remote_copy(
                    src_ref=ins[a].at[2 * px + py], dst_ref=outs[a].at[j],
                    send_sem=sems[0].at[a, j], recv_sem=sems[1].at[a, j], device_id=(px, py, c), device_id_type=MESH))
        if small is not None:
            for r, (fx, fy, fc) in enumerate(_relations(), start=1):
                px, py, pc = x ^ fx, y ^ fy, c ^ fc
                cps.append(pltpu.make_async_remote_copy(
                    src_ref=ins[n].at[4 * px + 2 * py + pc], dst_ref=outs[n].at[r],
                    send_sem=sems[2].at[r - 1], recv_sem=sems[3].at[r - 1], device_id=(px, py, pc),
                    device_id_type=MESH))
        return cps

    def start(ins, outs, sems):
        for cp in copies(ins, outs, sems):
            cp.start()

    def finish(ins, outs, sems):
        for cp in copies(ins, outs, sems):
            cp.wait()

    shapes = [jax.ShapeDtypeStruct((3,) + t.shape[1:], t.dtype) for t in chip_sums]
    sems = [pltpu.SemaphoreType.DMA((n, 3))] * 2
    if small is not None:
        shapes.append(jax.ShapeDtypeStruct(small.shape, small.dtype))
        sems += [pltpu.SemaphoreType.DMA((7,))] * 2
    return _Side(arrs, shapes, sems, start, finish)


def _small_scatter_side(small):
    def copies(ins, outs, sems):
        x, y, c, _ = _place()
        cps = []
        for r, (fx, fy, fc) in enumerate(_relations(), start=1):
            px, py, pc = x ^ fx, y ^ fy, c ^ fc
            cps.append(pltpu.make_async_remote_copy(
                src_ref=ins[0].at[4 * px + 2 * py + pc], dst_ref=outs[0].at[r],
                send_sem=sems[0].at[r - 1], recv_sem=sems[1].at[r - 1], device_id=(px, py, pc), device_id_type=MESH))
        return cps

    def start(ins, outs, sems):
        for cp in copies(ins, outs, sems):
            cp.start()

    def finish(ins, outs, sems):
        for cp in copies(ins, outs, sems):
            cp.wait()

    return _Side([small], [jax.ShapeDtypeStruct(small.shape, small.dtype)], [pltpu.SemaphoreType.DMA((7,))] * 2,
                 start, finish)


def _share_side(halves, small=None):
    n = len(halves)
    arrs = list(halves) + ([small] if small is not None else [])

    def copies(ins, outs, sems, mine):
        x, y, c, _ = _place()
        me = 4 * x + 2 * y + c
        cps = []
        for a in range(n):
            hr = halves[a].shape[0] // 2
            rows = pl.ds((c if mine else 1 - c) * hr, hr)
            cps.append(pltpu.make_async_remote_copy(
                src_ref=ins[a].at[rows], dst_ref=outs[a].at[rows],
                send_sem=sems[0].at[a], recv_sem=sems[1].at[a], device_id=(x, y, 1 - c), device_id_type=MESH))
        if small is not None:
            for r, (fx, fy, fc) in enumerate(_relations(), start=1):
                px, py, pc = x ^ fx, y ^ fy, c ^ fc
                seg = me if mine else 4 * px + 2 * py + pc
                cps.append(pltpu.make_async_remote_copy(
                    src_ref=ins[n].at[seg], dst_ref=outs[n].at[seg],
                    send_sem=sems[2].at[r - 1], recv_sem=sems[3].at[r - 1], device_id=(px, py, pc),
                    device_id_type=MESH))
        return cps

    def start(ins, outs, sems):
        for cp in copies(ins, outs, sems, True):
            cp.start()

    def finish(ins, outs, sems):
        for cp in copies(ins, outs, sems, False):
            cp.wait_recv()
        for cp in copies(ins, outs, sems, True):
            cp.wait_send()

    sems = [pltpu.SemaphoreType.DMA((n,))] * 2 + ([pltpu.SemaphoreType.DMA((7,))] * 2 if small is not None else [])
    return _Side(arrs, [jax.ShapeDtypeStruct(h.shape, h.dtype) for h in arrs], sems, start, finish,
                 aliases={i: i for i in range(len(arrs))})


def _mm_gathering(a, shard, order, *, name, tm=1024):
    s, k = a.shape
    nc = shard.shape[1]
    tm = _row_tile(s, tm)
    tn = nc // 2
    hr = k // 2

    qr = hr // 2

    def body(order_ref, a_ref, shard_ref, z_ref, full_ref, wbuf, send_ici, recv_ici, send_relay, recv_relay,
             send_d2d, recv_d2d, local_sem, load_sem):
        t, jj, i = pl.program_id(0), pl.program_id(1), pl.program_id(2)
        x, y, c, chips = _place()
        me = 2 * x + y
        nbrs = chips[:2]
        chip_of = [2 * px + py for px, py in chips]

        def quarter(q):
            return pl.ds(c * hr + q * qr, qr)

        def sibling_quarter(q):
            return pl.ds((1 - c) * hr + q * qr, qr)

        def whole(half):
            return pl.ds(half * hr, hr)

        def cols(h):
            return pl.ds(h * tn, tn)

        def direct(j, src_chip, h):
            return pltpu.make_async_remote_copy(
                src_ref=shard_ref.at[whole(c), cols(h)], dst_ref=full_ref.at[src_chip, whole(c), cols(h)],
                send_sem=send_ici.at[j, h], recv_sem=recv_ici.at[j, h], device_id=(*nbrs[j], c), device_id_type=MESH)

        def relay(j, src_chip, h):
            blk = full_ref.at[src_chip, quarter(j), cols(h)]
            return pltpu.make_async_remote_copy(
                src_ref=blk, dst_ref=blk, send_sem=send_relay.at[j, h], recv_sem=recv_relay.at[j, h],
                device_id=(*nbrs[1 - j], c), device_id_type=MESH)

        def d2d(j, chip, rows, h):
            blk = full_ref.at[chip, rows, cols(h)]
            return pltpu.make_async_remote_copy(
                src_ref=blk, dst_ref=blk, send_sem=send_d2d.at[j, h], recv_sem=recv_d2d.at[j, h],
                device_id=(x, y, 1 - c), device_id_type=MESH)

        def load(src, h):
            cp = pltpu.make_async_copy(src.at[:, cols(h)], wbuf.at[h], load_sem.at[h])
            cp.start()
            cp.wait()

        local = pltpu.make_async_copy(shard_ref, full_ref.at[me], local_sem)

        def at(tt, h):
            return functools.reduce(jnp.logical_and, [t == tt, jj == h, i == 0])

        for h in range(2):
            @pl.when(at(0, h))
            def _(h=h):
                if h == 0:
                    local.start()
                    for hh in range(2):
                        for j in range(2):
                            direct(j, me, hh).start()
                load(shard_ref, h)

            for j in range(2):
                @pl.when(at(j + 1, h))
                def _(j=j, h=h):
                    direct(j, chip_of[j], h).wait_recv()
                    relay(j, chip_of[j], h).start()
                    d2d(j, chip_of[j], whole(c), h).start()
                    d2d(j, chip_of[j], whole(1 - c), h).wait_recv()
                    load(full_ref.at[chip_of[j]], h)

            @pl.when(at(3, h))
            def _(h=h):
                for j in range(2):
                    relay(1 - j, chip_of[2], h).wait_recv()
                    d2d(2 + j, chip_of[2], quarter(1 - j), h).start()
                for j in range(2):
                    d2d(2 + j, chip_of[2], sibling_quarter(1 - j), h).wait_recv()
                load(full_ref.at[chip_of[2]], h)

        z_ref[...] = jnp.dot(a_ref[...], wbuf[jj], preferred_element_type=F32).astype(z_ref.dtype)

        last = functools.reduce(jnp.logical_and, [t == N_CHIPS - 1, jj == 1, i == s // tm - 1])

        @pl.when(last)
        def _():
            for h in range(2):
                for j in range(2):
                    direct(j, me, h).wait_send()
                    relay(j, chip_of[j], h).wait_send()
                    d2d(j, chip_of[j], whole(c), h).wait_send()
                    d2d(2 + j, chip_of[2], quarter(1 - j), h).wait_send()
            local.wait()

    return pl.pallas_call(
        body,
        grid_spec=pltpu.PrefetchScalarGridSpec(
            num_scalar_prefetch=1, grid=(N_CHIPS, 2, s // tm),
            in_specs=[pl.BlockSpec((tm, k), lambda t, jj, i, order: (i, 0)), HBM],
            out_specs=[pl.BlockSpec((tm, tn), lambda t, jj, i, order: (i, order[t] * 2 + jj)), HBM],
            scratch_shapes=[pltpu.VMEM((2, k, tn), BF16)] + [pltpu.SemaphoreType.DMA((2, 2))] * 4
            + [pltpu.SemaphoreType.DMA((4, 2))] * 2 + [pltpu.SemaphoreType.DMA, pltpu.SemaphoreType.DMA((2,))]),
        out_shape=[jax.ShapeDtypeStruct((s, N_CHIPS * nc), BF16), jax.ShapeDtypeStruct((N_CHIPS, k, nc), BF16)],
        name=name, compiler_params=_cparams(),
    )(order, a, shard)


def _col_tile(cols):
    return cols if cols <= 2048 else 512


def _add_sibling(grad, recv, core, *, name):
    k, r, c = grad.shape
    hr = r // 2
    tr = min(hr, 256)
    tc = _col_tile(c)
    nrb = hr // tr

    def body(core_ref, g_ref, r_ref, o_ref):
        o_ref[...] = (g_ref[...] + r_ref[...]).astype(BF16)

    return pl.pallas_call(
        body,
        grid_spec=pltpu.PrefetchScalarGridSpec(
            num_scalar_prefetch=1, grid=(k, nrb, c // tc),
            in_specs=[pl.BlockSpec((None, tr, tc), lambda kk, i, j, core: (kk, core[0] * nrb + i, j)),
                      pl.BlockSpec((None, tr, tc), lambda kk, i, j, core: (kk, i, j))],
            out_specs=pl.BlockSpec((None, tr, tc), lambda kk, i, j, core: (kk, i, j))),
        out_shape=jax.ShapeDtypeStruct((k, hr, c), BF16), name=name, compiler_params=_cparams(),
    )(core, grad, recv)


def _sum_chips(grad, from_sibling, recv, place, *, name):
    _, hr, c = from_sibling.shape
    tr = min(hr, 256)
    tc = _col_tile(c)
    nrb = hr // tr

    def body(place_ref, g_ref, s_ref, r0_ref, r1_ref, r2_ref, o_ref):
        own = g_ref[...] + s_ref[...]
        o_ref[...] = ((own + r0_ref[...].astype(F32)) + r1_ref[...].astype(F32)) + r2_ref[...].astype(F32)

    def rspec(j):
        return pl.BlockSpec((None, tr, tc), lambda i, jj, place: (j, i, jj))

    return pl.pallas_call(
        body,
        grid_spec=pltpu.PrefetchScalarGridSpec(
            num_scalar_prefetch=1, grid=(nrb, c // tc),
            in_specs=[pl.BlockSpec((None, tr, tc), lambda i, jj, place: (place[0], place[1] * nrb + i, jj)),
                      pl.BlockSpec((None, tr, tc), lambda i, jj, place: (place[0], i, jj)),
                      rspec(0), rspec(1), rspec(2)],
            out_specs=pl.BlockSpec((tr, tc), lambda i, jj, place: (place[1] * nrb + i, jj))),
        out_shape=jax.ShapeDtypeStruct((2 * hr, c), F32), name=name, compiler_params=_cparams(),
    )(place, grad, from_sibling, recv, recv, recv)


def _sum_small(small, recv, place):
    _, sr, _ = small.shape

    def body(place_ref, own_ref, r_ref, o_ref):
        acc = own_ref[...]
        for r in range(1, 8):
            acc = acc + r_ref[r]
        o_ref[...] = acc

    return pl.pallas_call(
        body,
        grid_spec=pltpu.PrefetchScalarGridSpec(
            num_scalar_prefetch=1, grid=(1,),
            in_specs=[pl.BlockSpec((None, sr, 128), lambda i, place: (place[2], 0, 0)),
                      pl.BlockSpec((8, sr, 128), lambda i, place: (0, 0, 0))],
            out_specs=pl.BlockSpec((None, sr, 128), lambda i, place: (place[2], 0, 0))),
        out_shape=jax.ShapeDtypeStruct(small.shape, F32), name="sum_small", compiler_params=_cparams(),
    )(place, small, recv)


def _spread_side(vec):
    def copies(ins, outs, sems):
        x, y, c, _ = _place()
        return [pltpu.make_async_remote_copy(
            src_ref=ins[0], dst_ref=outs[0].at[r], send_sem=sems[0].at[r - 1], recv_sem=sems[1].at[r - 1],
            device_id=(x ^ fx, y ^ fy, c ^ fc), device_id_type=MESH)
            for r, (fx, fy, fc) in enumerate(_relations(), start=1)]

    def start(ins, outs, sems):
        for cp in copies(ins, outs, sems):
            cp.start()

    def finish(ins, outs, sems):
        for cp in copies(ins, outs, sems):
            cp.wait()

    return _Side([vec], [jax.ShapeDtypeStruct((8,) + vec.shape, vec.dtype)], [pltpu.SemaphoreType.DMA((7,))] * 2,
                 start, finish)


def _sum_in_device_order(own, spread, place):
    def body(place_ref, own_ref, r_ref, o_ref):
        me = place_ref[2]
        acc = jnp.zeros_like(own_ref[...])
        for d in range(8):
            slot = jnp.where(me == d, 1, me ^ d)
            acc = acc + jnp.where(me == d, own_ref[...], r_ref[slot])
        o_ref[...] = acc

    return pl.pallas_call(
        body,
        grid_spec=pltpu.PrefetchScalarGridSpec(
            num_scalar_prefetch=1, grid=(1,),
            in_specs=[pl.BlockSpec(own.shape, lambda i, place: (0, 0)),
                      pl.BlockSpec(spread.shape, lambda i, place: (0, 0, 0))],
            out_specs=pl.BlockSpec(own.shape, lambda i, place: (0, 0))),
        out_shape=jax.ShapeDtypeStruct(own.shape, F32), name="sum_in_device_order", compiler_params=_cparams(),
    )(place, own, spread)


def _adamw(w, g, m, v, *, name):
    r, c = w.shape
    tr = 256 if r % 256 == 0 else r
    tc = _col_tile(c)
    bc1 = 1.0 - ADAM_B1 ** ADAM_STEP
    bc2 = 1.0 - ADAM_B2 ** ADAM_STEP

    def body(w_ref, g_ref, m_ref, v_ref, d_ref, nm_ref, nv_ref):
        gv = g_ref[...]
        nm = ADAM_B1 * m_ref[...] + (1.0 - ADAM_B1) * gv
        nv = ADAM_B2 * v_ref[...] + (1.0 - ADAM_B2) * (gv * gv)
        d_ref[...] = -ADAM_LR * ((nm / bc1) / (jnp.sqrt(nv / bc2) + ADAM_EPS) + ADAM_WD * w_ref[...])
        nm_ref[...] = nm
        nv_ref[...] = nv

    spec = pl.BlockSpec((tr, tc), lambda i, j: (i, j))
    outs, _ = _call(body, grid=(r // tr, c // tc), in_specs=[spec] * 4, out_specs=[spec] * 3,
                    out_shape=[jax.ShapeDtypeStruct((r, c), F32)] * 3, args=(w, g, m, v), name=name)
    return outs


SMALL_ORDER = ["a_ws", "a_bs", "a_norm_g", "a_ln_g", "a_ln_b", "kv_norm_g", "b_kv", "b_norm_g", "b_bq",
               "b_sinks", "final_norm_g"]
SHARDED_SMALL = {"a_norm_g", "a_ln_g", "a_ln_b"}
PACK_TILE = 8 * 128


def _rows128(a):
    flat = a.reshape(-1)
    return jnp.pad(flat, (0, (-flat.shape[0]) % PACK_TILE)).reshape(-1, 128)


def _pack_rows(parts, multiple):
    rows = [_rows128(p) for p in parts]
    total = sum(r.shape[0] for r in rows)
    pad = (-total) % multiple
    if pad:
        rows.append(jnp.zeros((pad, 128), rows[0].dtype))
    return jnp.concatenate(rows, axis=0)


def _unpack_rows(packed, shapes):
    out, row = [], 0
    for shp in shapes:
        size = math.prod(shp)
        nrow = -(-size // PACK_TILE) * 8
        out.append(packed[row:row + nrow].reshape(-1)[:size].reshape(shp))
        row += nrow
    return out


WEIGHTS = ["a_norm_g", "a_w_in", "a_ln_g", "a_ln_b", "a_ws", "a_bs", "a_w_out", "kv_norm_g", "w_kv", "b_kv",
           "b_norm_g", "b_w_in", "b_bq", "b_sinks", "b_w_out", "final_norm_g"]
BIG = ["a_w_in", "a_w_out", "w_kv", "b_w_in", "b_w_out"]


class _Reduction:
    def __init__(self, names, partials, core, place, small=None):
        self.names, self.partials, self.core, self.place, self.small = names, partials, core, place, small

    def exchange_side(self):
        return _exchange_side(self.partials)

    def took_exchange(self, from_sibling):
        self.from_sibling = from_sibling
        self.chip_sums = [_add_sibling(g, r, self.core, name="add_sibling_" + n)
                          for g, r, n in zip(self.partials, from_sibling, self.names)]

    def scatter_side(self):
        return _scatter_side(self.chip_sums, self.small)

    def took_scatter(self, arrived):
        big = arrived[:len(self.names)]
        self.halves = [_sum_chips(g, fs, r, self.place, name="sum_chips_" + n)
                       for g, fs, r, n in zip(self.partials, self.from_sibling, big, self.names)]
        self.small_mine = _sum_small(self.small, arrived[-1], self.place) if self.small is not None else None

    def share_side(self):
        return _share_side(self.halves, self.small_mine)

    def took_share(self, shared):
        self.grads = dict(zip(self.names, shared[:len(self.names)]))
        self.small_full = shared[-1] if self.small is not None else None


def _step(x, loss_target, p, m, v):
    xi, yi, ci = lax.axis_index("x"), lax.axis_index("y"), lax.axis_index("c")
    chip = 2 * xi + yi
    device = 4 * xi + 2 * yi + ci
    core = jnp.reshape(ci, (1,)).astype(jnp.int32)
    place = jnp.stack([chip, ci, device]).astype(jnp.int32)
    x, tgt = x[0], loss_target[0]
    s = x.shape[0]
    cos, sin = _rope_tables(s)

    shard2d = {n: p[n].reshape(p[n].shape[-2:]) for n in BIG}
    shard_bf = {n: shard2d[n].astype(BF16) for n in BIG}
    ws = p["a_ws"][0]
    ws_t = jnp.swapaxes(ws, 1, 2)
    bs_t = p["a_bs"][0].T
    kv_norm_g, b_kv = p["kv_norm_g"].reshape(1, -1), p["b_kv"].reshape(1, -1)
    final_norm_g = p["final_norm_g"].reshape(1, -1)

    vec_shapes = [p[n].shape for n in ("a_norm_g", "a_ln_g", "a_ln_b")]
    vec_pack = _pack_rows([p["a_norm_g"], p["a_ln_g"], p["a_ln_b"]], 16)
    (vec_all,) = _comm_call(_gather_side([vec_pack]), "gather_vectors")
    vecs = [_unpack_rows(vec_all[k], vec_shapes) for k in range(N_CHIPS)]
    a_norm_g, a_ln_g, a_ln_b = (jnp.concatenate([vk[t] for vk in vecs], axis=-1) for t in range(3))

    (n_a,) = _rms_fwd(x, [a_norm_g], name="rms_a")
    order = jnp.stack([chip, 2 * (1 - xi) + yi, 2 * xi + (1 - yi), 2 * (1 - xi) + (1 - yi)]).astype(jnp.int32)
    z, a_w_in = _mm_gathering(n_a, shard_bf["a_w_in"], order, name="mm_a_in")
    y, (a_w_out,) = _gate_fwd(z, a_ln_g, a_ln_b, ws, bs_t, side=_gather_side([shard_bf["a_w_out"]]))
    a_w_out = a_w_out.reshape(A_WIDTH, D_MODEL)
    (h1, n_kv, n_b), (w_kv, b_w_in) = _mm_residual_norms(
        y, a_w_out, x, [kv_norm_g, p["b_norm_g"]], name="mm_a_out",
        side=_gather_side([shard_bf["w_kv"], shard_bf["b_w_in"]]))
    w_kv = w_kv.reshape(D_MODEL, 2 * KV_WIDTH)
    kv = _mm_nn(n_kv, w_kv, name="mm_kv", tn=2 * KV_WIDTH)
    kr, vv = _kv_rope(kv, b_kv, cos, sin)
    zb = _mm_nn(n_b, b_w_in, name="mm_b_in", tn=512, tm=1024, out_dtype=BF16)
    yb, (b_w_out,) = _attn_fwd(zb, kr, vv, cos, sin, p["b_bq"], p["b_sinks"], side=_gather_side([shard_bf["b_w_out"]]))
    b_w_out = b_w_out.reshape(B_WIDTH, D_MODEL)
    loss_blk, dh2, dh2b, d_final_g = _mm_residual_loss(yb, b_w_out, h1, tgt, final_norm_g, name="mm_b_out")

    d_b_w_out = _mm_tn(yb, dh2b, name="mm_d_b_w_out", tm=B_WIDTH, tn=D_MODEL)
    red_bo = _Reduction(["b_w_out"], [d_b_w_out.reshape(N_CHIPS, B_WIDTH // N_CHIPS, D_MODEL)], core, place)
    dyb, got = _mm_nt(dh2b, b_w_out, name="mm_dyb", out_dtype=BF16, side=red_bo.exchange_side())
    red_bo.took_exchange(got)
    dzb, dk_rot, dv, d_bq, d_sinks = _attn_bwd(zb, dyb, kr, vv, cos, sin, p["b_bq"], p["b_sinks"])
    dkv, d_b_kv = _kv_rope_bwd(dk_rot, dv, cos, sin)
    d_b_w_in, got = _mm_tn(n_b, dzb, name="mm_d_b_w_in", tm=D_MODEL, tn=512, shards=N_CHIPS,
                           side=red_bo.scatter_side())
    red_bo.took_scatter(got)
    d_w_kv, got = _mm_tn(n_kv, dkv, name="mm_d_w_kv", tm=D_MODEL, tn=2 * KV_WIDTH, side=red_bo.share_side())
    red_bo.took_share(got)
    red_bi = _Reduction(["b_w_in", "w_kv"], [d_b_w_in, d_w_kv.reshape(N_CHIPS, D_MODEL // N_CHIPS, 2 * KV_WIDTH)],
                        core, place)
    (dh1, dh1b, d_kv_g, d_b_g), got = _mm_nt_rms_bwd(
        [(dkv, w_kv, kv_norm_g), (dzb, b_w_in, p["b_norm_g"])], h1, dh2, name="mm_dn_b", tm=512,
        side=red_bi.exchange_side())
    red_bi.took_exchange(got)

    d_a_w_out, got = _mm_tn(y, dh1b, name="mm_d_a_w_out", tm=1024, tn=D_MODEL, side=red_bi.scatter_side())
    red_bi.took_scatter(got)
    red_ao = _Reduction(["a_w_out"], [d_a_w_out.reshape(N_CHIPS, A_WIDTH // N_CHIPS, D_MODEL)], core, place)
    sides = [red_ao.exchange_side(), red_bi.share_side()]
    dy, got = _mm_nt(dh1b, a_w_out, name="mm_dy", tn=1024, out_dtype=BF16, side=_join(sides))
    got = _split(got, sides)
    red_ao.took_exchange(got[0])
    red_bi.took_share(got[1])
    (dz, d_ln_g, d_ln_b, d_ws, d_bs_t), got = _gate_bwd(z, dy, a_ln_g, a_ln_b, ws, ws_t, bs_t,
                                                        side=red_ao.scatter_side())
    red_ao.took_scatter(got)
    small = {
        "a_ws": d_ws, "a_bs": d_bs_t.T, "a_ln_g": d_ln_g, "a_ln_b": d_ln_b,
        "kv_norm_g": d_kv_g, "b_kv": d_b_kv, "b_norm_g": d_b_g, "b_bq": d_bq,
        "b_sinks": d_sinks[0:1, :N_Q_HEADS], "final_norm_g": d_final_g,
    }
    packed = [n for n in SMALL_ORDER if n != "a_norm_g"]
    small_shapes = [small[n].shape for n in packed] + [(1, 1)]
    small_pack = _pack_rows([small[n] for n in packed] + [loss_blk[0:1, 0:1]], 64)
    seg = small_pack.shape[0] // 8
    small_pack = small_pack.reshape(8, seg, 128)
    sides = [red_ao.share_side(), _small_scatter_side(small_pack)]
    d_a_w_in, got = _mm_tn(n_a, dz, name="mm_d_a_w_in", tm=D_MODEL, tn=1536, shards=N_CHIPS, side=_join(sides))
    got = _split(got, sides)
    red_ao.took_share(got[0])
    small_mine = _sum_small(small_pack, got[1][0], place)

    red_ai = _Reduction(["a_w_in"], [d_a_w_in], core, place)
    red_ai.took_exchange(_comm_call(red_ai.exchange_side(), "exchange_last"))
    (dx, _, d_a_g), got = _mm_nt_rms_bwd([(dz, a_w_in, a_norm_g)], x, dh1, name="mm_dn_a", tm=256,
                                         side=red_ai.scatter_side())
    red_ai.took_scatter(got)
    red_ai.small, red_ai.small_mine = small_pack, small_mine
    d_a_g = _rows128(d_a_g)
    sides = [red_ai.share_side(), _spread_side(d_a_g)]
    got = _split(_comm_call(_join(sides), "share_last"), sides)
    red_ai.took_share(got[0])
    small_full = dict(zip(packed + ["loss"], _unpack_rows(red_ai.small_full.reshape(8 * seg, 128), small_shapes)))
    small_full["a_norm_g"] = _sum_in_device_order(d_a_g, got[1][0], place).reshape(1, -1)
    loss = small_full["loss"].reshape(())

    grad_big = {**red_bo.grads, **red_bi.grads, **red_ao.grads, **red_ai.grads}
    grads = {}
    for n in SMALL_ORDER:
        gfull = small_full[n]
        if n in SHARDED_SMALL:
            width = p[n].shape[-1]
            gfull = lax.dynamic_slice_in_dim(gfull, chip * width, width, axis=-1)
        grads[n] = gfull.reshape(p[n].shape)
    for n in BIG:
        grads[n] = grad_big[n].reshape(p[n].shape)

    delta, new_m, new_v = {}, {}, {}
    for n in BIG:
        d, nm, nv = _adamw(shard2d[n], grad_big[n], m[n].reshape(shard2d[n].shape), v[n].reshape(shard2d[n].shape),
                           name="adamw_" + n)
        delta[n], new_m[n], new_v[n] = d.reshape(p[n].shape), nm.reshape(p[n].shape), nv.reshape(p[n].shape)
    shapes = [p[n].shape for n in SMALL_ORDER]
    packs = [_pack_rows([src[n] for n in SMALL_ORDER], 8) for src in (p, grads, m, v)]
    outs = _adamw(*packs, name="adamw_small")
    for res, packed in zip((delta, new_m, new_v), outs):
        for n, val in zip(SMALL_ORDER, _unpack_rows(packed, shapes)):
            res[n] = val

    return (loss, dx[None], *[grads[n] for n in WEIGHTS], *[delta[n] for n in WEIGHTS],
            *[new_m[n] for n in WEIGHTS], *[new_v[n] for n in WEIGHTS])


def kernel(x, a_norm_g, a_w_in, a_ln_g, a_ln_b, a_ws, a_bs, a_w_out, kv_norm_g, w_kv, b_kv, b_norm_g, b_w_in, b_bq, b_sinks, b_w_out, final_norm_g, loss_target, m_a_norm_g, m_a_w_in, m_a_ln_g, m_a_ln_b, m_a_ws, m_a_bs, m_a_w_out, m_kv_norm_g, m_w_kv, m_b_kv, m_b_norm_g, m_b_w_in, m_b_bq, m_b_sinks, m_b_w_out, m_final_norm_g, v_a_norm_g, v_a_w_in, v_a_ln_g, v_a_ln_b, v_a_ws, v_a_bs, v_a_w_out, v_kv_norm_g, v_w_kv, v_b_kv, v_b_norm_g, v_b_w_in, v_b_bq, v_b_sinks, v_b_w_out, v_final_norm_g):
    p = dict(a_norm_g=a_norm_g, a_w_in=a_w_in, a_ln_g=a_ln_g, a_ln_b=a_ln_b, a_ws=a_ws, a_bs=a_bs, a_w_out=a_w_out,
             kv_norm_g=kv_norm_g, w_kv=w_kv, b_kv=b_kv, b_norm_g=b_norm_g, b_w_in=b_w_in, b_bq=b_bq, b_sinks=b_sinks,
             b_w_out=b_w_out, final_norm_g=final_norm_g)
    m = dict(a_norm_g=m_a_norm_g, a_w_in=m_a_w_in, a_ln_g=m_a_ln_g, a_ln_b=m_a_ln_b, a_ws=m_a_ws, a_bs=m_a_bs,
             a_w_out=m_a_w_out, kv_norm_g=m_kv_norm_g, w_kv=m_w_kv, b_kv=m_b_kv, b_norm_g=m_b_norm_g, b_w_in=m_b_w_in,
             b_bq=m_b_bq, b_sinks=m_b_sinks, b_w_out=m_b_w_out, final_norm_g=m_final_norm_g)
    v = dict(a_norm_g=v_a_norm_g, a_w_in=v_a_w_in, a_ln_g=v_a_ln_g, a_ln_b=v_a_ln_b, a_ws=v_a_ws, a_bs=v_a_bs,
             a_w_out=v_a_w_out, kv_norm_g=v_kv_norm_g, w_kv=v_w_kv, b_kv=v_b_kv, b_norm_g=v_b_norm_g, b_w_in=v_b_w_in,
             b_bq=v_b_bq, b_sinks=v_b_sinks, b_w_out=v_b_w_out, final_norm_g=v_final_norm_g)
    return _step(x, loss_target, p, m, v)
```

```python
import functools
import math

import jax
import jax.numpy as jnp
from jax import lax
from jax.experimental import pallas as pl
from jax.experimental.pallas import tpu as pltpu

F32 = jnp.float32
BF16 = jnp.bfloat16

D_MODEL = 1024
CHUNK = 128
A_WIDTH = 2048
A_GROUPS = 16
HEAD_DIM = 64
N_Q_HEADS = 16
N_KV_HEADS = 2
Q_PER_KV = 8
B_WIDTH = 1024
KV_WIDTH = 128
ROPE_THETA = 10000.0
EPS = 1e-5
N_CHIPS = 4

ADAM_LR = 0.001
ADAM_B1 = 0.9
ADAM_B2 = 0.999
ADAM_EPS = 1e-08
ADAM_WD = 0.01
ADAM_STEP = 10

VMEM_LIMIT = 48 * 1024 * 1024
MESH = pl.DeviceIdType.MESH
NEG_BIG = -1e30
HBM = pl.BlockSpec(memory_space=pl.ANY)

NN = (((1,), (0,)), ((), ()))
NT = (((1,), (1,)), ((), ()))
TN = (((0,), (0,)), ((), ()))


def _cparams(**kw):
    return pltpu.CompilerParams(vmem_limit_bytes=VMEM_LIMIT, **kw)


class _Side:
    def __init__(self, ins, out_shapes, sems, start, finish, aliases=None):
        self.ins, self.out_shapes, self.sems = list(ins), list(out_shapes), list(sems)
        self.start, self.finish = start, finish
        self.aliases = dict(aliases or {})


def _join(sides):
    sides = [s for s in sides if s is not None]
    if not sides:
        return None
    offs, i, o, m = [], 0, 0, 0
    for s in sides:
        offs.append((i, o, m))
        i, o, m = i + len(s.ins), o + len(s.out_shapes), m + len(s.sems)

    def run(which):
        def go(ins, outs, sems):
            for s, (a, b, c) in zip(sides, offs):
                getattr(s, which)(ins[a:a + len(s.ins)], outs[b:b + len(s.out_shapes)], sems[c:c + len(s.sems)])
        return go

    aliases = {}
    for s, (a, b, _) in zip(sides, offs):
        aliases.update({a + k: b + v for k, v in s.aliases.items()})
    return _Side([x for s in sides for x in s.ins], [x for s in sides for x in s.out_shapes],
                 [x for s in sides for x in s.sems], run("start"), run("finish"), aliases)


def _split(side_outs, sides):
    out, pos = [], 0
    for s in sides:
        out.append(list(side_outs[pos:pos + len(s.out_shapes)]))
        pos += len(s.out_shapes)
    return out


def _call(body, *, grid, in_specs, out_specs, out_shape, args, name, scratch=(), side=None):
    in_specs, out_specs, out_shape, scratch = list(in_specs), list(out_specs), list(out_shape), list(scratch)
    if side is None:
        res = pl.pallas_call(body, grid=grid, in_specs=in_specs, out_specs=out_specs, out_shape=out_shape,
                             scratch_shapes=scratch, name=name, compiler_params=_cparams())(*args)
        return list(res), []
    n_in, n_out, n_sc = len(in_specs), len(out_specs), len(scratch)
    s_in, s_out = len(side.ins), len(side.out_shapes)

    def wrapped(*refs):
        ins, refs = refs[:n_in], refs[n_in:]
        side_ins, refs = refs[:s_in], refs[s_in:]
        outs, refs = refs[:n_out], refs[n_out:]
        side_outs, refs = refs[:s_out], refs[s_out:]
        scr, side_sems = refs[:n_sc], refs[n_sc:]
        ids = [pl.program_id(a) for a in range(len(grid))]
        first = functools.reduce(jnp.logical_and, [i == 0 for i in ids])
        last = functools.reduce(jnp.logical_and, [i == g - 1 for i, g in zip(ids, grid)])

        @pl.when(first)
        def _():
            side.start(side_ins, side_outs, side_sems)

        body(*ins, *outs, *scr)

        @pl.when(last)
        def _():
            side.finish(side_ins, side_outs, side_sems)

    res = pl.pallas_call(
        wrapped, grid=grid, in_specs=in_specs + [HBM] * s_in, out_specs=out_specs + [HBM] * s_out,
        out_shape=out_shape + side.out_shapes, scratch_shapes=scratch + side.sems,
        input_output_aliases={n_in + k: n_out + v for k, v in side.aliases.items()},
        name=name, compiler_params=_cparams(),
    )(*args, *side.ins)
    return list(res[:n_out]), list(res[n_out:])


def _comm_call(side, name):
    s_in, s_out = len(side.ins), len(side.out_shapes)

    def body(*refs):
        ins, outs, sems = refs[:s_in], refs[s_in:s_in + s_out], refs[s_in + s_out:]
        side.start(ins, outs, sems)
        side.finish(ins, outs, sems)

    return list(pl.pallas_call(
        body, in_specs=[HBM] * s_in, out_specs=[HBM] * s_out, out_shape=side.out_shapes, scratch_shapes=side.sems,
        input_output_aliases=side.aliases, name=name,
    )(*side.ins))


def _matmul(a, b, *, dims, grid, a_spec, b_spec, o_spec, out_shape, name, acc_axis=None,
            residual=None, r_spec=None, side=None):
    has_res = residual is not None

    def body(*refs):
        if has_res:
            a_ref, b_ref, r_ref, o_ref = refs
        else:
            a_ref, b_ref, o_ref = refs
        part = lax.dot_general(a_ref[...], b_ref[...], dims, preferred_element_type=F32)
        if acc_axis is None:
            if has_res:
                part = part + r_ref[...]
            o_ref[...] = part.astype(o_ref.dtype)
        else:
            k = pl.program_id(acc_axis)

            @pl.when(k == 0)
            def _():
                o_ref[...] = part

            @pl.when(k > 0)
            def _():
                o_ref[...] += part

    in_specs = [a_spec, b_spec] + ([r_spec] if has_res else [])
    args = (a, b) + ((residual,) if has_res else ())
    (out,), side_outs = _call(body, grid=grid, in_specs=in_specs, out_specs=[o_spec], out_shape=[out_shape],
                              args=args, name=name, side=side)
    return (out, side_outs) if side is not None else out


def _row_tile(s, want):
    return min(s, want)


def _mm_nn(a, b, *, name, tn, out_dtype=F32, residual=None, tm=512, side=None):
    s, k = a.shape
    tm = _row_tile(s, tm)
    if b.ndim == 3:
        nsh, _, nc = b.shape
        npb = nc // tn
        n = nsh * nc
        b_spec = pl.BlockSpec((None, k, tn), lambda i, j: (j // npb, 0, j % npb))
    else:
        n = b.shape[1]
        b_spec = pl.BlockSpec((k, tn), lambda i, j: (0, j))
    return _matmul(
        a, b, dims=NN, grid=(s // tm, n // tn),
        a_spec=pl.BlockSpec((tm, k), lambda i, j: (i, 0)), b_spec=b_spec,
        o_spec=pl.BlockSpec((tm, tn), lambda i, j: (i, j)),
        out_shape=jax.ShapeDtypeStruct((s, n), out_dtype), name=name, side=side,
        residual=residual, r_spec=pl.BlockSpec((tm, tn), lambda i, j: (i, j)) if residual is not None else None)


def _mm_nt(a, b, *, name, tn=None, tm=512, out_dtype=F32, side=None):
    s, k = a.shape
    tm = _row_tile(s, tm)
    n = b.shape[0]
    tn = n if tn is None else tn
    return _matmul(
        a, b, dims=NT, grid=(s // tm, n // tn),
        a_spec=pl.BlockSpec((tm, k), lambda i, j: (i, 0)),
        b_spec=pl.BlockSpec((tn, k), lambda i, j: (j, 0)),
        o_spec=pl.BlockSpec((tm, tn), lambda i, j: (i, j)),
        out_shape=jax.ShapeDtypeStruct((s, n), out_dtype), name=name, side=side)


def _mm_tn(a, b, *, name, tm, tn, tk=2048, shards=None, side=None):
    s, m = a.shape
    n = b.shape[1]
    tk = _row_tile(s, tk)
    if shards is None:
        o_spec = pl.BlockSpec((tm, tn), lambda i, j, kk: (i, j))
        out_shape = jax.ShapeDtypeStruct((m, n), F32)
    else:
        assert tm == m
        nc = n // shards
        npb = nc // tn
        o_spec = pl.BlockSpec((None, m, tn), lambda i, j, kk: (j // npb, 0, j % npb))
        out_shape = jax.ShapeDtypeStruct((shards, m, nc), F32)
    return _matmul(
        a, b, dims=TN, grid=(m // tm, n // tn, s // tk), acc_axis=2,
        a_spec=pl.BlockSpec((tk, tm), lambda i, j, kk: (kk, i)),
        b_spec=pl.BlockSpec((tk, tn), lambda i, j, kk: (kk, j)),
        o_spec=o_spec, out_shape=out_shape, name=name, side=side)


def _rstd(x):
    return lax.rsqrt(jnp.mean(x * x, axis=-1, keepdims=True) + EPS)


def _rms_fwd(x, gains, *, name, tr=256):
    s, d = x.shape
    tr = _row_tile(s, tr)
    ng = len(gains)

    def body(*refs):
        xv = refs[0][...]
        xh = xv * _rstd(xv)
        for t in range(ng):
            refs[1 + ng + t][...] = (xh * refs[1 + t][...]).astype(BF16)

    row = pl.BlockSpec((tr, d), lambda i: (i, 0))
    vec = pl.BlockSpec((1, d), lambda i: (0, 0))
    outs, _ = _call(body, grid=(s // tr,), in_specs=[row] + [vec] * ng, out_specs=[row] * ng,
                    out_shape=[jax.ShapeDtypeStruct((s, d), BF16)] * ng, args=(x, *gains), name=name)
    return outs


def _accumulate(i, ref, value):
    @pl.when(i == 0)
    def _():
        ref[...] = value

    @pl.when(i > 0)
    def _():
        ref[...] += value


def _mm_residual_norms(y, w, res, gains, *, name, tm=512, side=None):
    s, k = y.shape
    d = w.shape[1]
    tm = _row_tile(s, tm)
    ng = len(gains)

    def body(y_ref, w_ref, r_ref, *rest):
        g_refs, h_ref, n_refs = rest[:ng], rest[ng], rest[ng + 1:]
        h = r_ref[...] + jnp.dot(y_ref[...], w_ref[...], preferred_element_type=F32)
        h_ref[...] = h
        xh = h * _rstd(h)
        for t in range(ng):
            n_refs[t][...] = (xh * g_refs[t][...]).astype(BF16)

    row = pl.BlockSpec((tm, d), lambda i: (i, 0))
    vec = pl.BlockSpec((1, d), lambda i: (0, 0))
    return _call(
        body, grid=(s // tm,),
        in_specs=[pl.BlockSpec((tm, k), lambda i: (i, 0)), pl.BlockSpec((k, d), lambda i: (0, 0)), row] + [vec] * ng,
        out_specs=[row] * (1 + ng),
        out_shape=[jax.ShapeDtypeStruct((s, d), F32)] + [jax.ShapeDtypeStruct((s, d), BF16)] * ng,
        args=(y, w, res, *gains), name=name, side=side)


def _mm_residual_loss(y, w, res, tgt, gain, *, name, tm=512):
    s, k = y.shape
    d = w.shape[1]
    tm = _row_tile(s, tm)

    def body(y_ref, w_ref, r_ref, t_ref, g_ref, loss_ref, dh_ref, dhb_ref, dg_ref):
        i = pl.program_id(0)
        hv = r_ref[...] + jnp.dot(y_ref[...], w_ref[...], preferred_element_type=F32)
        g = g_ref[...]
        r = _rstd(hv)
        xh = hv * r
        diff = xh * g - t_ref[...]
        part = 0.5 / d * jnp.sum(jnp.sum(diff * diff, axis=-1, keepdims=True), axis=0, keepdims=True)
        dout = diff * (1.0 / d)
        a = dout * g
        dh = r * (a - xh * jnp.mean(a * xh, axis=-1, keepdims=True))
        dh_ref[...] = dh
        dhb_ref[...] = dh.astype(BF16)
        _accumulate(i, dg_ref, jnp.sum(dout * xh, axis=0, keepdims=True))
        _accumulate(i, loss_ref, jnp.broadcast_to(part, (8, 128)))

    row = pl.BlockSpec((tm, d), lambda i: (i, 0))
    vec = pl.BlockSpec((1, d), lambda i: (0, 0))
    outs, _ = _call(
        body, grid=(s // tm,),
        in_specs=[pl.BlockSpec((tm, k), lambda i: (i, 0)), pl.BlockSpec((k, d), lambda i: (0, 0)), row, row, vec],
        out_specs=[pl.BlockSpec((8, 128), lambda i: (0, 0)), row, row, vec],
        out_shape=[jax.ShapeDtypeStruct((8, 128), F32), jax.ShapeDtypeStruct((s, d), F32),
                   jax.ShapeDtypeStruct((s, d), BF16), jax.ShapeDtypeStruct((1, d), F32)],
        args=(y, w, res, tgt, gain), name=name)
    return outs


def _mm_nt_rms_bwd(terms, x, dres, *, name, tm, side=None):
    s, d = x.shape
    tm = _row_tile(s, tm)
    nt = len(terms)

    def body(*refs):
        a_refs, b_refs, g_refs = refs[0:3 * nt:3], refs[1:3 * nt:3], refs[2:3 * nt:3]
        x_ref, dres_ref = refs[3 * nt], refs[3 * nt + 1]
        dx_ref, dxb_ref = refs[3 * nt + 2], refs[3 * nt + 3]
        dg_refs = refs[3 * nt + 4:]
        i = pl.program_id(0)
        xv = x_ref[...]
        r = _rstd(xv)
        xh = xv * r
        acc = jnp.zeros_like(xv)
        for t in range(nt):
            b_ref = b_refs[t]
            if len(b_ref.shape) == 3:
                kc = b_ref.shape[2]
                dn = None
                for sh in range(b_ref.shape[0]):
                    part = lax.dot_general(a_refs[t][:, sh * kc:(sh + 1) * kc], b_ref[sh], NT, preferred_element_type=F32)
                    dn = part if dn is None else dn + part
            else:
                dn = lax.dot_general(a_refs[t][...], b_ref[...], NT, preferred_element_type=F32)
            acc = acc + dn * g_refs[t][...]
            _accumulate(i, dg_refs[t], jnp.sum(dn * xh, axis=0, keepdims=True))
        dx = dres_ref[...] + r * (acc - xh * jnp.mean(acc * xh, axis=-1, keepdims=True))
        dx_ref[...] = dx
        dxb_ref[...] = dx.astype(BF16)

    row = pl.BlockSpec((tm, d), lambda i: (i, 0))
    vec = pl.BlockSpec((1, d), lambda i: (0, 0))
    in_specs, args = [], []
    for a, b, g in terms:
        in_specs += [pl.BlockSpec((tm, a.shape[1]), lambda i: (i, 0)),
                     pl.BlockSpec(b.shape, (lambda i: (0, 0, 0)) if b.ndim == 3 else (lambda i: (0, 0))), vec]
        args += [a, b, g]
    return _call(
        body, grid=(s // tm,), in_specs=in_specs + [row, row], out_specs=[row, row] + [vec] * nt,
        out_shape=[jax.ShapeDtypeStruct((s, d), F32), jax.ShapeDtypeStruct((s, d), BF16)]
        + [jax.ShapeDtypeStruct((1, d), F32)] * nt,
        args=(*args, x, dres), name=name, side=side)


def _causal_mask(transposed=False):
    row = lax.broadcasted_iota(jnp.int32, (CHUNK, CHUNK), 0)
    col = lax.broadcasted_iota(jnp.int32, (CHUNK, CHUNK), 1)
    return col >= row if transposed else col <= row


def _silu_parts(g):
    sg = jax.nn.sigmoid(g)
    return g * sg, sg * (1.0 + g * (1.0 - sg))


def _gate_fwd(z, ln_g, ln_b, ws, bs_t, *, tr=256, side=None):
    s = z.shape[0]
    tr = _row_tile(s, tr)
    w = A_WIDTH

    def body(u_ref, v_ref, g_ref, lg_ref, lb_ref, ws_ref, bst_ref, y_ref):
        v = v_ref[...].astype(F32)
        mu = jnp.mean(v, axis=-1, keepdims=True)
        xc = v - mu
        rs = lax.rsqrt(jnp.mean(xc * xc, axis=-1, keepdims=True) + EPS)
        vln = (xc * rs * lg_ref[...] + lb_ref[...]).astype(BF16)
        mask = _causal_mask()
        for grp in range(A_GROUPS):
            cols = slice(grp * CHUNK, (grp + 1) * CHUNK)
            wsm = jnp.where(mask, ws_ref[grp], 0.0).astype(BF16)
            bcol = bst_ref[:, grp:grp + 1]
            for ci in range(tr // CHUNK):
                rows = slice(ci * CHUNK, (ci + 1) * CHUNK)
                sv = jnp.dot(wsm, vln[rows, cols], preferred_element_type=F32) + bcol
                gv = g_ref[rows, cols].astype(F32)
                y_ref[rows, cols] = (u_ref[rows, cols].astype(F32) * sv * (gv * jax.nn.sigmoid(gv))).astype(BF16)

    vec = pl.BlockSpec((1, w), lambda i: (0, 0))
    (y,), side_outs = _call(
        body, grid=(s // tr,),
        in_specs=[pl.BlockSpec((tr, w), lambda i: (i, 0)), pl.BlockSpec((tr, w), lambda i: (i, 1)),
                  pl.BlockSpec((tr, w), lambda i: (i, 2)), vec, vec,
                  pl.BlockSpec((A_GROUPS, CHUNK, CHUNK), lambda i: (0, 0, 0)),
                  pl.BlockSpec((CHUNK, A_GROUPS), lambda i: (0, 0))],
        out_specs=[pl.BlockSpec((tr, w), lambda i: (i, 0))],
        out_shape=[jax.ShapeDtypeStruct((s, w), BF16)], args=(z, z, z, ln_g, ln_b, ws, bs_t), name="gate_fwd",
        side=side)
    return y, side_outs


def _gate_bwd(z, dy, ln_g, ln_b, ws, ws_t, bs_t, *, tr=256, side=None):
    s = z.shape[0]
    tr = _row_tile(s, tr)
    w = A_WIDTH
    nsteps = s // tr

    def body(u_ref, v_ref, g_ref, dy_ref, lg_ref, lb_ref, ws_ref, wst_ref, bst_ref,
             dz_ref, dlg_ref, dlb_ref, dws_ref, dbst_ref, dvln_sc, dsv_sc):
        i = pl.program_id(0)

        @pl.when(i == 0)
        def _():
            dws_ref[...] = jnp.zeros_like(dws_ref)
            dsv_sc[...] = jnp.zeros_like(dsv_sc)

        v = v_ref[...].astype(F32)
        mu = jnp.mean(v, axis=-1, keepdims=True)
        xc = v - mu
        rs = lax.rsqrt(jnp.mean(xc * xc, axis=-1, keepdims=True) + EPS)
        xh = xc * rs
        lg = lg_ref[...]
        vln = (xh * lg + lb_ref[...]).astype(BF16)
        mask = _causal_mask()
        mask_t = _causal_mask(transposed=True)
        for grp in range(A_GROUPS):
            cols = slice(grp * CHUNK, (grp + 1) * CHUNK)
            wsm = jnp.where(mask, ws_ref[grp], 0.0).astype(BF16)
            wsm_t = jnp.where(mask_t, wst_ref[grp], 0.0).astype(BF16)
            bcol = bst_ref[:, grp:grp + 1]
            for ci in range(tr // CHUNK):
                rows = slice(ci * CHUNK, (ci + 1) * CHUNK)
                vb = vln[rows, cols]
                sv = jnp.dot(wsm, vb, preferred_element_type=F32) + bcol
                uv = u_ref[rows, cols].astype(F32)
                silu, dsilu = _silu_parts(g_ref[rows, cols].astype(F32))
                dyv = dy_ref[rows, cols].astype(F32)
                dyu = dyv * uv
                dz_ref[rows, cols] = (dyv * sv * silu).astype(BF16)
                dz_ref[rows, 2 * w + grp * CHUNK:2 * w + (grp + 1) * CHUNK] = (dyu * sv * dsilu).astype(BF16)
                dsv = dyu * silu
                dsvb = dsv.astype(BF16)
                dvln_sc[rows, cols] = jnp.dot(wsm_t, dsvb, preferred_element_type=F32)
                dws_ref[grp] += lax.dot_general(dsvb, vb, NT, preferred_element_type=F32)
                dsv_sc[grp] += dsv
        dvln = dvln_sc[...]
        dlg_t = jnp.sum(dvln * xh, axis=0, keepdims=True)
        dlb_t = jnp.sum(dvln, axis=0, keepdims=True)
        a = dvln * lg
        dv = rs * (a - jnp.mean(a, axis=-1, keepdims=True) - xh * jnp.mean(a * xh, axis=-1, keepdims=True))
        dz_ref[:, w:2 * w] = dv.astype(BF16)

        @pl.when(i == 0)
        def _():
            dlg_ref[...] = dlg_t
            dlb_ref[...] = dlb_t

        @pl.when(i > 0)
        def _():
            dlg_ref[...] += dlg_t
            dlb_ref[...] += dlb_t

        @pl.when(i == nsteps - 1)
        def _():
            for grp in range(A_GROUPS):
                dws_ref[grp] = jnp.where(mask, dws_ref[grp], 0.0)
                dbst_ref[:, grp:grp + 1] = jnp.sum(dsv_sc[grp], axis=-1, keepdims=True)

    vec = pl.BlockSpec((1, w), lambda i: (0, 0))
    wsspec = pl.BlockSpec((A_GROUPS, CHUNK, CHUNK), lambda i: (0, 0, 0))
    bsspec = pl.BlockSpec((CHUNK, A_GROUPS), lambda i: (0, 0))
    return _call(
        body, grid=(nsteps,),
        in_specs=[pl.BlockSpec((tr, w), lambda i: (i, 0)), pl.BlockSpec((tr, w), lambda i: (i, 1)),
                  pl.BlockSpec((tr, w), lambda i: (i, 2)), pl.BlockSpec((tr, w), lambda i: (i, 0)),
                  vec, vec, wsspec, wsspec, bsspec],
        out_specs=[pl.BlockSpec((tr, 3 * w), lambda i: (i, 0)), vec, vec, wsspec, bsspec],
        out_shape=[jax.ShapeDtypeStruct((s, 3 * w), BF16), jax.ShapeDtypeStruct((1, w), F32),
                   jax.ShapeDtypeStruct((1, w), F32), jax.ShapeDtypeStruct((A_GROUPS, CHUNK, CHUNK), F32),
                   jax.ShapeDtypeStruct((CHUNK, A_GROUPS), F32)],
        scratch=[pltpu.VMEM((tr, w), F32), pltpu.VMEM((A_GROUPS, CHUNK, CHUNK), F32)],
        args=(z, z, z, dy, ln_g, ln_b, ws, ws_t, bs_t), name="gate_bwd", side=side)


HEADS_PER_BLOCK = 128 // HEAD_DIM
BLOCKS_PER_KV = Q_PER_KV // HEADS_PER_BLOCK
SCALE = HEAD_DIM ** -0.5
LOG2_E = math.log2(math.e)


def _rope_tables(s):
    inv_freq = ROPE_THETA ** (-jnp.arange(0, HEAD_DIM, 2, dtype=F32) / HEAD_DIM)
    ang = jnp.arange(s, dtype=F32)[:, None] * inv_freq[None, :]
    cos, sin = jnp.cos(ang), jnp.sin(ang)
    cos2 = jnp.concatenate([cos, cos], axis=-1)
    sin2 = jnp.concatenate([-sin, sin], axis=-1)
    return jnp.tile(cos2, (1, 2)), jnp.tile(sin2, (1, 2))


def _swap_halves(x):
    n = x.shape[-1]
    lane = lax.broadcasted_iota(jnp.int32, x.shape, x.ndim - 1)
    first = (lane % HEAD_DIM) < (HEAD_DIM // 2)
    return jnp.where(first, pltpu.roll(x, n - HEAD_DIM // 2, x.ndim - 1), pltpu.roll(x, HEAD_DIM // 2, x.ndim - 1))


def _left_half(rows):
    return lax.broadcasted_iota(jnp.int32, (rows, 128), 1) < HEAD_DIM


def _dup_heads(x):
    left = _left_half(x.shape[0])
    swapped = pltpu.roll(x, HEAD_DIM, 1)
    return jnp.concatenate([jnp.where(left, x, swapped), jnp.where(left, swapped, x)], axis=-1)


def _fold_heads(a):
    b0, b1 = a[:, :128], a[:, 128:]
    f0 = b0 + pltpu.roll(b0, HEAD_DIM, 1)
    f1 = b1 + pltpu.roll(b1, HEAD_DIM, 1)
    return jnp.where(_left_half(a.shape[0]), f0, f1)


def _kv_rope(kv, b_kv, cos, sin, *, tr=512):
    s = kv.shape[0]
    tr = _row_tile(s, tr)

    def body(kv_ref, b_ref, c_ref, s_ref, k_ref, v_ref):
        x = kv_ref[...] + b_ref[...]
        k = x[:, :KV_WIDTH]
        k_ref[...] = _dup_heads(k * c_ref[...] + _swap_halves(k) * s_ref[...]).astype(BF16)
        v_ref[...] = _dup_heads(x[:, KV_WIDTH:]).astype(BF16)

    tab = pl.BlockSpec((tr, KV_WIDTH), lambda i: (i, 0))
    wide = pl.BlockSpec((tr, 2 * KV_WIDTH), lambda i: (i, 0))
    outs, _ = _call(body, grid=(s // tr,),
                    in_specs=[wide, pl.BlockSpec((1, 2 * KV_WIDTH), lambda i: (0, 0)), tab, tab],
                    out_specs=[wide, wide], out_shape=[jax.ShapeDtypeStruct((s, 2 * KV_WIDTH), BF16)] * 2,
                    args=(kv, b_kv, cos, sin), name="kv_rope")
    return outs


def _kv_rope_bwd(dk2, dv2, cos, sin, *, tr=512):
    s = dk2.shape[0]
    tr = _row_tile(s, tr)

    def body(dk_ref, dv_ref, c_ref, s_ref, dkv_ref, db_ref):
        i = pl.program_id(0)
        d = _fold_heads(dk_ref[...])
        dk = d * c_ref[...] + _swap_halves(d * s_ref[...])
        dvv = _fold_heads(dv_ref[...])
        dkv_ref[:, :KV_WIDTH] = dk.astype(BF16)
        dkv_ref[:, KV_WIDTH:] = dvv.astype(BF16)
        sk = jnp.sum(dk, axis=0, keepdims=True)
        sv = jnp.sum(dvv, axis=0, keepdims=True)

        @pl.when(i == 0)
        def _():
            db_ref[:, :KV_WIDTH] = sk
            db_ref[:, KV_WIDTH:] = sv

        @pl.when(i > 0)
        def _():
            db_ref[:, :KV_WIDTH] += sk
            db_ref[:, KV_WIDTH:] += sv

    tab = pl.BlockSpec((tr, KV_WIDTH), lambda i: (i, 0))
    wide = pl.BlockSpec((tr, 2 * KV_WIDTH), lambda i: (i, 0))
    outs, _ = _call(body, grid=(s // tr,), in_specs=[wide, wide, tab, tab],
                    out_specs=[wide, pl.BlockSpec((1, 2 * KV_WIDTH), lambda i: (0, 0))],
                    out_shape=[jax.ShapeDtypeStruct((s, 2 * KV_WIDTH), BF16),
                               jax.ShapeDtypeStruct((1, 2 * KV_WIDTH), F32)],
                    args=(dk2, dv2, cos, sin), name="kv_rope_bwd")
    return outs


def _from_previous():
    cols = Q_PER_KV * CHUNK
    k = lax.broadcasted_iota(jnp.int32, (CHUNK, cols), 0)
    q = lax.broadcasted_iota(jnp.int32, (CHUNK, cols), 1) & (CHUNK - 1)
    return k > q


def _fold(x2, prev):
    return jnp.where(prev, x2[:CHUNK], x2[CHUNK:])


def _unfold(x, prev):
    zero = jnp.zeros_like(x)
    return jnp.concatenate([jnp.where(prev, x, zero), jnp.where(prev, zero, x)], axis=0)


def _stack_heads(blocks, left):
    parts = []
    for b in blocks:
        parts.append(jnp.where(left, b, jnp.zeros_like(b)))
        parts.append(jnp.where(left, jnp.zeros_like(b), b))
    return jnp.concatenate(parts, axis=0)


def _unstack_heads(xt):
    top = lax.broadcasted_iota(jnp.int32, (128, CHUNK), 0) < HEAD_DIM
    return [jnp.where(top, xt[:, (2 * b) * CHUNK:(2 * b + 1) * CHUNK], xt[:, (2 * b + 1) * CHUNK:(2 * b + 2) * CHUNK]).T
            for b in range(BLOCKS_PER_KV)]


def _sink_row(sk_ref, kvh):
    return jnp.concatenate([jnp.full((1, CHUNK), sk_ref[0, kvh * Q_PER_KV + r], F32) for r in range(Q_PER_KV)], axis=1)


def _stacked_probs(qs, kd, prev, sink, i):
    sc2 = lax.dot_general(kd, qs, NT, preferred_element_type=F32)
    no_previous = jnp.where(i > 0, 0.0, NEG_BIG)
    sc = jnp.where(prev, sc2[:CHUNK] + no_previous, sc2[CHUNK:])
    sink = sink * (1.0 / SCALE)
    m = jnp.maximum(jnp.max(sc, axis=0, keepdims=True), sink)
    p = jnp.exp2((sc - m) * (SCALE * LOG2_E))
    esink = jnp.exp2((sink - m) * (SCALE * LOG2_E))
    inv = 1.0 / (jnp.sum(p, axis=0, keepdims=True) + esink)
    return p * inv, esink * inv


def _lane_block(b):
    return slice(b * 128, (b + 1) * 128)


def _rope_blocks(zq_ref, bq_ref, cos, sin, kvh):
    out = []
    for b in range(BLOCKS_PER_KV):
        cols = _lane_block(kvh * BLOCKS_PER_KV + b)
        q = zq_ref[:, cols].astype(F32) + bq_ref[:, cols]
        out.append((q * cos + _swap_halves(q) * sin).astype(BF16))
    return out


def _attn_specs():
    qspec = pl.BlockSpec((CHUNK, B_WIDTH), lambda i: (i, 0))
    gspec = pl.BlockSpec((CHUNK, B_WIDTH), lambda i: (i, 1))
    prev = pl.BlockSpec((CHUNK, 2 * KV_WIDTH), lambda i: (jnp.maximum(i - 1, 0), 0))
    cur = pl.BlockSpec((CHUNK, 2 * KV_WIDTH), lambda i: (i, 0))
    tab = pl.BlockSpec((CHUNK, KV_WIDTH), lambda i: (i, 0))
    bq = pl.BlockSpec((1, B_WIDTH), lambda i: (0, 0))
    sinks = pl.BlockSpec(memory_space=pltpu.SMEM)
    return qspec, gspec, prev, cur, tab, bq, sinks


def _attn_fwd(zb, k2, v2, cos, sin, b_bq, sinks, *, side=None):
    s = zb.shape[0]

    def body(zq_ref, zg_ref, kp_ref, kc_ref, vp_ref, vc_ref, c_ref, s_ref, bq_ref, sk_ref, y_ref):
        i = pl.program_id(0)
        cos, sin = c_ref[...], s_ref[...]
        kcat = jnp.concatenate([kp_ref[...], kc_ref[...]], axis=0)
        vcat = jnp.concatenate([vp_ref[...], vc_ref[...]], axis=0)
        prev = _from_previous()
        left = _left_half(CHUNK)
        for kvh in range(N_KV_HEADS):
            qs = _stack_heads(_rope_blocks(zq_ref, bq_ref, cos, sin, kvh), left)
            p, _ = _stacked_probs(qs, kcat[:, _lane_block(kvh)], prev, _sink_row(sk_ref, kvh), i)
            ot = lax.dot_general(vcat[:, _lane_block(kvh)], _unfold(p, prev).astype(BF16), TN,
                                 preferred_element_type=F32)
            for b, ob in enumerate(_unstack_heads(ot)):
                cols = _lane_block(kvh * BLOCKS_PER_KV + b)
                gv = zg_ref[:, cols].astype(F32)
                y_ref[:, cols] = (ob * (gv * jax.nn.sigmoid(gv))).astype(BF16)

    qspec, gspec, prev, cur, tab, bq, sk = _attn_specs()
    (y,), side_outs = _call(body, grid=(s // CHUNK,), in_specs=[qspec, gspec, prev, cur, prev, cur, tab, tab, bq, sk],
                            out_specs=[qspec], out_shape=[jax.ShapeDtypeStruct((s, B_WIDTH), BF16)],
                            args=(zb, zb, k2, k2, v2, v2, cos, sin, b_bq, sinks), name="attn_fwd", side=side)
    return y, side_outs


def _attn_bwd(zb, dyb, k2, v2, cos, sin, b_bq, sinks):
    s = zb.shape[0]

    def body(zq_ref, zg_ref, dy_ref, kp_ref, kc_ref, vp_ref, vc_ref, c_ref, s_ref, bq_ref, sk_ref,
             dz_ref, dk_ref, dv_ref, dbq_ref, dsk_ref):
        i = pl.program_id(0)

        @pl.when(i == 0)
        def _():
            dk_ref[...] = jnp.zeros_like(dk_ref)
            dv_ref[...] = jnp.zeros_like(dv_ref)
            dbq_ref[...] = jnp.zeros_like(dbq_ref)
            dsk_ref[...] = jnp.zeros_like(dsk_ref)

        cos, sin = c_ref[...], s_ref[...]
        kcat = jnp.concatenate([kp_ref[...], kc_ref[...]], axis=0)
        vcat = jnp.concatenate([vp_ref[...], vc_ref[...]], axis=0)
        prev = _from_previous()
        left = _left_half(CHUNK)
        lane = lax.broadcasted_iota(jnp.int32, (1, 128), 1)
        dsk_row = jnp.zeros((1, 128), F32)
        cur_rows = pl.ds(pl.multiple_of(i * CHUNK, CHUNK), CHUNK)
        for kvh in range(N_KV_HEADS):
            kd, vd = kcat[:, _lane_block(kvh)], vcat[:, _lane_block(kvh)]
            qs = _stack_heads(_rope_blocks(zq_ref, bq_ref, cos, sin, kvh), left)
            p, psink = _stacked_probs(qs, kd, prev, _sink_row(sk_ref, kvh), i)
            pb = _unfold(p, prev).astype(BF16)
            ot = lax.dot_general(vd, pb, TN, preferred_element_type=F32)
            gates, dys = [], []
            for b in range(BLOCKS_PER_KV):
                cols = _lane_block(kvh * BLOCKS_PER_KV + b)
                gates.append(_silu_parts(zg_ref[:, cols].astype(F32)))
                dys.append(dy_ref[:, cols].astype(F32))
            dos = _stack_heads([(dyv * silu).astype(BF16) for dyv, (silu, _) in zip(dys, gates)], left)
            dp = _fold(lax.dot_general(vd, dos, NT, preferred_element_type=F32), prev)
            delta = jnp.sum(p * dp, axis=0, keepdims=True)
            ds = _unfold(p * (dp - delta) * SCALE, prev).astype(BF16)
            dqt = lax.dot_general(kd, ds, TN, preferred_element_type=F32)
            dk_part = jnp.dot(ds, qs, preferred_element_type=F32)
            dv_part = jnp.dot(pb, dos, preferred_element_type=F32)
            dk_ref[cur_rows, _lane_block(kvh)] += dk_part[CHUNK:]
            dv_ref[cur_rows, _lane_block(kvh)] += dv_part[CHUNK:]

            @pl.when(i > 0)
            def _(kvh=kvh, dk_part=dk_part, dv_part=dv_part):
                prev_rows = pl.ds(pl.multiple_of((i - 1) * CHUNK, CHUNK), CHUNK)
                dk_ref[prev_rows, _lane_block(kvh)] += dk_part[:CHUNK]
                dv_ref[prev_rows, _lane_block(kvh)] += dv_part[:CHUNK]

            sink_grad = psink * delta
            for r in range(Q_PER_KV):
                dsink = -jnp.sum(sink_grad[:, r * CHUNK:(r + 1) * CHUNK], axis=1, keepdims=True)
                dsk_row = dsk_row + jnp.where(lane == kvh * Q_PER_KV + r, dsink, 0.0)
            blocks = zip(_unstack_heads(ot), _unstack_heads(dqt), dys, gates)
            for b, (ob, dqr, dyv, (_, dsilu)) in enumerate(blocks):
                blk = kvh * BLOCKS_PER_KV + b
                dq = dqr * cos + _swap_halves(dqr * sin)
                dbq_ref[:, _lane_block(blk)] += jnp.sum(dq, axis=0, keepdims=True)
                dz_ref[:, _lane_block(blk)] = dq.astype(BF16)
                dz_ref[:, _lane_block(B_WIDTH // 128 + blk)] = (dyv * ob * dsilu).astype(BF16)
        dsk_ref[0:1, :] += dsk_row

    qspec, gspec, prev, cur, tab, bq, sk = _attn_specs()
    full = pl.BlockSpec((s, 2 * KV_WIDTH), lambda i: (0, 0))
    outs, _ = _call(
        body, grid=(s // CHUNK,),
        in_specs=[qspec, gspec, qspec, prev, cur, prev, cur, tab, tab, bq, sk],
        out_specs=[pl.BlockSpec((CHUNK, 2 * B_WIDTH), lambda i: (i, 0)), full, full, bq,
                   pl.BlockSpec((8, 128), lambda i: (0, 0))],
        out_shape=[jax.ShapeDtypeStruct((s, 2 * B_WIDTH), BF16), jax.ShapeDtypeStruct((s, 2 * KV_WIDTH), F32),
                   jax.ShapeDtypeStruct((s, 2 * KV_WIDTH), F32), jax.ShapeDtypeStruct((1, B_WIDTH), F32),
                   jax.ShapeDtypeStruct((8, 128), F32)],
        args=(zb, zb, dyb, k2, k2, v2, v2, cos, sin, b_bq, sinks), name="attn_bwd")
    return outs


def _place():
    x, y, c = lax.axis_index("x"), lax.axis_index("y"), lax.axis_index("c")
    return x, y, c, [(1 - x, y), (x, 1 - y), (1 - x, 1 - y)]


def _relations():
    return [(r >> 2 & 1, r >> 1 & 1, r & 1) for r in range(1, 8)]


def _gather_side(arrs):
    n = len(arrs)

    def copies(ins, outs, sems):
        send_ici, recv_ici, send_d2d, recv_d2d, local_sem = sems
        x, y, c, chips = _place()
        me = 2 * x + y

        def rows(a, half):
            hr = arrs[a].shape[0] // 2
            return pl.ds(half * hr, hr)

        def ici(a, j, src_chip, to):
            return pltpu.make_async_remote_copy(
                src_ref=ins[a].at[rows(a, c)], dst_ref=outs[a].at[src_chip, rows(a, c)],
                send_sem=send_ici.at[a, j], recv_sem=recv_ici.at[a, j], device_id=to, device_id_type=MESH)

        def d2d(a, j, chip, half):
            blk = outs[a].at[chip, rows(a, half)]
            return pltpu.make_async_remote_copy(
                src_ref=blk, dst_ref=blk, send_sem=send_d2d.at[a, j], recv_sem=recv_d2d.at[a, j],
                device_id=(x, y, 1 - c), device_id_type=MESH)

        local = [pltpu.make_async_copy(ins[a], outs[a].at[me], local_sem.at[a]) for a in range(n)]
        pairs = [(a, j, chip) for a in range(n) for j, chip in enumerate(chips)]
        return c, me, local, ici, d2d, pairs

    def start(ins, outs, sems):
        c, me, local, ici, _, pairs = copies(ins, outs, sems)
        for cp in local:
            cp.start()
        for a, j, chip in pairs:
            ici(a, j, me, (*chip, c)).start()

    def finish(ins, outs, sems):
        c, me, local, ici, d2d, pairs = copies(ins, outs, sems)
        for a, j, (px, py) in pairs:
            ici(a, j, 2 * px + py, (px, py, c)).wait_recv()
            d2d(a, j, 2 * px + py, c).start()
        for a, j, (px, py) in pairs:
            d2d(a, j, 2 * px + py, 1 - c).wait_recv()
        for a, j, (px, py) in pairs:
            ici(a, j, me, (px, py, c)).wait_send()
            d2d(a, j, 2 * px + py, c).wait_send()
        for cp in local:
            cp.wait()

    return _Side(arrs, [jax.ShapeDtypeStruct((N_CHIPS,) + a.shape, a.dtype) for a in arrs],
                 [pltpu.SemaphoreType.DMA((n, 3))] * 4 + [pltpu.SemaphoreType.DMA((n,))], start, finish)


def _exchange_side(grads):
    n = len(grads)

    def copies(ins, outs, sems):
        send_sem, recv_sem = sems
        x, y, c, _ = _place()
        cps = []
        for a in range(n):
            hr = grads[a].shape[1] // 2
            cps.append(pltpu.make_async_remote_copy(
                src_ref=ins[a].at[:, pl.ds((1 - c) * hr, hr), :], dst_ref=outs[a],
                send_sem=send_sem.at[a], recv_sem=recv_sem.at[a], device_id=(x, y, 1 - c), device_id_type=MESH))
        return cps

    def start(ins, outs, sems):
        for cp in copies(ins, outs, sems):
            cp.start()

    def finish(ins, outs, sems):
        for cp in copies(ins, outs, sems):
            cp.wait()

    return _Side(grads, [jax.ShapeDtypeStruct((g.shape[0], g.shape[1] // 2, g.shape[2]), g.dtype) for g in grads],
                 [pltpu.SemaphoreType.DMA((n,))] * 2, start, finish)


def _scatter_side(chip_sums, small=None):
    n = len(chip_sums)
    arrs = list(chip_sums) + ([small] if small is not None else [])

    def copies(ins, outs, sems):
        x, y, c, chips = _place()
        cps = []
        for a in range(n):
            for j, (px, py) in enumerate(chips):
                cps.append(pltpu.make_async_remote_copy(
                    src_ref=ins[a].at[2 * px + py], dst_ref=outs[a].at[j],
                    send_sem=sems[0].at[a, j], recv_sem=sems[1].at[a, j], device_id=(px, py, c), device_id_type=MESH))
        if small is not None:
            for r, (fx, fy, fc) in enumerate(_relations(), start=1):
                px, py, pc = x ^ fx, y ^ fy, c ^ fc
                cps.append(pltpu.make_async_remote_copy(
                    src_ref=ins[n].at[4 * px + 2 * py + pc], dst_ref=outs[n].at[r],
                    send_sem=sems[2].at[r - 1], recv_sem=sems[3].at[r - 1], device_id=(px, py, pc),
                    device_id_type=MESH))
        return cps

    def start(ins, outs, sems):
        for cp in copies(ins, outs, sems):
            cp.start()

    def finish(ins, outs, sems):
        for cp in copies(ins, outs, sems):
            cp.wait()

    shapes = [jax.ShapeDtypeStruct((3,) + t.shape[1:], t.dtype) for t in chip_sums]
    sems = [pltpu.SemaphoreType.DMA((n, 3))] * 2
    if small is not None:
        shapes.append(jax.ShapeDtypeStruct(small.shape, small.dtype))
        sems += [pltpu.SemaphoreType.DMA((7,))] * 2
    return _Side(arrs, shapes, sems, start, finish)


def _small_scatter_side(small):
    def copies(ins, outs, sems):
        x, y, c, _ = _place()
        cps = []
        for r, (fx, fy, fc) in enumerate(_relations(), start=1):
            px, py, pc = x ^ fx, y ^ fy, c ^ fc
            cps.append(pltpu.make_async_remote_copy(
                src_ref=ins[0].at[4 * px + 2 * py + pc], dst_ref=outs[0].at[r],
                send_sem=sems[0].at[r - 1], recv_sem=sems[1].at[r - 1], device_id=(px, py, pc), device_id_type=MESH))
        return cps

    def start(ins, outs, sems):
        for cp in copies(ins, outs, sems):
            cp.start()

    def finish(ins, outs, sems):
        for cp in copies(ins, outs, sems):
            cp.wait()

    return _Side([small], [jax.ShapeDtypeStruct(small.shape, small.dtype)], [pltpu.SemaphoreType.DMA((7,))] * 2,
                 start, finish)


def _share_side(halves, small=None):
    n = len(halves)
    arrs = list(halves) + ([small] if small is not None else [])

    def copies(ins, outs, sems, mine):
        x, y, c, _ = _place()
        me = 4 * x + 2 * y + c
        cps = []
        for a in range(n):
            hr = halves[a].shape[0] // 2
            rows = pl.ds((c if mine else 1 - c) * hr, hr)
            cps.append(pltpu.make_async_remote_copy(
                src_ref=ins[a].at[rows], dst_ref=outs[a].at[rows],
                send_sem=sems[0].at[a], recv_sem=sems[1].at[a], device_id=(x, y, 1 - c), device_id_type=MESH))
        if small is not None:
            for r, (fx, fy, fc) in enumerate(_relations(), start=1):
                px, py, pc = x ^ fx, y ^ fy, c ^ fc
                seg = me if mine else 4 * px + 2 * py + pc
                cps.append(pltpu.make_async_remote_copy(
                    src_ref=ins[n].at[seg], dst_ref=outs[n].at[seg],
                    send_sem=sems[2].at[r - 1], recv_sem=sems[3].at[r - 1], device_id=(px, py, pc),
                    device_id_type=MESH))
        return cps

    def start(ins, outs, sems):
        for cp in copies(ins, outs, sems, True):
            cp.start()

    def finish(ins, outs, sems):
        for cp in copies(ins, outs, sems, False):
            cp.wait_recv()
        for cp in copies(ins, outs, sems, True):
            cp.wait_send()

    sems = [pltpu.SemaphoreType.DMA((n,))] * 2 + ([pltpu.SemaphoreType.DMA((7,))] * 2 if small is not None else [])
    return _Side(arrs, [jax.ShapeDtypeStruct(h.shape, h.dtype) for h in arrs], sems, start, finish,
                 aliases={i: i for i in range(len(arrs))})


GATHER_PIECES = [(0, 0), (0, 1), (1, 0), (2, 0), (1, 1), (2, 1), (3, 0), (3, 1)]


def _mm_gathering(a, shard, order, *, name, tm=1024):
    s, k = a.shape
    nc = shard.shape[1]
    tm = _row_tile(s, tm)
    tn = nc // 2
    hr = k // 2
    qr = hr // 2
    blocks = jnp.stack([order[src] * 2 + h for src, h in GATHER_PIECES]).astype(jnp.int32)
    halves = jnp.array([h for _, h in GATHER_PIECES], jnp.int32)

    def body(blocks_ref, halves_ref, a_ref, shard_ref, z_ref, full_ref, wbuf, send_ici, recv_ici, send_relay,
             recv_relay, send_d2d, recv_d2d, local_sem, load_sem):
        piece, i = pl.program_id(0), pl.program_id(1)
        x, y, c, chips = _place()
        me = 2 * x + y
        nbrs = chips[:2]
        chip_of = [2 * px + py for px, py in chips]

        def quarter(q):
            return pl.ds(c * hr + q * qr, qr)

        def sibling_quarter(q):
            return pl.ds((1 - c) * hr + q * qr, qr)

        def whole(half):
            return pl.ds(half * hr, hr)

        def cols(h):
            return pl.ds(h * tn, tn)

        def direct(j, src_chip, h):
            return pltpu.make_async_remote_copy(
                src_ref=shard_ref.at[whole(c), cols(h)], dst_ref=full_ref.at[src_chip, whole(c), cols(h)],
                send_sem=send_ici.at[j, h], recv_sem=recv_ici.at[j, h], device_id=(*nbrs[j], c), device_id_type=MESH)

        def relay(j, src_chip, h):
            blk = full_ref.at[src_chip, quarter(j), cols(h)]
            return pltpu.make_async_remote_copy(
                src_ref=blk, dst_ref=blk, send_sem=send_relay.at[j, h], recv_sem=recv_relay.at[j, h],
                device_id=(*nbrs[1 - j], c), device_id_type=MESH)

        def d2d(j, chip, rows, h):
            blk = full_ref.at[chip, rows, cols(h)]
            return pltpu.make_async_remote_copy(
                src_ref=blk, dst_ref=blk, send_sem=send_d2d.at[j, h], recv_sem=recv_d2d.at[j, h],
                device_id=(x, y, 1 - c), device_id_type=MESH)

        def load(src, h):
            cp = pltpu.make_async_copy(src.at[:, cols(h)], wbuf.at[h], load_sem.at[h])
            cp.start()
            cp.wait()

        local = pltpu.make_async_copy(shard_ref, full_ref.at[me], local_sem)

        def own(h):
            if h == 0:
                local.start()
                for hh in range(2):
                    for j in range(2):
                        direct(j, me, hh).start()
            load(shard_ref, h)

        def neighbour(j, h):
            direct(j, chip_of[j], h).wait_recv()
            relay(j, chip_of[j], h).start()
            d2d(j, chip_of[j], whole(c), h).start()
            d2d(j, chip_of[j], whole(1 - c), h).wait_recv()
            load(full_ref.at[chip_of[j]], h)

        def diagonal(h):
            for j in range(2):
                relay(1 - j, chip_of[2], h).wait_recv()
                d2d(2 + j, chip_of[2], quarter(1 - j), h).start()
            for j in range(2):
                d2d(2 + j, chip_of[2], sibling_quarter(1 - j), h).wait_recv()
            load(full_ref.at[chip_of[2]], h)

        for p, (src, h) in enumerate(GATHER_PIECES):
            @pl.when(jnp.logical_and(piece == p, i == 0))
            def _(src=src, h=h):
                if src == 0:
                    own(h)
                elif src == 3:
                    diagonal(h)
                else:
                    neighbour(src - 1, h)

        z_ref[...] = jnp.dot(a_ref[...], wbuf[halves_ref[piece]], preferred_element_type=F32).astype(z_ref.dtype)

        last = jnp.logical_and(piece == len(GATHER_PIECES) - 1, i == s // tm - 1)

        @pl.when(last)
        def _():
            for h in range(2):
                for j in range(2):
                    direct(j, me, h).wait_send()
                    relay(j, chip_of[j], h).wait_send()
                    d2d(j, chip_of[j], whole(c), h).wait_send()
                    d2d(2 + j, chip_of[2], quarter(1 - j), h).wait_send()
            local.wait()

    return pl.pallas_call(
        body,
        grid_spec=pltpu.PrefetchScalarGridSpec(
            num_scalar_prefetch=2, grid=(len(GATHER_PIECES), s // tm),
            in_specs=[pl.BlockSpec((tm, k), lambda p, i, blocks, halves: (i, 0)), HBM],
            out_specs=[pl.BlockSpec((tm, tn), lambda p, i, blocks, halves: (i, blocks[p])), HBM],
            scratch_shapes=[pltpu.VMEM((2, k, tn), BF16)] + [pltpu.SemaphoreType.DMA((2, 2))] * 4
            + [pltpu.SemaphoreType.DMA((4, 2))] * 2 + [pltpu.SemaphoreType.DMA, pltpu.SemaphoreType.DMA((2,))]),
        out_shape=[jax.ShapeDtypeStruct((s, N_CHIPS * nc), BF16), jax.ShapeDtypeStruct((N_CHIPS, k, nc), BF16)],
        name=name, compiler_params=_cparams(),
    )(blocks, halves, a, shard)


def _col_tile(cols):
    return cols if cols <= 2048 else 512


def _add_sibling(grad, recv, core, *, name):
    k, r, c = grad.shape
    hr = r // 2
    tr = min(hr, 256)
    tc = _col_tile(c)
    nrb = hr // tr

    def body(core_ref, g_ref, r_ref, o_ref):
        o_ref[...] = (g_ref[...] + r_ref[...]).astype(BF16)

    return pl.pallas_call(
        body,
        grid_spec=pltpu.PrefetchScalarGridSpec(
            num_scalar_prefetch=1, grid=(k, nrb, c // tc),
            in_specs=[pl.BlockSpec((None, tr, tc), lambda kk, i, j, core: (kk, core[0] * nrb + i, j)),
                      pl.BlockSpec((None, tr, tc), lambda kk, i, j, core: (kk, i, j))],
            out_specs=pl.BlockSpec((None, tr, tc), lambda kk, i, j, core: (kk, i, j))),
        out_shape=jax.ShapeDtypeStruct((k, hr, c), BF16), name=name, compiler_params=_cparams(),
    )(core, grad, recv)


def _sum_chips(grad, from_sibling, recv, place, *, name):
    _, hr, c = from_sibling.shape
    tr = min(hr, 256)
    tc = _col_tile(c)
    nrb = hr // tr

    def body(place_ref, g_ref, s_ref, r0_ref, r1_ref, r2_ref, o_ref):
        own = g_ref[...] + s_ref[...]
        o_ref[...] = ((own + r0_ref[...].astype(F32)) + r1_ref[...].astype(F32)) + r2_ref[...].astype(F32)

    def rspec(j):
        return pl.BlockSpec((None, tr, tc), lambda i, jj, place: (j, i, jj))

    return pl.pallas_call(
        body,
        grid_spec=pltpu.PrefetchScalarGridSpec(
            num_scalar_prefetch=1, grid=(nrb, c // tc),
            in_specs=[pl.BlockSpec((None, tr, tc), lambda i, jj, place: (place[0], place[1] * nrb + i, jj)),
                      pl.BlockSpec((None, tr, tc), lambda i, jj, place: (place[0], i, jj)),
                      rspec(0), rspec(1), rspec(2)],
            out_specs=pl.BlockSpec((tr, tc), lambda i, jj, place: (place[1] * nrb + i, jj))),
        out_shape=jax.ShapeDtypeStruct((2 * hr, c), F32), name=name, compiler_params=_cparams(),
    )(place, grad, from_sibling, recv, recv, recv)


def _sum_small(small, recv, place):
    _, sr, _ = small.shape

    def body(place_ref, own_ref, r_ref, o_ref):
        acc = own_ref[...]
        for r in range(1, 8):
            acc = acc + r_ref[r]
        o_ref[...] = acc

    return pl.pallas_call(
        body,
        grid_spec=pltpu.PrefetchScalarGridSpec(
            num_scalar_prefetch=1, grid=(1,),
            in_specs=[pl.BlockSpec((None, sr, 128), lambda i, place: (place[2], 0, 0)),
                      pl.BlockSpec((8, sr, 128), lambda i, place: (0, 0, 0))],
            out_specs=pl.BlockSpec((None, sr, 128), lambda i, place: (place[2], 0, 0))),
        out_shape=jax.ShapeDtypeStruct(small.shape, F32), name="sum_small", compiler_params=_cparams(),
    )(place, small, recv)


def _spread_side(vec):
    def copies(ins, outs, sems):
        x, y, c, _ = _place()
        return [pltpu.make_async_remote_copy(
            src_ref=ins[0], dst_ref=outs[0].at[r], send_sem=sems[0].at[r - 1], recv_sem=sems[1].at[r - 1],
            device_id=(x ^ fx, y ^ fy, c ^ fc), device_id_type=MESH)
            for r, (fx, fy, fc) in enumerate(_relations(), start=1)]

    def start(ins, outs, sems):
        for cp in copies(ins, outs, sems):
            cp.start()

    def finish(ins, outs, sems):
        for cp in copies(ins, outs, sems):
            cp.wait()

    return _Side([vec], [jax.ShapeDtypeStruct((8,) + vec.shape, vec.dtype)], [pltpu.SemaphoreType.DMA((7,))] * 2,
                 start, finish)


def _sum_in_device_order(own, spread, place):
    def body(place_ref, own_ref, r_ref, o_ref):
        me = place_ref[2]
        acc = jnp.zeros_like(own_ref[...])
        for d in range(8):
            slot = jnp.where(me == d, 1, me ^ d)
            acc = acc + jnp.where(me == d, own_ref[...], r_ref[slot])
        o_ref[...] = acc

    return pl.pallas_call(
        body,
        grid_spec=pltpu.PrefetchScalarGridSpec(
            num_scalar_prefetch=1, grid=(1,),
            in_specs=[pl.BlockSpec(own.shape, lambda i, place: (0, 0)),
                      pl.BlockSpec(spread.shape, lambda i, place: (0, 0, 0))],
            out_specs=pl.BlockSpec(own.shape, lambda i, place: (0, 0))),
        out_shape=jax.ShapeDtypeStruct(own.shape, F32), name="sum_in_device_order", compiler_params=_cparams(),
    )(place, own, spread)


def _adamw(w, g, m, v, *, name):
    r, c = w.shape
    tr = 256 if r % 256 == 0 else r
    tc = _col_tile(c)
    bc1 = 1.0 - ADAM_B1 ** ADAM_STEP
    bc2 = 1.0 - ADAM_B2 ** ADAM_STEP

    def body(w_ref, g_ref, m_ref, v_ref, d_ref, nm_ref, nv_ref):
        gv = g_ref[...]
        nm = ADAM_B1 * m_ref[...] + (1.0 - ADAM_B1) * gv
        nv = ADAM_B2 * v_ref[...] + (1.0 - ADAM_B2) * (gv * gv)
        d_ref[...] = -ADAM_LR * ((nm / bc1) / (jnp.sqrt(nv / bc2) + ADAM_EPS) + ADAM_WD * w_ref[...])
        nm_ref[...] = nm
        nv_ref[...] = nv

    spec = pl.BlockSpec((tr, tc), lambda i, j: (i, j))
    outs, _ = _call(body, grid=(r // tr, c // tc), in_specs=[spec] * 4, out_specs=[spec] * 3,
                    out_shape=[jax.ShapeDtypeStruct((r, c), F32)] * 3, args=(w, g, m, v), name=name)
    return outs


SMALL_ORDER = ["a_ws", "a_bs", "a_norm_g", "a_ln_g", "a_ln_b", "kv_norm_g", "b_kv", "b_norm_g", "b_bq",
               "b_sinks", "final_norm_g"]
SHARDED_SMALL = {"a_norm_g", "a_ln_g", "a_ln_b"}
PACK_TILE = 8 * 128


def _rows128(a):
    flat = a.reshape(-1)
    return jnp.pad(flat, (0, (-flat.shape[0]) % PACK_TILE)).reshape(-1, 128)


def _pack_rows(parts, multiple):
    rows = [_rows128(p) for p in parts]
    total = sum(r.shape[0] for r in rows)
    pad = (-total) % multiple
    if pad:
        rows.append(jnp.zeros((pad, 128), rows[0].dtype))
    return jnp.concatenate(rows, axis=0)


def _unpack_rows(packed, shapes):
    out, row = [], 0
    for shp in shapes:
        size = math.prod(shp)
        nrow = -(-size // PACK_TILE) * 8
        out.append(packed[row:row + nrow].reshape(-1)[:size].reshape(shp))
        row += nrow
    return out


WEIGHTS = ["a_norm_g", "a_w_in", "a_ln_g", "a_ln_b", "a_ws", "a_bs", "a_w_out", "kv_norm_g", "w_kv", "b_kv",
           "b_norm_g", "b_w_in", "b_bq", "b_sinks", "b_w_out", "final_norm_g"]
BIG = ["a_w_in", "a_w_out", "w_kv", "b_w_in", "b_w_out"]


class _Reduction:
    def __init__(self, names, partials, core, place, small=None):
        self.names, self.partials, self.core, self.place, self.small = names, partials, core, place, small

    def exchange_side(self):
        return _exchange_side(self.partials)

    def took_exchange(self, from_sibling):
        self.from_sibling = from_sibling
        self.chip_sums = [_add_sibling(g, r, self.core, name="add_sibling_" + n)
                          for g, r, n in zip(self.partials, from_sibling, self.names)]

    def scatter_side(self):
        return _scatter_side(self.chip_sums, self.small)

    def took_scatter(self, arrived):
        big = arrived[:len(self.names)]
        self.halves = [_sum_chips(g, fs, r, self.place, name="sum_chips_" + n)
                       for g, fs, r, n in zip(self.partials, self.from_sibling, big, self.names)]
        self.small_mine = _sum_small(self.small, arrived[-1], self.place) if self.small is not None else None

    def share_side(self):
        return _share_side(self.halves, self.small_mine)

    def took_share(self, shared):
        self.grads = dict(zip(self.names, shared[:len(self.names)]))
        self.small_full = shared[-1] if self.small is not None else None


def _step(x, loss_target, p, m, v):
    xi, yi, ci = lax.axis_index("x"), lax.axis_index("y"), lax.axis_index("c")
    chip = 2 * xi + yi
    device = 4 * xi + 2 * yi + ci
    core = jnp.reshape(ci, (1,)).astype(jnp.int32)
    place = jnp.stack([chip, ci, device]).astype(jnp.int32)
    x, tgt = x[0], loss_target[0]
    s = x.shape[0]
    cos, sin = _rope_tables(s)

    shard2d = {n: p[n].reshape(p[n].shape[-2:]) for n in BIG}
    shard_bf = {n: shard2d[n].astype(BF16) for n in BIG}
    ws = p["a_ws"][0]
    ws_t = jnp.swapaxes(ws, 1, 2)
    bs_t = p["a_bs"][0].T
    kv_norm_g, b_kv = p["kv_norm_g"].reshape(1, -1), p["b_kv"].reshape(1, -1)
    final_norm_g = p["final_norm_g"].reshape(1, -1)

    vec_shapes = [p[n].shape for n in ("a_norm_g", "a_ln_g", "a_ln_b")]
    vec_pack = _pack_rows([p["a_norm_g"], p["a_ln_g"], p["a_ln_b"]], 16)
    (vec_all,) = _comm_call(_gather_side([vec_pack]), "gather_vectors")
    vecs = [_unpack_rows(vec_all[k], vec_shapes) for k in range(N_CHIPS)]
    a_norm_g, a_ln_g, a_ln_b = (jnp.concatenate([vk[t] for vk in vecs], axis=-1) for t in range(3))

    (n_a,) = _rms_fwd(x, [a_norm_g], name="rms_a")
    order = jnp.stack([chip, 2 * (1 - xi) + yi, 2 * xi + (1 - yi), 2 * (1 - xi) + (1 - yi)]).astype(jnp.int32)
    z, a_w_in = _mm_gathering(n_a, shard_bf["a_w_in"], order, name="mm_a_in")
    y, (a_w_out,) = _gate_fwd(z, a_ln_g, a_ln_b, ws, bs_t, side=_gather_side([shard_bf["a_w_out"]]))
    a_w_out = a_w_out.reshape(A_WIDTH, D_MODEL)
    (h1, n_kv, n_b), (w_kv, b_w_in) = _mm_residual_norms(
        y, a_w_out, x, [kv_norm_g, p["b_norm_g"]], name="mm_a_out",
        side=_gather_side([shard_bf["w_kv"], shard_bf["b_w_in"]]))
    w_kv = w_kv.reshape(D_MODEL, 2 * KV_WIDTH)
    kv = _mm_nn(n_kv, w_kv, name="mm_kv", tn=2 * KV_WIDTH)
    kr, vv = _kv_rope(kv, b_kv, cos, sin)
    zb = _mm_nn(n_b, b_w_in, name="mm_b_in", tn=512, tm=1024, out_dtype=BF16)
    yb, (b_w_out,) = _attn_fwd(zb, kr, vv, cos, sin, p["b_bq"], p["b_sinks"], side=_gather_side([shard_bf["b_w_out"]]))
    b_w_out = b_w_out.reshape(B_WIDTH, D_MODEL)
    loss_blk, dh2, dh2b, d_final_g = _mm_residual_loss(yb, b_w_out, h1, tgt, final_norm_g, name="mm_b_out")

    d_b_w_out = _mm_tn(yb, dh2b, name="mm_d_b_w_out", tm=B_WIDTH, tn=D_MODEL)
    red_bo = _Reduction(["b_w_out"], [d_b_w_out.reshape(N_CHIPS, B_WIDTH // N_CHIPS, D_MODEL)], core, place)
    dyb, got = _mm_nt(dh2b, b_w_out, name="mm_dyb", out_dtype=BF16, side=red_bo.exchange_side())
    red_bo.took_exchange(got)
    dzb, dk_rot, dv, d_bq, d_sinks = _attn_bwd(zb, dyb, kr, vv, cos, sin, p["b_bq"], p["b_sinks"])
    dkv, d_b_kv = _kv_rope_bwd(dk_rot, dv, cos, sin)
    d_b_w_in, got = _mm_tn(n_b, dzb, name="mm_d_b_w_in", tm=D_MODEL, tn=512, shards=N_CHIPS,
                           side=red_bo.scatter_side())
    red_bo.took_scatter(got)
    d_w_kv, got = _mm_tn(n_kv, dkv, name="mm_d_w_kv", tm=D_MODEL, tn=2 * KV_WIDTH, side=red_bo.share_side())
    red_bo.took_share(got)
    red_bi = _Reduction(["b_w_in", "w_kv"], [d_b_w_in, d_w_kv.reshape(N_CHIPS, D_MODEL // N_CHIPS, 2 * KV_WIDTH)],
                        core, place)
    (dh1, dh1b, d_kv_g, d_b_g), got = _mm_nt_rms_bwd(
        [(dkv, w_kv, kv_norm_g), (dzb, b_w_in, p["b_norm_g"])], h1, dh2, name="mm_dn_b", tm=512,
        side=red_bi.exchange_side())
    red_bi.took_exchange(got)

    d_a_w_out, got = _mm_tn(y, dh1b, name="mm_d_a_w_out", tm=1024, tn=D_MODEL, side=red_bi.scatter_side())
    red_bi.took_scatter(got)
    red_ao = _Reduction(["a_w_out"], [d_a_w_out.reshape(N_CHIPS, A_WIDTH // N_CHIPS, D_MODEL)], core, place)
    sides = [red_ao.exchange_side(), red_bi.share_side()]
    dy, got = _mm_nt(dh1b, a_w_out, name="mm_dy", tn=1024, out_dtype=BF16, side=_join(sides))
    got = _split(got, sides)
    red_ao.took_exchange(got[0])
    red_bi.took_share(got[1])
    (dz, d_ln_g, d_ln_b, d_ws, d_bs_t), got = _gate_bwd(z, dy, a_ln_g, a_ln_b, ws, ws_t, bs_t,
                                                        side=red_ao.scatter_side())
    red_ao.took_scatter(got)
    small = {
        "a_ws": d_ws, "a_bs": d_bs_t.T, "a_ln_g": d_ln_g, "a_ln_b": d_ln_b,
        "kv_norm_g": d_kv_g, "b_kv": d_b_kv, "b_norm_g": d_b_g, "b_bq": d_bq,
        "b_sinks": d_sinks[0:1, :N_Q_HEADS], "final_norm_g": d_final_g,
    }
    packed = [n for n in SMALL_ORDER if n != "a_norm_g"]
    small_shapes = [small[n].shape for n in packed] + [(1, 1)]
    small_pack = _pack_rows([small[n] for n in packed] + [loss_blk[0:1, 0:1]], 64)
    seg = small_pack.shape[0] // 8
    small_pack = small_pack.reshape(8, seg, 128)
    sides = [red_ao.share_side(), _small_scatter_side(small_pack)]
    d_a_w_in, got = _mm_tn(n_a, dz, name="mm_d_a_w_in", tm=D_MODEL, tn=1536, shards=N_CHIPS, side=_join(sides))
    got = _split(got, sides)
    red_ao.took_share(got[0])
    small_mine = _sum_small(small_pack, got[1][0], place)

    red_ai = _Reduction(["a_w_in"], [d_a_w_in], core, place)
    red_ai.took_exchange(_comm_call(red_ai.exchange_side(), "exchange_last"))
    (dx, _, d_a_g), got = _mm_nt_rms_bwd([(dz, a_w_in, a_norm_g)], x, dh1, name="mm_dn_a", tm=256,
                                         side=red_ai.scatter_side())
    red_ai.took_scatter(got)
    red_ai.small, red_ai.small_mine = small_pack, small_mine
    d_a_g = _rows128(d_a_g)
    sides = [red_ai.share_side(), _spread_side(d_a_g)]
    got = _split(_comm_call(_join(sides), "share_last"), sides)
    red_ai.took_share(got[0])
    small_full = dict(zip(packed + ["loss"], _unpack_rows(red_ai.small_full.reshape(8 * seg, 128), small_shapes)))
    small_full["a_norm_g"] = _sum_in_device_order(d_a_g, got[1][0], place).reshape(1, -1)
    loss = small_full["loss"].reshape(())

    grad_big = {**red_bo.grads, **red_bi.grads, **red_ao.grads, **red_ai.grads}
    grads = {}
    for n in SMALL_ORDER:
        gfull = small_full[n]
        if n in SHARDED_SMALL:
            width = p[n].shape[-1]
            gfull = lax.dynamic_slice_in_dim(gfull, chip * width, width, axis=-1)
        grads[n] = gfull.reshape(p[n].shape)
    for n in BIG:
        grads[n] = grad_big[n].reshape(p[n].shape)

    delta, new_m, new_v = {}, {}, {}
    for n in BIG:
        d, nm, nv = _adamw(shard2d[n], grad_big[n], m[n].reshape(shard2d[n].shape), v[n].reshape(shard2d[n].shape),
                           name="adamw_" + n)
        delta[n], new_m[n], new_v[n] = d.reshape(p[n].shape), nm.reshape(p[n].shape), nv.reshape(p[n].shape)
    shapes = [p[n].shape for n in SMALL_ORDER]
    packs = [_pack_rows([src[n] for n in SMALL_ORDER], 8) for src in (p, grads, m, v)]
    outs = _adamw(*packs, name="adamw_small")
    for res, packed in zip((delta, new_m, new_v), outs):
        for n, val in zip(SMALL_ORDER, _unpack_rows(packed, shapes)):
            res[n] = val

    return (loss, dx[None], *[grads[n] for n in WEIGHTS], *[delta[n] for n in WEIGHTS],
            *[new_m[n] for n in WEIGHTS], *[new_v[n] for n in WEIGHTS])


def kernel(x, a_norm_g, a_w_in, a_ln_g, a_ln_b, a_ws, a_bs, a_w_out, kv_norm_g, w_kv, b_kv, b_norm_g, b_w_in, b_bq, b_sinks, b_w_out, final_norm_g, loss_target, m_a_norm_g, m_a_w_in, m_a_ln_g, m_a_ln_b, m_a_ws, m_a_bs, m_a_w_out, m_kv_norm_g, m_w_kv, m_b_kv, m_b_norm_g, m_b_w_in, m_b_bq, m_b_sinks, m_b_w_out, m_final_norm_g, v_a_norm_g, v_a_w_in, v_a_ln_g, v_a_ln_b, v_a_ws, v_a_bs, v_a_w_out, v_kv_norm_g, v_w_kv, v_b_kv, v_b_norm_g, v_b_w_in, v_b_bq, v_b_sinks, v_b_w_out, v_final_norm_g):
    p = dict(a_norm_g=a_norm_g, a_w_in=a_w_in, a_ln_g=a_ln_g, a_ln_b=a_ln_b, a_ws=a_ws, a_bs=a_bs, a_w_out=a_w_out,
             kv_norm_g=kv_norm_g, w_kv=w_kv, b_kv=b_kv, b_norm_g=b_norm_g, b_w_in=b_w_in, b_bq=b_bq, b_sinks=b_sinks,
             b_w_out=b_w_out, final_norm_g=final_norm_g)
    m = dict(a_norm_g=m_a_norm_g, a_w_in=m_a_w_in, a_ln_g=m_a_ln_g, a_ln_b=m_a_ln_b, a_ws=m_a_ws, a_bs=m_a_bs,
             a_w_out=m_a_w_out, kv_norm_g=m_kv_norm_g, w_kv=m_w_kv, b_kv=m_b_kv, b_norm_g=m_b_norm_g, b_w_in=m_b_w_in,
             b_bq=m_b_bq, b_sinks=m_b_sinks, b_w_out=m_b_w_out, final_norm_g=m_final_norm_g)
    v = dict(a_norm_g=v_a_norm_g, a_w_in=v_a_w_in, a_ln_g=v_a_ln_g, a_ln_b=v_a_ln_b, a_ws=v_a_ws, a_bs=v_a_bs,
             a_w_out=v_a_w_out, kv_norm_g=v_kv_norm_g, w_kv=v_w_kv, b_kv=v_b_kv, b_norm_g=v_b_norm_g, b_w_in=v_b_w_in,
             b_bq=v_b_bq, b_sinks=v_b_sinks, b_w_out=v_b_w_out, final_norm_g=v_final_norm_g)
    return _step(x, loss_target, p, m, v)
```

```python
import functools
import math

import jax
import jax.numpy as jnp
from jax import lax
from jax.experimental import pallas as pl
from jax.experimental.pallas import tpu as pltpu

F32 = jnp.float32
BF16 = jnp.bfloat16

D_MODEL = 1024
CHUNK = 128
A_WIDTH = 2048
A_GROUPS = 16
HEAD_DIM = 64
N_Q_HEADS = 16
N_KV_HEADS = 2
Q_PER_KV = 8
B_WIDTH = 1024
KV_WIDTH = 128
ROPE_THETA = 10000.0
EPS = 1e-5
N_CHIPS = 4

ADAM_LR = 0.001
ADAM_B1 = 0.9
ADAM_B2 = 0.999
ADAM_EPS = 1e-08
ADAM_WD = 0.01
ADAM_STEP = 10

VMEM_LIMIT = 48 * 1024 * 1024
MESH = pl.DeviceIdType.MESH
NEG_BIG = -1e30
HBM = pl.BlockSpec(memory_space=pl.ANY)

NN = (((1,), (0,)), ((), ()))
NT = (((1,), (1,)), ((), ()))
TN = (((0,), (0,)), ((), ()))


def _cparams(**kw):
    return pltpu.CompilerParams(vmem_limit_bytes=VMEM_LIMIT, **kw)


class _Side:
    def __init__(self, ins, out_shapes, sems, start, finish, aliases=None):
        self.ins, self.out_shapes, self.sems = list(ins), list(out_shapes), list(sems)
        self.start, self.finish = start, finish
        self.aliases = dict(aliases or {})


def _join(sides):
    sides = [s for s in sides if s is not None]
    if not sides:
        return None
    offs, i, o, m = [], 0, 0, 0
    for s in sides:
        offs.append((i, o, m))
        i, o, m = i + len(s.ins), o + len(s.out_shapes), m + len(s.sems)

    def run(which):
        def go(ins, outs, sems):
            for s, (a, b, c) in zip(sides, offs):
                getattr(s, which)(ins[a:a + len(s.ins)], outs[b:b + len(s.out_shapes)], sems[c:c + len(s.sems)])
        return go

    aliases = {}
    for s, (a, b, _) in zip(sides, offs):
        aliases.update({a + k: b + v for k, v in s.aliases.items()})
    return _Side([x for s in sides for x in s.ins], [x for s in sides for x in s.out_shapes],
                 [x for s in sides for x in s.sems], run("start"), run("finish"), aliases)


def _split(side_outs, sides):
    out, pos = [], 0
    for s in sides:
        out.append(list(side_outs[pos:pos + len(s.out_shapes)]))
        pos += len(s.out_shapes)
    return out


def _call(body, *, grid, in_specs, out_specs, out_shape, args, name, scratch=(), side=None):
    in_specs, out_specs, out_shape, scratch = list(in_specs), list(out_specs), list(out_shape), list(scratch)
    if side is None:
        res = pl.pallas_call(body, grid=grid, in_specs=in_specs, out_specs=out_specs, out_shape=out_shape,
                             scratch_shapes=scratch, name=name, compiler_params=_cparams())(*args)
        return list(res), []
    n_in, n_out, n_sc = len(in_specs), len(out_specs), len(scratch)
    s_in, s_out = len(side.ins), len(side.out_shapes)

    def wrapped(*refs):
        ins, refs = refs[:n_in], refs[n_in:]
        side_ins, refs = refs[:s_in], refs[s_in:]
        outs, refs = refs[:n_out], refs[n_out:]
        side_outs, refs = refs[:s_out], refs[s_out:]
        scr, side_sems = refs[:n_sc], refs[n_sc:]
        ids = [pl.program_id(a) for a in range(len(grid))]
        first = functools.reduce(jnp.logical_and, [i == 0 for i in ids])
        last = functools.reduce(jnp.logical_and, [i == g - 1 for i, g in zip(ids, grid)])

        @pl.when(first)
        def _():
            side.start(side_ins, side_outs, side_sems)

        body(*ins, *outs, *scr)

        @pl.when(last)
        def _():
            side.finish(side_ins, side_outs, side_sems)

    res = pl.pallas_call(
        wrapped, grid=grid, in_specs=in_specs + [HBM] * s_in, out_specs=out_specs + [HBM] * s_out,
        out_shape=out_shape + side.out_shapes, scratch_shapes=scratch + side.sems,
        input_output_aliases={n_in + k: n_out + v for k, v in side.aliases.items()},
        name=name, compiler_params=_cparams(),
    )(*args, *side.ins)
    return list(res[:n_out]), list(res[n_out:])


def _comm_call(side, name):
    s_in, s_out = len(side.ins), len(side.out_shapes)

    def body(*refs):
        ins, outs, sems = refs[:s_in], refs[s_in:s_in + s_out], refs[s_in + s_out:]
        side.start(ins, outs, sems)
        side.finish(ins, outs, sems)

    return list(pl.pallas_call(
        body, in_specs=[HBM] * s_in, out_specs=[HBM] * s_out, out_shape=side.out_shapes, scratch_shapes=side.sems,
        input_output_aliases=side.aliases, name=name,
    )(*side.ins))


def _matmul(a, b, *, dims, grid, a_spec, b_spec, o_spec, out_shape, name, acc_axis=None,
            residual=None, r_spec=None, side=None):
    has_res = residual is not None

    def body(*refs):
        if has_res:
            a_ref, b_ref, r_ref, o_ref = refs
        else:
            a_ref, b_ref, o_ref = refs
        part = lax.dot_general(a_ref[...], b_ref[...], dims, preferred_element_type=F32)
        if acc_axis is None:
            if has_res:
                part = part + r_ref[...]
            o_ref[...] = part.astype(o_ref.dtype)
        else:
            k = pl.program_id(acc_axis)

            @pl.when(k == 0)
            def _():
                o_ref[...] = part

            @pl.when(k > 0)
            def _():
                o_ref[...] += part

    in_specs = [a_spec, b_spec] + ([r_spec] if has_res else [])
    args = (a, b) + ((residual,) if has_res else ())
    (out,), side_outs = _call(body, grid=grid, in_specs=in_specs, out_specs=[o_spec], out_shape=[out_shape],
                              args=args, name=name, side=side)
    return (out, side_outs) if side is not None else out


def _row_tile(s, want):
    return min(s, want)


def _mm_nn(a, b, *, name, tn, out_dtype=F32, residual=None, tm=512, side=None):
    s, k = a.shape
    tm = _row_tile(s, tm)
    if b.ndim == 3:
        nsh, _, nc = b.shape
        npb = nc // tn
        n = nsh * nc
        b_spec = pl.BlockSpec((None, k, tn), lambda i, j: (j // npb, 0, j % npb))
    else:
        n = b.shape[1]
        b_spec = pl.BlockSpec((k, tn), lambda i, j: (0, j))
    return _matmul(
        a, b, dims=NN, grid=(s // tm, n // tn),
        a_spec=pl.BlockSpec((tm, k), lambda i, j: (i, 0)), b_spec=b_spec,
        o_spec=pl.BlockSpec((tm, tn), lambda i, j: (i, j)),
        out_shape=jax.ShapeDtypeStruct((s, n), out_dtype), name=name, side=side,
        residual=residual, r_spec=pl.BlockSpec((tm, tn), lambda i, j: (i, j)) if residual is not None else None)


def _mm_nt(a, b, *, name, tn=None, tm=512, out_dtype=F32, side=None):
    s, k = a.shape
    tm = _row_tile(s, tm)
    n = b.shape[0]
    tn = n if tn is None else tn
    return _matmul(
        a, b, dims=NT, grid=(s // tm, n // tn),
        a_spec=pl.BlockSpec((tm, k), lambda i, j: (i, 0)),
        b_spec=pl.BlockSpec((tn, k), lambda i, j: (j, 0)),
        o_spec=pl.BlockSpec((tm, tn), lambda i, j: (i, j)),
        out_shape=jax.ShapeDtypeStruct((s, n), out_dtype), name=name, side=side)


def _mm_tn(a, b, *, name, tm, tn, tk=2048, shards=None, side=None):
    s, m = a.shape
    n = b.shape[1]
    tk = _row_tile(s, tk)
    if shards is None:
        o_spec = pl.BlockSpec((tm, tn), lambda i, j, kk: (i, j))
        out_shape = jax.ShapeDtypeStruct((m, n), F32)
    else:
        assert tm == m
        nc = n // shards
        npb = nc // tn
        o_spec = pl.BlockSpec((None, m, tn), lambda i, j, kk: (j // npb, 0, j % npb))
        out_shape = jax.ShapeDtypeStruct((shards, m, nc), F32)
    return _matmul(
        a, b, dims=TN, grid=(m // tm, n // tn, s // tk), acc_axis=2,
        a_spec=pl.BlockSpec((tk, tm), lambda i, j, kk: (kk, i)),
        b_spec=pl.BlockSpec((tk, tn), lambda i, j, kk: (kk, j)),
        o_spec=o_spec, out_shape=out_shape, name=name, side=side)


def _rstd(x):
    return lax.rsqrt(jnp.mean(x * x, axis=-1, keepdims=True) + EPS)


def _rms_fwd(x, gains, *, name, tr=256):
    s, d = x.shape
    tr = _row_tile(s, tr)
    ng = len(gains)

    def body(*refs):
        xv = refs[0][...]
        xh = xv * _rstd(xv)
        for t in range(ng):
            refs[1 + ng + t][...] = (xh * refs[1 + t][...]).astype(BF16)

    row = pl.BlockSpec((tr, d), lambda i: (i, 0))
    vec = pl.BlockSpec((1, d), lambda i: (0, 0))
    outs, _ = _call(body, grid=(s // tr,), in_specs=[row] + [vec] * ng, out_specs=[row] * ng,
                    out_shape=[jax.ShapeDtypeStruct((s, d), BF16)] * ng, args=(x, *gains), name=name)
    return outs


def _accumulate(i, ref, value):
    @pl.when(i == 0)
    def _():
        ref[...] = value

    @pl.when(i > 0)
    def _():
        ref[...] += value


def _mm_residual_norms(y, w, res, gains, *, name, tm=512, side=None):
    s, k = y.shape
    d = w.shape[1]
    tm = _row_tile(s, tm)
    ng = len(gains)

    def body(y_ref, w_ref, r_ref, *rest):
        g_refs, h_ref, n_refs = rest[:ng], rest[ng], rest[ng + 1:]
        h = r_ref[...] + jnp.dot(y_ref[...], w_ref[...], preferred_element_type=F32)
        h_ref[...] = h
        xh = h * _rstd(h)
        for t in range(ng):
            n_refs[t][...] = (xh * g_refs[t][...]).astype(BF16)

    row = pl.BlockSpec((tm, d), lambda i: (i, 0))
    vec = pl.BlockSpec((1, d), lambda i: (0, 0))
    return _call(
        body, grid=(s // tm,),
        in_specs=[pl.BlockSpec((tm, k), lambda i: (i, 0)), pl.BlockSpec((k, d), lambda i: (0, 0)), row] + [vec] * ng,
        out_specs=[row] * (1 + ng),
        out_shape=[jax.ShapeDtypeStruct((s, d), F32)] + [jax.ShapeDtypeStruct((s, d), BF16)] * ng,
        args=(y, w, res, *gains), name=name, side=side)


def _mm_residual_loss(y, w, res, tgt, gain, *, name, tm=512):
    s, k = y.shape
    d = w.shape[1]
    tm = _row_tile(s, tm)

    def body(y_ref, w_ref, r_ref, t_ref, g_ref, loss_ref, dh_ref, dhb_ref, dg_ref):
        i = pl.program_id(0)
        hv = r_ref[...] + jnp.dot(y_ref[...], w_ref[...], preferred_element_type=F32)
        g = g_ref[...]
        r = _rstd(hv)
        xh = hv * r
        diff = xh * g - t_ref[...]
        part = 0.5 / d * jnp.sum(jnp.sum(diff * diff, axis=-1, keepdims=True), axis=0, keepdims=True)
        dout = diff * (1.0 / d)
        a = dout * g
        dh = r * (a - xh * jnp.mean(a * xh, axis=-1, keepdims=True))
        dh_ref[...] = dh
        dhb_ref[...] = dh.astype(BF16)
        _accumulate(i, dg_ref, jnp.sum(dout * xh, axis=0, keepdims=True))
        _accumulate(i, loss_ref, jnp.broadcast_to(part, (8, 128)))

    row = pl.BlockSpec((tm, d), lambda i: (i, 0))
    vec = pl.BlockSpec((1, d), lambda i: (0, 0))
    outs, _ = _call(
        body, grid=(s // tm,),
        in_specs=[pl.BlockSpec((tm, k), lambda i: (i, 0)), pl.BlockSpec((k, d), lambda i: (0, 0)), row, row, vec],
        out_specs=[pl.BlockSpec((8, 128), lambda i: (0, 0)), row, row, vec],
        out_shape=[jax.ShapeDtypeStruct((8, 128), F32), jax.ShapeDtypeStruct((s, d), F32),
                   jax.ShapeDtypeStruct((s, d), BF16), jax.ShapeDtypeStruct((1, d), F32)],
        args=(y, w, res, tgt, gain), name=name)
    return outs


def _mm_nt_rms_bwd(terms, x, dres, *, name, tm, side=None):
    s, d = x.shape
    tm = _row_tile(s, tm)
    nt = len(terms)

    def body(*refs):
        a_refs, b_refs, g_refs = refs[0:3 * nt:3], refs[1:3 * nt:3], refs[2:3 * nt:3]
        x_ref, dres_ref = refs[3 * nt], refs[3 * nt + 1]
        dx_ref, dxb_ref = refs[3 * nt + 2], refs[3 * nt + 3]
        dg_refs = refs[3 * nt + 4:]
        i = pl.program_id(0)
        xv = x_ref[...]
        r = _rstd(xv)
        xh = xv * r
        acc = jnp.zeros_like(xv)
        for t in range(nt):
            b_ref = b_refs[t]
            if len(b_ref.shape) == 3:
                kc = b_ref.shape[2]
                dn = None
                for sh in range(b_ref.shape[0]):
                    part = lax.dot_general(a_refs[t][:, sh * kc:(sh + 1) * kc], b_ref[sh], NT, preferred_element_type=F32)
                    dn = part if dn is None else dn + part
            else:
                dn = lax.dot_general(a_refs[t][...], b_ref[...], NT, preferred_element_type=F32)
            acc = acc + dn * g_refs[t][...]
            _accumulate(i, dg_refs[t], jnp.sum(dn * xh, axis=0, keepdims=True))
        dx = dres_ref[...] + r * (acc - xh * jnp.mean(acc * xh, axis=-1, keepdims=True))
        dx_ref[...] = dx
        dxb_ref[...] = dx.astype(BF16)

    row = pl.BlockSpec((tm, d), lambda i: (i, 0))
    vec = pl.BlockSpec((1, d), lambda i: (0, 0))
    in_specs, args = [], []
    for a, b, g in terms:
        in_specs += [pl.BlockSpec((tm, a.shape[1]), lambda i: (i, 0)),
                     pl.BlockSpec(b.shape, (lambda i: (0, 0, 0)) if b.ndim == 3 else (lambda i: (0, 0))), vec]
        args += [a, b, g]
    return _call(
        body, grid=(s // tm,), in_specs=in_specs + [row, row], out_specs=[row, row] + [vec] * nt,
        out_shape=[jax.ShapeDtypeStruct((s, d), F32), jax.ShapeDtypeStruct((s, d), BF16)]
        + [jax.ShapeDtypeStruct((1, d), F32)] * nt,
        args=(*args, x, dres), name=name, side=side)


def _causal_mask(transposed=False):
    row = lax.broadcasted_iota(jnp.int32, (CHUNK, CHUNK), 0)
    col = lax.broadcasted_iota(jnp.int32, (CHUNK, CHUNK), 1)
    return col >= row if transposed else col <= row


def _silu_parts(g):
    sg = jax.nn.sigmoid(g)
    return g * sg, sg * (1.0 + g * (1.0 - sg))


def _gate_fwd(z, ln_g, ln_b, ws, bs_t, *, tr=256, side=None):
    s = z.shape[0]
    tr = _row_tile(s, tr)
    w = A_WIDTH

    def body(u_ref, v_ref, g_ref, lg_ref, lb_ref, ws_ref, bst_ref, y_ref):
        v = v_ref[...].astype(F32)
        mu = jnp.mean(v, axis=-1, keepdims=True)
        xc = v - mu
        rs = lax.rsqrt(jnp.mean(xc * xc, axis=-1, keepdims=True) + EPS)
        vln = (xc * rs * lg_ref[...] + lb_ref[...]).astype(BF16)
        mask = _causal_mask()
        for grp in range(A_GROUPS):
            cols = slice(grp * CHUNK, (grp + 1) * CHUNK)
            wsm = jnp.where(mask, ws_ref[grp], 0.0).astype(BF16)
            bcol = bst_ref[:, grp:grp + 1]
            for ci in range(tr // CHUNK):
                rows = slice(ci * CHUNK, (ci + 1) * CHUNK)
                sv = jnp.dot(wsm, vln[rows, cols], preferred_element_type=F32) + bcol
                gv = g_ref[rows, cols].astype(F32)
                y_ref[rows, cols] = (u_ref[rows, cols].astype(F32) * sv * (gv * jax.nn.sigmoid(gv))).astype(BF16)

    vec = pl.BlockSpec((1, w), lambda i: (0, 0))
    (y,), side_outs = _call(
        body, grid=(s // tr,),
        in_specs=[pl.BlockSpec((tr, w), lambda i: (i, 0)), pl.BlockSpec((tr, w), lambda i: (i, 1)),
                  pl.BlockSpec((tr, w), lambda i: (i, 2)), vec, vec,
                  pl.BlockSpec((A_GROUPS, CHUNK, CHUNK), lambda i: (0, 0, 0)),
                  pl.BlockSpec((CHUNK, A_GROUPS), lambda i: (0, 0))],
        out_specs=[pl.BlockSpec((tr, w), lambda i: (i, 0))],
        out_shape=[jax.ShapeDtypeStruct((s, w), BF16)], args=(z, z, z, ln_g, ln_b, ws, bs_t), name="gate_fwd",
        side=side)
    return y, side_outs


def _gate_bwd(z, dy, ln_g, ln_b, ws, ws_t, bs_t, *, tr=256, side=None):
    s = z.shape[0]
    tr = _row_tile(s, tr)
    w = A_WIDTH
    nsteps = s // tr

    def body(u_ref, v_ref, g_ref, dy_ref, lg_ref, lb_ref, ws_ref, wst_ref, bst_ref,
             dz_ref, dlg_ref, dlb_ref, dws_ref, dbst_ref, dvln_sc, dsv_sc):
        i = pl.program_id(0)

        @pl.when(i == 0)
        def _():
            dws_ref[...] = jnp.zeros_like(dws_ref)
            dsv_sc[...] = jnp.zeros_like(dsv_sc)

        v = v_ref[...].astype(F32)
        mu = jnp.mean(v, axis=-1, keepdims=True)
        xc = v - mu
        rs = lax.rsqrt(jnp.mean(xc * xc, axis=-1, keepdims=True) + EPS)
        xh = xc * rs
        lg = lg_ref[...]
        vln = (xh * lg + lb_ref[...]).astype(BF16)
        mask = _causal_mask()
        mask_t = _causal_mask(transposed=True)
        for grp in range(A_GROUPS):
            cols = slice(grp * CHUNK, (grp + 1) * CHUNK)
            wsm = jnp.where(mask, ws_ref[grp], 0.0).astype(BF16)
            wsm_t = jnp.where(mask_t, wst_ref[grp], 0.0).astype(BF16)
            bcol = bst_ref[:, grp:grp + 1]
            for ci in range(tr // CHUNK):
                rows = slice(ci * CHUNK, (ci + 1) * CHUNK)
                vb = vln[rows, cols]
                sv = jnp.dot(wsm, vb, preferred_element_type=F32) + bcol
                uv = u_ref[rows, cols].astype(F32)
                silu, dsilu = _silu_parts(g_ref[rows, cols].astype(F32))
                dyv = dy_ref[rows, cols].astype(F32)
                dyu = dyv * uv
                dz_ref[rows, cols] = (dyv * sv * silu).astype(BF16)
                dz_ref[rows, 2 * w + grp * CHUNK:2 * w + (grp + 1) * CHUNK] = (dyu * sv * dsilu).astype(BF16)
                dsv = dyu * silu
                dsvb = dsv.astype(BF16)
                dvln_sc[rows, cols] = jnp.dot(wsm_t, dsvb, preferred_element_type=F32)
                dws_ref[grp] += lax.dot_general(dsvb, vb, NT, preferred_element_type=F32)
                dsv_sc[grp] += dsv
        dvln = dvln_sc[...]
        dlg_t = jnp.sum(dvln * xh, axis=0, keepdims=True)
        dlb_t = jnp.sum(dvln, axis=0, keepdims=True)
        a = dvln * lg
        dv = rs * (a - jnp.mean(a, axis=-1, keepdims=True) - xh * jnp.mean(a * xh, axis=-1, keepdims=True))
        dz_ref[:, w:2 * w] = dv.astype(BF16)

        @pl.when(i == 0)
        def _():
            dlg_ref[...] = dlg_t
            dlb_ref[...] = dlb_t

        @pl.when(i > 0)
        def _():
            dlg_ref[...] += dlg_t
            dlb_ref[...] += dlb_t

        @pl.when(i == nsteps - 1)
        def _():
            for grp in range(A_GROUPS):
                dws_ref[grp] = jnp.where(mask, dws_ref[grp], 0.0)
                dbst_ref[:, grp:grp + 1] = jnp.sum(dsv_sc[grp], axis=-1, keepdims=True)

    vec = pl.BlockSpec((1, w), lambda i: (0, 0))
    wsspec = pl.BlockSpec((A_GROUPS, CHUNK, CHUNK), lambda i: (0, 0, 0))
    bsspec = pl.BlockSpec((CHUNK, A_GROUPS), lambda i: (0, 0))
    return _call(
        body, grid=(nsteps,),
        in_specs=[pl.BlockSpec((tr, w), lambda i: (i, 0)), pl.BlockSpec((tr, w), lambda i: (i, 1)),
                  pl.BlockSpec((tr, w), lambda i: (i, 2)), pl.BlockSpec((tr, w), lambda i: (i, 0)),
                  vec, vec, wsspec, wsspec, bsspec],
        out_specs=[pl.BlockSpec((tr, 3 * w), lambda i: (i, 0)), vec, vec, wsspec, bsspec],
        out_shape=[jax.ShapeDtypeStruct((s, 3 * w), BF16), jax.ShapeDtypeStruct((1, w), F32),
                   jax.ShapeDtypeStruct((1, w), F32), jax.ShapeDtypeStruct((A_GROUPS, CHUNK, CHUNK), F32),
                   jax.ShapeDtypeStruct((CHUNK, A_GROUPS), F32)],
        scratch=[pltpu.VMEM((tr, w), F32), pltpu.VMEM((A_GROUPS, CHUNK, CHUNK), F32)],
        args=(z, z, z, dy, ln_g, ln_b, ws, ws_t, bs_t), name="gate_bwd", side=side)


HEADS_PER_BLOCK = 128 // HEAD_DIM
BLOCKS_PER_KV = Q_PER_KV // HEADS_PER_BLOCK
SCALE = HEAD_DIM ** -0.5
LOG2_E = math.log2(math.e)


def _rope_tables(s):
    inv_freq = ROPE_THETA ** (-jnp.arange(0, HEAD_DIM, 2, dtype=F32) / HEAD_DIM)
    ang = jnp.arange(s, dtype=F32)[:, None] * inv_freq[None, :]
    cos, sin = jnp.cos(ang), jnp.sin(ang)
    cos2 = jnp.concatenate([cos, cos], axis=-1)
    sin2 = jnp.concatenate([-sin, sin], axis=-1)
    return jnp.tile(cos2, (1, 2)), jnp.tile(sin2, (1, 2))


def _swap_halves(x):
    n = x.shape[-1]
    lane = lax.broadcasted_iota(jnp.int32, x.shape, x.ndim - 1)
    first = (lane % HEAD_DIM) < (HEAD_DIM // 2)
    return jnp.where(first, pltpu.roll(x, n - HEAD_DIM // 2, x.ndim - 1), pltpu.roll(x, HEAD_DIM // 2, x.ndim - 1))


def _left_half(rows):
    return lax.broadcasted_iota(jnp.int32, (rows, 128), 1) < HEAD_DIM


def _dup_heads(x):
    left = _left_half(x.shape[0])
    swapped = pltpu.roll(x, HEAD_DIM, 1)
    return jnp.concatenate([jnp.where(left, x, swapped), jnp.where(left, swapped, x)], axis=-1)


def _fold_heads(a):
    b0, b1 = a[:, :128], a[:, 128:]
    f0 = b0 + pltpu.roll(b0, HEAD_DIM, 1)
    f1 = b1 + pltpu.roll(b1, HEAD_DIM, 1)
    return jnp.where(_left_half(a.shape[0]), f0, f1)


def _kv_rope(kv, b_kv, cos, sin, *, tr=512):
    s = kv.shape[0]
    tr = _row_tile(s, tr)

    def body(kv_ref, b_ref, c_ref, s_ref, k_ref, v_ref):
        x = kv_ref[...] + b_ref[...]
        k = x[:, :KV_WIDTH]
        k_ref[...] = _dup_heads(k * c_ref[...] + _swap_halves(k) * s_ref[...]).astype(BF16)
        v_ref[...] = _dup_heads(x[:, KV_WIDTH:]).astype(BF16)

    tab = pl.BlockSpec((tr, KV_WIDTH), lambda i: (i, 0))
    wide = pl.BlockSpec((tr, 2 * KV_WIDTH), lambda i: (i, 0))
    outs, _ = _call(body, grid=(s // tr,),
                    in_specs=[wide, pl.BlockSpec((1, 2 * KV_WIDTH), lambda i: (0, 0)), tab, tab],
                    out_specs=[wide, wide], out_shape=[jax.ShapeDtypeStruct((s, 2 * KV_WIDTH), BF16)] * 2,
                    args=(kv, b_kv, cos, sin), name="kv_rope")
    return outs


def _kv_rope_bwd(dk2, dv2, cos, sin, *, tr=512):
    s = dk2.shape[0]
    tr = _row_tile(s, tr)

    def body(dk_ref, dv_ref, c_ref, s_ref, dkv_ref, db_ref):
        i = pl.program_id(0)
        d = _fold_heads(dk_ref[...])
        dk = d * c_ref[...] + _swap_halves(d * s_ref[...])
        dvv = _fold_heads(dv_ref[...])
        dkv_ref[:, :KV_WIDTH] = dk.astype(BF16)
        dkv_ref[:, KV_WIDTH:] = dvv.astype(BF16)
        sk = jnp.sum(dk, axis=0, keepdims=True)
        sv = jnp.sum(dvv, axis=0, keepdims=True)

        @pl.when(i == 0)
        def _():
            db_ref[:, :KV_WIDTH] = sk
            db_ref[:, KV_WIDTH:] = sv

        @pl.when(i > 0)
        def _():
            db_ref[:, :KV_WIDTH] += sk
            db_ref[:, KV_WIDTH:] += sv

    tab = pl.BlockSpec((tr, KV_WIDTH), lambda i: (i, 0))
    wide = pl.BlockSpec((tr, 2 * KV_WIDTH), lambda i: (i, 0))
    outs, _ = _call(body, grid=(s // tr,), in_specs=[wide, wide, tab, tab],
                    out_specs=[wide, pl.BlockSpec((1, 2 * KV_WIDTH), lambda i: (0, 0))],
                    out_shape=[jax.ShapeDtypeStruct((s, 2 * KV_WIDTH), BF16),
                               jax.ShapeDtypeStruct((1, 2 * KV_WIDTH), F32)],
                    args=(dk2, dv2, cos, sin), name="kv_rope_bwd")
    return outs


def _from_previous():
    cols = Q_PER_KV * CHUNK
    k = lax.broadcasted_iota(jnp.int32, (CHUNK, cols), 0)
    q = lax.broadcasted_iota(jnp.int32, (CHUNK, cols), 1) & (CHUNK - 1)
    return k > q


def _fold(x2, prev):
    return jnp.where(prev, x2[:CHUNK], x2[CHUNK:])


def _unfold(x, prev):
    zero = jnp.zeros_like(x)
    return jnp.concatenate([jnp.where(prev, x, zero), jnp.where(prev, zero, x)], axis=0)


def _stack_heads(blocks, left):
    parts = []
    for b in blocks:
        parts.append(jnp.where(left, b, jnp.zeros_like(b)))
        parts.append(jnp.where(left, jnp.zeros_like(b), b))
    return jnp.concatenate(parts, axis=0)


def _unstack_heads(xt):
    top = lax.broadcasted_iota(jnp.int32, (128, CHUNK), 0) < HEAD_DIM
    return [jnp.where(top, xt[:, (2 * b) * CHUNK:(2 * b + 1) * CHUNK], xt[:, (2 * b + 1) * CHUNK:(2 * b + 2) * CHUNK]).T
            for b in range(BLOCKS_PER_KV)]


def _sink_row(sk_ref, kvh):
    return jnp.concatenate([jnp.full((1, CHUNK), sk_ref[0, kvh * Q_PER_KV + r], F32) for r in range(Q_PER_KV)], axis=1)


def _stacked_probs(qs, kd, prev, sink, i):
    sc2 = lax.dot_general(kd, qs, NT, preferred_element_type=F32)
    no_previous = jnp.where(i > 0, 0.0, NEG_BIG)
    sc = jnp.where(prev, sc2[:CHUNK] + no_previous, sc2[CHUNK:])
    sink = sink * (1.0 / SCALE)
    m = jnp.maximum(jnp.max(sc, axis=0, keepdims=True), sink)
    p = jnp.exp2((sc - m) * (SCALE * LOG2_E))
    esink = jnp.exp2((sink - m) * (SCALE * LOG2_E))
    inv = 1.0 / (jnp.sum(p, axis=0, keepdims=True) + esink)
    return p * inv, esink * inv


def _lane_block(b):
    return slice(b * 128, (b + 1) * 128)


def _rope_blocks(zq_ref, bq_ref, cos, sin, kvh):
    out = []
    for b in range(BLOCKS_PER_KV):
        cols = _lane_block(kvh * BLOCKS_PER_KV + b)
        q = zq_ref[:, cols].astype(F32) + bq_ref[:, cols]
        out.append((q * cos + _swap_halves(q) * sin).astype(BF16))
    return out


def _attn_specs():
    qspec = pl.BlockSpec((CHUNK, B_WIDTH), lambda i: (i, 0))
    gspec = pl.BlockSpec((CHUNK, B_WIDTH), lambda i: (i, 1))
    prev = pl.BlockSpec((CHUNK, 2 * KV_WIDTH), lambda i: (jnp.maximum(i - 1, 0), 0))
    cur = pl.BlockSpec((CHUNK, 2 * KV_WIDTH), lambda i: (i, 0))
    tab = pl.BlockSpec((CHUNK, KV_WIDTH), lambda i: (i, 0))
    bq = pl.BlockSpec((1, B_WIDTH), lambda i: (0, 0))
    sinks = pl.BlockSpec(memory_space=pltpu.SMEM)
    return qspec, gspec, prev, cur, tab, bq, sinks


def _attn_fwd(zb, k2, v2, cos, sin, b_bq, sinks, *, side=None):
    s = zb.shape[0]

    def body(zq_ref, zg_ref, kp_ref, kc_ref, vp_ref, vc_ref, c_ref, s_ref, bq_ref, sk_ref, y_ref):
        i = pl.program_id(0)
        cos, sin = c_ref[...], s_ref[...]
        kcat = jnp.concatenate([kp_ref[...], kc_ref[...]], axis=0)
        vcat = jnp.concatenate([vp_ref[...], vc_ref[...]], axis=0)
        prev = _from_previous()
        left = _left_half(CHUNK)
        for kvh in range(N_KV_HEADS):
            qs = _stack_heads(_rope_blocks(zq_ref, bq_ref, cos, sin, kvh), left)
            p, _ = _stacked_probs(qs, kcat[:, _lane_block(kvh)], prev, _sink_row(sk_ref, kvh), i)
            ot = lax.dot_general(vcat[:, _lane_block(kvh)], _unfold(p, prev).astype(BF16), TN,
                                 preferred_element_type=F32)
            for b, ob in enumerate(_unstack_heads(ot)):
                cols = _lane_block(kvh * BLOCKS_PER_KV + b)
                gv = zg_ref[:, cols].astype(F32)
                y_ref[:, cols] = (ob * (gv * jax.nn.sigmoid(gv))).astype(BF16)

    qspec, gspec, prev, cur, tab, bq, sk = _attn_specs()
    (y,), side_outs = _call(body, grid=(s // CHUNK,), in_specs=[qspec, gspec, prev, cur, prev, cur, tab, tab, bq, sk],
                            out_specs=[qspec], out_shape=[jax.ShapeDtypeStruct((s, B_WIDTH), BF16)],
                            args=(zb, zb, k2, k2, v2, v2, cos, sin, b_bq, sinks), name="attn_fwd", side=side)
    return y, side_outs


def _attn_bwd(zb, dyb, k2, v2, cos, sin, b_bq, sinks):
    s = zb.shape[0]

    def body(zq_ref, zg_ref, dy_ref, kp_ref, kc_ref, vp_ref, vc_ref, c_ref, s_ref, bq_ref, sk_ref,
             dz_ref, dk_ref, dv_ref, dbq_ref, dsk_ref):
        i = pl.program_id(0)

        @pl.when(i == 0)
        def _():
            dk_ref[...] = jnp.zeros_like(dk_ref)
            dv_ref[...] = jnp.zeros_like(dv_ref)
            dbq_ref[...] = jnp.zeros_like(dbq_ref)
            dsk_ref[...] = jnp.zeros_like(dsk_ref)

        cos, sin = c_ref[...], s_ref[...]
        kcat = jnp.concatenate([kp_ref[...], kc_ref[...]], axis=0)
        vcat = jnp.concatenate([vp_ref[...], vc_ref[...]], axis=0)
        prev = _from_previous()
        left = _left_half(CHUNK)
        lane = lax.broadcasted_iota(jnp.int32, (1, 128), 1)
        dsk_row = jnp.zeros((1, 128), F32)
        cur_rows = pl.ds(pl.multiple_of(i * CHUNK, CHUNK), CHUNK)
        for kvh in range(N_KV_HEADS):
            kd, vd = kcat[:, _lane_block(kvh)], vcat[:, _lane_block(kvh)]
            qs = _stack_heads(_rope_blocks(zq_ref, bq_ref, cos, sin, kvh), left)
            p, psink = _stacked_probs(qs, kd, prev, _sink_row(sk_ref, kvh), i)
            pb = _unfold(p, prev).astype(BF16)
            ot = lax.dot_general(vd, pb, TN, preferred_element_type=F32)
            gates, dys = [], []
            for b in range(BLOCKS_PER_KV):
                cols = _lane_block(kvh * BLOCKS_PER_KV + b)
                gates.append(_silu_parts(zg_ref[:, cols].astype(F32)))
                dys.append(dy_ref[:, cols].astype(F32))
            dos = _stack_heads([(dyv * silu).astype(BF16) for dyv, (silu, _) in zip(dys, gates)], left)
            dp = _fold(lax.dot_general(vd, dos, NT, preferred_element_type=F32), prev)
            delta = jnp.sum(p * dp, axis=0, keepdims=True)
            ds = _unfold(p * (dp - delta) * SCALE, prev).astype(BF16)
            dqt = lax.dot_general(kd, ds, TN, preferred_element_type=F32)
            dk_part = jnp.dot(ds, qs, preferred_element_type=F32)
            dv_part = jnp.dot(pb, dos, preferred_element_type=F32)
            dk_ref[cur_rows, _lane_block(kvh)] += dk_part[CHUNK:]
            dv_ref[cur_rows, _lane_block(kvh)] += dv_part[CHUNK:]

            @pl.when(i > 0)
            def _(kvh=kvh, dk_part=dk_part, dv_part=dv_part):
                prev_rows = pl.ds(pl.multiple_of((i - 1) * CHUNK, CHUNK), CHUNK)
                dk_ref[prev_rows, _lane_block(kvh)] += dk_part[:CHUNK]
                dv_ref[prev_rows, _lane_block(kvh)] += dv_part[:CHUNK]

            sink_grad = psink * delta
            for r in range(Q_PER_KV):
                dsink = -jnp.sum(sink_grad[:, r * CHUNK:(r + 1) * CHUNK], axis=1, keepdims=True)
                dsk_row = dsk_row + jnp.where(lane == kvh * Q_PER_KV + r, dsink, 0.0)
            blocks = zip(_unstack_heads(ot), _unstack_heads(dqt), dys, gates)
            for b, (ob, dqr, dyv, (_, dsilu)) in enumerate(blocks):
                blk = kvh * BLOCKS_PER_KV + b
                dq = dqr * cos + _swap_halves(dqr * sin)
                dbq_ref[:, _lane_block(blk)] += jnp.sum(dq, axis=0, keepdims=True)
                dz_ref[:, _lane_block(blk)] = dq.astype(BF16)
                dz_ref[:, _lane_block(B_WIDTH // 128 + blk)] = (dyv * ob * dsilu).astype(BF16)
        dsk_ref[0:1, :] += dsk_row

    qspec, gspec, prev, cur, tab, bq, sk = _attn_specs()
    full = pl.BlockSpec((s, 2 * KV_WIDTH), lambda i: (0, 0))
    outs, _ = _call(
        body, grid=(s // CHUNK,),
        in_specs=[qspec, gspec, qspec, prev, cur, prev, cur, tab, tab, bq, sk],
        out_specs=[pl.BlockSpec((CHUNK, 2 * B_WIDTH), lambda i: (i, 0)), full, full, bq,
                   pl.BlockSpec((8, 128), lambda i: (0, 0))],
        out_shape=[jax.ShapeDtypeStruct((s, 2 * B_WIDTH), BF16), jax.ShapeDtypeStruct((s, 2 * KV_WIDTH), F32),
                   jax.ShapeDtypeStruct((s, 2 * KV_WIDTH), F32), jax.ShapeDtypeStruct((1, B_WIDTH), F32),
                   jax.ShapeDtypeStruct((8, 128), F32)],
        args=(zb, zb, dyb, k2, k2, v2, v2, cos, sin, b_bq, sinks), name="attn_bwd")
    return outs


def _place():
    x, y, c = lax.axis_index("x"), lax.axis_index("y"), lax.axis_index("c")
    return x, y, c, [(1 - x, y), (x, 1 - y), (1 - x, 1 - y)]


def _relations():
    return [(r >> 2 & 1, r >> 1 & 1, r & 1) for r in range(1, 8)]


def _gather_side(arrs):
    n = len(arrs)

    def copies(ins, outs, sems):
        send_ici, recv_ici, send_d2d, recv_d2d, local_sem = sems
        x, y, c, chips = _place()
        me = 2 * x + y

        def rows(a, half):
            hr = arrs[a].shape[0] // 2
            return pl.ds(half * hr, hr)

        def ici(a, j, src_chip, to):
            return pltpu.make_async_remote_copy(
                src_ref=ins[a].at[rows(a, c)], dst_ref=outs[a].at[src_chip, rows(a, c)],
                send_sem=send_ici.at[a, j], recv_sem=recv_ici.at[a, j], device_id=to, device_id_type=MESH)

        def d2d(a, j, chip, half):
            blk = outs[a].at[chip, rows(a, half)]
            return pltpu.make_async_remote_copy(
                src_ref=blk, dst_ref=blk, send_sem=send_d2d.at[a, j], recv_sem=recv_d2d.at[a, j],
                device_id=(x, y, 1 - c), device_id_type=MESH)

        local = [pltpu.make_async_copy(ins[a], outs[a].at[me], local_sem.at[a]) for a in range(n)]
        pairs = [(a, j, chip) for a in range(n) for j, chip in enumerate(chips)]
        return c, me, local, ici, d2d, pairs

    def start(ins, outs, sems):
        c, me, local, ici, _, pairs = copies(ins, outs, sems)
        for cp in local:
            cp.start()
        for a, j, chip in pairs:
            ici(a, j, me, (*chip, c)).start()

    def finish(ins, outs, sems):
        c, me, local, ici, d2d, pairs = copies(ins, outs, sems)
        for a, j, (px, py) in pairs:
            ici(a, j, 2 * px + py, (px, py, c)).wait_recv()
            d2d(a, j, 2 * px + py, c).start()
        for a, j, (px, py) in pairs:
            d2d(a, j, 2 * px + py, 1 - c).wait_recv()
        for a, j, (px, py) in pairs:
            ici(a, j, me, (px, py, c)).wait_send()
            d2d(a, j, 2 * px + py, c).wait_send()
        for cp in local:
            cp.wait()

    return _Side(arrs, [jax.ShapeDtypeStruct((N_CHIPS,) + a.shape, a.dtype) for a in arrs],
                 [pltpu.SemaphoreType.DMA((n, 3))] * 4 + [pltpu.SemaphoreType.DMA((n,))], start, finish)


def _exchange_side(grads):
    n = len(grads)

    def copies(ins, outs, sems):
        send_sem, recv_sem = sems
        x, y, c, _ = _place()
        cps = []
        for a in range(n):
            hr = grads[a].shape[1] // 2
            cps.append(pltpu.make_async_remote_copy(
                src_ref=ins[a].at[:, pl.ds((1 - c) * hr, hr), :], dst_ref=outs[a],
                send_sem=send_sem.at[a], recv_sem=recv_sem.at[a], device_id=(x, y, 1 - c), device_id_type=MESH))
        return cps

    def start(ins, outs, sems):
        for cp in copies(ins, outs, sems):
            cp.start()

    def finish(ins, outs, sems):
        for cp in copies(ins, outs, sems):
            cp.wait()

    return _Side(grads, [jax.ShapeDtypeStruct((g.shape[0], g.shape[1] // 2, g.shape[2]), g.dtype) for g in grads],
                 [pltpu.SemaphoreType.DMA((n,))] * 2, start, finish)


def _scatter_side(chip_sums, small=None):
    n = len(chip_sums)
    arrs = list(chip_sums) + ([small] if small is not None else [])

    def copies(ins, outs, sems):
        x, y, c, chips = _place()
        cps = []
        for a in range(n):
            for j, (px, py) in enumerate(chips):
                cps.append(pltpu.make_async_remote_copy(
                    src_ref=ins[a].at[2 * px + py], dst_ref=outs[a].at[j],
                    send_sem=sems[0].at[a, j], recv_sem=sems[1].at[a, j], device_id=(px, py, c), device_id_type=MESH))
        if small is not None:
            for r, (fx, fy, fc) in enumerate(_relations(), start=1):
                px, py, pc = x ^ fx, y ^ fy, c ^ fc
                cps.append(pltpu.make_async_remote_copy(
                    src_ref=ins[n].at[4 * px + 2 * py + pc], dst_ref=outs[n].at[r],
                    send_sem=sems[2].at[r - 1], recv_sem=sems[3].at[r - 1], device_id=(px, py, pc),
                    device_id_type=MESH))
        return cps

    def start(ins, outs, sems):
        for cp in copies(ins, outs, sems):
            cp.start()

    def finish(ins, outs, sems):
        for cp in copies(ins, outs, sems):
            cp.wait()

    shapes = [jax.ShapeDtypeStruct((3,) + t.shape[1:], t.dtype) for t in chip_sums]
    sems = [pltpu.SemaphoreType.DMA((n, 3))] * 2
    if small is not None:
        shapes.append(jax.ShapeDtypeStruct(small.shape, small.dtype))
        sems += [pltpu.SemaphoreType.DMA((7,))] * 2
    return _Side(arrs, shapes, sems, start, finish)


def _small_scatter_side(small):
    def copies(ins, outs, sems):
        x, y, c, _ = _place()
        cps = []
        for r, (fx, fy, fc) in enumerate(_relations(), start=1):
            px, py, pc = x ^ fx, y ^ fy, c ^ fc
            cps.append(pltpu.make_async_remote_copy(
                src_ref=ins[0].at[4 * px + 2 * py + pc], dst_ref=outs[0].at[r],
                send_sem=sems[0].at[r - 1], recv_sem=sems[1].at[r - 1], device_id=(px, py, pc), device_id_type=MESH))
        return cps

    def start(ins, outs, sems):
        for cp in copies(ins, outs, sems):
            cp.start()

    def finish(ins, outs, sems):
        for cp in copies(ins, outs, sems):
            cp.wait()

    return _Side([small], [jax.ShapeDtypeStruct(small.shape, small.dtype)], [pltpu.SemaphoreType.DMA((7,))] * 2,
                 start, finish)


def _share_side(halves, small=None):
    n = len(halves)
    arrs = list(halves) + ([small] if small is not None else [])

    def copies(ins, outs, sems, mine):
        x, y, c, _ = _place()
        me = 4 * x + 2 * y + c
        cps = []
        for a in range(n):
            hr = halves[a].shape[0] // 2
            rows = pl.ds((c if mine else 1 - c) * hr, hr)
            cps.append(pltpu.make_async_remote_copy(
                src_ref=ins[a].at[rows], dst_ref=outs[a].at[rows],
                send_sem=sems[0].at[a], recv_sem=sems[1].at[a], device_id=(x, y, 1 - c), device_id_type=MESH))
        if small is not None:
            for r, (fx, fy, fc) in enumerate(_relations(), start=1):
                px, py, pc = x ^ fx, y ^ fy, c ^ fc
                seg = me if mine else 4 * px + 2 * py + pc
                cps.append(pltpu.make_async_remote_copy(
                    src_ref=ins[n].at[seg], dst_ref=outs[n].at[seg],
                    send_sem=sems[2].at[r - 1], recv_sem=sems[3].at[r - 1], device_id=(px, py, pc),
                    device_id_type=MESH))
        return cps

    def start(ins, outs, sems):
        for cp in copies(ins, outs, sems, True):
            cp.start()

    def finish(ins, outs, sems):
        for cp in copies(ins, outs, sems, False):
            cp.wait_recv()
        for cp in copies(ins, outs, sems, True):
            cp.wait_send()

    sems = [pltpu.SemaphoreType.DMA((n,))] * 2 + ([pltpu.SemaphoreType.DMA((7,))] * 2 if small is not None else [])
    return _Side(arrs, [jax.ShapeDtypeStruct(h.shape, h.dtype) for h in arrs], sems, start, finish,
                 aliases={i: i for i in range(len(arrs))})


GATHER_PIECES = [(0, 0), (0, 1), (1, 0), (2, 0), (1, 1), (2, 1), (3, 0), (3, 1)]


def _mm_gathering(a, shard, order, *, name, tm=1024):
    s, k = a.shape
    nc = shard.shape[1]
    tm = _row_tile(s, tm)
    tn = nc // 2
    hr = k // 2
    qr = hr // 2
    blocks = jnp.stack([order[src] * 2 + h for src, h in GATHER_PIECES]).astype(jnp.int32)

    def body(blocks_ref, a_ref, shard_ref, z_ref, full_ref, wbuf, send_ici, recv_ici, send_relay,
             recv_relay, send_d2d, recv_d2d, local_sem, load_sem):
        piece, i = pl.program_id(0), pl.program_id(1)
        x, y, c, chips = _place()
        me = 2 * x + y
        nbrs = chips[:2]
        chip_of = [2 * px + py for px, py in chips]

        def quarter(q):
            return pl.ds(c * hr + q * qr, qr)

        def sibling_quarter(q):
            return pl.ds((1 - c) * hr + q * qr, qr)

        def whole(half):
            return pl.ds(half * hr, hr)

        def cols(h):
            return pl.ds(h * tn, tn)

        def direct(j, src_chip, h):
            return pltpu.make_async_remote_copy(
                src_ref=shard_ref.at[whole(c), cols(h)], dst_ref=full_ref.at[src_chip, whole(c), cols(h)],
                send_sem=send_ici.at[j, h], recv_sem=recv_ici.at[j, h], device_id=(*nbrs[j], c), device_id_type=MESH)

        def relay(j, src_chip, h):
            blk = full_ref.at[src_chip, quarter(j), cols(h)]
            return pltpu.make_async_remote_copy(
                src_ref=blk, dst_ref=blk, send_sem=send_relay.at[j, h], recv_sem=recv_relay.at[j, h],
                device_id=(*nbrs[1 - j], c), device_id_type=MESH)

        def d2d(j, chip, rows, h):
            blk = full_ref.at[chip, rows, cols(h)]
            return pltpu.make_async_remote_copy(
                src_ref=blk, dst_ref=blk, send_sem=send_d2d.at[j, h], recv_sem=recv_d2d.at[j, h],
                device_id=(x, y, 1 - c), device_id_type=MESH)

        def load(p):
            src, h = GATHER_PIECES[p]
            where = shard_ref if src == 0 else full_ref.at[chip_of[src - 1]]
            return pltpu.make_async_copy(where.at[:, cols(h)], wbuf.at[p % 2], load_sem.at[p % 2])

        local = pltpu.make_async_copy(shard_ref, full_ref.at[me], local_sem)

        def arrived(p):
            src, h = GATHER_PIECES[p]
            if src in (1, 2):
                j = src - 1
                direct(j, chip_of[j], h).wait_recv()
                relay(j, chip_of[j], h).start()
                d2d(j, chip_of[j], whole(c), h).start()
            elif src == 3:
                for j in range(2):
                    relay(1 - j, chip_of[2], h).wait_recv()
                    d2d(2 + j, chip_of[2], quarter(1 - j), h).start()

        def fetch(p):
            src, h = GATHER_PIECES[p]
            if src in (1, 2):
                d2d(src - 1, chip_of[src - 1], whole(1 - c), h).wait_recv()
            elif src == 3:
                for j in range(2):
                    d2d(2 + j, chip_of[2], sibling_quarter(1 - j), h).wait_recv()
            load(p).start()

        n_i = s // tm
        for p in range(len(GATHER_PIECES)):
            @pl.when(jnp.logical_and(piece == p, i == 0))
            def _(p=p):
                if p == 0:
                    local.start()
                    for hh in range(2):
                        for j in range(2):
                            direct(j, me, hh).start()
                    load(0).start()
                load(p).wait()

        z_ref[...] = jnp.dot(a_ref[...], wbuf[piece % 2], preferred_element_type=F32).astype(z_ref.dtype)

        for p in range(len(GATHER_PIECES) - 1):
            @pl.when(jnp.logical_and(piece == p, i == min(1, n_i - 1)))
            def _(p=p):
                arrived(p + 1)

            @pl.when(jnp.logical_and(piece == p, i == min(2, n_i - 1)))
            def _(p=p):
                fetch(p + 1)

        last = jnp.logical_and(piece == len(GATHER_PIECES) - 1, i == n_i - 1)

        @pl.when(last)
        def _():
            for h in range(2):
                for j in range(2):
                    direct(j, me, h).wait_send()
                    relay(j, chip_of[j], h).wait_send()
                    d2d(j, chip_of[j], whole(c), h).wait_send()
                    d2d(2 + j, chip_of[2], quarter(1 - j), h).wait_send()
            local.wait()

    return pl.pallas_call(
        body,
        grid_spec=pltpu.PrefetchScalarGridSpec(
            num_scalar_prefetch=1, grid=(len(GATHER_PIECES), s // tm),
            in_specs=[pl.BlockSpec((tm, k), lambda p, i, blocks: (i, 0)), HBM],
            out_specs=[pl.BlockSpec((tm, tn), lambda p, i, blocks: (i, blocks[p])), HBM],
            scratch_shapes=[pltpu.VMEM((2, k, tn), BF16)] + [pltpu.SemaphoreType.DMA((2, 2))] * 4
            + [pltpu.SemaphoreType.DMA((4, 2))] * 2 + [pltpu.SemaphoreType.DMA, pltpu.SemaphoreType.DMA((2,))]),
        out_shape=[jax.ShapeDtypeStruct((s, N_CHIPS * nc), BF16), jax.ShapeDtypeStruct((N_CHIPS, k, nc), BF16)],
        name=name, compiler_params=_cparams(),
    )(blocks, a, shard)


def _col_tile(cols):
    return cols if cols <= 2048 else 512


def _add_sibling(grad, recv, core, *, name):
    k, r, c = grad.shape
    hr = r // 2
    tr = min(hr, 256)
    tc = _col_tile(c)
    nrb = hr // tr

    def body(core_ref, g_ref, r_ref, o_ref):
        o_ref[...] = (g_ref[...] + r_ref[...]).astype(BF16)

    return pl.pallas_call(
        body,
        grid_spec=pltpu.PrefetchScalarGridSpec(
            num_scalar_prefetch=1, grid=(k, nrb, c // tc),
            in_specs=[pl.BlockSpec((None, tr, tc), lambda kk, i, j, core: (kk, core[0] * nrb + i, j)),
                      pl.BlockSpec((None, tr, tc), lambda kk, i, j, core: (kk, i, j))],
            out_specs=pl.BlockSpec((None, tr, tc), lambda kk, i, j, core: (kk, i, j))),
        out_shape=jax.ShapeDtypeStruct((k, hr, c), BF16), name=name, compiler_params=_cparams(),
    )(core, grad, recv)


def _sum_chips(grad, from_sibling, recv, place, *, name):
    _, hr, c = from_sibling.shape
    tr = min(hr, 256)
    tc = _col_tile(c)
    nrb = hr // tr

    def body(place_ref, g_ref, s_ref, r0_ref, r1_ref, r2_ref, o_ref):
        own = g_ref[...] + s_ref[...]
        o_ref[...] = ((own + r0_ref[...].astype(F32)) + r1_ref[...].astype(F32)) + r2_ref[...].astype(F32)

    def rspec(j):
        return pl.BlockSpec((None, tr, tc), lambda i, jj, place: (j, i, jj))

    return pl.pallas_call(
        body,
        grid_spec=pltpu.PrefetchScalarGridSpec(
            num_scalar_prefetch=1, grid=(nrb, c // tc),
            in_specs=[pl.BlockSpec((None, tr, tc), lambda i, jj, place: (place[0], place[1] * nrb + i, jj)),
                      pl.BlockSpec((None, tr, tc), lambda i, jj, place: (place[0], i, jj)),
                      rspec(0), rspec(1), rspec(2)],
            out_specs=pl.BlockSpec((tr, tc), lambda i, jj, place: (place[1] * nrb + i, jj))),
        out_shape=jax.ShapeDtypeStruct((2 * hr, c), F32), name=name, compiler_params=_cparams(),
    )(place, grad, from_sibling, recv, recv, recv)


def _sum_small(small, recv, place):
    _, sr, _ = small.shape

    def body(place_ref, own_ref, r_ref, o_ref):
        acc = own_ref[...]
        for r in range(1, 8):
            acc = acc + r_ref[r]
        o_ref[...] = acc

    return pl.pallas_call(
        body,
        grid_spec=pltpu.PrefetchScalarGridSpec(
            num_scalar_prefetch=1, grid=(1,),
            in_specs=[pl.BlockSpec((None, sr, 128), lambda i, place: (place[2], 0, 0)),
                      pl.BlockSpec((8, sr, 128), lambda i, place: (0, 0, 0))],
            out_specs=pl.BlockSpec((None, sr, 128), lambda i, place: (place[2], 0, 0))),
        out_shape=jax.ShapeDtypeStruct(small.shape, F32), name="sum_small", compiler_params=_cparams(),
    )(place, small, recv)


def _spread_side(vec):
    def copies(ins, outs, sems):
        x, y, c, _ = _place()
        return [pltpu.make_async_remote_copy(
            src_ref=ins[0], dst_ref=outs[0].at[r], send_sem=sems[0].at[r - 1], recv_sem=sems[1].at[r - 1],
            device_id=(x ^ fx, y ^ fy, c ^ fc), device_id_type=MESH)
            for r, (fx, fy, fc) in enumerate(_relations(), start=1)]

    def start(ins, outs, sems):
        for cp in copies(ins, outs, sems):
            cp.start()

    def finish(ins, outs, sems):
        for cp in copies(ins, outs, sems):
            cp.wait()

    return _Side([vec], [jax.ShapeDtypeStruct((8,) + vec.shape, vec.dtype)], [pltpu.SemaphoreType.DMA((7,))] * 2,
                 start, finish)


def _sum_in_device_order(own, spread, place):
    def body(place_ref, own_ref, r_ref, o_ref):
        me = place_ref[2]
        acc = jnp.zeros_like(own_ref[...])
        for d in range(8):
            slot = jnp.where(me == d, 1, me ^ d)
            acc = acc + jnp.where(me == d, own_ref[...], r_ref[slot])
        o_ref[...] = acc

    return pl.pallas_call(
        body,
        grid_spec=pltpu.PrefetchScalarGridSpec(
            num_scalar_prefetch=1, grid=(1,),
            in_specs=[pl.BlockSpec(own.shape, lambda i, place: (0, 0)),
                      pl.BlockSpec(spread.shape, lambda i, place: (0, 0, 0))],
            out_specs=pl.BlockSpec(own.shape, lambda i, place: (0, 0))),
        out_shape=jax.ShapeDtypeStruct(own.shape, F32), name="sum_in_device_order", compiler_params=_cparams(),
    )(place, own, spread)


def _adamw(w, g, m, v, *, name):
    r, c = w.shape
    tr = 256 if r % 256 == 0 else r
    tc = _col_tile(c)
    bc1 = 1.0 - ADAM_B1 ** ADAM_STEP
    bc2 = 1.0 - ADAM_B2 ** ADAM_STEP

    def body(w_ref, g_ref, m_ref, v_ref, d_ref, nm_ref, nv_ref):
        gv = g_ref[...]
        nm = ADAM_B1 * m_ref[...] + (1.0 - ADAM_B1) * gv
        nv = ADAM_B2 * v_ref[...] + (1.0 - ADAM_B2) * (gv * gv)
        d_ref[...] = -ADAM_LR * ((nm / bc1) / (jnp.sqrt(nv / bc2) + ADAM_EPS) + ADAM_WD * w_ref[...])
        nm_ref[...] = nm
        nv_ref[...] = nv

    spec = pl.BlockSpec((tr, tc), lambda i, j: (i, j))
    outs, _ = _call(body, grid=(r // tr, c // tc), in_specs=[spec] * 4, out_specs=[spec] * 3,
                    out_shape=[jax.ShapeDtypeStruct((r, c), F32)] * 3, args=(w, g, m, v), name=name)
    return outs


SMALL_ORDER = ["a_ws", "a_bs", "a_norm_g", "a_ln_g", "a_ln_b", "kv_norm_g", "b_kv", "b_norm_g", "b_bq",
               "b_sinks", "final_norm_g"]
SHARDED_SMALL = {"a_norm_g", "a_ln_g", "a_ln_b"}
PACK_TILE = 8 * 128


def _rows128(a):
    flat = a.reshape(-1)
    return jnp.pad(flat, (0, (-flat.shape[0]) % PACK_TILE)).reshape(-1, 128)


def _pack_rows(parts, multiple):
    rows = [_rows128(p) for p in parts]
    total = sum(r.shape[0] for r in rows)
    pad = (-total) % multiple
    if pad:
        rows.append(jnp.zeros((pad, 128), rows[0].dtype))
    return jnp.concatenate(rows, axis=0)


def _unpack_rows(packed, shapes):
    out, row = [], 0
    for shp in shapes:
        size = math.prod(shp)
        nrow = -(-size // PACK_TILE) * 8
        out.append(packed[row:row + nrow].reshape(-1)[:size].reshape(shp))
        row += nrow
    return out


WEIGHTS = ["a_norm_g", "a_w_in", "a_ln_g", "a_ln_b", "a_ws", "a_bs", "a_w_out", "kv_norm_g", "w_kv", "b_kv",
           "b_norm_g", "b_w_in", "b_bq", "b_sinks", "b_w_out", "final_norm_g"]
BIG = ["a_w_in", "a_w_out", "w_kv", "b_w_in", "b_w_out"]


class _Reduction:
    def __init__(self, names, partials, core, place, small=None):
        self.names, self.partials, self.core, self.place, self.small = names, partials, core, place, small

    def exchange_side(self):
        return _exchange_side(self.partials)

    def took_exchange(self, from_sibling):
        self.from_sibling = from_sibling
        self.chip_sums = [_add_sibling(g, r, self.core, name="add_sibling_" + n)
                          for g, r, n in zip(self.partials, from_sibling, self.names)]

    def scatter_side(self):
        return _scatter_side(self.chip_sums, self.small)

    def took_scatter(self, arrived):
        big = arrived[:len(self.names)]
        self.halves = [_sum_chips(g, fs, r, self.place, name="sum_chips_" + n)
                       for g, fs, r, n in zip(self.partials, self.from_sibling, big, self.names)]
        self.small_mine = _sum_small(self.small, arrived[-1], self.place) if self.small is not None else None

    def share_side(self):
        return _share_side(self.halves, self.small_mine)

    def took_share(self, shared):
        self.grads = dict(zip(self.names, shared[:len(self.names)]))
        self.small_full = shared[-1] if self.small is not None else None


def _step(x, loss_target, p, m, v):
    xi, yi, ci = lax.axis_index("x"), lax.axis_index("y"), lax.axis_index("c")
    chip = 2 * xi + yi
    device = 4 * xi + 2 * yi + ci
    core = jnp.reshape(ci, (1,)).astype(jnp.int32)
    place = jnp.stack([chip, ci, device]).astype(jnp.int32)
    x, tgt = x[0], loss_target[0]
    s = x.shape[0]
    cos, sin = _rope_tables(s)

    shard2d = {n: p[n].reshape(p[n].shape[-2:]) for n in BIG}
    shard_bf = {n: shard2d[n].astype(BF16) for n in BIG}
    ws = p["a_ws"][0]
    ws_t = jnp.swapaxes(ws, 1, 2)
    bs_t = p["a_bs"][0].T
    kv_norm_g, b_kv = p["kv_norm_g"].reshape(1, -1), p["b_kv"].reshape(1, -1)
    final_norm_g = p["final_norm_g"].reshape(1, -1)

    vec_shapes = [p[n].shape for n in ("a_norm_g", "a_ln_g", "a_ln_b")]
    vec_pack = _pack_rows([p["a_norm_g"], p["a_ln_g"], p["a_ln_b"]], 16)
    (vec_all,) = _comm_call(_gather_side([vec_pack]), "gather_vectors")
    vecs = [_unpack_rows(vec_all[k], vec_shapes) for k in range(N_CHIPS)]
    a_norm_g, a_ln_g, a_ln_b = (jnp.concatenate([vk[t] for vk in vecs], axis=-1) for t in range(3))

    (n_a,) = _rms_fwd(x, [a_norm_g], name="rms_a")
    order = jnp.stack([chip, 2 * (1 - xi) + yi, 2 * xi + (1 - yi), 2 * (1 - xi) + (1 - yi)]).astype(jnp.int32)
    z, a_w_in = _mm_gathering(n_a, shard_bf["a_w_in"], order, name="mm_a_in")
    y, (a_w_out,) = _gate_fwd(z, a_ln_g, a_ln_b, ws, bs_t, side=_gather_side([shard_bf["a_w_out"]]))
    a_w_out = a_w_out.reshape(A_WIDTH, D_MODEL)
    (h1, n_kv, n_b), (w_kv, b_w_in) = _mm_residual_norms(
        y, a_w_out, x, [kv_norm_g, p["b_norm_g"]], name="mm_a_out",
        side=_gather_side([shard_bf["w_kv"], shard_bf["b_w_in"]]))
    w_kv = w_kv.reshape(D_MODEL, 2 * KV_WIDTH)
    kv = _mm_nn(n_kv, w_kv, name="mm_kv", tn=2 * KV_WIDTH)
    kr, vv = _kv_rope(kv, b_kv, cos, sin)
    zb = _mm_nn(n_b, b_w_in, name="mm_b_in", tn=512, tm=1024, out_dtype=BF16)
    yb, (b_w_out,) = _attn_fwd(zb, kr, vv, cos, sin, p["b_bq"], p["b_sinks"], side=_gather_side([shard_bf["b_w_out"]]))
    b_w_out = b_w_out.reshape(B_WIDTH, D_MODEL)
    loss_blk, dh2, dh2b, d_final_g = _mm_residual_loss(yb, b_w_out, h1, tgt, final_norm_g, name="mm_b_out")

    d_b_w_out = _mm_tn(yb, dh2b, name="mm_d_b_w_out", tm=B_WIDTH, tn=D_MODEL)
    red_bo = _Reduction(["b_w_out"], [d_b_w_out.reshape(N_CHIPS, B_WIDTH // N_CHIPS, D_MODEL)], core, place)
    dyb, got = _mm_nt(dh2b, b_w_out, name="mm_dyb", out_dtype=BF16, side=red_bo.exchange_side())
    red_bo.took_exchange(got)
    dzb, dk_rot, dv, d_bq, d_sinks = _attn_bwd(zb, dyb, kr, vv, cos, sin, p["b_bq"], p["b_sinks"])
    dkv, d_b_kv = _kv_rope_bwd(dk_rot, dv, cos, sin)
    d_b_w_in, got = _mm_tn(n_b, dzb, name="mm_d_b_w_in", tm=D_MODEL, tn=512, shards=N_CHIPS,
                           side=red_bo.scatter_side())
    red_bo.took_scatter(got)
    d_w_kv, got = _mm_tn(n_kv, dkv, name="mm_d_w_kv", tm=D_MODEL, tn=2 * KV_WIDTH, side=red_bo.share_side())
    red_bo.took_share(got)
    red_bi = _Reduction(["b_w_in", "w_kv"], [d_b_w_in, d_w_kv.reshape(N_CHIPS, D_MODEL // N_CHIPS, 2 * KV_WIDTH)],
                        core, place)
    (dh1, dh1b, d_kv_g, d_b_g), got = _mm_nt_rms_bwd(
        [(dkv, w_kv, kv_norm_g), (dzb, b_w_in, p["b_norm_g"])], h1, dh2, name="mm_dn_b", tm=512,
        side=red_bi.exchange_side())
    red_bi.took_exchange(got)

    d_a_w_out, got = _mm_tn(y, dh1b, name="mm_d_a_w_out", tm=1024, tn=D_MODEL, side=red_bi.scatter_side())
    red_bi.took_scatter(got)
    red_ao = _Reduction(["a_w_out"], [d_a_w_out.reshape(N_CHIPS, A_WIDTH // N_CHIPS, D_MODEL)], core, place)
    sides = [red_ao.exchange_side(), red_bi.share_side()]
    dy, got = _mm_nt(dh1b, a_w_out, name="mm_dy", tn=1024, out_dtype=BF16, side=_join(sides))
    got = _split(got, sides)
    red_ao.took_exchange(got[0])
    red_bi.took_share(got[1])
    (dz, d_ln_g, d_ln_b, d_ws, d_bs_t), got = _gate_bwd(z, dy, a_ln_g, a_ln_b, ws, ws_t, bs_t,
                                                        side=red_ao.scatter_side())
    red_ao.took_scatter(got)
    small = {
        "a_ws": d_ws, "a_bs": d_bs_t.T, "a_ln_g": d_ln_g, "a_ln_b": d_ln_b,
        "kv_norm_g": d_kv_g, "b_kv": d_b_kv, "b_norm_g": d_b_g, "b_bq": d_bq,
        "b_sinks": d_sinks[0:1, :N_Q_HEADS], "final_norm_g": d_final_g,
    }
    packed = [n for n in SMALL_ORDER if n != "a_norm_g"]
    small_shapes = [small[n].shape for n in packed] + [(1, 1)]
    small_pack = _pack_rows([small[n] for n in packed] + [loss_blk[0:1, 0:1]], 64)
    seg = small_pack.shape[0] // 8
    small_pack = small_pack.reshape(8, seg, 128)
    sides = [red_ao.share_side(), _small_scatter_side(small_pack)]
    d_a_w_in, got = _mm_tn(n_a, dz, name="mm_d_a_w_in", tm=D_MODEL, tn=1536, shards=N_CHIPS, side=_join(sides))
    got = _split(got, sides)
    red_ao.took_share(got[0])
    small_mine = _sum_small(small_pack, got[1][0], place)

    red_ai = _Reduction(["a_w_in"], [d_a_w_in], core, place)
    red_ai.took_exchange(_comm_call(red_ai.exchange_side(), "exchange_last"))
    (dx, _, d_a_g), got = _mm_nt_rms_bwd([(dz, a_w_in, a_norm_g)], x, dh1, name="mm_dn_a", tm=256,
                                         side=red_ai.scatter_side())
    red_ai.took_scatter(got)
    red_ai.small, red_ai.small_mine = small_pack, small_mine
    d_a_g = _rows128(d_a_g)
    sides = [red_ai.share_side(), _spread_side(d_a_g)]
    got = _split(_comm_call(_join(sides), "share_last"), sides)
    red_ai.took_share(got[0])
    small_full = dict(zip(packed + ["loss"], _unpack_rows(red_ai.small_full.reshape(8 * seg, 128), small_shapes)))
    small_full["a_norm_g"] = _sum_in_device_order(d_a_g, got[1][0], place).reshape(1, -1)
    loss = small_full["loss"].reshape(())

    grad_big = {**red_bo.grads, **red_bi.grads, **red_ao.grads, **red_ai.grads}
    grads = {}
    for n in SMALL_ORDER:
        gfull = small_full[n]
        if n in SHARDED_SMALL:
            width = p[n].shape[-1]
            gfull = lax.dynamic_slice_in_dim(gfull, chip * width, width, axis=-1)
        grads[n] = gfull.reshape(p[n].shape)
    for n in BIG:
        grads[n] = grad_big[n].reshape(p[n].shape)

    delta, new_m, new_v = {}, {}, {}
    for n in BIG:
        d, nm, nv = _adamw(shard2d[n], grad_big[n], m[n].reshape(shard2d[n].shape), v[n].reshape(shard2d[n].shape),
                           name="adamw_" + n)
        delta[n], new_m[n], new_v[n] = d.reshape(p[n].shape), nm.reshape(p[n].shape), nv.reshape(p[n].shape)
    shapes = [p[n].shape for n in SMALL_ORDER]
    packs = [_pack_rows([src[n] for n in SMALL_ORDER], 8) for src in (p, grads, m, v)]
    outs = _adamw(*packs, name="adamw_small")
    for res, packed in zip((delta, new_m, new_v), outs):
        for n, val in zip(SMALL_ORDER, _unpack_rows(packed, shapes)):
            res[n] = val

    return (loss, dx[None], *[grads[n] for n in WEIGHTS], *[delta[n] for n in WEIGHTS],
            *[new_m[n] for n in WEIGHTS], *[new_v[n] for n in WEIGHTS])


def kernel(x, a_norm_g, a_w_in, a_ln_g, a_ln_b, a_ws, a_bs, a_w_out, kv_norm_g, w_kv, b_kv, b_norm_g, b_w_in, b_bq, b_sinks, b_w_out, final_norm_g, loss_target, m_a_norm_g, m_a_w_in, m_a_ln_g, m_a_ln_b, m_a_ws, m_a_bs, m_a_w_out, m_kv_norm_g, m_w_kv, m_b_kv, m_b_norm_g, m_b_w_in, m_b_bq, m_b_sinks, m_b_w_out, m_final_norm_g, v_a_norm_g, v_a_w_in, v_a_ln_g, v_a_ln_b, v_a_ws, v_a_bs, v_a_w_out, v_kv_norm_g, v_w_kv, v_b_kv, v_b_norm_g, v_b_w_in, v_b_bq, v_b_sinks, v_b_w_out, v_final_norm_g):
    p = dict(a_norm_g=a_norm_g, a_w_in=a_w_in, a_ln_g=a_ln_g, a_ln_b=a_ln_b, a_ws=a_ws, a_bs=a_bs, a_w_out=a_w_out,
             kv_norm_g=kv_norm_g, w_kv=w_kv, b_kv=b_kv, b_norm_g=b_norm_g, b_w_in=b_w_in, b_bq=b_bq, b_sinks=b_sinks,
             b_w_out=b_w_out, final_norm_g=final_norm_g)
    m = dict(a_norm_g=m_a_norm_g, a_w_in=m_a_w_in, a_ln_g=m_a_ln_g, a_ln_b=m_a_ln_b, a_ws=m_a_ws, a_bs=m_a_bs,
             a_w_out=m_a_w_out, kv_norm_g=m_kv_norm_g, w_kv=m_w_kv, b_kv=m_b_kv, b_norm_g=m_b_norm_g, b_w_in=m_b_w_in,
             b_bq=m_b_bq, b_sinks=m_b_sinks, b_w_out=m_b_w_out, final_norm_g=m_final_norm_g)
    v = dict(a_norm_g=v_a_norm_g, a_w_in=v_a_w_in, a_ln_g=v_a_ln_g, a_ln_b=v_a_ln_b, a_ws=v_a_ws, a_bs=v_a_bs,
             a_w_out=v_a_w_out, kv_norm_g=v_kv_norm_g, w_kv=v_w_kv, b_kv=v_b_kv, b_norm_g=v_b_norm_g, b_w_in=v_b_w_in,
             b_bq=v_b_bq, b_sinks=v_b_sinks, b_w_out=v_b_w_out, final_norm_g=v_final_norm_g)
    return _step(x, loss_target, p, m, v)
```

```python
import functools
import math

import jax
import jax.numpy as jnp
from jax import lax
from jax.experimental import pallas as pl
from jax.experimental.pallas import tpu as pltpu

F32 = jnp.float32
BF16 = jnp.bfloat16

D_MODEL = 1024
CHUNK = 128
A_WIDTH = 2048
A_GROUPS = 16
HEAD_DIM = 64
N_Q_HEADS = 16
N_KV_HEADS = 2
Q_PER_KV = 8
B_WIDTH = 1024
KV_WIDTH = 128
ROPE_THETA = 10000.0
EPS = 1e-5
N_CHIPS = 4

ADAM_LR = 0.001
ADAM_B1 = 0.9
ADAM_B2 = 0.999
ADAM_EPS = 1e-08
ADAM_WD = 0.01
ADAM_STEP = 10

VMEM_LIMIT = 48 * 1024 * 1024
MESH = pl.DeviceIdType.MESH
NEG_BIG = -1e30
HBM = pl.BlockSpec(memory_space=pl.ANY)

NN = (((1,), (0,)), ((), ()))
NT = (((1,), (1,)), ((), ()))
TN = (((0,), (0,)), ((), ()))


def _cparams(**kw):
    return pltpu.CompilerParams(vmem_limit_bytes=VMEM_LIMIT, **kw)


class _Side:
    def __init__(self, ins, out_shapes, sems, start, finish, aliases=None, passing=None):
        self.ins, self.out_shapes, self.sems = list(ins), list(out_shapes), list(sems)
        self.start, self.finish = start, finish
        self.passing = passing or (lambda ins, outs, sems: None)
        self.aliases = dict(aliases or {})


def _join(sides):
    sides = [s for s in sides if s is not None]
    if not sides:
        return None
    offs, i, o, m = [], 0, 0, 0
    for s in sides:
        offs.append((i, o, m))
        i, o, m = i + len(s.ins), o + len(s.out_shapes), m + len(s.sems)

    def run(which):
        def go(ins, outs, sems):
            for s, (a, b, c) in zip(sides, offs):
                getattr(s, which)(ins[a:a + len(s.ins)], outs[b:b + len(s.out_shapes)], sems[c:c + len(s.sems)])
        return go

    aliases = {}
    for s, (a, b, _) in zip(sides, offs):
        aliases.update({a + k: b + v for k, v in s.aliases.items()})
    return _Side([x for s in sides for x in s.ins], [x for s in sides for x in s.out_shapes],
                 [x for s in sides for x in s.sems], run("start"), run("finish"), aliases, run("passing"))


def _split(side_outs, sides):
    out, pos = [], 0
    for s in sides:
        out.append(list(side_outs[pos:pos + len(s.out_shapes)]))
        pos += len(s.out_shapes)
    return out


def _call(body, *, grid, in_specs, out_specs, out_shape, args, name, scratch=(), side=None):
    in_specs, out_specs, out_shape, scratch = list(in_specs), list(out_specs), list(out_shape), list(scratch)
    if side is None:
        res = pl.pallas_call(body, grid=grid, in_specs=in_specs, out_specs=out_specs, out_shape=out_shape,
                             scratch_shapes=scratch, name=name, compiler_params=_cparams())(*args)
        return list(res), []
    n_in, n_out, n_sc = len(in_specs), len(out_specs), len(scratch)
    s_in, s_out = len(side.ins), len(side.out_shapes)

    def wrapped(*refs):
        ins, refs = refs[:n_in], refs[n_in:]
        side_ins, refs = refs[:s_in], refs[s_in:]
        outs, refs = refs[:n_out], refs[n_out:]
        side_outs, refs = refs[:s_out], refs[s_out:]
        scr, side_sems = refs[:n_sc], refs[n_sc:]
        step = 0
        for a, g in enumerate(grid):
            step = step * g + pl.program_id(a)
        steps = math.prod(grid)

        @pl.when(step == 0)
        def _():
            side.start(side_ins, side_outs, side_sems)

        body(*ins, *outs, *scr)

        @pl.when(step == (3 * (steps - 1)) // 4)
        def _():
            side.passing(side_ins, side_outs, side_sems)

        @pl.when(step == steps - 1)
        def _():
            side.finish(side_ins, side_outs, side_sems)

    res = pl.pallas_call(
        wrapped, grid=grid, in_specs=in_specs + [HBM] * s_in, out_specs=out_specs + [HBM] * s_out,
        out_shape=out_shape + side.out_shapes, scratch_shapes=scratch + side.sems,
        input_output_aliases={n_in + k: n_out + v for k, v in side.aliases.items()},
        name=name, compiler_params=_cparams(),
    )(*args, *side.ins)
    return list(res[:n_out]), list(res[n_out:])


def _comm_call(side, name):
    s_in, s_out = len(side.ins), len(side.out_shapes)

    def body(*refs):
        ins, outs, sems = refs[:s_in], refs[s_in:s_in + s_out], refs[s_in + s_out:]
        side.start(ins, outs, sems)
        side.passing(ins, outs, sems)
        side.finish(ins, outs, sems)

    return list(pl.pallas_call(
        body, in_specs=[HBM] * s_in, out_specs=[HBM] * s_out, out_shape=side.out_shapes, scratch_shapes=side.sems,
        input_output_aliases=side.aliases, name=name,
    )(*side.ins))


def _matmul(a, b, *, dims, grid, a_spec, b_spec, o_spec, out_shape, name, acc_axis=None,
            residual=None, r_spec=None, side=None):
    has_res = residual is not None

    def body(*refs):
        if has_res:
            a_ref, b_ref, r_ref, o_ref = refs
        else:
            a_ref, b_ref, o_ref = refs
        part = lax.dot_general(a_ref[...], b_ref[...], dims, preferred_element_type=F32)
        if acc_axis is None:
            if has_res:
                part = part + r_ref[...]
            o_ref[...] = part.astype(o_ref.dtype)
        else:
            k = pl.program_id(acc_axis)

            @pl.when(k == 0)
            def _():
                o_ref[...] = part

            @pl.when(k > 0)
            def _():
                o_ref[...] += part

    in_specs = [a_spec, b_spec] + ([r_spec] if has_res else [])
    args = (a, b) + ((residual,) if has_res else ())
    (out,), side_outs = _call(body, grid=grid, in_specs=in_specs, out_specs=[o_spec], out_shape=[out_shape],
                              args=args, name=name, side=side)
    return (out, side_outs) if side is not None else out


def _row_tile(s, want):
    return min(s, want)


def _mm_nn(a, b, *, name, tn, out_dtype=F32, residual=None, tm=512, side=None):
    s, k = a.shape
    tm = _row_tile(s, tm)
    if b.ndim == 3:
        nsh, _, nc = b.shape
        npb = nc // tn
        n = nsh * nc
        b_spec = pl.BlockSpec((None, k, tn), lambda i, j: (j // npb, 0, j % npb))
    else:
        n = b.shape[1]
        b_spec = pl.BlockSpec((k, tn), lambda i, j: (0, j))
    return _matmul(
        a, b, dims=NN, grid=(s // tm, n // tn),
        a_spec=pl.BlockSpec((tm, k), lambda i, j: (i, 0)), b_spec=b_spec,
        o_spec=pl.BlockSpec((tm, tn), lambda i, j: (i, j)),
        out_shape=jax.ShapeDtypeStruct((s, n), out_dtype), name=name, side=side,
        residual=residual, r_spec=pl.BlockSpec((tm, tn), lambda i, j: (i, j)) if residual is not None else None)


def _mm_nt(a, b, *, name, tn=None, tm=512, out_dtype=F32, side=None):
    s, k = a.shape
    tm = _row_tile(s, tm)
    n = b.shape[0]
    tn = n if tn is None else tn
    return _matmul(
        a, b, dims=NT, grid=(s // tm, n // tn),
        a_spec=pl.BlockSpec((tm, k), lambda i, j: (i, 0)),
        b_spec=pl.BlockSpec((tn, k), lambda i, j: (j, 0)),
        o_spec=pl.BlockSpec((tm, tn), lambda i, j: (i, j)),
        out_shape=jax.ShapeDtypeStruct((s, n), out_dtype), name=name, side=side)


def _mm_tn(a, b, *, name, tm, tn, tk=2048, shards=None, side=None):
    s, m = a.shape
    n = b.shape[1]
    tk = _row_tile(s, tk)
    if shards is None:
        o_spec = pl.BlockSpec((tm, tn), lambda i, j, kk: (i, j))
        out_shape = jax.ShapeDtypeStruct((m, n), F32)
    else:
        assert tm == m
        nc = n // shards
        npb = nc // tn
        o_spec = pl.BlockSpec((None, m, tn), lambda i, j, kk: (j // npb, 0, j % npb))
        out_shape = jax.ShapeDtypeStruct((shards, m, nc), F32)
    return _matmul(
        a, b, dims=TN, grid=(m // tm, n // tn, s // tk), acc_axis=2,
        a_spec=pl.BlockSpec((tk, tm), lambda i, j, kk: (kk, i)),
        b_spec=pl.BlockSpec((tk, tn), lambda i, j, kk: (kk, j)),
        o_spec=o_spec, out_shape=out_shape, name=name, side=side)


def _rstd(x):
    return lax.rsqrt(jnp.mean(x * x, axis=-1, keepdims=True) + EPS)


def _rms_fwd(x, gains, *, name, tr=256):
    s, d = x.shape
    tr = _row_tile(s, tr)
    ng = len(gains)

    def body(*refs):
        xv = refs[0][...]
        xh = xv * _rstd(xv)
        for t in range(ng):
            refs[1 + ng + t][...] = (xh * refs[1 + t][...]).astype(BF16)

    row = pl.BlockSpec((tr, d), lambda i: (i, 0))
    vec = pl.BlockSpec((1, d), lambda i: (0, 0))
    outs, _ = _call(body, grid=(s // tr,), in_specs=[row] + [vec] * ng, out_specs=[row] * ng,
                    out_shape=[jax.ShapeDtypeStruct((s, d), BF16)] * ng, args=(x, *gains), name=name)
    return outs


def _accumulate(i, ref, value):
    @pl.when(i == 0)
    def _():
        ref[...] = value

    @pl.when(i > 0)
    def _():
        ref[...] += value


def _mm_residual_norms(y, w, res, gains, *, name, tm=512, side=None):
    s, k = y.shape
    d = w.shape[1]
    tm = _row_tile(s, tm)
    ng = len(gains)

    def body(y_ref, w_ref, r_ref, *rest):
        g_refs, h_ref, n_refs = rest[:ng], rest[ng], rest[ng + 1:]
        h = r_ref[...] + jnp.dot(y_ref[...], w_ref[...], preferred_element_type=F32)
        h_ref[...] = h
        xh = h * _rstd(h)
        for t in range(ng):
            n_refs[t][...] = (xh * g_refs[t][...]).astype(BF16)

    row = pl.BlockSpec((tm, d), lambda i: (i, 0))
    vec = pl.BlockSpec((1, d), lambda i: (0, 0))
    return _call(
        body, grid=(s // tm,),
        in_specs=[pl.BlockSpec((tm, k), lambda i: (i, 0)), pl.BlockSpec((k, d), lambda i: (0, 0)), row] + [vec] * ng,
        out_specs=[row] * (1 + ng),
        out_shape=[jax.ShapeDtypeStruct((s, d), F32)] + [jax.ShapeDtypeStruct((s, d), BF16)] * ng,
        args=(y, w, res, *gains), name=name, side=side)


def _mm_residual_loss(y, w, res, tgt, gain, *, name, tm=512):
    s, k = y.shape
    d = w.shape[1]
    tm = _row_tile(s, tm)

    def body(y_ref, w_ref, r_ref, t_ref, g_ref, loss_ref, dh_ref, dhb_ref, dg_ref):
        i = pl.program_id(0)
        hv = r_ref[...] + jnp.dot(y_ref[...], w_ref[...], preferred_element_type=F32)
        g = g_ref[...]
        r = _rstd(hv)
        xh = hv * r
        diff = xh * g - t_ref[...]
        part = 0.5 / d * jnp.sum(jnp.sum(diff * diff, axis=-1, keepdims=True), axis=0, keepdims=True)
        dout = diff * (1.0 / d)
        a = dout * g
        dh = r * (a - xh * jnp.mean(a * xh, axis=-1, keepdims=True))
        dh_ref[...] = dh
        dhb_ref[...] = dh.astype(BF16)
        _accumulate(i, dg_ref, jnp.sum(dout * xh, axis=0, keepdims=True))
        _accumulate(i, loss_ref, jnp.broadcast_to(part, (8, 128)))

    row = pl.BlockSpec((tm, d), lambda i: (i, 0))
    vec = pl.BlockSpec((1, d), lambda i: (0, 0))
    outs, _ = _call(
        body, grid=(s // tm,),
        in_specs=[pl.BlockSpec((tm, k), lambda i: (i, 0)), pl.BlockSpec((k, d), lambda i: (0, 0)), row, row, vec],
        out_specs=[pl.BlockSpec((8, 128), lambda i: (0, 0)), row, row, vec],
        out_shape=[jax.ShapeDtypeStruct((8, 128), F32), jax.ShapeDtypeStruct((s, d), F32),
                   jax.ShapeDtypeStruct((s, d), BF16), jax.ShapeDtypeStruct((1, d), F32)],
        args=(y, w, res, tgt, gain), name=name)
    return outs


def _mm_nt_rms_bwd(terms, x, dres, *, name, tm, side=None):
    s, d = x.shape
    tm = _row_tile(s, tm)
    nt = len(terms)

    def body(*refs):
        a_refs, b_refs, g_refs = refs[0:3 * nt:3], refs[1:3 * nt:3], refs[2:3 * nt:3]
        x_ref, dres_ref = refs[3 * nt], refs[3 * nt + 1]
        dx_ref, dxb_ref = refs[3 * nt + 2], refs[3 * nt + 3]
        dg_refs = refs[3 * nt + 4:]
        i = pl.program_id(0)
        xv = x_ref[...]
        r = _rstd(xv)
        xh = xv * r
        acc = jnp.zeros_like(xv)
        for t in range(nt):
            b_ref = b_refs[t]
            if len(b_ref.shape) == 3:
                kc = b_ref.shape[2]
                dn = None
                for sh in range(b_ref.shape[0]):
                    part = lax.dot_general(a_refs[t][:, sh * kc:(sh + 1) * kc], b_ref[sh], NT, preferred_element_type=F32)
                    dn = part if dn is None else dn + part
            else:
                dn = lax.dot_general(a_refs[t][...], b_ref[...], NT, preferred_element_type=F32)
            acc = acc + dn * g_refs[t][...]
            _accumulate(i, dg_refs[t], jnp.sum(dn * xh, axis=0, keepdims=True))
        dx = dres_ref[...] + r * (acc - xh * jnp.mean(acc * xh, axis=-1, keepdims=True))
        dx_ref[...] = dx
        dxb_ref[...] = dx.astype(BF16)

    row = pl.BlockSpec((tm, d), lambda i: (i, 0))
    vec = pl.BlockSpec((1, d), lambda i: (0, 0))
    in_specs, args = [], []
    for a, b, g in terms:
        in_specs += [pl.BlockSpec((tm, a.shape[1]), lambda i: (i, 0)),
                     pl.BlockSpec(b.shape, (lambda i: (0, 0, 0)) if b.ndim == 3 else (lambda i: (0, 0))), vec]
        args += [a, b, g]
    return _call(
        body, grid=(s // tm,), in_specs=in_specs + [row, row], out_specs=[row, row] + [vec] * nt,
        out_shape=[jax.ShapeDtypeStruct((s, d), F32), jax.ShapeDtypeStruct((s, d), BF16)]
        + [jax.ShapeDtypeStruct((1, d), F32)] * nt,
        args=(*args, x, dres), name=name, side=side)


def _causal_mask(transposed=False):
    row = lax.broadcasted_iota(jnp.int32, (CHUNK, CHUNK), 0)
    col = lax.broadcasted_iota(jnp.int32, (CHUNK, CHUNK), 1)
    return col >= row if transposed else col <= row


def _silu_parts(g):
    sg = jax.nn.sigmoid(g)
    return g * sg, sg * (1.0 + g * (1.0 - sg))


def _gate_fwd(z, ln_g, ln_b, ws, bs_t, *, tr=256, side=None):
    s = z.shape[0]
    tr = _row_tile(s, tr)
    w = A_WIDTH

    def body(u_ref, v_ref, g_ref, lg_ref, lb_ref, ws_ref, bst_ref, y_ref):
        v = v_ref[...].astype(F32)
        mu = jnp.mean(v, axis=-1, keepdims=True)
        xc = v - mu
        rs = lax.rsqrt(jnp.mean(xc * xc, axis=-1, keepdims=True) + EPS)
        vln = (xc * rs * lg_ref[...] + lb_ref[...]).astype(BF16)
        mask = _causal_mask()
        for grp in range(A_GROUPS):
            cols = slice(grp * CHUNK, (grp + 1) * CHUNK)
            wsm = jnp.where(mask, ws_ref[grp], 0.0).astype(BF16)
            bcol = bst_ref[:, grp:grp + 1]
            for ci in range(tr // CHUNK):
                rows = slice(ci * CHUNK, (ci + 1) * CHUNK)
                sv = jnp.dot(wsm, vln[rows, cols], preferred_element_type=F32) + bcol
                gv = g_ref[rows, cols].astype(F32)
                y_ref[rows, cols] = (u_ref[rows, cols].astype(F32) * sv * (gv * jax.nn.sigmoid(gv))).astype(BF16)

    vec = pl.BlockSpec((1, w), lambda i: (0, 0))
    (y,), side_outs = _call(
        body, grid=(s // tr,),
        in_specs=[pl.BlockSpec((tr, w), lambda i: (i, 0)), pl.BlockSpec((tr, w), lambda i: (i, 1)),
                  pl.BlockSpec((tr, w), lambda i: (i, 2)), vec, vec,
                  pl.BlockSpec((A_GROUPS, CHUNK, CHUNK), lambda i: (0, 0, 0)),
                  pl.BlockSpec((CHUNK, A_GROUPS), lambda i: (0, 0))],
        out_specs=[pl.BlockSpec((tr, w), lambda i: (i, 0))],
        out_shape=[jax.ShapeDtypeStruct((s, w), BF16)], args=(z, z, z, ln_g, ln_b, ws, bs_t), name="gate_fwd",
        side=side)
    return y, side_outs


def _gate_bwd(z, dy, ln_g, ln_b, ws, ws_t, bs_t, *, tr=256, side=None):
    s = z.shape[0]
    tr = _row_tile(s, tr)
    w = A_WIDTH
    nsteps = s // tr

    def body(u_ref, v_ref, g_ref, dy_ref, lg_ref, lb_ref, ws_ref, wst_ref, bst_ref,
             dz_ref, dlg_ref, dlb_ref, dws_ref, dbst_ref, dvln_sc, dsv_sc):
        i = pl.program_id(0)

        @pl.when(i == 0)
        def _():
            dws_ref[...] = jnp.zeros_like(dws_ref)
            dsv_sc[...] = jnp.zeros_like(dsv_sc)

        v = v_ref[...].astype(F32)
        mu = jnp.mean(v, axis=-1, keepdims=True)
        xc = v - mu
        rs = lax.rsqrt(jnp.mean(xc * xc, axis=-1, keepdims=True) + EPS)
        xh = xc * rs
        lg = lg_ref[...]
        vln = (xh * lg + lb_ref[...]).astype(BF16)
        mask = _causal_mask()
        mask_t = _causal_mask(transposed=True)
        for grp in range(A_GROUPS):
            cols = slice(grp * CHUNK, (grp + 1) * CHUNK)
            wsm = jnp.where(mask, ws_ref[grp], 0.0).astype(BF16)
            wsm_t = jnp.where(mask_t, wst_ref[grp], 0.0).astype(BF16)
            bcol = bst_ref[:, grp:grp + 1]
            for ci in range(tr // CHUNK):
                rows = slice(ci * CHUNK, (ci + 1) * CHUNK)
                vb = vln[rows, cols]
                sv = jnp.dot(wsm, vb, preferred_element_type=F32) + bcol
                uv = u_ref[rows, cols].astype(F32)
                silu, dsilu = _silu_parts(g_ref[rows, cols].astype(F32))
                dyv = dy_ref[rows, cols].astype(F32)
                dyu = dyv * uv
                dz_ref[rows, cols] = (dyv * sv * silu).astype(BF16)
                dz_ref[rows, 2 * w + grp * CHUNK:2 * w + (grp + 1) * CHUNK] = (dyu * sv * dsilu).astype(BF16)
                dsv = dyu * silu
                dsvb = dsv.astype(BF16)
                dvln_sc[rows, cols] = jnp.dot(wsm_t, dsvb, preferred_element_type=F32)
                dws_ref[grp] += lax.dot_general(dsvb, vb, NT, preferred_element_type=F32)
                dsv_sc[grp] += dsv
        dvln = dvln_sc[...]
        dlg_t = jnp.sum(dvln * xh, axis=0, keepdims=True)
        dlb_t = jnp.sum(dvln, axis=0, keepdims=True)
        a = dvln * lg
        dv = rs * (a - jnp.mean(a, axis=-1, keepdims=True) - xh * jnp.mean(a * xh, axis=-1, keepdims=True))
        dz_ref[:, w:2 * w] = dv.astype(BF16)

        @pl.when(i == 0)
        def _():
            dlg_ref[...] = dlg_t
            dlb_ref[...] = dlb_t

        @pl.when(i > 0)
        def _():
            dlg_ref[...] += dlg_t
            dlb_ref[...] += dlb_t

        @pl.when(i == nsteps - 1)
        def _():
            for grp in range(A_GROUPS):
                dws_ref[grp] = jnp.where(mask, dws_ref[grp], 0.0)
                dbst_ref[:, grp:grp + 1] = jnp.sum(dsv_sc[grp], axis=-1, keepdims=True)

    vec = pl.BlockSpec((1, w), lambda i: (0, 0))
    wsspec = pl.BlockSpec((A_GROUPS, CHUNK, CHUNK), lambda i: (0, 0, 0))
    bsspec = pl.BlockSpec((CHUNK, A_GROUPS), lambda i: (0, 0))
    return _call(
        body, grid=(nsteps,),
        in_specs=[pl.BlockSpec((tr, w), lambda i: (i, 0)), pl.BlockSpec((tr, w), lambda i: (i, 1)),
                  pl.BlockSpec((tr, w), lambda i: (i, 2)), pl.BlockSpec((tr, w), lambda i: (i, 0)),
                  vec, vec, wsspec, wsspec, bsspec],
        out_specs=[pl.BlockSpec((tr, 3 * w), lambda i: (i, 0)), vec, vec, wsspec, bsspec],
        out_shape=[jax.ShapeDtypeStruct((s, 3 * w), BF16), jax.ShapeDtypeStruct((1, w), F32),
                   jax.ShapeDtypeStruct((1, w), F32), jax.ShapeDtypeStruct((A_GROUPS, CHUNK, CHUNK), F32),
                   jax.ShapeDtypeStruct((CHUNK, A_GROUPS), F32)],
        scratch=[pltpu.VMEM((tr, w), F32), pltpu.VMEM((A_GROUPS, CHUNK, CHUNK), F32)],
        args=(z, z, z, dy, ln_g, ln_b, ws, ws_t, bs_t), name="gate_bwd", side=side)


HEADS_PER_BLOCK = 128 // HEAD_DIM
BLOCKS_PER_KV = Q_PER_KV // HEADS_PER_BLOCK
SCALE = HEAD_DIM ** -0.5
LOG2_E = math.log2(math.e)


def _rope_tables(s):
    inv_freq = ROPE_THETA ** (-jnp.arange(0, HEAD_DIM, 2, dtype=F32) / HEAD_DIM)
    ang = jnp.arange(s, dtype=F32)[:, None] * inv_freq[None, :]
    cos, sin = jnp.cos(ang), jnp.sin(ang)
    cos2 = jnp.concatenate([cos, cos], axis=-1)
    sin2 = jnp.concatenate([-sin, sin], axis=-1)
    return jnp.tile(cos2, (1, 2)), jnp.tile(sin2, (1, 2))


def _swap_halves(x):
    n = x.shape[-1]
    lane = lax.broadcasted_iota(jnp.int32, x.shape, x.ndim - 1)
    first = (lane % HEAD_DIM) < (HEAD_DIM // 2)
    return jnp.where(first, pltpu.roll(x, n - HEAD_DIM // 2, x.ndim - 1), pltpu.roll(x, HEAD_DIM // 2, x.ndim - 1))


def _left_half(rows):
    return lax.broadcasted_iota(jnp.int32, (rows, 128), 1) < HEAD_DIM


def _dup_heads(x):
    left = _left_half(x.shape[0])
    swapped = pltpu.roll(x, HEAD_DIM, 1)
    return jnp.concatenate([jnp.where(left, x, swapped), jnp.where(left, swapped, x)], axis=-1)


def _fold_heads(a):
    b0, b1 = a[:, :128], a[:, 128:]
    f0 = b0 + pltpu.roll(b0, HEAD_DIM, 1)
    f1 = b1 + pltpu.roll(b1, HEAD_DIM, 1)
    return jnp.where(_left_half(a.shape[0]), f0, f1)


def _kv_rope(kv, b_kv, cos, sin, *, tr=512):
    s = kv.shape[0]
    tr = _row_tile(s, tr)

    def body(kv_ref, b_ref, c_ref, s_ref, k_ref, v_ref):
        x = kv_ref[...] + b_ref[...]
        k = x[:, :KV_WIDTH]
        k_ref[...] = _dup_heads(k * c_ref[...] + _swap_halves(k) * s_ref[...]).astype(BF16)
        v_ref[...] = _dup_heads(x[:, KV_WIDTH:]).astype(BF16)

    tab = pl.BlockSpec((tr, KV_WIDTH), lambda i: (i, 0))
    wide = pl.BlockSpec((tr, 2 * KV_WIDTH), lambda i: (i, 0))
    outs, _ = _call(body, grid=(s // tr,),
                    in_specs=[wide, pl.BlockSpec((1, 2 * KV_WIDTH), lambda i: (0, 0)), tab, tab],
                    out_specs=[wide, wide], out_shape=[jax.ShapeDtypeStruct((s, 2 * KV_WIDTH), BF16)] * 2,
                    args=(kv, b_kv, cos, sin), name="kv_rope")
    return outs


def _kv_rope_bwd(dk2, dv2, cos, sin, *, tr=512):
    s = dk2.shape[0]
    tr = _row_tile(s, tr)

    def body(dk_ref, dv_ref, c_ref, s_ref, dkv_ref, db_ref):
        i = pl.program_id(0)
        d = _fold_heads(dk_ref[...])
        dk = d * c_ref[...] + _swap_halves(d * s_ref[...])
        dvv = _fold_heads(dv_ref[...])
        dkv_ref[:, :KV_WIDTH] = dk.astype(BF16)
        dkv_ref[:, KV_WIDTH:] = dvv.astype(BF16)
        sk = jnp.sum(dk, axis=0, keepdims=True)
        sv = jnp.sum(dvv, axis=0, keepdims=True)

        @pl.when(i == 0)
        def _():
            db_ref[:, :KV_WIDTH] = sk
            db_ref[:, KV_WIDTH:] = sv

        @pl.when(i > 0)
        def _():
            db_ref[:, :KV_WIDTH] += sk
            db_ref[:, KV_WIDTH:] += sv

    tab = pl.BlockSpec((tr, KV_WIDTH), lambda i: (i, 0))
    wide = pl.BlockSpec((tr, 2 * KV_WIDTH), lambda i: (i, 0))
    outs, _ = _call(body, grid=(s // tr,), in_specs=[wide, wide, tab, tab],
                    out_specs=[wide, pl.BlockSpec((1, 2 * KV_WIDTH), lambda i: (0, 0))],
                    out_shape=[jax.ShapeDtypeStruct((s, 2 * KV_WIDTH), BF16),
                               jax.ShapeDtypeStruct((1, 2 * KV_WIDTH), F32)],
                    args=(dk2, dv2, cos, sin), name="kv_rope_bwd")
    return outs


def _from_previous():
    cols = Q_PER_KV * CHUNK
    k = lax.broadcasted_iota(jnp.int32, (CHUNK, cols), 0)
    q = lax.broadcasted_iota(jnp.int32, (CHUNK, cols), 1) & (CHUNK - 1)
    return k > q


def _fold(x2, prev):
    return jnp.where(prev, x2[:CHUNK], x2[CHUNK:])


def _unfold(x, prev):
    zero = jnp.zeros_like(x)
    return jnp.concatenate([jnp.where(prev, x, zero), jnp.where(prev, zero, x)], axis=0)


def _stack_heads(blocks, left):
    parts = []
    for b in blocks:
        parts.append(jnp.where(left, b, jnp.zeros_like(b)))
        parts.append(jnp.where(left, jnp.zeros_like(b), b))
    return jnp.concatenate(parts, axis=0)


def _unstack_heads(xt):
    top = lax.broadcasted_iota(jnp.int32, (128, CHUNK), 0) < HEAD_DIM
    return [jnp.where(top, xt[:, (2 * b) * CHUNK:(2 * b + 1) * CHUNK], xt[:, (2 * b + 1) * CHUNK:(2 * b + 2) * CHUNK]).T
            for b in range(BLOCKS_PER_KV)]


def _sink_row(sk_ref, kvh):
    return jnp.concatenate([jnp.full((1, CHUNK), sk_ref[0, kvh * Q_PER_KV + r], F32) for r in range(Q_PER_KV)], axis=1)


def _stacked_probs(qs, kd, prev, sink, i):
    sc2 = lax.dot_general(kd, qs, NT, preferred_element_type=F32)
    no_previous = jnp.where(i > 0, 0.0, NEG_BIG)
    sc = jnp.where(prev, sc2[:CHUNK] + no_previous, sc2[CHUNK:])
    sink = sink * (1.0 / SCALE)
    m = jnp.maximum(jnp.max(sc, axis=0, keepdims=True), sink)
    p = jnp.exp2((sc - m) * (SCALE * LOG2_E))
    esink = jnp.exp2((sink - m) * (SCALE * LOG2_E))
    inv = 1.0 / (jnp.sum(p, axis=0, keepdims=True) + esink)
    return p * inv, esink * inv


def _lane_block(b):
    return slice(b * 128, (b + 1) * 128)


def _rope_blocks(zq_ref, bq_ref, cos, sin, kvh):
    out = []
    for b in range(BLOCKS_PER_KV):
        cols = _lane_block(kvh * BLOCKS_PER_KV + b)
        q = zq_ref[:, cols].astype(F32) + bq_ref[:, cols]
        out.append((q * cos + _swap_halves(q) * sin).astype(BF16))
    return out


def _attn_specs():
    qspec = pl.BlockSpec((CHUNK, B_WIDTH), lambda i: (i, 0))
    gspec = pl.BlockSpec((CHUNK, B_WIDTH), lambda i: (i, 1))
    prev = pl.BlockSpec((CHUNK, 2 * KV_WIDTH), lambda i: (jnp.maximum(i - 1, 0), 0))
    cur = pl.BlockSpec((CHUNK, 2 * KV_WIDTH), lambda i: (i, 0))
    tab = pl.BlockSpec((CHUNK, KV_WIDTH), lambda i: (i, 0))
    bq = pl.BlockSpec((1, B_WIDTH), lambda i: (0, 0))
    sinks = pl.BlockSpec(memory_space=pltpu.SMEM)
    return qspec, gspec, prev, cur, tab, bq, sinks


def _attn_fwd(zb, k2, v2, cos, sin, b_bq, sinks, *, side=None):
    s = zb.shape[0]

    def body(zq_ref, zg_ref, kp_ref, kc_ref, vp_ref, vc_ref, c_ref, s_ref, bq_ref, sk_ref, y_ref):
        i = pl.program_id(0)
        cos, sin = c_ref[...], s_ref[...]
        kcat = jnp.concatenate([kp_ref[...], kc_ref[...]], axis=0)
        vcat = jnp.concatenate([vp_ref[...], vc_ref[...]], axis=0)
        prev = _from_previous()
        left = _left_half(CHUNK)
        for kvh in range(N_KV_HEADS):
            qs = _stack_heads(_rope_blocks(zq_ref, bq_ref, cos, sin, kvh), left)
            p, _ = _stacked_probs(qs, kcat[:, _lane_block(kvh)], prev, _sink_row(sk_ref, kvh), i)
            ot = lax.dot_general(vcat[:, _lane_block(kvh)], _unfold(p, prev).astype(BF16), TN,
                                 preferred_element_type=F32)
            for b, ob in enumerate(_unstack_heads(ot)):
                cols = _lane_block(kvh * BLOCKS_PER_KV + b)
                gv = zg_ref[:, cols].astype(F32)
                y_ref[:, cols] = (ob * (gv * jax.nn.sigmoid(gv))).astype(BF16)

    qspec, gspec, prev, cur, tab, bq, sk = _attn_specs()
    (y,), side_outs = _call(body, grid=(s // CHUNK,), in_specs=[qspec, gspec, prev, cur, prev, cur, tab, tab, bq, sk],
                            out_specs=[qspec], out_shape=[jax.ShapeDtypeStruct((s, B_WIDTH), BF16)],
                            args=(zb, zb, k2, k2, v2, v2, cos, sin, b_bq, sinks), name="attn_fwd", side=side)
    return y, side_outs


def _attn_bwd(zb, dyb, k2, v2, cos, sin, b_bq, sinks):
    s = zb.shape[0]

    def body(zq_ref, zg_ref, dy_ref, kp_ref, kc_ref, vp_ref, vc_ref, c_ref, s_ref, bq_ref, sk_ref,
             dz_ref, dk_ref, dv_ref, dbq_ref, dsk_ref):
        i = pl.program_id(0)

        @pl.when(i == 0)
        def _():
            dk_ref[...] = jnp.zeros_like(dk_ref)
            dv_ref[...] = jnp.zeros_like(dv_ref)
            dbq_ref[...] = jnp.zeros_like(dbq_ref)
            dsk_ref[...] = jnp.zeros_like(dsk_ref)

        cos, sin = c_ref[...], s_ref[...]
        kcat = jnp.concatenate([kp_ref[...], kc_ref[...]], axis=0)
        vcat = jnp.concatenate([vp_ref[...], vc_ref[...]], axis=0)
        prev = _from_previous()
        left = _left_half(CHUNK)
        lane = lax.broadcasted_iota(jnp.int32, (1, 128), 1)
        dsk_row = jnp.zeros((1, 128), F32)
        cur_rows = pl.ds(pl.multiple_of(i * CHUNK, CHUNK), CHUNK)
        for kvh in range(N_KV_HEADS):
            kd, vd = kcat[:, _lane_block(kvh)], vcat[:, _lane_block(kvh)]
            qs = _stack_heads(_rope_blocks(zq_ref, bq_ref, cos, sin, kvh), left)
            p, psink = _stacked_probs(qs, kd, prev, _sink_row(sk_ref, kvh), i)
            pb = _unfold(p, prev).astype(BF16)
            ot = lax.dot_general(vd, pb, TN, preferred_element_type=F32)
            gates, dys = [], []
            for b in range(BLOCKS_PER_KV):
                cols = _lane_block(kvh * BLOCKS_PER_KV + b)
                gates.append(_silu_parts(zg_ref[:, cols].astype(F32)))
                dys.append(dy_ref[:, cols].astype(F32))
            dos = _stack_heads([(dyv * silu).astype(BF16) for dyv, (silu, _) in zip(dys, gates)], left)
            dp = _fold(lax.dot_general(vd, dos, NT, preferred_element_type=F32), prev)
            delta = jnp.sum(p * dp, axis=0, keepdims=True)
            ds = _unfold(p * (dp - delta) * SCALE, prev).astype(BF16)
            dqt = lax.dot_general(kd, ds, TN, preferred_element_type=F32)
            dk_part = jnp.dot(ds, qs, preferred_element_type=F32)
            dv_part = jnp.dot(pb, dos, preferred_element_type=F32)
            dk_ref[cur_rows, _lane_block(kvh)] += dk_part[CHUNK:]
            dv_ref[cur_rows, _lane_block(kvh)] += dv_part[CHUNK:]

            @pl.when(i > 0)
            def _(kvh=kvh, dk_part=dk_part, dv_part=dv_part):
                prev_rows = pl.ds(pl.multiple_of((i - 1) * CHUNK, CHUNK), CHUNK)
                dk_ref[prev_rows, _lane_block(kvh)] += dk_part[:CHUNK]
                dv_ref[prev_rows, _lane_block(kvh)] += dv_part[:CHUNK]

            sink_grad = psink * delta
            for r in range(Q_PER_KV):
                dsink = -jnp.sum(sink_grad[:, r * CHUNK:(r + 1) * CHUNK], axis=1, keepdims=True)
                dsk_row = dsk_row + jnp.where(lane == kvh * Q_PER_KV + r, dsink, 0.0)
            blocks = zip(_unstack_heads(ot), _unstack_heads(dqt), dys, gates)
            for b, (ob, dqr, dyv, (_, dsilu)) in enumerate(blocks):
                blk = kvh * BLOCKS_PER_KV + b
                dq = dqr * cos + _swap_halves(dqr * sin)
                dbq_ref[:, _lane_block(blk)] += jnp.sum(dq, axis=0, keepdims=True)
                dz_ref[:, _lane_block(blk)] = dq.astype(BF16)
                dz_ref[:, _lane_block(B_WIDTH // 128 + blk)] = (dyv * ob * dsilu).astype(BF16)
        dsk_ref[0:1, :] += dsk_row

    qspec, gspec, prev, cur, tab, bq, sk = _attn_specs()
    full = pl.BlockSpec((s, 2 * KV_WIDTH), lambda i: (0, 0))
    outs, _ = _call(
        body, grid=(s // CHUNK,),
        in_specs=[qspec, gspec, qspec, prev, cur, prev, cur, tab, tab, bq, sk],
        out_specs=[pl.BlockSpec((CHUNK, 2 * B_WIDTH), lambda i: (i, 0)), full, full, bq,
                   pl.BlockSpec((8, 128), lambda i: (0, 0))],
        out_shape=[jax.ShapeDtypeStruct((s, 2 * B_WIDTH), BF16), jax.ShapeDtypeStruct((s, 2 * KV_WIDTH), F32),
                   jax.ShapeDtypeStruct((s, 2 * KV_WIDTH), F32), jax.ShapeDtypeStruct((1, B_WIDTH), F32),
                   jax.ShapeDtypeStruct((8, 128), F32)],
        args=(zb, zb, dyb, k2, k2, v2, v2, cos, sin, b_bq, sinks), name="attn_bwd")
    return outs


def _place():
    x, y, c = lax.axis_index("x"), lax.axis_index("y"), lax.axis_index("c")
    return x, y, c, [(1 - x, y), (x, 1 - y), (1 - x, 1 - y)]


def _relations():
    return [(r >> 2 & 1, r >> 1 & 1, r & 1) for r in range(1, 8)]


def _gather_side(arrs):
    n = len(arrs)

    def copies(ins, outs, sems):
        send_ici, recv_ici, send_d2d, recv_d2d, local_sem = sems
        x, y, c, chips = _place()
        me = 2 * x + y

        def rows(a, half):
            hr = arrs[a].shape[0] // 2
            return pl.ds(half * hr, hr)

        def ici(a, j, src_chip, to):
            return pltpu.make_async_remote_copy(
                src_ref=ins[a].at[rows(a, c)], dst_ref=outs[a].at[src_chip, rows(a, c)],
                send_sem=send_ici.at[a, j], recv_sem=recv_ici.at[a, j], device_id=to, device_id_type=MESH)

        def d2d(a, j, chip, half):
            blk = outs[a].at[chip, rows(a, half)]
            return pltpu.make_async_remote_copy(
                src_ref=blk, dst_ref=blk, send_sem=send_d2d.at[a, j], recv_sem=recv_d2d.at[a, j],
                device_id=(x, y, 1 - c), device_id_type=MESH)

        local = [pltpu.make_async_copy(ins[a], outs[a].at[me], local_sem.at[a]) for a in range(n)]
        pairs = [(a, j, chip) for a in range(n) for j, chip in enumerate(chips)]
        return c, me, local, ici, d2d, pairs

    def start(ins, outs, sems):
        c, me, local, ici, _, pairs = copies(ins, outs, sems)
        for cp in local:
            cp.start()
        for a, j, chip in pairs:
            ici(a, j, me, (*chip, c)).start()

    def passing(ins, outs, sems):
        c, _, _, ici, d2d, pairs = copies(ins, outs, sems)
        for a, j, (px, py) in pairs:
            ici(a, j, 2 * px + py, (px, py, c)).wait_recv()
            d2d(a, j, 2 * px + py, c).start()

    def finish(ins, outs, sems):
        c, me, local, ici, d2d, pairs = copies(ins, outs, sems)
        for a, j, (px, py) in pairs:
            d2d(a, j, 2 * px + py, 1 - c).wait_recv()
        for a, j, (px, py) in pairs:
            ici(a, j, me, (px, py, c)).wait_send()
            d2d(a, j, 2 * px + py, c).wait_send()
        for cp in local:
            cp.wait()

    return _Side(arrs, [jax.ShapeDtypeStruct((N_CHIPS,) + a.shape, a.dtype) for a in arrs],
                 [pltpu.SemaphoreType.DMA((n, 3))] * 4 + [pltpu.SemaphoreType.DMA((n,))], start, finish,
                 passing=passing)


def _exchange_side(grads):
    n = len(grads)

    def copies(ins, outs, sems):
        send_sem, recv_sem = sems
        x, y, c, _ = _place()
        cps = []
        for a in range(n):
            hr = grads[a].shape[1] // 2
            cps.append(pltpu.make_async_remote_copy(
                src_ref=ins[a].at[:, pl.ds((1 - c) * hr, hr), :], dst_ref=outs[a],
                send_sem=send_sem.at[a], recv_sem=recv_sem.at[a], device_id=(x, y, 1 - c), device_id_type=MESH))
        return cps

    def start(ins, outs, sems):
        for cp in copies(ins, outs, sems):
            cp.start()

    def finish(ins, outs, sems):
        for cp in copies(ins, outs, sems):
            cp.wait()

    return _Side(grads, [jax.ShapeDtypeStruct((g.shape[0], g.shape[1] // 2, g.shape[2]), g.dtype) for g in grads],
                 [pltpu.SemaphoreType.DMA((n,))] * 2, start, finish)


def _scatter_side(chip_sums, small=None):
    n = len(chip_sums)
    arrs = list(chip_sums) + ([small] if small is not None else [])

    def copies(ins, outs, sems):
        x, y, c, chips = _place()
        cps = []
        for a in range(n):
            for j, (px, py) in enumerate(chips):
                cps.append(pltpu.make_async_remote_copy(
                    src_ref=ins[a].at[2 * px + py], dst_ref=outs[a].at[j],
                    send_sem=sems[0].at[a, j], recv_sem=sems[1].at[a, j], device_id=(px, py, c), device_id_type=MESH))
        if small is not None:
            for r, (fx, fy, fc) in enumerate(_relations(), start=1):
                px, py, pc = x ^ fx, y ^ fy, c ^ fc
                cps.append(pltpu.make_async_remote_copy(
                    src_ref=ins[n].at[4 * px + 2 * py + pc], dst_ref=outs[n].at[r],
                    send_sem=sems[2].at[r - 1], recv_sem=sems[3].at[r - 1], device_id=(px, py, pc),
                    device_id_type=MESH))
        return cps

    def start(ins, outs, sems):
        for cp in copies(ins, outs, sems):
            cp.start()

    def finish(ins, outs, sems):
        for cp in copies(ins, outs, sems):
            cp.wait()

    shapes = [jax.ShapeDtypeStruct((3,) + t.shape[1:], t.dtype) for t in chip_sums]
    sems = [pltpu.SemaphoreType.DMA((n, 3))] * 2
    if small is not None:
        shapes.append(jax.ShapeDtypeStruct(small.shape, small.dtype))
        sems += [pltpu.SemaphoreType.DMA((7,))] * 2
    return _Side(arrs, shapes, sems, start, finish)


def _small_scatter_side(small):
    def copies(ins, outs, sems):
        x, y, c, _ = _place()
        cps = []
        for r, (fx, fy, fc) in enumerate(_relations(), start=1):
            px, py, pc = x ^ fx, y ^ fy, c ^ fc
            cps.append(pltpu.make_async_remote_copy(
                src_ref=ins[0].at[4 * px + 2 * py + pc], dst_ref=outs[0].at[r],
                send_sem=sems[0].at[r - 1], recv_sem=sems[1].at[r - 1], device_id=(px, py, pc), device_id_type=MESH))
        return cps

    def start(ins, outs, sems):
        for cp in copies(ins, outs, sems):
            cp.start()

    def finish(ins, outs, sems):
        for cp in copies(ins, outs, sems):
            cp.wait()

    return _Side([small], [jax.ShapeDtypeStruct(small.shape, small.dtype)], [pltpu.SemaphoreType.DMA((7,))] * 2,
                 start, finish)


def _share_side(halves, small=None):
    n = len(halves)
    arrs = list(halves) + ([small] if small is not None else [])

    def copies(ins, outs, sems, mine):
        x, y, c, _ = _place()
        me = 4 * x + 2 * y + c
        cps = []
        for a in range(n):
            hr = halves[a].shape[0] // 2
            rows = pl.ds((c if mine else 1 - c) * hr, hr)
            cps.append(pltpu.make_async_remote_copy(
                src_ref=ins[a].at[rows], dst_ref=outs[a].at[rows],
                send_sem=sems[0].at[a], recv_sem=sems[1].at[a], device_id=(x, y, 1 - c), device_id_type=MESH))
        if small is not None:
            for r, (fx, fy, fc) in enumerate(_relations(), start=1):
                px, py, pc = x ^ fx, y ^ fy, c ^ fc
                seg = me if mine else 4 * px + 2 * py + pc
                cps.append(pltpu.make_async_remote_copy(
                    src_ref=ins[n].at[seg], dst_ref=outs[n].at[seg],
                    send_sem=sems[2].at[r - 1], recv_sem=sems[3].at[r - 1], device_id=(px, py, pc),
                    device_id_type=MESH))
        return cps

    def start(ins, outs, sems):
        for cp in copies(ins, outs, sems, True):
            cp.start()

    def finish(ins, outs, sems):
        for cp in copies(ins, outs, sems, False):
            cp.wait_recv()
        for cp in copies(ins, outs, sems, True):
            cp.wait_send()

    sems = [pltpu.SemaphoreType.DMA((n,))] * 2 + ([pltpu.SemaphoreType.DMA((7,))] * 2 if small is not None else [])
    return _Side(arrs, [jax.ShapeDtypeStruct(h.shape, h.dtype) for h in arrs], sems, start, finish,
                 aliases={i: i for i in range(len(arrs))})


GATHER_PIECES = [(0, 0), (0, 1), (1, 0), (2, 0), (1, 1), (2, 1), (3, 0), (3, 1)]


def _mm_gathering(a, shard, order, *, name, tm=1024):
    s, k = a.shape
    nc = shard.shape[1]
    tm = _row_tile(s, tm)
    tn = nc // 2
    hr = k // 2
    qr = hr // 2
    blocks = jnp.stack([order[src] * 2 + h for src, h in GATHER_PIECES]).astype(jnp.int32)

    def body(blocks_ref, a_ref, shard_ref, z_ref, full_ref, wbuf, send_ici, recv_ici, send_relay,
             recv_relay, send_d2d, recv_d2d, local_sem, load_sem):
        piece, i = pl.program_id(0), pl.program_id(1)
        x, y, c, chips = _place()
        me = 2 * x + y
        nbrs = chips[:2]
        chip_of = [2 * px + py for px, py in chips]

        def quarter(q):
            return pl.ds(c * hr + q * qr, qr)

        def sibling_quarter(q):
            return pl.ds((1 - c) * hr + q * qr, qr)

        def whole(half):
            return pl.ds(half * hr, hr)

        def cols(h):
            return pl.ds(h * tn, tn)

        def direct(j, src_chip, h):
            return pltpu.make_async_remote_copy(
                src_ref=shard_ref.at[whole(c), cols(h)], dst_ref=full_ref.at[src_chip, whole(c), cols(h)],
                send_sem=send_ici.at[j, h], recv_sem=recv_ici.at[j, h], device_id=(*nbrs[j], c), device_id_type=MESH)

        def relay(j, src_chip, h):
            blk = full_ref.at[src_chip, quarter(j), cols(h)]
            return pltpu.make_async_remote_copy(
                src_ref=blk, dst_ref=blk, send_sem=send_relay.at[j, h], recv_sem=recv_relay.at[j, h],
                device_id=(*nbrs[1 - j], c), device_id_type=MESH)

        def d2d(j, chip, rows, h):
            blk = full_ref.at[chip, rows, cols(h)]
            return pltpu.make_async_remote_copy(
                src_ref=blk, dst_ref=blk, send_sem=send_d2d.at[j, h], recv_sem=recv_d2d.at[j, h],
                device_id=(x, y, 1 - c), device_id_type=MESH)

        def load(p):
            src, h = GATHER_PIECES[p]
            where = shard_ref if src == 0 else full_ref.at[chip_of[src - 1]]
            return pltpu.make_async_copy(where.at[:, cols(h)], wbuf.at[p % 2], load_sem.at[p % 2])

        local = pltpu.make_async_copy(shard_ref, full_ref.at[me], local_sem)

        def arrived(p):
            src, h = GATHER_PIECES[p]
            if src in (1, 2):
                j = src - 1
                direct(j, chip_of[j], h).wait_recv()
                relay(j, chip_of[j], h).start()
                d2d(j, chip_of[j], whole(c), h).start()
            elif src == 3:
                for j in range(2):
                    relay(1 - j, chip_of[2], h).wait_recv()
                    d2d(2 + j, chip_of[2], quarter(1 - j), h).start()

        def fetch(p):
            src, h = GATHER_PIECES[p]
            if src in (1, 2):
                d2d(src - 1, chip_of[src - 1], whole(1 - c), h).wait_recv()
            elif src == 3:
                for j in range(2):
                    d2d(2 + j, chip_of[2], sibling_quarter(1 - j), h).wait_recv()
            load(p).start()

        n_i = s // tm
        for p in range(len(GATHER_PIECES)):
            @pl.when(jnp.logical_and(piece == p, i == 0))
            def _(p=p):
                if p == 0:
                    local.start()
                    for hh in range(2):
                        for j in range(2):
                            direct(j, me, hh).start()
                    load(0).start()
                load(p).wait()

        z_ref[...] = jnp.dot(a_ref[...], wbuf[piece % 2], preferred_element_type=F32).astype(z_ref.dtype)

        for p in range(len(GATHER_PIECES) - 1):
            @pl.when(jnp.logical_and(piece == p, i == min(1, n_i - 1)))
            def _(p=p):
                arrived(p + 1)

            @pl.when(jnp.logical_and(piece == p, i == min(2, n_i - 1)))
            def _(p=p):
                fetch(p + 1)

        last = jnp.logical_and(piece == len(GATHER_PIECES) - 1, i == n_i - 1)

        @pl.when(last)
        def _():
            for h in range(2):
                for j in range(2):
                    direct(j, me, h).wait_send()
                    relay(j, chip_of[j], h).wait_send()
                    d2d(j, chip_of[j], whole(c), h).wait_send()
                    d2d(2 + j, chip_of[2], quarter(1 - j), h).wait_send()
            local.wait()

    return pl.pallas_call(
        body,
        grid_spec=pltpu.PrefetchScalarGridSpec(
            num_scalar_prefetch=1, grid=(len(GATHER_PIECES), s // tm),
            in_specs=[pl.BlockSpec((tm, k), lambda p, i, blocks: (i, 0)), HBM],
            out_specs=[pl.BlockSpec((tm, tn), lambda p, i, blocks: (i, blocks[p])), HBM],
            scratch_shapes=[pltpu.VMEM((2, k, tn), BF16)] + [pltpu.SemaphoreType.DMA((2, 2))] * 4
            + [pltpu.SemaphoreType.DMA((4, 2))] * 2 + [pltpu.SemaphoreType.DMA, pltpu.SemaphoreType.DMA((2,))]),
        out_shape=[jax.ShapeDtypeStruct((s, N_CHIPS * nc), BF16), jax.ShapeDtypeStruct((N_CHIPS, k, nc), BF16)],
        name=name, compiler_params=_cparams(),
    )(blocks, a, shard)


def _mm_tn_exchanging(a, b, *, name, shards, tk=2048, side=None):
    s, m = a.shape
    nc = b.shape[1] // shards
    tk = _row_tile(s, tk)
    nk = s // tk
    hm = m // 2

    def body(a_ref, b_ref, part_ref, sib_ref, acc, keep_sem, send_sem, recv_sem):
        j, kk = pl.program_id(0), pl.program_id(1)
        x, y, c, _ = _place()

        def keep(jj, slot):
            mine = pl.ds(c * hm, hm)
            return pltpu.make_async_copy(acc.at[slot, mine], part_ref.at[jj, mine], keep_sem.at[slot])

        def give(jj, slot):
            return pltpu.make_async_remote_copy(
                src_ref=acc.at[slot, pl.ds((1 - c) * hm, hm)], dst_ref=sib_ref.at[jj],
                send_sem=send_sem.at[slot], recv_sem=recv_sem.at[jj], device_id=(x, y, 1 - c), device_id_type=MESH)

        part = lax.dot_general(a_ref[...], b_ref[...], TN, preferred_element_type=F32)
        for slot in range(2):
            @pl.when(j % 2 == slot)
            def _(slot=slot):
                @pl.when(jnp.logical_and(kk == 0, j >= 2))
                def _():
                    keep(j - 2, slot).wait()
                    give(j - 2, slot).wait_send()

                @pl.when(kk == 0)
                def _():
                    acc[slot] = part

                @pl.when(kk > 0)
                def _():
                    acc[slot] += part

                @pl.when(kk == nk - 1)
                def _():
                    keep(j, slot).start()
                    give(j, slot).start()

        @pl.when(jnp.logical_and(j == shards - 1, kk == nk - 1))
        def _():
            for jj in range(shards - 2, shards):
                keep(jj, jj % 2).wait()
                give(jj, jj % 2).wait_send()
            for jj in range(shards):
                give(jj, jj % 2).wait_recv()

    assert shards >= 2
    return _call(
        body, grid=(shards, nk),
        in_specs=[pl.BlockSpec((tk, m), lambda j, kk: (kk, 0)), pl.BlockSpec((tk, nc), lambda j, kk: (kk, j))],
        out_specs=[HBM, HBM],
        out_shape=[jax.ShapeDtypeStruct((shards, m, nc), F32), jax.ShapeDtypeStruct((shards, hm, nc), F32)],
        scratch=[pltpu.VMEM((2, m, nc), F32), pltpu.SemaphoreType.DMA((2,)), pltpu.SemaphoreType.DMA((2,)),
                 pltpu.SemaphoreType.DMA((shards,))],
        args=(a, b), name=name, side=side)


def _col_tile(cols):
    return cols if cols <= 2048 else 512


def _add_sibling(grad, recv, core, *, name):
    k, r, c = grad.shape
    hr = r // 2
    tr = min(hr, 256)
    tc = _col_tile(c)
    nrb = hr // tr

    def body(core_ref, g_ref, r_ref, o_ref):
        o_ref[...] = (g_ref[...] + r_ref[...]).astype(BF16)

    return pl.pallas_call(
        body,
        grid_spec=pltpu.PrefetchScalarGridSpec(
            num_scalar_prefetch=1, grid=(k, nrb, c // tc),
            in_specs=[pl.BlockSpec((None, tr, tc), lambda kk, i, j, core: (kk, core[0] * nrb + i, j)),
                      pl.BlockSpec((None, tr, tc), lambda kk, i, j, core: (kk, i, j))],
            out_specs=pl.BlockSpec((None, tr, tc), lambda kk, i, j, core: (kk, i, j))),
        out_shape=jax.ShapeDtypeStruct((k, hr, c), BF16), name=name, compiler_params=_cparams(),
    )(core, grad, recv)


def _sum_chips(grad, from_sibling, recv, place, *, name):
    _, hr, c = from_sibling.shape
    tr = min(hr, 256)
    tc = _col_tile(c)
    nrb = hr // tr

    def body(place_ref, g_ref, s_ref, r0_ref, r1_ref, r2_ref, o_ref):
        own = g_ref[...] + s_ref[...]
        o_ref[...] = ((own + r0_ref[...].astype(F32)) + r1_ref[...].astype(F32)) + r2_ref[...].astype(F32)

    def rspec(j):
        return pl.BlockSpec((None, tr, tc), lambda i, jj, place: (j, i, jj))

    return pl.pallas_call(
        body,
        grid_spec=pltpu.PrefetchScalarGridSpec(
            num_scalar_prefetch=1, grid=(nrb, c // tc),
            in_specs=[pl.BlockSpec((None, tr, tc), lambda i, jj, place: (place[0], place[1] * nrb + i, jj)),
                      pl.BlockSpec((None, tr, tc), lambda i, jj, place: (place[0], i, jj)),
                      rspec(0), rspec(1), rspec(2)],
            out_specs=pl.BlockSpec((tr, tc), lambda i, jj, place: (place[1] * nrb + i, jj))),
        out_shape=jax.ShapeDtypeStruct((2 * hr, c), F32), name=name, compiler_params=_cparams(),
    )(place, grad, from_sibling, recv, recv, recv)


def _sum_small(small, recv, place):
    _, sr, _ = small.shape

    def body(place_ref, own_ref, r_ref, o_ref):
        acc = own_ref[...]
        for r in range(1, 8):
            acc = acc + r_ref[r]
        o_ref[...] = acc

    return pl.pallas_call(
        body,
        grid_spec=pltpu.PrefetchScalarGridSpec(
            num_scalar_prefetch=1, grid=(1,),
            in_specs=[pl.BlockSpec((None, sr, 128), lambda i, place: (place[2], 0, 0)),
                      pl.BlockSpec((8, sr, 128), lambda i, place: (0, 0, 0))],
            out_specs=pl.BlockSpec((None, sr, 128), lambda i, place: (place[2], 0, 0))),
        out_shape=jax.ShapeDtypeStruct(small.shape, F32), name="sum_small", compiler_params=_cparams(),
    )(place, small, recv)


def _spread_side(vec):
    def copies(ins, outs, sems):
        x, y, c, _ = _place()
        return [pltpu.make_async_remote_copy(
            src_ref=ins[0], dst_ref=outs[0].at[r], send_sem=sems[0].at[r - 1], recv_sem=sems[1].at[r - 1],
            device_id=(x ^ fx, y ^ fy, c ^ fc), device_id_type=MESH)
            for r, (fx, fy, fc) in enumerate(_relations(), start=1)]

    def start(ins, outs, sems):
        for cp in copies(ins, outs, sems):
            cp.start()

    def finish(ins, outs, sems):
        for cp in copies(ins, outs, sems):
            cp.wait()

    return _Side([vec], [jax.ShapeDtypeStruct((8,) + vec.shape, vec.dtype)], [pltpu.SemaphoreType.DMA((7,))] * 2,
                 start, finish)


def _sum_in_device_order(own, spread, place):
    def body(place_ref, own_ref, r_ref, o_ref):
        me = place_ref[2]
        acc = jnp.zeros_like(own_ref[...])
        for d in range(8):
            slot = jnp.where(me == d, 1, me ^ d)
            acc = acc + jnp.where(me == d, own_ref[...], r_ref[slot])
        o_ref[...] = acc

    return pl.pallas_call(
        body,
        grid_spec=pltpu.PrefetchScalarGridSpec(
            num_scalar_prefetch=1, grid=(1,),
            in_specs=[pl.BlockSpec(own.shape, lambda i, place: (0, 0)),
                      pl.BlockSpec(spread.shape, lambda i, place: (0, 0, 0))],
            out_specs=pl.BlockSpec(own.shape, lambda i, place: (0, 0))),
        out_shape=jax.ShapeDtypeStruct(own.shape, F32), name="sum_in_device_order", compiler_params=_cparams(),
    )(place, own, spread)


def _adamw(w, g, m, v, *, name):
    r, c = w.shape
    tr = 256 if r % 256 == 0 else r
    tc = _col_tile(c)
    bc1 = 1.0 - ADAM_B1 ** ADAM_STEP
    bc2 = 1.0 - ADAM_B2 ** ADAM_STEP

    def body(w_ref, g_ref, m_ref, v_ref, d_ref, nm_ref, nv_ref):
        gv = g_ref[...]
        nm = ADAM_B1 * m_ref[...] + (1.0 - ADAM_B1) * gv
        nv = ADAM_B2 * v_ref[...] + (1.0 - ADAM_B2) * (gv * gv)
        d_ref[...] = -ADAM_LR * ((nm / bc1) / (jnp.sqrt(nv / bc2) + ADAM_EPS) + ADAM_WD * w_ref[...])
        nm_ref[...] = nm
        nv_ref[...] = nv

    spec = pl.BlockSpec((tr, tc), lambda i, j: (i, j))
    outs, _ = _call(body, grid=(r // tr, c // tc), in_specs=[spec] * 4, out_specs=[spec] * 3,
                    out_shape=[jax.ShapeDtypeStruct((r, c), F32)] * 3, args=(w, g, m, v), name=name)
    return outs


SMALL_ORDER = ["a_ws", "a_bs", "a_norm_g", "a_ln_g", "a_ln_b", "kv_norm_g", "b_kv", "b_norm_g", "b_bq",
               "b_sinks", "final_norm_g"]
SHARDED_SMALL = {"a_norm_g", "a_ln_g", "a_ln_b"}
PACK_TILE = 8 * 128


def _rows128(a):
    flat = a.reshape(-1)
    return jnp.pad(flat, (0, (-flat.shape[0]) % PACK_TILE)).reshape(-1, 128)


def _pack_rows(parts, multiple):
    rows = [_rows128(p) for p in parts]
    total = sum(r.shape[0] for r in rows)
    pad = (-total) % multiple
    if pad:
        rows.append(jnp.zeros((pad, 128), rows[0].dtype))
    return jnp.concatenate(rows, axis=0)


def _unpack_rows(packed, shapes):
    out, row = [], 0
    for shp in shapes:
        size = math.prod(shp)
        nrow = -(-size // PACK_TILE) * 8
        out.append(packed[row:row + nrow].reshape(-1)[:size].reshape(shp))
        row += nrow
    return out


WEIGHTS = ["a_norm_g", "a_w_in", "a_ln_g", "a_ln_b", "a_ws", "a_bs", "a_w_out", "kv_norm_g", "w_kv", "b_kv",
           "b_norm_g", "b_w_in", "b_bq", "b_sinks", "b_w_out", "final_norm_g"]
BIG = ["a_w_in", "a_w_out", "w_kv", "b_w_in", "b_w_out"]


class _Reduction:
    def __init__(self, names, partials, core, place, small=None):
        self.names, self.partials, self.core, self.place, self.small = names, partials, core, place, small

    def exchange_side(self):
        return _exchange_side(self.partials)

    def took_exchange(self, from_sibling):
        self.from_sibling = from_sibling
        self.chip_sums = [_add_sibling(g, r, self.core, name="add_sibling_" + n)
                          for g, r, n in zip(self.partials, from_sibling, self.names)]

    def scatter_side(self):
        return _scatter_side(self.chip_sums, self.small)

    def took_scatter(self, arrived):
        big = arrived[:len(self.names)]
        self.halves = [_sum_chips(g, fs, r, self.place, name="sum_chips_" + n)
                       for g, fs, r, n in zip(self.partials, self.from_sibling, big, self.names)]
        self.small_mine = _sum_small(self.small, arrived[-1], self.place) if self.small is not None else None

    def share_side(self):
        return _share_side(self.halves, self.small_mine)

    def took_share(self, shared):
        self.grads = dict(zip(self.names, shared[:len(self.names)]))
        self.small_full = shared[-1] if self.small is not None else None


def _step(x, loss_target, p, m, v):
    xi, yi, ci = lax.axis_index("x"), lax.axis_index("y"), lax.axis_index("c")
    chip = 2 * xi + yi
    device = 4 * xi + 2 * yi + ci
    core = jnp.reshape(ci, (1,)).astype(jnp.int32)
    place = jnp.stack([chip, ci, device]).astype(jnp.int32)
    x, tgt = x[0], loss_target[0]
    s = x.shape[0]
    cos, sin = _rope_tables(s)

    shard2d = {n: p[n].reshape(p[n].shape[-2:]) for n in BIG}
    shard_bf = {n: shard2d[n].astype(BF16) for n in BIG}
    ws = p["a_ws"][0]
    ws_t = jnp.swapaxes(ws, 1, 2)
    bs_t = p["a_bs"][0].T
    kv_norm_g, b_kv = p["kv_norm_g"].reshape(1, -1), p["b_kv"].reshape(1, -1)
    final_norm_g = p["final_norm_g"].reshape(1, -1)

    vec_shapes = [p[n].shape for n in ("a_norm_g", "a_ln_g", "a_ln_b")]
    vec_pack = _pack_rows([p["a_norm_g"], p["a_ln_g"], p["a_ln_b"]], 16)
    (vec_all,) = _comm_call(_gather_side([vec_pack]), "gather_vectors")
    vecs = [_unpack_rows(vec_all[k], vec_shapes) for k in range(N_CHIPS)]
    a_norm_g, a_ln_g, a_ln_b = (jnp.concatenate([vk[t] for vk in vecs], axis=-1) for t in range(3))

    (n_a,) = _rms_fwd(x, [a_norm_g], name="rms_a")
    order = jnp.stack([chip, 2 * (1 - xi) + yi, 2 * xi + (1 - yi), 2 * (1 - xi) + (1 - yi)]).astype(jnp.int32)
    z, a_w_in = _mm_gathering(n_a, shard_bf["a_w_in"], order, name="mm_a_in")
    y, (a_w_out,) = _gate_fwd(z, a_ln_g, a_ln_b, ws, bs_t, side=_gather_side([shard_bf["a_w_out"]]))
    a_w_out = a_w_out.reshape(A_WIDTH, D_MODEL)
    (h1, n_kv, n_b), (w_kv, b_w_in) = _mm_residual_norms(
        y, a_w_out, x, [kv_norm_g, p["b_norm_g"]], name="mm_a_out",
        side=_gather_side([shard_bf["w_kv"], shard_bf["b_w_in"]]))
    w_kv = w_kv.reshape(D_MODEL, 2 * KV_WIDTH)
    kv = _mm_nn(n_kv, w_kv, name="mm_kv", tn=2 * KV_WIDTH)
    kr, vv = _kv_rope(kv, b_kv, cos, sin)
    zb = _mm_nn(n_b, b_w_in, name="mm_b_in", tn=512, tm=1024, out_dtype=BF16)
    yb, (b_w_out,) = _attn_fwd(zb, kr, vv, cos, sin, p["b_bq"], p["b_sinks"], side=_gather_side([shard_bf["b_w_out"]]))
    b_w_out = b_w_out.reshape(B_WIDTH, D_MODEL)
    loss_blk, dh2, dh2b, d_final_g = _mm_residual_loss(yb, b_w_out, h1, tgt, final_norm_g, name="mm_b_out")

    d_b_w_out = _mm_tn(yb, dh2b, name="mm_d_b_w_out", tm=B_WIDTH, tn=D_MODEL)
    red_bo = _Reduction(["b_w_out"], [d_b_w_out.reshape(N_CHIPS, B_WIDTH // N_CHIPS, D_MODEL)], core, place)
    dyb, got = _mm_nt(dh2b, b_w_out, name="mm_dyb", out_dtype=BF16, side=red_bo.exchange_side())
    red_bo.took_exchange(got)
    dzb, dk_rot, dv, d_bq, d_sinks = _attn_bwd(zb, dyb, kr, vv, cos, sin, p["b_bq"], p["b_sinks"])
    dkv, d_b_kv = _kv_rope_bwd(dk_rot, dv, cos, sin)
    d_b_w_in, got = _mm_tn(n_b, dzb, name="mm_d_b_w_in", tm=D_MODEL, tn=512, shards=N_CHIPS,
                           side=red_bo.scatter_side())
    red_bo.took_scatter(got)
    d_w_kv, got = _mm_tn(n_kv, dkv, name="mm_d_w_kv", tm=D_MODEL, tn=2 * KV_WIDTH, side=red_bo.share_side())
    red_bo.took_share(got)
    red_bi = _Reduction(["b_w_in", "w_kv"], [d_b_w_in, d_w_kv.reshape(N_CHIPS, D_MODEL // N_CHIPS, 2 * KV_WIDTH)],
                        core, place)
    (dh1, dh1b, d_kv_g, d_b_g), got = _mm_nt_rms_bwd(
        [(dkv, w_kv, kv_norm_g), (dzb, b_w_in, p["b_norm_g"])], h1, dh2, name="mm_dn_b", tm=512,
        side=red_bi.exchange_side())
    red_bi.took_exchange(got)

    d_a_w_out, got = _mm_tn(y, dh1b, name="mm_d_a_w_out", tm=1024, tn=D_MODEL, side=red_bi.scatter_side())
    red_bi.took_scatter(got)
    red_ao = _Reduction(["a_w_out"], [d_a_w_out.reshape(N_CHIPS, A_WIDTH // N_CHIPS, D_MODEL)], core, place)
    sides = [red_ao.exchange_side(), red_bi.share_side()]
    dy, got = _mm_nt(dh1b, a_w_out, name="mm_dy", tn=1024, out_dtype=BF16, side=_join(sides))
    got = _split(got, sides)
    red_ao.took_exchange(got[0])
    red_bi.took_share(got[1])
    (dz, d_ln_g, d_ln_b, d_ws, d_bs_t), got = _gate_bwd(z, dy, a_ln_g, a_ln_b, ws, ws_t, bs_t,
                                                        side=red_ao.scatter_side())
    red_ao.took_scatter(got)
    small = {
        "a_ws": d_ws, "a_bs": d_bs_t.T, "a_ln_g": d_ln_g, "a_ln_b": d_ln_b,
        "kv_norm_g": d_kv_g, "b_kv": d_b_kv, "b_norm_g": d_b_g, "b_bq": d_bq,
        "b_sinks": d_sinks[0:1, :N_Q_HEADS], "final_norm_g": d_final_g,
    }
    packed = [n for n in SMALL_ORDER if n != "a_norm_g"]
    small_shapes = [small[n].shape for n in packed] + [(1, 1)]
    small_pack = _pack_rows([small[n] for n in packed] + [loss_blk[0:1, 0:1]], 64)
    seg = small_pack.shape[0] // 8
    small_pack = small_pack.reshape(8, seg, 128)
    sides = [red_ao.share_side(), _small_scatter_side(small_pack)]
    (d_a_w_in, from_sibling), got = _mm_tn_exchanging(n_a, dz, name="mm_d_a_w_in", shards=N_CHIPS, side=_join(sides))
    got = _split(got, sides)
    red_ao.took_share(got[0])
    small_mine = _sum_small(small_pack, got[1][0], place)

    red_ai = _Reduction(["a_w_in"], [d_a_w_in], core, place)
    red_ai.took_exchange([from_sibling])
    (dx, _, d_a_g), got = _mm_nt_rms_bwd([(dz, a_w_in, a_norm_g)], x, dh1, name="mm_dn_a", tm=256,
                                         side=red_ai.scatter_side())
    red_ai.took_scatter(got)
    red_ai.small, red_ai.small_mine = small_pack, small_mine
    d_a_g = _rows128(d_a_g)
    sides = [red_ai.share_side(), _spread_side(d_a_g)]
    got = _split(_comm_call(_join(sides), "share_last"), sides)
    red_ai.took_share(got[0])
    small_full = dict(zip(packed + ["loss"], _unpack_rows(red_ai.small_full.reshape(8 * seg, 128), small_shapes)))
    small_full["a_norm_g"] = _sum_in_device_order(d_a_g, got[1][0], place).reshape(1, -1)
    loss = small_full["loss"].reshape(())

    grad_big = {**red_bo.grads, **red_bi.grads, **red_ao.grads, **red_ai.grads}
    grads = {}
    for n in SMALL_ORDER:
        gfull = small_full[n]
        if n in SHARDED_SMALL:
            width = p[n].shape[-1]
            gfull = lax.dynamic_slice_in_dim(gfull, chip * width, width, axis=-1)
        grads[n] = gfull.reshape(p[n].shape)
    for n in BIG:
        grads[n] = grad_big[n].reshape(p[n].shape)

    delta, new_m, new_v = {}, {}, {}
    for n in BIG:
        d, nm, nv = _adamw(shard2d[n], grad_big[n], m[n].reshape(shard2d[n].shape), v[n].reshape(shard2d[n].shape),
                           name="adamw_" + n)
        delta[n], new_m[n], new_v[n] = d.reshape(p[n].shape), nm.reshape(p[n].shape), nv.reshape(p[n].shape)
    shapes = [p[n].shape for n in SMALL_ORDER]
    packs = [_pack_rows([src[n] for n in SMALL_ORDER], 8) for src in (p, grads, m, v)]
    outs = _adamw(*packs, name="adamw_small")
    for res, packed in zip((delta, new_m, new_v), outs):
        for n, val in zip(SMALL_ORDER, _unpack_rows(packed, shapes)):
            res[n] = val

    return (loss, dx[None], *[grads[n] for n in WEIGHTS], *[delta[n] for n in WEIGHTS],
            *[new_m[n] for n in WEIGHTS], *[new_v[n] for n in WEIGHTS])


def kernel(x, a_norm_g, a_w_in, a_ln_g, a_ln_b, a_ws, a_bs, a_w_out, kv_norm_g, w_kv, b_kv, b_norm_g, b_w_in, b_bq, b_sinks, b_w_out, final_norm_g, loss_target, m_a_norm_g, m_a_w_in, m_a_ln_g, m_a_ln_b, m_a_ws, m_a_bs, m_a_w_out, m_kv_norm_g, m_w_kv, m_b_kv, m_b_norm_g, m_b_w_in, m_b_bq, m_b_sinks, m_b_w_out, m_final_norm_g, v_a_norm_g, v_a_w_in, v_a_ln_g, v_a_ln_b, v_a_ws, v_a_bs, v_a_w_out, v_kv_norm_g, v_w_kv, v_b_kv, v_b_norm_g, v_b_w_in, v_b_bq, v_b_sinks, v_b_w_out, v_final_norm_g):
    p = dict(a_norm_g=a_norm_g, a_w_in=a_w_in, a_ln_g=a_ln_g, a_ln_b=a_ln_b, a_ws=a_ws, a_bs=a_bs, a_w_out=a_w_out,
             kv_norm_g=kv_norm_g, w_kv=w_kv, b_kv=b_kv, b_norm_g=b_norm_g, b_w_in=b_w_in, b_bq=b_bq, b_sinks=b_sinks,
             b_w_out=b_w_out, final_norm_g=final_norm_g)
    m = dict(a_norm_g=m_a_norm_g, a_w_in=m_a_w_in, a_ln_g=m_a_ln_g, a_ln_b=m_a_ln_b, a_ws=m_a_ws, a_bs=m_a_bs,
             a_w_out=m_a_w_out, kv_norm_g=m_kv_norm_g, w_kv=m_w_kv, b_kv=m_b_kv, b_norm_g=m_b_norm_g, b_w_in=m_b_w_in,
             b_bq=m_b_bq, b_sinks=m_b_sinks, b_w_out=m_b_w_out, final_norm_g=m_final_norm_g)
    v = dict(a_norm_g=v_a_norm_g, a_w_in=v_a_w_in, a_ln_g=v_a_ln_g, a_ln_b=v_a_ln_b, a_ws=v_a_ws, a_bs=v_a_bs,
             a_w_out=v_a_w_out, kv_norm_g=v_kv_norm_g, w_kv=v_w_kv, b_kv=v_b_kv, b_norm_g=v_b_norm_g, b_w_in=v_b_w_in,
             b_bq=v_b_bq, b_sinks=v_b_sinks, b_w_out=v_b_w_out, final_norm_g=v_final_norm_g)
    return _step(x, loss_target, p, m, v)
```

```python
import functools
import math

import jax
import jax.numpy as jnp
from jax import lax
from jax.experimental import pallas as pl
from jax.experimental.pallas import tpu as pltpu

F32 = jnp.float32
BF16 = jnp.bfloat16

D_MODEL = 1024
CHUNK = 128
A_WIDTH = 2048
A_GROUPS = 16
HEAD_DIM = 64
N_Q_HEADS = 16
N_KV_HEADS = 2
Q_PER_KV = 8
B_WIDTH = 1024
KV_WIDTH = 128
ROPE_THETA = 10000.0
EPS = 1e-5
N_CHIPS = 4

ADAM_LR = 0.001
ADAM_B1 = 0.9
ADAM_B2 = 0.999
ADAM_EPS = 1e-08
ADAM_WD = 0.01
ADAM_STEP = 10

VMEM_LIMIT = 48 * 1024 * 1024
MESH = pl.DeviceIdType.MESH
NEG_BIG = -1e30
HBM = pl.BlockSpec(memory_space=pl.ANY)

NN = (((1,), (0,)), ((), ()))
NT = (((1,), (1,)), ((), ()))
TN = (((0,), (0,)), ((), ()))


def _cparams(**kw):
    return pltpu.CompilerParams(vmem_limit_bytes=VMEM_LIMIT, **kw)


class _Side:
    def __init__(self, ins, out_shapes, sems, start, finish, aliases=None, passing=None):
        self.ins, self.out_shapes, self.sems = list(ins), list(out_shapes), list(sems)
        self.start, self.finish = start, finish
        self.passing = passing or (lambda ins, outs, sems: None)
        self.aliases = dict(aliases or {})


def _join(sides):
    sides = [s for s in sides if s is not None]
    if not sides:
        return None
    offs, i, o, m = [], 0, 0, 0
    for s in sides:
        offs.append((i, o, m))
        i, o, m = i + len(s.ins), o + len(s.out_shapes), m + len(s.sems)

    def run(which):
        def go(ins, outs, sems):
            for s, (a, b, c) in zip(sides, offs):
                getattr(s, which)(ins[a:a + len(s.ins)], outs[b:b + len(s.out_shapes)], sems[c:c + len(s.sems)])
        return go

    aliases = {}
    for s, (a, b, _) in zip(sides, offs):
        aliases.update({a + k: b + v for k, v in s.aliases.items()})
    return _Side([x for s in sides for x in s.ins], [x for s in sides for x in s.out_shapes],
                 [x for s in sides for x in s.sems], run("start"), run("finish"), aliases, run("passing"))


def _split(side_outs, sides):
    out, pos = [], 0
    for s in sides:
        out.append(list(side_outs[pos:pos + len(s.out_shapes)]))
        pos += len(s.out_shapes)
    return out


def _call(body, *, grid, in_specs, out_specs, out_shape, args, name, scratch=(), side=None):
    in_specs, out_specs, out_shape, scratch = list(in_specs), list(out_specs), list(out_shape), list(scratch)
    if side is None:
        res = pl.pallas_call(body, grid=grid, in_specs=in_specs, out_specs=out_specs, out_shape=out_shape,
                             scratch_shapes=scratch, name=name, compiler_params=_cparams())(*args)
        return list(res), []
    n_in, n_out, n_sc = len(in_specs), len(out_specs), len(scratch)
    s_in, s_out = len(side.ins), len(side.out_shapes)

    def wrapped(*refs):
        ins, refs = refs[:n_in], refs[n_in:]
        side_ins, refs = refs[:s_in], refs[s_in:]
        outs, refs = refs[:n_out], refs[n_out:]
        side_outs, refs = refs[:s_out], refs[s_out:]
        scr, side_sems = refs[:n_sc], refs[n_sc:]
        step = 0
        for a, g in enumerate(grid):
            step = step * g + pl.program_id(a)
        steps = math.prod(grid)

        @pl.when(step == 0)
        def _():
            side.start(side_ins, side_outs, side_sems)

        body(*ins, *outs, *scr)

        @pl.when(step == (3 * (steps - 1)) // 4)
        def _():
            side.passing(side_ins, side_outs, side_sems)

        @pl.when(step == steps - 1)
        def _():
            side.finish(side_ins, side_outs, side_sems)

    res = pl.pallas_call(
        wrapped, grid=grid, in_specs=in_specs + [HBM] * s_in, out_specs=out_specs + [HBM] * s_out,
        out_shape=out_shape + side.out_shapes, scratch_shapes=scratch + side.sems,
        input_output_aliases={n_in + k: n_out + v for k, v in side.aliases.items()},
        name=name, compiler_params=_cparams(),
    )(*args, *side.ins)
    return list(res[:n_out]), list(res[n_out:])


def _comm_call(side, name):
    s_in, s_out = len(side.ins), len(side.out_shapes)

    def body(*refs):
        ins, outs, sems = refs[:s_in], refs[s_in:s_in + s_out], refs[s_in + s_out:]
        side.start(ins, outs, sems)
        side.passing(ins, outs, sems)
        side.finish(ins, outs, sems)

    return list(pl.pallas_call(
        body, in_specs=[HBM] * s_in, out_specs=[HBM] * s_out, out_shape=side.out_shapes, scratch_shapes=side.sems,
        input_output_aliases=side.aliases, name=name,
    )(*side.ins))


def _matmul(a, b, *, dims, grid, a_spec, b_spec, o_spec, out_shape, name, acc_axis=None,
            residual=None, r_spec=None, side=None):
    has_res = residual is not None

    def body(*refs):
        if has_res:
            a_ref, b_ref, r_ref, o_ref = refs
        else:
            a_ref, b_ref, o_ref = refs
        part = lax.dot_general(a_ref[...], b_ref[...], dims, preferred_element_type=F32)
        if acc_axis is None:
            if has_res:
                part = part + r_ref[...]
            o_ref[...] = part.astype(o_ref.dtype)
        else:
            k = pl.program_id(acc_axis)

            @pl.when(k == 0)
            def _():
                o_ref[...] = part

            @pl.when(k > 0)
            def _():
                o_ref[...] += part

    in_specs = [a_spec, b_spec] + ([r_spec] if has_res else [])
    args = (a, b) + ((residual,) if has_res else ())
    (out,), side_outs = _call(body, grid=grid, in_specs=in_specs, out_specs=[o_spec], out_shape=[out_shape],
                              args=args, name=name, side=side)
    return (out, side_outs) if side is not None else out


def _row_tile(s, want):
    return min(s, want)


def _mm_nn(a, b, *, name, tn, out_dtype=F32, residual=None, tm=512, side=None):
    s, k = a.shape
    tm = _row_tile(s, tm)
    if b.ndim == 3:
        nsh, _, nc = b.shape
        npb = nc // tn
        n = nsh * nc
        b_spec = pl.BlockSpec((None, k, tn), lambda i, j: (j // npb, 0, j % npb))
    else:
        n = b.shape[1]
        b_spec = pl.BlockSpec((k, tn), lambda i, j: (0, j))
    return _matmul(
        a, b, dims=NN, grid=(s // tm, n // tn),
        a_spec=pl.BlockSpec((tm, k), lambda i, j: (i, 0)), b_spec=b_spec,
        o_spec=pl.BlockSpec((tm, tn), lambda i, j: (i, j)),
        out_shape=jax.ShapeDtypeStruct((s, n), out_dtype), name=name, side=side,
        residual=residual, r_spec=pl.BlockSpec((tm, tn), lambda i, j: (i, j)) if residual is not None else None)


def _mm_nt(a, b, *, name, tn=None, tm=512, out_dtype=F32, side=None):
    s, k = a.shape
    tm = _row_tile(s, tm)
    n = b.shape[0]
    tn = n if tn is None else tn
    return _matmul(
        a, b, dims=NT, grid=(s // tm, n // tn),
        a_spec=pl.BlockSpec((tm, k), lambda i, j: (i, 0)),
        b_spec=pl.BlockSpec((tn, k), lambda i, j: (j, 0)),
        o_spec=pl.BlockSpec((tm, tn), lambda i, j: (i, j)),
        out_shape=jax.ShapeDtypeStruct((s, n), out_dtype), name=name, side=side)


def _mm_tn(a, b, *, name, tm, tn, tk=2048, shards=None, side=None):
    s, m = a.shape
    n = b.shape[1]
    tk = _row_tile(s, tk)
    if shards is None:
        o_spec = pl.BlockSpec((tm, tn), lambda i, j, kk: (i, j))
        out_shape = jax.ShapeDtypeStruct((m, n), F32)
    else:
        assert tm == m
        nc = n // shards
        npb = nc // tn
        o_spec = pl.BlockSpec((None, m, tn), lambda i, j, kk: (j // npb, 0, j % npb))
        out_shape = jax.ShapeDtypeStruct((shards, m, nc), F32)
    return _matmul(
        a, b, dims=TN, grid=(m // tm, n // tn, s // tk), acc_axis=2,
        a_spec=pl.BlockSpec((tk, tm), lambda i, j, kk: (kk, i)),
        b_spec=pl.BlockSpec((tk, tn), lambda i, j, kk: (kk, j)),
        o_spec=o_spec, out_shape=out_shape, name=name, side=side)


def _rstd(x):
    return lax.rsqrt(jnp.mean(x * x, axis=-1, keepdims=True) + EPS)


def _rms_fwd(x, gains, *, name, tr=256):
    s, d = x.shape
    tr = _row_tile(s, tr)
    ng = len(gains)

    def body(*refs):
        xv = refs[0][...]
        xh = xv * _rstd(xv)
        for t in range(ng):
            refs[1 + ng + t][...] = (xh * refs[1 + t][...]).astype(BF16)

    row = pl.BlockSpec((tr, d), lambda i: (i, 0))
    vec = pl.BlockSpec((1, d), lambda i: (0, 0))
    outs, _ = _call(body, grid=(s // tr,), in_specs=[row] + [vec] * ng, out_specs=[row] * ng,
                    out_shape=[jax.ShapeDtypeStruct((s, d), BF16)] * ng, args=(x, *gains), name=name)
    return outs


def _accumulate(i, ref, value):
    @pl.when(i == 0)
    def _():
        ref[...] = value

    @pl.when(i > 0)
    def _():
        ref[...] += value


def _mm_residual_norms(y, w, res, gains, *, name, tm=512, side=None):
    s, k = y.shape
    d = w.shape[1]
    tm = _row_tile(s, tm)
    ng = len(gains)

    def body(y_ref, w_ref, r_ref, *rest):
        g_refs, h_ref, n_refs = rest[:ng], rest[ng], rest[ng + 1:]
        h = r_ref[...] + jnp.dot(y_ref[...], w_ref[...], preferred_element_type=F32)
        h_ref[...] = h
        xh = h * _rstd(h)
        for t in range(ng):
            n_refs[t][...] = (xh * g_refs[t][...]).astype(BF16)

    row = pl.BlockSpec((tm, d), lambda i: (i, 0))
    vec = pl.BlockSpec((1, d), lambda i: (0, 0))
    return _call(
        body, grid=(s // tm,),
        in_specs=[pl.BlockSpec((tm, k), lambda i: (i, 0)), pl.BlockSpec((k, d), lambda i: (0, 0)), row] + [vec] * ng,
        out_specs=[row] * (1 + ng),
        out_shape=[jax.ShapeDtypeStruct((s, d), F32)] + [jax.ShapeDtypeStruct((s, d), BF16)] * ng,
        args=(y, w, res, *gains), name=name, side=side)


def _mm_residual_loss(y, w, res, tgt, gain, *, name, tm=512):
    s, k = y.shape
    d = w.shape[1]
    tm = _row_tile(s, tm)

    def body(y_ref, w_ref, r_ref, t_ref, g_ref, loss_ref, dh_ref, dhb_ref, dg_ref):
        i = pl.program_id(0)
        hv = r_ref[...] + jnp.dot(y_ref[...], w_ref[...], preferred_element_type=F32)
        g = g_ref[...]
        r = _rstd(hv)
        xh = hv * r
        diff = xh * g - t_ref[...]
        part = 0.5 / d * jnp.sum(jnp.sum(diff * diff, axis=-1, keepdims=True), axis=0, keepdims=True)
        dout = diff * (1.0 / d)
        a = dout * g
        dh = r * (a - xh * jnp.mean(a * xh, axis=-1, keepdims=True))
        dh_ref[...] = dh
        dhb_ref[...] = dh.astype(BF16)
        _accumulate(i, dg_ref, jnp.sum(dout * xh, axis=0, keepdims=True))
        _accumulate(i, loss_ref, jnp.broadcast_to(part, (8, 128)))

    row = pl.BlockSpec((tm, d), lambda i: (i, 0))
    vec = pl.BlockSpec((1, d), lambda i: (0, 0))
    outs, _ = _call(
        body, grid=(s // tm,),
        in_specs=[pl.BlockSpec((tm, k), lambda i: (i, 0)), pl.BlockSpec((k, d), lambda i: (0, 0)), row, row, vec],
        out_specs=[pl.BlockSpec((8, 128), lambda i: (0, 0)), row, row, vec],
        out_shape=[jax.ShapeDtypeStruct((8, 128), F32), jax.ShapeDtypeStruct((s, d), F32),
                   jax.ShapeDtypeStruct((s, d), BF16), jax.ShapeDtypeStruct((1, d), F32)],
        args=(y, w, res, tgt, gain), name=name)
    return outs


def _mm_nt_rms_bwd(terms, x, dres, *, name, tm, side=None):
    s, d = x.shape
    tm = _row_tile(s, tm)
    nt = len(terms)

    def body(*refs):
        a_refs, b_refs, g_refs = refs[0:3 * nt:3], refs[1:3 * nt:3], refs[2:3 * nt:3]
        x_ref, dres_ref = refs[3 * nt], refs[3 * nt + 1]
        dx_ref, dxb_ref = refs[3 * nt + 2], refs[3 * nt + 3]
        dg_refs = refs[3 * nt + 4:]
        i = pl.program_id(0)
        xv = x_ref[...]
        r = _rstd(xv)
        xh = xv * r
        acc = jnp.zeros_like(xv)
        for t in range(nt):
            b_ref = b_refs[t]
            if len(b_ref.shape) == 3:
                kc = b_ref.shape[2]
                dn = None
                for sh in range(b_ref.shape[0]):
                    part = lax.dot_general(a_refs[t][:, sh * kc:(sh + 1) * kc], b_ref[sh], NT, preferred_element_type=F32)
                    dn = part if dn is None else dn + part
            else:
                dn = lax.dot_general(a_refs[t][...], b_ref[...], NT, preferred_element_type=F32)
            acc = acc + dn * g_refs[t][...]
            _accumulate(i, dg_refs[t], jnp.sum(dn * xh, axis=0, keepdims=True))
        dx = dres_ref[...] + r * (acc - xh * jnp.mean(acc * xh, axis=-1, keepdims=True))
        dx_ref[...] = dx
        dxb_ref[...] = dx.astype(BF16)

    row = pl.BlockSpec((tm, d), lambda i: (i, 0))
    vec = pl.BlockSpec((1, d), lambda i: (0, 0))
    in_specs, args = [], []
    for a, b, g in terms:
        in_specs += [pl.BlockSpec((tm, a.shape[1]), lambda i: (i, 0)),
                     pl.BlockSpec(b.shape, (lambda i: (0, 0, 0)) if b.ndim == 3 else (lambda i: (0, 0))), vec]
        args += [a, b, g]
    return _call(
        body, grid=(s // tm,), in_specs=in_specs + [row, row], out_specs=[row, row] + [vec] * nt,
        out_shape=[jax.ShapeDtypeStruct((s, d), F32), jax.ShapeDtypeStruct((s, d), BF16)]
        + [jax.ShapeDtypeStruct((1, d), F32)] * nt,
        args=(*args, x, dres), name=name, side=side)


def _causal_mask(transposed=False):
    row = lax.broadcasted_iota(jnp.int32, (CHUNK, CHUNK), 0)
    col = lax.broadcasted_iota(jnp.int32, (CHUNK, CHUNK), 1)
    return col >= row if transposed else col <= row


def _silu_parts(g):
    sg = jax.nn.sigmoid(g)
    return g * sg, sg * (1.0 + g * (1.0 - sg))


def _gate_fwd(z, ln_g, ln_b, ws, bs_t, *, tr=256, side=None):
    s = z.shape[0]
    tr = _row_tile(s, tr)
    w = A_WIDTH

    def body(u_ref, v_ref, g_ref, lg_ref, lb_ref, ws_ref, bst_ref, y_ref):
        v = v_ref[...].astype(F32)
        mu = jnp.mean(v, axis=-1, keepdims=True)
        xc = v - mu
        rs = lax.rsqrt(jnp.mean(xc * xc, axis=-1, keepdims=True) + EPS)
        vln = (xc * rs * lg_ref[...] + lb_ref[...]).astype(BF16)
        mask = _causal_mask()
        for grp in range(A_GROUPS):
            cols = slice(grp * CHUNK, (grp + 1) * CHUNK)
            wsm = jnp.where(mask, ws_ref[grp], 0.0).astype(BF16)
            bcol = bst_ref[:, grp:grp + 1]
            for ci in range(tr // CHUNK):
                rows = slice(ci * CHUNK, (ci + 1) * CHUNK)
                sv = jnp.dot(wsm, vln[rows, cols], preferred_element_type=F32) + bcol
                gv = g_ref[rows, cols].astype(F32)
                y_ref[rows, cols] = (u_ref[rows, cols].astype(F32) * sv * (gv * jax.nn.sigmoid(gv))).astype(BF16)

    vec = pl.BlockSpec((1, w), lambda i: (0, 0))
    (y,), side_outs = _call(
        body, grid=(s // tr,),
        in_specs=[pl.BlockSpec((tr, w), lambda i: (i, 0)), pl.BlockSpec((tr, w), lambda i: (i, 1)),
                  pl.BlockSpec((tr, w), lambda i: (i, 2)), vec, vec,
                  pl.BlockSpec((A_GROUPS, CHUNK, CHUNK), lambda i: (0, 0, 0)),
                  pl.BlockSpec((CHUNK, A_GROUPS), lambda i: (0, 0))],
        out_specs=[pl.BlockSpec((tr, w), lambda i: (i, 0))],
        out_shape=[jax.ShapeDtypeStruct((s, w), BF16)], args=(z, z, z, ln_g, ln_b, ws, bs_t), name="gate_fwd",
        side=side)
    return y, side_outs


def _gate_bwd(z, dy, ln_g, ln_b, ws, ws_t, bs_t, *, tr=256, side=None):
    s = z.shape[0]
    tr = _row_tile(s, tr)
    w = A_WIDTH
    nsteps = s // tr

    def body(u_ref, v_ref, g_ref, dy_ref, lg_ref, lb_ref, ws_ref, wst_ref, bst_ref,
             dz_ref, dlg_ref, dlb_ref, dws_ref, dbst_ref, dvln_sc, dsv_sc):
        i = pl.program_id(0)

        @pl.when(i == 0)
        def _():
            dws_ref[...] = jnp.zeros_like(dws_ref)
            dsv_sc[...] = jnp.zeros_like(dsv_sc)

        v = v_ref[...].astype(F32)
        mu = jnp.mean(v, axis=-1, keepdims=True)
        xc = v - mu
        rs = lax.rsqrt(jnp.mean(xc * xc, axis=-1, keepdims=True) + EPS)
        xh = xc * rs
        lg = lg_ref[...]
        vln = (xh * lg + lb_ref[...]).astype(BF16)
        mask = _causal_mask()
        mask_t = _causal_mask(transposed=True)
        for grp in range(A_GROUPS):
            cols = slice(grp * CHUNK, (grp + 1) * CHUNK)
            wsm = jnp.where(mask, ws_ref[grp], 0.0).astype(BF16)
            wsm_t = jnp.where(mask_t, wst_ref[grp], 0.0).astype(BF16)
            bcol = bst_ref[:, grp:grp + 1]
            for ci in range(tr // CHUNK):
                rows = slice(ci * CHUNK, (ci + 1) * CHUNK)
                vb = vln[rows, cols]
                sv = jnp.dot(wsm, vb, preferred_element_type=F32) + bcol
                uv = u_ref[rows, cols].astype(F32)
                silu, dsilu = _silu_parts(g_ref[rows, cols].astype(F32))
                dyv = dy_ref[rows, cols].astype(F32)
                dyu = dyv * uv
                dz_ref[rows, cols] = (dyv * sv * silu).astype(BF16)
                dz_ref[rows, 2 * w + grp * CHUNK:2 * w + (grp + 1) * CHUNK] = (dyu * sv * dsilu).astype(BF16)
                dsv = dyu * silu
                dsvb = dsv.astype(BF16)
                dvln_sc[rows, cols] = jnp.dot(wsm_t, dsvb, preferred_element_type=F32)
                dws_ref[grp] += lax.dot_general(dsvb, vb, NT, preferred_element_type=F32)
                dsv_sc[grp] += dsv
        dvln = dvln_sc[...]
        dlg_t = jnp.sum(dvln * xh, axis=0, keepdims=True)
        dlb_t = jnp.sum(dvln, axis=0, keepdims=True)
        a = dvln * lg
        dv = rs * (a - jnp.mean(a, axis=-1, keepdims=True) - xh * jnp.mean(a * xh, axis=-1, keepdims=True))
        dz_ref[:, w:2 * w] = dv.astype(BF16)

        @pl.when(i == 0)
        def _():
            dlg_ref[...] = dlg_t
            dlb_ref[...] = dlb_t

        @pl.when(i > 0)
        def _():
            dlg_ref[...] += dlg_t
            dlb_ref[...] += dlb_t

        @pl.when(i == nsteps - 1)
        def _():
            for grp in range(A_GROUPS):
                dws_ref[grp] = jnp.where(mask, dws_ref[grp], 0.0)
                dbst_ref[:, grp:grp + 1] = jnp.sum(dsv_sc[grp], axis=-1, keepdims=True)

    vec = pl.BlockSpec((1, w), lambda i: (0, 0))
    wsspec = pl.BlockSpec((A_GROUPS, CHUNK, CHUNK), lambda i: (0, 0, 0))
    bsspec = pl.BlockSpec((CHUNK, A_GROUPS), lambda i: (0, 0))
    return _call(
        body, grid=(nsteps,),
        in_specs=[pl.BlockSpec((tr, w), lambda i: (i, 0)), pl.BlockSpec((tr, w), lambda i: (i, 1)),
                  pl.BlockSpec((tr, w), lambda i: (i, 2)), pl.BlockSpec((tr, w), lambda i: (i, 0)),
                  vec, vec, wsspec, wsspec, bsspec],
        out_specs=[pl.BlockSpec((tr, 3 * w), lambda i: (i, 0)), vec, vec, wsspec, bsspec],
        out_shape=[jax.ShapeDtypeStruct((s, 3 * w), BF16), jax.ShapeDtypeStruct((1, w), F32),
                   jax.ShapeDtypeStruct((1, w), F32), jax.ShapeDtypeStruct((A_GROUPS, CHUNK, CHUNK), F32),
                   jax.ShapeDtypeStruct((CHUNK, A_GROUPS), F32)],
        scratch=[pltpu.VMEM((tr, w), F32), pltpu.VMEM((A_GROUPS, CHUNK, CHUNK), F32)],
        args=(z, z, z, dy, ln_g, ln_b, ws, ws_t, bs_t), name="gate_bwd", side=side)


HEADS_PER_BLOCK = 128 // HEAD_DIM
BLOCKS_PER_KV = Q_PER_KV // HEADS_PER_BLOCK
SCALE = HEAD_DIM ** -0.5
LOG2_E = math.log2(math.e)


def _rope_tables(s):
    lane = jnp.arange(128)
    inv_freq = ROPE_THETA ** (-(2 * (lane % (HEAD_DIM // 2))).astype(F32) / HEAD_DIM)
    sign = jnp.where(lane % HEAD_DIM < HEAD_DIM // 2, -1.0, 1.0).astype(F32)
    ang = jnp.arange(s, dtype=F32)[:, None] * inv_freq[None, :]
    return jnp.cos(ang), jnp.sin(ang) * sign[None, :]


def _swap_halves(x):
    n = x.shape[-1]
    lane = lax.broadcasted_iota(jnp.int32, x.shape, x.ndim - 1)
    first = (lane % HEAD_DIM) < (HEAD_DIM // 2)
    return jnp.where(first, pltpu.roll(x, n - HEAD_DIM // 2, x.ndim - 1), pltpu.roll(x, HEAD_DIM // 2, x.ndim - 1))


def _left_half(rows):
    return lax.broadcasted_iota(jnp.int32, (rows, 128), 1) < HEAD_DIM


def _dup_heads(x):
    left = _left_half(x.shape[0])
    swapped = pltpu.roll(x, HEAD_DIM, 1)
    return jnp.concatenate([jnp.where(left, x, swapped), jnp.where(left, swapped, x)], axis=-1)


def _fold_heads(a):
    b0, b1 = a[:, :128], a[:, 128:]
    f0 = b0 + pltpu.roll(b0, HEAD_DIM, 1)
    f1 = b1 + pltpu.roll(b1, HEAD_DIM, 1)
    return jnp.where(_left_half(a.shape[0]), f0, f1)


def _kv_rope(kv, b_kv, cos, sin, *, tr=512):
    s = kv.shape[0]
    tr = _row_tile(s, tr)

    def body(kv_ref, b_ref, c_ref, s_ref, k_ref, v_ref):
        x = kv_ref[...] + b_ref[...]
        k = x[:, :KV_WIDTH]
        k_ref[...] = _dup_heads(k * c_ref[...] + _swap_halves(k) * s_ref[...]).astype(BF16)
        v_ref[...] = _dup_heads(x[:, KV_WIDTH:]).astype(BF16)

    tab = pl.BlockSpec((tr, KV_WIDTH), lambda i: (i, 0))
    wide = pl.BlockSpec((tr, 2 * KV_WIDTH), lambda i: (i, 0))
    outs, _ = _call(body, grid=(s // tr,),
                    in_specs=[wide, pl.BlockSpec((1, 2 * KV_WIDTH), lambda i: (0, 0)), tab, tab],
                    out_specs=[wide, wide], out_shape=[jax.ShapeDtypeStruct((s, 2 * KV_WIDTH), BF16)] * 2,
                    args=(kv, b_kv, cos, sin), name="kv_rope")
    return outs


def _kv_rope_bwd(dk2, dv2, cos, sin, *, tr=512):
    s = dk2.shape[0]
    tr = _row_tile(s, tr)

    def body(dk_ref, dv_ref, c_ref, s_ref, dkv_ref, db_ref):
        i = pl.program_id(0)
        d = _fold_heads(dk_ref[...])
        dk = d * c_ref[...] + _swap_halves(d * s_ref[...])
        dvv = _fold_heads(dv_ref[...])
        dkv_ref[:, :KV_WIDTH] = dk.astype(BF16)
        dkv_ref[:, KV_WIDTH:] = dvv.astype(BF16)
        sk = jnp.sum(dk, axis=0, keepdims=True)
        sv = jnp.sum(dvv, axis=0, keepdims=True)

        @pl.when(i == 0)
        def _():
            db_ref[:, :KV_WIDTH] = sk
            db_ref[:, KV_WIDTH:] = sv

        @pl.when(i > 0)
        def _():
            db_ref[:, :KV_WIDTH] += sk
            db_ref[:, KV_WIDTH:] += sv

    tab = pl.BlockSpec((tr, KV_WIDTH), lambda i: (i, 0))
    wide = pl.BlockSpec((tr, 2 * KV_WIDTH), lambda i: (i, 0))
    outs, _ = _call(body, grid=(s // tr,), in_specs=[wide, wide, tab, tab],
                    out_specs=[wide, pl.BlockSpec((1, 2 * KV_WIDTH), lambda i: (0, 0))],
                    out_shape=[jax.ShapeDtypeStruct((s, 2 * KV_WIDTH), BF16),
                               jax.ShapeDtypeStruct((1, 2 * KV_WIDTH), F32)],
                    args=(dk2, dv2, cos, sin), name="kv_rope_bwd")
    return outs


def _from_previous():
    cols = Q_PER_KV * CHUNK
    k = lax.broadcasted_iota(jnp.int32, (CHUNK, cols), 0)
    q = lax.broadcasted_iota(jnp.int32, (CHUNK, cols), 1) & (CHUNK - 1)
    return k > q


def _fold(x2, prev):
    return jnp.where(prev, x2[:CHUNK], x2[CHUNK:])


def _unfold(x, prev):
    zero = jnp.zeros_like(x)
    return jnp.concatenate([jnp.where(prev, x, zero), jnp.where(prev, zero, x)], axis=0)


def _stack_heads(blocks, left):
    parts = []
    for b in blocks:
        parts.append(jnp.where(left, b, jnp.zeros_like(b)))
        parts.append(jnp.where(left, jnp.zeros_like(b), b))
    return jnp.concatenate(parts, axis=0)


def _unstack_heads(xt):
    top = lax.broadcasted_iota(jnp.int32, (128, CHUNK), 0) < HEAD_DIM
    return [jnp.where(top, xt[:, (2 * b) * CHUNK:(2 * b + 1) * CHUNK], xt[:, (2 * b + 1) * CHUNK:(2 * b + 2) * CHUNK]).T
            for b in range(BLOCKS_PER_KV)]


def _sink_row(sk_ref, kvh):
    return jnp.concatenate([jnp.full((1, CHUNK), sk_ref[0, kvh * Q_PER_KV + r], F32) for r in range(Q_PER_KV)], axis=1)


def _stacked_probs(qs, kd, prev, sink, i):
    sc2 = lax.dot_general(kd, qs, NT, preferred_element_type=F32)
    no_previous = jnp.where(i > 0, 0.0, NEG_BIG)
    sc = jnp.where(prev, sc2[:CHUNK] + no_previous, sc2[CHUNK:])
    sink = sink * (1.0 / SCALE)
    m = jnp.maximum(jnp.max(sc, axis=0, keepdims=True), sink)
    p = jnp.exp2((sc - m) * (SCALE * LOG2_E))
    esink = jnp.exp2((sink - m) * (SCALE * LOG2_E))
    inv = 1.0 / (jnp.sum(p, axis=0, keepdims=True) + esink)
    return p * inv, esink * inv


def _lane_block(b):
    return slice(b * 128, (b + 1) * 128)


def _rope_blocks(zq_ref, bq_ref, cos, sin, kvh):
    out = []
    for b in range(BLOCKS_PER_KV):
        cols = _lane_block(kvh * BLOCKS_PER_KV + b)
        q = zq_ref[:, cols].astype(F32) + bq_ref[:, cols]
        out.append((q * cos + _swap_halves(q) * sin).astype(BF16))
    return out


def _attn_specs():
    qspec = pl.BlockSpec((CHUNK, B_WIDTH), lambda i: (i, 0))
    gspec = pl.BlockSpec((CHUNK, B_WIDTH), lambda i: (i, 1))
    prev = pl.BlockSpec((CHUNK, 2 * KV_WIDTH), lambda i: (jnp.maximum(i - 1, 0), 0))
    cur = pl.BlockSpec((CHUNK, 2 * KV_WIDTH), lambda i: (i, 0))
    tab = pl.BlockSpec((CHUNK, KV_WIDTH), lambda i: (i, 0))
    bq = pl.BlockSpec((1, B_WIDTH), lambda i: (0, 0))
    sinks = pl.BlockSpec(memory_space=pltpu.SMEM)
    return qspec, gspec, prev, cur, tab, bq, sinks


def _attn_fwd(zb, k2, v2, cos, sin, b_bq, sinks, *, side=None):
    s = zb.shape[0]

    def body(zq_ref, zg_ref, kp_ref, kc_ref, vp_ref, vc_ref, c_ref, s_ref, bq_ref, sk_ref, y_ref):
        i = pl.program_id(0)
        cos, sin = c_ref[...], s_ref[...]
        kcat = jnp.concatenate([kp_ref[...], kc_ref[...]], axis=0)
        vcat = jnp.concatenate([vp_ref[...], vc_ref[...]], axis=0)
        prev = _from_previous()
        left = _left_half(CHUNK)
        for kvh in range(N_KV_HEADS):
            qs = _stack_heads(_rope_blocks(zq_ref, bq_ref, cos, sin, kvh), left)
            p, _ = _stacked_probs(qs, kcat[:, _lane_block(kvh)], prev, _sink_row(sk_ref, kvh), i)
            ot = lax.dot_general(vcat[:, _lane_block(kvh)], _unfold(p, prev).astype(BF16), TN,
                                 preferred_element_type=F32)
            for b, ob in enumerate(_unstack_heads(ot)):
                cols = _lane_block(kvh * BLOCKS_PER_KV + b)
                gv = zg_ref[:, cols].astype(F32)
                y_ref[:, cols] = (ob * (gv * jax.nn.sigmoid(gv))).astype(BF16)

    qspec, gspec, prev, cur, tab, bq, sk = _attn_specs()
    (y,), side_outs = _call(body, grid=(s // CHUNK,), in_specs=[qspec, gspec, prev, cur, prev, cur, tab, tab, bq, sk],
                            out_specs=[qspec], out_shape=[jax.ShapeDtypeStruct((s, B_WIDTH), BF16)],
                            args=(zb, zb, k2, k2, v2, v2, cos, sin, b_bq, sinks), name="attn_fwd", side=side)
    return y, side_outs


def _attn_bwd(zb, dyb, k2, v2, cos, sin, b_bq, sinks):
    s = zb.shape[0]

    def body(zq_ref, zg_ref, dy_ref, kp_ref, kc_ref, vp_ref, vc_ref, c_ref, s_ref, bq_ref, sk_ref,
             dz_ref, dk_ref, dv_ref, dbq_ref, dsk_ref):
        i = pl.program_id(0)

        @pl.when(i == 0)
        def _():
            dk_ref[...] = jnp.zeros_like(dk_ref)
            dv_ref[...] = jnp.zeros_like(dv_ref)
            dbq_ref[...] = jnp.zeros_like(dbq_ref)
            dsk_ref[...] = jnp.zeros_like(dsk_ref)

        cos, sin = c_ref[...], s_ref[...]
        kcat = jnp.concatenate([kp_ref[...], kc_ref[...]], axis=0)
        vcat = jnp.concatenate([vp_ref[...], vc_ref[...]], axis=0)
        prev = _from_previous()
        left = _left_half(CHUNK)
        lane = lax.broadcasted_iota(jnp.int32, (1, 128), 1)
        dsk_row = jnp.zeros((1, 128), F32)
        cur_rows = pl.ds(pl.multiple_of(i * CHUNK, CHUNK), CHUNK)
        for kvh in range(N_KV_HEADS):
            kd, vd = kcat[:, _lane_block(kvh)], vcat[:, _lane_block(kvh)]
            qs = _stack_heads(_rope_blocks(zq_ref, bq_ref, cos, sin, kvh), left)
            p, psink = _stacked_probs(qs, kd, prev, _sink_row(sk_ref, kvh), i)
            pb = _unfold(p, prev).astype(BF16)
            ot = lax.dot_general(vd, pb, TN, preferred_element_type=F32)
            gates, dys = [], []
            for b in range(BLOCKS_PER_KV):
                cols = _lane_block(kvh * BLOCKS_PER_KV + b)
                gates.append(_silu_parts(zg_ref[:, cols].astype(F32)))
                dys.append(dy_ref[:, cols].astype(F32))
            dos = _stack_heads([(dyv * silu).astype(BF16) for dyv, (silu, _) in zip(dys, gates)], left)
            dp = _fold(lax.dot_general(vd, dos, NT, preferred_element_type=F32), prev)
            delta = jnp.sum(p * dp, axis=0, keepdims=True)
            ds = _unfold(p * (dp - delta) * SCALE, prev).astype(BF16)
            dqt = lax.dot_general(kd, ds, TN, preferred_element_type=F32)
            dk_part = jnp.dot(ds, qs, preferred_element_type=F32)
            dv_part = jnp.dot(pb, dos, preferred_element_type=F32)
            dk_ref[cur_rows, _lane_block(kvh)] += dk_part[CHUNK:]
            dv_ref[cur_rows, _lane_block(kvh)] += dv_part[CHUNK:]

            @pl.when(i > 0)
            def _(kvh=kvh, dk_part=dk_part, dv_part=dv_part):
                prev_rows = pl.ds(pl.multiple_of((i - 1) * CHUNK, CHUNK), CHUNK)
                dk_ref[prev_rows, _lane_block(kvh)] += dk_part[:CHUNK]
                dv_ref[prev_rows, _lane_block(kvh)] += dv_part[:CHUNK]

            sink_grad = psink * delta
            for r in range(Q_PER_KV):
                dsink = -jnp.sum(sink_grad[:, r * CHUNK:(r + 1) * CHUNK], axis=1, keepdims=True)
                dsk_row = dsk_row + jnp.where(lane == kvh * Q_PER_KV + r, dsink, 0.0)
            blocks = zip(_unstack_heads(ot), _unstack_heads(dqt), dys, gates)
            for b, (ob, dqr, dyv, (_, dsilu)) in enumerate(blocks):
                blk = kvh * BLOCKS_PER_KV + b
                dq = dqr * cos + _swap_halves(dqr * sin)
                dbq_ref[:, _lane_block(blk)] += jnp.sum(dq, axis=0, keepdims=True)
                dz_ref[:, _lane_block(blk)] = dq.astype(BF16)
                dz_ref[:, _lane_block(B_WIDTH // 128 + blk)] = (dyv * ob * dsilu).astype(BF16)
        dsk_ref[0:1, :] += dsk_row

    qspec, gspec, prev, cur, tab, bq, sk = _attn_specs()
    full = pl.BlockSpec((s, 2 * KV_WIDTH), lambda i: (0, 0))
    outs, _ = _call(
        body, grid=(s // CHUNK,),
        in_specs=[qspec, gspec, qspec, prev, cur, prev, cur, tab, tab, bq, sk],
        out_specs=[pl.BlockSpec((CHUNK, 2 * B_WIDTH), lambda i: (i, 0)), full, full, bq,
                   pl.BlockSpec((8, 128), lambda i: (0, 0))],
        out_shape=[jax.ShapeDtypeStruct((s, 2 * B_WIDTH), BF16), jax.ShapeDtypeStruct((s, 2 * KV_WIDTH), F32),
                   jax.ShapeDtypeStruct((s, 2 * KV_WIDTH), F32), jax.ShapeDtypeStruct((1, B_WIDTH), F32),
                   jax.ShapeDtypeStruct((8, 128), F32)],
        args=(zb, zb, dyb, k2, k2, v2, v2, cos, sin, b_bq, sinks), name="attn_bwd")
    return outs


def _place():
    x, y, c = lax.axis_index("x"), lax.axis_index("y"), lax.axis_index("c")
    return x, y, c, [(1 - x, y), (x, 1 - y), (1 - x, 1 - y)]


def _relations():
    return [(r >> 2 & 1, r >> 1 & 1, r & 1) for r in range(1, 8)]


def _gather_side(arrs):
    n = len(arrs)

    def copies(ins, outs, sems):
        send_ici, recv_ici, send_d2d, recv_d2d, local_sem = sems
        x, y, c, chips = _place()
        me = 2 * x + y

        def rows(a, half):
            hr = arrs[a].shape[0] // 2
            return pl.ds(half * hr, hr)

        def ici(a, j, src_chip, to):
            return pltpu.make_async_remote_copy(
                src_ref=ins[a].at[rows(a, c)], dst_ref=outs[a].at[src_chip, rows(a, c)],
                send_sem=send_ici.at[a, j], recv_sem=recv_ici.at[a, j], device_id=to, device_id_type=MESH)

        def d2d(a, j, chip, half):
            blk = outs[a].at[chip, rows(a, half)]
            return pltpu.make_async_remote_copy(
                src_ref=blk, dst_ref=blk, send_sem=send_d2d.at[a, j], recv_sem=recv_d2d.at[a, j],
                device_id=(x, y, 1 - c), device_id_type=MESH)

        local = [pltpu.make_async_copy(ins[a], outs[a].at[me], local_sem.at[a]) for a in range(n)]
        pairs = [(a, j, chip) for a in range(n) for j, chip in enumerate(chips)]
        return c, me, local, ici, d2d, pairs

    def start(ins, outs, sems):
        c, me, local, ici, _, pairs = copies(ins, outs, sems)
        for cp in local:
            cp.start()
        for a, j, chip in pairs:
            ici(a, j, me, (*chip, c)).start()

    def passing(ins, outs, sems):
        c, _, _, ici, d2d, pairs = copies(ins, outs, sems)
        for a, j, (px, py) in pairs:
            ici(a, j, 2 * px + py, (px, py, c)).wait_recv()
            d2d(a, j, 2 * px + py, c).start()

    def finish(ins, outs, sems):
        c, me, local, ici, d2d, pairs = copies(ins, outs, sems)
        for a, j, (px, py) in pairs:
            d2d(a, j, 2 * px + py, 1 - c).wait_recv()
        for a, j, (px, py) in pairs:
            ici(a, j, me, (px, py, c)).wait_send()
            d2d(a, j, 2 * px + py, c).wait_send()
        for cp in local:
            cp.wait()

    return _Side(arrs, [jax.ShapeDtypeStruct((N_CHIPS,) + a.shape, a.dtype) for a in arrs],
                 [pltpu.SemaphoreType.DMA((n, 3))] * 4 + [pltpu.SemaphoreType.DMA((n,))], start, finish,
                 passing=passing)


def _exchange_side(grads):
    n = len(grads)

    def copies(ins, outs, sems):
        send_sem, recv_sem = sems
        x, y, c, _ = _place()
        cps = []
        for a in range(n):
            hr = grads[a].shape[1] // 2
            cps.append(pltpu.make_async_remote_copy(
                src_ref=ins[a].at[:, pl.ds((1 - c) * hr, hr), :], dst_ref=outs[a],
                send_sem=send_sem.at[a], recv_sem=recv_sem.at[a], device_id=(x, y, 1 - c), device_id_type=MESH))
        return cps

    def start(ins, outs, sems):
        for cp in copies(ins, outs, sems):
            cp.start()

    def finish(ins, outs, sems):
        for cp in copies(ins, outs, sems):
            cp.wait()

    return _Side(grads, [jax.ShapeDtypeStruct((g.shape[0], g.shape[1] // 2, g.shape[2]), g.dtype) for g in grads],
                 [pltpu.SemaphoreType.DMA((n,))] * 2, start, finish)


def _scatter_side(chip_sums, small=None):
    n = len(chip_sums)
    arrs = list(chip_sums) + ([small] if small is not None else [])

    def copies(ins, outs, sems):
        x, y, c, chips = _place()
        cps = []
        for a in range(n):
            for j, (px, py) in enumerate(chips):
                cps.append(pltpu.make_async_remote_copy(
                    src_ref=ins[a].at[2 * px + py], dst_ref=outs[a].at[j],
                    send_sem=sems[0].at[a, j], recv_sem=sems[1].at[a, j], device_id=(px, py, c), device_id_type=MESH))
        if small is not None:
            for r, (fx, fy, fc) in enumerate(_relations(), start=1):
                px, py, pc = x ^ fx, y ^ fy, c ^ fc
                cps.append(pltpu.make_async_remote_copy(
                    src_ref=ins[n].at[4 * px + 2 * py + pc], dst_ref=outs[n].at[r],
                    send_sem=sems[2].at[r - 1], recv_sem=sems[3].at[r - 1], device_id=(px, py, pc),
                    device_id_type=MESH))
        return cps

    def start(ins, outs, sems):
        for cp in copies(ins, outs, sems):
            cp.start()

    def finish(ins, outs, sems):
        for cp in copies(ins, outs, sems):
            cp.wait()

    shapes = [jax.ShapeDtypeStruct((3,) + t.shape[1:], t.dtype) for t in chip_sums]
    sems = [pltpu.SemaphoreType.DMA((n, 3))] * 2
    if small is not None:
        shapes.append(jax.ShapeDtypeStruct(small.shape, small.dtype))
        sems += [pltpu.SemaphoreType.DMA((7,))] * 2
    return _Side(arrs, shapes, sems, start, finish)


def _small_scatter_side(small):
    def copies(ins, outs, sems):
        x, y, c, _ = _place()
        cps = []
        for r, (fx, fy, fc) in enumerate(_relations(), start=1):
            px, py, pc = x ^ fx, y ^ fy, c ^ fc
            cps.append(pltpu.make_async_remote_copy(
                src_ref=ins[0].at[4 * px + 2 * py + pc], dst_ref=outs[0].at[r],
                send_sem=sems[0].at[r - 1], recv_sem=sems[1].at[r - 1], device_id=(px, py, pc), device_id_type=MESH))
        return cps

    def start(ins, outs, sems):
        for cp in copies(ins, outs, sems):
            cp.start()

    def finish(ins, outs, sems):
        for cp in copies(ins, outs, sems):
            cp.wait()

    return _Side([small], [jax.ShapeDtypeStruct(small.shape, small.dtype)], [pltpu.SemaphoreType.DMA((7,))] * 2,
                 start, finish)


def _small_share_side(small):
    return _share_side([], small)


def _share_side(halves, small=None):
    n = len(halves)
    arrs = list(halves) + ([small] if small is not None else [])

    def copies(ins, outs, sems, mine):
        x, y, c, _ = _place()
        me = 4 * x + 2 * y + c
        cps = []
        for a in range(n):
            hr = halves[a].shape[0] // 2
            rows = pl.ds((c if mine else 1 - c) * hr, hr)
            cps.append(pltpu.make_async_remote_copy(
                src_ref=ins[a].at[rows], dst_ref=outs[a].at[rows],
                send_sem=sems[0].at[a], recv_sem=sems[1].at[a], device_id=(x, y, 1 - c), device_id_type=MESH))
        if small is not None:
            for r, (fx, fy, fc) in enumerate(_relations(), start=1):
                px, py, pc = x ^ fx, y ^ fy, c ^ fc
                seg = me if mine else 4 * px + 2 * py + pc
                cps.append(pltpu.make_async_remote_copy(
                    src_ref=ins[n].at[seg], dst_ref=outs[n].at[seg],
                    send_sem=sems[-2].at[r - 1], recv_sem=sems[-1].at[r - 1], device_id=(px, py, pc),
                    device_id_type=MESH))
        return cps

    def start(ins, outs, sems):
        for cp in copies(ins, outs, sems, True):
            cp.start()

    def finish(ins, outs, sems):
        for cp in copies(ins, outs, sems, False):
            cp.wait_recv()
        for cp in copies(ins, outs, sems, True):
            cp.wait_send()

    sems = ([pltpu.SemaphoreType.DMA((n,))] * 2 if n else []) + (
        [pltpu.SemaphoreType.DMA((7,))] * 2 if small is not None else [])
    return _Side(arrs, [jax.ShapeDtypeStruct(h.shape, h.dtype) for h in arrs], sems, start, finish,
                 aliases={i: i for i in range(len(arrs))})


GATHER_PIECES = [(0, 0), (0, 1), (1, 0), (2, 0), (1, 1), (2, 1), (3, 0), (3, 1)]


def _mm_gathering(a, shard, order, *, name, tm=1024):
    s, k = a.shape
    nc = shard.shape[1]
    tm = _row_tile(s, tm)
    tn = nc // 2
    hr = k // 2
    qr = hr // 2
    blocks = jnp.stack([order[src] * 2 + h for src, h in GATHER_PIECES]).astype(jnp.int32)

    def body(blocks_ref, a_ref, shard_ref, z_ref, full_ref, wbuf, send_ici, recv_ici, send_relay,
             recv_relay, send_d2d, recv_d2d, local_sem, load_sem):
        piece, i = pl.program_id(0), pl.program_id(1)
        x, y, c, chips = _place()
        me = 2 * x + y
        nbrs = chips[:2]
        chip_of = [2 * px + py for px, py in chips]

        def quarter(q):
            return pl.ds(c * hr + q * qr, qr)

        def sibling_quarter(q):
            return pl.ds((1 - c) * hr + q * qr, qr)

        def whole(half):
            return pl.ds(half * hr, hr)

        def cols(h):
            return pl.ds(h * tn, tn)

        def direct(j, src_chip, h):
            return pltpu.make_async_remote_copy(
                src_ref=shard_ref.at[whole(c), cols(h)], dst_ref=full_ref.at[src_chip, whole(c), cols(h)],
                send_sem=send_ici.at[j, h], recv_sem=recv_ici.at[j, h], device_id=(*nbrs[j], c), device_id_type=MESH)

        def relay(j, src_chip, h):
            blk = full_ref.at[src_chip, quarter(j), cols(h)]
            return pltpu.make_async_remote_copy(
                src_ref=blk, dst_ref=blk, send_sem=send_relay.at[j, h], recv_sem=recv_relay.at[j, h],
                device_id=(*nbrs[1 - j], c), device_id_type=MESH)

        def d2d(j, chip, rows, h):
            blk = full_ref.at[chip, rows, cols(h)]
            return pltpu.make_async_remote_copy(
                src_ref=blk, dst_ref=blk, send_sem=send_d2d.at[j, h], recv_sem=recv_d2d.at[j, h],
                device_id=(x, y, 1 - c), device_id_type=MESH)

        def load(p):
            src, h = GATHER_PIECES[p]
            where = shard_ref if src == 0 else full_ref.at[chip_of[src - 1]]
            return pltpu.make_async_copy(where.at[:, cols(h)], wbuf.at[p % 2], load_sem.at[p % 2])

        local = pltpu.make_async_copy(shard_ref, full_ref.at[me], local_sem)

        def arrived(p):
            src, h = GATHER_PIECES[p]
            if src in (1, 2):
                j = src - 1
                direct(j, chip_of[j], h).wait_recv()
                relay(j, chip_of[j], h).start()
                d2d(j, chip_of[j], whole(c), h).start()
            elif src == 3:
                for j in range(2):
                    relay(1 - j, chip_of[2], h).wait_recv()
                    d2d(2 + j, chip_of[2], quarter(1 - j), h).start()

        def fetch(p):
            src, h = GATHER_PIECES[p]
            if src in (1, 2):
                d2d(src - 1, chip_of[src - 1], whole(1 - c), h).wait_recv()
            elif src == 3:
                for j in range(2):
                    d2d(2 + j, chip_of[2], sibling_quarter(1 - j), h).wait_recv()
            load(p).start()

        n_i = s // tm
        for p in range(len(GATHER_PIECES)):
            @pl.when(jnp.logical_and(piece == p, i == 0))
            def _(p=p):
                if p == 0:
                    local.start()
                    for hh in range(2):
                        for j in range(2):
                            direct(j, me, hh).start()
                    load(0).start()
                load(p).wait()

        z_ref[...] = jnp.dot(a_ref[...], wbuf[piece % 2], preferred_element_type=F32).astype(z_ref.dtype)

        for p in range(len(GATHER_PIECES) - 1):
            @pl.when(jnp.logical_and(piece == p, i == min(1, n_i - 1)))
            def _(p=p):
                arrived(p + 1)

            @pl.when(jnp.logical_and(piece == p, i == min(2, n_i - 1)))
            def _(p=p):
                fetch(p + 1)

        last = jnp.logical_and(piece == len(GATHER_PIECES) - 1, i == n_i - 1)

        @pl.when(last)
        def _():
            for h in range(2):
                for j in range(2):
                    direct(j, me, h).wait_send()
                    relay(j, chip_of[j], h).wait_send()
                    d2d(j, chip_of[j], whole(c), h).wait_send()
                    d2d(2 + j, chip_of[2], quarter(1 - j), h).wait_send()
            local.wait()

    return pl.pallas_call(
        body,
        grid_spec=pltpu.PrefetchScalarGridSpec(
            num_scalar_prefetch=1, grid=(len(GATHER_PIECES), s // tm),
            in_specs=[pl.BlockSpec((tm, k), lambda p, i, blocks: (i, 0)), HBM],
            out_specs=[pl.BlockSpec((tm, tn), lambda p, i, blocks: (i, blocks[p])), HBM],
            scratch_shapes=[pltpu.VMEM((2, k, tn), BF16)] + [pltpu.SemaphoreType.DMA((2, 2))] * 4
            + [pltpu.SemaphoreType.DMA((4, 2))] * 2 + [pltpu.SemaphoreType.DMA, pltpu.SemaphoreType.DMA((2,))]),
        out_shape=[jax.ShapeDtypeStruct((s, N_CHIPS * nc), BF16), jax.ShapeDtypeStruct((N_CHIPS, k, nc), BF16)],
        name=name, compiler_params=_cparams(),
    )(blocks, a, shard)


def _mm_tn_exchanging(a, b, *, name, shards, tk=2048, side=None):
    s, m = a.shape
    nc = b.shape[1] // shards
    tk = _row_tile(s, tk)
    nk = s // tk
    hm = m // 2

    def body(a_ref, b_ref, part_ref, sib_ref, acc, keep_sem, send_sem, recv_sem):
        j, kk = pl.program_id(0), pl.program_id(1)
        x, y, c, _ = _place()

        def keep(jj, slot):
            mine = pl.ds(c * hm, hm)
            return pltpu.make_async_copy(acc.at[slot, mine], part_ref.at[jj, mine], keep_sem.at[slot])

        def give(jj, slot):
            return pltpu.make_async_remote_copy(
                src_ref=acc.at[slot, pl.ds((1 - c) * hm, hm)], dst_ref=sib_ref.at[jj],
                send_sem=send_sem.at[slot], recv_sem=recv_sem.at[jj], device_id=(x, y, 1 - c), device_id_type=MESH)

        part = lax.dot_general(a_ref[...], b_ref[...], TN, preferred_element_type=F32)
        for slot in range(2):
            @pl.when(j % 2 == slot)
            def _(slot=slot):
                @pl.when(jnp.logical_and(kk == 0, j >= 2))
                def _():
                    keep(j - 2, slot).wait()
                    give(j - 2, slot).wait_send()

                @pl.when(kk == 0)
                def _():
                    acc[slot] = part

                @pl.when(kk > 0)
                def _():
                    acc[slot] += part

                @pl.when(kk == nk - 1)
                def _():
                    keep(j, slot).start()
                    give(j, slot).start()

        @pl.when(jnp.logical_and(j == shards - 1, kk == nk - 1))
        def _():
            for jj in range(shards - 2, shards):
                keep(jj, jj % 2).wait()
                give(jj, jj % 2).wait_send()
            for jj in range(shards):
                give(jj, jj % 2).wait_recv()

    assert shards >= 2
    return _call(
        body, grid=(shards, nk),
        in_specs=[pl.BlockSpec((tk, m), lambda j, kk: (kk, 0)), pl.BlockSpec((tk, nc), lambda j, kk: (kk, j))],
        out_specs=[HBM, HBM],
        out_shape=[jax.ShapeDtypeStruct((shards, m, nc), F32), jax.ShapeDtypeStruct((shards, hm, nc), F32)],
        scratch=[pltpu.VMEM((2, m, nc), F32), pltpu.SemaphoreType.DMA((2,)), pltpu.SemaphoreType.DMA((2,)),
                 pltpu.SemaphoreType.DMA((shards,))],
        args=(a, b), name=name, side=side)


def _col_tile(cols):
    return cols if cols <= 2048 else 512


def _add_sibling(grad, recv, core, *, name):
    k, r, c = grad.shape
    hr = r // 2
    tr = min(hr, 256)
    tc = _col_tile(c)
    nrb = hr // tr

    def body(core_ref, g_ref, r_ref, o_ref):
        o_ref[...] = (g_ref[...] + r_ref[...]).astype(BF16)

    return pl.pallas_call(
        body,
        grid_spec=pltpu.PrefetchScalarGridSpec(
            num_scalar_prefetch=1, grid=(k, nrb, c // tc),
            in_specs=[pl.BlockSpec((None, tr, tc), lambda kk, i, j, core: (kk, core[0] * nrb + i, j)),
                      pl.BlockSpec((None, tr, tc), lambda kk, i, j, core: (kk, i, j))],
            out_specs=pl.BlockSpec((None, tr, tc), lambda kk, i, j, core: (kk, i, j))),
        out_shape=jax.ShapeDtypeStruct((k, hr, c), BF16), name=name, compiler_params=_cparams(),
    )(core, grad, recv)


def _sum_chips(grad, from_sibling, recv, place, *, name):
    _, hr, c = from_sibling.shape
    tr = min(hr, 256)
    tc = _col_tile(c)
    nrb = hr // tr

    def body(place_ref, g_ref, s_ref, r0_ref, r1_ref, r2_ref, o_ref):
        own = g_ref[...] + s_ref[...]
        o_ref[...] = ((own + r0_ref[...].astype(F32)) + r1_ref[...].astype(F32)) + r2_ref[...].astype(F32)

    def rspec(j):
        return pl.BlockSpec((None, tr, tc), lambda i, jj, place: (j, i, jj))

    return pl.pallas_call(
        body,
        grid_spec=pltpu.PrefetchScalarGridSpec(
            num_scalar_prefetch=1, grid=(nrb, c // tc),
            in_specs=[pl.BlockSpec((None, tr, tc), lambda i, jj, place: (place[0], place[1] * nrb + i, jj)),
                      pl.BlockSpec((None, tr, tc), lambda i, jj, place: (place[0], i, jj)),
                      rspec(0), rspec(1), rspec(2)],
            out_specs=pl.BlockSpec((tr, tc), lambda i, jj, place: (place[1] * nrb + i, jj))),
        out_shape=jax.ShapeDtypeStruct((2 * hr, c), F32), name=name, compiler_params=_cparams(),
    )(place, grad, from_sibling, recv, recv, recv)


def _sum_small(small, recv, place):
    _, sr, _ = small.shape

    def body(place_ref, own_ref, r_ref, o_ref):
        acc = own_ref[...]
        for r in range(1, 8):
            acc = acc + r_ref[r]
        o_ref[...] = acc

    return pl.pallas_call(
        body,
        grid_spec=pltpu.PrefetchScalarGridSpec(
            num_scalar_prefetch=1, grid=(1,),
            in_specs=[pl.BlockSpec((None, sr, 128), lambda i, place: (place[2], 0, 0)),
                      pl.BlockSpec((8, sr, 128), lambda i, place: (0, 0, 0))],
            out_specs=pl.BlockSpec((None, sr, 128), lambda i, place: (place[2], 0, 0))),
        out_shape=jax.ShapeDtypeStruct(small.shape, F32), name="sum_small", compiler_params=_cparams(),
    )(place, small, recv)


def _spread_side(vec):
    def copies(ins, outs, sems):
        x, y, c, _ = _place()
        return [pltpu.make_async_remote_copy(
            src_ref=ins[0], dst_ref=outs[0].at[r], send_sem=sems[0].at[r - 1], recv_sem=sems[1].at[r - 1],
            device_id=(x ^ fx, y ^ fy, c ^ fc), device_id_type=MESH)
            for r, (fx, fy, fc) in enumerate(_relations(), start=1)]

    def start(ins, outs, sems):
        for cp in copies(ins, outs, sems):
            cp.start()

    def finish(ins, outs, sems):
        for cp in copies(ins, outs, sems):
            cp.wait()

    return _Side([vec], [jax.ShapeDtypeStruct((8,) + vec.shape, vec.dtype)], [pltpu.SemaphoreType.DMA((7,))] * 2,
                 start, finish)


def _sum_in_device_order(own, spread, place):
    def body(place_ref, own_ref, r_ref, o_ref):
        me = place_ref[2]
        acc = jnp.zeros_like(own_ref[...])
        for d in range(8):
            slot = jnp.where(me == d, 1, me ^ d)
            acc = acc + jnp.where(me == d, own_ref[...], r_ref[slot])
        o_ref[...] = acc

    return pl.pallas_call(
        body,
        grid_spec=pltpu.PrefetchScalarGridSpec(
            num_scalar_prefetch=1, grid=(1,),
            in_specs=[pl.BlockSpec(own.shape, lambda i, place: (0, 0)),
                      pl.BlockSpec(spread.shape, lambda i, place: (0, 0, 0))],
            out_specs=pl.BlockSpec(own.shape, lambda i, place: (0, 0))),
        out_shape=jax.ShapeDtypeStruct(own.shape, F32), name="sum_in_device_order", compiler_params=_cparams(),
    )(place, own, spread)


def _adamw(w, g, m, v, *, name):
    r, c = w.shape
    tr = 256 if r % 256 == 0 else r
    tc = _col_tile(c)
    bc1 = 1.0 - ADAM_B1 ** ADAM_STEP
    bc2 = 1.0 - ADAM_B2 ** ADAM_STEP

    def body(w_ref, g_ref, m_ref, v_ref, d_ref, nm_ref, nv_ref):
        gv = g_ref[...]
        nm = ADAM_B1 * m_ref[...] + (1.0 - ADAM_B1) * gv
        nv = ADAM_B2 * v_ref[...] + (1.0 - ADAM_B2) * (gv * gv)
        d_ref[...] = -ADAM_LR * ((nm / bc1) / (jnp.sqrt(nv / bc2) + ADAM_EPS) + ADAM_WD * w_ref[...])
        nm_ref[...] = nm
        nv_ref[...] = nv

    spec = pl.BlockSpec((tr, tc), lambda i, j: (i, j))
    outs, _ = _call(body, grid=(r // tr, c // tc), in_specs=[spec] * 4, out_specs=[spec] * 3,
                    out_shape=[jax.ShapeDtypeStruct((r, c), F32)] * 3, args=(w, g, m, v), name=name)
    return outs


SMALL_ORDER = ["a_ws", "a_bs", "a_norm_g", "a_ln_g", "a_ln_b", "kv_norm_g", "b_kv", "b_norm_g", "b_bq",
               "b_sinks", "final_norm_g"]
SHARDED_SMALL = {"a_norm_g", "a_ln_g", "a_ln_b"}
PACK_TILE = 8 * 128


def _rows128(a):
    flat = a.reshape(-1)
    return jnp.pad(flat, (0, (-flat.shape[0]) % PACK_TILE)).reshape(-1, 128)


def _pack_rows(parts, multiple):
    rows = [_rows128(p) for p in parts]
    total = sum(r.shape[0] for r in rows)
    pad = (-total) % multiple
    if pad:
        rows.append(jnp.zeros((pad, 128), rows[0].dtype))
    return jnp.concatenate(rows, axis=0)


def _unpack_rows(packed, shapes):
    out, row = [], 0
    for shp in shapes:
        size = math.prod(shp)
        nrow = -(-size // PACK_TILE) * 8
        out.append(packed[row:row + nrow].reshape(-1)[:size].reshape(shp))
        row += nrow
    return out


WEIGHTS = ["a_norm_g", "a_w_in", "a_ln_g", "a_ln_b", "a_ws", "a_bs", "a_w_out", "kv_norm_g", "w_kv", "b_kv",
           "b_norm_g", "b_w_in", "b_bq", "b_sinks", "b_w_out", "final_norm_g"]
BIG = ["a_w_in", "a_w_out", "w_kv", "b_w_in", "b_w_out"]


class _Reduction:
    def __init__(self, names, partials, core, place, small=None):
        self.names, self.partials, self.core, self.place, self.small = names, partials, core, place, small

    def exchange_side(self):
        return _exchange_side(self.partials)

    def took_exchange(self, from_sibling):
        self.from_sibling = from_sibling
        self.chip_sums = [_add_sibling(g, r, self.core, name="add_sibling_" + n)
                          for g, r, n in zip(self.partials, from_sibling, self.names)]

    def scatter_side(self):
        return _scatter_side(self.chip_sums, self.small)

    def took_scatter(self, arrived):
        big = arrived[:len(self.names)]
        self.halves = [_sum_chips(g, fs, r, self.place, name="sum_chips_" + n)
                       for g, fs, r, n in zip(self.partials, self.from_sibling, big, self.names)]
        self.small_mine = _sum_small(self.small, arrived[-1], self.place) if self.small is not None else None

    def share_side(self):
        return _share_side(self.halves, self.small_mine)

    def took_share(self, shared):
        self.grads = dict(zip(self.names, shared[:len(self.names)]))
        self.small_full = shared[-1] if self.small is not None else None


def _step(x, loss_target, p, m, v):
    xi, yi, ci = lax.axis_index("x"), lax.axis_index("y"), lax.axis_index("c")
    chip = 2 * xi + yi
    device = 4 * xi + 2 * yi + ci
    core = jnp.reshape(ci, (1,)).astype(jnp.int32)
    place = jnp.stack([chip, ci, device]).astype(jnp.int32)
    x, tgt = x[0], loss_target[0]
    s = x.shape[0]
    cos, sin = _rope_tables(s)

    shard2d = {n: p[n].reshape(p[n].shape[-2:]) for n in BIG}
    shard_bf = {n: shard2d[n].astype(BF16) for n in BIG}
    ws = p["a_ws"][0]
    ws_t = jnp.swapaxes(ws, 1, 2)
    bs_t = p["a_bs"][0].T
    kv_norm_g, b_kv = p["kv_norm_g"].reshape(1, -1), p["b_kv"].reshape(1, -1)
    final_norm_g = p["final_norm_g"].reshape(1, -1)

    vec_shapes = [p[n].shape for n in ("a_norm_g", "a_ln_g", "a_ln_b")]
    vec_pack = _pack_rows([p["a_norm_g"], p["a_ln_g"], p["a_ln_b"]], 16)
    (vec_all,) = _comm_call(_gather_side([vec_pack]), "gather_vectors")
    vecs = [_unpack_rows(vec_all[k], vec_shapes) for k in range(N_CHIPS)]
    a_norm_g, a_ln_g, a_ln_b = (jnp.concatenate([vk[t] for vk in vecs], axis=-1) for t in range(3))

    (n_a,) = _rms_fwd(x, [a_norm_g], name="rms_a")
    order = jnp.stack([chip, 2 * (1 - xi) + yi, 2 * xi + (1 - yi), 2 * (1 - xi) + (1 - yi)]).astype(jnp.int32)
    z, a_w_in = _mm_gathering(n_a, shard_bf["a_w_in"], order, name="mm_a_in")
    y, (a_w_out,) = _gate_fwd(z, a_ln_g, a_ln_b, ws, bs_t, side=_gather_side([shard_bf["a_w_out"]]))
    a_w_out = a_w_out.reshape(A_WIDTH, D_MODEL)
    (h1, n_kv, n_b), (w_kv, b_w_in) = _mm_residual_norms(
        y, a_w_out, x, [kv_norm_g, p["b_norm_g"]], name="mm_a_out",
        side=_gather_side([shard_bf["w_kv"], shard_bf["b_w_in"]]))
    w_kv = w_kv.reshape(D_MODEL, 2 * KV_WIDTH)
    kv = _mm_nn(n_kv, w_kv, name="mm_kv", tn=2 * KV_WIDTH)
    kr, vv = _kv_rope(kv, b_kv, cos, sin)
    zb = _mm_nn(n_b, b_w_in, name="mm_b_in", tn=512, tm=1024, out_dtype=BF16)
    yb, (b_w_out,) = _attn_fwd(zb, kr, vv, cos, sin, p["b_bq"], p["b_sinks"], side=_gather_side([shard_bf["b_w_out"]]))
    b_w_out = b_w_out.reshape(B_WIDTH, D_MODEL)
    loss_blk, dh2, dh2b, d_final_g = _mm_residual_loss(yb, b_w_out, h1, tgt, final_norm_g, name="mm_b_out")

    d_b_w_out = _mm_tn(yb, dh2b, name="mm_d_b_w_out", tm=B_WIDTH, tn=D_MODEL)
    red_bo = _Reduction(["b_w_out"], [d_b_w_out.reshape(N_CHIPS, B_WIDTH // N_CHIPS, D_MODEL)], core, place)
    dyb, got = _mm_nt(dh2b, b_w_out, name="mm_dyb", out_dtype=BF16, side=red_bo.exchange_side())
    red_bo.took_exchange(got)
    dzb, dk_rot, dv, d_bq, d_sinks = _attn_bwd(zb, dyb, kr, vv, cos, sin, p["b_bq"], p["b_sinks"])
    dkv, d_b_kv = _kv_rope_bwd(dk_rot, dv, cos, sin)
    d_b_w_in, got = _mm_tn(n_b, dzb, name="mm_d_b_w_in", tm=D_MODEL, tn=512, shards=N_CHIPS,
                           side=red_bo.scatter_side())
    red_bo.took_scatter(got)
    d_w_kv, got = _mm_tn(n_kv, dkv, name="mm_d_w_kv", tm=D_MODEL, tn=2 * KV_WIDTH, side=red_bo.share_side())
    red_bo.took_share(got)
    red_bi = _Reduction(["b_w_in", "w_kv"], [d_b_w_in, d_w_kv.reshape(N_CHIPS, D_MODEL // N_CHIPS, 2 * KV_WIDTH)],
                        core, place)
    (dh1, dh1b, d_kv_g, d_b_g), got = _mm_nt_rms_bwd(
        [(dkv, w_kv, kv_norm_g), (dzb, b_w_in, p["b_norm_g"])], h1, dh2, name="mm_dn_b", tm=512,
        side=red_bi.exchange_side())
    red_bi.took_exchange(got)

    d_a_w_out, got = _mm_tn(y, dh1b, name="mm_d_a_w_out", tm=1024, tn=D_MODEL, side=red_bi.scatter_side())
    red_bi.took_scatter(got)
    red_ao = _Reduction(["a_w_out"], [d_a_w_out.reshape(N_CHIPS, A_WIDTH // N_CHIPS, D_MODEL)], core, place)
    sides = [red_ao.exchange_side(), red_bi.share_side()]
    dy, got = _mm_nt(dh1b, a_w_out, name="mm_dy", tn=1024, out_dtype=BF16, side=_join(sides))
    got = _split(got, sides)
    red_ao.took_exchange(got[0])
    red_bi.took_share(got[1])
    (dz, d_ln_g, d_ln_b, d_ws, d_bs_t), got = _gate_bwd(z, dy, a_ln_g, a_ln_b, ws, ws_t, bs_t,
                                                        side=red_ao.scatter_side())
    red_ao.took_scatter(got)
    small = {
        "a_ws": d_ws, "a_bs": d_bs_t.T, "a_ln_g": d_ln_g, "a_ln_b": d_ln_b,
        "kv_norm_g": d_kv_g, "b_kv": d_b_kv, "b_norm_g": d_b_g, "b_bq": d_bq,
        "b_sinks": d_sinks[0:1, :N_Q_HEADS], "final_norm_g": d_final_g,
    }
    packed = [n for n in SMALL_ORDER if n != "a_norm_g"]
    small_shapes = [small[n].shape for n in packed] + [(1, 1)]
    small_pack = _pack_rows([small[n] for n in packed] + [loss_blk[0:1, 0:1]], 64)
    seg = small_pack.shape[0] // 8
    small_pack = small_pack.reshape(8, seg, 128)
    sides = [red_ao.share_side(), _small_scatter_side(small_pack)]
    (d_a_w_in, from_sibling), got = _mm_tn_exchanging(n_a, dz, name="mm_d_a_w_in", shards=N_CHIPS, side=_join(sides))
    got = _split(got, sides)
    red_ao.took_share(got[0])
    small_mine = _sum_small(small_pack, got[1][0], place)

    red_ai = _Reduction(["a_w_in"], [d_a_w_in], core, place)
    red_ai.took_exchange([from_sibling])
    sides = [red_ai.scatter_side(), _small_share_side(small_mine)]
    (dx, _, d_a_g), got = _mm_nt_rms_bwd([(dz, a_w_in, a_norm_g)], x, dh1, name="mm_dn_a", tm=256, side=_join(sides))
    got = _split(got, sides)
    red_ai.took_scatter(got[0])
    small_all = got[1][0]
    d_a_g = _rows128(d_a_g)
    sides = [red_ai.share_side(), _spread_side(d_a_g)]
    got = _split(_comm_call(_join(sides), "share_last"), sides)
    red_ai.took_share(got[0])
    small_full = dict(zip(packed + ["loss"], _unpack_rows(small_all.reshape(8 * seg, 128), small_shapes)))
    small_full["a_norm_g"] = _sum_in_device_order(d_a_g, got[1][0], place).reshape(1, -1)
    loss = small_full["loss"].reshape(())

    grad_big = {**red_bo.grads, **red_bi.grads, **red_ao.grads, **red_ai.grads}
    grads = {}
    for n in SMALL_ORDER:
        gfull = small_full[n]
        if n in SHARDED_SMALL:
            width = p[n].shape[-1]
            gfull = lax.dynamic_slice_in_dim(gfull, chip * width, width, axis=-1)
        grads[n] = gfull.reshape(p[n].shape)
    for n in BIG:
        grads[n] = grad_big[n].reshape(p[n].shape)

    delta, new_m, new_v = {}, {}, {}
    for n in BIG:
        d, nm, nv = _adamw(shard2d[n], grad_big[n], m[n].reshape(shard2d[n].shape), v[n].reshape(shard2d[n].shape),
                           name="adamw_" + n)
        delta[n], new_m[n], new_v[n] = d.reshape(p[n].shape), nm.reshape(p[n].shape), nv.reshape(p[n].shape)
    shapes = [p[n].shape for n in SMALL_ORDER]
    packs = [_pack_rows([src[n] for n in SMALL_ORDER], 8) for src in (p, grads, m, v)]
    outs = _adamw(*packs, name="adamw_small")
    for res, packed in zip((delta, new_m, new_v), outs):
        for n, val in zip(SMALL_ORDER, _unpack_rows(packed, shapes)):
            res[n] = val

    return (loss, dx[None], *[grads[n] for n in WEIGHTS], *[delta[n] for n in WEIGHTS],
            *[new_m[n] for n in WEIGHTS], *[new_v[n] for n in WEIGHTS])


def kernel(x, a_norm_g, a_w_in, a_ln_g, a_ln_b, a_ws, a_bs, a_w_out, kv_norm_g, w_kv, b_kv, b_norm_g, b_w_in, b_bq, b_sinks, b_w_out, final_norm_g, loss_target, m_a_norm_g, m_a_w_in, m_a_ln_g, m_a_ln_b, m_a_ws, m_a_bs, m_a_w_out, m_kv_norm_g, m_w_kv, m_b_kv, m_b_norm_g, m_b_w_in, m_b_bq, m_b_sinks, m_b_w_out, m_final_norm_g, v_a_norm_g, v_a_w_in, v_a_ln_g, v_a_ln_b, v_a_ws, v_a_bs, v_a_w_out, v_kv_norm_g, v_w_kv, v_b_kv, v_b_norm_g, v_b_w_in, v_b_bq, v_b_sinks, v_b_w_out, v_final_norm_g):
    p = dict(a_norm_g=a_norm_g, a_w_in=a_w_in, a_ln_g=a_ln_g, a_ln_b=a_ln_b, a_ws=a_ws, a_bs=a_bs, a_w_out=a_w_out,
             kv_norm_g=kv_norm_g, w_kv=w_kv, b_kv=b_kv, b_norm_g=b_norm_g, b_w_in=b_w_in, b_bq=b_bq, b_sinks=b_sinks,
             b_w_out=b_w_out, final_norm_g=final_norm_g)
    m = dict(a_norm_g=m_a_norm_g, a_w_in=m_a_w_in, a_ln_g=m_a_ln_g, a_ln_b=m_a_ln_b, a_ws=m_a_ws, a_bs=m_a_bs,
             a_w_out=m_a_w_out, kv_norm_g=m_kv_norm_g, w_kv=m_w_kv, b_kv=m_b_kv, b_norm_g=m_b_norm_g, b_w_in=m_b_w_in,
             b_bq=m_b_bq, b_sinks=m_b_sinks, b_w_out=m_b_w_out, final_norm_g=m_final_norm_g)
    v = dict(a_norm_g=v_a_norm_g, a_w_in=v_a_w_in, a_ln_g=v_a_ln_g, a_ln_b=v_a_ln_b, a_ws=v_a_ws, a_bs=v_a_bs,
             a_w_out=v_a_w_out, kv_norm_g=v_kv_norm_g, w_kv=v_w_kv, b_kv=v_b_kv, b_norm_g=v_b_norm_g, b_w_in=v_b_w_in,
             b_bq=v_b_bq, b_sinks=v_b_sinks, b_w_out=v_b_w_out, final_norm_g=v_final_norm_g)
    return _step(x, loss_target, p, m, v)
```

```python
import functools
import math

import jax
import jax.numpy as jnp
from jax import lax
from jax.experimental import pallas as pl
from jax.experimental.pallas import tpu as pltpu

F32 = jnp.float32
BF16 = jnp.bfloat16

D_MODEL = 1024
CHUNK = 128
A_WIDTH = 2048
A_GROUPS = 16
HEAD_DIM = 64
N_Q_HEADS = 16
N_KV_HEADS = 2
Q_PER_KV = 8
B_WIDTH = 1024
KV_WIDTH = 128
ROPE_THETA = 10000.0
EPS = 1e-5
N_CHIPS = 4

ADAM_LR = 0.001
ADAM_B1 = 0.9
ADAM_B2 = 0.999
ADAM_EPS = 1e-08
ADAM_WD = 0.01
ADAM_STEP = 10

VMEM_LIMIT = 48 * 1024 * 1024
MESH = pl.DeviceIdType.MESH
NEG_BIG = -1e30
HBM = pl.BlockSpec(memory_space=pl.ANY)

NN = (((1,), (0,)), ((), ()))
NT = (((1,), (1,)), ((), ()))
TN = (((0,), (0,)), ((), ()))


def _cparams(**kw):
    return pltpu.CompilerParams(vmem_limit_bytes=VMEM_LIMIT, **kw)


class _Side:
    def __init__(self, ins, out_shapes, sems, start, finish, aliases=None, passing=None):
        self.ins, self.out_shapes, self.sems = list(ins), list(out_shapes), list(sems)
        self.start, self.finish = start, finish
        self.passing = passing or (lambda ins, outs, sems: None)
        self.aliases = dict(aliases or {})


def _join(sides):
    sides = [s for s in sides if s is not None]
    if not sides:
        return None
    offs, i, o, m = [], 0, 0, 0
    for s in sides:
        offs.append((i, o, m))
        i, o, m = i + len(s.ins), o + len(s.out_shapes), m + len(s.sems)

    def run(which):
        def go(ins, outs, sems):
            for s, (a, b, c) in zip(sides, offs):
                getattr(s, which)(ins[a:a + len(s.ins)], outs[b:b + len(s.out_shapes)], sems[c:c + len(s.sems)])
        return go

    aliases = {}
    for s, (a, b, _) in zip(sides, offs):
        aliases.update({a + k: b + v for k, v in s.aliases.items()})
    return _Side([x for s in sides for x in s.ins], [x for s in sides for x in s.out_shapes],
                 [x for s in sides for x in s.sems], run("start"), run("finish"), aliases, run("passing"))


def _split(side_outs, sides):
    out, pos = [], 0
    for s in sides:
        out.append(list(side_outs[pos:pos + len(s.out_shapes)]))
        pos += len(s.out_shapes)
    return out


def _call(body, *, grid, in_specs, out_specs, out_shape, args, name, scratch=(), side=None, prefetch=()):
    in_specs, out_specs, out_shape, scratch = list(in_specs), list(out_specs), list(out_shape), list(scratch)
    n_pre = len(prefetch)
    if side is None:
        res = pl.pallas_call(
            body, grid_spec=pltpu.PrefetchScalarGridSpec(
                num_scalar_prefetch=n_pre, grid=grid, in_specs=in_specs, out_specs=out_specs, scratch_shapes=scratch),
            out_shape=out_shape, name=name, compiler_params=_cparams())(*prefetch, *args)
        return list(res), []
    n_in, n_out, n_sc = len(in_specs), len(out_specs), len(scratch)
    s_in, s_out = len(side.ins), len(side.out_shapes)

    def wrapped(*refs):
        pre, refs = refs[:n_pre], refs[n_pre:]
        ins, refs = refs[:n_in], refs[n_in:]
        side_ins, refs = refs[:s_in], refs[s_in:]
        outs, refs = refs[:n_out], refs[n_out:]
        side_outs, refs = refs[:s_out], refs[s_out:]
        scr, side_sems = refs[:n_sc], refs[n_sc:]
        step = 0
        for a, g in enumerate(grid):
            step = step * g + pl.program_id(a)
        steps = math.prod(grid)

        @pl.when(step == 0)
        def _():
            side.start(side_ins, side_outs, side_sems)

        body(*pre, *ins, *outs, *scr)

        @pl.when(step == (3 * (steps - 1)) // 4)
        def _():
            side.passing(side_ins, side_outs, side_sems)

        @pl.when(step == steps - 1)
        def _():
            side.finish(side_ins, side_outs, side_sems)

    res = pl.pallas_call(
        wrapped, grid_spec=pltpu.PrefetchScalarGridSpec(
            num_scalar_prefetch=n_pre, grid=grid, in_specs=in_specs + [HBM] * s_in,
            out_specs=out_specs + [HBM] * s_out, scratch_shapes=scratch + side.sems),
        out_shape=out_shape + side.out_shapes,
        input_output_aliases={n_pre + n_in + k: n_out + v for k, v in side.aliases.items()},
        name=name, compiler_params=_cparams(),
    )(*prefetch, *args, *side.ins)
    return list(res[:n_out]), list(res[n_out:])


def _comm_call(side, name):
    s_in, s_out = len(side.ins), len(side.out_shapes)

    def body(*refs):
        ins, outs, sems = refs[:s_in], refs[s_in:s_in + s_out], refs[s_in + s_out:]
        side.start(ins, outs, sems)
        side.passing(ins, outs, sems)
        side.finish(ins, outs, sems)

    return list(pl.pallas_call(
        body, in_specs=[HBM] * s_in, out_specs=[HBM] * s_out, out_shape=side.out_shapes, scratch_shapes=side.sems,
        input_output_aliases=side.aliases, name=name,
    )(*side.ins))


def _matmul(a, b, *, dims, grid, a_spec, b_spec, o_spec, out_shape, name, acc_axis=None,
            residual=None, r_spec=None, side=None):
    has_res = residual is not None

    def body(*refs):
        if has_res:
            a_ref, b_ref, r_ref, o_ref = refs
        else:
            a_ref, b_ref, o_ref = refs
        part = lax.dot_general(a_ref[...], b_ref[...], dims, preferred_element_type=F32)
        if acc_axis is None:
            if has_res:
                part = part + r_ref[...]
            o_ref[...] = part.astype(o_ref.dtype)
        else:
            k = pl.program_id(acc_axis)

            @pl.when(k == 0)
            def _():
                o_ref[...] = part

            @pl.when(k > 0)
            def _():
                o_ref[...] += part

    in_specs = [a_spec, b_spec] + ([r_spec] if has_res else [])
    args = (a, b) + ((residual,) if has_res else ())
    (out,), side_outs = _call(body, grid=grid, in_specs=in_specs, out_specs=[o_spec], out_shape=[out_shape],
                              args=args, name=name, side=side)
    return (out, side_outs) if side is not None else out


def _row_tile(s, want):
    return min(s, want)


def _mm_nn(a, b, *, name, tn, out_dtype=F32, residual=None, tm=512, side=None):
    s, k = a.shape
    tm = _row_tile(s, tm)
    if b.ndim == 3:
        nsh, _, nc = b.shape
        npb = nc // tn
        n = nsh * nc
        b_spec = pl.BlockSpec((None, k, tn), lambda i, j: (j // npb, 0, j % npb))
    else:
        n = b.shape[1]
        b_spec = pl.BlockSpec((k, tn), lambda i, j: (0, j))
    return _matmul(
        a, b, dims=NN, grid=(s // tm, n // tn),
        a_spec=pl.BlockSpec((tm, k), lambda i, j: (i, 0)), b_spec=b_spec,
        o_spec=pl.BlockSpec((tm, tn), lambda i, j: (i, j)),
        out_shape=jax.ShapeDtypeStruct((s, n), out_dtype), name=name, side=side,
        residual=residual, r_spec=pl.BlockSpec((tm, tn), lambda i, j: (i, j)) if residual is not None else None)


def _mm_nt(a, b, *, name, tn=None, tm=512, out_dtype=F32, side=None):
    s, k = a.shape
    tm = _row_tile(s, tm)
    n = b.shape[0]
    tn = n if tn is None else tn
    return _matmul(
        a, b, dims=NT, grid=(s // tm, n // tn),
        a_spec=pl.BlockSpec((tm, k), lambda i, j: (i, 0)),
        b_spec=pl.BlockSpec((tn, k), lambda i, j: (j, 0)),
        o_spec=pl.BlockSpec((tm, tn), lambda i, j: (i, j)),
        out_shape=jax.ShapeDtypeStruct((s, n), out_dtype), name=name, side=side)


def _mm_tn(a, b, *, name, tm, tn, tk=2048, shards=None, side=None):
    s, m = a.shape
    n = b.shape[1]
    tk = _row_tile(s, tk)
    if shards is None:
        o_spec = pl.BlockSpec((tm, tn), lambda i, j, kk: (i, j))
        out_shape = jax.ShapeDtypeStruct((m, n), F32)
    else:
        assert tm == m
        nc = n // shards
        npb = nc // tn
        o_spec = pl.BlockSpec((None, m, tn), lambda i, j, kk: (j // npb, 0, j % npb))
        out_shape = jax.ShapeDtypeStruct((shards, m, nc), F32)
    return _matmul(
        a, b, dims=TN, grid=(m // tm, n // tn, s // tk), acc_axis=2,
        a_spec=pl.BlockSpec((tk, tm), lambda i, j, kk: (kk, i)),
        b_spec=pl.BlockSpec((tk, tn), lambda i, j, kk: (kk, j)),
        o_spec=o_spec, out_shape=out_shape, name=name, side=side)


def _rstd(x):
    return lax.rsqrt(jnp.mean(x * x, axis=-1, keepdims=True) + EPS)


def _rms_fwd(x, gains, *, name, tr=256):
    s, d = x.shape
    tr = _row_tile(s, tr)
    ng = len(gains)

    def body(*refs):
        xv = refs[0][...]
        xh = xv * _rstd(xv)
        for t in range(ng):
            refs[1 + ng + t][...] = (xh * refs[1 + t][...]).astype(BF16)

    row = pl.BlockSpec((tr, d), lambda i: (i, 0))
    vec = pl.BlockSpec((1, d), lambda i: (0, 0))
    outs, _ = _call(body, grid=(s // tr,), in_specs=[row] + [vec] * ng, out_specs=[row] * ng,
                    out_shape=[jax.ShapeDtypeStruct((s, d), BF16)] * ng, args=(x, *gains), name=name)
    return outs


def _accumulate(i, ref, value):
    @pl.when(i == 0)
    def _():
        ref[...] = value

    @pl.when(i > 0)
    def _():
        ref[...] += value


def _mm_residual_norms(y, w, res, gains, *, name, tm=512, side=None):
    s, k = y.shape
    d = w.shape[1]
    tm = _row_tile(s, tm)
    ng = len(gains)

    def body(y_ref, w_ref, r_ref, *rest):
        g_refs, h_ref, n_refs = rest[:ng], rest[ng], rest[ng + 1:]
        h = r_ref[...] + jnp.dot(y_ref[...], w_ref[...], preferred_element_type=F32)
        h_ref[...] = h
        xh = h * _rstd(h)
        for t in range(ng):
            n_refs[t][...] = (xh * g_refs[t][...]).astype(BF16)

    row = pl.BlockSpec((tm, d), lambda i: (i, 0))
    vec = pl.BlockSpec((1, d), lambda i: (0, 0))
    return _call(
        body, grid=(s // tm,),
        in_specs=[pl.BlockSpec((tm, k), lambda i: (i, 0)), pl.BlockSpec((k, d), lambda i: (0, 0)), row] + [vec] * ng,
        out_specs=[row] * (1 + ng),
        out_shape=[jax.ShapeDtypeStruct((s, d), F32)] + [jax.ShapeDtypeStruct((s, d), BF16)] * ng,
        args=(y, w, res, *gains), name=name, side=side)


def _mm_residual_loss(y, w, res, tgt, gain, *, name, tm=512):
    s, k = y.shape
    d = w.shape[1]
    tm = _row_tile(s, tm)

    def body(y_ref, w_ref, r_ref, t_ref, g_ref, loss_ref, dh_ref, dhb_ref, dg_ref):
        i = pl.program_id(0)
        hv = r_ref[...] + jnp.dot(y_ref[...], w_ref[...], preferred_element_type=F32)
        g = g_ref[...]
        r = _rstd(hv)
        xh = hv * r
        diff = xh * g - t_ref[...]
        part = 0.5 / d * jnp.sum(jnp.sum(diff * diff, axis=-1, keepdims=True), axis=0, keepdims=True)
        dout = diff * (1.0 / d)
        a = dout * g
        dh = r * (a - xh * jnp.mean(a * xh, axis=-1, keepdims=True))
        dh_ref[...] = dh
        dhb_ref[...] = dh.astype(BF16)
        _accumulate(i, dg_ref, jnp.sum(dout * xh, axis=0, keepdims=True))
        _accumulate(i, loss_ref, jnp.broadcast_to(part, (8, 128)))

    row = pl.BlockSpec((tm, d), lambda i: (i, 0))
    vec = pl.BlockSpec((1, d), lambda i: (0, 0))
    outs, _ = _call(
        body, grid=(s // tm,),
        in_specs=[pl.BlockSpec((tm, k), lambda i: (i, 0)), pl.BlockSpec((k, d), lambda i: (0, 0)), row, row, vec],
        out_specs=[pl.BlockSpec((8, 128), lambda i: (0, 0)), row, row, vec],
        out_shape=[jax.ShapeDtypeStruct((8, 128), F32), jax.ShapeDtypeStruct((s, d), F32),
                   jax.ShapeDtypeStruct((s, d), BF16), jax.ShapeDtypeStruct((1, d), F32)],
        args=(y, w, res, tgt, gain), name=name)
    return outs


def _mm_nt_rms_bwd(terms, x, dres, *, name, tm, side=None):
    s, d = x.shape
    tm = _row_tile(s, tm)
    nt = len(terms)

    def body(*refs):
        a_refs, b_refs, g_refs = refs[0:3 * nt:3], refs[1:3 * nt:3], refs[2:3 * nt:3]
        x_ref, dres_ref = refs[3 * nt], refs[3 * nt + 1]
        dx_ref, dxb_ref = refs[3 * nt + 2], refs[3 * nt + 3]
        dg_refs = refs[3 * nt + 4:]
        i = pl.program_id(0)
        xv = x_ref[...]
        r = _rstd(xv)
        xh = xv * r
        acc = jnp.zeros_like(xv)
        for t in range(nt):
            b_ref = b_refs[t]
            if len(b_ref.shape) == 3:
                kc = b_ref.shape[2]
                dn = None
                for sh in range(b_ref.shape[0]):
                    part = lax.dot_general(a_refs[t][:, sh * kc:(sh + 1) * kc], b_ref[sh], NT, preferred_element_type=F32)
                    dn = part if dn is None else dn + part
            else:
                dn = lax.dot_general(a_refs[t][...], b_ref[...], NT, preferred_element_type=F32)
            acc = acc + dn * g_refs[t][...]
            _accumulate(i, dg_refs[t], jnp.sum(dn * xh, axis=0, keepdims=True))
        dx = dres_ref[...] + r * (acc - xh * jnp.mean(acc * xh, axis=-1, keepdims=True))
        dx_ref[...] = dx
        dxb_ref[...] = dx.astype(BF16)

    row = pl.BlockSpec((tm, d), lambda i: (i, 0))
    vec = pl.BlockSpec((1, d), lambda i: (0, 0))
    in_specs, args = [], []
    for a, b, g in terms:
        in_specs += [pl.BlockSpec((tm, a.shape[1]), lambda i: (i, 0)),
                     pl.BlockSpec(b.shape, (lambda i: (0, 0, 0)) if b.ndim == 3 else (lambda i: (0, 0))), vec]
        args += [a, b, g]
    return _call(
        body, grid=(s // tm,), in_specs=in_specs + [row, row], out_specs=[row, row] + [vec] * nt,
        out_shape=[jax.ShapeDtypeStruct((s, d), F32), jax.ShapeDtypeStruct((s, d), BF16)]
        + [jax.ShapeDtypeStruct((1, d), F32)] * nt,
        args=(*args, x, dres), name=name, side=side)


def _causal_mask(transposed=False):
    row = lax.broadcasted_iota(jnp.int32, (CHUNK, CHUNK), 0)
    col = lax.broadcasted_iota(jnp.int32, (CHUNK, CHUNK), 1)
    return col >= row if transposed else col <= row


def _silu_parts(g):
    sg = jax.nn.sigmoid(g)
    return g * sg, sg * (1.0 + g * (1.0 - sg))


def _gate_fwd(z, ln_g, ln_b, ws, bs_t, *, tr=256, side=None):
    s = z.shape[0]
    tr = _row_tile(s, tr)
    w = A_WIDTH

    def body(u_ref, v_ref, g_ref, lg_ref, lb_ref, ws_ref, bst_ref, y_ref):
        v = v_ref[...].astype(F32)
        mu = jnp.mean(v, axis=-1, keepdims=True)
        xc = v - mu
        rs = lax.rsqrt(jnp.mean(xc * xc, axis=-1, keepdims=True) + EPS)
        vln = (xc * rs * lg_ref[...] + lb_ref[...]).astype(BF16)
        mask = _causal_mask()
        for grp in range(A_GROUPS):
            cols = slice(grp * CHUNK, (grp + 1) * CHUNK)
            wsm = jnp.where(mask, ws_ref[grp], 0.0).astype(BF16)
            bcol = bst_ref[:, grp:grp + 1]
            for ci in range(tr // CHUNK):
                rows = slice(ci * CHUNK, (ci + 1) * CHUNK)
                sv = jnp.dot(wsm, vln[rows, cols], preferred_element_type=F32) + bcol
                gv = g_ref[rows, cols].astype(F32)
                y_ref[rows, cols] = (u_ref[rows, cols].astype(F32) * sv * (gv * jax.nn.sigmoid(gv))).astype(BF16)

    vec = pl.BlockSpec((1, w), lambda i: (0, 0))
    (y,), side_outs = _call(
        body, grid=(s // tr,),
        in_specs=[pl.BlockSpec((tr, w), lambda i: (i, 0)), pl.BlockSpec((tr, w), lambda i: (i, 1)),
                  pl.BlockSpec((tr, w), lambda i: (i, 2)), vec, vec,
                  pl.BlockSpec((A_GROUPS, CHUNK, CHUNK), lambda i: (0, 0, 0)),
                  pl.BlockSpec((CHUNK, A_GROUPS), lambda i: (0, 0))],
        out_specs=[pl.BlockSpec((tr, w), lambda i: (i, 0))],
        out_shape=[jax.ShapeDtypeStruct((s, w), BF16)], args=(z, z, z, ln_g, ln_b, ws, bs_t), name="gate_fwd",
        side=side)
    return y, side_outs


def _gate_bwd(z, dy, ln_g, ln_b, ws, ws_t, bs_t, *, tr=256, side=None):
    s = z.shape[0]
    tr = _row_tile(s, tr)
    w = A_WIDTH
    nsteps = s // tr

    def body(u_ref, v_ref, g_ref, dy_ref, lg_ref, lb_ref, ws_ref, wst_ref, bst_ref,
             dz_ref, dlg_ref, dlb_ref, dws_ref, dbst_ref, dvln_sc, dsv_sc):
        i = pl.program_id(0)

        @pl.when(i == 0)
        def _():
            dws_ref[...] = jnp.zeros_like(dws_ref)
            dsv_sc[...] = jnp.zeros_like(dsv_sc)

        v = v_ref[...].astype(F32)
        mu = jnp.mean(v, axis=-1, keepdims=True)
        xc = v - mu
        rs = lax.rsqrt(jnp.mean(xc * xc, axis=-1, keepdims=True) + EPS)
        xh = xc * rs
        lg = lg_ref[...]
        vln = (xh * lg + lb_ref[...]).astype(BF16)
        mask = _causal_mask()
        mask_t = _causal_mask(transposed=True)
        for grp in range(A_GROUPS):
            cols = slice(grp * CHUNK, (grp + 1) * CHUNK)
            wsm = jnp.where(mask, ws_ref[grp], 0.0).astype(BF16)
            wsm_t = jnp.where(mask_t, wst_ref[grp], 0.0).astype(BF16)
            bcol = bst_ref[:, grp:grp + 1]
            for ci in range(tr // CHUNK):
                rows = slice(ci * CHUNK, (ci + 1) * CHUNK)
                vb = vln[rows, cols]
                sv = jnp.dot(wsm, vb, preferred_element_type=F32) + bcol
                uv = u_ref[rows, cols].astype(F32)
                silu, dsilu = _silu_parts(g_ref[rows, cols].astype(F32))
                dyv = dy_ref[rows, cols].astype(F32)
                dyu = dyv * uv
                dz_ref[rows, cols] = (dyv * sv * silu).astype(BF16)
                dz_ref[rows, 2 * w + grp * CHUNK:2 * w + (grp + 1) * CHUNK] = (dyu * sv * dsilu).astype(BF16)
                dsv = dyu * silu
                dsvb = dsv.astype(BF16)
                dvln_sc[rows, cols] = jnp.dot(wsm_t, dsvb, preferred_element_type=F32)
                dws_ref[grp] += lax.dot_general(dsvb, vb, NT, preferred_element_type=F32)
                dsv_sc[grp] += dsv
        dvln = dvln_sc[...]
        dlg_t = jnp.sum(dvln * xh, axis=0, keepdims=True)
        dlb_t = jnp.sum(dvln, axis=0, keepdims=True)
        a = dvln * lg
        dv = rs * (a - jnp.mean(a, axis=-1, keepdims=True) - xh * jnp.mean(a * xh, axis=-1, keepdims=True))
        dz_ref[:, w:2 * w] = dv.astype(BF16)

        @pl.when(i == 0)
        def _():
            dlg_ref[...] = dlg_t
            dlb_ref[...] = dlb_t

        @pl.when(i > 0)
        def _():
            dlg_ref[...] += dlg_t
            dlb_ref[...] += dlb_t

        @pl.when(i == nsteps - 1)
        def _():
            for grp in range(A_GROUPS):
                dws_ref[grp] = jnp.where(mask, dws_ref[grp], 0.0)
                dbst_ref[:, grp:grp + 1] = jnp.sum(dsv_sc[grp], axis=-1, keepdims=True)

    vec = pl.BlockSpec((1, w), lambda i: (0, 0))
    wsspec = pl.BlockSpec((A_GROUPS, CHUNK, CHUNK), lambda i: (0, 0, 0))
    bsspec = pl.BlockSpec((CHUNK, A_GROUPS), lambda i: (0, 0))
    return _call(
        body, grid=(nsteps,),
        in_specs=[pl.BlockSpec((tr, w), lambda i: (i, 0)), pl.BlockSpec((tr, w), lambda i: (i, 1)),
                  pl.BlockSpec((tr, w), lambda i: (i, 2)), pl.BlockSpec((tr, w), lambda i: (i, 0)),
                  vec, vec, wsspec, wsspec, bsspec],
        out_specs=[pl.BlockSpec((tr, 3 * w), lambda i: (i, 0)), vec, vec, wsspec, bsspec],
        out_shape=[jax.ShapeDtypeStruct((s, 3 * w), BF16), jax.ShapeDtypeStruct((1, w), F32),
                   jax.ShapeDtypeStruct((1, w), F32), jax.ShapeDtypeStruct((A_GROUPS, CHUNK, CHUNK), F32),
                   jax.ShapeDtypeStruct((CHUNK, A_GROUPS), F32)],
        scratch=[pltpu.VMEM((tr, w), F32), pltpu.VMEM((A_GROUPS, CHUNK, CHUNK), F32)],
        args=(z, z, z, dy, ln_g, ln_b, ws, ws_t, bs_t), name="gate_bwd", side=side)


HEADS_PER_BLOCK = 128 // HEAD_DIM
BLOCKS_PER_KV = Q_PER_KV // HEADS_PER_BLOCK
SCALE = HEAD_DIM ** -0.5
LOG2_E = math.log2(math.e)


def _rope_tables(s):
    lane = jnp.arange(128)
    inv_freq = ROPE_THETA ** (-(2 * (lane % (HEAD_DIM // 2))).astype(F32) / HEAD_DIM)
    sign = jnp.where(lane % HEAD_DIM < HEAD_DIM // 2, -1.0, 1.0).astype(F32)
    ang = jnp.arange(s, dtype=F32)[:, None] * inv_freq[None, :]
    return jnp.cos(ang), jnp.sin(ang) * sign[None, :]


def _swap_halves(x):
    n = x.shape[-1]
    lane = lax.broadcasted_iota(jnp.int32, x.shape, x.ndim - 1)
    first = (lane % HEAD_DIM) < (HEAD_DIM // 2)
    return jnp.where(first, pltpu.roll(x, n - HEAD_DIM // 2, x.ndim - 1), pltpu.roll(x, HEAD_DIM // 2, x.ndim - 1))


def _left_half(rows):
    return lax.broadcasted_iota(jnp.int32, (rows, 128), 1) < HEAD_DIM


def _dup_heads(x):
    left = _left_half(x.shape[0])
    swapped = pltpu.roll(x, HEAD_DIM, 1)
    return jnp.concatenate([jnp.where(left, x, swapped), jnp.where(left, swapped, x)], axis=-1)


def _fold_heads(a):
    b0, b1 = a[:, :128], a[:, 128:]
    f0 = b0 + pltpu.roll(b0, HEAD_DIM, 1)
    f1 = b1 + pltpu.roll(b1, HEAD_DIM, 1)
    return jnp.where(_left_half(a.shape[0]), f0, f1)


def _kv_rope(kv, b_kv, cos, sin, *, tr=512):
    s = kv.shape[0]
    tr = _row_tile(s, tr)

    def body(kv_ref, b_ref, c_ref, s_ref, k_ref, v_ref):
        x = kv_ref[...] + b_ref[...]
        k = x[:, :KV_WIDTH]
        k_ref[...] = _dup_heads(k * c_ref[...] + _swap_halves(k) * s_ref[...]).astype(BF16)
        v_ref[...] = _dup_heads(x[:, KV_WIDTH:]).astype(BF16)

    tab = pl.BlockSpec((tr, KV_WIDTH), lambda i: (i, 0))
    wide = pl.BlockSpec((tr, 2 * KV_WIDTH), lambda i: (i, 0))
    outs, _ = _call(body, grid=(s // tr,),
                    in_specs=[wide, pl.BlockSpec((1, 2 * KV_WIDTH), lambda i: (0, 0)), tab, tab],
                    out_specs=[wide, wide], out_shape=[jax.ShapeDtypeStruct((s, 2 * KV_WIDTH), BF16)] * 2,
                    args=(kv, b_kv, cos, sin), name="kv_rope")
    return outs


def _kv_rope_bwd(dk2, dv2, cos, sin, *, tr=512):
    s = dk2.shape[0]
    tr = _row_tile(s, tr)

    def body(dk_ref, dv_ref, c_ref, s_ref, dkv_ref, db_ref):
        i = pl.program_id(0)
        d = _fold_heads(dk_ref[...])
        dk = d * c_ref[...] + _swap_halves(d * s_ref[...])
        dvv = _fold_heads(dv_ref[...])
        dkv_ref[:, :KV_WIDTH] = dk.astype(BF16)
        dkv_ref[:, KV_WIDTH:] = dvv.astype(BF16)
        sk = jnp.sum(dk, axis=0, keepdims=True)
        sv = jnp.sum(dvv, axis=0, keepdims=True)

        @pl.when(i == 0)
        def _():
            db_ref[:, :KV_WIDTH] = sk
            db_ref[:, KV_WIDTH:] = sv

        @pl.when(i > 0)
        def _():
            db_ref[:, :KV_WIDTH] += sk
            db_ref[:, KV_WIDTH:] += sv

    tab = pl.BlockSpec((tr, KV_WIDTH), lambda i: (i, 0))
    wide = pl.BlockSpec((tr, 2 * KV_WIDTH), lambda i: (i, 0))
    outs, _ = _call(body, grid=(s // tr,), in_specs=[wide, wide, tab, tab],
                    out_specs=[wide, pl.BlockSpec((1, 2 * KV_WIDTH), lambda i: (0, 0))],
                    out_shape=[jax.ShapeDtypeStruct((s, 2 * KV_WIDTH), BF16),
                               jax.ShapeDtypeStruct((1, 2 * KV_WIDTH), F32)],
                    args=(dk2, dv2, cos, sin), name="kv_rope_bwd")
    return outs


def _from_previous():
    cols = Q_PER_KV * CHUNK
    k = lax.broadcasted_iota(jnp.int32, (CHUNK, cols), 0)
    q = lax.broadcasted_iota(jnp.int32, (CHUNK, cols), 1) & (CHUNK - 1)
    return k > q


def _fold(x2, prev):
    return jnp.where(prev, x2[:CHUNK], x2[CHUNK:])


def _unfold(x, prev):
    zero = jnp.zeros_like(x)
    return jnp.concatenate([jnp.where(prev, x, zero), jnp.where(prev, zero, x)], axis=0)


def _stack_heads(blocks, left):
    parts = []
    for b in blocks:
        parts.append(jnp.where(left, b, jnp.zeros_like(b)))
        parts.append(jnp.where(left, jnp.zeros_like(b), b))
    return jnp.concatenate(parts, axis=0)


def _unstack_heads(xt):
    top = lax.broadcasted_iota(jnp.int32, (128, CHUNK), 0) < HEAD_DIM
    return [jnp.where(top, xt[:, (2 * b) * CHUNK:(2 * b + 1) * CHUNK], xt[:, (2 * b + 1) * CHUNK:(2 * b + 2) * CHUNK]).T
            for b in range(BLOCKS_PER_KV)]


def _sink_row(sk_ref, kvh):
    return jnp.concatenate([jnp.full((1, CHUNK), sk_ref[0, kvh * Q_PER_KV + r], F32) for r in range(Q_PER_KV)], axis=1)


def _stacked_probs(qs, kd, prev, sink, i):
    sc2 = lax.dot_general(kd, qs, NT, preferred_element_type=F32)
    no_previous = jnp.where(i > 0, 0.0, NEG_BIG)
    sc = jnp.where(prev, sc2[:CHUNK] + no_previous, sc2[CHUNK:])
    sink = sink * (1.0 / SCALE)
    m = jnp.maximum(jnp.max(sc, axis=0, keepdims=True), sink)
    p = jnp.exp2((sc - m) * (SCALE * LOG2_E))
    esink = jnp.exp2((sink - m) * (SCALE * LOG2_E))
    inv = 1.0 / (jnp.sum(p, axis=0, keepdims=True) + esink)
    return p * inv, esink * inv


def _lane_block(b):
    return slice(b * 128, (b + 1) * 128)


def _rope_blocks(zq_ref, bq_ref, cos, sin, kvh):
    out = []
    for b in range(BLOCKS_PER_KV):
        cols = _lane_block(kvh * BLOCKS_PER_KV + b)
        q = zq_ref[:, cols].astype(F32) + bq_ref[:, cols]
        out.append((q * cos + _swap_halves(q) * sin).astype(BF16))
    return out


def _attn_specs():
    qspec = pl.BlockSpec((CHUNK, B_WIDTH), lambda i: (i, 0))
    gspec = pl.BlockSpec((CHUNK, B_WIDTH), lambda i: (i, 1))
    prev = pl.BlockSpec((CHUNK, 2 * KV_WIDTH), lambda i: (jnp.maximum(i - 1, 0), 0))
    cur = pl.BlockSpec((CHUNK, 2 * KV_WIDTH), lambda i: (i, 0))
    tab = pl.BlockSpec((CHUNK, KV_WIDTH), lambda i: (i, 0))
    bq = pl.BlockSpec((1, B_WIDTH), lambda i: (0, 0))
    sinks = pl.BlockSpec(memory_space=pltpu.SMEM)
    return qspec, gspec, prev, cur, tab, bq, sinks


def _attn_fwd(zb, k2, v2, cos, sin, b_bq, sinks, *, side=None):
    s = zb.shape[0]

    def body(zq_ref, zg_ref, kp_ref, kc_ref, vp_ref, vc_ref, c_ref, s_ref, bq_ref, sk_ref, y_ref):
        i = pl.program_id(0)
        cos, sin = c_ref[...], s_ref[...]
        kcat = jnp.concatenate([kp_ref[...], kc_ref[...]], axis=0)
        vcat = jnp.concatenate([vp_ref[...], vc_ref[...]], axis=0)
        prev = _from_previous()
        left = _left_half(CHUNK)
        for kvh in range(N_KV_HEADS):
            qs = _stack_heads(_rope_blocks(zq_ref, bq_ref, cos, sin, kvh), left)
            p, _ = _stacked_probs(qs, kcat[:, _lane_block(kvh)], prev, _sink_row(sk_ref, kvh), i)
            ot = lax.dot_general(vcat[:, _lane_block(kvh)], _unfold(p, prev).astype(BF16), TN,
                                 preferred_element_type=F32)
            for b, ob in enumerate(_unstack_heads(ot)):
                cols = _lane_block(kvh * BLOCKS_PER_KV + b)
                gv = zg_ref[:, cols].astype(F32)
                y_ref[:, cols] = (ob * (gv * jax.nn.sigmoid(gv))).astype(BF16)

    qspec, gspec, prev, cur, tab, bq, sk = _attn_specs()
    (y,), side_outs = _call(body, grid=(s // CHUNK,), in_specs=[qspec, gspec, prev, cur, prev, cur, tab, tab, bq, sk],
                            out_specs=[qspec], out_shape=[jax.ShapeDtypeStruct((s, B_WIDTH), BF16)],
                            args=(zb, zb, k2, k2, v2, v2, cos, sin, b_bq, sinks), name="attn_fwd", side=side)
    return y, side_outs


def _attn_bwd(zb, dyb, k2, v2, cos, sin, b_bq, sinks):
    s = zb.shape[0]

    def body(zq_ref, zg_ref, dy_ref, kp_ref, kc_ref, vp_ref, vc_ref, c_ref, s_ref, bq_ref, sk_ref,
             dz_ref, dk_ref, dv_ref, dbq_ref, dsk_ref):
        i = pl.program_id(0)

        @pl.when(i == 0)
        def _():
            dk_ref[...] = jnp.zeros_like(dk_ref)
            dv_ref[...] = jnp.zeros_like(dv_ref)
            dbq_ref[...] = jnp.zeros_like(dbq_ref)
            dsk_ref[...] = jnp.zeros_like(dsk_ref)

        cos, sin = c_ref[...], s_ref[...]
        kcat = jnp.concatenate([kp_ref[...], kc_ref[...]], axis=0)
        vcat = jnp.concatenate([vp_ref[...], vc_ref[...]], axis=0)
        prev = _from_previous()
        left = _left_half(CHUNK)
        lane = lax.broadcasted_iota(jnp.int32, (1, 128), 1)
        dsk_row = jnp.zeros((1, 128), F32)
        cur_rows = pl.ds(pl.multiple_of(i * CHUNK, CHUNK), CHUNK)
        for kvh in range(N_KV_HEADS):
            kd, vd = kcat[:, _lane_block(kvh)], vcat[:, _lane_block(kvh)]
            qs = _stack_heads(_rope_blocks(zq_ref, bq_ref, cos, sin, kvh), left)
            p, psink = _stacked_probs(qs, kd, prev, _sink_row(sk_ref, kvh), i)
            pb = _unfold(p, prev).astype(BF16)
            ot = lax.dot_general(vd, pb, TN, preferred_element_type=F32)
            gates, dys = [], []
            for b in range(BLOCKS_PER_KV):
                cols = _lane_block(kvh * BLOCKS_PER_KV + b)
                gates.append(_silu_parts(zg_ref[:, cols].astype(F32)))
                dys.append(dy_ref[:, cols].astype(F32))
            dos = _stack_heads([(dyv * silu).astype(BF16) for dyv, (silu, _) in zip(dys, gates)], left)
            dp = _fold(lax.dot_general(vd, dos, NT, preferred_element_type=F32), prev)
            delta = jnp.sum(p * dp, axis=0, keepdims=True)
            ds = _unfold(p * (dp - delta) * SCALE, prev).astype(BF16)
            dqt = lax.dot_general(kd, ds, TN, preferred_element_type=F32)
            dk_part = jnp.dot(ds, qs, preferred_element_type=F32)
            dv_part = jnp.dot(pb, dos, preferred_element_type=F32)
            dk_ref[cur_rows, _lane_block(kvh)] += dk_part[CHUNK:]
            dv_ref[cur_rows, _lane_block(kvh)] += dv_part[CHUNK:]

            @pl.when(i > 0)
            def _(kvh=kvh, dk_part=dk_part, dv_part=dv_part):
                prev_rows = pl.ds(pl.multiple_of((i - 1) * CHUNK, CHUNK), CHUNK)
                dk_ref[prev_rows, _lane_block(kvh)] += dk_part[:CHUNK]
                dv_ref[prev_rows, _lane_block(kvh)] += dv_part[:CHUNK]

            sink_grad = psink * delta
            for r in range(Q_PER_KV):
                dsink = -jnp.sum(sink_grad[:, r * CHUNK:(r + 1) * CHUNK], axis=1, keepdims=True)
                dsk_row = dsk_row + jnp.where(lane == kvh * Q_PER_KV + r, dsink, 0.0)
            blocks = zip(_unstack_heads(ot), _unstack_heads(dqt), dys, gates)
            for b, (ob, dqr, dyv, (_, dsilu)) in enumerate(blocks):
                blk = kvh * BLOCKS_PER_KV + b
                dq = dqr * cos + _swap_halves(dqr * sin)
                dbq_ref[:, _lane_block(blk)] += jnp.sum(dq, axis=0, keepdims=True)
                dz_ref[:, _lane_block(blk)] = dq.astype(BF16)
                dz_ref[:, _lane_block(B_WIDTH // 128 + blk)] = (dyv * ob * dsilu).astype(BF16)
        dsk_ref[0:1, :] += dsk_row

    qspec, gspec, prev, cur, tab, bq, sk = _attn_specs()
    full = pl.BlockSpec((s, 2 * KV_WIDTH), lambda i: (0, 0))
    outs, _ = _call(
        body, grid=(s // CHUNK,),
        in_specs=[qspec, gspec, qspec, prev, cur, prev, cur, tab, tab, bq, sk],
        out_specs=[pl.BlockSpec((CHUNK, 2 * B_WIDTH), lambda i: (i, 0)), full, full, bq,
                   pl.BlockSpec((8, 128), lambda i: (0, 0))],
        out_shape=[jax.ShapeDtypeStruct((s, 2 * B_WIDTH), BF16), jax.ShapeDtypeStruct((s, 2 * KV_WIDTH), F32),
                   jax.ShapeDtypeStruct((s, 2 * KV_WIDTH), F32), jax.ShapeDtypeStruct((1, B_WIDTH), F32),
                   jax.ShapeDtypeStruct((8, 128), F32)],
        args=(zb, zb, dyb, k2, k2, v2, v2, cos, sin, b_bq, sinks), name="attn_bwd")
    return outs


def _place():
    x, y, c = lax.axis_index("x"), lax.axis_index("y"), lax.axis_index("c")
    return x, y, c, [(1 - x, y), (x, 1 - y), (1 - x, 1 - y)]


def _relations():
    return [(r >> 2 & 1, r >> 1 & 1, r & 1) for r in range(1, 8)]


def _gather_side(arrs):
    n = len(arrs)

    def copies(ins, outs, sems):
        send_ici, recv_ici, send_d2d, recv_d2d, local_sem = sems
        x, y, c, chips = _place()
        me = 2 * x + y

        def rows(a, half):
            hr = arrs[a].shape[0] // 2
            return pl.ds(half * hr, hr)

        def ici(a, j, src_chip, to):
            return pltpu.make_async_remote_copy(
                src_ref=ins[a].at[rows(a, c)], dst_ref=outs[a].at[src_chip, rows(a, c)],
                send_sem=send_ici.at[a, j], recv_sem=recv_ici.at[a, j], device_id=to, device_id_type=MESH)

        def d2d(a, j, chip, half):
            blk = outs[a].at[chip, rows(a, half)]
            return pltpu.make_async_remote_copy(
                src_ref=blk, dst_ref=blk, send_sem=send_d2d.at[a, j], recv_sem=recv_d2d.at[a, j],
                device_id=(x, y, 1 - c), device_id_type=MESH)

        local = [pltpu.make_async_copy(ins[a], outs[a].at[me], local_sem.at[a]) for a in range(n)]
        pairs = [(a, j, chip) for a in range(n) for j, chip in enumerate(chips)]
        return c, me, local, ici, d2d, pairs

    def start(ins, outs, sems):
        c, me, local, ici, _, pairs = copies(ins, outs, sems)
        for cp in local:
            cp.start()
        for a, j, chip in pairs:
            ici(a, j, me, (*chip, c)).start()

    def passing(ins, outs, sems):
        c, _, _, ici, d2d, pairs = copies(ins, outs, sems)
        for a, j, (px, py) in pairs:
            ici(a, j, 2 * px + py, (px, py, c)).wait_recv()
            d2d(a, j, 2 * px + py, c).start()

    def finish(ins, outs, sems):
        c, me, local, ici, d2d, pairs = copies(ins, outs, sems)
        for a, j, (px, py) in pairs:
            d2d(a, j, 2 * px + py, 1 - c).wait_recv()
        for a, j, (px, py) in pairs:
            ici(a, j, me, (px, py, c)).wait_send()
            d2d(a, j, 2 * px + py, c).wait_send()
        for cp in local:
            cp.wait()

    return _Side(arrs, [jax.ShapeDtypeStruct((N_CHIPS,) + a.shape, a.dtype) for a in arrs],
                 [pltpu.SemaphoreType.DMA((n, 3))] * 4 + [pltpu.SemaphoreType.DMA((n,))], start, finish,
                 passing=passing)


def _exchange_side(grads):
    n = len(grads)

    def copies(ins, outs, sems):
        send_sem, recv_sem = sems
        x, y, c, _ = _place()
        cps = []
        for a in range(n):
            hr = grads[a].shape[1] // 2
            cps.append(pltpu.make_async_remote_copy(
                src_ref=ins[a].at[:, pl.ds((1 - c) * hr, hr), :], dst_ref=outs[a],
                send_sem=send_sem.at[a], recv_sem=recv_sem.at[a], device_id=(x, y, 1 - c), device_id_type=MESH))
        return cps

    def start(ins, outs, sems):
        for cp in copies(ins, outs, sems):
            cp.start()

    def finish(ins, outs, sems):
        for cp in copies(ins, outs, sems):
            cp.wait()

    return _Side(grads, [jax.ShapeDtypeStruct((g.shape[0], g.shape[1] // 2, g.shape[2]), g.dtype) for g in grads],
                 [pltpu.SemaphoreType.DMA((n,))] * 2, start, finish)


def _scatter_side(chip_sums, small=None):
    n = len(chip_sums)
    arrs = list(chip_sums) + ([small] if small is not None else [])

    def copies(ins, outs, sems):
        x, y, c, chips = _place()
        cps = []
        for a in range(n):
            for j, (px, py) in enumerate(chips):
                cps.append(pltpu.make_async_remote_copy(
                    src_ref=ins[a].at[2 * px + py], dst_ref=outs[a].at[j],
                    send_sem=sems[0].at[a, j], recv_sem=sems[1].at[a, j], device_id=(px, py, c), device_id_type=MESH))
        if small is not None:
            for r, (fx, fy, fc) in enumerate(_relations(), start=1):
                px, py, pc = x ^ fx, y ^ fy, c ^ fc
                cps.append(pltpu.make_async_remote_copy(
                    src_ref=ins[n].at[4 * px + 2 * py + pc], dst_ref=outs[n].at[r],
                    send_sem=sems[2].at[r - 1], recv_sem=sems[3].at[r - 1], device_id=(px, py, pc),
                    device_id_type=MESH))
        return cps

    def start(ins, outs, sems):
        for cp in copies(ins, outs, sems):
            cp.start()

    def finish(ins, outs, sems):
        for cp in copies(ins, outs, sems):
            cp.wait()

    shapes = [jax.ShapeDtypeStruct((3,) + t.shape[1:], t.dtype) for t in chip_sums]
    sems = [pltpu.SemaphoreType.DMA((n, 3))] * 2
    if small is not None:
        shapes.append(jax.ShapeDtypeStruct(small.shape, small.dtype))
        sems += [pltpu.SemaphoreType.DMA((7,))] * 2
    return _Side(arrs, shapes, sems, start, finish)


def _small_scatter_side(small):
    def copies(ins, outs, sems):
        x, y, c, _ = _place()
        cps = []
        for r, (fx, fy, fc) in enumerate(_relations(), start=1):
            px, py, pc = x ^ fx, y ^ fy, c ^ fc
            cps.append(pltpu.make_async_remote_copy(
                src_ref=ins[0].at[4 * px + 2 * py + pc], dst_ref=outs[0].at[r],
                send_sem=sems[0].at[r - 1], recv_sem=sems[1].at[r - 1], device_id=(px, py, pc), device_id_type=MESH))
        return cps

    def start(ins, outs, sems):
        for cp in copies(ins, outs, sems):
            cp.start()

    def finish(ins, outs, sems):
        for cp in copies(ins, outs, sems):
            cp.wait()

    return _Side([small], [jax.ShapeDtypeStruct(small.shape, small.dtype)], [pltpu.SemaphoreType.DMA((7,))] * 2,
                 start, finish)


def _small_share_side(small):
    return _share_side([], small)


def _share_side(halves, small=None):
    n = len(halves)
    arrs = list(halves) + ([small] if small is not None else [])

    def copies(ins, outs, sems, mine):
        x, y, c, _ = _place()
        me = 4 * x + 2 * y + c
        cps = []
        for a in range(n):
            hr = halves[a].shape[0] // 2
            rows = pl.ds((c if mine else 1 - c) * hr, hr)
            cps.append(pltpu.make_async_remote_copy(
                src_ref=ins[a].at[rows], dst_ref=outs[a].at[rows],
                send_sem=sems[0].at[a], recv_sem=sems[1].at[a], device_id=(x, y, 1 - c), device_id_type=MESH))
        if small is not None:
            for r, (fx, fy, fc) in enumerate(_relations(), start=1):
                px, py, pc = x ^ fx, y ^ fy, c ^ fc
                seg = me if mine else 4 * px + 2 * py + pc
                cps.append(pltpu.make_async_remote_copy(
                    src_ref=ins[n].at[seg], dst_ref=outs[n].at[seg],
                    send_sem=sems[-2].at[r - 1], recv_sem=sems[-1].at[r - 1], device_id=(px, py, pc),
                    device_id_type=MESH))
        return cps

    def start(ins, outs, sems):
        for cp in copies(ins, outs, sems, True):
            cp.start()

    def finish(ins, outs, sems):
        for cp in copies(ins, outs, sems, False):
            cp.wait_recv()
        for cp in copies(ins, outs, sems, True):
            cp.wait_send()

    sems = ([pltpu.SemaphoreType.DMA((n,))] * 2 if n else []) + (
        [pltpu.SemaphoreType.DMA((7,))] * 2 if small is not None else [])
    return _Side(arrs, [jax.ShapeDtypeStruct(h.shape, h.dtype) for h in arrs], sems, start, finish,
                 aliases={i: i for i in range(len(arrs))})


GATHER_PIECES = [(0, 0), (0, 1), (1, 0), (2, 0), (1, 1), (2, 1), (3, 0), (3, 1)]


def _mm_gathering(a, shard, order, *, name, tm=1024):
    s, k = a.shape
    nc = shard.shape[1]
    tm = _row_tile(s, tm)
    tn = nc // 2
    hr = k // 2
    qr = hr // 2
    blocks = jnp.stack([order[src] * 2 + h for src, h in GATHER_PIECES]).astype(jnp.int32)

    def body(blocks_ref, a_ref, shard_ref, z_ref, full_ref, wbuf, send_ici, recv_ici, send_relay,
             recv_relay, send_d2d, recv_d2d, local_sem, load_sem):
        piece, i = pl.program_id(0), pl.program_id(1)
        x, y, c, chips = _place()
        me = 2 * x + y
        nbrs = chips[:2]
        chip_of = [2 * px + py for px, py in chips]

        def quarter(q):
            return pl.ds(c * hr + q * qr, qr)

        def sibling_quarter(q):
            return pl.ds((1 - c) * hr + q * qr, qr)

        def whole(half):
            return pl.ds(half * hr, hr)

        def cols(h):
            return pl.ds(h * tn, tn)

        def direct(j, src_chip, h):
            return pltpu.make_async_remote_copy(
                src_ref=shard_ref.at[whole(c), cols(h)], dst_ref=full_ref.at[src_chip, whole(c), cols(h)],
                send_sem=send_ici.at[j, h], recv_sem=recv_ici.at[j, h], device_id=(*nbrs[j], c), device_id_type=MESH)

        def relay(j, src_chip, h):
            blk = full_ref.at[src_chip, quarter(j), cols(h)]
            return pltpu.make_async_remote_copy(
                src_ref=blk, dst_ref=blk, send_sem=send_relay.at[j, h], recv_sem=recv_relay.at[j, h],
                device_id=(*nbrs[1 - j], c), device_id_type=MESH)

        def d2d(j, chip, rows, h):
            blk = full_ref.at[chip, rows, cols(h)]
            return pltpu.make_async_remote_copy(
                src_ref=blk, dst_ref=blk, send_sem=send_d2d.at[j, h], recv_sem=recv_d2d.at[j, h],
                device_id=(x, y, 1 - c), device_id_type=MESH)

        def load(p):
            src, h = GATHER_PIECES[p]
            where = shard_ref if src == 0 else full_ref.at[chip_of[src - 1]]
            return pltpu.make_async_copy(where.at[:, cols(h)], wbuf.at[p % 2], load_sem.at[p % 2])

        local = pltpu.make_async_copy(shard_ref, full_ref.at[me], local_sem)

        def arrived(p):
            src, h = GATHER_PIECES[p]
            if src in (1, 2):
                j = src - 1
                direct(j, chip_of[j], h).wait_recv()
                relay(j, chip_of[j], h).start()
                d2d(j, chip_of[j], whole(c), h).start()
            elif src == 3:
                for j in range(2):
                    relay(1 - j, chip_of[2], h).wait_recv()
                    d2d(2 + j, chip_of[2], quarter(1 - j), h).start()

        def fetch(p):
            src, h = GATHER_PIECES[p]
            if src in (1, 2):
                d2d(src - 1, chip_of[src - 1], whole(1 - c), h).wait_recv()
            elif src == 3:
                for j in range(2):
                    d2d(2 + j, chip_of[2], sibling_quarter(1 - j), h).wait_recv()
            load(p).start()

        n_i = s // tm
        for p in range(len(GATHER_PIECES)):
            @pl.when(jnp.logical_and(piece == p, i == 0))
            def _(p=p):
                if p == 0:
                    local.start()
                    for hh in range(2):
                        for j in range(2):
                            direct(j, me, hh).start()
                    load(0).start()
                load(p).wait()

        z_ref[...] = jnp.dot(a_ref[...], wbuf[piece % 2], preferred_element_type=F32).astype(z_ref.dtype)

        for p in range(len(GATHER_PIECES) - 1):
            @pl.when(jnp.logical_and(piece == p, i == min(1, n_i - 1)))
            def _(p=p):
                arrived(p + 1)

            @pl.when(jnp.logical_and(piece == p, i == min(2, n_i - 1)))
            def _(p=p):
                fetch(p + 1)

        last = jnp.logical_and(piece == len(GATHER_PIECES) - 1, i == n_i - 1)

        @pl.when(last)
        def _():
            for h in range(2):
                for j in range(2):
                    direct(j, me, h).wait_send()
                    relay(j, chip_of[j], h).wait_send()
                    d2d(j, chip_of[j], whole(c), h).wait_send()
                    d2d(2 + j, chip_of[2], quarter(1 - j), h).wait_send()
            local.wait()

    return pl.pallas_call(
        body,
        grid_spec=pltpu.PrefetchScalarGridSpec(
            num_scalar_prefetch=1, grid=(len(GATHER_PIECES), s // tm),
            in_specs=[pl.BlockSpec((tm, k), lambda p, i, blocks: (i, 0)), HBM],
            out_specs=[pl.BlockSpec((tm, tn), lambda p, i, blocks: (i, blocks[p])), HBM],
            scratch_shapes=[pltpu.VMEM((2, k, tn), BF16)] + [pltpu.SemaphoreType.DMA((2, 2))] * 4
            + [pltpu.SemaphoreType.DMA((4, 2))] * 2 + [pltpu.SemaphoreType.DMA, pltpu.SemaphoreType.DMA((2,))]),
        out_shape=[jax.ShapeDtypeStruct((s, N_CHIPS * nc), BF16), jax.ShapeDtypeStruct((N_CHIPS, k, nc), BF16)],
        name=name, compiler_params=_cparams(),
    )(blocks, a, shard)


def _mm_tn_exchanging(a, b, *, name, shards, tk=2048, side=None):
    s, m = a.shape
    nc = b.shape[1] // shards
    tk = _row_tile(s, tk)
    nk = s // tk
    hm = m // 2

    def body(a_ref, b_ref, part_ref, sib_ref, acc, keep_sem, send_sem, recv_sem):
        j, kk = pl.program_id(0), pl.program_id(1)
        x, y, c, _ = _place()

        def keep(jj, slot):
            mine = pl.ds(c * hm, hm)
            return pltpu.make_async_copy(acc.at[slot, mine], part_ref.at[jj, mine], keep_sem.at[slot])

        def give(jj, slot):
            return pltpu.make_async_remote_copy(
                src_ref=acc.at[slot, pl.ds((1 - c) * hm, hm)], dst_ref=sib_ref.at[jj],
                send_sem=send_sem.at[slot], recv_sem=recv_sem.at[jj], device_id=(x, y, 1 - c), device_id_type=MESH)

        part = lax.dot_general(a_ref[...], b_ref[...], TN, preferred_element_type=F32)
        for slot in range(2):
            @pl.when(j % 2 == slot)
            def _(slot=slot):
                @pl.when(jnp.logical_and(kk == 0, j >= 2))
                def _():
                    keep(j - 2, slot).wait()
                    give(j - 2, slot).wait_send()

                @pl.when(kk == 0)
                def _():
                    acc[slot] = part

                @pl.when(kk > 0)
                def _():
                    acc[slot] += part

                @pl.when(kk == nk - 1)
                def _():
                    keep(j, slot).start()
                    give(j, slot).start()

        @pl.when(jnp.logical_and(j == shards - 1, kk == nk - 1))
        def _():
            for jj in range(shards - 2, shards):
                keep(jj, jj % 2).wait()
                give(jj, jj % 2).wait_send()
            for jj in range(shards):
                give(jj, jj % 2).wait_recv()

    assert shards >= 2
    return _call(
        body, grid=(shards, nk),
        in_specs=[pl.BlockSpec((tk, m), lambda j, kk: (kk, 0)), pl.BlockSpec((tk, nc), lambda j, kk: (kk, j))],
        out_specs=[HBM, HBM],
        out_shape=[jax.ShapeDtypeStruct((shards, m, nc), F32), jax.ShapeDtypeStruct((shards, hm, nc), F32)],
        scratch=[pltpu.VMEM((2, m, nc), F32), pltpu.SemaphoreType.DMA((2,)), pltpu.SemaphoreType.DMA((2,)),
                 pltpu.SemaphoreType.DMA((shards,))],
        args=(a, b), name=name, side=side)


def _mm_tn_reducing(a, b, order, *, name, tk=1024, side=None):
    s, m = a.shape
    shards = N_CHIPS
    nc = b.shape[1] // shards
    tk = _row_tile(s, tk)
    nk = s // tk
    hm = m // 2

    def body(order_ref, a_ref, b_ref, own_ref, arrived_ref, acc, sib, stage, give_send, give_recv, scat_send, scat_recv):
        t, kk = pl.program_id(0), pl.program_id(1)
        x, y, c, chips = _place()
        mine, other = pl.ds(c * hm, hm), pl.ds((1 - c) * hm, hm)

        def give(slot):
            return pltpu.make_async_remote_copy(
                src_ref=acc.at[slot, other], dst_ref=sib.at[slot], send_sem=give_send.at[slot],
                recv_sem=give_recv.at[slot], device_id=(x, y, 1 - c), device_id_type=MESH)

        def scatter(j):
            return pltpu.make_async_remote_copy(
                src_ref=stage.at[j % 2], dst_ref=arrived_ref.at[j], send_sem=scat_send.at[j],
                recv_sem=scat_recv.at[j], device_id=(*chips[j], c), device_id_type=MESH)

        part = lax.dot_general(a_ref[...], b_ref[...], TN, preferred_element_type=F32)
        for tt in range(shards):
            slot = tt % 2

            @pl.when(t == tt)
            def _(tt=tt, slot=slot):
                @pl.when(kk == 0)
                def _():
                    if tt >= 2:
                        give(slot).wait_send()
                    acc[slot] = part

                @pl.when(kk > 0)
                def _():
                    acc[slot] += part

                @pl.when(kk == nk - 1)
                def _():
                    give(slot).start()
                    give(slot).wait_recv()
                    total = acc[slot, mine] + sib[slot]
                    if tt < shards - 1:
                        if tt >= 2:
                            scatter(tt - 2).wait_send()
                        stage[slot] = total.astype(BF16)
                        scatter(tt).start()
                    else:
                        own_ref[...] = total

        @pl.when(jnp.logical_and(t == shards - 1, kk == nk - 1))
        def _():
            for slot in range(2):
                give(slot).wait_send()
            for j in range(shards - 1):
                scatter(j).wait_recv()
            for j in range(max(0, shards - 3), shards - 1):
                scatter(j).wait_send()

    return _call(
        body, grid=(shards, nk), prefetch=(order,),
        in_specs=[pl.BlockSpec((tk, m), lambda t, kk, order: (kk, 0)),
                  pl.BlockSpec((tk, nc), lambda t, kk, order: (kk, order[t]))],
        out_specs=[pl.BlockSpec((hm, nc), lambda t, kk, order: (0, 0)), HBM],
        out_shape=[jax.ShapeDtypeStruct((hm, nc), F32), jax.ShapeDtypeStruct((shards - 1, hm, nc), BF16)],
        scratch=[pltpu.VMEM((2, m, nc), F32), pltpu.VMEM((2, hm, nc), F32), pltpu.VMEM((2, hm, nc), BF16),
                 pltpu.SemaphoreType.DMA((2,)), pltpu.SemaphoreType.DMA((2,)),
                 pltpu.SemaphoreType.DMA((shards - 1,)), pltpu.SemaphoreType.DMA((shards - 1,))],
        args=(a, b), name=name, side=side)


def _col_tile(cols):
    return cols if cols <= 2048 else 512


def _add_sibling(grad, recv, core, *, name):
    k, r, c = grad.shape
    hr = r // 2
    tr = min(hr, 256)
    tc = _col_tile(c)
    nrb = hr // tr

    def body(core_ref, g_ref, r_ref, o_ref):
        o_ref[...] = (g_ref[...] + r_ref[...]).astype(BF16)

    return pl.pallas_call(
        body,
        grid_spec=pltpu.PrefetchScalarGridSpec(
            num_scalar_prefetch=1, grid=(k, nrb, c // tc),
            in_specs=[pl.BlockSpec((None, tr, tc), lambda kk, i, j, core: (kk, core[0] * nrb + i, j)),
                      pl.BlockSpec((None, tr, tc), lambda kk, i, j, core: (kk, i, j))],
            out_specs=pl.BlockSpec((None, tr, tc), lambda kk, i, j, core: (kk, i, j))),
        out_shape=jax.ShapeDtypeStruct((k, hr, c), BF16), name=name, compiler_params=_cparams(),
    )(core, grad, recv)


def _sum_chips(grad, from_sibling, recv, place, *, name):
    _, hr, c = from_sibling.shape
    tr = min(hr, 256)
    tc = _col_tile(c)
    nrb = hr // tr

    def body(place_ref, g_ref, s_ref, r0_ref, r1_ref, r2_ref, o_ref):
        own = g_ref[...] + s_ref[...]
        o_ref[...] = ((own + r0_ref[...].astype(F32)) + r1_ref[...].astype(F32)) + r2_ref[...].astype(F32)

    def rspec(j):
        return pl.BlockSpec((None, tr, tc), lambda i, jj, place: (j, i, jj))

    return pl.pallas_call(
        body,
        grid_spec=pltpu.PrefetchScalarGridSpec(
            num_scalar_prefetch=1, grid=(nrb, c // tc),
            in_specs=[pl.BlockSpec((None, tr, tc), lambda i, jj, place: (place[0], place[1] * nrb + i, jj)),
                      pl.BlockSpec((None, tr, tc), lambda i, jj, place: (place[0], i, jj)),
                      rspec(0), rspec(1), rspec(2)],
            out_specs=pl.BlockSpec((tr, tc), lambda i, jj, place: (place[1] * nrb + i, jj))),
        out_shape=jax.ShapeDtypeStruct((2 * hr, c), F32), name=name, compiler_params=_cparams(),
    )(place, grad, from_sibling, recv, recv, recv)


def _sum_arrived(own, arrived, place, *, name):
    hr, c = own.shape
    tr = min(hr, 256)
    tc = _col_tile(c)
    nrb = hr // tr

    def body(place_ref, own_ref, r0_ref, r1_ref, r2_ref, o_ref):
        o_ref[...] = ((own_ref[...] + r0_ref[...].astype(F32)) + r1_ref[...].astype(F32)) + r2_ref[...].astype(F32)

    def rspec(j):
        return pl.BlockSpec((None, tr, tc), lambda i, jj, place: (j, i, jj))

    return pl.pallas_call(
        body,
        grid_spec=pltpu.PrefetchScalarGridSpec(
            num_scalar_prefetch=1, grid=(nrb, c // tc),
            in_specs=[pl.BlockSpec((tr, tc), lambda i, jj, place: (i, jj)), rspec(0), rspec(1), rspec(2)],
            out_specs=pl.BlockSpec((tr, tc), lambda i, jj, place: (place[1] * nrb + i, jj))),
        out_shape=jax.ShapeDtypeStruct((2 * hr, c), F32), name=name, compiler_params=_cparams(),
    )(place, own, arrived, arrived, arrived)


def _sum_small(small, recv, place):
    _, sr, _ = small.shape

    def body(place_ref, own_ref, r_ref, o_ref):
        acc = own_ref[...]
        for r in range(1, 8):
            acc = acc + r_ref[r]
        o_ref[...] = acc

    return pl.pallas_call(
        body,
        grid_spec=pltpu.PrefetchScalarGridSpec(
            num_scalar_prefetch=1, grid=(1,),
            in_specs=[pl.BlockSpec((None, sr, 128), lambda i, place: (place[2], 0, 0)),
                      pl.BlockSpec((8, sr, 128), lambda i, place: (0, 0, 0))],
            out_specs=pl.BlockSpec((None, sr, 128), lambda i, place: (place[2], 0, 0))),
        out_shape=jax.ShapeDtypeStruct(small.shape, F32), name="sum_small", compiler_params=_cparams(),
    )(place, small, recv)


def _spread_side(vec):
    def copies(ins, outs, sems):
        x, y, c, _ = _place()
        return [pltpu.make_async_remote_copy(
            src_ref=ins[0], dst_ref=outs[0].at[r], send_sem=sems[0].at[r - 1], recv_sem=sems[1].at[r - 1],
            device_id=(x ^ fx, y ^ fy, c ^ fc), device_id_type=MESH)
            for r, (fx, fy, fc) in enumerate(_relations(), start=1)]

    def start(ins, outs, sems):
        for cp in copies(ins, outs, sems):
            cp.start()

    def finish(ins, outs, sems):
        for cp in copies(ins, outs, sems):
            cp.wait()

    return _Side([vec], [jax.ShapeDtypeStruct((8,) + vec.shape, vec.dtype)], [pltpu.SemaphoreType.DMA((7,))] * 2,
                 start, finish)


def _sum_in_device_order(own, spread, place):
    def body(place_ref, own_ref, r_ref, o_ref):
        me = place_ref[2]
        acc = jnp.zeros_like(own_ref[...])
        for d in range(8):
            slot = jnp.where(me == d, 1, me ^ d)
            acc = acc + jnp.where(me == d, own_ref[...], r_ref[slot])
        o_ref[...] = acc

    return pl.pallas_call(
        body,
        grid_spec=pltpu.PrefetchScalarGridSpec(
            num_scalar_prefetch=1, grid=(1,),
            in_specs=[pl.BlockSpec(own.shape, lambda i, place: (0, 0)),
                      pl.BlockSpec(spread.shape, lambda i, place: (0, 0, 0))],
            out_specs=pl.BlockSpec(own.shape, lambda i, place: (0, 0))),
        out_shape=jax.ShapeDtypeStruct(own.shape, F32), name="sum_in_device_order", compiler_params=_cparams(),
    )(place, own, spread)


def _adamw(w, g, m, v, *, name):
    r, c = w.shape
    tr = 256 if r % 256 == 0 else r
    tc = _col_tile(c)
    bc1 = 1.0 - ADAM_B1 ** ADAM_STEP
    bc2 = 1.0 - ADAM_B2 ** ADAM_STEP

    def body(w_ref, g_ref, m_ref, v_ref, d_ref, nm_ref, nv_ref):
        gv = g_ref[...]
        nm = ADAM_B1 * m_ref[...] + (1.0 - ADAM_B1) * gv
        nv = ADAM_B2 * v_ref[...] + (1.0 - ADAM_B2) * (gv * gv)
        d_ref[...] = -ADAM_LR * ((nm / bc1) / (jnp.sqrt(nv / bc2) + ADAM_EPS) + ADAM_WD * w_ref[...])
        nm_ref[...] = nm
        nv_ref[...] = nv

    spec = pl.BlockSpec((tr, tc), lambda i, j: (i, j))
    outs, _ = _call(body, grid=(r // tr, c // tc), in_specs=[spec] * 4, out_specs=[spec] * 3,
                    out_shape=[jax.ShapeDtypeStruct((r, c), F32)] * 3, args=(w, g, m, v), name=name)
    return outs


SMALL_ORDER = ["a_ws", "a_bs", "a_norm_g", "a_ln_g", "a_ln_b", "kv_norm_g", "b_kv", "b_norm_g", "b_bq",
               "b_sinks", "final_norm_g"]
SHARDED_SMALL = {"a_norm_g", "a_ln_g", "a_ln_b"}
PACK_TILE = 8 * 128


def _rows128(a):
    flat = a.reshape(-1)
    return jnp.pad(flat, (0, (-flat.shape[0]) % PACK_TILE)).reshape(-1, 128)


def _pack_rows(parts, multiple):
    rows = [_rows128(p) for p in parts]
    total = sum(r.shape[0] for r in rows)
    pad = (-total) % multiple
    if pad:
        rows.append(jnp.zeros((pad, 128), rows[0].dtype))
    return jnp.concatenate(rows, axis=0)


def _unpack_rows(packed, shapes):
    out, row = [], 0
    for shp in shapes:
        size = math.prod(shp)
        nrow = -(-size // PACK_TILE) * 8
        out.append(packed[row:row + nrow].reshape(-1)[:size].reshape(shp))
        row += nrow
    return out


WEIGHTS = ["a_norm_g", "a_w_in", "a_ln_g", "a_ln_b", "a_ws", "a_bs", "a_w_out", "kv_norm_g", "w_kv", "b_kv",
           "b_norm_g", "b_w_in", "b_bq", "b_sinks", "b_w_out", "final_norm_g"]
BIG = ["a_w_in", "a_w_out", "w_kv", "b_w_in", "b_w_out"]


class _Reduction:
    def __init__(self, names, partials, core, place, small=None):
        self.names, self.partials, self.core, self.place, self.small = names, partials, core, place, small

    def exchange_side(self):
        return _exchange_side(self.partials)

    def took_exchange(self, from_sibling):
        self.from_sibling = from_sibling
        self.chip_sums = [_add_sibling(g, r, self.core, name="add_sibling_" + n)
                          for g, r, n in zip(self.partials, from_sibling, self.names)]

    def scatter_side(self):
        return _scatter_side(self.chip_sums, self.small)

    def took_scatter(self, arrived):
        big = arrived[:len(self.names)]
        self.halves = [_sum_chips(g, fs, r, self.place, name="sum_chips_" + n)
                       for g, fs, r, n in zip(self.partials, self.from_sibling, big, self.names)]
        self.small_mine = _sum_small(self.small, arrived[-1], self.place) if self.small is not None else None

    def share_side(self):
        return _share_side(self.halves, self.small_mine)

    def took_share(self, shared):
        self.grads = dict(zip(self.names, shared[:len(self.names)]))
        self.small_full = shared[-1] if self.small is not None else None


def _step(x, loss_target, p, m, v):
    xi, yi, ci = lax.axis_index("x"), lax.axis_index("y"), lax.axis_index("c")
    chip = 2 * xi + yi
    device = 4 * xi + 2 * yi + ci
    core = jnp.reshape(ci, (1,)).astype(jnp.int32)
    place = jnp.stack([chip, ci, device]).astype(jnp.int32)
    x, tgt = x[0], loss_target[0]
    s = x.shape[0]
    cos, sin = _rope_tables(s)

    shard2d = {n: p[n].reshape(p[n].shape[-2:]) for n in BIG}
    shard_bf = {n: shard2d[n].astype(BF16) for n in BIG}
    ws = p["a_ws"][0]
    ws_t = jnp.swapaxes(ws, 1, 2)
    bs_t = p["a_bs"][0].T
    kv_norm_g, b_kv = p["kv_norm_g"].reshape(1, -1), p["b_kv"].reshape(1, -1)
    final_norm_g = p["final_norm_g"].reshape(1, -1)

    vec_shapes = [p[n].shape for n in ("a_norm_g", "a_ln_g", "a_ln_b")]
    vec_pack = _pack_rows([p["a_norm_g"], p["a_ln_g"], p["a_ln_b"]], 16)
    (vec_all,) = _comm_call(_gather_side([vec_pack]), "gather_vectors")
    vecs = [_unpack_rows(vec_all[k], vec_shapes) for k in range(N_CHIPS)]
    a_norm_g, a_ln_g, a_ln_b = (jnp.concatenate([vk[t] for vk in vecs], axis=-1) for t in range(3))

    (n_a,) = _rms_fwd(x, [a_norm_g], name="rms_a")
    order = jnp.stack([chip, 2 * (1 - xi) + yi, 2 * xi + (1 - yi), 2 * (1 - xi) + (1 - yi)]).astype(jnp.int32)
    z, a_w_in = _mm_gathering(n_a, shard_bf["a_w_in"], order, name="mm_a_in")
    y, (a_w_out,) = _gate_fwd(z, a_ln_g, a_ln_b, ws, bs_t, side=_gather_side([shard_bf["a_w_out"]]))
    a_w_out = a_w_out.reshape(A_WIDTH, D_MODEL)
    (h1, n_kv, n_b), (w_kv, b_w_in) = _mm_residual_norms(
        y, a_w_out, x, [kv_norm_g, p["b_norm_g"]], name="mm_a_out",
        side=_gather_side([shard_bf["w_kv"], shard_bf["b_w_in"]]))
    w_kv = w_kv.reshape(D_MODEL, 2 * KV_WIDTH)
    kv = _mm_nn(n_kv, w_kv, name="mm_kv", tn=2 * KV_WIDTH)
    kr, vv = _kv_rope(kv, b_kv, cos, sin)
    zb = _mm_nn(n_b, b_w_in, name="mm_b_in", tn=512, tm=1024, out_dtype=BF16)
    yb, (b_w_out,) = _attn_fwd(zb, kr, vv, cos, sin, p["b_bq"], p["b_sinks"], side=_gather_side([shard_bf["b_w_out"]]))
    b_w_out = b_w_out.reshape(B_WIDTH, D_MODEL)
    loss_blk, dh2, dh2b, d_final_g = _mm_residual_loss(yb, b_w_out, h1, tgt, final_norm_g, name="mm_b_out")

    d_b_w_out = _mm_tn(yb, dh2b, name="mm_d_b_w_out", tm=B_WIDTH, tn=D_MODEL)
    red_bo = _Reduction(["b_w_out"], [d_b_w_out.reshape(N_CHIPS, B_WIDTH // N_CHIPS, D_MODEL)], core, place)
    dyb, got = _mm_nt(dh2b, b_w_out, name="mm_dyb", out_dtype=BF16, side=red_bo.exchange_side())
    red_bo.took_exchange(got)
    dzb, dk_rot, dv, d_bq, d_sinks = _attn_bwd(zb, dyb, kr, vv, cos, sin, p["b_bq"], p["b_sinks"])
    dkv, d_b_kv = _kv_rope_bwd(dk_rot, dv, cos, sin)
    d_b_w_in, got = _mm_tn(n_b, dzb, name="mm_d_b_w_in", tm=D_MODEL, tn=512, shards=N_CHIPS,
                           side=red_bo.scatter_side())
    red_bo.took_scatter(got)
    d_w_kv, got = _mm_tn(n_kv, dkv, name="mm_d_w_kv", tm=D_MODEL, tn=2 * KV_WIDTH, side=red_bo.share_side())
    red_bo.took_share(got)
    red_bi = _Reduction(["b_w_in", "w_kv"], [d_b_w_in, d_w_kv.reshape(N_CHIPS, D_MODEL // N_CHIPS, 2 * KV_WIDTH)],
                        core, place)
    (dh1, dh1b, d_kv_g, d_b_g), got = _mm_nt_rms_bwd(
        [(dkv, w_kv, kv_norm_g), (dzb, b_w_in, p["b_norm_g"])], h1, dh2, name="mm_dn_b", tm=512,
        side=red_bi.exchange_side())
    red_bi.took_exchange(got)

    d_a_w_out, got = _mm_tn(y, dh1b, name="mm_d_a_w_out", tm=1024, tn=D_MODEL, side=red_bi.scatter_side())
    red_bi.took_scatter(got)
    red_ao = _Reduction(["a_w_out"], [d_a_w_out.reshape(N_CHIPS, A_WIDTH // N_CHIPS, D_MODEL)], core, place)
    sides = [red_ao.exchange_side(), red_bi.share_side()]
    dy, got = _mm_nt(dh1b, a_w_out, name="mm_dy", tn=1024, out_dtype=BF16, side=_join(sides))
    got = _split(got, sides)
    red_ao.took_exchange(got[0])
    red_bi.took_share(got[1])
    (dz, d_ln_g, d_ln_b, d_ws, d_bs_t), got = _gate_bwd(z, dy, a_ln_g, a_ln_b, ws, ws_t, bs_t,
                                                        side=red_ao.scatter_side())
    red_ao.took_scatter(got)
    small = {
        "a_ws": d_ws, "a_bs": d_bs_t.T, "a_ln_g": d_ln_g, "a_ln_b": d_ln_b,
        "kv_norm_g": d_kv_g, "b_kv": d_b_kv, "b_norm_g": d_b_g, "b_bq": d_bq,
        "b_sinks": d_sinks[0:1, :N_Q_HEADS], "final_norm_g": d_final_g,
    }
    packed = [n for n in SMALL_ORDER if n != "a_norm_g"]
    small_shapes = [small[n].shape for n in packed] + [(1, 1)]
    small_pack = _pack_rows([small[n] for n in packed] + [loss_blk[0:1, 0:1]], 64)
    seg = small_pack.shape[0] // 8
    small_pack = small_pack.reshape(8, seg, 128)
    sides = [red_ao.share_side(), _small_scatter_side(small_pack)]
    own_last = jnp.stack([order[1], order[2], order[3], order[0]])
    (own_sum, arrived), got = _mm_tn_reducing(n_a, dz, own_last, name="mm_d_a_w_in", side=_join(sides))
    got = _split(got, sides)
    red_ao.took_share(got[0])
    small_mine = _sum_small(small_pack, got[1][0], place)
    half_ai = _sum_arrived(own_sum, arrived, place, name="sum_chips_a_w_in")
    (dx, _, d_a_g), _ = _mm_nt_rms_bwd([(dz, a_w_in, a_norm_g)], x, dh1, name="mm_dn_a", tm=256)
    d_a_g = _rows128(d_a_g)
    sides = [_share_side([half_ai], small_mine), _spread_side(d_a_g)]
    got = _split(_comm_call(_join(sides), "share_last"), sides)
    grad_ai, small_all = got[0]
    small_full = dict(zip(packed + ["loss"], _unpack_rows(small_all.reshape(8 * seg, 128), small_shapes)))
    small_full["a_norm_g"] = _sum_in_device_order(d_a_g, got[1][0], place).reshape(1, -1)
    loss = small_full["loss"].reshape(())

    grad_big = {**red_bo.grads, **red_bi.grads, **red_ao.grads, "a_w_in": grad_ai}
    grads = {}
    for n in SMALL_ORDER:
        gfull = small_full[n]
        if n in SHARDED_SMALL:
            width = p[n].shape[-1]
            gfull = lax.dynamic_slice_in_dim(gfull, chip * width, width, axis=-1)
        grads[n] = gfull.reshape(p[n].shape)
    for n in BIG:
        grads[n] = grad_big[n].reshape(p[n].shape)

    delta, new_m, new_v = {}, {}, {}
    for n in BIG:
        d, nm, nv = _adamw(shard2d[n], grad_big[n], m[n].reshape(shard2d[n].shape), v[n].reshape(shard2d[n].shape),
                           name="adamw_" + n)
        delta[n], new_m[n], new_v[n] = d.reshape(p[n].shape), nm.reshape(p[n].shape), nv.reshape(p[n].shape)
    shapes = [p[n].shape for n in SMALL_ORDER]
    packs = [_pack_rows([src[n] for n in SMALL_ORDER], 8) for src in (p, grads, m, v)]
    outs = _adamw(*packs, name="adamw_small")
    for res, packed in zip((delta, new_m, new_v), outs):
        for n, val in zip(SMALL_ORDER, _unpack_rows(packed, shapes)):
            res[n] = val

    return (loss, dx[None], *[grads[n] for n in WEIGHTS], *[delta[n] for n in WEIGHTS],
            *[new_m[n] for n in WEIGHTS], *[new_v[n] for n in WEIGHTS])


def kernel(x, a_norm_g, a_w_in, a_ln_g, a_ln_b, a_ws, a_bs, a_w_out, kv_norm_g, w_kv, b_kv, b_norm_g, b_w_in, b_bq, b_sinks, b_w_out, final_norm_g, loss_target, m_a_norm_g, m_a_w_in, m_a_ln_g, m_a_ln_b, m_a_ws, m_a_bs, m_a_w_out, m_kv_norm_g, m_w_kv, m_b_kv, m_b_norm_g, m_b_w_in, m_b_bq, m_b_sinks, m_b_w_out, m_final_norm_g, v_a_norm_g, v_a_w_in, v_a_ln_g, v_a_ln_b, v_a_ws, v_a_bs, v_a_w_out, v_kv_norm_g, v_w_kv, v_b_kv, v_b_norm_g, v_b_w_in, v_b_bq, v_b_sinks, v_b_w_out, v_final_norm_g):
    p = dict(a_norm_g=a_norm_g, a_w_in=a_w_in, a_ln_g=a_ln_g, a_ln_b=a_ln_b, a_ws=a_ws, a_bs=a_bs, a_w_out=a_w_out,
             kv_norm_g=kv_norm_g, w_kv=w_kv, b_kv=b_kv, b_norm_g=b_norm_g, b_w_in=b_w_in, b_bq=b_bq, b_sinks=b_sinks,
             b_w_out=b_w_out, final_norm_g=final_norm_g)
    m = dict(a_norm_g=m_a_norm_g, a_w_in=m_a_w_in, a_ln_g=m_a_ln_g, a_ln_b=m_a_ln_b, a_ws=m_a_ws, a_bs=m_a_bs,
             a_w_out=m_a_w_out, kv_norm_g=m_kv_norm_g, w_kv=m_w_kv, b_kv=m_b_kv, b_norm_g=m_b_norm_g, b_w_in=m_b_w_in,
             b_bq=m_b_bq, b_sinks=m_b_sinks, b_w_out=m_b_w_out, final_norm_g=m_final_norm_g)
    v = dict(a_norm_g=v_a_norm_g, a_w_in=v_a_w_in, a_ln_g=v_a_ln_g, a_ln_b=v_a_ln_b, a_ws=v_a_ws, a_bs=v_a_bs,
             a_w_out=v_a_w_out, kv_norm_g=v_kv_norm_g, w_kv=v_w_kv, b_kv=v_b_kv, b_norm_g=v_b_norm_g, b_w_in=v_b_w_in,
             b_bq=v_b_bq, b_sinks=v_b_sinks, b_w_out=v_b_w_out, final_norm_g=v_final_norm_g)
    return _step(x, loss_target, p, m, v)
```

```python
import functools
import math

import jax
import jax.numpy as jnp
from jax import lax
from jax.experimental import pallas as pl
from jax.experimental.pallas import tpu as pltpu

F32 = jnp.float32
BF16 = jnp.bfloat16

D_MODEL = 1024
CHUNK = 128
A_WIDTH = 2048
A_GROUPS = 16
HEAD_DIM = 64
N_Q_HEADS = 16
N_KV_HEADS = 2
Q_PER_KV = 8
B_WIDTH = 1024
KV_WIDTH = 128
ROPE_THETA = 10000.0
EPS = 1e-5
N_CHIPS = 4

ADAM_LR = 0.001
ADAM_B1 = 0.9
ADAM_B2 = 0.999
ADAM_EPS = 1e-08
ADAM_WD = 0.01
ADAM_STEP = 10

VMEM_LIMIT = 48 * 1024 * 1024
MESH = pl.DeviceIdType.MESH
NEG_BIG = -1e30
HBM = pl.BlockSpec(memory_space=pl.ANY)

NN = (((1,), (0,)), ((), ()))
NT = (((1,), (1,)), ((), ()))
TN = (((0,), (0,)), ((), ()))


def _cparams(**kw):
    return pltpu.CompilerParams(vmem_limit_bytes=VMEM_LIMIT, **kw)


class _Side:
    def __init__(self, ins, out_shapes, sems, start, finish, aliases=None, passing=None):
        self.ins, self.out_shapes, self.sems = list(ins), list(out_shapes), list(sems)
        self.start, self.finish = start, finish
        self.passing = passing or (lambda ins, outs, sems: None)
        self.aliases = dict(aliases or {})


def _join(sides):
    sides = [s for s in sides if s is not None]
    if not sides:
        return None
    offs, i, o, m = [], 0, 0, 0
    for s in sides:
        offs.append((i, o, m))
        i, o, m = i + len(s.ins), o + len(s.out_shapes), m + len(s.sems)

    def run(which):
        def go(ins, outs, sems):
            for s, (a, b, c) in zip(sides, offs):
                getattr(s, which)(ins[a:a + len(s.ins)], outs[b:b + len(s.out_shapes)], sems[c:c + len(s.sems)])
        return go

    aliases = {}
    for s, (a, b, _) in zip(sides, offs):
        aliases.update({a + k: b + v for k, v in s.aliases.items()})
    return _Side([x for s in sides for x in s.ins], [x for s in sides for x in s.out_shapes],
                 [x for s in sides for x in s.sems], run("start"), run("finish"), aliases, run("passing"))


def _split(side_outs, sides):
    out, pos = [], 0
    for s in sides:
        out.append(list(side_outs[pos:pos + len(s.out_shapes)]))
        pos += len(s.out_shapes)
    return out


def _call(body, *, grid, in_specs, out_specs, out_shape, args, name, scratch=(), side=None):
    in_specs, out_specs, out_shape, scratch = list(in_specs), list(out_specs), list(out_shape), list(scratch)
    if side is None:
        res = pl.pallas_call(body, grid=grid, in_specs=in_specs, out_specs=out_specs, out_shape=out_shape,
                             scratch_shapes=scratch, name=name, compiler_params=_cparams())(*args)
        return list(res), []
    n_in, n_out, n_sc = len(in_specs), len(out_specs), len(scratch)
    s_in, s_out = len(side.ins), len(side.out_shapes)

    def wrapped(*refs):
        ins, refs = refs[:n_in], refs[n_in:]
        side_ins, refs = refs[:s_in], refs[s_in:]
        outs, refs = refs[:n_out], refs[n_out:]
        side_outs, refs = refs[:s_out], refs[s_out:]
        scr, side_sems = refs[:n_sc], refs[n_sc:]
        step = 0
        for a, g in enumerate(grid):
            step = step * g + pl.program_id(a)
        steps = math.prod(grid)

        @pl.when(step == 0)
        def _():
            side.start(side_ins, side_outs, side_sems)

        body(*ins, *outs, *scr)

        @pl.when(step == (3 * (steps - 1)) // 4)
        def _():
            side.passing(side_ins, side_outs, side_sems)

        @pl.when(step == steps - 1)
        def _():
            side.finish(side_ins, side_outs, side_sems)

    res = pl.pallas_call(
        wrapped, grid=grid, in_specs=in_specs + [HBM] * s_in, out_specs=out_specs + [HBM] * s_out,
        out_shape=out_shape + side.out_shapes, scratch_shapes=scratch + side.sems,
        input_output_aliases={n_in + k: n_out + v for k, v in side.aliases.items()},
        name=name, compiler_params=_cparams(),
    )(*args, *side.ins)
    return list(res[:n_out]), list(res[n_out:])


def _comm_call(side, name):
    s_in, s_out = len(side.ins), len(side.out_shapes)

    def body(*refs):
        ins, outs, sems = refs[:s_in], refs[s_in:s_in + s_out], refs[s_in + s_out:]
        side.start(ins, outs, sems)
        side.passing(ins, outs, sems)
        side.finish(ins, outs, sems)

    return list(pl.pallas_call(
        body, in_specs=[HBM] * s_in, out_specs=[HBM] * s_out, out_shape=side.out_shapes, scratch_shapes=side.sems,
        input_output_aliases=side.aliases, name=name,
    )(*side.ins))


def _matmul(a, b, *, dims, grid, a_spec, b_spec, o_spec, out_shape, name, acc_axis=None,
            residual=None, r_spec=None, side=None):
    has_res = residual is not None

    def body(*refs):
        if has_res:
            a_ref, b_ref, r_ref, o_ref = refs
        else:
            a_ref, b_ref, o_ref = refs
        part = lax.dot_general(a_ref[...], b_ref[...], dims, preferred_element_type=F32)
        if acc_axis is None:
            if has_res:
                part = part + r_ref[...]
            o_ref[...] = part.astype(o_ref.dtype)
        else:
            k = pl.program_id(acc_axis)

            @pl.when(k == 0)
            def _():
                o_ref[...] = part

            @pl.when(k > 0)
            def _():
                o_ref[...] += part

    in_specs = [a_spec, b_spec] + ([r_spec] if has_res else [])
    args = (a, b) + ((residual,) if has_res else ())
    (out,), side_outs = _call(body, grid=grid, in_specs=in_specs, out_specs=[o_spec], out_shape=[out_shape],
                              args=args, name=name, side=side)
    return (out, side_outs) if side is not None else out


def _row_tile(s, want):
    return min(s, want)


def _mm_nn(a, b, *, name, tn, out_dtype=F32, residual=None, tm=512, side=None):
    s, k = a.shape
    tm = _row_tile(s, tm)
    if b.ndim == 3:
        nsh, _, nc = b.shape
        npb = nc // tn
        n = nsh * nc
        b_spec = pl.BlockSpec((None, k, tn), lambda i, j: (j // npb, 0, j % npb))
    else:
        n = b.shape[1]
        b_spec = pl.BlockSpec((k, tn), lambda i, j: (0, j))
    return _matmul(
        a, b, dims=NN, grid=(s // tm, n // tn),
        a_spec=pl.BlockSpec((tm, k), lambda i, j: (i, 0)), b_spec=b_spec,
        o_spec=pl.BlockSpec((tm, tn), lambda i, j: (i, j)),
        out_shape=jax.ShapeDtypeStruct((s, n), out_dtype), name=name, side=side,
        residual=residual, r_spec=pl.BlockSpec((tm, tn), lambda i, j: (i, j)) if residual is not None else None)


def _mm_nt(a, b, *, name, tn=None, tm=512, out_dtype=F32, side=None):
    s, k = a.shape
    tm = _row_tile(s, tm)
    n = b.shape[0]
    tn = n if tn is None else tn
    return _matmul(
        a, b, dims=NT, grid=(s // tm, n // tn),
        a_spec=pl.BlockSpec((tm, k), lambda i, j: (i, 0)),
        b_spec=pl.BlockSpec((tn, k), lambda i, j: (j, 0)),
        o_spec=pl.BlockSpec((tm, tn), lambda i, j: (i, j)),
        out_shape=jax.ShapeDtypeStruct((s, n), out_dtype), name=name, side=side)


def _mm_tn(a, b, *, name, tm, tn, tk=2048, shards=None, side=None):
    s, m = a.shape
    n = b.shape[1]
    tk = _row_tile(s, tk)
    if shards is None:
        o_spec = pl.BlockSpec((tm, tn), lambda i, j, kk: (i, j))
        out_shape = jax.ShapeDtypeStruct((m, n), F32)
    else:
        assert tm == m
        nc = n // shards
        npb = nc // tn
        o_spec = pl.BlockSpec((None, m, tn), lambda i, j, kk: (j // npb, 0, j % npb))
        out_shape = jax.ShapeDtypeStruct((shards, m, nc), F32)
    return _matmul(
        a, b, dims=TN, grid=(m // tm, n // tn, s // tk), acc_axis=2,
        a_spec=pl.BlockSpec((tk, tm), lambda i, j, kk: (kk, i)),
        b_spec=pl.BlockSpec((tk, tn), lambda i, j, kk: (kk, j)),
        o_spec=o_spec, out_shape=out_shape, name=name, side=side)


def _rstd(x):
    return lax.rsqrt(jnp.mean(x * x, axis=-1, keepdims=True) + EPS)


def _rms_fwd(x, gains, *, name, tr=256):
    s, d = x.shape
    tr = _row_tile(s, tr)
    ng = len(gains)

    def body(*refs):
        xv = refs[0][...]
        xh = xv * _rstd(xv)
        for t in range(ng):
            refs[1 + ng + t][...] = (xh * refs[1 + t][...]).astype(BF16)

    row = pl.BlockSpec((tr, d), lambda i: (i, 0))
    vec = pl.BlockSpec((1, d), lambda i: (0, 0))
    outs, _ = _call(body, grid=(s // tr,), in_specs=[row] + [vec] * ng, out_specs=[row] * ng,
                    out_shape=[jax.ShapeDtypeStruct((s, d), BF16)] * ng, args=(x, *gains), name=name)
    return outs


def _accumulate(i, ref, value):
    @pl.when(i == 0)
    def _():
        ref[...] = value

    @pl.when(i > 0)
    def _():
        ref[...] += value


def _mm_residual_norms(y, w, res, gains, *, name, tm=512, side=None):
    s, k = y.shape
    d = w.shape[1]
    tm = _row_tile(s, tm)
    ng = len(gains)

    def body(y_ref, w_ref, r_ref, *rest):
        g_refs, h_ref, n_refs = rest[:ng], rest[ng], rest[ng + 1:]
        h = r_ref[...] + jnp.dot(y_ref[...], w_ref[...], preferred_element_type=F32)
        h_ref[...] = h
        xh = h * _rstd(h)
        for t in range(ng):
            n_refs[t][...] = (xh * g_refs[t][...]).astype(BF16)

    row = pl.BlockSpec((tm, d), lambda i: (i, 0))
    vec = pl.BlockSpec((1, d), lambda i: (0, 0))
    return _call(
        body, grid=(s // tm,),
        in_specs=[pl.BlockSpec((tm, k), lambda i: (i, 0)), pl.BlockSpec((k, d), lambda i: (0, 0)), row] + [vec] * ng,
        out_specs=[row] * (1 + ng),
        out_shape=[jax.ShapeDtypeStruct((s, d), F32)] + [jax.ShapeDtypeStruct((s, d), BF16)] * ng,
        args=(y, w, res, *gains), name=name, side=side)


def _mm_residual_loss(y, w, res, tgt, gain, *, name, tm=512):
    s, k = y.shape
    d = w.shape[1]
    tm = _row_tile(s, tm)

    def body(y_ref, w_ref, r_ref, t_ref, g_ref, loss_ref, dh_ref, dhb_ref, dg_ref):
        i = pl.program_id(0)
        hv = r_ref[...] + jnp.dot(y_ref[...], w_ref[...], preferred_element_type=F32)
        g = g_ref[...]
        r = _rstd(hv)
        xh = hv * r
        diff = xh * g - t_ref[...]
        part = 0.5 / d * jnp.sum(jnp.sum(diff * diff, axis=-1, keepdims=True), axis=0, keepdims=True)
        dout = diff * (1.0 / d)
        a = dout * g
        dh = r * (a - xh * jnp.mean(a * xh, axis=-1, keepdims=True))
        dh_ref[...] = dh
        dhb_ref[...] = dh.astype(BF16)
        _accumulate(i, dg_ref, jnp.sum(dout * xh, axis=0, keepdims=True))
        _accumulate(i, loss_ref, jnp.broadcast_to(part, (8, 128)))

    row = pl.BlockSpec((tm, d), lambda i: (i, 0))
    vec = pl.BlockSpec((1, d), lambda i: (0, 0))
    outs, _ = _call(
        body, grid=(s // tm,),
        in_specs=[pl.BlockSpec((tm, k), lambda i: (i, 0)), pl.BlockSpec((k, d), lambda i: (0, 0)), row, row, vec],
        out_specs=[pl.BlockSpec((8, 128), lambda i: (0, 0)), row, row, vec],
        out_shape=[jax.ShapeDtypeStruct((8, 128), F32), jax.ShapeDtypeStruct((s, d), F32),
                   jax.ShapeDtypeStruct((s, d), BF16), jax.ShapeDtypeStruct((1, d), F32)],
        args=(y, w, res, tgt, gain), name=name)
    return outs


def _mm_nt_rms_bwd(terms, x, dres, *, name, tm, side=None):
    s, d = x.shape
    tm = _row_tile(s, tm)
    nt = len(terms)

    def body(*refs):
        a_refs, b_refs, g_refs = refs[0:3 * nt:3], refs[1:3 * nt:3], refs[2:3 * nt:3]
        x_ref, dres_ref = refs[3 * nt], refs[3 * nt + 1]
        dx_ref, dxb_ref = refs[3 * nt + 2], refs[3 * nt + 3]
        dg_refs = refs[3 * nt + 4:]
        i = pl.program_id(0)
        xv = x_ref[...]
        r = _rstd(xv)
        xh = xv * r
        acc = jnp.zeros_like(xv)
        for t in range(nt):
            b_ref = b_refs[t]
            if len(b_ref.shape) == 3:
                kc = b_ref.shape[2]
                dn = None
                for sh in range(b_ref.shape[0]):
                    part = lax.dot_general(a_refs[t][:, sh * kc:(sh + 1) * kc], b_ref[sh], NT, preferred_element_type=F32)
                    dn = part if dn is None else dn + part
            else:
                dn = lax.dot_general(a_refs[t][...], b_ref[...], NT, preferred_element_type=F32)
            acc = acc + dn * g_refs[t][...]
            _accumulate(i, dg_refs[t], jnp.sum(dn * xh, axis=0, keepdims=True))
        dx = dres_ref[...] + r * (acc - xh * jnp.mean(acc * xh, axis=-1, keepdims=True))
        dx_ref[...] = dx
        dxb_ref[...] = dx.astype(BF16)

    row = pl.BlockSpec((tm, d), lambda i: (i, 0))
    vec = pl.BlockSpec((1, d), lambda i: (0, 0))
    in_specs, args = [], []
    for a, b, g in terms:
        in_specs += [pl.BlockSpec((tm, a.shape[1]), lambda i: (i, 0)),
                     pl.BlockSpec(b.shape, (lambda i: (0, 0, 0)) if b.ndim == 3 else (lambda i: (0, 0))), vec]
        args += [a, b, g]
    return _call(
        body, grid=(s // tm,), in_specs=in_specs + [row, row], out_specs=[row, row] + [vec] * nt,
        out_shape=[jax.ShapeDtypeStruct((s, d), F32), jax.ShapeDtypeStruct((s, d), BF16)]
        + [jax.ShapeDtypeStruct((1, d), F32)] * nt,
        args=(*args, x, dres), name=name, side=side)


def _causal_mask(transposed=False):
    row = lax.broadcasted_iota(jnp.int32, (CHUNK, CHUNK), 0)
    col = lax.broadcasted_iota(jnp.int32, (CHUNK, CHUNK), 1)
    return col >= row if transposed else col <= row


def _silu_parts(g):
    sg = jax.nn.sigmoid(g)
    return g * sg, sg * (1.0 + g * (1.0 - sg))


def _gate_fwd(z, ln_g, ln_b, ws, bs_t, *, tr=256, side=None):
    s = z.shape[0]
    tr = _row_tile(s, tr)
    w = A_WIDTH

    def body(u_ref, v_ref, g_ref, lg_ref, lb_ref, ws_ref, bst_ref, y_ref):
        v = v_ref[...].astype(F32)
        mu = jnp.mean(v, axis=-1, keepdims=True)
        xc = v - mu
        rs = lax.rsqrt(jnp.mean(xc * xc, axis=-1, keepdims=True) + EPS)
        vln = (xc * rs * lg_ref[...] + lb_ref[...]).astype(BF16)
        mask = _causal_mask()
        for grp in range(A_GROUPS):
            cols = slice(grp * CHUNK, (grp + 1) * CHUNK)
            wsm = jnp.where(mask, ws_ref[grp], 0.0).astype(BF16)
            bcol = bst_ref[:, grp:grp + 1]
            for ci in range(tr // CHUNK):
                rows = slice(ci * CHUNK, (ci + 1) * CHUNK)
                sv = jnp.dot(wsm, vln[rows, cols], preferred_element_type=F32) + bcol
                gv = g_ref[rows, cols].astype(F32)
                y_ref[rows, cols] = (u_ref[rows, cols].astype(F32) * sv * (gv * jax.nn.sigmoid(gv))).astype(BF16)

    vec = pl.BlockSpec((1, w), lambda i: (0, 0))
    (y,), side_outs = _call(
        body, grid=(s // tr,),
        in_specs=[pl.BlockSpec((tr, w), lambda i: (i, 0)), pl.BlockSpec((tr, w), lambda i: (i, 1)),
                  pl.BlockSpec((tr, w), lambda i: (i, 2)), vec, vec,
                  pl.BlockSpec((A_GROUPS, CHUNK, CHUNK), lambda i: (0, 0, 0)),
                  pl.BlockSpec((CHUNK, A_GROUPS), lambda i: (0, 0))],
        out_specs=[pl.BlockSpec((tr, w), lambda i: (i, 0))],
        out_shape=[jax.ShapeDtypeStruct((s, w), BF16)], args=(z, z, z, ln_g, ln_b, ws, bs_t), name="gate_fwd",
        side=side)
    return y, side_outs


def _gate_bwd(z, dy, ln_g, ln_b, ws, ws_t, bs_t, *, tr=256, side=None):
    s = z.shape[0]
    tr = _row_tile(s, tr)
    w = A_WIDTH
    nsteps = s // tr

    def body(u_ref, v_ref, g_ref, dy_ref, lg_ref, lb_ref, ws_ref, wst_ref, bst_ref,
             dz_ref, dlg_ref, dlb_ref, dws_ref, dbst_ref, dvln_sc, dsv_sc):
        i = pl.program_id(0)

        @pl.when(i == 0)
        def _():
            dws_ref[...] = jnp.zeros_like(dws_ref)
            dsv_sc[...] = jnp.zeros_like(dsv_sc)

        v = v_ref[...].astype(F32)
        mu = jnp.mean(v, axis=-1, keepdims=True)
        xc = v - mu
        rs = lax.rsqrt(jnp.mean(xc * xc, axis=-1, keepdims=True) + EPS)
        xh = xc * rs
        lg = lg_ref[...]
        vln = (xh * lg + lb_ref[...]).astype(BF16)
        mask = _causal_mask()
        mask_t = _causal_mask(transposed=True)
        for grp in range(A_GROUPS):
            cols = slice(grp * CHUNK, (grp + 1) * CHUNK)
            wsm = jnp.where(mask, ws_ref[grp], 0.0).astype(BF16)
            wsm_t = jnp.where(mask_t, wst_ref[grp], 0.0).astype(BF16)
            bcol = bst_ref[:, grp:grp + 1]
            for ci in range(tr // CHUNK):
                rows = slice(ci * CHUNK, (ci + 1) * CHUNK)
                vb = vln[rows, cols]
                sv = jnp.dot(wsm, vb, preferred_element_type=F32) + bcol
                uv = u_ref[rows, cols].astype(F32)
                silu, dsilu = _silu_parts(g_ref[rows, cols].astype(F32))
                dyv = dy_ref[rows, cols].astype(F32)
                dyu = dyv * uv
                dz_ref[rows, cols] = (dyv * sv * silu).astype(BF16)
                dz_ref[rows, 2 * w + grp * CHUNK:2 * w + (grp + 1) * CHUNK] = (dyu * sv * dsilu).astype(BF16)
                dsv = dyu * silu
                dsvb = dsv.astype(BF16)
                dvln_sc[rows, cols] = jnp.dot(wsm_t, dsvb, preferred_element_type=F32)
                dws_ref[grp] += lax.dot_general(dsvb, vb, NT, preferred_element_type=F32)
                dsv_sc[grp] += dsv
        dvln = dvln_sc[...]
        dlg_t = jnp.sum(dvln * xh, axis=0, keepdims=True)
        dlb_t = jnp.sum(dvln, axis=0, keepdims=True)
        a = dvln * lg
        dv = rs * (a - jnp.mean(a, axis=-1, keepdims=True) - xh * jnp.mean(a * xh, axis=-1, keepdims=True))
        dz_ref[:, w:2 * w] = dv.astype(BF16)

        @pl.when(i == 0)
        def _():
            dlg_ref[...] = dlg_t
            dlb_ref[...] = dlb_t

        @pl.when(i > 0)
        def _():
            dlg_ref[...] += dlg_t
            dlb_ref[...] += dlb_t

        @pl.when(i == nsteps - 1)
        def _():
            for grp in range(A_GROUPS):
                dws_ref[grp] = jnp.where(mask, dws_ref[grp], 0.0)
                dbst_ref[:, grp:grp + 1] = jnp.sum(dsv_sc[grp], axis=-1, keepdims=True)

    vec = pl.BlockSpec((1, w), lambda i: (0, 0))
    wsspec = pl.BlockSpec((A_GROUPS, CHUNK, CHUNK), lambda i: (0, 0, 0))
    bsspec = pl.BlockSpec((CHUNK, A_GROUPS), lambda i: (0, 0))
    return _call(
        body, grid=(nsteps,),
        in_specs=[pl.BlockSpec((tr, w), lambda i: (i, 0)), pl.BlockSpec((tr, w), lambda i: (i, 1)),
                  pl.BlockSpec((tr, w), lambda i: (i, 2)), pl.BlockSpec((tr, w), lambda i: (i, 0)),
                  vec, vec, wsspec, wsspec, bsspec],
        out_specs=[pl.BlockSpec((tr, 3 * w), lambda i: (i, 0)), vec, vec, wsspec, bsspec],
        out_shape=[jax.ShapeDtypeStruct((s, 3 * w), BF16), jax.ShapeDtypeStruct((1, w), F32),
                   jax.ShapeDtypeStruct((1, w), F32), jax.ShapeDtypeStruct((A_GROUPS, CHUNK, CHUNK), F32),
                   jax.ShapeDtypeStruct((CHUNK, A_GROUPS), F32)],
        scratch=[pltpu.VMEM((tr, w), F32), pltpu.VMEM((A_GROUPS, CHUNK, CHUNK), F32)],
        args=(z, z, z, dy, ln_g, ln_b, ws, ws_t, bs_t), name="gate_bwd", side=side)


HEADS_PER_BLOCK = 128 // HEAD_DIM
BLOCKS_PER_KV = Q_PER_KV // HEADS_PER_BLOCK
SCALE = HEAD_DIM ** -0.5
LOG2_E = math.log2(math.e)


def _rope_tables(s):
    lane = jnp.arange(128)
    inv_freq = ROPE_THETA ** (-(2 * (lane % (HEAD_DIM // 2))).astype(F32) / HEAD_DIM)
    sign = jnp.where(lane % HEAD_DIM < HEAD_DIM // 2, -1.0, 1.0).astype(F32)
    ang = jnp.arange(s, dtype=F32)[:, None] * inv_freq[None, :]
    return jnp.cos(ang), jnp.sin(ang) * sign[None, :]


def _swap_halves(x):
    n = x.shape[-1]
    lane = lax.broadcasted_iota(jnp.int32, x.shape, x.ndim - 1)
    first = (lane % HEAD_DIM) < (HEAD_DIM // 2)
    return jnp.where(first, pltpu.roll(x, n - HEAD_DIM // 2, x.ndim - 1), pltpu.roll(x, HEAD_DIM // 2, x.ndim - 1))


def _left_half(rows):
    return lax.broadcasted_iota(jnp.int32, (rows, 128), 1) < HEAD_DIM


def _dup_heads(x):
    left = _left_half(x.shape[0])
    swapped = pltpu.roll(x, HEAD_DIM, 1)
    return jnp.concatenate([jnp.where(left, x, swapped), jnp.where(left, swapped, x)], axis=-1)


def _fold_heads(a):
    b0, b1 = a[:, :128], a[:, 128:]
    f0 = b0 + pltpu.roll(b0, HEAD_DIM, 1)
    f1 = b1 + pltpu.roll(b1, HEAD_DIM, 1)
    return jnp.where(_left_half(a.shape[0]), f0, f1)


def _kv_rope(kv, b_kv, cos, sin, *, tr=512):
    s = kv.shape[0]
    tr = _row_tile(s, tr)

    def body(kv_ref, b_ref, c_ref, s_ref, k_ref, v_ref):
        x = kv_ref[...] + b_ref[...]
        k = x[:, :KV_WIDTH]
        k_ref[...] = _dup_heads(k * c_ref[...] + _swap_halves(k) * s_ref[...]).astype(BF16)
        v_ref[...] = _dup_heads(x[:, KV_WIDTH:]).astype(BF16)

    tab = pl.BlockSpec((tr, KV_WIDTH), lambda i: (i, 0))
    wide = pl.BlockSpec((tr, 2 * KV_WIDTH), lambda i: (i, 0))
    outs, _ = _call(body, grid=(s // tr,),
                    in_specs=[wide, pl.BlockSpec((1, 2 * KV_WIDTH), lambda i: (0, 0)), tab, tab],
                    out_specs=[wide, wide], out_shape=[jax.ShapeDtypeStruct((s, 2 * KV_WIDTH), BF16)] * 2,
                    args=(kv, b_kv, cos, sin), name="kv_rope")
    return outs


def _kv_rope_bwd(dk2, dv2, cos, sin, *, tr=512):
    s = dk2.shape[0]
    tr = _row_tile(s, tr)

    def body(dk_ref, dv_ref, c_ref, s_ref, dkv_ref, db_ref):
        i = pl.program_id(0)
        d = _fold_heads(dk_ref[...])
        dk = d * c_ref[...] + _swap_halves(d * s_ref[...])
        dvv = _fold_heads(dv_ref[...])
        dkv_ref[:, :KV_WIDTH] = dk.astype(BF16)
        dkv_ref[:, KV_WIDTH:] = dvv.astype(BF16)
        sk = jnp.sum(dk, axis=0, keepdims=True)
        sv = jnp.sum(dvv, axis=0, keepdims=True)

        @pl.when(i == 0)
        def _():
            db_ref[:, :KV_WIDTH] = sk
            db_ref[:, KV_WIDTH:] = sv

        @pl.when(i > 0)
        def _():
            db_ref[:, :KV_WIDTH] += sk
            db_ref[:, KV_WIDTH:] += sv

    tab = pl.BlockSpec((tr, KV_WIDTH), lambda i: (i, 0))
    wide = pl.BlockSpec((tr, 2 * KV_WIDTH), lambda i: (i, 0))
    outs, _ = _call(body, grid=(s // tr,), in_specs=[wide, wide, tab, tab],
                    out_specs=[wide, pl.BlockSpec((1, 2 * KV_WIDTH), lambda i: (0, 0))],
                    out_shape=[jax.ShapeDtypeStruct((s, 2 * KV_WIDTH), BF16),
                               jax.ShapeDtypeStruct((1, 2 * KV_WIDTH), F32)],
                    args=(dk2, dv2, cos, sin), name="kv_rope_bwd")
    return outs


def _from_previous():
    cols = Q_PER_KV * CHUNK
    k = lax.broadcasted_iota(jnp.int32, (CHUNK, cols), 0)
    q = lax.broadcasted_iota(jnp.int32, (CHUNK, cols), 1) & (CHUNK - 1)
    return k > q


def _fold(x2, prev):
    return jnp.where(prev, x2[:CHUNK], x2[CHUNK:])


def _unfold(x, prev):
    zero = jnp.zeros_like(x)
    return jnp.concatenate([jnp.where(prev, x, zero), jnp.where(prev, zero, x)], axis=0)


def _stack_heads(blocks, left):
    parts = []
    for b in blocks:
        parts.append(jnp.where(left, b, jnp.zeros_like(b)))
        parts.append(jnp.where(left, jnp.zeros_like(b), b))
    return jnp.concatenate(parts, axis=0)


def _unstack_heads(xt):
    top = lax.broadcasted_iota(jnp.int32, (128, CHUNK), 0) < HEAD_DIM
    return [jnp.where(top, xt[:, (2 * b) * CHUNK:(2 * b + 1) * CHUNK], xt[:, (2 * b + 1) * CHUNK:(2 * b + 2) * CHUNK]).T
            for b in range(BLOCKS_PER_KV)]


def _sink_row(sk_ref, kvh):
    return jnp.concatenate([jnp.full((1, CHUNK), sk_ref[0, kvh * Q_PER_KV + r], F32) for r in range(Q_PER_KV)], axis=1)


def _stacked_probs(qs, kd, prev, sink, i):
    sc2 = lax.dot_general(kd, qs, NT, preferred_element_type=F32)
    no_previous = jnp.where(i > 0, 0.0, NEG_BIG)
    sc = jnp.where(prev, sc2[:CHUNK] + no_previous, sc2[CHUNK:])
    sink = sink * (1.0 / SCALE)
    m = jnp.maximum(jnp.max(sc, axis=0, keepdims=True), sink)
    p = jnp.exp2((sc - m) * (SCALE * LOG2_E))
    esink = jnp.exp2((sink - m) * (SCALE * LOG2_E))
    inv = 1.0 / (jnp.sum(p, axis=0, keepdims=True) + esink)
    return p * inv, esink * inv


def _lane_block(b):
    return slice(b * 128, (b + 1) * 128)


def _rope_blocks(zq_ref, bq_ref, cos, sin, kvh):
    out = []
    for b in range(BLOCKS_PER_KV):
        cols = _lane_block(kvh * BLOCKS_PER_KV + b)
        q = zq_ref[:, cols].astype(F32) + bq_ref[:, cols]
        out.append((q * cos + _swap_halves(q) * sin).astype(BF16))
    return out


def _attn_specs():
    qspec = pl.BlockSpec((CHUNK, B_WIDTH), lambda i: (i, 0))
    gspec = pl.BlockSpec((CHUNK, B_WIDTH), lambda i: (i, 1))
    prev = pl.BlockSpec((CHUNK, 2 * KV_WIDTH), lambda i: (jnp.maximum(i - 1, 0), 0))
    cur = pl.BlockSpec((CHUNK, 2 * KV_WIDTH), lambda i: (i, 0))
    tab = pl.BlockSpec((CHUNK, KV_WIDTH), lambda i: (i, 0))
    bq = pl.BlockSpec((1, B_WIDTH), lambda i: (0, 0))
    sinks = pl.BlockSpec(memory_space=pltpu.SMEM)
    return qspec, gspec, prev, cur, tab, bq, sinks


def _attn_fwd(zb, k2, v2, cos, sin, b_bq, sinks, *, side=None):
    s = zb.shape[0]

    def body(zq_ref, zg_ref, kp_ref, kc_ref, vp_ref, vc_ref, c_ref, s_ref, bq_ref, sk_ref, y_ref):
        i = pl.program_id(0)
        cos, sin = c_ref[...], s_ref[...]
        kcat = jnp.concatenate([kp_ref[...], kc_ref[...]], axis=0)
        vcat = jnp.concatenate([vp_ref[...], vc_ref[...]], axis=0)
        prev = _from_previous()
        left = _left_half(CHUNK)
        for kvh in range(N_KV_HEADS):
            qs = _stack_heads(_rope_blocks(zq_ref, bq_ref, cos, sin, kvh), left)
            p, _ = _stacked_probs(qs, kcat[:, _lane_block(kvh)], prev, _sink_row(sk_ref, kvh), i)
            ot = lax.dot_general(vcat[:, _lane_block(kvh)], _unfold(p, prev).astype(BF16), TN,
                                 preferred_element_type=F32)
            for b, ob in enumerate(_unstack_heads(ot)):
                cols = _lane_block(kvh * BLOCKS_PER_KV + b)
                gv = zg_ref[:, cols].astype(F32)
                y_ref[:, cols] = (ob * (gv * jax.nn.sigmoid(gv))).astype(BF16)

    qspec, gspec, prev, cur, tab, bq, sk = _attn_specs()
    (y,), side_outs = _call(body, grid=(s // CHUNK,), in_specs=[qspec, gspec, prev, cur, prev, cur, tab, tab, bq, sk],
                            out_specs=[qspec], out_shape=[jax.ShapeDtypeStruct((s, B_WIDTH), BF16)],
                            args=(zb, zb, k2, k2, v2, v2, cos, sin, b_bq, sinks), name="attn_fwd", side=side)
    return y, side_outs


def _attn_bwd(zb, dyb, k2, v2, cos, sin, b_bq, sinks, *, side=None):
    s = zb.shape[0]

    def body(zq_ref, zg_ref, dy_ref, kp_ref, kc_ref, vp_ref, vc_ref, c_ref, s_ref, bq_ref, sk_ref,
             dz_ref, dk_ref, dv_ref, dbq_ref, dsk_ref):
        i = pl.program_id(0)

        @pl.when(i == 0)
        def _():
            dk_ref[...] = jnp.zeros_like(dk_ref)
            dv_ref[...] = jnp.zeros_like(dv_ref)
            dbq_ref[...] = jnp.zeros_like(dbq_ref)
            dsk_ref[...] = jnp.zeros_like(dsk_ref)

        cos, sin = c_ref[...], s_ref[...]
        kcat = jnp.concatenate([kp_ref[...], kc_ref[...]], axis=0)
        vcat = jnp.concatenate([vp_ref[...], vc_ref[...]], axis=0)
        prev = _from_previous()
        left = _left_half(CHUNK)
        lane = lax.broadcasted_iota(jnp.int32, (1, 128), 1)
        dsk_row = jnp.zeros((1, 128), F32)
        cur_rows = pl.ds(pl.multiple_of(i * CHUNK, CHUNK), CHUNK)
        for kvh in range(N_KV_HEADS):
            kd, vd = kcat[:, _lane_block(kvh)], vcat[:, _lane_block(kvh)]
            qs = _stack_heads(_rope_blocks(zq_ref, bq_ref, cos, sin, kvh), left)
            p, psink = _stacked_probs(qs, kd, prev, _sink_row(sk_ref, kvh), i)
            pb = _unfold(p, prev).astype(BF16)
            ot = lax.dot_general(vd, pb, TN, preferred_element_type=F32)
            gates, dys = [], []
            for b in range(BLOCKS_PER_KV):
                cols = _lane_block(kvh * BLOCKS_PER_KV + b)
                gates.append(_silu_parts(zg_ref[:, cols].astype(F32)))
                dys.append(dy_ref[:, cols].astype(F32))
            dos = _stack_heads([(dyv * silu).astype(BF16) for dyv, (silu, _) in zip(dys, gates)], left)
            dp = _fold(lax.dot_general(vd, dos, NT, preferred_element_type=F32), prev)
            delta = jnp.sum(p * dp, axis=0, keepdims=True)
            ds = _unfold(p * (dp - delta) * SCALE, prev).astype(BF16)
            dqt = lax.dot_general(kd, ds, TN, preferred_element_type=F32)
            dk_part = jnp.dot(ds, qs, preferred_element_type=F32)
            dv_part = jnp.dot(pb, dos, preferred_element_type=F32)
            dk_ref[cur_rows, _lane_block(kvh)] += dk_part[CHUNK:]
            dv_ref[cur_rows, _lane_block(kvh)] += dv_part[CHUNK:]

            @pl.when(i > 0)
            def _(kvh=kvh, dk_part=dk_part, dv_part=dv_part):
                prev_rows = pl.ds(pl.multiple_of((i - 1) * CHUNK, CHUNK), CHUNK)
                dk_ref[prev_rows, _lane_block(kvh)] += dk_part[:CHUNK]
                dv_ref[prev_rows, _lane_block(kvh)] += dv_part[:CHUNK]

            sink_grad = psink * delta
            for r in range(Q_PER_KV):
                dsink = -jnp.sum(sink_grad[:, r * CHUNK:(r + 1) * CHUNK], axis=1, keepdims=True)
                dsk_row = dsk_row + jnp.where(lane == kvh * Q_PER_KV + r, dsink, 0.0)
            blocks = zip(_unstack_heads(ot), _unstack_heads(dqt), dys, gates)
            for b, (ob, dqr, dyv, (_, dsilu)) in enumerate(blocks):
                blk = kvh * BLOCKS_PER_KV + b
                dq = dqr * cos + _swap_halves(dqr * sin)
                dbq_ref[:, _lane_block(blk)] += jnp.sum(dq, axis=0, keepdims=True)
                dz_ref[:, _lane_block(blk)] = dq.astype(BF16)
                dz_ref[:, _lane_block(B_WIDTH // 128 + blk)] = (dyv * ob * dsilu).astype(BF16)
        dsk_ref[0:1, :] += dsk_row

    qspec, gspec, prev, cur, tab, bq, sk = _attn_specs()
    full = pl.BlockSpec((s, 2 * KV_WIDTH), lambda i: (0, 0))
    return _call(
        body, grid=(s // CHUNK,),
        in_specs=[qspec, gspec, qspec, prev, cur, prev, cur, tab, tab, bq, sk],
        out_specs=[pl.BlockSpec((CHUNK, 2 * B_WIDTH), lambda i: (i, 0)), full, full, bq,
                   pl.BlockSpec((8, 128), lambda i: (0, 0))],
        out_shape=[jax.ShapeDtypeStruct((s, 2 * B_WIDTH), BF16), jax.ShapeDtypeStruct((s, 2 * KV_WIDTH), F32),
                   jax.ShapeDtypeStruct((s, 2 * KV_WIDTH), F32), jax.ShapeDtypeStruct((1, B_WIDTH), F32),
                   jax.ShapeDtypeStruct((8, 128), F32)],
        args=(zb, zb, dyb, k2, k2, v2, v2, cos, sin, b_bq, sinks), name="attn_bwd", side=side)


def _place():
    x, y, c = lax.axis_index("x"), lax.axis_index("y"), lax.axis_index("c")
    return x, y, c, [(1 - x, y), (x, 1 - y), (1 - x, 1 - y)]


def _relations():
    return [(r >> 2 & 1, r >> 1 & 1, r & 1) for r in range(1, 8)]


def _gather_side(arrs):
    n = len(arrs)

    def copies(ins, outs, sems):
        send_ici, recv_ici, send_d2d, recv_d2d, local_sem = sems
        x, y, c, chips = _place()
        me = 2 * x + y

        def rows(a, half):
            hr = arrs[a].shape[0] // 2
            return pl.ds(half * hr, hr)

        def ici(a, j, src_chip, to):
            return pltpu.make_async_remote_copy(
                src_ref=ins[a].at[rows(a, c)], dst_ref=outs[a].at[src_chip, rows(a, c)],
                send_sem=send_ici.at[a, j], recv_sem=recv_ici.at[a, j], device_id=to, device_id_type=MESH)

        def d2d(a, j, chip, half):
            blk = outs[a].at[chip, rows(a, half)]
            return pltpu.make_async_remote_copy(
                src_ref=blk, dst_ref=blk, send_sem=send_d2d.at[a, j], recv_sem=recv_d2d.at[a, j],
                device_id=(x, y, 1 - c), device_id_type=MESH)

        local = [pltpu.make_async_copy(ins[a], outs[a].at[me], local_sem.at[a]) for a in range(n)]
        pairs = [(a, j, chip) for a in range(n) for j, chip in enumerate(chips)]
        return c, me, local, ici, d2d, pairs

    def start(ins, outs, sems):
        c, me, local, ici, _, pairs = copies(ins, outs, sems)
        for cp in local:
            cp.start()
        for a, j, chip in pairs:
            ici(a, j, me, (*chip, c)).start()

    def passing(ins, outs, sems):
        c, _, _, ici, d2d, pairs = copies(ins, outs, sems)
        for a, j, (px, py) in pairs:
            ici(a, j, 2 * px + py, (px, py, c)).wait_recv()
            d2d(a, j, 2 * px + py, c).start()

    def finish(ins, outs, sems):
        c, me, local, ici, d2d, pairs = copies(ins, outs, sems)
        for a, j, (px, py) in pairs:
            d2d(a, j, 2 * px + py, 1 - c).wait_recv()
        for a, j, (px, py) in pairs:
            ici(a, j, me, (px, py, c)).wait_send()
            d2d(a, j, 2 * px + py, c).wait_send()
        for cp in local:
            cp.wait()

    return _Side(arrs, [jax.ShapeDtypeStruct((N_CHIPS,) + a.shape, a.dtype) for a in arrs],
                 [pltpu.SemaphoreType.DMA((n, 3))] * 4 + [pltpu.SemaphoreType.DMA((n,))], start, finish,
                 passing=passing)


def _exchange_side(grads):
    n = len(grads)

    def copies(ins, outs, sems):
        send_sem, recv_sem = sems
        x, y, c, _ = _place()
        cps = []
        for a in range(n):
            hr = grads[a].shape[1] // 2
            cps.append(pltpu.make_async_remote_copy(
                src_ref=ins[a].at[:, pl.ds((1 - c) * hr, hr), :], dst_ref=outs[a],
                send_sem=send_sem.at[a], recv_sem=recv_sem.at[a], device_id=(x, y, 1 - c), device_id_type=MESH))
        return cps

    def start(ins, outs, sems):
        for cp in copies(ins, outs, sems):
            cp.start()

    def finish(ins, outs, sems):
        for cp in copies(ins, outs, sems):
            cp.wait()

    return _Side(grads, [jax.ShapeDtypeStruct((g.shape[0], g.shape[1] // 2, g.shape[2]), g.dtype) for g in grads],
                 [pltpu.SemaphoreType.DMA((n,))] * 2, start, finish)


def _scatter_side(chip_sums, small=None):
    n = len(chip_sums)
    arrs = list(chip_sums) + ([small] if small is not None else [])

    def copies(ins, outs, sems):
        x, y, c, chips = _place()
        cps = []
        for a in range(n):
            for j, (px, py) in enumerate(chips):
                cps.append(pltpu.make_async_remote_copy(
                    src_ref=ins[a].at[2 * px + py], dst_ref=outs[a].at[j],
                    send_sem=sems[0].at[a, j], recv_sem=sems[1].at[a, j], device_id=(px, py, c), device_id_type=MESH))
        if small is not None:
            for r, (fx, fy, fc) in enumerate(_relations(), start=1):
                px, py, pc = x ^ fx, y ^ fy, c ^ fc
                cps.append(pltpu.make_async_remote_copy(
                    src_ref=ins[n].at[4 * px + 2 * py + pc], dst_ref=outs[n].at[r],
                    send_sem=sems[2].at[r - 1], recv_sem=sems[3].at[r - 1], device_id=(px, py, pc),
                    device_id_type=MESH))
        return cps

    def start(ins, outs, sems):
        for cp in copies(ins, outs, sems):
            cp.start()

    def finish(ins, outs, sems):
        for cp in copies(ins, outs, sems):
            cp.wait()

    shapes = [jax.ShapeDtypeStruct((3,) + t.shape[1:], t.dtype) for t in chip_sums]
    sems = [pltpu.SemaphoreType.DMA((n, 3))] * 2
    if small is not None:
        shapes.append(jax.ShapeDtypeStruct(small.shape, small.dtype))
        sems += [pltpu.SemaphoreType.DMA((7,))] * 2
    return _Side(arrs, shapes, sems, start, finish)


def _small_scatter_side(small):
    def copies(ins, outs, sems):
        x, y, c, _ = _place()
        cps = []
        for r, (fx, fy, fc) in enumerate(_relations(), start=1):
            px, py, pc = x ^ fx, y ^ fy, c ^ fc
            cps.append(pltpu.make_async_remote_copy(
                src_ref=ins[0].at[4 * px + 2 * py + pc], dst_ref=outs[0].at[r],
                send_sem=sems[0].at[r - 1], recv_sem=sems[1].at[r - 1], device_id=(px, py, pc), device_id_type=MESH))
        return cps

    def start(ins, outs, sems):
        for cp in copies(ins, outs, sems):
            cp.start()

    def finish(ins, outs, sems):
        for cp in copies(ins, outs, sems):
            cp.wait()

    return _Side([small], [jax.ShapeDtypeStruct(small.shape, small.dtype)], [pltpu.SemaphoreType.DMA((7,))] * 2,
                 start, finish)


def _small_share_side(small):
    return _share_side([], small)


def _share_side(halves, small=None):
    n = len(halves)
    arrs = list(halves) + ([small] if small is not None else [])

    def copies(ins, outs, sems, mine):
        x, y, c, _ = _place()
        me = 4 * x + 2 * y + c
        cps = []
        for a in range(n):
            hr = halves[a].shape[0] // 2
            rows = pl.ds((c if mine else 1 - c) * hr, hr)
            cps.append(pltpu.make_async_remote_copy(
                src_ref=ins[a].at[rows], dst_ref=outs[a].at[rows],
                send_sem=sems[0].at[a], recv_sem=sems[1].at[a], device_id=(x, y, 1 - c), device_id_type=MESH))
        if small is not None:
            for r, (fx, fy, fc) in enumerate(_relations(), start=1):
                px, py, pc = x ^ fx, y ^ fy, c ^ fc
                seg = me if mine else 4 * px + 2 * py + pc
                cps.append(pltpu.make_async_remote_copy(
                    src_ref=ins[n].at[seg], dst_ref=outs[n].at[seg],
                    send_sem=sems[-2].at[r - 1], recv_sem=sems[-1].at[r - 1], device_id=(px, py, pc),
                    device_id_type=MESH))
        return cps

    def start(ins, outs, sems):
        for cp in copies(ins, outs, sems, True):
            cp.start()

    def finish(ins, outs, sems):
        for cp in copies(ins, outs, sems, False):
            cp.wait_recv()
        for cp in copies(ins, outs, sems, True):
            cp.wait_send()

    sems = ([pltpu.SemaphoreType.DMA((n,))] * 2 if n else []) + (
        [pltpu.SemaphoreType.DMA((7,))] * 2 if small is not None else [])
    return _Side(arrs, [jax.ShapeDtypeStruct(h.shape, h.dtype) for h in arrs], sems, start, finish,
                 aliases={i: i for i in range(len(arrs))})


GATHER_PIECES = [(0, 0), (0, 1), (1, 0), (2, 0), (1, 1), (2, 1), (3, 0), (3, 1)]


def _mm_gathering(a, shard, order, *, name, tm=1024):
    s, k = a.shape
    nc = shard.shape[1]
    tm = _row_tile(s, tm)
    tn = nc // 2
    hr = k // 2
    qr = hr // 2
    blocks = jnp.stack([order[src] * 2 + h for src, h in GATHER_PIECES]).astype(jnp.int32)

    def body(blocks_ref, a_ref, shard_ref, z_ref, full_ref, wbuf, send_ici, recv_ici, send_relay,
             recv_relay, send_d2d, recv_d2d, local_sem, load_sem):
        piece, i = pl.program_id(0), pl.program_id(1)
        x, y, c, chips = _place()
        me = 2 * x + y
        nbrs = chips[:2]
        chip_of = [2 * px + py for px, py in chips]

        def quarter(q):
            return pl.ds(c * hr + q * qr, qr)

        def sibling_quarter(q):
            return pl.ds((1 - c) * hr + q * qr, qr)

        def whole(half):
            return pl.ds(half * hr, hr)

        def cols(h):
            return pl.ds(h * tn, tn)

        def direct(j, src_chip, h):
            return pltpu.make_async_remote_copy(
                src_ref=shard_ref.at[whole(c), cols(h)], dst_ref=full_ref.at[src_chip, whole(c), cols(h)],
                send_sem=send_ici.at[j, h], recv_sem=recv_ici.at[j, h], device_id=(*nbrs[j], c), device_id_type=MESH)

        def relay(j, src_chip, h):
            blk = full_ref.at[src_chip, quarter(j), cols(h)]
            return pltpu.make_async_remote_copy(
                src_ref=blk, dst_ref=blk, send_sem=send_relay.at[j, h], recv_sem=recv_relay.at[j, h],
                device_id=(*nbrs[1 - j], c), device_id_type=MESH)

        def d2d(j, chip, rows, h):
            blk = full_ref.at[chip, rows, cols(h)]
            return pltpu.make_async_remote_copy(
                src_ref=blk, dst_ref=blk, send_sem=send_d2d.at[j, h], recv_sem=recv_d2d.at[j, h],
                device_id=(x, y, 1 - c), device_id_type=MESH)

        def load(p):
            src, h = GATHER_PIECES[p]
            where = shard_ref if src == 0 else full_ref.at[chip_of[src - 1]]
            return pltpu.make_async_copy(where.at[:, cols(h)], wbuf.at[p % 2], load_sem.at[p % 2])

        local = pltpu.make_async_copy(shard_ref, full_ref.at[me], local_sem)

        def arrived(p):
            src, h = GATHER_PIECES[p]
            if src in (1, 2):
                j = src - 1
                direct(j, chip_of[j], h).wait_recv()
                relay(j, chip_of[j], h).start()
                d2d(j, chip_of[j], whole(c), h).start()
            elif src == 3:
                for j in range(2):
                    relay(1 - j, chip_of[2], h).wait_recv()
                    d2d(2 + j, chip_of[2], quarter(1 - j), h).start()

        def fetch(p):
            src, h = GATHER_PIECES[p]
            if src in (1, 2):
                d2d(src - 1, chip_of[src - 1], whole(1 - c), h).wait_recv()
            elif src == 3:
                for j in range(2):
                    d2d(2 + j, chip_of[2], sibling_quarter(1 - j), h).wait_recv()
            load(p).start()

        n_i = s // tm
        for p in range(len(GATHER_PIECES)):
            @pl.when(jnp.logical_and(piece == p, i == 0))
            def _(p=p):
                if p == 0:
                    local.start()
                    for hh in range(2):
                        for j in range(2):
                            direct(j, me, hh).start()
                    load(0).start()
                load(p).wait()

        z_ref[...] = jnp.dot(a_ref[...], wbuf[piece % 2], preferred_element_type=F32).astype(z_ref.dtype)

        for p in range(len(GATHER_PIECES) - 1):
            @pl.when(jnp.logical_and(piece == p, i == min(1, n_i - 1)))
            def _(p=p):
                arrived(p + 1)

            @pl.when(jnp.logical_and(piece == p, i == min(2, n_i - 1)))
            def _(p=p):
                fetch(p + 1)

        last = jnp.logical_and(piece == len(GATHER_PIECES) - 1, i == n_i - 1)

        @pl.when(last)
        def _():
            for h in range(2):
                for j in range(2):
                    direct(j, me, h).wait_send()
                    relay(j, chip_of[j], h).wait_send()
                    d2d(j, chip_of[j], whole(c), h).wait_send()
                    d2d(2 + j, chip_of[2], quarter(1 - j), h).wait_send()
            local.wait()

    return pl.pallas_call(
        body,
        grid_spec=pltpu.PrefetchScalarGridSpec(
            num_scalar_prefetch=1, grid=(len(GATHER_PIECES), s // tm),
            in_specs=[pl.BlockSpec((tm, k), lambda p, i, blocks: (i, 0)), HBM],
            out_specs=[pl.BlockSpec((tm, tn), lambda p, i, blocks: (i, blocks[p])), HBM],
            scratch_shapes=[pltpu.VMEM((2, k, tn), BF16)] + [pltpu.SemaphoreType.DMA((2, 2))] * 4
            + [pltpu.SemaphoreType.DMA((4, 2))] * 2 + [pltpu.SemaphoreType.DMA, pltpu.SemaphoreType.DMA((2,))]),
        out_shape=[jax.ShapeDtypeStruct((s, N_CHIPS * nc), BF16), jax.ShapeDtypeStruct((N_CHIPS, k, nc), BF16)],
        name=name, compiler_params=_cparams(),
    )(blocks, a, shard)


def _mm_tn_exchanging(a, b, *, name, shards, tk=2048, side=None):
    s, m = a.shape
    nc = b.shape[1] // shards
    tk = _row_tile(s, tk)
    nk = s // tk
    hm = m // 2

    def body(a_ref, b_ref, part_ref, sib_ref, acc, keep_sem, send_sem, recv_sem):
        j, kk = pl.program_id(0), pl.program_id(1)
        x, y, c, _ = _place()

        def keep(jj, slot):
            mine = pl.ds(c * hm, hm)
            return pltpu.make_async_copy(acc.at[slot, mine], part_ref.at[jj, mine], keep_sem.at[slot])

        def give(jj, slot):
            return pltpu.make_async_remote_copy(
                src_ref=acc.at[slot, pl.ds((1 - c) * hm, hm)], dst_ref=sib_ref.at[jj],
                send_sem=send_sem.at[slot], recv_sem=recv_sem.at[jj], device_id=(x, y, 1 - c), device_id_type=MESH)

        part = lax.dot_general(a_ref[...], b_ref[...], TN, preferred_element_type=F32)
        for slot in range(2):
            @pl.when(j % 2 == slot)
            def _(slot=slot):
                @pl.when(jnp.logical_and(kk == 0, j >= 2))
                def _():
                    keep(j - 2, slot).wait()
                    give(j - 2, slot).wait_send()

                @pl.when(kk == 0)
                def _():
                    acc[slot] = part

                @pl.when(kk > 0)
                def _():
                    acc[slot] += part

                @pl.when(kk == nk - 1)
                def _():
                    keep(j, slot).start()
                    give(j, slot).start()

        @pl.when(jnp.logical_and(j == shards - 1, kk == nk - 1))
        def _():
            for jj in range(shards - 2, shards):
                keep(jj, jj % 2).wait()
                give(jj, jj % 2).wait_send()
            for jj in range(shards):
                give(jj, jj % 2).wait_recv()

    assert shards >= 2
    return _call(
        body, grid=(shards, nk),
        in_specs=[pl.BlockSpec((tk, m), lambda j, kk: (kk, 0)), pl.BlockSpec((tk, nc), lambda j, kk: (kk, j))],
        out_specs=[HBM, HBM],
        out_shape=[jax.ShapeDtypeStruct((shards, m, nc), F32), jax.ShapeDtypeStruct((shards, hm, nc), F32)],
        scratch=[pltpu.VMEM((2, m, nc), F32), pltpu.SemaphoreType.DMA((2,)), pltpu.SemaphoreType.DMA((2,)),
                 pltpu.SemaphoreType.DMA((shards,))],
        args=(a, b), name=name, side=side)


def _col_tile(cols):
    return cols if cols <= 2048 else 512


def _add_sibling(grad, recv, core, *, name):
    k, r, c = grad.shape
    hr = r // 2
    tr = min(hr, 256)
    tc = _col_tile(c)
    nrb = hr // tr

    def body(core_ref, g_ref, r_ref, o_ref):
        o_ref[...] = (g_ref[...] + r_ref[...]).astype(BF16)

    return pl.pallas_call(
        body,
        grid_spec=pltpu.PrefetchScalarGridSpec(
            num_scalar_prefetch=1, grid=(k, nrb, c // tc),
            in_specs=[pl.BlockSpec((None, tr, tc), lambda kk, i, j, core: (kk, core[0] * nrb + i, j)),
                      pl.BlockSpec((None, tr, tc), lambda kk, i, j, core: (kk, i, j))],
            out_specs=pl.BlockSpec((None, tr, tc), lambda kk, i, j, core: (kk, i, j))),
        out_shape=jax.ShapeDtypeStruct((k, hr, c), BF16), name=name, compiler_params=_cparams(),
    )(core, grad, recv)


def _sum_chips(grad, from_sibling, recv, place, *, name):
    _, hr, c = from_sibling.shape
    tr = min(hr, 256)
    tc = _col_tile(c)
    nrb = hr // tr

    def body(place_ref, g_ref, s_ref, r0_ref, r1_ref, r2_ref, o_ref):
        own = g_ref[...] + s_ref[...]
        o_ref[...] = ((own + r0_ref[...].astype(F32)) + r1_ref[...].astype(F32)) + r2_ref[...].astype(F32)

    def rspec(j):
        return pl.BlockSpec((None, tr, tc), lambda i, jj, place: (j, i, jj))

    return pl.pallas_call(
        body,
        grid_spec=pltpu.PrefetchScalarGridSpec(
            num_scalar_prefetch=1, grid=(nrb, c // tc),
            in_specs=[pl.BlockSpec((None, tr, tc), lambda i, jj, place: (place[0], place[1] * nrb + i, jj)),
                      pl.BlockSpec((None, tr, tc), lambda i, jj, place: (place[0], i, jj)),
                      rspec(0), rspec(1), rspec(2)],
            out_specs=pl.BlockSpec((tr, tc), lambda i, jj, place: (place[1] * nrb + i, jj))),
        out_shape=jax.ShapeDtypeStruct((2 * hr, c), F32), name=name, compiler_params=_cparams(),
    )(place, grad, from_sibling, recv, recv, recv)


def _sum_small(small, recv, place):
    _, sr, _ = small.shape

    def body(place_ref, own_ref, r_ref, o_ref):
        acc = own_ref[...]
        for r in range(1, 8):
            acc = acc + r_ref[r]
        o_ref[...] = acc

    return pl.pallas_call(
        body,
        grid_spec=pltpu.PrefetchScalarGridSpec(
            num_scalar_prefetch=1, grid=(1,),
            in_specs=[pl.BlockSpec((None, sr, 128), lambda i, place: (place[2], 0, 0)),
                      pl.BlockSpec((8, sr, 128), lambda i, place: (0, 0, 0))],
            out_specs=pl.BlockSpec((None, sr, 128), lambda i, place: (place[2], 0, 0))),
        out_shape=jax.ShapeDtypeStruct(small.shape, F32), name="sum_small", compiler_params=_cparams(),
    )(place, small, recv)


def _spread_side(vec):
    def copies(ins, outs, sems):
        x, y, c, _ = _place()
        return [pltpu.make_async_remote_copy(
            src_ref=ins[0], dst_ref=outs[0].at[r], send_sem=sems[0].at[r - 1], recv_sem=sems[1].at[r - 1],
            device_id=(x ^ fx, y ^ fy, c ^ fc), device_id_type=MESH)
            for r, (fx, fy, fc) in enumerate(_relations(), start=1)]

    def start(ins, outs, sems):
        for cp in copies(ins, outs, sems):
            cp.start()

    def finish(ins, outs, sems):
        for cp in copies(ins, outs, sems):
            cp.wait()

    return _Side([vec], [jax.ShapeDtypeStruct((8,) + vec.shape, vec.dtype)], [pltpu.SemaphoreType.DMA((7,))] * 2,
                 start, finish)


def _sum_in_device_order(own, spread, place):
    def body(place_ref, own_ref, r_ref, o_ref):
        me = place_ref[2]
        acc = jnp.zeros_like(own_ref[...])
        for d in range(8):
            slot = jnp.where(me == d, 1, me ^ d)
            acc = acc + jnp.where(me == d, own_ref[...], r_ref[slot])
        o_ref[...] = acc

    return pl.pallas_call(
        body,
        grid_spec=pltpu.PrefetchScalarGridSpec(
            num_scalar_prefetch=1, grid=(1,),
            in_specs=[pl.BlockSpec(own.shape, lambda i, place: (0, 0)),
                      pl.BlockSpec(spread.shape, lambda i, place: (0, 0, 0))],
            out_specs=pl.BlockSpec(own.shape, lambda i, place: (0, 0))),
        out_shape=jax.ShapeDtypeStruct(own.shape, F32), name="sum_in_device_order", compiler_params=_cparams(),
    )(place, own, spread)


def _adamw(w, g, m, v, *, name):
    r, c = w.shape
    tr = 256 if r % 256 == 0 else r
    tc = _col_tile(c)
    bc1 = 1.0 - ADAM_B1 ** ADAM_STEP
    bc2 = 1.0 - ADAM_B2 ** ADAM_STEP

    def body(w_ref, g_ref, m_ref, v_ref, d_ref, nm_ref, nv_ref):
        gv = g_ref[...]
        nm = ADAM_B1 * m_ref[...] + (1.0 - ADAM_B1) * gv
        nv = ADAM_B2 * v_ref[...] + (1.0 - ADAM_B2) * (gv * gv)
        d_ref[...] = -ADAM_LR * ((nm / bc1) / (jnp.sqrt(nv / bc2) + ADAM_EPS) + ADAM_WD * w_ref[...])
        nm_ref[...] = nm
        nv_ref[...] = nv

    spec = pl.BlockSpec((tr, tc), lambda i, j: (i, j))
    outs, _ = _call(body, grid=(r // tr, c // tc), in_specs=[spec] * 4, out_specs=[spec] * 3,
                    out_shape=[jax.ShapeDtypeStruct((r, c), F32)] * 3, args=(w, g, m, v), name=name)
    return outs


SMALL_ORDER = ["a_ws", "a_bs", "a_norm_g", "a_ln_g", "a_ln_b", "kv_norm_g", "b_kv", "b_norm_g", "b_bq",
               "b_sinks", "final_norm_g"]
SHARDED_SMALL = {"a_norm_g", "a_ln_g", "a_ln_b"}
PACK_TILE = 8 * 128


def _rows128(a):
    flat = a.reshape(-1)
    return jnp.pad(flat, (0, (-flat.shape[0]) % PACK_TILE)).reshape(-1, 128)


def _pack_rows(parts, multiple):
    rows = [_rows128(p) for p in parts]
    total = sum(r.shape[0] for r in rows)
    pad = (-total) % multiple
    if pad:
        rows.append(jnp.zeros((pad, 128), rows[0].dtype))
    return jnp.concatenate(rows, axis=0)


def _unpack_rows(packed, shapes):
    out, row = [], 0
    for shp in shapes:
        size = math.prod(shp)
        nrow = -(-size // PACK_TILE) * 8
        out.append(packed[row:row + nrow].reshape(-1)[:size].reshape(shp))
        row += nrow
    return out


WEIGHTS = ["a_norm_g", "a_w_in", "a_ln_g", "a_ln_b", "a_ws", "a_bs", "a_w_out", "kv_norm_g", "w_kv", "b_kv",
           "b_norm_g", "b_w_in", "b_bq", "b_sinks", "b_w_out", "final_norm_g"]
BIG = ["a_w_in", "a_w_out", "w_kv", "b_w_in", "b_w_out"]


class _Reduction:
    def __init__(self, names, partials, core, place, small=None):
        self.names, self.partials, self.core, self.place, self.small = names, partials, core, place, small

    def exchange_side(self):
        return _exchange_side(self.partials)

    def took_exchange(self, from_sibling):
        self.from_sibling = from_sibling
        self.chip_sums = [_add_sibling(g, r, self.core, name="add_sibling_" + n)
                          for g, r, n in zip(self.partials, from_sibling, self.names)]

    def scatter_side(self):
        return _scatter_side(self.chip_sums, self.small)

    def took_scatter(self, arrived):
        big = arrived[:len(self.names)]
        self.halves = [_sum_chips(g, fs, r, self.place, name="sum_chips_" + n)
                       for g, fs, r, n in zip(self.partials, self.from_sibling, big, self.names)]
        self.small_mine = _sum_small(self.small, arrived[-1], self.place) if self.small is not None else None

    def share_side(self):
        return _share_side(self.halves, self.small_mine)

    def took_share(self, shared):
        self.grads = dict(zip(self.names, shared[:len(self.names)]))
        self.small_full = shared[-1] if self.small is not None else None


def _step(x, loss_target, p, m, v):
    xi, yi, ci = lax.axis_index("x"), lax.axis_index("y"), lax.axis_index("c")
    chip = 2 * xi + yi
    device = 4 * xi + 2 * yi + ci
    core = jnp.reshape(ci, (1,)).astype(jnp.int32)
    place = jnp.stack([chip, ci, device]).astype(jnp.int32)
    x, tgt = x[0], loss_target[0]
    s = x.shape[0]
    cos, sin = _rope_tables(s)

    shard2d = {n: p[n].reshape(p[n].shape[-2:]) for n in BIG}
    shard_bf = {n: shard2d[n].astype(BF16) for n in BIG}
    ws = p["a_ws"][0]
    ws_t = jnp.swapaxes(ws, 1, 2)
    bs_t = p["a_bs"][0].T
    kv_norm_g, b_kv = p["kv_norm_g"].reshape(1, -1), p["b_kv"].reshape(1, -1)
    final_norm_g = p["final_norm_g"].reshape(1, -1)

    vec_shapes = [p[n].shape for n in ("a_norm_g", "a_ln_g", "a_ln_b")]
    vec_pack = _pack_rows([p["a_norm_g"], p["a_ln_g"], p["a_ln_b"]], 16)
    (vec_all,) = _comm_call(_gather_side([vec_pack]), "gather_vectors")
    vecs = [_unpack_rows(vec_all[k], vec_shapes) for k in range(N_CHIPS)]
    a_norm_g, a_ln_g, a_ln_b = (jnp.concatenate([vk[t] for vk in vecs], axis=-1) for t in range(3))

    (n_a,) = _rms_fwd(x, [a_norm_g], name="rms_a")
    order = jnp.stack([chip, 2 * (1 - xi) + yi, 2 * xi + (1 - yi), 2 * (1 - xi) + (1 - yi)]).astype(jnp.int32)
    z, a_w_in = _mm_gathering(n_a, shard_bf["a_w_in"], order, name="mm_a_in")
    y, (a_w_out,) = _gate_fwd(z, a_ln_g, a_ln_b, ws, bs_t, side=_gather_side([shard_bf["a_w_out"]]))
    a_w_out = a_w_out.reshape(A_WIDTH, D_MODEL)
    (h1, n_kv, n_b), (w_kv, b_w_in) = _mm_residual_norms(
        y, a_w_out, x, [kv_norm_g, p["b_norm_g"]], name="mm_a_out",
        side=_gather_side([shard_bf["w_kv"], shard_bf["b_w_in"]]))
    w_kv = w_kv.reshape(D_MODEL, 2 * KV_WIDTH)
    kv = _mm_nn(n_kv, w_kv, name="mm_kv", tn=2 * KV_WIDTH)
    kr, vv = _kv_rope(kv, b_kv, cos, sin)
    zb = _mm_nn(n_b, b_w_in, name="mm_b_in", tn=512, tm=1024, out_dtype=BF16)
    yb, (b_w_out,) = _attn_fwd(zb, kr, vv, cos, sin, p["b_bq"], p["b_sinks"], side=_gather_side([shard_bf["b_w_out"]]))
    b_w_out = b_w_out.reshape(B_WIDTH, D_MODEL)
    loss_blk, dh2, dh2b, d_final_g = _mm_residual_loss(yb, b_w_out, h1, tgt, final_norm_g, name="mm_b_out")

    d_b_w_out = _mm_tn(yb, dh2b, name="mm_d_b_w_out", tm=B_WIDTH, tn=D_MODEL)
    red_bo = _Reduction(["b_w_out"], [d_b_w_out.reshape(N_CHIPS, B_WIDTH // N_CHIPS, D_MODEL)], core, place)
    dyb, got = _mm_nt(dh2b, b_w_out, name="mm_dyb", out_dtype=BF16, side=red_bo.exchange_side())
    red_bo.took_exchange(got)
    (dzb, dk_rot, dv, d_bq, d_sinks), got = _attn_bwd(zb, dyb, kr, vv, cos, sin, p["b_bq"], p["b_sinks"],
                                                      side=red_bo.scatter_side())
    red_bo.took_scatter(got)
    dkv, d_b_kv = _kv_rope_bwd(dk_rot, dv, cos, sin)
    d_b_w_in = _mm_tn(n_b, dzb, name="mm_d_b_w_in", tm=D_MODEL, tn=512, shards=N_CHIPS)
    d_w_kv, got = _mm_tn(n_kv, dkv, name="mm_d_w_kv", tm=D_MODEL, tn=2 * KV_WIDTH, side=red_bo.share_side())
    red_bo.took_share(got)
    red_bi = _Reduction(["b_w_in", "w_kv"], [d_b_w_in, d_w_kv.reshape(N_CHIPS, D_MODEL // N_CHIPS, 2 * KV_WIDTH)],
                        core, place)
    (dh1, dh1b, d_kv_g, d_b_g), got = _mm_nt_rms_bwd(
        [(dkv, w_kv, kv_norm_g), (dzb, b_w_in, p["b_norm_g"])], h1, dh2, name="mm_dn_b", tm=512,
        side=red_bi.exchange_side())
    red_bi.took_exchange(got)

    d_a_w_out = _mm_tn(y, dh1b, name="mm_d_a_w_out", tm=1024, tn=D_MODEL)
    red_ao = _Reduction(["a_w_out"], [d_a_w_out.reshape(N_CHIPS, A_WIDTH // N_CHIPS, D_MODEL)], core, place)
    dy, got = _mm_nt(dh1b, a_w_out, name="mm_dy", tn=1024, out_dtype=BF16, side=red_ao.exchange_side())
    red_ao.took_exchange(got)
    sides = [red_bi.scatter_side(), red_ao.scatter_side()]
    (dz, d_ln_g, d_ln_b, d_ws, d_bs_t), got = _gate_bwd(z, dy, a_ln_g, a_ln_b, ws, ws_t, bs_t, side=_join(sides))
    got = _split(got, sides)
    red_bi.took_scatter(got[0])
    red_ao.took_scatter(got[1])
    small = {
        "a_ws": d_ws, "a_bs": d_bs_t.T, "a_ln_g": d_ln_g, "a_ln_b": d_ln_b,
        "kv_norm_g": d_kv_g, "b_kv": d_b_kv, "b_norm_g": d_b_g, "b_bq": d_bq,
        "b_sinks": d_sinks[0:1, :N_Q_HEADS], "final_norm_g": d_final_g,
    }
    packed = [n for n in SMALL_ORDER if n != "a_norm_g"]
    small_shapes = [small[n].shape for n in packed] + [(1, 1)]
    small_pack = _pack_rows([small[n] for n in packed] + [loss_blk[0:1, 0:1]], 64)
    seg = small_pack.shape[0] // 8
    small_pack = small_pack.reshape(8, seg, 128)
    sides = [red_bi.share_side(), red_ao.share_side(), _small_scatter_side(small_pack)]
    (d_a_w_in, from_sibling), got = _mm_tn_exchanging(n_a, dz, name="mm_d_a_w_in", shards=N_CHIPS, side=_join(sides))
    got = _split(got, sides)
    red_bi.took_share(got[0])
    red_ao.took_share(got[1])
    small_mine = _sum_small(small_pack, got[2][0], place)

    red_ai = _Reduction(["a_w_in"], [d_a_w_in], core, place)
    red_ai.took_exchange([from_sibling])
    sides = [red_ai.scatter_side(), _small_share_side(small_mine)]
    (dx, _, d_a_g), got = _mm_nt_rms_bwd([(dz, a_w_in, a_norm_g)], x, dh1, name="mm_dn_a", tm=256, side=_join(sides))
    got = _split(got, sides)
    red_ai.took_scatter(got[0])
    small_all = got[1][0]
    d_a_g = _rows128(d_a_g)
    sides = [red_ai.share_side(), _spread_side(d_a_g)]
    got = _split(_comm_call(_join(sides), "share_last"), sides)
    red_ai.took_share(got[0])
    small_full = dict(zip(packed + ["loss"], _unpack_rows(small_all.reshape(8 * seg, 128), small_shapes)))
    small_full["a_norm_g"] = _sum_in_device_order(d_a_g, got[1][0], place).reshape(1, -1)
    loss = small_full["loss"].reshape(())

    grad_big = {**red_bo.grads, **red_bi.grads, **red_ao.grads, **red_ai.grads}
    grads = {}
    for n in SMALL_ORDER:
        gfull = small_full[n]
        if n in SHARDED_SMALL:
            width = p[n].shape[-1]
            gfull = lax.dynamic_slice_in_dim(gfull, chip * width, width, axis=-1)
        grads[n] = gfull.reshape(p[n].shape)
    for n in BIG:
        grads[n] = grad_big[n].reshape(p[n].shape)

    delta, new_m, new_v = {}, {}, {}
    for n in BIG:
        d, nm, nv = _adamw(shard2d[n], grad_big[n], m[n].reshape(shard2d[n].shape), v[n].reshape(shard2d[n].shape),
                           name="adamw_" + n)
        delta[n], new_m[n], new_v[n] = d.reshape(p[n].shape), nm.reshape(p[n].shape), nv.reshape(p[n].shape)
    shapes = [p[n].shape for n in SMALL_ORDER]
    packs = [_pack_rows([src[n] for n in SMALL_ORDER], 8) for src in (p, grads, m, v)]
    outs = _adamw(*packs, name="adamw_small")
    for res, packed in zip((delta, new_m, new_v), outs):
        for n, val in zip(SMALL_ORDER, _unpack_rows(packed, shapes)):
            res[n] = val

    return (loss, dx[None], *[grads[n] for n in WEIGHTS], *[delta[n] for n in WEIGHTS],
            *[new_m[n] for n in WEIGHTS], *[new_v[n] for n in WEIGHTS])


def kernel(x, a_norm_g, a_w_in, a_ln_g, a_ln_b, a_ws, a_bs, a_w_out, kv_norm_g, w_kv, b_kv, b_norm_g, b_w_in, b_bq, b_sinks, b_w_out, final_norm_g, loss_target, m_a_norm_g, m_a_w_in, m_a_ln_g, m_a_ln_b, m_a_ws, m_a_bs, m_a_w_out, m_kv_norm_g, m_w_kv, m_b_kv, m_b_norm_g, m_b_w_in, m_b_bq, m_b_sinks, m_b_w_out, m_final_norm_g, v_a_norm_g, v_a_w_in, v_a_ln_g, v_a_ln_b, v_a_ws, v_a_bs, v_a_w_out, v_kv_norm_g, v_w_kv, v_b_kv, v_b_norm_g, v_b_w_in, v_b_bq, v_b_sinks, v_b_w_out, v_final_norm_g):
    p = dict(a_norm_g=a_norm_g, a_w_in=a_w_in, a_ln_g=a_ln_g, a_ln_b=a_ln_b, a_ws=a_ws, a_bs=a_bs, a_w_out=a_w_out,
             kv_norm_g=kv_norm_g, w_kv=w_kv, b_kv=b_kv, b_norm_g=b_norm_g, b_w_in=b_w_in, b_bq=b_bq, b_sinks=b_sinks,
             b_w_out=b_w_out, final_norm_g=final_norm_g)
    m = dict(a_norm_g=m_a_norm_g, a_w_in=m_a_w_in, a_ln_g=m_a_ln_g, a_ln_b=m_a_ln_b, a_ws=m_a_ws, a_bs=m_a_bs,
             a_w_out=m_a_w_out, kv_norm_g=m_kv_norm_g, w_kv=m_w_kv, b_kv=m_b_kv, b_norm_g=m_b_norm_g, b_w_in=m_b_w_in,
             b_bq=m_b_bq, b_sinks=m_b_sinks, b_w_out=m_b_w_out, final_norm_g=m_final_norm_g)
    v = dict(a_norm_g=v_a_norm_g, a_w_in=v_a_w_in, a_ln_g=v_a_ln_g, a_ln_b=v_a_ln_b, a_ws=v_a_ws, a_bs=v_a_bs,
             a_w_out=v_a_w_out, kv_norm_g=v_kv_norm_g, w_kv=v_w_kv, b_kv=v_b_kv, b_norm_g=v_b_norm_g, b_w_in=v_b_w_in,
             b_bq=v_b_bq, b_sinks=v_b_sinks, b_w_out=v_b_w_out, final_norm_g=v_final_norm_g)
    return _step(x, loss_target, p, m, v)
```

```python
import functools
import math

import jax
import jax.numpy as jnp
from jax import lax
from jax.experimental import pallas as pl
from jax.experimental.pallas import tpu as pltpu

F32 = jnp.float32
BF16 = jnp.bfloat16

D_MODEL = 1024
CHUNK = 128
A_WIDTH = 2048
A_GROUPS = 16
HEAD_DIM = 64
N_Q_HEADS = 16
N_KV_HEADS = 2
Q_PER_KV = 8
B_WIDTH = 1024
KV_WIDTH = 128
ROPE_THETA = 10000.0
EPS = 1e-5
N_CHIPS = 4

ADAM_LR = 0.001
ADAM_B1 = 0.9
ADAM_B2 = 0.999
ADAM_EPS = 1e-08
ADAM_WD = 0.01
ADAM_STEP = 10

VMEM_LIMIT = 48 * 1024 * 1024
MESH = pl.DeviceIdType.MESH
NEG_BIG = -1e30
HBM = pl.BlockSpec(memory_space=pl.ANY)

NN = (((1,), (0,)), ((), ()))
NT = (((1,), (1,)), ((), ()))
TN = (((0,), (0,)), ((), ()))


def _cparams(**kw):
    return pltpu.CompilerParams(vmem_limit_bytes=VMEM_LIMIT, **kw)


class _Side:
    def __init__(self, ins, out_shapes, sems, start, finish, aliases=None, passing=None):
        self.ins, self.out_shapes, self.sems = list(ins), list(out_shapes), list(sems)
        self.start, self.finish = start, finish
        self.passing = passing or (lambda ins, outs, sems: None)
        self.aliases = dict(aliases or {})


def _join(sides):
    sides = [s for s in sides if s is not None]
    if not sides:
        return None
    offs, i, o, m = [], 0, 0, 0
    for s in sides:
        offs.append((i, o, m))
        i, o, m = i + len(s.ins), o + len(s.out_shapes), m + len(s.sems)

    def run(which):
        def go(ins, outs, sems):
            for s, (a, b, c) in zip(sides, offs):
                getattr(s, which)(ins[a:a + len(s.ins)], outs[b:b + len(s.out_shapes)], sems[c:c + len(s.sems)])
        return go

    aliases = {}
    for s, (a, b, _) in zip(sides, offs):
        aliases.update({a + k: b + v for k, v in s.aliases.items()})
    return _Side([x for s in sides for x in s.ins], [x for s in sides for x in s.out_shapes],
                 [x for s in sides for x in s.sems], run("start"), run("finish"), aliases, run("passing"))


def _split(side_outs, sides):
    out, pos = [], 0
    for s in sides:
        out.append(list(side_outs[pos:pos + len(s.out_shapes)]))
        pos += len(s.out_shapes)
    return out


def _call(body, *, grid, in_specs, out_specs, out_shape, args, name, scratch=(), side=None):
    in_specs, out_specs, out_shape, scratch = list(in_specs), list(out_specs), list(out_shape), list(scratch)
    if side is None:
        res = pl.pallas_call(body, grid=grid, in_specs=in_specs, out_specs=out_specs, out_shape=out_shape,
                             scratch_shapes=scratch, name=name, compiler_params=_cparams())(*args)
        return list(res), []
    n_in, n_out, n_sc = len(in_specs), len(out_specs), len(scratch)
    s_in, s_out = len(side.ins), len(side.out_shapes)

    def wrapped(*refs):
        ins, refs = refs[:n_in], refs[n_in:]
        side_ins, refs = refs[:s_in], refs[s_in:]
        outs, refs = refs[:n_out], refs[n_out:]
        side_outs, refs = refs[:s_out], refs[s_out:]
        scr, side_sems = refs[:n_sc], refs[n_sc:]
        step = 0
        for a, g in enumerate(grid):
            step = step * g + pl.program_id(a)
        steps = math.prod(grid)

        @pl.when(step == 0)
        def _():
            side.start(side_ins, side_outs, side_sems)

        body(*ins, *outs, *scr)

        @pl.when(step == (3 * (steps - 1)) // 4)
        def _():
            side.passing(side_ins, side_outs, side_sems)

        @pl.when(step == steps - 1)
        def _():
            side.finish(side_ins, side_outs, side_sems)

    res = pl.pallas_call(
        wrapped, grid=grid, in_specs=in_specs + [HBM] * s_in, out_specs=out_specs + [HBM] * s_out,
        out_shape=out_shape + side.out_shapes, scratch_shapes=scratch + side.sems,
        input_output_aliases={n_in + k: n_out + v for k, v in side.aliases.items()},
        name=name, compiler_params=_cparams(),
    )(*args, *side.ins)
    return list(res[:n_out]), list(res[n_out:])


def _comm_call(side, name):
    s_in, s_out = len(side.ins), len(side.out_shapes)

    def body(*refs):
        ins, outs, sems = refs[:s_in], refs[s_in:s_in + s_out], refs[s_in + s_out:]
        side.start(ins, outs, sems)
        side.passing(ins, outs, sems)
        side.finish(ins, outs, sems)

    return list(pl.pallas_call(
        body, in_specs=[HBM] * s_in, out_specs=[HBM] * s_out, out_shape=side.out_shapes, scratch_shapes=side.sems,
        input_output_aliases=side.aliases, name=name,
    )(*side.ins))


def _matmul(a, b, *, dims, grid, a_spec, b_spec, o_spec, out_shape, name, acc_axis=None,
            residual=None, r_spec=None, side=None):
    has_res = residual is not None

    def body(*refs):
        if has_res:
            a_ref, b_ref, r_ref, o_ref = refs
        else:
            a_ref, b_ref, o_ref = refs
        part = lax.dot_general(a_ref[...], b_ref[...], dims, preferred_element_type=F32)
        if acc_axis is None:
            if has_res:
                part = part + r_ref[...]
            o_ref[...] = part.astype(o_ref.dtype)
        else:
            k = pl.program_id(acc_axis)

            @pl.when(k == 0)
            def _():
                o_ref[...] = part

            @pl.when(k > 0)
            def _():
                o_ref[...] += part

    in_specs = [a_spec, b_spec] + ([r_spec] if has_res else [])
    args = (a, b) + ((residual,) if has_res else ())
    (out,), side_outs = _call(body, grid=grid, in_specs=in_specs, out_specs=[o_spec], out_shape=[out_shape],
                              args=args, name=name, side=side)
    return (out, side_outs) if side is not None else out


def _row_tile(s, want):
    return min(s, want)


def _mm_nn(a, b, *, name, tn, out_dtype=F32, residual=None, tm=512, side=None):
    s, k = a.shape
    tm = _row_tile(s, tm)
    if b.ndim == 3:
        nsh, _, nc = b.shape
        npb = nc // tn
        n = nsh * nc
        b_spec = pl.BlockSpec((None, k, tn), lambda i, j: (j // npb, 0, j % npb))
    else:
        n = b.shape[1]
        b_spec = pl.BlockSpec((k, tn), lambda i, j: (0, j))
    return _matmul(
        a, b, dims=NN, grid=(s // tm, n // tn),
        a_spec=pl.BlockSpec((tm, k), lambda i, j: (i, 0)), b_spec=b_spec,
        o_spec=pl.BlockSpec((tm, tn), lambda i, j: (i, j)),
        out_shape=jax.ShapeDtypeStruct((s, n), out_dtype), name=name, side=side,
        residual=residual, r_spec=pl.BlockSpec((tm, tn), lambda i, j: (i, j)) if residual is not None else None)


def _mm_nt(a, b, *, name, tn=None, tm=512, out_dtype=F32, side=None):
    s, k = a.shape
    tm = _row_tile(s, tm)
    n = b.shape[0]
    tn = n if tn is None else tn
    return _matmul(
        a, b, dims=NT, grid=(s // tm, n // tn),
        a_spec=pl.BlockSpec((tm, k), lambda i, j: (i, 0)),
        b_spec=pl.BlockSpec((tn, k), lambda i, j: (j, 0)),
        o_spec=pl.BlockSpec((tm, tn), lambda i, j: (i, j)),
        out_shape=jax.ShapeDtypeStruct((s, n), out_dtype), name=name, side=side)


def _mm_tn(a, b, *, name, tm, tn, tk=2048, shards=None, side=None):
    s, m = a.shape
    n = b.shape[1]
    tk = _row_tile(s, tk)
    if shards is None:
        o_spec = pl.BlockSpec((tm, tn), lambda i, j, kk: (i, j))
        out_shape = jax.ShapeDtypeStruct((m, n), F32)
    else:
        assert tm == m
        nc = n // shards
        npb = nc // tn
        o_spec = pl.BlockSpec((None, m, tn), lambda i, j, kk: (j // npb, 0, j % npb))
        out_shape = jax.ShapeDtypeStruct((shards, m, nc), F32)
    return _matmul(
        a, b, dims=TN, grid=(m // tm, n // tn, s // tk), acc_axis=2,
        a_spec=pl.BlockSpec((tk, tm), lambda i, j, kk: (kk, i)),
        b_spec=pl.BlockSpec((tk, tn), lambda i, j, kk: (kk, j)),
        o_spec=o_spec, out_shape=out_shape, name=name, side=side)


def _rstd(x):
    return lax.rsqrt(jnp.mean(x * x, axis=-1, keepdims=True) + EPS)


def _rms_fwd(x, gains, *, name, tr=256):
    s, d = x.shape
    tr = _row_tile(s, tr)
    ng = len(gains)

    def body(*refs):
        xv = refs[0][...]
        xh = xv * _rstd(xv)
        for t in range(ng):
            refs[1 + ng + t][...] = (xh * refs[1 + t][...]).astype(BF16)

    row = pl.BlockSpec((tr, d), lambda i: (i, 0))
    vec = pl.BlockSpec((1, d), lambda i: (0, 0))
    outs, _ = _call(body, grid=(s // tr,), in_specs=[row] + [vec] * ng, out_specs=[row] * ng,
                    out_shape=[jax.ShapeDtypeStruct((s, d), BF16)] * ng, args=(x, *gains), name=name)
    return outs


def _accumulate(i, ref, value):
    @pl.when(i == 0)
    def _():
        ref[...] = value

    @pl.when(i > 0)
    def _():
        ref[...] += value


def _mm_residual_norms(y, w, res, gains, *, name, tm=512, side=None):
    s, k = y.shape
    d = w.shape[1]
    tm = _row_tile(s, tm)
    ng = len(gains)

    def body(y_ref, w_ref, r_ref, *rest):
        g_refs, h_ref, n_refs = rest[:ng], rest[ng], rest[ng + 1:]
        h = r_ref[...] + jnp.dot(y_ref[...], w_ref[...], preferred_element_type=F32)
        h_ref[...] = h
        xh = h * _rstd(h)
        for t in range(ng):
            n_refs[t][...] = (xh * g_refs[t][...]).astype(BF16)

    row = pl.BlockSpec((tm, d), lambda i: (i, 0))
    vec = pl.BlockSpec((1, d), lambda i: (0, 0))
    return _call(
        body, grid=(s // tm,),
        in_specs=[pl.BlockSpec((tm, k), lambda i: (i, 0)), pl.BlockSpec((k, d), lambda i: (0, 0)), row] + [vec] * ng,
        out_specs=[row] * (1 + ng),
        out_shape=[jax.ShapeDtypeStruct((s, d), F32)] + [jax.ShapeDtypeStruct((s, d), BF16)] * ng,
        args=(y, w, res, *gains), name=name, side=side)


def _mm_residual_loss(y, w, res, tgt, gain, *, name, tm=512):
    s, k = y.shape
    d = w.shape[1]
    tm = _row_tile(s, tm)

    def body(y_ref, w_ref, r_ref, t_ref, g_ref, loss_ref, dh_ref, dhb_ref, dg_ref):
        i = pl.program_id(0)
        hv = r_ref[...] + jnp.dot(y_ref[...], w_ref[...], preferred_element_type=F32)
        g = g_ref[...]
        r = _rstd(hv)
        xh = hv * r
        diff = xh * g - t_ref[...]
        part = 0.5 / d * jnp.sum(jnp.sum(diff * diff, axis=-1, keepdims=True), axis=0, keepdims=True)
        dout = diff * (1.0 / d)
        a = dout * g
        dh = r * (a - xh * jnp.mean(a * xh, axis=-1, keepdims=True))
        dh_ref[...] = dh
        dhb_ref[...] = dh.astype(BF16)
        _accumulate(i, dg_ref, jnp.sum(dout * xh, axis=0, keepdims=True))
        _accumulate(i, loss_ref, jnp.broadcast_to(part, (8, 128)))

    row = pl.BlockSpec((tm, d), lambda i: (i, 0))
    vec = pl.BlockSpec((1, d), lambda i: (0, 0))
    outs, _ = _call(
        body, grid=(s // tm,),
        in_specs=[pl.BlockSpec((tm, k), lambda i: (i, 0)), pl.BlockSpec((k, d), lambda i: (0, 0)), row, row, vec],
        out_specs=[pl.BlockSpec((8, 128), lambda i: (0, 0)), row, row, vec],
        out_shape=[jax.ShapeDtypeStruct((8, 128), F32), jax.ShapeDtypeStruct((s, d), F32),
                   jax.ShapeDtypeStruct((s, d), BF16), jax.ShapeDtypeStruct((1, d), F32)],
        args=(y, w, res, tgt, gain), name=name)
    return outs


def _mm_nt_rms_bwd(terms, x, dres, *, name, tm, side=None):
    s, d = x.shape
    tm = _row_tile(s, tm)
    nt = len(terms)

    def body(*refs):
        a_refs, b_refs, g_refs = refs[0:3 * nt:3], refs[1:3 * nt:3], refs[2:3 * nt:3]
        x_ref, dres_ref = refs[3 * nt], refs[3 * nt + 1]
        dx_ref, dxb_ref = refs[3 * nt + 2], refs[3 * nt + 3]
        dg_refs = refs[3 * nt + 4:]
        i = pl.program_id(0)
        xv = x_ref[...]
        r = _rstd(xv)
        xh = xv * r
        acc = jnp.zeros_like(xv)
        for t in range(nt):
            b_ref = b_refs[t]
            if len(b_ref.shape) == 3:
                kc = b_ref.shape[2]
                dn = None
                for sh in range(b_ref.shape[0]):
                    part = lax.dot_general(a_refs[t][:, sh * kc:(sh + 1) * kc], b_ref[sh], NT, preferred_element_type=F32)
                    dn = part if dn is None else dn + part
            else:
                dn = lax.dot_general(a_refs[t][...], b_ref[...], NT, preferred_element_type=F32)
            acc = acc + dn * g_refs[t][...]
            _accumulate(i, dg_refs[t], jnp.sum(dn * xh, axis=0, keepdims=True))
        dx = dres_ref[...] + r * (acc - xh * jnp.mean(acc * xh, axis=-1, keepdims=True))
        dx_ref[...] = dx
        dxb_ref[...] = dx.astype(BF16)

    row = pl.BlockSpec((tm, d), lambda i: (i, 0))
    vec = pl.BlockSpec((1, d), lambda i: (0, 0))
    in_specs, args = [], []
    for a, b, g in terms:
        in_specs += [pl.BlockSpec((tm, a.shape[1]), lambda i: (i, 0)),
                     pl.BlockSpec(b.shape, (lambda i: (0, 0, 0)) if b.ndim == 3 else (lambda i: (0, 0))), vec]
        args += [a, b, g]
    return _call(
        body, grid=(s // tm,), in_specs=in_specs + [row, row], out_specs=[row, row] + [vec] * nt,
        out_shape=[jax.ShapeDtypeStruct((s, d), F32), jax.ShapeDtypeStruct((s, d), BF16)]
        + [jax.ShapeDtypeStruct((1, d), F32)] * nt,
        args=(*args, x, dres), name=name, side=side)


def _causal_mask(transposed=False):
    row = lax.broadcasted_iota(jnp.int32, (CHUNK, CHUNK), 0)
    col = lax.broadcasted_iota(jnp.int32, (CHUNK, CHUNK), 1)
    return col >= row if transposed else col <= row


def _silu_parts(g):
    sg = jax.nn.sigmoid(g)
    return g * sg, sg * (1.0 + g * (1.0 - sg))


def _gate_fwd(z, ln_g, ln_b, ws, bs_t, *, tr=256, side=None):
    s = z.shape[0]
    tr = _row_tile(s, tr)
    w = A_WIDTH

    def body(u_ref, v_ref, g_ref, lg_ref, lb_ref, ws_ref, bst_ref, y_ref):
        v = v_ref[...].astype(F32)
        mu = jnp.mean(v, axis=-1, keepdims=True)
        xc = v - mu
        rs = lax.rsqrt(jnp.mean(xc * xc, axis=-1, keepdims=True) + EPS)
        vln = (xc * rs * lg_ref[...] + lb_ref[...]).astype(BF16)
        mask = _causal_mask()
        for grp in range(A_GROUPS):
            cols = slice(grp * CHUNK, (grp + 1) * CHUNK)
            wsm = jnp.where(mask, ws_ref[grp], 0.0).astype(BF16)
            bcol = bst_ref[:, grp:grp + 1]
            for ci in range(tr // CHUNK):
                rows = slice(ci * CHUNK, (ci + 1) * CHUNK)
                sv = jnp.dot(wsm, vln[rows, cols], preferred_element_type=F32) + bcol
                gv = g_ref[rows, cols].astype(F32)
                y_ref[rows, cols] = (u_ref[rows, cols].astype(F32) * sv * (gv * jax.nn.sigmoid(gv))).astype(BF16)

    vec = pl.BlockSpec((1, w), lambda i: (0, 0))
    (y,), side_outs = _call(
        body, grid=(s // tr,),
        in_specs=[pl.BlockSpec((tr, w), lambda i: (i, 0)), pl.BlockSpec((tr, w), lambda i: (i, 1)),
                  pl.BlockSpec((tr, w), lambda i: (i, 2)), vec, vec,
                  pl.BlockSpec((A_GROUPS, CHUNK, CHUNK), lambda i: (0, 0, 0)),
                  pl.BlockSpec((CHUNK, A_GROUPS), lambda i: (0, 0))],
        out_specs=[pl.BlockSpec((tr, w), lambda i: (i, 0))],
        out_shape=[jax.ShapeDtypeStruct((s, w), BF16)], args=(z, z, z, ln_g, ln_b, ws, bs_t), name="gate_fwd",
        side=side)
    return y, side_outs


def _gate_bwd(z, dy, ln_g, ln_b, ws, ws_t, bs_t, *, tr=256, side=None):
    s = z.shape[0]
    tr = _row_tile(s, tr)
    w = A_WIDTH
    nsteps = s // tr

    def body(u_ref, v_ref, g_ref, dy_ref, lg_ref, lb_ref, ws_ref, wst_ref, bst_ref,
             dz_ref, dlg_ref, dlb_ref, dws_ref, dbst_ref, dvln_sc, dsv_sc):
        i = pl.program_id(0)

        @pl.when(i == 0)
        def _():
            dws_ref[...] = jnp.zeros_like(dws_ref)
            dsv_sc[...] = jnp.zeros_like(dsv_sc)

        v = v_ref[...].astype(F32)
        mu = jnp.mean(v, axis=-1, keepdims=True)
        xc = v - mu
        rs = lax.rsqrt(jnp.mean(xc * xc, axis=-1, keepdims=True) + EPS)
        xh = xc * rs
        lg = lg_ref[...]
        vln = (xh * lg + lb_ref[...]).astype(BF16)
        mask = _causal_mask()
        mask_t = _causal_mask(transposed=True)
        for grp in range(A_GROUPS):
            cols = slice(grp * CHUNK, (grp + 1) * CHUNK)
            wsm = jnp.where(mask, ws_ref[grp], 0.0).astype(BF16)
            wsm_t = jnp.where(mask_t, wst_ref[grp], 0.0).astype(BF16)
            bcol = bst_ref[:, grp:grp + 1]
            for ci in range(tr // CHUNK):
                rows = slice(ci * CHUNK, (ci + 1) * CHUNK)
                vb = vln[rows, cols]
                sv = jnp.dot(wsm, vb, preferred_element_type=F32) + bcol
                uv = u_ref[rows, cols].astype(F32)
                silu, dsilu = _silu_parts(g_ref[rows, cols].astype(F32))
                dyv = dy_ref[rows, cols].astype(F32)
                dyu = dyv * uv
                dz_ref[rows, cols] = (dyv * sv * silu).astype(BF16)
                dz_ref[rows, 2 * w + grp * CHUNK:2 * w + (grp + 1) * CHUNK] = (dyu * sv * dsilu).astype(BF16)
                dsv = dyu * silu
                dsvb = dsv.astype(BF16)
                dvln_sc[rows, cols] = jnp.dot(wsm_t, dsvb, preferred_element_type=F32)
                dws_ref[grp] += lax.dot_general(dsvb, vb, NT, preferred_element_type=F32)
                dsv_sc[grp] += dsv
        dvln = dvln_sc[...]
        dlg_t = jnp.sum(dvln * xh, axis=0, keepdims=True)
        dlb_t = jnp.sum(dvln, axis=0, keepdims=True)
        a = dvln * lg
        dv = rs * (a - jnp.mean(a, axis=-1, keepdims=True) - xh * jnp.mean(a * xh, axis=-1, keepdims=True))
        dz_ref[:, w:2 * w] = dv.astype(BF16)

        @pl.when(i == 0)
        def _():
            dlg_ref[...] = dlg_t
            dlb_ref[...] = dlb_t

        @pl.when(i > 0)
        def _():
            dlg_ref[...] += dlg_t
            dlb_ref[...] += dlb_t

        @pl.when(i == nsteps - 1)
        def _():
            for grp in range(A_GROUPS):
                dws_ref[grp] = jnp.where(mask, dws_ref[grp], 0.0)
                dbst_ref[:, grp:grp + 1] = jnp.sum(dsv_sc[grp], axis=-1, keepdims=True)

    vec = pl.BlockSpec((1, w), lambda i: (0, 0))
    wsspec = pl.BlockSpec((A_GROUPS, CHUNK, CHUNK), lambda i: (0, 0, 0))
    bsspec = pl.BlockSpec((CHUNK, A_GROUPS), lambda i: (0, 0))
    return _call(
        body, grid=(nsteps,),
        in_specs=[pl.BlockSpec((tr, w), lambda i: (i, 0)), pl.BlockSpec((tr, w), lambda i: (i, 1)),
                  pl.BlockSpec((tr, w), lambda i: (i, 2)), pl.BlockSpec((tr, w), lambda i: (i, 0)),
                  vec, vec, wsspec, wsspec, bsspec],
        out_specs=[pl.BlockSpec((tr, 3 * w), lambda i: (i, 0)), vec, vec, wsspec, bsspec],
        out_shape=[jax.ShapeDtypeStruct((s, 3 * w), BF16), jax.ShapeDtypeStruct((1, w), F32),
                   jax.ShapeDtypeStruct((1, w), F32), jax.ShapeDtypeStruct((A_GROUPS, CHUNK, CHUNK), F32),
                   jax.ShapeDtypeStruct((CHUNK, A_GROUPS), F32)],
        scratch=[pltpu.VMEM((tr, w), F32), pltpu.VMEM((A_GROUPS, CHUNK, CHUNK), F32)],
        args=(z, z, z, dy, ln_g, ln_b, ws, ws_t, bs_t), name="gate_bwd", side=side)


HEADS_PER_BLOCK = 128 // HEAD_DIM
BLOCKS_PER_KV = Q_PER_KV // HEADS_PER_BLOCK
SCALE = HEAD_DIM ** -0.5
LOG2_E = math.log2(math.e)


def _rope_tables(s):
    lane = jnp.arange(128)
    inv_freq = ROPE_THETA ** (-(2 * (lane % (HEAD_DIM // 2))).astype(F32) / HEAD_DIM)
    sign = jnp.where(lane % HEAD_DIM < HEAD_DIM // 2, -1.0, 1.0).astype(F32)
    ang = jnp.arange(s, dtype=F32)[:, None] * inv_freq[None, :]
    return jnp.cos(ang), jnp.sin(ang) * sign[None, :]


def _swap_halves(x):
    n = x.shape[-1]
    lane = lax.broadcasted_iota(jnp.int32, x.shape, x.ndim - 1)
    first = (lane % HEAD_DIM) < (HEAD_DIM // 2)
    return jnp.where(first, pltpu.roll(x, n - HEAD_DIM // 2, x.ndim - 1), pltpu.roll(x, HEAD_DIM // 2, x.ndim - 1))


def _left_half(rows):
    return lax.broadcasted_iota(jnp.int32, (rows, 128), 1) < HEAD_DIM


def _dup_heads(x):
    left = _left_half(x.shape[0])
    swapped = pltpu.roll(x, HEAD_DIM, 1)
    return jnp.concatenate([jnp.where(left, x, swapped), jnp.where(left, swapped, x)], axis=-1)


def _fold_heads(a):
    b0, b1 = a[:, :128], a[:, 128:]
    f0 = b0 + pltpu.roll(b0, HEAD_DIM, 1)
    f1 = b1 + pltpu.roll(b1, HEAD_DIM, 1)
    return jnp.where(_left_half(a.shape[0]), f0, f1)


def _kv_rope(kv, b_kv, cos, sin, *, tr=512):
    s = kv.shape[0]
    tr = _row_tile(s, tr)

    def body(kv_ref, b_ref, c_ref, s_ref, k_ref, v_ref):
        x = kv_ref[...] + b_ref[...]
        k = x[:, :KV_WIDTH]
        k_ref[...] = _dup_heads(k * c_ref[...] + _swap_halves(k) * s_ref[...]).astype(BF16)
        v_ref[...] = _dup_heads(x[:, KV_WIDTH:]).astype(BF16)

    tab = pl.BlockSpec((tr, KV_WIDTH), lambda i: (i, 0))
    wide = pl.BlockSpec((tr, 2 * KV_WIDTH), lambda i: (i, 0))
    outs, _ = _call(body, grid=(s // tr,),
                    in_specs=[wide, pl.BlockSpec((1, 2 * KV_WIDTH), lambda i: (0, 0)), tab, tab],
                    out_specs=[wide, wide], out_shape=[jax.ShapeDtypeStruct((s, 2 * KV_WIDTH), BF16)] * 2,
                    args=(kv, b_kv, cos, sin), name="kv_rope")
    return outs


def _kv_rope_bwd(dk2, dv2, cos, sin, *, tr=512):
    s = dk2.shape[0]
    tr = _row_tile(s, tr)

    def body(dk_ref, dv_ref, c_ref, s_ref, dkv_ref, db_ref):
        i = pl.program_id(0)
        d = _fold_heads(dk_ref[...])
        dk = d * c_ref[...] + _swap_halves(d * s_ref[...])
        dvv = _fold_heads(dv_ref[...])
        dkv_ref[:, :KV_WIDTH] = dk.astype(BF16)
        dkv_ref[:, KV_WIDTH:] = dvv.astype(BF16)
        sk = jnp.sum(dk, axis=0, keepdims=True)
        sv = jnp.sum(dvv, axis=0, keepdims=True)

        @pl.when(i == 0)
        def _():
            db_ref[:, :KV_WIDTH] = sk
            db_ref[:, KV_WIDTH:] = sv

        @pl.when(i > 0)
        def _():
            db_ref[:, :KV_WIDTH] += sk
            db_ref[:, KV_WIDTH:] += sv

    tab = pl.BlockSpec((tr, KV_WIDTH), lambda i: (i, 0))
    wide = pl.BlockSpec((tr, 2 * KV_WIDTH), lambda i: (i, 0))
    outs, _ = _call(body, grid=(s // tr,), in_specs=[wide, wide, tab, tab],
                    out_specs=[wide, pl.BlockSpec((1, 2 * KV_WIDTH), lambda i: (0, 0))],
                    out_shape=[jax.ShapeDtypeStruct((s, 2 * KV_WIDTH), BF16),
                               jax.ShapeDtypeStruct((1, 2 * KV_WIDTH), F32)],
                    args=(dk2, dv2, cos, sin), name="kv_rope_bwd")
    return outs


def _from_previous():
    cols = Q_PER_KV * CHUNK
    k = lax.broadcasted_iota(jnp.int32, (CHUNK, cols), 0)
    q = lax.broadcasted_iota(jnp.int32, (CHUNK, cols), 1) & (CHUNK - 1)
    return k > q


def _fold(x2, prev):
    return jnp.where(prev, x2[:CHUNK], x2[CHUNK:])


def _unfold(x, prev):
    zero = jnp.zeros_like(x)
    return jnp.concatenate([jnp.where(prev, x, zero), jnp.where(prev, zero, x)], axis=0)


def _stack_heads(blocks, left):
    parts = []
    for b in blocks:
        parts.append(jnp.where(left, b, jnp.zeros_like(b)))
        parts.append(jnp.where(left, jnp.zeros_like(b), b))
    return jnp.concatenate(parts, axis=0)


def _unstack_heads(xt):
    top = lax.broadcasted_iota(jnp.int32, (128, CHUNK), 0) < HEAD_DIM
    return [jnp.where(top, xt[:, (2 * b) * CHUNK:(2 * b + 1) * CHUNK], xt[:, (2 * b + 1) * CHUNK:(2 * b + 2) * CHUNK]).T
            for b in range(BLOCKS_PER_KV)]


def _sink_row(sk_ref, kvh):
    return jnp.concatenate([jnp.full((1, CHUNK), sk_ref[0, kvh * Q_PER_KV + r], F32) for r in range(Q_PER_KV)], axis=1)


def _stacked_probs(qs, kd, prev, sink, i):
    sc2 = lax.dot_general(kd, qs, NT, preferred_element_type=F32)
    no_previous = jnp.where(i > 0, 0.0, NEG_BIG)
    sc = jnp.where(prev, sc2[:CHUNK] + no_previous, sc2[CHUNK:])
    sink = sink * (1.0 / SCALE)
    m = jnp.maximum(jnp.max(sc, axis=0, keepdims=True), sink)
    p = jnp.exp2((sc - m) * (SCALE * LOG2_E))
    esink = jnp.exp2((sink - m) * (SCALE * LOG2_E))
    inv = 1.0 / (jnp.sum(p, axis=0, keepdims=True) + esink)
    return p * inv, esink * inv


def _lane_block(b):
    return slice(b * 128, (b + 1) * 128)


def _rope_blocks(zq_ref, bq_ref, cos, sin, kvh):
    out = []
    for b in range(BLOCKS_PER_KV):
        cols = _lane_block(kvh * BLOCKS_PER_KV + b)
        q = zq_ref[:, cols].astype(F32) + bq_ref[:, cols]
        out.append((q * cos + _swap_halves(q) * sin).astype(BF16))
    return out


def _attn_specs():
    qspec = pl.BlockSpec((CHUNK, B_WIDTH), lambda i: (i, 0))
    gspec = pl.BlockSpec((CHUNK, B_WIDTH), lambda i: (i, 1))
    prev = pl.BlockSpec((CHUNK, 2 * KV_WIDTH), lambda i: (jnp.maximum(i - 1, 0), 0))
    cur = pl.BlockSpec((CHUNK, 2 * KV_WIDTH), lambda i: (i, 0))
    tab = pl.BlockSpec((CHUNK, KV_WIDTH), lambda i: (i, 0))
    bq = pl.BlockSpec((1, B_WIDTH), lambda i: (0, 0))
    sinks = pl.BlockSpec(memory_space=pltpu.SMEM)
    return qspec, gspec, prev, cur, tab, bq, sinks


def _attn_fwd(zb, k2, v2, cos, sin, b_bq, sinks, *, side=None):
    s = zb.shape[0]

    def body(zq_ref, zg_ref, kp_ref, kc_ref, vp_ref, vc_ref, c_ref, s_ref, bq_ref, sk_ref, y_ref):
        i = pl.program_id(0)
        cos, sin = c_ref[...], s_ref[...]
        kcat = jnp.concatenate([kp_ref[...], kc_ref[...]], axis=0)
        vcat = jnp.concatenate([vp_ref[...], vc_ref[...]], axis=0)
        prev = _from_previous()
        left = _left_half(CHUNK)
        for kvh in range(N_KV_HEADS):
            qs = _stack_heads(_rope_blocks(zq_ref, bq_ref, cos, sin, kvh), left)
            p, _ = _stacked_probs(qs, kcat[:, _lane_block(kvh)], prev, _sink_row(sk_ref, kvh), i)
            ot = lax.dot_general(vcat[:, _lane_block(kvh)], _unfold(p, prev).astype(BF16), TN,
                                 preferred_element_type=F32)
            for b, ob in enumerate(_unstack_heads(ot)):
                cols = _lane_block(kvh * BLOCKS_PER_KV + b)
                gv = zg_ref[:, cols].astype(F32)
                y_ref[:, cols] = (ob * (gv * jax.nn.sigmoid(gv))).astype(BF16)

    qspec, gspec, prev, cur, tab, bq, sk = _attn_specs()
    (y,), side_outs = _call(body, grid=(s // CHUNK,), in_specs=[qspec, gspec, prev, cur, prev, cur, tab, tab, bq, sk],
                            out_specs=[qspec], out_shape=[jax.ShapeDtypeStruct((s, B_WIDTH), BF16)],
                            args=(zb, zb, k2, k2, v2, v2, cos, sin, b_bq, sinks), name="attn_fwd", side=side)
    return y, side_outs


def _attn_bwd(zb, dyb, k2, v2, cos, sin, b_bq, sinks, *, side=None):
    s = zb.shape[0]

    def body(zq_ref, zg_ref, dy_ref, kp_ref, kc_ref, vp_ref, vc_ref, c_ref, s_ref, bq_ref, sk_ref,
             dz_ref, dk_ref, dv_ref, dbq_ref, dsk_ref):
        i = pl.program_id(0)

        @pl.when(i == 0)
        def _():
            dk_ref[...] = jnp.zeros_like(dk_ref)
            dv_ref[...] = jnp.zeros_like(dv_ref)
            dbq_ref[...] = jnp.zeros_like(dbq_ref)
            dsk_ref[...] = jnp.zeros_like(dsk_ref)

        cos, sin = c_ref[...], s_ref[...]
        kcat = jnp.concatenate([kp_ref[...], kc_ref[...]], axis=0)
        vcat = jnp.concatenate([vp_ref[...], vc_ref[...]], axis=0)
        prev = _from_previous()
        left = _left_half(CHUNK)
        lane = lax.broadcasted_iota(jnp.int32, (1, 128), 1)
        dsk_row = jnp.zeros((1, 128), F32)
        cur_rows = pl.ds(pl.multiple_of(i * CHUNK, CHUNK), CHUNK)
        for kvh in range(N_KV_HEADS):
            kd, vd = kcat[:, _lane_block(kvh)], vcat[:, _lane_block(kvh)]
            qs = _stack_heads(_rope_blocks(zq_ref, bq_ref, cos, sin, kvh), left)
            p, psink = _stacked_probs(qs, kd, prev, _sink_row(sk_ref, kvh), i)
            pb = _unfold(p, prev).astype(BF16)
            ot = lax.dot_general(vd, pb, TN, preferred_element_type=F32)
            gates, dys = [], []
            for b in range(BLOCKS_PER_KV):
                cols = _lane_block(kvh * BLOCKS_PER_KV + b)
                gates.append(_silu_parts(zg_ref[:, cols].astype(F32)))
                dys.append(dy_ref[:, cols].astype(F32))
            dos = _stack_heads([(dyv * silu).astype(BF16) for dyv, (silu, _) in zip(dys, gates)], left)
            dp = _fold(lax.dot_general(vd, dos, NT, preferred_element_type=F32), prev)
            delta = jnp.sum(p * dp, axis=0, keepdims=True)
            ds = _unfold(p * (dp - delta) * SCALE, prev).astype(BF16)
            dqt = lax.dot_general(kd, ds, TN, preferred_element_type=F32)
            dk_part = jnp.dot(ds, qs, preferred_element_type=F32)
            dv_part = jnp.dot(pb, dos, preferred_element_type=F32)
            dk_ref[cur_rows, _lane_block(kvh)] += dk_part[CHUNK:]
            dv_ref[cur_rows, _lane_block(kvh)] += dv_part[CHUNK:]

            @pl.when(i > 0)
            def _(kvh=kvh, dk_part=dk_part, dv_part=dv_part):
                prev_rows = pl.ds(pl.multiple_of((i - 1) * CHUNK, CHUNK), CHUNK)
                dk_ref[prev_rows, _lane_block(kvh)] += dk_part[:CHUNK]
                dv_ref[prev_rows, _lane_block(kvh)] += dv_part[:CHUNK]

            sink_grad = psink * delta
            for r in range(Q_PER_KV):
                dsink = -jnp.sum(sink_grad[:, r * CHUNK:(r + 1) * CHUNK], axis=1, keepdims=True)
                dsk_row = dsk_row + jnp.where(lane == kvh * Q_PER_KV + r, dsink, 0.0)
            blocks = zip(_unstack_heads(ot), _unstack_heads(dqt), dys, gates)
            for b, (ob, dqr, dyv, (_, dsilu)) in enumerate(blocks):
                blk = kvh * BLOCKS_PER_KV + b
                dq = dqr * cos + _swap_halves(dqr * sin)
                dbq_ref[:, _lane_block(blk)] += jnp.sum(dq, axis=0, keepdims=True)
                dz_ref[:, _lane_block(blk)] = dq.astype(BF16)
                dz_ref[:, _lane_block(B_WIDTH // 128 + blk)] = (dyv * ob * dsilu).astype(BF16)
        dsk_ref[0:1, :] += dsk_row

    qspec, gspec, prev, cur, tab, bq, sk = _attn_specs()
    full = pl.BlockSpec((s, 2 * KV_WIDTH), lambda i: (0, 0))
    return _call(
        body, grid=(s // CHUNK,),
        in_specs=[qspec, gspec, qspec, prev, cur, prev, cur, tab, tab, bq, sk],
        out_specs=[pl.BlockSpec((CHUNK, 2 * B_WIDTH), lambda i: (i, 0)), full, full, bq,
                   pl.BlockSpec((8, 128), lambda i: (0, 0))],
        out_shape=[jax.ShapeDtypeStruct((s, 2 * B_WIDTH), BF16), jax.ShapeDtypeStruct((s, 2 * KV_WIDTH), F32),
                   jax.ShapeDtypeStruct((s, 2 * KV_WIDTH), F32), jax.ShapeDtypeStruct((1, B_WIDTH), F32),
                   jax.ShapeDtypeStruct((8, 128), F32)],
        args=(zb, zb, dyb, k2, k2, v2, v2, cos, sin, b_bq, sinks), name="attn_bwd", side=side)


def _place():
    x, y, c = lax.axis_index("x"), lax.axis_index("y"), lax.axis_index("c")
    return x, y, c, [(1 - x, y), (x, 1 - y), (1 - x, 1 - y)]


def _relations():
    return [(r >> 2 & 1, r >> 1 & 1, r & 1) for r in range(1, 8)]


def _gather_side(arrs):
    n = len(arrs)

    def copies(ins, outs, sems):
        send_ici, recv_ici, send_d2d, recv_d2d, local_sem = sems
        x, y, c, chips = _place()
        me = 2 * x + y

        def rows(a, half):
            hr = arrs[a].shape[0] // 2
            return pl.ds(half * hr, hr)

        def ici(a, j, src_chip, to):
            return pltpu.make_async_remote_copy(
                src_ref=ins[a].at[rows(a, c)], dst_ref=outs[a].at[src_chip, rows(a, c)],
                send_sem=send_ici.at[a, j], recv_sem=recv_ici.at[a, j], device_id=to, device_id_type=MESH)

        def d2d(a, j, chip, half):
            blk = outs[a].at[chip, rows(a, half)]
            return pltpu.make_async_remote_copy(
                src_ref=blk, dst_ref=blk, send_sem=send_d2d.at[a, j], recv_sem=recv_d2d.at[a, j],
                device_id=(x, y, 1 - c), device_id_type=MESH)

        local = [pltpu.make_async_copy(ins[a], outs[a].at[me], local_sem.at[a]) for a in range(n)]
        pairs = [(a, j, chip) for a in range(n) for j, chip in enumerate(chips)]
        return c, me, local, ici, d2d, pairs

    def start(ins, outs, sems):
        c, me, local, ici, _, pairs = copies(ins, outs, sems)
        for cp in local:
            cp.start()
        for a, j, chip in pairs:
            ici(a, j, me, (*chip, c)).start()

    def passing(ins, outs, sems):
        c, _, _, ici, d2d, pairs = copies(ins, outs, sems)
        for a, j, (px, py) in pairs:
            ici(a, j, 2 * px + py, (px, py, c)).wait_recv()
            d2d(a, j, 2 * px + py, c).start()

    def finish(ins, outs, sems):
        c, me, local, ici, d2d, pairs = copies(ins, outs, sems)
        for a, j, (px, py) in pairs:
            d2d(a, j, 2 * px + py, 1 - c).wait_recv()
        for a, j, (px, py) in pairs:
            ici(a, j, me, (px, py, c)).wait_send()
            d2d(a, j, 2 * px + py, c).wait_send()
        for cp in local:
            cp.wait()

    return _Side(arrs, [jax.ShapeDtypeStruct((N_CHIPS,) + a.shape, a.dtype) for a in arrs],
                 [pltpu.SemaphoreType.DMA((n, 3))] * 4 + [pltpu.SemaphoreType.DMA((n,))], start, finish,
                 passing=passing)


def _exchange_side(grads):
    n = len(grads)

    def copies(ins, outs, sems):
        send_sem, recv_sem = sems
        x, y, c, _ = _place()
        cps = []
        for a in range(n):
            hr = grads[a].shape[1] // 2
            cps.append(pltpu.make_async_remote_copy(
                src_ref=ins[a].at[:, pl.ds((1 - c) * hr, hr), :], dst_ref=outs[a],
                send_sem=send_sem.at[a], recv_sem=recv_sem.at[a], device_id=(x, y, 1 - c), device_id_type=MESH))
        return cps

    def start(ins, outs, sems):
        for cp in copies(ins, outs, sems):
            cp.start()

    def finish(ins, outs, sems):
        for cp in copies(ins, outs, sems):
            cp.wait()

    return _Side(grads, [jax.ShapeDtypeStruct((g.shape[0], g.shape[1] // 2, g.shape[2]), g.dtype) for g in grads],
                 [pltpu.SemaphoreType.DMA((n,))] * 2, start, finish)


def _scatter_side(chip_sums, small=None):
    n = len(chip_sums)
    arrs = list(chip_sums) + ([small] if small is not None else [])

    def copies(ins, outs, sems):
        x, y, c, chips = _place()
        cps = []
        for a in range(n):
            for j, (px, py) in enumerate(chips):
                cps.append(pltpu.make_async_remote_copy(
                    src_ref=ins[a].at[2 * px + py], dst_ref=outs[a].at[j],
                    send_sem=sems[0].at[a, j], recv_sem=sems[1].at[a, j], device_id=(px, py, c), device_id_type=MESH))
        if small is not None:
            for r, (fx, fy, fc) in enumerate(_relations(), start=1):
                px, py, pc = x ^ fx, y ^ fy, c ^ fc
                cps.append(pltpu.make_async_remote_copy(
                    src_ref=ins[n].at[4 * px + 2 * py + pc], dst_ref=outs[n].at[r],
                    send_sem=sems[2].at[r - 1], recv_sem=sems[3].at[r - 1], device_id=(px, py, pc),
                    device_id_type=MESH))
        return cps

    def start(ins, outs, sems):
        for cp in copies(ins, outs, sems):
            cp.start()

    def finish(ins, outs, sems):
        for cp in copies(ins, outs, sems):
            cp.wait()

    shapes = [jax.ShapeDtypeStruct((3,) + t.shape[1:], t.dtype) for t in chip_sums]
    sems = [pltpu.SemaphoreType.DMA((n, 3))] * 2
    if small is not None:
        shapes.append(jax.ShapeDtypeStruct(small.shape, small.dtype))
        sems += [pltpu.SemaphoreType.DMA((7,))] * 2
    return _Side(arrs, shapes, sems, start, finish)


def _small_scatter_side(small):
    def copies(ins, outs, sems):
        x, y, c, _ = _place()
        cps = []
        for r, (fx, fy, fc) in enumerate(_relations(), start=1):
            px, py, pc = x ^ fx, y ^ fy, c ^ fc
            cps.append(pltpu.make_async_remote_copy(
                src_ref=ins[0].at[4 * px + 2 * py + pc], dst_ref=outs[0].at[r],
                send_sem=sems[0].at[r - 1], recv_sem=sems[1].at[r - 1], device_id=(px, py, pc), device_id_type=MESH))
        return cps

    def start(ins, outs, sems):
        for cp in copies(ins, outs, sems):
            cp.start()

    def finish(ins, outs, sems):
        for cp in copies(ins, outs, sems):
            cp.wait()

    return _Side([small], [jax.ShapeDtypeStruct(small.shape, small.dtype)], [pltpu.SemaphoreType.DMA((7,))] * 2,
                 start, finish)


def _small_share_side(small):
    return _share_side([], small)


def _share_side(halves, small=None):
    n = len(halves)
    arrs = list(halves) + ([small] if small is not None else [])

    def copies(ins, outs, sems, mine):
        x, y, c, _ = _place()
        me = 4 * x + 2 * y + c
        cps = []
        for a in range(n):
            hr = halves[a].shape[0] // 2
            rows = pl.ds((c if mine else 1 - c) * hr, hr)
            cps.append(pltpu.make_async_remote_copy(
                src_ref=ins[a].at[rows], dst_ref=outs[a].at[rows],
                send_sem=sems[0].at[a], recv_sem=sems[1].at[a], device_id=(x, y, 1 - c), device_id_type=MESH))
        if small is not None:
            for r, (fx, fy, fc) in enumerate(_relations(), start=1):
                px, py, pc = x ^ fx, y ^ fy, c ^ fc
                seg = me if mine else 4 * px + 2 * py + pc
                cps.append(pltpu.make_async_remote_copy(
                    src_ref=ins[n].at[seg], dst_ref=outs[n].at[seg],
                    send_sem=sems[-2].at[r - 1], recv_sem=sems[-1].at[r - 1], device_id=(px, py, pc),
                    device_id_type=MESH))
        return cps

    def start(ins, outs, sems):
        for cp in copies(ins, outs, sems, True):
            cp.start()

    def finish(ins, outs, sems):
        for cp in copies(ins, outs, sems, False):
            cp.wait_recv()
        for cp in copies(ins, outs, sems, True):
            cp.wait_send()

    sems = ([pltpu.SemaphoreType.DMA((n,))] * 2 if n else []) + (
        [pltpu.SemaphoreType.DMA((7,))] * 2 if small is not None else [])
    return _Side(arrs, [jax.ShapeDtypeStruct(h.shape, h.dtype) for h in arrs], sems, start, finish,
                 aliases={i: i for i in range(len(arrs))})


GATHER_PIECES = [(0, 0), (0, 1), (1, 0), (2, 0), (1, 1), (2, 1), (3, 0), (3, 1)]


def _mm_gathering(a, shard, order, *, name, tm=1024):
    s, k = a.shape
    nc = shard.shape[1]
    tm = _row_tile(s, tm)
    tn = nc // 2
    hr = k // 2
    qr = hr // 2
    blocks = jnp.stack([order[src] * 2 + h for src, h in GATHER_PIECES]).astype(jnp.int32)

    def body(blocks_ref, a_ref, shard_ref, z_ref, full_ref, wbuf, send_ici, recv_ici, send_relay,
             recv_relay, send_d2d, recv_d2d, local_sem, load_sem):
        piece, i = pl.program_id(0), pl.program_id(1)
        x, y, c, chips = _place()
        me = 2 * x + y
        nbrs = chips[:2]
        chip_of = [2 * px + py for px, py in chips]

        def quarter(q):
            return pl.ds(c * hr + q * qr, qr)

        def sibling_quarter(q):
            return pl.ds((1 - c) * hr + q * qr, qr)

        def whole(half):
            return pl.ds(half * hr, hr)

        def cols(h):
            return pl.ds(h * tn, tn)

        def direct(j, src_chip, h):
            return pltpu.make_async_remote_copy(
                src_ref=shard_ref.at[whole(c), cols(h)], dst_ref=full_ref.at[src_chip, whole(c), cols(h)],
                send_sem=send_ici.at[j, h], recv_sem=recv_ici.at[j, h], device_id=(*nbrs[j], c), device_id_type=MESH)

        def relay(j, src_chip, h):
            blk = full_ref.at[src_chip, quarter(j), cols(h)]
            return pltpu.make_async_remote_copy(
                src_ref=blk, dst_ref=blk, send_sem=send_relay.at[j, h], recv_sem=recv_relay.at[j, h],
                device_id=(*nbrs[1 - j], c), device_id_type=MESH)

        def d2d(j, chip, rows, h):
            blk = full_ref.at[chip, rows, cols(h)]
            return pltpu.make_async_remote_copy(
                src_ref=blk, dst_ref=blk, send_sem=send_d2d.at[j, h], recv_sem=recv_d2d.at[j, h],
                device_id=(x, y, 1 - c), device_id_type=MESH)

        def load(p):
            src, h = GATHER_PIECES[p]
            where = shard_ref if src == 0 else full_ref.at[chip_of[src - 1]]
            return pltpu.make_async_copy(where.at[:, cols(h)], wbuf.at[p % 2], load_sem.at[p % 2])

        local = pltpu.make_async_copy(shard_ref, full_ref.at[me], local_sem)

        def arrived(p):
            src, h = GATHER_PIECES[p]
            if src in (1, 2):
                j = src - 1
                direct(j, chip_of[j], h).wait_recv()
                relay(j, chip_of[j], h).start()
                d2d(j, chip_of[j], whole(c), h).start()
            elif src == 3:
                for j in range(2):
                    relay(1 - j, chip_of[2], h).wait_recv()
                    d2d(2 + j, chip_of[2], quarter(1 - j), h).start()

        def fetch(p):
            src, h = GATHER_PIECES[p]
            if src in (1, 2):
                d2d(src - 1, chip_of[src - 1], whole(1 - c), h).wait_recv()
            elif src == 3:
                for j in range(2):
                    d2d(2 + j, chip_of[2], sibling_quarter(1 - j), h).wait_recv()
            load(p).start()

        n_i = s // tm
        for p in range(len(GATHER_PIECES)):
            @pl.when(jnp.logical_and(piece == p, i == 0))
            def _(p=p):
                if p == 0:
                    local.start()
                    for hh in range(2):
                        for j in range(2):
                            direct(j, me, hh).start()
                    load(0).start()
                load(p).wait()

        z_ref[...] = jnp.dot(a_ref[...], wbuf[piece % 2], preferred_element_type=F32).astype(z_ref.dtype)

        for p in range(len(GATHER_PIECES) - 1):
            @pl.when(jnp.logical_and(piece == p, i == min(1, n_i - 1)))
            def _(p=p):
                arrived(p + 1)

            @pl.when(jnp.logical_and(piece == p, i == min(2, n_i - 1)))
            def _(p=p):
                fetch(p + 1)

        last = jnp.logical_and(piece == len(GATHER_PIECES) - 1, i == n_i - 1)

        @pl.when(last)
        def _():
            for h in range(2):
                for j in range(2):
                    direct(j, me, h).wait_send()
                    relay(j, chip_of[j], h).wait_send()
                    d2d(j, chip_of[j], whole(c), h).wait_send()
                    d2d(2 + j, chip_of[2], quarter(1 - j), h).wait_send()
            local.wait()

    return pl.pallas_call(
        body,
        grid_spec=pltpu.PrefetchScalarGridSpec(
            num_scalar_prefetch=1, grid=(len(GATHER_PIECES), s // tm),
            in_specs=[pl.BlockSpec((tm, k), lambda p, i, blocks: (i, 0)), HBM],
            out_specs=[pl.BlockSpec((tm, tn), lambda p, i, blocks: (i, blocks[p])), HBM],
            scratch_shapes=[pltpu.VMEM((2, k, tn), BF16)] + [pltpu.SemaphoreType.DMA((2, 2))] * 4
            + [pltpu.SemaphoreType.DMA((4, 2))] * 2 + [pltpu.SemaphoreType.DMA, pltpu.SemaphoreType.DMA((2,))]),
        out_shape=[jax.ShapeDtypeStruct((s, N_CHIPS * nc), BF16), jax.ShapeDtypeStruct((N_CHIPS, k, nc), BF16)],
        name=name, compiler_params=_cparams(),
    )(blocks, a, shard)


def _mm_tn_exchanging(a, b, *, name, shards, tk=2048, side=None):
    s, m = a.shape
    nc = b.shape[1] // shards
    tk = _row_tile(s, tk)
    nk = s // tk
    hm = m // 2

    def body(a_ref, b_ref, part_ref, sib_ref, acc, keep_sem, send_sem, recv_sem):
        j, kk = pl.program_id(0), pl.program_id(1)
        x, y, c, _ = _place()

        def keep(jj, slot):
            mine = pl.ds(c * hm, hm)
            return pltpu.make_async_copy(acc.at[slot, mine], part_ref.at[jj], keep_sem.at[slot])

        def give(jj, slot):
            return pltpu.make_async_remote_copy(
                src_ref=acc.at[slot, pl.ds((1 - c) * hm, hm)], dst_ref=sib_ref.at[jj],
                send_sem=send_sem.at[slot], recv_sem=recv_sem.at[jj], device_id=(x, y, 1 - c), device_id_type=MESH)

        part = lax.dot_general(a_ref[...], b_ref[...], TN, preferred_element_type=F32)
        for slot in range(2):
            @pl.when(j % 2 == slot)
            def _(slot=slot):
                @pl.when(jnp.logical_and(kk == 0, j >= 2))
                def _():
                    keep(j - 2, slot).wait()
                    give(j - 2, slot).wait_send()

                @pl.when(kk == 0)
                def _():
                    acc[slot] = part

                @pl.when(kk > 0)
                def _():
                    acc[slot] += part

                @pl.when(kk == nk - 1)
                def _():
                    keep(j, slot).start()
                    give(j, slot).start()

        @pl.when(jnp.logical_and(j == shards - 1, kk == nk - 1))
        def _():
            for jj in range(shards - 2, shards):
                keep(jj, jj % 2).wait()
                give(jj, jj % 2).wait_send()
            for jj in range(shards):
                give(jj, jj % 2).wait_recv()

    assert shards >= 2
    return _call(
        body, grid=(shards, nk),
        in_specs=[pl.BlockSpec((tk, m), lambda j, kk: (kk, 0)), pl.BlockSpec((tk, nc), lambda j, kk: (kk, j))],
        out_specs=[HBM, HBM],
        out_shape=[jax.ShapeDtypeStruct((shards, hm, nc), F32), jax.ShapeDtypeStruct((shards, hm, nc), F32)],
        scratch=[pltpu.VMEM((2, m, nc), F32), pltpu.SemaphoreType.DMA((2,)), pltpu.SemaphoreType.DMA((2,)),
                 pltpu.SemaphoreType.DMA((shards,))],
        args=(a, b), name=name, side=side)


def _col_tile(cols):
    return cols if cols <= 2048 else 512


def _add_sibling(grad, recv, core, *, name):
    k, r, c = grad.shape
    hr = r // 2
    tr = min(hr, 256)
    tc = _col_tile(c)
    nrb = hr // tr

    def body(core_ref, g_ref, r_ref, o_ref):
        o_ref[...] = (g_ref[...] + r_ref[...]).astype(BF16)

    return pl.pallas_call(
        body,
        grid_spec=pltpu.PrefetchScalarGridSpec(
            num_scalar_prefetch=1, grid=(k, nrb, c // tc),
            in_specs=[pl.BlockSpec((None, tr, tc), lambda kk, i, j, core: (kk, core[0] * nrb + i, j)),
                      pl.BlockSpec((None, tr, tc), lambda kk, i, j, core: (kk, i, j))],
            out_specs=pl.BlockSpec((None, tr, tc), lambda kk, i, j, core: (kk, i, j))),
        out_shape=jax.ShapeDtypeStruct((k, hr, c), BF16), name=name, compiler_params=_cparams(),
    )(core, grad, recv)


def _sum_chips(grad, from_sibling, recv, place, *, name):
    _, hr, c = from_sibling.shape
    tr = min(hr, 256)
    tc = _col_tile(c)
    nrb = hr // tr

    def body(place_ref, g_ref, s_ref, r0_ref, r1_ref, r2_ref, o_ref):
        own = g_ref[...] + s_ref[...]
        o_ref[...] = ((own + r0_ref[...].astype(F32)) + r1_ref[...].astype(F32)) + r2_ref[...].astype(F32)

    def rspec(j):
        return pl.BlockSpec((None, tr, tc), lambda i, jj, place: (j, i, jj))

    return pl.pallas_call(
        body,
        grid_spec=pltpu.PrefetchScalarGridSpec(
            num_scalar_prefetch=1, grid=(nrb, c // tc),
            in_specs=[pl.BlockSpec((None, tr, tc), lambda i, jj, place: (place[0], place[1] * nrb + i, jj)),
                      pl.BlockSpec((None, tr, tc), lambda i, jj, place: (place[0], i, jj)),
                      rspec(0), rspec(1), rspec(2)],
            out_specs=pl.BlockSpec((tr, tc), lambda i, jj, place: (place[1] * nrb + i, jj))),
        out_shape=jax.ShapeDtypeStruct((2 * hr, c), F32), name=name, compiler_params=_cparams(),
    )(place, grad, from_sibling, recv, recv, recv)


def _add_halves(mine, theirs, *, name, side=None):
    k, hr, c = mine.shape
    tr = min(hr, 256)
    tc = _col_tile(c)

    def body(a_ref, b_ref, o_ref):
        o_ref[...] = (a_ref[...] + b_ref[...]).astype(BF16)

    spec = pl.BlockSpec((None, tr, tc), lambda kk, i, j: (kk, i, j))
    (out,), side_outs = _call(body, grid=(k, hr // tr, c // tc), in_specs=[spec, spec], out_specs=[spec],
                              out_shape=[jax.ShapeDtypeStruct((k, hr, c), BF16)], args=(mine, theirs), name=name,
                              side=side)
    return out, side_outs


def _sum_halves(mine, theirs, recv, place, *, name):
    _, hr, c = mine.shape
    tr = min(hr, 256)
    tc = _col_tile(c)
    nrb = hr // tr

    def body(place_ref, a_ref, b_ref, r0_ref, r1_ref, r2_ref, o_ref):
        own = a_ref[...] + b_ref[...]
        o_ref[...] = ((own + r0_ref[...].astype(F32)) + r1_ref[...].astype(F32)) + r2_ref[...].astype(F32)

    def rspec(j):
        return pl.BlockSpec((None, tr, tc), lambda i, jj, place: (j, i, jj))

    own_spec = pl.BlockSpec((None, tr, tc), lambda i, jj, place: (place[0], i, jj))
    return pl.pallas_call(
        body,
        grid_spec=pltpu.PrefetchScalarGridSpec(
            num_scalar_prefetch=1, grid=(nrb, c // tc),
            in_specs=[own_spec, own_spec, rspec(0), rspec(1), rspec(2)],
            out_specs=pl.BlockSpec((tr, tc), lambda i, jj, place: (place[1] * nrb + i, jj))),
        out_shape=jax.ShapeDtypeStruct((2 * hr, c), F32), name=name, compiler_params=_cparams(),
    )(place, mine, theirs, recv, recv, recv)


def _sum_small(small, recv, place):
    _, sr, _ = small.shape

    def body(place_ref, own_ref, r_ref, o_ref):
        acc = own_ref[...]
        for r in range(1, 8):
            acc = acc + r_ref[r]
        o_ref[...] = acc

    return pl.pallas_call(
        body,
        grid_spec=pltpu.PrefetchScalarGridSpec(
            num_scalar_prefetch=1, grid=(1,),
            in_specs=[pl.BlockSpec((None, sr, 128), lambda i, place: (place[2], 0, 0)),
                      pl.BlockSpec((8, sr, 128), lambda i, place: (0, 0, 0))],
            out_specs=pl.BlockSpec((None, sr, 128), lambda i, place: (place[2], 0, 0))),
        out_shape=jax.ShapeDtypeStruct(small.shape, F32), name="sum_small", compiler_params=_cparams(),
    )(place, small, recv)


def _spread_side(vec):
    def copies(ins, outs, sems):
        x, y, c, _ = _place()
        return [pltpu.make_async_remote_copy(
            src_ref=ins[0], dst_ref=outs[0].at[r], send_sem=sems[0].at[r - 1], recv_sem=sems[1].at[r - 1],
            device_id=(x ^ fx, y ^ fy, c ^ fc), device_id_type=MESH)
            for r, (fx, fy, fc) in enumerate(_relations(), start=1)]

    def start(ins, outs, sems):
        for cp in copies(ins, outs, sems):
            cp.start()

    def finish(ins, outs, sems):
        for cp in copies(ins, outs, sems):
            cp.wait()

    return _Side([vec], [jax.ShapeDtypeStruct((8,) + vec.shape, vec.dtype)], [pltpu.SemaphoreType.DMA((7,))] * 2,
                 start, finish)


def _sum_in_device_order(own, spread, place):
    def body(place_ref, own_ref, r_ref, o_ref):
        me = place_ref[2]
        acc = jnp.zeros_like(own_ref[...])
        for d in range(8):
            slot = jnp.where(me == d, 1, me ^ d)
            acc = acc + jnp.where(me == d, own_ref[...], r_ref[slot])
        o_ref[...] = acc

    return pl.pallas_call(
        body,
        grid_spec=pltpu.PrefetchScalarGridSpec(
            num_scalar_prefetch=1, grid=(1,),
            in_specs=[pl.BlockSpec(own.shape, lambda i, place: (0, 0)),
                      pl.BlockSpec(spread.shape, lambda i, place: (0, 0, 0))],
            out_specs=pl.BlockSpec(own.shape, lambda i, place: (0, 0))),
        out_shape=jax.ShapeDtypeStruct(own.shape, F32), name="sum_in_device_order", compiler_params=_cparams(),
    )(place, own, spread)


def _adamw(w, g, m, v, *, name):
    r, c = w.shape
    tr = 256 if r % 256 == 0 else r
    tc = _col_tile(c)
    bc1 = 1.0 - ADAM_B1 ** ADAM_STEP
    bc2 = 1.0 - ADAM_B2 ** ADAM_STEP

    def body(w_ref, g_ref, m_ref, v_ref, d_ref, nm_ref, nv_ref, gout_ref):
        gv = g_ref[...]
        nm = ADAM_B1 * m_ref[...] + (1.0 - ADAM_B1) * gv
        nv = ADAM_B2 * v_ref[...] + (1.0 - ADAM_B2) * (gv * gv)
        d_ref[...] = -ADAM_LR * ((nm / bc1) / (jnp.sqrt(nv / bc2) + ADAM_EPS) + ADAM_WD * w_ref[...])
        nm_ref[...] = nm
        nv_ref[...] = nv
        gout_ref[...] = gv

    spec = pl.BlockSpec((tr, tc), lambda i, j: (i, j))
    outs, _ = _call(body, grid=(r // tr, c // tc), in_specs=[spec] * 4, out_specs=[spec] * 4,
                    out_shape=[jax.ShapeDtypeStruct((r, c), F32)] * 4, args=(w, g, m, v), name=name)
    return outs


SMALL_ORDER = ["a_ws", "a_bs", "a_norm_g", "a_ln_g", "a_ln_b", "kv_norm_g", "b_kv", "b_norm_g", "b_bq",
               "b_sinks", "final_norm_g"]
SHARDED_SMALL = {"a_norm_g", "a_ln_g", "a_ln_b"}
PACK_TILE = 8 * 128


def _rows128(a):
    flat = a.reshape(-1)
    return jnp.pad(flat, (0, (-flat.shape[0]) % PACK_TILE)).reshape(-1, 128)


def _pack_rows(parts, multiple):
    rows = [_rows128(p) for p in parts]
    total = sum(r.shape[0] for r in rows)
    pad = (-total) % multiple
    if pad:
        rows.append(jnp.zeros((pad, 128), rows[0].dtype))
    return jnp.concatenate(rows, axis=0)


def _unpack_rows(packed, shapes):
    out, row = [], 0
    for shp in shapes:
        size = math.prod(shp)
        nrow = -(-size // PACK_TILE) * 8
        out.append(packed[row:row + nrow].reshape(-1)[:size].reshape(shp))
        row += nrow
    return out


WEIGHTS = ["a_norm_g", "a_w_in", "a_ln_g", "a_ln_b", "a_ws", "a_bs", "a_w_out", "kv_norm_g", "w_kv", "b_kv",
           "b_norm_g", "b_w_in", "b_bq", "b_sinks", "b_w_out", "final_norm_g"]
BIG = ["a_w_in", "a_w_out", "w_kv", "b_w_in", "b_w_out"]


class _Reduction:
    def __init__(self, names, partials, core, place, small=None):
        self.names, self.partials, self.core, self.place, self.small = names, partials, core, place, small

    def exchange_side(self):
        return _exchange_side(self.partials)

    def took_exchange(self, from_sibling):
        self.from_sibling = from_sibling
        self.chip_sums = [_add_sibling(g, r, self.core, name="add_sibling_" + n)
                          for g, r, n in zip(self.partials, from_sibling, self.names)]

    def scatter_side(self):
        return _scatter_side(self.chip_sums, self.small)

    def took_scatter(self, arrived):
        big = arrived[:len(self.names)]
        self.halves = [_sum_chips(g, fs, r, self.place, name="sum_chips_" + n)
                       for g, fs, r, n in zip(self.partials, self.from_sibling, big, self.names)]
        self.small_mine = _sum_small(self.small, arrived[-1], self.place) if self.small is not None else None

    def share_side(self):
        return _share_side(self.halves, self.small_mine)

    def took_share(self, shared):
        self.grads = dict(zip(self.names, shared[:len(self.names)]))
        self.small_full = shared[-1] if self.small is not None else None


def _step(x, loss_target, p, m, v):
    xi, yi, ci = lax.axis_index("x"), lax.axis_index("y"), lax.axis_index("c")
    chip = 2 * xi + yi
    device = 4 * xi + 2 * yi + ci
    core = jnp.reshape(ci, (1,)).astype(jnp.int32)
    place = jnp.stack([chip, ci, device]).astype(jnp.int32)
    x, tgt = x[0], loss_target[0]
    s = x.shape[0]
    cos, sin = _rope_tables(s)

    shard2d = {n: p[n].reshape(p[n].shape[-2:]) for n in BIG}
    shard_bf = {n: shard2d[n].astype(BF16) for n in BIG}
    ws = p["a_ws"][0]
    ws_t = jnp.swapaxes(ws, 1, 2)
    bs_t = p["a_bs"][0].T
    kv_norm_g, b_kv = p["kv_norm_g"].reshape(1, -1), p["b_kv"].reshape(1, -1)
    final_norm_g = p["final_norm_g"].reshape(1, -1)

    vec_shapes = [p[n].shape for n in ("a_norm_g", "a_ln_g", "a_ln_b")]
    vec_pack = _pack_rows([p["a_norm_g"], p["a_ln_g"], p["a_ln_b"]], 16)
    (vec_all,) = _comm_call(_gather_side([vec_pack]), "gather_vectors")
    vecs = [_unpack_rows(vec_all[k], vec_shapes) for k in range(N_CHIPS)]
    a_norm_g, a_ln_g, a_ln_b = (jnp.concatenate([vk[t] for vk in vecs], axis=-1) for t in range(3))

    (n_a,) = _rms_fwd(x, [a_norm_g], name="rms_a")
    order = jnp.stack([chip, 2 * (1 - xi) + yi, 2 * xi + (1 - yi), 2 * (1 - xi) + (1 - yi)]).astype(jnp.int32)
    z, a_w_in = _mm_gathering(n_a, shard_bf["a_w_in"], order, name="mm_a_in")
    y, (a_w_out,) = _gate_fwd(z, a_ln_g, a_ln_b, ws, bs_t, side=_gather_side([shard_bf["a_w_out"]]))
    a_w_out = a_w_out.reshape(A_WIDTH, D_MODEL)
    (h1, n_kv, n_b), (w_kv, b_w_in) = _mm_residual_norms(
        y, a_w_out, x, [kv_norm_g, p["b_norm_g"]], name="mm_a_out",
        side=_gather_side([shard_bf["w_kv"], shard_bf["b_w_in"]]))
    w_kv = w_kv.reshape(D_MODEL, 2 * KV_WIDTH)
    kv = _mm_nn(n_kv, w_kv, name="mm_kv", tn=2 * KV_WIDTH)
    kr, vv = _kv_rope(kv, b_kv, cos, sin)
    zb = _mm_nn(n_b, b_w_in, name="mm_b_in", tn=512, tm=1024, out_dtype=BF16)
    yb, (b_w_out,) = _attn_fwd(zb, kr, vv, cos, sin, p["b_bq"], p["b_sinks"], side=_gather_side([shard_bf["b_w_out"]]))
    b_w_out = b_w_out.reshape(B_WIDTH, D_MODEL)
    loss_blk, dh2, dh2b, d_final_g = _mm_residual_loss(yb, b_w_out, h1, tgt, final_norm_g, name="mm_b_out")

    d_b_w_out = _mm_tn(yb, dh2b, name="mm_d_b_w_out", tm=B_WIDTH, tn=D_MODEL)
    red_bo = _Reduction(["b_w_out"], [d_b_w_out.reshape(N_CHIPS, B_WIDTH // N_CHIPS, D_MODEL)], core, place)
    dyb, got = _mm_nt(dh2b, b_w_out, name="mm_dyb", out_dtype=BF16, side=red_bo.exchange_side())
    red_bo.took_exchange(got)
    (dzb, dk_rot, dv, d_bq, d_sinks), got = _attn_bwd(zb, dyb, kr, vv, cos, sin, p["b_bq"], p["b_sinks"],
                                                      side=red_bo.scatter_side())
    red_bo.took_scatter(got)
    dkv, d_b_kv = _kv_rope_bwd(dk_rot, dv, cos, sin)
    d_b_w_in = _mm_tn(n_b, dzb, name="mm_d_b_w_in", tm=D_MODEL, tn=512, shards=N_CHIPS)
    d_w_kv, got = _mm_tn(n_kv, dkv, name="mm_d_w_kv", tm=D_MODEL, tn=2 * KV_WIDTH, side=red_bo.share_side())
    red_bo.took_share(got)
    red_bi = _Reduction(["b_w_in", "w_kv"], [d_b_w_in, d_w_kv.reshape(N_CHIPS, D_MODEL // N_CHIPS, 2 * KV_WIDTH)],
                        core, place)
    (dh1, dh1b, d_kv_g, d_b_g), got = _mm_nt_rms_bwd(
        [(dkv, w_kv, kv_norm_g), (dzb, b_w_in, p["b_norm_g"])], h1, dh2, name="mm_dn_b", tm=512,
        side=red_bi.exchange_side())
    red_bi.took_exchange(got)

    d_a_w_out = _mm_tn(y, dh1b, name="mm_d_a_w_out", tm=1024, tn=D_MODEL)
    red_ao = _Reduction(["a_w_out"], [d_a_w_out.reshape(N_CHIPS, A_WIDTH // N_CHIPS, D_MODEL)], core, place)
    dy, got = _mm_nt(dh1b, a_w_out, name="mm_dy", tn=1024, out_dtype=BF16, side=red_ao.exchange_side())
    red_ao.took_exchange(got)
    sides = [red_bi.scatter_side(), red_ao.scatter_side()]
    (dz, d_ln_g, d_ln_b, d_ws, d_bs_t), got = _gate_bwd(z, dy, a_ln_g, a_ln_b, ws, ws_t, bs_t, side=_join(sides))
    got = _split(got, sides)
    red_bi.took_scatter(got[0])
    red_ao.took_scatter(got[1])
    small = {
        "a_ws": d_ws, "a_bs": d_bs_t.T, "a_ln_g": d_ln_g, "a_ln_b": d_ln_b,
        "kv_norm_g": d_kv_g, "b_kv": d_b_kv, "b_norm_g": d_b_g, "b_bq": d_bq,
        "b_sinks": d_sinks[0:1, :N_Q_HEADS], "final_norm_g": d_final_g,
    }
    packed = [n for n in SMALL_ORDER if n != "a_norm_g"]
    small_shapes = [small[n].shape for n in packed] + [(1, 1)]
    small_pack = _pack_rows([small[n] for n in packed] + [loss_blk[0:1, 0:1]], 64)
    seg = small_pack.shape[0] // 8
    small_pack = small_pack.reshape(8, seg, 128)
    sides = [red_bi.share_side(), red_ao.share_side(), _small_scatter_side(small_pack)]
    (d_a_w_in, from_sibling), got = _mm_tn_exchanging(n_a, dz, name="mm_d_a_w_in", shards=N_CHIPS, side=_join(sides))
    got = _split(got, sides)
    red_bi.took_share(got[0])
    red_ao.took_share(got[1])
    small_mine = _sum_small(small_pack, got[2][0], place)

    chip_sum, (small_all,) = _add_halves(d_a_w_in, from_sibling, name="add_sibling_a_w_in",
                                         side=_small_share_side(small_mine))
    (dx, _, d_a_g), (arrived,) = _mm_nt_rms_bwd([(dz, a_w_in, a_norm_g)], x, dh1, name="mm_dn_a", tm=256,
                                                side=_scatter_side([chip_sum]))
    half_ai = _sum_halves(d_a_w_in, from_sibling, arrived, place, name="sum_chips_a_w_in")
    d_a_g = _rows128(d_a_g)
    sides = [_share_side([half_ai]), _spread_side(d_a_g)]
    got = _split(_comm_call(_join(sides), "share_last"), sides)
    grad_ai = got[0][0]
    small_full = dict(zip(packed + ["loss"], _unpack_rows(small_all.reshape(8 * seg, 128), small_shapes)))
    small_full["a_norm_g"] = _sum_in_device_order(d_a_g, got[1][0], place).reshape(1, -1)
    loss = small_full["loss"].reshape(())

    grad_big = {**red_bo.grads, **red_bi.grads, **red_ao.grads, "a_w_in": grad_ai}
    grads = {}
    for n in SMALL_ORDER:
        gfull = small_full[n]
        if n in SHARDED_SMALL:
            width = p[n].shape[-1]
            gfull = lax.dynamic_slice_in_dim(gfull, chip * width, width, axis=-1)
        grads[n] = gfull.reshape(p[n].shape)

    delta, new_m, new_v = {}, {}, {}
    for n in BIG:
        d, nm, nv, g = _adamw(shard2d[n], grad_big[n], m[n].reshape(shard2d[n].shape),
                              v[n].reshape(shard2d[n].shape), name="adamw_" + n)
        delta[n], new_m[n], new_v[n] = d.reshape(p[n].shape), nm.reshape(p[n].shape), nv.reshape(p[n].shape)
        grads[n] = g.reshape(p[n].shape)
    shapes = [p[n].shape for n in SMALL_ORDER]
    packs = [_pack_rows([src[n] for n in SMALL_ORDER], 8) for src in (p, grads, m, v)]
    outs = _adamw(*packs, name="adamw_small")[:3]
    for res, packed in zip((delta, new_m, new_v), outs):
        for n, val in zip(SMALL_ORDER, _unpack_rows(packed, shapes)):
            res[n] = val

    return (loss, dx[None], *[grads[n] for n in WEIGHTS], *[delta[n] for n in WEIGHTS],
            *[new_m[n] for n in WEIGHTS], *[new_v[n] for n in WEIGHTS])


def kernel(x, a_norm_g, a_w_in, a_ln_g, a_ln_b, a_ws, a_bs, a_w_out, kv_norm_g, w_kv, b_kv, b_norm_g, b_w_in, b_bq, b_sinks, b_w_out, final_norm_g, loss_target, m_a_norm_g, m_a_w_in, m_a_ln_g, m_a_ln_b, m_a_ws, m_a_bs, m_a_w_out, m_kv_norm_g, m_w_kv, m_b_kv, m_b_norm_g, m_b_w_in, m_b_bq, m_b_sinks, m_b_w_out, m_final_norm_g, v_a_norm_g, v_a_w_in, v_a_ln_g, v_a_ln_b, v_a_ws, v_a_bs, v_a_w_out, v_kv_norm_g, v_w_kv, v_b_kv, v_b_norm_g, v_b_w_in, v_b_bq, v_b_sinks, v_b_w_out, v_final_norm_g):
    p = dict(a_norm_g=a_norm_g, a_w_in=a_w_in, a_ln_g=a_ln_g, a_ln_b=a_ln_b, a_ws=a_ws, a_bs=a_bs, a_w_out=a_w_out,
             kv_norm_g=kv_norm_g, w_kv=w_kv, b_kv=b_kv, b_norm_g=b_norm_g, b_w_in=b_w_in, b_bq=b_bq, b_sinks=b_sinks,
             b_w_out=b_w_out, final_norm_g=final_norm_g)
    m = dict(a_norm_g=m_a_norm_g, a_w_in=m_a_w_in, a_ln_g=m_a_ln_g, a_ln_b=m_a_ln_b, a_ws=m_a_ws, a_bs=m_a_bs,
             a_w_out=m_a_w_out, kv_norm_g=m_kv_norm_g, w_kv=m_w_kv, b_kv=m_b_kv, b_norm_g=m_b_norm_g, b_w_in=m_b_w_in,
             b_bq=m_b_bq, b_sinks=m_b_sinks, b_w_out=m_b_w_out, final_norm_g=m_final_norm_g)
    v = dict(a_norm_g=v_a_norm_g, a_w_in=v_a_w_in, a_ln_g=v_a_ln_g, a_ln_b=v_a_ln_b, a_ws=v_a_ws, a_bs=v_a_bs,
             a_w_out=v_a_w_out, kv_norm_g=v_kv_norm_g, w_kv=v_w_kv, b_kv=v_b_kv, b_norm_g=v_b_norm_g, b_w_in=v_b_w_in,
             b_bq=v_b_bq, b_sinks=v_b_sinks, b_w_out=v_b_w_out, final_norm_g=v_final_norm_g)
    return _step(x, loss_target, p, m, v)
```

```python
import functools
import math

import jax
import jax.numpy as jnp
from jax import lax
from jax.experimental import pallas as pl
from jax.experimental.pallas import tpu as pltpu

F32 = jnp.float32
BF16 = jnp.bfloat16

D_MODEL = 1024
CHUNK = 128
A_WIDTH = 2048
A_GROUPS = 16
HEAD_DIM = 64
N_Q_HEADS = 16
N_KV_HEADS = 2
Q_PER_KV = 8
B_WIDTH = 1024
KV_WIDTH = 128
ROPE_THETA = 10000.0
EPS = 1e-5
N_CHIPS = 4

ADAM_LR = 0.001
ADAM_B1 = 0.9
ADAM_B2 = 0.999
ADAM_EPS = 1e-08
ADAM_WD = 0.01
ADAM_STEP = 10

VMEM_LIMIT = 48 * 1024 * 1024
MESH = pl.DeviceIdType.MESH
NEG_BIG = -1e30
HBM = pl.BlockSpec(memory_space=pl.ANY)

NN = (((1,), (0,)), ((), ()))
NT = (((1,), (1,)), ((), ()))
TN = (((0,), (0,)), ((), ()))


def _cparams(**kw):
    return pltpu.CompilerParams(vmem_limit_bytes=VMEM_LIMIT, **kw)


class _Side:
    def __init__(self, ins, out_shapes, sems, start, finish, aliases=None, passing=None):
        self.ins, self.out_shapes, self.sems = list(ins), list(out_shapes), list(sems)
        self.start, self.finish = start, finish
        self.passing = passing or (lambda ins, outs, sems: None)
        self.aliases = dict(aliases or {})


def _join(sides):
    sides = [s for s in sides if s is not None]
    if not sides:
        return None
    offs, i, o, m = [], 0, 0, 0
    for s in sides:
        offs.append((i, o, m))
        i, o, m = i + len(s.ins), o + len(s.out_shapes), m + len(s.sems)

    def run(which):
        def go(ins, outs, sems):
            for s, (a, b, c) in zip(sides, offs):
                getattr(s, which)(ins[a:a + len(s.ins)], outs[b:b + len(s.out_shapes)], sems[c:c + len(s.sems)])
        return go

    aliases = {}
    for s, (a, b, _) in zip(sides, offs):
        aliases.update({a + k: b + v for k, v in s.aliases.items()})
    return _Side([x for s in sides for x in s.ins], [x for s in sides for x in s.out_shapes],
                 [x for s in sides for x in s.sems], run("start"), run("finish"), aliases, run("passing"))


def _split(side_outs, sides):
    out, pos = [], 0
    for s in sides:
        out.append(list(side_outs[pos:pos + len(s.out_shapes)]))
        pos += len(s.out_shapes)
    return out


def _call(body, *, grid, in_specs, out_specs, out_shape, args, name, scratch=(), side=None):
    in_specs, out_specs, out_shape, scratch = list(in_specs), list(out_specs), list(out_shape), list(scratch)
    if side is None:
        res = pl.pallas_call(body, grid=grid, in_specs=in_specs, out_specs=out_specs, out_shape=out_shape,
                             scratch_shapes=scratch, name=name, compiler_params=_cparams())(*args)
        return list(res), []
    n_in, n_out, n_sc = len(in_specs), len(out_specs), len(scratch)
    s_in, s_out = len(side.ins), len(side.out_shapes)

    def wrapped(*refs):
        ins, refs = refs[:n_in], refs[n_in:]
        side_ins, refs = refs[:s_in], refs[s_in:]
        outs, refs = refs[:n_out], refs[n_out:]
        side_outs, refs = refs[:s_out], refs[s_out:]
        scr, side_sems = refs[:n_sc], refs[n_sc:]
        step = 0
        for a, g in enumerate(grid):
            step = step * g + pl.program_id(a)
        steps = math.prod(grid)

        @pl.when(step == 0)
        def _():
            side.start(side_ins, side_outs, side_sems)

        body(*ins, *outs, *scr)

        @pl.when(step == (3 * (steps - 1)) // 4)
        def _():
            side.passing(side_ins, side_outs, side_sems)

        @pl.when(step == steps - 1)
        def _():
            side.finish(side_ins, side_outs, side_sems)

    res = pl.pallas_call(
        wrapped, grid=grid, in_specs=in_specs + [HBM] * s_in, out_specs=out_specs + [HBM] * s_out,
        out_shape=out_shape + side.out_shapes, scratch_shapes=scratch + side.sems,
        input_output_aliases={n_in + k: n_out + v for k, v in side.aliases.items()},
        name=name, compiler_params=_cparams(),
    )(*args, *side.ins)
    return list(res[:n_out]), list(res[n_out:])


def _comm_call(side, name):
    s_in, s_out = len(side.ins), len(side.out_shapes)

    def body(*refs):
        ins, outs, sems = refs[:s_in], refs[s_in:s_in + s_out], refs[s_in + s_out:]
        side.start(ins, outs, sems)
        side.passing(ins, outs, sems)
        side.finish(ins, outs, sems)

    return list(pl.pallas_call(
        body, in_specs=[HBM] * s_in, out_specs=[HBM] * s_out, out_shape=side.out_shapes, scratch_shapes=side.sems,
        input_output_aliases=side.aliases, name=name,
    )(*side.ins))


def _matmul(a, b, *, dims, grid, a_spec, b_spec, o_spec, out_shape, name, acc_axis=None,
            residual=None, r_spec=None, side=None):
    has_res = residual is not None

    def body(*refs):
        if has_res:
            a_ref, b_ref, r_ref, o_ref = refs
        else:
            a_ref, b_ref, o_ref = refs
        part = lax.dot_general(a_ref[...], b_ref[...], dims, preferred_element_type=F32)
        if acc_axis is None:
            if has_res:
                part = part + r_ref[...]
            o_ref[...] = part.astype(o_ref.dtype)
        else:
            k = pl.program_id(acc_axis)

            @pl.when(k == 0)
            def _():
                o_ref[...] = part

            @pl.when(k > 0)
            def _():
                o_ref[...] += part

    in_specs = [a_spec, b_spec] + ([r_spec] if has_res else [])
    args = (a, b) + ((residual,) if has_res else ())
    (out,), side_outs = _call(body, grid=grid, in_specs=in_specs, out_specs=[o_spec], out_shape=[out_shape],
                              args=args, name=name, side=side)
    return (out, side_outs) if side is not None else out


def _row_tile(s, want):
    return min(s, want)


def _mm_nn(a, b, *, name, tn, out_dtype=F32, residual=None, tm=512, side=None):
    s, k = a.shape
    tm = _row_tile(s, tm)
    if b.ndim == 3:
        nsh, _, nc = b.shape
        npb = nc // tn
        n = nsh * nc
        b_spec = pl.BlockSpec((None, k, tn), lambda i, j: (j // npb, 0, j % npb))
    else:
        n = b.shape[1]
        b_spec = pl.BlockSpec((k, tn), lambda i, j: (0, j))
    return _matmul(
        a, b, dims=NN, grid=(s // tm, n // tn),
        a_spec=pl.BlockSpec((tm, k), lambda i, j: (i, 0)), b_spec=b_spec,
        o_spec=pl.BlockSpec((tm, tn), lambda i, j: (i, j)),
        out_shape=jax.ShapeDtypeStruct((s, n), out_dtype), name=name, side=side,
        residual=residual, r_spec=pl.BlockSpec((tm, tn), lambda i, j: (i, j)) if residual is not None else None)


def _mm_nt(a, b, *, name, tn=None, tm=512, out_dtype=F32, side=None):
    s, k = a.shape
    tm = _row_tile(s, tm)
    n = b.shape[0]
    tn = n if tn is None else tn
    return _matmul(
        a, b, dims=NT, grid=(s // tm, n // tn),
        a_spec=pl.BlockSpec((tm, k), lambda i, j: (i, 0)),
        b_spec=pl.BlockSpec((tn, k), lambda i, j: (j, 0)),
        o_spec=pl.BlockSpec((tm, tn), lambda i, j: (i, j)),
        out_shape=jax.ShapeDtypeStruct((s, n), out_dtype), name=name, side=side)


def _mm_tn(a, b, *, name, tm, tn, tk=2048, shards=None, side=None):
    s, m = a.shape
    n = b.shape[1]
    tk = _row_tile(s, tk)
    if shards is None:
        o_spec = pl.BlockSpec((tm, tn), lambda i, j, kk: (i, j))
        out_shape = jax.ShapeDtypeStruct((m, n), F32)
    else:
        assert tm == m
        nc = n // shards
        npb = nc // tn
        o_spec = pl.BlockSpec((None, m, tn), lambda i, j, kk: (j // npb, 0, j % npb))
        out_shape = jax.ShapeDtypeStruct((shards, m, nc), F32)
    return _matmul(
        a, b, dims=TN, grid=(m // tm, n // tn, s // tk), acc_axis=2,
        a_spec=pl.BlockSpec((tk, tm), lambda i, j, kk: (kk, i)),
        b_spec=pl.BlockSpec((tk, tn), lambda i, j, kk: (kk, j)),
        o_spec=o_spec, out_shape=out_shape, name=name, side=side)


def _rstd(x):
    return lax.rsqrt(jnp.mean(x * x, axis=-1, keepdims=True) + EPS)


def _rms_fwd(x, gains, *, name, tr=256):
    s, d = x.shape
    tr = _row_tile(s, tr)
    ng = len(gains)

    def body(*refs):
        xv = refs[0][...]
        xh = xv * _rstd(xv)
        for t in range(ng):
            refs[1 + ng + t][...] = (xh * refs[1 + t][...]).astype(BF16)

    row = pl.BlockSpec((tr, d), lambda i: (i, 0))
    vec = pl.BlockSpec((1, d), lambda i: (0, 0))
    outs, _ = _call(body, grid=(s // tr,), in_specs=[row] + [vec] * ng, out_specs=[row] * ng,
                    out_shape=[jax.ShapeDtypeStruct((s, d), BF16)] * ng, args=(x, *gains), name=name)
    return outs


def _accumulate(i, ref, value):
    @pl.when(i == 0)
    def _():
        ref[...] = value

    @pl.when(i > 0)
    def _():
        ref[...] += value


def _mm_residual_norms(y, w, res, gains, *, name, tm=256, side=None):
    s, k = y.shape
    d = w.shape[1]
    tm = _row_tile(s, tm)
    ng = len(gains)

    def body(y_ref, w_ref, r_ref, *rest):
        g_refs, h_ref, n_refs = rest[:ng], rest[ng], rest[ng + 1:]
        h = r_ref[...] + jnp.dot(y_ref[...], w_ref[...], preferred_element_type=F32)
        h_ref[...] = h
        xh = h * _rstd(h)
        for t in range(ng):
            n_refs[t][...] = (xh * g_refs[t][...]).astype(BF16)

    row = pl.BlockSpec((tm, d), lambda i: (i, 0))
    vec = pl.BlockSpec((1, d), lambda i: (0, 0))
    return _call(
        body, grid=(s // tm,),
        in_specs=[pl.BlockSpec((tm, k), lambda i: (i, 0)), pl.BlockSpec((k, d), lambda i: (0, 0)), row] + [vec] * ng,
        out_specs=[row] * (1 + ng),
        out_shape=[jax.ShapeDtypeStruct((s, d), F32)] + [jax.ShapeDtypeStruct((s, d), BF16)] * ng,
        args=(y, w, res, *gains), name=name, side=side)


def _mm_residual_loss(y, w, res, tgt, gain, *, name, tm=512):
    s, k = y.shape
    d = w.shape[1]
    tm = _row_tile(s, tm)

    def body(y_ref, w_ref, r_ref, t_ref, g_ref, loss_ref, dh_ref, dhb_ref, dg_ref):
        i = pl.program_id(0)
        hv = r_ref[...] + jnp.dot(y_ref[...], w_ref[...], preferred_element_type=F32)
        g = g_ref[...]
        r = _rstd(hv)
        xh = hv * r
        diff = xh * g - t_ref[...]
        part = 0.5 / d * jnp.sum(jnp.sum(diff * diff, axis=-1, keepdims=True), axis=0, keepdims=True)
        dout = diff * (1.0 / d)
        a = dout * g
        dh = r * (a - xh * jnp.mean(a * xh, axis=-1, keepdims=True))
        dh_ref[...] = dh
        dhb_ref[...] = dh.astype(BF16)
        _accumulate(i, dg_ref, jnp.sum(dout * xh, axis=0, keepdims=True))
        _accumulate(i, loss_ref, jnp.broadcast_to(part, (8, 128)))

    row = pl.BlockSpec((tm, d), lambda i: (i, 0))
    vec = pl.BlockSpec((1, d), lambda i: (0, 0))
    outs, _ = _call(
        body, grid=(s // tm,),
        in_specs=[pl.BlockSpec((tm, k), lambda i: (i, 0)), pl.BlockSpec((k, d), lambda i: (0, 0)), row, row, vec],
        out_specs=[pl.BlockSpec((8, 128), lambda i: (0, 0)), row, row, vec],
        out_shape=[jax.ShapeDtypeStruct((8, 128), F32), jax.ShapeDtypeStruct((s, d), F32),
                   jax.ShapeDtypeStruct((s, d), BF16), jax.ShapeDtypeStruct((1, d), F32)],
        args=(y, w, res, tgt, gain), name=name)
    return outs


def _mm_nt_rms_bwd(terms, x, dres, *, name, tm, side=None):
    s, d = x.shape
    tm = _row_tile(s, tm)
    nt = len(terms)

    def body(*refs):
        a_refs, b_refs, g_refs = refs[0:3 * nt:3], refs[1:3 * nt:3], refs[2:3 * nt:3]
        x_ref, dres_ref = refs[3 * nt], refs[3 * nt + 1]
        dx_ref, dxb_ref = refs[3 * nt + 2], refs[3 * nt + 3]
        dg_refs = refs[3 * nt + 4:]
        i = pl.program_id(0)
        xv = x_ref[...]
        r = _rstd(xv)
        xh = xv * r
        acc = jnp.zeros_like(xv)
        for t in range(nt):
            b_ref = b_refs[t]
            if len(b_ref.shape) == 3:
                kc = b_ref.shape[2]
                dn = None
                for sh in range(b_ref.shape[0]):
                    part = lax.dot_general(a_refs[t][:, sh * kc:(sh + 1) * kc], b_ref[sh], NT, preferred_element_type=F32)
                    dn = part if dn is None else dn + part
            else:
                dn = lax.dot_general(a_refs[t][...], b_ref[...], NT, preferred_element_type=F32)
            acc = acc + dn * g_refs[t][...]
            _accumulate(i, dg_refs[t], jnp.sum(dn * xh, axis=0, keepdims=True))
        dx = dres_ref[...] + r * (acc - xh * jnp.mean(acc * xh, axis=-1, keepdims=True))
        dx_ref[...] = dx
        dxb_ref[...] = dx.astype(BF16)

    row = pl.BlockSpec((tm, d), lambda i: (i, 0))
    vec = pl.BlockSpec((1, d), lambda i: (0, 0))
    in_specs, args = [], []
    for a, b, g in terms:
        in_specs += [pl.BlockSpec((tm, a.shape[1]), lambda i: (i, 0)),
                     pl.BlockSpec(b.shape, (lambda i: (0, 0, 0)) if b.ndim == 3 else (lambda i: (0, 0))), vec]
        args += [a, b, g]
    return _call(
        body, grid=(s // tm,), in_specs=in_specs + [row, row], out_specs=[row, row] + [vec] * nt,
        out_shape=[jax.ShapeDtypeStruct((s, d), F32), jax.ShapeDtypeStruct((s, d), BF16)]
        + [jax.ShapeDtypeStruct((1, d), F32)] * nt,
        args=(*args, x, dres), name=name, side=side)


def _causal_mask(transposed=False):
    row = lax.broadcasted_iota(jnp.int32, (CHUNK, CHUNK), 0)
    col = lax.broadcasted_iota(jnp.int32, (CHUNK, CHUNK), 1)
    return col >= row if transposed else col <= row


def _silu_parts(g):
    sg = jax.nn.sigmoid(g)
    return g * sg, sg * (1.0 + g * (1.0 - sg))


def _gate_fwd(z, ln_g, ln_b, ws, bs_t, *, tr=256, side=None):
    s = z.shape[0]
    tr = _row_tile(s, tr)
    w = A_WIDTH

    def body(u_ref, v_ref, g_ref, lg_ref, lb_ref, ws_ref, bst_ref, y_ref):
        v = v_ref[...].astype(F32)
        mu = jnp.mean(v, axis=-1, keepdims=True)
        xc = v - mu
        rs = lax.rsqrt(jnp.mean(xc * xc, axis=-1, keepdims=True) + EPS)
        vln = (xc * rs * lg_ref[...] + lb_ref[...]).astype(BF16)
        mask = _causal_mask()
        for grp in range(A_GROUPS):
            cols = slice(grp * CHUNK, (grp + 1) * CHUNK)
            wsm = jnp.where(mask, ws_ref[grp], 0.0).astype(BF16)
            bcol = bst_ref[:, grp:grp + 1]
            for ci in range(tr // CHUNK):
                rows = slice(ci * CHUNK, (ci + 1) * CHUNK)
                sv = jnp.dot(wsm, vln[rows, cols], preferred_element_type=F32) + bcol
                gv = g_ref[rows, cols].astype(F32)
                y_ref[rows, cols] = (u_ref[rows, cols].astype(F32) * sv * (gv * jax.nn.sigmoid(gv))).astype(BF16)

    vec = pl.BlockSpec((1, w), lambda i: (0, 0))
    (y,), side_outs = _call(
        body, grid=(s // tr,),
        in_specs=[pl.BlockSpec((tr, w), lambda i: (i, 0)), pl.BlockSpec((tr, w), lambda i: (i, 1)),
                  pl.BlockSpec((tr, w), lambda i: (i, 2)), vec, vec,
                  pl.BlockSpec((A_GROUPS, CHUNK, CHUNK), lambda i: (0, 0, 0)),
                  pl.BlockSpec((CHUNK, A_GROUPS), lambda i: (0, 0))],
        out_specs=[pl.BlockSpec((tr, w), lambda i: (i, 0))],
        out_shape=[jax.ShapeDtypeStruct((s, w), BF16)], args=(z, z, z, ln_g, ln_b, ws, bs_t), name="gate_fwd",
        side=side)
    return y, side_outs


def _gate_bwd(z, dy, ln_g, ln_b, ws, ws_t, bs_t, *, tr=256, side=None):
    s = z.shape[0]
    tr = _row_tile(s, tr)
    w = A_WIDTH
    nsteps = s // tr

    def body(u_ref, v_ref, g_ref, dy_ref, lg_ref, lb_ref, ws_ref, wst_ref, bst_ref,
             dz_ref, dlg_ref, dlb_ref, dws_ref, dbst_ref, dvln_sc, dsv_sc):
        i = pl.program_id(0)

        @pl.when(i == 0)
        def _():
            dws_ref[...] = jnp.zeros_like(dws_ref)
            dsv_sc[...] = jnp.zeros_like(dsv_sc)

        v = v_ref[...].astype(F32)
        mu = jnp.mean(v, axis=-1, keepdims=True)
        xc = v - mu
        rs = lax.rsqrt(jnp.mean(xc * xc, axis=-1, keepdims=True) + EPS)
        xh = xc * rs
        lg = lg_ref[...]
        vln = (xh * lg + lb_ref[...]).astype(BF16)
        mask = _causal_mask()
        mask_t = _causal_mask(transposed=True)
        for grp in range(A_GROUPS):
            cols = slice(grp * CHUNK, (grp + 1) * CHUNK)
            wsm = jnp.where(mask, ws_ref[grp], 0.0).astype(BF16)
            wsm_t = jnp.where(mask_t, wst_ref[grp], 0.0).astype(BF16)
            bcol = bst_ref[:, grp:grp + 1]
            for ci in range(tr // CHUNK):
                rows = slice(ci * CHUNK, (ci + 1) * CHUNK)
                vb = vln[rows, cols]
                sv = jnp.dot(wsm, vb, preferred_element_type=F32) + bcol
                uv = u_ref[rows, cols].astype(F32)
                silu, dsilu = _silu_parts(g_ref[rows, cols].astype(F32))
                dyv = dy_ref[rows, cols].astype(F32)
                dyu = dyv * uv
                dz_ref[rows, cols] = (dyv * sv * silu).astype(BF16)
                dz_ref[rows, 2 * w + grp * CHUNK:2 * w + (grp + 1) * CHUNK] = (dyu * sv * dsilu).astype(BF16)
                dsv = dyu * silu
                dsvb = dsv.astype(BF16)
                dvln_sc[rows, cols] = jnp.dot(wsm_t, dsvb, preferred_element_type=F32)
                dws_ref[grp] += lax.dot_general(dsvb, vb, NT, preferred_element_type=F32)
                dsv_sc[grp] += dsv
        dvln = dvln_sc[...]
        dlg_t = jnp.sum(dvln * xh, axis=0, keepdims=True)
        dlb_t = jnp.sum(dvln, axis=0, keepdims=True)
        a = dvln * lg
        dv = rs * (a - jnp.mean(a, axis=-1, keepdims=True) - xh * jnp.mean(a * xh, axis=-1, keepdims=True))
        dz_ref[:, w:2 * w] = dv.astype(BF16)

        @pl.when(i == 0)
        def _():
            dlg_ref[...] = dlg_t
            dlb_ref[...] = dlb_t

        @pl.when(i > 0)
        def _():
            dlg_ref[...] += dlg_t
            dlb_ref[...] += dlb_t

        @pl.when(i == nsteps - 1)
        def _():
            for grp in range(A_GROUPS):
                dws_ref[grp] = jnp.where(mask, dws_ref[grp], 0.0)
                dbst_ref[:, grp:grp + 1] = jnp.sum(dsv_sc[grp], axis=-1, keepdims=True)

    vec = pl.BlockSpec((1, w), lambda i: (0, 0))
    wsspec = pl.BlockSpec((A_GROUPS, CHUNK, CHUNK), lambda i: (0, 0, 0))
    bsspec = pl.BlockSpec((CHUNK, A_GROUPS), lambda i: (0, 0))
    return _call(
        body, grid=(nsteps,),
        in_specs=[pl.BlockSpec((tr, w), lambda i: (i, 0)), pl.BlockSpec((tr, w), lambda i: (i, 1)),
                  pl.BlockSpec((tr, w), lambda i: (i, 2)), pl.BlockSpec((tr, w), lambda i: (i, 0)),
                  vec, vec, wsspec, wsspec, bsspec],
        out_specs=[pl.BlockSpec((tr, 3 * w), lambda i: (i, 0)), vec, vec, wsspec, bsspec],
        out_shape=[jax.ShapeDtypeStruct((s, 3 * w), BF16), jax.ShapeDtypeStruct((1, w), F32),
                   jax.ShapeDtypeStruct((1, w), F32), jax.ShapeDtypeStruct((A_GROUPS, CHUNK, CHUNK), F32),
                   jax.ShapeDtypeStruct((CHUNK, A_GROUPS), F32)],
        scratch=[pltpu.VMEM((tr, w), F32), pltpu.VMEM((A_GROUPS, CHUNK, CHUNK), F32)],
        args=(z, z, z, dy, ln_g, ln_b, ws, ws_t, bs_t), name="gate_bwd", side=side)


HEADS_PER_BLOCK = 128 // HEAD_DIM
BLOCKS_PER_KV = Q_PER_KV // HEADS_PER_BLOCK
SCALE = HEAD_DIM ** -0.5
LOG2_E = math.log2(math.e)


def _rope_tables(s):
    lane = jnp.arange(128)
    inv_freq = ROPE_THETA ** (-(2 * (lane % (HEAD_DIM // 2))).astype(F32) / HEAD_DIM)
    sign = jnp.where(lane % HEAD_DIM < HEAD_DIM // 2, -1.0, 1.0).astype(F32)
    ang = jnp.arange(s, dtype=F32)[:, None] * inv_freq[None, :]
    return jnp.cos(ang), jnp.sin(ang) * sign[None, :]


def _swap_halves(x):
    n = x.shape[-1]
    lane = lax.broadcasted_iota(jnp.int32, x.shape, x.ndim - 1)
    first = (lane % HEAD_DIM) < (HEAD_DIM // 2)
    return jnp.where(first, pltpu.roll(x, n - HEAD_DIM // 2, x.ndim - 1), pltpu.roll(x, HEAD_DIM // 2, x.ndim - 1))


def _left_half(rows):
    return lax.broadcasted_iota(jnp.int32, (rows, 128), 1) < HEAD_DIM


def _dup_heads(x):
    left = _left_half(x.shape[0])
    swapped = pltpu.roll(x, HEAD_DIM, 1)
    return jnp.concatenate([jnp.where(left, x, swapped), jnp.where(left, swapped, x)], axis=-1)


def _fold_heads(a):
    b0, b1 = a[:, :128], a[:, 128:]
    f0 = b0 + pltpu.roll(b0, HEAD_DIM, 1)
    f1 = b1 + pltpu.roll(b1, HEAD_DIM, 1)
    return jnp.where(_left_half(a.shape[0]), f0, f1)


def _kv_rope(kv, b_kv, cos, sin, *, tr=512):
    s = kv.shape[0]
    tr = _row_tile(s, tr)

    def body(kv_ref, b_ref, c_ref, s_ref, k_ref, v_ref):
        x = kv_ref[...] + b_ref[...]
        k = x[:, :KV_WIDTH]
        k_ref[...] = _dup_heads(k * c_ref[...] + _swap_halves(k) * s_ref[...]).astype(BF16)
        v_ref[...] = _dup_heads(x[:, KV_WIDTH:]).astype(BF16)

    tab = pl.BlockSpec((tr, KV_WIDTH), lambda i: (i, 0))
    wide = pl.BlockSpec((tr, 2 * KV_WIDTH), lambda i: (i, 0))
    outs, _ = _call(body, grid=(s // tr,),
                    in_specs=[wide, pl.BlockSpec((1, 2 * KV_WIDTH), lambda i: (0, 0)), tab, tab],
                    out_specs=[wide, wide], out_shape=[jax.ShapeDtypeStruct((s, 2 * KV_WIDTH), BF16)] * 2,
                    args=(kv, b_kv, cos, sin), name="kv_rope")
    return outs


def _kv_rope_bwd(dk2, dv2, cos, sin, *, tr=512):
    s = dk2.shape[0]
    tr = _row_tile(s, tr)

    def body(dk_ref, dv_ref, c_ref, s_ref, dkv_ref, db_ref):
        i = pl.program_id(0)
        d = _fold_heads(dk_ref[...])
        dk = d * c_ref[...] + _swap_halves(d * s_ref[...])
        dvv = _fold_heads(dv_ref[...])
        dkv_ref[:, :KV_WIDTH] = dk.astype(BF16)
        dkv_ref[:, KV_WIDTH:] = dvv.astype(BF16)
        sk = jnp.sum(dk, axis=0, keepdims=True)
        sv = jnp.sum(dvv, axis=0, keepdims=True)

        @pl.when(i == 0)
        def _():
            db_ref[:, :KV_WIDTH] = sk
            db_ref[:, KV_WIDTH:] = sv

        @pl.when(i > 0)
        def _():
            db_ref[:, :KV_WIDTH] += sk
            db_ref[:, KV_WIDTH:] += sv

    tab = pl.BlockSpec((tr, KV_WIDTH), lambda i: (i, 0))
    wide = pl.BlockSpec((tr, 2 * KV_WIDTH), lambda i: (i, 0))
    outs, _ = _call(body, grid=(s // tr,), in_specs=[wide, wide, tab, tab],
                    out_specs=[wide, pl.BlockSpec((1, 2 * KV_WIDTH), lambda i: (0, 0))],
                    out_shape=[jax.ShapeDtypeStruct((s, 2 * KV_WIDTH), BF16),
                               jax.ShapeDtypeStruct((1, 2 * KV_WIDTH), F32)],
                    args=(dk2, dv2, cos, sin), name="kv_rope_bwd")
    return outs


def _from_previous():
    cols = Q_PER_KV * CHUNK
    k = lax.broadcasted_iota(jnp.int32, (CHUNK, cols), 0)
    q = lax.broadcasted_iota(jnp.int32, (CHUNK, cols), 1) & (CHUNK - 1)
    return k > q


def _fold(x2, prev):
    return jnp.where(prev, x2[:CHUNK], x2[CHUNK:])


def _unfold(x, prev):
    zero = jnp.zeros_like(x)
    return jnp.concatenate([jnp.where(prev, x, zero), jnp.where(prev, zero, x)], axis=0)


def _stack_heads(blocks, left):
    parts = []
    for b in blocks:
        parts.append(jnp.where(left, b, jnp.zeros_like(b)))
        parts.append(jnp.where(left, jnp.zeros_like(b), b))
    return jnp.concatenate(parts, axis=0)


def _unstack_heads(xt):
    top = lax.broadcasted_iota(jnp.int32, (128, CHUNK), 0) < HEAD_DIM
    return [jnp.where(top, xt[:, (2 * b) * CHUNK:(2 * b + 1) * CHUNK], xt[:, (2 * b + 1) * CHUNK:(2 * b + 2) * CHUNK]).T
            for b in range(BLOCKS_PER_KV)]


def _sink_row(sk_ref, kvh):
    return jnp.concatenate([jnp.full((1, CHUNK), sk_ref[0, kvh * Q_PER_KV + r], F32) for r in range(Q_PER_KV)], axis=1)


def _stacked_probs(qs, kd, prev, sink, i):
    sc2 = lax.dot_general(kd, qs, NT, preferred_element_type=F32)
    no_previous = jnp.where(i > 0, 0.0, NEG_BIG)
    sc = jnp.where(prev, sc2[:CHUNK] + no_previous, sc2[CHUNK:])
    sink = sink * (1.0 / SCALE)
    m = jnp.maximum(jnp.max(sc, axis=0, keepdims=True), sink)
    p = jnp.exp2((sc - m) * (SCALE * LOG2_E))
    esink = jnp.exp2((sink - m) * (SCALE * LOG2_E))
    inv = 1.0 / (jnp.sum(p, axis=0, keepdims=True) + esink)
    return p * inv, esink * inv


def _lane_block(b):
    return slice(b * 128, (b + 1) * 128)


def _rope_blocks(zq_ref, bq_ref, cos, sin, kvh):
    out = []
    for b in range(BLOCKS_PER_KV):
        cols = _lane_block(kvh * BLOCKS_PER_KV + b)
        q = zq_ref[:, cols].astype(F32) + bq_ref[:, cols]
        out.append((q * cos + _swap_halves(q) * sin).astype(BF16))
    return out


def _attn_specs():
    qspec = pl.BlockSpec((CHUNK, B_WIDTH), lambda i: (i, 0))
    gspec = pl.BlockSpec((CHUNK, B_WIDTH), lambda i: (i, 1))
    prev = pl.BlockSpec((CHUNK, 2 * KV_WIDTH), lambda i: (jnp.maximum(i - 1, 0), 0))
    cur = pl.BlockSpec((CHUNK, 2 * KV_WIDTH), lambda i: (i, 0))
    tab = pl.BlockSpec((CHUNK, KV_WIDTH), lambda i: (i, 0))
    bq = pl.BlockSpec((1, B_WIDTH), lambda i: (0, 0))
    sinks = pl.BlockSpec(memory_space=pltpu.SMEM)
    return qspec, gspec, prev, cur, tab, bq, sinks


def _attn_fwd(zb, k2, v2, cos, sin, b_bq, sinks, *, side=None):
    s = zb.shape[0]

    def body(zq_ref, zg_ref, kp_ref, kc_ref, vp_ref, vc_ref, c_ref, s_ref, bq_ref, sk_ref, y_ref):
        i = pl.program_id(0)
        cos, sin = c_ref[...], s_ref[...]
        kcat = jnp.concatenate([kp_ref[...], kc_ref[...]], axis=0)
        vcat = jnp.concatenate([vp_ref[...], vc_ref[...]], axis=0)
        prev = _from_previous()
        left = _left_half(CHUNK)
        for kvh in range(N_KV_HEADS):
            qs = _stack_heads(_rope_blocks(zq_ref, bq_ref, cos, sin, kvh), left)
            p, _ = _stacked_probs(qs, kcat[:, _lane_block(kvh)], prev, _sink_row(sk_ref, kvh), i)
            ot = lax.dot_general(vcat[:, _lane_block(kvh)], _unfold(p, prev).astype(BF16), TN,
                                 preferred_element_type=F32)
            for b, ob in enumerate(_unstack_heads(ot)):
                cols = _lane_block(kvh * BLOCKS_PER_KV + b)
                gv = zg_ref[:, cols].astype(F32)
                y_ref[:, cols] = (ob * (gv * jax.nn.sigmoid(gv))).astype(BF16)

    qspec, gspec, prev, cur, tab, bq, sk = _attn_specs()
    (y,), side_outs = _call(body, grid=(s // CHUNK,), in_specs=[qspec, gspec, prev, cur, prev, cur, tab, tab, bq, sk],
                            out_specs=[qspec], out_shape=[jax.ShapeDtypeStruct((s, B_WIDTH), BF16)],
                            args=(zb, zb, k2, k2, v2, v2, cos, sin, b_bq, sinks), name="attn_fwd", side=side)
    return y, side_outs


def _attn_bwd(zb, dyb, k2, v2, cos, sin, b_bq, sinks, *, side=None):
    s = zb.shape[0]

    def body(zq_ref, zg_ref, dy_ref, kp_ref, kc_ref, vp_ref, vc_ref, c_ref, s_ref, bq_ref, sk_ref,
             dz_ref, dk_ref, dv_ref, dbq_ref, dsk_ref):
        i = pl.program_id(0)

        @pl.when(i == 0)
        def _():
            dk_ref[...] = jnp.zeros_like(dk_ref)
            dv_ref[...] = jnp.zeros_like(dv_ref)
            dbq_ref[...] = jnp.zeros_like(dbq_ref)
            dsk_ref[...] = jnp.zeros_like(dsk_ref)

        cos, sin = c_ref[...], s_ref[...]
        kcat = jnp.concatenate([kp_ref[...], kc_ref[...]], axis=0)
        vcat = jnp.concatenate([vp_ref[...], vc_ref[...]], axis=0)
        prev = _from_previous()
        left = _left_half(CHUNK)
        lane = lax.broadcasted_iota(jnp.int32, (1, 128), 1)
        dsk_row = jnp.zeros((1, 128), F32)
        cur_rows = pl.ds(pl.multiple_of(i * CHUNK, CHUNK), CHUNK)
        for kvh in range(N_KV_HEADS):
            kd, vd = kcat[:, _lane_block(kvh)], vcat[:, _lane_block(kvh)]
            qs = _stack_heads(_rope_blocks(zq_ref, bq_ref, cos, sin, kvh), left)
            p, psink = _stacked_probs(qs, kd, prev, _sink_row(sk_ref, kvh), i)
            pb = _unfold(p, prev).astype(BF16)
            ot = lax.dot_general(vd, pb, TN, preferred_element_type=F32)
            gates, dys = [], []
            for b in range(BLOCKS_PER_KV):
                cols = _lane_block(kvh * BLOCKS_PER_KV + b)
                gates.append(_silu_parts(zg_ref[:, cols].astype(F32)))
                dys.append(dy_ref[:, cols].astype(F32))
            dos = _stack_heads([(dyv * silu).astype(BF16) for dyv, (silu, _) in zip(dys, gates)], left)
            dp = _fold(lax.dot_general(vd, dos, NT, preferred_element_type=F32), prev)
            delta = jnp.sum(p * dp, axis=0, keepdims=True)
            ds = _unfold(p * (dp - delta) * SCALE, prev).astype(BF16)
            dqt = lax.dot_general(kd, ds, TN, preferred_element_type=F32)
            dk_part = jnp.dot(ds, qs, preferred_element_type=F32)
            dv_part = jnp.dot(pb, dos, preferred_element_type=F32)
            dk_ref[cur_rows, _lane_block(kvh)] += dk_part[CHUNK:]
            dv_ref[cur_rows, _lane_block(kvh)] += dv_part[CHUNK:]

            @pl.when(i > 0)
            def _(kvh=kvh, dk_part=dk_part, dv_part=dv_part):
                prev_rows = pl.ds(pl.multiple_of((i - 1) * CHUNK, CHUNK), CHUNK)
                dk_ref[prev_rows, _lane_block(kvh)] += dk_part[:CHUNK]
                dv_ref[prev_rows, _lane_block(kvh)] += dv_part[:CHUNK]

            sink_grad = psink * delta
            for r in range(Q_PER_KV):
                dsink = -jnp.sum(sink_grad[:, r * CHUNK:(r + 1) * CHUNK], axis=1, keepdims=True)
                dsk_row = dsk_row + jnp.where(lane == kvh * Q_PER_KV + r, dsink, 0.0)
            blocks = zip(_unstack_heads(ot), _unstack_heads(dqt), dys, gates)
            for b, (ob, dqr, dyv, (_, dsilu)) in enumerate(blocks):
                blk = kvh * BLOCKS_PER_KV + b
                dq = dqr * cos + _swap_halves(dqr * sin)
                dbq_ref[:, _lane_block(blk)] += jnp.sum(dq, axis=0, keepdims=True)
                dz_ref[:, _lane_block(blk)] = dq.astype(BF16)
                dz_ref[:, _lane_block(B_WIDTH // 128 + blk)] = (dyv * ob * dsilu).astype(BF16)
        dsk_ref[0:1, :] += dsk_row

    qspec, gspec, prev, cur, tab, bq, sk = _attn_specs()
    full = pl.BlockSpec((s, 2 * KV_WIDTH), lambda i: (0, 0))
    return _call(
        body, grid=(s // CHUNK,),
        in_specs=[qspec, gspec, qspec, prev, cur, prev, cur, tab, tab, bq, sk],
        out_specs=[pl.BlockSpec((CHUNK, 2 * B_WIDTH), lambda i: (i, 0)), full, full, bq,
                   pl.BlockSpec((8, 128), lambda i: (0, 0))],
        out_shape=[jax.ShapeDtypeStruct((s, 2 * B_WIDTH), BF16), jax.ShapeDtypeStruct((s, 2 * KV_WIDTH), F32),
                   jax.ShapeDtypeStruct((s, 2 * KV_WIDTH), F32), jax.ShapeDtypeStruct((1, B_WIDTH), F32),
                   jax.ShapeDtypeStruct((8, 128), F32)],
        args=(zb, zb, dyb, k2, k2, v2, v2, cos, sin, b_bq, sinks), name="attn_bwd", side=side)


def _place():
    x, y, c = lax.axis_index("x"), lax.axis_index("y"), lax.axis_index("c")
    return x, y, c, [(1 - x, y), (x, 1 - y), (1 - x, 1 - y)]


def _relations():
    return [(r >> 2 & 1, r >> 1 & 1, r & 1) for r in range(1, 8)]


def _gather_side(arrs):
    n = len(arrs)

    def copies(ins, outs, sems):
        send_ici, recv_ici, send_d2d, recv_d2d, local_sem = sems
        x, y, c, chips = _place()
        me = 2 * x + y

        def rows(a, half):
            hr = arrs[a].shape[0] // 2
            return pl.ds(half * hr, hr)

        def ici(a, j, src_chip, to):
            return pltpu.make_async_remote_copy(
                src_ref=ins[a].at[rows(a, c)], dst_ref=outs[a].at[src_chip, rows(a, c)],
                send_sem=send_ici.at[a, j], recv_sem=recv_ici.at[a, j], device_id=to, device_id_type=MESH)

        def d2d(a, j, chip, half):
            blk = outs[a].at[chip, rows(a, half)]
            return pltpu.make_async_remote_copy(
                src_ref=blk, dst_ref=blk, send_sem=send_d2d.at[a, j], recv_sem=recv_d2d.at[a, j],
                device_id=(x, y, 1 - c), device_id_type=MESH)

        local = [pltpu.make_async_copy(ins[a], outs[a].at[me], local_sem.at[a]) for a in range(n)]
        pairs = [(a, j, chip) for a in range(n) for j, chip in enumerate(chips)]
        return c, me, local, ici, d2d, pairs

    def start(ins, outs, sems):
        c, me, local, ici, _, pairs = copies(ins, outs, sems)
        for cp in local:
            cp.start()
        for a, j, chip in pairs:
            ici(a, j, me, (*chip, c)).start()

    def passing(ins, outs, sems):
        c, _, _, ici, d2d, pairs = copies(ins, outs, sems)
        for a, j, (px, py) in pairs:
            ici(a, j, 2 * px + py, (px, py, c)).wait_recv()
            d2d(a, j, 2 * px + py, c).start()

    def finish(ins, outs, sems):
        c, me, local, ici, d2d, pairs = copies(ins, outs, sems)
        for a, j, (px, py) in pairs:
            d2d(a, j, 2 * px + py, 1 - c).wait_recv()
        for a, j, (px, py) in pairs:
            ici(a, j, me, (px, py, c)).wait_send()
            d2d(a, j, 2 * px + py, c).wait_send()
        for cp in local:
            cp.wait()

    return _Side(arrs, [jax.ShapeDtypeStruct((N_CHIPS,) + a.shape, a.dtype) for a in arrs],
                 [pltpu.SemaphoreType.DMA((n, 3))] * 4 + [pltpu.SemaphoreType.DMA((n,))], start, finish,
                 passing=passing)


def _exchange_side(grads):
    n = len(grads)

    def copies(ins, outs, sems):
        send_sem, recv_sem = sems
        x, y, c, _ = _place()
        cps = []
        for a in range(n):
            hr = grads[a].shape[1] // 2
            cps.append(pltpu.make_async_remote_copy(
                src_ref=ins[a].at[:, pl.ds((1 - c) * hr, hr), :], dst_ref=outs[a],
                send_sem=send_sem.at[a], recv_sem=recv_sem.at[a], device_id=(x, y, 1 - c), device_id_type=MESH))
        return cps

    def start(ins, outs, sems):
        for cp in copies(ins, outs, sems):
            cp.start()

    def finish(ins, outs, sems):
        for cp in copies(ins, outs, sems):
            cp.wait()

    return _Side(grads, [jax.ShapeDtypeStruct((g.shape[0], g.shape[1] // 2, g.shape[2]), g.dtype) for g in grads],
                 [pltpu.SemaphoreType.DMA((n,))] * 2, start, finish)


def _scatter_side(chip_sums, small=None):
    n = len(chip_sums)
    arrs = list(chip_sums) + ([small] if small is not None else [])

    def copies(ins, outs, sems):
        x, y, c, chips = _place()
        cps = []
        for a in range(n):
            for j, (px, py) in enumerate(chips):
                cps.append(pltpu.make_async_remote_copy(
                    src_ref=ins[a].at[2 * px + py], dst_ref=outs[a].at[j],
                    send_sem=sems[0].at[a, j], recv_sem=sems[1].at[a, j], device_id=(px, py, c), device_id_type=MESH))
        if small is not None:
            for r, (fx, fy, fc) in enumerate(_relations(), start=1):
                px, py, pc = x ^ fx, y ^ fy, c ^ fc
                cps.append(pltpu.make_async_remote_copy(
                    src_ref=ins[n].at[4 * px + 2 * py + pc], dst_ref=outs[n].at[r],
                    send_sem=sems[2].at[r - 1], recv_sem=sems[3].at[r - 1], device_id=(px, py, pc),
                    device_id_type=MESH))
        return cps

    def start(ins, outs, sems):
        for cp in copies(ins, outs, sems):
            cp.start()

    def finish(ins, outs, sems):
        for cp in copies(ins, outs, sems):
            cp.wait()

    shapes = [jax.ShapeDtypeStruct((3,) + t.shape[1:], t.dtype) for t in chip_sums]
    sems = [pltpu.SemaphoreType.DMA((n, 3))] * 2
    if small is not None:
        shapes.append(jax.ShapeDtypeStruct(small.shape, small.dtype))
        sems += [pltpu.SemaphoreType.DMA((7,))] * 2
    return _Side(arrs, shapes, sems, start, finish)


def _small_scatter_side(small):
    def copies(ins, outs, sems):
        x, y, c, _ = _place()
        cps = []
        for r, (fx, fy, fc) in enumerate(_relations(), start=1):
            px, py, pc = x ^ fx, y ^ fy, c ^ fc
            cps.append(pltpu.make_async_remote_copy(
                src_ref=ins[0].at[4 * px + 2 * py + pc], dst_ref=outs[0].at[r],
                send_sem=sems[0].at[r - 1], recv_sem=sems[1].at[r - 1], device_id=(px, py, pc), device_id_type=MESH))
        return cps

    def start(ins, outs, sems):
        for cp in copies(ins, outs, sems):
            cp.start()

    def finish(ins, outs, sems):
        for cp in copies(ins, outs, sems):
            cp.wait()

    return _Side([small], [jax.ShapeDtypeStruct(small.shape, small.dtype)], [pltpu.SemaphoreType.DMA((7,))] * 2,
                 start, finish)


def _small_share_side(small):
    return _share_side([], small)


def _share_side(halves, small=None):
    n = len(halves)
    arrs = list(halves) + ([small] if small is not None else [])

    def copies(ins, outs, sems, mine):
        x, y, c, _ = _place()
        me = 4 * x + 2 * y + c
        cps = []
        for a in range(n):
            hr = halves[a].shape[0] // 2
            rows = pl.ds((c if mine else 1 - c) * hr, hr)
            cps.append(pltpu.make_async_remote_copy(
                src_ref=ins[a].at[rows], dst_ref=outs[a].at[rows],
                send_sem=sems[0].at[a], recv_sem=sems[1].at[a], device_id=(x, y, 1 - c), device_id_type=MESH))
        if small is not None:
            for r, (fx, fy, fc) in enumerate(_relations(), start=1):
                px, py, pc = x ^ fx, y ^ fy, c ^ fc
                seg = me if mine else 4 * px + 2 * py + pc
                cps.append(pltpu.make_async_remote_copy(
                    src_ref=ins[n].at[seg], dst_ref=outs[n].at[seg],
                    send_sem=sems[-2].at[r - 1], recv_sem=sems[-1].at[r - 1], device_id=(px, py, pc),
                    device_id_type=MESH))
        return cps

    def start(ins, outs, sems):
        for cp in copies(ins, outs, sems, True):
            cp.start()

    def finish(ins, outs, sems):
        for cp in copies(ins, outs, sems, False):
            cp.wait_recv()
        for cp in copies(ins, outs, sems, True):
            cp.wait_send()

    sems = ([pltpu.SemaphoreType.DMA((n,))] * 2 if n else []) + (
        [pltpu.SemaphoreType.DMA((7,))] * 2 if small is not None else [])
    return _Side(arrs, [jax.ShapeDtypeStruct(h.shape, h.dtype) for h in arrs], sems, start, finish,
                 aliases={i: i for i in range(len(arrs))})


GATHER_PIECES = [(0, 0), (0, 1), (1, 0), (2, 0), (1, 1), (2, 1), (3, 0), (3, 1)]


def _mm_gathering(a, shard, order, *, name, tm=1024):
    s, k = a.shape
    nc = shard.shape[1]
    tm = _row_tile(s, tm)
    tn = nc // 2
    hr = k // 2
    qr = hr // 2
    blocks = jnp.stack([order[src] * 2 + h for src, h in GATHER_PIECES]).astype(jnp.int32)

    def body(blocks_ref, a_ref, shard_ref, z_ref, full_ref, wbuf, send_ici, recv_ici, send_relay,
             recv_relay, send_d2d, recv_d2d, local_sem, load_sem):
        piece, i = pl.program_id(0), pl.program_id(1)
        x, y, c, chips = _place()
        me = 2 * x + y
        nbrs = chips[:2]
        chip_of = [2 * px + py for px, py in chips]

        def quarter(q):
            return pl.ds(c * hr + q * qr, qr)

        def sibling_quarter(q):
            return pl.ds((1 - c) * hr + q * qr, qr)

        def whole(half):
            return pl.ds(half * hr, hr)

        def cols(h):
            return pl.ds(h * tn, tn)

        def direct(j, src_chip, h):
            return pltpu.make_async_remote_copy(
                src_ref=shard_ref.at[whole(c), cols(h)], dst_ref=full_ref.at[src_chip, whole(c), cols(h)],
                send_sem=send_ici.at[j, h], recv_sem=recv_ici.at[j, h], device_id=(*nbrs[j], c), device_id_type=MESH)

        def relay(j, src_chip, h):
            blk = full_ref.at[src_chip, quarter(j), cols(h)]
            return pltpu.make_async_remote_copy(
                src_ref=blk, dst_ref=blk, send_sem=send_relay.at[j, h], recv_sem=recv_relay.at[j, h],
                device_id=(*nbrs[1 - j], c), device_id_type=MESH)

        def d2d(j, chip, rows, h):
            blk = full_ref.at[chip, rows, cols(h)]
            return pltpu.make_async_remote_copy(
                src_ref=blk, dst_ref=blk, send_sem=send_d2d.at[j, h], recv_sem=recv_d2d.at[j, h],
                device_id=(x, y, 1 - c), device_id_type=MESH)

        def load(p):
            src, h = GATHER_PIECES[p]
            where = shard_ref if src == 0 else full_ref.at[chip_of[src - 1]]
            return pltpu.make_async_copy(where.at[:, cols(h)], wbuf.at[p % 2], load_sem.at[p % 2])

        local = pltpu.make_async_copy(shard_ref, full_ref.at[me], local_sem)

        def arrived(p):
            src, h = GATHER_PIECES[p]
            if src in (1, 2):
                j = src - 1
                direct(j, chip_of[j], h).wait_recv()
                relay(j, chip_of[j], h).start()
                d2d(j, chip_of[j], whole(c), h).start()
            elif src == 3:
                for j in range(2):
                    relay(1 - j, chip_of[2], h).wait_recv()
                    d2d(2 + j, chip_of[2], quarter(1 - j), h).start()

        def fetch(p):
            src, h = GATHER_PIECES[p]
            if src in (1, 2):
                d2d(src - 1, chip_of[src - 1], whole(1 - c), h).wait_recv()
            elif src == 3:
                for j in range(2):
                    d2d(2 + j, chip_of[2], sibling_quarter(1 - j), h).wait_recv()
            load(p).start()

        n_i = s // tm
        for p in range(len(GATHER_PIECES)):
            @pl.when(jnp.logical_and(piece == p, i == 0))
            def _(p=p):
                if p == 0:
                    local.start()
                    for hh in range(2):
                        for j in range(2):
                            direct(j, me, hh).start()
                    load(0).start()
                load(p).wait()

        z_ref[...] = jnp.dot(a_ref[...], wbuf[piece % 2], preferred_element_type=F32).astype(z_ref.dtype)

        for p in range(len(GATHER_PIECES) - 1):
            @pl.when(jnp.logical_and(piece == p, i == min(1, n_i - 1)))
            def _(p=p):
                arrived(p + 1)

            @pl.when(jnp.logical_and(piece == p, i == min(2, n_i - 1)))
            def _(p=p):
                fetch(p + 1)

        last = jnp.logical_and(piece == len(GATHER_PIECES) - 1, i == n_i - 1)

        @pl.when(last)
        def _():
            for h in range(2):
                for j in range(2):
                    direct(j, me, h).wait_send()
                    relay(j, chip_of[j], h).wait_send()
                    d2d(j, chip_of[j], whole(c), h).wait_send()
                    d2d(2 + j, chip_of[2], quarter(1 - j), h).wait_send()
            local.wait()

    return pl.pallas_call(
        body,
        grid_spec=pltpu.PrefetchScalarGridSpec(
            num_scalar_prefetch=1, grid=(len(GATHER_PIECES), s // tm),
            in_specs=[pl.BlockSpec((tm, k), lambda p, i, blocks: (i, 0)), HBM],
            out_specs=[pl.BlockSpec((tm, tn), lambda p, i, blocks: (i, blocks[p])), HBM],
            scratch_shapes=[pltpu.VMEM((2, k, tn), BF16)] + [pltpu.SemaphoreType.DMA((2, 2))] * 4
            + [pltpu.SemaphoreType.DMA((4, 2))] * 2 + [pltpu.SemaphoreType.DMA, pltpu.SemaphoreType.DMA((2,))]),
        out_shape=[jax.ShapeDtypeStruct((s, N_CHIPS * nc), BF16), jax.ShapeDtypeStruct((N_CHIPS, k, nc), BF16)],
        name=name, compiler_params=_cparams(),
    )(blocks, a, shard)


def _mm_tn_exchanging(a, b, *, name, shards, tk=2048, side=None):
    s, m = a.shape
    nc = b.shape[1] // shards
    tk = _row_tile(s, tk)
    nk = s // tk
    hm = m // 2

    def body(a_ref, b_ref, part_ref, sib_ref, acc, keep_sem, send_sem, recv_sem):
        j, kk = pl.program_id(0), pl.program_id(1)
        x, y, c, _ = _place()

        def keep(jj, slot):
            mine = pl.ds(c * hm, hm)
            return pltpu.make_async_copy(acc.at[slot, mine], part_ref.at[jj], keep_sem.at[slot])

        def give(jj, slot):
            return pltpu.make_async_remote_copy(
                src_ref=acc.at[slot, pl.ds((1 - c) * hm, hm)], dst_ref=sib_ref.at[jj],
                send_sem=send_sem.at[slot], recv_sem=recv_sem.at[jj], device_id=(x, y, 1 - c), device_id_type=MESH)

        part = lax.dot_general(a_ref[...], b_ref[...], TN, preferred_element_type=F32)
        for slot in range(2):
            @pl.when(j % 2 == slot)
            def _(slot=slot):
                @pl.when(jnp.logical_and(kk == 0, j >= 2))
                def _():
                    keep(j - 2, slot).wait()
                    give(j - 2, slot).wait_send()

                @pl.when(kk == 0)
                def _():
                    acc[slot] = part

                @pl.when(kk > 0)
                def _():
                    acc[slot] += part

                @pl.when(kk == nk - 1)
                def _():
                    keep(j, slot).start()
                    give(j, slot).start()

        @pl.when(jnp.logical_and(j == shards - 1, kk == nk - 1))
        def _():
            for jj in range(shards - 2, shards):
                keep(jj, jj % 2).wait()
                give(jj, jj % 2).wait_send()
            for jj in range(shards):
                give(jj, jj % 2).wait_recv()

    assert shards >= 2
    return _call(
        body, grid=(shards, nk),
        in_specs=[pl.BlockSpec((tk, m), lambda j, kk: (kk, 0)), pl.BlockSpec((tk, nc), lambda j, kk: (kk, j))],
        out_specs=[HBM, HBM],
        out_shape=[jax.ShapeDtypeStruct((shards, hm, nc), F32), jax.ShapeDtypeStruct((shards, hm, nc), F32)],
        scratch=[pltpu.VMEM((2, m, nc), F32), pltpu.SemaphoreType.DMA((2,)), pltpu.SemaphoreType.DMA((2,)),
                 pltpu.SemaphoreType.DMA((shards,))],
        args=(a, b), name=name, side=side)


def _col_tile(cols):
    return cols if cols <= 2048 else 512


def _add_sibling(grad, recv, core, *, name):
    k, r, c = grad.shape
    hr = r // 2
    tr = min(hr, 256)
    tc = _col_tile(c)
    nrb = hr // tr

    def body(core_ref, g_ref, r_ref, o_ref):
        o_ref[...] = (g_ref[...] + r_ref[...]).astype(BF16)

    return pl.pallas_call(
        body,
        grid_spec=pltpu.PrefetchScalarGridSpec(
            num_scalar_prefetch=1, grid=(k, nrb, c // tc),
            in_specs=[pl.BlockSpec((None, tr, tc), lambda kk, i, j, core: (kk, core[0] * nrb + i, j)),
                      pl.BlockSpec((None, tr, tc), lambda kk, i, j, core: (kk, i, j))],
            out_specs=pl.BlockSpec((None, tr, tc), lambda kk, i, j, core: (kk, i, j))),
        out_shape=jax.ShapeDtypeStruct((k, hr, c), BF16), name=name, compiler_params=_cparams(),
    )(core, grad, recv)


def _sum_chips(grad, from_sibling, recv, place, *, name):
    _, hr, c = from_sibling.shape
    tr = min(hr, 256)
    tc = _col_tile(c)
    nrb = hr // tr

    def body(place_ref, g_ref, s_ref, r0_ref, r1_ref, r2_ref, o_ref):
        own = g_ref[...] + s_ref[...]
        o_ref[...] = ((own + r0_ref[...].astype(F32)) + r1_ref[...].astype(F32)) + r2_ref[...].astype(F32)

    def rspec(j):
        return pl.BlockSpec((None, tr, tc), lambda i, jj, place: (j, i, jj))

    return pl.pallas_call(
        body,
        grid_spec=pltpu.PrefetchScalarGridSpec(
            num_scalar_prefetch=1, grid=(nrb, c // tc),
            in_specs=[pl.BlockSpec((None, tr, tc), lambda i, jj, place: (place[0], place[1] * nrb + i, jj)),
                      pl.BlockSpec((None, tr, tc), lambda i, jj, place: (place[0], i, jj)),
                      rspec(0), rspec(1), rspec(2)],
            out_specs=pl.BlockSpec((tr, tc), lambda i, jj, place: (place[1] * nrb + i, jj))),
        out_shape=jax.ShapeDtypeStruct((2 * hr, c), F32), name=name, compiler_params=_cparams(),
    )(place, grad, from_sibling, recv, recv, recv)


def _add_halves(mine, theirs, *, name, side=None):
    k, hr, c = mine.shape
    tr = min(hr, 256)
    tc = _col_tile(c)

    def body(a_ref, b_ref, o_ref):
        o_ref[...] = (a_ref[...] + b_ref[...]).astype(BF16)

    spec = pl.BlockSpec((None, tr, tc), lambda kk, i, j: (kk, i, j))
    (out,), side_outs = _call(body, grid=(k, hr // tr, c // tc), in_specs=[spec, spec], out_specs=[spec],
                              out_shape=[jax.ShapeDtypeStruct((k, hr, c), BF16)], args=(mine, theirs), name=name,
                              side=side)
    return out, side_outs


def _sum_halves(mine, theirs, recv, place, *, name):
    _, hr, c = mine.shape
    tr = min(hr, 256)
    tc = _col_tile(c)
    nrb = hr // tr

    def body(place_ref, a_ref, b_ref, r0_ref, r1_ref, r2_ref, o_ref):
        own = a_ref[...] + b_ref[...]
        o_ref[...] = ((own + r0_ref[...].astype(F32)) + r1_ref[...].astype(F32)) + r2_ref[...].astype(F32)

    def rspec(j):
        return pl.BlockSpec((None, tr, tc), lambda i, jj, place: (j, i, jj))

    own_spec = pl.BlockSpec((None, tr, tc), lambda i, jj, place: (place[0], i, jj))
    return pl.pallas_call(
        body,
        grid_spec=pltpu.PrefetchScalarGridSpec(
            num_scalar_prefetch=1, grid=(nrb, c // tc),
            in_specs=[own_spec, own_spec, rspec(0), rspec(1), rspec(2)],
            out_specs=pl.BlockSpec((tr, tc), lambda i, jj, place: (place[1] * nrb + i, jj))),
        out_shape=jax.ShapeDtypeStruct((2 * hr, c), F32), name=name, compiler_params=_cparams(),
    )(place, mine, theirs, recv, recv, recv)


def _sum_small(small, recv, place):
    _, sr, _ = small.shape

    def body(place_ref, own_ref, r_ref, o_ref):
        acc = own_ref[...]
        for r in range(1, 8):
            acc = acc + r_ref[r]
        o_ref[...] = acc

    return pl.pallas_call(
        body,
        grid_spec=pltpu.PrefetchScalarGridSpec(
            num_scalar_prefetch=1, grid=(1,),
            in_specs=[pl.BlockSpec((None, sr, 128), lambda i, place: (place[2], 0, 0)),
                      pl.BlockSpec((8, sr, 128), lambda i, place: (0, 0, 0))],
            out_specs=pl.BlockSpec((None, sr, 128), lambda i, place: (place[2], 0, 0))),
        out_shape=jax.ShapeDtypeStruct(small.shape, F32), name="sum_small", compiler_params=_cparams(),
    )(place, small, recv)


def _spread_side(vec):
    def copies(ins, outs, sems):
        x, y, c, _ = _place()
        return [pltpu.make_async_remote_copy(
            src_ref=ins[0], dst_ref=outs[0].at[r], send_sem=sems[0].at[r - 1], recv_sem=sems[1].at[r - 1],
            device_id=(x ^ fx, y ^ fy, c ^ fc), device_id_type=MESH)
            for r, (fx, fy, fc) in enumerate(_relations(), start=1)]

    def start(ins, outs, sems):
        for cp in copies(ins, outs, sems):
            cp.start()

    def finish(ins, outs, sems):
        for cp in copies(ins, outs, sems):
            cp.wait()

    return _Side([vec], [jax.ShapeDtypeStruct((8,) + vec.shape, vec.dtype)], [pltpu.SemaphoreType.DMA((7,))] * 2,
                 start, finish)


def _sum_in_device_order(own, spread, place):
    def body(place_ref, own_ref, r_ref, o_ref):
        me = place_ref[2]
        acc = jnp.zeros_like(own_ref[...])
        for d in range(8):
            slot = jnp.where(me == d, 1, me ^ d)
            acc = acc + jnp.where(me == d, own_ref[...], r_ref[slot])
        o_ref[...] = acc

    return pl.pallas_call(
        body,
        grid_spec=pltpu.PrefetchScalarGridSpec(
            num_scalar_prefetch=1, grid=(1,),
            in_specs=[pl.BlockSpec(own.shape, lambda i, place: (0, 0)),
                      pl.BlockSpec(spread.shape, lambda i, place: (0, 0, 0))],
            out_specs=pl.BlockSpec(own.shape, lambda i, place: (0, 0))),
        out_shape=jax.ShapeDtypeStruct(own.shape, F32), name="sum_in_device_order", compiler_params=_cparams(),
    )(place, own, spread)


def _adamw(w, g, m, v, *, name):
    r, c = w.shape
    tr = 256 if r % 256 == 0 else r
    tc = _col_tile(c)
    bc1 = 1.0 - ADAM_B1 ** ADAM_STEP
    bc2 = 1.0 - ADAM_B2 ** ADAM_STEP

    def body(w_ref, g_ref, m_ref, v_ref, d_ref, nm_ref, nv_ref, gout_ref):
        gv = g_ref[...]
        nm = ADAM_B1 * m_ref[...] + (1.0 - ADAM_B1) * gv
        nv = ADAM_B2 * v_ref[...] + (1.0 - ADAM_B2) * (gv * gv)
        d_ref[...] = -ADAM_LR * ((nm / bc1) / (jnp.sqrt(nv / bc2) + ADAM_EPS) + ADAM_WD * w_ref[...])
        nm_ref[...] = nm
        nv_ref[...] = nv
        gout_ref[...] = gv

    spec = pl.BlockSpec((tr, tc), lambda i, j: (i, j))
    outs, _ = _call(body, grid=(r // tr, c // tc), in_specs=[spec] * 4, out_specs=[spec] * 4,
                    out_shape=[jax.ShapeDtypeStruct((r, c), F32)] * 4, args=(w, g, m, v), name=name)
    return outs


SMALL_ORDER = ["a_ws", "a_bs", "a_norm_g", "a_ln_g", "a_ln_b", "kv_norm_g", "b_kv", "b_norm_g", "b_bq",
               "b_sinks", "final_norm_g"]
SHARDED_SMALL = {"a_norm_g", "a_ln_g", "a_ln_b"}
PACK_TILE = 8 * 128


def _rows128(a):
    flat = a.reshape(-1)
    return jnp.pad(flat, (0, (-flat.shape[0]) % PACK_TILE)).reshape(-1, 128)


def _pack_rows(parts, multiple):
    rows = [_rows128(p) for p in parts]
    total = sum(r.shape[0] for r in rows)
    pad = (-total) % multiple
    if pad:
        rows.append(jnp.zeros((pad, 128), rows[0].dtype))
    return jnp.concatenate(rows, axis=0)


def _unpack_rows(packed, shapes):
    out, row = [], 0
    for shp in shapes:
        size = math.prod(shp)
        nrow = -(-size // PACK_TILE) * 8
        out.append(packed[row:row + nrow].reshape(-1)[:size].reshape(shp))
        row += nrow
    return out


WEIGHTS = ["a_norm_g", "a_w_in", "a_ln_g", "a_ln_b", "a_ws", "a_bs", "a_w_out", "kv_norm_g", "w_kv", "b_kv",
           "b_norm_g", "b_w_in", "b_bq", "b_sinks", "b_w_out", "final_norm_g"]
BIG = ["a_w_in", "a_w_out", "w_kv", "b_w_in", "b_w_out"]


class _Reduction:
    def __init__(self, names, partials, core, place, small=None):
        self.names, self.partials, self.core, self.place, self.small = names, partials, core, place, small

    def exchange_side(self):
        return _exchange_side(self.partials)

    def took_exchange(self, from_sibling):
        self.from_sibling = from_sibling
        self.chip_sums = [_add_sibling(g, r, self.core, name="add_sibling_" + n)
                          for g, r, n in zip(self.partials, from_sibling, self.names)]

    def scatter_side(self):
        return _scatter_side(self.chip_sums, self.small)

    def took_scatter(self, arrived):
        big = arrived[:len(self.names)]
        self.halves = [_sum_chips(g, fs, r, self.place, name="sum_chips_" + n)
                       for g, fs, r, n in zip(self.partials, self.from_sibling, big, self.names)]
        self.small_mine = _sum_small(self.small, arrived[-1], self.place) if self.small is not None else None

    def share_side(self):
        return _share_side(self.halves, self.small_mine)

    def took_share(self, shared):
        self.grads = dict(zip(self.names, shared[:len(self.names)]))
        self.small_full = shared[-1] if self.small is not None else None


def _step(x, loss_target, p, m, v):
    xi, yi, ci = lax.axis_index("x"), lax.axis_index("y"), lax.axis_index("c")
    chip = 2 * xi + yi
    device = 4 * xi + 2 * yi + ci
    core = jnp.reshape(ci, (1,)).astype(jnp.int32)
    place = jnp.stack([chip, ci, device]).astype(jnp.int32)
    x, tgt = x[0], loss_target[0]
    s = x.shape[0]
    cos, sin = _rope_tables(s)

    shard2d = {n: p[n].reshape(p[n].shape[-2:]) for n in BIG}
    shard_bf = {n: shard2d[n].astype(BF16) for n in BIG}
    ws = p["a_ws"][0]
    ws_t = jnp.swapaxes(ws, 1, 2)
    bs_t = p["a_bs"][0].T
    kv_norm_g, b_kv = p["kv_norm_g"].reshape(1, -1), p["b_kv"].reshape(1, -1)
    final_norm_g = p["final_norm_g"].reshape(1, -1)

    vec_shapes = [p[n].shape for n in ("a_norm_g", "a_ln_g", "a_ln_b")]
    vec_pack = _pack_rows([p["a_norm_g"], p["a_ln_g"], p["a_ln_b"]], 16)
    (vec_all,) = _comm_call(_gather_side([vec_pack]), "gather_vectors")
    vecs = [_unpack_rows(vec_all[k], vec_shapes) for k in range(N_CHIPS)]
    a_norm_g, a_ln_g, a_ln_b = (jnp.concatenate([vk[t] for vk in vecs], axis=-1) for t in range(3))

    (n_a,) = _rms_fwd(x, [a_norm_g], name="rms_a")
    order = jnp.stack([chip, 2 * (1 - xi) + yi, 2 * xi + (1 - yi), 2 * (1 - xi) + (1 - yi)]).astype(jnp.int32)
    z, a_w_in = _mm_gathering(n_a, shard_bf["a_w_in"], order, name="mm_a_in")
    y, (a_w_out,) = _gate_fwd(z, a_ln_g, a_ln_b, ws, bs_t, side=_gather_side([shard_bf["a_w_out"]]))
    a_w_out = a_w_out.reshape(A_WIDTH, D_MODEL)
    (h1, n_kv, n_b), (w_kv, b_w_in) = _mm_residual_norms(
        y, a_w_out, x, [kv_norm_g, p["b_norm_g"]], name="mm_a_out",
        side=_gather_side([shard_bf["w_kv"], shard_bf["b_w_in"]]))
    w_kv = w_kv.reshape(D_MODEL, 2 * KV_WIDTH)
    kv = _mm_nn(n_kv, w_kv, name="mm_kv", tn=2 * KV_WIDTH)
    kr, vv = _kv_rope(kv, b_kv, cos, sin)
    zb = _mm_nn(n_b, b_w_in, name="mm_b_in", tn=512, tm=1024, out_dtype=BF16)
    yb, (b_w_out,) = _attn_fwd(zb, kr, vv, cos, sin, p["b_bq"], p["b_sinks"], side=_gather_side([shard_bf["b_w_out"]]))
    b_w_out = b_w_out.reshape(B_WIDTH, D_MODEL)
    loss_blk, dh2, dh2b, d_final_g = _mm_residual_loss(yb, b_w_out, h1, tgt, final_norm_g, name="mm_b_out")

    d_b_w_out = _mm_tn(yb, dh2b, name="mm_d_b_w_out", tm=B_WIDTH, tn=D_MODEL)
    red_bo = _Reduction(["b_w_out"], [d_b_w_out.reshape(N_CHIPS, B_WIDTH // N_CHIPS, D_MODEL)], core, place)
    dyb, got = _mm_nt(dh2b, b_w_out, name="mm_dyb", out_dtype=BF16, side=red_bo.exchange_side())
    red_bo.took_exchange(got)
    (dzb, dk_rot, dv, d_bq, d_sinks), got = _attn_bwd(zb, dyb, kr, vv, cos, sin, p["b_bq"], p["b_sinks"],
                                                      side=red_bo.scatter_side())
    red_bo.took_scatter(got)
    dkv, d_b_kv = _kv_rope_bwd(dk_rot, dv, cos, sin)
    d_b_w_in = _mm_tn(n_b, dzb, name="mm_d_b_w_in", tm=D_MODEL, tn=512, shards=N_CHIPS)
    d_w_kv, got = _mm_tn(n_kv, dkv, name="mm_d_w_kv", tm=D_MODEL, tn=2 * KV_WIDTH, side=red_bo.share_side())
    red_bo.took_share(got)
    red_bi = _Reduction(["b_w_in", "w_kv"], [d_b_w_in, d_w_kv.reshape(N_CHIPS, D_MODEL // N_CHIPS, 2 * KV_WIDTH)],
                        core, place)
    (dh1, dh1b, d_kv_g, d_b_g), got = _mm_nt_rms_bwd(
        [(dkv, w_kv, kv_norm_g), (dzb, b_w_in, p["b_norm_g"])], h1, dh2, name="mm_dn_b", tm=256,
        side=red_bi.exchange_side())
    red_bi.took_exchange(got)

    d_a_w_out = _mm_tn(y, dh1b, name="mm_d_a_w_out", tm=1024, tn=D_MODEL)
    red_ao = _Reduction(["a_w_out"], [d_a_w_out.reshape(N_CHIPS, A_WIDTH // N_CHIPS, D_MODEL)], core, place)
    dy, got = _mm_nt(dh1b, a_w_out, name="mm_dy", tn=1024, out_dtype=BF16, side=red_ao.exchange_side())
    red_ao.took_exchange(got)
    sides = [red_bi.scatter_side(), red_ao.scatter_side()]
    (dz, d_ln_g, d_ln_b, d_ws, d_bs_t), got = _gate_bwd(z, dy, a_ln_g, a_ln_b, ws, ws_t, bs_t, side=_join(sides))
    got = _split(got, sides)
    red_bi.took_scatter(got[0])
    red_ao.took_scatter(got[1])
    small = {
        "a_ws": d_ws, "a_bs": d_bs_t.T, "a_ln_g": d_ln_g, "a_ln_b": d_ln_b,
        "kv_norm_g": d_kv_g, "b_kv": d_b_kv, "b_norm_g": d_b_g, "b_bq": d_bq,
        "b_sinks": d_sinks[0:1, :N_Q_HEADS], "final_norm_g": d_final_g,
    }
    packed = [n for n in SMALL_ORDER if n != "a_norm_g"]
    small_shapes = [small[n].shape for n in packed] + [(1, 1)]
    small_pack = _pack_rows([small[n] for n in packed] + [loss_blk[0:1, 0:1]], 64)
    seg = small_pack.shape[0] // 8
    small_pack = small_pack.reshape(8, seg, 128)
    sides = [red_bi.share_side(), red_ao.share_side(), _small_scatter_side(small_pack)]
    (d_a_w_in, from_sibling), got = _mm_tn_exchanging(n_a, dz, name="mm_d_a_w_in", shards=N_CHIPS, side=_join(sides))
    got = _split(got, sides)
    red_bi.took_share(got[0])
    red_ao.took_share(got[1])
    small_mine = _sum_small(small_pack, got[2][0], place)

    chip_sum, (small_all,) = _add_halves(d_a_w_in, from_sibling, name="add_sibling_a_w_in",
                                         side=_small_share_side(small_mine))
    (dx, _, d_a_g), (arrived,) = _mm_nt_rms_bwd([(dz, a_w_in, a_norm_g)], x, dh1, name="mm_dn_a", tm=256,
                                                side=_scatter_side([chip_sum]))
    half_ai = _sum_halves(d_a_w_in, from_sibling, arrived, place, name="sum_chips_a_w_in")
    d_a_g = _rows128(d_a_g)
    sides = [_share_side([half_ai]), _spread_side(d_a_g)]
    got = _split(_comm_call(_join(sides), "share_last"), sides)
    grad_ai = got[0][0]
    small_full = dict(zip(packed + ["loss"], _unpack_rows(small_all.reshape(8 * seg, 128), small_shapes)))
    small_full["a_norm_g"] = _sum_in_device_order(d_a_g, got[1][0], place).reshape(1, -1)
    loss = small_full["loss"].reshape(())

    grad_big = {**red_bo.grads, **red_bi.grads, **red_ao.grads, "a_w_in": grad_ai}
    grads = {}
    for n in SMALL_ORDER:
        gfull = small_full[n]
        if n in SHARDED_SMALL:
            width = p[n].shape[-1]
            gfull = lax.dynamic_slice_in_dim(gfull, chip * width, width, axis=-1)
        grads[n] = gfull.reshape(p[n].shape)

    delta, new_m, new_v = {}, {}, {}
    for n in BIG:
        d, nm, nv, g = _adamw(shard2d[n], grad_big[n], m[n].reshape(shard2d[n].shape),
                              v[n].reshape(shard2d[n].shape), name="adamw_" + n)
        delta[n], new_m[n], new_v[n] = d.reshape(p[n].shape), nm.reshape(p[n].shape), nv.reshape(p[n].shape)
        grads[n] = g.reshape(p[n].shape)
    shapes = [p[n].shape for n in SMALL_ORDER]
    packs = [_pack_rows([src[n] for n in SMALL_ORDER], 8) for src in (p, grads, m, v)]
    outs = _adamw(*packs, name="adamw_small")[:3]
    for res, packed in zip((delta, new_m, new_v), outs):
        for n, val in zip(SMALL_ORDER, _unpack_rows(packed, shapes)):
            res[n] = val

    return (loss, dx[None], *[grads[n] for n in WEIGHTS], *[delta[n] for n in WEIGHTS],
            *[new_m[n] for n in WEIGHTS], *[new_v[n] for n in WEIGHTS])


def kernel(x, a_norm_g, a_w_in, a_ln_g, a_ln_b, a_ws, a_bs, a_w_out, kv_norm_g, w_kv, b_kv, b_norm_g, b_w_in, b_bq, b_sinks, b_w_out, final_norm_g, loss_target, m_a_norm_g, m_a_w_in, m_a_ln_g, m_a_ln_b, m_a_ws, m_a_bs, m_a_w_out, m_kv_norm_g, m_w_kv, m_b_kv, m_b_norm_g, m_b_w_in, m_b_bq, m_b_sinks, m_b_w_out, m_final_norm_g, v_a_norm_g, v_a_w_in, v_a_ln_g, v_a_ln_b, v_a_ws, v_a_bs, v_a_w_out, v_kv_norm_g, v_w_kv, v_b_kv, v_b_norm_g, v_b_w_in, v_b_bq, v_b_sinks, v_b_w_out, v_final_norm_g):
    p = dict(a_norm_g=a_norm_g, a_w_in=a_w_in, a_ln_g=a_ln_g, a_ln_b=a_ln_b, a_ws=a_ws, a_bs=a_bs, a_w_out=a_w_out,
             kv_norm_g=kv_norm_g, w_kv=w_kv, b_kv=b_kv, b_norm_g=b_norm_g, b_w_in=b_w_in, b_bq=b_bq, b_sinks=b_sinks,
             b_w_out=b_w_out, final_norm_g=final_norm_g)
    m = dict(a_norm_g=m_a_norm_g, a_w_in=m_a_w_in, a_ln_g=m_a_ln_g, a_ln_b=m_a_ln_b, a_ws=m_a_ws, a_bs=m_a_bs,
             a_w_out=m_a_w_out, kv_norm_g=m_kv_norm_g, w_kv=m_w_kv, b_kv=m_b_kv, b_norm_g=m_b_norm_g, b_w_in=m_b_w_in,
             b_bq=m_b_bq, b_sinks=m_b_sinks, b_w_out=m_b_w_out, final_norm_g=m_final_norm_g)
    v = dict(a_norm_g=v_a_norm_g, a_w_in=v_a_w_in, a_ln_g=v_a_ln_g, a_ln_b=v_a_ln_b, a_ws=v_a_ws, a_bs=v_a_bs,
             a_w_out=v_a_w_out, kv_norm_g=v_kv_norm_g, w_kv=v_w_kv, b_kv=v_b_kv, b_norm_g=v_b_norm_g, b_w_in=v_b_w_in,
             b_bq=v_b_bq, b_sinks=v_b_sinks, b_w_out=v_b_w_out, final_norm_g=v_final_norm_g)
    return _step(x, loss_target, p, m, v)
```

```python
import functools
import math

import jax
import jax.numpy as jnp
from jax import lax
from jax.experimental import pallas as pl
from jax.experimental.pallas import tpu as pltpu

F32 = jnp.float32
BF16 = jnp.bfloat16

D_MODEL = 1024
CHUNK = 128
A_WIDTH = 2048
A_GROUPS = 16
HEAD_DIM = 64
N_Q_HEADS = 16
N_KV_HEADS = 2
Q_PER_KV = 8
B_WIDTH = 1024
KV_WIDTH = 128
ROPE_THETA = 10000.0
EPS = 1e-5
N_CHIPS = 4

ADAM_LR = 0.001
ADAM_B1 = 0.9
ADAM_B2 = 0.999
ADAM_EPS = 1e-08
ADAM_WD = 0.01
ADAM_STEP = 10

VMEM_LIMIT = 48 * 1024 * 1024
MESH = pl.DeviceIdType.MESH
NEG_BIG = -1e30
HBM = pl.BlockSpec(memory_space=pl.ANY)

NN = (((1,), (0,)), ((), ()))
NT = (((1,), (1,)), ((), ()))
TN = (((0,), (0,)), ((), ()))


def _cparams(**kw):
    return pltpu.CompilerParams(vmem_limit_bytes=VMEM_LIMIT, **kw)


class _Side:
    def __init__(self, ins, out_shapes, sems, start, finish, aliases=None, passing=None):
        self.ins, self.out_shapes, self.sems = list(ins), list(out_shapes), list(sems)
        self.start, self.finish = start, finish
        self.passing = passing or (lambda ins, outs, sems: None)
        self.aliases = dict(aliases or {})


def _join(sides):
    sides = [s for s in sides if s is not None]
    if not sides:
        return None
    offs, i, o, m = [], 0, 0, 0
    for s in sides:
        offs.append((i, o, m))
        i, o, m = i + len(s.ins), o + len(s.out_shapes), m + len(s.sems)

    def run(which):
        def go(ins, outs, sems):
            for s, (a, b, c) in zip(sides, offs):
                getattr(s, which)(ins[a:a + len(s.ins)], outs[b:b + len(s.out_shapes)], sems[c:c + len(s.sems)])
        return go

    aliases = {}
    for s, (a, b, _) in zip(sides, offs):
        aliases.update({a + k: b + v for k, v in s.aliases.items()})
    return _Side([x for s in sides for x in s.ins], [x for s in sides for x in s.out_shapes],
                 [x for s in sides for x in s.sems], run("start"), run("finish"), aliases, run("passing"))


def _split(side_outs, sides):
    out, pos = [], 0
    for s in sides:
        out.append(list(side_outs[pos:pos + len(s.out_shapes)]))
        pos += len(s.out_shapes)
    return out


def _call(body, *, grid, in_specs, out_specs, out_shape, args, name, scratch=(), side=None):
    in_specs, out_specs, out_shape, scratch = list(in_specs), list(out_specs), list(out_shape), list(scratch)
    if side is None:
        res = pl.pallas_call(body, grid=grid, in_specs=in_specs, out_specs=out_specs, out_shape=out_shape,
                             scratch_shapes=scratch, name=name, compiler_params=_cparams())(*args)
        return list(res), []
    n_in, n_out, n_sc = len(in_specs), len(out_specs), len(scratch)
    s_in, s_out = len(side.ins), len(side.out_shapes)

    def wrapped(*refs):
        ins, refs = refs[:n_in], refs[n_in:]
        side_ins, refs = refs[:s_in], refs[s_in:]
        outs, refs = refs[:n_out], refs[n_out:]
        side_outs, refs = refs[:s_out], refs[s_out:]
        scr, side_sems = refs[:n_sc], refs[n_sc:]
        step = 0
        for a, g in enumerate(grid):
            step = step * g + pl.program_id(a)
        steps = math.prod(grid)

        @pl.when(step == 0)
        def _():
            side.start(side_ins, side_outs, side_sems)

        body(*ins, *outs, *scr)

        @pl.when(step == (3 * (steps - 1)) // 4)
        def _():
            side.passing(side_ins, side_outs, side_sems)

        @pl.when(step == steps - 1)
        def _():
            side.finish(side_ins, side_outs, side_sems)

    res = pl.pallas_call(
        wrapped, grid=grid, in_specs=in_specs + [HBM] * s_in, out_specs=out_specs + [HBM] * s_out,
        out_shape=out_shape + side.out_shapes, scratch_shapes=scratch + side.sems,
        input_output_aliases={n_in + k: n_out + v for k, v in side.aliases.items()},
        name=name, compiler_params=_cparams(),
    )(*args, *side.ins)
    return list(res[:n_out]), list(res[n_out:])


def _comm_call(side, name):
    s_in, s_out = len(side.ins), len(side.out_shapes)

    def body(*refs):
        ins, outs, sems = refs[:s_in], refs[s_in:s_in + s_out], refs[s_in + s_out:]
        side.start(ins, outs, sems)
        side.passing(ins, outs, sems)
        side.finish(ins, outs, sems)

    return list(pl.pallas_call(
        body, in_specs=[HBM] * s_in, out_specs=[HBM] * s_out, out_shape=side.out_shapes, scratch_shapes=side.sems,
        input_output_aliases=side.aliases, name=name,
    )(*side.ins))


def _matmul(a, b, *, dims, grid, a_spec, b_spec, o_spec, out_shape, name, acc_axis=None,
            residual=None, r_spec=None, side=None):
    has_res = residual is not None

    def body(*refs):
        if has_res:
            a_ref, b_ref, r_ref, o_ref = refs
        else:
            a_ref, b_ref, o_ref = refs
        part = lax.dot_general(a_ref[...], b_ref[...], dims, preferred_element_type=F32)
        if acc_axis is None:
            if has_res:
                part = part + r_ref[...]
            o_ref[...] = part.astype(o_ref.dtype)
        else:
            k = pl.program_id(acc_axis)

            @pl.when(k == 0)
            def _():
                o_ref[...] = part

            @pl.when(k > 0)
            def _():
                o_ref[...] += part

    in_specs = [a_spec, b_spec] + ([r_spec] if has_res else [])
    args = (a, b) + ((residual,) if has_res else ())
    (out,), side_outs = _call(body, grid=grid, in_specs=in_specs, out_specs=[o_spec], out_shape=[out_shape],
                              args=args, name=name, side=side)
    return (out, side_outs) if side is not None else out


def _row_tile(s, want):
    return min(s, want)


def _mm_nn(a, b, *, name, tn, out_dtype=F32, residual=None, tm=512, side=None):
    s, k = a.shape
    tm = _row_tile(s, tm)
    if b.ndim == 3:
        nsh, _, nc = b.shape
        npb = nc // tn
        n = nsh * nc
        b_spec = pl.BlockSpec((None, k, tn), lambda i, j: (j // npb, 0, j % npb))
    else:
        n = b.shape[1]
        b_spec = pl.BlockSpec((k, tn), lambda i, j: (0, j))
    return _matmul(
        a, b, dims=NN, grid=(s // tm, n // tn),
        a_spec=pl.BlockSpec((tm, k), lambda i, j: (i, 0)), b_spec=b_spec,
        o_spec=pl.BlockSpec((tm, tn), lambda i, j: (i, j)),
        out_shape=jax.ShapeDtypeStruct((s, n), out_dtype), name=name, side=side,
        residual=residual, r_spec=pl.BlockSpec((tm, tn), lambda i, j: (i, j)) if residual is not None else None)


def _mm_nt(a, b, *, name, tn=None, tm=512, out_dtype=F32, side=None):
    s, k = a.shape
    tm = _row_tile(s, tm)
    n = b.shape[0]
    tn = n if tn is None else tn
    return _matmul(
        a, b, dims=NT, grid=(s // tm, n // tn),
        a_spec=pl.BlockSpec((tm, k), lambda i, j: (i, 0)),
        b_spec=pl.BlockSpec((tn, k), lambda i, j: (j, 0)),
        o_spec=pl.BlockSpec((tm, tn), lambda i, j: (i, j)),
        out_shape=jax.ShapeDtypeStruct((s, n), out_dtype), name=name, side=side)


def _mm_tn(a, b, *, name, tm, tn, tk=2048, shards=None, side=None):
    s, m = a.shape
    n = b.shape[1]
    tk = _row_tile(s, tk)
    if shards is None:
        o_spec = pl.BlockSpec((tm, tn), lambda i, j, kk: (i, j))
        out_shape = jax.ShapeDtypeStruct((m, n), F32)
    else:
        assert tm == m
        nc = n // shards
        npb = nc // tn
        o_spec = pl.BlockSpec((None, m, tn), lambda i, j, kk: (j // npb, 0, j % npb))
        out_shape = jax.ShapeDtypeStruct((shards, m, nc), F32)
    return _matmul(
        a, b, dims=TN, grid=(m // tm, n // tn, s // tk), acc_axis=2,
        a_spec=pl.BlockSpec((tk, tm), lambda i, j, kk: (kk, i)),
        b_spec=pl.BlockSpec((tk, tn), lambda i, j, kk: (kk, j)),
        o_spec=o_spec, out_shape=out_shape, name=name, side=side)


def _rstd(x):
    return lax.rsqrt(jnp.mean(x * x, axis=-1, keepdims=True) + EPS)


def _rms_fwd(x, gains, *, name, tr=1024):
    s, d = x.shape
    tr = _row_tile(s, tr)
    ng = len(gains)

    def body(*refs):
        xv = refs[0][...]
        xh = xv * _rstd(xv)
        for t in range(ng):
            refs[1 + ng + t][...] = (xh * refs[1 + t][...]).astype(BF16)

    row = pl.BlockSpec((tr, d), lambda i: (i, 0))
    vec = pl.BlockSpec((1, d), lambda i: (0, 0))
    outs, _ = _call(body, grid=(s // tr,), in_specs=[row] + [vec] * ng, out_specs=[row] * ng,
                    out_shape=[jax.ShapeDtypeStruct((s, d), BF16)] * ng, args=(x, *gains), name=name)
    return outs


def _accumulate(i, ref, value):
    @pl.when(i == 0)
    def _():
        ref[...] = value

    @pl.when(i > 0)
    def _():
        ref[...] += value


def _mm_residual_norms(y, w, res, gains, *, name, tm=512, side=None):
    s, k = y.shape
    d = w.shape[1]
    tm = _row_tile(s, tm)
    ng = len(gains)

    def body(y_ref, w_ref, r_ref, *rest):
        g_refs, h_ref, n_refs = rest[:ng], rest[ng], rest[ng + 1:]
        h = r_ref[...] + jnp.dot(y_ref[...], w_ref[...], preferred_element_type=F32)
        h_ref[...] = h
        xh = h * _rstd(h)
        for t in range(ng):
            n_refs[t][...] = (xh * g_refs[t][...]).astype(BF16)

    row = pl.BlockSpec((tm, d), lambda i: (i, 0))
    vec = pl.BlockSpec((1, d), lambda i: (0, 0))
    return _call(
        body, grid=(s // tm,),
        in_specs=[pl.BlockSpec((tm, k), lambda i: (i, 0)), pl.BlockSpec((k, d), lambda i: (0, 0)), row] + [vec] * ng,
        out_specs=[row] * (1 + ng),
        out_shape=[jax.ShapeDtypeStruct((s, d), F32)] + [jax.ShapeDtypeStruct((s, d), BF16)] * ng,
        args=(y, w, res, *gains), name=name, side=side)


def _mm_residual_loss(y, w, res, tgt, gain, *, name, tm=512):
    s, k = y.shape
    d = w.shape[1]
    tm = _row_tile(s, tm)

    def body(y_ref, w_ref, r_ref, t_ref, g_ref, loss_ref, dh_ref, dhb_ref, dg_ref):
        i = pl.program_id(0)
        hv = r_ref[...] + jnp.dot(y_ref[...], w_ref[...], preferred_element_type=F32)
        g = g_ref[...]
        r = _rstd(hv)
        xh = hv * r
        diff = xh * g - t_ref[...]
        part = 0.5 / d * jnp.sum(jnp.sum(diff * diff, axis=-1, keepdims=True), axis=0, keepdims=True)
        dout = diff * (1.0 / d)
        a = dout * g
        dh = r * (a - xh * jnp.mean(a * xh, axis=-1, keepdims=True))
        dh_ref[...] = dh
        dhb_ref[...] = dh.astype(BF16)
        _accumulate(i, dg_ref, jnp.sum(dout * xh, axis=0, keepdims=True))
        _accumulate(i, loss_ref, jnp.broadcast_to(part, (8, 128)))

    row = pl.BlockSpec((tm, d), lambda i: (i, 0))
    vec = pl.BlockSpec((1, d), lambda i: (0, 0))
    outs, _ = _call(
        body, grid=(s // tm,),
        in_specs=[pl.BlockSpec((tm, k), lambda i: (i, 0)), pl.BlockSpec((k, d), lambda i: (0, 0)), row, row, vec],
        out_specs=[pl.BlockSpec((8, 128), lambda i: (0, 0)), row, row, vec],
        out_shape=[jax.ShapeDtypeStruct((8, 128), F32), jax.ShapeDtypeStruct((s, d), F32),
                   jax.ShapeDtypeStruct((s, d), BF16), jax.ShapeDtypeStruct((1, d), F32)],
        args=(y, w, res, tgt, gain), name=name)
    return outs


def _mm_nt_rms_bwd(terms, x, dres, *, name, tm, side=None):
    s, d = x.shape
    tm = _row_tile(s, tm)
    nt = len(terms)

    def body(*refs):
        a_refs, b_refs, g_refs = refs[0:3 * nt:3], refs[1:3 * nt:3], refs[2:3 * nt:3]
        x_ref, dres_ref = refs[3 * nt], refs[3 * nt + 1]
        dx_ref, dxb_ref = refs[3 * nt + 2], refs[3 * nt + 3]
        dg_refs = refs[3 * nt + 4:]
        i = pl.program_id(0)
        xv = x_ref[...]
        r = _rstd(xv)
        xh = xv * r
        acc = jnp.zeros_like(xv)
        for t in range(nt):
            b_ref = b_refs[t]
            if len(b_ref.shape) == 3:
                kc = b_ref.shape[2]
                dn = None
                for sh in range(b_ref.shape[0]):
                    part = lax.dot_general(a_refs[t][:, sh * kc:(sh + 1) * kc], b_ref[sh], NT, preferred_element_type=F32)
                    dn = part if dn is None else dn + part
            else:
                dn = lax.dot_general(a_refs[t][...], b_ref[...], NT, preferred_element_type=F32)
            acc = acc + dn * g_refs[t][...]
            _accumulate(i, dg_refs[t], jnp.sum(dn * xh, axis=0, keepdims=True))
        dx = dres_ref[...] + r * (acc - xh * jnp.mean(acc * xh, axis=-1, keepdims=True))
        dx_ref[...] = dx
        dxb_ref[...] = dx.astype(BF16)

    row = pl.BlockSpec((tm, d), lambda i: (i, 0))
    vec = pl.BlockSpec((1, d), lambda i: (0, 0))
    in_specs, args = [], []
    for a, b, g in terms:
        in_specs += [pl.BlockSpec((tm, a.shape[1]), lambda i: (i, 0)),
                     pl.BlockSpec(b.shape, (lambda i: (0, 0, 0)) if b.ndim == 3 else (lambda i: (0, 0))), vec]
        args += [a, b, g]
    return _call(
        body, grid=(s // tm,), in_specs=in_specs + [row, row], out_specs=[row, row] + [vec] * nt,
        out_shape=[jax.ShapeDtypeStruct((s, d), F32), jax.ShapeDtypeStruct((s, d), BF16)]
        + [jax.ShapeDtypeStruct((1, d), F32)] * nt,
        args=(*args, x, dres), name=name, side=side)


def _causal_mask(transposed=False):
    row = lax.broadcasted_iota(jnp.int32, (CHUNK, CHUNK), 0)
    col = lax.broadcasted_iota(jnp.int32, (CHUNK, CHUNK), 1)
    return col >= row if transposed else col <= row


def _silu_parts(g):
    sg = jax.nn.sigmoid(g)
    return g * sg, sg * (1.0 + g * (1.0 - sg))


def _gate_fwd(z, ln_g, ln_b, ws, bs_t, *, tr=256, side=None):
    s = z.shape[0]
    tr = _row_tile(s, tr)
    w = A_WIDTH

    def body(u_ref, v_ref, g_ref, lg_ref, lb_ref, ws_ref, bst_ref, y_ref):
        v = v_ref[...].astype(F32)
        mu = jnp.mean(v, axis=-1, keepdims=True)
        xc = v - mu
        rs = lax.rsqrt(jnp.mean(xc * xc, axis=-1, keepdims=True) + EPS)
        vln = (xc * rs * lg_ref[...] + lb_ref[...]).astype(BF16)
        mask = _causal_mask()
        for grp in range(A_GROUPS):
            cols = slice(grp * CHUNK, (grp + 1) * CHUNK)
            wsm = jnp.where(mask, ws_ref[grp], 0.0).astype(BF16)
            bcol = bst_ref[:, grp:grp + 1]
            for ci in range(tr // CHUNK):
                rows = slice(ci * CHUNK, (ci + 1) * CHUNK)
                sv = jnp.dot(wsm, vln[rows, cols], preferred_element_type=F32) + bcol
                gv = g_ref[rows, cols].astype(F32)
                y_ref[rows, cols] = (u_ref[rows, cols].astype(F32) * sv * (gv * jax.nn.sigmoid(gv))).astype(BF16)

    vec = pl.BlockSpec((1, w), lambda i: (0, 0))
    (y,), side_outs = _call(
        body, grid=(s // tr,),
        in_specs=[pl.BlockSpec((tr, w), lambda i: (i, 0)), pl.BlockSpec((tr, w), lambda i: (i, 1)),
                  pl.BlockSpec((tr, w), lambda i: (i, 2)), vec, vec,
                  pl.BlockSpec((A_GROUPS, CHUNK, CHUNK), lambda i: (0, 0, 0)),
                  pl.BlockSpec((CHUNK, A_GROUPS), lambda i: (0, 0))],
        out_specs=[pl.BlockSpec((tr, w), lambda i: (i, 0))],
        out_shape=[jax.ShapeDtypeStruct((s, w), BF16)], args=(z, z, z, ln_g, ln_b, ws, bs_t), name="gate_fwd",
        side=side)
    return y, side_outs


def _gate_bwd(z, dy, ln_g, ln_b, ws, ws_t, bs_t, *, tr=256, side=None):
    s = z.shape[0]
    tr = _row_tile(s, tr)
    w = A_WIDTH
    nsteps = s // tr

    def body(u_ref, v_ref, g_ref, dy_ref, lg_ref, lb_ref, ws_ref, wst_ref, bst_ref,
             dz_ref, dlg_ref, dlb_ref, dws_ref, dbst_ref, dvln_sc, dsv_sc):
        i = pl.program_id(0)

        @pl.when(i == 0)
        def _():
            dws_ref[...] = jnp.zeros_like(dws_ref)
            dsv_sc[...] = jnp.zeros_like(dsv_sc)

        v = v_ref[...].astype(F32)
        mu = jnp.mean(v, axis=-1, keepdims=True)
        xc = v - mu
        rs = lax.rsqrt(jnp.mean(xc * xc, axis=-1, keepdims=True) + EPS)
        xh = xc * rs
        lg = lg_ref[...]
        vln = (xh * lg + lb_ref[...]).astype(BF16)
        mask = _causal_mask()
        mask_t = _causal_mask(transposed=True)
        for grp in range(A_GROUPS):
            cols = slice(grp * CHUNK, (grp + 1) * CHUNK)
            wsm = jnp.where(mask, ws_ref[grp], 0.0).astype(BF16)
            wsm_t = jnp.where(mask_t, wst_ref[grp], 0.0).astype(BF16)
            bcol = bst_ref[:, grp:grp + 1]
            for ci in range(tr // CHUNK):
                rows = slice(ci * CHUNK, (ci + 1) * CHUNK)
                vb = vln[rows, cols]
                sv = jnp.dot(wsm, vb, preferred_element_type=F32) + bcol
                uv = u_ref[rows, cols].astype(F32)
                silu, dsilu = _silu_parts(g_ref[rows, cols].astype(F32))
                dyv = dy_ref[rows, cols].astype(F32)
                dyu = dyv * uv
                dz_ref[rows, cols] = (dyv * sv * silu).astype(BF16)
                dz_ref[rows, 2 * w + grp * CHUNK:2 * w + (grp + 1) * CHUNK] = (dyu * sv * dsilu).astype(BF16)
                dsv = dyu * silu
                dsvb = dsv.astype(BF16)
                dvln_sc[rows, cols] = jnp.dot(wsm_t, dsvb, preferred_element_type=F32)
                dws_ref[grp] += lax.dot_general(dsvb, vb, NT, preferred_element_type=F32)
                dsv_sc[grp] += dsv
        dvln = dvln_sc[...]
        dlg_t = jnp.sum(dvln * xh, axis=0, keepdims=True)
        dlb_t = jnp.sum(dvln, axis=0, keepdims=True)
        a = dvln * lg
        dv = rs * (a - jnp.mean(a, axis=-1, keepdims=True) - xh * jnp.mean(a * xh, axis=-1, keepdims=True))
        dz_ref[:, w:2 * w] = dv.astype(BF16)

        @pl.when(i == 0)
        def _():
            dlg_ref[...] = dlg_t
            dlb_ref[...] = dlb_t

        @pl.when(i > 0)
        def _():
            dlg_ref[...] += dlg_t
            dlb_ref[...] += dlb_t

        @pl.when(i == nsteps - 1)
        def _():
            for grp in range(A_GROUPS):
                dws_ref[grp] = jnp.where(mask, dws_ref[grp], 0.0)
                dbst_ref[:, grp:grp + 1] = jnp.sum(dsv_sc[grp], axis=-1, keepdims=True)

    vec = pl.BlockSpec((1, w), lambda i: (0, 0))
    wsspec = pl.BlockSpec((A_GROUPS, CHUNK, CHUNK), lambda i: (0, 0, 0))
    bsspec = pl.BlockSpec((CHUNK, A_GROUPS), lambda i: (0, 0))
    return _call(
        body, grid=(nsteps,),
        in_specs=[pl.BlockSpec((tr, w), lambda i: (i, 0)), pl.BlockSpec((tr, w), lambda i: (i, 1)),
                  pl.BlockSpec((tr, w), lambda i: (i, 2)), pl.BlockSpec((tr, w), lambda i: (i, 0)),
                  vec, vec, wsspec, wsspec, bsspec],
        out_specs=[pl.BlockSpec((tr, 3 * w), lambda i: (i, 0)), vec, vec, wsspec, bsspec],
        out_shape=[jax.ShapeDtypeStruct((s, 3 * w), BF16), jax.ShapeDtypeStruct((1, w), F32),
                   jax.ShapeDtypeStruct((1, w), F32), jax.ShapeDtypeStruct((A_GROUPS, CHUNK, CHUNK), F32),
                   jax.ShapeDtypeStruct((CHUNK, A_GROUPS), F32)],
        scratch=[pltpu.VMEM((tr, w), F32), pltpu.VMEM((A_GROUPS, CHUNK, CHUNK), F32)],
        args=(z, z, z, dy, ln_g, ln_b, ws, ws_t, bs_t), name="gate_bwd", side=side)


HEADS_PER_BLOCK = 128 // HEAD_DIM
BLOCKS_PER_KV = Q_PER_KV // HEADS_PER_BLOCK
SCALE = HEAD_DIM ** -0.5
LOG2_E = math.log2(math.e)


def _rope_tables(s):
    lane = jnp.arange(128)
    inv_freq = ROPE_THETA ** (-(2 * (lane % (HEAD_DIM // 2))).astype(F32) / HEAD_DIM)
    sign = jnp.where(lane % HEAD_DIM < HEAD_DIM // 2, -1.0, 1.0).astype(F32)
    ang = jnp.arange(s, dtype=F32)[:, None] * inv_freq[None, :]
    return jnp.cos(ang), jnp.sin(ang) * sign[None, :]


def _swap_halves(x):
    n = x.shape[-1]
    lane = lax.broadcasted_iota(jnp.int32, x.shape, x.ndim - 1)
    first = (lane % HEAD_DIM) < (HEAD_DIM // 2)
    return jnp.where(first, pltpu.roll(x, n - HEAD_DIM // 2, x.ndim - 1), pltpu.roll(x, HEAD_DIM // 2, x.ndim - 1))


def _left_half(rows):
    return lax.broadcasted_iota(jnp.int32, (rows, 128), 1) < HEAD_DIM


def _dup_heads(x):
    left = _left_half(x.shape[0])
    swapped = pltpu.roll(x, HEAD_DIM, 1)
    return jnp.concatenate([jnp.where(left, x, swapped), jnp.where(left, swapped, x)], axis=-1)


def _fold_heads(a):
    b0, b1 = a[:, :128], a[:, 128:]
    f0 = b0 + pltpu.roll(b0, HEAD_DIM, 1)
    f1 = b1 + pltpu.roll(b1, HEAD_DIM, 1)
    return jnp.where(_left_half(a.shape[0]), f0, f1)


def _kv_rope(kv, b_kv, cos, sin, *, tr=2048):
    s = kv.shape[0]
    tr = _row_tile(s, tr)

    def body(kv_ref, b_ref, c_ref, s_ref, k_ref, v_ref):
        x = kv_ref[...] + b_ref[...]
        k = x[:, :KV_WIDTH]
        k_ref[...] = _dup_heads(k * c_ref[...] + _swap_halves(k) * s_ref[...]).astype(BF16)
        v_ref[...] = _dup_heads(x[:, KV_WIDTH:]).astype(BF16)

    tab = pl.BlockSpec((tr, KV_WIDTH), lambda i: (i, 0))
    wide = pl.BlockSpec((tr, 2 * KV_WIDTH), lambda i: (i, 0))
    outs, _ = _call(body, grid=(s // tr,),
                    in_specs=[wide, pl.BlockSpec((1, 2 * KV_WIDTH), lambda i: (0, 0)), tab, tab],
                    out_specs=[wide, wide], out_shape=[jax.ShapeDtypeStruct((s, 2 * KV_WIDTH), BF16)] * 2,
                    args=(kv, b_kv, cos, sin), name="kv_rope")
    return outs


def _kv_rope_bwd(dk2, dv2, cos, sin, *, tr=2048):
    s = dk2.shape[0]
    tr = _row_tile(s, tr)

    def body(dk_ref, dv_ref, c_ref, s_ref, dkv_ref, db_ref):
        i = pl.program_id(0)
        d = _fold_heads(dk_ref[...])
        dk = d * c_ref[...] + _swap_halves(d * s_ref[...])
        dvv = _fold_heads(dv_ref[...])
        dkv_ref[:, :KV_WIDTH] = dk.astype(BF16)
        dkv_ref[:, KV_WIDTH:] = dvv.astype(BF16)
        sk = jnp.sum(dk, axis=0, keepdims=True)
        sv = jnp.sum(dvv, axis=0, keepdims=True)

        @pl.when(i == 0)
        def _():
            db_ref[:, :KV_WIDTH] = sk
            db_ref[:, KV_WIDTH:] = sv

        @pl.when(i > 0)
        def _():
            db_ref[:, :KV_WIDTH] += sk
            db_ref[:, KV_WIDTH:] += sv

    tab = pl.BlockSpec((tr, KV_WIDTH), lambda i: (i, 0))
    wide = pl.BlockSpec((tr, 2 * KV_WIDTH), lambda i: (i, 0))
    outs, _ = _call(body, grid=(s // tr,), in_specs=[wide, wide, tab, tab],
                    out_specs=[wide, pl.BlockSpec((1, 2 * KV_WIDTH), lambda i: (0, 0))],
                    out_shape=[jax.ShapeDtypeStruct((s, 2 * KV_WIDTH), BF16),
                               jax.ShapeDtypeStruct((1, 2 * KV_WIDTH), F32)],
                    args=(dk2, dv2, cos, sin), name="kv_rope_bwd")
    return outs


def _from_previous():
    cols = Q_PER_KV * CHUNK
    k = lax.broadcasted_iota(jnp.int32, (CHUNK, cols), 0)
    q = lax.broadcasted_iota(jnp.int32, (CHUNK, cols), 1) & (CHUNK - 1)
    return k > q


def _fold(x2, prev):
    return jnp.where(prev, x2[:CHUNK], x2[CHUNK:])


def _unfold(x, prev):
    zero = jnp.zeros_like(x)
    return jnp.concatenate([jnp.where(prev, x, zero), jnp.where(prev, zero, x)], axis=0)


def _stack_heads(blocks, left):
    parts = []
    for b in blocks:
        parts.append(jnp.where(left, b, jnp.zeros_like(b)))
        parts.append(jnp.where(left, jnp.zeros_like(b), b))
    return jnp.concatenate(parts, axis=0)


def _unstack_heads(xt):
    top = lax.broadcasted_iota(jnp.int32, (128, CHUNK), 0) < HEAD_DIM
    return [jnp.where(top, xt[:, (2 * b) * CHUNK:(2 * b + 1) * CHUNK], xt[:, (2 * b + 1) * CHUNK:(2 * b + 2) * CHUNK]).T
            for b in range(BLOCKS_PER_KV)]


def _sink_row(sk_ref, kvh):
    return jnp.concatenate([jnp.full((1, CHUNK), sk_ref[0, kvh * Q_PER_KV + r], F32) for r in range(Q_PER_KV)], axis=1)


def _stacked_probs(qs, kd, prev, sink, i):
    sc2 = lax.dot_general(kd, qs, NT, preferred_element_type=F32)
    no_previous = jnp.where(i > 0, 0.0, NEG_BIG)
    sc = jnp.where(prev, sc2[:CHUNK] + no_previous, sc2[CHUNK:])
    sink = sink * (1.0 / SCALE)
    m = jnp.maximum(jnp.max(sc, axis=0, keepdims=True), sink)
    p = jnp.exp2((sc - m) * (SCALE * LOG2_E))
    esink = jnp.exp2((sink - m) * (SCALE * LOG2_E))
    inv = 1.0 / (jnp.sum(p, axis=0, keepdims=True) + esink)
    return p * inv, esink * inv


def _lane_block(b):
    return slice(b * 128, (b + 1) * 128)


def _rope_blocks(zq_ref, bq_ref, cos, sin, kvh):
    out = []
    for b in range(BLOCKS_PER_KV):
        cols = _lane_block(kvh * BLOCKS_PER_KV + b)
        q = zq_ref[:, cols].astype(F32) + bq_ref[:, cols]
        out.append((q * cos + _swap_halves(q) * sin).astype(BF16))
    return out


def _attn_specs():
    qspec = pl.BlockSpec((CHUNK, B_WIDTH), lambda i: (i, 0))
    gspec = pl.BlockSpec((CHUNK, B_WIDTH), lambda i: (i, 1))
    prev = pl.BlockSpec((CHUNK, 2 * KV_WIDTH), lambda i: (jnp.maximum(i - 1, 0), 0))
    cur = pl.BlockSpec((CHUNK, 2 * KV_WIDTH), lambda i: (i, 0))
    tab = pl.BlockSpec((CHUNK, KV_WIDTH), lambda i: (i, 0))
    bq = pl.BlockSpec((1, B_WIDTH), lambda i: (0, 0))
    sinks = pl.BlockSpec(memory_space=pltpu.SMEM)
    return qspec, gspec, prev, cur, tab, bq, sinks


def _attn_fwd(zb, k2, v2, cos, sin, b_bq, sinks, *, side=None):
    s = zb.shape[0]

    def body(zq_ref, zg_ref, kp_ref, kc_ref, vp_ref, vc_ref, c_ref, s_ref, bq_ref, sk_ref, y_ref):
        i = pl.program_id(0)
        cos, sin = c_ref[...], s_ref[...]
        kcat = jnp.concatenate([kp_ref[...], kc_ref[...]], axis=0)
        vcat = jnp.concatenate([vp_ref[...], vc_ref[...]], axis=0)
        prev = _from_previous()
        left = _left_half(CHUNK)
        for kvh in range(N_KV_HEADS):
            qs = _stack_heads(_rope_blocks(zq_ref, bq_ref, cos, sin, kvh), left)
            p, _ = _stacked_probs(qs, kcat[:, _lane_block(kvh)], prev, _sink_row(sk_ref, kvh), i)
            ot = lax.dot_general(vcat[:, _lane_block(kvh)], _unfold(p, prev).astype(BF16), TN,
                                 preferred_element_type=F32)
            for b, ob in enumerate(_unstack_heads(ot)):
                cols = _lane_block(kvh * BLOCKS_PER_KV + b)
                gv = zg_ref[:, cols].astype(F32)
                y_ref[:, cols] = (ob * (gv * jax.nn.sigmoid(gv))).astype(BF16)

    qspec, gspec, prev, cur, tab, bq, sk = _attn_specs()
    (y,), side_outs = _call(body, grid=(s // CHUNK,), in_specs=[qspec, gspec, prev, cur, prev, cur, tab, tab, bq, sk],
                            out_specs=[qspec], out_shape=[jax.ShapeDtypeStruct((s, B_WIDTH), BF16)],
                            args=(zb, zb, k2, k2, v2, v2, cos, sin, b_bq, sinks), name="attn_fwd", side=side)
    return y, side_outs


def _attn_bwd(zb, dyb, k2, v2, cos, sin, b_bq, sinks, *, side=None):
    s = zb.shape[0]

    def body(zq_ref, zg_ref, dy_ref, kp_ref, kc_ref, vp_ref, vc_ref, c_ref, s_ref, bq_ref, sk_ref,
             dz_ref, dk_ref, dv_ref, dbq_ref, dsk_ref):
        i = pl.program_id(0)

        @pl.when(i == 0)
        def _():
            dk_ref[...] = jnp.zeros_like(dk_ref)
            dv_ref[...] = jnp.zeros_like(dv_ref)
            dbq_ref[...] = jnp.zeros_like(dbq_ref)
            dsk_ref[...] = jnp.zeros_like(dsk_ref)

        cos, sin = c_ref[...], s_ref[...]
        kcat = jnp.concatenate([kp_ref[...], kc_ref[...]], axis=0)
        vcat = jnp.concatenate([vp_ref[...], vc_ref[...]], axis=0)
        prev = _from_previous()
        left = _left_half(CHUNK)
        lane = lax.broadcasted_iota(jnp.int32, (1, 128), 1)
        dsk_row = jnp.zeros((1, 128), F32)
        cur_rows = pl.ds(pl.multiple_of(i * CHUNK, CHUNK), CHUNK)
        for kvh in range(N_KV_HEADS):
            kd, vd = kcat[:, _lane_block(kvh)], vcat[:, _lane_block(kvh)]
            qs = _stack_heads(_rope_blocks(zq_ref, bq_ref, cos, sin, kvh), left)
            p, psink = _stacked_probs(qs, kd, prev, _sink_row(sk_ref, kvh), i)
            pb = _unfold(p, prev).astype(BF16)
            ot = lax.dot_general(vd, pb, TN, preferred_element_type=F32)
            gates, dys = [], []
            for b in range(BLOCKS_PER_KV):
                cols = _lane_block(kvh * BLOCKS_PER_KV + b)
                gates.append(_silu_parts(zg_ref[:, cols].astype(F32)))
                dys.append(dy_ref[:, cols].astype(F32))
            dos = _stack_heads([(dyv * silu).astype(BF16) for dyv, (silu, _) in zip(dys, gates)], left)
            dp = _fold(lax.dot_general(vd, dos, NT, preferred_element_type=F32), prev)
            delta = jnp.sum(p * dp, axis=0, keepdims=True)
            ds = _unfold(p * (dp - delta) * SCALE, prev).astype(BF16)
            dqt = lax.dot_general(kd, ds, TN, preferred_element_type=F32)
            dk_part = jnp.dot(ds, qs, preferred_element_type=F32)
            dv_part = jnp.dot(pb, dos, preferred_element_type=F32)
            dk_ref[cur_rows, _lane_block(kvh)] += dk_part[CHUNK:]
            dv_ref[cur_rows, _lane_block(kvh)] += dv_part[CHUNK:]

            @pl.when(i > 0)
            def _(kvh=kvh, dk_part=dk_part, dv_part=dv_part):
                prev_rows = pl.ds(pl.multiple_of((i - 1) * CHUNK, CHUNK), CHUNK)
                dk_ref[prev_rows, _lane_block(kvh)] += dk_part[:CHUNK]
                dv_ref[prev_rows, _lane_block(kvh)] += dv_part[:CHUNK]

            sink_grad = psink * delta
            for r in range(Q_PER_KV):
                dsink = -jnp.sum(sink_grad[:, r * CHUNK:(r + 1) * CHUNK], axis=1, keepdims=True)
                dsk_row = dsk_row + jnp.where(lane == kvh * Q_PER_KV + r, dsink, 0.0)
            blocks = zip(_unstack_heads(ot), _unstack_heads(dqt), dys, gates)
            for b, (ob, dqr, dyv, (_, dsilu)) in enumerate(blocks):
                blk = kvh * BLOCKS_PER_KV + b
                dq = dqr * cos + _swap_halves(dqr * sin)
                dbq_ref[:, _lane_block(blk)] += jnp.sum(dq, axis=0, keepdims=True)
                dz_ref[:, _lane_block(blk)] = dq.astype(BF16)
                dz_ref[:, _lane_block(B_WIDTH // 128 + blk)] = (dyv * ob * dsilu).astype(BF16)
        dsk_ref[0:1, :] += dsk_row

    qspec, gspec, prev, cur, tab, bq, sk = _attn_specs()
    full = pl.BlockSpec((s, 2 * KV_WIDTH), lambda i: (0, 0))
    return _call(
        body, grid=(s // CHUNK,),
        in_specs=[qspec, gspec, qspec, prev, cur, prev, cur, tab, tab, bq, sk],
        out_specs=[pl.BlockSpec((CHUNK, 2 * B_WIDTH), lambda i: (i, 0)), full, full, bq,
                   pl.BlockSpec((8, 128), lambda i: (0, 0))],
        out_shape=[jax.ShapeDtypeStruct((s, 2 * B_WIDTH), BF16), jax.ShapeDtypeStruct((s, 2 * KV_WIDTH), F32),
                   jax.ShapeDtypeStruct((s, 2 * KV_WIDTH), F32), jax.ShapeDtypeStruct((1, B_WIDTH), F32),
                   jax.ShapeDtypeStruct((8, 128), F32)],
        args=(zb, zb, dyb, k2, k2, v2, v2, cos, sin, b_bq, sinks), name="attn_bwd", side=side)


def _place():
    x, y, c = lax.axis_index("x"), lax.axis_index("y"), lax.axis_index("c")
    return x, y, c, [(1 - x, y), (x, 1 - y), (1 - x, 1 - y)]


def _relations():
    return [(r >> 2 & 1, r >> 1 & 1, r & 1) for r in range(1, 8)]


def _gather_side(arrs):
    n = len(arrs)

    def copies(ins, outs, sems):
        send_ici, recv_ici, send_d2d, recv_d2d, local_sem = sems
        x, y, c, chips = _place()
        me = 2 * x + y

        def rows(a, half):
            hr = arrs[a].shape[0] // 2
            return pl.ds(half * hr, hr)

        def ici(a, j, src_chip, to):
            return pltpu.make_async_remote_copy(
                src_ref=ins[a].at[rows(a, c)], dst_ref=outs[a].at[src_chip, rows(a, c)],
                send_sem=send_ici.at[a, j], recv_sem=recv_ici.at[a, j], device_id=to, device_id_type=MESH)

        def d2d(a, j, chip, half):
            blk = outs[a].at[chip, rows(a, half)]
            return pltpu.make_async_remote_copy(
                src_ref=blk, dst_ref=blk, send_sem=send_d2d.at[a, j], recv_sem=recv_d2d.at[a, j],
                device_id=(x, y, 1 - c), device_id_type=MESH)

        local = [pltpu.make_async_copy(ins[a], outs[a].at[me], local_sem.at[a]) for a in range(n)]
        pairs = [(a, j, chip) for a in range(n) for j, chip in enumerate(chips)]
        return c, me, local, ici, d2d, pairs

    def start(ins, outs, sems):
        c, me, local, ici, _, pairs = copies(ins, outs, sems)
        for cp in local:
            cp.start()
        for a, j, chip in pairs:
            ici(a, j, me, (*chip, c)).start()

    def passing(ins, outs, sems):
        c, _, _, ici, d2d, pairs = copies(ins, outs, sems)
        for a, j, (px, py) in pairs:
            ici(a, j, 2 * px + py, (px, py, c)).wait_recv()
            d2d(a, j, 2 * px + py, c).start()

    def finish(ins, outs, sems):
        c, me, local, ici, d2d, pairs = copies(ins, outs, sems)
        for a, j, (px, py) in pairs:
            d2d(a, j, 2 * px + py, 1 - c).wait_recv()
        for a, j, (px, py) in pairs:
            ici(a, j, me, (px, py, c)).wait_send()
            d2d(a, j, 2 * px + py, c).wait_send()
        for cp in local:
            cp.wait()

    return _Side(arrs, [jax.ShapeDtypeStruct((N_CHIPS,) + a.shape, a.dtype) for a in arrs],
                 [pltpu.SemaphoreType.DMA((n, 3))] * 4 + [pltpu.SemaphoreType.DMA((n,))], start, finish,
                 passing=passing)


def _exchange_side(grads):
    n = len(grads)

    def copies(ins, outs, sems):
        send_sem, recv_sem = sems
        x, y, c, _ = _place()
        cps = []
        for a in range(n):
            hr = grads[a].shape[1] // 2
            cps.append(pltpu.make_async_remote_copy(
                src_ref=ins[a].at[:, pl.ds((1 - c) * hr, hr), :], dst_ref=outs[a],
                send_sem=send_sem.at[a], recv_sem=recv_sem.at[a], device_id=(x, y, 1 - c), device_id_type=MESH))
        return cps

    def start(ins, outs, sems):
        for cp in copies(ins, outs, sems):
            cp.start()

    def finish(ins, outs, sems):
        for cp in copies(ins, outs, sems):
            cp.wait()

    return _Side(grads, [jax.ShapeDtypeStruct((g.shape[0], g.shape[1] // 2, g.shape[2]), g.dtype) for g in grads],
                 [pltpu.SemaphoreType.DMA((n,))] * 2, start, finish)


def _scatter_side(chip_sums, small=None):
    n = len(chip_sums)
    arrs = list(chip_sums) + ([small] if small is not None else [])

    def copies(ins, outs, sems):
        x, y, c, chips = _place()
        cps = []
        for a in range(n):
            for j, (px, py) in enumerate(chips):
                cps.append(pltpu.make_async_remote_copy(
                    src_ref=ins[a].at[2 * px + py], dst_ref=outs[a].at[j],
                    send_sem=sems[0].at[a, j], recv_sem=sems[1].at[a, j], device_id=(px, py, c), device_id_type=MESH))
        if small is not None:
            for r, (fx, fy, fc) in enumerate(_relations(), start=1):
                px, py, pc = x ^ fx, y ^ fy, c ^ fc
                cps.append(pltpu.make_async_remote_copy(
                    src_ref=ins[n].at[4 * px + 2 * py + pc], dst_ref=outs[n].at[r],
                    send_sem=sems[2].at[r - 1], recv_sem=sems[3].at[r - 1], device_id=(px, py, pc),
                    device_id_type=MESH))
        return cps

    def start(ins, outs, sems):
        for cp in copies(ins, outs, sems):
            cp.start()

    def finish(ins, outs, sems):
        for cp in copies(ins, outs, sems):
            cp.wait()

    shapes = [jax.ShapeDtypeStruct((3,) + t.shape[1:], t.dtype) for t in chip_sums]
    sems = [pltpu.SemaphoreType.DMA((n, 3))] * 2
    if small is not None:
        shapes.append(jax.ShapeDtypeStruct(small.shape, small.dtype))
        sems += [pltpu.SemaphoreType.DMA((7,))] * 2
    return _Side(arrs, shapes, sems, start, finish)


def _small_scatter_side(small):
    def copies(ins, outs, sems):
        x, y, c, _ = _place()
        cps = []
        for r, (fx, fy, fc) in enumerate(_relations(), start=1):
            px, py, pc = x ^ fx, y ^ fy, c ^ fc
            cps.append(pltpu.make_async_remote_copy(
                src_ref=ins[0].at[4 * px + 2 * py + pc], dst_ref=outs[0].at[r],
                send_sem=sems[0].at[r - 1], recv_sem=sems[1].at[r - 1], device_id=(px, py, pc), device_id_type=MESH))
        return cps

    def start(ins, outs, sems):
        for cp in copies(ins, outs, sems):
            cp.start()

    def finish(ins, outs, sems):
        for cp in copies(ins, outs, sems):
            cp.wait()

    return _Side([small], [jax.ShapeDtypeStruct(small.shape, small.dtype)], [pltpu.SemaphoreType.DMA((7,))] * 2,
                 start, finish)


def _small_share_side(small):
    return _share_side([], small)


def _share_side(halves, small=None):
    n = len(halves)
    arrs = list(halves) + ([small] if small is not None else [])

    def copies(ins, outs, sems, mine):
        x, y, c, _ = _place()
        me = 4 * x + 2 * y + c
        cps = []
        for a in range(n):
            hr = halves[a].shape[0] // 2
            rows = pl.ds((c if mine else 1 - c) * hr, hr)
            cps.append(pltpu.make_async_remote_copy(
                src_ref=ins[a].at[rows], dst_ref=outs[a].at[rows],
                send_sem=sems[0].at[a], recv_sem=sems[1].at[a], device_id=(x, y, 1 - c), device_id_type=MESH))
        if small is not None:
            for r, (fx, fy, fc) in enumerate(_relations(), start=1):
                px, py, pc = x ^ fx, y ^ fy, c ^ fc
                seg = me if mine else 4 * px + 2 * py + pc
                cps.append(pltpu.make_async_remote_copy(
                    src_ref=ins[n].at[seg], dst_ref=outs[n].at[seg],
                    send_sem=sems[-2].at[r - 1], recv_sem=sems[-1].at[r - 1], device_id=(px, py, pc),
                    device_id_type=MESH))
        return cps

    def start(ins, outs, sems):
        for cp in copies(ins, outs, sems, True):
            cp.start()

    def finish(ins, outs, sems):
        for cp in copies(ins, outs, sems, False):
            cp.wait_recv()
        for cp in copies(ins, outs, sems, True):
            cp.wait_send()

    sems = ([pltpu.SemaphoreType.DMA((n,))] * 2 if n else []) + (
        [pltpu.SemaphoreType.DMA((7,))] * 2 if small is not None else [])
    return _Side(arrs, [jax.ShapeDtypeStruct(h.shape, h.dtype) for h in arrs], sems, start, finish,
                 aliases={i: i for i in range(len(arrs))})


GATHER_PIECES = [(0, 0), (0, 1), (1, 0), (2, 0), (1, 1), (2, 1), (3, 0), (3, 1)]


def _mm_gathering(a, shard, order, *, name, tm=1024):
    s, k = a.shape
    nc = shard.shape[1]
    tm = _row_tile(s, tm)
    tn = nc // 2
    hr = k // 2
    qr = hr // 2
    blocks = jnp.stack([order[src] * 2 + h for src, h in GATHER_PIECES]).astype(jnp.int32)

    def body(blocks_ref, a_ref, shard_ref, z_ref, full_ref, wbuf, send_ici, recv_ici, send_relay,
             recv_relay, send_d2d, recv_d2d, local_sem, load_sem):
        piece, i = pl.program_id(0), pl.program_id(1)
        x, y, c, chips = _place()
        me = 2 * x + y
        nbrs = chips[:2]
        chip_of = [2 * px + py for px, py in chips]

        def quarter(q):
            return pl.ds(c * hr + q * qr, qr)

        def sibling_quarter(q):
            return pl.ds((1 - c) * hr + q * qr, qr)

        def whole(half):
            return pl.ds(half * hr, hr)

        def cols(h):
            return pl.ds(h * tn, tn)

        def direct(j, src_chip, h):
            return pltpu.make_async_remote_copy(
                src_ref=shard_ref.at[whole(c), cols(h)], dst_ref=full_ref.at[src_chip, whole(c), cols(h)],
                send_sem=send_ici.at[j, h], recv_sem=recv_ici.at[j, h], device_id=(*nbrs[j], c), device_id_type=MESH)

        def relay(j, src_chip, h):
            blk = full_ref.at[src_chip, quarter(j), cols(h)]
            return pltpu.make_async_remote_copy(
                src_ref=blk, dst_ref=blk, send_sem=send_relay.at[j, h], recv_sem=recv_relay.at[j, h],
                device_id=(*nbrs[1 - j], c), device_id_type=MESH)

        def d2d(j, chip, rows, h):
            blk = full_ref.at[chip, rows, cols(h)]
            return pltpu.make_async_remote_copy(
                src_ref=blk, dst_ref=blk, send_sem=send_d2d.at[j, h], recv_sem=recv_d2d.at[j, h],
                device_id=(x, y, 1 - c), device_id_type=MESH)

        def load(p):
            src, h = GATHER_PIECES[p]
            where = shard_ref if src == 0 else full_ref.at[chip_of[src - 1]]
            return pltpu.make_async_copy(where.at[:, cols(h)], wbuf.at[p % 2], load_sem.at[p % 2])

        local = pltpu.make_async_copy(shard_ref, full_ref.at[me], local_sem)

        def arrived(p):
            src, h = GATHER_PIECES[p]
            if src in (1, 2):
                j = src - 1
                direct(j, chip_of[j], h).wait_recv()
                relay(j, chip_of[j], h).start()
                d2d(j, chip_of[j], whole(c), h).start()
            elif src == 3:
                for j in range(2):
                    relay(1 - j, chip_of[2], h).wait_recv()
                    d2d(2 + j, chip_of[2], quarter(1 - j), h).start()

        def fetch(p):
            src, h = GATHER_PIECES[p]
            if src in (1, 2):
                d2d(src - 1, chip_of[src - 1], whole(1 - c), h).wait_recv()
            elif src == 3:
                for j in range(2):
                    d2d(2 + j, chip_of[2], sibling_quarter(1 - j), h).wait_recv()
            load(p).start()

        n_i = s // tm
        for p in range(len(GATHER_PIECES)):
            @pl.when(jnp.logical_and(piece == p, i == 0))
            def _(p=p):
                if p == 0:
                    local.start()
                    for hh in range(2):
                        for j in range(2):
                            direct(j, me, hh).start()
                    load(0).start()
                load(p).wait()

        z_ref[...] = jnp.dot(a_ref[...], wbuf[piece % 2], preferred_element_type=F32).astype(z_ref.dtype)

        for p in range(len(GATHER_PIECES) - 1):
            @pl.when(jnp.logical_and(piece == p, i == min(1, n_i - 1)))
            def _(p=p):
                arrived(p + 1)

            @pl.when(jnp.logical_and(piece == p, i == min(2, n_i - 1)))
            def _(p=p):
                fetch(p + 1)

        last = jnp.logical_and(piece == len(GATHER_PIECES) - 1, i == n_i - 1)

        @pl.when(last)
        def _():
            for h in range(2):
                for j in range(2):
                    direct(j, me, h).wait_send()
                    relay(j, chip_of[j], h).wait_send()
                    d2d(j, chip_of[j], whole(c), h).wait_send()
                    d2d(2 + j, chip_of[2], quarter(1 - j), h).wait_send()
            local.wait()

    return pl.pallas_call(
        body,
        grid_spec=pltpu.PrefetchScalarGridSpec(
            num_scalar_prefetch=1, grid=(len(GATHER_PIECES), s // tm),
            in_specs=[pl.BlockSpec((tm, k), lambda p, i, blocks: (i, 0)), HBM],
            out_specs=[pl.BlockSpec((tm, tn), lambda p, i, blocks: (i, blocks[p])), HBM],
            scratch_shapes=[pltpu.VMEM((2, k, tn), BF16)] + [pltpu.SemaphoreType.DMA((2, 2))] * 4
            + [pltpu.SemaphoreType.DMA((4, 2))] * 2 + [pltpu.SemaphoreType.DMA, pltpu.SemaphoreType.DMA((2,))]),
        out_shape=[jax.ShapeDtypeStruct((s, N_CHIPS * nc), BF16), jax.ShapeDtypeStruct((N_CHIPS, k, nc), BF16)],
        name=name, compiler_params=_cparams(),
    )(blocks, a, shard)


def _mm_tn_exchanging(a, b, *, name, shards, tk=2048, side=None):
    s, m = a.shape
    nc = b.shape[1] // shards
    tk = _row_tile(s, tk)
    nk = s // tk
    hm = m // 2

    def body(a_ref, b_ref, part_ref, sib_ref, acc, keep_sem, send_sem, recv_sem):
        j, kk = pl.program_id(0), pl.program_id(1)
        x, y, c, _ = _place()

        def keep(jj, slot):
            mine = pl.ds(c * hm, hm)
            return pltpu.make_async_copy(acc.at[slot, mine], part_ref.at[jj], keep_sem.at[slot])

        def give(jj, slot):
            return pltpu.make_async_remote_copy(
                src_ref=acc.at[slot, pl.ds((1 - c) * hm, hm)], dst_ref=sib_ref.at[jj],
                send_sem=send_sem.at[slot], recv_sem=recv_sem.at[jj], device_id=(x, y, 1 - c), device_id_type=MESH)

        part = lax.dot_general(a_ref[...], b_ref[...], TN, preferred_element_type=F32)
        for slot in range(2):
            @pl.when(j % 2 == slot)
            def _(slot=slot):
                @pl.when(jnp.logical_and(kk == 0, j >= 2))
                def _():
                    keep(j - 2, slot).wait()
                    give(j - 2, slot).wait_send()

                @pl.when(kk == 0)
                def _():
                    acc[slot] = part

                @pl.when(kk > 0)
                def _():
                    acc[slot] += part

                @pl.when(kk == nk - 1)
                def _():
                    keep(j, slot).start()
                    give(j, slot).start()

        @pl.when(jnp.logical_and(j == shards - 1, kk == nk - 1))
        def _():
            for jj in range(shards - 2, shards):
                keep(jj, jj % 2).wait()
                give(jj, jj % 2).wait_send()
            for jj in range(shards):
                give(jj, jj % 2).wait_recv()

    assert shards >= 2
    return _call(
        body, grid=(shards, nk),
        in_specs=[pl.BlockSpec((tk, m), lambda j, kk: (kk, 0)), pl.BlockSpec((tk, nc), lambda j, kk: (kk, j))],
        out_specs=[HBM, HBM],
        out_shape=[jax.ShapeDtypeStruct((shards, hm, nc), F32), jax.ShapeDtypeStruct((shards, hm, nc), F32)],
        scratch=[pltpu.VMEM((2, m, nc), F32), pltpu.SemaphoreType.DMA((2,)), pltpu.SemaphoreType.DMA((2,)),
                 pltpu.SemaphoreType.DMA((shards,))],
        args=(a, b), name=name, side=side)


def _col_tile(cols):
    return cols if cols <= 2048 else 512


def _add_sibling(grad, recv, core, *, name):
    k, r, c = grad.shape
    hr = r // 2
    tr = min(hr, 256)
    tc = _col_tile(c)
    nrb = hr // tr

    def body(core_ref, g_ref, r_ref, o_ref):
        o_ref[...] = (g_ref[...] + r_ref[...]).astype(BF16)

    return pl.pallas_call(
        body,
        grid_spec=pltpu.PrefetchScalarGridSpec(
            num_scalar_prefetch=1, grid=(k, nrb, c // tc),
            in_specs=[pl.BlockSpec((None, tr, tc), lambda kk, i, j, core: (kk, core[0] * nrb + i, j)),
                      pl.BlockSpec((None, tr, tc), lambda kk, i, j, core: (kk, i, j))],
            out_specs=pl.BlockSpec((None, tr, tc), lambda kk, i, j, core: (kk, i, j))),
        out_shape=jax.ShapeDtypeStruct((k, hr, c), BF16), name=name, compiler_params=_cparams(),
    )(core, grad, recv)


def _sum_chips(grad, from_sibling, recv, place, *, name):
    _, hr, c = from_sibling.shape
    tr = min(hr, 256)
    tc = _col_tile(c)
    nrb = hr // tr

    def body(place_ref, g_ref, s_ref, r0_ref, r1_ref, r2_ref, o_ref):
        own = g_ref[...] + s_ref[...]
        o_ref[...] = ((own + r0_ref[...].astype(F32)) + r1_ref[...].astype(F32)) + r2_ref[...].astype(F32)

    def rspec(j):
        return pl.BlockSpec((None, tr, tc), lambda i, jj, place: (j, i, jj))

    return pl.pallas_call(
        body,
        grid_spec=pltpu.PrefetchScalarGridSpec(
            num_scalar_prefetch=1, grid=(nrb, c // tc),
            in_specs=[pl.BlockSpec((None, tr, tc), lambda i, jj, place: (place[0], place[1] * nrb + i, jj)),
                      pl.BlockSpec((None, tr, tc), lambda i, jj, place: (place[0], i, jj)),
                      rspec(0), rspec(1), rspec(2)],
            out_specs=pl.BlockSpec((tr, tc), lambda i, jj, place: (place[1] * nrb + i, jj))),
        out_shape=jax.ShapeDtypeStruct((2 * hr, c), F32), name=name, compiler_params=_cparams(),
    )(place, grad, from_sibling, recv, recv, recv)


def _add_halves(mine, theirs, *, name, side=None):
    k, hr, c = mine.shape
    tr = min(hr, 256)
    tc = _col_tile(c)

    def body(a_ref, b_ref, o_ref):
        o_ref[...] = (a_ref[...] + b_ref[...]).astype(BF16)

    spec = pl.BlockSpec((None, tr, tc), lambda kk, i, j: (kk, i, j))
    (out,), side_outs = _call(body, grid=(k, hr // tr, c // tc), in_specs=[spec, spec], out_specs=[spec],
                              out_shape=[jax.ShapeDtypeStruct((k, hr, c), BF16)], args=(mine, theirs), name=name,
                              side=side)
    return out, side_outs


def _sum_halves(mine, theirs, recv, place, *, name):
    _, hr, c = mine.shape
    tr = min(hr, 256)
    tc = _col_tile(c)
    nrb = hr // tr

    def body(place_ref, a_ref, b_ref, r0_ref, r1_ref, r2_ref, o_ref):
        own = a_ref[...] + b_ref[...]
        o_ref[...] = ((own + r0_ref[...].astype(F32)) + r1_ref[...].astype(F32)) + r2_ref[...].astype(F32)

    def rspec(j):
        return pl.BlockSpec((None, tr, tc), lambda i, jj, place: (j, i, jj))

    own_spec = pl.BlockSpec((None, tr, tc), lambda i, jj, place: (place[0], i, jj))
    return pl.pallas_call(
        body,
        grid_spec=pltpu.PrefetchScalarGridSpec(
            num_scalar_prefetch=1, grid=(nrb, c // tc),
            in_specs=[own_spec, own_spec, rspec(0), rspec(1), rspec(2)],
            out_specs=pl.BlockSpec((tr, tc), lambda i, jj, place: (place[1] * nrb + i, jj))),
        out_shape=jax.ShapeDtypeStruct((2 * hr, c), F32), name=name, compiler_params=_cparams(),
    )(place, mine, theirs, recv, recv, recv)


def _sum_small(small, recv, place):
    _, sr, _ = small.shape

    def body(place_ref, own_ref, r_ref, o_ref):
        acc = own_ref[...]
        for r in range(1, 8):
            acc = acc + r_ref[r]
        o_ref[...] = acc

    return pl.pallas_call(
        body,
        grid_spec=pltpu.PrefetchScalarGridSpec(
            num_scalar_prefetch=1, grid=(1,),
            in_specs=[pl.BlockSpec((None, sr, 128), lambda i, place: (place[2], 0, 0)),
                      pl.BlockSpec((8, sr, 128), lambda i, place: (0, 0, 0))],
            out_specs=pl.BlockSpec((None, sr, 128), lambda i, place: (place[2], 0, 0))),
        out_shape=jax.ShapeDtypeStruct(small.shape, F32), name="sum_small", compiler_params=_cparams(),
    )(place, small, recv)


def _spread_side(vec):
    def copies(ins, outs, sems):
        x, y, c, _ = _place()
        return [pltpu.make_async_remote_copy(
            src_ref=ins[0], dst_ref=outs[0].at[r], send_sem=sems[0].at[r - 1], recv_sem=sems[1].at[r - 1],
            device_id=(x ^ fx, y ^ fy, c ^ fc), device_id_type=MESH)
            for r, (fx, fy, fc) in enumerate(_relations(), start=1)]

    def start(ins, outs, sems):
        for cp in copies(ins, outs, sems):
            cp.start()

    def finish(ins, outs, sems):
        for cp in copies(ins, outs, sems):
            cp.wait()

    return _Side([vec], [jax.ShapeDtypeStruct((8,) + vec.shape, vec.dtype)], [pltpu.SemaphoreType.DMA((7,))] * 2,
                 start, finish)


def _sum_in_device_order(own, spread, place):
    def body(place_ref, own_ref, r_ref, o_ref):
        me = place_ref[2]
        acc = jnp.zeros_like(own_ref[...])
        for d in range(8):
            slot = jnp.where(me == d, 1, me ^ d)
            acc = acc + jnp.where(me == d, own_ref[...], r_ref[slot])
        o_ref[...] = acc

    return pl.pallas_call(
        body,
        grid_spec=pltpu.PrefetchScalarGridSpec(
            num_scalar_prefetch=1, grid=(1,),
            in_specs=[pl.BlockSpec(own.shape, lambda i, place: (0, 0)),
                      pl.BlockSpec(spread.shape, lambda i, place: (0, 0, 0))],
            out_specs=pl.BlockSpec(own.shape, lambda i, place: (0, 0))),
        out_shape=jax.ShapeDtypeStruct(own.shape, F32), name="sum_in_device_order", compiler_params=_cparams(),
    )(place, own, spread)


def _adamw(w, g, m, v, *, name):
    r, c = w.shape
    tr = 256 if r % 256 == 0 else r
    tc = _col_tile(c)
    bc1 = 1.0 - ADAM_B1 ** ADAM_STEP
    bc2 = 1.0 - ADAM_B2 ** ADAM_STEP

    def body(w_ref, g_ref, m_ref, v_ref, d_ref, nm_ref, nv_ref, gout_ref):
        gv = g_ref[...]
        nm = ADAM_B1 * m_ref[...] + (1.0 - ADAM_B1) * gv
        nv = ADAM_B2 * v_ref[...] + (1.0 - ADAM_B2) * (gv * gv)
        d_ref[...] = -ADAM_LR * ((nm / bc1) / (jnp.sqrt(nv / bc2) + ADAM_EPS) + ADAM_WD * w_ref[...])
        nm_ref[...] = nm
        nv_ref[...] = nv
        gout_ref[...] = gv

    spec = pl.BlockSpec((tr, tc), lambda i, j: (i, j))
    outs, _ = _call(body, grid=(r // tr, c // tc), in_specs=[spec] * 4, out_specs=[spec] * 4,
                    out_shape=[jax.ShapeDtypeStruct((r, c), F32)] * 4, args=(w, g, m, v), name=name)
    return outs


SMALL_ORDER = ["a_ws", "a_bs", "a_norm_g", "a_ln_g", "a_ln_b", "kv_norm_g", "b_kv", "b_norm_g", "b_bq",
               "b_sinks", "final_norm_g"]
SHARDED_SMALL = {"a_norm_g", "a_ln_g", "a_ln_b"}
PACK_TILE = 8 * 128


def _rows128(a):
    flat = a.reshape(-1)
    return jnp.pad(flat, (0, (-flat.shape[0]) % PACK_TILE)).reshape(-1, 128)


def _pack_rows(parts, multiple):
    rows = [_rows128(p) for p in parts]
    total = sum(r.shape[0] for r in rows)
    pad = (-total) % multiple
    if pad:
        rows.append(jnp.zeros((pad, 128), rows[0].dtype))
    return jnp.concatenate(rows, axis=0)


def _unpack_rows(packed, shapes):
    out, row = [], 0
    for shp in shapes:
        size = math.prod(shp)
        nrow = -(-size // PACK_TILE) * 8
        out.append(packed[row:row + nrow].reshape(-1)[:size].reshape(shp))
        row += nrow
    return out


WEIGHTS = ["a_norm_g", "a_w_in", "a_ln_g", "a_ln_b", "a_ws", "a_bs", "a_w_out", "kv_norm_g", "w_kv", "b_kv",
           "b_norm_g", "b_w_in", "b_bq", "b_sinks", "b_w_out", "final_norm_g"]
BIG = ["a_w_in", "a_w_out", "w_kv", "b_w_in", "b_w_out"]


class _Reduction:
    def __init__(self, names, partials, core, place, small=None):
        self.names, self.partials, self.core, self.place, self.small = names, partials, core, place, small

    def exchange_side(self):
        return _exchange_side(self.partials)

    def took_exchange(self, from_sibling):
        self.from_sibling = from_sibling
        self.chip_sums = [_add_sibling(g, r, self.core, name="add_sibling_" + n)
                          for g, r, n in zip(self.partials, from_sibling, self.names)]

    def scatter_side(self):
        return _scatter_side(self.chip_sums, self.small)

    def took_scatter(self, arrived):
        big = arrived[:len(self.names)]
        self.halves = [_sum_chips(g, fs, r, self.place, name="sum_chips_" + n)
                       for g, fs, r, n in zip(self.partials, self.from_sibling, big, self.names)]
        self.small_mine = _sum_small(self.small, arrived[-1], self.place) if self.small is not None else None

    def share_side(self):
        return _share_side(self.halves, self.small_mine)

    def took_share(self, shared):
        self.grads = dict(zip(self.names, shared[:len(self.names)]))
        self.small_full = shared[-1] if self.small is not None else None


def _step(x, loss_target, p, m, v):
    xi, yi, ci = lax.axis_index("x"), lax.axis_index("y"), lax.axis_index("c")
    chip = 2 * xi + yi
    device = 4 * xi + 2 * yi + ci
    core = jnp.reshape(ci, (1,)).astype(jnp.int32)
    place = jnp.stack([chip, ci, device]).astype(jnp.int32)
    x, tgt = x[0], loss_target[0]
    s = x.shape[0]
    cos, sin = _rope_tables(s)

    shard2d = {n: p[n].reshape(p[n].shape[-2:]) for n in BIG}
    shard_bf = {n: shard2d[n].astype(BF16) for n in BIG}
    ws = p["a_ws"][0]
    ws_t = jnp.swapaxes(ws, 1, 2)
    bs_t = p["a_bs"][0].T
    kv_norm_g, b_kv = p["kv_norm_g"].reshape(1, -1), p["b_kv"].reshape(1, -1)
    final_norm_g = p["final_norm_g"].reshape(1, -1)

    vec_shapes = [p[n].shape for n in ("a_norm_g", "a_ln_g", "a_ln_b")]
    vec_pack = _pack_rows([p["a_norm_g"], p["a_ln_g"], p["a_ln_b"]], 16)
    (vec_all,) = _comm_call(_gather_side([vec_pack]), "gather_vectors")
    vecs = [_unpack_rows(vec_all[k], vec_shapes) for k in range(N_CHIPS)]
    a_norm_g, a_ln_g, a_ln_b = (jnp.concatenate([vk[t] for vk in vecs], axis=-1) for t in range(3))

    (n_a,) = _rms_fwd(x, [a_norm_g], name="rms_a")
    order = jnp.stack([chip, 2 * (1 - xi) + yi, 2 * xi + (1 - yi), 2 * (1 - xi) + (1 - yi)]).astype(jnp.int32)
    z, a_w_in = _mm_gathering(n_a, shard_bf["a_w_in"], order, name="mm_a_in")
    y, (a_w_out,) = _gate_fwd(z, a_ln_g, a_ln_b, ws, bs_t, side=_gather_side([shard_bf["a_w_out"]]))
    a_w_out = a_w_out.reshape(A_WIDTH, D_MODEL)
    (h1, n_kv, n_b), (w_kv, b_w_in) = _mm_residual_norms(
        y, a_w_out, x, [kv_norm_g, p["b_norm_g"]], name="mm_a_out",
        side=_gather_side([shard_bf["w_kv"], shard_bf["b_w_in"]]))
    w_kv = w_kv.reshape(D_MODEL, 2 * KV_WIDTH)
    kv = _mm_nn(n_kv, w_kv, name="mm_kv", tn=2 * KV_WIDTH, tm=2048)
    kr, vv = _kv_rope(kv, b_kv, cos, sin)
    zb = _mm_nn(n_b, b_w_in, name="mm_b_in", tn=512, tm=1024, out_dtype=BF16)
    yb, (b_w_out,) = _attn_fwd(zb, kr, vv, cos, sin, p["b_bq"], p["b_sinks"], side=_gather_side([shard_bf["b_w_out"]]))
    b_w_out = b_w_out.reshape(B_WIDTH, D_MODEL)
    loss_blk, dh2, dh2b, d_final_g = _mm_residual_loss(yb, b_w_out, h1, tgt, final_norm_g, name="mm_b_out")

    d_b_w_out = _mm_tn(yb, dh2b, name="mm_d_b_w_out", tm=B_WIDTH, tn=D_MODEL)
    red_bo = _Reduction(["b_w_out"], [d_b_w_out.reshape(N_CHIPS, B_WIDTH // N_CHIPS, D_MODEL)], core, place)
    dyb, got = _mm_nt(dh2b, b_w_out, name="mm_dyb", out_dtype=BF16, side=red_bo.exchange_side())
    red_bo.took_exchange(got)
    (dzb, dk_rot, dv, d_bq, d_sinks), got = _attn_bwd(zb, dyb, kr, vv, cos, sin, p["b_bq"], p["b_sinks"],
                                                      side=red_bo.scatter_side())
    red_bo.took_scatter(got)
    dkv, d_b_kv = _kv_rope_bwd(dk_rot, dv, cos, sin)
    d_b_w_in = _mm_tn(n_b, dzb, name="mm_d_b_w_in", tm=D_MODEL, tn=512, shards=N_CHIPS)
    d_w_kv, got = _mm_tn(n_kv, dkv, name="mm_d_w_kv", tm=D_MODEL, tn=2 * KV_WIDTH, side=red_bo.share_side())
    red_bo.took_share(got)
    red_bi = _Reduction(["b_w_in", "w_kv"], [d_b_w_in, d_w_kv.reshape(N_CHIPS, D_MODEL // N_CHIPS, 2 * KV_WIDTH)],
                        core, place)
    (dh1, dh1b, d_kv_g, d_b_g), got = _mm_nt_rms_bwd(
        [(dkv, w_kv, kv_norm_g), (dzb, b_w_in, p["b_norm_g"])], h1, dh2, name="mm_dn_b", tm=512,
        side=red_bi.exchange_side())
    red_bi.took_exchange(got)

    d_a_w_out = _mm_tn(y, dh1b, name="mm_d_a_w_out", tm=1024, tn=D_MODEL)
    red_ao = _Reduction(["a_w_out"], [d_a_w_out.reshape(N_CHIPS, A_WIDTH // N_CHIPS, D_MODEL)], core, place)
    dy, got = _mm_nt(dh1b, a_w_out, name="mm_dy", tn=1024, out_dtype=BF16, side=red_ao.exchange_side())
    red_ao.took_exchange(got)
    sides = [red_bi.scatter_side(), red_ao.scatter_side()]
    (dz, d_ln_g, d_ln_b, d_ws, d_bs_t), got = _gate_bwd(z, dy, a_ln_g, a_ln_b, ws, ws_t, bs_t, side=_join(sides))
    got = _split(got, sides)
    red_bi.took_scatter(got[0])
    red_ao.took_scatter(got[1])
    small = {
        "a_ws": d_ws, "a_bs": d_bs_t.T, "a_ln_g": d_ln_g, "a_ln_b": d_ln_b,
        "kv_norm_g": d_kv_g, "b_kv": d_b_kv, "b_norm_g": d_b_g, "b_bq": d_bq,
        "b_sinks": d_sinks[0:1, :N_Q_HEADS], "final_norm_g": d_final_g,
    }
    packed = [n for n in SMALL_ORDER if n != "a_norm_g"]
    small_shapes = [small[n].shape for n in packed] + [(1, 1)]
    small_pack = _pack_rows([small[n] for n in packed] + [loss_blk[0:1, 0:1]], 64)
    seg = small_pack.shape[0] // 8
    small_pack = small_pack.reshape(8, seg, 128)
    sides = [red_bi.share_side(), red_ao.share_side(), _small_scatter_side(small_pack)]
    (d_a_w_in, from_sibling), got = _mm_tn_exchanging(n_a, dz, name="mm_d_a_w_in", shards=N_CHIPS, side=_join(sides))
    got = _split(got, sides)
    red_bi.took_share(got[0])
    red_ao.took_share(got[1])
    small_mine = _sum_small(small_pack, got[2][0], place)

    chip_sum, (small_all,) = _add_halves(d_a_w_in, from_sibling, name="add_sibling_a_w_in",
                                         side=_small_share_side(small_mine))
    (dx, _, d_a_g), (arrived,) = _mm_nt_rms_bwd([(dz, a_w_in, a_norm_g)], x, dh1, name="mm_dn_a", tm=256,
                                                side=_scatter_side([chip_sum]))
    half_ai = _sum_halves(d_a_w_in, from_sibling, arrived, place, name="sum_chips_a_w_in")
    d_a_g = _rows128(d_a_g)
    sides = [_share_side([half_ai]), _spread_side(d_a_g)]
    got = _split(_comm_call(_join(sides), "share_last"), sides)
    grad_ai = got[0][0]
    small_full = dict(zip(packed + ["loss"], _unpack_rows(small_all.reshape(8 * seg, 128), small_shapes)))
    small_full["a_norm_g"] = _sum_in_device_order(d_a_g, got[1][0], place).reshape(1, -1)
    loss = small_full["loss"].reshape(())

    grad_big = {**red_bo.grads, **red_bi.grads, **red_ao.grads, "a_w_in": grad_ai}
    grads = {}
    for n in SMALL_ORDER:
        gfull = small_full[n]
        if n in SHARDED_SMALL:
            width = p[n].shape[-1]
            gfull = lax.dynamic_slice_in_dim(gfull, chip * width, width, axis=-1)
        grads[n] = gfull.reshape(p[n].shape)

    delta, new_m, new_v = {}, {}, {}
    for n in BIG:
        d, nm, nv, g = _adamw(shard2d[n], grad_big[n], m[n].reshape(shard2d[n].shape),
                              v[n].reshape(shard2d[n].shape), name="adamw_" + n)
        delta[n], new_m[n], new_v[n] = d.reshape(p[n].shape), nm.reshape(p[n].shape), nv.reshape(p[n].shape)
        grads[n] = g.reshape(p[n].shape)
    shapes = [p[n].shape for n in SMALL_ORDER]
    packs = [_pack_rows([src[n] for n in SMALL_ORDER], 8) for src in (p, grads, m, v)]
    outs = _adamw(*packs, name="adamw_small")[:3]
    for res, packed in zip((delta, new_m, new_v), outs):
        for n, val in zip(SMALL_ORDER, _unpack_rows(packed, shapes)):
            res[n] = val

    return (loss, dx[None], *[grads[n] for n in WEIGHTS], *[delta[n] for n in WEIGHTS],
            *[new_m[n] for n in WEIGHTS], *[new_v[n] for n in WEIGHTS])


def kernel(x, a_norm_g, a_w_in, a_ln_g, a_ln_b, a_ws, a_bs, a_w_out, kv_norm_g, w_kv, b_kv, b_norm_g, b_w_in, b_bq, b_sinks, b_w_out, final_norm_g, loss_target, m_a_norm_g, m_a_w_in, m_a_ln_g, m_a_ln_b, m_a_ws, m_a_bs, m_a_w_out, m_kv_norm_g, m_w_kv, m_b_kv, m_b_norm_g, m_b_w_in, m_b_bq, m_b_sinks, m_b_w_out, m_final_norm_g, v_a_norm_g, v_a_w_in, v_a_ln_g, v_a_ln_b, v_a_ws, v_a_bs, v_a_w_out, v_kv_norm_g, v_w_kv, v_b_kv, v_b_norm_g, v_b_w_in, v_b_bq, v_b_sinks, v_b_w_out, v_final_norm_g):
    p = dict(a_norm_g=a_norm_g, a_w_in=a_w_in, a_ln_g=a_ln_g, a_ln_b=a_ln_b, a_ws=a_ws, a_bs=a_bs, a_w_out=a_w_out,
             kv_norm_g=kv_norm_g, w_kv=w_kv, b_kv=b_kv, b_norm_g=b_norm_g, b_w_in=b_w_in, b_bq=b_bq, b_sinks=b_sinks,
             b_w_out=b_w_out, final_norm_g=final_norm_g)
    m = dict(a_norm_g=m_a_norm_g, a_w_in=m_a_w_in, a_ln_g=m_a_ln_g, a_ln_b=m_a_ln_b, a_ws=m_a_ws, a_bs=m_a_bs,
             a_w_out=m_a_w_out, kv_norm_g=m_kv_norm_g, w_kv=m_w_kv, b_kv=m_b_kv, b_norm_g=m_b_norm_g, b_w_in=m_b_w_in,
             b_bq=m_b_bq, b_sinks=m_b_sinks, b_w_out=m_b_w_out, final_norm_g=m_final_norm_g)
    v = dict(a_norm_g=v_a_norm_g, a_w_in=v_a_w_in, a_ln_g=v_a_ln_g, a_ln_b=v_a_ln_b, a_ws=v_a_ws, a_bs=v_a_bs,
             a_w_out=v_a_w_out, kv_norm_g=v_kv_norm_g, w_kv=v_w_kv, b_kv=v_b_kv, b_norm_g=v_b_norm_g, b_w_in=v_b_w_in,
             b_bq=v_b_bq, b_sinks=v_b_sinks, b_w_out=v_b_w_out, final_norm_g=v_final_norm_g)
    return _step(x, loss_target, p, m, v)
```

```python
import functools
import math

import jax
import jax.numpy as jnp
from jax import lax
from jax.experimental import pallas as pl
from jax.experimental.pallas import tpu as pltpu

F32 = jnp.float32
BF16 = jnp.bfloat16

D_MODEL = 1024
CHUNK = 128
A_WIDTH = 2048
A_GROUPS = 16
HEAD_DIM = 64
N_Q_HEADS = 16
N_KV_HEADS = 2
Q_PER_KV = 8
B_WIDTH = 1024
KV_WIDTH = 128
ROPE_THETA = 10000.0
EPS = 1e-5
N_CHIPS = 4

ADAM_LR = 0.001
ADAM_B1 = 0.9
ADAM_B2 = 0.999
ADAM_EPS = 1e-08
ADAM_WD = 0.01
ADAM_STEP = 10

VMEM_LIMIT = 48 * 1024 * 1024
MESH = pl.DeviceIdType.MESH
NEG_BIG = -1e30
HBM = pl.BlockSpec(memory_space=pl.ANY)

NN = (((1,), (0,)), ((), ()))
NT = (((1,), (1,)), ((), ()))
TN = (((0,), (0,)), ((), ()))


def _cparams(**kw):
    return pltpu.CompilerParams(vmem_limit_bytes=VMEM_LIMIT, **kw)


class _Side:
    def __init__(self, ins, out_shapes, sems, start, finish, aliases=None, passing=None):
        self.ins, self.out_shapes, self.sems = list(ins), list(out_shapes), list(sems)
        self.start, self.finish = start, finish
        self.passing = passing or (lambda ins, outs, sems: None)
        self.aliases = dict(aliases or {})


def _join(sides):
    sides = [s for s in sides if s is not None]
    if not sides:
        return None
    offs, i, o, m = [], 0, 0, 0
    for s in sides:
        offs.append((i, o, m))
        i, o, m = i + len(s.ins), o + len(s.out_shapes), m + len(s.sems)

    def run(which):
        def go(ins, outs, sems):
            for s, (a, b, c) in zip(sides, offs):
                getattr(s, which)(ins[a:a + len(s.ins)], outs[b:b + len(s.out_shapes)], sems[c:c + len(s.sems)])
        return go

    aliases = {}
    for s, (a, b, _) in zip(sides, offs):
        aliases.update({a + k: b + v for k, v in s.aliases.items()})
    return _Side([x for s in sides for x in s.ins], [x for s in sides for x in s.out_shapes],
                 [x for s in sides for x in s.sems], run("start"), run("finish"), aliases, run("passing"))


def _split(side_outs, sides):
    out, pos = [], 0
    for s in sides:
        out.append(list(side_outs[pos:pos + len(s.out_shapes)]))
        pos += len(s.out_shapes)
    return out


def _call(body, *, grid, in_specs, out_specs, out_shape, args, name, scratch=(), side=None):
    in_specs, out_specs, out_shape, scratch = list(in_specs), list(out_specs), list(out_shape), list(scratch)
    if side is None:
        res = pl.pallas_call(body, grid=grid, in_specs=in_specs, out_specs=out_specs, out_shape=out_shape,
                             scratch_shapes=scratch, name=name, compiler_params=_cparams())(*args)
        return list(res), []
    n_in, n_out, n_sc = len(in_specs), len(out_specs), len(scratch)
    s_in, s_out = len(side.ins), len(side.out_shapes)

    def wrapped(*refs):
        ins, refs = refs[:n_in], refs[n_in:]
        side_ins, refs = refs[:s_in], refs[s_in:]
        outs, refs = refs[:n_out], refs[n_out:]
        side_outs, refs = refs[:s_out], refs[s_out:]
        scr, side_sems = refs[:n_sc], refs[n_sc:]
        step = 0
        for a, g in enumerate(grid):
            step = step * g + pl.program_id(a)
        steps = math.prod(grid)

        @pl.when(step == 0)
        def _():
            side.start(side_ins, side_outs, side_sems)

        body(*ins, *outs, *scr)

        @pl.when(step == (3 * (steps - 1)) // 4)
        def _():
            side.passing(side_ins, side_outs, side_sems)

        @pl.when(step == steps - 1)
        def _():
            side.finish(side_ins, side_outs, side_sems)

    res = pl.pallas_call(
        wrapped, grid=grid, in_specs=in_specs + [HBM] * s_in, out_specs=out_specs + [HBM] * s_out,
        out_shape=out_shape + side.out_shapes, scratch_shapes=scratch + side.sems,
        input_output_aliases={n_in + k: n_out + v for k, v in side.aliases.items()},
        name=name, compiler_params=_cparams(),
    )(*args, *side.ins)
    return list(res[:n_out]), list(res[n_out:])


def _comm_call(side, name):
    s_in, s_out = len(side.ins), len(side.out_shapes)

    def body(*refs):
        ins, outs, sems = refs[:s_in], refs[s_in:s_in + s_out], refs[s_in + s_out:]
        side.start(ins, outs, sems)
        side.passing(ins, outs, sems)
        side.finish(ins, outs, sems)

    return list(pl.pallas_call(
        body, in_specs=[HBM] * s_in, out_specs=[HBM] * s_out, out_shape=side.out_shapes, scratch_shapes=side.sems,
        input_output_aliases=side.aliases, name=name,
    )(*side.ins))


def _matmul(a, b, *, dims, grid, a_spec, b_spec, o_spec, out_shape, name, acc_axis=None,
            residual=None, r_spec=None, side=None):
    has_res = residual is not None

    def body(*refs):
        if has_res:
            a_ref, b_ref, r_ref, o_ref = refs
        else:
            a_ref, b_ref, o_ref = refs
        part = lax.dot_general(a_ref[...], b_ref[...], dims, preferred_element_type=F32)
        if acc_axis is None:
            if has_res:
                part = part + r_ref[...]
            o_ref[...] = part.astype(o_ref.dtype)
        else:
            k = pl.program_id(acc_axis)

            @pl.when(k == 0)
            def _():
                o_ref[...] = part

            @pl.when(k > 0)
            def _():
                o_ref[...] += part

    in_specs = [a_spec, b_spec] + ([r_spec] if has_res else [])
    args = (a, b) + ((residual,) if has_res else ())
    (out,), side_outs = _call(body, grid=grid, in_specs=in_specs, out_specs=[o_spec], out_shape=[out_shape],
                              args=args, name=name, side=side)
    return (out, side_outs) if side is not None else out


def _row_tile(s, want):
    return min(s, want)


def _mm_nn(a, b, *, name, tn, out_dtype=F32, residual=None, tm=512, side=None):
    s, k = a.shape
    tm = _row_tile(s, tm)
    if b.ndim == 3:
        nsh, _, nc = b.shape
        npb = nc // tn
        n = nsh * nc
        b_spec = pl.BlockSpec((None, k, tn), lambda i, j: (j // npb, 0, j % npb))
    else:
        n = b.shape[1]
        b_spec = pl.BlockSpec((k, tn), lambda i, j: (0, j))
    return _matmul(
        a, b, dims=NN, grid=(s // tm, n // tn),
        a_spec=pl.BlockSpec((tm, k), lambda i, j: (i, 0)), b_spec=b_spec,
        o_spec=pl.BlockSpec((tm, tn), lambda i, j: (i, j)),
        out_shape=jax.ShapeDtypeStruct((s, n), out_dtype), name=name, side=side,
        residual=residual, r_spec=pl.BlockSpec((tm, tn), lambda i, j: (i, j)) if residual is not None else None)


def _mm_nt(a, b, *, name, tn=None, tm=512, out_dtype=F32, side=None):
    s, k = a.shape
    tm = _row_tile(s, tm)
    n = b.shape[0]
    tn = n if tn is None else tn
    return _matmul(
        a, b, dims=NT, grid=(s // tm, n // tn),
        a_spec=pl.BlockSpec((tm, k), lambda i, j: (i, 0)),
        b_spec=pl.BlockSpec((tn, k), lambda i, j: (j, 0)),
        o_spec=pl.BlockSpec((tm, tn), lambda i, j: (i, j)),
        out_shape=jax.ShapeDtypeStruct((s, n), out_dtype), name=name, side=side)


def _mm_tn(a, b, *, name, tm, tn, tk=2048, shards=None, side=None):
    s, m = a.shape
    n = b.shape[1]
    tk = _row_tile(s, tk)
    if shards is None:
        o_spec = pl.BlockSpec((tm, tn), lambda i, j, kk: (i, j))
        out_shape = jax.ShapeDtypeStruct((m, n), F32)
    else:
        assert tm == m
        nc = n // shards
        npb = nc // tn
        o_spec = pl.BlockSpec((None, m, tn), lambda i, j, kk: (j // npb, 0, j % npb))
        out_shape = jax.ShapeDtypeStruct((shards, m, nc), F32)
    return _matmul(
        a, b, dims=TN, grid=(m // tm, n // tn, s // tk), acc_axis=2,
        a_spec=pl.BlockSpec((tk, tm), lambda i, j, kk: (kk, i)),
        b_spec=pl.BlockSpec((tk, tn), lambda i, j, kk: (kk, j)),
        o_spec=o_spec, out_shape=out_shape, name=name, side=side)


def _rstd(x):
    return lax.rsqrt(jnp.mean(x * x, axis=-1, keepdims=True) + EPS)


def _rms_fwd(x, gains, *, name, tr=1024):
    s, d = x.shape
    tr = _row_tile(s, tr)
    ng = len(gains)

    def body(*refs):
        xv = refs[0][...]
        xh = xv * _rstd(xv)
        for t in range(ng):
            refs[1 + ng + t][...] = (xh * refs[1 + t][...]).astype(BF16)

    row = pl.BlockSpec((tr, d), lambda i: (i, 0))
    vec = pl.BlockSpec((1, d), lambda i: (0, 0))
    outs, _ = _call(body, grid=(s // tr,), in_specs=[row] + [vec] * ng, out_specs=[row] * ng,
                    out_shape=[jax.ShapeDtypeStruct((s, d), BF16)] * ng, args=(x, *gains), name=name)
    return outs


def _accumulate(i, ref, value):
    @pl.when(i == 0)
    def _():
        ref[...] = value

    @pl.when(i > 0)
    def _():
        ref[...] += value


def _mm_residual_norms(y, w, res, gains, *, name, tm=512, side=None):
    s, k = y.shape
    d = w.shape[1]
    tm = _row_tile(s, tm)
    ng = len(gains)

    def body(y_ref, w_ref, r_ref, *rest):
        g_refs, h_ref, n_refs = rest[:ng], rest[ng], rest[ng + 1:]
        h = r_ref[...] + jnp.dot(y_ref[...], w_ref[...], preferred_element_type=F32)
        h_ref[...] = h
        xh = h * _rstd(h)
        for t in range(ng):
            n_refs[t][...] = (xh * g_refs[t][...]).astype(BF16)

    row = pl.BlockSpec((tm, d), lambda i: (i, 0))
    vec = pl.BlockSpec((1, d), lambda i: (0, 0))
    return _call(
        body, grid=(s // tm,),
        in_specs=[pl.BlockSpec((tm, k), lambda i: (i, 0)), pl.BlockSpec((k, d), lambda i: (0, 0)), row] + [vec] * ng,
        out_specs=[row] * (1 + ng),
        out_shape=[jax.ShapeDtypeStruct((s, d), F32)] + [jax.ShapeDtypeStruct((s, d), BF16)] * ng,
        args=(y, w, res, *gains), name=name, side=side)


def _mm_residual_loss(y, w, res, tgt, gain, *, name, tm=512):
    s, k = y.shape
    d = w.shape[1]
    tm = _row_tile(s, tm)

    def body(y_ref, w_ref, r_ref, t_ref, g_ref, loss_ref, dh_ref, dhb_ref, dg_ref):
        i = pl.program_id(0)
        hv = r_ref[...] + jnp.dot(y_ref[...], w_ref[...], preferred_element_type=F32)
        g = g_ref[...]
        r = _rstd(hv)
        xh = hv * r
        diff = xh * g - t_ref[...]
        part = 0.5 / d * jnp.sum(jnp.sum(diff * diff, axis=-1, keepdims=True), axis=0, keepdims=True)
        dout = diff * (1.0 / d)
        a = dout * g
        dh = r * (a - xh * jnp.mean(a * xh, axis=-1, keepdims=True))
        dh_ref[...] = dh
        dhb_ref[...] = dh.astype(BF16)
        _accumulate(i, dg_ref, jnp.sum(dout * xh, axis=0, keepdims=True))
        _accumulate(i, loss_ref, jnp.broadcast_to(part, (8, 128)))

    row = pl.BlockSpec((tm, d), lambda i: (i, 0))
    vec = pl.BlockSpec((1, d), lambda i: (0, 0))
    outs, _ = _call(
        body, grid=(s // tm,),
        in_specs=[pl.BlockSpec((tm, k), lambda i: (i, 0)), pl.BlockSpec((k, d), lambda i: (0, 0)), row, row, vec],
        out_specs=[pl.BlockSpec((8, 128), lambda i: (0, 0)), row, row, vec],
        out_shape=[jax.ShapeDtypeStruct((8, 128), F32), jax.ShapeDtypeStruct((s, d), F32),
                   jax.ShapeDtypeStruct((s, d), BF16), jax.ShapeDtypeStruct((1, d), F32)],
        args=(y, w, res, tgt, gain), name=name)
    return outs


def _mm_nt_rms_bwd(terms, x, dres, *, name, tm, side=None):
    s, d = x.shape
    tm = _row_tile(s, tm)
    nt = len(terms)

    def body(*refs):
        a_refs, b_refs, g_refs = refs[0:3 * nt:3], refs[1:3 * nt:3], refs[2:3 * nt:3]
        x_ref, dres_ref = refs[3 * nt], refs[3 * nt + 1]
        dx_ref, dxb_ref = refs[3 * nt + 2], refs[3 * nt + 3]
        dg_refs = refs[3 * nt + 4:]
        i = pl.program_id(0)
        xv = x_ref[...]
        r = _rstd(xv)
        xh = xv * r
        acc = jnp.zeros_like(xv)
        for t in range(nt):
            b_ref = b_refs[t]
            if len(b_ref.shape) == 3:
                kc = b_ref.shape[2]
                dn = None
                for sh in range(b_ref.shape[0]):
                    part = lax.dot_general(a_refs[t][:, sh * kc:(sh + 1) * kc], b_ref[sh], NT, preferred_element_type=F32)
                    dn = part if dn is None else dn + part
            else:
                dn = lax.dot_general(a_refs[t][...], b_ref[...], NT, preferred_element_type=F32)
            acc = acc + dn * g_refs[t][...]
            _accumulate(i, dg_refs[t], jnp.sum(dn * xh, axis=0, keepdims=True))
        dx = dres_ref[...] + r * (acc - xh * jnp.mean(acc * xh, axis=-1, keepdims=True))
        dx_ref[...] = dx
        dxb_ref[...] = dx.astype(BF16)

    row = pl.BlockSpec((tm, d), lambda i: (i, 0))
    vec = pl.BlockSpec((1, d), lambda i: (0, 0))
    in_specs, args = [], []
    for a, b, g in terms:
        in_specs += [pl.BlockSpec((tm, a.shape[1]), lambda i: (i, 0)),
                     pl.BlockSpec(b.shape, (lambda i: (0, 0, 0)) if b.ndim == 3 else (lambda i: (0, 0))), vec]
        args += [a, b, g]
    return _call(
        body, grid=(s // tm,), in_specs=in_specs + [row, row], out_specs=[row, row] + [vec] * nt,
        out_shape=[jax.ShapeDtypeStruct((s, d), F32), jax.ShapeDtypeStruct((s, d), BF16)]
        + [jax.ShapeDtypeStruct((1, d), F32)] * nt,
        args=(*args, x, dres), name=name, side=side)


def _causal_mask(transposed=False):
    row = lax.broadcasted_iota(jnp.int32, (CHUNK, CHUNK), 0)
    col = lax.broadcasted_iota(jnp.int32, (CHUNK, CHUNK), 1)
    return col >= row if transposed else col <= row


def _silu_parts(g):
    sg = jax.nn.sigmoid(g)
    return g * sg, sg * (1.0 + g * (1.0 - sg))


def _gate_fwd(z, ln_g, ln_b, ws, bs_t, *, tr=512, side=None):
    s = z.shape[0]
    tr = _row_tile(s, tr)
    w = A_WIDTH

    def body(u_ref, v_ref, g_ref, lg_ref, lb_ref, ws_ref, bst_ref, y_ref):
        v = v_ref[...].astype(F32)
        mu = jnp.mean(v, axis=-1, keepdims=True)
        xc = v - mu
        rs = lax.rsqrt(jnp.mean(xc * xc, axis=-1, keepdims=True) + EPS)
        vln = (xc * rs * lg_ref[...] + lb_ref[...]).astype(BF16)
        mask = _causal_mask()
        for grp in range(A_GROUPS):
            cols = slice(grp * CHUNK, (grp + 1) * CHUNK)
            wsm = jnp.where(mask, ws_ref[grp], 0.0).astype(BF16)
            bcol = bst_ref[:, grp:grp + 1]
            for ci in range(tr // CHUNK):
                rows = slice(ci * CHUNK, (ci + 1) * CHUNK)
                sv = jnp.dot(wsm, vln[rows, cols], preferred_element_type=F32) + bcol
                gv = g_ref[rows, cols].astype(F32)
                y_ref[rows, cols] = (u_ref[rows, cols].astype(F32) * sv * (gv * jax.nn.sigmoid(gv))).astype(BF16)

    vec = pl.BlockSpec((1, w), lambda i: (0, 0))
    (y,), side_outs = _call(
        body, grid=(s // tr,),
        in_specs=[pl.BlockSpec((tr, w), lambda i: (i, 0)), pl.BlockSpec((tr, w), lambda i: (i, 1)),
                  pl.BlockSpec((tr, w), lambda i: (i, 2)), vec, vec,
                  pl.BlockSpec((A_GROUPS, CHUNK, CHUNK), lambda i: (0, 0, 0)),
                  pl.BlockSpec((CHUNK, A_GROUPS), lambda i: (0, 0))],
        out_specs=[pl.BlockSpec((tr, w), lambda i: (i, 0))],
        out_shape=[jax.ShapeDtypeStruct((s, w), BF16)], args=(z, z, z, ln_g, ln_b, ws, bs_t), name="gate_fwd",
        side=side)
    return y, side_outs


def _gate_bwd(z, dy, ln_g, ln_b, ws, ws_t, bs_t, *, tr=256, side=None):
    s = z.shape[0]
    tr = _row_tile(s, tr)
    w = A_WIDTH
    nsteps = s // tr

    def body(u_ref, v_ref, g_ref, dy_ref, lg_ref, lb_ref, ws_ref, wst_ref, bst_ref,
             dz_ref, dlg_ref, dlb_ref, dws_ref, dbst_ref, dvln_sc, dsv_sc):
        i = pl.program_id(0)

        @pl.when(i == 0)
        def _():
            dws_ref[...] = jnp.zeros_like(dws_ref)
            dsv_sc[...] = jnp.zeros_like(dsv_sc)

        v = v_ref[...].astype(F32)
        mu = jnp.mean(v, axis=-1, keepdims=True)
        xc = v - mu
        rs = lax.rsqrt(jnp.mean(xc * xc, axis=-1, keepdims=True) + EPS)
        xh = xc * rs
        lg = lg_ref[...]
        vln = (xh * lg + lb_ref[...]).astype(BF16)
        mask = _causal_mask()
        mask_t = _causal_mask(transposed=True)
        for grp in range(A_GROUPS):
            cols = slice(grp * CHUNK, (grp + 1) * CHUNK)
            wsm = jnp.where(mask, ws_ref[grp], 0.0).astype(BF16)
            wsm_t = jnp.where(mask_t, wst_ref[grp], 0.0).astype(BF16)
            bcol = bst_ref[:, grp:grp + 1]
            for ci in range(tr // CHUNK):
                rows = slice(ci * CHUNK, (ci + 1) * CHUNK)
                vb = vln[rows, cols]
                sv = jnp.dot(wsm, vb, preferred_element_type=F32) + bcol
                uv = u_ref[rows, cols].astype(F32)
                silu, dsilu = _silu_parts(g_ref[rows, cols].astype(F32))
                dyv = dy_ref[rows, cols].astype(F32)
                dyu = dyv * uv
                dz_ref[rows, cols] = (dyv * sv * silu).astype(BF16)
                dz_ref[rows, 2 * w + grp * CHUNK:2 * w + (grp + 1) * CHUNK] = (dyu * sv * dsilu).astype(BF16)
                dsv = dyu * silu
                dsvb = dsv.astype(BF16)
                dvln_sc[rows, cols] = jnp.dot(wsm_t, dsvb, preferred_element_type=F32)
                dws_ref[grp] += lax.dot_general(dsvb, vb, NT, preferred_element_type=F32)
                dsv_sc[grp] += dsv
        dvln = dvln_sc[...]
        dlg_t = jnp.sum(dvln * xh, axis=0, keepdims=True)
        dlb_t = jnp.sum(dvln, axis=0, keepdims=True)
        a = dvln * lg
        dv = rs * (a - jnp.mean(a, axis=-1, keepdims=True) - xh * jnp.mean(a * xh, axis=-1, keepdims=True))
        dz_ref[:, w:2 * w] = dv.astype(BF16)

        @pl.when(i == 0)
        def _():
            dlg_ref[...] = dlg_t
            dlb_ref[...] = dlb_t

        @pl.when(i > 0)
        def _():
            dlg_ref[...] += dlg_t
            dlb_ref[...] += dlb_t

        @pl.when(i == nsteps - 1)
        def _():
            for grp in range(A_GROUPS):
                dws_ref[grp] = jnp.where(mask, dws_ref[grp], 0.0)
                dbst_ref[:, grp:grp + 1] = jnp.sum(dsv_sc[grp], axis=-1, keepdims=True)

    vec = pl.BlockSpec((1, w), lambda i: (0, 0))
    wsspec = pl.BlockSpec((A_GROUPS, CHUNK, CHUNK), lambda i: (0, 0, 0))
    bsspec = pl.BlockSpec((CHUNK, A_GROUPS), lambda i: (0, 0))
    return _call(
        body, grid=(nsteps,),
        in_specs=[pl.BlockSpec((tr, w), lambda i: (i, 0)), pl.BlockSpec((tr, w), lambda i: (i, 1)),
                  pl.BlockSpec((tr, w), lambda i: (i, 2)), pl.BlockSpec((tr, w), lambda i: (i, 0)),
                  vec, vec, wsspec, wsspec, bsspec],
        out_specs=[pl.BlockSpec((tr, 3 * w), lambda i: (i, 0)), vec, vec, wsspec, bsspec],
        out_shape=[jax.ShapeDtypeStruct((s, 3 * w), BF16), jax.ShapeDtypeStruct((1, w), F32),
                   jax.ShapeDtypeStruct((1, w), F32), jax.ShapeDtypeStruct((A_GROUPS, CHUNK, CHUNK), F32),
                   jax.ShapeDtypeStruct((CHUNK, A_GROUPS), F32)],
        scratch=[pltpu.VMEM((tr, w), F32), pltpu.VMEM((A_GROUPS, CHUNK, CHUNK), F32)],
        args=(z, z, z, dy, ln_g, ln_b, ws, ws_t, bs_t), name="gate_bwd", side=side)


HEADS_PER_BLOCK = 128 // HEAD_DIM
BLOCKS_PER_KV = Q_PER_KV // HEADS_PER_BLOCK
SCALE = HEAD_DIM ** -0.5
LOG2_E = math.log2(math.e)


def _rope_tables(s):
    lane = jnp.arange(128)
    inv_freq = ROPE_THETA ** (-(2 * (lane % (HEAD_DIM // 2))).astype(F32) / HEAD_DIM)
    sign = jnp.where(lane % HEAD_DIM < HEAD_DIM // 2, -1.0, 1.0).astype(F32)
    ang = jnp.arange(s, dtype=F32)[:, None] * inv_freq[None, :]
    return jnp.cos(ang), jnp.sin(ang) * sign[None, :]


def _swap_halves(x):
    n = x.shape[-1]
    lane = lax.broadcasted_iota(jnp.int32, x.shape, x.ndim - 1)
    first = (lane % HEAD_DIM) < (HEAD_DIM // 2)
    return jnp.where(first, pltpu.roll(x, n - HEAD_DIM // 2, x.ndim - 1), pltpu.roll(x, HEAD_DIM // 2, x.ndim - 1))


def _left_half(rows):
    return lax.broadcasted_iota(jnp.int32, (rows, 128), 1) < HEAD_DIM


def _dup_heads(x):
    left = _left_half(x.shape[0])
    swapped = pltpu.roll(x, HEAD_DIM, 1)
    return jnp.concatenate([jnp.where(left, x, swapped), jnp.where(left, swapped, x)], axis=-1)


def _fold_heads(a):
    b0, b1 = a[:, :128], a[:, 128:]
    f0 = b0 + pltpu.roll(b0, HEAD_DIM, 1)
    f1 = b1 + pltpu.roll(b1, HEAD_DIM, 1)
    return jnp.where(_left_half(a.shape[0]), f0, f1)


def _kv_rope(n_kv, w_kv, b_kv, cos, sin, *, tr=2048):
    s, d = n_kv.shape
    tr = _row_tile(s, tr)

    def body(n_ref, w_ref, b_ref, c_ref, s_ref, k_ref, v_ref):
        x = jnp.dot(n_ref[...], w_ref[...], preferred_element_type=F32) + b_ref[...]
        k = x[:, :KV_WIDTH]
        k_ref[...] = _dup_heads(k * c_ref[...] + _swap_halves(k) * s_ref[...]).astype(BF16)
        v_ref[...] = _dup_heads(x[:, KV_WIDTH:]).astype(BF16)

    tab = pl.BlockSpec((tr, KV_WIDTH), lambda i: (i, 0))
    wide = pl.BlockSpec((tr, 2 * KV_WIDTH), lambda i: (i, 0))
    outs, _ = _call(body, grid=(s // tr,),
                    in_specs=[pl.BlockSpec((tr, d), lambda i: (i, 0)), pl.BlockSpec((d, 2 * KV_WIDTH), lambda i: (0, 0)),
                              pl.BlockSpec((1, 2 * KV_WIDTH), lambda i: (0, 0)), tab, tab],
                    out_specs=[wide, wide], out_shape=[jax.ShapeDtypeStruct((s, 2 * KV_WIDTH), BF16)] * 2,
                    args=(n_kv, w_kv, b_kv, cos, sin), name="kv_rope")
    return outs


def _kv_rope_bwd(dk2, dv2, cos, sin, *, tr=2048):
    s = dk2.shape[0]
    tr = _row_tile(s, tr)

    def body(dk_ref, dv_ref, c_ref, s_ref, dkv_ref, db_ref):
        i = pl.program_id(0)
        d = _fold_heads(dk_ref[...])
        dk = d * c_ref[...] + _swap_halves(d * s_ref[...])
        dvv = _fold_heads(dv_ref[...])
        dkv_ref[:, :KV_WIDTH] = dk.astype(BF16)
        dkv_ref[:, KV_WIDTH:] = dvv.astype(BF16)
        sk = jnp.sum(dk, axis=0, keepdims=True)
        sv = jnp.sum(dvv, axis=0, keepdims=True)

        @pl.when(i == 0)
        def _():
            db_ref[:, :KV_WIDTH] = sk
            db_ref[:, KV_WIDTH:] = sv

        @pl.when(i > 0)
        def _():
            db_ref[:, :KV_WIDTH] += sk
            db_ref[:, KV_WIDTH:] += sv

    tab = pl.BlockSpec((tr, KV_WIDTH), lambda i: (i, 0))
    wide = pl.BlockSpec((tr, 2 * KV_WIDTH), lambda i: (i, 0))
    outs, _ = _call(body, grid=(s // tr,), in_specs=[wide, wide, tab, tab],
                    out_specs=[wide, pl.BlockSpec((1, 2 * KV_WIDTH), lambda i: (0, 0))],
                    out_shape=[jax.ShapeDtypeStruct((s, 2 * KV_WIDTH), BF16),
                               jax.ShapeDtypeStruct((1, 2 * KV_WIDTH), F32)],
                    args=(dk2, dv2, cos, sin), name="kv_rope_bwd")
    return outs


def _from_previous():
    cols = Q_PER_KV * CHUNK
    k = lax.broadcasted_iota(jnp.int32, (CHUNK, cols), 0)
    q = lax.broadcasted_iota(jnp.int32, (CHUNK, cols), 1) & (CHUNK - 1)
    return k > q


def _fold(x2, prev):
    return jnp.where(prev, x2[:CHUNK], x2[CHUNK:])


def _unfold(x, prev):
    zero = jnp.zeros_like(x)
    return jnp.concatenate([jnp.where(prev, x, zero), jnp.where(prev, zero, x)], axis=0)


def _stack_heads(blocks, left):
    parts = []
    for b in blocks:
        parts.append(jnp.where(left, b, jnp.zeros_like(b)))
        parts.append(jnp.where(left, jnp.zeros_like(b), b))
    return jnp.concatenate(parts, axis=0)


def _unstack_heads(xt):
    top = lax.broadcasted_iota(jnp.int32, (128, CHUNK), 0) < HEAD_DIM
    return [jnp.where(top, xt[:, (2 * b) * CHUNK:(2 * b + 1) * CHUNK], xt[:, (2 * b + 1) * CHUNK:(2 * b + 2) * CHUNK]).T
            for b in range(BLOCKS_PER_KV)]


def _sink_row(sk_ref, kvh):
    return jnp.concatenate([jnp.full((1, CHUNK), sk_ref[0, kvh * Q_PER_KV + r], F32) for r in range(Q_PER_KV)], axis=1)


def _stacked_probs(qs, kd, prev, sink, i):
    sc2 = lax.dot_general(kd, qs, NT, preferred_element_type=F32)
    no_previous = jnp.where(i > 0, 0.0, NEG_BIG)
    sc = jnp.where(prev, sc2[:CHUNK] + no_previous, sc2[CHUNK:])
    sink = sink * (1.0 / SCALE)
    m = jnp.maximum(jnp.max(sc, axis=0, keepdims=True), sink)
    p = jnp.exp2((sc - m) * (SCALE * LOG2_E))
    esink = jnp.exp2((sink - m) * (SCALE * LOG2_E))
    inv = 1.0 / (jnp.sum(p, axis=0, keepdims=True) + esink)
    return p * inv, esink * inv


def _lane_block(b):
    return slice(b * 128, (b + 1) * 128)


def _rope_blocks(zq_ref, bq_ref, cos, sin, kvh):
    out = []
    for b in range(BLOCKS_PER_KV):
        cols = _lane_block(kvh * BLOCKS_PER_KV + b)
        q = zq_ref[:, cols].astype(F32) + bq_ref[:, cols]
        out.append((q * cos + _swap_halves(q) * sin).astype(BF16))
    return out


def _attn_specs():
    qspec = pl.BlockSpec((CHUNK, B_WIDTH), lambda i: (i, 0))
    gspec = pl.BlockSpec((CHUNK, B_WIDTH), lambda i: (i, 1))
    prev = pl.BlockSpec((CHUNK, 2 * KV_WIDTH), lambda i: (jnp.maximum(i - 1, 0), 0))
    cur = pl.BlockSpec((CHUNK, 2 * KV_WIDTH), lambda i: (i, 0))
    tab = pl.BlockSpec((CHUNK, KV_WIDTH), lambda i: (i, 0))
    bq = pl.BlockSpec((1, B_WIDTH), lambda i: (0, 0))
    sinks = pl.BlockSpec(memory_space=pltpu.SMEM)
    return qspec, gspec, prev, cur, tab, bq, sinks


def _attn_fwd(zb, k2, v2, cos, sin, b_bq, sinks, *, side=None):
    s = zb.shape[0]

    def body(zq_ref, zg_ref, kp_ref, kc_ref, vp_ref, vc_ref, c_ref, s_ref, bq_ref, sk_ref, y_ref):
        i = pl.program_id(0)
        cos, sin = c_ref[...], s_ref[...]
        kcat = jnp.concatenate([kp_ref[...], kc_ref[...]], axis=0)
        vcat = jnp.concatenate([vp_ref[...], vc_ref[...]], axis=0)
        prev = _from_previous()
        left = _left_half(CHUNK)
        for kvh in range(N_KV_HEADS):
            qs = _stack_heads(_rope_blocks(zq_ref, bq_ref, cos, sin, kvh), left)
            p, _ = _stacked_probs(qs, kcat[:, _lane_block(kvh)], prev, _sink_row(sk_ref, kvh), i)
            ot = lax.dot_general(vcat[:, _lane_block(kvh)], _unfold(p, prev).astype(BF16), TN,
                                 preferred_element_type=F32)
            for b, ob in enumerate(_unstack_heads(ot)):
                cols = _lane_block(kvh * BLOCKS_PER_KV + b)
                gv = zg_ref[:, cols].astype(F32)
                y_ref[:, cols] = (ob * (gv * jax.nn.sigmoid(gv))).astype(BF16)

    qspec, gspec, prev, cur, tab, bq, sk = _attn_specs()
    (y,), side_outs = _call(body, grid=(s // CHUNK,), in_specs=[qspec, gspec, prev, cur, prev, cur, tab, tab, bq, sk],
                            out_specs=[qspec], out_shape=[jax.ShapeDtypeStruct((s, B_WIDTH), BF16)],
                            args=(zb, zb, k2, k2, v2, v2, cos, sin, b_bq, sinks), name="attn_fwd", side=side)
    return y, side_outs


def _attn_bwd(zb, dyb, k2, v2, cos, sin, b_bq, sinks, *, side=None):
    s = zb.shape[0]

    def body(zq_ref, zg_ref, dy_ref, kp_ref, kc_ref, vp_ref, vc_ref, c_ref, s_ref, bq_ref, sk_ref,
             dz_ref, dk_ref, dv_ref, dbq_ref, dsk_ref):
        i = pl.program_id(0)

        @pl.when(i == 0)
        def _():
            dk_ref[...] = jnp.zeros_like(dk_ref)
            dv_ref[...] = jnp.zeros_like(dv_ref)
            dbq_ref[...] = jnp.zeros_like(dbq_ref)
            dsk_ref[...] = jnp.zeros_like(dsk_ref)

        cos, sin = c_ref[...], s_ref[...]
        kcat = jnp.concatenate([kp_ref[...], kc_ref[...]], axis=0)
        vcat = jnp.concatenate([vp_ref[...], vc_ref[...]], axis=0)
        prev = _from_previous()
        left = _left_half(CHUNK)
        lane = lax.broadcasted_iota(jnp.int32, (1, 128), 1)
        dsk_row = jnp.zeros((1, 128), F32)
        cur_rows = pl.ds(pl.multiple_of(i * CHUNK, CHUNK), CHUNK)
        for kvh in range(N_KV_HEADS):
            kd, vd = kcat[:, _lane_block(kvh)], vcat[:, _lane_block(kvh)]
            qs = _stack_heads(_rope_blocks(zq_ref, bq_ref, cos, sin, kvh), left)
            p, psink = _stacked_probs(qs, kd, prev, _sink_row(sk_ref, kvh), i)
            pb = _unfold(p, prev).astype(BF16)
            ot = lax.dot_general(vd, pb, TN, preferred_element_type=F32)
            gates, dys = [], []
            for b in range(BLOCKS_PER_KV):
                cols = _lane_block(kvh * BLOCKS_PER_KV + b)
                gates.append(_silu_parts(zg_ref[:, cols].astype(F32)))
                dys.append(dy_ref[:, cols].astype(F32))
            dos = _stack_heads([(dyv * silu).astype(BF16) for dyv, (silu, _) in zip(dys, gates)], left)
            dp = _fold(lax.dot_general(vd, dos, NT, preferred_element_type=F32), prev)
            delta = jnp.sum(p * dp, axis=0, keepdims=True)
            ds = _unfold(p * (dp - delta) * SCALE, prev).astype(BF16)
            dqt = lax.dot_general(kd, ds, TN, preferred_element_type=F32)
            dk_part = jnp.dot(ds, qs, preferred_element_type=F32)
            dv_part = jnp.dot(pb, dos, preferred_element_type=F32)
            dk_ref[cur_rows, _lane_block(kvh)] += dk_part[CHUNK:]
            dv_ref[cur_rows, _lane_block(kvh)] += dv_part[CHUNK:]

            @pl.when(i > 0)
            def _(kvh=kvh, dk_part=dk_part, dv_part=dv_part):
                prev_rows = pl.ds(pl.multiple_of((i - 1) * CHUNK, CHUNK), CHUNK)
                dk_ref[prev_rows, _lane_block(kvh)] += dk_part[:CHUNK]
                dv_ref[prev_rows, _lane_block(kvh)] += dv_part[:CHUNK]

            sink_grad = psink * delta
            for r in range(Q_PER_KV):
                dsink = -jnp.sum(sink_grad[:, r * CHUNK:(r + 1) * CHUNK], axis=1, keepdims=True)
                dsk_row = dsk_row + jnp.where(lane == kvh * Q_PER_KV + r, dsink, 0.0)
            blocks = zip(_unstack_heads(ot), _unstack_heads(dqt), dys, gates)
            for b, (ob, dqr, dyv, (_, dsilu)) in enumerate(blocks):
                blk = kvh * BLOCKS_PER_KV + b
                dq = dqr * cos + _swap_halves(dqr * sin)
                dbq_ref[:, _lane_block(blk)] += jnp.sum(dq, axis=0, keepdims=True)
                dz_ref[:, _lane_block(blk)] = dq.astype(BF16)
                dz_ref[:, _lane_block(B_WIDTH // 128 + blk)] = (dyv * ob * dsilu).astype(BF16)
        dsk_ref[0:1, :] += dsk_row

    qspec, gspec, prev, cur, tab, bq, sk = _attn_specs()
    full = pl.BlockSpec((s, 2 * KV_WIDTH), lambda i: (0, 0))
    return _call(
        body, grid=(s // CHUNK,),
        in_specs=[qspec, gspec, qspec, prev, cur, prev, cur, tab, tab, bq, sk],
        out_specs=[pl.BlockSpec((CHUNK, 2 * B_WIDTH), lambda i: (i, 0)), full, full, bq,
                   pl.BlockSpec((8, 128), lambda i: (0, 0))],
        out_shape=[jax.ShapeDtypeStruct((s, 2 * B_WIDTH), BF16), jax.ShapeDtypeStruct((s, 2 * KV_WIDTH), F32),
                   jax.ShapeDtypeStruct((s, 2 * KV_WIDTH), F32), jax.ShapeDtypeStruct((1, B_WIDTH), F32),
                   jax.ShapeDtypeStruct((8, 128), F32)],
        args=(zb, zb, dyb, k2, k2, v2, v2, cos, sin, b_bq, sinks), name="attn_bwd", side=side)


def _place():
    x, y, c = lax.axis_index("x"), lax.axis_index("y"), lax.axis_index("c")
    return x, y, c, [(1 - x, y), (x, 1 - y), (1 - x, 1 - y)]


def _relations():
    return [(r >> 2 & 1, r >> 1 & 1, r & 1) for r in range(1, 8)]


def _gather_side(arrs):
    n = len(arrs)

    def copies(ins, outs, sems):
        send_ici, recv_ici, send_d2d, recv_d2d, local_sem = sems
        x, y, c, chips = _place()
        me = 2 * x + y

        def rows(a, half):
            hr = arrs[a].shape[0] // 2
            return pl.ds(half * hr, hr)

        def ici(a, j, src_chip, to):
            return pltpu.make_async_remote_copy(
                src_ref=ins[a].at[rows(a, c)], dst_ref=outs[a].at[src_chip, rows(a, c)],
                send_sem=send_ici.at[a, j], recv_sem=recv_ici.at[a, j], device_id=to, device_id_type=MESH)

        def d2d(a, j, chip, half):
            blk = outs[a].at[chip, rows(a, half)]
            return pltpu.make_async_remote_copy(
                src_ref=blk, dst_ref=blk, send_sem=send_d2d.at[a, j], recv_sem=recv_d2d.at[a, j],
                device_id=(x, y, 1 - c), device_id_type=MESH)

        local = [pltpu.make_async_copy(ins[a], outs[a].at[me], local_sem.at[a]) for a in range(n)]
        pairs = [(a, j, chip) for a in range(n) for j, chip in enumerate(chips)]
        return c, me, local, ici, d2d, pairs

    def start(ins, outs, sems):
        c, me, local, ici, _, pairs = copies(ins, outs, sems)
        for cp in local:
            cp.start()
        for a, j, chip in pairs:
            ici(a, j, me, (*chip, c)).start()

    def passing(ins, outs, sems):
        c, _, _, ici, d2d, pairs = copies(ins, outs, sems)
        for a, j, (px, py) in pairs:
            ici(a, j, 2 * px + py, (px, py, c)).wait_recv()
            d2d(a, j, 2 * px + py, c).start()

    def finish(ins, outs, sems):
        c, me, local, ici, d2d, pairs = copies(ins, outs, sems)
        for a, j, (px, py) in pairs:
            d2d(a, j, 2 * px + py, 1 - c).wait_recv()
        for a, j, (px, py) in pairs:
            ici(a, j, me, (px, py, c)).wait_send()
            d2d(a, j, 2 * px + py, c).wait_send()
        for cp in local:
            cp.wait()

    return _Side(arrs, [jax.ShapeDtypeStruct((N_CHIPS,) + a.shape, a.dtype) for a in arrs],
                 [pltpu.SemaphoreType.DMA((n, 3))] * 4 + [pltpu.SemaphoreType.DMA((n,))], start, finish,
                 passing=passing)


def _exchange_side(grads):
    n = len(grads)

    def copies(ins, outs, sems):
        send_sem, recv_sem = sems
        x, y, c, _ = _place()
        cps = []
        for a in range(n):
            hr = grads[a].shape[1] // 2
            cps.append(pltpu.make_async_remote_copy(
                src_ref=ins[a].at[:, pl.ds((1 - c) * hr, hr), :], dst_ref=outs[a],
                send_sem=send_sem.at[a], recv_sem=recv_sem.at[a], device_id=(x, y, 1 - c), device_id_type=MESH))
        return cps

    def start(ins, outs, sems):
        for cp in copies(ins, outs, sems):
            cp.start()

    def finish(ins, outs, sems):
        for cp in copies(ins, outs, sems):
            cp.wait()

    return _Side(grads, [jax.ShapeDtypeStruct((g.shape[0], g.shape[1] // 2, g.shape[2]), g.dtype) for g in grads],
                 [pltpu.SemaphoreType.DMA((n,))] * 2, start, finish)


def _scatter_side(chip_sums, small=None):
    n = len(chip_sums)
    arrs = list(chip_sums) + ([small] if small is not None else [])

    def copies(ins, outs, sems):
        x, y, c, chips = _place()
        cps = []
        for a in range(n):
            for j, (px, py) in enumerate(chips):
                cps.append(pltpu.make_async_remote_copy(
                    src_ref=ins[a].at[2 * px + py], dst_ref=outs[a].at[j],
                    send_sem=sems[0].at[a, j], recv_sem=sems[1].at[a, j], device_id=(px, py, c), device_id_type=MESH))
        if small is not None:
            for r, (fx, fy, fc) in enumerate(_relations(), start=1):
                px, py, pc = x ^ fx, y ^ fy, c ^ fc
                cps.append(pltpu.make_async_remote_copy(
                    src_ref=ins[n].at[4 * px + 2 * py + pc], dst_ref=outs[n].at[r],
                    send_sem=sems[2].at[r - 1], recv_sem=sems[3].at[r - 1], device_id=(px, py, pc),
                    device_id_type=MESH))
        return cps

    def start(ins, outs, sems):
        for cp in copies(ins, outs, sems):
            cp.start()

    def finish(ins, outs, sems):
        for cp in copies(ins, outs, sems):
            cp.wait()

    shapes = [jax.ShapeDtypeStruct((3,) + t.shape[1:], t.dtype) for t in chip_sums]
    sems = [pltpu.SemaphoreType.DMA((n, 3))] * 2
    if small is not None:
        shapes.append(jax.ShapeDtypeStruct(small.shape, small.dtype))
        sems += [pltpu.SemaphoreType.DMA((7,))] * 2
    return _Side(arrs, shapes, sems, start, finish)


def _small_scatter_side(small):
    def copies(ins, outs, sems):
        x, y, c, _ = _place()
        cps = []
        for r, (fx, fy, fc) in enumerate(_relations(), start=1):
            px, py, pc = x ^ fx, y ^ fy, c ^ fc
            cps.append(pltpu.make_async_remote_copy(
                src_ref=ins[0].at[4 * px + 2 * py + pc], dst_ref=outs[0].at[r],
                send_sem=sems[0].at[r - 1], recv_sem=sems[1].at[r - 1], device_id=(px, py, pc), device_id_type=MESH))
        return cps

    def start(ins, outs, sems):
        for cp in copies(ins, outs, sems):
            cp.start()

    def finish(ins, outs, sems):
        for cp in copies(ins, outs, sems):
            cp.wait()

    return _Side([small], [jax.ShapeDtypeStruct(small.shape, small.dtype)], [pltpu.SemaphoreType.DMA((7,))] * 2,
                 start, finish)


def _small_share_side(small):
    return _share_side([], small)


def _share_side(halves, small=None):
    n = len(halves)
    arrs = list(halves) + ([small] if small is not None else [])

    def copies(ins, outs, sems, mine):
        x, y, c, _ = _place()
        me = 4 * x + 2 * y + c
        cps = []
        for a in range(n):
            hr = halves[a].shape[0] // 2
            rows = pl.ds((c if mine else 1 - c) * hr, hr)
            cps.append(pltpu.make_async_remote_copy(
                src_ref=ins[a].at[rows], dst_ref=outs[a].at[rows],
                send_sem=sems[0].at[a], recv_sem=sems[1].at[a], device_id=(x, y, 1 - c), device_id_type=MESH))
        if small is not None:
            for r, (fx, fy, fc) in enumerate(_relations(), start=1):
                px, py, pc = x ^ fx, y ^ fy, c ^ fc
                seg = me if mine else 4 * px + 2 * py + pc
                cps.append(pltpu.make_async_remote_copy(
                    src_ref=ins[n].at[seg], dst_ref=outs[n].at[seg],
                    send_sem=sems[-2].at[r - 1], recv_sem=sems[-1].at[r - 1], device_id=(px, py, pc),
                    device_id_type=MESH))
        return cps

    def start(ins, outs, sems):
        for cp in copies(ins, outs, sems, True):
            cp.start()

    def finish(ins, outs, sems):
        for cp in copies(ins, outs, sems, False):
            cp.wait_recv()
        for cp in copies(ins, outs, sems, True):
            cp.wait_send()

    sems = ([pltpu.SemaphoreType.DMA((n,))] * 2 if n else []) + (
        [pltpu.SemaphoreType.DMA((7,))] * 2 if small is not None else [])
    return _Side(arrs, [jax.ShapeDtypeStruct(h.shape, h.dtype) for h in arrs], sems, start, finish,
                 aliases={i: i for i in range(len(arrs))})


GATHER_PIECES = [(0, 0), (0, 1), (1, 0), (2, 0), (1, 1), (2, 1), (3, 0), (3, 1)]


def _mm_gathering(a, shard, order, *, name, tm=1024):
    s, k = a.shape
    nc = shard.shape[1]
    tm = _row_tile(s, tm)
    tn = nc // 2
    hr = k // 2
    qr = hr // 2
    blocks = jnp.stack([order[src] * 2 + h for src, h in GATHER_PIECES]).astype(jnp.int32)

    def body(blocks_ref, a_ref, shard_ref, z_ref, full_ref, wbuf, send_ici, recv_ici, send_relay,
             recv_relay, send_d2d, recv_d2d, local_sem, load_sem):
        piece, i = pl.program_id(0), pl.program_id(1)
        x, y, c, chips = _place()
        me = 2 * x + y
        nbrs = chips[:2]
        chip_of = [2 * px + py for px, py in chips]

        def quarter(q):
            return pl.ds(c * hr + q * qr, qr)

        def sibling_quarter(q):
            return pl.ds((1 - c) * hr + q * qr, qr)

        def whole(half):
            return pl.ds(half * hr, hr)

        def cols(h):
            return pl.ds(h * tn, tn)

        def direct(j, src_chip, h):
            return pltpu.make_async_remote_copy(
                src_ref=shard_ref.at[whole(c), cols(h)], dst_ref=full_ref.at[src_chip, whole(c), cols(h)],
                send_sem=send_ici.at[j, h], recv_sem=recv_ici.at[j, h], device_id=(*nbrs[j], c), device_id_type=MESH)

        def relay(j, src_chip, h):
            blk = full_ref.at[src_chip, quarter(j), cols(h)]
            return pltpu.make_async_remote_copy(
                src_ref=blk, dst_ref=blk, send_sem=send_relay.at[j, h], recv_sem=recv_relay.at[j, h],
                device_id=(*nbrs[1 - j], c), device_id_type=MESH)

        def d2d(j, chip, rows, h):
            blk = full_ref.at[chip, rows, cols(h)]
            return pltpu.make_async_remote_copy(
                src_ref=blk, dst_ref=blk, send_sem=send_d2d.at[j, h], recv_sem=recv_d2d.at[j, h],
                device_id=(x, y, 1 - c), device_id_type=MESH)

        def load(p):
            src, h = GATHER_PIECES[p]
            where = shard_ref if src == 0 else full_ref.at[chip_of[src - 1]]
            return pltpu.make_async_copy(where.at[:, cols(h)], wbuf.at[p % 2], load_sem.at[p % 2])

        local = pltpu.make_async_copy(shard_ref, full_ref.at[me], local_sem)

        def arrived(p):
            src, h = GATHER_PIECES[p]
            if src in (1, 2):
                j = src - 1
                direct(j, chip_of[j], h).wait_recv()
                relay(j, chip_of[j], h).start()
                d2d(j, chip_of[j], whole(c), h).start()
            elif src == 3:
                for j in range(2):
                    relay(1 - j, chip_of[2], h).wait_recv()
                    d2d(2 + j, chip_of[2], quarter(1 - j), h).start()

        def fetch(p):
            src, h = GATHER_PIECES[p]
            if src in (1, 2):
                d2d(src - 1, chip_of[src - 1], whole(1 - c), h).wait_recv()
            elif src == 3:
                for j in range(2):
                    d2d(2 + j, chip_of[2], sibling_quarter(1 - j), h).wait_recv()
            load(p).start()

        n_i = s // tm
        for p in range(len(GATHER_PIECES)):
            @pl.when(jnp.logical_and(piece == p, i == 0))
            def _(p=p):
                if p == 0:
                    local.start()
                    for hh in range(2):
                        for j in range(2):
                            direct(j, me, hh).start()
                    load(0).start()
                load(p).wait()

        z_ref[...] = jnp.dot(a_ref[...], wbuf[piece % 2], preferred_element_type=F32).astype(z_ref.dtype)

        for p in range(len(GATHER_PIECES) - 1):
            @pl.when(jnp.logical_and(piece == p, i == min(1, n_i - 1)))
            def _(p=p):
                arrived(p + 1)

            @pl.when(jnp.logical_and(piece == p, i == min(2, n_i - 1)))
            def _(p=p):
                fetch(p + 1)

        last = jnp.logical_and(piece == len(GATHER_PIECES) - 1, i == n_i - 1)

        @pl.when(last)
        def _():
            for h in range(2):
                for j in range(2):
                    direct(j, me, h).wait_send()
                    relay(j, chip_of[j], h).wait_send()
                    d2d(j, chip_of[j], whole(c), h).wait_send()
                    d2d(2 + j, chip_of[2], quarter(1 - j), h).wait_send()
            local.wait()

    return pl.pallas_call(
        body,
        grid_spec=pltpu.PrefetchScalarGridSpec(
            num_scalar_prefetch=1, grid=(len(GATHER_PIECES), s // tm),
            in_specs=[pl.BlockSpec((tm, k), lambda p, i, blocks: (i, 0)), HBM],
            out_specs=[pl.BlockSpec((tm, tn), lambda p, i, blocks: (i, blocks[p])), HBM],
            scratch_shapes=[pltpu.VMEM((2, k, tn), BF16)] + [pltpu.SemaphoreType.DMA((2, 2))] * 4
            + [pltpu.SemaphoreType.DMA((4, 2))] * 2 + [pltpu.SemaphoreType.DMA, pltpu.SemaphoreType.DMA((2,))]),
        out_shape=[jax.ShapeDtypeStruct((s, N_CHIPS * nc), BF16), jax.ShapeDtypeStruct((N_CHIPS, k, nc), BF16)],
        name=name, compiler_params=_cparams(),
    )(blocks, a, shard)


def _mm_tn_exchanging(a, b, *, name, shards, tk=2048, side=None):
    s, m = a.shape
    nc = b.shape[1] // shards
    tk = _row_tile(s, tk)
    nk = s // tk
    hm = m // 2

    def body(a_ref, b_ref, part_ref, sib_ref, acc, keep_sem, send_sem, recv_sem):
        j, kk = pl.program_id(0), pl.program_id(1)
        x, y, c, _ = _place()

        def keep(jj, slot):
            mine = pl.ds(c * hm, hm)
            return pltpu.make_async_copy(acc.at[slot, mine], part_ref.at[jj], keep_sem.at[slot])

        def give(jj, slot):
            return pltpu.make_async_remote_copy(
                src_ref=acc.at[slot, pl.ds((1 - c) * hm, hm)], dst_ref=sib_ref.at[jj],
                send_sem=send_sem.at[slot], recv_sem=recv_sem.at[jj], device_id=(x, y, 1 - c), device_id_type=MESH)

        part = lax.dot_general(a_ref[...], b_ref[...], TN, preferred_element_type=F32)
        for slot in range(2):
            @pl.when(j % 2 == slot)
            def _(slot=slot):
                @pl.when(jnp.logical_and(kk == 0, j >= 2))
                def _():
                    keep(j - 2, slot).wait()
                    give(j - 2, slot).wait_send()

                @pl.when(kk == 0)
                def _():
                    acc[slot] = part

                @pl.when(kk > 0)
                def _():
                    acc[slot] += part

                @pl.when(kk == nk - 1)
                def _():
                    keep(j, slot).start()
                    give(j, slot).start()

        @pl.when(jnp.logical_and(j == shards - 1, kk == nk - 1))
        def _():
            for jj in range(shards - 2, shards):
                keep(jj, jj % 2).wait()
                give(jj, jj % 2).wait_send()
            for jj in range(shards):
                give(jj, jj % 2).wait_recv()

    assert shards >= 2
    return _call(
        body, grid=(shards, nk),
        in_specs=[pl.BlockSpec((tk, m), lambda j, kk: (kk, 0)), pl.BlockSpec((tk, nc), lambda j, kk: (kk, j))],
        out_specs=[HBM, HBM],
        out_shape=[jax.ShapeDtypeStruct((shards, hm, nc), F32), jax.ShapeDtypeStruct((shards, hm, nc), F32)],
        scratch=[pltpu.VMEM((2, m, nc), F32), pltpu.SemaphoreType.DMA((2,)), pltpu.SemaphoreType.DMA((2,)),
                 pltpu.SemaphoreType.DMA((shards,))],
        args=(a, b), name=name, side=side)


def _col_tile(cols):
    return cols if cols <= 2048 else 512


def _add_sibling(grad, recv, core, *, name):
    k, r, c = grad.shape
    hr = r // 2
    tr = min(hr, 256)
    tc = _col_tile(c)
    nrb = hr // tr

    def body(core_ref, g_ref, r_ref, o_ref):
        o_ref[...] = (g_ref[...] + r_ref[...]).astype(BF16)

    return pl.pallas_call(
        body,
        grid_spec=pltpu.PrefetchScalarGridSpec(
            num_scalar_prefetch=1, grid=(k, nrb, c // tc),
            in_specs=[pl.BlockSpec((None, tr, tc), lambda kk, i, j, core: (kk, core[0] * nrb + i, j)),
                      pl.BlockSpec((None, tr, tc), lambda kk, i, j, core: (kk, i, j))],
            out_specs=pl.BlockSpec((None, tr, tc), lambda kk, i, j, core: (kk, i, j))),
        out_shape=jax.ShapeDtypeStruct((k, hr, c), BF16), name=name, compiler_params=_cparams(),
    )(core, grad, recv)


def _sum_chips(grad, from_sibling, recv, place, *, name):
    _, hr, c = from_sibling.shape
    tr = min(hr, 256)
    tc = _col_tile(c)
    nrb = hr // tr

    def body(place_ref, g_ref, s_ref, r0_ref, r1_ref, r2_ref, o_ref):
        own = g_ref[...] + s_ref[...]
        o_ref[...] = ((own + r0_ref[...].astype(F32)) + r1_ref[...].astype(F32)) + r2_ref[...].astype(F32)

    def rspec(j):
        return pl.BlockSpec((None, tr, tc), lambda i, jj, place: (j, i, jj))

    return pl.pallas_call(
        body,
        grid_spec=pltpu.PrefetchScalarGridSpec(
            num_scalar_prefetch=1, grid=(nrb, c // tc),
            in_specs=[pl.BlockSpec((None, tr, tc), lambda i, jj, place: (place[0], place[1] * nrb + i, jj)),
                      pl.BlockSpec((None, tr, tc), lambda i, jj, place: (place[0], i, jj)),
                      rspec(0), rspec(1), rspec(2)],
            out_specs=pl.BlockSpec((tr, tc), lambda i, jj, place: (place[1] * nrb + i, jj))),
        out_shape=jax.ShapeDtypeStruct((2 * hr, c), F32), name=name, compiler_params=_cparams(),
    )(place, grad, from_sibling, recv, recv, recv)


def _add_halves(mine, theirs, *, name, side=None):
    k, hr, c = mine.shape
    tr = min(hr, 256)
    tc = _col_tile(c)

    def body(a_ref, b_ref, o_ref):
        o_ref[...] = (a_ref[...] + b_ref[...]).astype(BF16)

    spec = pl.BlockSpec((None, tr, tc), lambda kk, i, j: (kk, i, j))
    (out,), side_outs = _call(body, grid=(k, hr // tr, c // tc), in_specs=[spec, spec], out_specs=[spec],
                              out_shape=[jax.ShapeDtypeStruct((k, hr, c), BF16)], args=(mine, theirs), name=name,
                              side=side)
    return out, side_outs


def _sum_halves(mine, theirs, recv, place, *, name):
    _, hr, c = mine.shape
    tr = min(hr, 256)
    tc = _col_tile(c)
    nrb = hr // tr

    def body(place_ref, a_ref, b_ref, r0_ref, r1_ref, r2_ref, o_ref):
        own = a_ref[...] + b_ref[...]
        o_ref[...] = ((own + r0_ref[...].astype(F32)) + r1_ref[...].astype(F32)) + r2_ref[...].astype(F32)

    def rspec(j):
        return pl.BlockSpec((None, tr, tc), lambda i, jj, place: (j, i, jj))

    own_spec = pl.BlockSpec((None, tr, tc), lambda i, jj, place: (place[0], i, jj))
    return pl.pallas_call(
        body,
        grid_spec=pltpu.PrefetchScalarGridSpec(
            num_scalar_prefetch=1, grid=(nrb, c // tc),
            in_specs=[own_spec, own_spec, rspec(0), rspec(1), rspec(2)],
            out_specs=pl.BlockSpec((tr, tc), lambda i, jj, place: (place[1] * nrb + i, jj))),
        out_shape=jax.ShapeDtypeStruct((2 * hr, c), F32), name=name, compiler_params=_cparams(),
    )(place, mine, theirs, recv, recv, recv)


def _sum_small(small, recv, place):
    _, sr, _ = small.shape

    def body(place_ref, own_ref, r_ref, o_ref):
        acc = own_ref[...]
        for r in range(1, 8):
            acc = acc + r_ref[r]
        o_ref[...] = acc

    return pl.pallas_call(
        body,
        grid_spec=pltpu.PrefetchScalarGridSpec(
            num_scalar_prefetch=1, grid=(1,),
            in_specs=[pl.BlockSpec((None, sr, 128), lambda i, place: (place[2], 0, 0)),
                      pl.BlockSpec((8, sr, 128), lambda i, place: (0, 0, 0))],
            out_specs=pl.BlockSpec((None, sr, 128), lambda i, place: (place[2], 0, 0))),
        out_shape=jax.ShapeDtypeStruct(small.shape, F32), name="sum_small", compiler_params=_cparams(),
    )(place, small, recv)


def _spread_side(vec):
    def copies(ins, outs, sems):
        x, y, c, _ = _place()
        return [pltpu.make_async_remote_copy(
            src_ref=ins[0], dst_ref=outs[0].at[r], send_sem=sems[0].at[r - 1], recv_sem=sems[1].at[r - 1],
            device_id=(x ^ fx, y ^ fy, c ^ fc), device_id_type=MESH)
            for r, (fx, fy, fc) in enumerate(_relations(), start=1)]

    def start(ins, outs, sems):
        for cp in copies(ins, outs, sems):
            cp.start()

    def finish(ins, outs, sems):
        for cp in copies(ins, outs, sems):
            cp.wait()

    return _Side([vec], [jax.ShapeDtypeStruct((8,) + vec.shape, vec.dtype)], [pltpu.SemaphoreType.DMA((7,))] * 2,
                 start, finish)


def _sum_in_device_order(own, spread, place):
    def body(place_ref, own_ref, r_ref, o_ref):
        me = place_ref[2]
        acc = jnp.zeros_like(own_ref[...])
        for d in range(8):
            slot = jnp.where(me == d, 1, me ^ d)
            acc = acc + jnp.where(me == d, own_ref[...], r_ref[slot])
        o_ref[...] = acc

    return pl.pallas_call(
        body,
        grid_spec=pltpu.PrefetchScalarGridSpec(
            num_scalar_prefetch=1, grid=(1,),
            in_specs=[pl.BlockSpec(own.shape, lambda i, place: (0, 0)),
                      pl.BlockSpec(spread.shape, lambda i, place: (0, 0, 0))],
            out_specs=pl.BlockSpec(own.shape, lambda i, place: (0, 0))),
        out_shape=jax.ShapeDtypeStruct(own.shape, F32), name="sum_in_device_order", compiler_params=_cparams(),
    )(place, own, spread)


def _adamw(w, g, m, v, *, name):
    r, c = w.shape
    tr = 256 if r % 256 == 0 else r
    tc = _col_tile(c)
    bc1 = 1.0 - ADAM_B1 ** ADAM_STEP
    bc2 = 1.0 - ADAM_B2 ** ADAM_STEP

    def body(w_ref, g_ref, m_ref, v_ref, d_ref, nm_ref, nv_ref, gout_ref):
        gv = g_ref[...]
        nm = ADAM_B1 * m_ref[...] + (1.0 - ADAM_B1) * gv
        nv = ADAM_B2 * v_ref[...] + (1.0 - ADAM_B2) * (gv * gv)
        d_ref[...] = -ADAM_LR * ((nm / bc1) / (jnp.sqrt(nv / bc2) + ADAM_EPS) + ADAM_WD * w_ref[...])
        nm_ref[...] = nm
        nv_ref[...] = nv
        gout_ref[...] = gv

    spec = pl.BlockSpec((tr, tc), lambda i, j: (i, j))
    outs, _ = _call(body, grid=(r // tr, c // tc), in_specs=[spec] * 4, out_specs=[spec] * 4,
                    out_shape=[jax.ShapeDtypeStruct((r, c), F32)] * 4, args=(w, g, m, v), name=name)
    return outs


SMALL_ORDER = ["a_ws", "a_bs", "a_norm_g", "a_ln_g", "a_ln_b", "kv_norm_g", "b_kv", "b_norm_g", "b_bq",
               "b_sinks", "final_norm_g"]
SHARDED_SMALL = {"a_norm_g", "a_ln_g", "a_ln_b"}
PACK_TILE = 8 * 128


def _rows128(a):
    flat = a.reshape(-1)
    return jnp.pad(flat, (0, (-flat.shape[0]) % PACK_TILE)).reshape(-1, 128)


def _pack_rows(parts, multiple):
    rows = [_rows128(p) for p in parts]
    total = sum(r.shape[0] for r in rows)
    pad = (-total) % multiple
    if pad:
        rows.append(jnp.zeros((pad, 128), rows[0].dtype))
    return jnp.concatenate(rows, axis=0)


def _unpack_rows(packed, shapes):
    out, row = [], 0
    for shp in shapes:
        size = math.prod(shp)
        nrow = -(-size // PACK_TILE) * 8
        out.append(packed[row:row + nrow].reshape(-1)[:size].reshape(shp))
        row += nrow
    return out


WEIGHTS = ["a_norm_g", "a_w_in", "a_ln_g", "a_ln_b", "a_ws", "a_bs", "a_w_out", "kv_norm_g", "w_kv", "b_kv",
           "b_norm_g", "b_w_in", "b_bq", "b_sinks", "b_w_out", "final_norm_g"]
BIG = ["a_w_in", "a_w_out", "w_kv", "b_w_in", "b_w_out"]


class _Reduction:
    def __init__(self, names, partials, core, place, small=None):
        self.names, self.partials, self.core, self.place, self.small = names, partials, core, place, small

    def exchange_side(self):
        return _exchange_side(self.partials)

    def took_exchange(self, from_sibling):
        self.from_sibling = from_sibling
        self.chip_sums = [_add_sibling(g, r, self.core, name="add_sibling_" + n)
                          for g, r, n in zip(self.partials, from_sibling, self.names)]

    def scatter_side(self):
        return _scatter_side(self.chip_sums, self.small)

    def took_scatter(self, arrived):
        big = arrived[:len(self.names)]
        self.halves = [_sum_chips(g, fs, r, self.place, name="sum_chips_" + n)
                       for g, fs, r, n in zip(self.partials, self.from_sibling, big, self.names)]
        self.small_mine = _sum_small(self.small, arrived[-1], self.place) if self.small is not None else None

    def share_side(self):
        return _share_side(self.halves, self.small_mine)

    def took_share(self, shared):
        self.grads = dict(zip(self.names, shared[:len(self.names)]))
        self.small_full = shared[-1] if self.small is not None else None


def _step(x, loss_target, p, m, v):
    xi, yi, ci = lax.axis_index("x"), lax.axis_index("y"), lax.axis_index("c")
    chip = 2 * xi + yi
    device = 4 * xi + 2 * yi + ci
    core = jnp.reshape(ci, (1,)).astype(jnp.int32)
    place = jnp.stack([chip, ci, device]).astype(jnp.int32)
    x, tgt = x[0], loss_target[0]
    s = x.shape[0]
    cos, sin = _rope_tables(s)

    shard2d = {n: p[n].reshape(p[n].shape[-2:]) for n in BIG}
    shard_bf = {n: shard2d[n].astype(BF16) for n in BIG}
    ws = p["a_ws"][0]
    ws_t = jnp.swapaxes(ws, 1, 2)
    bs_t = p["a_bs"][0].T
    kv_norm_g, b_kv = p["kv_norm_g"].reshape(1, -1), p["b_kv"].reshape(1, -1)
    final_norm_g = p["final_norm_g"].reshape(1, -1)

    vec_shapes = [p[n].shape for n in ("a_norm_g", "a_ln_g", "a_ln_b")]
    vec_pack = _pack_rows([p["a_norm_g"], p["a_ln_g"], p["a_ln_b"]], 16)
    (vec_all,) = _comm_call(_gather_side([vec_pack]), "gather_vectors")
    vecs = [_unpack_rows(vec_all[k], vec_shapes) for k in range(N_CHIPS)]
    a_norm_g, a_ln_g, a_ln_b = (jnp.concatenate([vk[t] for vk in vecs], axis=-1) for t in range(3))

    (n_a,) = _rms_fwd(x, [a_norm_g], name="rms_a")
    order = jnp.stack([chip, 2 * (1 - xi) + yi, 2 * xi + (1 - yi), 2 * (1 - xi) + (1 - yi)]).astype(jnp.int32)
    z, a_w_in = _mm_gathering(n_a, shard_bf["a_w_in"], order, name="mm_a_in")
    y, (a_w_out,) = _gate_fwd(z, a_ln_g, a_ln_b, ws, bs_t, side=_gather_side([shard_bf["a_w_out"]]))
    a_w_out = a_w_out.reshape(A_WIDTH, D_MODEL)
    (h1, n_kv, n_b), (w_kv, b_w_in) = _mm_residual_norms(
        y, a_w_out, x, [kv_norm_g, p["b_norm_g"]], name="mm_a_out",
        side=_gather_side([shard_bf["w_kv"], shard_bf["b_w_in"]]))
    w_kv = w_kv.reshape(D_MODEL, 2 * KV_WIDTH)
    kr, vv = _kv_rope(n_kv, w_kv, b_kv, cos, sin)
    zb = _mm_nn(n_b, b_w_in, name="mm_b_in", tn=512, tm=1024, out_dtype=BF16)
    yb, (b_w_out,) = _attn_fwd(zb, kr, vv, cos, sin, p["b_bq"], p["b_sinks"], side=_gather_side([shard_bf["b_w_out"]]))
    b_w_out = b_w_out.reshape(B_WIDTH, D_MODEL)
    loss_blk, dh2, dh2b, d_final_g = _mm_residual_loss(yb, b_w_out, h1, tgt, final_norm_g, name="mm_b_out")

    d_b_w_out = _mm_tn(yb, dh2b, name="mm_d_b_w_out", tm=B_WIDTH, tn=D_MODEL)
    red_bo = _Reduction(["b_w_out"], [d_b_w_out.reshape(N_CHIPS, B_WIDTH // N_CHIPS, D_MODEL)], core, place)
    dyb, got = _mm_nt(dh2b, b_w_out, name="mm_dyb", out_dtype=BF16, side=red_bo.exchange_side())
    red_bo.took_exchange(got)
    (dzb, dk_rot, dv, d_bq, d_sinks), got = _attn_bwd(zb, dyb, kr, vv, cos, sin, p["b_bq"], p["b_sinks"],
                                                      side=red_bo.scatter_side())
    red_bo.took_scatter(got)
    dkv, d_b_kv = _kv_rope_bwd(dk_rot, dv, cos, sin)
    d_b_w_in = _mm_tn(n_b, dzb, name="mm_d_b_w_in", tm=D_MODEL, tn=512, shards=N_CHIPS)
    d_w_kv, got = _mm_tn(n_kv, dkv, name="mm_d_w_kv", tm=D_MODEL, tn=2 * KV_WIDTH, side=red_bo.share_side())
    red_bo.took_share(got)
    red_bi = _Reduction(["b_w_in", "w_kv"], [d_b_w_in, d_w_kv.reshape(N_CHIPS, D_MODEL // N_CHIPS, 2 * KV_WIDTH)],
                        core, place)
    (dh1, dh1b, d_kv_g, d_b_g), got = _mm_nt_rms_bwd(
        [(dkv, w_kv, kv_norm_g), (dzb, b_w_in, p["b_norm_g"])], h1, dh2, name="mm_dn_b", tm=512,
        side=red_bi.exchange_side())
    red_bi.took_exchange(got)

    d_a_w_out = _mm_tn(y, dh1b, name="mm_d_a_w_out", tm=1024, tn=D_MODEL)
    red_ao = _Reduction(["a_w_out"], [d_a_w_out.reshape(N_CHIPS, A_WIDTH // N_CHIPS, D_MODEL)], core, place)
    dy, got = _mm_nt(dh1b, a_w_out, name="mm_dy", tn=1024, out_dtype=BF16, side=red_ao.exchange_side())
    red_ao.took_exchange(got)
    sides = [red_bi.scatter_side(), red_ao.scatter_side()]
    (dz, d_ln_g, d_ln_b, d_ws, d_bs_t), got = _gate_bwd(z, dy, a_ln_g, a_ln_b, ws, ws_t, bs_t, side=_join(sides))
    got = _split(got, sides)
    red_bi.took_scatter(got[0])
    red_ao.took_scatter(got[1])
    small = {
        "a_ws": d_ws, "a_bs": d_bs_t.T, "a_ln_g": d_ln_g, "a_ln_b": d_ln_b,
        "kv_norm_g": d_kv_g, "b_kv": d_b_kv, "b_norm_g": d_b_g, "b_bq": d_bq,
        "b_sinks": d_sinks[0:1, :N_Q_HEADS], "final_norm_g": d_final_g,
    }
    packed = [n for n in SMALL_ORDER if n != "a_norm_g"]
    small_shapes = [small[n].shape for n in packed] + [(1, 1)]
    small_pack = _pack_rows([small[n] for n in packed] + [loss_blk[0:1, 0:1]], 64)
    seg = small_pack.shape[0] // 8
    small_pack = small_pack.reshape(8, seg, 128)
    sides = [red_bi.share_side(), red_ao.share_side(), _small_scatter_side(small_pack)]
    (d_a_w_in, from_sibling), got = _mm_tn_exchanging(n_a, dz, name="mm_d_a_w_in", shards=N_CHIPS, side=_join(sides))
    got = _split(got, sides)
    red_bi.took_share(got[0])
    red_ao.took_share(got[1])
    small_mine = _sum_small(small_pack, got[2][0], place)

    chip_sum, (small_all,) = _add_halves(d_a_w_in, from_sibling, name="add_sibling_a_w_in",
                                         side=_small_share_side(small_mine))
    (dx, _, d_a_g), (arrived,) = _mm_nt_rms_bwd([(dz, a_w_in, a_norm_g)], x, dh1, name="mm_dn_a", tm=256,
                                                side=_scatter_side([chip_sum]))
    half_ai = _sum_halves(d_a_w_in, from_sibling, arrived, place, name="sum_chips_a_w_in")
    d_a_g = _rows128(d_a_g)
    sides = [_share_side([half_ai]), _spread_side(d_a_g)]
    got = _split(_comm_call(_join(sides), "share_last"), sides)
    grad_ai = got[0][0]
    small_full = dict(zip(packed + ["loss"], _unpack_rows(small_all.reshape(8 * seg, 128), small_shapes)))
    small_full["a_norm_g"] = _sum_in_device_order(d_a_g, got[1][0], place).reshape(1, -1)
    loss = small_full["loss"].reshape(())

    grad_big = {**red_bo.grads, **red_bi.grads, **red_ao.grads, "a_w_in": grad_ai}
    grads = {}
    for n in SMALL_ORDER:
        gfull = small_full[n]
        if n in SHARDED_SMALL:
            width = p[n].shape[-1]
            gfull = lax.dynamic_slice_in_dim(gfull, chip * width, width, axis=-1)
        grads[n] = gfull.reshape(p[n].shape)

    delta, new_m, new_v = {}, {}, {}
    for n in BIG:
        d, nm, nv, g = _adamw(shard2d[n], grad_big[n], m[n].reshape(shard2d[n].shape),
                              v[n].reshape(shard2d[n].shape), name="adamw_" + n)
        delta[n], new_m[n], new_v[n] = d.reshape(p[n].shape), nm.reshape(p[n].shape), nv.reshape(p[n].shape)
        grads[n] = g.reshape(p[n].shape)
    shapes = [p[n].shape for n in SMALL_ORDER]
    packs = [_pack_rows([src[n] for n in SMALL_ORDER], 8) for src in (p, grads, m, v)]
    outs = _adamw(*packs, name="adamw_small")[:3]
    for res, packed in zip((delta, new_m, new_v), outs):
        for n, val in zip(SMALL_ORDER, _unpack_rows(packed, shapes)):
            res[n] = val

    return (loss, dx[None], *[grads[n] for n in WEIGHTS], *[delta[n] for n in WEIGHTS],
            *[new_m[n] for n in WEIGHTS], *[new_v[n] for n in WEIGHTS])


def kernel(x, a_norm_g, a_w_in, a_ln_g, a_ln_b, a_ws, a_bs, a_w_out, kv_norm_g, w_kv, b_kv, b_norm_g, b_w_in, b_bq, b_sinks, b_w_out, final_norm_g, loss_target, m_a_norm_g, m_a_w_in, m_a_ln_g, m_a_ln_b, m_a_ws, m_a_bs, m_a_w_out, m_kv_norm_g, m_w_kv, m_b_kv, m_b_norm_g, m_b_w_in, m_b_bq, m_b_sinks, m_b_w_out, m_final_norm_g, v_a_norm_g, v_a_w_in, v_a_ln_g, v_a_ln_b, v_a_ws, v_a_bs, v_a_w_out, v_kv_norm_g, v_w_kv, v_b_kv, v_b_norm_g, v_b_w_in, v_b_bq, v_b_sinks, v_b_w_out, v_final_norm_g):
    p = dict(a_norm_g=a_norm_g, a_w_in=a_w_in, a_ln_g=a_ln_g, a_ln_b=a_ln_b, a_ws=a_ws, a_bs=a_bs, a_w_out=a_w_out,
             kv_norm_g=kv_norm_g, w_kv=w_kv, b_kv=b_kv, b_norm_g=b_norm_g, b_w_in=b_w_in, b_bq=b_bq, b_sinks=b_sinks,
             b_w_out=b_w_out, final_norm_g=final_norm_g)
    m = dict(a_norm_g=m_a_norm_g, a_w_in=m_a_w_in, a_ln_g=m_a_ln_g, a_ln_b=m_a_ln_b, a_ws=m_a_ws, a_bs=m_a_bs,
             a_w_out=m_a_w_out, kv_norm_g=m_kv_norm_g, w_kv=m_w_kv, b_kv=m_b_kv, b_norm_g=m_b_norm_g, b_w_in=m_b_w_in,
             b_bq=m_b_bq, b_sinks=m_b_sinks, b_w_out=m_b_w_out, final_norm_g=m_final_norm_g)
    v = dict(a_norm_g=v_a_norm_g, a_w_in=v_a_w_in, a_ln_g=v_a_ln_g, a_ln_b=v_a_ln_b, a_ws=v_a_ws, a_bs=v_a_bs,
             a_w_out=v_a_w_out, kv_norm_g=v_kv_norm_g, w_kv=v_w_kv, b_kv=v_b_kv, b_norm_g=v_b_norm_g, b_w_in=v_b_w_in,
             b_bq=v_b_bq, b_sinks=v_b_sinks, b_w_out=v_b_w_out, final_norm_g=v_final_norm_g)
    return _step(x, loss_target, p, m, v)
```

```python
import functools
import math

import jax
import jax.numpy as jnp
from jax import lax
from jax.experimental import pallas as pl
from jax.experimental.pallas import tpu as pltpu

F32 = jnp.float32
BF16 = jnp.bfloat16

D_MODEL = 1024
CHUNK = 128
A_WIDTH = 2048
A_GROUPS = 16
HEAD_DIM = 64
N_Q_HEADS = 16
N_KV_HEADS = 2
Q_PER_KV = 8
B_WIDTH = 1024
KV_WIDTH = 128
ROPE_THETA = 10000.0
EPS = 1e-5
N_CHIPS = 4

ADAM_LR = 0.001
ADAM_B1 = 0.9
ADAM_B2 = 0.999
ADAM_EPS = 1e-08
ADAM_WD = 0.01
ADAM_STEP = 10

VMEM_LIMIT = 48 * 1024 * 1024
MESH = pl.DeviceIdType.MESH
NEG_BIG = -1e30
HBM = pl.BlockSpec(memory_space=pl.ANY)

NN = (((1,), (0,)), ((), ()))
NT = (((1,), (1,)), ((), ()))
TN = (((0,), (0,)), ((), ()))


def _cparams(**kw):
    return pltpu.CompilerParams(vmem_limit_bytes=VMEM_LIMIT, **kw)


class _Side:
    def __init__(self, ins, out_shapes, sems, start, finish, aliases=None, passing=None):
        self.ins, self.out_shapes, self.sems = list(ins), list(out_shapes), list(sems)
        self.start, self.finish = start, finish
        self.passing = passing or (lambda ins, outs, sems: None)
        self.aliases = dict(aliases or {})


def _join(sides):
    sides = [s for s in sides if s is not None]
    if not sides:
        return None
    offs, i, o, m = [], 0, 0, 0
    for s in sides:
        offs.append((i, o, m))
        i, o, m = i + len(s.ins), o + len(s.out_shapes), m + len(s.sems)

    def run(which):
        def go(ins, outs, sems):
            for s, (a, b, c) in zip(sides, offs):
                getattr(s, which)(ins[a:a + len(s.ins)], outs[b:b + len(s.out_shapes)], sems[c:c + len(s.sems)])
        return go

    aliases = {}
    for s, (a, b, _) in zip(sides, offs):
        aliases.update({a + k: b + v for k, v in s.aliases.items()})
    return _Side([x for s in sides for x in s.ins], [x for s in sides for x in s.out_shapes],
                 [x for s in sides for x in s.sems], run("start"), run("finish"), aliases, run("passing"))


def _split(side_outs, sides):
    out, pos = [], 0
    for s in sides:
        out.append(list(side_outs[pos:pos + len(s.out_shapes)]))
        pos += len(s.out_shapes)
    return out


def _call(body, *, grid, in_specs, out_specs, out_shape, args, name, scratch=(), side=None):
    in_specs, out_specs, out_shape, scratch = list(in_specs), list(out_specs), list(out_shape), list(scratch)
    if side is None:
        res = pl.pallas_call(body, grid=grid, in_specs=in_specs, out_specs=out_specs, out_shape=out_shape,
                             scratch_shapes=scratch, name=name, compiler_params=_cparams())(*args)
        return list(res), []
    n_in, n_out, n_sc = len(in_specs), len(out_specs), len(scratch)
    s_in, s_out = len(side.ins), len(side.out_shapes)

    def wrapped(*refs):
        ins, refs = refs[:n_in], refs[n_in:]
        side_ins, refs = refs[:s_in], refs[s_in:]
        outs, refs = refs[:n_out], refs[n_out:]
        side_outs, refs = refs[:s_out], refs[s_out:]
        scr, side_sems = refs[:n_sc], refs[n_sc:]
        step = 0
        for a, g in enumerate(grid):
            step = step * g + pl.program_id(a)
        steps = math.prod(grid)

        @pl.when(step == 0)
        def _():
            side.start(side_ins, side_outs, side_sems)

        body(*ins, *outs, *scr)

        @pl.when(step == (3 * (steps - 1)) // 4)
        def _():
            side.passing(side_ins, side_outs, side_sems)

        @pl.when(step == steps - 1)
        def _():
            side.finish(side_ins, side_outs, side_sems)

    res = pl.pallas_call(
        wrapped, grid=grid, in_specs=in_specs + [HBM] * s_in, out_specs=out_specs + [HBM] * s_out,
        out_shape=out_shape + side.out_shapes, scratch_shapes=scratch + side.sems,
        input_output_aliases={n_in + k: n_out + v for k, v in side.aliases.items()},
        name=name, compiler_params=_cparams(),
    )(*args, *side.ins)
    return list(res[:n_out]), list(res[n_out:])


def _comm_call(side, name):
    s_in, s_out = len(side.ins), len(side.out_shapes)

    def body(*refs):
        ins, outs, sems = refs[:s_in], refs[s_in:s_in + s_out], refs[s_in + s_out:]
        side.start(ins, outs, sems)
        side.passing(ins, outs, sems)
        side.finish(ins, outs, sems)

    return list(pl.pallas_call(
        body, in_specs=[HBM] * s_in, out_specs=[HBM] * s_out, out_shape=side.out_shapes, scratch_shapes=side.sems,
        input_output_aliases=side.aliases, name=name,
    )(*side.ins))


def _matmul(a, b, *, dims, grid, a_spec, b_spec, o_spec, out_shape, name, acc_axis=None,
            residual=None, r_spec=None, side=None):
    has_res = residual is not None

    def body(*refs):
        if has_res:
            a_ref, b_ref, r_ref, o_ref = refs
        else:
            a_ref, b_ref, o_ref = refs
        part = lax.dot_general(a_ref[...], b_ref[...], dims, preferred_element_type=F32)
        if acc_axis is None:
            if has_res:
                part = part + r_ref[...]
            o_ref[...] = part.astype(o_ref.dtype)
        else:
            k = pl.program_id(acc_axis)

            @pl.when(k == 0)
            def _():
                o_ref[...] = part

            @pl.when(k > 0)
            def _():
                o_ref[...] += part

    in_specs = [a_spec, b_spec] + ([r_spec] if has_res else [])
    args = (a, b) + ((residual,) if has_res else ())
    (out,), side_outs = _call(body, grid=grid, in_specs=in_specs, out_specs=[o_spec], out_shape=[out_shape],
                              args=args, name=name, side=side)
    return (out, side_outs) if side is not None else out


def _row_tile(s, want):
    return min(s, want)


def _mm_nn(a, b, *, name, tn, out_dtype=F32, residual=None, tm=512, side=None):
    s, k = a.shape
    tm = _row_tile(s, tm)
    if b.ndim == 3:
        nsh, _, nc = b.shape
        npb = nc // tn
        n = nsh * nc
        b_spec = pl.BlockSpec((None, k, tn), lambda i, j: (j // npb, 0, j % npb))
    else:
        n = b.shape[1]
        b_spec = pl.BlockSpec((k, tn), lambda i, j: (0, j))
    return _matmul(
        a, b, dims=NN, grid=(s // tm, n // tn),
        a_spec=pl.BlockSpec((tm, k), lambda i, j: (i, 0)), b_spec=b_spec,
        o_spec=pl.BlockSpec((tm, tn), lambda i, j: (i, j)),
        out_shape=jax.ShapeDtypeStruct((s, n), out_dtype), name=name, side=side,
        residual=residual, r_spec=pl.BlockSpec((tm, tn), lambda i, j: (i, j)) if residual is not None else None)


def _mm_nt(a, b, *, name, tn=None, tm=512, out_dtype=F32, side=None):
    s, k = a.shape
    tm = _row_tile(s, tm)
    n = b.shape[0]
    tn = n if tn is None else tn
    return _matmul(
        a, b, dims=NT, grid=(s // tm, n // tn),
        a_spec=pl.BlockSpec((tm, k), lambda i, j: (i, 0)),
        b_spec=pl.BlockSpec((tn, k), lambda i, j: (j, 0)),
        o_spec=pl.BlockSpec((tm, tn), lambda i, j: (i, j)),
        out_shape=jax.ShapeDtypeStruct((s, n), out_dtype), name=name, side=side)


def _mm_tn(a, b, *, name, tm, tn, tk=2048, shards=None, side=None):
    s, m = a.shape
    n = b.shape[1]
    tk = _row_tile(s, tk)
    if shards is None:
        o_spec = pl.BlockSpec((tm, tn), lambda i, j, kk: (i, j))
        out_shape = jax.ShapeDtypeStruct((m, n), F32)
    else:
        assert tm == m
        nc = n // shards
        npb = nc // tn
        o_spec = pl.BlockSpec((None, m, tn), lambda i, j, kk: (j // npb, 0, j % npb))
        out_shape = jax.ShapeDtypeStruct((shards, m, nc), F32)
    return _matmul(
        a, b, dims=TN, grid=(m // tm, n // tn, s // tk), acc_axis=2,
        a_spec=pl.BlockSpec((tk, tm), lambda i, j, kk: (kk, i)),
        b_spec=pl.BlockSpec((tk, tn), lambda i, j, kk: (kk, j)),
        o_spec=o_spec, out_shape=out_shape, name=name, side=side)


def _rstd(x):
    return lax.rsqrt(jnp.mean(x * x, axis=-1, keepdims=True) + EPS)


def _rms_fwd(x, gains, *, name, tr=1024):
    s, d = x.shape
    tr = _row_tile(s, tr)
    ng = len(gains)

    def body(*refs):
        xv = refs[0][...]
        xh = xv * _rstd(xv)
        for t in range(ng):
            refs[1 + ng + t][...] = (xh * refs[1 + t][...]).astype(BF16)

    row = pl.BlockSpec((tr, d), lambda i: (i, 0))
    vec = pl.BlockSpec((1, d), lambda i: (0, 0))
    outs, _ = _call(body, grid=(s // tr,), in_specs=[row] + [vec] * ng, out_specs=[row] * ng,
                    out_shape=[jax.ShapeDtypeStruct((s, d), BF16)] * ng, args=(x, *gains), name=name)
    return outs


def _accumulate(i, ref, value):
    @pl.when(i == 0)
    def _():
        ref[...] = value

    @pl.when(i > 0)
    def _():
        ref[...] += value


def _mm_residual_norms(y, w, res, gains, *, name, tm=512, side=None):
    s, k = y.shape
    d = w.shape[1]
    tm = _row_tile(s, tm)
    ng = len(gains)

    def body(y_ref, w_ref, r_ref, *rest):
        g_refs, h_ref, n_refs = rest[:ng], rest[ng], rest[ng + 1:]
        h = r_ref[...] + jnp.dot(y_ref[...], w_ref[...], preferred_element_type=F32)
        h_ref[...] = h
        xh = h * _rstd(h)
        for t in range(ng):
            n_refs[t][...] = (xh * g_refs[t][...]).astype(BF16)

    row = pl.BlockSpec((tm, d), lambda i: (i, 0))
    vec = pl.BlockSpec((1, d), lambda i: (0, 0))
    return _call(
        body, grid=(s // tm,),
        in_specs=[pl.BlockSpec((tm, k), lambda i: (i, 0)), pl.BlockSpec((k, d), lambda i: (0, 0)), row] + [vec] * ng,
        out_specs=[row] * (1 + ng),
        out_shape=[jax.ShapeDtypeStruct((s, d), F32)] + [jax.ShapeDtypeStruct((s, d), BF16)] * ng,
        args=(y, w, res, *gains), name=name, side=side)


def _mm_residual_loss(y, w, res, tgt, gain, *, name, tm=512):
    s, k = y.shape
    d = w.shape[1]
    tm = _row_tile(s, tm)

    def body(y_ref, w_ref, r_ref, t_ref, g_ref, loss_ref, dh_ref, dhb_ref, dg_ref):
        i = pl.program_id(0)
        hv = r_ref[...] + jnp.dot(y_ref[...], w_ref[...], preferred_element_type=F32)
        g = g_ref[...]
        r = _rstd(hv)
        xh = hv * r
        diff = xh * g - t_ref[...]
        part = 0.5 / d * jnp.sum(jnp.sum(diff * diff, axis=-1, keepdims=True), axis=0, keepdims=True)
        dout = diff * (1.0 / d)
        a = dout * g
        dh = r * (a - xh * jnp.mean(a * xh, axis=-1, keepdims=True))
        dh_ref[...] = dh
        dhb_ref[...] = dh.astype(BF16)
        _accumulate(i, dg_ref, jnp.sum(dout * xh, axis=0, keepdims=True))
        _accumulate(i, loss_ref, jnp.broadcast_to(part, (8, 128)))

    row = pl.BlockSpec((tm, d), lambda i: (i, 0))
    vec = pl.BlockSpec((1, d), lambda i: (0, 0))
    outs, _ = _call(
        body, grid=(s // tm,),
        in_specs=[pl.BlockSpec((tm, k), lambda i: (i, 0)), pl.BlockSpec((k, d), lambda i: (0, 0)), row, row, vec],
        out_specs=[pl.BlockSpec((8, 128), lambda i: (0, 0)), row, row, vec],
        out_shape=[jax.ShapeDtypeStruct((8, 128), F32), jax.ShapeDtypeStruct((s, d), F32),
                   jax.ShapeDtypeStruct((s, d), BF16), jax.ShapeDtypeStruct((1, d), F32)],
        args=(y, w, res, tgt, gain), name=name)
    return outs


def _mm_nt_rms_bwd(terms, x, dres, *, name, tm, side=None):
    s, d = x.shape
    tm = _row_tile(s, tm)
    nt = len(terms)

    def body(*refs):
        a_refs, b_refs, g_refs = refs[0:3 * nt:3], refs[1:3 * nt:3], refs[2:3 * nt:3]
        x_ref, dres_ref = refs[3 * nt], refs[3 * nt + 1]
        dx_ref, dxb_ref = refs[3 * nt + 2], refs[3 * nt + 3]
        dg_refs = refs[3 * nt + 4:]
        i = pl.program_id(0)
        xv = x_ref[...]
        r = _rstd(xv)
        xh = xv * r
        acc = jnp.zeros_like(xv)
        for t in range(nt):
            b_ref = b_refs[t]
            if len(b_ref.shape) == 3:
                kc = b_ref.shape[2]
                dn = None
                for sh in range(b_ref.shape[0]):
                    part = lax.dot_general(a_refs[t][:, sh * kc:(sh + 1) * kc], b_ref[sh], NT, preferred_element_type=F32)
                    dn = part if dn is None else dn + part
            else:
                dn = lax.dot_general(a_refs[t][...], b_ref[...], NT, preferred_element_type=F32)
            acc = acc + dn * g_refs[t][...]
            _accumulate(i, dg_refs[t], jnp.sum(dn * xh, axis=0, keepdims=True))
        dx = dres_ref[...] + r * (acc - xh * jnp.mean(acc * xh, axis=-1, keepdims=True))
        dx_ref[...] = dx
        dxb_ref[...] = dx.astype(BF16)

    row = pl.BlockSpec((tm, d), lambda i: (i, 0))
    vec = pl.BlockSpec((1, d), lambda i: (0, 0))
    in_specs, args = [], []
    for a, b, g in terms:
        in_specs += [pl.BlockSpec((tm, a.shape[1]), lambda i: (i, 0)),
                     pl.BlockSpec(b.shape, (lambda i: (0, 0, 0)) if b.ndim == 3 else (lambda i: (0, 0))), vec]
        args += [a, b, g]
    return _call(
        body, grid=(s // tm,), in_specs=in_specs + [row, row], out_specs=[row, row] + [vec] * nt,
        out_shape=[jax.ShapeDtypeStruct((s, d), F32), jax.ShapeDtypeStruct((s, d), BF16)]
        + [jax.ShapeDtypeStruct((1, d), F32)] * nt,
        args=(*args, x, dres), name=name, side=side)


def _causal_mask(transposed=False):
    row = lax.broadcasted_iota(jnp.int32, (CHUNK, CHUNK), 0)
    col = lax.broadcasted_iota(jnp.int32, (CHUNK, CHUNK), 1)
    return col >= row if transposed else col <= row


def _silu_parts(g):
    sg = jax.nn.sigmoid(g)
    return g * sg, sg * (1.0 + g * (1.0 - sg))


def _gate_fwd(z, ln_g, ln_b, ws, bs_t, *, tr=512, side=None):
    s = z.shape[0]
    tr = _row_tile(s, tr)
    w = A_WIDTH

    def body(u_ref, v_ref, g_ref, lg_ref, lb_ref, ws_ref, bst_ref, y_ref):
        v = v_ref[...].astype(F32)
        mu = jnp.mean(v, axis=-1, keepdims=True)
        xc = v - mu
        rs = lax.rsqrt(jnp.mean(xc * xc, axis=-1, keepdims=True) + EPS)
        vln = (xc * rs * lg_ref[...] + lb_ref[...]).astype(BF16)
        mask = _causal_mask()
        for grp in range(A_GROUPS):
            cols = slice(grp * CHUNK, (grp + 1) * CHUNK)
            wsm = jnp.where(mask, ws_ref[grp], 0.0).astype(BF16)
            bcol = bst_ref[:, grp:grp + 1]
            for ci in range(tr // CHUNK):
                rows = slice(ci * CHUNK, (ci + 1) * CHUNK)
                sv = jnp.dot(wsm, vln[rows, cols], preferred_element_type=F32) + bcol
                gv = g_ref[rows, cols].astype(F32)
                y_ref[rows, cols] = (u_ref[rows, cols].astype(F32) * sv * (gv * jax.nn.sigmoid(gv))).astype(BF16)

    vec = pl.BlockSpec((1, w), lambda i: (0, 0))
    (y,), side_outs = _call(
        body, grid=(s // tr,),
        in_specs=[pl.BlockSpec((tr, w), lambda i: (i, 0)), pl.BlockSpec((tr, w), lambda i: (i, 1)),
                  pl.BlockSpec((tr, w), lambda i: (i, 2)), vec, vec,
                  pl.BlockSpec((A_GROUPS, CHUNK, CHUNK), lambda i: (0, 0, 0)),
                  pl.BlockSpec((CHUNK, A_GROUPS), lambda i: (0, 0))],
        out_specs=[pl.BlockSpec((tr, w), lambda i: (i, 0))],
        out_shape=[jax.ShapeDtypeStruct((s, w), BF16)], args=(z, z, z, ln_g, ln_b, ws, bs_t), name="gate_fwd",
        side=side)
    return y, side_outs


def _gate_bwd(z, dy, ln_g, ln_b, ws, ws_t, bs_t, *, tr=256, side=None):
    s = z.shape[0]
    tr = _row_tile(s, tr)
    w = A_WIDTH
    nsteps = s // tr

    def body(u_ref, v_ref, g_ref, dy_ref, lg_ref, lb_ref, ws_ref, wst_ref, bst_ref,
             dz_ref, dlg_ref, dlb_ref, dws_ref, dbst_ref, dvln_sc, dsv_sc):
        i = pl.program_id(0)

        @pl.when(i == 0)
        def _():
            dws_ref[...] = jnp.zeros_like(dws_ref)
            dsv_sc[...] = jnp.zeros_like(dsv_sc)

        v = v_ref[...].astype(F32)
        mu = jnp.mean(v, axis=-1, keepdims=True)
        xc = v - mu
        rs = lax.rsqrt(jnp.mean(xc * xc, axis=-1, keepdims=True) + EPS)
        xh = xc * rs
        lg = lg_ref[...]
        vln = (xh * lg + lb_ref[...]).astype(BF16)
        mask = _causal_mask()
        mask_t = _causal_mask(transposed=True)
        for grp in range(A_GROUPS):
            cols = slice(grp * CHUNK, (grp + 1) * CHUNK)
            wsm = jnp.where(mask, ws_ref[grp], 0.0).astype(BF16)
            wsm_t = jnp.where(mask_t, wst_ref[grp], 0.0).astype(BF16)
            bcol = bst_ref[:, grp:grp + 1]
            for ci in range(tr // CHUNK):
                rows = slice(ci * CHUNK, (ci + 1) * CHUNK)
                vb = vln[rows, cols]
                sv = jnp.dot(wsm, vb, preferred_element_type=F32) + bcol
                uv = u_ref[rows, cols].astype(F32)
                silu, dsilu = _silu_parts(g_ref[rows, cols].astype(F32))
                dyv = dy_ref[rows, cols].astype(F32)
                dyu = dyv * uv
                dz_ref[rows, cols] = (dyv * sv * silu).astype(BF16)
                dz_ref[rows, 2 * w + grp * CHUNK:2 * w + (grp + 1) * CHUNK] = (dyu * sv * dsilu).astype(BF16)
                dsv = dyu * silu
                dsvb = dsv.astype(BF16)
                dvln_sc[rows, cols] = jnp.dot(wsm_t, dsvb, preferred_element_type=F32)
                dws_ref[grp] += lax.dot_general(dsvb, vb, NT, preferred_element_type=F32)
                dsv_sc[grp] += dsv
        dvln = dvln_sc[...]
        dlg_t = jnp.sum(dvln * xh, axis=0, keepdims=True)
        dlb_t = jnp.sum(dvln, axis=0, keepdims=True)
        a = dvln * lg
        dv = rs * (a - jnp.mean(a, axis=-1, keepdims=True) - xh * jnp.mean(a * xh, axis=-1, keepdims=True))
        dz_ref[:, w:2 * w] = dv.astype(BF16)

        @pl.when(i == 0)
        def _():
            dlg_ref[...] = dlg_t
            dlb_ref[...] = dlb_t

        @pl.when(i > 0)
        def _():
            dlg_ref[...] += dlg_t
            dlb_ref[...] += dlb_t

        @pl.when(i == nsteps - 1)
        def _():
            for grp in range(A_GROUPS):
                dws_ref[grp] = jnp.where(mask, dws_ref[grp], 0.0)
                dbst_ref[:, grp:grp + 1] = jnp.sum(dsv_sc[grp], axis=-1, keepdims=True)

    vec = pl.BlockSpec((1, w), lambda i: (0, 0))
    wsspec = pl.BlockSpec((A_GROUPS, CHUNK, CHUNK), lambda i: (0, 0, 0))
    bsspec = pl.BlockSpec((CHUNK, A_GROUPS), lambda i: (0, 0))
    return _call(
        body, grid=(nsteps,),
        in_specs=[pl.BlockSpec((tr, w), lambda i: (i, 0)), pl.BlockSpec((tr, w), lambda i: (i, 1)),
                  pl.BlockSpec((tr, w), lambda i: (i, 2)), pl.BlockSpec((tr, w), lambda i: (i, 0)),
                  vec, vec, wsspec, wsspec, bsspec],
        out_specs=[pl.BlockSpec((tr, 3 * w), lambda i: (i, 0)), vec, vec, wsspec, bsspec],
        out_shape=[jax.ShapeDtypeStruct((s, 3 * w), BF16), jax.ShapeDtypeStruct((1, w), F32),
                   jax.ShapeDtypeStruct((1, w), F32), jax.ShapeDtypeStruct((A_GROUPS, CHUNK, CHUNK), F32),
                   jax.ShapeDtypeStruct((CHUNK, A_GROUPS), F32)],
        scratch=[pltpu.VMEM((tr, w), F32), pltpu.VMEM((A_GROUPS, CHUNK, CHUNK), F32)],
        args=(z, z, z, dy, ln_g, ln_b, ws, ws_t, bs_t), name="gate_bwd", side=side)


HEADS_PER_BLOCK = 128 // HEAD_DIM
BLOCKS_PER_KV = Q_PER_KV // HEADS_PER_BLOCK
SCALE = HEAD_DIM ** -0.5
LOG2_E = math.log2(math.e)


def _rope_tables(s):
    lane = jnp.arange(128)
    inv_freq = ROPE_THETA ** (-(2 * (lane % (HEAD_DIM // 2))).astype(F32) / HEAD_DIM)
    sign = jnp.where(lane % HEAD_DIM < HEAD_DIM // 2, -1.0, 1.0).astype(F32)
    ang = jnp.arange(s, dtype=F32)[:, None] * inv_freq[None, :]
    return jnp.cos(ang), jnp.sin(ang) * sign[None, :]


def _swap_halves(x):
    n = x.shape[-1]
    lane = lax.broadcasted_iota(jnp.int32, x.shape, x.ndim - 1)
    first = (lane % HEAD_DIM) < (HEAD_DIM // 2)
    return jnp.where(first, pltpu.roll(x, n - HEAD_DIM // 2, x.ndim - 1), pltpu.roll(x, HEAD_DIM // 2, x.ndim - 1))


def _left_half(rows):
    return lax.broadcasted_iota(jnp.int32, (rows, 128), 1) < HEAD_DIM


def _dup_heads(x):
    left = _left_half(x.shape[0])
    swapped = pltpu.roll(x, HEAD_DIM, 1)
    return jnp.concatenate([jnp.where(left, x, swapped), jnp.where(left, swapped, x)], axis=-1)


def _fold_heads(a):
    b0, b1 = a[:, :128], a[:, 128:]
    f0 = b0 + pltpu.roll(b0, HEAD_DIM, 1)
    f1 = b1 + pltpu.roll(b1, HEAD_DIM, 1)
    return jnp.where(_left_half(a.shape[0]), f0, f1)


def _kv_rope(n_kv, w_kv, b_kv, cos, sin, *, tr=2048):
    s, d = n_kv.shape
    tr = _row_tile(s, tr)

    def body(n_ref, w_ref, b_ref, c_ref, s_ref, k_ref, v_ref):
        x = jnp.dot(n_ref[...], w_ref[...], preferred_element_type=F32) + b_ref[...]
        k = x[:, :KV_WIDTH]
        k_ref[...] = _dup_heads(k * c_ref[...] + _swap_halves(k) * s_ref[...]).astype(BF16)
        v_ref[...] = _dup_heads(x[:, KV_WIDTH:]).astype(BF16)

    tab = pl.BlockSpec((tr, KV_WIDTH), lambda i: (i, 0))
    wide = pl.BlockSpec((tr, 2 * KV_WIDTH), lambda i: (i, 0))
    outs, _ = _call(body, grid=(s // tr,),
                    in_specs=[pl.BlockSpec((tr, d), lambda i: (i, 0)), pl.BlockSpec((d, 2 * KV_WIDTH), lambda i: (0, 0)),
                              pl.BlockSpec((1, 2 * KV_WIDTH), lambda i: (0, 0)), tab, tab],
                    out_specs=[wide, wide], out_shape=[jax.ShapeDtypeStruct((s, 2 * KV_WIDTH), BF16)] * 2,
                    args=(n_kv, w_kv, b_kv, cos, sin), name="kv_rope")
    return outs


def _kv_rope_bwd(dk2, dv2, cos, sin, *, tr=2048):
    s = dk2.shape[0]
    tr = _row_tile(s, tr)

    def body(dk_ref, dv_ref, c_ref, s_ref, dkv_ref, db_ref):
        i = pl.program_id(0)
        d = _fold_heads(dk_ref[...])
        dk = d * c_ref[...] + _swap_halves(d * s_ref[...])
        dvv = _fold_heads(dv_ref[...])
        dkv_ref[:, :KV_WIDTH] = dk.astype(BF16)
        dkv_ref[:, KV_WIDTH:] = dvv.astype(BF16)
        sk = jnp.sum(dk, axis=0, keepdims=True)
        sv = jnp.sum(dvv, axis=0, keepdims=True)

        @pl.when(i == 0)
        def _():
            db_ref[:, :KV_WIDTH] = sk
            db_ref[:, KV_WIDTH:] = sv

        @pl.when(i > 0)
        def _():
            db_ref[:, :KV_WIDTH] += sk
            db_ref[:, KV_WIDTH:] += sv

    tab = pl.BlockSpec((tr, KV_WIDTH), lambda i: (i, 0))
    wide = pl.BlockSpec((tr, 2 * KV_WIDTH), lambda i: (i, 0))
    outs, _ = _call(body, grid=(s // tr,), in_specs=[wide, wide, tab, tab],
                    out_specs=[wide, pl.BlockSpec((1, 2 * KV_WIDTH), lambda i: (0, 0))],
                    out_shape=[jax.ShapeDtypeStruct((s, 2 * KV_WIDTH), BF16),
                               jax.ShapeDtypeStruct((1, 2 * KV_WIDTH), F32)],
                    args=(dk2, dv2, cos, sin), name="kv_rope_bwd")
    return outs


def _from_previous():
    cols = Q_PER_KV * CHUNK
    k = lax.broadcasted_iota(jnp.int32, (CHUNK, cols), 0)
    q = lax.broadcasted_iota(jnp.int32, (CHUNK, cols), 1) & (CHUNK - 1)
    return k > q


def _fold(x2, prev):
    return jnp.where(prev, x2[:CHUNK], x2[CHUNK:])


def _unfold(x, prev):
    zero = jnp.zeros_like(x)
    return jnp.concatenate([jnp.where(prev, x, zero), jnp.where(prev, zero, x)], axis=0)


def _stack_heads(blocks, left):
    parts = []
    for b in blocks:
        parts.append(jnp.where(left, b, jnp.zeros_like(b)))
        parts.append(jnp.where(left, jnp.zeros_like(b), b))
    return jnp.concatenate(parts, axis=0)


def _unstack_heads(xt):
    top = lax.broadcasted_iota(jnp.int32, (128, CHUNK), 0) < HEAD_DIM
    return [jnp.where(top, xt[:, (2 * b) * CHUNK:(2 * b + 1) * CHUNK], xt[:, (2 * b + 1) * CHUNK:(2 * b + 2) * CHUNK]).T
            for b in range(BLOCKS_PER_KV)]


def _sink_row(sk_ref, kvh):
    return jnp.concatenate([jnp.full((1, CHUNK), sk_ref[0, kvh * Q_PER_KV + r], F32) for r in range(Q_PER_KV)], axis=1)


def _stacked_probs(qs, kd, prev, sink, i):
    sc2 = lax.dot_general(kd, qs, NT, preferred_element_type=F32)
    no_previous = jnp.where(i > 0, 0.0, NEG_BIG)
    sc = jnp.where(prev, sc2[:CHUNK] + no_previous, sc2[CHUNK:])
    sink = sink * (1.0 / SCALE)
    m = jnp.maximum(jnp.max(sc, axis=0, keepdims=True), sink)
    p = jnp.exp2((sc - m) * (SCALE * LOG2_E))
    esink = jnp.exp2((sink - m) * (SCALE * LOG2_E))
    inv = 1.0 / (jnp.sum(p, axis=0, keepdims=True) + esink)
    return p * inv, esink * inv


def _lane_block(b):
    return slice(b * 128, (b + 1) * 128)


def _rope_blocks(zq_ref, bq_ref, cos, sin, kvh, rows):
    out = []
    for b in range(BLOCKS_PER_KV):
        cols = _lane_block(kvh * BLOCKS_PER_KV + b)
        q = zq_ref[rows, cols].astype(F32) + bq_ref[:, cols]
        out.append((q * cos + _swap_halves(q) * sin).astype(BF16))
    return out


CHUNKS_PER_STEP = 2


def _attn_specs():
    rows = CHUNKS_PER_STEP * CHUNK
    qspec = pl.BlockSpec((rows, B_WIDTH), lambda i: (i, 0))
    gspec = pl.BlockSpec((rows, B_WIDTH), lambda i: (i, 1))
    prev = pl.BlockSpec((CHUNK, 2 * KV_WIDTH), lambda i: (jnp.maximum(CHUNKS_PER_STEP * i - 1, 0), 0))
    cur = pl.BlockSpec((rows, 2 * KV_WIDTH), lambda i: (i, 0))
    tab = pl.BlockSpec((rows, KV_WIDTH), lambda i: (i, 0))
    bq = pl.BlockSpec((1, B_WIDTH), lambda i: (0, 0))
    sinks = pl.BlockSpec(memory_space=pltpu.SMEM)
    return qspec, gspec, prev, cur, tab, bq, sinks


def _chunk_keys(prev_ref, cur_ref, sub):
    before = prev_ref[...] if sub == 0 else cur_ref[(sub - 1) * CHUNK:sub * CHUNK]
    return jnp.concatenate([before, cur_ref[sub * CHUNK:(sub + 1) * CHUNK]], axis=0)


def _attn_fwd(zb, k2, v2, cos, sin, b_bq, sinks, *, side=None):
    s = zb.shape[0]

    def body(zq_ref, zg_ref, kp_ref, kc_ref, vp_ref, vc_ref, c_ref, s_ref, bq_ref, sk_ref, y_ref):
        prev = _from_previous()
        left = _left_half(CHUNK)
        for sub in range(CHUNKS_PER_STEP):
            chunk = CHUNKS_PER_STEP * pl.program_id(0) + sub
            rows = slice(sub * CHUNK, (sub + 1) * CHUNK)
            cos, sin = c_ref[rows, :], s_ref[rows, :]
            kcat, vcat = _chunk_keys(kp_ref, kc_ref, sub), _chunk_keys(vp_ref, vc_ref, sub)
            for kvh in range(N_KV_HEADS):
                qs = _stack_heads(_rope_blocks(zq_ref, bq_ref, cos, sin, kvh, rows), left)
                p, _ = _stacked_probs(qs, kcat[:, _lane_block(kvh)], prev, _sink_row(sk_ref, kvh), chunk)
                ot = lax.dot_general(vcat[:, _lane_block(kvh)], _unfold(p, prev).astype(BF16), TN,
                                     preferred_element_type=F32)
                for b, ob in enumerate(_unstack_heads(ot)):
                    cols = _lane_block(kvh * BLOCKS_PER_KV + b)
                    gv = zg_ref[rows, cols].astype(F32)
                    y_ref[rows, cols] = (ob * (gv * jax.nn.sigmoid(gv))).astype(BF16)

    qspec, gspec, prev, cur, tab, bq, sk = _attn_specs()
    (y,), side_outs = _call(body, grid=(s // (CHUNKS_PER_STEP * CHUNK),),
                            in_specs=[qspec, gspec, prev, cur, prev, cur, tab, tab, bq, sk],
                            out_specs=[qspec], out_shape=[jax.ShapeDtypeStruct((s, B_WIDTH), BF16)],
                            args=(zb, zb, k2, k2, v2, v2, cos, sin, b_bq, sinks), name="attn_fwd", side=side)
    return y, side_outs


def _attn_bwd(zb, dyb, k2, v2, cos, sin, b_bq, sinks, *, side=None):
    s = zb.shape[0]

    def body(zq_ref, zg_ref, dy_ref, kp_ref, kc_ref, vp_ref, vc_ref, c_ref, s_ref, bq_ref, sk_ref,
             dz_ref, dk_ref, dv_ref, dbq_ref, dsk_ref):
        i = pl.program_id(0)

        @pl.when(i == 0)
        def _():
            dk_ref[...] = jnp.zeros_like(dk_ref)
            dv_ref[...] = jnp.zeros_like(dv_ref)
            dbq_ref[...] = jnp.zeros_like(dbq_ref)
            dsk_ref[...] = jnp.zeros_like(dsk_ref)

        prev = _from_previous()
        left = _left_half(CHUNK)
        lane = lax.broadcasted_iota(jnp.int32, (1, 128), 1)
        dsk_row = jnp.zeros((1, 128), F32)
        for sub in range(CHUNKS_PER_STEP):
            chunk = CHUNKS_PER_STEP * i + sub
            rows = slice(sub * CHUNK, (sub + 1) * CHUNK)
            cos, sin = c_ref[rows, :], s_ref[rows, :]
            kcat, vcat = _chunk_keys(kp_ref, kc_ref, sub), _chunk_keys(vp_ref, vc_ref, sub)
            cur_rows = pl.ds(pl.multiple_of(chunk * CHUNK, CHUNK), CHUNK)
            for kvh in range(N_KV_HEADS):
                kd, vd = kcat[:, _lane_block(kvh)], vcat[:, _lane_block(kvh)]
                qs = _stack_heads(_rope_blocks(zq_ref, bq_ref, cos, sin, kvh, rows), left)
                p, psink = _stacked_probs(qs, kd, prev, _sink_row(sk_ref, kvh), chunk)
                pb = _unfold(p, prev).astype(BF16)
                ot = lax.dot_general(vd, pb, TN, preferred_element_type=F32)
                gates, dys = [], []
                for b in range(BLOCKS_PER_KV):
                    cols = _lane_block(kvh * BLOCKS_PER_KV + b)
                    gates.append(_silu_parts(zg_ref[rows, cols].astype(F32)))
                    dys.append(dy_ref[rows, cols].astype(F32))
                dos = _stack_heads([(dyv * silu).astype(BF16) for dyv, (silu, _) in zip(dys, gates)], left)
                dp = _fold(lax.dot_general(vd, dos, NT, preferred_element_type=F32), prev)
                delta = jnp.sum(p * dp, axis=0, keepdims=True)
                ds = _unfold(p * (dp - delta) * SCALE, prev).astype(BF16)
                dqt = lax.dot_general(kd, ds, TN, preferred_element_type=F32)
                dk_part = jnp.dot(ds, qs, preferred_element_type=F32)
                dv_part = jnp.dot(pb, dos, preferred_element_type=F32)
                dk_ref[cur_rows, _lane_block(kvh)] += dk_part[CHUNK:]
                dv_ref[cur_rows, _lane_block(kvh)] += dv_part[CHUNK:]

                @pl.when(chunk > 0)
                def _(kvh=kvh, chunk=chunk, dk_part=dk_part, dv_part=dv_part):
                    prev_rows = pl.ds(pl.multiple_of((chunk - 1) * CHUNK, CHUNK), CHUNK)
                    dk_ref[prev_rows, _lane_block(kvh)] += dk_part[:CHUNK]
                    dv_ref[prev_rows, _lane_block(kvh)] += dv_part[:CHUNK]

                sink_grad = psink * delta
                for r in range(Q_PER_KV):
                    dsink = -jnp.sum(sink_grad[:, r * CHUNK:(r + 1) * CHUNK], axis=1, keepdims=True)
                    dsk_row = dsk_row + jnp.where(lane == kvh * Q_PER_KV + r, dsink, 0.0)
                blocks = zip(_unstack_heads(ot), _unstack_heads(dqt), dys, gates)
                for b, (ob, dqr, dyv, (_, dsilu)) in enumerate(blocks):
                    blk = kvh * BLOCKS_PER_KV + b
                    dq = dqr * cos + _swap_halves(dqr * sin)
                    dbq_ref[:, _lane_block(blk)] += jnp.sum(dq, axis=0, keepdims=True)
                    dz_ref[rows, _lane_block(blk)] = dq.astype(BF16)
                    dz_ref[rows, _lane_block(B_WIDTH // 128 + blk)] = (dyv * ob * dsilu).astype(BF16)
        dsk_ref[0:1, :] += dsk_row

    qspec, gspec, prev, cur, tab, bq, sk = _attn_specs()
    full = pl.BlockSpec((s, 2 * KV_WIDTH), lambda i: (0, 0))
    return _call(
        body, grid=(s // (CHUNKS_PER_STEP * CHUNK),),
        in_specs=[qspec, gspec, qspec, prev, cur, prev, cur, tab, tab, bq, sk],
        out_specs=[pl.BlockSpec((CHUNKS_PER_STEP * CHUNK, 2 * B_WIDTH), lambda i: (i, 0)), full, full, bq,
                   pl.BlockSpec((8, 128), lambda i: (0, 0))],
        out_shape=[jax.ShapeDtypeStruct((s, 2 * B_WIDTH), BF16), jax.ShapeDtypeStruct((s, 2 * KV_WIDTH), F32),
                   jax.ShapeDtypeStruct((s, 2 * KV_WIDTH), F32), jax.ShapeDtypeStruct((1, B_WIDTH), F32),
                   jax.ShapeDtypeStruct((8, 128), F32)],
        args=(zb, zb, dyb, k2, k2, v2, v2, cos, sin, b_bq, sinks), name="attn_bwd", side=side)


def _place():
    x, y, c = lax.axis_index("x"), lax.axis_index("y"), lax.axis_index("c")
    return x, y, c, [(1 - x, y), (x, 1 - y), (1 - x, 1 - y)]


def _relations():
    return [(r >> 2 & 1, r >> 1 & 1, r & 1) for r in range(1, 8)]


def _gather_side(arrs):
    n = len(arrs)

    def copies(ins, outs, sems):
        send_ici, recv_ici, send_d2d, recv_d2d, local_sem = sems
        x, y, c, chips = _place()
        me = 2 * x + y

        def rows(a, half):
            hr = arrs[a].shape[0] // 2
            return pl.ds(half * hr, hr)

        def ici(a, j, src_chip, to):
            return pltpu.make_async_remote_copy(
                src_ref=ins[a].at[rows(a, c)], dst_ref=outs[a].at[src_chip, rows(a, c)],
                send_sem=send_ici.at[a, j], recv_sem=recv_ici.at[a, j], device_id=to, device_id_type=MESH)

        def d2d(a, j, chip, half):
            blk = outs[a].at[chip, rows(a, half)]
            return pltpu.make_async_remote_copy(
                src_ref=blk, dst_ref=blk, send_sem=send_d2d.at[a, j], recv_sem=recv_d2d.at[a, j],
                device_id=(x, y, 1 - c), device_id_type=MESH)

        local = [pltpu.make_async_copy(ins[a], outs[a].at[me], local_sem.at[a]) for a in range(n)]
        pairs = [(a, j, chip) for a in range(n) for j, chip in enumerate(chips)]
        return c, me, local, ici, d2d, pairs

    def start(ins, outs, sems):
        c, me, local, ici, _, pairs = copies(ins, outs, sems)
        for cp in local:
            cp.start()
        for a, j, chip in pairs:
            ici(a, j, me, (*chip, c)).start()

    def passing(ins, outs, sems):
        c, _, _, ici, d2d, pairs = copies(ins, outs, sems)
        for a, j, (px, py) in pairs:
            ici(a, j, 2 * px + py, (px, py, c)).wait_recv()
            d2d(a, j, 2 * px + py, c).start()

    def finish(ins, outs, sems):
        c, me, local, ici, d2d, pairs = copies(ins, outs, sems)
        for a, j, (px, py) in pairs:
            d2d(a, j, 2 * px + py, 1 - c).wait_recv()
        for a, j, (px, py) in pairs:
            ici(a, j, me, (px, py, c)).wait_send()
            d2d(a, j, 2 * px + py, c).wait_send()
        for cp in local:
            cp.wait()

    return _Side(arrs, [jax.ShapeDtypeStruct((N_CHIPS,) + a.shape, a.dtype) for a in arrs],
                 [pltpu.SemaphoreType.DMA((n, 3))] * 4 + [pltpu.SemaphoreType.DMA((n,))], start, finish,
                 passing=passing)


def _exchange_side(grads):
    n = len(grads)

    def copies(ins, outs, sems):
        send_sem, recv_sem = sems
        x, y, c, _ = _place()
        cps = []
        for a in range(n):
            hr = grads[a].shape[1] // 2
            cps.append(pltpu.make_async_remote_copy(
                src_ref=ins[a].at[:, pl.ds((1 - c) * hr, hr), :], dst_ref=outs[a],
                send_sem=send_sem.at[a], recv_sem=recv_sem.at[a], device_id=(x, y, 1 - c), device_id_type=MESH))
        return cps

    def start(ins, outs, sems):
        for cp in copies(ins, outs, sems):
            cp.start()

    def finish(ins, outs, sems):
        for cp in copies(ins, outs, sems):
            cp.wait()

    return _Side(grads, [jax.ShapeDtypeStruct((g.shape[0], g.shape[1] // 2, g.shape[2]), g.dtype) for g in grads],
                 [pltpu.SemaphoreType.DMA((n,))] * 2, start, finish)


def _scatter_side(chip_sums, small=None):
    n = len(chip_sums)
    arrs = list(chip_sums) + ([small] if small is not None else [])

    def copies(ins, outs, sems):
        x, y, c, chips = _place()
        cps = []
        for a in range(n):
            for j, (px, py) in enumerate(chips):
                cps.append(pltpu.make_async_remote_copy(
                    src_ref=ins[a].at[2 * px + py], dst_ref=outs[a].at[j],
                    send_sem=sems[0].at[a, j], recv_sem=sems[1].at[a, j], device_id=(px, py, c), device_id_type=MESH))
        if small is not None:
            for r, (fx, fy, fc) in enumerate(_relations(), start=1):
                px, py, pc = x ^ fx, y ^ fy, c ^ fc
                cps.append(pltpu.make_async_remote_copy(
                    src_ref=ins[n].at[4 * px + 2 * py + pc], dst_ref=outs[n].at[r],
                    send_sem=sems[2].at[r - 1], recv_sem=sems[3].at[r - 1], device_id=(px, py, pc),
                    device_id_type=MESH))
        return cps

    def start(ins, outs, sems):
        for cp in copies(ins, outs, sems):
            cp.start()

    def finish(ins, outs, sems):
        for cp in copies(ins, outs, sems):
            cp.wait()

    shapes = [jax.ShapeDtypeStruct((3,) + t.shape[1:], t.dtype) for t in chip_sums]
    sems = [pltpu.SemaphoreType.DMA((n, 3))] * 2
    if small is not None:
        shapes.append(jax.ShapeDtypeStruct(small.shape, small.dtype))
        sems += [pltpu.SemaphoreType.DMA((7,))] * 2
    return _Side(arrs, shapes, sems, start, finish)


def _small_scatter_side(small):
    def copies(ins, outs, sems):
        x, y, c, _ = _place()
        cps = []
        for r, (fx, fy, fc) in enumerate(_relations(), start=1):
            px, py, pc = x ^ fx, y ^ fy, c ^ fc
            cps.append(pltpu.make_async_remote_copy(
                src_ref=ins[0].at[4 * px + 2 * py + pc], dst_ref=outs[0].at[r],
                send_sem=sems[0].at[r - 1], recv_sem=sems[1].at[r - 1], device_id=(px, py, pc), device_id_type=MESH))
        return cps

    def start(ins, outs, sems):
        for cp in copies(ins, outs, sems):
            cp.start()

    def finish(ins, outs, sems):
        for cp in copies(ins, outs, sems):
            cp.wait()

    return _Side([small], [jax.ShapeDtypeStruct(small.shape, small.dtype)], [pltpu.SemaphoreType.DMA((7,))] * 2,
                 start, finish)


def _small_share_side(small):
    return _share_side([], small)


def _share_side(halves, small=None):
    n = len(halves)
    arrs = list(halves) + ([small] if small is not None else [])

    def copies(ins, outs, sems, mine):
        x, y, c, _ = _place()
        me = 4 * x + 2 * y + c
        cps = []
        for a in range(n):
            hr = halves[a].shape[0] // 2
            rows = pl.ds((c if mine else 1 - c) * hr, hr)
            cps.append(pltpu.make_async_remote_copy(
                src_ref=ins[a].at[rows], dst_ref=outs[a].at[rows],
                send_sem=sems[0].at[a], recv_sem=sems[1].at[a], device_id=(x, y, 1 - c), device_id_type=MESH))
        if small is not None:
            for r, (fx, fy, fc) in enumerate(_relations(), start=1):
                px, py, pc = x ^ fx, y ^ fy, c ^ fc
                seg = me if mine else 4 * px + 2 * py + pc
                cps.append(pltpu.make_async_remote_copy(
                    src_ref=ins[n].at[seg], dst_ref=outs[n].at[seg],
                    send_sem=sems[-2].at[r - 1], recv_sem=sems[-1].at[r - 1], device_id=(px, py, pc),
                    device_id_type=MESH))
        return cps

    def start(ins, outs, sems):
        for cp in copies(ins, outs, sems, True):
            cp.start()

    def finish(ins, outs, sems):
        for cp in copies(ins, outs, sems, False):
            cp.wait_recv()
        for cp in copies(ins, outs, sems, True):
            cp.wait_send()

    sems = ([pltpu.SemaphoreType.DMA((n,))] * 2 if n else []) + (
        [pltpu.SemaphoreType.DMA((7,))] * 2 if small is not None else [])
    return _Side(arrs, [jax.ShapeDtypeStruct(h.shape, h.dtype) for h in arrs], sems, start, finish,
                 aliases={i: i for i in range(len(arrs))})


GATHER_PIECES = [(0, 0), (0, 1), (1, 0), (2, 0), (1, 1), (2, 1), (3, 0), (3, 1)]


def _mm_gathering(a, shard, order, *, name, tm=1024):
    s, k = a.shape
    nc = shard.shape[1]
    tm = _row_tile(s, tm)
    tn = nc // 2
    hr = k // 2
    qr = hr // 2
    blocks = jnp.stack([order[src] * 2 + h for src, h in GATHER_PIECES]).astype(jnp.int32)

    def body(blocks_ref, a_ref, shard_ref, z_ref, full_ref, wbuf, send_ici, recv_ici, send_relay,
             recv_relay, send_d2d, recv_d2d, local_sem, load_sem):
        piece, i = pl.program_id(0), pl.program_id(1)
        x, y, c, chips = _place()
        me = 2 * x + y
        nbrs = chips[:2]
        chip_of = [2 * px + py for px, py in chips]

        def quarter(q):
            return pl.ds(c * hr + q * qr, qr)

        def sibling_quarter(q):
            return pl.ds((1 - c) * hr + q * qr, qr)

        def whole(half):
            return pl.ds(half * hr, hr)

        def cols(h):
            return pl.ds(h * tn, tn)

        def direct(j, src_chip, h):
            return pltpu.make_async_remote_copy(
                src_ref=shard_ref.at[whole(c), cols(h)], dst_ref=full_ref.at[src_chip, whole(c), cols(h)],
                send_sem=send_ici.at[j, h], recv_sem=recv_ici.at[j, h], device_id=(*nbrs[j], c), device_id_type=MESH)

        def relay(j, src_chip, h):
            blk = full_ref.at[src_chip, quarter(j), cols(h)]
            return pltpu.make_async_remote_copy(
                src_ref=blk, dst_ref=blk, send_sem=send_relay.at[j, h], recv_sem=recv_relay.at[j, h],
                device_id=(*nbrs[1 - j], c), device_id_type=MESH)

        def d2d(j, chip, rows, h):
            blk = full_ref.at[chip, rows, cols(h)]
            return pltpu.make_async_remote_copy(
                src_ref=blk, dst_ref=blk, send_sem=send_d2d.at[j, h], recv_sem=recv_d2d.at[j, h],
                device_id=(x, y, 1 - c), device_id_type=MESH)

        def load(p):
            src, h = GATHER_PIECES[p]
            where = shard_ref if src == 0 else full_ref.at[chip_of[src - 1]]
            return pltpu.make_async_copy(where.at[:, cols(h)], wbuf.at[p % 2], load_sem.at[p % 2])

        local = pltpu.make_async_copy(shard_ref, full_ref.at[me], local_sem)

        def arrived(p):
            src, h = GATHER_PIECES[p]
            if src in (1, 2):
                j = src - 1
                direct(j, chip_of[j], h).wait_recv()
                relay(j, chip_of[j], h).start()
                d2d(j, chip_of[j], whole(c), h).start()
            elif src == 3:
                for j in range(2):
                    relay(1 - j, chip_of[2], h).wait_recv()
                    d2d(2 + j, chip_of[2], quarter(1 - j), h).start()

        def fetch(p):
            src, h = GATHER_PIECES[p]
            if src in (1, 2):
                d2d(src - 1, chip_of[src - 1], whole(1 - c), h).wait_recv()
            elif src == 3:
                for j in range(2):
                    d2d(2 + j, chip_of[2], sibling_quarter(1 - j), h).wait_recv()
            load(p).start()

        n_i = s // tm
        for p in range(len(GATHER_PIECES)):
            @pl.when(jnp.logical_and(piece == p, i == 0))
            def _(p=p):
                if p == 0:
                    local.start()
                    for hh in range(2):
                        for j in range(2):
                            direct(j, me, hh).start()
                    load(0).start()
                load(p).wait()

        z_ref[...] = jnp.dot(a_ref[...], wbuf[piece % 2], preferred_element_type=F32).astype(z_ref.dtype)

        for p in range(len(GATHER_PIECES) - 1):
            @pl.when(jnp.logical_and(piece == p, i == min(1, n_i - 1)))
            def _(p=p):
                arrived(p + 1)

            @pl.when(jnp.logical_and(piece == p, i == min(2, n_i - 1)))
            def _(p=p):
                fetch(p + 1)

        last = jnp.logical_and(piece == len(GATHER_PIECES) - 1, i == n_i - 1)

        @pl.when(last)
        def _():
            for h in range(2):
                for j in range(2):
                    direct(j, me, h).wait_send()
                    relay(j, chip_of[j], h).wait_send()
                    d2d(j, chip_of[j], whole(c), h).wait_send()
                    d2d(2 + j, chip_of[2], quarter(1 - j), h).wait_send()
            local.wait()

    return pl.pallas_call(
        body,
        grid_spec=pltpu.PrefetchScalarGridSpec(
            num_scalar_prefetch=1, grid=(len(GATHER_PIECES), s // tm),
            in_specs=[pl.BlockSpec((tm, k), lambda p, i, blocks: (i, 0)), HBM],
            out_specs=[pl.BlockSpec((tm, tn), lambda p, i, blocks: (i, blocks[p])), HBM],
            scratch_shapes=[pltpu.VMEM((2, k, tn), BF16)] + [pltpu.SemaphoreType.DMA((2, 2))] * 4
            + [pltpu.SemaphoreType.DMA((4, 2))] * 2 + [pltpu.SemaphoreType.DMA, pltpu.SemaphoreType.DMA((2,))]),
        out_shape=[jax.ShapeDtypeStruct((s, N_CHIPS * nc), BF16), jax.ShapeDtypeStruct((N_CHIPS, k, nc), BF16)],
        name=name, compiler_params=_cparams(),
    )(blocks, a, shard)


def _mm_tn_exchanging(a, b, *, name, shards, tk=2048, side=None):
    s, m = a.shape
    nc = b.shape[1] // shards
    tk = _row_tile(s, tk)
    nk = s // tk
    hm = m // 2

    def body(a_ref, b_ref, part_ref, sib_ref, acc, keep_sem, send_sem, recv_sem):
        j, kk = pl.program_id(0), pl.program_id(1)
        x, y, c, _ = _place()

        def keep(jj, slot):
            mine = pl.ds(c * hm, hm)
            return pltpu.make_async_copy(acc.at[slot, mine], part_ref.at[jj], keep_sem.at[slot])

        def give(jj, slot):
            return pltpu.make_async_remote_copy(
                src_ref=acc.at[slot, pl.ds((1 - c) * hm, hm)], dst_ref=sib_ref.at[jj],
                send_sem=send_sem.at[slot], recv_sem=recv_sem.at[jj], device_id=(x, y, 1 - c), device_id_type=MESH)

        part = lax.dot_general(a_ref[...], b_ref[...], TN, preferred_element_type=F32)
        for slot in range(2):
            @pl.when(j % 2 == slot)
            def _(slot=slot):
                @pl.when(jnp.logical_and(kk == 0, j >= 2))
                def _():
                    keep(j - 2, slot).wait()
                    give(j - 2, slot).wait_send()

                @pl.when(kk == 0)
                def _():
                    acc[slot] = part

                @pl.when(kk > 0)
                def _():
                    acc[slot] += part

                @pl.when(kk == nk - 1)
                def _():
                    keep(j, slot).start()
                    give(j, slot).start()

        @pl.when(jnp.logical_and(j == shards - 1, kk == nk - 1))
        def _():
            for jj in range(shards - 2, shards):
                keep(jj, jj % 2).wait()
                give(jj, jj % 2).wait_send()
            for jj in range(shards):
                give(jj, jj % 2).wait_recv()

    assert shards >= 2
    return _call(
        body, grid=(shards, nk),
        in_specs=[pl.BlockSpec((tk, m), lambda j, kk: (kk, 0)), pl.BlockSpec((tk, nc), lambda j, kk: (kk, j))],
        out_specs=[HBM, HBM],
        out_shape=[jax.ShapeDtypeStruct((shards, hm, nc), F32), jax.ShapeDtypeStruct((shards, hm, nc), F32)],
        scratch=[pltpu.VMEM((2, m, nc), F32), pltpu.SemaphoreType.DMA((2,)), pltpu.SemaphoreType.DMA((2,)),
                 pltpu.SemaphoreType.DMA((shards,))],
        args=(a, b), name=name, side=side)


def _col_tile(cols):
    return cols if cols <= 2048 else 512


def _add_sibling(grad, recv, core, *, name):
    k, r, c = grad.shape
    hr = r // 2
    tr = min(hr, 256)
    tc = _col_tile(c)
    nrb = hr // tr

    def body(core_ref, g_ref, r_ref, o_ref):
        o_ref[...] = (g_ref[...] + r_ref[...]).astype(BF16)

    return pl.pallas_call(
        body,
        grid_spec=pltpu.PrefetchScalarGridSpec(
            num_scalar_prefetch=1, grid=(k, nrb, c // tc),
            in_specs=[pl.BlockSpec((None, tr, tc), lambda kk, i, j, core: (kk, core[0] * nrb + i, j)),
                      pl.BlockSpec((None, tr, tc), lambda kk, i, j, core: (kk, i, j))],
            out_specs=pl.BlockSpec((None, tr, tc), lambda kk, i, j, core: (kk, i, j))),
        out_shape=jax.ShapeDtypeStruct((k, hr, c), BF16), name=name, compiler_params=_cparams(),
    )(core, grad, recv)


def _sum_chips(grad, from_sibling, recv, place, *, name):
    _, hr, c = from_sibling.shape
    tr = min(hr, 256)
    tc = _col_tile(c)
    nrb = hr // tr

    def body(place_ref, g_ref, s_ref, r0_ref, r1_ref, r2_ref, o_ref):
        own = g_ref[...] + s_ref[...]
        o_ref[...] = ((own + r0_ref[...].astype(F32)) + r1_ref[...].astype(F32)) + r2_ref[...].astype(F32)

    def rspec(j):
        return pl.BlockSpec((None, tr, tc), lambda i, jj, place: (j, i, jj))

    return pl.pallas_call(
        body,
        grid_spec=pltpu.PrefetchScalarGridSpec(
            num_scalar_prefetch=1, grid=(nrb, c // tc),
            in_specs=[pl.BlockSpec((None, tr, tc), lambda i, jj, place: (place[0], place[1] * nrb + i, jj)),
                      pl.BlockSpec((None, tr, tc), lambda i, jj, place: (place[0], i, jj)),
                      rspec(0), rspec(1), rspec(2)],
            out_specs=pl.BlockSpec((tr, tc), lambda i, jj, place: (place[1] * nrb + i, jj))),
        out_shape=jax.ShapeDtypeStruct((2 * hr, c), F32), name=name, compiler_params=_cparams(),
    )(place, grad, from_sibling, recv, recv, recv)


def _add_halves(mine, theirs, *, name, side=None):
    k, hr, c = mine.shape
    tr = min(hr, 256)
    tc = _col_tile(c)

    def body(a_ref, b_ref, o_ref):
        o_ref[...] = (a_ref[...] + b_ref[...]).astype(BF16)

    spec = pl.BlockSpec((None, tr, tc), lambda kk, i, j: (kk, i, j))
    (out,), side_outs = _call(body, grid=(k, hr // tr, c // tc), in_specs=[spec, spec], out_specs=[spec],
                              out_shape=[jax.ShapeDtypeStruct((k, hr, c), BF16)], args=(mine, theirs), name=name,
                              side=side)
    return out, side_outs


def _sum_halves(mine, theirs, recv, place, *, name):
    _, hr, c = mine.shape
    tr = min(hr, 256)
    tc = _col_tile(c)
    nrb = hr // tr

    def body(place_ref, a_ref, b_ref, r0_ref, r1_ref, r2_ref, o_ref):
        own = a_ref[...] + b_ref[...]
        o_ref[...] = ((own + r0_ref[...].astype(F32)) + r1_ref[...].astype(F32)) + r2_ref[...].astype(F32)

    def rspec(j):
        return pl.BlockSpec((None, tr, tc), lambda i, jj, place: (j, i, jj))

    own_spec = pl.BlockSpec((None, tr, tc), lambda i, jj, place: (place[0], i, jj))
    return pl.pallas_call(
        body,
        grid_spec=pltpu.PrefetchScalarGridSpec(
            num_scalar_prefetch=1, grid=(nrb, c // tc),
            in_specs=[own_spec, own_spec, rspec(0), rspec(1), rspec(2)],
            out_specs=pl.BlockSpec((tr, tc), lambda i, jj, place: (place[1] * nrb + i, jj))),
        out_shape=jax.ShapeDtypeStruct((2 * hr, c), F32), name=name, compiler_params=_cparams(),
    )(place, mine, theirs, recv, recv, recv)


def _sum_small(small, recv, place):
    _, sr, _ = small.shape

    def body(place_ref, own_ref, r_ref, o_ref):
        acc = own_ref[...]
        for r in range(1, 8):
            acc = acc + r_ref[r]
        o_ref[...] = acc

    return pl.pallas_call(
        body,
        grid_spec=pltpu.PrefetchScalarGridSpec(
            num_scalar_prefetch=1, grid=(1,),
            in_specs=[pl.BlockSpec((None, sr, 128), lambda i, place: (place[2], 0, 0)),
                      pl.BlockSpec((8, sr, 128), lambda i, place: (0, 0, 0))],
            out_specs=pl.BlockSpec((None, sr, 128), lambda i, place: (place[2], 0, 0))),
        out_shape=jax.ShapeDtypeStruct(small.shape, F32), name="sum_small", compiler_params=_cparams(),
    )(place, small, recv)


def _spread_side(vec):
    def copies(ins, outs, sems):
        x, y, c, _ = _place()
        return [pltpu.make_async_remote_copy(
            src_ref=ins[0], dst_ref=outs[0].at[r], send_sem=sems[0].at[r - 1], recv_sem=sems[1].at[r - 1],
            device_id=(x ^ fx, y ^ fy, c ^ fc), device_id_type=MESH)
            for r, (fx, fy, fc) in enumerate(_relations(), start=1)]

    def start(ins, outs, sems):
        for cp in copies(ins, outs, sems):
            cp.start()

    def finish(ins, outs, sems):
        for cp in copies(ins, outs, sems):
            cp.wait()

    return _Side([vec], [jax.ShapeDtypeStruct((8,) + vec.shape, vec.dtype)], [pltpu.SemaphoreType.DMA((7,))] * 2,
                 start, finish)


def _sum_in_device_order(own, spread, place):
    def body(place_ref, own_ref, r_ref, o_ref):
        me = place_ref[2]
        acc = jnp.zeros_like(own_ref[...])
        for d in range(8):
            slot = jnp.where(me == d, 1, me ^ d)
            acc = acc + jnp.where(me == d, own_ref[...], r_ref[slot])
        o_ref[...] = acc

    return pl.pallas_call(
        body,
        grid_spec=pltpu.PrefetchScalarGridSpec(
            num_scalar_prefetch=1, grid=(1,),
            in_specs=[pl.BlockSpec(own.shape, lambda i, place: (0, 0)),
                      pl.BlockSpec(spread.shape, lambda i, place: (0, 0, 0))],
            out_specs=pl.BlockSpec(own.shape, lambda i, place: (0, 0))),
        out_shape=jax.ShapeDtypeStruct(own.shape, F32), name="sum_in_device_order", compiler_params=_cparams(),
    )(place, own, spread)


def _adamw(w, g, m, v, *, name):
    r, c = w.shape
    tr = 256 if r % 256 == 0 else r
    tc = _col_tile(c)
    bc1 = 1.0 - ADAM_B1 ** ADAM_STEP
    bc2 = 1.0 - ADAM_B2 ** ADAM_STEP

    def body(w_ref, g_ref, m_ref, v_ref, d_ref, nm_ref, nv_ref, gout_ref):
        gv = g_ref[...]
        nm = ADAM_B1 * m_ref[...] + (1.0 - ADAM_B1) * gv
        nv = ADAM_B2 * v_ref[...] + (1.0 - ADAM_B2) * (gv * gv)
        d_ref[...] = -ADAM_LR * ((nm / bc1) / (jnp.sqrt(nv / bc2) + ADAM_EPS) + ADAM_WD * w_ref[...])
        nm_ref[...] = nm
        nv_ref[...] = nv
        gout_ref[...] = gv

    spec = pl.BlockSpec((tr, tc), lambda i, j: (i, j))
    outs, _ = _call(body, grid=(r // tr, c // tc), in_specs=[spec] * 4, out_specs=[spec] * 4,
                    out_shape=[jax.ShapeDtypeStruct((r, c), F32)] * 4, args=(w, g, m, v), name=name)
    return outs


SMALL_ORDER = ["a_ws", "a_bs", "a_norm_g", "a_ln_g", "a_ln_b", "kv_norm_g", "b_kv", "b_norm_g", "b_bq",
               "b_sinks", "final_norm_g"]
SHARDED_SMALL = {"a_norm_g", "a_ln_g", "a_ln_b"}
PACK_TILE = 8 * 128


def _rows128(a):
    flat = a.reshape(-1)
    return jnp.pad(flat, (0, (-flat.shape[0]) % PACK_TILE)).reshape(-1, 128)


def _pack_rows(parts, multiple):
    rows = [_rows128(p) for p in parts]
    total = sum(r.shape[0] for r in rows)
    pad = (-total) % multiple
    if pad:
        rows.append(jnp.zeros((pad, 128), rows[0].dtype))
    return jnp.concatenate(rows, axis=0)


def _unpack_rows(packed, shapes):
    out, row = [], 0
    for shp in shapes:
        size = math.prod(shp)
        nrow = -(-size // PACK_TILE) * 8
        out.append(packed[row:row + nrow].reshape(-1)[:size].reshape(shp))
        row += nrow
    return out


WEIGHTS = ["a_norm_g", "a_w_in", "a_ln_g", "a_ln_b", "a_ws", "a_bs", "a_w_out", "kv_norm_g", "w_kv", "b_kv",
           "b_norm_g", "b_w_in", "b_bq", "b_sinks", "b_w_out", "final_norm_g"]
BIG = ["a_w_in", "a_w_out", "w_kv", "b_w_in", "b_w_out"]


class _Reduction:
    def __init__(self, names, partials, core, place, small=None):
        self.names, self.partials, self.core, self.place, self.small = names, partials, core, place, small

    def exchange_side(self):
        return _exchange_side(self.partials)

    def took_exchange(self, from_sibling):
        self.from_sibling = from_sibling
        self.chip_sums = [_add_sibling(g, r, self.core, name="add_sibling_" + n)
                          for g, r, n in zip(self.partials, from_sibling, self.names)]

    def scatter_side(self):
        return _scatter_side(self.chip_sums, self.small)

    def took_scatter(self, arrived):
        big = arrived[:len(self.names)]
        self.halves = [_sum_chips(g, fs, r, self.place, name="sum_chips_" + n)
                       for g, fs, r, n in zip(self.partials, self.from_sibling, big, self.names)]
        self.small_mine = _sum_small(self.small, arrived[-1], self.place) if self.small is not None else None

    def share_side(self):
        return _share_side(self.halves, self.small_mine)

    def took_share(self, shared):
        self.grads = dict(zip(self.names, shared[:len(self.names)]))
        self.small_full = shared[-1] if self.small is not None else None


def _step(x, loss_target, p, m, v):
    xi, yi, ci = lax.axis_index("x"), lax.axis_index("y"), lax.axis_index("c")
    chip = 2 * xi + yi
    device = 4 * xi + 2 * yi + ci
    core = jnp.reshape(ci, (1,)).astype(jnp.int32)
    place = jnp.stack([chip, ci, device]).astype(jnp.int32)
    x, tgt = x[0], loss_target[0]
    s = x.shape[0]
    cos, sin = _rope_tables(s)

    shard2d = {n: p[n].reshape(p[n].shape[-2:]) for n in BIG}
    shard_bf = {n: shard2d[n].astype(BF16) for n in BIG}
    ws = p["a_ws"][0]
    ws_t = jnp.swapaxes(ws, 1, 2)
    bs_t = p["a_bs"][0].T
    kv_norm_g, b_kv = p["kv_norm_g"].reshape(1, -1), p["b_kv"].reshape(1, -1)
    final_norm_g = p["final_norm_g"].reshape(1, -1)

    vec_shapes = [p[n].shape for n in ("a_norm_g", "a_ln_g", "a_ln_b")]
    vec_pack = _pack_rows([p["a_norm_g"], p["a_ln_g"], p["a_ln_b"]], 16)
    (vec_all,) = _comm_call(_gather_side([vec_pack]), "gather_vectors")
    vecs = [_unpack_rows(vec_all[k], vec_shapes) for k in range(N_CHIPS)]
    a_norm_g, a_ln_g, a_ln_b = (jnp.concatenate([vk[t] for vk in vecs], axis=-1) for t in range(3))

    (n_a,) = _rms_fwd(x, [a_norm_g], name="rms_a")
    order = jnp.stack([chip, 2 * (1 - xi) + yi, 2 * xi + (1 - yi), 2 * (1 - xi) + (1 - yi)]).astype(jnp.int32)
    z, a_w_in = _mm_gathering(n_a, shard_bf["a_w_in"], order, name="mm_a_in")
    y, (a_w_out,) = _gate_fwd(z, a_ln_g, a_ln_b, ws, bs_t, side=_gather_side([shard_bf["a_w_out"]]))
    a_w_out = a_w_out.reshape(A_WIDTH, D_MODEL)
    (h1, n_kv, n_b), (w_kv, b_w_in) = _mm_residual_norms(
        y, a_w_out, x, [kv_norm_g, p["b_norm_g"]], name="mm_a_out",
        side=_gather_side([shard_bf["w_kv"], shard_bf["b_w_in"]]))
    w_kv = w_kv.reshape(D_MODEL, 2 * KV_WIDTH)
    kr, vv = _kv_rope(n_kv, w_kv, b_kv, cos, sin)
    zb = _mm_nn(n_b, b_w_in, name="mm_b_in", tn=512, tm=1024, out_dtype=BF16)
    yb, (b_w_out,) = _attn_fwd(zb, kr, vv, cos, sin, p["b_bq"], p["b_sinks"], side=_gather_side([shard_bf["b_w_out"]]))
    b_w_out = b_w_out.reshape(B_WIDTH, D_MODEL)
    loss_blk, dh2, dh2b, d_final_g = _mm_residual_loss(yb, b_w_out, h1, tgt, final_norm_g, name="mm_b_out")

    d_b_w_out = _mm_tn(yb, dh2b, name="mm_d_b_w_out", tm=B_WIDTH, tn=D_MODEL)
    red_bo = _Reduction(["b_w_out"], [d_b_w_out.reshape(N_CHIPS, B_WIDTH // N_CHIPS, D_MODEL)], core, place)
    dyb, got = _mm_nt(dh2b, b_w_out, name="mm_dyb", out_dtype=BF16, side=red_bo.exchange_side())
    red_bo.took_exchange(got)
    (dzb, dk_rot, dv, d_bq, d_sinks), got = _attn_bwd(zb, dyb, kr, vv, cos, sin, p["b_bq"], p["b_sinks"],
                                                      side=red_bo.scatter_side())
    red_bo.took_scatter(got)
    dkv, d_b_kv = _kv_rope_bwd(dk_rot, dv, cos, sin)
    d_b_w_in = _mm_tn(n_b, dzb, name="mm_d_b_w_in", tm=D_MODEL, tn=512, shards=N_CHIPS)
    d_w_kv, got = _mm_tn(n_kv, dkv, name="mm_d_w_kv", tm=D_MODEL, tn=2 * KV_WIDTH, side=red_bo.share_side())
    red_bo.took_share(got)
    red_bi = _Reduction(["b_w_in", "w_kv"], [d_b_w_in, d_w_kv.reshape(N_CHIPS, D_MODEL // N_CHIPS, 2 * KV_WIDTH)],
                        core, place)
    (dh1, dh1b, d_kv_g, d_b_g), got = _mm_nt_rms_bwd(
        [(dkv, w_kv, kv_norm_g), (dzb, b_w_in, p["b_norm_g"])], h1, dh2, name="mm_dn_b", tm=512,
        side=red_bi.exchange_side())
    red_bi.took_exchange(got)

    d_a_w_out = _mm_tn(y, dh1b, name="mm_d_a_w_out", tm=1024, tn=D_MODEL)
    red_ao = _Reduction(["a_w_out"], [d_a_w_out.reshape(N_CHIPS, A_WIDTH // N_CHIPS, D_MODEL)], core, place)
    dy, got = _mm_nt(dh1b, a_w_out, name="mm_dy", tn=1024, out_dtype=BF16, side=red_ao.exchange_side())
    red_ao.took_exchange(got)
    sides = [red_bi.scatter_side(), red_ao.scatter_side()]
    (dz, d_ln_g, d_ln_b, d_ws, d_bs_t), got = _gate_bwd(z, dy, a_ln_g, a_ln_b, ws, ws_t, bs_t, side=_join(sides))
    got = _split(got, sides)
    red_bi.took_scatter(got[0])
    red_ao.took_scatter(got[1])
    small = {
        "a_ws": d_ws, "a_bs": d_bs_t.T, "a_ln_g": d_ln_g, "a_ln_b": d_ln_b,
        "kv_norm_g": d_kv_g, "b_kv": d_b_kv, "b_norm_g": d_b_g, "b_bq": d_bq,
        "b_sinks": d_sinks[0:1, :N_Q_HEADS], "final_norm_g": d_final_g,
    }
    packed = [n for n in SMALL_ORDER if n != "a_norm_g"]
    small_shapes = [small[n].shape for n in packed] + [(1, 1)]
    small_pack = _pack_rows([small[n] for n in packed] + [loss_blk[0:1, 0:1]], 64)
    seg = small_pack.shape[0] // 8
    small_pack = small_pack.reshape(8, seg, 128)
    sides = [red_bi.share_side(), red_ao.share_side(), _small_scatter_side(small_pack)]
    (d_a_w_in, from_sibling), got = _mm_tn_exchanging(n_a, dz, name="mm_d_a_w_in", shards=N_CHIPS, side=_join(sides))
    got = _split(got, sides)
    red_bi.took_share(got[0])
    red_ao.took_share(got[1])
    small_mine = _sum_small(small_pack, got[2][0], place)

    chip_sum, (small_all,) = _add_halves(d_a_w_in, from_sibling, name="add_sibling_a_w_in",
                                         side=_small_share_side(small_mine))
    (dx, _, d_a_g), (arrived,) = _mm_nt_rms_bwd([(dz, a_w_in, a_norm_g)], x, dh1, name="mm_dn_a", tm=256,
                                                side=_scatter_side([chip_sum]))
    half_ai = _sum_halves(d_a_w_in, from_sibling, arrived, place, name="sum_chips_a_w_in")
    d_a_g = _rows128(d_a_g)
    sides = [_share_side([half_ai]), _spread_side(d_a_g)]
    got = _split(_comm_call(_join(sides), "share_last"), sides)
    grad_ai = got[0][0]
    small_full = dict(zip(packed + ["loss"], _unpack_rows(small_all.reshape(8 * seg, 128), small_shapes)))
    small_full["a_norm_g"] = _sum_in_device_order(d_a_g, got[1][0], place).reshape(1, -1)
    loss = small_full["loss"].reshape(())

    grad_big = {**red_bo.grads, **red_bi.grads, **red_ao.grads, "a_w_in": grad_ai}
    grads = {}
    for n in SMALL_ORDER:
        gfull = small_full[n]
        if n in SHARDED_SMALL:
            width = p[n].shape[-1]
            gfull = lax.dynamic_slice_in_dim(gfull, chip * width, width, axis=-1)
        grads[n] = gfull.reshape(p[n].shape)

    delta, new_m, new_v = {}, {}, {}
    for n in BIG:
        d, nm, nv, g = _adamw(shard2d[n], grad_big[n], m[n].reshape(shard2d[n].shape),
                              v[n].reshape(shard2d[n].shape), name="adamw_" + n)
        delta[n], new_m[n], new_v[n] = d.reshape(p[n].shape), nm.reshape(p[n].shape), nv.reshape(p[n].shape)
        grads[n] = g.reshape(p[n].shape)
    shapes = [p[n].shape for n in SMALL_ORDER]
    packs = [_pack_rows([src[n] for n in SMALL_ORDER], 8) for src in (p, grads, m, v)]
    outs = _adamw(*packs, name="adamw_small")[:3]
    for res, packed in zip((delta, new_m, new_v), outs):
        for n, val in zip(SMALL_ORDER, _unpack_rows(packed, shapes)):
            res[n] = val

    return (loss, dx[None], *[grads[n] for n in WEIGHTS], *[delta[n] for n in WEIGHTS],
            *[new_m[n] for n in WEIGHTS], *[new_v[n] for n in WEIGHTS])


def kernel(x, a_norm_g, a_w_in, a_ln_g, a_ln_b, a_ws, a_bs, a_w_out, kv_norm_g, w_kv, b_kv, b_norm_g, b_w_in, b_bq, b_sinks, b_w_out, final_norm_g, loss_target, m_a_norm_g, m_a_w_in, m_a_ln_g, m_a_ln_b, m_a_ws, m_a_bs, m_a_w_out, m_kv_norm_g, m_w_kv, m_b_kv, m_b_norm_g, m_b_w_in, m_b_bq, m_b_sinks, m_b_w_out, m_final_norm_g, v_a_norm_g, v_a_w_in, v_a_ln_g, v_a_ln_b, v_a_ws, v_a_bs, v_a_w_out, v_kv_norm_g, v_w_kv, v_b_kv, v_b_norm_g, v_b_w_in, v_b_bq, v_b_sinks, v_b_w_out, v_final_norm_g):
    p = dict(a_norm_g=a_norm_g, a_w_in=a_w_in, a_ln_g=a_ln_g, a_ln_b=a_ln_b, a_ws=a_ws, a_bs=a_bs, a_w_out=a_w_out,
             kv_norm_g=kv_norm_g, w_kv=w_kv, b_kv=b_kv, b_norm_g=b_norm_g, b_w_in=b_w_in, b_bq=b_bq, b_sinks=b_sinks,
             b_w_out=b_w_out, final_norm_g=final_norm_g)
    m = dict(a_norm_g=m_a_norm_g, a_w_in=m_a_w_in, a_ln_g=m_a_ln_g, a_ln_b=m_a_ln_b, a_ws=m_a_ws, a_bs=m_a_bs,
             a_w_out=m_a_w_out, kv_norm_g=m_kv_norm_g, w_kv=m_w_kv, b_kv=m_b_kv, b_norm_g=m_b_norm_g, b_w_in=m_b_w_in,
             b_bq=m_b_bq, b_sinks=m_b_sinks, b_w_out=m_b_w_out, final_norm_g=m_final_norm_g)
    v = dict(a_norm_g=v_a_norm_g, a_w_in=v_a_w_in, a_ln_g=v_a_ln_g, a_ln_b=v_a_ln_b, a_ws=v_a_ws, a_bs=v_a_bs,
             a_w_out=v_a_w_out, kv_norm_g=v_kv_norm_g, w_kv=v_w_kv, b_kv=v_b_kv, b_norm_g=v_b_norm_g, b_w_in=v_b_w_in,
             b_bq=v_b_bq, b_sinks=v_b_sinks, b_w_out=v_b_w_out, final_norm_g=v_final_norm_g)
    return _step(x, loss_target, p, m, v)
```

```python
import functools
import math

import jax
import jax.numpy as jnp
from jax import lax
from jax.experimental import pallas as pl
from jax.experimental.pallas import tpu as pltpu

F32 = jnp.float32
BF16 = jnp.bfloat16

D_MODEL = 1024
CHUNK = 128
A_WIDTH = 2048
A_GROUPS = 16
HEAD_DIM = 64
N_Q_HEADS = 16
N_KV_HEADS = 2
Q_PER_KV = 8
B_WIDTH = 1024
KV_WIDTH = 128
ROPE_THETA = 10000.0
EPS = 1e-5
N_CHIPS = 4

ADAM_LR = 0.001
ADAM_B1 = 0.9
ADAM_B2 = 0.999
ADAM_EPS = 1e-08
ADAM_WD = 0.01
ADAM_STEP = 10

VMEM_LIMIT = 48 * 1024 * 1024
MESH = pl.DeviceIdType.MESH
NEG_BIG = -1e30
HBM = pl.BlockSpec(memory_space=pl.ANY)

NN = (((1,), (0,)), ((), ()))
NT = (((1,), (1,)), ((), ()))
TN = (((0,), (0,)), ((), ()))


def _cparams(**kw):
    return pltpu.CompilerParams(vmem_limit_bytes=VMEM_LIMIT, **kw)


class _Side:
    def __init__(self, ins, out_shapes, sems, start, finish, aliases=None, passing=None):
        self.ins, self.out_shapes, self.sems = list(ins), list(out_shapes), list(sems)
        self.start, self.finish = start, finish
        self.passing = passing or (lambda ins, outs, sems: None)
        self.aliases = dict(aliases or {})


def _join(sides):
    sides = [s for s in sides if s is not None]
    if not sides:
        return None
    offs, i, o, m = [], 0, 0, 0
    for s in sides:
        offs.append((i, o, m))
        i, o, m = i + len(s.ins), o + len(s.out_shapes), m + len(s.sems)

    def run(which):
        def go(ins, outs, sems):
            for s, (a, b, c) in zip(sides, offs):
                getattr(s, which)(ins[a:a + len(s.ins)], outs[b:b + len(s.out_shapes)], sems[c:c + len(s.sems)])
        return go

    aliases = {}
    for s, (a, b, _) in zip(sides, offs):
        aliases.update({a + k: b + v for k, v in s.aliases.items()})
    return _Side([x for s in sides for x in s.ins], [x for s in sides for x in s.out_shapes],
                 [x for s in sides for x in s.sems], run("start"), run("finish"), aliases, run("passing"))


def _split(side_outs, sides):
    out, pos = [], 0
    for s in sides:
        out.append(list(side_outs[pos:pos + len(s.out_shapes)]))
        pos += len(s.out_shapes)
    return out


def _call(body, *, grid, in_specs, out_specs, out_shape, args, name, scratch=(), side=None):
    in_specs, out_specs, out_shape, scratch = list(in_specs), list(out_specs), list(out_shape), list(scratch)
    if side is None:
        res = pl.pallas_call(body, grid=grid, in_specs=in_specs, out_specs=out_specs, out_shape=out_shape,
                             scratch_shapes=scratch, name=name, compiler_params=_cparams())(*args)
        return list(res), []
    n_in, n_out, n_sc = len(in_specs), len(out_specs), len(scratch)
    s_in, s_out = len(side.ins), len(side.out_shapes)

    def wrapped(*refs):
        ins, refs = refs[:n_in], refs[n_in:]
        side_ins, refs = refs[:s_in], refs[s_in:]
        outs, refs = refs[:n_out], refs[n_out:]
        side_outs, refs = refs[:s_out], refs[s_out:]
        scr, side_sems = refs[:n_sc], refs[n_sc:]
        step = 0
        for a, g in enumerate(grid):
            step = step * g + pl.program_id(a)
        steps = math.prod(grid)

        @pl.when(step == 0)
        def _():
            side.start(side_ins, side_outs, side_sems)

        body(*ins, *outs, *scr)

        @pl.when(step == (3 * (steps - 1)) // 4)
        def _():
            side.passing(side_ins, side_outs, side_sems)

        @pl.when(step == steps - 1)
        def _():
            side.finish(side_ins, side_outs, side_sems)

    res = pl.pallas_call(
        wrapped, grid=grid, in_specs=in_specs + [HBM] * s_in, out_specs=out_specs + [HBM] * s_out,
        out_shape=out_shape + side.out_shapes, scratch_shapes=scratch + side.sems,
        input_output_aliases={n_in + k: n_out + v for k, v in side.aliases.items()},
        name=name, compiler_params=_cparams(),
    )(*args, *side.ins)
    return list(res[:n_out]), list(res[n_out:])


def _comm_call(side, name):
    s_in, s_out = len(side.ins), len(side.out_shapes)

    def body(*refs):
        ins, outs, sems = refs[:s_in], refs[s_in:s_in + s_out], refs[s_in + s_out:]
        side.start(ins, outs, sems)
        side.passing(ins, outs, sems)
        side.finish(ins, outs, sems)

    return list(pl.pallas_call(
        body, in_specs=[HBM] * s_in, out_specs=[HBM] * s_out, out_shape=side.out_shapes, scratch_shapes=side.sems,
        input_output_aliases=side.aliases, name=name,
    )(*side.ins))


def _matmul(a, b, *, dims, grid, a_spec, b_spec, o_spec, out_shape, name, acc_axis=None,
            residual=None, r_spec=None, side=None):
    has_res = residual is not None

    def body(*refs):
        if has_res:
            a_ref, b_ref, r_ref, o_ref = refs
        else:
            a_ref, b_ref, o_ref = refs
        part = lax.dot_general(a_ref[...], b_ref[...], dims, preferred_element_type=F32)
        if acc_axis is None:
            if has_res:
                part = part + r_ref[...]
            o_ref[...] = part.astype(o_ref.dtype)
        else:
            k = pl.program_id(acc_axis)

            @pl.when(k == 0)
            def _():
                o_ref[...] = part

            @pl.when(k > 0)
            def _():
                o_ref[...] += part

    in_specs = [a_spec, b_spec] + ([r_spec] if has_res else [])
    args = (a, b) + ((residual,) if has_res else ())
    (out,), side_outs = _call(body, grid=grid, in_specs=in_specs, out_specs=[o_spec], out_shape=[out_shape],
                              args=args, name=name, side=side)
    return (out, side_outs) if side is not None else out


def _row_tile(s, want):
    return min(s, want)


def _mm_nn(a, b, *, name, tn, out_dtype=F32, residual=None, tm=512, side=None):
    s, k = a.shape
    tm = _row_tile(s, tm)
    if b.ndim == 3:
        nsh, _, nc = b.shape
        npb = nc // tn
        n = nsh * nc
        b_spec = pl.BlockSpec((None, k, tn), lambda i, j: (j // npb, 0, j % npb))
    else:
        n = b.shape[1]
        b_spec = pl.BlockSpec((k, tn), lambda i, j: (0, j))
    return _matmul(
        a, b, dims=NN, grid=(s // tm, n // tn),
        a_spec=pl.BlockSpec((tm, k), lambda i, j: (i, 0)), b_spec=b_spec,
        o_spec=pl.BlockSpec((tm, tn), lambda i, j: (i, j)),
        out_shape=jax.ShapeDtypeStruct((s, n), out_dtype), name=name, side=side,
        residual=residual, r_spec=pl.BlockSpec((tm, tn), lambda i, j: (i, j)) if residual is not None else None)


def _mm_nt(a, b, *, name, tn=None, tm=512, out_dtype=F32, side=None):
    s, k = a.shape
    tm = _row_tile(s, tm)
    n = b.shape[0]
    tn = n if tn is None else tn
    return _matmul(
        a, b, dims=NT, grid=(s // tm, n // tn),
        a_spec=pl.BlockSpec((tm, k), lambda i, j: (i, 0)),
        b_spec=pl.BlockSpec((tn, k), lambda i, j: (j, 0)),
        o_spec=pl.BlockSpec((tm, tn), lambda i, j: (i, j)),
        out_shape=jax.ShapeDtypeStruct((s, n), out_dtype), name=name, side=side)


def _mm_tn(a, b, *, name, tm, tn, tk=2048, shards=None, side=None):
    s, m = a.shape
    n = b.shape[1]
    tk = _row_tile(s, tk)
    if shards is None:
        o_spec = pl.BlockSpec((tm, tn), lambda i, j, kk: (i, j))
        out_shape = jax.ShapeDtypeStruct((m, n), F32)
    else:
        assert tm == m
        nc = n // shards
        npb = nc // tn
        o_spec = pl.BlockSpec((None, m, tn), lambda i, j, kk: (j // npb, 0, j % npb))
        out_shape = jax.ShapeDtypeStruct((shards, m, nc), F32)
    return _matmul(
        a, b, dims=TN, grid=(m // tm, n // tn, s // tk), acc_axis=2,
        a_spec=pl.BlockSpec((tk, tm), lambda i, j, kk: (kk, i)),
        b_spec=pl.BlockSpec((tk, tn), lambda i, j, kk: (kk, j)),
        o_spec=o_spec, out_shape=out_shape, name=name, side=side)


def _rstd(x):
    return lax.rsqrt(jnp.mean(x * x, axis=-1, keepdims=True) + EPS)


def _rms_fwd(x, gains, *, name, tr=1024):
    s, d = x.shape
    tr = _row_tile(s, tr)
    ng = len(gains)

    def body(*refs):
        xv = refs[0][...]
        xh = xv * _rstd(xv)
        for t in range(ng):
            refs[1 + ng + t][...] = (xh * refs[1 + t][...]).astype(BF16)

    row = pl.BlockSpec((tr, d), lambda i: (i, 0))
    vec = pl.BlockSpec((1, d), lambda i: (0, 0))
    outs, _ = _call(body, grid=(s // tr,), in_specs=[row] + [vec] * ng, out_specs=[row] * ng,
                    out_shape=[jax.ShapeDtypeStruct((s, d), BF16)] * ng, args=(x, *gains), name=name)
    return outs


def _accumulate(i, ref, value):
    @pl.when(i == 0)
    def _():
        ref[...] = value

    @pl.when(i > 0)
    def _():
        ref[...] += value


def _mm_residual_norms(y, w, res, gains, *, name, tm=512, side=None):
    s, k = y.shape
    d = w.shape[1]
    tm = _row_tile(s, tm)
    ng = len(gains)

    def body(y_ref, w_ref, r_ref, *rest):
        g_refs, h_ref, n_refs = rest[:ng], rest[ng], rest[ng + 1:]
        h = r_ref[...] + jnp.dot(y_ref[...], w_ref[...], preferred_element_type=F32)
        h_ref[...] = h
        xh = h * _rstd(h)
        for t in range(ng):
            n_refs[t][...] = (xh * g_refs[t][...]).astype(BF16)

    row = pl.BlockSpec((tm, d), lambda i: (i, 0))
    vec = pl.BlockSpec((1, d), lambda i: (0, 0))
    return _call(
        body, grid=(s // tm,),
        in_specs=[pl.BlockSpec((tm, k), lambda i: (i, 0)), pl.BlockSpec((k, d), lambda i: (0, 0)), row] + [vec] * ng,
        out_specs=[row] * (1 + ng),
        out_shape=[jax.ShapeDtypeStruct((s, d), F32)] + [jax.ShapeDtypeStruct((s, d), BF16)] * ng,
        args=(y, w, res, *gains), name=name, side=side)


def _mm_residual_loss(y, w, res, tgt, gain, *, name, tm=512):
    s, k = y.shape
    d = w.shape[1]
    tm = _row_tile(s, tm)

    def body(y_ref, w_ref, r_ref, t_ref, g_ref, loss_ref, dh_ref, dhb_ref, dg_ref):
        i = pl.program_id(0)
        hv = r_ref[...] + jnp.dot(y_ref[...], w_ref[...], preferred_element_type=F32)
        g = g_ref[...]
        r = _rstd(hv)
        xh = hv * r
        diff = xh * g - t_ref[...]
        part = 0.5 / d * jnp.sum(jnp.sum(diff * diff, axis=-1, keepdims=True), axis=0, keepdims=True)
        dout = diff * (1.0 / d)
        a = dout * g
        dh = r * (a - xh * jnp.mean(a * xh, axis=-1, keepdims=True))
        dh_ref[...] = dh
        dhb_ref[...] = dh.astype(BF16)
        _accumulate(i, dg_ref, jnp.sum(dout * xh, axis=0, keepdims=True))
        _accumulate(i, loss_ref, jnp.broadcast_to(part, (8, 128)))

    row = pl.BlockSpec((tm, d), lambda i: (i, 0))
    vec = pl.BlockSpec((1, d), lambda i: (0, 0))
    outs, _ = _call(
        body, grid=(s // tm,),
        in_specs=[pl.BlockSpec((tm, k), lambda i: (i, 0)), pl.BlockSpec((k, d), lambda i: (0, 0)), row, row, vec],
        out_specs=[pl.BlockSpec((8, 128), lambda i: (0, 0)), row, row, vec],
        out_shape=[jax.ShapeDtypeStruct((8, 128), F32), jax.ShapeDtypeStruct((s, d), F32),
                   jax.ShapeDtypeStruct((s, d), BF16), jax.ShapeDtypeStruct((1, d), F32)],
        args=(y, w, res, tgt, gain), name=name)
    return outs


def _mm_nt_rms_bwd(terms, x, dres, *, name, tm, side=None):
    s, d = x.shape
    tm = _row_tile(s, tm)
    nt = len(terms)

    def body(*refs):
        a_refs, b_refs, g_refs = refs[0:3 * nt:3], refs[1:3 * nt:3], refs[2:3 * nt:3]
        x_ref, dres_ref = refs[3 * nt], refs[3 * nt + 1]
        dx_ref, dxb_ref = refs[3 * nt + 2], refs[3 * nt + 3]
        dg_refs = refs[3 * nt + 4:]
        i = pl.program_id(0)
        xv = x_ref[...]
        r = _rstd(xv)
        xh = xv * r
        acc = jnp.zeros_like(xv)
        for t in range(nt):
            b_ref = b_refs[t]
            if len(b_ref.shape) == 3:
                kc = b_ref.shape[2]
                dn = None
                for sh in range(b_ref.shape[0]):
                    part = lax.dot_general(a_refs[t][:, sh * kc:(sh + 1) * kc], b_ref[sh], NT, preferred_element_type=F32)
                    dn = part if dn is None else dn + part
            else:
                dn = lax.dot_general(a_refs[t][...], b_ref[...], NT, preferred_element_type=F32)
            acc = acc + dn * g_refs[t][...]
            _accumulate(i, dg_refs[t], jnp.sum(dn * xh, axis=0, keepdims=True))
        dx = dres_ref[...] + r * (acc - xh * jnp.mean(acc * xh, axis=-1, keepdims=True))
        dx_ref[...] = dx
        dxb_ref[...] = dx.astype(BF16)

    row = pl.BlockSpec((tm, d), lambda i: (i, 0))
    vec = pl.BlockSpec((1, d), lambda i: (0, 0))
    in_specs, args = [], []
    for a, b, g in terms:
        in_specs += [pl.BlockSpec((tm, a.shape[1]), lambda i: (i, 0)),
                     pl.BlockSpec(b.shape, (lambda i: (0, 0, 0)) if b.ndim == 3 else (lambda i: (0, 0))), vec]
        args += [a, b, g]
    return _call(
        body, grid=(s // tm,), in_specs=in_specs + [row, row], out_specs=[row, row] + [vec] * nt,
        out_shape=[jax.ShapeDtypeStruct((s, d), F32), jax.ShapeDtypeStruct((s, d), BF16)]
        + [jax.ShapeDtypeStruct((1, d), F32)] * nt,
        args=(*args, x, dres), name=name, side=side)


def _causal_mask(transposed=False):
    row = lax.broadcasted_iota(jnp.int32, (CHUNK, CHUNK), 0)
    col = lax.broadcasted_iota(jnp.int32, (CHUNK, CHUNK), 1)
    return col >= row if transposed else col <= row


def _silu_parts(g):
    sg = jax.nn.sigmoid(g)
    return g * sg, sg * (1.0 + g * (1.0 - sg))


def _gate_fwd(z, ln_g, ln_b, ws, bs_t, *, tr=512, side=None):
    s = z.shape[0]
    tr = _row_tile(s, tr)
    w = A_WIDTH

    def body(u_ref, v_ref, g_ref, lg_ref, lb_ref, ws_ref, bst_ref, y_ref):
        v = v_ref[...].astype(F32)
        mu = jnp.mean(v, axis=-1, keepdims=True)
        xc = v - mu
        rs = lax.rsqrt(jnp.mean(xc * xc, axis=-1, keepdims=True) + EPS)
        vln = (xc * rs * lg_ref[...] + lb_ref[...]).astype(BF16)
        mask = _causal_mask()
        for grp in range(A_GROUPS):
            cols = slice(grp * CHUNK, (grp + 1) * CHUNK)
            wsm = jnp.where(mask, ws_ref[grp], 0.0).astype(BF16)
            bcol = bst_ref[:, grp:grp + 1]
            for ci in range(tr // CHUNK):
                rows = slice(ci * CHUNK, (ci + 1) * CHUNK)
                sv = jnp.dot(wsm, vln[rows, cols], preferred_element_type=F32) + bcol
                gv = g_ref[rows, cols].astype(F32)
                y_ref[rows, cols] = (u_ref[rows, cols].astype(F32) * sv * (gv * jax.nn.sigmoid(gv))).astype(BF16)

    vec = pl.BlockSpec((1, w), lambda i: (0, 0))
    (y,), side_outs = _call(
        body, grid=(s // tr,),
        in_specs=[pl.BlockSpec((tr, w), lambda i: (i, 0)), pl.BlockSpec((tr, w), lambda i: (i, 1)),
                  pl.BlockSpec((tr, w), lambda i: (i, 2)), vec, vec,
                  pl.BlockSpec((A_GROUPS, CHUNK, CHUNK), lambda i: (0, 0, 0)),
                  pl.BlockSpec((CHUNK, A_GROUPS), lambda i: (0, 0))],
        out_specs=[pl.BlockSpec((tr, w), lambda i: (i, 0))],
        out_shape=[jax.ShapeDtypeStruct((s, w), BF16)], args=(z, z, z, ln_g, ln_b, ws, bs_t), name="gate_fwd",
        side=side)
    return y, side_outs


def _gate_bwd(z, dy, ln_g, ln_b, ws, ws_t, bs_t, *, tr=256, side=None):
    s = z.shape[0]
    tr = _row_tile(s, tr)
    w = A_WIDTH
    nsteps = s // tr

    def body(u_ref, v_ref, g_ref, dy_ref, lg_ref, lb_ref, ws_ref, wst_ref, bst_ref,
             dz_ref, dlg_ref, dlb_ref, dws_ref, dbst_ref, dvln_sc, dsv_sc):
        i = pl.program_id(0)

        @pl.when(i == 0)
        def _():
            dws_ref[...] = jnp.zeros_like(dws_ref)
            dsv_sc[...] = jnp.zeros_like(dsv_sc)

        v = v_ref[...].astype(F32)
        mu = jnp.mean(v, axis=-1, keepdims=True)
        xc = v - mu
        rs = lax.rsqrt(jnp.mean(xc * xc, axis=-1, keepdims=True) + EPS)
        xh = xc * rs
        lg = lg_ref[...]
        vln = (xh * lg + lb_ref[...]).astype(BF16)
        mask = _causal_mask()
        mask_t = _causal_mask(transposed=True)
        for grp in range(A_GROUPS):
            cols = slice(grp * CHUNK, (grp + 1) * CHUNK)
            wsm = jnp.where(mask, ws_ref[grp], 0.0).astype(BF16)
            wsm_t = jnp.where(mask_t, wst_ref[grp], 0.0).astype(BF16)
            bcol = bst_ref[:, grp:grp + 1]
            for ci in range(tr // CHUNK):
                rows = slice(ci * CHUNK, (ci + 1) * CHUNK)
                vb = vln[rows, cols]
                sv = jnp.dot(wsm, vb, preferred_element_type=F32) + bcol
                uv = u_ref[rows, cols].astype(F32)
                silu, dsilu = _silu_parts(g_ref[rows, cols].astype(F32))
                dyv = dy_ref[rows, cols].astype(F32)
                dyu = dyv * uv
                dz_ref[rows, cols] = (dyv * sv * silu).astype(BF16)
                dz_ref[rows, 2 * w + grp * CHUNK:2 * w + (grp + 1) * CHUNK] = (dyu * sv * dsilu).astype(BF16)
                dsv = dyu * silu
                dsvb = dsv.astype(BF16)
                dvln_sc[rows, cols] = jnp.dot(wsm_t, dsvb, preferred_element_type=F32)
                dws_ref[grp] += lax.dot_general(dsvb, vb, NT, preferred_element_type=F32)
                dsv_sc[grp] += dsv
        dvln = dvln_sc[...]
        dlg_t = jnp.sum(dvln * xh, axis=0, keepdims=True)
        dlb_t = jnp.sum(dvln, axis=0, keepdims=True)
        a = dvln * lg
        dv = rs * (a - jnp.mean(a, axis=-1, keepdims=True) - xh * jnp.mean(a * xh, axis=-1, keepdims=True))
        dz_ref[:, w:2 * w] = dv.astype(BF16)

        @pl.when(i == 0)
        def _():
            dlg_ref[...] = dlg_t
            dlb_ref[...] = dlb_t

        @pl.when(i > 0)
        def _():
            dlg_ref[...] += dlg_t
            dlb_ref[...] += dlb_t

        @pl.when(i == nsteps - 1)
        def _():
            for grp in range(A_GROUPS):
                dws_ref[grp] = jnp.where(mask, dws_ref[grp], 0.0)
                dbst_ref[:, grp:grp + 1] = jnp.sum(dsv_sc[grp], axis=-1, keepdims=True)

    vec = pl.BlockSpec((1, w), lambda i: (0, 0))
    wsspec = pl.BlockSpec((A_GROUPS, CHUNK, CHUNK), lambda i: (0, 0, 0))
    bsspec = pl.BlockSpec((CHUNK, A_GROUPS), lambda i: (0, 0))
    return _call(
        body, grid=(nsteps,),
        in_specs=[pl.BlockSpec((tr, w), lambda i: (i, 0)), pl.BlockSpec((tr, w), lambda i: (i, 1)),
                  pl.BlockSpec((tr, w), lambda i: (i, 2)), pl.BlockSpec((tr, w), lambda i: (i, 0)),
                  vec, vec, wsspec, wsspec, bsspec],
        out_specs=[pl.BlockSpec((tr, 3 * w), lambda i: (i, 0)), vec, vec, wsspec, bsspec],
        out_shape=[jax.ShapeDtypeStruct((s, 3 * w), BF16), jax.ShapeDtypeStruct((1, w), F32),
                   jax.ShapeDtypeStruct((1, w), F32), jax.ShapeDtypeStruct((A_GROUPS, CHUNK, CHUNK), F32),
                   jax.ShapeDtypeStruct((CHUNK, A_GROUPS), F32)],
        scratch=[pltpu.VMEM((tr, w), F32), pltpu.VMEM((A_GROUPS, CHUNK, CHUNK), F32)],
        args=(z, z, z, dy, ln_g, ln_b, ws, ws_t, bs_t), name="gate_bwd", side=side)


HEADS_PER_BLOCK = 128 // HEAD_DIM
BLOCKS_PER_KV = Q_PER_KV // HEADS_PER_BLOCK
SCALE = HEAD_DIM ** -0.5
LOG2_E = math.log2(math.e)


def _rope_tables(s):
    lane = jnp.arange(128)
    inv_freq = ROPE_THETA ** (-(2 * (lane % (HEAD_DIM // 2))).astype(F32) / HEAD_DIM)
    sign = jnp.where(lane % HEAD_DIM < HEAD_DIM // 2, -1.0, 1.0).astype(F32)
    ang = jnp.arange(s, dtype=F32)[:, None] * inv_freq[None, :]
    return jnp.cos(ang), jnp.sin(ang) * sign[None, :]


def _swap_halves(x):
    n = x.shape[-1]
    lane = lax.broadcasted_iota(jnp.int32, x.shape, x.ndim - 1)
    first = (lane % HEAD_DIM) < (HEAD_DIM // 2)
    return jnp.where(first, pltpu.roll(x, n - HEAD_DIM // 2, x.ndim - 1), pltpu.roll(x, HEAD_DIM // 2, x.ndim - 1))


def _left_half(rows):
    return lax.broadcasted_iota(jnp.int32, (rows, 128), 1) < HEAD_DIM


def _dup_heads(x):
    left = _left_half(x.shape[0])
    swapped = pltpu.roll(x, HEAD_DIM, 1)
    return jnp.concatenate([jnp.where(left, x, swapped), jnp.where(left, swapped, x)], axis=-1)


def _fold_heads(a):
    b0, b1 = a[:, :128], a[:, 128:]
    f0 = b0 + pltpu.roll(b0, HEAD_DIM, 1)
    f1 = b1 + pltpu.roll(b1, HEAD_DIM, 1)
    return jnp.where(_left_half(a.shape[0]), f0, f1)


def _kv_rope(n_kv, w_kv, b_kv, cos, sin, *, tr=2048):
    s, d = n_kv.shape
    tr = _row_tile(s, tr)

    def body(n_ref, w_ref, b_ref, c_ref, s_ref, k_ref, v_ref):
        x = jnp.dot(n_ref[...], w_ref[...], preferred_element_type=F32) + b_ref[...]
        k = x[:, :KV_WIDTH]
        k_ref[...] = _dup_heads(k * c_ref[...] + _swap_halves(k) * s_ref[...]).astype(BF16)
        v_ref[...] = _dup_heads(x[:, KV_WIDTH:]).astype(BF16)

    tab = pl.BlockSpec((tr, KV_WIDTH), lambda i: (i, 0))
    wide = pl.BlockSpec((tr, 2 * KV_WIDTH), lambda i: (i, 0))
    outs, _ = _call(body, grid=(s // tr,),
                    in_specs=[pl.BlockSpec((tr, d), lambda i: (i, 0)), pl.BlockSpec((d, 2 * KV_WIDTH), lambda i: (0, 0)),
                              pl.BlockSpec((1, 2 * KV_WIDTH), lambda i: (0, 0)), tab, tab],
                    out_specs=[wide, wide], out_shape=[jax.ShapeDtypeStruct((s, 2 * KV_WIDTH), BF16)] * 2,
                    args=(n_kv, w_kv, b_kv, cos, sin), name="kv_rope")
    return outs


def _kv_rope_bwd(dk2, dv2, cos, sin, *, tr=2048):
    s = dk2.shape[0]
    tr = _row_tile(s, tr)

    def body(dk_ref, dv_ref, c_ref, s_ref, dkv_ref, db_ref):
        i = pl.program_id(0)
        d = _fold_heads(dk_ref[...])
        dk = d * c_ref[...] + _swap_halves(d * s_ref[...])
        dvv = _fold_heads(dv_ref[...])
        dkv_ref[:, :KV_WIDTH] = dk.astype(BF16)
        dkv_ref[:, KV_WIDTH:] = dvv.astype(BF16)
        sk = jnp.sum(dk, axis=0, keepdims=True)
        sv = jnp.sum(dvv, axis=0, keepdims=True)

        @pl.when(i == 0)
        def _():
            db_ref[:, :KV_WIDTH] = sk
            db_ref[:, KV_WIDTH:] = sv

        @pl.when(i > 0)
        def _():
            db_ref[:, :KV_WIDTH] += sk
            db_ref[:, KV_WIDTH:] += sv

    tab = pl.BlockSpec((tr, KV_WIDTH), lambda i: (i, 0))
    wide = pl.BlockSpec((tr, 2 * KV_WIDTH), lambda i: (i, 0))
    outs, _ = _call(body, grid=(s // tr,), in_specs=[wide, wide, tab, tab],
                    out_specs=[wide, pl.BlockSpec((1, 2 * KV_WIDTH), lambda i: (0, 0))],
                    out_shape=[jax.ShapeDtypeStruct((s, 2 * KV_WIDTH), BF16),
                               jax.ShapeDtypeStruct((1, 2 * KV_WIDTH), F32)],
                    args=(dk2, dv2, cos, sin), name="kv_rope_bwd")
    return outs


def _from_previous():
    cols = Q_PER_KV * CHUNK
    k = lax.broadcasted_iota(jnp.int32, (CHUNK, cols), 0)
    q = lax.broadcasted_iota(jnp.int32, (CHUNK, cols), 1) & (CHUNK - 1)
    return k > q


def _fold(x2, prev):
    return jnp.where(prev, x2[:CHUNK], x2[CHUNK:])


def _unfold(x, prev):
    zero = jnp.zeros_like(x)
    return jnp.concatenate([jnp.where(prev, x, zero), jnp.where(prev, zero, x)], axis=0)


def _stack_heads(blocks, left):
    parts = []
    for b in blocks:
        parts.append(jnp.where(left, b, jnp.zeros_like(b)))
        parts.append(jnp.where(left, jnp.zeros_like(b), b))
    return jnp.concatenate(parts, axis=0)


def _unstack_heads(xt):
    top = lax.broadcasted_iota(jnp.int32, (128, CHUNK), 0) < HEAD_DIM
    return [jnp.where(top, xt[:, (2 * b) * CHUNK:(2 * b + 1) * CHUNK], xt[:, (2 * b + 1) * CHUNK:(2 * b + 2) * CHUNK]).T
            for b in range(BLOCKS_PER_KV)]


def _sink_row(sk_ref, kvh):
    return jnp.concatenate([jnp.full((1, CHUNK), sk_ref[0, kvh * Q_PER_KV + r], F32) for r in range(Q_PER_KV)], axis=1)


def _stacked_probs(qs, kd, prev, sink, i):
    sc2 = lax.dot_general(kd, qs, NT, preferred_element_type=F32)
    no_previous = jnp.where(i > 0, 0.0, NEG_BIG)
    sc = jnp.where(prev, sc2[:CHUNK] + no_previous, sc2[CHUNK:])
    sink = sink * (1.0 / SCALE)
    m = jnp.maximum(jnp.max(sc, axis=0, keepdims=True), sink)
    p = jnp.exp2((sc - m) * (SCALE * LOG2_E))
    esink = jnp.exp2((sink - m) * (SCALE * LOG2_E))
    inv = 1.0 / (jnp.sum(p, axis=0, keepdims=True) + esink)
    return p * inv, esink * inv


def _lane_block(b):
    return slice(b * 128, (b + 1) * 128)


def _rope_blocks(zq_ref, bq_ref, cos, sin, kvh, rows):
    out = []
    for b in range(BLOCKS_PER_KV):
        cols = _lane_block(kvh * BLOCKS_PER_KV + b)
        q = zq_ref[rows, cols].astype(F32) + bq_ref[:, cols]
        out.append((q * cos + _swap_halves(q) * sin).astype(BF16))
    return out


CHUNKS_PER_STEP = 4


def _attn_specs():
    rows = CHUNKS_PER_STEP * CHUNK
    qspec = pl.BlockSpec((rows, B_WIDTH), lambda i: (i, 0))
    gspec = pl.BlockSpec((rows, B_WIDTH), lambda i: (i, 1))
    prev = pl.BlockSpec((CHUNK, 2 * KV_WIDTH), lambda i: (jnp.maximum(CHUNKS_PER_STEP * i - 1, 0), 0))
    cur = pl.BlockSpec((rows, 2 * KV_WIDTH), lambda i: (i, 0))
    tab = pl.BlockSpec((rows, KV_WIDTH), lambda i: (i, 0))
    bq = pl.BlockSpec((1, B_WIDTH), lambda i: (0, 0))
    sinks = pl.BlockSpec(memory_space=pltpu.SMEM)
    return qspec, gspec, prev, cur, tab, bq, sinks


def _chunk_keys(prev_ref, cur_ref, sub):
    before = prev_ref[...] if sub == 0 else cur_ref[(sub - 1) * CHUNK:sub * CHUNK]
    return jnp.concatenate([before, cur_ref[sub * CHUNK:(sub + 1) * CHUNK]], axis=0)


def _attn_fwd(zb, k2, v2, cos, sin, b_bq, sinks, *, side=None):
    s = zb.shape[0]

    def body(zq_ref, zg_ref, kp_ref, kc_ref, vp_ref, vc_ref, c_ref, s_ref, bq_ref, sk_ref, y_ref):
        prev = _from_previous()
        left = _left_half(CHUNK)
        for sub in range(CHUNKS_PER_STEP):
            chunk = CHUNKS_PER_STEP * pl.program_id(0) + sub
            rows = slice(sub * CHUNK, (sub + 1) * CHUNK)
            cos, sin = c_ref[rows, :], s_ref[rows, :]
            kcat, vcat = _chunk_keys(kp_ref, kc_ref, sub), _chunk_keys(vp_ref, vc_ref, sub)
            for kvh in range(N_KV_HEADS):
                qs = _stack_heads(_rope_blocks(zq_ref, bq_ref, cos, sin, kvh, rows), left)
                p, _ = _stacked_probs(qs, kcat[:, _lane_block(kvh)], prev, _sink_row(sk_ref, kvh), chunk)
                ot = lax.dot_general(vcat[:, _lane_block(kvh)], _unfold(p, prev).astype(BF16), TN,
                                     preferred_element_type=F32)
                for b, ob in enumerate(_unstack_heads(ot)):
                    cols = _lane_block(kvh * BLOCKS_PER_KV + b)
                    gv = zg_ref[rows, cols].astype(F32)
                    y_ref[rows, cols] = (ob * (gv * jax.nn.sigmoid(gv))).astype(BF16)

    qspec, gspec, prev, cur, tab, bq, sk = _attn_specs()
    (y,), side_outs = _call(body, grid=(s // (CHUNKS_PER_STEP * CHUNK),),
                            in_specs=[qspec, gspec, prev, cur, prev, cur, tab, tab, bq, sk],
                            out_specs=[qspec], out_shape=[jax.ShapeDtypeStruct((s, B_WIDTH), BF16)],
                            args=(zb, zb, k2, k2, v2, v2, cos, sin, b_bq, sinks), name="attn_fwd", side=side)
    return y, side_outs


def _attn_bwd(zb, dyb, k2, v2, cos, sin, b_bq, sinks, *, side=None):
    s = zb.shape[0]

    def body(zq_ref, zg_ref, dy_ref, kp_ref, kc_ref, vp_ref, vc_ref, c_ref, s_ref, bq_ref, sk_ref,
             dz_ref, dk_ref, dv_ref, dbq_ref, dsk_ref):
        i = pl.program_id(0)

        @pl.when(i == 0)
        def _():
            dk_ref[...] = jnp.zeros_like(dk_ref)
            dv_ref[...] = jnp.zeros_like(dv_ref)
            dbq_ref[...] = jnp.zeros_like(dbq_ref)
            dsk_ref[...] = jnp.zeros_like(dsk_ref)

        prev = _from_previous()
        left = _left_half(CHUNK)
        lane = lax.broadcasted_iota(jnp.int32, (1, 128), 1)
        dsk_row = jnp.zeros((1, 128), F32)
        for sub in range(CHUNKS_PER_STEP):
            chunk = CHUNKS_PER_STEP * i + sub
            rows = slice(sub * CHUNK, (sub + 1) * CHUNK)
            cos, sin = c_ref[rows, :], s_ref[rows, :]
            kcat, vcat = _chunk_keys(kp_ref, kc_ref, sub), _chunk_keys(vp_ref, vc_ref, sub)
            cur_rows = pl.ds(pl.multiple_of(chunk * CHUNK, CHUNK), CHUNK)
            for kvh in range(N_KV_HEADS):
                kd, vd = kcat[:, _lane_block(kvh)], vcat[:, _lane_block(kvh)]
                qs = _stack_heads(_rope_blocks(zq_ref, bq_ref, cos, sin, kvh, rows), left)
                p, psink = _stacked_probs(qs, kd, prev, _sink_row(sk_ref, kvh), chunk)
                pb = _unfold(p, prev).astype(BF16)
                ot = lax.dot_general(vd, pb, TN, preferred_element_type=F32)
                gates, dys = [], []
                for b in range(BLOCKS_PER_KV):
                    cols = _lane_block(kvh * BLOCKS_PER_KV + b)
                    gates.append(_silu_parts(zg_ref[rows, cols].astype(F32)))
                    dys.append(dy_ref[rows, cols].astype(F32))
                dos = _stack_heads([(dyv * silu).astype(BF16) for dyv, (silu, _) in zip(dys, gates)], left)
                dp = _fold(lax.dot_general(vd, dos, NT, preferred_element_type=F32), prev)
                delta = jnp.sum(p * dp, axis=0, keepdims=True)
                ds = _unfold(p * (dp - delta) * SCALE, prev).astype(BF16)
                dqt = lax.dot_general(kd, ds, TN, preferred_element_type=F32)
                dk_part = jnp.dot(ds, qs, preferred_element_type=F32)
                dv_part = jnp.dot(pb, dos, preferred_element_type=F32)
                dk_ref[cur_rows, _lane_block(kvh)] += dk_part[CHUNK:]
                dv_ref[cur_rows, _lane_block(kvh)] += dv_part[CHUNK:]

                @pl.when(chunk > 0)
                def _(kvh=kvh, chunk=chunk, dk_part=dk_part, dv_part=dv_part):
                    prev_rows = pl.ds(pl.multiple_of((chunk - 1) * CHUNK, CHUNK), CHUNK)
                    dk_ref[prev_rows, _lane_block(kvh)] += dk_part[:CHUNK]
                    dv_ref[prev_rows, _lane_block(kvh)] += dv_part[:CHUNK]

                sink_grad = psink * delta
                for r in range(Q_PER_KV):
                    dsink = -jnp.sum(sink_grad[:, r * CHUNK:(r + 1) * CHUNK], axis=1, keepdims=True)
                    dsk_row = dsk_row + jnp.where(lane == kvh * Q_PER_KV + r, dsink, 0.0)
                blocks = zip(_unstack_heads(ot), _unstack_heads(dqt), dys, gates)
                for b, (ob, dqr, dyv, (_, dsilu)) in enumerate(blocks):
                    blk = kvh * BLOCKS_PER_KV + b
                    dq = dqr * cos + _swap_halves(dqr * sin)
                    dbq_ref[:, _lane_block(blk)] += jnp.sum(dq, axis=0, keepdims=True)
                    dz_ref[rows, _lane_block(blk)] = dq.astype(BF16)
                    dz_ref[rows, _lane_block(B_WIDTH // 128 + blk)] = (dyv * ob * dsilu).astype(BF16)
        dsk_ref[0:1, :] += dsk_row

    qspec, gspec, prev, cur, tab, bq, sk = _attn_specs()
    full = pl.BlockSpec((s, 2 * KV_WIDTH), lambda i: (0, 0))
    return _call(
        body, grid=(s // (CHUNKS_PER_STEP * CHUNK),),
        in_specs=[qspec, gspec, qspec, prev, cur, prev, cur, tab, tab, bq, sk],
        out_specs=[pl.BlockSpec((CHUNKS_PER_STEP * CHUNK, 2 * B_WIDTH), lambda i: (i, 0)), full, full, bq,
                   pl.BlockSpec((8, 128), lambda i: (0, 0))],
        out_shape=[jax.ShapeDtypeStruct((s, 2 * B_WIDTH), BF16), jax.ShapeDtypeStruct((s, 2 * KV_WIDTH), F32),
                   jax.ShapeDtypeStruct((s, 2 * KV_WIDTH), F32), jax.ShapeDtypeStruct((1, B_WIDTH), F32),
                   jax.ShapeDtypeStruct((8, 128), F32)],
        args=(zb, zb, dyb, k2, k2, v2, v2, cos, sin, b_bq, sinks), name="attn_bwd", side=side)


def _place():
    x, y, c = lax.axis_index("x"), lax.axis_index("y"), lax.axis_index("c")
    return x, y, c, [(1 - x, y), (x, 1 - y), (1 - x, 1 - y)]


def _relations():
    return [(r >> 2 & 1, r >> 1 & 1, r & 1) for r in range(1, 8)]


def _gather_side(arrs):
    n = len(arrs)

    def copies(ins, outs, sems):
        send_ici, recv_ici, send_d2d, recv_d2d, local_sem = sems
        x, y, c, chips = _place()
        me = 2 * x + y

        def rows(a, half):
            hr = arrs[a].shape[0] // 2
            return pl.ds(half * hr, hr)

        def ici(a, j, src_chip, to):
            return pltpu.make_async_remote_copy(
                src_ref=ins[a].at[rows(a, c)], dst_ref=outs[a].at[src_chip, rows(a, c)],
                send_sem=send_ici.at[a, j], recv_sem=recv_ici.at[a, j], device_id=to, device_id_type=MESH)

        def d2d(a, j, chip, half):
            blk = outs[a].at[chip, rows(a, half)]
            return pltpu.make_async_remote_copy(
                src_ref=blk, dst_ref=blk, send_sem=send_d2d.at[a, j], recv_sem=recv_d2d.at[a, j],
                device_id=(x, y, 1 - c), device_id_type=MESH)

        local = [pltpu.make_async_copy(ins[a], outs[a].at[me], local_sem.at[a]) for a in range(n)]
        pairs = [(a, j, chip) for a in range(n) for j, chip in enumerate(chips)]
        return c, me, local, ici, d2d, pairs

    def start(ins, outs, sems):
        c, me, local, ici, _, pairs = copies(ins, outs, sems)
        for cp in local:
            cp.start()
        for a, j, chip in pairs:
            ici(a, j, me, (*chip, c)).start()

    def passing(ins, outs, sems):
        c, _, _, ici, d2d, pairs = copies(ins, outs, sems)
        for a, j, (px, py) in pairs:
            ici(a, j, 2 * px + py, (px, py, c)).wait_recv()
            d2d(a, j, 2 * px + py, c).start()

    def finish(ins, outs, sems):
        c, me, local, ici, d2d, pairs = copies(ins, outs, sems)
        for a, j, (px, py) in pairs:
            d2d(a, j, 2 * px + py, 1 - c).wait_recv()
        for a, j, (px, py) in pairs:
            ici(a, j, me, (px, py, c)).wait_send()
            d2d(a, j, 2 * px + py, c).wait_send()
        for cp in local:
            cp.wait()

    return _Side(arrs, [jax.ShapeDtypeStruct((N_CHIPS,) + a.shape, a.dtype) for a in arrs],
                 [pltpu.SemaphoreType.DMA((n, 3))] * 4 + [pltpu.SemaphoreType.DMA((n,))], start, finish,
                 passing=passing)


def _exchange_side(grads):
    n = len(grads)

    def copies(ins, outs, sems):
        send_sem, recv_sem = sems
        x, y, c, _ = _place()
        cps = []
        for a in range(n):
            hr = grads[a].shape[1] // 2
            cps.append(pltpu.make_async_remote_copy(
                src_ref=ins[a].at[:, pl.ds((1 - c) * hr, hr), :], dst_ref=outs[a],
                send_sem=send_sem.at[a], recv_sem=recv_sem.at[a], device_id=(x, y, 1 - c), device_id_type=MESH))
        return cps

    def start(ins, outs, sems):
        for cp in copies(ins, outs, sems):
            cp.start()

    def finish(ins, outs, sems):
        for cp in copies(ins, outs, sems):
            cp.wait()

    return _Side(grads, [jax.ShapeDtypeStruct((g.shape[0], g.shape[1] // 2, g.shape[2]), g.dtype) for g in grads],
                 [pltpu.SemaphoreType.DMA((n,))] * 2, start, finish)


def _scatter_side(chip_sums, small=None):
    n = len(chip_sums)
    arrs = list(chip_sums) + ([small] if small is not None else [])

    def copies(ins, outs, sems):
        x, y, c, chips = _place()
        cps = []
        for a in range(n):
            for j, (px, py) in enumerate(chips):
                cps.append(pltpu.make_async_remote_copy(
                    src_ref=ins[a].at[2 * px + py], dst_ref=outs[a].at[j],
                    send_sem=sems[0].at[a, j], recv_sem=sems[1].at[a, j], device_id=(px, py, c), device_id_type=MESH))
        if small is not None:
            for r, (fx, fy, fc) in enumerate(_relations(), start=1):
                px, py, pc = x ^ fx, y ^ fy, c ^ fc
                cps.append(pltpu.make_async_remote_copy(
                    src_ref=ins[n].at[4 * px + 2 * py + pc], dst_ref=outs[n].at[r],
                    send_sem=sems[2].at[r - 1], recv_sem=sems[3].at[r - 1], device_id=(px, py, pc),
                    device_id_type=MESH))
        return cps

    def start(ins, outs, sems):
        for cp in copies(ins, outs, sems):
            cp.start()

    def finish(ins, outs, sems):
        for cp in copies(ins, outs, sems):
            cp.wait()

    shapes = [jax.ShapeDtypeStruct((3,) + t.shape[1:], t.dtype) for t in chip_sums]
    sems = [pltpu.SemaphoreType.DMA((n, 3))] * 2
    if small is not None:
        shapes.append(jax.ShapeDtypeStruct(small.shape, small.dtype))
        sems += [pltpu.SemaphoreType.DMA((7,))] * 2
    return _Side(arrs, shapes, sems, start, finish)


def _small_scatter_side(small):
    def copies(ins, outs, sems):
        x, y, c, _ = _place()
        cps = []
        for r, (fx, fy, fc) in enumerate(_relations(), start=1):
            px, py, pc = x ^ fx, y ^ fy, c ^ fc
            cps.append(pltpu.make_async_remote_copy(
                src_ref=ins[0].at[4 * px + 2 * py + pc], dst_ref=outs[0].at[r],
                send_sem=sems[0].at[r - 1], recv_sem=sems[1].at[r - 1], device_id=(px, py, pc), device_id_type=MESH))
        return cps

    def start(ins, outs, sems):
        for cp in copies(ins, outs, sems):
            cp.start()

    def finish(ins, outs, sems):
        for cp in copies(ins, outs, sems):
            cp.wait()

    return _Side([small], [jax.ShapeDtypeStruct(small.shape, small.dtype)], [pltpu.SemaphoreType.DMA((7,))] * 2,
                 start, finish)


def _small_share_side(small):
    return _share_side([], small)


def _share_side(halves, small=None):
    n = len(halves)
    arrs = list(halves) + ([small] if small is not None else [])

    def copies(ins, outs, sems, mine):
        x, y, c, _ = _place()
        me = 4 * x + 2 * y + c
        cps = []
        for a in range(n):
            hr = halves[a].shape[0] // 2
            rows = pl.ds((c if mine else 1 - c) * hr, hr)
            cps.append(pltpu.make_async_remote_copy(
                src_ref=ins[a].at[rows], dst_ref=outs[a].at[rows],
                send_sem=sems[0].at[a], recv_sem=sems[1].at[a], device_id=(x, y, 1 - c), device_id_type=MESH))
        if small is not None:
            for r, (fx, fy, fc) in enumerate(_relations(), start=1):
                px, py, pc = x ^ fx, y ^ fy, c ^ fc
                seg = me if mine else 4 * px + 2 * py + pc
                cps.append(pltpu.make_async_remote_copy(
                    src_ref=ins[n].at[seg], dst_ref=outs[n].at[seg],
                    send_sem=sems[-2].at[r - 1], recv_sem=sems[-1].at[r - 1], device_id=(px, py, pc),
                    device_id_type=MESH))
        return cps

    def start(ins, outs, sems):
        for cp in copies(ins, outs, sems, True):
            cp.start()

    def finish(ins, outs, sems):
        for cp in copies(ins, outs, sems, False):
            cp.wait_recv()
        for cp in copies(ins, outs, sems, True):
            cp.wait_send()

    sems = ([pltpu.SemaphoreType.DMA((n,))] * 2 if n else []) + (
        [pltpu.SemaphoreType.DMA((7,))] * 2 if small is not None else [])
    return _Side(arrs, [jax.ShapeDtypeStruct(h.shape, h.dtype) for h in arrs], sems, start, finish,
                 aliases={i: i for i in range(len(arrs))})


GATHER_PIECES = [(0, 0), (0, 1), (1, 0), (2, 0), (1, 1), (2, 1), (3, 0), (3, 1)]


def _mm_gathering(a, shard, order, *, name, tm=1024):
    s, k = a.shape
    nc = shard.shape[1]
    tm = _row_tile(s, tm)
    tn = nc // 2
    hr = k // 2
    qr = hr // 2
    blocks = jnp.stack([order[src] * 2 + h for src, h in GATHER_PIECES]).astype(jnp.int32)

    def body(blocks_ref, a_ref, shard_ref, z_ref, full_ref, wbuf, send_ici, recv_ici, send_relay,
             recv_relay, send_d2d, recv_d2d, local_sem, load_sem):
        piece, i = pl.program_id(0), pl.program_id(1)
        x, y, c, chips = _place()
        me = 2 * x + y
        nbrs = chips[:2]
        chip_of = [2 * px + py for px, py in chips]

        def quarter(q):
            return pl.ds(c * hr + q * qr, qr)

        def sibling_quarter(q):
            return pl.ds((1 - c) * hr + q * qr, qr)

        def whole(half):
            return pl.ds(half * hr, hr)

        def cols(h):
            return pl.ds(h * tn, tn)

        def direct(j, src_chip, h):
            return pltpu.make_async_remote_copy(
                src_ref=shard_ref.at[whole(c), cols(h)], dst_ref=full_ref.at[src_chip, whole(c), cols(h)],
                send_sem=send_ici.at[j, h], recv_sem=recv_ici.at[j, h], device_id=(*nbrs[j], c), device_id_type=MESH)

        def relay(j, src_chip, h):
            blk = full_ref.at[src_chip, quarter(j), cols(h)]
            return pltpu.make_async_remote_copy(
                src_ref=blk, dst_ref=blk, send_sem=send_relay.at[j, h], recv_sem=recv_relay.at[j, h],
                device_id=(*nbrs[1 - j], c), device_id_type=MESH)

        def d2d(j, chip, rows, h):
            blk = full_ref.at[chip, rows, cols(h)]
            return pltpu.make_async_remote_copy(
                src_ref=blk, dst_ref=blk, send_sem=send_d2d.at[j, h], recv_sem=recv_d2d.at[j, h],
                device_id=(x, y, 1 - c), device_id_type=MESH)

        def load(p):
            src, h = GATHER_PIECES[p]
            where = shard_ref if src == 0 else full_ref.at[chip_of[src - 1]]
            return pltpu.make_async_copy(where.at[:, cols(h)], wbuf.at[p % 2], load_sem.at[p % 2])

        local = pltpu.make_async_copy(shard_ref, full_ref.at[me], local_sem)

        def arrived(p):
            src, h = GATHER_PIECES[p]
            if src in (1, 2):
                j = src - 1
                direct(j, chip_of[j], h).wait_recv()
                relay(j, chip_of[j], h).start()
                d2d(j, chip_of[j], whole(c), h).start()
            elif src == 3:
                for j in range(2):
                    relay(1 - j, chip_of[2], h).wait_recv()
                    d2d(2 + j, chip_of[2], quarter(1 - j), h).start()

        def fetch(p):
            src, h = GATHER_PIECES[p]
            if src in (1, 2):
                d2d(src - 1, chip_of[src - 1], whole(1 - c), h).wait_recv()
            elif src == 3:
                for j in range(2):
                    d2d(2 + j, chip_of[2], sibling_quarter(1 - j), h).wait_recv()
            load(p).start()

        n_i = s // tm
        for p in range(len(GATHER_PIECES)):
            @pl.when(jnp.logical_and(piece == p, i == 0))
            def _(p=p):
                if p == 0:
                    local.start()
                    for hh in range(2):
                        for j in range(2):
                            direct(j, me, hh).start()
                    load(0).start()
                load(p).wait()

        z_ref[...] = jnp.dot(a_ref[...], wbuf[piece % 2], preferred_element_type=F32).astype(z_ref.dtype)

        for p in range(len(GATHER_PIECES) - 1):
            @pl.when(jnp.logical_and(piece == p, i == min(1, n_i - 1)))
            def _(p=p):
                arrived(p + 1)

            @pl.when(jnp.logical_and(piece == p, i == min(2, n_i - 1)))
            def _(p=p):
                fetch(p + 1)

        last = jnp.logical_and(piece == len(GATHER_PIECES) - 1, i == n_i - 1)

        @pl.when(last)
        def _():
            for h in range(2):
                for j in range(2):
                    direct(j, me, h).wait_send()
                    relay(j, chip_of[j], h).wait_send()
                    d2d(j, chip_of[j], whole(c), h).wait_send()
                    d2d(2 + j, chip_of[2], quarter(1 - j), h).wait_send()
            local.wait()

    return pl.pallas_call(
        body,
        grid_spec=pltpu.PrefetchScalarGridSpec(
            num_scalar_prefetch=1, grid=(len(GATHER_PIECES), s // tm),
            in_specs=[pl.BlockSpec((tm, k), lambda p, i, blocks: (i, 0)), HBM],
            out_specs=[pl.BlockSpec((tm, tn), lambda p, i, blocks: (i, blocks[p])), HBM],
            scratch_shapes=[pltpu.VMEM((2, k, tn), BF16)] + [pltpu.SemaphoreType.DMA((2, 2))] * 4
            + [pltpu.SemaphoreType.DMA((4, 2))] * 2 + [pltpu.SemaphoreType.DMA, pltpu.SemaphoreType.DMA((2,))]),
        out_shape=[jax.ShapeDtypeStruct((s, N_CHIPS * nc), BF16), jax.ShapeDtypeStruct((N_CHIPS, k, nc), BF16)],
        name=name, compiler_params=_cparams(),
    )(blocks, a, shard)


def _mm_tn_exchanging(a, b, *, name, shards, tk=2048, side=None):
    s, m = a.shape
    nc = b.shape[1] // shards
    tk = _row_tile(s, tk)
    nk = s // tk
    hm = m // 2

    def body(a_ref, b_ref, part_ref, sib_ref, acc, keep_sem, send_sem, recv_sem):
        j, kk = pl.program_id(0), pl.program_id(1)
        x, y, c, _ = _place()

        def keep(jj, slot):
            mine = pl.ds(c * hm, hm)
            return pltpu.make_async_copy(acc.at[slot, mine], part_ref.at[jj], keep_sem.at[slot])

        def give(jj, slot):
            return pltpu.make_async_remote_copy(
                src_ref=acc.at[slot, pl.ds((1 - c) * hm, hm)], dst_ref=sib_ref.at[jj],
                send_sem=send_sem.at[slot], recv_sem=recv_sem.at[jj], device_id=(x, y, 1 - c), device_id_type=MESH)

        part = lax.dot_general(a_ref[...], b_ref[...], TN, preferred_element_type=F32)
        for slot in range(2):
            @pl.when(j % 2 == slot)
            def _(slot=slot):
                @pl.when(jnp.logical_and(kk == 0, j >= 2))
                def _():
                    keep(j - 2, slot).wait()
                    give(j - 2, slot).wait_send()

                @pl.when(kk == 0)
                def _():
                    acc[slot] = part

                @pl.when(kk > 0)
                def _():
                    acc[slot] += part

                @pl.when(kk == nk - 1)
                def _():
                    keep(j, slot).start()
                    give(j, slot).start()

        @pl.when(jnp.logical_and(j == shards - 1, kk == nk - 1))
        def _():
            for jj in range(shards - 2, shards):
                keep(jj, jj % 2).wait()
                give(jj, jj % 2).wait_send()
            for jj in range(shards):
                give(jj, jj % 2).wait_recv()

    assert shards >= 2
    return _call(
        body, grid=(shards, nk),
        in_specs=[pl.BlockSpec((tk, m), lambda j, kk: (kk, 0)), pl.BlockSpec((tk, nc), lambda j, kk: (kk, j))],
        out_specs=[HBM, HBM],
        out_shape=[jax.ShapeDtypeStruct((shards, hm, nc), F32), jax.ShapeDtypeStruct((shards, hm, nc), F32)],
        scratch=[pltpu.VMEM((2, m, nc), F32), pltpu.SemaphoreType.DMA((2,)), pltpu.SemaphoreType.DMA((2,)),
                 pltpu.SemaphoreType.DMA((shards,))],
        args=(a, b), name=name, side=side)


def _col_tile(cols):
    return cols if cols <= 2048 else 512


def _add_sibling(grad, recv, core, *, name):
    k, r, c = grad.shape
    hr = r // 2
    tr = min(hr, 256)
    tc = _col_tile(c)
    nrb = hr // tr

    def body(core_ref, g_ref, r_ref, o_ref):
        o_ref[...] = (g_ref[...] + r_ref[...]).astype(BF16)

    return pl.pallas_call(
        body,
        grid_spec=pltpu.PrefetchScalarGridSpec(
            num_scalar_prefetch=1, grid=(k, nrb, c // tc),
            in_specs=[pl.BlockSpec((None, tr, tc), lambda kk, i, j, core: (kk, core[0] * nrb + i, j)),
                      pl.BlockSpec((None, tr, tc), lambda kk, i, j, core: (kk, i, j))],
            out_specs=pl.BlockSpec((None, tr, tc), lambda kk, i, j, core: (kk, i, j))),
        out_shape=jax.ShapeDtypeStruct((k, hr, c), BF16), name=name, compiler_params=_cparams(),
    )(core, grad, recv)


def _sum_chips(grad, from_sibling, recv, place, *, name):
    _, hr, c = from_sibling.shape
    tr = min(hr, 256)
    tc = _col_tile(c)
    nrb = hr // tr

    def body(place_ref, g_ref, s_ref, r0_ref, r1_ref, r2_ref, o_ref):
        own = g_ref[...] + s_ref[...]
        o_ref[...] = ((own + r0_ref[...].astype(F32)) + r1_ref[...].astype(F32)) + r2_ref[...].astype(F32)

    def rspec(j):
        return pl.BlockSpec((None, tr, tc), lambda i, jj, place: (j, i, jj))

    return pl.pallas_call(
        body,
        grid_spec=pltpu.PrefetchScalarGridSpec(
            num_scalar_prefetch=1, grid=(nrb, c // tc),
            in_specs=[pl.BlockSpec((None, tr, tc), lambda i, jj, place: (place[0], place[1] * nrb + i, jj)),
                      pl.BlockSpec((None, tr, tc), lambda i, jj, place: (place[0], i, jj)),
                      rspec(0), rspec(1), rspec(2)],
            out_specs=pl.BlockSpec((tr, tc), lambda i, jj, place: (place[1] * nrb + i, jj))),
        out_shape=jax.ShapeDtypeStruct((2 * hr, c), F32), name=name, compiler_params=_cparams(),
    )(place, grad, from_sibling, recv, recv, recv)


def _add_halves(mine, theirs, *, name, side=None):
    k, hr, c = mine.shape
    tr = min(hr, 256)
    tc = _col_tile(c)

    def body(a_ref, b_ref, o_ref):
        o_ref[...] = (a_ref[...] + b_ref[...]).astype(BF16)

    spec = pl.BlockSpec((None, tr, tc), lambda kk, i, j: (kk, i, j))
    (out,), side_outs = _call(body, grid=(k, hr // tr, c // tc), in_specs=[spec, spec], out_specs=[spec],
                              out_shape=[jax.ShapeDtypeStruct((k, hr, c), BF16)], args=(mine, theirs), name=name,
                              side=side)
    return out, side_outs


def _sum_halves(mine, theirs, recv, place, *, name):
    _, hr, c = mine.shape
    tr = min(hr, 256)
    tc = _col_tile(c)
    nrb = hr // tr

    def body(place_ref, a_ref, b_ref, r0_ref, r1_ref, r2_ref, o_ref):
        own = a_ref[...] + b_ref[...]
        o_ref[...] = ((own + r0_ref[...].astype(F32)) + r1_ref[...].astype(F32)) + r2_ref[...].astype(F32)

    def rspec(j):
        return pl.BlockSpec((None, tr, tc), lambda i, jj, place: (j, i, jj))

    own_spec = pl.BlockSpec((None, tr, tc), lambda i, jj, place: (place[0], i, jj))
    return pl.pallas_call(
        body,
        grid_spec=pltpu.PrefetchScalarGridSpec(
            num_scalar_prefetch=1, grid=(nrb, c // tc),
            in_specs=[own_spec, own_spec, rspec(0), rspec(1), rspec(2)],
            out_specs=pl.BlockSpec((tr, tc), lambda i, jj, place: (place[1] * nrb + i, jj))),
        out_shape=jax.ShapeDtypeStruct((2 * hr, c), F32), name=name, compiler_params=_cparams(),
    )(place, mine, theirs, recv, recv, recv)


def _sum_small(small, recv, place):
    _, sr, _ = small.shape

    def body(place_ref, own_ref, r_ref, o_ref):
        acc = own_ref[...]
        for r in range(1, 8):
            acc = acc + r_ref[r]
        o_ref[...] = acc

    return pl.pallas_call(
        body,
        grid_spec=pltpu.PrefetchScalarGridSpec(
            num_scalar_prefetch=1, grid=(1,),
            in_specs=[pl.BlockSpec((None, sr, 128), lambda i, place: (place[2], 0, 0)),
                      pl.BlockSpec((8, sr, 128), lambda i, place: (0, 0, 0))],
            out_specs=pl.BlockSpec((None, sr, 128), lambda i, place: (place[2], 0, 0))),
        out_shape=jax.ShapeDtypeStruct(small.shape, F32), name="sum_small", compiler_params=_cparams(),
    )(place, small, recv)


def _spread_side(vec):
    def copies(ins, outs, sems):
        x, y, c, _ = _place()
        return [pltpu.make_async_remote_copy(
            src_ref=ins[0], dst_ref=outs[0].at[r], send_sem=sems[0].at[r - 1], recv_sem=sems[1].at[r - 1],
            device_id=(x ^ fx, y ^ fy, c ^ fc), device_id_type=MESH)
            for r, (fx, fy, fc) in enumerate(_relations(), start=1)]

    def start(ins, outs, sems):
        for cp in copies(ins, outs, sems):
            cp.start()

    def finish(ins, outs, sems):
        for cp in copies(ins, outs, sems):
            cp.wait()

    return _Side([vec], [jax.ShapeDtypeStruct((8,) + vec.shape, vec.dtype)], [pltpu.SemaphoreType.DMA((7,))] * 2,
                 start, finish)


def _sum_in_device_order(own, spread, place):
    def body(place_ref, own_ref, r_ref, o_ref):
        me = place_ref[2]
        acc = jnp.zeros_like(own_ref[...])
        for d in range(8):
            slot = jnp.where(me == d, 1, me ^ d)
            acc = acc + jnp.where(me == d, own_ref[...], r_ref[slot])
        o_ref[...] = acc

    return pl.pallas_call(
        body,
        grid_spec=pltpu.PrefetchScalarGridSpec(
            num_scalar_prefetch=1, grid=(1,),
            in_specs=[pl.BlockSpec(own.shape, lambda i, place: (0, 0)),
                      pl.BlockSpec(spread.shape, lambda i, place: (0, 0, 0))],
            out_specs=pl.BlockSpec(own.shape, lambda i, place: (0, 0))),
        out_shape=jax.ShapeDtypeStruct(own.shape, F32), name="sum_in_device_order", compiler_params=_cparams(),
    )(place, own, spread)


def _adamw(w, g, m, v, *, name):
    r, c = w.shape
    tr = 256 if r % 256 == 0 else r
    tc = _col_tile(c)
    bc1 = 1.0 - ADAM_B1 ** ADAM_STEP
    bc2 = 1.0 - ADAM_B2 ** ADAM_STEP

    def body(w_ref, g_ref, m_ref, v_ref, d_ref, nm_ref, nv_ref, gout_ref):
        gv = g_ref[...]
        nm = ADAM_B1 * m_ref[...] + (1.0 - ADAM_B1) * gv
        nv = ADAM_B2 * v_ref[...] + (1.0 - ADAM_B2) * (gv * gv)
        d_ref[...] = -ADAM_LR * ((nm / bc1) / (jnp.sqrt(nv / bc2) + ADAM_EPS) + ADAM_WD * w_ref[...])
        nm_ref[...] = nm
        nv_ref[...] = nv
        gout_ref[...] = gv

    spec = pl.BlockSpec((tr, tc), lambda i, j: (i, j))
    outs, _ = _call(body, grid=(r // tr, c // tc), in_specs=[spec] * 4, out_specs=[spec] * 4,
                    out_shape=[jax.ShapeDtypeStruct((r, c), F32)] * 4, args=(w, g, m, v), name=name)
    return outs


SMALL_ORDER = ["a_ws", "a_bs", "a_norm_g", "a_ln_g", "a_ln_b", "kv_norm_g", "b_kv", "b_norm_g", "b_bq",
               "b_sinks", "final_norm_g"]
SHARDED_SMALL = {"a_norm_g", "a_ln_g", "a_ln_b"}
PACK_TILE = 8 * 128


def _rows128(a):
    flat = a.reshape(-1)
    return jnp.pad(flat, (0, (-flat.shape[0]) % PACK_TILE)).reshape(-1, 128)


def _pack_rows(parts, multiple):
    rows = [_rows128(p) for p in parts]
    total = sum(r.shape[0] for r in rows)
    pad = (-total) % multiple
    if pad:
        rows.append(jnp.zeros((pad, 128), rows[0].dtype))
    return jnp.concatenate(rows, axis=0)


def _unpack_rows(packed, shapes):
    out, row = [], 0
    for shp in shapes:
        size = math.prod(shp)
        nrow = -(-size // PACK_TILE) * 8
        out.append(packed[row:row + nrow].reshape(-1)[:size].reshape(shp))
        row += nrow
    return out


WEIGHTS = ["a_norm_g", "a_w_in", "a_ln_g", "a_ln_b", "a_ws", "a_bs", "a_w_out", "kv_norm_g", "w_kv", "b_kv",
           "b_norm_g", "b_w_in", "b_bq", "b_sinks", "b_w_out", "final_norm_g"]
BIG = ["a_w_in", "a_w_out", "w_kv", "b_w_in", "b_w_out"]


class _Reduction:
    def __init__(self, names, partials, core, place, small=None):
        self.names, self.partials, self.core, self.place, self.small = names, partials, core, place, small

    def exchange_side(self):
        return _exchange_side(self.partials)

    def took_exchange(self, from_sibling):
        self.from_sibling = from_sibling
        self.chip_sums = [_add_sibling(g, r, self.core, name="add_sibling_" + n)
                          for g, r, n in zip(self.partials, from_sibling, self.names)]

    def scatter_side(self):
        return _scatter_side(self.chip_sums, self.small)

    def took_scatter(self, arrived):
        big = arrived[:len(self.names)]
        self.halves = [_sum_chips(g, fs, r, self.place, name="sum_chips_" + n)
                       for g, fs, r, n in zip(self.partials, self.from_sibling, big, self.names)]
        self.small_mine = _sum_small(self.small, arrived[-1], self.place) if self.small is not None else None

    def share_side(self):
        return _share_side(self.halves, self.small_mine)

    def took_share(self, shared):
        self.grads = dict(zip(self.names, shared[:len(self.names)]))
        self.small_full = shared[-1] if self.small is not None else None


def _step(x, loss_target, p, m, v):
    xi, yi, ci = lax.axis_index("x"), lax.axis_index("y"), lax.axis_index("c")
    chip = 2 * xi + yi
    device = 4 * xi + 2 * yi + ci
    core = jnp.reshape(ci, (1,)).astype(jnp.int32)
    place = jnp.stack([chip, ci, device]).astype(jnp.int32)
    x, tgt = x[0], loss_target[0]
    s = x.shape[0]
    cos, sin = _rope_tables(s)

    shard2d = {n: p[n].reshape(p[n].shape[-2:]) for n in BIG}
    shard_bf = {n: shard2d[n].astype(BF16) for n in BIG}
    ws = p["a_ws"][0]
    ws_t = jnp.swapaxes(ws, 1, 2)
    bs_t = p["a_bs"][0].T
    kv_norm_g, b_kv = p["kv_norm_g"].reshape(1, -1), p["b_kv"].reshape(1, -1)
    final_norm_g = p["final_norm_g"].reshape(1, -1)

    vec_shapes = [p[n].shape for n in ("a_norm_g", "a_ln_g", "a_ln_b")]
    vec_pack = _pack_rows([p["a_norm_g"], p["a_ln_g"], p["a_ln_b"]], 16)
    (vec_all,) = _comm_call(_gather_side([vec_pack]), "gather_vectors")
    vecs = [_unpack_rows(vec_all[k], vec_shapes) for k in range(N_CHIPS)]
    a_norm_g, a_ln_g, a_ln_b = (jnp.concatenate([vk[t] for vk in vecs], axis=-1) for t in range(3))

    (n_a,) = _rms_fwd(x, [a_norm_g], name="rms_a")
    order = jnp.stack([chip, 2 * (1 - xi) + yi, 2 * xi + (1 - yi), 2 * (1 - xi) + (1 - yi)]).astype(jnp.int32)
    z, a_w_in = _mm_gathering(n_a, shard_bf["a_w_in"], order, name="mm_a_in")
    y, (a_w_out,) = _gate_fwd(z, a_ln_g, a_ln_b, ws, bs_t, side=_gather_side([shard_bf["a_w_out"]]))
    a_w_out = a_w_out.reshape(A_WIDTH, D_MODEL)
    (h1, n_kv, n_b), (w_kv, b_w_in) = _mm_residual_norms(
        y, a_w_out, x, [kv_norm_g, p["b_norm_g"]], name="mm_a_out",
        side=_gather_side([shard_bf["w_kv"], shard_bf["b_w_in"]]))
    w_kv = w_kv.reshape(D_MODEL, 2 * KV_WIDTH)
    kr, vv = _kv_rope(n_kv, w_kv, b_kv, cos, sin)
    zb = _mm_nn(n_b, b_w_in, name="mm_b_in", tn=512, tm=2048, out_dtype=BF16)
    yb, (b_w_out,) = _attn_fwd(zb, kr, vv, cos, sin, p["b_bq"], p["b_sinks"], side=_gather_side([shard_bf["b_w_out"]]))
    b_w_out = b_w_out.reshape(B_WIDTH, D_MODEL)
    loss_blk, dh2, dh2b, d_final_g = _mm_residual_loss(yb, b_w_out, h1, tgt, final_norm_g, name="mm_b_out")

    d_b_w_out = _mm_tn(yb, dh2b, name="mm_d_b_w_out", tm=B_WIDTH, tn=D_MODEL)
    red_bo = _Reduction(["b_w_out"], [d_b_w_out.reshape(N_CHIPS, B_WIDTH // N_CHIPS, D_MODEL)], core, place)
    dyb, got = _mm_nt(dh2b, b_w_out, name="mm_dyb", tm=1024, out_dtype=BF16, side=red_bo.exchange_side())
    red_bo.took_exchange(got)
    (dzb, dk_rot, dv, d_bq, d_sinks), got = _attn_bwd(zb, dyb, kr, vv, cos, sin, p["b_bq"], p["b_sinks"],
                                                      side=red_bo.scatter_side())
    red_bo.took_scatter(got)
    dkv, d_b_kv = _kv_rope_bwd(dk_rot, dv, cos, sin)
    d_b_w_in = _mm_tn(n_b, dzb, name="mm_d_b_w_in", tm=D_MODEL, tn=512, shards=N_CHIPS)
    d_w_kv, got = _mm_tn(n_kv, dkv, name="mm_d_w_kv", tm=D_MODEL, tn=2 * KV_WIDTH, side=red_bo.share_side())
    red_bo.took_share(got)
    red_bi = _Reduction(["b_w_in", "w_kv"], [d_b_w_in, d_w_kv.reshape(N_CHIPS, D_MODEL // N_CHIPS, 2 * KV_WIDTH)],
                        core, place)
    (dh1, dh1b, d_kv_g, d_b_g), got = _mm_nt_rms_bwd(
        [(dkv, w_kv, kv_norm_g), (dzb, b_w_in, p["b_norm_g"])], h1, dh2, name="mm_dn_b", tm=512,
        side=red_bi.exchange_side())
    red_bi.took_exchange(got)

    d_a_w_out = _mm_tn(y, dh1b, name="mm_d_a_w_out", tm=1024, tn=D_MODEL)
    red_ao = _Reduction(["a_w_out"], [d_a_w_out.reshape(N_CHIPS, A_WIDTH // N_CHIPS, D_MODEL)], core, place)
    dy, got = _mm_nt(dh1b, a_w_out, name="mm_dy", tn=1024, tm=1024, out_dtype=BF16, side=red_ao.exchange_side())
    red_ao.took_exchange(got)
    sides = [red_bi.scatter_side(), red_ao.scatter_side()]
    (dz, d_ln_g, d_ln_b, d_ws, d_bs_t), got = _gate_bwd(z, dy, a_ln_g, a_ln_b, ws, ws_t, bs_t, side=_join(sides))
    got = _split(got, sides)
    red_bi.took_scatter(got[0])
    red_ao.took_scatter(got[1])
    small = {
        "a_ws": d_ws, "a_bs": d_bs_t.T, "a_ln_g": d_ln_g, "a_ln_b": d_ln_b,
        "kv_norm_g": d_kv_g, "b_kv": d_b_kv, "b_norm_g": d_b_g, "b_bq": d_bq,
        "b_sinks": d_sinks[0:1, :N_Q_HEADS], "final_norm_g": d_final_g,
    }
    packed = [n for n in SMALL_ORDER if n != "a_norm_g"]
    small_shapes = [small[n].shape for n in packed] + [(1, 1)]
    small_pack = _pack_rows([small[n] for n in packed] + [loss_blk[0:1, 0:1]], 64)
    seg = small_pack.shape[0] // 8
    small_pack = small_pack.reshape(8, seg, 128)
    sides = [red_bi.share_side(), red_ao.share_side(), _small_scatter_side(small_pack)]
    (d_a_w_in, from_sibling), got = _mm_tn_exchanging(n_a, dz, name="mm_d_a_w_in", shards=N_CHIPS, side=_join(sides))
    got = _split(got, sides)
    red_bi.took_share(got[0])
    red_ao.took_share(got[1])
    small_mine = _sum_small(small_pack, got[2][0], place)

    chip_sum, (small_all,) = _add_halves(d_a_w_in, from_sibling, name="add_sibling_a_w_in",
                                         side=_small_share_side(small_mine))
    (dx, _, d_a_g), (arrived,) = _mm_nt_rms_bwd([(dz, a_w_in, a_norm_g)], x, dh1, name="mm_dn_a", tm=256,
                                                side=_scatter_side([chip_sum]))
    half_ai = _sum_halves(d_a_w_in, from_sibling, arrived, place, name="sum_chips_a_w_in")
    d_a_g = _rows128(d_a_g)
    sides = [_share_side([half_ai]), _spread_side(d_a_g)]
    got = _split(_comm_call(_join(sides), "share_last"), sides)
    grad_ai = got[0][0]
    small_full = dict(zip(packed + ["loss"], _unpack_rows(small_all.reshape(8 * seg, 128), small_shapes)))
    small_full["a_norm_g"] = _sum_in_device_order(d_a_g, got[1][0], place).reshape(1, -1)
    loss = small_full["loss"].reshape(())

    grad_big = {**red_bo.grads, **red_bi.grads, **red_ao.grads, "a_w_in": grad_ai}
    grads = {}
    for n in SMALL_ORDER:
        gfull = small_full[n]
        if n in SHARDED_SMALL:
            width = p[n].shape[-1]
            gfull = lax.dynamic_slice_in_dim(gfull, chip * width, width, axis=-1)
        grads[n] = gfull.reshape(p[n].shape)

    delta, new_m, new_v = {}, {}, {}
    for n in BIG:
        d, nm, nv, g = _adamw(shard2d[n], grad_big[n], m[n].reshape(shard2d[n].shape),
                              v[n].reshape(shard2d[n].shape), name="adamw_" + n)
        delta[n], new_m[n], new_v[n] = d.reshape(p[n].shape), nm.reshape(p[n].shape), nv.reshape(p[n].shape)
        grads[n] = g.reshape(p[n].shape)
    shapes = [p[n].shape for n in SMALL_ORDER]
    packs = [_pack_rows([src[n] for n in SMALL_ORDER], 8) for src in (p, grads, m, v)]
    outs = _adamw(*packs, name="adamw_small")[:3]
    for res, packed in zip((delta, new_m, new_v), outs):
        for n, val in zip(SMALL_ORDER, _unpack_rows(packed, shapes)):
            res[n] = val

    return (loss, dx[None], *[grads[n] for n in WEIGHTS], *[delta[n] for n in WEIGHTS],
            *[new_m[n] for n in WEIGHTS], *[new_v[n] for n in WEIGHTS])


def kernel(x, a_norm_g, a_w_in, a_ln_g, a_ln_b, a_ws, a_bs, a_w_out, kv_norm_g, w_kv, b_kv, b_norm_g, b_w_in, b_bq, b_sinks, b_w_out, final_norm_g, loss_target, m_a_norm_g, m_a_w_in, m_a_ln_g, m_a_ln_b, m_a_ws, m_a_bs, m_a_w_out, m_kv_norm_g, m_w_kv, m_b_kv, m_b_norm_g, m_b_w_in, m_b_bq, m_b_sinks, m_b_w_out, m_final_norm_g, v_a_norm_g, v_a_w_in, v_a_ln_g, v_a_ln_b, v_a_ws, v_a_bs, v_a_w_out, v_kv_norm_g, v_w_kv, v_b_kv, v_b_norm_g, v_b_w_in, v_b_bq, v_b_sinks, v_b_w_out, v_final_norm_g):
    p = dict(a_norm_g=a_norm_g, a_w_in=a_w_in, a_ln_g=a_ln_g, a_ln_b=a_ln_b, a_ws=a_ws, a_bs=a_bs, a_w_out=a_w_out,
             kv_norm_g=kv_norm_g, w_kv=w_kv, b_kv=b_kv, b_norm_g=b_norm_g, b_w_in=b_w_in, b_bq=b_bq, b_sinks=b_sinks,
             b_w_out=b_w_out, final_norm_g=final_norm_g)
    m = dict(a_norm_g=m_a_norm_g, a_w_in=m_a_w_in, a_ln_g=m_a_ln_g, a_ln_b=m_a_ln_b, a_ws=m_a_ws, a_bs=m_a_bs,
             a_w_out=m_a_w_out, kv_norm_g=m_kv_norm_g, w_kv=m_w_kv, b_kv=m_b_kv, b_norm_g=m_b_norm_g, b_w_in=m_b_w_in,
             b_bq=m_b_bq, b_sinks=m_b_sinks, b_w_out=m_b_w_out, final_norm_g=m_final_norm_g)
    v = dict(a_norm_g=v_a_norm_g, a_w_in=v_a_w_in, a_ln_g=v_a_ln_g, a_ln_b=v_a_ln_b, a_ws=v_a_ws, a_bs=v_a_bs,
             a_w_out=v_a_w_out, kv_norm_g=v_kv_norm_g, w_kv=v_w_kv, b_kv=v_b_kv, b_norm_g=v_b_norm_g, b_w_in=v_b_w_in,
             b_bq=v_b_bq, b_sinks=v_b_sinks, b_w_out=v_b_w_out, final_norm_g=v_final_norm_g)
    return _step(x, loss_target, p, m, v)
```

```python
import functools
import math

import jax
import jax.numpy as jnp
from jax import lax
from jax.experimental import pallas as pl
from jax.experimental.pallas import tpu as pltpu

F32 = jnp.float32
BF16 = jnp.bfloat16

D_MODEL = 1024
CHUNK = 128
A_WIDTH = 2048
A_GROUPS = 16
HEAD_DIM = 64
N_Q_HEADS = 16
N_KV_HEADS = 2
Q_PER_KV = 8
B_WIDTH = 1024
KV_WIDTH = 128
ROPE_THETA = 10000.0
EPS = 1e-5
N_CHIPS = 4

ADAM_LR = 0.001
ADAM_B1 = 0.9
ADAM_B2 = 0.999
ADAM_EPS = 1e-08
ADAM_WD = 0.01
ADAM_STEP = 10

VMEM_LIMIT = 48 * 1024 * 1024
MESH = pl.DeviceIdType.MESH
NEG_BIG = -1e30
HBM = pl.BlockSpec(memory_space=pl.ANY)

NN = (((1,), (0,)), ((), ()))
NT = (((1,), (1,)), ((), ()))
TN = (((0,), (0,)), ((), ()))


def _cparams(**kw):
    return pltpu.CompilerParams(vmem_limit_bytes=VMEM_LIMIT, **kw)


class _Side:
    def __init__(self, ins, out_shapes, sems, start, finish, aliases=None, passing=None):
        self.ins, self.out_shapes, self.sems = list(ins), list(out_shapes), list(sems)
        self.start, self.finish = start, finish
        self.passing = passing or (lambda ins, outs, sems: None)
        self.aliases = dict(aliases or {})


def _join(sides):
    sides = [s for s in sides if s is not None]
    if not sides:
        return None
    offs, i, o, m = [], 0, 0, 0
    for s in sides:
        offs.append((i, o, m))
        i, o, m = i + len(s.ins), o + len(s.out_shapes), m + len(s.sems)

    def run(which):
        def go(ins, outs, sems):
            for s, (a, b, c) in zip(sides, offs):
                getattr(s, which)(ins[a:a + len(s.ins)], outs[b:b + len(s.out_shapes)], sems[c:c + len(s.sems)])
        return go

    aliases = {}
    for s, (a, b, _) in zip(sides, offs):
        aliases.update({a + k: b + v for k, v in s.aliases.items()})
    return _Side([x for s in sides for x in s.ins], [x for s in sides for x in s.out_shapes],
                 [x for s in sides for x in s.sems], run("start"), run("finish"), aliases, run("passing"))


def _split(side_outs, sides):
    out, pos = [], 0
    for s in sides:
        out.append(list(side_outs[pos:pos + len(s.out_shapes)]))
        pos += len(s.out_shapes)
    return out


def _call(body, *, grid, in_specs, out_specs, out_shape, args, name, scratch=(), side=None):
    in_specs, out_specs, out_shape, scratch = list(in_specs), list(out_specs), list(out_shape), list(scratch)
    if side is None:
        res = pl.pallas_call(body, grid=grid, in_specs=in_specs, out_specs=out_specs, out_shape=out_shape,
                             scratch_shapes=scratch, name=name, compiler_params=_cparams())(*args)
        return list(res), []
    n_in, n_out, n_sc = len(in_specs), len(out_specs), len(scratch)
    s_in, s_out = len(side.ins), len(side.out_shapes)

    def wrapped(*refs):
        ins, refs = refs[:n_in], refs[n_in:]
        side_ins, refs = refs[:s_in], refs[s_in:]
        outs, refs = refs[:n_out], refs[n_out:]
        side_outs, refs = refs[:s_out], refs[s_out:]
        scr, side_sems = refs[:n_sc], refs[n_sc:]
        step = 0
        for a, g in enumerate(grid):
            step = step * g + pl.program_id(a)
        steps = math.prod(grid)

        @pl.when(step == 0)
        def _():
            side.start(side_ins, side_outs, side_sems)

        body(*ins, *outs, *scr)

        @pl.when(step == (3 * (steps - 1)) // 4)
        def _():
            side.passing(side_ins, side_outs, side_sems)

        @pl.when(step == steps - 1)
        def _():
            side.finish(side_ins, side_outs, side_sems)

    res = pl.pallas_call(
        wrapped, grid=grid, in_specs=in_specs + [HBM] * s_in, out_specs=out_specs + [HBM] * s_out,
        out_shape=out_shape + side.out_shapes, scratch_shapes=scratch + side.sems,
        input_output_aliases={n_in + k: n_out + v for k, v in side.aliases.items()},
        name=name, compiler_params=_cparams(),
    )(*args, *side.ins)
    return list(res[:n_out]), list(res[n_out:])


def _comm_call(side, name):
    s_in, s_out = len(side.ins), len(side.out_shapes)

    def body(*refs):
        ins, outs, sems = refs[:s_in], refs[s_in:s_in + s_out], refs[s_in + s_out:]
        side.start(ins, outs, sems)
        side.passing(ins, outs, sems)
        side.finish(ins, outs, sems)

    return list(pl.pallas_call(
        body, in_specs=[HBM] * s_in, out_specs=[HBM] * s_out, out_shape=side.out_shapes, scratch_shapes=side.sems,
        input_output_aliases=side.aliases, name=name,
    )(*side.ins))


def _matmul(a, b, *, dims, grid, a_spec, b_spec, o_spec, out_shape, name, acc_axis=None,
            residual=None, r_spec=None, side=None):
    has_res = residual is not None

    def body(*refs):
        if has_res:
            a_ref, b_ref, r_ref, o_ref = refs
        else:
            a_ref, b_ref, o_ref = refs
        part = lax.dot_general(a_ref[...], b_ref[...], dims, preferred_element_type=F32)
        if acc_axis is None:
            if has_res:
                part = part + r_ref[...]
            o_ref[...] = part.astype(o_ref.dtype)
        else:
            k = pl.program_id(acc_axis)

            @pl.when(k == 0)
            def _():
                o_ref[...] = part

            @pl.when(k > 0)
            def _():
                o_ref[...] += part

    in_specs = [a_spec, b_spec] + ([r_spec] if has_res else [])
    args = (a, b) + ((residual,) if has_res else ())
    (out,), side_outs = _call(body, grid=grid, in_specs=in_specs, out_specs=[o_spec], out_shape=[out_shape],
                              args=args, name=name, side=side)
    return (out, side_outs) if side is not None else out


def _row_tile(s, want):
    return min(s, want)


def _mm_nn(a, b, *, name, tn, out_dtype=F32, residual=None, tm=512, side=None):
    s, k = a.shape
    tm = _row_tile(s, tm)
    if b.ndim == 3:
        nsh, _, nc = b.shape
        npb = nc // tn
        n = nsh * nc
        b_spec = pl.BlockSpec((None, k, tn), lambda i, j: (j // npb, 0, j % npb))
    else:
        n = b.shape[1]
        b_spec = pl.BlockSpec((k, tn), lambda i, j: (0, j))
    return _matmul(
        a, b, dims=NN, grid=(s // tm, n // tn),
        a_spec=pl.BlockSpec((tm, k), lambda i, j: (i, 0)), b_spec=b_spec,
        o_spec=pl.BlockSpec((tm, tn), lambda i, j: (i, j)),
        out_shape=jax.ShapeDtypeStruct((s, n), out_dtype), name=name, side=side,
        residual=residual, r_spec=pl.BlockSpec((tm, tn), lambda i, j: (i, j)) if residual is not None else None)


def _mm_nt(a, b, *, name, tn=None, tm=512, out_dtype=F32, side=None):
    s, k = a.shape
    tm = _row_tile(s, tm)
    n = b.shape[0]
    tn = n if tn is None else tn
    return _matmul(
        a, b, dims=NT, grid=(s // tm, n // tn),
        a_spec=pl.BlockSpec((tm, k), lambda i, j: (i, 0)),
        b_spec=pl.BlockSpec((tn, k), lambda i, j: (j, 0)),
        o_spec=pl.BlockSpec((tm, tn), lambda i, j: (i, j)),
        out_shape=jax.ShapeDtypeStruct((s, n), out_dtype), name=name, side=side)


def _mm_tn(a, b, *, name, tm, tn, tk=2048, shards=None, side=None):
    s, m = a.shape
    n = b.shape[1]
    tk = _row_tile(s, tk)
    if shards is None:
        o_spec = pl.BlockSpec((tm, tn), lambda i, j, kk: (i, j))
        out_shape = jax.ShapeDtypeStruct((m, n), F32)
    else:
        assert tm == m
        nc = n // shards
        npb = nc // tn
        o_spec = pl.BlockSpec((None, m, tn), lambda i, j, kk: (j // npb, 0, j % npb))
        out_shape = jax.ShapeDtypeStruct((shards, m, nc), F32)
    return _matmul(
        a, b, dims=TN, grid=(m // tm, n // tn, s // tk), acc_axis=2,
        a_spec=pl.BlockSpec((tk, tm), lambda i, j, kk: (kk, i)),
        b_spec=pl.BlockSpec((tk, tn), lambda i, j, kk: (kk, j)),
        o_spec=o_spec, out_shape=out_shape, name=name, side=side)


def _rstd(x):
    return lax.rsqrt(jnp.mean(x * x, axis=-1, keepdims=True) + EPS)


def _rms_fwd(x, gains, *, name, tr=1024):
    s, d = x.shape
    tr = _row_tile(s, tr)
    ng = len(gains)

    def body(*refs):
        xv = refs[0][...]
        xh = xv * _rstd(xv)
        for t in range(ng):
            refs[1 + ng + t][...] = (xh * refs[1 + t][...]).astype(BF16)

    row = pl.BlockSpec((tr, d), lambda i: (i, 0))
    vec = pl.BlockSpec((1, d), lambda i: (0, 0))
    outs, _ = _call(body, grid=(s // tr,), in_specs=[row] + [vec] * ng, out_specs=[row] * ng,
                    out_shape=[jax.ShapeDtypeStruct((s, d), BF16)] * ng, args=(x, *gains), name=name)
    return outs


def _accumulate(i, ref, value):
    @pl.when(i == 0)
    def _():
        ref[...] = value

    @pl.when(i > 0)
    def _():
        ref[...] += value


def _mm_residual_norms(y, w, res, gains, *, name, tm=512, side=None):
    s, k = y.shape
    d = w.shape[1]
    tm = _row_tile(s, tm)
    ng = len(gains)

    def body(y_ref, w_ref, r_ref, *rest):
        g_refs, h_ref, n_refs = rest[:ng], rest[ng], rest[ng + 1:]
        h = r_ref[...] + jnp.dot(y_ref[...], w_ref[...], preferred_element_type=F32)
        h_ref[...] = h
        xh = h * _rstd(h)
        for t in range(ng):
            n_refs[t][...] = (xh * g_refs[t][...]).astype(BF16)

    row = pl.BlockSpec((tm, d), lambda i: (i, 0))
    vec = pl.BlockSpec((1, d), lambda i: (0, 0))
    return _call(
        body, grid=(s // tm,),
        in_specs=[pl.BlockSpec((tm, k), lambda i: (i, 0)), pl.BlockSpec((k, d), lambda i: (0, 0)), row] + [vec] * ng,
        out_specs=[row] * (1 + ng),
        out_shape=[jax.ShapeDtypeStruct((s, d), F32)] + [jax.ShapeDtypeStruct((s, d), BF16)] * ng,
        args=(y, w, res, *gains), name=name, side=side)


def _mm_residual_loss(y, w, res, tgt, gain, *, name, tm=512):
    s, k = y.shape
    d = w.shape[1]
    tm = _row_tile(s, tm)

    def body(y_ref, w_ref, r_ref, t_ref, g_ref, loss_ref, dh_ref, dhb_ref, dg_ref):
        i = pl.program_id(0)
        hv = r_ref[...] + jnp.dot(y_ref[...], w_ref[...], preferred_element_type=F32)
        g = g_ref[...]
        r = _rstd(hv)
        xh = hv * r
        diff = xh * g - t_ref[...]
        part = 0.5 / d * jnp.sum(jnp.sum(diff * diff, axis=-1, keepdims=True), axis=0, keepdims=True)
        dout = diff * (1.0 / d)
        a = dout * g
        dh = r * (a - xh * jnp.mean(a * xh, axis=-1, keepdims=True))
        dh_ref[...] = dh
        dhb_ref[...] = dh.astype(BF16)
        _accumulate(i, dg_ref, jnp.sum(dout * xh, axis=0, keepdims=True))
        _accumulate(i, loss_ref, jnp.broadcast_to(part, (8, 128)))

    row = pl.BlockSpec((tm, d), lambda i: (i, 0))
    vec = pl.BlockSpec((1, d), lambda i: (0, 0))
    outs, _ = _call(
        body, grid=(s // tm,),
        in_specs=[pl.BlockSpec((tm, k), lambda i: (i, 0)), pl.BlockSpec((k, d), lambda i: (0, 0)), row, row, vec],
        out_specs=[pl.BlockSpec((8, 128), lambda i: (0, 0)), row, row, vec],
        out_shape=[jax.ShapeDtypeStruct((8, 128), F32), jax.ShapeDtypeStruct((s, d), F32),
                   jax.ShapeDtypeStruct((s, d), BF16), jax.ShapeDtypeStruct((1, d), F32)],
        args=(y, w, res, tgt, gain), name=name)
    return outs


def _mm_nt_rms_bwd(terms, x, dres, *, name, tm, side=None):
    s, d = x.shape
    tm = _row_tile(s, tm)
    nt = len(terms)

    def body(*refs):
        a_refs, b_refs, g_refs = refs[0:3 * nt:3], refs[1:3 * nt:3], refs[2:3 * nt:3]
        x_ref, dres_ref = refs[3 * nt], refs[3 * nt + 1]
        dx_ref, dxb_ref = refs[3 * nt + 2], refs[3 * nt + 3]
        dg_refs = refs[3 * nt + 4:]
        i = pl.program_id(0)
        xv = x_ref[...]
        r = _rstd(xv)
        xh = xv * r
        acc = jnp.zeros_like(xv)
        for t in range(nt):
            b_ref = b_refs[t]
            if len(b_ref.shape) == 3:
                kc = b_ref.shape[2]
                dn = None
                for sh in range(b_ref.shape[0]):
                    part = lax.dot_general(a_refs[t][:, sh * kc:(sh + 1) * kc], b_ref[sh], NT, preferred_element_type=F32)
                    dn = part if dn is None else dn + part
            else:
                dn = lax.dot_general(a_refs[t][...], b_ref[...], NT, preferred_element_type=F32)
            acc = acc + dn * g_refs[t][...]
            _accumulate(i, dg_refs[t], jnp.sum(dn * xh, axis=0, keepdims=True))
        dx = dres_ref[...] + r * (acc - xh * jnp.mean(acc * xh, axis=-1, keepdims=True))
        dx_ref[...] = dx
        dxb_ref[...] = dx.astype(BF16)

    row = pl.BlockSpec((tm, d), lambda i: (i, 0))
    vec = pl.BlockSpec((1, d), lambda i: (0, 0))
    in_specs, args = [], []
    for a, b, g in terms:
        in_specs += [pl.BlockSpec((tm, a.shape[1]), lambda i: (i, 0)),
                     pl.BlockSpec(b.shape, (lambda i: (0, 0, 0)) if b.ndim == 3 else (lambda i: (0, 0))), vec]
        args += [a, b, g]
    return _call(
        body, grid=(s // tm,), in_specs=in_specs + [row, row], out_specs=[row, row] + [vec] * nt,
        out_shape=[jax.ShapeDtypeStruct((s, d), F32), jax.ShapeDtypeStruct((s, d), BF16)]
        + [jax.ShapeDtypeStruct((1, d), F32)] * nt,
        args=(*args, x, dres), name=name, side=side)


def _causal_mask(transposed=False):
    row = lax.broadcasted_iota(jnp.int32, (CHUNK, CHUNK), 0)
    col = lax.broadcasted_iota(jnp.int32, (CHUNK, CHUNK), 1)
    return col >= row if transposed else col <= row


def _silu_parts(g):
    sg = jax.nn.sigmoid(g)
    return g * sg, sg * (1.0 + g * (1.0 - sg))


def _gate_fwd(z, ln_g, ln_b, ws, bs_t, *, tr=512, side=None):
    s = z.shape[0]
    tr = _row_tile(s, tr)
    w = A_WIDTH

    def body(u_ref, v_ref, g_ref, lg_ref, lb_ref, ws_ref, bst_ref, y_ref):
        v = v_ref[...].astype(F32)
        mu = jnp.mean(v, axis=-1, keepdims=True)
        xc = v - mu
        rs = lax.rsqrt(jnp.mean(xc * xc, axis=-1, keepdims=True) + EPS)
        vln = (xc * rs * lg_ref[...] + lb_ref[...]).astype(BF16)
        mask = _causal_mask()
        for grp in range(A_GROUPS):
            cols = slice(grp * CHUNK, (grp + 1) * CHUNK)
            wsm = jnp.where(mask, ws_ref[grp], 0.0).astype(BF16)
            bcol = bst_ref[:, grp:grp + 1]
            for ci in range(tr // CHUNK):
                rows = slice(ci * CHUNK, (ci + 1) * CHUNK)
                sv = jnp.dot(wsm, vln[rows, cols], preferred_element_type=F32) + bcol
                gv = g_ref[rows, cols].astype(F32)
                y_ref[rows, cols] = (u_ref[rows, cols].astype(F32) * sv * (gv * jax.nn.sigmoid(gv))).astype(BF16)

    vec = pl.BlockSpec((1, w), lambda i: (0, 0))
    (y,), side_outs = _call(
        body, grid=(s // tr,),
        in_specs=[pl.BlockSpec((tr, w), lambda i: (i, 0)), pl.BlockSpec((tr, w), lambda i: (i, 1)),
                  pl.BlockSpec((tr, w), lambda i: (i, 2)), vec, vec,
                  pl.BlockSpec((A_GROUPS, CHUNK, CHUNK), lambda i: (0, 0, 0)),
                  pl.BlockSpec((CHUNK, A_GROUPS), lambda i: (0, 0))],
        out_specs=[pl.BlockSpec((tr, w), lambda i: (i, 0))],
        out_shape=[jax.ShapeDtypeStruct((s, w), BF16)], args=(z, z, z, ln_g, ln_b, ws, bs_t), name="gate_fwd",
        side=side)
    return y, side_outs


def _gate_bwd(z, dy, ln_g, ln_b, ws, ws_t, bs_t, *, tr=256, side=None):
    s = z.shape[0]
    tr = _row_tile(s, tr)
    w = A_WIDTH
    nsteps = s // tr

    def body(u_ref, v_ref, g_ref, dy_ref, lg_ref, lb_ref, ws_ref, wst_ref, bst_ref,
             dz_ref, dlg_ref, dlb_ref, dws_ref, dbst_ref, dvln_sc, dsv_sc):
        i = pl.program_id(0)

        @pl.when(i == 0)
        def _():
            dws_ref[...] = jnp.zeros_like(dws_ref)
            dsv_sc[...] = jnp.zeros_like(dsv_sc)

        v = v_ref[...].astype(F32)
        mu = jnp.mean(v, axis=-1, keepdims=True)
        xc = v - mu
        rs = lax.rsqrt(jnp.mean(xc * xc, axis=-1, keepdims=True) + EPS)
        xh = xc * rs
        lg = lg_ref[...]
        vln = (xh * lg + lb_ref[...]).astype(BF16)
        mask = _causal_mask()
        mask_t = _causal_mask(transposed=True)
        for grp in range(A_GROUPS):
            cols = slice(grp * CHUNK, (grp + 1) * CHUNK)
            wsm = jnp.where(mask, ws_ref[grp], 0.0).astype(BF16)
            wsm_t = jnp.where(mask_t, wst_ref[grp], 0.0).astype(BF16)
            bcol = bst_ref[:, grp:grp + 1]
            for ci in range(tr // CHUNK):
                rows = slice(ci * CHUNK, (ci + 1) * CHUNK)
                vb = vln[rows, cols]
                sv = jnp.dot(wsm, vb, preferred_element_type=F32) + bcol
                uv = u_ref[rows, cols].astype(F32)
                silu, dsilu = _silu_parts(g_ref[rows, cols].astype(F32))
                dyv = dy_ref[rows, cols].astype(F32)
                dyu = dyv * uv
                dz_ref[rows, cols] = (dyv * sv * silu).astype(BF16)
                dz_ref[rows, 2 * w + grp * CHUNK:2 * w + (grp + 1) * CHUNK] = (dyu * sv * dsilu).astype(BF16)
                dsv = dyu * silu
                dsvb = dsv.astype(BF16)
                dvln_sc[rows, cols] = jnp.dot(wsm_t, dsvb, preferred_element_type=F32)
                dws_ref[grp] += lax.dot_general(dsvb, vb, NT, preferred_element_type=F32)
                dsv_sc[grp] += dsv
        dvln = dvln_sc[...]
        dlg_t = jnp.sum(dvln * xh, axis=0, keepdims=True)
        dlb_t = jnp.sum(dvln, axis=0, keepdims=True)
        a = dvln * lg
        dv = rs * (a - jnp.mean(a, axis=-1, keepdims=True) - xh * jnp.mean(a * xh, axis=-1, keepdims=True))
        dz_ref[:, w:2 * w] = dv.astype(BF16)

        @pl.when(i == 0)
        def _():
            dlg_ref[...] = dlg_t
            dlb_ref[...] = dlb_t

        @pl.when(i > 0)
        def _():
            dlg_ref[...] += dlg_t
            dlb_ref[...] += dlb_t

        @pl.when(i == nsteps - 1)
        def _():
            for grp in range(A_GROUPS):
                dws_ref[grp] = jnp.where(mask, dws_ref[grp], 0.0)
                dbst_ref[:, grp:grp + 1] = jnp.sum(dsv_sc[grp], axis=-1, keepdims=True)

    vec = pl.BlockSpec((1, w), lambda i: (0, 0))
    wsspec = pl.BlockSpec((A_GROUPS, CHUNK, CHUNK), lambda i: (0, 0, 0))
    bsspec = pl.BlockSpec((CHUNK, A_GROUPS), lambda i: (0, 0))
    return _call(
        body, grid=(nsteps,),
        in_specs=[pl.BlockSpec((tr, w), lambda i: (i, 0)), pl.BlockSpec((tr, w), lambda i: (i, 1)),
                  pl.BlockSpec((tr, w), lambda i: (i, 2)), pl.BlockSpec((tr, w), lambda i: (i, 0)),
                  vec, vec, wsspec, wsspec, bsspec],
        out_specs=[pl.BlockSpec((tr, 3 * w), lambda i: (i, 0)), vec, vec, wsspec, bsspec],
        out_shape=[jax.ShapeDtypeStruct((s, 3 * w), BF16), jax.ShapeDtypeStruct((1, w), F32),
                   jax.ShapeDtypeStruct((1, w), F32), jax.ShapeDtypeStruct((A_GROUPS, CHUNK, CHUNK), F32),
                   jax.ShapeDtypeStruct((CHUNK, A_GROUPS), F32)],
        scratch=[pltpu.VMEM((tr, w), F32), pltpu.VMEM((A_GROUPS, CHUNK, CHUNK), F32)],
        args=(z, z, z, dy, ln_g, ln_b, ws, ws_t, bs_t), name="gate_bwd", side=side)


HEADS_PER_BLOCK = 128 // HEAD_DIM
BLOCKS_PER_KV = Q_PER_KV // HEADS_PER_BLOCK
SCALE = HEAD_DIM ** -0.5
LOG2_E = math.log2(math.e)


def _rope_tables(s):
    lane = jnp.arange(128)
    inv_freq = ROPE_THETA ** (-(2 * (lane % (HEAD_DIM // 2))).astype(F32) / HEAD_DIM)
    sign = jnp.where(lane % HEAD_DIM < HEAD_DIM // 2, -1.0, 1.0).astype(F32)
    ang = jnp.arange(s, dtype=F32)[:, None] * inv_freq[None, :]
    return jnp.cos(ang), jnp.sin(ang) * sign[None, :]


def _swap_halves(x):
    n = x.shape[-1]
    lane = lax.broadcasted_iota(jnp.int32, x.shape, x.ndim - 1)
    first = (lane % HEAD_DIM) < (HEAD_DIM // 2)
    return jnp.where(first, pltpu.roll(x, n - HEAD_DIM // 2, x.ndim - 1), pltpu.roll(x, HEAD_DIM // 2, x.ndim - 1))


def _left_half(rows):
    return lax.broadcasted_iota(jnp.int32, (rows, 128), 1) < HEAD_DIM


def _dup_heads(x):
    left = _left_half(x.shape[0])
    swapped = pltpu.roll(x, HEAD_DIM, 1)
    return jnp.concatenate([jnp.where(left, x, swapped), jnp.where(left, swapped, x)], axis=-1)


def _fold_heads(a):
    b0, b1 = a[:, :128], a[:, 128:]
    f0 = b0 + pltpu.roll(b0, HEAD_DIM, 1)
    f1 = b1 + pltpu.roll(b1, HEAD_DIM, 1)
    return jnp.where(_left_half(a.shape[0]), f0, f1)


def _kv_rope(n_kv, w_kv, b_kv, cos, sin, *, tr=2048):
    s, d = n_kv.shape
    tr = _row_tile(s, tr)

    def body(n_ref, w_ref, b_ref, c_ref, s_ref, k_ref, v_ref):
        x = jnp.dot(n_ref[...], w_ref[...], preferred_element_type=F32) + b_ref[...]
        k = x[:, :KV_WIDTH]
        k_ref[...] = _dup_heads(k * c_ref[...] + _swap_halves(k) * s_ref[...]).astype(BF16)
        v_ref[...] = _dup_heads(x[:, KV_WIDTH:]).astype(BF16)

    tab = pl.BlockSpec((tr, KV_WIDTH), lambda i: (i, 0))
    wide = pl.BlockSpec((tr, 2 * KV_WIDTH), lambda i: (i, 0))
    outs, _ = _call(body, grid=(s // tr,),
                    in_specs=[pl.BlockSpec((tr, d), lambda i: (i, 0)), pl.BlockSpec((d, 2 * KV_WIDTH), lambda i: (0, 0)),
                              pl.BlockSpec((1, 2 * KV_WIDTH), lambda i: (0, 0)), tab, tab],
                    out_specs=[wide, wide], out_shape=[jax.ShapeDtypeStruct((s, 2 * KV_WIDTH), BF16)] * 2,
                    args=(n_kv, w_kv, b_kv, cos, sin), name="kv_rope")
    return outs


def _kv_rope_bwd(dk2, dv2, cos, sin, *, tr=2048):
    s = dk2.shape[0]
    tr = _row_tile(s, tr)

    def body(dk_ref, dv_ref, c_ref, s_ref, dkv_ref, db_ref):
        i = pl.program_id(0)
        d = _fold_heads(dk_ref[...])
        dk = d * c_ref[...] + _swap_halves(d * s_ref[...])
        dvv = _fold_heads(dv_ref[...])
        dkv_ref[:, :KV_WIDTH] = dk.astype(BF16)
        dkv_ref[:, KV_WIDTH:] = dvv.astype(BF16)
        sk = jnp.sum(dk, axis=0, keepdims=True)
        sv = jnp.sum(dvv, axis=0, keepdims=True)

        @pl.when(i == 0)
        def _():
            db_ref[:, :KV_WIDTH] = sk
            db_ref[:, KV_WIDTH:] = sv

        @pl.when(i > 0)
        def _():
            db_ref[:, :KV_WIDTH] += sk
            db_ref[:, KV_WIDTH:] += sv

    tab = pl.BlockSpec((tr, KV_WIDTH), lambda i: (i, 0))
    wide = pl.BlockSpec((tr, 2 * KV_WIDTH), lambda i: (i, 0))
    outs, _ = _call(body, grid=(s // tr,), in_specs=[wide, wide, tab, tab],
                    out_specs=[wide, pl.BlockSpec((1, 2 * KV_WIDTH), lambda i: (0, 0))],
                    out_shape=[jax.ShapeDtypeStruct((s, 2 * KV_WIDTH), BF16),
                               jax.ShapeDtypeStruct((1, 2 * KV_WIDTH), F32)],
                    args=(dk2, dv2, cos, sin), name="kv_rope_bwd")
    return outs


def _from_previous():
    cols = Q_PER_KV * CHUNK
    k = lax.broadcasted_iota(jnp.int32, (CHUNK, cols), 0)
    q = lax.broadcasted_iota(jnp.int32, (CHUNK, cols), 1) & (CHUNK - 1)
    return k > q


def _fold(x2, prev):
    return jnp.where(prev, x2[:CHUNK], x2[CHUNK:])


def _unfold(x, prev):
    zero = jnp.zeros_like(x)
    return jnp.concatenate([jnp.where(prev, x, zero), jnp.where(prev, zero, x)], axis=0)


def _stack_heads(blocks, left):
    parts = []
    for b in blocks:
        parts.append(jnp.where(left, b, jnp.zeros_like(b)))
        parts.append(jnp.where(left, jnp.zeros_like(b), b))
    return jnp.concatenate(parts, axis=0)


def _unstack_heads(xt):
    top = lax.broadcasted_iota(jnp.int32, (128, CHUNK), 0) < HEAD_DIM
    return [jnp.where(top, xt[:, (2 * b) * CHUNK:(2 * b + 1) * CHUNK], xt[:, (2 * b + 1) * CHUNK:(2 * b + 2) * CHUNK]).T
            for b in range(BLOCKS_PER_KV)]


def _sink_row(sk_ref, kvh):
    return jnp.concatenate([jnp.full((1, CHUNK), sk_ref[0, kvh * Q_PER_KV + r], F32) for r in range(Q_PER_KV)], axis=1)


def _stacked_probs(qs, kd, prev, sink, i):
    sc2 = lax.dot_general(kd, qs, NT, preferred_element_type=F32)
    no_previous = jnp.where(i > 0, 0.0, NEG_BIG)
    sc = jnp.where(prev, sc2[:CHUNK] + no_previous, sc2[CHUNK:])
    sink = sink * (1.0 / SCALE)
    m = jnp.maximum(jnp.max(sc, axis=0, keepdims=True), sink)
    p = jnp.exp2((sc - m) * (SCALE * LOG2_E))
    esink = jnp.exp2((sink - m) * (SCALE * LOG2_E))
    inv = 1.0 / (jnp.sum(p, axis=0, keepdims=True) + esink)
    return p * inv, esink * inv


def _lane_block(b):
    return slice(b * 128, (b + 1) * 128)


def _rope_blocks(zq_ref, bq_ref, cos, sin, kvh, rows):
    out = []
    for b in range(BLOCKS_PER_KV):
        cols = _lane_block(kvh * BLOCKS_PER_KV + b)
        q = zq_ref[rows, cols].astype(F32) + bq_ref[:, cols]
        out.append((q * cos + _swap_halves(q) * sin).astype(BF16))
    return out


CHUNKS_PER_STEP = 8


def _attn_specs():
    rows = CHUNKS_PER_STEP * CHUNK
    qspec = pl.BlockSpec((rows, B_WIDTH), lambda i: (i, 0))
    gspec = pl.BlockSpec((rows, B_WIDTH), lambda i: (i, 1))
    prev = pl.BlockSpec((CHUNK, 2 * KV_WIDTH), lambda i: (jnp.maximum(CHUNKS_PER_STEP * i - 1, 0), 0))
    cur = pl.BlockSpec((rows, 2 * KV_WIDTH), lambda i: (i, 0))
    tab = pl.BlockSpec((rows, KV_WIDTH), lambda i: (i, 0))
    bq = pl.BlockSpec((1, B_WIDTH), lambda i: (0, 0))
    sinks = pl.BlockSpec(memory_space=pltpu.SMEM)
    return qspec, gspec, prev, cur, tab, bq, sinks


def _chunk_keys(prev_ref, cur_ref, sub):
    before = prev_ref[...] if sub == 0 else cur_ref[(sub - 1) * CHUNK:sub * CHUNK]
    return jnp.concatenate([before, cur_ref[sub * CHUNK:(sub + 1) * CHUNK]], axis=0)


def _attn_fwd(zb, k2, v2, cos, sin, b_bq, sinks, *, side=None):
    s = zb.shape[0]

    def body(zq_ref, zg_ref, kp_ref, kc_ref, vp_ref, vc_ref, c_ref, s_ref, bq_ref, sk_ref, y_ref):
        prev = _from_previous()
        left = _left_half(CHUNK)
        for sub in range(CHUNKS_PER_STEP):
            chunk = CHUNKS_PER_STEP * pl.program_id(0) + sub
            rows = slice(sub * CHUNK, (sub + 1) * CHUNK)
            cos, sin = c_ref[rows, :], s_ref[rows, :]
            kcat, vcat = _chunk_keys(kp_ref, kc_ref, sub), _chunk_keys(vp_ref, vc_ref, sub)
            for kvh in range(N_KV_HEADS):
                qs = _stack_heads(_rope_blocks(zq_ref, bq_ref, cos, sin, kvh, rows), left)
                p, _ = _stacked_probs(qs, kcat[:, _lane_block(kvh)], prev, _sink_row(sk_ref, kvh), chunk)
                ot = lax.dot_general(vcat[:, _lane_block(kvh)], _unfold(p, prev).astype(BF16), TN,
                                     preferred_element_type=F32)
                for b, ob in enumerate(_unstack_heads(ot)):
                    cols = _lane_block(kvh * BLOCKS_PER_KV + b)
                    gv = zg_ref[rows, cols].astype(F32)
                    y_ref[rows, cols] = (ob * (gv * jax.nn.sigmoid(gv))).astype(BF16)

    qspec, gspec, prev, cur, tab, bq, sk = _attn_specs()
    (y,), side_outs = _call(body, grid=(s // (CHUNKS_PER_STEP * CHUNK),),
                            in_specs=[qspec, gspec, prev, cur, prev, cur, tab, tab, bq, sk],
                            out_specs=[qspec], out_shape=[jax.ShapeDtypeStruct((s, B_WIDTH), BF16)],
                            args=(zb, zb, k2, k2, v2, v2, cos, sin, b_bq, sinks), name="attn_fwd", side=side)
    return y, side_outs


def _attn_bwd(zb, dyb, k2, v2, cos, sin, b_bq, sinks, *, side=None):
    s = zb.shape[0]

    def body(zq_ref, zg_ref, dy_ref, kp_ref, kc_ref, vp_ref, vc_ref, c_ref, s_ref, bq_ref, sk_ref,
             dz_ref, dk_ref, dv_ref, dbq_ref, dsk_ref):
        i = pl.program_id(0)

        @pl.when(i == 0)
        def _():
            dk_ref[...] = jnp.zeros_like(dk_ref)
            dv_ref[...] = jnp.zeros_like(dv_ref)
            dbq_ref[...] = jnp.zeros_like(dbq_ref)
            dsk_ref[...] = jnp.zeros_like(dsk_ref)

        prev = _from_previous()
        left = _left_half(CHUNK)
        lane = lax.broadcasted_iota(jnp.int32, (1, 128), 1)
        dsk_row = jnp.zeros((1, 128), F32)
        for sub in range(CHUNKS_PER_STEP):
            chunk = CHUNKS_PER_STEP * i + sub
            rows = slice(sub * CHUNK, (sub + 1) * CHUNK)
            cos, sin = c_ref[rows, :], s_ref[rows, :]
            kcat, vcat = _chunk_keys(kp_ref, kc_ref, sub), _chunk_keys(vp_ref, vc_ref, sub)
            cur_rows = pl.ds(pl.multiple_of(chunk * CHUNK, CHUNK), CHUNK)
            for kvh in range(N_KV_HEADS):
                kd, vd = kcat[:, _lane_block(kvh)], vcat[:, _lane_block(kvh)]
                qs = _stack_heads(_rope_blocks(zq_ref, bq_ref, cos, sin, kvh, rows), left)
                p, psink = _stacked_probs(qs, kd, prev, _sink_row(sk_ref, kvh), chunk)
                pb = _unfold(p, prev).astype(BF16)
                ot = lax.dot_general(vd, pb, TN, preferred_element_type=F32)
                gates, dys = [], []
                for b in range(BLOCKS_PER_KV):
                    cols = _lane_block(kvh * BLOCKS_PER_KV + b)
                    gates.append(_silu_parts(zg_ref[rows, cols].astype(F32)))
                    dys.append(dy_ref[rows, cols].astype(F32))
                dos = _stack_heads([(dyv * silu).astype(BF16) for dyv, (silu, _) in zip(dys, gates)], left)
                dp = _fold(lax.dot_general(vd, dos, NT, preferred_element_type=F32), prev)
                delta = jnp.sum(p * dp, axis=0, keepdims=True)
                ds = _unfold(p * (dp - delta) * SCALE, prev).astype(BF16)
                dqt = lax.dot_general(kd, ds, TN, preferred_element_type=F32)
                dk_part = jnp.dot(ds, qs, preferred_element_type=F32)
                dv_part = jnp.dot(pb, dos, preferred_element_type=F32)
                dk_ref[cur_rows, _lane_block(kvh)] += dk_part[CHUNK:]
                dv_ref[cur_rows, _lane_block(kvh)] += dv_part[CHUNK:]

                @pl.when(chunk > 0)
                def _(kvh=kvh, chunk=chunk, dk_part=dk_part, dv_part=dv_part):
                    prev_rows = pl.ds(pl.multiple_of((chunk - 1) * CHUNK, CHUNK), CHUNK)
                    dk_ref[prev_rows, _lane_block(kvh)] += dk_part[:CHUNK]
                    dv_ref[prev_rows, _lane_block(kvh)] += dv_part[:CHUNK]

                sink_grad = psink * delta
                for r in range(Q_PER_KV):
                    dsink = -jnp.sum(sink_grad[:, r * CHUNK:(r + 1) * CHUNK], axis=1, keepdims=True)
                    dsk_row = dsk_row + jnp.where(lane == kvh * Q_PER_KV + r, dsink, 0.0)
                blocks = zip(_unstack_heads(ot), _unstack_heads(dqt), dys, gates)
                for b, (ob, dqr, dyv, (_, dsilu)) in enumerate(blocks):
                    blk = kvh * BLOCKS_PER_KV + b
                    dq = dqr * cos + _swap_halves(dqr * sin)
                    dbq_ref[:, _lane_block(blk)] += jnp.sum(dq, axis=0, keepdims=True)
                    dz_ref[rows, _lane_block(blk)] = dq.astype(BF16)
                    dz_ref[rows, _lane_block(B_WIDTH // 128 + blk)] = (dyv * ob * dsilu).astype(BF16)
        dsk_ref[0:1, :] += dsk_row

    qspec, gspec, prev, cur, tab, bq, sk = _attn_specs()
    full = pl.BlockSpec((s, 2 * KV_WIDTH), lambda i: (0, 0))
    return _call(
        body, grid=(s // (CHUNKS_PER_STEP * CHUNK),),
        in_specs=[qspec, gspec, qspec, prev, cur, prev, cur, tab, tab, bq, sk],
        out_specs=[pl.BlockSpec((CHUNKS_PER_STEP * CHUNK, 2 * B_WIDTH), lambda i: (i, 0)), full, full, bq,
                   pl.BlockSpec((8, 128), lambda i: (0, 0))],
        out_shape=[jax.ShapeDtypeStruct((s, 2 * B_WIDTH), BF16), jax.ShapeDtypeStruct((s, 2 * KV_WIDTH), F32),
                   jax.ShapeDtypeStruct((s, 2 * KV_WIDTH), F32), jax.ShapeDtypeStruct((1, B_WIDTH), F32),
                   jax.ShapeDtypeStruct((8, 128), F32)],
        args=(zb, zb, dyb, k2, k2, v2, v2, cos, sin, b_bq, sinks), name="attn_bwd", side=side)


def _place():
    x, y, c = lax.axis_index("x"), lax.axis_index("y"), lax.axis_index("c")
    return x, y, c, [(1 - x, y), (x, 1 - y), (1 - x, 1 - y)]


def _relations():
    return [(r >> 2 & 1, r >> 1 & 1, r & 1) for r in range(1, 8)]


def _gather_side(arrs):
    n = len(arrs)

    def copies(ins, outs, sems):
        send_ici, recv_ici, send_d2d, recv_d2d, local_sem = sems
        x, y, c, chips = _place()
        me = 2 * x + y

        def rows(a, half):
            hr = arrs[a].shape[0] // 2
            return pl.ds(half * hr, hr)

        def ici(a, j, src_chip, to):
            return pltpu.make_async_remote_copy(
                src_ref=ins[a].at[rows(a, c)], dst_ref=outs[a].at[src_chip, rows(a, c)],
                send_sem=send_ici.at[a, j], recv_sem=recv_ici.at[a, j], device_id=to, device_id_type=MESH)

        def d2d(a, j, chip, half):
            blk = outs[a].at[chip, rows(a, half)]
            return pltpu.make_async_remote_copy(
                src_ref=blk, dst_ref=blk, send_sem=send_d2d.at[a, j], recv_sem=recv_d2d.at[a, j],
                device_id=(x, y, 1 - c), device_id_type=MESH)

        local = [pltpu.make_async_copy(ins[a], outs[a].at[me], local_sem.at[a]) for a in range(n)]
        pairs = [(a, j, chip) for a in range(n) for j, chip in enumerate(chips)]
        return c, me, local, ici, d2d, pairs

    def start(ins, outs, sems):
        c, me, local, ici, _, pairs = copies(ins, outs, sems)
        for cp in local:
            cp.start()
        for a, j, chip in pairs:
            ici(a, j, me, (*chip, c)).start()

    def passing(ins, outs, sems):
        c, _, _, ici, d2d, pairs = copies(ins, outs, sems)
        for a, j, (px, py) in pairs:
            ici(a, j, 2 * px + py, (px, py, c)).wait_recv()
            d2d(a, j, 2 * px + py, c).start()

    def finish(ins, outs, sems):
        c, me, local, ici, d2d, pairs = copies(ins, outs, sems)
        for a, j, (px, py) in pairs:
            d2d(a, j, 2 * px + py, 1 - c).wait_recv()
        for a, j, (px, py) in pairs:
            ici(a, j, me, (px, py, c)).wait_send()
            d2d(a, j, 2 * px + py, c).wait_send()
        for cp in local:
            cp.wait()

    return _Side(arrs, [jax.ShapeDtypeStruct((N_CHIPS,) + a.shape, a.dtype) for a in arrs],
                 [pltpu.SemaphoreType.DMA((n, 3))] * 4 + [pltpu.SemaphoreType.DMA((n,))], start, finish,
                 passing=passing)


def _exchange_side(grads):
    n = len(grads)

    def copies(ins, outs, sems):
        send_sem, recv_sem = sems
        x, y, c, _ = _place()
        cps = []
        for a in range(n):
            hr = grads[a].shape[1] // 2
            cps.append(pltpu.make_async_remote_copy(
                src_ref=ins[a].at[:, pl.ds((1 - c) * hr, hr), :], dst_ref=outs[a],
                send_sem=send_sem.at[a], recv_sem=recv_sem.at[a], device_id=(x, y, 1 - c), device_id_type=MESH))
        return cps

    def start(ins, outs, sems):
        for cp in copies(ins, outs, sems):
            cp.start()

    def finish(ins, outs, sems):
        for cp in copies(ins, outs, sems):
            cp.wait()

    return _Side(grads, [jax.ShapeDtypeStruct((g.shape[0], g.shape[1] // 2, g.shape[2]), g.dtype) for g in grads],
                 [pltpu.SemaphoreType.DMA((n,))] * 2, start, finish)


def _scatter_side(chip_sums, small=None):
    n = len(chip_sums)
    arrs = list(chip_sums) + ([small] if small is not None else [])

    def copies(ins, outs, sems):
        x, y, c, chips = _place()
        cps = []
        for a in range(n):
            for j, (px, py) in enumerate(chips):
                cps.append(pltpu.make_async_remote_copy(
                    src_ref=ins[a].at[2 * px + py], dst_ref=outs[a].at[j],
                    send_sem=sems[0].at[a, j], recv_sem=sems[1].at[a, j], device_id=(px, py, c), device_id_type=MESH))
        if small is not None:
            for r, (fx, fy, fc) in enumerate(_relations(), start=1):
                px, py, pc = x ^ fx, y ^ fy, c ^ fc
                cps.append(pltpu.make_async_remote_copy(
                    src_ref=ins[n].at[4 * px + 2 * py + pc], dst_ref=outs[n].at[r],
                    send_sem=sems[2].at[r - 1], recv_sem=sems[3].at[r - 1], device_id=(px, py, pc),
                    device_id_type=MESH))
        return cps

    def start(ins, outs, sems):
        for cp in copies(ins, outs, sems):
            cp.start()

    def finish(ins, outs, sems):
        for cp in copies(ins, outs, sems):
            cp.wait()

    shapes = [jax.ShapeDtypeStruct((3,) + t.shape[1:], t.dtype) for t in chip_sums]
    sems = [pltpu.SemaphoreType.DMA((n, 3))] * 2
    if small is not None:
        shapes.append(jax.ShapeDtypeStruct(small.shape, small.dtype))
        sems += [pltpu.SemaphoreType.DMA((7,))] * 2
    return _Side(arrs, shapes, sems, start, finish)


def _small_scatter_side(small):
    def copies(ins, outs, sems):
        x, y, c, _ = _place()
        cps = []
        for r, (fx, fy, fc) in enumerate(_relations(), start=1):
            px, py, pc = x ^ fx, y ^ fy, c ^ fc
            cps.append(pltpu.make_async_remote_copy(
                src_ref=ins[0].at[4 * px + 2 * py + pc], dst_ref=outs[0].at[r],
                send_sem=sems[0].at[r - 1], recv_sem=sems[1].at[r - 1], device_id=(px, py, pc), device_id_type=MESH))
        return cps

    def start(ins, outs, sems):
        for cp in copies(ins, outs, sems):
            cp.start()

    def finish(ins, outs, sems):
        for cp in copies(ins, outs, sems):
            cp.wait()

    return _Side([small], [jax.ShapeDtypeStruct(small.shape, small.dtype)], [pltpu.SemaphoreType.DMA((7,))] * 2,
                 start, finish)


def _small_share_side(small):
    return _share_side([], small)


def _share_side(halves, small=None):
    n = len(halves)
    arrs = list(halves) + ([small] if small is not None else [])

    def copies(ins, outs, sems, mine):
        x, y, c, _ = _place()
        me = 4 * x + 2 * y + c
        cps = []
        for a in range(n):
            hr = halves[a].shape[0] // 2
            rows = pl.ds((c if mine else 1 - c) * hr, hr)
            cps.append(pltpu.make_async_remote_copy(
                src_ref=ins[a].at[rows], dst_ref=outs[a].at[rows],
                send_sem=sems[0].at[a], recv_sem=sems[1].at[a], device_id=(x, y, 1 - c), device_id_type=MESH))
        if small is not None:
            for r, (fx, fy, fc) in enumerate(_relations(), start=1):
                px, py, pc = x ^ fx, y ^ fy, c ^ fc
                seg = me if mine else 4 * px + 2 * py + pc
                cps.append(pltpu.make_async_remote_copy(
                    src_ref=ins[n].at[seg], dst_ref=outs[n].at[seg],
                    send_sem=sems[-2].at[r - 1], recv_sem=sems[-1].at[r - 1], device_id=(px, py, pc),
                    device_id_type=MESH))
        return cps

    def start(ins, outs, sems):
        for cp in copies(ins, outs, sems, True):
            cp.start()

    def finish(ins, outs, sems):
        for cp in copies(ins, outs, sems, False):
            cp.wait_recv()
        for cp in copies(ins, outs, sems, True):
            cp.wait_send()

    sems = ([pltpu.SemaphoreType.DMA((n,))] * 2 if n else []) + (
        [pltpu.SemaphoreType.DMA((7,))] * 2 if small is not None else [])
    return _Side(arrs, [jax.ShapeDtypeStruct(h.shape, h.dtype) for h in arrs], sems, start, finish,
                 aliases={i: i for i in range(len(arrs))})


GATHER_PIECES = [(0, 0), (0, 1), (1, 0), (2, 0), (1, 1), (2, 1), (3, 0), (3, 1)]


def _mm_gathering(a, shard, order, *, name, tm=2048):
    s, k = a.shape
    nc = shard.shape[1]
    tm = _row_tile(s, tm)
    tn = nc // 2
    hr = k // 2
    qr = hr // 2
    blocks = jnp.stack([order[src] * 2 + h for src, h in GATHER_PIECES]).astype(jnp.int32)

    def body(blocks_ref, a_ref, shard_ref, z_ref, full_ref, wbuf, send_ici, recv_ici, send_relay,
             recv_relay, send_d2d, recv_d2d, local_sem, load_sem):
        piece, i = pl.program_id(0), pl.program_id(1)
        x, y, c, chips = _place()
        me = 2 * x + y
        nbrs = chips[:2]
        chip_of = [2 * px + py for px, py in chips]

        def quarter(q):
            return pl.ds(c * hr + q * qr, qr)

        def sibling_quarter(q):
            return pl.ds((1 - c) * hr + q * qr, qr)

        def whole(half):
            return pl.ds(half * hr, hr)

        def cols(h):
            return pl.ds(h * tn, tn)

        def direct(j, src_chip, h):
            return pltpu.make_async_remote_copy(
                src_ref=shard_ref.at[whole(c), cols(h)], dst_ref=full_ref.at[src_chip, whole(c), cols(h)],
                send_sem=send_ici.at[j, h], recv_sem=recv_ici.at[j, h], device_id=(*nbrs[j], c), device_id_type=MESH)

        def relay(j, src_chip, h):
            blk = full_ref.at[src_chip, quarter(j), cols(h)]
            return pltpu.make_async_remote_copy(
                src_ref=blk, dst_ref=blk, send_sem=send_relay.at[j, h], recv_sem=recv_relay.at[j, h],
                device_id=(*nbrs[1 - j], c), device_id_type=MESH)

        def d2d(j, chip, rows, h):
            blk = full_ref.at[chip, rows, cols(h)]
            return pltpu.make_async_remote_copy(
                src_ref=blk, dst_ref=blk, send_sem=send_d2d.at[j, h], recv_sem=recv_d2d.at[j, h],
                device_id=(x, y, 1 - c), device_id_type=MESH)

        def load(p):
            src, h = GATHER_PIECES[p]
            where = shard_ref if src == 0 else full_ref.at[chip_of[src - 1]]
            return pltpu.make_async_copy(where.at[:, cols(h)], wbuf.at[p % 2], load_sem.at[p % 2])

        local = pltpu.make_async_copy(shard_ref, full_ref.at[me], local_sem)

        def arrived(p):
            src, h = GATHER_PIECES[p]
            if src in (1, 2):
                j = src - 1
                direct(j, chip_of[j], h).wait_recv()
                relay(j, chip_of[j], h).start()
                d2d(j, chip_of[j], whole(c), h).start()
            elif src == 3:
                for j in range(2):
                    relay(1 - j, chip_of[2], h).wait_recv()
                    d2d(2 + j, chip_of[2], quarter(1 - j), h).start()

        def fetch(p):
            src, h = GATHER_PIECES[p]
            if src in (1, 2):
                d2d(src - 1, chip_of[src - 1], whole(1 - c), h).wait_recv()
            elif src == 3:
                for j in range(2):
                    d2d(2 + j, chip_of[2], sibling_quarter(1 - j), h).wait_recv()
            load(p).start()

        n_i = s // tm
        for p in range(len(GATHER_PIECES)):
            @pl.when(jnp.logical_and(piece == p, i == 0))
            def _(p=p):
                if p == 0:
                    local.start()
                    for hh in range(2):
                        for j in range(2):
                            direct(j, me, hh).start()
                    load(0).start()
                load(p).wait()

        z_ref[...] = jnp.dot(a_ref[...], wbuf[piece % 2], preferred_element_type=F32).astype(z_ref.dtype)

        for p in range(len(GATHER_PIECES) - 1):
            @pl.when(jnp.logical_and(piece == p, i == min(1, n_i - 1)))
            def _(p=p):
                arrived(p + 1)

            @pl.when(jnp.logical_and(piece == p, i == min(2, n_i - 1)))
            def _(p=p):
                fetch(p + 1)

        last = jnp.logical_and(piece == len(GATHER_PIECES) - 1, i == n_i - 1)

        @pl.when(last)
        def _():
            for h in range(2):
                for j in range(2):
                    direct(j, me, h).wait_send()
                    relay(j, chip_of[j], h).wait_send()
                    d2d(j, chip_of[j], whole(c), h).wait_send()
                    d2d(2 + j, chip_of[2], quarter(1 - j), h).wait_send()
            local.wait()

    return pl.pallas_call(
        body,
        grid_spec=pltpu.PrefetchScalarGridSpec(
            num_scalar_prefetch=1, grid=(len(GATHER_PIECES), s // tm),
            in_specs=[pl.BlockSpec((tm, k), lambda p, i, blocks: (i, 0)), HBM],
            out_specs=[pl.BlockSpec((tm, tn), lambda p, i, blocks: (i, blocks[p])), HBM],
            scratch_shapes=[pltpu.VMEM((2, k, tn), BF16)] + [pltpu.SemaphoreType.DMA((2, 2))] * 4
            + [pltpu.SemaphoreType.DMA((4, 2))] * 2 + [pltpu.SemaphoreType.DMA, pltpu.SemaphoreType.DMA((2,))]),
        out_shape=[jax.ShapeDtypeStruct((s, N_CHIPS * nc), BF16), jax.ShapeDtypeStruct((N_CHIPS, k, nc), BF16)],
        name=name, compiler_params=_cparams(),
    )(blocks, a, shard)


def _mm_tn_exchanging(a, b, *, name, shards, tk=2048, side=None):
    s, m = a.shape
    nc = b.shape[1] // shards
    tk = _row_tile(s, tk)
    nk = s // tk
    hm = m // 2

    def body(a_ref, b_ref, part_ref, sib_ref, acc, keep_sem, send_sem, recv_sem):
        j, kk = pl.program_id(0), pl.program_id(1)
        x, y, c, _ = _place()

        def keep(jj, slot):
            mine = pl.ds(c * hm, hm)
            return pltpu.make_async_copy(acc.at[slot, mine], part_ref.at[jj], keep_sem.at[slot])

        def give(jj, slot):
            return pltpu.make_async_remote_copy(
                src_ref=acc.at[slot, pl.ds((1 - c) * hm, hm)], dst_ref=sib_ref.at[jj],
                send_sem=send_sem.at[slot], recv_sem=recv_sem.at[jj], device_id=(x, y, 1 - c), device_id_type=MESH)

        part = lax.dot_general(a_ref[...], b_ref[...], TN, preferred_element_type=F32)
        for slot in range(2):
            @pl.when(j % 2 == slot)
            def _(slot=slot):
                @pl.when(jnp.logical_and(kk == 0, j >= 2))
                def _():
                    keep(j - 2, slot).wait()
                    give(j - 2, slot).wait_send()

                @pl.when(kk == 0)
                def _():
                    acc[slot] = part

                @pl.when(kk > 0)
                def _():
                    acc[slot] += part

                @pl.when(kk == nk - 1)
                def _():
                    keep(j, slot).start()
                    give(j, slot).start()

        @pl.when(jnp.logical_and(j == shards - 1, kk == nk - 1))
        def _():
            for jj in range(shards - 2, shards):
                keep(jj, jj % 2).wait()
                give(jj, jj % 2).wait_send()
            for jj in range(shards):
                give(jj, jj % 2).wait_recv()

    assert shards >= 2
    return _call(
        body, grid=(shards, nk),
        in_specs=[pl.BlockSpec((tk, m), lambda j, kk: (kk, 0)), pl.BlockSpec((tk, nc), lambda j, kk: (kk, j))],
        out_specs=[HBM, HBM],
        out_shape=[jax.ShapeDtypeStruct((shards, hm, nc), F32), jax.ShapeDtypeStruct((shards, hm, nc), F32)],
        scratch=[pltpu.VMEM((2, m, nc), F32), pltpu.SemaphoreType.DMA((2,)), pltpu.SemaphoreType.DMA((2,)),
                 pltpu.SemaphoreType.DMA((shards,))],
        args=(a, b), name=name, side=side)


def _col_tile(cols):
    return cols if cols <= 2048 else 512


def _add_sibling(grad, recv, core, *, name):
    k, r, c = grad.shape
    hr = r // 2
    tr = min(hr, 256)
    tc = _col_tile(c)
    nrb = hr // tr

    def body(core_ref, g_ref, r_ref, o_ref):
        o_ref[...] = (g_ref[...] + r_ref[...]).astype(BF16)

    return pl.pallas_call(
        body,
        grid_spec=pltpu.PrefetchScalarGridSpec(
            num_scalar_prefetch=1, grid=(k, nrb, c // tc),
            in_specs=[pl.BlockSpec((None, tr, tc), lambda kk, i, j, core: (kk, core[0] * nrb + i, j)),
                      pl.BlockSpec((None, tr, tc), lambda kk, i, j, core: (kk, i, j))],
            out_specs=pl.BlockSpec((None, tr, tc), lambda kk, i, j, core: (kk, i, j))),
        out_shape=jax.ShapeDtypeStruct((k, hr, c), BF16), name=name, compiler_params=_cparams(),
    )(core, grad, recv)


def _sum_chips(grad, from_sibling, recv, place, *, name):
    _, hr, c = from_sibling.shape
    tr = min(hr, 256)
    tc = _col_tile(c)
    nrb = hr // tr

    def body(place_ref, g_ref, s_ref, r0_ref, r1_ref, r2_ref, o_ref):
        own = g_ref[...] + s_ref[...]
        o_ref[...] = ((own + r0_ref[...].astype(F32)) + r1_ref[...].astype(F32)) + r2_ref[...].astype(F32)

    def rspec(j):
        return pl.BlockSpec((None, tr, tc), lambda i, jj, place: (j, i, jj))

    return pl.pallas_call(
        body,
        grid_spec=pltpu.PrefetchScalarGridSpec(
            num_scalar_prefetch=1, grid=(nrb, c // tc),
            in_specs=[pl.BlockSpec((None, tr, tc), lambda i, jj, place: (place[0], place[1] * nrb + i, jj)),
                      pl.BlockSpec((None, tr, tc), lambda i, jj, place: (place[0], i, jj)),
                      rspec(0), rspec(1), rspec(2)],
            out_specs=pl.BlockSpec((tr, tc), lambda i, jj, place: (place[1] * nrb + i, jj))),
        out_shape=jax.ShapeDtypeStruct((2 * hr, c), F32), name=name, compiler_params=_cparams(),
    )(place, grad, from_sibling, recv, recv, recv)


def _add_halves(mine, theirs, *, name, side=None):
    k, hr, c = mine.shape
    tr = min(hr, 256)
    tc = _col_tile(c)

    def body(a_ref, b_ref, o_ref):
        o_ref[...] = (a_ref[...] + b_ref[...]).astype(BF16)

    spec = pl.BlockSpec((None, tr, tc), lambda kk, i, j: (kk, i, j))
    (out,), side_outs = _call(body, grid=(k, hr // tr, c // tc), in_specs=[spec, spec], out_specs=[spec],
                              out_shape=[jax.ShapeDtypeStruct((k, hr, c), BF16)], args=(mine, theirs), name=name,
                              side=side)
    return out, side_outs


def _sum_halves(mine, theirs, recv, place, *, name):
    _, hr, c = mine.shape
    tr = min(hr, 256)
    tc = _col_tile(c)
    nrb = hr // tr

    def body(place_ref, a_ref, b_ref, r0_ref, r1_ref, r2_ref, o_ref):
        own = a_ref[...] + b_ref[...]
        o_ref[...] = ((own + r0_ref[...].astype(F32)) + r1_ref[...].astype(F32)) + r2_ref[...].astype(F32)

    def rspec(j):
        return pl.BlockSpec((None, tr, tc), lambda i, jj, place: (j, i, jj))

    own_spec = pl.BlockSpec((None, tr, tc), lambda i, jj, place: (place[0], i, jj))
    return pl.pallas_call(
        body,
        grid_spec=pltpu.PrefetchScalarGridSpec(
            num_scalar_prefetch=1, grid=(nrb, c // tc),
            in_specs=[own_spec, own_spec, rspec(0), rspec(1), rspec(2)],
            out_specs=pl.BlockSpec((tr, tc), lambda i, jj, place: (place[1] * nrb + i, jj))),
        out_shape=jax.ShapeDtypeStruct((2 * hr, c), F32), name=name, compiler_params=_cparams(),
    )(place, mine, theirs, recv, recv, recv)


def _sum_small(small, recv, place):
    _, sr, _ = small.shape

    def body(place_ref, own_ref, r_ref, o_ref):
        acc = own_ref[...]
        for r in range(1, 8):
            acc = acc + r_ref[r]
        o_ref[...] = acc

    return pl.pallas_call(
        body,
        grid_spec=pltpu.PrefetchScalarGridSpec(
            num_scalar_prefetch=1, grid=(1,),
            in_specs=[pl.BlockSpec((None, sr, 128), lambda i, place: (place[2], 0, 0)),
                      pl.BlockSpec((8, sr, 128), lambda i, place: (0, 0, 0))],
            out_specs=pl.BlockSpec((None, sr, 128), lambda i, place: (place[2], 0, 0))),
        out_shape=jax.ShapeDtypeStruct(small.shape, F32), name="sum_small", compiler_params=_cparams(),
    )(place, small, recv)


def _spread_side(vec):
    def copies(ins, outs, sems):
        x, y, c, _ = _place()
        return [pltpu.make_async_remote_copy(
            src_ref=ins[0], dst_ref=outs[0].at[r], send_sem=sems[0].at[r - 1], recv_sem=sems[1].at[r - 1],
            device_id=(x ^ fx, y ^ fy, c ^ fc), device_id_type=MESH)
            for r, (fx, fy, fc) in enumerate(_relations(), start=1)]

    def start(ins, outs, sems):
        for cp in copies(ins, outs, sems):
            cp.start()

    def finish(ins, outs, sems):
        for cp in copies(ins, outs, sems):
            cp.wait()

    return _Side([vec], [jax.ShapeDtypeStruct((8,) + vec.shape, vec.dtype)], [pltpu.SemaphoreType.DMA((7,))] * 2,
                 start, finish)


def _sum_in_device_order(own, spread, place):
    def body(place_ref, own_ref, r_ref, o_ref):
        me = place_ref[2]
        acc = jnp.zeros_like(own_ref[...])
        for d in range(8):
            slot = jnp.where(me == d, 1, me ^ d)
            acc = acc + jnp.where(me == d, own_ref[...], r_ref[slot])
        o_ref[...] = acc

    return pl.pallas_call(
        body,
        grid_spec=pltpu.PrefetchScalarGridSpec(
            num_scalar_prefetch=1, grid=(1,),
            in_specs=[pl.BlockSpec(own.shape, lambda i, place: (0, 0)),
                      pl.BlockSpec(spread.shape, lambda i, place: (0, 0, 0))],
            out_specs=pl.BlockSpec(own.shape, lambda i, place: (0, 0))),
        out_shape=jax.ShapeDtypeStruct(own.shape, F32), name="sum_in_device_order", compiler_params=_cparams(),
    )(place, own, spread)


def _adamw(w, g, m, v, *, name):
    r, c = w.shape
    tr = 256 if r % 256 == 0 else r
    tc = _col_tile(c)
    bc1 = 1.0 - ADAM_B1 ** ADAM_STEP
    bc2 = 1.0 - ADAM_B2 ** ADAM_STEP

    def body(w_ref, g_ref, m_ref, v_ref, d_ref, nm_ref, nv_ref, gout_ref):
        gv = g_ref[...]
        nm = ADAM_B1 * m_ref[...] + (1.0 - ADAM_B1) * gv
        nv = ADAM_B2 * v_ref[...] + (1.0 - ADAM_B2) * (gv * gv)
        d_ref[...] = -ADAM_LR * ((nm / bc1) / (jnp.sqrt(nv / bc2) + ADAM_EPS) + ADAM_WD * w_ref[...])
        nm_ref[...] = nm
        nv_ref[...] = nv
        gout_ref[...] = gv

    spec = pl.BlockSpec((tr, tc), lambda i, j: (i, j))
    outs, _ = _call(body, grid=(r // tr, c // tc), in_specs=[spec] * 4, out_specs=[spec] * 4,
                    out_shape=[jax.ShapeDtypeStruct((r, c), F32)] * 4, args=(w, g, m, v), name=name)
    return outs


SMALL_ORDER = ["a_ws", "a_bs", "a_norm_g", "a_ln_g", "a_ln_b", "kv_norm_g", "b_kv", "b_norm_g", "b_bq",
               "b_sinks", "final_norm_g"]
SHARDED_SMALL = {"a_norm_g", "a_ln_g", "a_ln_b"}
PACK_TILE = 8 * 128


def _rows128(a):
    flat = a.reshape(-1)
    return jnp.pad(flat, (0, (-flat.shape[0]) % PACK_TILE)).reshape(-1, 128)


def _pack_rows(parts, multiple):
    rows = [_rows128(p) for p in parts]
    total = sum(r.shape[0] for r in rows)
    pad = (-total) % multiple
    if pad:
        rows.append(jnp.zeros((pad, 128), rows[0].dtype))
    return jnp.concatenate(rows, axis=0)


def _unpack_rows(packed, shapes):
    out, row = [], 0
    for shp in shapes:
        size = math.prod(shp)
        nrow = -(-size // PACK_TILE) * 8
        out.append(packed[row:row + nrow].reshape(-1)[:size].reshape(shp))
        row += nrow
    return out


WEIGHTS = ["a_norm_g", "a_w_in", "a_ln_g", "a_ln_b", "a_ws", "a_bs", "a_w_out", "kv_norm_g", "w_kv", "b_kv",
           "b_norm_g", "b_w_in", "b_bq", "b_sinks", "b_w_out", "final_norm_g"]
BIG = ["a_w_in", "a_w_out", "w_kv", "b_w_in", "b_w_out"]


class _Reduction:
    def __init__(self, names, partials, core, place, small=None):
        self.names, self.partials, self.core, self.place, self.small = names, partials, core, place, small

    def exchange_side(self):
        return _exchange_side(self.partials)

    def took_exchange(self, from_sibling):
        self.from_sibling = from_sibling
        self.chip_sums = [_add_sibling(g, r, self.core, name="add_sibling_" + n)
                          for g, r, n in zip(self.partials, from_sibling, self.names)]

    def scatter_side(self):
        return _scatter_side(self.chip_sums, self.small)

    def took_scatter(self, arrived):
        big = arrived[:len(self.names)]
        self.halves = [_sum_chips(g, fs, r, self.place, name="sum_chips_" + n)
                       for g, fs, r, n in zip(self.partials, self.from_sibling, big, self.names)]
        self.small_mine = _sum_small(self.small, arrived[-1], self.place) if self.small is not None else None

    def share_side(self):
        return _share_side(self.halves, self.small_mine)

    def took_share(self, shared):
        self.grads = dict(zip(self.names, shared[:len(self.names)]))
        self.small_full = shared[-1] if self.small is not None else None


def _step(x, loss_target, p, m, v):
    xi, yi, ci = lax.axis_index("x"), lax.axis_index("y"), lax.axis_index("c")
    chip = 2 * xi + yi
    device = 4 * xi + 2 * yi + ci
    core = jnp.reshape(ci, (1,)).astype(jnp.int32)
    place = jnp.stack([chip, ci, device]).astype(jnp.int32)
    x, tgt = x[0], loss_target[0]
    s = x.shape[0]
    cos, sin = _rope_tables(s)

    shard2d = {n: p[n].reshape(p[n].shape[-2:]) for n in BIG}
    shard_bf = {n: shard2d[n].astype(BF16) for n in BIG}
    ws = p["a_ws"][0]
    ws_t = jnp.swapaxes(ws, 1, 2)
    bs_t = p["a_bs"][0].T
    kv_norm_g, b_kv = p["kv_norm_g"].reshape(1, -1), p["b_kv"].reshape(1, -1)
    final_norm_g = p["final_norm_g"].reshape(1, -1)

    vec_shapes = [p[n].shape for n in ("a_norm_g", "a_ln_g", "a_ln_b")]
    vec_pack = _pack_rows([p["a_norm_g"], p["a_ln_g"], p["a_ln_b"]], 16)
    (vec_all,) = _comm_call(_gather_side([vec_pack]), "gather_vectors")
    vecs = [_unpack_rows(vec_all[k], vec_shapes) for k in range(N_CHIPS)]
    a_norm_g, a_ln_g, a_ln_b = (jnp.concatenate([vk[t] for vk in vecs], axis=-1) for t in range(3))

    (n_a,) = _rms_fwd(x, [a_norm_g], name="rms_a")
    order = jnp.stack([chip, 2 * (1 - xi) + yi, 2 * xi + (1 - yi), 2 * (1 - xi) + (1 - yi)]).astype(jnp.int32)
    z, a_w_in = _mm_gathering(n_a, shard_bf["a_w_in"], order, name="mm_a_in")
    y, (a_w_out,) = _gate_fwd(z, a_ln_g, a_ln_b, ws, bs_t, side=_gather_side([shard_bf["a_w_out"]]))
    a_w_out = a_w_out.reshape(A_WIDTH, D_MODEL)
    (h1, n_kv, n_b), (w_kv, b_w_in) = _mm_residual_norms(
        y, a_w_out, x, [kv_norm_g, p["b_norm_g"]], name="mm_a_out",
        side=_gather_side([shard_bf["w_kv"], shard_bf["b_w_in"]]))
    w_kv = w_kv.reshape(D_MODEL, 2 * KV_WIDTH)
    kr, vv = _kv_rope(n_kv, w_kv, b_kv, cos, sin)
    zb = _mm_nn(n_b, b_w_in, name="mm_b_in", tn=512, tm=2048, out_dtype=BF16)
    yb, (b_w_out,) = _attn_fwd(zb, kr, vv, cos, sin, p["b_bq"], p["b_sinks"], side=_gather_side([shard_bf["b_w_out"]]))
    b_w_out = b_w_out.reshape(B_WIDTH, D_MODEL)
    loss_blk, dh2, dh2b, d_final_g = _mm_residual_loss(yb, b_w_out, h1, tgt, final_norm_g, name="mm_b_out")

    d_b_w_out = _mm_tn(yb, dh2b, name="mm_d_b_w_out", tm=B_WIDTH, tn=D_MODEL)
    red_bo = _Reduction(["b_w_out"], [d_b_w_out.reshape(N_CHIPS, B_WIDTH // N_CHIPS, D_MODEL)], core, place)
    dyb, got = _mm_nt(dh2b, b_w_out, name="mm_dyb", tm=1024, out_dtype=BF16, side=red_bo.exchange_side())
    red_bo.took_exchange(got)
    (dzb, dk_rot, dv, d_bq, d_sinks), got = _attn_bwd(zb, dyb, kr, vv, cos, sin, p["b_bq"], p["b_sinks"],
                                                      side=red_bo.scatter_side())
    red_bo.took_scatter(got)
    dkv, d_b_kv = _kv_rope_bwd(dk_rot, dv, cos, sin)
    d_b_w_in = _mm_tn(n_b, dzb, name="mm_d_b_w_in", tm=D_MODEL, tn=512, shards=N_CHIPS)
    d_w_kv, got = _mm_tn(n_kv, dkv, name="mm_d_w_kv", tm=D_MODEL, tn=2 * KV_WIDTH, side=red_bo.share_side())
    red_bo.took_share(got)
    red_bi = _Reduction(["b_w_in", "w_kv"], [d_b_w_in, d_w_kv.reshape(N_CHIPS, D_MODEL // N_CHIPS, 2 * KV_WIDTH)],
                        core, place)
    (dh1, dh1b, d_kv_g, d_b_g), got = _mm_nt_rms_bwd(
        [(dkv, w_kv, kv_norm_g), (dzb, b_w_in, p["b_norm_g"])], h1, dh2, name="mm_dn_b", tm=512,
        side=red_bi.exchange_side())
    red_bi.took_exchange(got)

    d_a_w_out = _mm_tn(y, dh1b, name="mm_d_a_w_out", tm=1024, tn=D_MODEL)
    red_ao = _Reduction(["a_w_out"], [d_a_w_out.reshape(N_CHIPS, A_WIDTH // N_CHIPS, D_MODEL)], core, place)
    dy, got = _mm_nt(dh1b, a_w_out, name="mm_dy", tn=1024, tm=1024, out_dtype=BF16, side=red_ao.exchange_side())
    red_ao.took_exchange(got)
    sides = [red_bi.scatter_side(), red_ao.scatter_side()]
    (dz, d_ln_g, d_ln_b, d_ws, d_bs_t), got = _gate_bwd(z, dy, a_ln_g, a_ln_b, ws, ws_t, bs_t, side=_join(sides))
    got = _split(got, sides)
    red_bi.took_scatter(got[0])
    red_ao.took_scatter(got[1])
    small = {
        "a_ws": d_ws, "a_bs": d_bs_t.T, "a_ln_g": d_ln_g, "a_ln_b": d_ln_b,
        "kv_norm_g": d_kv_g, "b_kv": d_b_kv, "b_norm_g": d_b_g, "b_bq": d_bq,
        "b_sinks": d_sinks[0:1, :N_Q_HEADS], "final_norm_g": d_final_g,
    }
    packed = [n for n in SMALL_ORDER if n != "a_norm_g"]
    small_shapes = [small[n].shape for n in packed] + [(1, 1)]
    small_pack = _pack_rows([small[n] for n in packed] + [loss_blk[0:1, 0:1]], 64)
    seg = small_pack.shape[0] // 8
    small_pack = small_pack.reshape(8, seg, 128)
    sides = [red_bi.share_side(), red_ao.share_side(), _small_scatter_side(small_pack)]
    (d_a_w_in, from_sibling), got = _mm_tn_exchanging(n_a, dz, name="mm_d_a_w_in", shards=N_CHIPS, side=_join(sides))
    got = _split(got, sides)
    red_bi.took_share(got[0])
    red_ao.took_share(got[1])
    small_mine = _sum_small(small_pack, got[2][0], place)

    chip_sum, (small_all,) = _add_halves(d_a_w_in, from_sibling, name="add_sibling_a_w_in",
                                         side=_small_share_side(small_mine))
    (dx, _, d_a_g), (arrived,) = _mm_nt_rms_bwd([(dz, a_w_in, a_norm_g)], x, dh1, name="mm_dn_a", tm=256,
                                                side=_scatter_side([chip_sum]))
    half_ai = _sum_halves(d_a_w_in, from_sibling, arrived, place, name="sum_chips_a_w_in")
    d_a_g = _rows128(d_a_g)
    sides = [_share_side([half_ai]), _spread_side(d_a_g)]
    got = _split(_comm_call(_join(sides), "share_last"), sides)
    grad_ai = got[0][0]
    small_full = dict(zip(packed + ["loss"], _unpack_rows(small_all.reshape(8 * seg, 128), small_shapes)))
    small_full["a_norm_g"] = _sum_in_device_order(d_a_g, got[1][0], place).reshape(1, -1)
    loss = small_full["loss"].reshape(())

    grad_big = {**red_bo.grads, **red_bi.grads, **red_ao.grads, "a_w_in": grad_ai}
    grads = {}
    for n in SMALL_ORDER:
        gfull = small_full[n]
        if n in SHARDED_SMALL:
            width = p[n].shape[-1]
            gfull = lax.dynamic_slice_in_dim(gfull, chip * width, width, axis=-1)
        grads[n] = gfull.reshape(p[n].shape)

    delta, new_m, new_v = {}, {}, {}
    for n in BIG:
        d, nm, nv, g = _adamw(shard2d[n], grad_big[n], m[n].reshape(shard2d[n].shape),
                              v[n].reshape(shard2d[n].shape), name="adamw_" + n)
        delta[n], new_m[n], new_v[n] = d.reshape(p[n].shape), nm.reshape(p[n].shape), nv.reshape(p[n].shape)
        grads[n] = g.reshape(p[n].shape)
    shapes = [p[n].shape for n in SMALL_ORDER]
    packs = [_pack_rows([src[n] for n in SMALL_ORDER], 8) for src in (p, grads, m, v)]
    outs = _adamw(*packs, name="adamw_small")[:3]
    for res, packed in zip((delta, new_m, new_v), outs):
        for n, val in zip(SMALL_ORDER, _unpack_rows(packed, shapes)):
            res[n] = val

    return (loss, dx[None], *[grads[n] for n in WEIGHTS], *[delta[n] for n in WEIGHTS],
            *[new_m[n] for n in WEIGHTS], *[new_v[n] for n in WEIGHTS])


def kernel(x, a_norm_g, a_w_in, a_ln_g, a_ln_b, a_ws, a_bs, a_w_out, kv_norm_g, w_kv, b_kv, b_norm_g, b_w_in, b_bq, b_sinks, b_w_out, final_norm_g, loss_target, m_a_norm_g, m_a_w_in, m_a_ln_g, m_a_ln_b, m_a_ws, m_a_bs, m_a_w_out, m_kv_norm_g, m_w_kv, m_b_kv, m_b_norm_g, m_b_w_in, m_b_bq, m_b_sinks, m_b_w_out, m_final_norm_g, v_a_norm_g, v_a_w_in, v_a_ln_g, v_a_ln_b, v_a_ws, v_a_bs, v_a_w_out, v_kv_norm_g, v_w_kv, v_b_kv, v_b_norm_g, v_b_w_in, v_b_bq, v_b_sinks, v_b_w_out, v_final_norm_g):
    p = dict(a_norm_g=a_norm_g, a_w_in=a_w_in, a_ln_g=a_ln_g, a_ln_b=a_ln_b, a_ws=a_ws, a_bs=a_bs, a_w_out=a_w_out,
             kv_norm_g=kv_norm_g, w_kv=w_kv, b_kv=b_kv, b_norm_g=b_norm_g, b_w_in=b_w_in, b_bq=b_bq, b_sinks=b_sinks,
             b_w_out=b_w_out, final_norm_g=final_norm_g)
    m = dict(a_norm_g=m_a_norm_g, a_w_in=m_a_w_in, a_ln_g=m_a_ln_g, a_ln_b=m_a_ln_b, a_ws=m_a_ws, a_bs=m_a_bs,
             a_w_out=m_a_w_out, kv_norm_g=m_kv_norm_g, w_kv=m_w_kv, b_kv=m_b_kv, b_norm_g=m_b_norm_g, b_w_in=m_b_w_in,
             b_bq=m_b_bq, b_sinks=m_b_sinks, b_w_out=m_b_w_out, final_norm_g=m_final_norm_g)
    v = dict(a_norm_g=v_a_norm_g, a_w_in=v_a_w_in, a_ln_g=v_a_ln_g, a_ln_b=v_a_ln_b, a_ws=v_a_ws, a_bs=v_a_bs,
             a_w_out=v_a_w_out, kv_norm_g=v_kv_norm_g, w_kv=v_w_kv, b_kv=v_b_kv, b_norm_g=v_b_norm_g, b_w_in=v_b_w_in,
             b_bq=v_b_bq, b_sinks=v_b_sinks, b_w_out=v_b_w_out, final_norm_g=v_final_norm_g)
    return _step(x, loss_target, p, m, v)
```

```python
import functools
import math

import jax
import jax.numpy as jnp
from jax import lax
from jax.experimental import pallas as pl
from jax.experimental.pallas import tpu as pltpu

F32 = jnp.float32
BF16 = jnp.bfloat16

D_MODEL = 1024
CHUNK = 128
A_WIDTH = 2048
A_GROUPS = 16
HEAD_DIM = 64
N_Q_HEADS = 16
N_KV_HEADS = 2
Q_PER_KV = 8
B_WIDTH = 1024
KV_WIDTH = 128
ROPE_THETA = 10000.0
EPS = 1e-5
N_CHIPS = 4

ADAM_LR = 0.001
ADAM_B1 = 0.9
ADAM_B2 = 0.999
ADAM_EPS = 1e-08
ADAM_WD = 0.01
ADAM_STEP = 10

VMEM_LIMIT = 48 * 1024 * 1024
MESH = pl.DeviceIdType.MESH
NEG_BIG = -1e30
HBM = pl.BlockSpec(memory_space=pl.ANY)

NN = (((1,), (0,)), ((), ()))
NT = (((1,), (1,)), ((), ()))
TN = (((0,), (0,)), ((), ()))


def _cparams(**kw):
    return pltpu.CompilerParams(vmem_limit_bytes=VMEM_LIMIT, **kw)


class _Side:
    def __init__(self, ins, out_shapes, sems, start, finish, aliases=None, passing=None):
        self.ins, self.out_shapes, self.sems = list(ins), list(out_shapes), list(sems)
        self.start, self.finish = start, finish
        self.passing = passing or (lambda ins, outs, sems: None)
        self.aliases = dict(aliases or {})


def _join(sides):
    sides = [s for s in sides if s is not None]
    if not sides:
        return None
    offs, i, o, m = [], 0, 0, 0
    for s in sides:
        offs.append((i, o, m))
        i, o, m = i + len(s.ins), o + len(s.out_shapes), m + len(s.sems)

    def run(which):
        def go(ins, outs, sems):
            for s, (a, b, c) in zip(sides, offs):
                getattr(s, which)(ins[a:a + len(s.ins)], outs[b:b + len(s.out_shapes)], sems[c:c + len(s.sems)])
        return go

    aliases = {}
    for s, (a, b, _) in zip(sides, offs):
        aliases.update({a + k: b + v for k, v in s.aliases.items()})
    return _Side([x for s in sides for x in s.ins], [x for s in sides for x in s.out_shapes],
                 [x for s in sides for x in s.sems], run("start"), run("finish"), aliases, run("passing"))


def _split(side_outs, sides):
    out, pos = [], 0
    for s in sides:
        out.append(list(side_outs[pos:pos + len(s.out_shapes)]))
        pos += len(s.out_shapes)
    return out


def _call(body, *, grid, in_specs, out_specs, out_shape, args, name, scratch=(), side=None):
    in_specs, out_specs, out_shape, scratch = list(in_specs), list(out_specs), list(out_shape), list(scratch)
    if side is None:
        res = pl.pallas_call(body, grid=grid, in_specs=in_specs, out_specs=out_specs, out_shape=out_shape,
                             scratch_shapes=scratch, name=name, compiler_params=_cparams())(*args)
        return list(res), []
    n_in, n_out, n_sc = len(in_specs), len(out_specs), len(scratch)
    s_in, s_out = len(side.ins), len(side.out_shapes)

    def wrapped(*refs):
        ins, refs = refs[:n_in], refs[n_in:]
        side_ins, refs = refs[:s_in], refs[s_in:]
        outs, refs = refs[:n_out], refs[n_out:]
        side_outs, refs = refs[:s_out], refs[s_out:]
        scr, side_sems = refs[:n_sc], refs[n_sc:]
        step = 0
        for a, g in enumerate(grid):
            step = step * g + pl.program_id(a)
        steps = math.prod(grid)

        @pl.when(step == 0)
        def _():
            side.start(side_ins, side_outs, side_sems)

        body(*ins, *outs, *scr)

        @pl.when(step == (3 * (steps - 1)) // 4)
        def _():
            side.passing(side_ins, side_outs, side_sems)

        @pl.when(step == steps - 1)
        def _():
            side.finish(side_ins, side_outs, side_sems)

    res = pl.pallas_call(
        wrapped, grid=grid, in_specs=in_specs + [HBM] * s_in, out_specs=out_specs + [HBM] * s_out,
        out_shape=out_shape + side.out_shapes, scratch_shapes=scratch + side.sems,
        input_output_aliases={n_in + k: n_out + v for k, v in side.aliases.items()},
        name=name, compiler_params=_cparams(),
    )(*args, *side.ins)
    return list(res[:n_out]), list(res[n_out:])


def _comm_call(side, name):
    s_in, s_out = len(side.ins), len(side.out_shapes)

    def body(*refs):
        ins, outs, sems = refs[:s_in], refs[s_in:s_in + s_out], refs[s_in + s_out:]
        side.start(ins, outs, sems)
        side.passing(ins, outs, sems)
        side.finish(ins, outs, sems)

    return list(pl.pallas_call(
        body, in_specs=[HBM] * s_in, out_specs=[HBM] * s_out, out_shape=side.out_shapes, scratch_shapes=side.sems,
        input_output_aliases=side.aliases, name=name,
    )(*side.ins))


def _matmul(a, b, *, dims, grid, a_spec, b_spec, o_spec, out_shape, name, acc_axis=None,
            residual=None, r_spec=None, side=None):
    has_res = residual is not None

    def body(*refs):
        if has_res:
            a_ref, b_ref, r_ref, o_ref = refs
        else:
            a_ref, b_ref, o_ref = refs
        part = lax.dot_general(a_ref[...], b_ref[...], dims, preferred_element_type=F32)
        if acc_axis is None:
            if has_res:
                part = part + r_ref[...]
            o_ref[...] = part.astype(o_ref.dtype)
        else:
            k = pl.program_id(acc_axis)

            @pl.when(k == 0)
            def _():
                o_ref[...] = part

            @pl.when(k > 0)
            def _():
                o_ref[...] += part

    in_specs = [a_spec, b_spec] + ([r_spec] if has_res else [])
    args = (a, b) + ((residual,) if has_res else ())
    (out,), side_outs = _call(body, grid=grid, in_specs=in_specs, out_specs=[o_spec], out_shape=[out_shape],
                              args=args, name=name, side=side)
    return (out, side_outs) if side is not None else out


def _row_tile(s, want):
    return min(s, want)


def _mm_nn(a, b, *, name, tn, out_dtype=F32, residual=None, tm=512, side=None):
    s, k = a.shape
    tm = _row_tile(s, tm)
    if b.ndim == 3:
        nsh, _, nc = b.shape
        npb = nc // tn
        n = nsh * nc
        b_spec = pl.BlockSpec((None, k, tn), lambda i, j: (j // npb, 0, j % npb))
    else:
        n = b.shape[1]
        b_spec = pl.BlockSpec((k, tn), lambda i, j: (0, j))
    return _matmul(
        a, b, dims=NN, grid=(s // tm, n // tn),
        a_spec=pl.BlockSpec((tm, k), lambda i, j: (i, 0)), b_spec=b_spec,
        o_spec=pl.BlockSpec((tm, tn), lambda i, j: (i, j)),
        out_shape=jax.ShapeDtypeStruct((s, n), out_dtype), name=name, side=side,
        residual=residual, r_spec=pl.BlockSpec((tm, tn), lambda i, j: (i, j)) if residual is not None else None)


def _mm_nt(a, b, *, name, tn=None, tm=512, out_dtype=F32, side=None):
    s, k = a.shape
    tm = _row_tile(s, tm)
    n = b.shape[0]
    tn = n if tn is None else tn
    return _matmul(
        a, b, dims=NT, grid=(s // tm, n // tn),
        a_spec=pl.BlockSpec((tm, k), lambda i, j: (i, 0)),
        b_spec=pl.BlockSpec((tn, k), lambda i, j: (j, 0)),
        o_spec=pl.BlockSpec((tm, tn), lambda i, j: (i, j)),
        out_shape=jax.ShapeDtypeStruct((s, n), out_dtype), name=name, side=side)


def _mm_tn(a, b, *, name, tm, tn, tk=2048, shards=None, side=None):
    s, m = a.shape
    n = b.shape[1]
    tk = _row_tile(s, tk)
    if shards is None:
        o_spec = pl.BlockSpec((tm, tn), lambda i, j, kk: (i, j))
        out_shape = jax.ShapeDtypeStruct((m, n), F32)
    else:
        assert tm == m
        nc = n // shards
        npb = nc // tn
        o_spec = pl.BlockSpec((None, m, tn), lambda i, j, kk: (j // npb, 0, j % npb))
        out_shape = jax.ShapeDtypeStruct((shards, m, nc), F32)
    return _matmul(
        a, b, dims=TN, grid=(m // tm, n // tn, s // tk), acc_axis=2,
        a_spec=pl.BlockSpec((tk, tm), lambda i, j, kk: (kk, i)),
        b_spec=pl.BlockSpec((tk, tn), lambda i, j, kk: (kk, j)),
        o_spec=o_spec, out_shape=out_shape, name=name, side=side)


def _rstd(x):
    return lax.rsqrt(jnp.mean(x * x, axis=-1, keepdims=True) + EPS)


def _rms_fwd(x, gains, *, name, tr=1024):
    s, d = x.shape
    tr = _row_tile(s, tr)
    ng = len(gains)

    def body(*refs):
        xv = refs[0][...]
        xh = xv * _rstd(xv)
        for t in range(ng):
            refs[1 + ng + t][...] = (xh * refs[1 + t][...]).astype(BF16)

    row = pl.BlockSpec((tr, d), lambda i: (i, 0))
    vec = pl.BlockSpec((1, d), lambda i: (0, 0))
    outs, _ = _call(body, grid=(s // tr,), in_specs=[row] + [vec] * ng, out_specs=[row] * ng,
                    out_shape=[jax.ShapeDtypeStruct((s, d), BF16)] * ng, args=(x, *gains), name=name)
    return outs


def _accumulate(i, ref, value):
    @pl.when(i == 0)
    def _():
        ref[...] = value

    @pl.when(i > 0)
    def _():
        ref[...] += value


def _mm_residual_norms(y, w, res, gains, *, name, tm=512, side=None):
    s, k = y.shape
    d = w.shape[1]
    tm = _row_tile(s, tm)
    ng = len(gains)

    def body(y_ref, w_ref, r_ref, *rest):
        g_refs, h_ref, n_refs = rest[:ng], rest[ng], rest[ng + 1:]
        h = r_ref[...] + jnp.dot(y_ref[...], w_ref[...], preferred_element_type=F32)
        h_ref[...] = h
        xh = h * _rstd(h)
        for t in range(ng):
            n_refs[t][...] = (xh * g_refs[t][...]).astype(BF16)

    row = pl.BlockSpec((tm, d), lambda i: (i, 0))
    vec = pl.BlockSpec((1, d), lambda i: (0, 0))
    return _call(
        body, grid=(s // tm,),
        in_specs=[pl.BlockSpec((tm, k), lambda i: (i, 0)), pl.BlockSpec((k, d), lambda i: (0, 0)), row] + [vec] * ng,
        out_specs=[row] * (1 + ng),
        out_shape=[jax.ShapeDtypeStruct((s, d), F32)] + [jax.ShapeDtypeStruct((s, d), BF16)] * ng,
        args=(y, w, res, *gains), name=name, side=side)


def _mm_residual_loss(y, w, res, tgt, gain, *, name, tm=512):
    s, k = y.shape
    d = w.shape[1]
    tm = _row_tile(s, tm)

    def body(y_ref, w_ref, r_ref, t_ref, g_ref, loss_ref, dh_ref, dhb_ref, dg_ref):
        i = pl.program_id(0)
        hv = r_ref[...] + jnp.dot(y_ref[...], w_ref[...], preferred_element_type=F32)
        g = g_ref[...]
        r = _rstd(hv)
        xh = hv * r
        diff = xh * g - t_ref[...]
        part = 0.5 / d * jnp.sum(jnp.sum(diff * diff, axis=-1, keepdims=True), axis=0, keepdims=True)
        dout = diff * (1.0 / d)
        a = dout * g
        dh = r * (a - xh * jnp.mean(a * xh, axis=-1, keepdims=True))
        dh_ref[...] = dh
        dhb_ref[...] = dh.astype(BF16)
        _accumulate(i, dg_ref, jnp.sum(dout * xh, axis=0, keepdims=True))
        _accumulate(i, loss_ref, jnp.broadcast_to(part, (8, 128)))

    row = pl.BlockSpec((tm, d), lambda i: (i, 0))
    vec = pl.BlockSpec((1, d), lambda i: (0, 0))
    outs, _ = _call(
        body, grid=(s // tm,),
        in_specs=[pl.BlockSpec((tm, k), lambda i: (i, 0)), pl.BlockSpec((k, d), lambda i: (0, 0)), row, row, vec],
        out_specs=[pl.BlockSpec((8, 128), lambda i: (0, 0)), row, row, vec],
        out_shape=[jax.ShapeDtypeStruct((8, 128), F32), jax.ShapeDtypeStruct((s, d), F32),
                   jax.ShapeDtypeStruct((s, d), BF16), jax.ShapeDtypeStruct((1, d), F32)],
        args=(y, w, res, tgt, gain), name=name)
    return outs


def _mm_nt_rms_bwd(terms, x, dres, *, name, tm, side=None):
    s, d = x.shape
    tm = _row_tile(s, tm)
    nt = len(terms)

    def body(*refs):
        a_refs, b_refs, g_refs = refs[0:3 * nt:3], refs[1:3 * nt:3], refs[2:3 * nt:3]
        x_ref, dres_ref = refs[3 * nt], refs[3 * nt + 1]
        dx_ref, dxb_ref = refs[3 * nt + 2], refs[3 * nt + 3]
        dg_refs = refs[3 * nt + 4:]
        i = pl.program_id(0)
        xv = x_ref[...]
        r = _rstd(xv)
        xh = xv * r
        acc = jnp.zeros_like(xv)
        for t in range(nt):
            b_ref = b_refs[t]
            if len(b_ref.shape) == 3:
                kc = b_ref.shape[2]
                dn = None
                for sh in range(b_ref.shape[0]):
                    part = lax.dot_general(a_refs[t][:, sh * kc:(sh + 1) * kc], b_ref[sh], NT, preferred_element_type=F32)
                    dn = part if dn is None else dn + part
            else:
                dn = lax.dot_general(a_refs[t][...], b_ref[...], NT, preferred_element_type=F32)
            acc = acc + dn * g_refs[t][...]
            _accumulate(i, dg_refs[t], jnp.sum(dn * xh, axis=0, keepdims=True))
        dx = dres_ref[...] + r * (acc - xh * jnp.mean(acc * xh, axis=-1, keepdims=True))
        dx_ref[...] = dx
        dxb_ref[...] = dx.astype(BF16)

    row = pl.BlockSpec((tm, d), lambda i: (i, 0))
    vec = pl.BlockSpec((1, d), lambda i: (0, 0))
    in_specs, args = [], []
    for a, b, g in terms:
        in_specs += [pl.BlockSpec((tm, a.shape[1]), lambda i: (i, 0)),
                     pl.BlockSpec(b.shape, (lambda i: (0, 0, 0)) if b.ndim == 3 else (lambda i: (0, 0))), vec]
        args += [a, b, g]
    return _call(
        body, grid=(s // tm,), in_specs=in_specs + [row, row], out_specs=[row, row] + [vec] * nt,
        out_shape=[jax.ShapeDtypeStruct((s, d), F32), jax.ShapeDtypeStruct((s, d), BF16)]
        + [jax.ShapeDtypeStruct((1, d), F32)] * nt,
        args=(*args, x, dres), name=name, side=side)


def _causal_mask(transposed=False):
    row = lax.broadcasted_iota(jnp.int32, (CHUNK, CHUNK), 0)
    col = lax.broadcasted_iota(jnp.int32, (CHUNK, CHUNK), 1)
    return col >= row if transposed else col <= row


def _silu_parts(g):
    sg = jax.nn.sigmoid(g)
    return g * sg, sg * (1.0 + g * (1.0 - sg))


def _gate_fwd(z, ln_g, ln_b, ws, bs_t, *, tr=512, side=None):
    s = z.shape[0]
    tr = _row_tile(s, tr)
    w = A_WIDTH

    def body(u_ref, v_ref, g_ref, lg_ref, lb_ref, ws_ref, bst_ref, y_ref):
        v = v_ref[...].astype(F32)
        mu = jnp.mean(v, axis=-1, keepdims=True)
        xc = v - mu
        rs = lax.rsqrt(jnp.mean(xc * xc, axis=-1, keepdims=True) + EPS)
        vln = (xc * rs * lg_ref[...] + lb_ref[...]).astype(BF16)
        mask = _causal_mask()
        for grp in range(A_GROUPS):
            cols = slice(grp * CHUNK, (grp + 1) * CHUNK)
            wsm = jnp.where(mask, ws_ref[grp], 0.0).astype(BF16)
            bcol = bst_ref[:, grp:grp + 1]
            for ci in range(tr // CHUNK):
                rows = slice(ci * CHUNK, (ci + 1) * CHUNK)
                sv = jnp.dot(wsm, vln[rows, cols], preferred_element_type=F32) + bcol
                gv = g_ref[rows, cols].astype(F32)
                y_ref[rows, cols] = (u_ref[rows, cols].astype(F32) * sv * (gv * jax.nn.sigmoid(gv))).astype(BF16)

    vec = pl.BlockSpec((1, w), lambda i: (0, 0))
    (y,), side_outs = _call(
        body, grid=(s // tr,),
        in_specs=[pl.BlockSpec((tr, w), lambda i: (i, 0)), pl.BlockSpec((tr, w), lambda i: (i, 1)),
                  pl.BlockSpec((tr, w), lambda i: (i, 2)), vec, vec,
                  pl.BlockSpec((A_GROUPS, CHUNK, CHUNK), lambda i: (0, 0, 0)),
                  pl.BlockSpec((CHUNK, A_GROUPS), lambda i: (0, 0))],
        out_specs=[pl.BlockSpec((tr, w), lambda i: (i, 0))],
        out_shape=[jax.ShapeDtypeStruct((s, w), BF16)], args=(z, z, z, ln_g, ln_b, ws, bs_t), name="gate_fwd",
        side=side)
    return y, side_outs


def _gate_bwd(z, dy, ln_g, ln_b, ws, ws_t, bs_t, *, tr=256, side=None):
    s = z.shape[0]
    tr = _row_tile(s, tr)
    w = A_WIDTH
    nsteps = s // tr

    def body(u_ref, v_ref, g_ref, dy_ref, lg_ref, lb_ref, ws_ref, wst_ref, bst_ref,
             dz_ref, dlg_ref, dlb_ref, dws_ref, dbst_ref, dvln_sc, dsv_sc):
        i = pl.program_id(0)

        @pl.when(i == 0)
        def _():
            dws_ref[...] = jnp.zeros_like(dws_ref)
            dsv_sc[...] = jnp.zeros_like(dsv_sc)

        v = v_ref[...].astype(F32)
        mu = jnp.mean(v, axis=-1, keepdims=True)
        xc = v - mu
        rs = lax.rsqrt(jnp.mean(xc * xc, axis=-1, keepdims=True) + EPS)
        xh = xc * rs
        lg = lg_ref[...]
        vln = (xh * lg + lb_ref[...]).astype(BF16)
        mask = _causal_mask()
        mask_t = _causal_mask(transposed=True)
        for grp in range(A_GROUPS):
            cols = slice(grp * CHUNK, (grp + 1) * CHUNK)
            wsm = jnp.where(mask, ws_ref[grp], 0.0).astype(BF16)
            wsm_t = jnp.where(mask_t, wst_ref[grp], 0.0).astype(BF16)
            bcol = bst_ref[:, grp:grp + 1]
            for ci in range(tr // CHUNK):
                rows = slice(ci * CHUNK, (ci + 1) * CHUNK)
                vb = vln[rows, cols]
                sv = jnp.dot(wsm, vb, preferred_element_type=F32) + bcol
                uv = u_ref[rows, cols].astype(F32)
                silu, dsilu = _silu_parts(g_ref[rows, cols].astype(F32))
                dyv = dy_ref[rows, cols].astype(F32)
                dyu = dyv * uv
                dz_ref[rows, cols] = (dyv * sv * silu).astype(BF16)
                dz_ref[rows, 2 * w + grp * CHUNK:2 * w + (grp + 1) * CHUNK] = (dyu * sv * dsilu).astype(BF16)
                dsv = dyu * silu
                dsvb = dsv.astype(BF16)
                dvln_sc[rows, cols] = jnp.dot(wsm_t, dsvb, preferred_element_type=F32)
                dws_ref[grp] += lax.dot_general(dsvb, vb, NT, preferred_element_type=F32)
                dsv_sc[grp] += dsv
        dvln = dvln_sc[...]
        dlg_t = jnp.sum(dvln * xh, axis=0, keepdims=True)
        dlb_t = jnp.sum(dvln, axis=0, keepdims=True)
        a = dvln * lg
        dv = rs * (a - jnp.mean(a, axis=-1, keepdims=True) - xh * jnp.mean(a * xh, axis=-1, keepdims=True))
        dz_ref[:, w:2 * w] = dv.astype(BF16)

        @pl.when(i == 0)
        def _():
            dlg_ref[...] = dlg_t
            dlb_ref[...] = dlb_t

        @pl.when(i > 0)
        def _():
            dlg_ref[...] += dlg_t
            dlb_ref[...] += dlb_t

        @pl.when(i == nsteps - 1)
        def _():
            for grp in range(A_GROUPS):
                dws_ref[grp] = jnp.where(mask, dws_ref[grp], 0.0)
                dbst_ref[:, grp:grp + 1] = jnp.sum(dsv_sc[grp], axis=-1, keepdims=True)

    vec = pl.BlockSpec((1, w), lambda i: (0, 0))
    wsspec = pl.BlockSpec((A_GROUPS, CHUNK, CHUNK), lambda i: (0, 0, 0))
    bsspec = pl.BlockSpec((CHUNK, A_GROUPS), lambda i: (0, 0))
    return _call(
        body, grid=(nsteps,),
        in_specs=[pl.BlockSpec((tr, w), lambda i: (i, 0)), pl.BlockSpec((tr, w), lambda i: (i, 1)),
                  pl.BlockSpec((tr, w), lambda i: (i, 2)), pl.BlockSpec((tr, w), lambda i: (i, 0)),
                  vec, vec, wsspec, wsspec, bsspec],
        out_specs=[pl.BlockSpec((tr, 3 * w), lambda i: (i, 0)), vec, vec, wsspec, bsspec],
        out_shape=[jax.ShapeDtypeStruct((s, 3 * w), BF16), jax.ShapeDtypeStruct((1, w), F32),
                   jax.ShapeDtypeStruct((1, w), F32), jax.ShapeDtypeStruct((A_GROUPS, CHUNK, CHUNK), F32),
                   jax.ShapeDtypeStruct((CHUNK, A_GROUPS), F32)],
        scratch=[pltpu.VMEM((tr, w), F32), pltpu.VMEM((A_GROUPS, CHUNK, CHUNK), F32)],
        args=(z, z, z, dy, ln_g, ln_b, ws, ws_t, bs_t), name="gate_bwd", side=side)


HEADS_PER_BLOCK = 128 // HEAD_DIM
BLOCKS_PER_KV = Q_PER_KV // HEADS_PER_BLOCK
SCALE = HEAD_DIM ** -0.5
LOG2_E = math.log2(math.e)


def _rope_tables(s):
    lane = jnp.arange(128)
    inv_freq = ROPE_THETA ** (-(2 * (lane % (HEAD_DIM // 2))).astype(F32) / HEAD_DIM)
    sign = jnp.where(lane % HEAD_DIM < HEAD_DIM // 2, -1.0, 1.0).astype(F32)
    ang = jnp.arange(s, dtype=F32)[:, None] * inv_freq[None, :]
    return jnp.cos(ang), jnp.sin(ang) * sign[None, :]


def _swap_halves(x):
    n = x.shape[-1]
    lane = lax.broadcasted_iota(jnp.int32, x.shape, x.ndim - 1)
    first = (lane % HEAD_DIM) < (HEAD_DIM // 2)
    return jnp.where(first, pltpu.roll(x, n - HEAD_DIM // 2, x.ndim - 1), pltpu.roll(x, HEAD_DIM // 2, x.ndim - 1))


def _left_half(rows):
    return lax.broadcasted_iota(jnp.int32, (rows, 128), 1) < HEAD_DIM


def _dup_heads(x):
    left = _left_half(x.shape[0])
    swapped = pltpu.roll(x, HEAD_DIM, 1)
    return jnp.concatenate([jnp.where(left, x, swapped), jnp.where(left, swapped, x)], axis=-1)


def _fold_heads(a):
    b0, b1 = a[:, :128], a[:, 128:]
    f0 = b0 + pltpu.roll(b0, HEAD_DIM, 1)
    f1 = b1 + pltpu.roll(b1, HEAD_DIM, 1)
    return jnp.where(_left_half(a.shape[0]), f0, f1)


def _kv_rope(n_kv, w_kv, b_kv, cos, sin, *, tr=2048):
    s, d = n_kv.shape
    tr = _row_tile(s, tr)

    def body(n_ref, w_ref, b_ref, c_ref, s_ref, k_ref, v_ref):
        x = jnp.dot(n_ref[...], w_ref[...], preferred_element_type=F32) + b_ref[...]
        k = x[:, :KV_WIDTH]
        k_ref[...] = _dup_heads(k * c_ref[...] + _swap_halves(k) * s_ref[...]).astype(BF16)
        v_ref[...] = _dup_heads(x[:, KV_WIDTH:]).astype(BF16)

    tab = pl.BlockSpec((tr, KV_WIDTH), lambda i: (i, 0))
    wide = pl.BlockSpec((tr, 2 * KV_WIDTH), lambda i: (i, 0))
    outs, _ = _call(body, grid=(s // tr,),
                    in_specs=[pl.BlockSpec((tr, d), lambda i: (i, 0)), pl.BlockSpec((d, 2 * KV_WIDTH), lambda i: (0, 0)),
                              pl.BlockSpec((1, 2 * KV_WIDTH), lambda i: (0, 0)), tab, tab],
                    out_specs=[wide, wide], out_shape=[jax.ShapeDtypeStruct((s, 2 * KV_WIDTH), BF16)] * 2,
                    args=(n_kv, w_kv, b_kv, cos, sin), name="kv_rope")
    return outs


def _kv_rope_bwd(dk2, dv2, cos, sin, *, tr=2048):
    s = dk2.shape[0]
    tr = _row_tile(s, tr)

    def body(dk_ref, dv_ref, c_ref, s_ref, dkv_ref, db_ref):
        i = pl.program_id(0)
        d = _fold_heads(dk_ref[...])
        dk = d * c_ref[...] + _swap_halves(d * s_ref[...])
        dvv = _fold_heads(dv_ref[...])
        dkv_ref[:, :KV_WIDTH] = dk.astype(BF16)
        dkv_ref[:, KV_WIDTH:] = dvv.astype(BF16)
        sk = jnp.sum(dk, axis=0, keepdims=True)
        sv = jnp.sum(dvv, axis=0, keepdims=True)

        @pl.when(i == 0)
        def _():
            db_ref[:, :KV_WIDTH] = sk
            db_ref[:, KV_WIDTH:] = sv

        @pl.when(i > 0)
        def _():
            db_ref[:, :KV_WIDTH] += sk
            db_ref[:, KV_WIDTH:] += sv

    tab = pl.BlockSpec((tr, KV_WIDTH), lambda i: (i, 0))
    wide = pl.BlockSpec((tr, 2 * KV_WIDTH), lambda i: (i, 0))
    outs, _ = _call(body, grid=(s // tr,), in_specs=[wide, wide, tab, tab],
                    out_specs=[wide, pl.BlockSpec((1, 2 * KV_WIDTH), lambda i: (0, 0))],
                    out_shape=[jax.ShapeDtypeStruct((s, 2 * KV_WIDTH), BF16),
                               jax.ShapeDtypeStruct((1, 2 * KV_WIDTH), F32)],
                    args=(dk2, dv2, cos, sin), name="kv_rope_bwd")
    return outs


def _from_previous():
    cols = Q_PER_KV * CHUNK
    k = lax.broadcasted_iota(jnp.int32, (CHUNK, cols), 0)
    q = lax.broadcasted_iota(jnp.int32, (CHUNK, cols), 1) & (CHUNK - 1)
    return k > q


def _fold(x2, prev):
    return jnp.where(prev, x2[:CHUNK], x2[CHUNK:])


def _unfold(x, prev):
    zero = jnp.zeros_like(x)
    return jnp.concatenate([jnp.where(prev, x, zero), jnp.where(prev, zero, x)], axis=0)


def _stack_heads(blocks, left):
    parts = []
    for b in blocks:
        parts.append(jnp.where(left, b, jnp.zeros_like(b)))
        parts.append(jnp.where(left, jnp.zeros_like(b), b))
    return jnp.concatenate(parts, axis=0)


def _unstack_heads(xt):
    top = lax.broadcasted_iota(jnp.int32, (128, CHUNK), 0) < HEAD_DIM
    return [jnp.where(top, xt[:, (2 * b) * CHUNK:(2 * b + 1) * CHUNK], xt[:, (2 * b + 1) * CHUNK:(2 * b + 2) * CHUNK]).T
            for b in range(BLOCKS_PER_KV)]


def _sink_row(sk_ref, kvh):
    return jnp.concatenate([jnp.full((1, CHUNK), sk_ref[0, kvh * Q_PER_KV + r], F32) for r in range(Q_PER_KV)], axis=1)


def _stacked_probs(qs, kd, prev, sink, i):
    sc2 = lax.dot_general(kd, qs, NT, preferred_element_type=F32)
    no_previous = jnp.where(i > 0, 0.0, NEG_BIG)
    sc = jnp.where(prev, sc2[:CHUNK] + no_previous, sc2[CHUNK:])
    sink = sink * (1.0 / SCALE)
    m = jnp.maximum(jnp.max(sc, axis=0, keepdims=True), sink)
    p = jnp.exp2((sc - m) * (SCALE * LOG2_E))
    esink = jnp.exp2((sink - m) * (SCALE * LOG2_E))
    inv = 1.0 / (jnp.sum(p, axis=0, keepdims=True) + esink)
    return p * inv, esink * inv


def _lane_block(b):
    return slice(b * 128, (b + 1) * 128)


def _rope_blocks(zq_ref, bq_ref, cos, sin, kvh, rows):
    out = []
    for b in range(BLOCKS_PER_KV):
        cols = _lane_block(kvh * BLOCKS_PER_KV + b)
        q = zq_ref[rows, cols].astype(F32) + bq_ref[:, cols]
        out.append((q * cos + _swap_halves(q) * sin).astype(BF16))
    return out


CHUNKS_PER_STEP = 4


def _attn_specs():
    rows = CHUNKS_PER_STEP * CHUNK
    qspec = pl.BlockSpec((rows, B_WIDTH), lambda i: (i, 0))
    gspec = pl.BlockSpec((rows, B_WIDTH), lambda i: (i, 1))
    prev = pl.BlockSpec((CHUNK, 2 * KV_WIDTH), lambda i: (jnp.maximum(CHUNKS_PER_STEP * i - 1, 0), 0))
    cur = pl.BlockSpec((rows, 2 * KV_WIDTH), lambda i: (i, 0))
    tab = pl.BlockSpec((rows, KV_WIDTH), lambda i: (i, 0))
    bq = pl.BlockSpec((1, B_WIDTH), lambda i: (0, 0))
    sinks = pl.BlockSpec(memory_space=pltpu.SMEM)
    return qspec, gspec, prev, cur, tab, bq, sinks


def _chunk_keys(prev_ref, cur_ref, sub):
    before = prev_ref[...] if sub == 0 else cur_ref[(sub - 1) * CHUNK:sub * CHUNK]
    return jnp.concatenate([before, cur_ref[sub * CHUNK:(sub + 1) * CHUNK]], axis=0)


def _attn_fwd(zb, k2, v2, cos, sin, b_bq, sinks, *, side=None):
    s = zb.shape[0]

    def body(zq_ref, zg_ref, kp_ref, kc_ref, vp_ref, vc_ref, c_ref, s_ref, bq_ref, sk_ref, y_ref):
        prev = _from_previous()
        left = _left_half(CHUNK)
        for sub in range(CHUNKS_PER_STEP):
            chunk = CHUNKS_PER_STEP * pl.program_id(0) + sub
            rows = slice(sub * CHUNK, (sub + 1) * CHUNK)
            cos, sin = c_ref[rows, :], s_ref[rows, :]
            kcat, vcat = _chunk_keys(kp_ref, kc_ref, sub), _chunk_keys(vp_ref, vc_ref, sub)
            for kvh in range(N_KV_HEADS):
                qs = _stack_heads(_rope_blocks(zq_ref, bq_ref, cos, sin, kvh, rows), left)
                p, _ = _stacked_probs(qs, kcat[:, _lane_block(kvh)], prev, _sink_row(sk_ref, kvh), chunk)
                ot = lax.dot_general(vcat[:, _lane_block(kvh)], _unfold(p, prev).astype(BF16), TN,
                                     preferred_element_type=F32)
                for b, ob in enumerate(_unstack_heads(ot)):
                    cols = _lane_block(kvh * BLOCKS_PER_KV + b)
                    gv = zg_ref[rows, cols].astype(F32)
                    y_ref[rows, cols] = (ob * (gv * jax.nn.sigmoid(gv))).astype(BF16)

    qspec, gspec, prev, cur, tab, bq, sk = _attn_specs()
    (y,), side_outs = _call(body, grid=(s // (CHUNKS_PER_STEP * CHUNK),),
                            in_specs=[qspec, gspec, prev, cur, prev, cur, tab, tab, bq, sk],
                            out_specs=[qspec], out_shape=[jax.ShapeDtypeStruct((s, B_WIDTH), BF16)],
                            args=(zb, zb, k2, k2, v2, v2, cos, sin, b_bq, sinks), name="attn_fwd", side=side)
    return y, side_outs


def _attn_bwd(zb, dyb, k2, v2, cos, sin, b_bq, sinks, *, side=None):
    s = zb.shape[0]

    def body(zq_ref, zg_ref, dy_ref, kp_ref, kc_ref, vp_ref, vc_ref, c_ref, s_ref, bq_ref, sk_ref,
             dz_ref, dk_ref, dv_ref, dbq_ref, dsk_ref):
        i = pl.program_id(0)

        @pl.when(i == 0)
        def _():
            dk_ref[...] = jnp.zeros_like(dk_ref)
            dv_ref[...] = jnp.zeros_like(dv_ref)
            dbq_ref[...] = jnp.zeros_like(dbq_ref)
            dsk_ref[...] = jnp.zeros_like(dsk_ref)

        prev = _from_previous()
        left = _left_half(CHUNK)
        lane = lax.broadcasted_iota(jnp.int32, (1, 128), 1)
        dsk_row = jnp.zeros((1, 128), F32)
        for sub in range(CHUNKS_PER_STEP):
            chunk = CHUNKS_PER_STEP * i + sub
            rows = slice(sub * CHUNK, (sub + 1) * CHUNK)
            cos, sin = c_ref[rows, :], s_ref[rows, :]
            kcat, vcat = _chunk_keys(kp_ref, kc_ref, sub), _chunk_keys(vp_ref, vc_ref, sub)
            cur_rows = pl.ds(pl.multiple_of(chunk * CHUNK, CHUNK), CHUNK)
            for kvh in range(N_KV_HEADS):
                kd, vd = kcat[:, _lane_block(kvh)], vcat[:, _lane_block(kvh)]
                qs = _stack_heads(_rope_blocks(zq_ref, bq_ref, cos, sin, kvh, rows), left)
                p, psink = _stacked_probs(qs, kd, prev, _sink_row(sk_ref, kvh), chunk)
                pb = _unfold(p, prev).astype(BF16)
                ot = lax.dot_general(vd, pb, TN, preferred_element_type=F32)
                gates, dys = [], []
                for b in range(BLOCKS_PER_KV):
                    cols = _lane_block(kvh * BLOCKS_PER_KV + b)
                    gates.append(_silu_parts(zg_ref[rows, cols].astype(F32)))
                    dys.append(dy_ref[rows, cols].astype(F32))
                dos = _stack_heads([(dyv * silu).astype(BF16) for dyv, (silu, _) in zip(dys, gates)], left)
                dp = _fold(lax.dot_general(vd, dos, NT, preferred_element_type=F32), prev)
                delta = jnp.sum(p * dp, axis=0, keepdims=True)
                ds = _unfold(p * (dp - delta) * SCALE, prev).astype(BF16)
                dqt = lax.dot_general(kd, ds, TN, preferred_element_type=F32)
                dk_part = jnp.dot(ds, qs, preferred_element_type=F32)
                dv_part = jnp.dot(pb, dos, preferred_element_type=F32)
                dk_ref[cur_rows, _lane_block(kvh)] += dk_part[CHUNK:]
                dv_ref[cur_rows, _lane_block(kvh)] += dv_part[CHUNK:]

                @pl.when(chunk > 0)
                def _(kvh=kvh, chunk=chunk, dk_part=dk_part, dv_part=dv_part):
                    prev_rows = pl.ds(pl.multiple_of((chunk - 1) * CHUNK, CHUNK), CHUNK)
                    dk_ref[prev_rows, _lane_block(kvh)] += dk_part[:CHUNK]
                    dv_ref[prev_rows, _lane_block(kvh)] += dv_part[:CHUNK]

                sink_grad = psink * delta
                for r in range(Q_PER_KV):
                    dsink = -jnp.sum(sink_grad[:, r * CHUNK:(r + 1) * CHUNK], axis=1, keepdims=True)
                    dsk_row = dsk_row + jnp.where(lane == kvh * Q_PER_KV + r, dsink, 0.0)
                blocks = zip(_unstack_heads(ot), _unstack_heads(dqt), dys, gates)
                for b, (ob, dqr, dyv, (_, dsilu)) in enumerate(blocks):
                    blk = kvh * BLOCKS_PER_KV + b
                    dq = dqr * cos + _swap_halves(dqr * sin)
                    dbq_ref[:, _lane_block(blk)] += jnp.sum(dq, axis=0, keepdims=True)
                    dz_ref[rows, _lane_block(blk)] = dq.astype(BF16)
                    dz_ref[rows, _lane_block(B_WIDTH // 128 + blk)] = (dyv * ob * dsilu).astype(BF16)
        dsk_ref[0:1, :] += dsk_row

    qspec, gspec, prev, cur, tab, bq, sk = _attn_specs()
    full = pl.BlockSpec((s, 2 * KV_WIDTH), lambda i: (0, 0))
    return _call(
        body, grid=(s // (CHUNKS_PER_STEP * CHUNK),),
        in_specs=[qspec, gspec, qspec, prev, cur, prev, cur, tab, tab, bq, sk],
        out_specs=[pl.BlockSpec((CHUNKS_PER_STEP * CHUNK, 2 * B_WIDTH), lambda i: (i, 0)), full, full, bq,
                   pl.BlockSpec((8, 128), lambda i: (0, 0))],
        out_shape=[jax.ShapeDtypeStruct((s, 2 * B_WIDTH), BF16), jax.ShapeDtypeStruct((s, 2 * KV_WIDTH), F32),
                   jax.ShapeDtypeStruct((s, 2 * KV_WIDTH), F32), jax.ShapeDtypeStruct((1, B_WIDTH), F32),
                   jax.ShapeDtypeStruct((8, 128), F32)],
        args=(zb, zb, dyb, k2, k2, v2, v2, cos, sin, b_bq, sinks), name="attn_bwd", side=side)


def _place():
    x, y, c = lax.axis_index("x"), lax.axis_index("y"), lax.axis_index("c")
    return x, y, c, [(1 - x, y), (x, 1 - y), (1 - x, 1 - y)]


def _relations():
    return [(r >> 2 & 1, r >> 1 & 1, r & 1) for r in range(1, 8)]


def _gather_side(arrs):
    n = len(arrs)

    def copies(ins, outs, sems):
        send_ici, recv_ici, send_d2d, recv_d2d, local_sem = sems
        x, y, c, chips = _place()
        me = 2 * x + y

        def rows(a, half):
            hr = arrs[a].shape[0] // 2
            return pl.ds(half * hr, hr)

        def ici(a, j, src_chip, to):
            return pltpu.make_async_remote_copy(
                src_ref=ins[a].at[rows(a, c)], dst_ref=outs[a].at[src_chip, rows(a, c)],
                send_sem=send_ici.at[a, j], recv_sem=recv_ici.at[a, j], device_id=to, device_id_type=MESH)

        def d2d(a, j, chip, half):
            blk = outs[a].at[chip, rows(a, half)]
            return pltpu.make_async_remote_copy(
                src_ref=blk, dst_ref=blk, send_sem=send_d2d.at[a, j], recv_sem=recv_d2d.at[a, j],
                device_id=(x, y, 1 - c), device_id_type=MESH)

        local = [pltpu.make_async_copy(ins[a], outs[a].at[me], local_sem.at[a]) for a in range(n)]
        pairs = [(a, j, chip) for a in range(n) for j, chip in enumerate(chips)]
        return c, me, local, ici, d2d, pairs

    def start(ins, outs, sems):
        c, me, local, ici, _, pairs = copies(ins, outs, sems)
        for cp in local:
            cp.start()
        for a, j, chip in pairs:
            ici(a, j, me, (*chip, c)).start()

    def passing(ins, outs, sems):
        c, _, _, ici, d2d, pairs = copies(ins, outs, sems)
        for a, j, (px, py) in pairs:
            ici(a, j, 2 * px + py, (px, py, c)).wait_recv()
            d2d(a, j, 2 * px + py, c).start()

    def finish(ins, outs, sems):
        c, me, local, ici, d2d, pairs = copies(ins, outs, sems)
        for a, j, (px, py) in pairs:
            d2d(a, j, 2 * px + py, 1 - c).wait_recv()
        for a, j, (px, py) in pairs:
            ici(a, j, me, (px, py, c)).wait_send()
            d2d(a, j, 2 * px + py, c).wait_send()
        for cp in local:
            cp.wait()

    return _Side(arrs, [jax.ShapeDtypeStruct((N_CHIPS,) + a.shape, a.dtype) for a in arrs],
                 [pltpu.SemaphoreType.DMA((n, 3))] * 4 + [pltpu.SemaphoreType.DMA((n,))], start, finish,
                 passing=passing)


def _exchange_side(grads):
    n = len(grads)

    def copies(ins, outs, sems):
        send_sem, recv_sem = sems
        x, y, c, _ = _place()
        cps = []
        for a in range(n):
            hr = grads[a].shape[1] // 2
            cps.append(pltpu.make_async_remote_copy(
                src_ref=ins[a].at[:, pl.ds((1 - c) * hr, hr), :], dst_ref=outs[a],
                send_sem=send_sem.at[a], recv_sem=recv_sem.at[a], device_id=(x, y, 1 - c), device_id_type=MESH))
        return cps

    def start(ins, outs, sems):
        for cp in copies(ins, outs, sems):
            cp.start()

    def finish(ins, outs, sems):
        for cp in copies(ins, outs, sems):
            cp.wait()

    return _Side(grads, [jax.ShapeDtypeStruct((g.shape[0], g.shape[1] // 2, g.shape[2]), g.dtype) for g in grads],
                 [pltpu.SemaphoreType.DMA((n,))] * 2, start, finish)


def _scatter_side(chip_sums, small=None):
    n = len(chip_sums)
    arrs = list(chip_sums) + ([small] if small is not None else [])

    def copies(ins, outs, sems):
        x, y, c, chips = _place()
        cps = []
        for a in range(n):
            for j, (px, py) in enumerate(chips):
                cps.append(pltpu.make_async_remote_copy(
                    src_ref=ins[a].at[2 * px + py], dst_ref=outs[a].at[j],
                    send_sem=sems[0].at[a, j], recv_sem=sems[1].at[a, j], device_id=(px, py, c), device_id_type=MESH))
        if small is not None:
            for r, (fx, fy, fc) in enumerate(_relations(), start=1):
                px, py, pc = x ^ fx, y ^ fy, c ^ fc
                cps.append(pltpu.make_async_remote_copy(
                    src_ref=ins[n].at[4 * px + 2 * py + pc], dst_ref=outs[n].at[r],
                    send_sem=sems[2].at[r - 1], recv_sem=sems[3].at[r - 1], device_id=(px, py, pc),
                    device_id_type=MESH))
        return cps

    def start(ins, outs, sems):
        for cp in copies(ins, outs, sems):
            cp.start()

    def finish(ins, outs, sems):
        for cp in copies(ins, outs, sems):
            cp.wait()

    shapes = [jax.ShapeDtypeStruct((3,) + t.shape[1:], t.dtype) for t in chip_sums]
    sems = [pltpu.SemaphoreType.DMA((n, 3))] * 2
    if small is not None:
        shapes.append(jax.ShapeDtypeStruct(small.shape, small.dtype))
        sems += [pltpu.SemaphoreType.DMA((7,))] * 2
    return _Side(arrs, shapes, sems, start, finish)


def _small_scatter_side(small):
    def copies(ins, outs, sems):
        x, y, c, _ = _place()
        cps = []
        for r, (fx, fy, fc) in enumerate(_relations(), start=1):
            px, py, pc = x ^ fx, y ^ fy, c ^ fc
            cps.append(pltpu.make_async_remote_copy(
                src_ref=ins[0].at[4 * px + 2 * py + pc], dst_ref=outs[0].at[r],
                send_sem=sems[0].at[r - 1], recv_sem=sems[1].at[r - 1], device_id=(px, py, pc), device_id_type=MESH))
        return cps

    def start(ins, outs, sems):
        for cp in copies(ins, outs, sems):
            cp.start()

    def finish(ins, outs, sems):
        for cp in copies(ins, outs, sems):
            cp.wait()

    return _Side([small], [jax.ShapeDtypeStruct(small.shape, small.dtype)], [pltpu.SemaphoreType.DMA((7,))] * 2,
                 start, finish)


def _small_share_side(small):
    return _share_side([], small)


def _share_side(halves, small=None):
    n = len(halves)
    arrs = list(halves) + ([small] if small is not None else [])

    def copies(ins, outs, sems, mine):
        x, y, c, _ = _place()
        me = 4 * x + 2 * y + c
        cps = []
        for a in range(n):
            hr = halves[a].shape[0] // 2
            rows = pl.ds((c if mine else 1 - c) * hr, hr)
            cps.append(pltpu.make_async_remote_copy(
                src_ref=ins[a].at[rows], dst_ref=outs[a].at[rows],
                send_sem=sems[0].at[a], recv_sem=sems[1].at[a], device_id=(x, y, 1 - c), device_id_type=MESH))
        if small is not None:
            for r, (fx, fy, fc) in enumerate(_relations(), start=1):
                px, py, pc = x ^ fx, y ^ fy, c ^ fc
                seg = me if mine else 4 * px + 2 * py + pc
                cps.append(pltpu.make_async_remote_copy(
                    src_ref=ins[n].at[seg], dst_ref=outs[n].at[seg],
                    send_sem=sems[-2].at[r - 1], recv_sem=sems[-1].at[r - 1], device_id=(px, py, pc),
                    device_id_type=MESH))
        return cps

    def start(ins, outs, sems):
        for cp in copies(ins, outs, sems, True):
            cp.start()

    def finish(ins, outs, sems):
        for cp in copies(ins, outs, sems, False):
            cp.wait_recv()
        for cp in copies(ins, outs, sems, True):
            cp.wait_send()

    sems = ([pltpu.SemaphoreType.DMA((n,))] * 2 if n else []) + (
        [pltpu.SemaphoreType.DMA((7,))] * 2 if small is not None else [])
    return _Side(arrs, [jax.ShapeDtypeStruct(h.shape, h.dtype) for h in arrs], sems, start, finish,
                 aliases={i: i for i in range(len(arrs))})


GATHER_PIECES = [(0, 0), (0, 1), (1, 0), (2, 0), (1, 1), (2, 1), (3, 0), (3, 1)]


def _mm_gathering(a, shard, order, *, name, tm=1024):
    s, k = a.shape
    nc = shard.shape[1]
    tm = _row_tile(s, tm)
    tn = nc // 2
    hr = k // 2
    qr = hr // 2
    blocks = jnp.stack([order[src] * 2 + h for src, h in GATHER_PIECES]).astype(jnp.int32)

    def body(blocks_ref, a_ref, shard_ref, z_ref, full_ref, wbuf, send_ici, recv_ici, send_relay,
             recv_relay, send_d2d, recv_d2d, local_sem, load_sem):
        piece, i = pl.program_id(0), pl.program_id(1)
        x, y, c, chips = _place()
        me = 2 * x + y
        nbrs = chips[:2]
        chip_of = [2 * px + py for px, py in chips]

        def quarter(q):
            return pl.ds(c * hr + q * qr, qr)

        def sibling_quarter(q):
            return pl.ds((1 - c) * hr + q * qr, qr)

        def whole(half):
            return pl.ds(half * hr, hr)

        def cols(h):
            return pl.ds(h * tn, tn)

        def direct(j, src_chip, h):
            return pltpu.make_async_remote_copy(
                src_ref=shard_ref.at[whole(c), cols(h)], dst_ref=full_ref.at[src_chip, whole(c), cols(h)],
                send_sem=send_ici.at[j, h], recv_sem=recv_ici.at[j, h], device_id=(*nbrs[j], c), device_id_type=MESH)

        def relay(j, src_chip, h):
            blk = full_ref.at[src_chip, quarter(j), cols(h)]
            return pltpu.make_async_remote_copy(
                src_ref=blk, dst_ref=blk, send_sem=send_relay.at[j, h], recv_sem=recv_relay.at[j, h],
                device_id=(*nbrs[1 - j], c), device_id_type=MESH)

        def d2d(j, chip, rows, h):
            blk = full_ref.at[chip, rows, cols(h)]
            return pltpu.make_async_remote_copy(
                src_ref=blk, dst_ref=blk, send_sem=send_d2d.at[j, h], recv_sem=recv_d2d.at[j, h],
                device_id=(x, y, 1 - c), device_id_type=MESH)

        def load(p):
            src, h = GATHER_PIECES[p]
            where = shard_ref if src == 0 else full_ref.at[chip_of[src - 1]]
            return pltpu.make_async_copy(where.at[:, cols(h)], wbuf.at[p % 2], load_sem.at[p % 2])

        local = pltpu.make_async_copy(shard_ref, full_ref.at[me], local_sem)

        def arrived(p):
            src, h = GATHER_PIECES[p]
            if src in (1, 2):
                j = src - 1
                direct(j, chip_of[j], h).wait_recv()
                relay(j, chip_of[j], h).start()
                d2d(j, chip_of[j], whole(c), h).start()
            elif src == 3:
                for j in range(2):
                    relay(1 - j, chip_of[2], h).wait_recv()
                    d2d(2 + j, chip_of[2], quarter(1 - j), h).start()

        def fetch(p):
            src, h = GATHER_PIECES[p]
            if src in (1, 2):
                d2d(src - 1, chip_of[src - 1], whole(1 - c), h).wait_recv()
            elif src == 3:
                for j in range(2):
                    d2d(2 + j, chip_of[2], sibling_quarter(1 - j), h).wait_recv()
            load(p).start()

        n_i = s // tm
        for p in range(len(GATHER_PIECES)):
            @pl.when(jnp.logical_and(piece == p, i == 0))
            def _(p=p):
                if p == 0:
                    local.start()
                    for hh in range(2):
                        for j in range(2):
                            direct(j, me, hh).start()
                    load(0).start()
                load(p).wait()

        z_ref[...] = jnp.dot(a_ref[...], wbuf[piece % 2], preferred_element_type=F32).astype(z_ref.dtype)

        for p in range(len(GATHER_PIECES) - 1):
            @pl.when(jnp.logical_and(piece == p, i == min(1, n_i - 1)))
            def _(p=p):
                arrived(p + 1)

            @pl.when(jnp.logical_and(piece == p, i == min(2, n_i - 1)))
            def _(p=p):
                fetch(p + 1)

        last = jnp.logical_and(piece == len(GATHER_PIECES) - 1, i == n_i - 1)

        @pl.when(last)
        def _():
            for h in range(2):
                for j in range(2):
                    direct(j, me, h).wait_send()
                    relay(j, chip_of[j], h).wait_send()
                    d2d(j, chip_of[j], whole(c), h).wait_send()
                    d2d(2 + j, chip_of[2], quarter(1 - j), h).wait_send()
            local.wait()

    return pl.pallas_call(
        body,
        grid_spec=pltpu.PrefetchScalarGridSpec(
            num_scalar_prefetch=1, grid=(len(GATHER_PIECES), s // tm),
            in_specs=[pl.BlockSpec((tm, k), lambda p, i, blocks: (i, 0)), HBM],
            out_specs=[pl.BlockSpec((tm, tn), lambda p, i, blocks: (i, blocks[p])), HBM],
            scratch_shapes=[pltpu.VMEM((2, k, tn), BF16)] + [pltpu.SemaphoreType.DMA((2, 2))] * 4
            + [pltpu.SemaphoreType.DMA((4, 2))] * 2 + [pltpu.SemaphoreType.DMA, pltpu.SemaphoreType.DMA((2,))]),
        out_shape=[jax.ShapeDtypeStruct((s, N_CHIPS * nc), BF16), jax.ShapeDtypeStruct((N_CHIPS, k, nc), BF16)],
        name=name, compiler_params=_cparams(),
    )(blocks, a, shard)


def _mm_tn_exchanging(a, b, *, name, shards, tk=2048, side=None):
    s, m = a.shape
    nc = b.shape[1] // shards
    tk = _row_tile(s, tk)
    nk = s // tk
    hm = m // 2

    def body(a_ref, b_ref, part_ref, sib_ref, acc, keep_sem, send_sem, recv_sem):
        j, kk = pl.program_id(0), pl.program_id(1)
        x, y, c, _ = _place()

        def keep(jj, slot):
            mine = pl.ds(c * hm, hm)
            return pltpu.make_async_copy(acc.at[slot, mine], part_ref.at[jj], keep_sem.at[slot])

        def give(jj, slot):
            return pltpu.make_async_remote_copy(
                src_ref=acc.at[slot, pl.ds((1 - c) * hm, hm)], dst_ref=sib_ref.at[jj],
                send_sem=send_sem.at[slot], recv_sem=recv_sem.at[jj], device_id=(x, y, 1 - c), device_id_type=MESH)

        part = lax.dot_general(a_ref[...], b_ref[...], TN, preferred_element_type=F32)
        for slot in range(2):
            @pl.when(j % 2 == slot)
            def _(slot=slot):
                @pl.when(jnp.logical_and(kk == 0, j >= 2))
                def _():
                    keep(j - 2, slot).wait()
                    give(j - 2, slot).wait_send()

                @pl.when(kk == 0)
                def _():
                    acc[slot] = part

                @pl.when(kk > 0)
                def _():
                    acc[slot] += part

                @pl.when(kk == nk - 1)
                def _():
                    keep(j, slot).start()
                    give(j, slot).start()

        @pl.when(jnp.logical_and(j == shards - 1, kk == nk - 1))
        def _():
            for jj in range(shards - 2, shards):
                keep(jj, jj % 2).wait()
                give(jj, jj % 2).wait_send()
            for jj in range(shards):
                give(jj, jj % 2).wait_recv()

    assert shards >= 2
    return _call(
        body, grid=(shards, nk),
        in_specs=[pl.BlockSpec((tk, m), lambda j, kk: (kk, 0)), pl.BlockSpec((tk, nc), lambda j, kk: (kk, j))],
        out_specs=[HBM, HBM],
        out_shape=[jax.ShapeDtypeStruct((shards, hm, nc), F32), jax.ShapeDtypeStruct((shards, hm, nc), F32)],
        scratch=[pltpu.VMEM((2, m, nc), F32), pltpu.SemaphoreType.DMA((2,)), pltpu.SemaphoreType.DMA((2,)),
                 pltpu.SemaphoreType.DMA((shards,))],
        args=(a, b), name=name, side=side)


def _col_tile(cols):
    return cols if cols <= 2048 else 512


def _add_sibling(grad, recv, core, *, name):
    k, r, c = grad.shape
    hr = r // 2
    tr = min(hr, 256)
    tc = _col_tile(c)
    nrb = hr // tr

    def body(core_ref, g_ref, r_ref, o_ref):
        o_ref[...] = (g_ref[...] + r_ref[...]).astype(BF16)

    return pl.pallas_call(
        body,
        grid_spec=pltpu.PrefetchScalarGridSpec(
            num_scalar_prefetch=1, grid=(k, nrb, c // tc),
            in_specs=[pl.BlockSpec((None, tr, tc), lambda kk, i, j, core: (kk, core[0] * nrb + i, j)),
                      pl.BlockSpec((None, tr, tc), lambda kk, i, j, core: (kk, i, j))],
            out_specs=pl.BlockSpec((None, tr, tc), lambda kk, i, j, core: (kk, i, j))),
        out_shape=jax.ShapeDtypeStruct((k, hr, c), BF16), name=name, compiler_params=_cparams(),
    )(core, grad, recv)


def _sum_chips(grad, from_sibling, recv, place, *, name):
    _, hr, c = from_sibling.shape
    tr = min(hr, 256)
    tc = _col_tile(c)
    nrb = hr // tr

    def body(place_ref, g_ref, s_ref, r0_ref, r1_ref, r2_ref, o_ref):
        own = g_ref[...] + s_ref[...]
        o_ref[...] = ((own + r0_ref[...].astype(F32)) + r1_ref[...].astype(F32)) + r2_ref[...].astype(F32)

    def rspec(j):
        return pl.BlockSpec((None, tr, tc), lambda i, jj, place: (j, i, jj))

    return pl.pallas_call(
        body,
        grid_spec=pltpu.PrefetchScalarGridSpec(
            num_scalar_prefetch=1, grid=(nrb, c // tc),
            in_specs=[pl.BlockSpec((None, tr, tc), lambda i, jj, place: (place[0], place[1] * nrb + i, jj)),
                      pl.BlockSpec((None, tr, tc), lambda i, jj, place: (place[0], i, jj)),
                      rspec(0), rspec(1), rspec(2)],
            out_specs=pl.BlockSpec((tr, tc), lambda i, jj, place: (place[1] * nrb + i, jj))),
        out_shape=jax.ShapeDtypeStruct((2 * hr, c), F32), name=name, compiler_params=_cparams(),
    )(place, grad, from_sibling, recv, recv, recv)


def _add_halves(mine, theirs, *, name, side=None):
    k, hr, c = mine.shape
    tr = min(hr, 256)
    tc = _col_tile(c)

    def body(a_ref, b_ref, o_ref):
        o_ref[...] = (a_ref[...] + b_ref[...]).astype(BF16)

    spec = pl.BlockSpec((None, tr, tc), lambda kk, i, j: (kk, i, j))
    (out,), side_outs = _call(body, grid=(k, hr // tr, c // tc), in_specs=[spec, spec], out_specs=[spec],
                              out_shape=[jax.ShapeDtypeStruct((k, hr, c), BF16)], args=(mine, theirs), name=name,
                              side=side)
    return out, side_outs


def _sum_halves(mine, theirs, recv, place, *, name):
    _, hr, c = mine.shape
    tr = min(hr, 256)
    tc = _col_tile(c)
    nrb = hr // tr

    def body(place_ref, a_ref, b_ref, r0_ref, r1_ref, r2_ref, o_ref):
        own = a_ref[...] + b_ref[...]
        o_ref[...] = ((own + r0_ref[...].astype(F32)) + r1_ref[...].astype(F32)) + r2_ref[...].astype(F32)

    def rspec(j):
        return pl.BlockSpec((None, tr, tc), lambda i, jj, place: (j, i, jj))

    own_spec = pl.BlockSpec((None, tr, tc), lambda i, jj, place: (place[0], i, jj))
    return pl.pallas_call(
        body,
        grid_spec=pltpu.PrefetchScalarGridSpec(
            num_scalar_prefetch=1, grid=(nrb, c // tc),
            in_specs=[own_spec, own_spec, rspec(0), rspec(1), rspec(2)],
            out_specs=pl.BlockSpec((tr, tc), lambda i, jj, place: (place[1] * nrb + i, jj))),
        out_shape=jax.ShapeDtypeStruct((2 * hr, c), F32), name=name, compiler_params=_cparams(),
    )(place, mine, theirs, recv, recv, recv)


def _sum_small(small, recv, place):
    _, sr, _ = small.shape

    def body(place_ref, own_ref, r_ref, o_ref):
        acc = own_ref[...]
        for r in range(1, 8):
            acc = acc + r_ref[r]
        o_ref[...] = acc

    return pl.pallas_call(
        body,
        grid_spec=pltpu.PrefetchScalarGridSpec(
            num_scalar_prefetch=1, grid=(1,),
            in_specs=[pl.BlockSpec((None, sr, 128), lambda i, place: (place[2], 0, 0)),
                      pl.BlockSpec((8, sr, 128), lambda i, place: (0, 0, 0))],
            out_specs=pl.BlockSpec((None, sr, 128), lambda i, place: (place[2], 0, 0))),
        out_shape=jax.ShapeDtypeStruct(small.shape, F32), name="sum_small", compiler_params=_cparams(),
    )(place, small, recv)


def _spread_side(vec):
    def copies(ins, outs, sems):
        x, y, c, _ = _place()
        return [pltpu.make_async_remote_copy(
            src_ref=ins[0], dst_ref=outs[0].at[r], send_sem=sems[0].at[r - 1], recv_sem=sems[1].at[r - 1],
            device_id=(x ^ fx, y ^ fy, c ^ fc), device_id_type=MESH)
            for r, (fx, fy, fc) in enumerate(_relations(), start=1)]

    def start(ins, outs, sems):
        for cp in copies(ins, outs, sems):
            cp.start()

    def finish(ins, outs, sems):
        for cp in copies(ins, outs, sems):
            cp.wait()

    return _Side([vec], [jax.ShapeDtypeStruct((8,) + vec.shape, vec.dtype)], [pltpu.SemaphoreType.DMA((7,))] * 2,
                 start, finish)


def _sum_in_device_order(own, spread, place):
    def body(place_ref, own_ref, r_ref, o_ref):
        me = place_ref[2]
        acc = jnp.zeros_like(own_ref[...])
        for d in range(8):
            slot = jnp.where(me == d, 1, me ^ d)
            acc = acc + jnp.where(me == d, own_ref[...], r_ref[slot])
        o_ref[...] = acc

    return pl.pallas_call(
        body,
        grid_spec=pltpu.PrefetchScalarGridSpec(
            num_scalar_prefetch=1, grid=(1,),
            in_specs=[pl.BlockSpec(own.shape, lambda i, place: (0, 0)),
                      pl.BlockSpec(spread.shape, lambda i, place: (0, 0, 0))],
            out_specs=pl.BlockSpec(own.shape, lambda i, place: (0, 0))),
        out_shape=jax.ShapeDtypeStruct(own.shape, F32), name="sum_in_device_order", compiler_params=_cparams(),
    )(place, own, spread)


def _adamw(w, g, m, v, *, name):
    r, c = w.shape
    tr = 256 if r % 256 == 0 else r
    tc = _col_tile(c)
    bc1 = 1.0 - ADAM_B1 ** ADAM_STEP
    bc2 = 1.0 - ADAM_B2 ** ADAM_STEP

    def body(w_ref, g_ref, m_ref, v_ref, d_ref, nm_ref, nv_ref, gout_ref):
        gv = g_ref[...]
        nm = ADAM_B1 * m_ref[...] + (1.0 - ADAM_B1) * gv
        nv = ADAM_B2 * v_ref[...] + (1.0 - ADAM_B2) * (gv * gv)
        d_ref[...] = -ADAM_LR * ((nm / bc1) / (jnp.sqrt(nv / bc2) + ADAM_EPS) + ADAM_WD * w_ref[...])
        nm_ref[...] = nm
        nv_ref[...] = nv
        gout_ref[...] = gv

    spec = pl.BlockSpec((tr, tc), lambda i, j: (i, j))
    outs, _ = _call(body, grid=(r // tr, c // tc), in_specs=[spec] * 4, out_specs=[spec] * 4,
                    out_shape=[jax.ShapeDtypeStruct((r, c), F32)] * 4, args=(w, g, m, v), name=name)
    return outs


SMALL_ORDER = ["a_ws", "a_bs", "a_norm_g", "a_ln_g", "a_ln_b", "kv_norm_g", "b_kv", "b_norm_g", "b_bq",
               "b_sinks", "final_norm_g"]
SHARDED_SMALL = {"a_norm_g", "a_ln_g", "a_ln_b"}
PACK_TILE = 8 * 128


def _rows128(a):
    flat = a.reshape(-1)
    return jnp.pad(flat, (0, (-flat.shape[0]) % PACK_TILE)).reshape(-1, 128)


def _pack_rows(parts, multiple):
    rows = [_rows128(p) for p in parts]
    total = sum(r.shape[0] for r in rows)
    pad = (-total) % multiple
    if pad:
        rows.append(jnp.zeros((pad, 128), rows[0].dtype))
    return jnp.concatenate(rows, axis=0)


def _unpack_rows(packed, shapes):
    out, row = [], 0
    for shp in shapes:
        size = math.prod(shp)
        nrow = -(-size // PACK_TILE) * 8
        out.append(packed[row:row + nrow].reshape(-1)[:size].reshape(shp))
        row += nrow
    return out


WEIGHTS = ["a_norm_g", "a_w_in", "a_ln_g", "a_ln_b", "a_ws", "a_bs", "a_w_out", "kv_norm_g", "w_kv", "b_kv",
           "b_norm_g", "b_w_in", "b_bq", "b_sinks", "b_w_out", "final_norm_g"]
BIG = ["a_w_in", "a_w_out", "w_kv", "b_w_in", "b_w_out"]


class _Reduction:
    def __init__(self, names, partials, core, place, small=None):
        self.names, self.partials, self.core, self.place, self.small = names, partials, core, place, small

    def exchange_side(self):
        return _exchange_side(self.partials)

    def took_exchange(self, from_sibling):
        self.from_sibling = from_sibling
        self.chip_sums = [_add_sibling(g, r, self.core, name="add_sibling_" + n)
                          for g, r, n in zip(self.partials, from_sibling, self.names)]

    def scatter_side(self):
        return _scatter_side(self.chip_sums, self.small)

    def took_scatter(self, arrived):
        big = arrived[:len(self.names)]
        self.halves = [_sum_chips(g, fs, r, self.place, name="sum_chips_" + n)
                       for g, fs, r, n in zip(self.partials, self.from_sibling, big, self.names)]
        self.small_mine = _sum_small(self.small, arrived[-1], self.place) if self.small is not None else None

    def share_side(self):
        return _share_side(self.halves, self.small_mine)

    def took_share(self, shared):
        self.grads = dict(zip(self.names, shared[:len(self.names)]))
        self.small_full = shared[-1] if self.small is not None else None


def _step(x, loss_target, p, m, v):
    xi, yi, ci = lax.axis_index("x"), lax.axis_index("y"), lax.axis_index("c")
    chip = 2 * xi + yi
    device = 4 * xi + 2 * yi + ci
    core = jnp.reshape(ci, (1,)).astype(jnp.int32)
    place = jnp.stack([chip, ci, device]).astype(jnp.int32)
    x, tgt = x[0], loss_target[0]
    s = x.shape[0]
    cos, sin = _rope_tables(s)

    shard2d = {n: p[n].reshape(p[n].shape[-2:]) for n in BIG}
    shard_bf = {n: shard2d[n].astype(BF16) for n in BIG}
    ws = p["a_ws"][0]
    ws_t = jnp.swapaxes(ws, 1, 2)
    bs_t = p["a_bs"][0].T
    kv_norm_g, b_kv = p["kv_norm_g"].reshape(1, -1), p["b_kv"].reshape(1, -1)
    final_norm_g = p["final_norm_g"].reshape(1, -1)

    vec_shapes = [p[n].shape for n in ("a_norm_g", "a_ln_g", "a_ln_b")]
    vec_pack = _pack_rows([p["a_norm_g"], p["a_ln_g"], p["a_ln_b"]], 16)
    (vec_all,) = _comm_call(_gather_side([vec_pack]), "gather_vectors")
    vecs = [_unpack_rows(vec_all[k], vec_shapes) for k in range(N_CHIPS)]
    a_norm_g, a_ln_g, a_ln_b = (jnp.concatenate([vk[t] for vk in vecs], axis=-1) for t in range(3))

    (n_a,) = _rms_fwd(x, [a_norm_g], name="rms_a")
    order = jnp.stack([chip, 2 * (1 - xi) + yi, 2 * xi + (1 - yi), 2 * (1 - xi) + (1 - yi)]).astype(jnp.int32)
    z, a_w_in = _mm_gathering(n_a, shard_bf["a_w_in"], order, name="mm_a_in")
    y, (a_w_out, w_kv) = _gate_fwd(z, a_ln_g, a_ln_b, ws, bs_t,
                                   side=_gather_side([shard_bf["a_w_out"], shard_bf["w_kv"]]))
    a_w_out, w_kv = a_w_out.reshape(A_WIDTH, D_MODEL), w_kv.reshape(D_MODEL, 2 * KV_WIDTH)
    (h1, n_kv, n_b), (b_w_in,) = _mm_residual_norms(
        y, a_w_out, x, [kv_norm_g, p["b_norm_g"]], name="mm_a_out", side=_gather_side([shard_bf["b_w_in"]]))
    kr, vv = _kv_rope(n_kv, w_kv, b_kv, cos, sin)
    zb = _mm_nn(n_b, b_w_in, name="mm_b_in", tn=512, tm=2048, out_dtype=BF16)
    yb, (b_w_out,) = _attn_fwd(zb, kr, vv, cos, sin, p["b_bq"], p["b_sinks"], side=_gather_side([shard_bf["b_w_out"]]))
    b_w_out = b_w_out.reshape(B_WIDTH, D_MODEL)
    loss_blk, dh2, dh2b, d_final_g = _mm_residual_loss(yb, b_w_out, h1, tgt, final_norm_g, name="mm_b_out")

    d_b_w_out = _mm_tn(yb, dh2b, name="mm_d_b_w_out", tm=B_WIDTH, tn=D_MODEL)
    red_bo = _Reduction(["b_w_out"], [d_b_w_out.reshape(N_CHIPS, B_WIDTH // N_CHIPS, D_MODEL)], core, place)
    dyb, got = _mm_nt(dh2b, b_w_out, name="mm_dyb", tm=1024, out_dtype=BF16, side=red_bo.exchange_side())
    red_bo.took_exchange(got)
    (dzb, dk_rot, dv, d_bq, d_sinks), got = _attn_bwd(zb, dyb, kr, vv, cos, sin, p["b_bq"], p["b_sinks"],
                                                      side=red_bo.scatter_side())
    red_bo.took_scatter(got)
    dkv, d_b_kv = _kv_rope_bwd(dk_rot, dv, cos, sin)
    d_b_w_in = _mm_tn(n_b, dzb, name="mm_d_b_w_in", tm=D_MODEL, tn=512, tk=4096, shards=N_CHIPS)
    d_w_kv, got = _mm_tn(n_kv, dkv, name="mm_d_w_kv", tm=D_MODEL, tn=2 * KV_WIDTH, tk=4096,
                         side=red_bo.share_side())
    red_bo.took_share(got)
    red_bi = _Reduction(["b_w_in", "w_kv"], [d_b_w_in, d_w_kv.reshape(N_CHIPS, D_MODEL // N_CHIPS, 2 * KV_WIDTH)],
                        core, place)
    (dh1, dh1b, d_kv_g, d_b_g), got = _mm_nt_rms_bwd(
        [(dkv, w_kv, kv_norm_g), (dzb, b_w_in, p["b_norm_g"])], h1, dh2, name="mm_dn_b", tm=512,
        side=red_bi.exchange_side())
    red_bi.took_exchange(got)

    d_a_w_out = _mm_tn(y, dh1b, name="mm_d_a_w_out", tm=1024, tn=D_MODEL)
    red_ao = _Reduction(["a_w_out"], [d_a_w_out.reshape(N_CHIPS, A_WIDTH // N_CHIPS, D_MODEL)], core, place)
    dy, got = _mm_nt(dh1b, a_w_out, name="mm_dy", tn=1024, tm=1024, out_dtype=BF16, side=red_ao.exchange_side())
    red_ao.took_exchange(got)
    sides = [red_bi.scatter_side(), red_ao.scatter_side()]
    (dz, d_ln_g, d_ln_b, d_ws, d_bs_t), got = _gate_bwd(z, dy, a_ln_g, a_ln_b, ws, ws_t, bs_t, side=_join(sides))
    got = _split(got, sides)
    red_bi.took_scatter(got[0])
    red_ao.took_scatter(got[1])
    small = {
        "a_ws": d_ws, "a_bs": d_bs_t.T, "a_ln_g": d_ln_g, "a_ln_b": d_ln_b,
        "kv_norm_g": d_kv_g, "b_kv": d_b_kv, "b_norm_g": d_b_g, "b_bq": d_bq,
        "b_sinks": d_sinks[0:1, :N_Q_HEADS], "final_norm_g": d_final_g,
    }
    packed = [n for n in SMALL_ORDER if n != "a_norm_g"]
    small_shapes = [small[n].shape for n in packed] + [(1, 1)]
    small_pack = _pack_rows([small[n] for n in packed] + [loss_blk[0:1, 0:1]], 64)
    seg = small_pack.shape[0] // 8
    small_pack = small_pack.reshape(8, seg, 128)
    sides = [red_bi.share_side(), red_ao.share_side(), _small_scatter_side(small_pack)]
    (d_a_w_in, from_sibling), got = _mm_tn_exchanging(n_a, dz, name="mm_d_a_w_in", shards=N_CHIPS, side=_join(sides))
    got = _split(got, sides)
    red_bi.took_share(got[0])
    red_ao.took_share(got[1])
    small_mine = _sum_small(small_pack, got[2][0], place)

    chip_sum, (small_all,) = _add_halves(d_a_w_in, from_sibling, name="add_sibling_a_w_in",
                                         side=_small_share_side(small_mine))
    (dx, _, d_a_g), (arrived,) = _mm_nt_rms_bwd([(dz, a_w_in, a_norm_g)], x, dh1, name="mm_dn_a", tm=256,
                                                side=_scatter_side([chip_sum]))
    half_ai = _sum_halves(d_a_w_in, from_sibling, arrived, place, name="sum_chips_a_w_in")
    d_a_g = _rows128(d_a_g)
    sides = [_share_side([half_ai]), _spread_side(d_a_g)]
    got = _split(_comm_call(_join(sides), "share_last"), sides)
    grad_ai = got[0][0]
    small_full = dict(zip(packed + ["loss"], _unpack_rows(small_all.reshape(8 * seg, 128), small_shapes)))
    small_full["a_norm_g"] = _sum_in_device_order(d_a_g, got[1][0], place).reshape(1, -1)
    loss = small_full["loss"].reshape(())

    grad_big = {**red_bo.grads, **red_bi.grads, **red_ao.grads, "a_w_in": grad_ai}
    grads = {}
    for n in SMALL_ORDER:
        gfull = small_full[n]
        if n in SHARDED_SMALL:
            width = p[n].shape[-1]
            gfull = lax.dynamic_slice_in_dim(gfull, chip * width, width, axis=-1)
        grads[n] = gfull.reshape(p[n].shape)

    delta, new_m, new_v = {}, {}, {}
    for n in BIG:
        d, nm, nv, g = _adamw(shard2d[n], grad_big[n], m[n].reshape(shard2d[n].shape),
                              v[n].reshape(shard2d[n].shape), name="adamw_" + n)
        delta[n], new_m[n], new_v[n] = d.reshape(p[n].shape), nm.reshape(p[n].shape), nv.reshape(p[n].shape)
        grads[n] = g.reshape(p[n].shape)
    shapes = [p[n].shape for n in SMALL_ORDER]
    packs = [_pack_rows([src[n] for n in SMALL_ORDER], 8) for src in (p, grads, m, v)]
    outs = _adamw(*packs, name="adamw_small")[:3]
    for res, packed in zip((delta, new_m, new_v), outs):
        for n, val in zip(SMALL_ORDER, _unpack_rows(packed, shapes)):
            res[n] = val

    return (loss, dx[None], *[grads[n] for n in WEIGHTS], *[delta[n] for n in WEIGHTS],
            *[new_m[n] for n in WEIGHTS], *[new_v[n] for n in WEIGHTS])


def kernel(x, a_norm_g, a_w_in, a_ln_g, a_ln_b, a_ws, a_bs, a_w_out, kv_norm_g, w_kv, b_kv, b_norm_g, b_w_in, b_bq, b_sinks, b_w_out, final_norm_g, loss_target, m_a_norm_g, m_a_w_in, m_a_ln_g, m_a_ln_b, m_a_ws, m_a_bs, m_a_w_out, m_kv_norm_g, m_w_kv, m_b_kv, m_b_norm_g, m_b_w_in, m_b_bq, m_b_sinks, m_b_w_out, m_final_norm_g, v_a_norm_g, v_a_w_in, v_a_ln_g, v_a_ln_b, v_a_ws, v_a_bs, v_a_w_out, v_kv_norm_g, v_w_kv, v_b_kv, v_b_norm_g, v_b_w_in, v_b_bq, v_b_sinks, v_b_w_out, v_final_norm_g):
    p = dict(a_norm_g=a_norm_g, a_w_in=a_w_in, a_ln_g=a_ln_g, a_ln_b=a_ln_b, a_ws=a_ws, a_bs=a_bs, a_w_out=a_w_out,
             kv_norm_g=kv_norm_g, w_kv=w_kv, b_kv=b_kv, b_norm_g=b_norm_g, b_w_in=b_w_in, b_bq=b_bq, b_sinks=b_sinks,
             b_w_out=b_w_out, final_norm_g=final_norm_g)
    m = dict(a_norm_g=m_a_norm_g, a_w_in=m_a_w_in, a_ln_g=m_a_ln_g, a_ln_b=m_a_ln_b, a_ws=m_a_ws, a_bs=m_a_bs,
             a_w_out=m_a_w_out, kv_norm_g=m_kv_norm_g, w_kv=m_w_kv, b_kv=m_b_kv, b_norm_g=m_b_norm_g, b_w_in=m_b_w_in,
             b_bq=m_b_bq, b_sinks=m_b_sinks, b_w_out=m_b_w_out, final_norm_g=m_final_norm_g)
    v = dict(a_norm_g=v_a_norm_g, a_w_in=v_a_w_in, a_ln_g=v_a_ln_g, a_ln_b=v_a_ln_b, a_ws=v_a_ws, a_bs=v_a_bs,
             a_w_out=v_a_w_out, kv_norm_g=v_kv_norm_g, w_kv=v_w_kv, b_kv=v_b_kv, b_norm_g=v_b_norm_g, b_w_in=v_b_w_in,
             b_bq=v_b_bq, b_sinks=v_b_sinks, b_w_out=v_b_w_out, final_norm_g=v_final_norm_g)
    return _step(x, loss_target, p, m, v)
```

```python
import functools
import math

import jax
import jax.numpy as jnp
from jax import lax
from jax.experimental import pallas as pl
from jax.experimental.pallas import tpu as pltpu

F32 = jnp.float32
BF16 = jnp.bfloat16

D_MODEL = 1024
CHUNK = 128
A_WIDTH = 2048
A_GROUPS = 16
HEAD_DIM = 64
N_Q_HEADS = 16
N_KV_HEADS = 2
Q_PER_KV = 8
B_WIDTH = 1024
KV_WIDTH = 128
ROPE_THETA = 10000.0
EPS = 1e-5
N_CHIPS = 4

ADAM_LR = 0.001
ADAM_B1 = 0.9
ADAM_B2 = 0.999
ADAM_EPS = 1e-08
ADAM_WD = 0.01
ADAM_STEP = 10

VMEM_LIMIT = 48 * 1024 * 1024
MESH = pl.DeviceIdType.MESH
NEG_BIG = -1e30
HBM = pl.BlockSpec(memory_space=pl.ANY)

NN = (((1,), (0,)), ((), ()))
NT = (((1,), (1,)), ((), ()))
TN = (((0,), (0,)), ((), ()))


def _cparams(**kw):
    return pltpu.CompilerParams(vmem_limit_bytes=VMEM_LIMIT, **kw)


class _Side:
    def __init__(self, ins, out_shapes, sems, start, finish, aliases=None, passing=None):
        self.ins, self.out_shapes, self.sems = list(ins), list(out_shapes), list(sems)
        self.start, self.finish = start, finish
        self.passing = passing or (lambda ins, outs, sems: None)
        self.aliases = dict(aliases or {})


def _join(sides):
    sides = [s for s in sides if s is not None]
    if not sides:
        return None
    offs, i, o, m = [], 0, 0, 0
    for s in sides:
        offs.append((i, o, m))
        i, o, m = i + len(s.ins), o + len(s.out_shapes), m + len(s.sems)

    def run(which):
        def go(ins, outs, sems):
            for s, (a, b, c) in zip(sides, offs):
                getattr(s, which)(ins[a:a + len(s.ins)], outs[b:b + len(s.out_shapes)], sems[c:c + len(s.sems)])
        return go

    aliases = {}
    for s, (a, b, _) in zip(sides, offs):
        aliases.update({a + k: b + v for k, v in s.aliases.items()})
    return _Side([x for s in sides for x in s.ins], [x for s in sides for x in s.out_shapes],
                 [x for s in sides for x in s.sems], run("start"), run("finish"), aliases, run("passing"))


def _split(side_outs, sides):
    out, pos = [], 0
    for s in sides:
        out.append(list(side_outs[pos:pos + len(s.out_shapes)]))
        pos += len(s.out_shapes)
    return out


def _call(body, *, grid, in_specs, out_specs, out_shape, args, name, scratch=(), side=None):
    in_specs, out_specs, out_shape, scratch = list(in_specs), list(out_specs), list(out_shape), list(scratch)
    if side is None:
        res = pl.pallas_call(body, grid=grid, in_specs=in_specs, out_specs=out_specs, out_shape=out_shape,
                             scratch_shapes=scratch, name=name, compiler_params=_cparams())(*args)
        return list(res), []
    n_in, n_out, n_sc = len(in_specs), len(out_specs), len(scratch)
    s_in, s_out = len(side.ins), len(side.out_shapes)

    def wrapped(*refs):
        ins, refs = refs[:n_in], refs[n_in:]
        side_ins, refs = refs[:s_in], refs[s_in:]
        outs, refs = refs[:n_out], refs[n_out:]
        side_outs, refs = refs[:s_out], refs[s_out:]
        scr, side_sems = refs[:n_sc], refs[n_sc:]
        step = 0
        for a, g in enumerate(grid):
            step = step * g + pl.program_id(a)
        steps = math.prod(grid)

        @pl.when(step == 0)
        def _():
            side.start(side_ins, side_outs, side_sems)

        body(*ins, *outs, *scr)

        @pl.when(step == (3 * (steps - 1)) // 4)
        def _():
            side.passing(side_ins, side_outs, side_sems)

        @pl.when(step == steps - 1)
        def _():
            side.finish(side_ins, side_outs, side_sems)

    res = pl.pallas_call(
        wrapped, grid=grid, in_specs=in_specs + [HBM] * s_in, out_specs=out_specs + [HBM] * s_out,
        out_shape=out_shape + side.out_shapes, scratch_shapes=scratch + side.sems,
        input_output_aliases={n_in + k: n_out + v for k, v in side.aliases.items()},
        name=name, compiler_params=_cparams(),
    )(*args, *side.ins)
    return list(res[:n_out]), list(res[n_out:])


def _comm_call(side, name):
    s_in, s_out = len(side.ins), len(side.out_shapes)

    def body(*refs):
        ins, outs, sems = refs[:s_in], refs[s_in:s_in + s_out], refs[s_in + s_out:]
        side.start(ins, outs, sems)
        side.passing(ins, outs, sems)
        side.finish(ins, outs, sems)

    return list(pl.pallas_call(
        body, in_specs=[HBM] * s_in, out_specs=[HBM] * s_out, out_shape=side.out_shapes, scratch_shapes=side.sems,
        input_output_aliases=side.aliases, name=name,
    )(*side.ins))


def _matmul(a, b, *, dims, grid, a_spec, b_spec, o_spec, out_shape, name, acc_axis=None,
            residual=None, r_spec=None, side=None):
    has_res = residual is not None

    def body(*refs):
        if has_res:
            a_ref, b_ref, r_ref, o_ref = refs
        else:
            a_ref, b_ref, o_ref = refs
        part = lax.dot_general(a_ref[...], b_ref[...], dims, preferred_element_type=F32)
        if acc_axis is None:
            if has_res:
                part = part + r_ref[...]
            o_ref[...] = part.astype(o_ref.dtype)
        else:
            k = pl.program_id(acc_axis)

            @pl.when(k == 0)
            def _():
                o_ref[...] = part

            @pl.when(k > 0)
            def _():
                o_ref[...] += part

    in_specs = [a_spec, b_spec] + ([r_spec] if has_res else [])
    args = (a, b) + ((residual,) if has_res else ())
    (out,), side_outs = _call(body, grid=grid, in_specs=in_specs, out_specs=[o_spec], out_shape=[out_shape],
                              args=args, name=name, side=side)
    return (out, side_outs) if side is not None else out


def _row_tile(s, want):
    return min(s, want)


def _mm_nn(a, b, *, name, tn, out_dtype=F32, residual=None, tm=512, side=None):
    s, k = a.shape
    tm = _row_tile(s, tm)
    if b.ndim == 3:
        nsh, _, nc = b.shape
        npb = nc // tn
        n = nsh * nc
        b_spec = pl.BlockSpec((None, k, tn), lambda i, j: (j // npb, 0, j % npb))
    else:
        n = b.shape[1]
        b_spec = pl.BlockSpec((k, tn), lambda i, j: (0, j))
    return _matmul(
        a, b, dims=NN, grid=(s // tm, n // tn),
        a_spec=pl.BlockSpec((tm, k), lambda i, j: (i, 0)), b_spec=b_spec,
        o_spec=pl.BlockSpec((tm, tn), lambda i, j: (i, j)),
        out_shape=jax.ShapeDtypeStruct((s, n), out_dtype), name=name, side=side,
        residual=residual, r_spec=pl.BlockSpec((tm, tn), lambda i, j: (i, j)) if residual is not None else None)


def _mm_nt(a, b, *, name, tn=None, tm=512, out_dtype=F32, side=None):
    s, k = a.shape
    tm = _row_tile(s, tm)
    n = b.shape[0]
    tn = n if tn is None else tn
    return _matmul(
        a, b, dims=NT, grid=(s // tm, n // tn),
        a_spec=pl.BlockSpec((tm, k), lambda i, j: (i, 0)),
        b_spec=pl.BlockSpec((tn, k), lambda i, j: (j, 0)),
        o_spec=pl.BlockSpec((tm, tn), lambda i, j: (i, j)),
        out_shape=jax.ShapeDtypeStruct((s, n), out_dtype), name=name, side=side)


def _mm_tn(a, b, *, name, tm, tn, tk=2048, shards=None, side=None):
    s, m = a.shape
    n = b.shape[1]
    tk = _row_tile(s, tk)
    if shards is None:
        o_spec = pl.BlockSpec((tm, tn), lambda i, j, kk: (i, j))
        out_shape = jax.ShapeDtypeStruct((m, n), F32)
    else:
        assert tm == m
        nc = n // shards
        npb = nc // tn
        o_spec = pl.BlockSpec((None, m, tn), lambda i, j, kk: (j // npb, 0, j % npb))
        out_shape = jax.ShapeDtypeStruct((shards, m, nc), F32)
    return _matmul(
        a, b, dims=TN, grid=(m // tm, n // tn, s // tk), acc_axis=2,
        a_spec=pl.BlockSpec((tk, tm), lambda i, j, kk: (kk, i)),
        b_spec=pl.BlockSpec((tk, tn), lambda i, j, kk: (kk, j)),
        o_spec=o_spec, out_shape=out_shape, name=name, side=side)


def _rstd(x):
    return lax.rsqrt(jnp.mean(x * x, axis=-1, keepdims=True) + EPS)


def _rms_fwd(x, gains, *, name, tr=1024):
    s, d = x.shape
    tr = _row_tile(s, tr)
    ng = len(gains)

    def body(*refs):
        xv = refs[0][...]
        xh = xv * _rstd(xv)
        for t in range(ng):
            refs[1 + ng + t][...] = (xh * refs[1 + t][...]).astype(BF16)

    row = pl.BlockSpec((tr, d), lambda i: (i, 0))
    vec = pl.BlockSpec((1, d), lambda i: (0, 0))
    outs, _ = _call(body, grid=(s // tr,), in_specs=[row] + [vec] * ng, out_specs=[row] * ng,
                    out_shape=[jax.ShapeDtypeStruct((s, d), BF16)] * ng, args=(x, *gains), name=name)
    return outs


def _accumulate(i, ref, value):
    @pl.when(i == 0)
    def _():
        ref[...] = value

    @pl.when(i > 0)
    def _():
        ref[...] += value


def _mm_residual_norms(y, w, res, gains, *, name, tm=512, side=None):
    s, k = y.shape
    d = w.shape[1]
    tm = _row_tile(s, tm)
    ng = len(gains)

    def body(y_ref, w_ref, r_ref, *rest):
        g_refs, h_ref, n_refs = rest[:ng], rest[ng], rest[ng + 1:]
        h = r_ref[...] + jnp.dot(y_ref[...], w_ref[...], preferred_element_type=F32)
        h_ref[...] = h
        xh = h * _rstd(h)
        for t in range(ng):
            n_refs[t][...] = (xh * g_refs[t][...]).astype(BF16)

    row = pl.BlockSpec((tm, d), lambda i: (i, 0))
    vec = pl.BlockSpec((1, d), lambda i: (0, 0))
    return _call(
        body, grid=(s // tm,),
        in_specs=[pl.BlockSpec((tm, k), lambda i: (i, 0)), pl.BlockSpec((k, d), lambda i: (0, 0)), row] + [vec] * ng,
        out_specs=[row] * (1 + ng),
        out_shape=[jax.ShapeDtypeStruct((s, d), F32)] + [jax.ShapeDtypeStruct((s, d), BF16)] * ng,
        args=(y, w, res, *gains), name=name, side=side)


def _mm_residual_loss(y, w, res, tgt, gain, *, name, tm=512):
    s, k = y.shape
    d = w.shape[1]
    tm = _row_tile(s, tm)

    def body(y_ref, w_ref, r_ref, t_ref, g_ref, loss_ref, dh_ref, dhb_ref, dg_ref):
        i = pl.program_id(0)
        hv = r_ref[...] + jnp.dot(y_ref[...], w_ref[...], preferred_element_type=F32)
        g = g_ref[...]
        r = _rstd(hv)
        xh = hv * r
        diff = xh * g - t_ref[...]
        part = 0.5 / d * jnp.sum(jnp.sum(diff * diff, axis=-1, keepdims=True), axis=0, keepdims=True)
        dout = diff * (1.0 / d)
        a = dout * g
        dh = r * (a - xh * jnp.mean(a * xh, axis=-1, keepdims=True))
        dh_ref[...] = dh
        dhb_ref[...] = dh.astype(BF16)
        _accumulate(i, dg_ref, jnp.sum(dout * xh, axis=0, keepdims=True))
        _accumulate(i, loss_ref, jnp.broadcast_to(part, (8, 128)))

    row = pl.BlockSpec((tm, d), lambda i: (i, 0))
    vec = pl.BlockSpec((1, d), lambda i: (0, 0))
    outs, _ = _call(
        body, grid=(s // tm,),
        in_specs=[pl.BlockSpec((tm, k), lambda i: (i, 0)), pl.BlockSpec((k, d), lambda i: (0, 0)), row, row, vec],
        out_specs=[pl.BlockSpec((8, 128), lambda i: (0, 0)), row, row, vec],
        out_shape=[jax.ShapeDtypeStruct((8, 128), F32), jax.ShapeDtypeStruct((s, d), F32),
                   jax.ShapeDtypeStruct((s, d), BF16), jax.ShapeDtypeStruct((1, d), F32)],
        args=(y, w, res, tgt, gain), name=name)
    return outs


def _mm_nt_rms_bwd(terms, x, dres, *, name, tm, side=None):
    s, d = x.shape
    tm = _row_tile(s, tm)
    nt = len(terms)

    def body(*refs):
        a_refs, b_refs, g_refs = refs[0:3 * nt:3], refs[1:3 * nt:3], refs[2:3 * nt:3]
        x_ref, dres_ref = refs[3 * nt], refs[3 * nt + 1]
        dx_ref, dxb_ref = refs[3 * nt + 2], refs[3 * nt + 3]
        dg_refs = refs[3 * nt + 4:]
        i = pl.program_id(0)
        xv = x_ref[...]
        r = _rstd(xv)
        xh = xv * r
        acc = jnp.zeros_like(xv)
        for t in range(nt):
            b_ref = b_refs[t]
            if len(b_ref.shape) == 3:
                kc = b_ref.shape[2]
                dn = None
                for sh in range(b_ref.shape[0]):
                    part = lax.dot_general(a_refs[t][:, sh * kc:(sh + 1) * kc], b_ref[sh], NT, preferred_element_type=F32)
                    dn = part if dn is None else dn + part
            else:
                dn = lax.dot_general(a_refs[t][...], b_ref[...], NT, preferred_element_type=F32)
            acc = acc + dn * g_refs[t][...]
            _accumulate(i, dg_refs[t], jnp.sum(dn * xh, axis=0, keepdims=True))
        dx = dres_ref[...] + r * (acc - xh * jnp.mean(acc * xh, axis=-1, keepdims=True))
        dx_ref[...] = dx
        dxb_ref[...] = dx.astype(BF16)

    row = pl.BlockSpec((tm, d), lambda i: (i, 0))
    vec = pl.BlockSpec((1, d), lambda i: (0, 0))
    in_specs, args = [], []
    for a, b, g in terms:
        in_specs += [pl.BlockSpec((tm, a.shape[1]), lambda i: (i, 0)),
                     pl.BlockSpec(b.shape, (lambda i: (0, 0, 0)) if b.ndim == 3 else (lambda i: (0, 0))), vec]
        args += [a, b, g]
    return _call(
        body, grid=(s // tm,), in_specs=in_specs + [row, row], out_specs=[row, row] + [vec] * nt,
        out_shape=[jax.ShapeDtypeStruct((s, d), F32), jax.ShapeDtypeStruct((s, d), BF16)]
        + [jax.ShapeDtypeStruct((1, d), F32)] * nt,
        args=(*args, x, dres), name=name, side=side)


def _causal_mask(transposed=False):
    row = lax.broadcasted_iota(jnp.int32, (CHUNK, CHUNK), 0)
    col = lax.broadcasted_iota(jnp.int32, (CHUNK, CHUNK), 1)
    return col >= row if transposed else col <= row


def _silu_parts(g):
    sg = jax.nn.sigmoid(g)
    return g * sg, sg * (1.0 + g * (1.0 - sg))


def _gate_fwd(z, ln_g, ln_b, ws, bs_t, *, tr=512, side=None):
    s = z.shape[0]
    tr = _row_tile(s, tr)
    w = A_WIDTH

    def body(u_ref, v_ref, g_ref, lg_ref, lb_ref, ws_ref, bst_ref, y_ref):
        v = v_ref[...].astype(F32)
        mu = jnp.mean(v, axis=-1, keepdims=True)
        xc = v - mu
        rs = lax.rsqrt(jnp.mean(xc * xc, axis=-1, keepdims=True) + EPS)
        vln = (xc * rs * lg_ref[...] + lb_ref[...]).astype(BF16)
        mask = _causal_mask()
        for grp in range(A_GROUPS):
            cols = slice(grp * CHUNK, (grp + 1) * CHUNK)
            wsm = jnp.where(mask, ws_ref[grp], 0.0).astype(BF16)
            bcol = bst_ref[:, grp:grp + 1]
            for ci in range(tr // CHUNK):
                rows = slice(ci * CHUNK, (ci + 1) * CHUNK)
                sv = jnp.dot(wsm, vln[rows, cols], preferred_element_type=F32) + bcol
                gv = g_ref[rows, cols].astype(F32)
                y_ref[rows, cols] = (u_ref[rows, cols].astype(F32) * sv * (gv * jax.nn.sigmoid(gv))).astype(BF16)

    vec = pl.BlockSpec((1, w), lambda i: (0, 0))
    (y,), side_outs = _call(
        body, grid=(s // tr,),
        in_specs=[pl.BlockSpec((tr, w), lambda i: (i, 0)), pl.BlockSpec((tr, w), lambda i: (i, 1)),
                  pl.BlockSpec((tr, w), lambda i: (i, 2)), vec, vec,
                  pl.BlockSpec((A_GROUPS, CHUNK, CHUNK), lambda i: (0, 0, 0)),
                  pl.BlockSpec((CHUNK, A_GROUPS), lambda i: (0, 0))],
        out_specs=[pl.BlockSpec((tr, w), lambda i: (i, 0))],
        out_shape=[jax.ShapeDtypeStruct((s, w), BF16)], args=(z, z, z, ln_g, ln_b, ws, bs_t), name="gate_fwd",
        side=side)
    return y, side_outs


def _gate_bwd(z, dy, ln_g, ln_b, ws, ws_t, bs_t, *, tr=256, side=None):
    s = z.shape[0]
    tr = _row_tile(s, tr)
    w = A_WIDTH
    nsteps = s // tr

    def body(u_ref, v_ref, g_ref, dy_ref, lg_ref, lb_ref, ws_ref, wst_ref, bst_ref,
             dz_ref, dlg_ref, dlb_ref, dws_ref, dbst_ref, dvln_sc, dsv_sc):
        i = pl.program_id(0)

        @pl.when(i == 0)
        def _():
            dws_ref[...] = jnp.zeros_like(dws_ref)
            dsv_sc[...] = jnp.zeros_like(dsv_sc)

        v = v_ref[...].astype(F32)
        mu = jnp.mean(v, axis=-1, keepdims=True)
        xc = v - mu
        rs = lax.rsqrt(jnp.mean(xc * xc, axis=-1, keepdims=True) + EPS)
        xh = xc * rs
        lg = lg_ref[...]
        vln = (xh * lg + lb_ref[...]).astype(BF16)
        mask = _causal_mask()
        mask_t = _causal_mask(transposed=True)
        for grp in range(A_GROUPS):
            cols = slice(grp * CHUNK, (grp + 1) * CHUNK)
            wsm = jnp.where(mask, ws_ref[grp], 0.0).astype(BF16)
            wsm_t = jnp.where(mask_t, wst_ref[grp], 0.0).astype(BF16)
            bcol = bst_ref[:, grp:grp + 1]
            for ci in range(tr // CHUNK):
                rows = slice(ci * CHUNK, (ci + 1) * CHUNK)
                vb = vln[rows, cols]
                sv = jnp.dot(wsm, vb, preferred_element_type=F32) + bcol
                uv = u_ref[rows, cols].astype(F32)
                silu, dsilu = _silu_parts(g_ref[rows, cols].astype(F32))
                dyv = dy_ref[rows, cols].astype(F32)
                dyu = dyv * uv
                dz_ref[rows, cols] = (dyv * sv * silu).astype(BF16)
                dz_ref[rows, 2 * w + grp * CHUNK:2 * w + (grp + 1) * CHUNK] = (dyu * sv * dsilu).astype(BF16)
                dsv = dyu * silu
                dsvb = dsv.astype(BF16)
                dvln_sc[rows, cols] = jnp.dot(wsm_t, dsvb, preferred_element_type=F32)
                dws_ref[grp] += lax.dot_general(dsvb, vb, NT, preferred_element_type=F32)
                dsv_sc[grp] += dsv
        dvln = dvln_sc[...]
        dlg_t = jnp.sum(dvln * xh, axis=0, keepdims=True)
        dlb_t = jnp.sum(dvln, axis=0, keepdims=True)
        a = dvln * lg
        dv = rs * (a - jnp.mean(a, axis=-1, keepdims=True) - xh * jnp.mean(a * xh, axis=-1, keepdims=True))
        dz_ref[:, w:2 * w] = dv.astype(BF16)

        @pl.when(i == 0)
        def _():
            dlg_ref[...] = dlg_t
            dlb_ref[...] = dlb_t

        @pl.when(i > 0)
        def _():
            dlg_ref[...] += dlg_t
            dlb_ref[...] += dlb_t

        @pl.when(i == nsteps - 1)
        def _():
            for grp in range(A_GROUPS):
                dws_ref[grp] = jnp.where(mask, dws_ref[grp], 0.0)
                dbst_ref[:, grp:grp + 1] = jnp.sum(dsv_sc[grp], axis=-1, keepdims=True)

    vec = pl.BlockSpec((1, w), lambda i: (0, 0))
    wsspec = pl.BlockSpec((A_GROUPS, CHUNK, CHUNK), lambda i: (0, 0, 0))
    bsspec = pl.BlockSpec((CHUNK, A_GROUPS), lambda i: (0, 0))
    return _call(
        body, grid=(nsteps,),
        in_specs=[pl.BlockSpec((tr, w), lambda i: (i, 0)), pl.BlockSpec((tr, w), lambda i: (i, 1)),
                  pl.BlockSpec((tr, w), lambda i: (i, 2)), pl.BlockSpec((tr, w), lambda i: (i, 0)),
                  vec, vec, wsspec, wsspec, bsspec],
        out_specs=[pl.BlockSpec((tr, 3 * w), lambda i: (i, 0)), vec, vec, wsspec, bsspec],
        out_shape=[jax.ShapeDtypeStruct((s, 3 * w), BF16), jax.ShapeDtypeStruct((1, w), F32),
                   jax.ShapeDtypeStruct((1, w), F32), jax.ShapeDtypeStruct((A_GROUPS, CHUNK, CHUNK), F32),
                   jax.ShapeDtypeStruct((CHUNK, A_GROUPS), F32)],
        scratch=[pltpu.VMEM((tr, w), F32), pltpu.VMEM((A_GROUPS, CHUNK, CHUNK), F32)],
        args=(z, z, z, dy, ln_g, ln_b, ws, ws_t, bs_t), name="gate_bwd", side=side)


HEADS_PER_BLOCK = 128 // HEAD_DIM
BLOCKS_PER_KV = Q_PER_KV // HEADS_PER_BLOCK
SCALE = HEAD_DIM ** -0.5
LOG2_E = math.log2(math.e)


def _rope_tables(s):
    lane = jnp.arange(128)
    inv_freq = ROPE_THETA ** (-(2 * (lane % (HEAD_DIM // 2))).astype(F32) / HEAD_DIM)
    sign = jnp.where(lane % HEAD_DIM < HEAD_DIM // 2, -1.0, 1.0).astype(F32)
    ang = jnp.arange(s, dtype=F32)[:, None] * inv_freq[None, :]
    return jnp.cos(ang), jnp.sin(ang) * sign[None, :]


def _swap_halves(x):
    n = x.shape[-1]
    lane = lax.broadcasted_iota(jnp.int32, x.shape, x.ndim - 1)
    first = (lane % HEAD_DIM) < (HEAD_DIM // 2)
    return jnp.where(first, pltpu.roll(x, n - HEAD_DIM // 2, x.ndim - 1), pltpu.roll(x, HEAD_DIM // 2, x.ndim - 1))


def _left_half(rows):
    return lax.broadcasted_iota(jnp.int32, (rows, 128), 1) < HEAD_DIM


def _dup_heads(x):
    left = _left_half(x.shape[0])
    swapped = pltpu.roll(x, HEAD_DIM, 1)
    return jnp.concatenate([jnp.where(left, x, swapped), jnp.where(left, swapped, x)], axis=-1)


def _fold_heads(a):
    b0, b1 = a[:, :128], a[:, 128:]
    f0 = b0 + pltpu.roll(b0, HEAD_DIM, 1)
    f1 = b1 + pltpu.roll(b1, HEAD_DIM, 1)
    return jnp.where(_left_half(a.shape[0]), f0, f1)


def _kv_rope(n_kv, w_kv, b_kv, cos, sin, *, tr=2048):
    s, d = n_kv.shape
    tr = _row_tile(s, tr)

    def body(n_ref, w_ref, b_ref, c_ref, s_ref, k_ref, v_ref):
        x = jnp.dot(n_ref[...], w_ref[...], preferred_element_type=F32) + b_ref[...]
        k = x[:, :KV_WIDTH]
        k_ref[...] = _dup_heads(k * c_ref[...] + _swap_halves(k) * s_ref[...]).astype(BF16)
        v_ref[...] = _dup_heads(x[:, KV_WIDTH:]).astype(BF16)

    tab = pl.BlockSpec((tr, KV_WIDTH), lambda i: (i, 0))
    wide = pl.BlockSpec((tr, 2 * KV_WIDTH), lambda i: (i, 0))
    outs, _ = _call(body, grid=(s // tr,),
                    in_specs=[pl.BlockSpec((tr, d), lambda i: (i, 0)), pl.BlockSpec((d, 2 * KV_WIDTH), lambda i: (0, 0)),
                              pl.BlockSpec((1, 2 * KV_WIDTH), lambda i: (0, 0)), tab, tab],
                    out_specs=[wide, wide], out_shape=[jax.ShapeDtypeStruct((s, 2 * KV_WIDTH), BF16)] * 2,
                    args=(n_kv, w_kv, b_kv, cos, sin), name="kv_rope")
    return outs


def _kv_rope_bwd(dk2, dv2, cos, sin, *, tr=2048):
    s = dk2.shape[0]
    tr = _row_tile(s, tr)

    def body(dk_ref, dv_ref, c_ref, s_ref, dkv_ref, db_ref):
        i = pl.program_id(0)
        d = _fold_heads(dk_ref[...])
        dk = d * c_ref[...] + _swap_halves(d * s_ref[...])
        dvv = _fold_heads(dv_ref[...])
        dkv_ref[:, :KV_WIDTH] = dk.astype(BF16)
        dkv_ref[:, KV_WIDTH:] = dvv.astype(BF16)
        sk = jnp.sum(dk, axis=0, keepdims=True)
        sv = jnp.sum(dvv, axis=0, keepdims=True)

        @pl.when(i == 0)
        def _():
            db_ref[:, :KV_WIDTH] = sk
            db_ref[:, KV_WIDTH:] = sv

        @pl.when(i > 0)
        def _():
            db_ref[:, :KV_WIDTH] += sk
            db_ref[:, KV_WIDTH:] += sv

    tab = pl.BlockSpec((tr, KV_WIDTH), lambda i: (i, 0))
    wide = pl.BlockSpec((tr, 2 * KV_WIDTH), lambda i: (i, 0))
    outs, _ = _call(body, grid=(s // tr,), in_specs=[wide, wide, tab, tab],
                    out_specs=[wide, pl.BlockSpec((1, 2 * KV_WIDTH), lambda i: (0, 0))],
                    out_shape=[jax.ShapeDtypeStruct((s, 2 * KV_WIDTH), BF16),
                               jax.ShapeDtypeStruct((1, 2 * KV_WIDTH), F32)],
                    args=(dk2, dv2, cos, sin), name="kv_rope_bwd")
    return outs


def _from_previous():
    cols = Q_PER_KV * CHUNK
    k = lax.broadcasted_iota(jnp.int32, (CHUNK, cols), 0)
    q = lax.broadcasted_iota(jnp.int32, (CHUNK, cols), 1) & (CHUNK - 1)
    return k > q


def _fold(x2, prev):
    return jnp.where(prev, x2[:CHUNK], x2[CHUNK:])


def _unfold(x, prev):
    zero = jnp.zeros_like(x)
    return jnp.concatenate([jnp.where(prev, x, zero), jnp.where(prev, zero, x)], axis=0)


def _stack_heads(blocks, left):
    parts = []
    for b in blocks:
        parts.append(jnp.where(left, b, jnp.zeros_like(b)))
        parts.append(jnp.where(left, jnp.zeros_like(b), b))
    return jnp.concatenate(parts, axis=0)


def _unstack_heads(xt):
    top = lax.broadcasted_iota(jnp.int32, (128, CHUNK), 0) < HEAD_DIM
    return [jnp.where(top, xt[:, (2 * b) * CHUNK:(2 * b + 1) * CHUNK], xt[:, (2 * b + 1) * CHUNK:(2 * b + 2) * CHUNK]).T
            for b in range(BLOCKS_PER_KV)]


def _sink_row(sk_ref, kvh):
    return jnp.concatenate([jnp.full((1, CHUNK), sk_ref[0, kvh * Q_PER_KV + r], F32) for r in range(Q_PER_KV)], axis=1)


def _stacked_probs(qs, kd, prev, sink, i):
    sc2 = lax.dot_general(kd, qs, NT, preferred_element_type=F32)
    no_previous = jnp.where(i > 0, 0.0, NEG_BIG)
    sc = jnp.where(prev, sc2[:CHUNK] + no_previous, sc2[CHUNK:])
    sink = sink * (1.0 / SCALE)
    m = jnp.maximum(jnp.max(sc, axis=0, keepdims=True), sink)
    p = jnp.exp2((sc - m) * (SCALE * LOG2_E))
    esink = jnp.exp2((sink - m) * (SCALE * LOG2_E))
    inv = 1.0 / (jnp.sum(p, axis=0, keepdims=True) + esink)
    return p * inv, esink * inv


def _lane_block(b):
    return slice(b * 128, (b + 1) * 128)


def _rope_blocks(zq_ref, bq_ref, cos, sin, kvh, rows):
    out = []
    for b in range(BLOCKS_PER_KV):
        cols = _lane_block(kvh * BLOCKS_PER_KV + b)
        q = zq_ref[rows, cols].astype(F32) + bq_ref[:, cols]
        out.append((q * cos + _swap_halves(q) * sin).astype(BF16))
    return out


CHUNKS_PER_STEP = 4


def _attn_specs():
    rows = CHUNKS_PER_STEP * CHUNK
    qspec = pl.BlockSpec((rows, B_WIDTH), lambda i: (i, 0))
    gspec = pl.BlockSpec((rows, B_WIDTH), lambda i: (i, 1))
    prev = pl.BlockSpec((CHUNK, 2 * KV_WIDTH), lambda i: (jnp.maximum(CHUNKS_PER_STEP * i - 1, 0), 0))
    cur = pl.BlockSpec((rows, 2 * KV_WIDTH), lambda i: (i, 0))
    tab = pl.BlockSpec((rows, KV_WIDTH), lambda i: (i, 0))
    bq = pl.BlockSpec((1, B_WIDTH), lambda i: (0, 0))
    sinks = pl.BlockSpec(memory_space=pltpu.SMEM)
    return qspec, gspec, prev, cur, tab, bq, sinks


def _chunk_keys(prev_ref, cur_ref, sub):
    before = prev_ref[...] if sub == 0 else cur_ref[(sub - 1) * CHUNK:sub * CHUNK]
    return jnp.concatenate([before, cur_ref[sub * CHUNK:(sub + 1) * CHUNK]], axis=0)


def _attn_fwd(zb, k2, v2, cos, sin, b_bq, sinks, *, side=None):
    s = zb.shape[0]

    def body(zq_ref, zg_ref, kp_ref, kc_ref, vp_ref, vc_ref, c_ref, s_ref, bq_ref, sk_ref, y_ref):
        prev = _from_previous()
        left = _left_half(CHUNK)
        for sub in range(CHUNKS_PER_STEP):
            chunk = CHUNKS_PER_STEP * pl.program_id(0) + sub
            rows = slice(sub * CHUNK, (sub + 1) * CHUNK)
            cos, sin = c_ref[rows, :], s_ref[rows, :]
            kcat, vcat = _chunk_keys(kp_ref, kc_ref, sub), _chunk_keys(vp_ref, vc_ref, sub)
            for kvh in range(N_KV_HEADS):
                qs = _stack_heads(_rope_blocks(zq_ref, bq_ref, cos, sin, kvh, rows), left)
                p, _ = _stacked_probs(qs, kcat[:, _lane_block(kvh)], prev, _sink_row(sk_ref, kvh), chunk)
                ot = lax.dot_general(vcat[:, _lane_block(kvh)], _unfold(p, prev).astype(BF16), TN,
                                     preferred_element_type=F32)
                for b, ob in enumerate(_unstack_heads(ot)):
                    cols = _lane_block(kvh * BLOCKS_PER_KV + b)
                    gv = zg_ref[rows, cols].astype(F32)
                    y_ref[rows, cols] = (ob * (gv * jax.nn.sigmoid(gv))).astype(BF16)

    qspec, gspec, prev, cur, tab, bq, sk = _attn_specs()
    (y,), side_outs = _call(body, grid=(s // (CHUNKS_PER_STEP * CHUNK),),
                            in_specs=[qspec, gspec, prev, cur, prev, cur, tab, tab, bq, sk],
                            out_specs=[qspec], out_shape=[jax.ShapeDtypeStruct((s, B_WIDTH), BF16)],
                            args=(zb, zb, k2, k2, v2, v2, cos, sin, b_bq, sinks), name="attn_fwd", side=side)
    return y, side_outs


def _attn_bwd(zb, dyb, k2, v2, cos, sin, b_bq, sinks, *, side=None):
    s = zb.shape[0]

    def body(zq_ref, zg_ref, dy_ref, kp_ref, kc_ref, vp_ref, vc_ref, c_ref, s_ref, bq_ref, sk_ref,
             dz_ref, dk_ref, dv_ref, dbq_ref, dsk_ref):
        i = pl.program_id(0)

        @pl.when(i == 0)
        def _():
            dk_ref[...] = jnp.zeros_like(dk_ref)
            dv_ref[...] = jnp.zeros_like(dv_ref)
            dbq_ref[...] = jnp.zeros_like(dbq_ref)
            dsk_ref[...] = jnp.zeros_like(dsk_ref)

        prev = _from_previous()
        left = _left_half(CHUNK)
        lane = lax.broadcasted_iota(jnp.int32, (1, 128), 1)
        dsk_row = jnp.zeros((1, 128), F32)
        for sub in range(CHUNKS_PER_STEP):
            chunk = CHUNKS_PER_STEP * i + sub
            rows = slice(sub * CHUNK, (sub + 1) * CHUNK)
            cos, sin = c_ref[rows, :], s_ref[rows, :]
            kcat, vcat = _chunk_keys(kp_ref, kc_ref, sub), _chunk_keys(vp_ref, vc_ref, sub)
            cur_rows = pl.ds(pl.multiple_of(chunk * CHUNK, CHUNK), CHUNK)
            for kvh in range(N_KV_HEADS):
                kd, vd = kcat[:, _lane_block(kvh)], vcat[:, _lane_block(kvh)]
                qs = _stack_heads(_rope_blocks(zq_ref, bq_ref, cos, sin, kvh, rows), left)
                p, psink = _stacked_probs(qs, kd, prev, _sink_row(sk_ref, kvh), chunk)
                pb = _unfold(p, prev).astype(BF16)
                ot = lax.dot_general(vd, pb, TN, preferred_element_type=F32)
                gates, dys = [], []
                for b in range(BLOCKS_PER_KV):
                    cols = _lane_block(kvh * BLOCKS_PER_KV + b)
                    gates.append(_silu_parts(zg_ref[rows, cols].astype(F32)))
                    dys.append(dy_ref[rows, cols].astype(F32))
                dos = _stack_heads([(dyv * silu).astype(BF16) for dyv, (silu, _) in zip(dys, gates)], left)
                dp = _fold(lax.dot_general(vd, dos, NT, preferred_element_type=F32), prev)
                delta = jnp.sum(p * dp, axis=0, keepdims=True)
                ds = _unfold(p * (dp - delta) * SCALE, prev).astype(BF16)
                dqt = lax.dot_general(kd, ds, TN, preferred_element_type=F32)
                dk_part = jnp.dot(ds, qs, preferred_element_type=F32)
                dv_part = jnp.dot(pb, dos, preferred_element_type=F32)
                dk_ref[cur_rows, _lane_block(kvh)] += dk_part[CHUNK:]
                dv_ref[cur_rows, _lane_block(kvh)] += dv_part[CHUNK:]

                @pl.when(chunk > 0)
                def _(kvh=kvh, chunk=chunk, dk_part=dk_part, dv_part=dv_part):
                    prev_rows = pl.ds(pl.multiple_of((chunk - 1) * CHUNK, CHUNK), CHUNK)
                    dk_ref[prev_rows, _lane_block(kvh)] += dk_part[:CHUNK]
                    dv_ref[prev_rows, _lane_block(kvh)] += dv_part[:CHUNK]

                sink_grad = psink * delta
                for r in range(Q_PER_KV):
                    dsink = -jnp.sum(sink_grad[:, r * CHUNK:(r + 1) * CHUNK], axis=1, keepdims=True)
                    dsk_row = dsk_row + jnp.where(lane == kvh * Q_PER_KV + r, dsink, 0.0)
                blocks = zip(_unstack_heads(ot), _unstack_heads(dqt), dys, gates)
                for b, (ob, dqr, dyv, (_, dsilu)) in enumerate(blocks):
                    blk = kvh * BLOCKS_PER_KV + b
                    dq = dqr * cos + _swap_halves(dqr * sin)
                    dbq_ref[:, _lane_block(blk)] += jnp.sum(dq, axis=0, keepdims=True)
                    dz_ref[rows, _lane_block(blk)] = dq.astype(BF16)
                    dz_ref[rows, _lane_block(B_WIDTH // 128 + blk)] = (dyv * ob * dsilu).astype(BF16)
        dsk_ref[0:1, :] += dsk_row

    qspec, gspec, prev, cur, tab, bq, sk = _attn_specs()
    full = pl.BlockSpec((s, 2 * KV_WIDTH), lambda i: (0, 0))
    return _call(
        body, grid=(s // (CHUNKS_PER_STEP * CHUNK),),
        in_specs=[qspec, gspec, qspec, prev, cur, prev, cur, tab, tab, bq, sk],
        out_specs=[pl.BlockSpec((CHUNKS_PER_STEP * CHUNK, 2 * B_WIDTH), lambda i: (i, 0)), full, full, bq,
                   pl.BlockSpec((8, 128), lambda i: (0, 0))],
        out_shape=[jax.ShapeDtypeStruct((s, 2 * B_WIDTH), BF16), jax.ShapeDtypeStruct((s, 2 * KV_WIDTH), F32),
                   jax.ShapeDtypeStruct((s, 2 * KV_WIDTH), F32), jax.ShapeDtypeStruct((1, B_WIDTH), F32),
                   jax.ShapeDtypeStruct((8, 128), F32)],
        args=(zb, zb, dyb, k2, k2, v2, v2, cos, sin, b_bq, sinks), name="attn_bwd", side=side)


def _place():
    x, y, c = lax.axis_index("x"), lax.axis_index("y"), lax.axis_index("c")
    return x, y, c, [(1 - x, y), (x, 1 - y), (1 - x, 1 - y)]


def _relations():
    return [(r >> 2 & 1, r >> 1 & 1, r & 1) for r in range(1, 8)]


def _gather_side(arrs):
    n = len(arrs)

    def copies(ins, outs, sems):
        send_ici, recv_ici, send_d2d, recv_d2d, local_sem = sems
        x, y, c, chips = _place()
        me = 2 * x + y

        def rows(a, half):
            hr = arrs[a].shape[0] // 2
            return pl.ds(half * hr, hr)

        def ici(a, j, src_chip, to):
            return pltpu.make_async_remote_copy(
                src_ref=ins[a].at[rows(a, c)], dst_ref=outs[a].at[src_chip, rows(a, c)],
                send_sem=send_ici.at[a, j], recv_sem=recv_ici.at[a, j], device_id=to, device_id_type=MESH)

        def d2d(a, j, chip, half):
            blk = outs[a].at[chip, rows(a, half)]
            return pltpu.make_async_remote_copy(
                src_ref=blk, dst_ref=blk, send_sem=send_d2d.at[a, j], recv_sem=recv_d2d.at[a, j],
                device_id=(x, y, 1 - c), device_id_type=MESH)

        local = [pltpu.make_async_copy(ins[a], outs[a].at[me], local_sem.at[a]) for a in range(n)]
        pairs = [(a, j, chip) for a in range(n) for j, chip in enumerate(chips)]
        return c, me, local, ici, d2d, pairs

    def start(ins, outs, sems):
        c, me, local, ici, _, pairs = copies(ins, outs, sems)
        for cp in local:
            cp.start()
        for a, j, chip in pairs:
            ici(a, j, me, (*chip, c)).start()

    def passing(ins, outs, sems):
        c, _, _, ici, d2d, pairs = copies(ins, outs, sems)
        for a, j, (px, py) in pairs:
            ici(a, j, 2 * px + py, (px, py, c)).wait_recv()
            d2d(a, j, 2 * px + py, c).start()

    def finish(ins, outs, sems):
        c, me, local, ici, d2d, pairs = copies(ins, outs, sems)
        for a, j, (px, py) in pairs:
            d2d(a, j, 2 * px + py, 1 - c).wait_recv()
        for a, j, (px, py) in pairs:
            ici(a, j, me, (px, py, c)).wait_send()
            d2d(a, j, 2 * px + py, c).wait_send()
        for cp in local:
            cp.wait()

    return _Side(arrs, [jax.ShapeDtypeStruct((N_CHIPS,) + a.shape, a.dtype) for a in arrs],
                 [pltpu.SemaphoreType.DMA((n, 3))] * 4 + [pltpu.SemaphoreType.DMA((n,))], start, finish,
                 passing=passing)


def _exchange_side(grads):
    n = len(grads)

    def copies(ins, outs, sems):
        send_sem, recv_sem = sems
        x, y, c, _ = _place()
        cps = []
        for a in range(n):
            hr = grads[a].shape[1] // 2
            cps.append(pltpu.make_async_remote_copy(
                src_ref=ins[a].at[:, pl.ds((1 - c) * hr, hr), :], dst_ref=outs[a],
                send_sem=send_sem.at[a], recv_sem=recv_sem.at[a], device_id=(x, y, 1 - c), device_id_type=MESH))
        return cps

    def start(ins, outs, sems):
        for cp in copies(ins, outs, sems):
            cp.start()

    def finish(ins, outs, sems):
        for cp in copies(ins, outs, sems):
            cp.wait()

    return _Side(grads, [jax.ShapeDtypeStruct((g.shape[0], g.shape[1] // 2, g.shape[2]), g.dtype) for g in grads],
                 [pltpu.SemaphoreType.DMA((n,))] * 2, start, finish)


def _scatter_side(chip_sums, small=None):
    n = len(chip_sums)
    arrs = list(chip_sums) + ([small] if small is not None else [])

    def copies(ins, outs, sems):
        x, y, c, chips = _place()
        cps = []
        for a in range(n):
            for j, (px, py) in enumerate(chips):
                cps.append(pltpu.make_async_remote_copy(
                    src_ref=ins[a].at[2 * px + py], dst_ref=outs[a].at[j],
                    send_sem=sems[0].at[a, j], recv_sem=sems[1].at[a, j], device_id=(px, py, c), device_id_type=MESH))
        if small is not None:
            for r, (fx, fy, fc) in enumerate(_relations(), start=1):
                px, py, pc = x ^ fx, y ^ fy, c ^ fc
                cps.append(pltpu.make_async_remote_copy(
                    src_ref=ins[n].at[4 * px + 2 * py + pc], dst_ref=outs[n].at[r],
                    send_sem=sems[2].at[r - 1], recv_sem=sems[3].at[r - 1], device_id=(px, py, pc),
                    device_id_type=MESH))
        return cps

    def start(ins, outs, sems):
        for cp in copies(ins, outs, sems):
            cp.start()

    def finish(ins, outs, sems):
        for cp in copies(ins, outs, sems):
            cp.wait()

    shapes = [jax.ShapeDtypeStruct((3,) + t.shape[1:], t.dtype) for t in chip_sums]
    sems = [pltpu.SemaphoreType.DMA((n, 3))] * 2
    if small is not None:
        shapes.append(jax.ShapeDtypeStruct(small.shape, small.dtype))
        sems += [pltpu.SemaphoreType.DMA((7,))] * 2
    return _Side(arrs, shapes, sems, start, finish)


def _small_scatter_side(small):
    def copies(ins, outs, sems):
        x, y, c, _ = _place()
        cps = []
        for r, (fx, fy, fc) in enumerate(_relations(), start=1):
            px, py, pc = x ^ fx, y ^ fy, c ^ fc
            cps.append(pltpu.make_async_remote_copy(
                src_ref=ins[0].at[4 * px + 2 * py + pc], dst_ref=outs[0].at[r],
                send_sem=sems[0].at[r - 1], recv_sem=sems[1].at[r - 1], device_id=(px, py, pc), device_id_type=MESH))
        return cps

    def start(ins, outs, sems):
        for cp in copies(ins, outs, sems):
            cp.start()

    def finish(ins, outs, sems):
        for cp in copies(ins, outs, sems):
            cp.wait()

    return _Side([small], [jax.ShapeDtypeStruct(small.shape, small.dtype)], [pltpu.SemaphoreType.DMA((7,))] * 2,
                 start, finish)


def _small_share_side(small):
    return _share_side([], small)


def _share_side(halves, small=None):
    n = len(halves)
    arrs = list(halves) + ([small] if small is not None else [])

    def copies(ins, outs, sems, mine):
        x, y, c, _ = _place()
        me = 4 * x + 2 * y + c
        cps = []
        for a in range(n):
            hr = halves[a].shape[0] // 2
            rows = pl.ds((c if mine else 1 - c) * hr, hr)
            cps.append(pltpu.make_async_remote_copy(
                src_ref=ins[a].at[rows], dst_ref=outs[a].at[rows],
                send_sem=sems[0].at[a], recv_sem=sems[1].at[a], device_id=(x, y, 1 - c), device_id_type=MESH))
        if small is not None:
            for r, (fx, fy, fc) in enumerate(_relations(), start=1):
                px, py, pc = x ^ fx, y ^ fy, c ^ fc
                seg = me if mine else 4 * px + 2 * py + pc
                cps.append(pltpu.make_async_remote_copy(
                    src_ref=ins[n].at[seg], dst_ref=outs[n].at[seg],
                    send_sem=sems[-2].at[r - 1], recv_sem=sems[-1].at[r - 1], device_id=(px, py, pc),
                    device_id_type=MESH))
        return cps

    def start(ins, outs, sems):
        for cp in copies(ins, outs, sems, True):
            cp.start()

    def finish(ins, outs, sems):
        for cp in copies(ins, outs, sems, False):
            cp.wait_recv()
        for cp in copies(ins, outs, sems, True):
            cp.wait_send()

    sems = ([pltpu.SemaphoreType.DMA((n,))] * 2 if n else []) + (
        [pltpu.SemaphoreType.DMA((7,))] * 2 if small is not None else [])
    return _Side(arrs, [jax.ShapeDtypeStruct(h.shape, h.dtype) for h in arrs], sems, start, finish,
                 aliases={i: i for i in range(len(arrs))})


GATHER_PIECES = [(0, 0), (0, 1), (1, 0), (2, 0), (1, 1), (2, 1), (3, 0), (3, 1)]


def _mm_gathering(a, shard, order, *, name, tm=1024):
    s, k = a.shape
    nc = shard.shape[1]
    tm = _row_tile(s, tm)
    tn = nc // 2
    hr = k // 2
    qr = hr // 2
    blocks = jnp.stack([order[src] * 2 + h for src, h in GATHER_PIECES]).astype(jnp.int32)

    def body(blocks_ref, a_ref, shard_ref, z_ref, full_ref, wbuf, send_ici, recv_ici, send_relay,
             recv_relay, send_d2d, recv_d2d, local_sem, load_sem):
        piece, i = pl.program_id(0), pl.program_id(1)
        x, y, c, chips = _place()
        me = 2 * x + y
        nbrs = chips[:2]
        chip_of = [2 * px + py for px, py in chips]

        def quarter(q):
            return pl.ds(c * hr + q * qr, qr)

        def sibling_quarter(q):
            return pl.ds((1 - c) * hr + q * qr, qr)

        def whole(half):
            return pl.ds(half * hr, hr)

        def cols(h):
            return pl.ds(h * tn, tn)

        def direct(j, src_chip, h):
            return pltpu.make_async_remote_copy(
                src_ref=shard_ref.at[whole(c), cols(h)], dst_ref=full_ref.at[src_chip, whole(c), cols(h)],
                send_sem=send_ici.at[j, h], recv_sem=recv_ici.at[j, h], device_id=(*nbrs[j], c), device_id_type=MESH)

        def relay(j, src_chip, h):
            blk = full_ref.at[src_chip, quarter(j), cols(h)]
            return pltpu.make_async_remote_copy(
                src_ref=blk, dst_ref=blk, send_sem=send_relay.at[j, h], recv_sem=recv_relay.at[j, h],
                device_id=(*nbrs[1 - j], c), device_id_type=MESH)

        def d2d(j, chip, rows, h):
            blk = full_ref.at[chip, rows, cols(h)]
            return pltpu.make_async_remote_copy(
                src_ref=blk, dst_ref=blk, send_sem=send_d2d.at[j, h], recv_sem=recv_d2d.at[j, h],
                device_id=(x, y, 1 - c), device_id_type=MESH)

        def load(p):
            src, h = GATHER_PIECES[p]
            where = shard_ref if src == 0 else full_ref.at[chip_of[src - 1]]
            return pltpu.make_async_copy(where.at[:, cols(h)], wbuf.at[p % 2], load_sem.at[p % 2])

        local = pltpu.make_async_copy(shard_ref, full_ref.at[me], local_sem)

        def arrived(p):
            src, h = GATHER_PIECES[p]
            if src in (1, 2):
                j = src - 1
                direct(j, chip_of[j], h).wait_recv()
                relay(j, chip_of[j], h).start()
                d2d(j, chip_of[j], whole(c), h).start()
            elif src == 3:
                for j in range(2):
                    relay(1 - j, chip_of[2], h).wait_recv()
                    d2d(2 + j, chip_of[2], quarter(1 - j), h).start()

        def fetch(p):
            src, h = GATHER_PIECES[p]
            if src in (1, 2):
                d2d(src - 1, chip_of[src - 1], whole(1 - c), h).wait_recv()
            elif src == 3:
                for j in range(2):
                    d2d(2 + j, chip_of[2], sibling_quarter(1 - j), h).wait_recv()
            load(p).start()

        n_i = s // tm
        for p in range(len(GATHER_PIECES)):
            @pl.when(jnp.logical_and(piece == p, i == 0))
            def _(p=p):
                if p == 0:
                    local.start()
                    for hh in range(2):
                        for j in range(2):
                            direct(j, me, hh).start()
                    load(0).start()
                load(p).wait()

        z_ref[...] = jnp.dot(a_ref[...], wbuf[piece % 2], preferred_element_type=F32).astype(z_ref.dtype)

        for p in range(len(GATHER_PIECES) - 1):
            @pl.when(jnp.logical_and(piece == p, i == min(1, n_i - 1)))
            def _(p=p):
                arrived(p + 1)

            @pl.when(jnp.logical_and(piece == p, i == min(2, n_i - 1)))
            def _(p=p):
                fetch(p + 1)

        last = jnp.logical_and(piece == len(GATHER_PIECES) - 1, i == n_i - 1)

        @pl.when(last)
        def _():
            for h in range(2):
                for j in range(2):
                    direct(j, me, h).wait_send()
                    relay(j, chip_of[j], h).wait_send()
                    d2d(j, chip_of[j], whole(c), h).wait_send()
                    d2d(2 + j, chip_of[2], quarter(1 - j), h).wait_send()
            local.wait()

    return pl.pallas_call(
        body,
        grid_spec=pltpu.PrefetchScalarGridSpec(
            num_scalar_prefetch=1, grid=(len(GATHER_PIECES), s // tm),
            in_specs=[pl.BlockSpec((tm, k), lambda p, i, blocks: (i, 0)), HBM],
            out_specs=[pl.BlockSpec((tm, tn), lambda p, i, blocks: (i, blocks[p])), HBM],
            scratch_shapes=[pltpu.VMEM((2, k, tn), BF16)] + [pltpu.SemaphoreType.DMA((2, 2))] * 4
            + [pltpu.SemaphoreType.DMA((4, 2))] * 2 + [pltpu.SemaphoreType.DMA, pltpu.SemaphoreType.DMA((2,))]),
        out_shape=[jax.ShapeDtypeStruct((s, N_CHIPS * nc), BF16), jax.ShapeDtypeStruct((N_CHIPS, k, nc), BF16)],
        name=name, compiler_params=_cparams(),
    )(blocks, a, shard)


def _mm_tn_exchanging(a, b, *, name, shards, tk=2048, side=None):
    s, m = a.shape
    nc = b.shape[1] // shards
    tk = _row_tile(s, tk)
    nk = s // tk
    hm = m // 2

    def body(a_ref, b_ref, part_ref, sib_ref, acc, keep_sem, send_sem, recv_sem):
        j, kk = pl.program_id(0), pl.program_id(1)
        x, y, c, _ = _place()

        def keep(jj, slot):
            mine = pl.ds(c * hm, hm)
            return pltpu.make_async_copy(acc.at[slot, mine], part_ref.at[jj], keep_sem.at[slot])

        def give(jj, slot):
            return pltpu.make_async_remote_copy(
                src_ref=acc.at[slot, pl.ds((1 - c) * hm, hm)], dst_ref=sib_ref.at[jj],
                send_sem=send_sem.at[slot], recv_sem=recv_sem.at[jj], device_id=(x, y, 1 - c), device_id_type=MESH)

        part = lax.dot_general(a_ref[...], b_ref[...], TN, preferred_element_type=F32)
        for slot in range(2):
            @pl.when(j % 2 == slot)
            def _(slot=slot):
                @pl.when(jnp.logical_and(kk == 0, j >= 2))
                def _():
                    keep(j - 2, slot).wait()
                    give(j - 2, slot).wait_send()

                @pl.when(kk == 0)
                def _():
                    acc[slot] = part

                @pl.when(kk > 0)
                def _():
                    acc[slot] += part

                @pl.when(kk == nk - 1)
                def _():
                    keep(j, slot).start()
                    give(j, slot).start()

        @pl.when(jnp.logical_and(j == shards - 1, kk == nk - 1))
        def _():
            for jj in range(shards - 2, shards):
                keep(jj, jj % 2).wait()
                give(jj, jj % 2).wait_send()
            for jj in range(shards):
                give(jj, jj % 2).wait_recv()

    assert shards >= 2
    return _call(
        body, grid=(shards, nk),
        in_specs=[pl.BlockSpec((tk, m), lambda j, kk: (kk, 0)), pl.BlockSpec((tk, nc), lambda j, kk: (kk, j))],
        out_specs=[HBM, HBM],
        out_shape=[jax.ShapeDtypeStruct((shards, hm, nc), F32), jax.ShapeDtypeStruct((shards, hm, nc), F32)],
        scratch=[pltpu.VMEM((2, m, nc), F32), pltpu.SemaphoreType.DMA((2,)), pltpu.SemaphoreType.DMA((2,)),
                 pltpu.SemaphoreType.DMA((shards,))],
        args=(a, b), name=name, side=side)


def _col_tile(cols):
    return cols if cols <= 2048 else 512


def _add_sibling(grad, recv, core, *, name):
    k, r, c = grad.shape
    hr = r // 2
    tr = min(hr, 256)
    tc = _col_tile(c)
    nrb = hr // tr

    def body(core_ref, g_ref, r_ref, o_ref):
        o_ref[...] = (g_ref[...] + r_ref[...]).astype(BF16)

    return pl.pallas_call(
        body,
        grid_spec=pltpu.PrefetchScalarGridSpec(
            num_scalar_prefetch=1, grid=(k, nrb, c // tc),
            in_specs=[pl.BlockSpec((None, tr, tc), lambda kk, i, j, core: (kk, core[0] * nrb + i, j)),
                      pl.BlockSpec((None, tr, tc), lambda kk, i, j, core: (kk, i, j))],
            out_specs=pl.BlockSpec((None, tr, tc), lambda kk, i, j, core: (kk, i, j))),
        out_shape=jax.ShapeDtypeStruct((k, hr, c), BF16), name=name, compiler_params=_cparams(),
    )(core, grad, recv)


def _sum_chips(grad, from_sibling, recv, place, *, name):
    _, hr, c = from_sibling.shape
    tr = min(hr, 256)
    tc = _col_tile(c)
    nrb = hr // tr

    def body(place_ref, g_ref, s_ref, r0_ref, r1_ref, r2_ref, o_ref):
        own = g_ref[...] + s_ref[...]
        o_ref[...] = ((own + r0_ref[...].astype(F32)) + r1_ref[...].astype(F32)) + r2_ref[...].astype(F32)

    def rspec(j):
        return pl.BlockSpec((None, tr, tc), lambda i, jj, place: (j, i, jj))

    return pl.pallas_call(
        body,
        grid_spec=pltpu.PrefetchScalarGridSpec(
            num_scalar_prefetch=1, grid=(nrb, c // tc),
            in_specs=[pl.BlockSpec((None, tr, tc), lambda i, jj, place: (place[0], place[1] * nrb + i, jj)),
                      pl.BlockSpec((None, tr, tc), lambda i, jj, place: (place[0], i, jj)),
                      rspec(0), rspec(1), rspec(2)],
            out_specs=pl.BlockSpec((tr, tc), lambda i, jj, place: (place[1] * nrb + i, jj))),
        out_shape=jax.ShapeDtypeStruct((2 * hr, c), F32), name=name, compiler_params=_cparams(),
    )(place, grad, from_sibling, recv, recv, recv)


def _add_halves(mine, theirs, *, name, side=None):
    k, hr, c = mine.shape
    tr = min(hr, 256)
    tc = _col_tile(c)

    def body(a_ref, b_ref, o_ref):
        o_ref[...] = (a_ref[...] + b_ref[...]).astype(BF16)

    spec = pl.BlockSpec((None, tr, tc), lambda kk, i, j: (kk, i, j))
    (out,), side_outs = _call(body, grid=(k, hr // tr, c // tc), in_specs=[spec, spec], out_specs=[spec],
                              out_shape=[jax.ShapeDtypeStruct((k, hr, c), BF16)], args=(mine, theirs), name=name,
                              side=side)
    return out, side_outs


def _sum_halves(mine, theirs, recv, place, *, name):
    _, hr, c = mine.shape
    tr = min(hr, 256)
    tc = _col_tile(c)
    nrb = hr // tr

    def body(place_ref, a_ref, b_ref, r0_ref, r1_ref, r2_ref, o_ref):
        own = a_ref[...] + b_ref[...]
        o_ref[...] = ((own + r0_ref[...].astype(F32)) + r1_ref[...].astype(F32)) + r2_ref[...].astype(F32)

    def rspec(j):
        return pl.BlockSpec((None, tr, tc), lambda i, jj, place: (j, i, jj))

    own_spec = pl.BlockSpec((None, tr, tc), lambda i, jj, place: (place[0], i, jj))
    return pl.pallas_call(
        body,
        grid_spec=pltpu.PrefetchScalarGridSpec(
            num_scalar_prefetch=1, grid=(nrb, c // tc),
            in_specs=[own_spec, own_spec, rspec(0), rspec(1), rspec(2)],
            out_specs=pl.BlockSpec((tr, tc), lambda i, jj, place: (place[1] * nrb + i, jj))),
        out_shape=jax.ShapeDtypeStruct((2 * hr, c), F32), name=name, compiler_params=_cparams(),
    )(place, mine, theirs, recv, recv, recv)


def _sum_small(small, recv, place):
    _, sr, _ = small.shape

    def body(place_ref, own_ref, r_ref, o_ref):
        acc = own_ref[...]
        for r in range(1, 8):
            acc = acc + r_ref[r]
        o_ref[...] = acc

    return pl.pallas_call(
        body,
        grid_spec=pltpu.PrefetchScalarGridSpec(
            num_scalar_prefetch=1, grid=(1,),
            in_specs=[pl.BlockSpec((None, sr, 128), lambda i, place: (place[2], 0, 0)),
                      pl.BlockSpec((8, sr, 128), lambda i, place: (0, 0, 0))],
            out_specs=pl.BlockSpec((None, sr, 128), lambda i, place: (place[2], 0, 0))),
        out_shape=jax.ShapeDtypeStruct(small.shape, F32), name="sum_small", compiler_params=_cparams(),
    )(place, small, recv)


def _spread_side(vec):
    def copies(ins, outs, sems):
        x, y, c, _ = _place()
        return [pltpu.make_async_remote_copy(
            src_ref=ins[0], dst_ref=outs[0].at[r], send_sem=sems[0].at[r - 1], recv_sem=sems[1].at[r - 1],
            device_id=(x ^ fx, y ^ fy, c ^ fc), device_id_type=MESH)
            for r, (fx, fy, fc) in enumerate(_relations(), start=1)]

    def start(ins, outs, sems):
        for cp in copies(ins, outs, sems):
            cp.start()

    def finish(ins, outs, sems):
        for cp in copies(ins, outs, sems):
            cp.wait()

    return _Side([vec], [jax.ShapeDtypeStruct((8,) + vec.shape, vec.dtype)], [pltpu.SemaphoreType.DMA((7,))] * 2,
                 start, finish)


def _sum_in_device_order(own, spread, place):
    def body(place_ref, own_ref, r_ref, o_ref):
        me = place_ref[2]
        acc = jnp.zeros_like(own_ref[...])
        for d in range(8):
            slot = jnp.where(me == d, 1, me ^ d)
            acc = acc + jnp.where(me == d, own_ref[...], r_ref[slot])
        o_ref[...] = acc

    return pl.pallas_call(
        body,
        grid_spec=pltpu.PrefetchScalarGridSpec(
            num_scalar_prefetch=1, grid=(1,),
            in_specs=[pl.BlockSpec(own.shape, lambda i, place: (0, 0)),
                      pl.BlockSpec(spread.shape, lambda i, place: (0, 0, 0))],
            out_specs=pl.BlockSpec(own.shape, lambda i, place: (0, 0))),
        out_shape=jax.ShapeDtypeStruct(own.shape, F32), name="sum_in_device_order", compiler_params=_cparams(),
    )(place, own, spread)


def _adamw(w, g, m, v, *, name):
    r, c = w.shape
    tr = 256 if r % 256 == 0 else r
    tc = _col_tile(c)
    bc1 = 1.0 - ADAM_B1 ** ADAM_STEP
    bc2 = 1.0 - ADAM_B2 ** ADAM_STEP

    def body(w_ref, g_ref, m_ref, v_ref, d_ref, nm_ref, nv_ref, gout_ref):
        gv = g_ref[...]
        nm = ADAM_B1 * m_ref[...] + (1.0 - ADAM_B1) * gv
        nv = ADAM_B2 * v_ref[...] + (1.0 - ADAM_B2) * (gv * gv)
        d_ref[...] = -ADAM_LR * ((nm / bc1) / (jnp.sqrt(nv / bc2) + ADAM_EPS) + ADAM_WD * w_ref[...])
        nm_ref[...] = nm
        nv_ref[...] = nv
        gout_ref[...] = gv

    spec = pl.BlockSpec((tr, tc), lambda i, j: (i, j))
    outs, _ = _call(body, grid=(r // tr, c // tc), in_specs=[spec] * 4, out_specs=[spec] * 4,
                    out_shape=[jax.ShapeDtypeStruct((r, c), F32)] * 4, args=(w, g, m, v), name=name)
    return outs


SMALL_ORDER = ["a_ws", "a_bs", "a_norm_g", "a_ln_g", "a_ln_b", "kv_norm_g", "b_kv", "b_norm_g", "b_bq",
               "b_sinks", "final_norm_g"]
SHARDED_SMALL = {"a_norm_g", "a_ln_g", "a_ln_b"}
PACK_TILE = 8 * 128


def _rows128(a):
    flat = a.reshape(-1)
    return jnp.pad(flat, (0, (-flat.shape[0]) % PACK_TILE)).reshape(-1, 128)


def _pack_rows(parts, multiple):
    rows = [_rows128(p) for p in parts]
    total = sum(r.shape[0] for r in rows)
    pad = (-total) % multiple
    if pad:
        rows.append(jnp.zeros((pad, 128), rows[0].dtype))
    return jnp.concatenate(rows, axis=0)


def _unpack_rows(packed, shapes):
    out, row = [], 0
    for shp in shapes:
        size = math.prod(shp)
        nrow = -(-size // PACK_TILE) * 8
        out.append(packed[row:row + nrow].reshape(-1)[:size].reshape(shp))
        row += nrow
    return out


WEIGHTS = ["a_norm_g", "a_w_in", "a_ln_g", "a_ln_b", "a_ws", "a_bs", "a_w_out", "kv_norm_g", "w_kv", "b_kv",
           "b_norm_g", "b_w_in", "b_bq", "b_sinks", "b_w_out", "final_norm_g"]
BIG = ["a_w_in", "a_w_out", "w_kv", "b_w_in", "b_w_out"]


class _Reduction:
    def __init__(self, names, partials, core, place, small=None):
        self.names, self.partials, self.core, self.place, self.small = names, partials, core, place, small

    def exchange_side(self):
        return _exchange_side(self.partials)

    def took_exchange(self, from_sibling):
        self.from_sibling = from_sibling
        self.chip_sums = [_add_sibling(g, r, self.core, name="add_sibling_" + n)
                          for g, r, n in zip(self.partials, from_sibling, self.names)]

    def scatter_side(self):
        return _scatter_side(self.chip_sums, self.small)

    def took_scatter(self, arrived):
        big = arrived[:len(self.names)]
        self.halves = [_sum_chips(g, fs, r, self.place, name="sum_chips_" + n)
                       for g, fs, r, n in zip(self.partials, self.from_sibling, big, self.names)]
        self.small_mine = _sum_small(self.small, arrived[-1], self.place) if self.small is not None else None

    def share_side(self):
        return _share_side(self.halves, self.small_mine)

    def took_share(self, shared):
        self.grads = dict(zip(self.names, shared[:len(self.names)]))
        self.small_full = shared[-1] if self.small is not None else None


def _step(x, loss_target, p, m, v):
    xi, yi, ci = lax.axis_index("x"), lax.axis_index("y"), lax.axis_index("c")
    chip = 2 * xi + yi
    device = 4 * xi + 2 * yi + ci
    core = jnp.reshape(ci, (1,)).astype(jnp.int32)
    place = jnp.stack([chip, ci, device]).astype(jnp.int32)
    x, tgt = x[0], loss_target[0]
    s = x.shape[0]
    cos, sin = _rope_tables(s)

    shard2d = {n: p[n].reshape(p[n].shape[-2:]) for n in BIG}
    shard_bf = {n: shard2d[n].astype(BF16) for n in BIG}
    ws = p["a_ws"][0]
    ws_t = jnp.swapaxes(ws, 1, 2)
    bs_t = p["a_bs"][0].T
    kv_norm_g, b_kv = p["kv_norm_g"].reshape(1, -1), p["b_kv"].reshape(1, -1)
    final_norm_g = p["final_norm_g"].reshape(1, -1)

    vec_shapes = [p[n].shape for n in ("a_norm_g", "a_ln_g", "a_ln_b")]
    vec_pack = _pack_rows([p["a_norm_g"], p["a_ln_g"], p["a_ln_b"]], 16)
    (vec_all,) = _comm_call(_gather_side([vec_pack]), "gather_vectors")
    vecs = [_unpack_rows(vec_all[k], vec_shapes) for k in range(N_CHIPS)]
    a_norm_g, a_ln_g, a_ln_b = (jnp.concatenate([vk[t] for vk in vecs], axis=-1) for t in range(3))

    (n_a,) = _rms_fwd(x, [a_norm_g], name="rms_a")
    order = jnp.stack([chip, 2 * (1 - xi) + yi, 2 * xi + (1 - yi), 2 * (1 - xi) + (1 - yi)]).astype(jnp.int32)
    z, a_w_in = _mm_gathering(n_a, shard_bf["a_w_in"], order, name="mm_a_in")
    y, (a_w_out,) = _gate_fwd(z, a_ln_g, a_ln_b, ws, bs_t, side=_gather_side([shard_bf["a_w_out"]]))
    a_w_out = a_w_out.reshape(A_WIDTH, D_MODEL)
    (h1, n_kv, n_b), (w_kv, b_w_in) = _mm_residual_norms(
        y, a_w_out, x, [kv_norm_g, p["b_norm_g"]], name="mm_a_out",
        side=_gather_side([shard_bf["w_kv"], shard_bf["b_w_in"]]))
    w_kv = w_kv.reshape(D_MODEL, 2 * KV_WIDTH)
    kr, vv = _kv_rope(n_kv, w_kv, b_kv, cos, sin)
    zb = _mm_nn(n_b, b_w_in, name="mm_b_in", tn=512, tm=2048, out_dtype=BF16)
    yb, (b_w_out,) = _attn_fwd(zb, kr, vv, cos, sin, p["b_bq"], p["b_sinks"], side=_gather_side([shard_bf["b_w_out"]]))
    b_w_out = b_w_out.reshape(B_WIDTH, D_MODEL)
    loss_blk, dh2, dh2b, d_final_g = _mm_residual_loss(yb, b_w_out, h1, tgt, final_norm_g, name="mm_b_out")

    d_b_w_out = _mm_tn(yb, dh2b, name="mm_d_b_w_out", tm=B_WIDTH, tn=D_MODEL)
    red_bo = _Reduction(["b_w_out"], [d_b_w_out.reshape(N_CHIPS, B_WIDTH // N_CHIPS, D_MODEL)], core, place)
    dyb, got = _mm_nt(dh2b, b_w_out, name="mm_dyb", tm=1024, out_dtype=BF16, side=red_bo.exchange_side())
    red_bo.took_exchange(got)
    (dzb, dk_rot, dv, d_bq, d_sinks), got = _attn_bwd(zb, dyb, kr, vv, cos, sin, p["b_bq"], p["b_sinks"],
                                                      side=red_bo.scatter_side())
    red_bo.took_scatter(got)
    dkv, d_b_kv = _kv_rope_bwd(dk_rot, dv, cos, sin)
    d_b_w_in = _mm_tn(n_b, dzb, name="mm_d_b_w_in", tm=D_MODEL, tn=512, tk=4096, shards=N_CHIPS)
    d_w_kv, got = _mm_tn(n_kv, dkv, name="mm_d_w_kv", tm=D_MODEL, tn=2 * KV_WIDTH, tk=4096,
                         side=red_bo.share_side())
    red_bo.took_share(got)
    red_bi = _Reduction(["b_w_in", "w_kv"], [d_b_w_in, d_w_kv.reshape(N_CHIPS, D_MODEL // N_CHIPS, 2 * KV_WIDTH)],
                        core, place)
    (dh1, dh1b, d_kv_g, d_b_g), got = _mm_nt_rms_bwd(
        [(dkv, w_kv, kv_norm_g), (dzb, b_w_in, p["b_norm_g"])], h1, dh2, name="mm_dn_b", tm=512,
        side=red_bi.exchange_side())
    red_bi.took_exchange(got)

    d_a_w_out = _mm_tn(y, dh1b, name="mm_d_a_w_out", tm=1024, tn=D_MODEL)
    red_ao = _Reduction(["a_w_out"], [d_a_w_out.reshape(N_CHIPS, A_WIDTH // N_CHIPS, D_MODEL)], core, place)
    dy, got = _mm_nt(dh1b, a_w_out, name="mm_dy", tn=1024, tm=1024, out_dtype=BF16, side=red_ao.exchange_side())
    red_ao.took_exchange(got)
    sides = [red_bi.scatter_side(), red_ao.scatter_side()]
    (dz, d_ln_g, d_ln_b, d_ws, d_bs_t), got = _gate_bwd(z, dy, a_ln_g, a_ln_b, ws, ws_t, bs_t, side=_join(sides))
    got = _split(got, sides)
    red_bi.took_scatter(got[0])
    red_ao.took_scatter(got[1])
    small = {
        "a_ws": d_ws, "a_bs": d_bs_t.T, "a_ln_g": d_ln_g, "a_ln_b": d_ln_b,
        "kv_norm_g": d_kv_g, "b_kv": d_b_kv, "b_norm_g": d_b_g, "b_bq": d_bq,
        "b_sinks": d_sinks[0:1, :N_Q_HEADS], "final_norm_g": d_final_g,
    }
    packed = [n for n in SMALL_ORDER if n != "a_norm_g"]
    small_shapes = [small[n].shape for n in packed] + [(1, 1)]
    small_pack = _pack_rows([small[n] for n in packed] + [loss_blk[0:1, 0:1]], 64)
    seg = small_pack.shape[0] // 8
    small_pack = small_pack.reshape(8, seg, 128)
    sides = [red_bi.share_side(), red_ao.share_side(), _small_scatter_side(small_pack)]
    (d_a_w_in, from_sibling), got = _mm_tn_exchanging(n_a, dz, name="mm_d_a_w_in", shards=N_CHIPS, side=_join(sides))
    got = _split(got, sides)
    red_bi.took_share(got[0])
    red_ao.took_share(got[1])
    small_mine = _sum_small(small_pack, got[2][0], place)

    chip_sum, (small_all,) = _add_halves(d_a_w_in, from_sibling, name="add_sibling_a_w_in",
                                         side=_small_share_side(small_mine))
    (dx, _, d_a_g), (arrived,) = _mm_nt_rms_bwd([(dz, a_w_in, a_norm_g)], x, dh1, name="mm_dn_a", tm=256,
                                                side=_scatter_side([chip_sum]))
    half_ai = _sum_halves(d_a_w_in, from_sibling, arrived, place, name="sum_chips_a_w_in")
    d_a_g = _rows128(d_a_g)
    sides = [_share_side([half_ai]), _spread_side(d_a_g)]
    got = _split(_comm_call(_join(sides), "share_last"), sides)
    grad_ai = got[0][0]
    small_full = dict(zip(packed + ["loss"], _unpack_rows(small_all.reshape(8 * seg, 128), small_shapes)))
    small_full["a_norm_g"] = _sum_in_device_order(d_a_g, got[1][0], place).reshape(1, -1)
    loss = small_full["loss"].reshape(())

    grad_big = {**red_bo.grads, **red_bi.grads, **red_ao.grads, "a_w_in": grad_ai}
    grads = {}
    for n in SMALL_ORDER:
        gfull = small_full[n]
        if n in SHARDED_SMALL:
            width = p[n].shape[-1]
            gfull = lax.dynamic_slice_in_dim(gfull, chip * width, width, axis=-1)
        grads[n] = gfull.reshape(p[n].shape)

    delta, new_m, new_v = {}, {}, {}
    for n in BIG:
        d, nm, nv, g = _adamw(shard2d[n], grad_big[n], m[n].reshape(shard2d[n].shape),
                              v[n].reshape(shard2d[n].shape), name="adamw_" + n)
        delta[n], new_m[n], new_v[n] = d.reshape(p[n].shape), nm.reshape(p[n].shape), nv.reshape(p[n].shape)
        grads[n] = g.reshape(p[n].shape)
    shapes = [p[n].shape for n in SMALL_ORDER]
    packs = [_pack_rows([src[n] for n in SMALL_ORDER], 8) for src in (p, grads, m, v)]
    outs = _adamw(*packs, name="adamw_small")[:3]
    for res, packed in zip((delta, new_m, new_v), outs):
        for n, val in zip(SMALL_ORDER, _unpack_rows(packed, shapes)):
            res[n] = val

    return (loss, dx[None], *[grads[n] for n in WEIGHTS], *[delta[n] for n in WEIGHTS],
            *[new_m[n] for n in WEIGHTS], *[new_v[n] for n in WEIGHTS])


def kernel(x, a_norm_g, a_w_in, a_ln_g, a_ln_b, a_ws, a_bs, a_w_out, kv_norm_g, w_kv, b_kv, b_norm_g, b_w_in, b_bq, b_sinks, b_w_out, final_norm_g, loss_target, m_a_norm_g, m_a_w_in, m_a_ln_g, m_a_ln_b, m_a_ws, m_a_bs, m_a_w_out, m_kv_norm_g, m_w_kv, m_b_kv, m_b_norm_g, m_b_w_in, m_b_bq, m_b_sinks, m_b_w_out, m_final_norm_g, v_a_norm_g, v_a_w_in, v_a_ln_g, v_a_ln_b, v_a_ws, v_a_bs, v_a_w_out, v_kv_norm_g, v_w_kv, v_b_kv, v_b_norm_g, v_b_w_in, v_b_bq, v_b_sinks, v_b_w_out, v_final_norm_g):
    p = dict(a_norm_g=a_norm_g, a_w_in=a_w_in, a_ln_g=a_ln_g, a_ln_b=a_ln_b, a_ws=a_ws, a_bs=a_bs, a_w_out=a_w_out,
             kv_norm_g=kv_norm_g, w_kv=w_kv, b_kv=b_kv, b_norm_g=b_norm_g, b_w_in=b_w_in, b_bq=b_bq, b_sinks=b_sinks,
             b_w_out=b_w_out, final_norm_g=final_norm_g)
    m = dict(a_norm_g=m_a_norm_g, a_w_in=m_a_w_in, a_ln_g=m_a_ln_g, a_ln_b=m_a_ln_b, a_ws=m_a_ws, a_bs=m_a_bs,
             a_w_out=m_a_w_out, kv_norm_g=m_kv_norm_g, w_kv=m_w_kv, b_kv=m_b_kv, b_norm_g=m_b_norm_g, b_w_in=m_b_w_in,
             b_bq=m_b_bq, b_sinks=m_b_sinks, b_w_out=m_b_w_out, final_norm_g=m_final_norm_g)
    v = dict(a_norm_g=v_a_norm_g, a_w_in=v_a_w_in, a_ln_g=v_a_ln_g, a_ln_b=v_a_ln_b, a_ws=v_a_ws, a_bs=v_a_bs,
             a_w_out=v_a_w_out, kv_norm_g=v_kv_norm_g, w_kv=v_w_kv, b_kv=v_b_kv, b_norm_g=v_b_norm_g, b_w_in=v_b_w_in,
             b_bq=v_b_bq, b_sinks=v_b_sinks, b_w_out=v_b_w_out, final_norm_g=v_final_norm_g)
    return _step(x, loss_target, p, m, v)
```

```python
import functools
import math

import jax
import jax.numpy as jnp
from jax import lax
from jax.experimental import pallas as pl
from jax.experimental.pallas import tpu as pltpu

F32 = jnp.float32
BF16 = jnp.bfloat16

D_MODEL = 1024
CHUNK = 128
A_WIDTH = 2048
A_GROUPS = 16
HEAD_DIM = 64
N_Q_HEADS = 16
N_KV_HEADS = 2
Q_PER_KV = 8
B_WIDTH = 1024
KV_WIDTH = 128
ROPE_THETA = 10000.0
EPS = 1e-5
N_CHIPS = 4

ADAM_LR = 0.001
ADAM_B1 = 0.9
ADAM_B2 = 0.999
ADAM_EPS = 1e-08
ADAM_WD = 0.01
ADAM_STEP = 10

VMEM_LIMIT = 48 * 1024 * 1024
MESH = pl.DeviceIdType.MESH
NEG_BIG = -1e30
HBM = pl.BlockSpec(memory_space=pl.ANY)

NN = (((1,), (0,)), ((), ()))
NT = (((1,), (1,)), ((), ()))
TN = (((0,), (0,)), ((), ()))


def _cparams(**kw):
    return pltpu.CompilerParams(vmem_limit_bytes=VMEM_LIMIT, **kw)


class _Side:
    def __init__(self, ins, out_shapes, sems, start, finish, aliases=None, passing=None):
        self.ins, self.out_shapes, self.sems = list(ins), list(out_shapes), list(sems)
        self.start, self.finish = start, finish
        self.passing = passing or (lambda ins, outs, sems: None)
        self.aliases = dict(aliases or {})


def _join(sides):
    sides = [s for s in sides if s is not None]
    if not sides:
        return None
    offs, i, o, m = [], 0, 0, 0
    for s in sides:
        offs.append((i, o, m))
        i, o, m = i + len(s.ins), o + len(s.out_shapes), m + len(s.sems)

    def run(which):
        def go(ins, outs, sems):
            for s, (a, b, c) in zip(sides, offs):
                getattr(s, which)(ins[a:a + len(s.ins)], outs[b:b + len(s.out_shapes)], sems[c:c + len(s.sems)])
        return go

    aliases = {}
    for s, (a, b, _) in zip(sides, offs):
        aliases.update({a + k: b + v for k, v in s.aliases.items()})
    return _Side([x for s in sides for x in s.ins], [x for s in sides for x in s.out_shapes],
                 [x for s in sides for x in s.sems], run("start"), run("finish"), aliases, run("passing"))


def _split(side_outs, sides):
    out, pos = [], 0
    for s in sides:
        out.append(list(side_outs[pos:pos + len(s.out_shapes)]))
        pos += len(s.out_shapes)
    return out


def _call(body, *, grid, in_specs, out_specs, out_shape, args, name, scratch=(), side=None):
    in_specs, out_specs, out_shape, scratch = list(in_specs), list(out_specs), list(out_shape), list(scratch)
    if side is None:
        res = pl.pallas_call(body, grid=grid, in_specs=in_specs, out_specs=out_specs, out_shape=out_shape,
                             scratch_shapes=scratch, name=name, compiler_params=_cparams())(*args)
        return list(res), []
    n_in, n_out, n_sc = len(in_specs), len(out_specs), len(scratch)
    s_in, s_out = len(side.ins), len(side.out_shapes)

    def wrapped(*refs):
        ins, refs = refs[:n_in], refs[n_in:]
        side_ins, refs = refs[:s_in], refs[s_in:]
        outs, refs = refs[:n_out], refs[n_out:]
        side_outs, refs = refs[:s_out], refs[s_out:]
        scr, side_sems = refs[:n_sc], refs[n_sc:]
        step = 0
        for a, g in enumerate(grid):
            step = step * g + pl.program_id(a)
        steps = math.prod(grid)

        @pl.when(step == 0)
        def _():
            side.start(side_ins, side_outs, side_sems)

        body(*ins, *outs, *scr)

        @pl.when(step == (3 * (steps - 1)) // 4)
        def _():
            side.passing(side_ins, side_outs, side_sems)

        @pl.when(step == steps - 1)
        def _():
            side.finish(side_ins, side_outs, side_sems)

    res = pl.pallas_call(
        wrapped, grid=grid, in_specs=in_specs + [HBM] * s_in, out_specs=out_specs + [HBM] * s_out,
        out_shape=out_shape + side.out_shapes, scratch_shapes=scratch + side.sems,
        input_output_aliases={n_in + k: n_out + v for k, v in side.aliases.items()},
        name=name, compiler_params=_cparams(),
    )(*args, *side.ins)
    return list(res[:n_out]), list(res[n_out:])


def _comm_call(side, name):
    s_in, s_out = len(side.ins), len(side.out_shapes)

    def body(*refs):
        ins, outs, sems = refs[:s_in], refs[s_in:s_in + s_out], refs[s_in + s_out:]
        side.start(ins, outs, sems)
        side.passing(ins, outs, sems)
        side.finish(ins, outs, sems)

    return list(pl.pallas_call(
        body, in_specs=[HBM] * s_in, out_specs=[HBM] * s_out, out_shape=side.out_shapes, scratch_shapes=side.sems,
        input_output_aliases=side.aliases, name=name,
    )(*side.ins))


def _matmul(a, b, *, dims, grid, a_spec, b_spec, o_spec, out_shape, name, acc_axis=None,
            residual=None, r_spec=None, side=None):
    has_res = residual is not None

    def body(*refs):
        if has_res:
            a_ref, b_ref, r_ref, o_ref = refs
        else:
            a_ref, b_ref, o_ref = refs
        part = lax.dot_general(a_ref[...], b_ref[...], dims, preferred_element_type=F32)
        if acc_axis is None:
            if has_res:
                part = part + r_ref[...]
            o_ref[...] = part.astype(o_ref.dtype)
        else:
            k = pl.program_id(acc_axis)

            @pl.when(k == 0)
            def _():
                o_ref[...] = part

            @pl.when(k > 0)
            def _():
                o_ref[...] += part

    in_specs = [a_spec, b_spec] + ([r_spec] if has_res else [])
    args = (a, b) + ((residual,) if has_res else ())
    (out,), side_outs = _call(body, grid=grid, in_specs=in_specs, out_specs=[o_spec], out_shape=[out_shape],
                              args=args, name=name, side=side)
    return (out, side_outs) if side is not None else out


def _row_tile(s, want):
    return min(s, want)


def _mm_nn(a, b, *, name, tn, out_dtype=F32, residual=None, tm=512, side=None):
    s, k = a.shape
    tm = _row_tile(s, tm)
    if b.ndim == 3:
        nsh, _, nc = b.shape
        npb = nc // tn
        n = nsh * nc
        b_spec = pl.BlockSpec((None, k, tn), lambda i, j: (j // npb, 0, j % npb))
    else:
        n = b.shape[1]
        b_spec = pl.BlockSpec((k, tn), lambda i, j: (0, j))
    return _matmul(
        a, b, dims=NN, grid=(s // tm, n // tn),
        a_spec=pl.BlockSpec((tm, k), lambda i, j: (i, 0)), b_spec=b_spec,
        o_spec=pl.BlockSpec((tm, tn), lambda i, j: (i, j)),
        out_shape=jax.ShapeDtypeStruct((s, n), out_dtype), name=name, side=side,
        residual=residual, r_spec=pl.BlockSpec((tm, tn), lambda i, j: (i, j)) if residual is not None else None)


def _mm_nt(a, b, *, name, tn=None, tm=512, out_dtype=F32, side=None):
    s, k = a.shape
    tm = _row_tile(s, tm)
    n = b.shape[0]
    tn = n if tn is None else tn
    return _matmul(
        a, b, dims=NT, grid=(s // tm, n // tn),
        a_spec=pl.BlockSpec((tm, k), lambda i, j: (i, 0)),
        b_spec=pl.BlockSpec((tn, k), lambda i, j: (j, 0)),
        o_spec=pl.BlockSpec((tm, tn), lambda i, j: (i, j)),
        out_shape=jax.ShapeDtypeStruct((s, n), out_dtype), name=name, side=side)


def _mm_tn(a, b, *, name, tm, tn, tk=2048, shards=None, side=None):
    s, m = a.shape
    n = b.shape[1]
    tk = _row_tile(s, tk)
    if shards is None:
        o_spec = pl.BlockSpec((tm, tn), lambda i, j, kk: (i, j))
        out_shape = jax.ShapeDtypeStruct((m, n), F32)
    else:
        assert tm == m
        nc = n // shards
        npb = nc // tn
        o_spec = pl.BlockSpec((None, m, tn), lambda i, j, kk: (j // npb, 0, j % npb))
        out_shape = jax.ShapeDtypeStruct((shards, m, nc), F32)
    return _matmul(
        a, b, dims=TN, grid=(m // tm, n // tn, s // tk), acc_axis=2,
        a_spec=pl.BlockSpec((tk, tm), lambda i, j, kk: (kk, i)),
        b_spec=pl.BlockSpec((tk, tn), lambda i, j, kk: (kk, j)),
        o_spec=o_spec, out_shape=out_shape, name=name, side=side)


def _rstd(x):
    return lax.rsqrt(jnp.mean(x * x, axis=-1, keepdims=True) + EPS)


def _rms_fwd(x, gains, *, name, tr=1024):
    s, d = x.shape
    tr = _row_tile(s, tr)
    ng = len(gains)

    def body(*refs):
        xv = refs[0][...]
        xh = xv * _rstd(xv)
        for t in range(ng):
            refs[1 + ng + t][...] = (xh * refs[1 + t][...]).astype(BF16)

    row = pl.BlockSpec((tr, d), lambda i: (i, 0))
    vec = pl.BlockSpec((1, d), lambda i: (0, 0))
    outs, _ = _call(body, grid=(s // tr,), in_specs=[row] + [vec] * ng, out_specs=[row] * ng,
                    out_shape=[jax.ShapeDtypeStruct((s, d), BF16)] * ng, args=(x, *gains), name=name)
    return outs


def _accumulate(i, ref, value):
    @pl.when(i == 0)
    def _():
        ref[...] = value

    @pl.when(i > 0)
    def _():
        ref[...] += value


def _mm_residual_norms(y, w, res, gains, *, name, tm=512, side=None):
    s, k = y.shape
    d = w.shape[1]
    tm = _row_tile(s, tm)
    ng = len(gains)

    def body(y_ref, w_ref, r_ref, *rest):
        g_refs, h_ref, n_refs = rest[:ng], rest[ng], rest[ng + 1:]
        h = r_ref[...] + jnp.dot(y_ref[...], w_ref[...], preferred_element_type=F32)
        h_ref[...] = h
        xh = h * _rstd(h)
        for t in range(ng):
            n_refs[t][...] = (xh * g_refs[t][...]).astype(BF16)

    row = pl.BlockSpec((tm, d), lambda i: (i, 0))
    vec = pl.BlockSpec((1, d), lambda i: (0, 0))
    return _call(
        body, grid=(s // tm,),
        in_specs=[pl.BlockSpec((tm, k), lambda i: (i, 0)), pl.BlockSpec((k, d), lambda i: (0, 0)), row] + [vec] * ng,
        out_specs=[row] * (1 + ng),
        out_shape=[jax.ShapeDtypeStruct((s, d), F32)] + [jax.ShapeDtypeStruct((s, d), BF16)] * ng,
        args=(y, w, res, *gains), name=name, side=side)


def _mm_residual_loss(y, w, res, tgt, gain, *, name, tm=512):
    s, k = y.shape
    d = w.shape[1]
    tm = _row_tile(s, tm)

    def body(y_ref, w_ref, r_ref, t_ref, g_ref, loss_ref, dh_ref, dhb_ref, dg_ref):
        i = pl.program_id(0)
        hv = r_ref[...] + jnp.dot(y_ref[...], w_ref[...], preferred_element_type=F32)
        g = g_ref[...]
        r = _rstd(hv)
        xh = hv * r
        diff = xh * g - t_ref[...]
        part = 0.5 / d * jnp.sum(jnp.sum(diff * diff, axis=-1, keepdims=True), axis=0, keepdims=True)
        dout = diff * (1.0 / d)
        a = dout * g
        dh = r * (a - xh * jnp.mean(a * xh, axis=-1, keepdims=True))
        dh_ref[...] = dh
        dhb_ref[...] = dh.astype(BF16)
        _accumulate(i, dg_ref, jnp.sum(dout * xh, axis=0, keepdims=True))
        _accumulate(i, loss_ref, jnp.broadcast_to(part, (8, 128)))

    row = pl.BlockSpec((tm, d), lambda i: (i, 0))
    vec = pl.BlockSpec((1, d), lambda i: (0, 0))
    outs, _ = _call(
        body, grid=(s // tm,),
        in_specs=[pl.BlockSpec((tm, k), lambda i: (i, 0)), pl.BlockSpec((k, d), lambda i: (0, 0)), row, row, vec],
        out_specs=[pl.BlockSpec((8, 128), lambda i: (0, 0)), row, row, vec],
        out_shape=[jax.ShapeDtypeStruct((8, 128), F32), jax.ShapeDtypeStruct((s, d), F32),
                   jax.ShapeDtypeStruct((s, d), BF16), jax.ShapeDtypeStruct((1, d), F32)],
        args=(y, w, res, tgt, gain), name=name)
    return outs


def _mm_nt_rms_bwd(terms, x, dres, *, name, tm, side=None):
    s, d = x.shape
    tm = _row_tile(s, tm)
    nt = len(terms)

    def body(*refs):
        a_refs, b_refs, g_refs = refs[0:3 * nt:3], refs[1:3 * nt:3], refs[2:3 * nt:3]
        x_ref, dres_ref = refs[3 * nt], refs[3 * nt + 1]
        dx_ref, dxb_ref = refs[3 * nt + 2], refs[3 * nt + 3]
        dg_refs = refs[3 * nt + 4:]
        i = pl.program_id(0)
        xv = x_ref[...]
        r = _rstd(xv)
        xh = xv * r
        acc = jnp.zeros_like(xv)
        for t in range(nt):
            b_ref = b_refs[t]
            if len(b_ref.shape) == 3:
                kc = b_ref.shape[2]
                dn = None
                for sh in range(b_ref.shape[0]):
                    part = lax.dot_general(a_refs[t][:, sh * kc:(sh + 1) * kc], b_ref[sh], NT, preferred_element_type=F32)
                    dn = part if dn is None else dn + part
            else:
                dn = lax.dot_general(a_refs[t][...], b_ref[...], NT, preferred_element_type=F32)
            acc = acc + dn * g_refs[t][...]
            _accumulate(i, dg_refs[t], jnp.sum(dn * xh, axis=0, keepdims=True))
        dx = dres_ref[...] + r * (acc - xh * jnp.mean(acc * xh, axis=-1, keepdims=True))
        dx_ref[...] = dx
        dxb_ref[...] = dx.astype(BF16)

    row = pl.BlockSpec((tm, d), lambda i: (i, 0))
    vec = pl.BlockSpec((1, d), lambda i: (0, 0))
    in_specs, args = [], []
    for a, b, g in terms:
        in_specs += [pl.BlockSpec((tm, a.shape[1]), lambda i: (i, 0)),
                     pl.BlockSpec(b.shape, (lambda i: (0, 0, 0)) if b.ndim == 3 else (lambda i: (0, 0))), vec]
        args += [a, b, g]
    return _call(
        body, grid=(s // tm,), in_specs=in_specs + [row, row], out_specs=[row, row] + [vec] * nt,
        out_shape=[jax.ShapeDtypeStruct((s, d), F32), jax.ShapeDtypeStruct((s, d), BF16)]
        + [jax.ShapeDtypeStruct((1, d), F32)] * nt,
        args=(*args, x, dres), name=name, side=side)


def _causal_mask(transposed=False):
    row = lax.broadcasted_iota(jnp.int32, (CHUNK, CHUNK), 0)
    col = lax.broadcasted_iota(jnp.int32, (CHUNK, CHUNK), 1)
    return col >= row if transposed else col <= row


def _silu_parts(g):
    sg = jax.nn.sigmoid(g)
    return g * sg, sg * (1.0 + g * (1.0 - sg))


def _gate_fwd(z, ln_g, ln_b, ws, bs_t, *, tr=512, side=None):
    s = z.shape[0]
    tr = _row_tile(s, tr)
    w = A_WIDTH

    def body(u_ref, v_ref, g_ref, lg_ref, lb_ref, ws_ref, bst_ref, y_ref):
        v = v_ref[...].astype(F32)
        mu = jnp.mean(v, axis=-1, keepdims=True)
        xc = v - mu
        rs = lax.rsqrt(jnp.mean(xc * xc, axis=-1, keepdims=True) + EPS)
        vln = (xc * rs * lg_ref[...] + lb_ref[...]).astype(BF16)
        mask = _causal_mask()
        for grp in range(A_GROUPS):
            cols = slice(grp * CHUNK, (grp + 1) * CHUNK)
            wsm = jnp.where(mask, ws_ref[grp], 0.0).astype(BF16)
            bcol = bst_ref[:, grp:grp + 1]
            for ci in range(tr // CHUNK):
                rows = slice(ci * CHUNK, (ci + 1) * CHUNK)
                sv = jnp.dot(wsm, vln[rows, cols], preferred_element_type=F32) + bcol
                gv = g_ref[rows, cols].astype(F32)
                y_ref[rows, cols] = (u_ref[rows, cols].astype(F32) * sv * (gv * jax.nn.sigmoid(gv))).astype(BF16)

    vec = pl.BlockSpec((1, w), lambda i: (0, 0))
    (y,), side_outs = _call(
        body, grid=(s // tr,),
        in_specs=[pl.BlockSpec((tr, w), lambda i: (i, 0)), pl.BlockSpec((tr, w), lambda i: (i, 1)),
                  pl.BlockSpec((tr, w), lambda i: (i, 2)), vec, vec,
                  pl.BlockSpec((A_GROUPS, CHUNK, CHUNK), lambda i: (0, 0, 0)),
                  pl.BlockSpec((CHUNK, A_GROUPS), lambda i: (0, 0))],
        out_specs=[pl.BlockSpec((tr, w), lambda i: (i, 0))],
        out_shape=[jax.ShapeDtypeStruct((s, w), BF16)], args=(z, z, z, ln_g, ln_b, ws, bs_t), name="gate_fwd",
        side=side)
    return y, side_outs


def _gate_bwd(z, dy, ln_g, ln_b, ws, ws_t, bs_t, *, tr=256, side=None):
    s = z.shape[0]
    tr = _row_tile(s, tr)
    w = A_WIDTH
    nsteps = s // tr

    def body(u_ref, v_ref, g_ref, dy_ref, lg_ref, lb_ref, ws_ref, wst_ref, bst_ref,
             dz_ref, dlg_ref, dlb_ref, dws_ref, dbst_ref, dvln_sc, dsv_sc):
        i = pl.program_id(0)

        @pl.when(i == 0)
        def _():
            dws_ref[...] = jnp.zeros_like(dws_ref)
            dsv_sc[...] = jnp.zeros_like(dsv_sc)

        v = v_ref[...].astype(F32)
        mu = jnp.mean(v, axis=-1, keepdims=True)
        xc = v - mu
        rs = lax.rsqrt(jnp.mean(xc * xc, axis=-1, keepdims=True) + EPS)
        xh = xc * rs
        lg = lg_ref[...]
        vln = (xh * lg + lb_ref[...]).astype(BF16)
        mask = _causal_mask()
        mask_t = _causal_mask(transposed=True)
        for grp in range(A_GROUPS):
            cols = slice(grp * CHUNK, (grp + 1) * CHUNK)
            wsm = jnp.where(mask, ws_ref[grp], 0.0).astype(BF16)
            wsm_t = jnp.where(mask_t, wst_ref[grp], 0.0).astype(BF16)
            bcol = bst_ref[:, grp:grp + 1]
            for ci in range(tr // CHUNK):
                rows = slice(ci * CHUNK, (ci + 1) * CHUNK)
                vb = vln[rows, cols]
                sv = jnp.dot(wsm, vb, preferred_element_type=F32) + bcol
                uv = u_ref[rows, cols].astype(F32)
                silu, dsilu = _silu_parts(g_ref[rows, cols].astype(F32))
                dyv = dy_ref[rows, cols].astype(F32)
                dyu = dyv * uv
                dz_ref[rows, cols] = (dyv * sv * silu).astype(BF16)
                dz_ref[rows, 2 * w + grp * CHUNK:2 * w + (grp + 1) * CHUNK] = (dyu * sv * dsilu).astype(BF16)
                dsv = dyu * silu
                dsvb = dsv.astype(BF16)
                dvln_sc[rows, cols] = jnp.dot(wsm_t, dsvb, preferred_element_type=F32)
                dws_ref[grp] += lax.dot_general(dsvb, vb, NT, preferred_element_type=F32)
                dsv_sc[grp] += dsv
        dvln = dvln_sc[...]
        dlg_t = jnp.sum(dvln * xh, axis=0, keepdims=True)
        dlb_t = jnp.sum(dvln, axis=0, keepdims=True)
        a = dvln * lg
        dv = rs * (a - jnp.mean(a, axis=-1, keepdims=True) - xh * jnp.mean(a * xh, axis=-1, keepdims=True))
        dz_ref[:, w:2 * w] = dv.astype(BF16)

        @pl.when(i == 0)
        def _():
            dlg_ref[...] = dlg_t
            dlb_ref[...] = dlb_t

        @pl.when(i > 0)
        def _():
            dlg_ref[...] += dlg_t
            dlb_ref[...] += dlb_t

        @pl.when(i == nsteps - 1)
        def _():
            for grp in range(A_GROUPS):
                dws_ref[grp] = jnp.where(mask, dws_ref[grp], 0.0)
                dbst_ref[:, grp:grp + 1] = jnp.sum(dsv_sc[grp], axis=-1, keepdims=True)

    vec = pl.BlockSpec((1, w), lambda i: (0, 0))
    wsspec = pl.BlockSpec((A_GROUPS, CHUNK, CHUNK), lambda i: (0, 0, 0))
    bsspec = pl.BlockSpec((CHUNK, A_GROUPS), lambda i: (0, 0))
    return _call(
        body, grid=(nsteps,),
        in_specs=[pl.BlockSpec((tr, w), lambda i: (i, 0)), pl.BlockSpec((tr, w), lambda i: (i, 1)),
                  pl.BlockSpec((tr, w), lambda i: (i, 2)), pl.BlockSpec((tr, w), lambda i: (i, 0)),
                  vec, vec, wsspec, wsspec, bsspec],
        out_specs=[pl.BlockSpec((tr, 3 * w), lambda i: (i, 0)), vec, vec, wsspec, bsspec],
        out_shape=[jax.ShapeDtypeStruct((s, 3 * w), BF16), jax.ShapeDtypeStruct((1, w), F32),
                   jax.ShapeDtypeStruct((1, w), F32), jax.ShapeDtypeStruct((A_GROUPS, CHUNK, CHUNK), F32),
                   jax.ShapeDtypeStruct((CHUNK, A_GROUPS), F32)],
        scratch=[pltpu.VMEM((tr, w), F32), pltpu.VMEM((A_GROUPS, CHUNK, CHUNK), F32)],
        args=(z, z, z, dy, ln_g, ln_b, ws, ws_t, bs_t), name="gate_bwd", side=side)


HEADS_PER_BLOCK = 128 // HEAD_DIM
BLOCKS_PER_KV = Q_PER_KV // HEADS_PER_BLOCK
SCALE = HEAD_DIM ** -0.5
LOG2_E = math.log2(math.e)


def _rope_tables(s):
    lane = jnp.arange(128)
    inv_freq = ROPE_THETA ** (-(2 * (lane % (HEAD_DIM // 2))).astype(F32) / HEAD_DIM)
    sign = jnp.where(lane % HEAD_DIM < HEAD_DIM // 2, -1.0, 1.0).astype(F32)
    ang = jnp.arange(s, dtype=F32)[:, None] * inv_freq[None, :]
    return jnp.cos(ang), jnp.sin(ang) * sign[None, :]


def _swap_halves(x):
    n = x.shape[-1]
    lane = lax.broadcasted_iota(jnp.int32, x.shape, x.ndim - 1)
    first = (lane % HEAD_DIM) < (HEAD_DIM // 2)
    return jnp.where(first, pltpu.roll(x, n - HEAD_DIM // 2, x.ndim - 1), pltpu.roll(x, HEAD_DIM // 2, x.ndim - 1))


def _left_half(rows):
    return lax.broadcasted_iota(jnp.int32, (rows, 128), 1) < HEAD_DIM


def _dup_heads(x):
    left = _left_half(x.shape[0])
    swapped = pltpu.roll(x, HEAD_DIM, 1)
    return jnp.concatenate([jnp.where(left, x, swapped), jnp.where(left, swapped, x)], axis=-1)


def _fold_heads(a):
    b0, b1 = a[:, :128], a[:, 128:]
    f0 = b0 + pltpu.roll(b0, HEAD_DIM, 1)
    f1 = b1 + pltpu.roll(b1, HEAD_DIM, 1)
    return jnp.where(_left_half(a.shape[0]), f0, f1)


def _kv_rope(n_kv, w_kv, b_kv, cos, sin, *, tr=2048):
    s, d = n_kv.shape
    tr = _row_tile(s, tr)

    def body(n_ref, w_ref, b_ref, c_ref, s_ref, k_ref, v_ref):
        x = jnp.dot(n_ref[...], w_ref[...], preferred_element_type=F32) + b_ref[...]
        k = x[:, :KV_WIDTH]
        k_ref[...] = _dup_heads(k * c_ref[...] + _swap_halves(k) * s_ref[...]).astype(BF16)
        v_ref[...] = _dup_heads(x[:, KV_WIDTH:]).astype(BF16)

    tab = pl.BlockSpec((tr, KV_WIDTH), lambda i: (i, 0))
    wide = pl.BlockSpec((tr, 2 * KV_WIDTH), lambda i: (i, 0))
    outs, _ = _call(body, grid=(s // tr,),
                    in_specs=[pl.BlockSpec((tr, d), lambda i: (i, 0)), pl.BlockSpec((d, 2 * KV_WIDTH), lambda i: (0, 0)),
                              pl.BlockSpec((1, 2 * KV_WIDTH), lambda i: (0, 0)), tab, tab],
                    out_specs=[wide, wide], out_shape=[jax.ShapeDtypeStruct((s, 2 * KV_WIDTH), BF16)] * 2,
                    args=(n_kv, w_kv, b_kv, cos, sin), name="kv_rope")
    return outs


def _kv_rope_bwd(dk2, dv2, cos, sin, *, tr=2048):
    s = dk2.shape[0]
    tr = _row_tile(s, tr)

    def body(dk_ref, dv_ref, c_ref, s_ref, dkv_ref, db_ref):
        i = pl.program_id(0)
        d = _fold_heads(dk_ref[...])
        dk = d * c_ref[...] + _swap_halves(d * s_ref[...])
        dvv = _fold_heads(dv_ref[...])
        dkv_ref[:, :KV_WIDTH] = dk.astype(BF16)
        dkv_ref[:, KV_WIDTH:] = dvv.astype(BF16)
        sk = jnp.sum(dk, axis=0, keepdims=True)
        sv = jnp.sum(dvv, axis=0, keepdims=True)

        @pl.when(i == 0)
        def _():
            db_ref[:, :KV_WIDTH] = sk
            db_ref[:, KV_WIDTH:] = sv

        @pl.when(i > 0)
        def _():
            db_ref[:, :KV_WIDTH] += sk
            db_ref[:, KV_WIDTH:] += sv

    tab = pl.BlockSpec((tr, KV_WIDTH), lambda i: (i, 0))
    wide = pl.BlockSpec((tr, 2 * KV_WIDTH), lambda i: (i, 0))
    outs, _ = _call(body, grid=(s // tr,), in_specs=[wide, wide, tab, tab],
                    out_specs=[wide, pl.BlockSpec((1, 2 * KV_WIDTH), lambda i: (0, 0))],
                    out_shape=[jax.ShapeDtypeStruct((s, 2 * KV_WIDTH), BF16),
                               jax.ShapeDtypeStruct((1, 2 * KV_WIDTH), F32)],
                    args=(dk2, dv2, cos, sin), name="kv_rope_bwd")
    return outs


def _from_previous():
    cols = Q_PER_KV * CHUNK
    k = lax.broadcasted_iota(jnp.int32, (CHUNK, cols), 0)
    q = lax.broadcasted_iota(jnp.int32, (CHUNK, cols), 1) & (CHUNK - 1)
    return k > q


def _fold(x2, prev):
    return jnp.where(prev, x2[:CHUNK], x2[CHUNK:])


def _unfold(x, prev):
    zero = jnp.zeros_like(x)
    return jnp.concatenate([jnp.where(prev, x, zero), jnp.where(prev, zero, x)], axis=0)


def _stack_heads(blocks, left):
    parts = []
    for b in blocks:
        parts.append(jnp.where(left, b, jnp.zeros_like(b)))
        parts.append(jnp.where(left, jnp.zeros_like(b), b))
    return jnp.concatenate(parts, axis=0)


def _unstack_heads(xt):
    top = lax.broadcasted_iota(jnp.int32, (128, CHUNK), 0) < HEAD_DIM
    return [jnp.where(top, xt[:, (2 * b) * CHUNK:(2 * b + 1) * CHUNK], xt[:, (2 * b + 1) * CHUNK:(2 * b + 2) * CHUNK]).T
            for b in range(BLOCKS_PER_KV)]


def _sink_row(sk_ref, kvh):
    return jnp.concatenate([jnp.full((1, CHUNK), sk_ref[0, kvh * Q_PER_KV + r], F32) for r in range(Q_PER_KV)], axis=1)


def _stacked_probs(qs, kd, prev, sink, i):
    sc2 = lax.dot_general(kd, qs, NT, preferred_element_type=F32)
    no_previous = jnp.where(i > 0, 0.0, NEG_BIG)
    sc = jnp.where(prev, sc2[:CHUNK] + no_previous, sc2[CHUNK:])
    sink = sink * (1.0 / SCALE)
    m = jnp.maximum(jnp.max(sc, axis=0, keepdims=True), sink)
    p = jnp.exp2((sc - m) * (SCALE * LOG2_E))
    esink = jnp.exp2((sink - m) * (SCALE * LOG2_E))
    inv = 1.0 / (jnp.sum(p, axis=0, keepdims=True) + esink)
    return p * inv, esink * inv


def _lane_block(b):
    return slice(b * 128, (b + 1) * 128)


def _rope_blocks(zq_ref, bq_ref, cos, sin, kvh, rows):
    out = []
    for b in range(BLOCKS_PER_KV):
        cols = _lane_block(kvh * BLOCKS_PER_KV + b)
        q = zq_ref[rows, cols].astype(F32) + bq_ref[:, cols]
        out.append((q * cos + _swap_halves(q) * sin).astype(BF16))
    return out


CHUNKS_PER_STEP = 4


def _attn_specs():
    rows = CHUNKS_PER_STEP * CHUNK
    qspec = pl.BlockSpec((rows, B_WIDTH), lambda i: (i, 0))
    gspec = pl.BlockSpec((rows, B_WIDTH), lambda i: (i, 1))
    prev = pl.BlockSpec((CHUNK, 2 * KV_WIDTH), lambda i: (jnp.maximum(CHUNKS_PER_STEP * i - 1, 0), 0))
    cur = pl.BlockSpec((rows, 2 * KV_WIDTH), lambda i: (i, 0))
    tab = pl.BlockSpec((rows, KV_WIDTH), lambda i: (i, 0))
    bq = pl.BlockSpec((1, B_WIDTH), lambda i: (0, 0))
    sinks = pl.BlockSpec(memory_space=pltpu.SMEM)
    return qspec, gspec, prev, cur, tab, bq, sinks


def _chunk_keys(prev_ref, cur_ref, sub):
    before = prev_ref[...] if sub == 0 else cur_ref[(sub - 1) * CHUNK:sub * CHUNK]
    return jnp.concatenate([before, cur_ref[sub * CHUNK:(sub + 1) * CHUNK]], axis=0)


def _attn_fwd(zb, k2, v2, cos, sin, b_bq, sinks, *, side=None):
    s = zb.shape[0]

    def body(zq_ref, zg_ref, kp_ref, kc_ref, vp_ref, vc_ref, c_ref, s_ref, bq_ref, sk_ref, y_ref):
        prev = _from_previous()
        left = _left_half(CHUNK)
        for sub in range(CHUNKS_PER_STEP):
            chunk = CHUNKS_PER_STEP * pl.program_id(0) + sub
            rows = slice(sub * CHUNK, (sub + 1) * CHUNK)
            cos, sin = c_ref[rows, :], s_ref[rows, :]
            kcat, vcat = _chunk_keys(kp_ref, kc_ref, sub), _chunk_keys(vp_ref, vc_ref, sub)
            for kvh in range(N_KV_HEADS):
                qs = _stack_heads(_rope_blocks(zq_ref, bq_ref, cos, sin, kvh, rows), left)
                p, _ = _stacked_probs(qs, kcat[:, _lane_block(kvh)], prev, _sink_row(sk_ref, kvh), chunk)
                ot = lax.dot_general(vcat[:, _lane_block(kvh)], _unfold(p, prev).astype(BF16), TN,
                                     preferred_element_type=F32)
                for b, ob in enumerate(_unstack_heads(ot)):
                    cols = _lane_block(kvh * BLOCKS_PER_KV + b)
                    gv = zg_ref[rows, cols].astype(F32)
                    y_ref[rows, cols] = (ob * (gv * jax.nn.sigmoid(gv))).astype(BF16)

    qspec, gspec, prev, cur, tab, bq, sk = _attn_specs()
    (y,), side_outs = _call(body, grid=(s // (CHUNKS_PER_STEP * CHUNK),),
                            in_specs=[qspec, gspec, prev, cur, prev, cur, tab, tab, bq, sk],
                            out_specs=[qspec], out_shape=[jax.ShapeDtypeStruct((s, B_WIDTH), BF16)],
                            args=(zb, zb, k2, k2, v2, v2, cos, sin, b_bq, sinks), name="attn_fwd", side=side)
    return y, side_outs


def _attn_bwd(zb, dyb, k2, v2, cos, sin, b_bq, sinks, *, side=None):
    s = zb.shape[0]

    def body(zq_ref, zg_ref, dy_ref, kp_ref, kc_ref, vp_ref, vc_ref, c_ref, s_ref, bq_ref, sk_ref,
             dz_ref, dk_ref, dv_ref, dbq_ref, dsk_ref):
        i = pl.program_id(0)

        @pl.when(i == 0)
        def _():
            dk_ref[...] = jnp.zeros_like(dk_ref)
            dv_ref[...] = jnp.zeros_like(dv_ref)
            dbq_ref[...] = jnp.zeros_like(dbq_ref)
            dsk_ref[...] = jnp.zeros_like(dsk_ref)

        prev = _from_previous()
        left = _left_half(CHUNK)
        lane = lax.broadcasted_iota(jnp.int32, (1, 128), 1)
        dsk_row = jnp.zeros((1, 128), F32)
        for sub in range(CHUNKS_PER_STEP):
            chunk = CHUNKS_PER_STEP * i + sub
            rows = slice(sub * CHUNK, (sub + 1) * CHUNK)
            cos, sin = c_ref[rows, :], s_ref[rows, :]
            kcat, vcat = _chunk_keys(kp_ref, kc_ref, sub), _chunk_keys(vp_ref, vc_ref, sub)
            cur_rows = pl.ds(pl.multiple_of(chunk * CHUNK, CHUNK), CHUNK)
            for kvh in range(N_KV_HEADS):
                kd, vd = kcat[:, _lane_block(kvh)], vcat[:, _lane_block(kvh)]
                qs = _stack_heads(_rope_blocks(zq_ref, bq_ref, cos, sin, kvh, rows), left)
                p, psink = _stacked_probs(qs, kd, prev, _sink_row(sk_ref, kvh), chunk)
                pb = _unfold(p, prev).astype(BF16)
                ot = lax.dot_general(vd, pb, TN, preferred_element_type=F32)
                gates, dys = [], []
                for b in range(BLOCKS_PER_KV):
                    cols = _lane_block(kvh * BLOCKS_PER_KV + b)
                    gates.append(_silu_parts(zg_ref[rows, cols].astype(F32)))
                    dys.append(dy_ref[rows, cols].astype(F32))
                dos = _stack_heads([(dyv * silu).astype(BF16) for dyv, (silu, _) in zip(dys, gates)], left)
                dp = _fold(lax.dot_general(vd, dos, NT, preferred_element_type=F32), prev)
                delta = jnp.sum(p * dp, axis=0, keepdims=True)
                ds = _unfold(p * (dp - delta) * SCALE, prev).astype(BF16)
                dqt = lax.dot_general(kd, ds, TN, preferred_element_type=F32)
                dk_part = jnp.dot(ds, qs, preferred_element_type=F32)
                dv_part = jnp.dot(pb, dos, preferred_element_type=F32)
                dk_ref[cur_rows, _lane_block(kvh)] += dk_part[CHUNK:]
                dv_ref[cur_rows, _lane_block(kvh)] += dv_part[CHUNK:]

                @pl.when(chunk > 0)
                def _(kvh=kvh, chunk=chunk, dk_part=dk_part, dv_part=dv_part):
                    prev_rows = pl.ds(pl.multiple_of((chunk - 1) * CHUNK, CHUNK), CHUNK)
                    dk_ref[prev_rows, _lane_block(kvh)] += dk_part[:CHUNK]
                    dv_ref[prev_rows, _lane_block(kvh)] += dv_part[:CHUNK]

                sink_grad = psink * delta
                for r in range(Q_PER_KV):
                    dsink = -jnp.sum(sink_grad[:, r * CHUNK:(r + 1) * CHUNK], axis=1, keepdims=True)
                    dsk_row = dsk_row + jnp.where(lane == kvh * Q_PER_KV + r, dsink, 0.0)
                blocks = zip(_unstack_heads(ot), _unstack_heads(dqt), dys, gates)
                for b, (ob, dqr, dyv, (_, dsilu)) in enumerate(blocks):
                    blk = kvh * BLOCKS_PER_KV + b
                    dq = dqr * cos + _swap_halves(dqr * sin)
                    dbq_ref[:, _lane_block(blk)] += jnp.sum(dq, axis=0, keepdims=True)
                    dz_ref[rows, _lane_block(blk)] = dq.astype(BF16)
                    dz_ref[rows, _lane_block(B_WIDTH // 128 + blk)] = (dyv * ob * dsilu).astype(BF16)
        dsk_ref[0:1, :] += dsk_row

    qspec, gspec, prev, cur, tab, bq, sk = _attn_specs()
    full = pl.BlockSpec((s, 2 * KV_WIDTH), lambda i: (0, 0))
    return _call(
        body, grid=(s // (CHUNKS_PER_STEP * CHUNK),),
        in_specs=[qspec, gspec, qspec, prev, cur, prev, cur, tab, tab, bq, sk],
        out_specs=[pl.BlockSpec((CHUNKS_PER_STEP * CHUNK, 2 * B_WIDTH), lambda i: (i, 0)), full, full, bq,
                   pl.BlockSpec((8, 128), lambda i: (0, 0))],
        out_shape=[jax.ShapeDtypeStruct((s, 2 * B_WIDTH), BF16), jax.ShapeDtypeStruct((s, 2 * KV_WIDTH), F32),
                   jax.ShapeDtypeStruct((s, 2 * KV_WIDTH), F32), jax.ShapeDtypeStruct((1, B_WIDTH), F32),
                   jax.ShapeDtypeStruct((8, 128), F32)],
        args=(zb, zb, dyb, k2, k2, v2, v2, cos, sin, b_bq, sinks), name="attn_bwd", side=side)


def _place():
    x, y, c = lax.axis_index("x"), lax.axis_index("y"), lax.axis_index("c")
    return x, y, c, [(1 - x, y), (x, 1 - y), (1 - x, 1 - y)]


def _relations():
    return [(r >> 2 & 1, r >> 1 & 1, r & 1) for r in range(1, 8)]


def _gather_side(arrs):
    n = len(arrs)

    def copies(ins, outs, sems):
        send_ici, recv_ici, send_d2d, recv_d2d, local_sem = sems
        x, y, c, chips = _place()
        me = 2 * x + y

        def rows(a, half):
            hr = arrs[a].shape[0] // 2
            return pl.ds(half * hr, hr)

        def ici(a, j, src_chip, to):
            return pltpu.make_async_remote_copy(
                src_ref=ins[a].at[rows(a, c)], dst_ref=outs[a].at[src_chip, rows(a, c)],
                send_sem=send_ici.at[a, j], recv_sem=recv_ici.at[a, j], device_id=to, device_id_type=MESH)

        def d2d(a, j, chip, half):
            blk = outs[a].at[chip, rows(a, half)]
            return pltpu.make_async_remote_copy(
                src_ref=blk, dst_ref=blk, send_sem=send_d2d.at[a, j], recv_sem=recv_d2d.at[a, j],
                device_id=(x, y, 1 - c), device_id_type=MESH)

        local = [pltpu.make_async_copy(ins[a], outs[a].at[me], local_sem.at[a]) for a in range(n)]
        pairs = [(a, j, chip) for a in range(n) for j, chip in enumerate(chips)]
        return c, me, local, ici, d2d, pairs

    def start(ins, outs, sems):
        c, me, local, ici, _, pairs = copies(ins, outs, sems)
        for cp in local:
            cp.start()
        for a, j, chip in pairs:
            ici(a, j, me, (*chip, c)).start()

    def passing(ins, outs, sems):
        c, _, _, ici, d2d, pairs = copies(ins, outs, sems)
        for a, j, (px, py) in pairs:
            ici(a, j, 2 * px + py, (px, py, c)).wait_recv()
            d2d(a, j, 2 * px + py, c).start()

    def finish(ins, outs, sems):
        c, me, local, ici, d2d, pairs = copies(ins, outs, sems)
        for a, j, (px, py) in pairs:
            d2d(a, j, 2 * px + py, 1 - c).wait_recv()
        for a, j, (px, py) in pairs:
            ici(a, j, me, (px, py, c)).wait_send()
            d2d(a, j, 2 * px + py, c).wait_send()
        for cp in local:
            cp.wait()

    return _Side(arrs, [jax.ShapeDtypeStruct((N_CHIPS,) + a.shape, a.dtype) for a in arrs],
                 [pltpu.SemaphoreType.DMA((n, 3))] * 4 + [pltpu.SemaphoreType.DMA((n,))], start, finish,
                 passing=passing)


def _exchange_side(grads):
    n = len(grads)

    def copies(ins, outs, sems):
        send_sem, recv_sem = sems
        x, y, c, _ = _place()
        cps = []
        for a in range(n):
            hr = grads[a].shape[1] // 2
            cps.append(pltpu.make_async_remote_copy(
                src_ref=ins[a].at[:, pl.ds((1 - c) * hr, hr), :], dst_ref=outs[a],
                send_sem=send_sem.at[a], recv_sem=recv_sem.at[a], device_id=(x, y, 1 - c), device_id_type=MESH))
        return cps

    def start(ins, outs, sems):
        for cp in copies(ins, outs, sems):
            cp.start()

    def finish(ins, outs, sems):
        for cp in copies(ins, outs, sems):
            cp.wait()

    return _Side(grads, [jax.ShapeDtypeStruct((g.shape[0], g.shape[1] // 2, g.shape[2]), g.dtype) for g in grads],
                 [pltpu.SemaphoreType.DMA((n,))] * 2, start, finish)


def _scatter_side(chip_sums, small=None):
    n = len(chip_sums)
    arrs = list(chip_sums) + ([small] if small is not None else [])

    def copies(ins, outs, sems):
        x, y, c, chips = _place()
        cps = []
        for a in range(n):
            for j, (px, py) in enumerate(chips):
                cps.append(pltpu.make_async_remote_copy(
                    src_ref=ins[a].at[2 * px + py], dst_ref=outs[a].at[j],
                    send_sem=sems[0].at[a, j], recv_sem=sems[1].at[a, j], device_id=(px, py, c), device_id_type=MESH))
        if small is not None:
            for r, (fx, fy, fc) in enumerate(_relations(), start=1):
                px, py, pc = x ^ fx, y ^ fy, c ^ fc
                cps.append(pltpu.make_async_remote_copy(
                    src_ref=ins[n].at[4 * px + 2 * py + pc], dst_ref=outs[n].at[r],
                    send_sem=sems[2].at[r - 1], recv_sem=sems[3].at[r - 1], device_id=(px, py, pc),
                    device_id_type=MESH))
        return cps

    def start(ins, outs, sems):
        for cp in copies(ins, outs, sems):
            cp.start()

    def finish(ins, outs, sems):
        for cp in copies(ins, outs, sems):
            cp.wait()

    shapes = [jax.ShapeDtypeStruct((3,) + t.shape[1:], t.dtype) for t in chip_sums]
    sems = [pltpu.SemaphoreType.DMA((n, 3))] * 2
    if small is not None:
        shapes.append(jax.ShapeDtypeStruct(small.shape, small.dtype))
        sems += [pltpu.SemaphoreType.DMA((7,))] * 2
    return _Side(arrs, shapes, sems, start, finish)


def _small_scatter_side(small):
    def copies(ins, outs, sems):
        x, y, c, _ = _place()
        cps = []
        for r, (fx, fy, fc) in enumerate(_relations(), start=1):
            px, py, pc = x ^ fx, y ^ fy, c ^ fc
            cps.append(pltpu.make_async_remote_copy(
                src_ref=ins[0].at[4 * px + 2 * py + pc], dst_ref=outs[0].at[r],
                send_sem=sems[0].at[r - 1], recv_sem=sems[1].at[r - 1], device_id=(px, py, pc), device_id_type=MESH))
        return cps

    def start(ins, outs, sems):
        for cp in copies(ins, outs, sems):
            cp.start()

    def finish(ins, outs, sems):
        for cp in copies(ins, outs, sems):
            cp.wait()

    return _Side([small], [jax.ShapeDtypeStruct(small.shape, small.dtype)], [pltpu.SemaphoreType.DMA((7,))] * 2,
                 start, finish)


def _small_share_side(small):
    return _share_side([], small)


def _share_side(halves, small=None):
    n = len(halves)
    arrs = list(halves) + ([small] if small is not None else [])

    def copies(ins, outs, sems, mine):
        x, y, c, _ = _place()
        me = 4 * x + 2 * y + c
        cps = []
        for a in range(n):
            hr = halves[a].shape[0] // 2
            rows = pl.ds((c if mine else 1 - c) * hr, hr)
            cps.append(pltpu.make_async_remote_copy(
                src_ref=ins[a].at[rows], dst_ref=outs[a].at[rows],
                send_sem=sems[0].at[a], recv_sem=sems[1].at[a], device_id=(x, y, 1 - c), device_id_type=MESH))
        if small is not None:
            for r, (fx, fy, fc) in enumerate(_relations(), start=1):
                px, py, pc = x ^ fx, y ^ fy, c ^ fc
                seg = me if mine else 4 * px + 2 * py + pc
                cps.append(pltpu.make_async_remote_copy(
                    src_ref=ins[n].at[seg], dst_ref=outs[n].at[seg],
                    send_sem=sems[-2].at[r - 1], recv_sem=sems[-1].at[r - 1], device_id=(px, py, pc),
                    device_id_type=MESH))
        return cps

    def start(ins, outs, sems):
        for cp in copies(ins, outs, sems, True):
            cp.start()

    def finish(ins, outs, sems):
        for cp in copies(ins, outs, sems, False):
            cp.wait_recv()
        for cp in copies(ins, outs, sems, True):
            cp.wait_send()

    sems = ([pltpu.SemaphoreType.DMA((n,))] * 2 if n else []) + (
        [pltpu.SemaphoreType.DMA((7,))] * 2 if small is not None else [])
    return _Side(arrs, [jax.ShapeDtypeStruct(h.shape, h.dtype) for h in arrs], sems, start, finish,
                 aliases={i: i for i in range(len(arrs))})


GATHER_PIECES = [(0, 0), (0, 1), (1, 0), (2, 0), (1, 1), (2, 1), (3, 0), (3, 1)]


def _mm_gathering(a, shard, order, *, name, tm=1024):
    s, k = a.shape
    nc = shard.shape[1]
    tm = _row_tile(s, tm)
    tn = nc // 2
    hr = k // 2
    qr = hr // 2
    blocks = jnp.stack([order[src] * 2 + h for src, h in GATHER_PIECES]).astype(jnp.int32)

    def body(blocks_ref, a_ref, shard_ref, z_ref, full_ref, wbuf, send_ici, recv_ici, send_relay,
             recv_relay, send_d2d, recv_d2d, local_sem, load_sem):
        piece, i = pl.program_id(0), pl.program_id(1)
        x, y, c, chips = _place()
        me = 2 * x + y
        nbrs = chips[:2]
        chip_of = [2 * px + py for px, py in chips]

        def quarter(q):
            return pl.ds(c * hr + q * qr, qr)

        def sibling_quarter(q):
            return pl.ds((1 - c) * hr + q * qr, qr)

        def whole(half):
            return pl.ds(half * hr, hr)

        def cols(h):
            return pl.ds(h * tn, tn)

        def direct(j, src_chip, h):
            return pltpu.make_async_remote_copy(
                src_ref=shard_ref.at[whole(c), cols(h)], dst_ref=full_ref.at[src_chip, whole(c), cols(h)],
                send_sem=send_ici.at[j, h], recv_sem=recv_ici.at[j, h], device_id=(*nbrs[j], c), device_id_type=MESH)

        def relay(j, src_chip, h):
            blk = full_ref.at[src_chip, quarter(j), cols(h)]
            return pltpu.make_async_remote_copy(
                src_ref=blk, dst_ref=blk, send_sem=send_relay.at[j, h], recv_sem=recv_relay.at[j, h],
                device_id=(*nbrs[1 - j], c), device_id_type=MESH)

        def d2d(j, chip, rows, h):
            blk = full_ref.at[chip, rows, cols(h)]
            return pltpu.make_async_remote_copy(
                src_ref=blk, dst_ref=blk, send_sem=send_d2d.at[j, h], recv_sem=recv_d2d.at[j, h],
                device_id=(x, y, 1 - c), device_id_type=MESH)

        def load(p):
            src, h = GATHER_PIECES[p]
            where = shard_ref if src == 0 else full_ref.at[chip_of[src - 1]]
            return pltpu.make_async_copy(where.at[:, cols(h)], wbuf.at[p % 2], load_sem.at[p % 2])

        local = pltpu.make_async_copy(shard_ref, full_ref.at[me], local_sem)

        def arrived(p):
            src, h = GATHER_PIECES[p]
            if src in (1, 2):
                j = src - 1
                direct(j, chip_of[j], h).wait_recv()
                relay(j, chip_of[j], h).start()
                d2d(j, chip_of[j], whole(c), h).start()
            elif src == 3:
                for j in range(2):
                    relay(1 - j, chip_of[2], h).wait_recv()
                    d2d(2 + j, chip_of[2], quarter(1 - j), h).start()

        def fetch(p):
            src, h = GATHER_PIECES[p]
            if src in (1, 2):
                d2d(src - 1, chip_of[src - 1], whole(1 - c), h).wait_recv()
            elif src == 3:
                for j in range(2):
                    d2d(2 + j, chip_of[2], sibling_quarter(1 - j), h).wait_recv()
            load(p).start()

        n_i = s // tm
        for p in range(len(GATHER_PIECES)):
            @pl.when(jnp.logical_and(piece == p, i == 0))
            def _(p=p):
                if p == 0:
                    local.start()
                    for hh in range(2):
                        for j in range(2):
                            direct(j, me, hh).start()
                    load(0).start()
                load(p).wait()

        z_ref[...] = jnp.dot(a_ref[...], wbuf[piece % 2], preferred_element_type=F32).astype(z_ref.dtype)

        for p in range(len(GATHER_PIECES) - 1):
            @pl.when(jnp.logical_and(piece == p, i == min(1, n_i - 1)))
            def _(p=p):
                arrived(p + 1)

            @pl.when(jnp.logical_and(piece == p, i == min(2, n_i - 1)))
            def _(p=p):
                fetch(p + 1)

        last = jnp.logical_and(piece == len(GATHER_PIECES) - 1, i == n_i - 1)

        @pl.when(last)
        def _():
            for h in range(2):
                for j in range(2):
                    direct(j, me, h).wait_send()
                    relay(j, chip_of[j], h).wait_send()
                    d2d(j, chip_of[j], whole(c), h).wait_send()
                    d2d(2 + j, chip_of[2], quarter(1 - j), h).wait_send()
            local.wait()

    return pl.pallas_call(
        body,
        grid_spec=pltpu.PrefetchScalarGridSpec(
            num_scalar_prefetch=1, grid=(len(GATHER_PIECES), s // tm),
            in_specs=[pl.BlockSpec((tm, k), lambda p, i, blocks: (i, 0)), HBM],
            out_specs=[pl.BlockSpec((tm, tn), lambda p, i, blocks: (i, blocks[p])), HBM],
            scratch_shapes=[pltpu.VMEM((2, k, tn), BF16)] + [pltpu.SemaphoreType.DMA((2, 2))] * 4
            + [pltpu.SemaphoreType.DMA((4, 2))] * 2 + [pltpu.SemaphoreType.DMA, pltpu.SemaphoreType.DMA((2,))]),
        out_shape=[jax.ShapeDtypeStruct((s, N_CHIPS * nc), BF16), jax.ShapeDtypeStruct((N_CHIPS, k, nc), BF16)],
        name=name, compiler_params=_cparams(),
    )(blocks, a, shard)


def _mm_tn_exchanging(a, b, *, name, shards, tk=2048, side=None):
    s, m = a.shape
    nc = b.shape[1] // shards
    tk = _row_tile(s, tk)
    nk = s // tk
    hm = m // 2

    def body(a_ref, b_ref, part_ref, sib_ref, acc, keep_sem, send_sem, recv_sem):
        j, kk = pl.program_id(0), pl.program_id(1)
        x, y, c, _ = _place()

        def keep(jj, slot):
            mine = pl.ds(c * hm, hm)
            return pltpu.make_async_copy(acc.at[slot, mine], part_ref.at[jj], keep_sem.at[slot])

        def give(jj, slot):
            return pltpu.make_async_remote_copy(
                src_ref=acc.at[slot, pl.ds((1 - c) * hm, hm)], dst_ref=sib_ref.at[jj],
                send_sem=send_sem.at[slot], recv_sem=recv_sem.at[jj], device_id=(x, y, 1 - c), device_id_type=MESH)

        part = lax.dot_general(a_ref[...], b_ref[...], TN, preferred_element_type=F32)
        for slot in range(2):
            @pl.when(j % 2 == slot)
            def _(slot=slot):
                @pl.when(jnp.logical_and(kk == 0, j >= 2))
                def _():
                    keep(j - 2, slot).wait()
                    give(j - 2, slot).wait_send()

                @pl.when(kk == 0)
                def _():
                    acc[slot] = part

                @pl.when(kk > 0)
                def _():
                    acc[slot] += part

                @pl.when(kk == nk - 1)
                def _():
                    keep(j, slot).start()
                    give(j, slot).start()

        @pl.when(jnp.logical_and(j == shards - 1, kk == nk - 1))
        def _():
            for jj in range(shards - 2, shards):
                keep(jj, jj % 2).wait()
                give(jj, jj % 2).wait_send()
            for jj in range(shards):
                give(jj, jj % 2).wait_recv()

    assert shards >= 2
    return _call(
        body, grid=(shards, nk),
        in_specs=[pl.BlockSpec((tk, m), lambda j, kk: (kk, 0)), pl.BlockSpec((tk, nc), lambda j, kk: (kk, j))],
        out_specs=[HBM, HBM],
        out_shape=[jax.ShapeDtypeStruct((shards, hm, nc), F32), jax.ShapeDtypeStruct((shards, hm, nc), F32)],
        scratch=[pltpu.VMEM((2, m, nc), F32), pltpu.SemaphoreType.DMA((2,)), pltpu.SemaphoreType.DMA((2,)),
                 pltpu.SemaphoreType.DMA((shards,))],
        args=(a, b), name=name, side=side)


def _col_tile(cols):
    return cols if cols <= 2048 else 512


def _add_sibling(grad, recv, core, *, name):
    k, r, c = grad.shape
    hr = r // 2
    tr = min(hr, 256)
    tc = _col_tile(c)
    nrb = hr // tr

    def body(core_ref, g_ref, r_ref, o_ref):
        o_ref[...] = (g_ref[...] + r_ref[...]).astype(BF16)

    return pl.pallas_call(
        body,
        grid_spec=pltpu.PrefetchScalarGridSpec(
            num_scalar_prefetch=1, grid=(k, nrb, c // tc),
            in_specs=[pl.BlockSpec((None, tr, tc), lambda kk, i, j, core: (kk, core[0] * nrb + i, j)),
                      pl.BlockSpec((None, tr, tc), lambda kk, i, j, core: (kk, i, j))],
            out_specs=pl.BlockSpec((None, tr, tc), lambda kk, i, j, core: (kk, i, j))),
        out_shape=jax.ShapeDtypeStruct((k, hr, c), BF16), name=name, compiler_params=_cparams(),
    )(core, grad, recv)


def _sum_chips(grad, from_sibling, recv, place, *, name):
    _, hr, c = from_sibling.shape
    tr = min(hr, 256)
    tc = _col_tile(c)
    nrb = hr // tr

    def body(place_ref, g_ref, s_ref, r0_ref, r1_ref, r2_ref, o_ref):
        own = g_ref[...] + s_ref[...]
        o_ref[...] = ((own + r0_ref[...].astype(F32)) + r1_ref[...].astype(F32)) + r2_ref[...].astype(F32)

    def rspec(j):
        return pl.BlockSpec((None, tr, tc), lambda i, jj, place: (j, i, jj))

    return pl.pallas_call(
        body,
        grid_spec=pltpu.PrefetchScalarGridSpec(
            num_scalar_prefetch=1, grid=(nrb, c // tc),
            in_specs=[pl.BlockSpec((None, tr, tc), lambda i, jj, place: (place[0], place[1] * nrb + i, jj)),
                      pl.BlockSpec((None, tr, tc), lambda i, jj, place: (place[0], i, jj)),
                      rspec(0), rspec(1), rspec(2)],
            out_specs=pl.BlockSpec((tr, tc), lambda i, jj, place: (place[1] * nrb + i, jj))),
        out_shape=jax.ShapeDtypeStruct((2 * hr, c), F32), name=name, compiler_params=_cparams(),
    )(place, grad, from_sibling, recv, recv, recv)


def _add_halves(mine, theirs, *, name, side=None):
    k, hr, c = mine.shape
    tr = min(hr, 256)
    tc = _col_tile(c)

    def body(a_ref, b_ref, o_ref):
        o_ref[...] = (a_ref[...] + b_ref[...]).astype(BF16)

    spec = pl.BlockSpec((None, tr, tc), lambda kk, i, j: (kk, i, j))
    (out,), side_outs = _call(body, grid=(k, hr // tr, c // tc), in_specs=[spec, spec], out_specs=[spec],
                              out_shape=[jax.ShapeDtypeStruct((k, hr, c), BF16)], args=(mine, theirs), name=name,
                              side=side)
    return out, side_outs


def _sum_halves(mine, theirs, recv, place, *, name):
    _, hr, c = mine.shape
    tr = min(hr, 256)
    tc = _col_tile(c)
    nrb = hr // tr

    def body(place_ref, a_ref, b_ref, r0_ref, r1_ref, r2_ref, o_ref):
        own = a_ref[...] + b_ref[...]
        o_ref[...] = ((own + r0_ref[...].astype(F32)) + r1_ref[...].astype(F32)) + r2_ref[...].astype(F32)

    def rspec(j):
        return pl.BlockSpec((None, tr, tc), lambda i, jj, place: (j, i, jj))

    own_spec = pl.BlockSpec((None, tr, tc), lambda i, jj, place: (place[0], i, jj))
    return pl.pallas_call(
        body,
        grid_spec=pltpu.PrefetchScalarGridSpec(
            num_scalar_prefetch=1, grid=(nrb, c // tc),
            in_specs=[own_spec, own_spec, rspec(0), rspec(1), rspec(2)],
            out_specs=pl.BlockSpec((tr, tc), lambda i, jj, place: (place[1] * nrb + i, jj))),
        out_shape=jax.ShapeDtypeStruct((2 * hr, c), F32), name=name, compiler_params=_cparams(),
    )(place, mine, theirs, recv, recv, recv)


def _sum_small(small, recv, place):
    _, sr, _ = small.shape

    def body(place_ref, own_ref, r_ref, o_ref):
        acc = own_ref[...]
        for r in range(1, 8):
            acc = acc + r_ref[r]
        o_ref[...] = acc

    return pl.pallas_call(
        body,
        grid_spec=pltpu.PrefetchScalarGridSpec(
            num_scalar_prefetch=1, grid=(1,),
            in_specs=[pl.BlockSpec((None, sr, 128), lambda i, place: (place[2], 0, 0)),
                      pl.BlockSpec((8, sr, 128), lambda i, place: (0, 0, 0))],
            out_specs=pl.BlockSpec((None, sr, 128), lambda i, place: (place[2], 0, 0))),
        out_shape=jax.ShapeDtypeStruct(small.shape, F32), name="sum_small", compiler_params=_cparams(),
    )(place, small, recv)


def _spread_side(vec):
    def copies(ins, outs, sems):
        x, y, c, _ = _place()
        return [pltpu.make_async_remote_copy(
            src_ref=ins[0], dst_ref=outs[0].at[r], send_sem=sems[0].at[r - 1], recv_sem=sems[1].at[r - 1],
            device_id=(x ^ fx, y ^ fy, c ^ fc), device_id_type=MESH)
            for r, (fx, fy, fc) in enumerate(_relations(), start=1)]

    def start(ins, outs, sems):
        for cp in copies(ins, outs, sems):
            cp.start()

    def finish(ins, outs, sems):
        for cp in copies(ins, outs, sems):
            cp.wait()

    return _Side([vec], [jax.ShapeDtypeStruct((8,) + vec.shape, vec.dtype)], [pltpu.SemaphoreType.DMA((7,))] * 2,
                 start, finish)


def _sum_in_device_order(own, spread, place):
    def body(place_ref, own_ref, r_ref, o_ref):
        me = place_ref[2]
        acc = jnp.zeros_like(own_ref[...])
        for d in range(8):
            slot = jnp.where(me == d, 1, me ^ d)
            acc = acc + jnp.where(me == d, own_ref[...], r_ref[slot])
        o_ref[...] = acc

    return pl.pallas_call(
        body,
        grid_spec=pltpu.PrefetchScalarGridSpec(
            num_scalar_prefetch=1, grid=(1,),
            in_specs=[pl.BlockSpec(own.shape, lambda i, place: (0, 0)),
                      pl.BlockSpec(spread.shape, lambda i, place: (0, 0, 0))],
            out_specs=pl.BlockSpec(own.shape, lambda i, place: (0, 0))),
        out_shape=jax.ShapeDtypeStruct(own.shape, F32), name="sum_in_device_order", compiler_params=_cparams(),
    )(place, own, spread)


def _adamw(w, g, m, v, *, name):
    r, c = w.shape
    tr = 256 if r % 256 == 0 else r
    tc = _col_tile(c)
    bc1 = 1.0 - ADAM_B1 ** ADAM_STEP
    bc2 = 1.0 - ADAM_B2 ** ADAM_STEP

    def body(w_ref, g_ref, m_ref, v_ref, d_ref, nm_ref, nv_ref, gout_ref):
        gv = g_ref[...]
        nm = ADAM_B1 * m_ref[...] + (1.0 - ADAM_B1) * gv
        nv = ADAM_B2 * v_ref[...] + (1.0 - ADAM_B2) * (gv * gv)
        d_ref[...] = -ADAM_LR * ((nm / bc1) / (jnp.sqrt(nv / bc2) + ADAM_EPS) + ADAM_WD * w_ref[...])
        nm_ref[...] = nm
        nv_ref[...] = nv
        gout_ref[...] = gv

    spec = pl.BlockSpec((tr, tc), lambda i, j: (i, j))
    outs, _ = _call(body, grid=(r // tr, c // tc), in_specs=[spec] * 4, out_specs=[spec] * 4,
                    out_shape=[jax.ShapeDtypeStruct((r, c), F32)] * 4, args=(w, g, m, v), name=name)
    return outs


SMALL_ORDER = ["a_ws", "a_bs", "a_norm_g", "a_ln_g", "a_ln_b", "kv_norm_g", "b_kv", "b_norm_g", "b_bq",
               "b_sinks", "final_norm_g"]
SHARDED_SMALL = {"a_norm_g", "a_ln_g", "a_ln_b"}
PACK_TILE = 8 * 128


def _rows128(a):
    flat = a.reshape(-1)
    return jnp.pad(flat, (0, (-flat.shape[0]) % PACK_TILE)).reshape(-1, 128)


def _pack_rows(parts, multiple):
    rows = [_rows128(p) for p in parts]
    total = sum(r.shape[0] for r in rows)
    pad = (-total) % multiple
    if pad:
        rows.append(jnp.zeros((pad, 128), rows[0].dtype))
    return jnp.concatenate(rows, axis=0)


def _unpack_rows(packed, shapes):
    out, row = [], 0
    for shp in shapes:
        size = math.prod(shp)
        nrow = -(-size // PACK_TILE) * 8
        out.append(packed[row:row + nrow].reshape(-1)[:size].reshape(shp))
        row += nrow
    return out


WEIGHTS = ["a_norm_g", "a_w_in", "a_ln_g", "a_ln_b", "a_ws", "a_bs", "a_w_out", "kv_norm_g", "w_kv", "b_kv",
           "b_norm_g", "b_w_in", "b_bq", "b_sinks", "b_w_out", "final_norm_g"]
BIG = ["a_w_in", "a_w_out", "w_kv", "b_w_in", "b_w_out"]


class _Reduction:
    def __init__(self, names, partials, core, place, small=None):
        self.names, self.partials, self.core, self.place, self.small = names, partials, core, place, small

    def exchange_side(self):
        return _exchange_side(self.partials)

    def took_exchange(self, from_sibling):
        self.from_sibling = from_sibling
        self.chip_sums = [_add_sibling(g, r, self.core, name="add_sibling_" + n)
                          for g, r, n in zip(self.partials, from_sibling, self.names)]

    def scatter_side(self):
        return _scatter_side(self.chip_sums, self.small)

    def took_scatter(self, arrived):
        big = arrived[:len(self.names)]
        self.halves = [_sum_chips(g, fs, r, self.place, name="sum_chips_" + n)
                       for g, fs, r, n in zip(self.partials, self.from_sibling, big, self.names)]
        self.small_mine = _sum_small(self.small, arrived[-1], self.place) if self.small is not None else None

    def share_side(self):
        return _share_side(self.halves, self.small_mine)

    def took_share(self, shared):
        self.grads = dict(zip(self.names, shared[:len(self.names)]))
        self.small_full = shared[-1] if self.small is not None else None


def _step(x, loss_target, p, m, v):
    xi, yi, ci = lax.axis_index("x"), lax.axis_index("y"), lax.axis_index("c")
    chip = 2 * xi + yi
    device = 4 * xi + 2 * yi + ci
    core = jnp.reshape(ci, (1,)).astype(jnp.int32)
    place = jnp.stack([chip, ci, device]).astype(jnp.int32)
    x, tgt = x[0], loss_target[0]
    s = x.shape[0]
    cos, sin = _rope_tables(s)

    shard2d = {n: p[n].reshape(p[n].shape[-2:]) for n in BIG}
    shard_bf = {n: shard2d[n].astype(BF16) for n in BIG}
    ws = p["a_ws"][0]
    ws_t = jnp.swapaxes(ws, 1, 2)
    bs_t = p["a_bs"][0].T
    kv_norm_g, b_kv = p["kv_norm_g"].reshape(1, -1), p["b_kv"].reshape(1, -1)
    final_norm_g = p["final_norm_g"].reshape(1, -1)

    vec_shapes = [p[n].shape for n in ("a_norm_g", "a_ln_g", "a_ln_b")]
    vec_pack = _pack_rows([p["a_norm_g"], p["a_ln_g"], p["a_ln_b"]], 16)
    (vec_all,) = _comm_call(_gather_side([vec_pack]), "gather_vectors")
    vecs = [_unpack_rows(vec_all[k], vec_shapes) for k in range(N_CHIPS)]
    a_norm_g, a_ln_g, a_ln_b = (jnp.concatenate([vk[t] for vk in vecs], axis=-1) for t in range(3))

    (n_a,) = _rms_fwd(x, [a_norm_g], name="rms_a")
    order = jnp.stack([chip, 2 * (1 - xi) + yi, 2 * xi + (1 - yi), 2 * (1 - xi) + (1 - yi)]).astype(jnp.int32)
    z, a_w_in = _mm_gathering(n_a, shard_bf["a_w_in"], order, name="mm_a_in")
    y, (a_w_out,) = _gate_fwd(z, a_ln_g, a_ln_b, ws, bs_t, side=_gather_side([shard_bf["a_w_out"]]))
    a_w_out = a_w_out.reshape(A_WIDTH, D_MODEL)
    (h1, n_kv, n_b), (w_kv, b_w_in) = _mm_residual_norms(
        y, a_w_out, x, [kv_norm_g, p["b_norm_g"]], name="mm_a_out",
        side=_gather_side([shard_bf["w_kv"], shard_bf["b_w_in"]]))
    w_kv = w_kv.reshape(D_MODEL, 2 * KV_WIDTH)
    kr, vv = _kv_rope(n_kv, w_kv, b_kv, cos, sin)
    zb = _mm_nn(n_b, b_w_in, name="mm_b_in", tn=512, tm=2048, out_dtype=BF16)
    yb, (b_w_out,) = _attn_fwd(zb, kr, vv, cos, sin, p["b_bq"], p["b_sinks"], side=_gather_side([shard_bf["b_w_out"]]))
    b_w_out = b_w_out.reshape(B_WIDTH, D_MODEL)
    loss_blk, dh2, dh2b, d_final_g = _mm_residual_loss(yb, b_w_out, h1, tgt, final_norm_g, name="mm_b_out")

    d_b_w_out = _mm_tn(yb, dh2b, name="mm_d_b_w_out", tm=B_WIDTH, tn=512, tk=4096)
    red_bo = _Reduction(["b_w_out"], [d_b_w_out.reshape(N_CHIPS, B_WIDTH // N_CHIPS, D_MODEL)], core, place)
    dyb, got = _mm_nt(dh2b, b_w_out, name="mm_dyb", tm=1024, out_dtype=BF16, side=red_bo.exchange_side())
    red_bo.took_exchange(got)
    (dzb, dk_rot, dv, d_bq, d_sinks), got = _attn_bwd(zb, dyb, kr, vv, cos, sin, p["b_bq"], p["b_sinks"],
                                                      side=red_bo.scatter_side())
    red_bo.took_scatter(got)
    dkv, d_b_kv = _kv_rope_bwd(dk_rot, dv, cos, sin)
    d_b_w_in = _mm_tn(n_b, dzb, name="mm_d_b_w_in", tm=D_MODEL, tn=512, tk=4096, shards=N_CHIPS)
    d_w_kv, got = _mm_tn(n_kv, dkv, name="mm_d_w_kv", tm=D_MODEL, tn=2 * KV_WIDTH, tk=4096,
                         side=red_bo.share_side())
    red_bo.took_share(got)
    red_bi = _Reduction(["b_w_in", "w_kv"], [d_b_w_in, d_w_kv.reshape(N_CHIPS, D_MODEL // N_CHIPS, 2 * KV_WIDTH)],
                        core, place)
    (dh1, dh1b, d_kv_g, d_b_g), got = _mm_nt_rms_bwd(
        [(dkv, w_kv, kv_norm_g), (dzb, b_w_in, p["b_norm_g"])], h1, dh2, name="mm_dn_b", tm=512,
        side=red_bi.exchange_side())
    red_bi.took_exchange(got)

    d_a_w_out = _mm_tn(y, dh1b, name="mm_d_a_w_out", tm=1024, tn=512, tk=4096)
    red_ao = _Reduction(["a_w_out"], [d_a_w_out.reshape(N_CHIPS, A_WIDTH // N_CHIPS, D_MODEL)], core, place)
    dy, got = _mm_nt(dh1b, a_w_out, name="mm_dy", tn=1024, tm=1024, out_dtype=BF16, side=red_ao.exchange_side())
    red_ao.took_exchange(got)
    sides = [red_bi.scatter_side(), red_ao.scatter_side()]
    (dz, d_ln_g, d_ln_b, d_ws, d_bs_t), got = _gate_bwd(z, dy, a_ln_g, a_ln_b, ws, ws_t, bs_t, side=_join(sides))
    got = _split(got, sides)
    red_bi.took_scatter(got[0])
    red_ao.took_scatter(got[1])
    small = {
        "a_ws": d_ws, "a_bs": d_bs_t.T, "a_ln_g": d_ln_g, "a_ln_b": d_ln_b,
        "kv_norm_g": d_kv_g, "b_kv": d_b_kv, "b_norm_g": d_b_g, "b_bq": d_bq,
        "b_sinks": d_sinks[0:1, :N_Q_HEADS], "final_norm_g": d_final_g,
    }
    packed = [n for n in SMALL_ORDER if n != "a_norm_g"]
    small_shapes = [small[n].shape for n in packed] + [(1, 1)]
    small_pack = _pack_rows([small[n] for n in packed] + [loss_blk[0:1, 0:1]], 64)
    seg = small_pack.shape[0] // 8
    small_pack = small_pack.reshape(8, seg, 128)
    sides = [red_bi.share_side(), red_ao.share_side(), _small_scatter_side(small_pack)]
    (d_a_w_in, from_sibling), got = _mm_tn_exchanging(n_a, dz, name="mm_d_a_w_in", shards=N_CHIPS, side=_join(sides))
    got = _split(got, sides)
    red_bi.took_share(got[0])
    red_ao.took_share(got[1])
    small_mine = _sum_small(small_pack, got[2][0], place)

    chip_sum, (small_all,) = _add_halves(d_a_w_in, from_sibling, name="add_sibling_a_w_in",
                                         side=_small_share_side(small_mine))
    (dx, _, d_a_g), (arrived,) = _mm_nt_rms_bwd([(dz, a_w_in, a_norm_g)], x, dh1, name="mm_dn_a", tm=256,
                                                side=_scatter_side([chip_sum]))
    half_ai = _sum_halves(d_a_w_in, from_sibling, arrived, place, name="sum_chips_a_w_in")
    d_a_g = _rows128(d_a_g)
    sides = [_share_side([half_ai]), _spread_side(d_a_g)]
    got = _split(_comm_call(_join(sides), "share_last"), sides)
    grad_ai = got[0][0]
    small_full = dict(zip(packed + ["loss"], _unpack_rows(small_all.reshape(8 * seg, 128), small_shapes)))
    small_full["a_norm_g"] = _sum_in_device_order(d_a_g, got[1][0], place).reshape(1, -1)
    loss = small_full["loss"].reshape(())

    grad_big = {**red_bo.grads, **red_bi.grads, **red_ao.grads, "a_w_in": grad_ai}
    grads = {}
    for n in SMALL_ORDER:
        gfull = small_full[n]
        if n in SHARDED_SMALL:
            width = p[n].shape[-1]
            gfull = lax.dynamic_slice_in_dim(gfull, chip * width, width, axis=-1)
        grads[n] = gfull.reshape(p[n].shape)

    delta, new_m, new_v = {}, {}, {}
    for n in BIG:
        d, nm, nv, g = _adamw(shard2d[n], grad_big[n], m[n].reshape(shard2d[n].shape),
                              v[n].reshape(shard2d[n].shape), name="adamw_" + n)
        delta[n], new_m[n], new_v[n] = d.reshape(p[n].shape), nm.reshape(p[n].shape), nv.reshape(p[n].shape)
        grads[n] = g.reshape(p[n].shape)
    shapes = [p[n].shape for n in SMALL_ORDER]
    packs = [_pack_rows([src[n] for n in SMALL_ORDER], 8) for src in (p, grads, m, v)]
    outs = _adamw(*packs, name="adamw_small")[:3]
    for res, packed in zip((delta, new_m, new_v), outs):
        for n, val in zip(SMALL_ORDER, _unpack_rows(packed, shapes)):
            res[n] = val

    return (loss, dx[None], *[grads[n] for n in WEIGHTS], *[delta[n] for n in WEIGHTS],
            *[new_m[n] for n in WEIGHTS], *[new_v[n] for n in WEIGHTS])


def kernel(x, a_norm_g, a_w_in, a_ln_g, a_ln_b, a_ws, a_bs, a_w_out, kv_norm_g, w_kv, b_kv, b_norm_g, b_w_in, b_bq, b_sinks, b_w_out, final_norm_g, loss_target, m_a_norm_g, m_a_w_in, m_a_ln_g, m_a_ln_b, m_a_ws, m_a_bs, m_a_w_out, m_kv_norm_g, m_w_kv, m_b_kv, m_b_norm_g, m_b_w_in, m_b_bq, m_b_sinks, m_b_w_out, m_final_norm_g, v_a_norm_g, v_a_w_in, v_a_ln_g, v_a_ln_b, v_a_ws, v_a_bs, v_a_w_out, v_kv_norm_g, v_w_kv, v_b_kv, v_b_norm_g, v_b_w_in, v_b_bq, v_b_sinks, v_b_w_out, v_final_norm_g):
    p = dict(a_norm_g=a_norm_g, a_w_in=a_w_in, a_ln_g=a_ln_g, a_ln_b=a_ln_b, a_ws=a_ws, a_bs=a_bs, a_w_out=a_w_out,
             kv_norm_g=kv_norm_g, w_kv=w_kv, b_kv=b_kv, b_norm_g=b_norm_g, b_w_in=b_w_in, b_bq=b_bq, b_sinks=b_sinks,
             b_w_out=b_w_out, final_norm_g=final_norm_g)
    m = dict(a_norm_g=m_a_norm_g, a_w_in=m_a_w_in, a_ln_g=m_a_ln_g, a_ln_b=m_a_ln_b, a_ws=m_a_ws, a_bs=m_a_bs,
             a_w_out=m_a_w_out, kv_norm_g=m_kv_norm_g, w_kv=m_w_kv, b_kv=m_b_kv, b_norm_g=m_b_norm_g, b_w_in=m_b_w_in,
             b_bq=m_b_bq, b_sinks=m_b_sinks, b_w_out=m_b_w_out, final_norm_g=m_final_norm_g)
    v = dict(a_norm_g=v_a_norm_g, a_w_in=v_a_w_in, a_ln_g=v_a_ln_g, a_ln_b=v_a_ln_b, a_ws=v_a_ws, a_bs=v_a_bs,
             a_w_out=v_a_w_out, kv_norm_g=v_kv_norm_g, w_kv=v_w_kv, b_kv=v_b_kv, b_norm_g=v_b_norm_g, b_w_in=v_b_w_in,
             b_bq=v_b_bq, b_sinks=v_b_sinks, b_w_out=v_b_w_out, final_norm_g=v_final_norm_g)
    return _step(x, loss_target, p, m, v)
```

```python
import functools
import math

import jax
import jax.numpy as jnp
from jax import lax
from jax.experimental import pallas as pl
from jax.experimental.pallas import tpu as pltpu

F32 = jnp.float32
BF16 = jnp.bfloat16

D_MODEL = 1024
CHUNK = 128
A_WIDTH = 2048
A_GROUPS = 16
HEAD_DIM = 64
N_Q_HEADS = 16
N_KV_HEADS = 2
Q_PER_KV = 8
B_WIDTH = 1024
KV_WIDTH = 128
ROPE_THETA = 10000.0
EPS = 1e-5
N_CHIPS = 4

ADAM_LR = 0.001
ADAM_B1 = 0.9
ADAM_B2 = 0.999
ADAM_EPS = 1e-08
ADAM_WD = 0.01
ADAM_STEP = 10

VMEM_LIMIT = 48 * 1024 * 1024
MESH = pl.DeviceIdType.MESH
NEG_BIG = -1e30
HBM = pl.BlockSpec(memory_space=pl.ANY)

NN = (((1,), (0,)), ((), ()))
NT = (((1,), (1,)), ((), ()))
TN = (((0,), (0,)), ((), ()))


def _cparams(**kw):
    return pltpu.CompilerParams(vmem_limit_bytes=VMEM_LIMIT, **kw)


class _Side:
    def __init__(self, ins, out_shapes, sems, start, finish, aliases=None, passing=None):
        self.ins, self.out_shapes, self.sems = list(ins), list(out_shapes), list(sems)
        self.start, self.finish = start, finish
        self.passing = passing or (lambda ins, outs, sems: None)
        self.aliases = dict(aliases or {})


def _join(sides):
    sides = [s for s in sides if s is not None]
    if not sides:
        return None
    offs, i, o, m = [], 0, 0, 0
    for s in sides:
        offs.append((i, o, m))
        i, o, m = i + len(s.ins), o + len(s.out_shapes), m + len(s.sems)

    def run(which):
        def go(ins, outs, sems):
            for s, (a, b, c) in zip(sides, offs):
                getattr(s, which)(ins[a:a + len(s.ins)], outs[b:b + len(s.out_shapes)], sems[c:c + len(s.sems)])
        return go

    aliases = {}
    for s, (a, b, _) in zip(sides, offs):
        aliases.update({a + k: b + v for k, v in s.aliases.items()})
    return _Side([x for s in sides for x in s.ins], [x for s in sides for x in s.out_shapes],
                 [x for s in sides for x in s.sems], run("start"), run("finish"), aliases, run("passing"))


def _split(side_outs, sides):
    out, pos = [], 0
    for s in sides:
        out.append(list(side_outs[pos:pos + len(s.out_shapes)]))
        pos += len(s.out_shapes)
    return out


def _call(body, *, grid, in_specs, out_specs, out_shape, args, name, scratch=(), side=None):
    in_specs, out_specs, out_shape, scratch = list(in_specs), list(out_specs), list(out_shape), list(scratch)
    if side is None:
        res = pl.pallas_call(body, grid=grid, in_specs=in_specs, out_specs=out_specs, out_shape=out_shape,
                             scratch_shapes=scratch, name=name, compiler_params=_cparams())(*args)
        return list(res), []
    n_in, n_out, n_sc = len(in_specs), len(out_specs), len(scratch)
    s_in, s_out = len(side.ins), len(side.out_shapes)

    def wrapped(*refs):
        ins, refs = refs[:n_in], refs[n_in:]
        side_ins, refs = refs[:s_in], refs[s_in:]
        outs, refs = refs[:n_out], refs[n_out:]
        side_outs, refs = refs[:s_out], refs[s_out:]
        scr, side_sems = refs[:n_sc], refs[n_sc:]
        step = 0
        for a, g in enumerate(grid):
            step = step * g + pl.program_id(a)
        steps = math.prod(grid)

        @pl.when(step == 0)
        def _():
            side.start(side_ins, side_outs, side_sems)

        body(*ins, *outs, *scr)

        @pl.when(step == (3 * (steps - 1)) // 4)
        def _():
            side.passing(side_ins, side_outs, side_sems)

        @pl.when(step == steps - 1)
        def _():
            side.finish(side_ins, side_outs, side_sems)

    res = pl.pallas_call(
        wrapped, grid=grid, in_specs=in_specs + [HBM] * s_in, out_specs=out_specs + [HBM] * s_out,
        out_shape=out_shape + side.out_shapes, scratch_shapes=scratch + side.sems,
        input_output_aliases={n_in + k: n_out + v for k, v in side.aliases.items()},
        name=name, compiler_params=_cparams(),
    )(*args, *side.ins)
    return list(res[:n_out]), list(res[n_out:])


def _comm_call(side, name):
    s_in, s_out = len(side.ins), len(side.out_shapes)

    def body(*refs):
        ins, outs, sems = refs[:s_in], refs[s_in:s_in + s_out], refs[s_in + s_out:]
        side.start(ins, outs, sems)
        side.passing(ins, outs, sems)
        side.finish(ins, outs, sems)

    return list(pl.pallas_call(
        body, in_specs=[HBM] * s_in, out_specs=[HBM] * s_out, out_shape=side.out_shapes, scratch_shapes=side.sems,
        input_output_aliases=side.aliases, name=name,
    )(*side.ins))


def _matmul(a, b, *, dims, grid, a_spec, b_spec, o_spec, out_shape, name, acc_axis=None,
            residual=None, r_spec=None, side=None):
    has_res = residual is not None

    def body(*refs):
        if has_res:
            a_ref, b_ref, r_ref, o_ref = refs
        else:
            a_ref, b_ref, o_ref = refs
        part = lax.dot_general(a_ref[...], b_ref[...], dims, preferred_element_type=F32)
        if acc_axis is None:
            if has_res:
                part = part + r_ref[...]
            o_ref[...] = part.astype(o_ref.dtype)
        else:
            k = pl.program_id(acc_axis)

            @pl.when(k == 0)
            def _():
                o_ref[...] = part

            @pl.when(k > 0)
            def _():
                o_ref[...] += part

    in_specs = [a_spec, b_spec] + ([r_spec] if has_res else [])
    args = (a, b) + ((residual,) if has_res else ())
    (out,), side_outs = _call(body, grid=grid, in_specs=in_specs, out_specs=[o_spec], out_shape=[out_shape],
                              args=args, name=name, side=side)
    return (out, side_outs) if side is not None else out


def _row_tile(s, want):
    return min(s, want)


def _mm_nn(a, b, *, name, tn, out_dtype=F32, residual=None, tm=512, side=None):
    s, k = a.shape
    tm = _row_tile(s, tm)
    if b.ndim == 3:
        nsh, _, nc = b.shape
        npb = nc // tn
        n = nsh * nc
        b_spec = pl.BlockSpec((None, k, tn), lambda i, j: (j // npb, 0, j % npb))
    else:
        n = b.shape[1]
        b_spec = pl.BlockSpec((k, tn), lambda i, j: (0, j))
    return _matmul(
        a, b, dims=NN, grid=(s // tm, n // tn),
        a_spec=pl.BlockSpec((tm, k), lambda i, j: (i, 0)), b_spec=b_spec,
        o_spec=pl.BlockSpec((tm, tn), lambda i, j: (i, j)),
        out_shape=jax.ShapeDtypeStruct((s, n), out_dtype), name=name, side=side,
        residual=residual, r_spec=pl.BlockSpec((tm, tn), lambda i, j: (i, j)) if residual is not None else None)


def _mm_nt(a, b, *, name, tn=None, tm=512, out_dtype=F32, side=None):
    s, k = a.shape
    tm = _row_tile(s, tm)
    n = b.shape[0]
    tn = n if tn is None else tn
    return _matmul(
        a, b, dims=NT, grid=(s // tm, n // tn),
        a_spec=pl.BlockSpec((tm, k), lambda i, j: (i, 0)),
        b_spec=pl.BlockSpec((tn, k), lambda i, j: (j, 0)),
        o_spec=pl.BlockSpec((tm, tn), lambda i, j: (i, j)),
        out_shape=jax.ShapeDtypeStruct((s, n), out_dtype), name=name, side=side)


def _mm_tn(a, b, *, name, tm, tn, tk=2048, shards=None, side=None):
    s, m = a.shape
    n = b.shape[1]
    tk = _row_tile(s, tk)
    if shards is None:
        o_spec = pl.BlockSpec((tm, tn), lambda i, j, kk: (i, j))
        out_shape = jax.ShapeDtypeStruct((m, n), F32)
    else:
        assert tm == m
        nc = n // shards
        npb = nc // tn
        o_spec = pl.BlockSpec((None, m, tn), lambda i, j, kk: (j // npb, 0, j % npb))
        out_shape = jax.ShapeDtypeStruct((shards, m, nc), F32)
    return _matmul(
        a, b, dims=TN, grid=(m // tm, n // tn, s // tk), acc_axis=2,
        a_spec=pl.BlockSpec((tk, tm), lambda i, j, kk: (kk, i)),
        b_spec=pl.BlockSpec((tk, tn), lambda i, j, kk: (kk, j)),
        o_spec=o_spec, out_shape=out_shape, name=name, side=side)


def _rstd(x):
    return lax.rsqrt(jnp.mean(x * x, axis=-1, keepdims=True) + EPS)


def _rms_fwd(x, gains, *, name, tr=1024):
    s, d = x.shape
    tr = _row_tile(s, tr)
    ng = len(gains)

    def body(*refs):
        xv = refs[0][...]
        xh = xv * _rstd(xv)
        for t in range(ng):
            refs[1 + ng + t][...] = (xh * refs[1 + t][...]).astype(BF16)

    row = pl.BlockSpec((tr, d), lambda i: (i, 0))
    vec = pl.BlockSpec((1, d), lambda i: (0, 0))
    outs, _ = _call(body, grid=(s // tr,), in_specs=[row] + [vec] * ng, out_specs=[row] * ng,
                    out_shape=[jax.ShapeDtypeStruct((s, d), BF16)] * ng, args=(x, *gains), name=name)
    return outs


def _rms_fwd_ring(x, gain, *, name, tr=512):
    s, d = x.shape
    tr = _row_tile(s, tr)
    n = s // tr

    def body(x_hbm, g_ref, o_hbm, xbuf, obuf, in_sem, out_sem):
        def fetch(i):
            return pltpu.make_async_copy(x_hbm.at[pl.ds(i * tr, tr)], xbuf.at[i % 3], in_sem.at[i % 3])

        def put(i):
            return pltpu.make_async_copy(obuf.at[i % 2], o_hbm.at[pl.ds(i * tr, tr)], out_sem.at[i % 2])

        for i in range(min(2, n)):
            fetch(i).start()
        for i in range(n):
            if i + 2 < n:
                fetch(i + 2).start()
            fetch(i).wait()
            if i >= 2:
                put(i - 2).wait()
            xv = xbuf[i % 3]
            obuf[i % 2] = (xv * _rstd(xv) * g_ref[...]).astype(BF16)
            put(i).start()
        for i in range(max(0, n - 2), n):
            put(i).wait()

    return pl.pallas_call(
        body, in_specs=[HBM, pl.BlockSpec(memory_space=pltpu.VMEM)], out_specs=HBM,
        out_shape=jax.ShapeDtypeStruct((s, d), BF16),
        scratch_shapes=[pltpu.VMEM((3, tr, d), F32), pltpu.VMEM((2, tr, d), BF16),
                        pltpu.SemaphoreType.DMA((3,)), pltpu.SemaphoreType.DMA((2,))],
        name=name, compiler_params=_cparams(),
    )(x, gain)


def _accumulate(i, ref, value):
    @pl.when(i == 0)
    def _():
        ref[...] = value

    @pl.when(i > 0)
    def _():
        ref[...] += value


def _mm_residual_norms(y, w, res, gains, *, name, tm=512, side=None):
    s, k = y.shape
    d = w.shape[1]
    tm = _row_tile(s, tm)
    ng = len(gains)

    def body(y_ref, w_ref, r_ref, *rest):
        g_refs, h_ref, n_refs = rest[:ng], rest[ng], rest[ng + 1:]
        h = r_ref[...] + jnp.dot(y_ref[...], w_ref[...], preferred_element_type=F32)
        h_ref[...] = h
        xh = h * _rstd(h)
        for t in range(ng):
            n_refs[t][...] = (xh * g_refs[t][...]).astype(BF16)

    row = pl.BlockSpec((tm, d), lambda i: (i, 0))
    vec = pl.BlockSpec((1, d), lambda i: (0, 0))
    return _call(
        body, grid=(s // tm,),
        in_specs=[pl.BlockSpec((tm, k), lambda i: (i, 0)), pl.BlockSpec((k, d), lambda i: (0, 0)), row] + [vec] * ng,
        out_specs=[row] * (1 + ng),
        out_shape=[jax.ShapeDtypeStruct((s, d), F32)] + [jax.ShapeDtypeStruct((s, d), BF16)] * ng,
        args=(y, w, res, *gains), name=name, side=side)


def _mm_residual_loss(y, w, res, tgt, gain, *, name, tm=512):
    s, k = y.shape
    d = w.shape[1]
    tm = _row_tile(s, tm)

    def body(y_ref, w_ref, r_ref, t_ref, g_ref, loss_ref, dh_ref, dhb_ref, dg_ref):
        i = pl.program_id(0)
        hv = r_ref[...] + jnp.dot(y_ref[...], w_ref[...], preferred_element_type=F32)
        g = g_ref[...]
        r = _rstd(hv)
        xh = hv * r
        diff = xh * g - t_ref[...]
        part = 0.5 / d * jnp.sum(jnp.sum(diff * diff, axis=-1, keepdims=True), axis=0, keepdims=True)
        dout = diff * (1.0 / d)
        a = dout * g
        dh = r * (a - xh * jnp.mean(a * xh, axis=-1, keepdims=True))
        dh_ref[...] = dh
        dhb_ref[...] = dh.astype(BF16)
        _accumulate(i, dg_ref, jnp.sum(dout * xh, axis=0, keepdims=True))
        _accumulate(i, loss_ref, jnp.broadcast_to(part, (8, 128)))

    row = pl.BlockSpec((tm, d), lambda i: (i, 0))
    vec = pl.BlockSpec((1, d), lambda i: (0, 0))
    outs, _ = _call(
        body, grid=(s // tm,),
        in_specs=[pl.BlockSpec((tm, k), lambda i: (i, 0)), pl.BlockSpec((k, d), lambda i: (0, 0)), row, row, vec],
        out_specs=[pl.BlockSpec((8, 128), lambda i: (0, 0)), row, row, vec],
        out_shape=[jax.ShapeDtypeStruct((8, 128), F32), jax.ShapeDtypeStruct((s, d), F32),
                   jax.ShapeDtypeStruct((s, d), BF16), jax.ShapeDtypeStruct((1, d), F32)],
        args=(y, w, res, tgt, gain), name=name)
    return outs


def _mm_nt_rms_bwd(terms, x, dres, *, name, tm, side=None):
    s, d = x.shape
    tm = _row_tile(s, tm)
    nt = len(terms)

    def body(*refs):
        a_refs, b_refs, g_refs = refs[0:3 * nt:3], refs[1:3 * nt:3], refs[2:3 * nt:3]
        x_ref, dres_ref = refs[3 * nt], refs[3 * nt + 1]
        dx_ref, dxb_ref = refs[3 * nt + 2], refs[3 * nt + 3]
        dg_refs = refs[3 * nt + 4:]
        i = pl.program_id(0)
        xv = x_ref[...]
        r = _rstd(xv)
        xh = xv * r
        acc = jnp.zeros_like(xv)
        for t in range(nt):
            b_ref = b_refs[t]
            if len(b_ref.shape) == 3:
                kc = b_ref.shape[2]
                dn = None
                for sh in range(b_ref.shape[0]):
                    part = lax.dot_general(a_refs[t][:, sh * kc:(sh + 1) * kc], b_ref[sh], NT, preferred_element_type=F32)
                    dn = part if dn is None else dn + part
            else:
                dn = lax.dot_general(a_refs[t][...], b_ref[...], NT, preferred_element_type=F32)
            acc = acc + dn * g_refs[t][...]
            _accumulate(i, dg_refs[t], jnp.sum(dn * xh, axis=0, keepdims=True))
        dx = dres_ref[...] + r * (acc - xh * jnp.mean(acc * xh, axis=-1, keepdims=True))
        dx_ref[...] = dx
        dxb_ref[...] = dx.astype(BF16)

    row = pl.BlockSpec((tm, d), lambda i: (i, 0))
    vec = pl.BlockSpec((1, d), lambda i: (0, 0))
    in_specs, args = [], []
    for a, b, g in terms:
        in_specs += [pl.BlockSpec((tm, a.shape[1]), lambda i: (i, 0)),
                     pl.BlockSpec(b.shape, (lambda i: (0, 0, 0)) if b.ndim == 3 else (lambda i: (0, 0))), vec]
        args += [a, b, g]
    return _call(
        body, grid=(s // tm,), in_specs=in_specs + [row, row], out_specs=[row, row] + [vec] * nt,
        out_shape=[jax.ShapeDtypeStruct((s, d), F32), jax.ShapeDtypeStruct((s, d), BF16)]
        + [jax.ShapeDtypeStruct((1, d), F32)] * nt,
        args=(*args, x, dres), name=name, side=side)


def _causal_mask(transposed=False):
    row = lax.broadcasted_iota(jnp.int32, (CHUNK, CHUNK), 0)
    col = lax.broadcasted_iota(jnp.int32, (CHUNK, CHUNK), 1)
    return col >= row if transposed else col <= row


def _silu_parts(g):
    sg = jax.nn.sigmoid(g)
    return g * sg, sg * (1.0 + g * (1.0 - sg))


def _gate_fwd(z, ln_g, ln_b, ws, bs_t, *, tr=512, side=None):
    s = z.shape[0]
    tr = _row_tile(s, tr)
    w = A_WIDTH

    def body(u_ref, v_ref, g_ref, lg_ref, lb_ref, ws_ref, bst_ref, y_ref):
        v = v_ref[...].astype(F32)
        mu = jnp.mean(v, axis=-1, keepdims=True)
        xc = v - mu
        rs = lax.rsqrt(jnp.mean(xc * xc, axis=-1, keepdims=True) + EPS)
        vln = (xc * rs * lg_ref[...] + lb_ref[...]).astype(BF16)
        mask = _causal_mask()
        for grp in range(A_GROUPS):
            cols = slice(grp * CHUNK, (grp + 1) * CHUNK)
            wsm = jnp.where(mask, ws_ref[grp], 0.0).astype(BF16)
            bcol = bst_ref[:, grp:grp + 1]
            for ci in range(tr // CHUNK):
                rows = slice(ci * CHUNK, (ci + 1) * CHUNK)
                sv = jnp.dot(wsm, vln[rows, cols], preferred_element_type=F32) + bcol
                gv = g_ref[rows, cols].astype(F32)
                y_ref[rows, cols] = (u_ref[rows, cols].astype(F32) * sv * (gv * jax.nn.sigmoid(gv))).astype(BF16)

    vec = pl.BlockSpec((1, w), lambda i: (0, 0))
    (y,), side_outs = _call(
        body, grid=(s // tr,),
        in_specs=[pl.BlockSpec((tr, w), lambda i: (i, 0)), pl.BlockSpec((tr, w), lambda i: (i, 1)),
                  pl.BlockSpec((tr, w), lambda i: (i, 2)), vec, vec,
                  pl.BlockSpec((A_GROUPS, CHUNK, CHUNK), lambda i: (0, 0, 0)),
                  pl.BlockSpec((CHUNK, A_GROUPS), lambda i: (0, 0))],
        out_specs=[pl.BlockSpec((tr, w), lambda i: (i, 0))],
        out_shape=[jax.ShapeDtypeStruct((s, w), BF16)], args=(z, z, z, ln_g, ln_b, ws, bs_t), name="gate_fwd",
        side=side)
    return y, side_outs


def _gate_bwd(z, dy, ln_g, ln_b, ws, ws_t, bs_t, *, tr=256, side=None):
    s = z.shape[0]
    tr = _row_tile(s, tr)
    w = A_WIDTH
    nsteps = s // tr

    def body(u_ref, v_ref, g_ref, dy_ref, lg_ref, lb_ref, ws_ref, wst_ref, bst_ref,
             dz_ref, dlg_ref, dlb_ref, dws_ref, dbst_ref, dvln_sc, dsv_sc):
        i = pl.program_id(0)

        @pl.when(i == 0)
        def _():
            dws_ref[...] = jnp.zeros_like(dws_ref)
            dsv_sc[...] = jnp.zeros_like(dsv_sc)

        v = v_ref[...].astype(F32)
        mu = jnp.mean(v, axis=-1, keepdims=True)
        xc = v - mu
        rs = lax.rsqrt(jnp.mean(xc * xc, axis=-1, keepdims=True) + EPS)
        xh = xc * rs
        lg = lg_ref[...]
        vln = (xh * lg + lb_ref[...]).astype(BF16)
        mask = _causal_mask()
        mask_t = _causal_mask(transposed=True)
        for grp in range(A_GROUPS):
            cols = slice(grp * CHUNK, (grp + 1) * CHUNK)
            wsm = jnp.where(mask, ws_ref[grp], 0.0).astype(BF16)
            wsm_t = jnp.where(mask_t, wst_ref[grp], 0.0).astype(BF16)
            bcol = bst_ref[:, grp:grp + 1]
            for ci in range(tr // CHUNK):
                rows = slice(ci * CHUNK, (ci + 1) * CHUNK)
                vb = vln[rows, cols]
                sv = jnp.dot(wsm, vb, preferred_element_type=F32) + bcol
                uv = u_ref[rows, cols].astype(F32)
                silu, dsilu = _silu_parts(g_ref[rows, cols].astype(F32))
                dyv = dy_ref[rows, cols].astype(F32)
                dyu = dyv * uv
                dz_ref[rows, cols] = (dyv * sv * silu).astype(BF16)
                dz_ref[rows, 2 * w + grp * CHUNK:2 * w + (grp + 1) * CHUNK] = (dyu * sv * dsilu).astype(BF16)
                dsv = dyu * silu
                dsvb = dsv.astype(BF16)
                dvln_sc[rows, cols] = jnp.dot(wsm_t, dsvb, preferred_element_type=F32)
                dws_ref[grp] += lax.dot_general(dsvb, vb, NT, preferred_element_type=F32)
                dsv_sc[grp] += dsv
        dvln = dvln_sc[...]
        dlg_t = jnp.sum(dvln * xh, axis=0, keepdims=True)
        dlb_t = jnp.sum(dvln, axis=0, keepdims=True)
        a = dvln * lg
        dv = rs * (a - jnp.mean(a, axis=-1, keepdims=True) - xh * jnp.mean(a * xh, axis=-1, keepdims=True))
        dz_ref[:, w:2 * w] = dv.astype(BF16)

        @pl.when(i == 0)
        def _():
            dlg_ref[...] = dlg_t
            dlb_ref[...] = dlb_t

        @pl.when(i > 0)
        def _():
            dlg_ref[...] += dlg_t
            dlb_ref[...] += dlb_t

        @pl.when(i == nsteps - 1)
        def _():
            for grp in range(A_GROUPS):
                dws_ref[grp] = jnp.where(mask, dws_ref[grp], 0.0)
                dbst_ref[:, grp:grp + 1] = jnp.sum(dsv_sc[grp], axis=-1, keepdims=True)

    vec = pl.BlockSpec((1, w), lambda i: (0, 0))
    wsspec = pl.BlockSpec((A_GROUPS, CHUNK, CHUNK), lambda i: (0, 0, 0))
    bsspec = pl.BlockSpec((CHUNK, A_GROUPS), lambda i: (0, 0))
    return _call(
        body, grid=(nsteps,),
        in_specs=[pl.BlockSpec((tr, w), lambda i: (i, 0)), pl.BlockSpec((tr, w), lambda i: (i, 1)),
                  pl.BlockSpec((tr, w), lambda i: (i, 2)), pl.BlockSpec((tr, w), lambda i: (i, 0)),
                  vec, vec, wsspec, wsspec, bsspec],
        out_specs=[pl.BlockSpec((tr, 3 * w), lambda i: (i, 0)), vec, vec, wsspec, bsspec],
        out_shape=[jax.ShapeDtypeStruct((s, 3 * w), BF16), jax.ShapeDtypeStruct((1, w), F32),
                   jax.ShapeDtypeStruct((1, w), F32), jax.ShapeDtypeStruct((A_GROUPS, CHUNK, CHUNK), F32),
                   jax.ShapeDtypeStruct((CHUNK, A_GROUPS), F32)],
        scratch=[pltpu.VMEM((tr, w), F32), pltpu.VMEM((A_GROUPS, CHUNK, CHUNK), F32)],
        args=(z, z, z, dy, ln_g, ln_b, ws, ws_t, bs_t), name="gate_bwd", side=side)


HEADS_PER_BLOCK = 128 // HEAD_DIM
BLOCKS_PER_KV = Q_PER_KV // HEADS_PER_BLOCK
SCALE = HEAD_DIM ** -0.5
LOG2_E = math.log2(math.e)


def _rope_tables(s):
    lane = jnp.arange(128)
    inv_freq = ROPE_THETA ** (-(2 * (lane % (HEAD_DIM // 2))).astype(F32) / HEAD_DIM)
    sign = jnp.where(lane % HEAD_DIM < HEAD_DIM // 2, -1.0, 1.0).astype(F32)
    ang = jnp.arange(s, dtype=F32)[:, None] * inv_freq[None, :]
    return jnp.cos(ang), jnp.sin(ang) * sign[None, :]


def _swap_halves(x):
    n = x.shape[-1]
    lane = lax.broadcasted_iota(jnp.int32, x.shape, x.ndim - 1)
    first = (lane % HEAD_DIM) < (HEAD_DIM // 2)
    return jnp.where(first, pltpu.roll(x, n - HEAD_DIM // 2, x.ndim - 1), pltpu.roll(x, HEAD_DIM // 2, x.ndim - 1))


def _left_half(rows):
    return lax.broadcasted_iota(jnp.int32, (rows, 128), 1) < HEAD_DIM


def _dup_heads(x):
    left = _left_half(x.shape[0])
    swapped = pltpu.roll(x, HEAD_DIM, 1)
    return jnp.concatenate([jnp.where(left, x, swapped), jnp.where(left, swapped, x)], axis=-1)


def _fold_heads(a):
    b0, b1 = a[:, :128], a[:, 128:]
    f0 = b0 + pltpu.roll(b0, HEAD_DIM, 1)
    f1 = b1 + pltpu.roll(b1, HEAD_DIM, 1)
    return jnp.where(_left_half(a.shape[0]), f0, f1)


def _kv_rope(n_kv, w_kv, b_kv, cos, sin, *, tr=2048):
    s, d = n_kv.shape
    tr = _row_tile(s, tr)

    def body(n_ref, w_ref, b_ref, c_ref, s_ref, k_ref, v_ref):
        x = jnp.dot(n_ref[...], w_ref[...], preferred_element_type=F32) + b_ref[...]
        k = x[:, :KV_WIDTH]
        k_ref[...] = _dup_heads(k * c_ref[...] + _swap_halves(k) * s_ref[...]).astype(BF16)
        v_ref[...] = _dup_heads(x[:, KV_WIDTH:]).astype(BF16)

    tab = pl.BlockSpec((tr, KV_WIDTH), lambda i: (i, 0))
    wide = pl.BlockSpec((tr, 2 * KV_WIDTH), lambda i: (i, 0))
    outs, _ = _call(body, grid=(s // tr,),
                    in_specs=[pl.BlockSpec((tr, d), lambda i: (i, 0)), pl.BlockSpec((d, 2 * KV_WIDTH), lambda i: (0, 0)),
                              pl.BlockSpec((1, 2 * KV_WIDTH), lambda i: (0, 0)), tab, tab],
                    out_specs=[wide, wide], out_shape=[jax.ShapeDtypeStruct((s, 2 * KV_WIDTH), BF16)] * 2,
                    args=(n_kv, w_kv, b_kv, cos, sin), name="kv_rope")
    return outs


def _kv_rope_bwd(dk2, dv2, cos, sin, *, tr=2048):
    s = dk2.shape[0]
    tr = _row_tile(s, tr)

    def body(dk_ref, dv_ref, c_ref, s_ref, dkv_ref, db_ref):
        i = pl.program_id(0)
        d = _fold_heads(dk_ref[...])
        dk = d * c_ref[...] + _swap_halves(d * s_ref[...])
        dvv = _fold_heads(dv_ref[...])
        dkv_ref[:, :KV_WIDTH] = dk.astype(BF16)
        dkv_ref[:, KV_WIDTH:] = dvv.astype(BF16)
        sk = jnp.sum(dk, axis=0, keepdims=True)
        sv = jnp.sum(dvv, axis=0, keepdims=True)

        @pl.when(i == 0)
        def _():
            db_ref[:, :KV_WIDTH] = sk
            db_ref[:, KV_WIDTH:] = sv

        @pl.when(i > 0)
        def _():
            db_ref[:, :KV_WIDTH] += sk
            db_ref[:, KV_WIDTH:] += sv

    tab = pl.BlockSpec((tr, KV_WIDTH), lambda i: (i, 0))
    wide = pl.BlockSpec((tr, 2 * KV_WIDTH), lambda i: (i, 0))
    outs, _ = _call(body, grid=(s // tr,), in_specs=[wide, wide, tab, tab],
                    out_specs=[wide, pl.BlockSpec((1, 2 * KV_WIDTH), lambda i: (0, 0))],
                    out_shape=[jax.ShapeDtypeStruct((s, 2 * KV_WIDTH), BF16),
                               jax.ShapeDtypeStruct((1, 2 * KV_WIDTH), F32)],
                    args=(dk2, dv2, cos, sin), name="kv_rope_bwd")
    return outs


def _from_previous():
    cols = Q_PER_KV * CHUNK
    k = lax.broadcasted_iota(jnp.int32, (CHUNK, cols), 0)
    q = lax.broadcasted_iota(jnp.int32, (CHUNK, cols), 1) & (CHUNK - 1)
    return k > q


def _fold(x2, prev):
    return jnp.where(prev, x2[:CHUNK], x2[CHUNK:])


def _unfold(x, prev):
    zero = jnp.zeros_like(x)
    return jnp.concatenate([jnp.where(prev, x, zero), jnp.where(prev, zero, x)], axis=0)


def _stack_heads(blocks, left):
    parts = []
    for b in blocks:
        parts.append(jnp.where(left, b, jnp.zeros_like(b)))
        parts.append(jnp.where(left, jnp.zeros_like(b), b))
    return jnp.concatenate(parts, axis=0)


def _unstack_heads(xt):
    top = lax.broadcasted_iota(jnp.int32, (128, CHUNK), 0) < HEAD_DIM
    return [jnp.where(top, xt[:, (2 * b) * CHUNK:(2 * b + 1) * CHUNK], xt[:, (2 * b + 1) * CHUNK:(2 * b + 2) * CHUNK]).T
            for b in range(BLOCKS_PER_KV)]


def _sink_row(sk_ref, kvh):
    return jnp.concatenate([jnp.full((1, CHUNK), sk_ref[0, kvh * Q_PER_KV + r], F32) for r in range(Q_PER_KV)], axis=1)


def _stacked_probs(qs, kd, prev, sink, i):
    sc2 = lax.dot_general(kd, qs, NT, preferred_element_type=F32)
    no_previous = jnp.where(i > 0, 0.0, NEG_BIG)
    sc = jnp.where(prev, sc2[:CHUNK] + no_previous, sc2[CHUNK:])
    sink = sink * (1.0 / SCALE)
    m = jnp.maximum(jnp.max(sc, axis=0, keepdims=True), sink)
    p = jnp.exp2((sc - m) * (SCALE * LOG2_E))
    esink = jnp.exp2((sink - m) * (SCALE * LOG2_E))
    inv = 1.0 / (jnp.sum(p, axis=0, keepdims=True) + esink)
    return p * inv, esink * inv


def _lane_block(b):
    return slice(b * 128, (b + 1) * 128)


def _rope_blocks(zq_ref, bq_ref, cos, sin, kvh, rows):
    out = []
    for b in range(BLOCKS_PER_KV):
        cols = _lane_block(kvh * BLOCKS_PER_KV + b)
        q = zq_ref[rows, cols].astype(F32) + bq_ref[:, cols]
        out.append((q * cos + _swap_halves(q) * sin).astype(BF16))
    return out


CHUNKS_PER_STEP = 4


def _attn_specs():
    rows = CHUNKS_PER_STEP * CHUNK
    qspec = pl.BlockSpec((rows, B_WIDTH), lambda i: (i, 0))
    gspec = pl.BlockSpec((rows, B_WIDTH), lambda i: (i, 1))
    prev = pl.BlockSpec((CHUNK, 2 * KV_WIDTH), lambda i: (jnp.maximum(CHUNKS_PER_STEP * i - 1, 0), 0))
    cur = pl.BlockSpec((rows, 2 * KV_WIDTH), lambda i: (i, 0))
    tab = pl.BlockSpec((rows, KV_WIDTH), lambda i: (i, 0))
    bq = pl.BlockSpec((1, B_WIDTH), lambda i: (0, 0))
    sinks = pl.BlockSpec(memory_space=pltpu.SMEM)
    return qspec, gspec, prev, cur, tab, bq, sinks


def _chunk_keys(prev_ref, cur_ref, sub):
    before = prev_ref[...] if sub == 0 else cur_ref[(sub - 1) * CHUNK:sub * CHUNK]
    return jnp.concatenate([before, cur_ref[sub * CHUNK:(sub + 1) * CHUNK]], axis=0)


def _attn_fwd(zb, k2, v2, cos, sin, b_bq, sinks, *, side=None):
    s = zb.shape[0]

    def body(zq_ref, zg_ref, kp_ref, kc_ref, vp_ref, vc_ref, c_ref, s_ref, bq_ref, sk_ref, y_ref):
        prev = _from_previous()
        left = _left_half(CHUNK)
        for sub in range(CHUNKS_PER_STEP):
            chunk = CHUNKS_PER_STEP * pl.program_id(0) + sub
            rows = slice(sub * CHUNK, (sub + 1) * CHUNK)
            cos, sin = c_ref[rows, :], s_ref[rows, :]
            kcat, vcat = _chunk_keys(kp_ref, kc_ref, sub), _chunk_keys(vp_ref, vc_ref, sub)
            for kvh in range(N_KV_HEADS):
                qs = _stack_heads(_rope_blocks(zq_ref, bq_ref, cos, sin, kvh, rows), left)
                p, _ = _stacked_probs(qs, kcat[:, _lane_block(kvh)], prev, _sink_row(sk_ref, kvh), chunk)
                ot = lax.dot_general(vcat[:, _lane_block(kvh)], _unfold(p, prev).astype(BF16), TN,
                                     preferred_element_type=F32)
                for b, ob in enumerate(_unstack_heads(ot)):
                    cols = _lane_block(kvh * BLOCKS_PER_KV + b)
                    gv = zg_ref[rows, cols].astype(F32)
                    y_ref[rows, cols] = (ob * (gv * jax.nn.sigmoid(gv))).astype(BF16)

    qspec, gspec, prev, cur, tab, bq, sk = _attn_specs()
    (y,), side_outs = _call(body, grid=(s // (CHUNKS_PER_STEP * CHUNK),),
                            in_specs=[qspec, gspec, prev, cur, prev, cur, tab, tab, bq, sk],
                            out_specs=[qspec], out_shape=[jax.ShapeDtypeStruct((s, B_WIDTH), BF16)],
                            args=(zb, zb, k2, k2, v2, v2, cos, sin, b_bq, sinks), name="attn_fwd", side=side)
    return y, side_outs


def _attn_bwd(zb, dyb, k2, v2, cos, sin, b_bq, sinks, *, side=None):
    s = zb.shape[0]

    def body(zq_ref, zg_ref, dy_ref, kp_ref, kc_ref, vp_ref, vc_ref, c_ref, s_ref, bq_ref, sk_ref,
             dz_ref, dk_ref, dv_ref, dbq_ref, dsk_ref):
        i = pl.program_id(0)

        @pl.when(i == 0)
        def _():
            dk_ref[...] = jnp.zeros_like(dk_ref)
            dv_ref[...] = jnp.zeros_like(dv_ref)
            dbq_ref[...] = jnp.zeros_like(dbq_ref)
            dsk_ref[...] = jnp.zeros_like(dsk_ref)

        prev = _from_previous()
        left = _left_half(CHUNK)
        lane = lax.broadcasted_iota(jnp.int32, (1, 128), 1)
        dsk_row = jnp.zeros((1, 128), F32)
        for sub in range(CHUNKS_PER_STEP):
            chunk = CHUNKS_PER_STEP * i + sub
            rows = slice(sub * CHUNK, (sub + 1) * CHUNK)
            cos, sin = c_ref[rows, :], s_ref[rows, :]
            kcat, vcat = _chunk_keys(kp_ref, kc_ref, sub), _chunk_keys(vp_ref, vc_ref, sub)
            cur_rows = pl.ds(pl.multiple_of(chunk * CHUNK, CHUNK), CHUNK)
            for kvh in range(N_KV_HEADS):
                kd, vd = kcat[:, _lane_block(kvh)], vcat[:, _lane_block(kvh)]
                qs = _stack_heads(_rope_blocks(zq_ref, bq_ref, cos, sin, kvh, rows), left)
                p, psink = _stacked_probs(qs, kd, prev, _sink_row(sk_ref, kvh), chunk)
                pb = _unfold(p, prev).astype(BF16)
                ot = lax.dot_general(vd, pb, TN, preferred_element_type=F32)
                gates, dys = [], []
                for b in range(BLOCKS_PER_KV):
                    cols = _lane_block(kvh * BLOCKS_PER_KV + b)
                    gates.append(_silu_parts(zg_ref[rows, cols].astype(F32)))
                    dys.append(dy_ref[rows, cols].astype(F32))
                dos = _stack_heads([(dyv * silu).astype(BF16) for dyv, (silu, _) in zip(dys, gates)], left)
                dp = _fold(lax.dot_general(vd, dos, NT, preferred_element_type=F32), prev)
                delta = jnp.sum(p * dp, axis=0, keepdims=True)
                ds = _unfold(p * (dp - delta) * SCALE, prev).astype(BF16)
                dqt = lax.dot_general(kd, ds, TN, preferred_element_type=F32)
                dk_part = jnp.dot(ds, qs, preferred_element_type=F32)
                dv_part = jnp.dot(pb, dos, preferred_element_type=F32)
                dk_ref[cur_rows, _lane_block(kvh)] += dk_part[CHUNK:]
                dv_ref[cur_rows, _lane_block(kvh)] += dv_part[CHUNK:]

                @pl.when(chunk > 0)
                def _(kvh=kvh, chunk=chunk, dk_part=dk_part, dv_part=dv_part):
                    prev_rows = pl.ds(pl.multiple_of((chunk - 1) * CHUNK, CHUNK), CHUNK)
                    dk_ref[prev_rows, _lane_block(kvh)] += dk_part[:CHUNK]
                    dv_ref[prev_rows, _lane_block(kvh)] += dv_part[:CHUNK]

                sink_grad = psink * delta
                for r in range(Q_PER_KV):
                    dsink = -jnp.sum(sink_grad[:, r * CHUNK:(r + 1) * CHUNK], axis=1, keepdims=True)
                    dsk_row = dsk_row + jnp.where(lane == kvh * Q_PER_KV + r, dsink, 0.0)
                blocks = zip(_unstack_heads(ot), _unstack_heads(dqt), dys, gates)
                for b, (ob, dqr, dyv, (_, dsilu)) in enumerate(blocks):
                    blk = kvh * BLOCKS_PER_KV + b
                    dq = dqr * cos + _swap_halves(dqr * sin)
                    dbq_ref[:, _lane_block(blk)] += jnp.sum(dq, axis=0, keepdims=True)
                    dz_ref[rows, _lane_block(blk)] = dq.astype(BF16)
                    dz_ref[rows, _lane_block(B_WIDTH // 128 + blk)] = (dyv * ob * dsilu).astype(BF16)
        dsk_ref[0:1, :] += dsk_row

    qspec, gspec, prev, cur, tab, bq, sk = _attn_specs()
    full = pl.BlockSpec((s, 2 * KV_WIDTH), lambda i: (0, 0))
    return _call(
        body, grid=(s // (CHUNKS_PER_STEP * CHUNK),),
        in_specs=[qspec, gspec, qspec, prev, cur, prev, cur, tab, tab, bq, sk],
        out_specs=[pl.BlockSpec((CHUNKS_PER_STEP * CHUNK, 2 * B_WIDTH), lambda i: (i, 0)), full, full, bq,
                   pl.BlockSpec((8, 128), lambda i: (0, 0))],
        out_shape=[jax.ShapeDtypeStruct((s, 2 * B_WIDTH), BF16), jax.ShapeDtypeStruct((s, 2 * KV_WIDTH), F32),
                   jax.ShapeDtypeStruct((s, 2 * KV_WIDTH), F32), jax.ShapeDtypeStruct((1, B_WIDTH), F32),
                   jax.ShapeDtypeStruct((8, 128), F32)],
        args=(zb, zb, dyb, k2, k2, v2, v2, cos, sin, b_bq, sinks), name="attn_bwd", side=side)


def _place():
    x, y, c = lax.axis_index("x"), lax.axis_index("y"), lax.axis_index("c")
    return x, y, c, [(1 - x, y), (x, 1 - y), (1 - x, 1 - y)]


def _relations():
    return [(r >> 2 & 1, r >> 1 & 1, r & 1) for r in range(1, 8)]


def _gather_side(arrs):
    n = len(arrs)

    def copies(ins, outs, sems):
        send_ici, recv_ici, send_d2d, recv_d2d, local_sem = sems
        x, y, c, chips = _place()
        me = 2 * x + y

        def rows(a, half):
            hr = arrs[a].shape[0] // 2
            return pl.ds(half * hr, hr)

        def ici(a, j, src_chip, to):
            return pltpu.make_async_remote_copy(
                src_ref=ins[a].at[rows(a, c)], dst_ref=outs[a].at[src_chip, rows(a, c)],
                send_sem=send_ici.at[a, j], recv_sem=recv_ici.at[a, j], device_id=to, device_id_type=MESH)

        def d2d(a, j, chip, half):
            blk = outs[a].at[chip, rows(a, half)]
            return pltpu.make_async_remote_copy(
                src_ref=blk, dst_ref=blk, send_sem=send_d2d.at[a, j], recv_sem=recv_d2d.at[a, j],
                device_id=(x, y, 1 - c), device_id_type=MESH)

        local = [pltpu.make_async_copy(ins[a], outs[a].at[me], local_sem.at[a]) for a in range(n)]
        pairs = [(a, j, chip) for a in range(n) for j, chip in enumerate(chips)]
        return c, me, local, ici, d2d, pairs

    def start(ins, outs, sems):
        c, me, local, ici, _, pairs = copies(ins, outs, sems)
        for cp in local:
            cp.start()
        for a, j, chip in pairs:
            ici(a, j, me, (*chip, c)).start()

    def passing(ins, outs, sems):
        c, _, _, ici, d2d, pairs = copies(ins, outs, sems)
        for a, j, (px, py) in pairs:
            ici(a, j, 2 * px + py, (px, py, c)).wait_recv()
            d2d(a, j, 2 * px + py, c).start()

    def finish(ins, outs, sems):
        c, me, local, ici, d2d, pairs = copies(ins, outs, sems)
        for a, j, (px, py) in pairs:
            d2d(a, j, 2 * px + py, 1 - c).wait_recv()
        for a, j, (px, py) in pairs:
            ici(a, j, me, (px, py, c)).wait_send()
            d2d(a, j, 2 * px + py, c).wait_send()
        for cp in local:
            cp.wait()

    return _Side(arrs, [jax.ShapeDtypeStruct((N_CHIPS,) + a.shape, a.dtype) for a in arrs],
                 [pltpu.SemaphoreType.DMA((n, 3))] * 4 + [pltpu.SemaphoreType.DMA((n,))], start, finish,
                 passing=passing)


def _exchange_side(grads):
    n = len(grads)

    def copies(ins, outs, sems):
        send_sem, recv_sem = sems
        x, y, c, _ = _place()
        cps = []
        for a in range(n):
            hr = grads[a].shape[1] // 2
            cps.append(pltpu.make_async_remote_copy(
                src_ref=ins[a].at[:, pl.ds((1 - c) * hr, hr), :], dst_ref=outs[a],
                send_sem=send_sem.at[a], recv_sem=recv_sem.at[a], device_id=(x, y, 1 - c), device_id_type=MESH))
        return cps

    def start(ins, outs, sems):
        for cp in copies(ins, outs, sems):
            cp.start()

    def finish(ins, outs, sems):
        for cp in copies(ins, outs, sems):
            cp.wait()

    return _Side(grads, [jax.ShapeDtypeStruct((g.shape[0], g.shape[1] // 2, g.shape[2]), g.dtype) for g in grads],
                 [pltpu.SemaphoreType.DMA((n,))] * 2, start, finish)


def _scatter_side(chip_sums, small=None):
    n = len(chip_sums)
    arrs = list(chip_sums) + ([small] if small is not None else [])

    def copies(ins, outs, sems):
        x, y, c, chips = _place()
        cps = []
        for a in range(n):
            for j, (px, py) in enumerate(chips):
                cps.append(pltpu.make_async_remote_copy(
                    src_ref=ins[a].at[2 * px + py], dst_ref=outs[a].at[j],
                    send_sem=sems[0].at[a, j], recv_sem=sems[1].at[a, j], device_id=(px, py, c), device_id_type=MESH))
        if small is not None:
            for r, (fx, fy, fc) in enumerate(_relations(), start=1):
                px, py, pc = x ^ fx, y ^ fy, c ^ fc
                cps.append(pltpu.make_async_remote_copy(
                    src_ref=ins[n].at[4 * px + 2 * py + pc], dst_ref=outs[n].at[r],
                    send_sem=sems[2].at[r - 1], recv_sem=sems[3].at[r - 1], device_id=(px, py, pc),
                    device_id_type=MESH))
        return cps

    def start(ins, outs, sems):
        for cp in copies(ins, outs, sems):
            cp.start()

    def finish(ins, outs, sems):
        for cp in copies(ins, outs, sems):
            cp.wait()

    shapes = [jax.ShapeDtypeStruct((3,) + t.shape[1:], t.dtype) for t in chip_sums]
    sems = [pltpu.SemaphoreType.DMA((n, 3))] * 2
    if small is not None:
        shapes.append(jax.ShapeDtypeStruct(small.shape, small.dtype))
        sems += [pltpu.SemaphoreType.DMA((7,))] * 2
    return _Side(arrs, shapes, sems, start, finish)


def _small_scatter_side(small):
    def copies(ins, outs, sems):
        x, y, c, _ = _place()
        cps = []
        for r, (fx, fy, fc) in enumerate(_relations(), start=1):
            px, py, pc = x ^ fx, y ^ fy, c ^ fc
            cps.append(pltpu.make_async_remote_copy(
                src_ref=ins[0].at[4 * px + 2 * py + pc], dst_ref=outs[0].at[r],
                send_sem=sems[0].at[r - 1], recv_sem=sems[1].at[r - 1], device_id=(px, py, pc), device_id_type=MESH))
        return cps

    def start(ins, outs, sems):
        for cp in copies(ins, outs, sems):
            cp.start()

    def finish(ins, outs, sems):
        for cp in copies(ins, outs, sems):
            cp.wait()

    return _Side([small], [jax.ShapeDtypeStruct(small.shape, small.dtype)], [pltpu.SemaphoreType.DMA((7,))] * 2,
                 start, finish)


def _small_share_side(small):
    return _share_side([], small)


def _share_side(halves, small=None):
    n = len(halves)
    arrs = list(halves) + ([small] if small is not None else [])

    def copies(ins, outs, sems, mine):
        x, y, c, _ = _place()
        me = 4 * x + 2 * y + c
        cps = []
        for a in range(n):
            hr = halves[a].shape[0] // 2
            rows = pl.ds((c if mine else 1 - c) * hr, hr)
            cps.append(pltpu.make_async_remote_copy(
                src_ref=ins[a].at[rows], dst_ref=outs[a].at[rows],
                send_sem=sems[0].at[a], recv_sem=sems[1].at[a], device_id=(x, y, 1 - c), device_id_type=MESH))
        if small is not None:
            for r, (fx, fy, fc) in enumerate(_relations(), start=1):
                px, py, pc = x ^ fx, y ^ fy, c ^ fc
                seg = me if mine else 4 * px + 2 * py + pc
                cps.append(pltpu.make_async_remote_copy(
                    src_ref=ins[n].at[seg], dst_ref=outs[n].at[seg],
                    send_sem=sems[-2].at[r - 1], recv_sem=sems[-1].at[r - 1], device_id=(px, py, pc),
                    device_id_type=MESH))
        return cps

    def start(ins, outs, sems):
        for cp in copies(ins, outs, sems, True):
            cp.start()

    def finish(ins, outs, sems):
        for cp in copies(ins, outs, sems, False):
            cp.wait_recv()
        for cp in copies(ins, outs, sems, True):
            cp.wait_send()

    sems = ([pltpu.SemaphoreType.DMA((n,))] * 2 if n else []) + (
        [pltpu.SemaphoreType.DMA((7,))] * 2 if small is not None else [])
    return _Side(arrs, [jax.ShapeDtypeStruct(h.shape, h.dtype) for h in arrs], sems, start, finish,
                 aliases={i: i for i in range(len(arrs))})


GATHER_PIECES = [(0, 0), (0, 1), (1, 0), (2, 0), (1, 1), (2, 1), (3, 0), (3, 1)]


def _mm_gathering(a, shard, order, *, name, tm=1024):
    s, k = a.shape
    nc = shard.shape[1]
    tm = _row_tile(s, tm)
    tn = nc // 2
    hr = k // 2
    qr = hr // 2
    blocks = jnp.stack([order[src] * 2 + h for src, h in GATHER_PIECES]).astype(jnp.int32)

    def body(blocks_ref, a_ref, shard_ref, z_ref, full_ref, wbuf, send_ici, recv_ici, send_relay,
             recv_relay, send_d2d, recv_d2d, local_sem, load_sem):
        piece, i = pl.program_id(0), pl.program_id(1)
        x, y, c, chips = _place()
        me = 2 * x + y
        nbrs = chips[:2]
        chip_of = [2 * px + py for px, py in chips]

        def quarter(q):
            return pl.ds(c * hr + q * qr, qr)

        def sibling_quarter(q):
            return pl.ds((1 - c) * hr + q * qr, qr)

        def whole(half):
            return pl.ds(half * hr, hr)

        def cols(h):
            return pl.ds(h * tn, tn)

        def direct(j, src_chip, h):
            return pltpu.make_async_remote_copy(
                src_ref=shard_ref.at[whole(c), cols(h)], dst_ref=full_ref.at[src_chip, whole(c), cols(h)],
                send_sem=send_ici.at[j, h], recv_sem=recv_ici.at[j, h], device_id=(*nbrs[j], c), device_id_type=MESH)

        def relay(j, src_chip, h):
            blk = full_ref.at[src_chip, quarter(j), cols(h)]
            return pltpu.make_async_remote_copy(
                src_ref=blk, dst_ref=blk, send_sem=send_relay.at[j, h], recv_sem=recv_relay.at[j, h],
                device_id=(*nbrs[1 - j], c), device_id_type=MESH)

        def d2d(j, chip, rows, h):
            blk = full_ref.at[chip, rows, cols(h)]
            return pltpu.make_async_remote_copy(
                src_ref=blk, dst_ref=blk, send_sem=send_d2d.at[j, h], recv_sem=recv_d2d.at[j, h],
                device_id=(x, y, 1 - c), device_id_type=MESH)

        def load(p):
            src, h = GATHER_PIECES[p]
            where = shard_ref if src == 0 else full_ref.at[chip_of[src - 1]]
            return pltpu.make_async_copy(where.at[:, cols(h)], wbuf.at[p % 2], load_sem.at[p % 2])

        local = pltpu.make_async_copy(shard_ref, full_ref.at[me], local_sem)

        def arrived(p):
            src, h = GATHER_PIECES[p]
            if src in (1, 2):
                j = src - 1
                direct(j, chip_of[j], h).wait_recv()
                relay(j, chip_of[j], h).start()
                d2d(j, chip_of[j], whole(c), h).start()
            elif src == 3:
                for j in range(2):
                    relay(1 - j, chip_of[2], h).wait_recv()
                    d2d(2 + j, chip_of[2], quarter(1 - j), h).start()

        def fetch(p):
            src, h = GATHER_PIECES[p]
            if src in (1, 2):
                d2d(src - 1, chip_of[src - 1], whole(1 - c), h).wait_recv()
            elif src == 3:
                for j in range(2):
                    d2d(2 + j, chip_of[2], sibling_quarter(1 - j), h).wait_recv()
            load(p).start()

        n_i = s // tm
        for p in range(len(GATHER_PIECES)):
            @pl.when(jnp.logical_and(piece == p, i == 0))
            def _(p=p):
                if p == 0:
                    local.start()
                    for hh in range(2):
                        for j in range(2):
                            direct(j, me, hh).start()
                    load(0).start()
                load(p).wait()

        z_ref[...] = jnp.dot(a_ref[...], wbuf[piece % 2], preferred_element_type=F32).astype(z_ref.dtype)

        for p in range(len(GATHER_PIECES) - 1):
            @pl.when(jnp.logical_and(piece == p, i == min(1, n_i - 1)))
            def _(p=p):
                arrived(p + 1)

            @pl.when(jnp.logical_and(piece == p, i == min(2, n_i - 1)))
            def _(p=p):
                fetch(p + 1)

        last = jnp.logical_and(piece == len(GATHER_PIECES) - 1, i == n_i - 1)

        @pl.when(last)
        def _():
            for h in range(2):
                for j in range(2):
                    direct(j, me, h).wait_send()
                    relay(j, chip_of[j], h).wait_send()
                    d2d(j, chip_of[j], whole(c), h).wait_send()
                    d2d(2 + j, chip_of[2], quarter(1 - j), h).wait_send()
            local.wait()

    return pl.pallas_call(
        body,
        grid_spec=pltpu.PrefetchScalarGridSpec(
            num_scalar_prefetch=1, grid=(len(GATHER_PIECES), s // tm),
            in_specs=[pl.BlockSpec((tm, k), lambda p, i, blocks: (i, 0)), HBM],
            out_specs=[pl.BlockSpec((tm, tn), lambda p, i, blocks: (i, blocks[p])), HBM],
            scratch_shapes=[pltpu.VMEM((2, k, tn), BF16)] + [pltpu.SemaphoreType.DMA((2, 2))] * 4
            + [pltpu.SemaphoreType.DMA((4, 2))] * 2 + [pltpu.SemaphoreType.DMA, pltpu.SemaphoreType.DMA((2,))]),
        out_shape=[jax.ShapeDtypeStruct((s, N_CHIPS * nc), BF16), jax.ShapeDtypeStruct((N_CHIPS, k, nc), BF16)],
        name=name, compiler_params=_cparams(),
    )(blocks, a, shard)


def _mm_tn_exchanging(a, b, *, name, shards, tk=2048, side=None):
    s, m = a.shape
    nc = b.shape[1] // shards
    tk = _row_tile(s, tk)
    nk = s // tk
    hm = m // 2

    def body(a_ref, b_ref, part_ref, sib_ref, acc, keep_sem, send_sem, recv_sem):
        j, kk = pl.program_id(0), pl.program_id(1)
        x, y, c, _ = _place()

        def keep(jj, slot):
            mine = pl.ds(c * hm, hm)
            return pltpu.make_async_copy(acc.at[slot, mine], part_ref.at[jj], keep_sem.at[slot])

        def give(jj, slot):
            return pltpu.make_async_remote_copy(
                src_ref=acc.at[slot, pl.ds((1 - c) * hm, hm)], dst_ref=sib_ref.at[jj],
                send_sem=send_sem.at[slot], recv_sem=recv_sem.at[jj], device_id=(x, y, 1 - c), device_id_type=MESH)

        part = lax.dot_general(a_ref[...], b_ref[...], TN, preferred_element_type=F32)
        for slot in range(2):
            @pl.when(j % 2 == slot)
            def _(slot=slot):
                @pl.when(jnp.logical_and(kk == 0, j >= 2))
                def _():
                    keep(j - 2, slot).wait()
                    give(j - 2, slot).wait_send()

                @pl.when(kk == 0)
                def _():
                    acc[slot] = part

                @pl.when(kk > 0)
                def _():
                    acc[slot] += part

                @pl.when(kk == nk - 1)
                def _():
                    keep(j, slot).start()
                    give(j, slot).start()

        @pl.when(jnp.logical_and(j == shards - 1, kk == nk - 1))
        def _():
            for jj in range(shards - 2, shards):
                keep(jj, jj % 2).wait()
                give(jj, jj % 2).wait_send()
            for jj in range(shards):
                give(jj, jj % 2).wait_recv()

    assert shards >= 2
    return _call(
        body, grid=(shards, nk),
        in_specs=[pl.BlockSpec((tk, m), lambda j, kk: (kk, 0)), pl.BlockSpec((tk, nc), lambda j, kk: (kk, j))],
        out_specs=[HBM, HBM],
        out_shape=[jax.ShapeDtypeStruct((shards, hm, nc), F32), jax.ShapeDtypeStruct((shards, hm, nc), F32)],
        scratch=[pltpu.VMEM((2, m, nc), F32), pltpu.SemaphoreType.DMA((2,)), pltpu.SemaphoreType.DMA((2,)),
                 pltpu.SemaphoreType.DMA((shards,))],
        args=(a, b), name=name, side=side)


def _col_tile(cols):
    return cols if cols <= 2048 else 512


def _add_sibling(grad, recv, core, *, name):
    k, r, c = grad.shape
    hr = r // 2
    tr = min(hr, 256)
    tc = _col_tile(c)
    nrb = hr // tr

    def body(core_ref, g_ref, r_ref, o_ref):
        o_ref[...] = (g_ref[...] + r_ref[...]).astype(BF16)

    return pl.pallas_call(
        body,
        grid_spec=pltpu.PrefetchScalarGridSpec(
            num_scalar_prefetch=1, grid=(k, nrb, c // tc),
            in_specs=[pl.BlockSpec((None, tr, tc), lambda kk, i, j, core: (kk, core[0] * nrb + i, j)),
                      pl.BlockSpec((None, tr, tc), lambda kk, i, j, core: (kk, i, j))],
            out_specs=pl.BlockSpec((None, tr, tc), lambda kk, i, j, core: (kk, i, j))),
        out_shape=jax.ShapeDtypeStruct((k, hr, c), BF16), name=name, compiler_params=_cparams(),
    )(core, grad, recv)


def _sum_chips(grad, from_sibling, recv, place, *, name):
    _, hr, c = from_sibling.shape
    tr = min(hr, 256)
    tc = _col_tile(c)
    nrb = hr // tr

    def body(place_ref, g_ref, s_ref, r0_ref, r1_ref, r2_ref, o_ref):
        own = g_ref[...] + s_ref[...]
        o_ref[...] = ((own + r0_ref[...].astype(F32)) + r1_ref[...].astype(F32)) + r2_ref[...].astype(F32)

    def rspec(j):
        return pl.BlockSpec((None, tr, tc), lambda i, jj, place: (j, i, jj))

    return pl.pallas_call(
        body,
        grid_spec=pltpu.PrefetchScalarGridSpec(
            num_scalar_prefetch=1, grid=(nrb, c // tc),
            in_specs=[pl.BlockSpec((None, tr, tc), lambda i, jj, place: (place[0], place[1] * nrb + i, jj)),
                      pl.BlockSpec((None, tr, tc), lambda i, jj, place: (place[0], i, jj)),
                      rspec(0), rspec(1), rspec(2)],
            out_specs=pl.BlockSpec((tr, tc), lambda i, jj, place: (place[1] * nrb + i, jj))),
        out_shape=jax.ShapeDtypeStruct((2 * hr, c), F32), name=name, compiler_params=_cparams(),
    )(place, grad, from_sibling, recv, recv, recv)


def _add_halves(mine, theirs, *, name, side=None):
    k, hr, c = mine.shape
    tr = min(hr, 256)
    tc = _col_tile(c)

    def body(a_ref, b_ref, o_ref):
        o_ref[...] = (a_ref[...] + b_ref[...]).astype(BF16)

    spec = pl.BlockSpec((None, tr, tc), lambda kk, i, j: (kk, i, j))
    (out,), side_outs = _call(body, grid=(k, hr // tr, c // tc), in_specs=[spec, spec], out_specs=[spec],
                              out_shape=[jax.ShapeDtypeStruct((k, hr, c), BF16)], args=(mine, theirs), name=name,
                              side=side)
    return out, side_outs


def _sum_halves(mine, theirs, recv, place, *, name):
    _, hr, c = mine.shape
    tr = min(hr, 256)
    tc = _col_tile(c)
    nrb = hr // tr

    def body(place_ref, a_ref, b_ref, r0_ref, r1_ref, r2_ref, o_ref):
        own = a_ref[...] + b_ref[...]
        o_ref[...] = ((own + r0_ref[...].astype(F32)) + r1_ref[...].astype(F32)) + r2_ref[...].astype(F32)

    def rspec(j):
        return pl.BlockSpec((None, tr, tc), lambda i, jj, place: (j, i, jj))

    own_spec = pl.BlockSpec((None, tr, tc), lambda i, jj, place: (place[0], i, jj))
    return pl.pallas_call(
        body,
        grid_spec=pltpu.PrefetchScalarGridSpec(
            num_scalar_prefetch=1, grid=(nrb, c // tc),
            in_specs=[own_spec, own_spec, rspec(0), rspec(1), rspec(2)],
            out_specs=pl.BlockSpec((tr, tc), lambda i, jj, place: (place[1] * nrb + i, jj))),
        out_shape=jax.ShapeDtypeStruct((2 * hr, c), F32), name=name, compiler_params=_cparams(),
    )(place, mine, theirs, recv, recv, recv)


def _sum_small(small, recv, place):
    _, sr, _ = small.shape

    def body(place_ref, own_ref, r_ref, o_ref):
        acc = own_ref[...]
        for r in range(1, 8):
            acc = acc + r_ref[r]
        o_ref[...] = acc

    return pl.pallas_call(
        body,
        grid_spec=pltpu.PrefetchScalarGridSpec(
            num_scalar_prefetch=1, grid=(1,),
            in_specs=[pl.BlockSpec((None, sr, 128), lambda i, place: (place[2], 0, 0)),
                      pl.BlockSpec((8, sr, 128), lambda i, place: (0, 0, 0))],
            out_specs=pl.BlockSpec((None, sr, 128), lambda i, place: (place[2], 0, 0))),
        out_shape=jax.ShapeDtypeStruct(small.shape, F32), name="sum_small", compiler_params=_cparams(),
    )(place, small, recv)


def _spread_side(vec):
    def copies(ins, outs, sems):
        x, y, c, _ = _place()
        return [pltpu.make_async_remote_copy(
            src_ref=ins[0], dst_ref=outs[0].at[r], send_sem=sems[0].at[r - 1], recv_sem=sems[1].at[r - 1],
            device_id=(x ^ fx, y ^ fy, c ^ fc), device_id_type=MESH)
            for r, (fx, fy, fc) in enumerate(_relations(), start=1)]

    def start(ins, outs, sems):
        for cp in copies(ins, outs, sems):
            cp.start()

    def finish(ins, outs, sems):
        for cp in copies(ins, outs, sems):
            cp.wait()

    return _Side([vec], [jax.ShapeDtypeStruct((8,) + vec.shape, vec.dtype)], [pltpu.SemaphoreType.DMA((7,))] * 2,
                 start, finish)


def _sum_in_device_order(own, spread, place):
    def body(place_ref, own_ref, r_ref, o_ref):
        me = place_ref[2]
        acc = jnp.zeros_like(own_ref[...])
        for d in range(8):
            slot = jnp.where(me == d, 1, me ^ d)
            acc = acc + jnp.where(me == d, own_ref[...], r_ref[slot])
        o_ref[...] = acc

    return pl.pallas_call(
        body,
        grid_spec=pltpu.PrefetchScalarGridSpec(
            num_scalar_prefetch=1, grid=(1,),
            in_specs=[pl.BlockSpec(own.shape, lambda i, place: (0, 0)),
                      pl.BlockSpec(spread.shape, lambda i, place: (0, 0, 0))],
            out_specs=pl.BlockSpec(own.shape, lambda i, place: (0, 0))),
        out_shape=jax.ShapeDtypeStruct(own.shape, F32), name="sum_in_device_order", compiler_params=_cparams(),
    )(place, own, spread)


def _adamw(w, g, m, v, *, name):
    r, c = w.shape
    tr = 256 if r % 256 == 0 else r
    tc = _col_tile(c)
    bc1 = 1.0 - ADAM_B1 ** ADAM_STEP
    bc2 = 1.0 - ADAM_B2 ** ADAM_STEP

    def body(w_ref, g_ref, m_ref, v_ref, d_ref, nm_ref, nv_ref, gout_ref):
        gv = g_ref[...]
        nm = ADAM_B1 * m_ref[...] + (1.0 - ADAM_B1) * gv
        nv = ADAM_B2 * v_ref[...] + (1.0 - ADAM_B2) * (gv * gv)
        d_ref[...] = -ADAM_LR * ((nm / bc1) / (jnp.sqrt(nv / bc2) + ADAM_EPS) + ADAM_WD * w_ref[...])
        nm_ref[...] = nm
        nv_ref[...] = nv
        gout_ref[...] = gv

    spec = pl.BlockSpec((tr, tc), lambda i, j: (i, j))
    outs, _ = _call(body, grid=(r // tr, c // tc), in_specs=[spec] * 4, out_specs=[spec] * 4,
                    out_shape=[jax.ShapeDtypeStruct((r, c), F32)] * 4, args=(w, g, m, v), name=name)
    return outs


SMALL_ORDER = ["a_ws", "a_bs", "a_norm_g", "a_ln_g", "a_ln_b", "kv_norm_g", "b_kv", "b_norm_g", "b_bq",
               "b_sinks", "final_norm_g"]
SHARDED_SMALL = {"a_norm_g", "a_ln_g", "a_ln_b"}
PACK_TILE = 8 * 128


def _rows128(a):
    flat = a.reshape(-1)
    return jnp.pad(flat, (0, (-flat.shape[0]) % PACK_TILE)).reshape(-1, 128)


def _pack_rows(parts, multiple):
    rows = [_rows128(p) for p in parts]
    total = sum(r.shape[0] for r in rows)
    pad = (-total) % multiple
    if pad:
        rows.append(jnp.zeros((pad, 128), rows[0].dtype))
    return jnp.concatenate(rows, axis=0)


def _unpack_rows(packed, shapes):
    out, row = [], 0
    for shp in shapes:
        size = math.prod(shp)
        nrow = -(-size // PACK_TILE) * 8
        out.append(packed[row:row + nrow].reshape(-1)[:size].reshape(shp))
        row += nrow
    return out


WEIGHTS = ["a_norm_g", "a_w_in", "a_ln_g", "a_ln_b", "a_ws", "a_bs", "a_w_out", "kv_norm_g", "w_kv", "b_kv",
           "b_norm_g", "b_w_in", "b_bq", "b_sinks", "b_w_out", "final_norm_g"]
BIG = ["a_w_in", "a_w_out", "w_kv", "b_w_in", "b_w_out"]


class _Reduction:
    def __init__(self, names, partials, core, place, small=None):
        self.names, self.partials, self.core, self.place, self.small = names, partials, core, place, small

    def exchange_side(self):
        return _exchange_side(self.partials)

    def took_exchange(self, from_sibling):
        self.from_sibling = from_sibling
        self.chip_sums = [_add_sibling(g, r, self.core, name="add_sibling_" + n)
                          for g, r, n in zip(self.partials, from_sibling, self.names)]

    def scatter_side(self):
        return _scatter_side(self.chip_sums, self.small)

    def took_scatter(self, arrived):
        big = arrived[:len(self.names)]
        self.halves = [_sum_chips(g, fs, r, self.place, name="sum_chips_" + n)
                       for g, fs, r, n in zip(self.partials, self.from_sibling, big, self.names)]
        self.small_mine = _sum_small(self.small, arrived[-1], self.place) if self.small is not None else None

    def share_side(self):
        return _share_side(self.halves, self.small_mine)

    def took_share(self, shared):
        self.grads = dict(zip(self.names, shared[:len(self.names)]))
        self.small_full = shared[-1] if self.small is not None else None


def _step(x, loss_target, p, m, v):
    xi, yi, ci = lax.axis_index("x"), lax.axis_index("y"), lax.axis_index("c")
    chip = 2 * xi + yi
    device = 4 * xi + 2 * yi + ci
    core = jnp.reshape(ci, (1,)).astype(jnp.int32)
    place = jnp.stack([chip, ci, device]).astype(jnp.int32)
    x, tgt = x[0], loss_target[0]
    s = x.shape[0]
    cos, sin = _rope_tables(s)

    shard2d = {n: p[n].reshape(p[n].shape[-2:]) for n in BIG}
    shard_bf = {n: shard2d[n].astype(BF16) for n in BIG}
    ws = p["a_ws"][0]
    ws_t = jnp.swapaxes(ws, 1, 2)
    bs_t = p["a_bs"][0].T
    kv_norm_g, b_kv = p["kv_norm_g"].reshape(1, -1), p["b_kv"].reshape(1, -1)
    final_norm_g = p["final_norm_g"].reshape(1, -1)

    vec_shapes = [p[n].shape for n in ("a_norm_g", "a_ln_g", "a_ln_b")]
    vec_pack = _pack_rows([p["a_norm_g"], p["a_ln_g"], p["a_ln_b"]], 16)
    (vec_all,) = _comm_call(_gather_side([vec_pack]), "gather_vectors")
    vecs = [_unpack_rows(vec_all[k], vec_shapes) for k in range(N_CHIPS)]
    a_norm_g, a_ln_g, a_ln_b = (jnp.concatenate([vk[t] for vk in vecs], axis=-1) for t in range(3))

    n_a = _rms_fwd_ring(x, a_norm_g, name="rms_a")
    order = jnp.stack([chip, 2 * (1 - xi) + yi, 2 * xi + (1 - yi), 2 * (1 - xi) + (1 - yi)]).astype(jnp.int32)
    z, a_w_in = _mm_gathering(n_a, shard_bf["a_w_in"], order, name="mm_a_in")
    y, (a_w_out,) = _gate_fwd(z, a_ln_g, a_ln_b, ws, bs_t, side=_gather_side([shard_bf["a_w_out"]]))
    a_w_out = a_w_out.reshape(A_WIDTH, D_MODEL)
    (h1, n_kv, n_b), (w_kv, b_w_in) = _mm_residual_norms(
        y, a_w_out, x, [kv_norm_g, p["b_norm_g"]], name="mm_a_out",
        side=_gather_side([shard_bf["w_kv"], shard_bf["b_w_in"]]))
    w_kv = w_kv.reshape(D_MODEL, 2 * KV_WIDTH)
    kr, vv = _kv_rope(n_kv, w_kv, b_kv, cos, sin)
    zb = _mm_nn(n_b, b_w_in, name="mm_b_in", tn=512, tm=2048, out_dtype=BF16)
    yb, (b_w_out,) = _attn_fwd(zb, kr, vv, cos, sin, p["b_bq"], p["b_sinks"], side=_gather_side([shard_bf["b_w_out"]]))
    b_w_out = b_w_out.reshape(B_WIDTH, D_MODEL)
    loss_blk, dh2, dh2b, d_final_g = _mm_residual_loss(yb, b_w_out, h1, tgt, final_norm_g, name="mm_b_out")

    d_b_w_out = _mm_tn(yb, dh2b, name="mm_d_b_w_out", tm=B_WIDTH, tn=D_MODEL)
    red_bo = _Reduction(["b_w_out"], [d_b_w_out.reshape(N_CHIPS, B_WIDTH // N_CHIPS, D_MODEL)], core, place)
    dyb, got = _mm_nt(dh2b, b_w_out, name="mm_dyb", tm=1024, out_dtype=BF16, side=red_bo.exchange_side())
    red_bo.took_exchange(got)
    (dzb, dk_rot, dv, d_bq, d_sinks), got = _attn_bwd(zb, dyb, kr, vv, cos, sin, p["b_bq"], p["b_sinks"],
                                                      side=red_bo.scatter_side())
    red_bo.took_scatter(got)
    dkv, d_b_kv = _kv_rope_bwd(dk_rot, dv, cos, sin)
    d_b_w_in = _mm_tn(n_b, dzb, name="mm_d_b_w_in", tm=D_MODEL, tn=512, tk=4096, shards=N_CHIPS)
    d_w_kv, got = _mm_tn(n_kv, dkv, name="mm_d_w_kv", tm=D_MODEL, tn=2 * KV_WIDTH, tk=4096,
                         side=red_bo.share_side())
    red_bo.took_share(got)
    red_bi = _Reduction(["b_w_in", "w_kv"], [d_b_w_in, d_w_kv.reshape(N_CHIPS, D_MODEL // N_CHIPS, 2 * KV_WIDTH)],
                        core, place)
    (dh1, dh1b, d_kv_g, d_b_g), got = _mm_nt_rms_bwd(
        [(dkv, w_kv, kv_norm_g), (dzb, b_w_in, p["b_norm_g"])], h1, dh2, name="mm_dn_b", tm=512,
        side=red_bi.exchange_side())
    red_bi.took_exchange(got)

    d_a_w_out = _mm_tn(y, dh1b, name="mm_d_a_w_out", tm=1024, tn=D_MODEL)
    red_ao = _Reduction(["a_w_out"], [d_a_w_out.reshape(N_CHIPS, A_WIDTH // N_CHIPS, D_MODEL)], core, place)
    dy, got = _mm_nt(dh1b, a_w_out, name="mm_dy", tn=1024, tm=1024, out_dtype=BF16, side=red_ao.exchange_side())
    red_ao.took_exchange(got)
    sides = [red_bi.scatter_side(), red_ao.scatter_side()]
    (dz, d_ln_g, d_ln_b, d_ws, d_bs_t), got = _gate_bwd(z, dy, a_ln_g, a_ln_b, ws, ws_t, bs_t, side=_join(sides))
    got = _split(got, sides)
    red_bi.took_scatter(got[0])
    red_ao.took_scatter(got[1])
    small = {
        "a_ws": d_ws, "a_bs": d_bs_t.T, "a_ln_g": d_ln_g, "a_ln_b": d_ln_b,
        "kv_norm_g": d_kv_g, "b_kv": d_b_kv, "b_norm_g": d_b_g, "b_bq": d_bq,
        "b_sinks": d_sinks[0:1, :N_Q_HEADS], "final_norm_g": d_final_g,
    }
    packed = [n for n in SMALL_ORDER if n != "a_norm_g"]
    small_shapes = [small[n].shape for n in packed] + [(1, 1)]
    small_pack = _pack_rows([small[n] for n in packed] + [loss_blk[0:1, 0:1]], 64)
    seg = small_pack.shape[0] // 8
    small_pack = small_pack.reshape(8, seg, 128)
    sides = [red_bi.share_side(), red_ao.share_side(), _small_scatter_side(small_pack)]
    (d_a_w_in, from_sibling), got = _mm_tn_exchanging(n_a, dz, name="mm_d_a_w_in", shards=N_CHIPS, side=_join(sides))
    got = _split(got, sides)
    red_bi.took_share(got[0])
    red_ao.took_share(got[1])
    small_mine = _sum_small(small_pack, got[2][0], place)

    chip_sum, (small_all,) = _add_halves(d_a_w_in, from_sibling, name="add_sibling_a_w_in",
                                         side=_small_share_side(small_mine))
    (dx, _, d_a_g), (arrived,) = _mm_nt_rms_bwd([(dz, a_w_in, a_norm_g)], x, dh1, name="mm_dn_a", tm=256,
                                                side=_scatter_side([chip_sum]))
    half_ai = _sum_halves(d_a_w_in, from_sibling, arrived, place, name="sum_chips_a_w_in")
    d_a_g = _rows128(d_a_g)
    sides = [_share_side([half_ai]), _spread_side(d_a_g)]
    got = _split(_comm_call(_join(sides), "share_last"), sides)
    grad_ai = got[0][0]
    small_full = dict(zip(packed + ["loss"], _unpack_rows(small_all.reshape(8 * seg, 128), small_shapes)))
    small_full["a_norm_g"] = _sum_in_device_order(d_a_g, got[1][0], place).reshape(1, -1)
    loss = small_full["loss"].reshape(())

    grad_big = {**red_bo.grads, **red_bi.grads, **red_ao.grads, "a_w_in": grad_ai}
    grads = {}
    for n in SMALL_ORDER:
        gfull = small_full[n]
        if n in SHARDED_SMALL:
            width = p[n].shape[-1]
            gfull = lax.dynamic_slice_in_dim(gfull, chip * width, width, axis=-1)
        grads[n] = gfull.reshape(p[n].shape)

    delta, new_m, new_v = {}, {}, {}
    for n in BIG:
        d, nm, nv, g = _adamw(shard2d[n], grad_big[n], m[n].reshape(shard2d[n].shape),
                              v[n].reshape(shard2d[n].shape), name="adamw_" + n)
        delta[n], new_m[n], new_v[n] = d.reshape(p[n].shape), nm.reshape(p[n].shape), nv.reshape(p[n].shape)
        grads[n] = g.reshape(p[n].shape)
    shapes = [p[n].shape for n in SMALL_ORDER]
    packs = [_pack_rows([src[n] for n in SMALL_ORDER], 8) for src in (p, grads, m, v)]
    outs = _adamw(*packs, name="adamw_small")[:3]
    for res, packed in zip((delta, new_m, new_v), outs):
        for n, val in zip(SMALL_ORDER, _unpack_rows(packed, shapes)):
            res[n] = val

    return (loss, dx[None], *[grads[n] for n in WEIGHTS], *[delta[n] for n in WEIGHTS],
            *[new_m[n] for n in WEIGHTS], *[new_v[n] for n in WEIGHTS])


def kernel(x, a_norm_g, a_w_in, a_ln_g, a_ln_b, a_ws, a_bs, a_w_out, kv_norm_g, w_kv, b_kv, b_norm_g, b_w_in, b_bq, b_sinks, b_w_out, final_norm_g, loss_target, m_a_norm_g, m_a_w_in, m_a_ln_g, m_a_ln_b, m_a_ws, m_a_bs, m_a_w_out, m_kv_norm_g, m_w_kv, m_b_kv, m_b_norm_g, m_b_w_in, m_b_bq, m_b_sinks, m_b_w_out, m_final_norm_g, v_a_norm_g, v_a_w_in, v_a_ln_g, v_a_ln_b, v_a_ws, v_a_bs, v_a_w_out, v_kv_norm_g, v_w_kv, v_b_kv, v_b_norm_g, v_b_w_in, v_b_bq, v_b_sinks, v_b_w_out, v_final_norm_g):
    p = dict(a_norm_g=a_norm_g, a_w_in=a_w_in, a_ln_g=a_ln_g, a_ln_b=a_ln_b, a_ws=a_ws, a_bs=a_bs, a_w_out=a_w_out,
             kv_norm_g=kv_norm_g, w_kv=w_kv, b_kv=b_kv, b_norm_g=b_norm_g, b_w_in=b_w_in, b_bq=b_bq, b_sinks=b_sinks,
             b_w_out=b_w_out, final_norm_g=final_norm_g)
    m = dict(a_norm_g=m_a_norm_g, a_w_in=m_a_w_in, a_ln_g=m_a_ln_g, a_ln_b=m_a_ln_b, a_ws=m_a_ws, a_bs=m_a_bs,
             a_w_out=m_a_w_out, kv_norm_g=m_kv_norm_g, w_kv=m_w_kv, b_kv=m_b_kv, b_norm_g=m_b_norm_g, b_w_in=m_b_w_in,
             b_bq=m_b_bq, b_sinks=m_b_sinks, b_w_out=m_b_w_out, final_norm_g=m_final_norm_g)
    v = dict(a_norm_g=v_a_norm_g, a_w_in=v_a_w_in, a_ln_g=v_a_ln_g, a_ln_b=v_a_ln_b, a_ws=v_a_ws, a_bs=v_a_bs,
             a_w_out=v_a_w_out, kv_norm_g=v_kv_norm_g, w_kv=v_w_kv, b_kv=v_b_kv, b_norm_g=v_b_norm_g, b_w_in=v_b_w_in,
             b_bq=v_b_bq, b_sinks=v_b_sinks, b_w_out=v_b_w_out, final_norm_g=v_final_norm_g)
    return _step(x, loss_target, p, m, v)
```
